```python
import jax, jax.numpy as jnp
from jax import lax
import numpy as np

D_MODEL = 1024
BATCH = 8
SEQ = 2048
DEPTH = 2

N_META = 16
BLOCK = 128
SSD_HEADS = 8
SSD_HEAD_DIM = 64
SSD_D = SSD_HEADS * SSD_HEAD_DIM
SSD_GROUPS = 2
SSD_STATE = 64
SSD_CONV = 4
SSD_CONV_DIM = SSD_D + 2 * SSD_GROUPS * SSD_STATE
FOX_HEADS = 4
FOX_HEAD_DIM = 64
FOX_D = FOX_HEADS * FOX_HEAD_DIM
MLA_HEADS = 4
MLA_Q_LORA = 256
MLA_KV_LORA = 128
MLA_NOPE = 64
MLA_ROPE = 32
MLA_V = 64
MLA_D = MLA_HEADS * MLA_V
ROPE_THETA = 10000.0
D_MIX = SSD_D + FOX_D + MLA_D
IN_SIZES = [SSD_D, SSD_CONV_DIM, SSD_HEADS,
            FOX_D, FOX_D, FOX_D, FOX_HEADS,
            MLA_Q_LORA, MLA_KV_LORA, MLA_ROPE]
N_IN = sum(IN_SIZES)
IN_SPLITS = [int(s) for s in np.cumsum(IN_SIZES)[:-1]]
D_FF = 2816
ALPHA = (2 * DEPTH) ** 0.25
BETA = (8 * DEPTH) ** -0.25
EPS = 1e-5

kernel_name = "hybrid_ssd_fox_mla_macaron_deepnorm"


def layer_norm(x, g, b):
    xf = x.astype(jnp.float32)
    mu = jnp.mean(xf, -1, keepdims=True)
    var = jnp.mean(jnp.square(xf - mu), -1, keepdims=True)
    return ((xf - mu) * lax.rsqrt(var + EPS) * g + b).astype(x.dtype)


def rms_norm(x, g):
    xf = x.astype(jnp.float32)
    y = xf * lax.rsqrt(jnp.mean(jnp.square(xf), -1, keepdims=True) + EPS)
    return (y * g).astype(x.dtype)


def swiglu(x, w_gate, w_up, w_down):
    return (jax.nn.silu(x @ w_gate) * (x @ w_up)) @ w_down


def rope(x, cos, sin):
    x1, x2 = jnp.split(x.astype(jnp.float32), 2, axis=-1)
    return jnp.concatenate([x1 * cos - x2 * sin, x2 * cos + x1 * sin], -1).astype(x.dtype)


def block_edges(total):
    return sorted(set([0] + list(range(N_META, total, BLOCK)) + [total]))


def blocked_causal_attention(logits_fn, v):
    total = v.shape[1]
    outs = []
    edges = block_edges(total)
    for q0, q1 in zip(edges[:-1], edges[1:]):
        s = logits_fn(q0, q1).astype(jnp.float32)
        causal = jnp.arange(q1)[None, :] <= jnp.arange(q0, q1)[:, None]
        s = jnp.where(causal, s, -jnp.inf)
        p = jax.nn.softmax(s, axis=-1).astype(v.dtype)
        outs.append(jnp.einsum('bhqk,bkhd->bqhd', p, v[:, :q1]))
    return jnp.concatenate(outs, axis=1)


def causal_depthwise_conv(x, w, bias):
    out = lax.conv_general_dilated(
        x, w[:, None, :], window_strides=(1,), padding=[(SSD_CONV - 1, 0)],
        dimension_numbers=('NWC', 'WIO', 'NWC'), feature_group_count=x.shape[-1])
    return out + bias


def ssd_chunked(x, dt, A, Bm, Cm):
    b, l, h, p = x.shape
    n = Bm.shape[-1]
    nc = l // BLOCK
    x = x.reshape(b, nc, BLOCK, h, p)
    dt = dt.reshape(b, nc, BLOCK, h)
    Bm = Bm.reshape(b, nc, BLOCK, h, n)
    Cm = Cm.reshape(b, nc, BLOCK, h, n)
    a = jnp.moveaxis(dt * A, -1, 1)
    a_cum = jnp.cumsum(a, axis=-1)
    xdt = x * dt[..., None]
    idx = jnp.arange(BLOCK)
    causal = idx[:, None] >= idx[None, :]
    seg = jnp.exp(jnp.where(causal, a_cum[..., :, None] - a_cum[..., None, :], -jnp.inf))
    cb = jnp.einsum('bclhn,bcshn->bhcls', Cm, Bm)
    y_diag = jnp.einsum('bhcls,bcshp->bclhp', cb * seg, xdt)
    decay_states = jnp.exp(a_cum[..., -1:] - a_cum)
    states = jnp.einsum('bclhn,bhcl,bclhp->bchpn', Bm, decay_states, xdt)
    chunk_decay = jnp.exp(a_cum[..., -1])

    def step(s, inp):
        st, dec = inp
        return s * dec[..., None, None] + st, s

    init = jnp.zeros((b, h, p, n), x.dtype)
    _, prev = lax.scan(step, init, (jnp.moveaxis(states, 1, 0), jnp.moveaxis(chunk_decay, 2, 0)))
    prev = jnp.moveaxis(prev, 0, 1)
    y_off = jnp.einsum('bclhn,bchpn,bhcl->bclhp', Cm, prev, jnp.exp(a_cum))
    return (y_diag + y_off).reshape(b, l, h, p)


def ssd_mixer(z, xbc, dt_raw, conv_w, conv_b, dt_bias, a_log, d_skip, norm_g):
    b, L, _ = xbc.shape
    f32 = jnp.float32
    xbc = jax.nn.silu(causal_depthwise_conv(xbc, conv_w, conv_b)).astype(f32)
    xs, Bm, Cm = jnp.split(xbc, [SSD_D, SSD_D + SSD_GROUPS * SSD_STATE], axis=-1)
    xs = xs.reshape(b, L, SSD_HEADS, SSD_HEAD_DIM)
    rep = SSD_HEADS // SSD_GROUPS
    Bm = jnp.repeat(Bm.reshape(b, L, SSD_GROUPS, SSD_STATE), rep, axis=2)
    Cm = jnp.repeat(Cm.reshape(b, L, SSD_GROUPS, SSD_STATE), rep, axis=2)
    dt = jax.nn.softplus(dt_raw.astype(f32) + dt_bias.astype(f32))
    A = -jnp.exp(a_log.astype(f32))
    pad = (-L) % BLOCK
    padf = lambda t: jnp.pad(t, ((0, 0), (pad, 0)) + ((0, 0),) * (t.ndim - 2))
    y = ssd_chunked(padf(xs), padf(dt), A, padf(Bm), padf(Cm))[:, pad:]
    y = y + d_skip.astype(f32)[:, None] * xs
    y = y.reshape(b, L, SSD_D) * jax.nn.silu(z.astype(f32))
    y = rms_norm(y.reshape(b, L, SSD_GROUPS, SSD_D // SSD_GROUPS), 1.0).reshape(b, L, SSD_D) * norm_g
    return y.astype(z.dtype)


def fox_mixer(q, k, v, f_raw, f_b):
    b, L, _ = q.shape
    q = q.reshape(b, L, FOX_HEADS, FOX_HEAD_DIM)
    k = k.reshape(b, L, FOX_HEADS, FOX_HEAD_DIM)
    v = v.reshape(b, L, FOX_HEADS, FOX_HEAD_DIM)
    log_f = jax.nn.log_sigmoid(f_raw.astype(jnp.float32) + f_b.astype(jnp.float32))
    c = jnp.cumsum(log_f, axis=1).transpose(0, 2, 1)
    scale = FOX_HEAD_DIM ** -0.5

    def logits(q0, q1):
        s = jnp.einsum('bqhd,bkhd->bhqk', q[:, q0:q1], k[:, :q1]).astype(jnp.float32) * scale
        return s + (c[:, :, q0:q1, None] - c[:, :, None, :q1])

    return blocked_causal_attention(logits, v).reshape(b, L, FOX_D)


def mla_mixer(cq, ckv, k_rope, q_norm_g, w_uq, kv_norm_g, w_ukv, cos, sin):
    b, L, _ = cq.shape
    qh = (rms_norm(cq, q_norm_g) @ w_uq).reshape(b, L, MLA_HEADS, MLA_NOPE + MLA_ROPE)
    q_nope, q_rope = qh[..., :MLA_NOPE], qh[..., MLA_NOPE:]
    q_rope = rope(q_rope, cos[None, :, None, :], sin[None, :, None, :])
    kv = (rms_norm(ckv, kv_norm_g) @ w_ukv).reshape(b, L, MLA_HEADS, MLA_NOPE + MLA_V)
    k_nope, v = kv[..., :MLA_NOPE], kv[..., MLA_NOPE:]
    k_rope = rope(k_rope, cos[None], sin[None])
    scale = (MLA_NOPE + MLA_ROPE) ** -0.5

    def logits(q0, q1):
        s = jnp.einsum('bqhd,bkhd->bhqk', q_nope[:, q0:q1], k_nope[:, :q1])
        s = s + jnp.einsum('bqhr,bkr->bhqk', q_rope[:, q0:q1], k_rope[:, :q1])
        return s * scale

    return blocked_causal_attention(logits, v).reshape(b, L, MLA_D)


def _fwd_setup_inputs(seed: int = 0) -> dict:
    key = jax.random.key(seed)
    ks = iter(jax.random.split(key, 48))
    f32 = jnp.float32
    Dm, F, NL = D_MODEL, D_FF, DEPTH

    def nrm(shape, scale):
        return jax.random.normal(next(ks), shape, f32) * scale

    def gain(shape):
        return 1.0 + nrm(shape, 0.02)

    u = jax.random.uniform(next(ks), (NL, SSD_HEADS), f32)
    dt0 = jnp.exp(u * (np.log(0.1) - np.log(0.001)) + np.log(0.001))
    dt_bias = dt0 + jnp.log(-jnp.expm1(-dt0))
    a_log = jnp.log(jax.random.uniform(next(ks), (NL, SSD_HEADS), f32, 1.0, 16.0))
    return {
        "x": nrm((BATCH, SEQ, Dm), 1.0),
        "meta": nrm((N_META, Dm), 1.0),
        "ffn1_w_gate": nrm((NL, Dm, F), Dm ** -0.5),
        "ffn1_w_up": nrm((NL, Dm, F), Dm ** -0.5),
        "ffn1_w_down": nrm((NL, F, Dm), F ** -0.5 * BETA),
        "ln1_g": gain((NL, Dm)),
        "ln1_b": nrm((NL, Dm), 0.02),
        "w_in": nrm((NL, Dm, N_IN), Dm ** -0.5),
        "conv_w": nrm((NL, SSD_CONV, SSD_CONV_DIM), SSD_CONV ** -0.5),
        "conv_b": nrm((NL, SSD_CONV_DIM), 0.02),
        "dt_bias": dt_bias,
        "a_log": a_log,
        "d_skip": gain((NL, SSD_HEADS)),
        "ssd_norm_g": gain((NL, SSD_D)),
        "fox_f_b": 3.0 + nrm((NL, FOX_HEADS), 0.5),
        "mla_q_norm_g": gain((NL, MLA_Q_LORA)),
        "mla_w_uq": nrm((NL, MLA_Q_LORA, MLA_HEADS * (MLA_NOPE + MLA_ROPE)), MLA_Q_LORA ** -0.5),
        "mla_kv_norm_g": gain((NL, MLA_KV_LORA)),
        "mla_w_ukv": nrm((NL, MLA_KV_LORA, MLA_HEADS * (MLA_NOPE + MLA_V)), MLA_KV_LORA ** -0.5),
        "w_out": nrm((NL, D_MIX, Dm), D_MIX ** -0.5 * BETA),
        "ln2_g": gain((NL, Dm)),
        "ln2_b": nrm((NL, Dm), 0.02),
        "ffn2_w_gate": nrm((NL, Dm, F), Dm ** -0.5),
        "ffn2_w_up": nrm((NL, Dm, F), Dm ** -0.5),
        "ffn2_w_down": nrm((NL, F, Dm), F ** -0.5 * BETA),
        "ln3_g": gain((NL, Dm)),
        "ln3_b": nrm((NL, Dm), 0.02),
    }


def _fwd_reference(x, meta, ffn1_w_gate, ffn1_w_up, ffn1_w_down, ln1_g, ln1_b, w_in,
              conv_w, conv_b, dt_bias, a_log, d_skip, ssd_norm_g, fox_f_b,
              mla_q_norm_g, mla_w_uq, mla_kv_norm_g, mla_w_ukv, w_out, ln2_g, ln2_b,
              ffn2_w_gate, ffn2_w_up, ffn2_w_down, ln3_g, ln3_b):
    b = x.shape[0]
    h = jnp.concatenate([jnp.broadcast_to(meta[None].astype(x.dtype), (b, N_META, D_MODEL)), x], axis=1)
    total = h.shape[1]
    pos = jnp.arange(total, dtype=jnp.float32)
    inv_freq = 1.0 / (ROPE_THETA ** (jnp.arange(0, MLA_ROPE, 2, dtype=jnp.float32) / MLA_ROPE))
    ang = pos[:, None] * inv_freq[None, :]
    cos, sin = jnp.cos(ang), jnp.sin(ang)

    for l in range(DEPTH):
        h = layer_norm(ALPHA * h + 0.5 * swiglu(h, ffn1_w_gate[l], ffn1_w_up[l], ffn1_w_down[l]),
                       ln1_g[l], ln1_b[l])
        proj = h @ w_in[l]
        (z, xbc, dt_raw, fq, fk, fv, f_raw, cq, ckv, k_rope) = jnp.split(proj, IN_SPLITS, axis=-1)
        y_ssd = ssd_mixer(z, xbc, dt_raw, conv_w[l], conv_b[l], dt_bias[l], a_log[l],
                          d_skip[l], ssd_norm_g[l])
        y_fox = fox_mixer(fq, fk, fv, f_raw, fox_f_b[l])
        y_mla = mla_mixer(cq, ckv, k_rope, mla_q_norm_g[l], mla_w_uq[l], mla_kv_norm_g[l],
                          mla_w_ukv[l], cos, sin)
        mix = jnp.concatenate([y_ssd, y_fox.astype(h.dtype), y_mla.astype(h.dtype)], axis=-1) @ w_out[l]
        h = layer_norm(ALPHA * h + mix, ln2_g[l], ln2_b[l])
        h = layer_norm(ALPHA * h + 0.5 * swiglu(h, ffn2_w_gate[l], ffn2_w_up[l], ffn2_w_down[l]),
                       ln3_g[l], ln3_b[l])
    return h[:, N_META:]


import jax as _jax
import jax.numpy as _jnp

TWIN_FORMAT = 'train_step'
FWD_PARAMS = ['x', 'meta', 'ffn1_w_gate', 'ffn1_w_up', 'ffn1_w_down', 'ln1_g', 'ln1_b', 'w_in', 'conv_w', 'conv_b', 'dt_bias', 'a_log', 'd_skip', 'ssd_norm_g', 'fox_f_b', 'mla_q_norm_g', 'mla_w_uq', 'mla_kv_norm_g', 'mla_w_ukv', 'w_out', 'ln2_g', 'ln2_b', 'ffn2_w_gate', 'ffn2_w_up', 'ffn2_w_down', 'ln3_g', 'ln3_b']
TWIN_WEIGHTS = ['meta', 'ffn1_w_gate', 'ffn1_w_up', 'ffn1_w_down', 'ln1_g', 'ln1_b', 'w_in', 'conv_w', 'conv_b', 'dt_bias', 'a_log', 'd_skip', 'ssd_norm_g', 'fox_f_b', 'mla_q_norm_g', 'mla_w_uq', 'mla_kv_norm_g', 'mla_w_ukv', 'w_out', 'ln2_g', 'ln2_b', 'ffn2_w_gate', 'ffn2_w_up', 'ffn2_w_down', 'ln3_g', 'ln3_b']
TWIN_DIFF_INPUT = 'x'
TWIN_INPUTS = ['x', 'meta', 'ffn1_w_gate', 'ffn1_w_up', 'ffn1_w_down', 'ln1_g', 'ln1_b', 'w_in', 'conv_w', 'conv_b', 'dt_bias', 'a_log', 'd_skip', 'ssd_norm_g', 'fox_f_b', 'mla_q_norm_g', 'mla_w_uq', 'mla_kv_norm_g', 'mla_w_ukv', 'w_out', 'ln2_g', 'ln2_b', 'ffn2_w_gate', 'ffn2_w_up', 'ffn2_w_down', 'ln3_g', 'ln3_b', 'loss_target', 'm_meta', 'm_ffn1_w_gate', 'm_ffn1_w_up', 'm_ffn1_w_down', 'm_ln1_g', 'm_ln1_b', 'm_w_in', 'm_conv_w', 'm_conv_b', 'm_dt_bias', 'm_a_log', 'm_d_skip', 'm_ssd_norm_g', 'm_fox_f_b', 'm_mla_q_norm_g', 'm_mla_w_uq', 'm_mla_kv_norm_g', 'm_mla_w_ukv', 'm_w_out', 'm_ln2_g', 'm_ln2_b', 'm_ffn2_w_gate', 'm_ffn2_w_up', 'm_ffn2_w_down', 'm_ln3_g', 'm_ln3_b', 'v_meta', 'v_ffn1_w_gate', 'v_ffn1_w_up', 'v_ffn1_w_down', 'v_ln1_g', 'v_ln1_b', 'v_w_in', 'v_conv_w', 'v_conv_b', 'v_dt_bias', 'v_a_log', 'v_d_skip', 'v_ssd_norm_g', 'v_fox_f_b', 'v_mla_q_norm_g', 'v_mla_w_uq', 'v_mla_kv_norm_g', 'v_mla_w_ukv', 'v_w_out', 'v_ln2_g', 'v_ln2_b', 'v_ffn2_w_gate', 'v_ffn2_w_up', 'v_ffn2_w_down', 'v_ln3_g', 'v_ln3_b']
TWIN_OUTPUTS = ['loss', 'grad_x', 'grad_meta', 'grad_ffn1_w_gate', 'grad_ffn1_w_up', 'grad_ffn1_w_down', 'grad_ln1_g', 'grad_ln1_b', 'grad_w_in', 'grad_conv_w', 'grad_conv_b', 'grad_dt_bias', 'grad_a_log', 'grad_d_skip', 'grad_ssd_norm_g', 'grad_fox_f_b', 'grad_mla_q_norm_g', 'grad_mla_w_uq', 'grad_mla_kv_norm_g', 'grad_mla_w_ukv', 'grad_w_out', 'grad_ln2_g', 'grad_ln2_b', 'grad_ffn2_w_gate', 'grad_ffn2_w_up', 'grad_ffn2_w_down', 'grad_ln3_g', 'grad_ln3_b', 'delta_meta', 'delta_ffn1_w_gate', 'delta_ffn1_w_up', 'delta_ffn1_w_down', 'delta_ln1_g', 'delta_ln1_b', 'delta_w_in', 'delta_conv_w', 'delta_conv_b', 'delta_dt_bias', 'delta_a_log', 'delta_d_skip', 'delta_ssd_norm_g', 'delta_fox_f_b', 'delta_mla_q_norm_g', 'delta_mla_w_uq', 'delta_mla_kv_norm_g', 'delta_mla_w_ukv', 'delta_w_out', 'delta_ln2_g', 'delta_ln2_b', 'delta_ffn2_w_gate', 'delta_ffn2_w_up', 'delta_ffn2_w_down', 'delta_ln3_g', 'delta_ln3_b', 'new_m_meta', 'new_m_ffn1_w_gate', 'new_m_ffn1_w_up', 'new_m_ffn1_w_down', 'new_m_ln1_g', 'new_m_ln1_b', 'new_m_w_in', 'new_m_conv_w', 'new_m_conv_b', 'new_m_dt_bias', 'new_m_a_log', 'new_m_d_skip', 'new_m_ssd_norm_g', 'new_m_fox_f_b', 'new_m_mla_q_norm_g', 'new_m_mla_w_uq', 'new_m_mla_kv_norm_g', 'new_m_mla_w_ukv', 'new_m_w_out', 'new_m_ln2_g', 'new_m_ln2_b', 'new_m_ffn2_w_gate', 'new_m_ffn2_w_up', 'new_m_ffn2_w_down', 'new_m_ln3_g', 'new_m_ln3_b', 'new_v_meta', 'new_v_ffn1_w_gate', 'new_v_ffn1_w_up', 'new_v_ffn1_w_down', 'new_v_ln1_g', 'new_v_ln1_b', 'new_v_w_in', 'new_v_conv_w', 'new_v_conv_b', 'new_v_dt_bias', 'new_v_a_log', 'new_v_d_skip', 'new_v_ssd_norm_g', 'new_v_fox_f_b', 'new_v_mla_q_norm_g', 'new_v_mla_w_uq', 'new_v_mla_kv_norm_g', 'new_v_mla_w_ukv', 'new_v_w_out', 'new_v_ln2_g', 'new_v_ln2_b', 'new_v_ffn2_w_gate', 'new_v_ffn2_w_up', 'new_v_ffn2_w_down', 'new_v_ln3_g', 'new_v_ln3_b']
TWIN_LEAF_KINDS = {'loss': 'loss', 'grad_x': 'grad_x', 'grad_meta': 'grad_w', 'grad_ffn1_w_gate': 'grad_w', 'grad_ffn1_w_up': 'grad_w', 'grad_ffn1_w_down': 'grad_w', 'grad_ln1_g': 'grad_w', 'grad_ln1_b': 'grad_w', 'grad_w_in': 'grad_w', 'grad_conv_w': 'grad_w', 'grad_conv_b': 'grad_w', 'grad_dt_bias': 'grad_w', 'grad_a_log': 'grad_w', 'grad_d_skip': 'grad_w', 'grad_ssd_norm_g': 'grad_w', 'grad_fox_f_b': 'grad_w', 'grad_mla_q_norm_g': 'grad_w', 'grad_mla_w_uq': 'grad_w', 'grad_mla_kv_norm_g': 'grad_w', 'grad_mla_w_ukv': 'grad_w', 'grad_w_out': 'grad_w', 'grad_ln2_g': 'grad_w', 'grad_ln2_b': 'grad_w', 'grad_ffn2_w_gate': 'grad_w', 'grad_ffn2_w_up': 'grad_w', 'grad_ffn2_w_down': 'grad_w', 'grad_ln3_g': 'grad_w', 'grad_ln3_b': 'grad_w', 'delta_meta': 'delta_w', 'delta_ffn1_w_gate': 'delta_w', 'delta_ffn1_w_up': 'delta_w', 'delta_ffn1_w_down': 'delta_w', 'delta_ln1_g': 'delta_w', 'delta_ln1_b': 'delta_w', 'delta_w_in': 'delta_w', 'delta_conv_w': 'delta_w', 'delta_conv_b': 'delta_w', 'delta_dt_bias': 'delta_w', 'delta_a_log': 'delta_w', 'delta_d_skip': 'delta_w', 'delta_ssd_norm_g': 'delta_w', 'delta_fox_f_b': 'delta_w', 'delta_mla_q_norm_g': 'delta_w', 'delta_mla_w_uq': 'delta_w', 'delta_mla_kv_norm_g': 'delta_w', 'delta_mla_w_ukv': 'delta_w', 'delta_w_out': 'delta_w', 'delta_ln2_g': 'delta_w', 'delta_ln2_b': 'delta_w', 'delta_ffn2_w_gate': 'delta_w', 'delta_ffn2_w_up': 'delta_w', 'delta_ffn2_w_down': 'delta_w', 'delta_ln3_g': 'delta_w', 'delta_ln3_b': 'delta_w', 'new_m_meta': 'new_m', 'new_m_ffn1_w_gate': 'new_m', 'new_m_ffn1_w_up': 'new_m', 'new_m_ffn1_w_down': 'new_m', 'new_m_ln1_g': 'new_m', 'new_m_ln1_b': 'new_m', 'new_m_w_in': 'new_m', 'new_m_conv_w': 'new_m', 'new_m_conv_b': 'new_m', 'new_m_dt_bias': 'new_m', 'new_m_a_log': 'new_m', 'new_m_d_skip': 'new_m', 'new_m_ssd_norm_g': 'new_m', 'new_m_fox_f_b': 'new_m', 'new_m_mla_q_norm_g': 'new_m', 'new_m_mla_w_uq': 'new_m', 'new_m_mla_kv_norm_g': 'new_m', 'new_m_mla_w_ukv': 'new_m', 'new_m_w_out': 'new_m', 'new_m_ln2_g': 'new_m', 'new_m_ln2_b': 'new_m', 'new_m_ffn2_w_gate': 'new_m', 'new_m_ffn2_w_up': 'new_m', 'new_m_ffn2_w_down': 'new_m', 'new_m_ln3_g': 'new_m', 'new_m_ln3_b': 'new_m', 'new_v_meta': 'new_v', 'new_v_ffn1_w_gate': 'new_v', 'new_v_ffn1_w_up': 'new_v', 'new_v_ffn1_w_down': 'new_v', 'new_v_ln1_g': 'new_v', 'new_v_ln1_b': 'new_v', 'new_v_w_in': 'new_v', 'new_v_conv_w': 'new_v', 'new_v_conv_b': 'new_v', 'new_v_dt_bias': 'new_v', 'new_v_a_log': 'new_v', 'new_v_d_skip': 'new_v', 'new_v_ssd_norm_g': 'new_v', 'new_v_fox_f_b': 'new_v', 'new_v_mla_q_norm_g': 'new_v', 'new_v_mla_w_uq': 'new_v', 'new_v_mla_kv_norm_g': 'new_v', 'new_v_mla_w_ukv': 'new_v', 'new_v_w_out': 'new_v', 'new_v_ln2_g': 'new_v', 'new_v_ln2_b': 'new_v', 'new_v_ffn2_w_gate': 'new_v', 'new_v_ffn2_w_up': 'new_v', 'new_v_ffn2_w_down': 'new_v', 'new_v_ln3_g': 'new_v', 'new_v_ln3_b': 'new_v'}


def _forward(args):
    return _fwd_reference(*[args[k] for k in FWD_PARAMS])


def _output_shape():
    out = _jax.eval_shape(lambda: _forward(_fwd_setup_inputs(0)))
    return out.shape, out.dtype

N_MICROBATCH = 1
ADAM_LR = 0.001
ADAM_B1 = 0.9
ADAM_B2 = 0.999
ADAM_EPS = 1e-08
ADAM_WD = 0.01
ADAM_STEP = 10
PER_EXAMPLE_BATCH_AXIS = {'x': 0, 'loss_target': 0}
SHARED_INPUTS = []
_WEIGHT_DTYPES = {'meta': _jnp.float32, 'ffn1_w_gate': _jnp.float32, 'ffn1_w_up': _jnp.float32, 'ffn1_w_down': _jnp.float32, 'ln1_g': _jnp.float32, 'ln1_b': _jnp.float32, 'w_in': _jnp.float32, 'conv_w': _jnp.float32, 'conv_b': _jnp.float32, 'dt_bias': _jnp.float32, 'a_log': _jnp.float32, 'd_skip': _jnp.float32, 'ssd_norm_g': _jnp.float32, 'fox_f_b': _jnp.float32, 'mla_q_norm_g': _jnp.float32, 'mla_w_uq': _jnp.float32, 'mla_kv_norm_g': _jnp.float32, 'mla_w_ukv': _jnp.float32, 'w_out': _jnp.float32, 'ln2_g': _jnp.float32, 'ln2_b': _jnp.float32, 'ffn2_w_gate': _jnp.float32, 'ffn2_w_up': _jnp.float32, 'ffn2_w_down': _jnp.float32, 'ln3_g': _jnp.float32, 'ln3_b': _jnp.float32}
MOMENT_SCALE = {'meta': 1.723207e-03, 'ffn1_w_gate': 8.769811e-03, 'ffn1_w_up': 8.504550e-03, 'ffn1_w_down': 2.819349e-02, 'ln1_g': 5.115115e-01, 'ln1_b': 2.709041e-01, 'w_in': 3.174304e-02, 'conv_w': 3.921304e-02, 'conv_b': 5.492348e-02, 'dt_bias': 1.557058e-01, 'a_log': 1.598184e-01, 'd_skip': 2.677369e-01, 'ssd_norm_g': 4.543175e-02, 'fox_f_b': 6.549377e-02, 'mla_q_norm_g': 7.358380e-03, 'mla_w_uq': 6.011945e-03, 'mla_kv_norm_g': 1.739183e-02, 'mla_w_ukv': 7.616986e-03, 'w_out': 6.679409e-02, 'ln2_g': 5.526728e-01, 'ln2_b': 2.688332e-01, 'ffn2_w_gate': 8.396648e-03, 'ffn2_w_up': 8.167986e-03, 'ffn2_w_down': 2.702948e-02, 'ln3_g': 1.133461e+01, 'ln3_b': 8.461342e-01}


def _to_microbatches(a, axis):
    t = _jnp.moveaxis(a, axis, 0)
    t = t.reshape((N_MICROBATCH, t.shape[0] // N_MICROBATCH) + t.shape[1:])
    return _jnp.moveaxis(t, 1, axis + 1)


def setup_inputs(seed: int = 0) -> dict:
    inp = _fwd_setup_inputs(seed)
    key = _jax.random.fold_in(_jax.random.key(seed), 7919)
    shape, _ = _output_shape()
    out = dict(inp)
    out["loss_target"] = _jax.random.normal(_jax.random.fold_in(key, 0), shape, _jnp.float32)
    for i, name in enumerate(TWIN_WEIGHTS):
        w = inp[name].astype(_jnp.float32)
        if MOMENT_SCALE is None:
            s = _jnp.sqrt(_jnp.mean(_jnp.square(w)) + 1e-30)
        else:
            s = MOMENT_SCALE[name]
        km, kv = _jax.random.split(_jax.random.fold_in(key, i + 1))
        out[name] = w
        out["m_" + name] = s * _jax.random.normal(km, w.shape, _jnp.float32)
        out["v_" + name] = (s * s) * _jax.random.uniform(kv, w.shape, _jnp.float32, 0.5, 1.5)
    if N_MICROBATCH > 1:
        for name, axis in PER_EXAMPLE_BATCH_AXIS.items():
            out[name] = _to_microbatches(out[name], axis)
    return {'x': out['x'], 'meta': out['meta'], 'ffn1_w_gate': out['ffn1_w_gate'], 'ffn1_w_up': out['ffn1_w_up'], 'ffn1_w_down': out['ffn1_w_down'], 'ln1_g': out['ln1_g'], 'ln1_b': out['ln1_b'], 'w_in': out['w_in'], 'conv_w': out['conv_w'], 'conv_b': out['conv_b'], 'dt_bias': out['dt_bias'], 'a_log': out['a_log'], 'd_skip': out['d_skip'], 'ssd_norm_g': out['ssd_norm_g'], 'fox_f_b': out['fox_f_b'], 'mla_q_norm_g': out['mla_q_norm_g'], 'mla_w_uq': out['mla_w_uq'], 'mla_kv_norm_g': out['mla_kv_norm_g'], 'mla_w_ukv': out['mla_w_ukv'], 'w_out': out['w_out'], 'ln2_g': out['ln2_g'], 'ln2_b': out['ln2_b'], 'ffn2_w_gate': out['ffn2_w_gate'], 'ffn2_w_up': out['ffn2_w_up'], 'ffn2_w_down': out['ffn2_w_down'], 'ln3_g': out['ln3_g'], 'ln3_b': out['ln3_b'], 'loss_target': out['loss_target'], 'm_meta': out['m_meta'], 'm_ffn1_w_gate': out['m_ffn1_w_gate'], 'm_ffn1_w_up': out['m_ffn1_w_up'], 'm_ffn1_w_down': out['m_ffn1_w_down'], 'm_ln1_g': out['m_ln1_g'], 'm_ln1_b': out['m_ln1_b'], 'm_w_in': out['m_w_in'], 'm_conv_w': out['m_conv_w'], 'm_conv_b': out['m_conv_b'], 'm_dt_bias': out['m_dt_bias'], 'm_a_log': out['m_a_log'], 'm_d_skip': out['m_d_skip'], 'm_ssd_norm_g': out['m_ssd_norm_g'], 'm_fox_f_b': out['m_fox_f_b'], 'm_mla_q_norm_g': out['m_mla_q_norm_g'], 'm_mla_w_uq': out['m_mla_w_uq'], 'm_mla_kv_norm_g': out['m_mla_kv_norm_g'], 'm_mla_w_ukv': out['m_mla_w_ukv'], 'm_w_out': out['m_w_out'], 'm_ln2_g': out['m_ln2_g'], 'm_ln2_b': out['m_ln2_b'], 'm_ffn2_w_gate': out['m_ffn2_w_gate'], 'm_ffn2_w_up': out['m_ffn2_w_up'], 'm_ffn2_w_down': out['m_ffn2_w_down'], 'm_ln3_g': out['m_ln3_g'], 'm_ln3_b': out['m_ln3_b'], 'v_meta': out['v_meta'], 'v_ffn1_w_gate': out['v_ffn1_w_gate'], 'v_ffn1_w_up': out['v_ffn1_w_up'], 'v_ffn1_w_down': out['v_ffn1_w_down'], 'v_ln1_g': out['v_ln1_g'], 'v_ln1_b': out['v_ln1_b'], 'v_w_in': out['v_w_in'], 'v_conv_w': out['v_conv_w'], 'v_conv_b': out['v_conv_b'], 'v_dt_bias': out['v_dt_bias'], 'v_a_log': out['v_a_log'], 'v_d_skip': out['v_d_skip'], 'v_ssd_norm_g': out['v_ssd_norm_g'], 'v_fox_f_b': out['v_fox_f_b'], 'v_mla_q_norm_g': out['v_mla_q_norm_g'], 'v_mla_w_uq': out['v_mla_w_uq'], 'v_mla_kv_norm_g': out['v_mla_kv_norm_g'], 'v_mla_w_ukv': out['v_mla_w_ukv'], 'v_w_out': out['v_w_out'], 'v_ln2_g': out['v_ln2_g'], 'v_ln2_b': out['v_ln2_b'], 'v_ffn2_w_gate': out['v_ffn2_w_gate'], 'v_ffn2_w_up': out['v_ffn2_w_up'], 'v_ffn2_w_down': out['v_ffn2_w_down'], 'v_ln3_g': out['v_ln3_g'], 'v_ln3_b': out['v_ln3_b']}


def _loss(weights, diff, rest, loss_target):
    with _jax.named_scope("forward"):
        args = {**rest, TWIN_DIFF_INPUT: diff, **{k: w.astype(_WEIGHT_DTYPES[k]) for k, w in weights.items()}}
        y = _forward(args)
    with _jax.named_scope("loss_head"):
        err = _jnp.square(y.astype(_jnp.float32) - loss_target)
        return 0.5 * _jnp.sum(_jnp.mean(err, axis=-1)) if err.ndim else 0.5 * err


def _adamw(w, g, m, v):
    m = ADAM_B1 * m + (1.0 - ADAM_B1) * g
    v = ADAM_B2 * v + (1.0 - ADAM_B2) * _jnp.square(g)
    m_hat = m / (1.0 - ADAM_B1 ** ADAM_STEP)
    v_hat = v / (1.0 - ADAM_B2 ** ADAM_STEP)
    delta = -ADAM_LR * (m_hat / (_jnp.sqrt(v_hat) + ADAM_EPS) + ADAM_WD * w)
    return delta, m, v


def reference(x, meta, ffn1_w_gate, ffn1_w_up, ffn1_w_down, ln1_g, ln1_b, w_in, conv_w, conv_b, dt_bias, a_log, d_skip, ssd_norm_g, fox_f_b, mla_q_norm_g, mla_w_uq, mla_kv_norm_g, mla_w_ukv, w_out, ln2_g, ln2_b, ffn2_w_gate, ffn2_w_up, ffn2_w_down, ln3_g, ln3_b, loss_target, m_meta, m_ffn1_w_gate, m_ffn1_w_up, m_ffn1_w_down, m_ln1_g, m_ln1_b, m_w_in, m_conv_w, m_conv_b, m_dt_bias, m_a_log, m_d_skip, m_ssd_norm_g, m_fox_f_b, m_mla_q_norm_g, m_mla_w_uq, m_mla_kv_norm_g, m_mla_w_ukv, m_w_out, m_ln2_g, m_ln2_b, m_ffn2_w_gate, m_ffn2_w_up, m_ffn2_w_down, m_ln3_g, m_ln3_b, v_meta, v_ffn1_w_gate, v_ffn1_w_up, v_ffn1_w_down, v_ln1_g, v_ln1_b, v_w_in, v_conv_w, v_conv_b, v_dt_bias, v_a_log, v_d_skip, v_ssd_norm_g, v_fox_f_b, v_mla_q_norm_g, v_mla_w_uq, v_mla_kv_norm_g, v_mla_w_ukv, v_w_out, v_ln2_g, v_ln2_b, v_ffn2_w_gate, v_ffn2_w_up, v_ffn2_w_down, v_ln3_g, v_ln3_b):
    given = dict(x=x, meta=meta, ffn1_w_gate=ffn1_w_gate, ffn1_w_up=ffn1_w_up, ffn1_w_down=ffn1_w_down, ln1_g=ln1_g, ln1_b=ln1_b, w_in=w_in, conv_w=conv_w, conv_b=conv_b, dt_bias=dt_bias, a_log=a_log, d_skip=d_skip, ssd_norm_g=ssd_norm_g, fox_f_b=fox_f_b, mla_q_norm_g=mla_q_norm_g, mla_w_uq=mla_w_uq, mla_kv_norm_g=mla_kv_norm_g, mla_w_ukv=mla_w_ukv, w_out=w_out, ln2_g=ln2_g, ln2_b=ln2_b, ffn2_w_gate=ffn2_w_gate, ffn2_w_up=ffn2_w_up, ffn2_w_down=ffn2_w_down, ln3_g=ln3_g, ln3_b=ln3_b, loss_target=loss_target, m_meta=m_meta, m_ffn1_w_gate=m_ffn1_w_gate, m_ffn1_w_up=m_ffn1_w_up, m_ffn1_w_down=m_ffn1_w_down, m_ln1_g=m_ln1_g, m_ln1_b=m_ln1_b, m_w_in=m_w_in, m_conv_w=m_conv_w, m_conv_b=m_conv_b, m_dt_bias=m_dt_bias, m_a_log=m_a_log, m_d_skip=m_d_skip, m_ssd_norm_g=m_ssd_norm_g, m_fox_f_b=m_fox_f_b, m_mla_q_norm_g=m_mla_q_norm_g, m_mla_w_uq=m_mla_w_uq, m_mla_kv_norm_g=m_mla_kv_norm_g, m_mla_w_ukv=m_mla_w_ukv, m_w_out=m_w_out, m_ln2_g=m_ln2_g, m_ln2_b=m_ln2_b, m_ffn2_w_gate=m_ffn2_w_gate, m_ffn2_w_up=m_ffn2_w_up, m_ffn2_w_down=m_ffn2_w_down, m_ln3_g=m_ln3_g, m_ln3_b=m_ln3_b, v_meta=v_meta, v_ffn1_w_gate=v_ffn1_w_gate, v_ffn1_w_up=v_ffn1_w_up, v_ffn1_w_down=v_ffn1_w_down, v_ln1_g=v_ln1_g, v_ln1_b=v_ln1_b, v_w_in=v_w_in, v_conv_w=v_conv_w, v_conv_b=v_conv_b, v_dt_bias=v_dt_bias, v_a_log=v_a_log, v_d_skip=v_d_skip, v_ssd_norm_g=v_ssd_norm_g, v_fox_f_b=v_fox_f_b, v_mla_q_norm_g=v_mla_q_norm_g, v_mla_w_uq=v_mla_w_uq, v_mla_kv_norm_g=v_mla_kv_norm_g, v_mla_w_ukv=v_mla_w_ukv, v_w_out=v_w_out, v_ln2_g=v_ln2_g, v_ln2_b=v_ln2_b, v_ffn2_w_gate=v_ffn2_w_gate, v_ffn2_w_up=v_ffn2_w_up, v_ffn2_w_down=v_ffn2_w_down, v_ln3_g=v_ln3_g, v_ln3_b=v_ln3_b)
    weights = {n: given[n] for n in TWIN_WEIGHTS}
    shared = {n: given[n] for n in SHARED_INPUTS}
    per_example = {n: given[n] for n in ['x']}
    grad_fn = _jax.value_and_grad(_loss, argnums=(0, 1))

    def one_microbatch(ex, loss_target):
        ex = dict(ex)
        diff = ex.pop(TWIN_DIFF_INPUT)
        return grad_fn(weights, diff, {**shared, **ex}, loss_target)

    if N_MICROBATCH == 1:
        loss, (grad_w, grad_x) = one_microbatch(per_example, given["loss_target"])
    else:
        def body(carry, xs):
            loss_sum, grad_sum = carry
            l_k, (gw_k, gx_k) = one_microbatch(xs[0], xs[1])
            with _jax.named_scope("update"):
                return (loss_sum + l_k, _jax.tree.map(_jnp.add, grad_sum, gw_k)), gx_k

        init = (_jnp.zeros((), _jnp.float32), _jax.tree.map(_jnp.zeros_like, weights))
        (loss, grad_w), grad_x = _jax.lax.scan(body, init, (per_example, given["loss_target"]))
    with _jax.named_scope("update"):
        delta_w, new_m, new_v = {}, {}, {}
        for n in TWIN_WEIGHTS:
            delta_w[n], new_m[n], new_v[n] = _adamw(weights[n], grad_w[n], given["m_" + n], given["v_" + n])
    return (loss, grad_x, *[grad_w[n] for n in TWIN_WEIGHTS], *[delta_w[n] for n in TWIN_WEIGHTS],
            *[new_m[n] for n in TWIN_WEIGHTS], *[new_v[n] for n in TWIN_WEIGHTS])
```

```python
import functools

import numpy as np
import jax
import jax.numpy as jnp
from jax import lax
from jax.experimental import pallas as pl
from jax.experimental.pallas import tpu as pltpu

F32 = jnp.float32
BF16 = jnp.bfloat16
MESH = pl.DeviceIdType.MESH

D_MODEL = 1024
SEQ = 2048
N_META = 16
BLOCK = 128
PAD_ROWS = 112
LP = PAD_ROWS + N_META + SEQ
N_CHUNK = LP // BLOCK
DEPTH = 2
D_FF = 2816
N_CHIPS = 4
FF_SHARD = D_FF // N_CHIPS
HP = 768
FP = N_CHIPS * HP
SSD_HEADS, SSD_HD, SSD_D, SSD_GROUPS, SSD_STATE, SSD_CONV = 8, 64, 512, 2, 64, 4
FOX_HEADS, FOX_HD, FOX_D = 4, 64, 256
MLA_HEADS, MLA_Q_LORA, MLA_KV_LORA, MLA_NOPE, MLA_ROPE, MLA_V, MLA_D = 4, 256, 128, 64, 32, 64, 256
ROPE_HALF = MLA_ROPE // 2
ROPE_THETA = 10000.0
N_IN = 2476
IN_SHARD = N_IN // N_CHIPS
IN_SHARD_P = 640
ALPHA = (2 * DEPTH) ** 0.25
EPS = 1e-5
ADAM_LR, ADAM_B1, ADAM_B2, ADAM_EPS, ADAM_WD, ADAM_STEP = 0.001, 0.9, 0.999, 1e-08, 0.01, 10
NEG = -1e30
TM = 544

VMEM_LIMIT_BYTES = 56 * 1024 * 1024

PC_Z, PC_XBC, PC_FQ, PC_FK, PC_FV, PC_CQ, PC_CKV, PC_DT, PC_FR, PC_KR, PC_END = (
    0, 512, 1280, 1536, 1792, 2048, 2304, 2432, 2560, 2688, 2816)
OC_Z, OC_XBC, OC_DT, OC_FQ, OC_FK, OC_FV, OC_FR, OC_CQ, OC_CKV, OC_KR = (
    0, 512, 1280, 1288, 1544, 1800, 2056, 2060, 2316, 2444)


def _cparams(sem=None):
    return pltpu.CompilerParams(dimension_semantics=sem, vmem_limit_bytes=VMEM_LIMIT_BYTES)


def _tile(n, cap, mult):
    best = None
    for t in range(mult, min(n, cap) + 1, mult):
        if n % t == 0:
            best = t
    return best if best is not None else n


def _bs(shape, fn):
    return pl.BlockSpec(shape, fn)


ANY = pl.BlockSpec(memory_space=pl.ANY)


def _dims(ca, cb):
    return (((ca,), (cb,)), ((), ()))


def _raw_bdot(a, b, ca, cb):
    return lax.dot_general(a.astype(BF16), b.astype(BF16), _dims(ca, cb), preferred_element_type=F32)


@functools.partial(jax.custom_vjp, nondiff_argnums=(2, 3))
def _bdot(a, b, ca, cb):
    return _raw_bdot(a, b, ca, cb)


def _bdot_fwd(a, b, ca, cb):
    return _raw_bdot(a, b, ca, cb), (a, b)


def _bdot_bwd(ca, cb, res, g):
    a, b = res
    if (ca, cb) == (1, 0):
        return _raw_bdot(g, b, 1, 1), _raw_bdot(a, g, 0, 0)
    if (ca, cb) == (1, 1):
        return _raw_bdot(g, b, 1, 0), _raw_bdot(g, a, 0, 0)
    if (ca, cb) == (0, 0):
        return _raw_bdot(b, g, 1, 1), _raw_bdot(a, g, 1, 0)
    raise NotImplementedError((ca, cb))


_bdot.defvjp(_bdot_fwd, _bdot_bwd)


def _mm_core(a, b, *, a_spec, b_spec, o_spec, grid, out_shape, ca, cb, name, add=None, into=None):
    nk = grid[2]
    has_add, has_into = add is not None, into is not None
    acc_shape = tuple(d for d in o_spec.block_shape if d is not None)

    def body(*refs):
        a_ref, b_ref = refs[0], refs[1]
        p = 2
        add_ref = refs[p] if has_add else None
        p += int(has_add) + int(has_into)
        o_ref, acc_ref = refs[p], refs[p + 1]
        k = pl.program_id(2)

        @pl.when(k == 0)
        def _():
            acc_ref[...] = jnp.zeros_like(acc_ref)

        acc_ref[...] += _raw_bdot(a_ref[...], b_ref[...], ca, cb)

        @pl.when(k == nk - 1)
        def _():
            r = acc_ref[...]
            if has_add:
                r = r + add_ref[...]
            o_ref[...] = r

    ins = [a, b] + ([add] if has_add else []) + ([into] if has_into else [])
    in_specs = [a_spec, b_spec] + ([o_spec] if has_add else []) + ([ANY] if has_into else [])
    return pl.pallas_call(
        body, name=name, grid=grid, in_specs=in_specs, out_specs=o_spec,
        out_shape=jax.ShapeDtypeStruct(out_shape, F32), scratch_shapes=[pltpu.VMEM(acc_shape, F32)],
        input_output_aliases=({len(ins) - 1: 0} if has_into else {}),
        compiler_params=_cparams(("parallel", "parallel", "arbitrary")),
    )(*ins)


def _mm(a, b, *, ta=False, tb=False, add=None, name):
    if ta:
        K, M = a.shape
    else:
        M, K = a.shape
    if tb:
        N, Kb = b.shape
    else:
        Kb, N = b.shape
    assert K == Kb, (a.shape, b.shape, ta, tb)
    tm = _tile(M, TM, 128 if ta else 8)
    tn = _tile(N, 512, 128)
    tk = _tile(K, 2816, 8 if (ta and not tb) else 128)
    a_spec = _bs((tk, tm), lambda i, j, k: (k, i)) if ta else _bs((tm, tk), lambda i, j, k: (i, k))
    b_spec = _bs((tn, tk), lambda i, j, k: (j, k)) if tb else _bs((tk, tn), lambda i, j, k: (k, j))
    return _mm_core(a, b, a_spec=a_spec, b_spec=b_spec, o_spec=_bs((tm, tn), lambda i, j, k: (i, j)),
                    grid=(M // tm, N // tn, K // tk), out_shape=(M, N), ca=0 if ta else 1, cb=1 if tb else 0,
                    name=name, add=add)


def _row_entry(r, ncol):
    if isinstance(r, tuple):
        return r
    return r, r.shape[1] // ncol, 0


def _rowwise(fn, rows, pars, out_cols, *, name, tile, ncol=1):
    rows = [_row_entry(r, ncol) for r in rows]
    L = rows[0][0].shape[0]
    nr, npar = len(rows), len(pars)
    in_specs = [_bs((tile, w), lambda g, i, o=o: (i, o + g)) for _, w, o in rows]
    in_specs += [_bs((p.shape[0], p.shape[1] // ncol), lambda g, i: (0, g)) for p in pars]
    out_specs = [_bs((tile, c // ncol), lambda g, i: (i, g)) for c in out_cols]

    def body(*refs):
        ins, outs = refs[:nr + npar], refs[nr + npar:]
        row0 = pl.program_id(1) * tile
        res = fn(row0, *[r[...] for r in ins])
        for o, v in zip(outs, res):
            o[...] = v

    return pl.pallas_call(
        body, name=name, grid=(ncol, L // tile), in_specs=in_specs, out_specs=out_specs,
        out_shape=[jax.ShapeDtypeStruct((L, c), F32) for c in out_cols],
        compiler_params=_cparams(("parallel", "parallel")),
    )(*[r[0] for r in rows], *pars)


def _rowwise_bwd(fn, rows, pars, douts, *, name, tile, ncol=1, row_grad=None):
    rows = [_row_entry(r, ncol) for r in rows]
    L = rows[0][0].shape[0]
    nr, npar, nd = len(rows), len(pars), len(douts)
    row_grad = [True] * nr if row_grad is None else row_grad
    in_specs = [_bs((tile, w), lambda g, i, o=o: (i, o + g)) for _, w, o in rows]
    in_specs += [_bs((p.shape[0], p.shape[1] // ncol), lambda g, i: (0, g)) for p in pars]
    in_specs += [_bs((tile, d.shape[1] // ncol), lambda g, i: (i, g)) for d in douts]
    g_widths = [w * ncol for (_, w, _), f in zip(rows, row_grad) if f]
    out_specs = [_bs((tile, w // ncol), lambda g, i: (i, g)) for w in g_widths]
    out_specs += [_bs((p.shape[0], p.shape[1] // ncol), lambda g, i: (0, g)) for p in pars]
    out_shape = [jax.ShapeDtypeStruct((L, w), F32) for w in g_widths]
    out_shape += [jax.ShapeDtypeStruct(p.shape, F32) for p in pars]

    def body(*refs):
        ins = refs[:nr + npar]
        dos = refs[nr + npar:nr + npar + nd]
        outs = refs[nr + npar + nd:]
        i = pl.program_id(1)
        row0 = i * tile
        _, vjp = jax.vjp(lambda *a: tuple(fn(row0, *a)), *[r[...] for r in ins])
        grads = vjp(tuple(d[...] for d in dos))
        o = 0
        for j in range(nr):
            if row_grad[j]:
                outs[o][...] = grads[j]
                o += 1
        for j in range(npar):
            g, ref = grads[nr + j], outs[o + j]

            @pl.when(i == 0)
            def _(g=g, ref=ref):
                ref[...] = g

            @pl.when(i > 0)
            def _(g=g, ref=ref):
                ref[...] += g

    res = pl.pallas_call(
        body, name=name, grid=(ncol, L // tile), in_specs=in_specs, out_specs=out_specs, out_shape=out_shape,
        compiler_params=_cparams(("parallel", "arbitrary")),
    )(*[r[0] for r in rows], *pars, *douts)
    return res[:len(g_widths)], res[len(g_widths):]


def _row_ids(row0, shape):
    return row0 + lax.broadcasted_iota(jnp.int32, shape, 0)


def _sigmoid(x):
    return 1.0 / (1.0 + jnp.exp(-x))


@jax.custom_vjp
def _softplus(x):
    return jnp.maximum(x, 0.0) + jnp.log(1.0 + jnp.exp(-jnp.abs(x)))


def _softplus_fwd(x):
    return _softplus(x), x


def _softplus_bwd(x, g):
    return (g * _sigmoid(x),)


_softplus.defvjp(_softplus_fwd, _softplus_bwd)


def _silu(x):
    return x * _sigmoid(x)


def _swiglu_fn(row0, g, u):
    return (_silu(g) * u,)


def _make_res_ln_fn(scale):
    def fn(row0, h, o, gam, bet):
        pre = ALPHA * h + scale * o
        mu = jnp.mean(pre, axis=-1, keepdims=True)
        xc = pre - mu
        var = jnp.mean(xc * xc, axis=-1, keepdims=True)
        return (xc * lax.rsqrt(var + EPS) * gam + bet,)
    return fn


def _ssd_pre_fn(row0, raw, bias):
    dt = _softplus(raw + bias)
    return (jnp.where(_row_ids(row0, raw.shape) >= PAD_ROWS, dt, 0.0),)


def _ssd_post_fn(row0, y, xs, z, dskip, normg):
    v = (y + dskip * xs) * _silu(z)
    v = v * lax.rsqrt(jnp.mean(v * v, axis=-1, keepdims=True) + EPS)
    return (v * normg,)


def _mla_norm_fn(row0, cq, ckv, gq, gkv):
    qn = cq * lax.rsqrt(jnp.mean(cq * cq, axis=-1, keepdims=True) + EPS) * gq
    cn = ckv * lax.rsqrt(jnp.mean(ckv * ckv, axis=-1, keepdims=True) + EPS) * gkv
    return qn, cn


def _rope_fn(row0, q, k, cosf, sins):
    return (q * cosf + pltpu.roll(q, 64, 1) * sins, k * cosf + pltpu.roll(k, 64, 1) * sins)


def _rope_t_fn(row0, gq, gk, cosf, sins):
    return (gq * cosf + pltpu.roll(gq * sins, 64, 1), gk * cosf + pltpu.roll(gk * sins, 64, 1))


def _conv_fwd(x, x_off, w, b, *, name):
    C = w.shape[1]

    def body(x_ref, w_ref, b_ref, o_ref):
        rows = lax.broadcasted_iota(jnp.int32, (LP, BLOCK), 0)
        xv = jnp.where(rows >= PAD_ROWS, x_ref[...], 0.0)
        acc = b_ref[...] + w_ref[3:4, :] * xv
        for k in range(SSD_CONV - 1):
            acc = acc + w_ref[k:k + 1, :] * pltpu.roll(xv, SSD_CONV - 1 - k, 0)
        o_ref[...] = _silu(acc)

    return pl.pallas_call(
        body, name=name, grid=(C // BLOCK,),
        in_specs=[_bs((LP, BLOCK), lambda j: (0, j + x_off)), _bs((SSD_CONV, BLOCK), lambda j: (0, j)),
                  _bs((1, BLOCK), lambda j: (0, j))],
        out_specs=_bs((LP, BLOCK), lambda j: (0, j)),
        out_shape=jax.ShapeDtypeStruct((LP, C), F32), compiler_params=_cparams(("parallel",)),
    )(x, w, b)


def _conv_bwd(x, x_off, w, b, dout, *, name):
    C = w.shape[1]

    def body(x_ref, w_ref, b_ref, do_ref, dx_ref, dw_ref, db_ref):
        rows = lax.broadcasted_iota(jnp.int32, (LP, BLOCK), 0)
        real = rows >= PAD_ROWS
        xv = jnp.where(real, x_ref[...], 0.0)
        shifted = [pltpu.roll(xv, SSD_CONV - 1 - k, 0) for k in range(SSD_CONV - 1)] + [xv]
        acc = b_ref[...]
        for k in range(SSD_CONV):
            acc = acc + w_ref[k:k + 1, :] * shifted[k]
        sig = _sigmoid(acc)
        dacc = jnp.where(real, do_ref[...] * (sig * (1.0 + acc * (1.0 - sig))), 0.0)
        db_ref[...] = jnp.sum(dacc, axis=0, keepdims=True)
        dx = w_ref[3:4, :] * dacc
        for k in range(SSD_CONV):
            dw_ref[k:k + 1, :] = jnp.sum(dacc * shifted[k], axis=0, keepdims=True)
            if k < SSD_CONV - 1:
                dx = dx + w_ref[k:k + 1, :] * pltpu.roll(dacc, LP - (SSD_CONV - 1 - k), 0)
        dx_ref[...] = jnp.where(real, dx, 0.0)

    return pl.pallas_call(
        body, name=name, grid=(C // BLOCK,),
        in_specs=[_bs((LP, BLOCK), lambda j: (0, j + x_off)), _bs((SSD_CONV, BLOCK), lambda j: (0, j)),
                  _bs((1, BLOCK), lambda j: (0, j)), _bs((LP, BLOCK), lambda j: (0, j))],
        out_specs=[_bs((LP, BLOCK), lambda j: (0, j)), _bs((SSD_CONV, BLOCK), lambda j: (0, j)),
                   _bs((1, BLOCK), lambda j: (0, j))],
        out_shape=[jax.ShapeDtypeStruct((LP, C), F32), jax.ShapeDtypeStruct((SSD_CONV, C), F32),
                   jax.ShapeDtypeStruct((1, C), F32)],
        compiler_params=_cparams(("parallel",)),
    )(x, w, b, dout)


def _ssd_chunk(x, bm, cm, dtc, dtr, alog, prev):
    lane = lax.broadcasted_iota(jnp.int32, alog.shape, 1)
    a_neg = -jnp.exp(jnp.sum(jnp.where(lane == 0, alog, 0.0), axis=1, keepdims=True))
    ac_in = dtc * a_neg
    ar_in = dtr * a_neg
    li = lax.broadcasted_iota(jnp.int32, (BLOCK, BLOCK), 0)
    si = lax.broadcasted_iota(jnp.int32, (BLOCK, BLOCK), 1)
    causal = li >= si
    acum_c = jnp.sum(jnp.where(causal, ar_in, 0.0), axis=1, keepdims=True)
    acum_r = jnp.sum(jnp.where(li <= si, ac_in, 0.0), axis=0, keepdims=True)
    total = jnp.sum(ar_in, axis=1, keepdims=True)
    seg = jnp.exp(jnp.where(causal, acum_c - acum_r, NEG))
    xdt = x * dtc
    cb = _bdot(cm, bm, 1, 1)
    y = _bdot(cb * seg, xdt, 1, 0) + _bdot(cm, prev, 1, 1) * jnp.exp(acum_c)
    st = _bdot(xdt, bm * jnp.exp(total - acum_c), 0, 0)
    return y, prev * jnp.exp(total) + st


def _ssd_specs(rev):
    ci = (lambda c: N_CHUNK - 1 - c) if rev else (lambda c: c)
    rep = SSD_HEADS // SSD_GROUPS
    x_spec = _bs((None, BLOCK, SSD_HD), lambda h, c: (h, ci(c), 0))
    g_spec = _bs((None, BLOCK, SSD_STATE), lambda h, c: (h // rep, ci(c), 0))
    dtc_spec = _bs((None, BLOCK, 1), lambda h, c: (h, ci(c), 0))
    dtr_spec = _bs((None, 1, BLOCK), lambda h, c: (h, 0, ci(c)))
    al_spec = _bs((None, 1, BLOCK), lambda h, c: (h, 0, 0))
    st_spec = _bs((None, None, SSD_HD, SSD_STATE), lambda h, c: (h, ci(c), 0, 0))
    return x_spec, g_spec, dtc_spec, dtr_spec, al_spec, st_spec


def _ssd_fwd(x, bm, cm, dtc, dtr, alog, *, name):
    x_spec, g_spec, dtc_spec, dtr_spec, al_spec, st_spec = _ssd_specs(False)

    def body(x_ref, b_ref, c_ref, dtc_ref, dtr_ref, al_ref, y_ref, prev_ref, state):
        @pl.when(pl.program_id(1) == 0)
        def _():
            state[...] = jnp.zeros_like(state)

        prev = state[...]
        prev_ref[...] = prev
        y, new = _ssd_chunk(x_ref[...], b_ref[...], c_ref[...], dtc_ref[...], dtr_ref[...], al_ref[...], prev)
        y_ref[...] = y
        state[...] = new

    return pl.pallas_call(
        body, name=name, grid=(SSD_HEADS, N_CHUNK),
        in_specs=[x_spec, g_spec, g_spec, dtc_spec, dtr_spec, al_spec], out_specs=[x_spec, st_spec],
        out_shape=[jax.ShapeDtypeStruct((SSD_HEADS, LP, SSD_HD), F32),
                   jax.ShapeDtypeStruct((SSD_HEADS, N_CHUNK, SSD_HD, SSD_STATE), F32)],
        scratch_shapes=[pltpu.VMEM((SSD_HD, SSD_STATE), F32)],
        compiler_params=_cparams(("parallel", "arbitrary")),
    )(x, bm, cm, dtc, dtr, alog)


def _ssd_bwd(x, bm, cm, dtc, dtr, alog, prevs, dy, *, name):
    x_spec, g_spec, dtc_spec, dtr_spec, al_spec, st_spec = _ssd_specs(True)

    def body(x_ref, b_ref, c_ref, dtc_ref, dtr_ref, al_ref, prev_ref, dy_ref,
             dx_ref, db_ref, dc_ref, ddtc_ref, ddtr_ref, dal_ref, dstate):
        c = pl.program_id(1)

        @pl.when(c == 0)
        def _():
            dstate[...] = jnp.zeros_like(dstate)

        _, vjp = jax.vjp(_ssd_chunk, x_ref[...], b_ref[...], c_ref[...], dtc_ref[...], dtr_ref[...], al_ref[...],
                         prev_ref[...])
        dx, db, dc, ddtc, ddtr, dal, dprev = vjp((dy_ref[...], dstate[...]))
        dx_ref[...] = dx
        db_ref[...] = db
        dc_ref[...] = dc
        ddtc_ref[...] = ddtc
        ddtr_ref[...] = ddtr
        dstate[...] = dprev

        @pl.when(c == 0)
        def _():
            dal_ref[...] = dal

        @pl.when(c > 0)
        def _():
            dal_ref[...] += dal

    hs = jax.ShapeDtypeStruct((SSD_HEADS, LP, SSD_HD), F32)
    return pl.pallas_call(
        body, name=name, grid=(SSD_HEADS, N_CHUNK),
        in_specs=[x_spec, g_spec, g_spec, dtc_spec, dtr_spec, al_spec, st_spec, x_spec],
        out_specs=[x_spec, x_spec, x_spec, dtc_spec, dtr_spec, al_spec],
        out_shape=[hs, hs, hs, jax.ShapeDtypeStruct((SSD_HEADS, LP, 1), F32),
                   jax.ShapeDtypeStruct((SSD_HEADS, 1, LP), F32), jax.ShapeDtypeStruct((SSD_HEADS, 1, BLOCK), F32)],
        scratch_shapes=[pltpu.VMEM((SSD_HD, SSD_STATE), F32)],
        compiler_params=_cparams(("parallel", "arbitrary")),
    )(x, bm, cm, dtc, dtr, alog, prevs, dy)


def _tri_dot(tri, v):
    hi = v.astype(BF16)
    r1 = v - hi.astype(F32)
    mid = r1.astype(BF16)
    lo = (r1 - mid.astype(F32)).astype(BF16)
    t = tri.astype(BF16)
    d = lambda p: lax.dot_general(t, p, _dims(1, 0), preferred_element_type=F32)
    return d(hi) + d(mid) + d(lo)


def _fox_gate_fwd(raw, raw_blk, bias, *, name):
    def body(raw_ref, b_ref, c_ref, ct_ref, carry):
        j = pl.program_id(0)

        @pl.when(j == 0)
        def _():
            carry[...] = jnp.zeros_like(carry)

        rows = j * BLOCK + lax.broadcasted_iota(jnp.int32, (BLOCK, BLOCK), 0)
        lf = jnp.where(rows >= PAD_ROWS, -_softplus(-(raw_ref[...] + b_ref[...])), 0.0)
        li = lax.broadcasted_iota(jnp.int32, (BLOCK, BLOCK), 0)
        si = lax.broadcasted_iota(jnp.int32, (BLOCK, BLOCK), 1)
        cv = _tri_dot(jnp.where(li >= si, 1.0, 0.0), lf) + carry[...]
        c_ref[...] = cv
        ct_ref[...] = cv.T
        carry[...] += jnp.sum(lf, axis=0, keepdims=True)

    return pl.pallas_call(
        body, name=name, grid=(N_CHUNK,),
        in_specs=[_bs((BLOCK, BLOCK), lambda j: (j, raw_blk)), _bs((1, BLOCK), lambda j: (0, 0))],
        out_specs=[_bs((BLOCK, BLOCK), lambda j: (j, 0)), _bs((BLOCK, BLOCK), lambda j: (0, j))],
        out_shape=[jax.ShapeDtypeStruct((LP, BLOCK), F32), jax.ShapeDtypeStruct((BLOCK, LP), F32)],
        scratch_shapes=[pltpu.VMEM((1, BLOCK), F32)], compiler_params=_cparams(("arbitrary",)),
    )(raw, bias)


def _fox_gate_bwd(raw, raw_blk, bias, dc, dct, *, name):
    rj = lambda j: N_CHUNK - 1 - j

    def body(raw_ref, b_ref, dc_ref, dct_ref, draw_ref, db_ref, carry):
        j = pl.program_id(0)

        @pl.when(j == 0)
        def _():
            carry[...] = jnp.zeros_like(carry)

        rows = (N_CHUNK - 1 - j) * BLOCK + lax.broadcasted_iota(jnp.int32, (BLOCK, BLOCK), 0)
        li = lax.broadcasted_iota(jnp.int32, (BLOCK, BLOCK), 0)
        si = lax.broadcasted_iota(jnp.int32, (BLOCK, BLOCK), 1)
        dcv = dc_ref[...] + dct_ref[...].T
        dlf = _tri_dot(jnp.where(li <= si, 1.0, 0.0), dcv) + carry[...]
        carry[...] += jnp.sum(dcv, axis=0, keepdims=True)
        draw = jnp.where(rows >= PAD_ROWS, dlf * (1.0 - _sigmoid(raw_ref[...] + b_ref[...])), 0.0)
        draw_ref[...] = draw
        dsum = jnp.sum(draw, axis=0, keepdims=True)

        @pl.when(j == 0)
        def _():
            db_ref[...] = dsum

        @pl.when(j > 0)
        def _():
            db_ref[...] += dsum

    return pl.pallas_call(
        body, name=name, grid=(N_CHUNK,),
        in_specs=[_bs((BLOCK, BLOCK), lambda j: (rj(j), raw_blk)), _bs((1, BLOCK), lambda j: (0, 0)),
                  _bs((BLOCK, BLOCK), lambda j: (rj(j), 0)), _bs((BLOCK, BLOCK), lambda j: (0, rj(j)))],
        out_specs=[_bs((BLOCK, BLOCK), lambda j: (rj(j), 0)), _bs((1, BLOCK), lambda j: (0, 0))],
        out_shape=[jax.ShapeDtypeStruct((LP, BLOCK), F32), jax.ShapeDtypeStruct((1, BLOCK), F32)],
        scratch_shapes=[pltpu.VMEM((1, BLOCK), F32)], compiler_params=_cparams(("arbitrary",)),
    )(raw, bias, dc, dct)


ATT_W = 256


def _lane_head(width, per, mod=None):
    lane = lax.broadcasted_iota(jnp.int32, (1, width), 1)
    if mod is not None:
        lane = lane % mod
    return lane // per


def _attn_mask(i):
    r = i * BLOCK + lax.broadcasted_iota(jnp.int32, (BLOCK, LP), 0)
    c = lax.broadcasted_iota(jnp.int32, (BLOCK, LP), 1)
    return (c <= r) & ((c >= PAD_ROWS) | (r < PAD_ROWS))


def _attn_specs(q, k, v, bias, rope):
    qspec = lambda blk, w=ATT_W: _bs((BLOCK, w), lambda i: (i, blk))
    fspec = lambda blk, w=ATT_W: _bs((LP, w), lambda i: (0, blk))
    ins = [q[0], k[0], v[0]]
    specs = [qspec(q[1]), fspec(k[1]), fspec(v[1])]
    if bias is not None:
        ins += [bias[0], bias[1]]
        specs += [qspec(0, BLOCK), _bs((BLOCK, LP), lambda i: (0, 0))]
    if rope is not None:
        ins += [rope[0][0], rope[1][0]]
        specs += [qspec(rope[0][1], BLOCK), fspec(rope[1][1], BLOCK)]
    return ins, specs, qspec, fspec


def _attn_fwd(q, k, v, *, scale, name, bias=None, rope=None):
    ins, specs, qspec, fspec = _attn_specs(q, k, v, bias, rope)
    has_bias, has_rope = bias is not None, rope is not None

    def body(*refs):
        it = iter(refs)
        q_ref, k_ref, v_ref = next(it), next(it), next(it)
        if has_bias:
            c_ref, ct_ref = next(it), next(it)
        if has_rope:
            qr_ref, kr_ref = next(it), next(it)
        o_ref, lse_ref = next(it), next(it)
        ok = _attn_mask(pl.program_id(0))
        qv, kv, vv = q_ref[...], k_ref[...], v_ref[...]
        hid, l128 = _lane_head(ATT_W, FOX_HD), _lane_head(BLOCK, 1)
        if has_rope:
            rid = _lane_head(BLOCK, ROPE_HALF, 64)
            qrv, krv = qr_ref[...], kr_ref[...]
        o_acc = jnp.zeros((BLOCK, ATT_W), F32)
        lse_acc = jnp.zeros((BLOCK, BLOCK), F32)
        for h in range(FOX_HEADS):
            s = _raw_bdot(jnp.where(hid == h, qv, 0.0), kv, 1, 1)
            if has_rope:
                s = s + _raw_bdot(jnp.where(rid == h, qrv, 0.0), krv, 1, 1)
            s = s * scale
            if has_bias:
                cq = jnp.sum(jnp.where(l128 == h, c_ref[...], 0.0), axis=1, keepdims=True)
                s = s + (cq - ct_ref[h:h + 1, :])
            s = jnp.where(ok, s, NEG)
            m = jnp.max(s, axis=1, keepdims=True)
            p = jnp.exp(s - m)
            l = jnp.sum(p, axis=1, keepdims=True)
            o_acc = jnp.where(hid == h, _raw_bdot(p, vv, 1, 0) / l, o_acc)
            lse_acc = jnp.where(l128 == h, m + jnp.log(l), lse_acc)
        o_ref[...] = o_acc
        lse_ref[...] = lse_acc

    return pl.pallas_call(
        body, name=name, grid=(N_CHUNK,), in_specs=specs, out_specs=[qspec(0), qspec(0, BLOCK)],
        out_shape=[jax.ShapeDtypeStruct((LP, ATT_W), F32), jax.ShapeDtypeStruct((LP, BLOCK), F32)],
        compiler_params=_cparams(("parallel",)),
    )(*ins)


def _attn_bwd(q, k, v, o, lse, do, *, scale, name, bias=None, rope=None):
    ins, specs, qspec, fspec = _attn_specs(q, k, v, bias, rope)
    has_bias, has_rope = bias is not None, rope is not None
    ins += [o, lse, do[0]]
    specs += [qspec(0), qspec(0, BLOCK), qspec(do[1])]

    def body(*refs):
        it = iter(refs)
        q_ref, k_ref, v_ref = next(it), next(it), next(it)
        if has_bias:
            c_ref, ct_ref = next(it), next(it)
        if has_rope:
            qr_ref, kr_ref = next(it), next(it)
        o_ref, lse_ref, do_ref = next(it), next(it), next(it)
        dq_ref, dk_ref, dv_ref = next(it), next(it), next(it)
        if has_bias:
            dc_ref, dct_ref = next(it), next(it)
        if has_rope:
            dqr_ref, dkr_ref = next(it), next(it)
        i = pl.program_id(0)
        ok = _attn_mask(i)
        qv, kv, vv, ov, dov, lsev = q_ref[...], k_ref[...], v_ref[...], o_ref[...], do_ref[...], lse_ref[...]
        hid, l128 = _lane_head(ATT_W, FOX_HD), _lane_head(BLOCK, 1)
        if has_rope:
            rid = _lane_head(BLOCK, ROPE_HALF, 64)
            qrv, krv = qr_ref[...], kr_ref[...]
            dqr_acc = jnp.zeros((BLOCK, BLOCK), F32)
            dkr_acc = jnp.zeros((LP, BLOCK), F32)
        dq_acc = jnp.zeros((BLOCK, ATT_W), F32)
        dk_acc = jnp.zeros((LP, ATT_W), F32)
        dv_acc = jnp.zeros((LP, ATT_W), F32)
        dc_acc = jnp.zeros((BLOCK, BLOCK), F32)
        dct_rows = []
        for h in range(FOX_HEADS):
            qm = jnp.where(hid == h, qv, 0.0)
            s = _raw_bdot(qm, kv, 1, 1)
            if has_rope:
                qrm = jnp.where(rid == h, qrv, 0.0)
                s = s + _raw_bdot(qrm, krv, 1, 1)
            s = s * scale
            if has_bias:
                cq = jnp.sum(jnp.where(l128 == h, c_ref[...], 0.0), axis=1, keepdims=True)
                s = s + (cq - ct_ref[h:h + 1, :])
            s = jnp.where(ok, s, NEG)
            p = jnp.exp(s - jnp.sum(jnp.where(l128 == h, lsev, 0.0), axis=1, keepdims=True))
            dom = jnp.where(hid == h, dov, 0.0)
            dp = _raw_bdot(dom, vv, 1, 1)
            delta = jnp.sum(dom * ov, axis=1, keepdims=True)
            ds = p * (dp - delta)
            dq_acc = jnp.where(hid == h, _raw_bdot(ds, kv, 1, 0) * scale, dq_acc)
            dk_acc = dk_acc + _raw_bdot(ds, qm, 0, 0) * scale
            dv_acc = dv_acc + _raw_bdot(p, dom, 0, 0)
            if has_rope:
                dqr_acc = jnp.where(rid == h, _raw_bdot(ds, krv, 1, 0) * scale, dqr_acc)
                dkr_acc = dkr_acc + _raw_bdot(ds, qrm, 0, 0) * scale
            if has_bias:
                dc_acc = jnp.where(l128 == h, jnp.sum(ds, axis=1, keepdims=True), dc_acc)
                dct_rows.append(-jnp.sum(ds, axis=0, keepdims=True))
        dq_ref[...] = dq_acc
        if has_bias:
            dc_ref[...] = dc_acc
        if has_rope:
            dqr_ref[...] = dqr_acc

        @pl.when(i == 0)
        def _():
            dk_ref[...] = dk_acc
            dv_ref[...] = dv_acc
            if has_rope:
                dkr_ref[...] = dkr_acc
            if has_bias:
                dct_ref[...] = jnp.zeros_like(dct_ref)
                for h in range(FOX_HEADS):
                    dct_ref[h:h + 1, :] = dct_rows[h]

        @pl.when(i > 0)
        def _():
            dk_ref[...] += dk_acc
            dv_ref[...] += dv_acc
            if has_rope:
                dkr_ref[...] += dkr_acc
            if has_bias:
                for h in range(FOX_HEADS):
                    dct_ref[h:h + 1, :] += dct_rows[h]

    wide = jax.ShapeDtypeStruct((LP, ATT_W), F32)
    narrow = jax.ShapeDtypeStruct((LP, BLOCK), F32)
    out_specs = [qspec(0), fspec(0), fspec(0)]
    out_shape = [wide, wide, wide]
    if has_bias:
        out_specs += [qspec(0, BLOCK), _bs((BLOCK, LP), lambda i: (0, 0))]
        out_shape += [narrow, jax.ShapeDtypeStruct((BLOCK, LP), F32)]
    if has_rope:
        out_specs += [qspec(0, BLOCK), fspec(0, BLOCK)]
        out_shape += [narrow, narrow]
    return pl.pallas_call(
        body, name=name, grid=(N_CHUNK,), in_specs=specs, out_specs=out_specs, out_shape=out_shape,
        compiler_params=_cparams(("arbitrary",)),
    )(*ins)


def _loss_head(y, target, *, name):
    tile = 272

    def body(y_ref, t_ref, dy_ref, loss_ref):
        i = pl.program_id(0)
        rows = i * tile + lax.broadcasted_iota(jnp.int32, (tile, D_MODEL), 0)
        err = jnp.where(rows >= BLOCK, y_ref[...] - t_ref[...], 0.0)
        dy_ref[...] = err * (1.0 / D_MODEL)
        part = 0.5 * jnp.sum(jnp.sum(err * err, axis=1, keepdims=True) * (1.0 / D_MODEL), axis=0, keepdims=True)
        part = jnp.broadcast_to(part, (1, BLOCK))

        @pl.when(i == 0)
        def _():
            loss_ref[...] = part

        @pl.when(i > 0)
        def _():
            loss_ref[...] += part

    return pl.pallas_call(
        body, name=name, grid=(LP // tile,),
        in_specs=[_bs((tile, D_MODEL), lambda i: (i, 0)), _bs((tile, D_MODEL), lambda i: (i, 0))],
        out_specs=[_bs((tile, D_MODEL), lambda i: (i, 0)), _bs((1, BLOCK), lambda i: (0, 0))],
        out_shape=[jax.ShapeDtypeStruct((LP, D_MODEL), F32), jax.ShapeDtypeStruct((1, BLOCK), F32)],
        compiler_params=_cparams(("arbitrary",)),
    )(y, target)


def _adamw(w, g, m, v, *, name):
    if w.ndim == 2:
        w, g, m, v = w[None], g[None], m[None], v[None]
        squeeze = True
    else:
        squeeze = False
    NL, R, C = w.shape
    CG = g.shape[2]
    tile = _tile(R, 256, 8)

    def body(w_ref, g_ref, m_ref, v_ref, go_ref, d_ref, nm_ref, nv_ref):
        gv = g_ref[:, :C]
        nm = ADAM_B1 * m_ref[...] + (1.0 - ADAM_B1) * gv
        nv = ADAM_B2 * v_ref[...] + (1.0 - ADAM_B2) * (gv * gv)
        m_hat = nm / (1.0 - ADAM_B1 ** ADAM_STEP)
        v_hat = nv / (1.0 - ADAM_B2 ** ADAM_STEP)
        go_ref[...] = gv
        d_ref[...] = -ADAM_LR * (m_hat / (jnp.sqrt(v_hat) + ADAM_EPS) + ADAM_WD * w_ref[...])
        nm_ref[...] = nm
        nv_ref[...] = nv

    spec = _bs((None, tile, C), lambda l, i: (l, i, 0))
    gspec = _bs((None, tile, CG), lambda l, i: (l, i, 0))
    res = pl.pallas_call(
        body, name=name, grid=(NL, R // tile), in_specs=[spec, gspec, spec, spec], out_specs=[spec] * 4,
        out_shape=[jax.ShapeDtypeStruct((NL, R, C), F32)] * 4, compiler_params=_cparams(("parallel", "parallel")),
    )(w, g, m, v)
    return [r[0] for r in res] if squeeze else res


def _my_pos():
    return lax.axis_index("x"), lax.axis_index("y"), lax.axis_index("c")


def _other_chips(x, y):
    return [(1 - x, y), (x, 1 - y), (1 - x, 1 - y)]


def _allgather_chips(shards):
    n = len(shards)
    per = 7

    def body(*refs):
        ins, outs = refs[:n], refs[n:2 * n]
        send_sems, recv_sems = refs[2 * n], refs[2 * n + 1]
        x, y, c = _my_pos()
        chips = _other_chips(x, y)
        sibling, me = (x, y, 1 - c), 2 * x + y

        def cp(a, kk, src, dst, to):
            return pltpu.make_async_remote_copy(src_ref=src, dst_ref=dst, send_sem=send_sems.at[per * a + kk],
                                                recv_sem=recv_sems.at[per * a + kk], device_id=to, device_id_type=MESH)

        sends = []
        for a in range(n):
            for j, chip in enumerate(chips):
                sends.append(cp(a, j, ins[a].at[c], outs[a].at[me, c], (*chip, c)))
            sends.append(cp(a, 3, ins[a], outs[a].at[me], sibling))
        for s in sends:
            s.start()
        for a in range(n):
            for j, chip in enumerate(chips):
                slab = outs[a].at[2 * chip[0] + chip[1], c]
                cp(a, j, slab, slab, (x, y, c)).wait_recv()
                fwd = cp(a, 4 + j, slab, slab, sibling)
                fwd.start()
                sends.append(fwd)
        for a in range(n):
            cp(a, 3, ins[a], outs[a].at[me], (x, y, c)).wait_recv()
            for j, chip in enumerate(chips):
                slab = outs[a].at[2 * chip[0] + chip[1], 1 - c]
                cp(a, 4 + j, slab, slab, (x, y, c)).wait_recv()
        for s in sends:
            s.wait_send()

    return pl.pallas_call(
        body, name="allgather_chips", in_specs=[ANY] * n, out_specs=[ANY] * n,
        out_shape=[jax.ShapeDtypeStruct((N_CHIPS,) + s.shape, s.dtype) for s in shards],
        scratch_shapes=[pltpu.SemaphoreType.DMA((per * n,)), pltpu.SemaphoreType.DMA((per * n,))],
    )(*shards)


def _rs_swap_layers(gs):
    n = len(gs)

    def body(*refs):
        ins, outs = refs[:n], refs[n:2 * n]
        send_sems, recv_sems = refs[2 * n], refs[2 * n + 1]
        x, y, c = _my_pos()
        cps = [pltpu.make_async_remote_copy(
            src_ref=ins[a].at[:, 1 - c], dst_ref=outs[a], send_sem=send_sems.at[a], recv_sem=recv_sems.at[a],
            device_id=(x, y, 1 - c), device_id_type=MESH) for a in range(n)]
        for cp in cps:
            cp.start()
        for cp in cps:
            cp.wait()

    return pl.pallas_call(
        body, name="rs_swap_layers", in_specs=[ANY] * n, out_specs=[ANY] * n,
        out_shape=[jax.ShapeDtypeStruct((N_CHIPS,) + g.shape[2:], g.dtype) for g in gs],
        scratch_shapes=[pltpu.SemaphoreType.DMA((n,)), pltpu.SemaphoreType.DMA((n,))],
    )(*gs)


def _rs_add_pair(g, r, pos, *, name):
    _, _, R, C = g.shape
    tile = _tile(R, 512, 16)

    def body(pos_ref, g_ref, r_ref, o32_ref, o16_ref):
        s = g_ref[...] + r_ref[...]
        o32_ref[...] = s
        o16_ref[...] = s.astype(BF16)

    spec = _bs((None, tile, C), lambda k, i, pos_ref: (k, i, 0))
    grid_spec = pltpu.PrefetchScalarGridSpec(
        num_scalar_prefetch=1, grid=(N_CHIPS, R // tile),
        in_specs=[_bs((None, None, tile, C), lambda k, i, pos_ref: (k, pos_ref[1], i, 0)), spec],
        out_specs=[spec, spec])
    return pl.pallas_call(
        body, name=name, grid_spec=grid_spec,
        out_shape=[jax.ShapeDtypeStruct((N_CHIPS, R, C), F32), jax.ShapeDtypeStruct((N_CHIPS, R, C), BF16)],
        compiler_params=_cparams(("parallel", "parallel")),
    )(pos, g, r)


def _rs_exchange_chips(ps):
    n = len(ps)

    def body(*refs):
        ins, outs = refs[:n], refs[n:2 * n]
        send_sems, recv_sems = refs[2 * n], refs[2 * n + 1]
        x, y, c = _my_pos()
        cps = []
        for a in range(n):
            for j, chip in enumerate(_other_chips(x, y)):
                cps.append(pltpu.make_async_remote_copy(
                    src_ref=ins[a].at[2 * chip[0] + chip[1]], dst_ref=outs[a].at[j], send_sem=send_sems.at[3 * a + j],
                    recv_sem=recv_sems.at[3 * a + j], device_id=(*chip, c), device_id_type=MESH))
        for cp in cps:
            cp.start()
        for cp in cps:
            cp.wait()

    return pl.pallas_call(
        body, name="rs_exchange_chips", in_specs=[ANY] * n, out_specs=[ANY] * n,
        out_shape=[jax.ShapeDtypeStruct((3,) + p.shape[1:], p.dtype) for p in ps],
        scratch_shapes=[pltpu.SemaphoreType.DMA((3 * n,)), pltpu.SemaphoreType.DMA((3 * n,))],
    )(*ps)


def _rs_add_chips(p32, r16, pos, *, name):
    _, R, C = p32.shape
    tile = _tile(R, 512, 16)

    def body(pos_ref, p_ref, r_ref, o_ref):
        o_ref[...] = ((p_ref[...] + r_ref[0].astype(F32)) + r_ref[1].astype(F32)) + r_ref[2].astype(F32)

    grid_spec = pltpu.PrefetchScalarGridSpec(
        num_scalar_prefetch=1, grid=(R // tile,),
        in_specs=[_bs((None, tile, C), lambda i, pos_ref: (pos_ref[0], i, 0)),
                  _bs((3, tile, C), lambda i, pos_ref: (0, i, 0))],
        out_specs=_bs((None, tile, C), lambda i, pos_ref: (pos_ref[1], i, 0)))
    return pl.pallas_call(
        body, name=name, grid_spec=grid_spec, out_shape=jax.ShapeDtypeStruct((2, R, C), F32),
        compiler_params=_cparams(("parallel",)),
    )(pos, p32, r16)


def _rs_join_layers(fs):
    n = len(fs)

    def body(*refs):
        outs = refs[n:2 * n]
        send_sems, recv_sems = refs[2 * n], refs[2 * n + 1]
        x, y, c = _my_pos()
        for a in range(n):
            pltpu.make_async_remote_copy(src_ref=outs[a].at[c], dst_ref=outs[a].at[c], send_sem=send_sems.at[a],
                                         recv_sem=recv_sems.at[a], device_id=(x, y, 1 - c), device_id_type=MESH).start()
        for a in range(n):
            pltpu.make_async_remote_copy(src_ref=outs[a].at[c], dst_ref=outs[a].at[1 - c], send_sem=send_sems.at[a],
                                         recv_sem=recv_sems.at[a], device_id=(x, y, 1 - c), device_id_type=MESH).wait()

    return pl.pallas_call(
        body, name="rs_join_layers", in_specs=[ANY] * n, out_specs=[ANY] * n,
        out_shape=[jax.ShapeDtypeStruct(f.shape, f.dtype) for f in fs],
        input_output_aliases={a: a for a in range(n)},
        scratch_shapes=[pltpu.SemaphoreType.DMA((n,)), pltpu.SemaphoreType.DMA((n,))],
    )(*fs)


def _reduce_scatter(gs, names):
    x, y, c = _my_pos()
    pos = jnp.stack([2 * x + y, c]).astype(jnp.int32)
    r1 = _rs_swap_layers(gs)
    pairs = [_rs_add_pair(g, r, pos, name=f"rs_add_pair_{nm}") for g, r, nm in zip(gs, r1, names)]
    r2 = _rs_exchange_chips([p[1] for p in pairs])
    fs = [_rs_add_chips(p[0], r, pos, name=f"rs_add_chips_{nm}") for p, r, nm in zip(pairs, r2, names)]
    return _rs_join_layers(fs)


def _allreduce_small(buf):
    R, W = buf.shape

    def body(b_ref, o_ref, gather, send_sems, recv_sems):
        x, y, c = _my_pos()
        me = 4 * x + 2 * y + c
        gather[me] = b_ref[...]
        cps = []
        for d in range(1, 8):
            peer = (x ^ (d >> 2), y ^ ((d >> 1) & 1), c ^ (d & 1))
            cps.append(pltpu.make_async_remote_copy(
                src_ref=b_ref, dst_ref=gather.at[me], send_sem=send_sems.at[d - 1], recv_sem=recv_sems.at[d - 1],
                device_id=peer, device_id_type=MESH))
        for cp in cps:
            cp.start()
        for d in range(1, 8):
            pltpu.make_async_remote_copy(
                src_ref=b_ref, dst_ref=gather.at[me ^ d], send_sem=send_sems.at[d - 1], recv_sem=recv_sems.at[d - 1],
                device_id=(x, y, c), device_id_type=MESH).wait_recv()
        for cp in cps:
            cp.wait_send()
        acc = gather[0]
        for d in range(1, 8):
            acc = acc + gather[d]
        o_ref[...] = acc

    vm = pl.BlockSpec(memory_space=pltpu.VMEM)
    return pl.pallas_call(
        body, name="allreduce_small", in_specs=[vm], out_specs=vm, out_shape=jax.ShapeDtypeStruct((R, W), F32),
        scratch_shapes=[pltpu.VMEM((8, R, W), F32), pltpu.SemaphoreType.DMA((7,)), pltpu.SemaphoreType.DMA((7,))],
    )(buf)


def _heads(a, h, d):
    return a.reshape(a.shape[0], h, d).transpose(1, 0, 2)


def _unheads(a):
    h, L, d = a.shape
    return a.transpose(1, 0, 2).reshape(L, h * d)


def _rope_tables():
    pos = jnp.maximum(jnp.arange(LP, dtype=F32) - PAD_ROWS, 0.0)
    inv_freq = 1.0 / (ROPE_THETA ** (jnp.arange(0, MLA_ROPE, 2, dtype=F32) / MLA_ROPE))
    ang = pos[:, None] * inv_freq[None, :]
    cos, sin = jnp.tile(jnp.cos(ang), (1, MLA_HEADS)), jnp.tile(jnp.sin(ang), (1, MLA_HEADS))
    return jnp.concatenate([cos, cos], axis=1), jnp.concatenate([-sin, sin], axis=1)


def _lane_pad(a, width=BLOCK):
    return jnp.pad(a, ((0, 0), (0, width - a.shape[1])))


def _pad_in_proj(w):
    sl = lambda start, size: w[:, start:start + size]
    return jnp.concatenate([
        sl(OC_Z, 512), sl(OC_XBC, 768), sl(OC_FQ, 256), sl(OC_FK, 256), sl(OC_FV, 256), sl(OC_CQ, 256), sl(OC_CKV, 128),
        _lane_pad(sl(OC_DT, SSD_HEADS)), _lane_pad(sl(OC_FR, FOX_HEADS)),
        jnp.tile(sl(OC_KR, ROPE_HALF), (1, MLA_HEADS)), jnp.tile(sl(OC_KR + ROPE_HALF, ROPE_HALF), (1, MLA_HEADS))], axis=1)


def _unpad_in_proj(wp):
    sl = lambda start, size: wp[:, start:start + size]
    rope = lambda start: sl(start, 64).reshape(wp.shape[0], MLA_HEADS, ROPE_HALF).sum(axis=1)
    return jnp.concatenate([
        sl(PC_Z, 512), sl(PC_XBC, 768), sl(PC_DT, SSD_HEADS), sl(PC_FQ, 256), sl(PC_FK, 256), sl(PC_FV, 256),
        sl(PC_FR, FOX_HEADS), sl(PC_CQ, 256), sl(PC_CKV, 128), rope(PC_KR), rope(PC_KR + 64)], axis=1)


def _regroup_uq(w):
    w3 = w.reshape(w.shape[0], MLA_HEADS, MLA_NOPE + MLA_ROPE)
    return jnp.concatenate([w3[:, :, :MLA_NOPE].reshape(w.shape[0], -1),
                            w3[:, :, MLA_NOPE:MLA_NOPE + ROPE_HALF].reshape(w.shape[0], -1),
                            w3[:, :, MLA_NOPE + ROPE_HALF:].reshape(w.shape[0], -1)], axis=1)


def _ungroup_uq(wp):
    n = wp.shape[0]
    return jnp.concatenate([wp[:, :256].reshape(n, MLA_HEADS, MLA_NOPE), wp[:, 256:320].reshape(n, MLA_HEADS, ROPE_HALF),
                            wp[:, 320:].reshape(n, MLA_HEADS, ROPE_HALF)], axis=2).reshape(n, -1)


def _regroup_ukv(w):
    w3 = w.reshape(w.shape[0], MLA_HEADS, MLA_NOPE + MLA_V)
    return jnp.concatenate([w3[:, :, :MLA_NOPE].reshape(w.shape[0], -1), w3[:, :, MLA_NOPE:].reshape(w.shape[0], -1)],
                           axis=1)


def _ungroup_ukv(wp):
    n = wp.shape[0]
    return jnp.concatenate([wp[:, :256].reshape(n, MLA_HEADS, MLA_NOPE), wp[:, 256:].reshape(n, MLA_HEADS, MLA_V)],
                           axis=2).reshape(n, -1)


N_I = LP // TM


def _chunk_cols_mm(a, w, l, chunk_w, *, name):
    K = a.shape[1]
    return _mm_core(a, w, a_spec=_bs((TM, K), lambda i, j, k: (i, 0)),
                    b_spec=_bs((None, None, K, chunk_w), lambda i, j, k: (j, l, 0, 0)),
                    o_spec=_bs((TM, chunk_w), lambda i, j, k: (i, j)), grid=(N_I, N_CHIPS, 1),
                    out_shape=(LP, N_CHIPS * chunk_w), ca=1, cb=0, name=name)


def _chunk_cols_dx(g, w, l, chunk_w, add, *, name):
    K = w.shape[2]
    tn = _tile(K, 512, 128)
    return _mm_core(g, w, a_spec=_bs((TM, chunk_w), lambda i, j, k: (i, k)),
                    b_spec=_bs((None, None, tn, chunk_w), lambda i, j, k: (k, l, j, 0)),
                    o_spec=_bs((TM, tn), lambda i, j, k: (i, j)), grid=(N_I, K // tn, N_CHIPS),
                    out_shape=(LP, K), ca=1, cb=1, name=name, add=add)


def _chunk_cols_dw(a, g, l, chunk_w, into, *, name):
    K = a.shape[1]
    tm = _tile(K, 512, 128)
    return _mm_core(a, g, a_spec=_bs((LP, tm), lambda i, j, k: (0, i)), b_spec=_bs((LP, chunk_w), lambda i, j, k: (0, j)),
                    o_spec=_bs((None, None, tm, chunk_w), lambda i, j, k: (j, l, i, 0)), grid=(K // tm, N_CHIPS, 1),
                    out_shape=(N_CHIPS, DEPTH, K, chunk_w), ca=0, cb=0, name=name, into=into)


def _chunk_rows_mm(a, w, l, chunk_h, *, name):
    N = w.shape[3]
    tn = _tile(N, 512, 128)
    return _mm_core(a, w, a_spec=_bs((TM, chunk_h), lambda i, j, k: (i, k)),
                    b_spec=_bs((None, None, chunk_h, tn), lambda i, j, k: (k, l, 0, j)),
                    o_spec=_bs((TM, tn), lambda i, j, k: (i, j)), grid=(N_I, N // tn, N_CHIPS),
                    out_shape=(LP, N), ca=1, cb=0, name=name)


def _chunk_rows_dx(g, w, l, chunk_h, *, name):
    N = w.shape[3]
    return _mm_core(g, w, a_spec=_bs((TM, N), lambda i, j, k: (i, 0)),
                    b_spec=_bs((None, None, chunk_h, N), lambda i, j, k: (j, l, 0, 0)),
                    o_spec=_bs((TM, chunk_h), lambda i, j, k: (i, j)), grid=(N_I, N_CHIPS, 1),
                    out_shape=(LP, N_CHIPS * chunk_h), ca=1, cb=1, name=name)


def _chunk_rows_dw(a, g, l, chunk_h, into, *, name):
    N = g.shape[1]
    tn = _tile(N, 512, 128)
    return _mm_core(a, g, a_spec=_bs((LP, chunk_h), lambda i, j, k: (0, i)), b_spec=_bs((LP, tn), lambda i, j, k: (0, j)),
                    o_spec=_bs((None, None, chunk_h, tn), lambda i, j, k: (i, l, 0, j)), grid=(N_CHIPS, N // tn, 1),
                    out_shape=(N_CHIPS, DEPTH, chunk_h, N), ca=0, cb=0, name=name, into=into)


def _ffn_fwd(h, W, pre, l, gam, bet, tag):
    g = _chunk_cols_mm(h, W[pre + "_w_gate"], l, HP, name=f"{tag}_gate")
    u = _chunk_cols_mm(h, W[pre + "_w_up"], l, HP, name=f"{tag}_up")
    (act,) = _rowwise(_swiglu_fn, [g, u], [], [FP], name=f"{tag}_swiglu", tile=136)
    o = _chunk_rows_mm(act, W[pre + "_w_down"], l, HP, name=f"{tag}_down")
    (out,) = _rowwise(_make_res_ln_fn(0.5), [h, o], [gam, bet], [D_MODEL], name=f"{tag}_ln", tile=272)
    return out, (h, g, u, act, o)


def _ffn_bwd(dout, saved, W, pre, l, gam, bet, GB, tag):
    h, g, u, act, o = saved
    (dh_a, do), (dgam, dbet) = _rowwise_bwd(_make_res_ln_fn(0.5), [h, o], [gam, bet], [dout], name=f"{tag}_ln_bwd",
                                            tile=272)
    dact = _chunk_rows_dx(do, W[pre + "_w_down"], l, HP, name=f"{tag}_down_dx")
    GB[pre + "_w_down"] = _chunk_rows_dw(act, do, l, HP, GB.get(pre + "_w_down"), name=f"{tag}_down_dw")
    (dg, du), _ = _rowwise_bwd(_swiglu_fn, [g, u], [], [dact], name=f"{tag}_swiglu_bwd", tile=136)
    GB[pre + "_w_gate"] = _chunk_cols_dw(h, dg, l, HP, GB.get(pre + "_w_gate"), name=f"{tag}_gate_dw")
    GB[pre + "_w_up"] = _chunk_cols_dw(h, du, l, HP, GB.get(pre + "_w_up"), name=f"{tag}_up_dw")
    dh = _chunk_cols_dx(dg, W[pre + "_w_gate"], l, HP, dh_a, name=f"{tag}_gate_dx")
    dh = _chunk_cols_dx(du, W[pre + "_w_up"], l, HP, dh, name=f"{tag}_up_dx")
    return dh, dgam, dbet


def _mixer_fwd(h1, W, l, cosf, sins):
    tag = f"l{l}"
    proj = _mm(h1, W["w_in_p"][l], name=f"{tag}_in_proj")
    sv = {"h1": h1, "proj": proj}
    conv_w, conv_b = W["conv_w"][l], W["conv_b"][l][None]
    xc = _conv_fwd(proj, PC_XBC // BLOCK, conv_w, conv_b, name=f"{tag}_conv")
    dt_bias = _lane_pad(W["dt_bias"][l][None])
    (dt,) = _rowwise(_ssd_pre_fn, [(proj, BLOCK, PC_DT // BLOCK)], [dt_bias], [BLOCK], name=f"{tag}_ssd_dt", tile=272)
    xh = _heads(xc[:, :SSD_D], SSD_HEADS, SSD_HD)
    bm = _heads(xc[:, SSD_D:SSD_D + 128], SSD_GROUPS, SSD_STATE)
    cm = _heads(xc[:, SSD_D + 128:], SSD_GROUPS, SSD_STATE)
    dt8 = dt[:, :SSD_HEADS].T
    dtc, dtr = dt8[:, :, None], dt8[:, None, :]
    alog = jnp.broadcast_to(W["a_log"][l][:, None, None], (SSD_HEADS, 1, BLOCK))
    yh, prevs = _ssd_fwd(xh, bm, cm, dtc, dtr, alog, name=f"{tag}_ssd")
    y_raw = _unheads(yh)
    dskip = jnp.repeat(W["d_skip"][l], SSD_HD)[None]
    normg = W["ssd_norm_g"][l][None]
    post_rows = [y_raw, (xc, 256, 0), (proj, 256, PC_Z // 256)]
    (y_ssd,) = _rowwise(_ssd_post_fn, post_rows, [dskip, normg], [SSD_D], name=f"{tag}_ssd_post", tile=272,
                        ncol=SSD_GROUPS)
    sv.update(conv_w=conv_w, conv_b=conv_b, dt_bias=dt_bias, xh=xh, bm=bm, cm=cm, dtc=dtc, dtr=dtr, alog=alog,
              prevs=prevs, post_rows=post_rows, dskip=dskip, normg=normg)
    f_b = _lane_pad(W["fox_f_b"][l][None])
    cg, cgt = _fox_gate_fwd(proj, PC_FR // BLOCK, f_b, name=f"{tag}_fox_gate")
    fox_qkv = ((proj, PC_FQ // ATT_W), (proj, PC_FK // ATT_W), (proj, PC_FV // ATT_W))
    y_fox, lse_f = _attn_fwd(*fox_qkv, scale=FOX_HD ** -0.5, name=f"{tag}_fox_attn", bias=(cg, cgt))
    sv.update(f_b=f_b, cg=cg, cgt=cgt, fox_qkv=fox_qkv, y_fox=y_fox, lse_f=lse_f)
    gq, gkv = W["mla_q_norm_g"][l][None], W["mla_kv_norm_g"][l][None]
    norm_rows = [(proj, 256, PC_CQ // 256), (proj, BLOCK, PC_CKV // BLOCK)]
    qn, cn = _rowwise(_mla_norm_fn, norm_rows, [gq, gkv], [MLA_Q_LORA, MLA_KV_LORA], name=f"{tag}_mla_norm", tile=272)
    qh = _mm(qn, W["mla_w_uq_p"][l], name=f"{tag}_mla_uq")
    kvh = _mm(cn, W["mla_w_ukv_p"][l], name=f"{tag}_mla_ukv")
    qr, kr = _rowwise(_rope_fn, [(qh, BLOCK, 2), (proj, BLOCK, PC_KR // BLOCK), cosf, sins], [], [BLOCK, BLOCK],
                      name=f"{tag}_rope", tile=272)
    mla_qkv = ((qh, 0), (kvh, 0), (kvh, 1))
    y_mla, lse_m = _attn_fwd(*mla_qkv, scale=(MLA_NOPE + MLA_ROPE) ** -0.5, name=f"{tag}_mla_attn",
                             rope=((qr, 0), (kr, 0)))
    sv.update(gq=gq, gkv=gkv, norm_rows=norm_rows, qn=qn, cn=cn, qr=qr, kr=kr, mla_qkv=mla_qkv, y_mla=y_mla, lse_m=lse_m)
    ycat = jnp.concatenate([y_ssd, y_fox, y_mla], axis=1)
    mix = _chunk_rows_mm(ycat, W["w_out"], l, 256, name=f"{tag}_out_proj")
    (h2,) = _rowwise(_make_res_ln_fn(1.0), [h1, mix], [W["ln2_g"][l][None], W["ln2_b"][l][None]], [D_MODEL],
                     name=f"{tag}_ln2", tile=272)
    sv.update(mix=mix, ycat=ycat)
    return h2, sv


def _mixer_bwd(dh2, sv, W, l, cosf, sins, GB):
    tag = f"l{l}"
    G = {}
    proj = sv["proj"]
    ln2g, ln2b = W["ln2_g"][l][None], W["ln2_b"][l][None]
    (dh1_a, dmix), (dln2g, dln2b) = _rowwise_bwd(
        _make_res_ln_fn(1.0), [sv["h1"], sv["mix"]], [ln2g, ln2b], [dh2], name=f"{tag}_ln2_bwd", tile=272)
    G["ln2_g"], G["ln2_b"] = dln2g[0], dln2b[0]
    dycat = _chunk_rows_dx(dmix, W["w_out"], l, 256, name=f"{tag}_out_proj_dx")
    GB["w_out"] = _chunk_rows_dw(sv["ycat"], dmix, l, 256, GB.get("w_out"), name=f"{tag}_out_proj_dw")
    (dy_raw, dxs_a, dz), (ddskip, dnormg) = _rowwise_bwd(
        _ssd_post_fn, sv["post_rows"], [sv["dskip"], sv["normg"]], [dycat[:, :SSD_D]],
        name=f"{tag}_ssd_post_bwd", tile=272, ncol=SSD_GROUPS)
    G["ssd_norm_g"] = dnormg[0]
    G["d_skip"] = ddskip.reshape(SSD_HEADS, SSD_HD).sum(axis=1)
    dxh, dbh, dch, ddtc, ddtr, dal = _ssd_bwd(sv["xh"], sv["bm"], sv["cm"], sv["dtc"], sv["dtr"], sv["alog"],
                                              sv["prevs"], _heads(dy_raw, SSD_HEADS, SSD_HD), name=f"{tag}_ssd_bwd")
    G["a_log"] = dal[:, 0, 0]
    rep = SSD_HEADS // SSD_GROUPS
    dbm = dbh.reshape(SSD_GROUPS, rep, LP, SSD_STATE).sum(axis=1)
    dcm = dch.reshape(SSD_GROUPS, rep, LP, SSD_STATE).sum(axis=1)
    dxc = jnp.concatenate([dxs_a + _unheads(dxh), _unheads(dbm), _unheads(dcm)], axis=1)
    dxbc, G["conv_w"], dconv_b = _conv_bwd(proj, PC_XBC // BLOCK, sv["conv_w"], sv["conv_b"], dxc,
                                           name=f"{tag}_conv_bwd")
    G["conv_b"] = dconv_b[0]
    ddt = _lane_pad((ddtc[:, :, 0] + ddtr[:, 0, :]).T)
    (ddt_raw,), (ddt_bias,) = _rowwise_bwd(_ssd_pre_fn, [(proj, BLOCK, PC_DT // BLOCK)], [sv["dt_bias"]], [ddt],
                                           name=f"{tag}_ssd_dt_bwd", tile=272)
    G["dt_bias"] = ddt_bias[0, :SSD_HEADS]
    dfq, dfk, dfv, dcg, dcgt = _attn_bwd(*sv["fox_qkv"], sv["y_fox"], sv["lse_f"], (dycat, SSD_D // ATT_W),
                                         scale=FOX_HD ** -0.5, name=f"{tag}_fox_attn_bwd", bias=(sv["cg"], sv["cgt"]))
    df_raw, dfb = _fox_gate_bwd(proj, PC_FR // BLOCK, sv["f_b"], dcg, dcgt, name=f"{tag}_fox_gate_bwd")
    G["fox_f_b"] = dfb[0, :FOX_HEADS]
    dqn_h, dkn_h, dv_h, dqr, dkr = _attn_bwd(
        *sv["mla_qkv"], sv["y_mla"], sv["lse_m"], (dycat, (SSD_D + FOX_D) // ATT_W),
        scale=(MLA_NOPE + MLA_ROPE) ** -0.5, name=f"{tag}_mla_attn_bwd", rope=((sv["qr"], 0), (sv["kr"], 0)))
    dq_rope, dk_rope = _rowwise(_rope_t_fn, [dqr, dkr, cosf, sins], [], [BLOCK, BLOCK], name=f"{tag}_rope_bwd",
                                tile=272)
    dqh = jnp.concatenate([dqn_h, dq_rope], axis=1)
    dkvh = jnp.concatenate([dkn_h, dv_h], axis=1)
    dqn = _mm(dqh, W["mla_w_uq_p"][l], tb=True, name=f"{tag}_mla_uq_dx")
    G["mla_w_uq_p"] = _mm(sv["qn"], dqh, ta=True, name=f"{tag}_mla_uq_dw")
    dcn = _mm(dkvh, W["mla_w_ukv_p"][l], tb=True, name=f"{tag}_mla_ukv_dx")
    G["mla_w_ukv_p"] = _mm(sv["cn"], dkvh, ta=True, name=f"{tag}_mla_ukv_dw")
    (dcq, dckv), (dgq, dgkv) = _rowwise_bwd(_mla_norm_fn, sv["norm_rows"], [sv["gq"], sv["gkv"]], [dqn, dcn],
                                            name=f"{tag}_mla_norm_bwd", tile=272)
    G["mla_q_norm_g"], G["mla_kv_norm_g"] = dgq[0], dgkv[0]
    dproj = jnp.concatenate([dz, dxbc, dfq, dfk, dfv, dcq, dckv, ddt_raw, df_raw, dk_rope], axis=1)
    dh1 = _mm(dproj, W["w_in_p"][l], tb=True, add=dh1_a, name=f"{tag}_in_proj_dx")
    G["w_in_p"] = _mm(sv["h1"], dproj, ta=True, name=f"{tag}_in_proj_dw")
    return dh1, G


def _local_step(x, target, W):
    h = jnp.concatenate([jnp.zeros((PAD_ROWS, D_MODEL), F32), W["meta"], x], axis=0)
    tgt = jnp.concatenate([jnp.zeros((BLOCK, D_MODEL), F32), target], axis=0)
    cosf, sins = _rope_tables()
    ln = lambda n, l: W[n][l][None]
    saved = []
    for l in range(DEPTH):
        h1, s1 = _ffn_fwd(h, W, "ffn1", l, ln("ln1_g", l), ln("ln1_b", l), f"l{l}_ffn1")
        h2, sm = _mixer_fwd(h1, W, l, cosf, sins)
        h, s2 = _ffn_fwd(h2, W, "ffn2", l, ln("ln3_g", l), ln("ln3_b", l), f"l{l}_ffn2")
        saved.append((s1, sm, s2))
    dh, loss = _loss_head(h, tgt, name="loss_head")
    grads = [None] * DEPTH
    GB = {}
    for l in reversed(range(DEPTH)):
        s1, sm, s2 = saved[l]
        G = {}
        dh, dg, db = _ffn_bwd(dh, s2, W, "ffn2", l, ln("ln3_g", l), ln("ln3_b", l), GB, f"l{l}_ffn2")
        G["ln3_g"], G["ln3_b"] = dg[0], db[0]
        dh, Gm = _mixer_bwd(dh, sm, W, l, cosf, sins, GB)
        G.update(Gm)
        dh, dg, db = _ffn_bwd(dh, s1, W, "ffn1", l, ln("ln1_g", l), ln("ln1_b", l), GB, f"l{l}_ffn1")
        G["ln1_g"], G["ln1_b"] = dg[0], db[0]
        grads[l] = G
    return loss, dh, grads, GB


WEIGHTS = ['meta', 'ffn1_w_gate', 'ffn1_w_up', 'ffn1_w_down', 'ln1_g', 'ln1_b', 'w_in', 'conv_w', 'conv_b', 'dt_bias',
           'a_log', 'd_skip', 'ssd_norm_g', 'fox_f_b', 'mla_q_norm_g', 'mla_w_uq', 'mla_kv_norm_g', 'mla_w_ukv',
           'w_out', 'ln2_g', 'ln2_b', 'ffn2_w_gate', 'ffn2_w_up', 'ffn2_w_down', 'ln3_g', 'ln3_b']
SMALL = ["ln1_g", "ln1_b", "conv_b", "dt_bias", "a_log", "d_skip", "ssd_norm_g", "fox_f_b", "mla_q_norm_g",
         "mla_kv_norm_g", "ln2_g", "ln2_b", "ln3_g", "ln3_b"]
MATMUL_W = ["ffn1_w_gate", "ffn1_w_up", "ffn1_w_down", "w_in", "mla_w_uq", "mla_w_ukv", "w_out", "ffn2_w_gate",
            "ffn2_w_up", "ffn2_w_down"]
SMALL_ROWS = 312


def _pad_to(a, axis, size):
    pads = [(0, 0)] * a.ndim
    pads[axis] = (0, size - a.shape[axis])
    return jnp.pad(a, pads)


def _chip_cols(full, chip, width):
    return lax.dynamic_slice_in_dim(full, chip * width, width, axis=full.ndim - 1)


def kernel(x, meta, ffn1_w_gate, ffn1_w_up, ffn1_w_down, ln1_g, ln1_b, w_in, conv_w, conv_b, dt_bias, a_log, d_skip, ssd_norm_g, fox_f_b, mla_q_norm_g, mla_w_uq, mla_kv_norm_g, mla_w_ukv, w_out, ln2_g, ln2_b, ffn2_w_gate, ffn2_w_up, ffn2_w_down, ln3_g, ln3_b, loss_target, m_meta, m_ffn1_w_gate, m_ffn1_w_up, m_ffn1_w_down, m_ln1_g, m_ln1_b, m_w_in, m_conv_w, m_conv_b, m_dt_bias, m_a_log, m_d_skip, m_ssd_norm_g, m_fox_f_b, m_mla_q_norm_g, m_mla_w_uq, m_mla_kv_norm_g, m_mla_w_ukv, m_w_out, m_ln2_g, m_ln2_b, m_ffn2_w_gate, m_ffn2_w_up, m_ffn2_w_down, m_ln3_g, m_ln3_b, v_meta, v_ffn1_w_gate, v_ffn1_w_up, v_ffn1_w_down, v_ln1_g, v_ln1_b, v_w_in, v_conv_w, v_conv_b, v_dt_bias, v_a_log, v_d_skip, v_ssd_norm_g, v_fox_f_b, v_mla_q_norm_g, v_mla_w_uq, v_mla_kv_norm_g, v_mla_w_ukv, v_w_out, v_ln2_g, v_ln2_b, v_ffn2_w_gate, v_ffn2_w_up, v_ffn2_w_down, v_ln3_g, v_ln3_b):
    args = dict(locals())
    w = {n: args[n] for n in WEIGHTS}
    m = {n: args["m_" + n] for n in WEIGHTS}
    v = {n: args["v_" + n] for n in WEIGHTS}
    xcoord, ycoord, _ = _my_pos()
    chip = 2 * xcoord + ycoord

    send = {}
    for pre in ("ffn1", "ffn2"):
        send[pre + "_w_gate"] = _pad_to(w[pre + "_w_gate"], 2, HP).astype(BF16)
        send[pre + "_w_up"] = _pad_to(w[pre + "_w_up"], 2, HP).astype(BF16)
        send[pre + "_w_down"] = _pad_to(w[pre + "_w_down"], 1, HP).astype(BF16)
    send["w_in"] = _pad_to(w["w_in"], 2, IN_SHARD_P).astype(BF16)
    for n in ("mla_w_uq", "mla_w_ukv", "w_out"):
        send[n] = w[n].astype(BF16)
    send["meta"] = w["meta"].reshape(2, N_META // 2, D_MODEL // N_CHIPS)
    send["conv_w"] = w["conv_w"]
    order = ["ffn1_w_gate", "ffn1_w_up", "ffn1_w_down", "meta", "w_in", "conv_w", "mla_w_uq", "mla_w_ukv", "w_out",
             "ffn2_w_gate", "ffn2_w_up", "ffn2_w_down"]
    got = dict(zip(order, _allgather_chips([send[n] for n in order])))

    W = {n: got[n] for n in got if n.startswith("ffn") or n == "w_out"}
    cat = lambda n, cut=None: jnp.concatenate([got[n][k][..., :cut] for k in range(N_CHIPS)], axis=-1)
    w_in_full = cat("w_in", IN_SHARD)
    W["w_in_p"] = [_pad_in_proj(w_in_full[l]) for l in range(DEPTH)]
    W["mla_w_uq_p"] = [_regroup_uq(cat("mla_w_uq")[l]) for l in range(DEPTH)]
    W["mla_w_ukv_p"] = [_regroup_ukv(cat("mla_w_ukv")[l]) for l in range(DEPTH)]
    W["conv_w"] = cat("conv_w")
    W["meta"] = jnp.concatenate([got["meta"][k].reshape(N_META, D_MODEL // N_CHIPS) for k in range(N_CHIPS)], axis=1)
    for n in SMALL:
        W[n] = w[n]

    loss, dh0, G, GB = _local_step(x[0], loss_target[0], W)

    def chunked(name, ungroup, width, pad):
        per_layer = []
        for l in range(DEPTH):
            full = ungroup(G[l][name])
            sh = jnp.moveaxis(full.reshape(full.shape[0], N_CHIPS, width), 1, 0)
            per_layer.append(_pad_to(sh, 2, pad))
        return jnp.stack(per_layer, axis=1)

    GB["w_in"] = chunked("w_in_p", _unpad_in_proj, IN_SHARD, IN_SHARD_P)
    GB["mla_w_uq"] = chunked("mla_w_uq_p", _ungroup_uq, MLA_NOPE + MLA_ROPE, MLA_NOPE + MLA_ROPE)
    GB["mla_w_ukv"] = chunked("mla_w_ukv_p", _ungroup_ukv, MLA_NOPE + MLA_V, MLA_NOPE + MLA_V)
    reduced = dict(zip(MATMUL_W, _reduce_scatter([GB[n] for n in MATMUL_W], MATMUL_W)))

    small_parts = [jnp.stack([G[l][n] for l in range(DEPTH)]).reshape(-1) for n in SMALL]
    small_parts += [jnp.stack([G[l]["conv_w"] for l in range(DEPTH)]).reshape(-1), dh0[PAD_ROWS:BLOCK].reshape(-1),
                    loss[0, :1]]
    flat = jnp.concatenate(small_parts)
    flat = jnp.pad(flat, (0, SMALL_ROWS * BLOCK - flat.shape[0]))
    red = _allreduce_small(flat.reshape(SMALL_ROWS, BLOCK)).reshape(-1)
    grads, off = {}, 0
    for n in SMALL:
        size = int(np.prod(w[n].shape))
        grads[n] = red[off:off + size].reshape(w[n].shape)
        off += size
    conv_full = red[off:off + DEPTH * SSD_CONV * 768].reshape(DEPTH, SSD_CONV, 768)
    off += DEPTH * SSD_CONV * 768
    meta_full = red[off:off + N_META * D_MODEL].reshape(N_META, D_MODEL)
    off += N_META * D_MODEL
    loss_out = red[off]
    grads["conv_w"] = _chip_cols(conv_full, chip, 768 // N_CHIPS)
    grads["meta"] = _chip_cols(meta_full, chip, D_MODEL // N_CHIPS)

    delta, new_m, new_v = {}, {}, {}
    for n in MATMUL_W:
        grads[n], delta[n], new_m[n], new_v[n] = _adamw(w[n], reduced[n], m[n], v[n], name=f"adamw_{n}")
    rest = [n for n in WEIGHTS if n not in MATMUL_W]

    def pack_small(d):
        f = jnp.concatenate([d[n].reshape(-1) for n in rest])
        tot = -(-f.shape[0] // (8 * BLOCK)) * 8 * BLOCK
        return jnp.pad(f, (0, tot - f.shape[0])).reshape(-1, BLOCK)

    _, d2, m2, v2 = _adamw(pack_small(w), pack_small(grads), pack_small(m), pack_small(v), name="adamw_small")
    off = 0
    for n in rest:
        size = int(np.prod(w[n].shape))
        for dst, src in ((delta, d2), (new_m, m2), (new_v, v2)):
            dst[n] = src.reshape(-1)[off:off + size].reshape(w[n].shape)
        off += size

    grad_x = dh0[BLOCK:][None]
    return (loss_out, grad_x, *[grads[n] for n in WEIGHTS], *[delta[n] for n in WEIGHTS],
            *[new_m[n] for n in WEIGHTS], *[new_v[n] for n in WEIGHTS])
```

```python
import functools

import numpy as np
import jax
import jax.numpy as jnp
from jax import lax
from jax.experimental import pallas as pl
from jax.experimental.pallas import tpu as pltpu

F32 = jnp.float32
BF16 = jnp.bfloat16
MESH = pl.DeviceIdType.MESH

D_MODEL = 1024
SEQ = 2048
N_META = 16
BLOCK = 128
PAD_ROWS = 112
LP = PAD_ROWS + N_META + SEQ
N_CHUNK = LP // BLOCK
DEPTH = 2
D_FF = 2816
N_CHIPS = 4
FF_SHARD = D_FF // N_CHIPS
HP = 768
FP = N_CHIPS * HP
SSD_HEADS, SSD_HD, SSD_D, SSD_GROUPS, SSD_STATE, SSD_CONV = 8, 64, 512, 2, 64, 4
FOX_HEADS, FOX_HD, FOX_D = 4, 64, 256
MLA_HEADS, MLA_Q_LORA, MLA_KV_LORA, MLA_NOPE, MLA_ROPE, MLA_V, MLA_D = 4, 256, 128, 64, 32, 64, 256
ROPE_HALF = MLA_ROPE // 2
ROPE_THETA = 10000.0
N_IN = 2476
IN_SHARD = N_IN // N_CHIPS
IN_SHARD_P = 640
ALPHA = (2 * DEPTH) ** 0.25
EPS = 1e-5
ADAM_LR, ADAM_B1, ADAM_B2, ADAM_EPS, ADAM_WD, ADAM_STEP = 0.001, 0.9, 0.999, 1e-08, 0.01, 10
NEG = -1e30
TM = 544

VMEM_LIMIT_BYTES = 56 * 1024 * 1024

PC_Z, PC_XBC, PC_FQ, PC_FK, PC_FV, PC_CQ, PC_CKV, PC_DT, PC_FR, PC_KR, PC_END = (
    0, 512, 1280, 1536, 1792, 2048, 2304, 2432, 2560, 2688, 2816)
OC_Z, OC_XBC, OC_DT, OC_FQ, OC_FK, OC_FV, OC_FR, OC_CQ, OC_CKV, OC_KR = (
    0, 512, 1280, 1288, 1544, 1800, 2056, 2060, 2316, 2444)


def _cparams(sem=None):
    return pltpu.CompilerParams(dimension_semantics=sem, vmem_limit_bytes=VMEM_LIMIT_BYTES)


def _tile(n, cap, mult):
    best = None
    for t in range(mult, min(n, cap) + 1, mult):
        if n % t == 0:
            best = t
    return best if best is not None else n


def _bs(shape, fn):
    return pl.BlockSpec(shape, fn)


ANY = pl.BlockSpec(memory_space=pl.ANY)


def _dims(ca, cb):
    return (((ca,), (cb,)), ((), ()))


def _raw_bdot(a, b, ca, cb):
    return lax.dot_general(a.astype(BF16), b.astype(BF16), _dims(ca, cb), preferred_element_type=F32)


@functools.partial(jax.custom_vjp, nondiff_argnums=(2, 3))
def _bdot(a, b, ca, cb):
    return _raw_bdot(a, b, ca, cb)


def _bdot_fwd(a, b, ca, cb):
    return _raw_bdot(a, b, ca, cb), (a, b)


def _bdot_bwd(ca, cb, res, g):
    a, b = res
    if (ca, cb) == (1, 0):
        return _raw_bdot(g, b, 1, 1), _raw_bdot(a, g, 0, 0)
    if (ca, cb) == (1, 1):
        return _raw_bdot(g, b, 1, 0), _raw_bdot(g, a, 0, 0)
    if (ca, cb) == (0, 0):
        return _raw_bdot(b, g, 1, 1), _raw_bdot(a, g, 1, 0)
    raise NotImplementedError((ca, cb))


_bdot.defvjp(_bdot_fwd, _bdot_bwd)


def _mm_core(a, b, *, a_spec, b_spec, o_spec, grid, out_shape, ca, cb, name, add=None, into=None):
    nk = grid[2]
    has_add, has_into = add is not None, into is not None
    acc_shape = tuple(d for d in o_spec.block_shape if d is not None)

    def body(*refs):
        a_ref, b_ref = refs[0], refs[1]
        p = 2
        add_ref = refs[p] if has_add else None
        p += int(has_add) + int(has_into)
        o_ref, acc_ref = refs[p], refs[p + 1]
        k = pl.program_id(2)

        @pl.when(k == 0)
        def _():
            acc_ref[...] = jnp.zeros_like(acc_ref)

        acc_ref[...] += _raw_bdot(a_ref[...], b_ref[...], ca, cb)

        @pl.when(k == nk - 1)
        def _():
            r = acc_ref[...]
            if has_add:
                r = r + add_ref[...]
            o_ref[...] = r

    ins = [a, b] + ([add] if has_add else []) + ([into] if has_into else [])
    in_specs = [a_spec, b_spec] + ([o_spec] if has_add else []) + ([ANY] if has_into else [])
    return pl.pallas_call(
        body, name=name, grid=grid, in_specs=in_specs, out_specs=o_spec,
        out_shape=jax.ShapeDtypeStruct(out_shape, F32), scratch_shapes=[pltpu.VMEM(acc_shape, F32)],
        input_output_aliases=({len(ins) - 1: 0} if has_into else {}),
        compiler_params=_cparams(("parallel", "parallel", "arbitrary")),
    )(*ins)


MM_VMEM_BUDGET = 40 * 1024 * 1024


def _divisors(n, mult):
    return [t for t in range(mult, n + 1, mult) if n % t == 0] or [n]


def _pick_tiles(M, N, K, a_bytes, b_bytes, ta, has_add):
    best = None
    for tm in _divisors(M, 128 if ta else 16):
        for tn in _divisors(N, 128):
            vmem = 2 * tm * K * a_bytes + 2 * K * tn * b_bytes + (3 + 2 * int(has_add)) * tm * tn * 4
            if vmem <= MM_VMEM_BUDGET:
                key = ((M // tm) * (N // tn), -tn)
                if best is None or key < best[0]:
                    best = (key, tm, tn)
    assert best is not None, (M, N, K)
    return best[1], best[2], K


def _mm(a, b, *, ta=False, tb=False, add=None, name):
    if ta:
        K, M = a.shape
    else:
        M, K = a.shape
    if tb:
        N, Kb = b.shape
    else:
        Kb, N = b.shape
    assert K == Kb, (a.shape, b.shape, ta, tb)
    tm, tn, tk = _pick_tiles(M, N, K, a.dtype.itemsize, b.dtype.itemsize, ta, add is not None)
    a_spec = _bs((tk, tm), lambda i, j, k: (k, i)) if ta else _bs((tm, tk), lambda i, j, k: (i, k))
    b_spec = _bs((tn, tk), lambda i, j, k: (j, k)) if tb else _bs((tk, tn), lambda i, j, k: (k, j))
    return _mm_core(a, b, a_spec=a_spec, b_spec=b_spec, o_spec=_bs((tm, tn), lambda i, j, k: (i, j)),
                    grid=(M // tm, N // tn, K // tk), out_shape=(M, N), ca=0 if ta else 1, cb=1 if tb else 0,
                    name=name, add=add)


def _row_entry(r, ncol):
    if isinstance(r, tuple):
        return r
    return r, r.shape[1] // ncol, 0


def _rowwise(fn, rows, pars, out_cols, *, name, tile, ncol=1, out_dtypes=None):
    rows = [_row_entry(r, ncol) for r in rows]
    L = rows[0][0].shape[0]
    nr, npar = len(rows), len(pars)
    in_specs = [_bs((tile, w), lambda g, i, o=o: (i, o + g)) for _, w, o in rows]
    in_specs += [_bs((p.shape[0], p.shape[1] // ncol), lambda g, i: (0, g)) for p in pars]
    out_specs = [_bs((tile, c // ncol), lambda g, i: (i, g)) for c in out_cols]

    def body(*refs):
        ins, outs = refs[:nr + npar], refs[nr + npar:]
        row0 = pl.program_id(1) * tile
        res = fn(row0, *[r[...] for r in ins])
        for o, v in zip(outs, res):
            o[...] = v.astype(o.dtype)

    return pl.pallas_call(
        body, name=name, grid=(ncol, L // tile), in_specs=in_specs, out_specs=out_specs,
        out_shape=[jax.ShapeDtypeStruct((L, c), d) for c, d in zip(out_cols, out_dtypes or [F32] * len(out_cols))],
        compiler_params=_cparams(("parallel", "parallel")),
    )(*[r[0] for r in rows], *pars)


def _rowwise_bwd(fn, rows, pars, douts, *, name, tile, ncol=1, row_grad=None, grad_dtypes=None):
    rows = [_row_entry(r, ncol) for r in rows]
    L = rows[0][0].shape[0]
    nr, npar, nd = len(rows), len(pars), len(douts)
    row_grad = [True] * nr if row_grad is None else row_grad
    in_specs = [_bs((tile, w), lambda g, i, o=o: (i, o + g)) for _, w, o in rows]
    in_specs += [_bs((p.shape[0], p.shape[1] // ncol), lambda g, i: (0, g)) for p in pars]
    in_specs += [_bs((tile, d.shape[1] // ncol), lambda g, i: (i, g)) for d in douts]
    g_widths = [w * ncol for (_, w, _), f in zip(rows, row_grad) if f]
    out_specs = [_bs((tile, w // ncol), lambda g, i: (i, g)) for w in g_widths]
    out_specs += [_bs((p.shape[0], p.shape[1] // ncol), lambda g, i: (0, g)) for p in pars]
    out_shape = [jax.ShapeDtypeStruct((L, w), d) for w, d in zip(g_widths, grad_dtypes or [F32] * len(g_widths))]
    out_shape += [jax.ShapeDtypeStruct(p.shape, F32) for p in pars]

    def body(*refs):
        ins = refs[:nr + npar]
        dos = refs[nr + npar:nr + npar + nd]
        outs = refs[nr + npar + nd:]
        i = pl.program_id(1)
        row0 = i * tile
        _, vjp = jax.vjp(lambda *a: tuple(fn(row0, *a)), *[r[...] for r in ins])
        grads = vjp(tuple(d[...].astype(F32) for d in dos))
        o = 0
        for j in range(nr):
            if row_grad[j]:
                outs[o][...] = grads[j].astype(outs[o].dtype)
                o += 1
        for j in range(npar):
            g, ref = grads[nr + j], outs[o + j]

            @pl.when(i == 0)
            def _(g=g, ref=ref):
                ref[...] = g

            @pl.when(i > 0)
            def _(g=g, ref=ref):
                ref[...] += g

    res = pl.pallas_call(
        body, name=name, grid=(ncol, L // tile), in_specs=in_specs, out_specs=out_specs, out_shape=out_shape,
        compiler_params=_cparams(("parallel", "arbitrary")),
    )(*[r[0] for r in rows], *pars, *douts)
    return res[:len(g_widths)], res[len(g_widths):]


def _row_ids(row0, shape):
    return row0 + lax.broadcasted_iota(jnp.int32, shape, 0)


def _sigmoid(x):
    return 1.0 / (1.0 + jnp.exp(-x))


@jax.custom_vjp
def _softplus(x):
    return jnp.maximum(x, 0.0) + jnp.log(1.0 + jnp.exp(-jnp.abs(x)))


def _softplus_fwd(x):
    return _softplus(x), x


def _softplus_bwd(x, g):
    return (g * _sigmoid(x),)


_softplus.defvjp(_softplus_fwd, _softplus_bwd)


def _silu(x):
    return x * _sigmoid(x)


def _swiglu_fn(row0, g, u):
    return (_silu(g) * u,)


def _make_res_ln_fn(scale):
    def fn(row0, h, o, gam, bet):
        pre = ALPHA * h + scale * o
        mu = jnp.mean(pre, axis=-1, keepdims=True)
        xc = pre - mu
        var = jnp.mean(xc * xc, axis=-1, keepdims=True)
        return (xc * lax.rsqrt(var + EPS) * gam + bet,)
    return fn


def _ssd_pre_fn(row0, raw, bias):
    dt = _softplus(raw + bias)
    return (jnp.where(_row_ids(row0, raw.shape) >= PAD_ROWS, dt, 0.0),)


def _ssd_post_fn(row0, y, xs, z, dskip, normg):
    v = (y + dskip * xs) * _silu(z)
    v = v * lax.rsqrt(jnp.mean(v * v, axis=-1, keepdims=True) + EPS)
    return (v * normg,)


def _mla_norm_fn(row0, cq, ckv, gq, gkv):
    qn = cq * lax.rsqrt(jnp.mean(cq * cq, axis=-1, keepdims=True) + EPS) * gq
    cn = ckv * lax.rsqrt(jnp.mean(ckv * ckv, axis=-1, keepdims=True) + EPS) * gkv
    return qn, cn


def _rope_fn(row0, q, k, cosf, sins):
    return (q * cosf + pltpu.roll(q, 64, 1) * sins, k * cosf + pltpu.roll(k, 64, 1) * sins)


def _rope_t_fn(row0, gq, gk, cosf, sins):
    return (gq * cosf + pltpu.roll(gq * sins, 64, 1), gk * cosf + pltpu.roll(gk * sins, 64, 1))


def _conv_fwd(x, x_off, w, b, *, name):
    C = w.shape[1]

    def body(x_ref, w_ref, b_ref, o_ref):
        rows = lax.broadcasted_iota(jnp.int32, (LP, BLOCK), 0)
        xv = jnp.where(rows >= PAD_ROWS, x_ref[...], 0.0)
        acc = b_ref[...] + w_ref[3:4, :] * xv
        for k in range(SSD_CONV - 1):
            acc = acc + w_ref[k:k + 1, :] * pltpu.roll(xv, SSD_CONV - 1 - k, 0)
        o_ref[...] = _silu(acc)

    return pl.pallas_call(
        body, name=name, grid=(C // BLOCK,),
        in_specs=[_bs((LP, BLOCK), lambda j: (0, j + x_off)), _bs((SSD_CONV, BLOCK), lambda j: (0, j)),
                  _bs((1, BLOCK), lambda j: (0, j))],
        out_specs=_bs((LP, BLOCK), lambda j: (0, j)),
        out_shape=jax.ShapeDtypeStruct((LP, C), F32), compiler_params=_cparams(("parallel",)),
    )(x, w, b)


def _conv_bwd(x, x_off, w, b, dout, *, name):
    C = w.shape[1]

    def body(x_ref, w_ref, b_ref, do_ref, dx_ref, dw_ref, db_ref):
        rows = lax.broadcasted_iota(jnp.int32, (LP, BLOCK), 0)
        real = rows >= PAD_ROWS
        xv = jnp.where(real, x_ref[...], 0.0)
        shifted = [pltpu.roll(xv, SSD_CONV - 1 - k, 0) for k in range(SSD_CONV - 1)] + [xv]
        acc = b_ref[...]
        for k in range(SSD_CONV):
            acc = acc + w_ref[k:k + 1, :] * shifted[k]
        sig = _sigmoid(acc)
        dacc = jnp.where(real, do_ref[...] * (sig * (1.0 + acc * (1.0 - sig))), 0.0)
        db_ref[...] = jnp.sum(dacc, axis=0, keepdims=True)
        dx = w_ref[3:4, :] * dacc
        for k in range(SSD_CONV):
            dw_ref[k:k + 1, :] = jnp.sum(dacc * shifted[k], axis=0, keepdims=True)
            if k < SSD_CONV - 1:
                dx = dx + w_ref[k:k + 1, :] * pltpu.roll(dacc, LP - (SSD_CONV - 1 - k), 0)
        dx_ref[...] = jnp.where(real, dx, 0.0)

    return pl.pallas_call(
        body, name=name, grid=(C // BLOCK,),
        in_specs=[_bs((LP, BLOCK), lambda j: (0, j + x_off)), _bs((SSD_CONV, BLOCK), lambda j: (0, j)),
                  _bs((1, BLOCK), lambda j: (0, j)), _bs((LP, BLOCK), lambda j: (0, j))],
        out_specs=[_bs((LP, BLOCK), lambda j: (0, j)), _bs((SSD_CONV, BLOCK), lambda j: (0, j)),
                   _bs((1, BLOCK), lambda j: (0, j))],
        out_shape=[jax.ShapeDtypeStruct((LP, C), F32), jax.ShapeDtypeStruct((SSD_CONV, C), F32),
                   jax.ShapeDtypeStruct((1, C), F32)],
        compiler_params=_cparams(("parallel",)),
    )(x, w, b, dout)


_BDIMS = {"nn": (((2,), (1,)), ((0,), (0,))), "nt": (((2,), (2,)), ((0,), (0,))), "tn": (((1,), (1,)), ((0,), (0,)))}


def _raw_bdot3(a, b, mode):
    return lax.dot_general(a.astype(BF16), b.astype(BF16), _BDIMS[mode], preferred_element_type=F32)


@functools.partial(jax.custom_vjp, nondiff_argnums=(2,))
def _bdot3(a, b, mode):
    return _raw_bdot3(a, b, mode)


def _bdot3_fwd(a, b, mode):
    return _raw_bdot3(a, b, mode), (a, b)


def _bdot3_bwd(mode, res, g):
    a, b = res
    if mode == "nn":
        return _raw_bdot3(g, b, "nt"), _raw_bdot3(a, g, "tn")
    if mode == "nt":
        return _raw_bdot3(g, b, "nn"), _raw_bdot3(g, a, "tn")
    return _raw_bdot3(b, g, "nt"), _raw_bdot3(a, g, "nn")


_bdot3.defvjp(_bdot3_fwd, _bdot3_bwd)


def _ssd_chunk(x, bm, cm, dtc, dtr, alog, prev):
    rep = SSD_HEADS // SSD_GROUPS
    per_head = lambda t: jnp.broadcast_to(t[:, None], (SSD_GROUPS, rep) + t.shape[1:]).reshape((SSD_HEADS,) + t.shape[1:])
    bm, cm = per_head(bm), per_head(cm)
    lane = lax.broadcasted_iota(jnp.int32, alog.shape, 2)
    a_neg = -jnp.exp(jnp.sum(jnp.where(lane == 0, alog, 0.0), axis=2, keepdims=True))
    ac_in = dtc * a_neg
    ar_in = dtr * a_neg
    li = lax.broadcasted_iota(jnp.int32, (1, BLOCK, BLOCK), 1)
    si = lax.broadcasted_iota(jnp.int32, (1, BLOCK, BLOCK), 2)
    causal = li >= si
    acum_c = jnp.sum(jnp.where(causal, ar_in, 0.0), axis=2, keepdims=True)
    acum_r = jnp.sum(jnp.where(li <= si, ac_in, 0.0), axis=1, keepdims=True)
    total = jnp.sum(ar_in, axis=2, keepdims=True)
    seg = jnp.exp(jnp.where(causal, acum_c - acum_r, NEG))
    xdt = x * dtc
    cb = _bdot3(cm, bm, "nt")
    y = _bdot3(cb * seg, xdt, "nn") + _bdot3(cm, prev, "nt") * jnp.exp(acum_c)
    st = _bdot3(xdt, bm * jnp.exp(total - acum_c), "tn")
    return y, prev * jnp.exp(total) + st


def _ssd_specs(rev):
    ci = (lambda c: N_CHUNK - 1 - c) if rev else (lambda c: c)
    x_spec = _bs((SSD_HEADS, BLOCK, SSD_HD), lambda c: (0, ci(c), 0))
    g_spec = _bs((SSD_GROUPS, BLOCK, SSD_STATE), lambda c: (0, ci(c), 0))
    dtc_spec = _bs((SSD_HEADS, BLOCK, 1), lambda c: (0, ci(c), 0))
    dtr_spec = _bs((SSD_HEADS, 1, BLOCK), lambda c: (0, 0, ci(c)))
    al_spec = _bs((SSD_HEADS, 1, BLOCK), lambda c: (0, 0, 0))
    st_spec = _bs((None, SSD_HEADS, SSD_HD, SSD_STATE), lambda c: (ci(c), 0, 0, 0))
    return x_spec, g_spec, dtc_spec, dtr_spec, al_spec, st_spec


def _ssd_fwd(x, bm, cm, dtc, dtr, alog, *, name):
    x_spec, g_spec, dtc_spec, dtr_spec, al_spec, st_spec = _ssd_specs(False)

    def body(x_ref, b_ref, c_ref, dtc_ref, dtr_ref, al_ref, y_ref, prev_ref, state):
        @pl.when(pl.program_id(0) == 0)
        def _():
            state[...] = jnp.zeros_like(state)

        prev = state[...]
        prev_ref[...] = prev
        y, new = _ssd_chunk(x_ref[...], b_ref[...], c_ref[...], dtc_ref[...], dtr_ref[...], al_ref[...], prev)
        y_ref[...] = y
        state[...] = new

    return pl.pallas_call(
        body, name=name, grid=(N_CHUNK,),
        in_specs=[x_spec, g_spec, g_spec, dtc_spec, dtr_spec, al_spec], out_specs=[x_spec, st_spec],
        out_shape=[jax.ShapeDtypeStruct((SSD_HEADS, LP, SSD_HD), F32),
                   jax.ShapeDtypeStruct((N_CHUNK, SSD_HEADS, SSD_HD, SSD_STATE), F32)],
        scratch_shapes=[pltpu.VMEM((SSD_HEADS, SSD_HD, SSD_STATE), F32)],
        compiler_params=_cparams(("arbitrary",)),
    )(x, bm, cm, dtc, dtr, alog)


def _ssd_bwd(x, bm, cm, dtc, dtr, alog, prevs, dy, *, name):
    x_spec, g_spec, dtc_spec, dtr_spec, al_spec, st_spec = _ssd_specs(True)

    def body(x_ref, b_ref, c_ref, dtc_ref, dtr_ref, al_ref, prev_ref, dy_ref,
             dx_ref, db_ref, dc_ref, ddtc_ref, ddtr_ref, dal_ref, dstate):
        c = pl.program_id(0)

        @pl.when(c == 0)
        def _():
            dstate[...] = jnp.zeros_like(dstate)

        _, vjp = jax.vjp(_ssd_chunk, x_ref[...], b_ref[...], c_ref[...], dtc_ref[...], dtr_ref[...], al_ref[...],
                         prev_ref[...])
        dx, db, dc, ddtc, ddtr, dal, dprev = vjp((dy_ref[...], dstate[...]))
        dx_ref[...] = dx
        db_ref[...] = db
        dc_ref[...] = dc
        ddtc_ref[...] = ddtc
        ddtr_ref[...] = ddtr
        dstate[...] = dprev

        @pl.when(c == 0)
        def _():
            dal_ref[...] = dal

        @pl.when(c > 0)
        def _():
            dal_ref[...] += dal

    hs = jax.ShapeDtypeStruct((SSD_HEADS, LP, SSD_HD), F32)
    gs = jax.ShapeDtypeStruct((SSD_GROUPS, LP, SSD_STATE), F32)
    return pl.pallas_call(
        body, name=name, grid=(N_CHUNK,),
        in_specs=[x_spec, g_spec, g_spec, dtc_spec, dtr_spec, al_spec, st_spec, x_spec],
        out_specs=[x_spec, g_spec, g_spec, dtc_spec, dtr_spec, al_spec],
        out_shape=[hs, gs, gs, jax.ShapeDtypeStruct((SSD_HEADS, LP, 1), F32),
                   jax.ShapeDtypeStruct((SSD_HEADS, 1, LP), F32), jax.ShapeDtypeStruct((SSD_HEADS, 1, BLOCK), F32)],
        scratch_shapes=[pltpu.VMEM((SSD_HEADS, SSD_HD, SSD_STATE), F32)],
        compiler_params=_cparams(("arbitrary",)),
    )(x, bm, cm, dtc, dtr, alog, prevs, dy)


def _tri_dot(tri, v):
    hi = v.astype(BF16)
    r1 = v - hi.astype(F32)
    mid = r1.astype(BF16)
    lo = (r1 - mid.astype(F32)).astype(BF16)
    t = tri.astype(BF16)
    d = lambda p: lax.dot_general(t, p, _dims(1, 0), preferred_element_type=F32)
    return d(hi) + d(mid) + d(lo)


def _fox_gate_fwd(raw, raw_blk, bias, *, name):
    def body(raw_ref, b_ref, c_ref, ct_ref, carry):
        j = pl.program_id(0)

        @pl.when(j == 0)
        def _():
            carry[...] = jnp.zeros_like(carry)

        rows = j * BLOCK + lax.broadcasted_iota(jnp.int32, (BLOCK, BLOCK), 0)
        lf = jnp.where(rows >= PAD_ROWS, -_softplus(-(raw_ref[...] + b_ref[...])), 0.0)
        li = lax.broadcasted_iota(jnp.int32, (BLOCK, BLOCK), 0)
        si = lax.broadcasted_iota(jnp.int32, (BLOCK, BLOCK), 1)
        cv = _tri_dot(jnp.where(li >= si, 1.0, 0.0), lf) + carry[...]
        c_ref[...] = cv
        ct_ref[...] = cv.T
        carry[...] += jnp.sum(lf, axis=0, keepdims=True)

    return pl.pallas_call(
        body, name=name, grid=(N_CHUNK,),
        in_specs=[_bs((BLOCK, BLOCK), lambda j: (j, raw_blk)), _bs((1, BLOCK), lambda j: (0, 0))],
        out_specs=[_bs((BLOCK, BLOCK), lambda j: (j, 0)), _bs((BLOCK, BLOCK), lambda j: (0, j))],
        out_shape=[jax.ShapeDtypeStruct((LP, BLOCK), F32), jax.ShapeDtypeStruct((BLOCK, LP), F32)],
        scratch_shapes=[pltpu.VMEM((1, BLOCK), F32)], compiler_params=_cparams(("arbitrary",)),
    )(raw, bias)


def _fox_gate_bwd(raw, raw_blk, bias, dc, dct, *, name):
    rj = lambda j: N_CHUNK - 1 - j

    def body(raw_ref, b_ref, dc_ref, dct_ref, draw_ref, db_ref, carry):
        j = pl.program_id(0)

        @pl.when(j == 0)
        def _():
            carry[...] = jnp.zeros_like(carry)

        rows = (N_CHUNK - 1 - j) * BLOCK + lax.broadcasted_iota(jnp.int32, (BLOCK, BLOCK), 0)
        li = lax.broadcasted_iota(jnp.int32, (BLOCK, BLOCK), 0)
        si = lax.broadcasted_iota(jnp.int32, (BLOCK, BLOCK), 1)
        dcv = dc_ref[...] + dct_ref[...].T
        dlf = _tri_dot(jnp.where(li <= si, 1.0, 0.0), dcv) + carry[...]
        carry[...] += jnp.sum(dcv, axis=0, keepdims=True)
        draw = jnp.where(rows >= PAD_ROWS, dlf * (1.0 - _sigmoid(raw_ref[...] + b_ref[...])), 0.0)
        draw_ref[...] = draw
        dsum = jnp.sum(draw, axis=0, keepdims=True)

        @pl.when(j == 0)
        def _():
            db_ref[...] = dsum

        @pl.when(j > 0)
        def _():
            db_ref[...] += dsum

    return pl.pallas_call(
        body, name=name, grid=(N_CHUNK,),
        in_specs=[_bs((BLOCK, BLOCK), lambda j: (rj(j), raw_blk)), _bs((1, BLOCK), lambda j: (0, 0)),
                  _bs((BLOCK, BLOCK), lambda j: (rj(j), 0)), _bs((BLOCK, BLOCK), lambda j: (0, rj(j)))],
        out_specs=[_bs((BLOCK, BLOCK), lambda j: (rj(j), 0)), _bs((1, BLOCK), lambda j: (0, 0))],
        out_shape=[jax.ShapeDtypeStruct((LP, BLOCK), F32), jax.ShapeDtypeStruct((1, BLOCK), F32)],
        scratch_shapes=[pltpu.VMEM((1, BLOCK), F32)], compiler_params=_cparams(("arbitrary",)),
    )(raw, bias, dc, dct)


ATT_W = 256


def _lane_head(width, per, mod=None):
    lane = lax.broadcasted_iota(jnp.int32, (1, width), 1)
    if mod is not None:
        lane = lane % mod
    return lane // per


def _attn_mask(i):
    r = i * BLOCK + lax.broadcasted_iota(jnp.int32, (BLOCK, LP), 0)
    c = lax.broadcasted_iota(jnp.int32, (BLOCK, LP), 1)
    return (c <= r) & ((c >= PAD_ROWS) | (r < PAD_ROWS))


def _attn_specs(q, k, v, bias, rope):
    qspec = lambda blk, w=ATT_W: _bs((BLOCK, w), lambda i: (i, blk))
    fspec = lambda blk, w=ATT_W: _bs((LP, w), lambda i: (0, blk))
    ins = [q[0], k[0], v[0]]
    specs = [qspec(q[1]), fspec(k[1]), fspec(v[1])]
    if bias is not None:
        ins += [bias[0], bias[1]]
        specs += [qspec(0, BLOCK), _bs((BLOCK, LP), lambda i: (0, 0))]
    if rope is not None:
        ins += [rope[0][0], rope[1][0]]
        specs += [qspec(rope[0][1], BLOCK), fspec(rope[1][1], BLOCK)]
    return ins, specs, qspec, fspec


def _attn_fwd(q, k, v, *, scale, name, bias=None, rope=None):
    ins, specs, qspec, fspec = _attn_specs(q, k, v, bias, rope)
    has_bias, has_rope = bias is not None, rope is not None

    def body(*refs):
        it = iter(refs)
        q_ref, k_ref, v_ref = next(it), next(it), next(it)
        if has_bias:
            c_ref, ct_ref = next(it), next(it)
        if has_rope:
            qr_ref, kr_ref = next(it), next(it)
        o_ref, lse_ref = next(it), next(it)
        ok = _attn_mask(pl.program_id(0))
        qv, kv, vv = q_ref[...], k_ref[...], v_ref[...]
        hid, l128 = _lane_head(ATT_W, FOX_HD), _lane_head(BLOCK, 1)
        if has_rope:
            rid = _lane_head(BLOCK, ROPE_HALF, 64)
            qrv, krv = qr_ref[...], kr_ref[...]
        o_acc = jnp.zeros((BLOCK, ATT_W), F32)
        lse_acc = jnp.zeros((BLOCK, BLOCK), F32)
        for h in range(FOX_HEADS):
            s = _raw_bdot(jnp.where(hid == h, qv, 0.0), kv, 1, 1)
            if has_rope:
                s = s + _raw_bdot(jnp.where(rid == h, qrv, 0.0), krv, 1, 1)
            s = s * scale
            if has_bias:
                cq = jnp.sum(jnp.where(l128 == h, c_ref[...], 0.0), axis=1, keepdims=True)
                s = s + (cq - ct_ref[h:h + 1, :])
            s = jnp.where(ok, s, NEG)
            m = jnp.max(s, axis=1, keepdims=True)
            p = jnp.exp(s - m)
            l = jnp.sum(p, axis=1, keepdims=True)
            o_acc = jnp.where(hid == h, _raw_bdot(p, vv, 1, 0) / l, o_acc)
            lse_acc = jnp.where(l128 == h, m + jnp.log(l), lse_acc)
        o_ref[...] = o_acc
        lse_ref[...] = lse_acc

    return pl.pallas_call(
        body, name=name, grid=(N_CHUNK,), in_specs=specs, out_specs=[qspec(0), qspec(0, BLOCK)],
        out_shape=[jax.ShapeDtypeStruct((LP, ATT_W), F32), jax.ShapeDtypeStruct((LP, BLOCK), F32)],
        compiler_params=_cparams(("parallel",)),
    )(*ins)


def _attn_bwd(q, k, v, o, lse, do, *, scale, name, bias=None, rope=None):
    ins, specs, qspec, fspec = _attn_specs(q, k, v, bias, rope)
    has_bias, has_rope = bias is not None, rope is not None
    ins += [o, lse, do[0]]
    specs += [qspec(0), qspec(0, BLOCK), qspec(do[1])]

    def body(*refs):
        it = iter(refs)
        q_ref, k_ref, v_ref = next(it), next(it), next(it)
        if has_bias:
            c_ref, ct_ref = next(it), next(it)
        if has_rope:
            qr_ref, kr_ref = next(it), next(it)
        o_ref, lse_ref, do_ref = next(it), next(it), next(it)
        dq_ref, dk_ref, dv_ref = next(it), next(it), next(it)
        if has_bias:
            dc_ref, dct_ref = next(it), next(it)
        if has_rope:
            dqr_ref, dkr_ref = next(it), next(it)
        i = pl.program_id(0)
        ok = _attn_mask(i)
        qv, kv, vv, ov, dov, lsev = q_ref[...], k_ref[...], v_ref[...], o_ref[...], do_ref[...], lse_ref[...]
        hid, l128 = _lane_head(ATT_W, FOX_HD), _lane_head(BLOCK, 1)
        if has_rope:
            rid = _lane_head(BLOCK, ROPE_HALF, 64)
            qrv, krv = qr_ref[...], kr_ref[...]
            dqr_acc = jnp.zeros((BLOCK, BLOCK), F32)
            dkr_acc = jnp.zeros((LP, BLOCK), F32)
        dq_acc = jnp.zeros((BLOCK, ATT_W), F32)
        dk_acc = jnp.zeros((LP, ATT_W), F32)
        dv_acc = jnp.zeros((LP, ATT_W), F32)
        dc_acc = jnp.zeros((BLOCK, BLOCK), F32)
        dct_rows = []
        for h in range(FOX_HEADS):
            qm = jnp.where(hid == h, qv, 0.0)
            s = _raw_bdot(qm, kv, 1, 1)
            if has_rope:
                qrm = jnp.where(rid == h, qrv, 0.0)
                s = s + _raw_bdot(qrm, krv, 1, 1)
            s = s * scale
            if has_bias:
                cq = jnp.sum(jnp.where(l128 == h, c_ref[...], 0.0), axis=1, keepdims=True)
                s = s + (cq - ct_ref[h:h + 1, :])
            s = jnp.where(ok, s, NEG)
            p = jnp.exp(s - jnp.sum(jnp.where(l128 == h, lsev, 0.0), axis=1, keepdims=True))
            dom = jnp.where(hid == h, dov, 0.0)
            dp = _raw_bdot(dom, vv, 1, 1)
            delta = jnp.sum(dom * ov, axis=1, keepdims=True)
            ds = p * (dp - delta)
            dq_acc = jnp.where(hid == h, _raw_bdot(ds, kv, 1, 0) * scale, dq_acc)
            dk_acc = dk_acc + _raw_bdot(ds, qm, 0, 0) * scale
            dv_acc = dv_acc + _raw_bdot(p, dom, 0, 0)
            if has_rope:
                dqr_acc = jnp.where(rid == h, _raw_bdot(ds, krv, 1, 0) * scale, dqr_acc)
                dkr_acc = dkr_acc + _raw_bdot(ds, qrm, 0, 0) * scale
            if has_bias:
                dc_acc = jnp.where(l128 == h, jnp.sum(ds, axis=1, keepdims=True), dc_acc)
                dct_rows.append(-jnp.sum(ds, axis=0, keepdims=True))
        dq_ref[...] = dq_acc
        if has_bias:
            dc_ref[...] = dc_acc
        if has_rope:
            dqr_ref[...] = dqr_acc

        @pl.when(i == 0)
        def _():
            dk_ref[...] = dk_acc
            dv_ref[...] = dv_acc
            if has_rope:
                dkr_ref[...] = dkr_acc
            if has_bias:
                dct_ref[...] = jnp.zeros_like(dct_ref)
                for h in range(FOX_HEADS):
                    dct_ref[h:h + 1, :] = dct_rows[h]

        @pl.when(i > 0)
        def _():
            dk_ref[...] += dk_acc
            dv_ref[...] += dv_acc
            if has_rope:
                dkr_ref[...] += dkr_acc
            if has_bias:
                for h in range(FOX_HEADS):
                    dct_ref[h:h + 1, :] += dct_rows[h]

    wide = jax.ShapeDtypeStruct((LP, ATT_W), F32)
    narrow = jax.ShapeDtypeStruct((LP, BLOCK), F32)
    out_specs = [qspec(0), fspec(0), fspec(0)]
    out_shape = [wide, wide, wide]
    if has_bias:
        out_specs += [qspec(0, BLOCK), _bs((BLOCK, LP), lambda i: (0, 0))]
        out_shape += [narrow, jax.ShapeDtypeStruct((BLOCK, LP), F32)]
    if has_rope:
        out_specs += [qspec(0, BLOCK), fspec(0, BLOCK)]
        out_shape += [narrow, narrow]
    return pl.pallas_call(
        body, name=name, grid=(N_CHUNK,), in_specs=specs, out_specs=out_specs, out_shape=out_shape,
        compiler_params=_cparams(("arbitrary",)),
    )(*ins)


def _loss_head(y, target, *, name):
    tile = 272

    def body(y_ref, t_ref, dy_ref, loss_ref):
        i = pl.program_id(0)
        rows = i * tile + lax.broadcasted_iota(jnp.int32, (tile, D_MODEL), 0)
        err = jnp.where(rows >= BLOCK, y_ref[...] - t_ref[...], 0.0)
        dy_ref[...] = err * (1.0 / D_MODEL)
        part = 0.5 * jnp.sum(jnp.sum(err * err, axis=1, keepdims=True) * (1.0 / D_MODEL), axis=0, keepdims=True)
        part = jnp.broadcast_to(part, (1, BLOCK))

        @pl.when(i == 0)
        def _():
            loss_ref[...] = part

        @pl.when(i > 0)
        def _():
            loss_ref[...] += part

    return pl.pallas_call(
        body, name=name, grid=(LP // tile,),
        in_specs=[_bs((tile, D_MODEL), lambda i: (i, 0)), _bs((tile, D_MODEL), lambda i: (i, 0))],
        out_specs=[_bs((tile, D_MODEL), lambda i: (i, 0)), _bs((1, BLOCK), lambda i: (0, 0))],
        out_shape=[jax.ShapeDtypeStruct((LP, D_MODEL), F32), jax.ShapeDtypeStruct((1, BLOCK), F32)],
        compiler_params=_cparams(("arbitrary",)),
    )(y, target)


def _adamw(w, g, m, v, *, name):
    if w.ndim == 2:
        w, g, m, v = w[None], g[None], m[None], v[None]
        squeeze = True
    else:
        squeeze = False
    NL, R, C = w.shape
    CG = g.shape[2]
    tile = _tile(R, 256, 8)

    def body(w_ref, g_ref, m_ref, v_ref, go_ref, d_ref, nm_ref, nv_ref):
        gv = g_ref[:, :C]
        nm = ADAM_B1 * m_ref[...] + (1.0 - ADAM_B1) * gv
        nv = ADAM_B2 * v_ref[...] + (1.0 - ADAM_B2) * (gv * gv)
        m_hat = nm / (1.0 - ADAM_B1 ** ADAM_STEP)
        v_hat = nv / (1.0 - ADAM_B2 ** ADAM_STEP)
        go_ref[...] = gv
        d_ref[...] = -ADAM_LR * (m_hat / (jnp.sqrt(v_hat) + ADAM_EPS) + ADAM_WD * w_ref[...])
        nm_ref[...] = nm
        nv_ref[...] = nv

    spec = _bs((None, tile, C), lambda l, i: (l, i, 0))
    gspec = _bs((None, tile, CG), lambda l, i: (l, i, 0))
    res = pl.pallas_call(
        body, name=name, grid=(NL, R // tile), in_specs=[spec, gspec, spec, spec], out_specs=[spec] * 4,
        out_shape=[jax.ShapeDtypeStruct((NL, R, C), F32)] * 4, compiler_params=_cparams(("parallel", "parallel")),
    )(w, g, m, v)
    return [r[0] for r in res] if squeeze else res


def _my_pos():
    return lax.axis_index("x"), lax.axis_index("y"), lax.axis_index("c")


def _other_chips(x, y):
    return [(1 - x, y), (x, 1 - y), (1 - x, 1 - y)]


def _allgather_chips(shards):
    n = len(shards)
    per = 7

    def body(*refs):
        ins, outs = refs[:n], refs[n:2 * n]
        send_sems, recv_sems = refs[2 * n], refs[2 * n + 1]
        x, y, c = _my_pos()
        chips = _other_chips(x, y)
        sibling, me = (x, y, 1 - c), 2 * x + y

        def cp(a, kk, src, dst, to):
            return pltpu.make_async_remote_copy(src_ref=src, dst_ref=dst, send_sem=send_sems.at[per * a + kk],
                                                recv_sem=recv_sems.at[per * a + kk], device_id=to, device_id_type=MESH)

        sends = []
        for a in range(n):
            for j, chip in enumerate(chips):
                sends.append(cp(a, j, ins[a].at[c], outs[a].at[me, c], (*chip, c)))
            sends.append(cp(a, 3, ins[a], outs[a].at[me], sibling))
        for s in sends:
            s.start()
        for a in range(n):
            for j, chip in enumerate(chips):
                slab = outs[a].at[2 * chip[0] + chip[1], c]
                cp(a, j, slab, slab, (x, y, c)).wait_recv()
                fwd = cp(a, 4 + j, slab, slab, sibling)
                fwd.start()
                sends.append(fwd)
        for a in range(n):
            cp(a, 3, ins[a], outs[a].at[me], (x, y, c)).wait_recv()
            for j, chip in enumerate(chips):
                slab = outs[a].at[2 * chip[0] + chip[1], 1 - c]
                cp(a, 4 + j, slab, slab, (x, y, c)).wait_recv()
        for s in sends:
            s.wait_send()

    return pl.pallas_call(
        body, name="allgather_chips", in_specs=[ANY] * n, out_specs=[ANY] * n,
        out_shape=[jax.ShapeDtypeStruct((N_CHIPS,) + s.shape, s.dtype) for s in shards],
        scratch_shapes=[pltpu.SemaphoreType.DMA((per * n,)), pltpu.SemaphoreType.DMA((per * n,))],
    )(*shards)


def _rs_swap_layers(gs):
    n = len(gs)

    def body(*refs):
        ins, outs = refs[:n], refs[n:2 * n]
        send_sems, recv_sems = refs[2 * n], refs[2 * n + 1]
        x, y, c = _my_pos()
        cps = [pltpu.make_async_remote_copy(
            src_ref=ins[a].at[:, 1 - c], dst_ref=outs[a], send_sem=send_sems.at[a], recv_sem=recv_sems.at[a],
            device_id=(x, y, 1 - c), device_id_type=MESH) for a in range(n)]
        for cp in cps:
            cp.start()
        for cp in cps:
            cp.wait()

    return pl.pallas_call(
        body, name="rs_swap_layers", in_specs=[ANY] * n, out_specs=[ANY] * n,
        out_shape=[jax.ShapeDtypeStruct((N_CHIPS,) + g.shape[2:], g.dtype) for g in gs],
        scratch_shapes=[pltpu.SemaphoreType.DMA((n,)), pltpu.SemaphoreType.DMA((n,))],
    )(*gs)


def _rs_add_pair(g, r, pos, *, name):
    _, _, R, C = g.shape
    tile = _tile(R, 512, 16)

    def body(pos_ref, g_ref, r_ref, o32_ref, o16_ref):
        s = g_ref[...] + r_ref[...]
        o32_ref[...] = s
        o16_ref[...] = s.astype(BF16)

    spec = _bs((None, tile, C), lambda k, i, pos_ref: (k, i, 0))
    grid_spec = pltpu.PrefetchScalarGridSpec(
        num_scalar_prefetch=1, grid=(N_CHIPS, R // tile),
        in_specs=[_bs((None, None, tile, C), lambda k, i, pos_ref: (k, pos_ref[1], i, 0)), spec],
        out_specs=[spec, spec])
    return pl.pallas_call(
        body, name=name, grid_spec=grid_spec,
        out_shape=[jax.ShapeDtypeStruct((N_CHIPS, R, C), F32), jax.ShapeDtypeStruct((N_CHIPS, R, C), BF16)],
        compiler_params=_cparams(("parallel", "parallel")),
    )(pos, g, r)


def _rs_exchange_chips(ps):
    n = len(ps)

    def body(*refs):
        ins, outs = refs[:n], refs[n:2 * n]
        send_sems, recv_sems = refs[2 * n], refs[2 * n + 1]
        x, y, c = _my_pos()
        cps = []
        for a in range(n):
            for j, chip in enumerate(_other_chips(x, y)):
                cps.append(pltpu.make_async_remote_copy(
                    src_ref=ins[a].at[2 * chip[0] + chip[1]], dst_ref=outs[a].at[j], send_sem=send_sems.at[3 * a + j],
                    recv_sem=recv_sems.at[3 * a + j], device_id=(*chip, c), device_id_type=MESH))
        for cp in cps:
            cp.start()
        for cp in cps:
            cp.wait()

    return pl.pallas_call(
        body, name="rs_exchange_chips", in_specs=[ANY] * n, out_specs=[ANY] * n,
        out_shape=[jax.ShapeDtypeStruct((3,) + p.shape[1:], p.dtype) for p in ps],
        scratch_shapes=[pltpu.SemaphoreType.DMA((3 * n,)), pltpu.SemaphoreType.DMA((3 * n,))],
    )(*ps)


def _rs_add_chips(p32, r16, pos, *, name):
    _, R, C = p32.shape
    tile = _tile(R, 512, 16)

    def body(pos_ref, p_ref, r_ref, o_ref):
        o_ref[...] = ((p_ref[...] + r_ref[0].astype(F32)) + r_ref[1].astype(F32)) + r_ref[2].astype(F32)

    grid_spec = pltpu.PrefetchScalarGridSpec(
        num_scalar_prefetch=1, grid=(R // tile,),
        in_specs=[_bs((None, tile, C), lambda i, pos_ref: (pos_ref[0], i, 0)),
                  _bs((3, tile, C), lambda i, pos_ref: (0, i, 0))],
        out_specs=_bs((None, tile, C), lambda i, pos_ref: (pos_ref[1], i, 0)))
    return pl.pallas_call(
        body, name=name, grid_spec=grid_spec, out_shape=jax.ShapeDtypeStruct((2, R, C), F32),
        compiler_params=_cparams(("parallel",)),
    )(pos, p32, r16)


def _rs_join_layers(fs):
    n = len(fs)

    def body(*refs):
        outs = refs[n:2 * n]
        send_sems, recv_sems = refs[2 * n], refs[2 * n + 1]
        x, y, c = _my_pos()
        for a in range(n):
            pltpu.make_async_remote_copy(src_ref=outs[a].at[c], dst_ref=outs[a].at[c], send_sem=send_sems.at[a],
                                         recv_sem=recv_sems.at[a], device_id=(x, y, 1 - c), device_id_type=MESH).start()
        for a in range(n):
            pltpu.make_async_remote_copy(src_ref=outs[a].at[c], dst_ref=outs[a].at[1 - c], send_sem=send_sems.at[a],
                                         recv_sem=recv_sems.at[a], device_id=(x, y, 1 - c), device_id_type=MESH).wait()

    return pl.pallas_call(
        body, name="rs_join_layers", in_specs=[ANY] * n, out_specs=[ANY] * n,
        out_shape=[jax.ShapeDtypeStruct(f.shape, f.dtype) for f in fs],
        input_output_aliases={a: a for a in range(n)},
        scratch_shapes=[pltpu.SemaphoreType.DMA((n,)), pltpu.SemaphoreType.DMA((n,))],
    )(*fs)


def _reduce_scatter(gs, names):
    x, y, c = _my_pos()
    pos = jnp.stack([2 * x + y, c]).astype(jnp.int32)
    r1 = _rs_swap_layers(gs)
    pairs = [_rs_add_pair(g, r, pos, name=f"rs_add_pair_{nm}") for g, r, nm in zip(gs, r1, names)]
    r2 = _rs_exchange_chips([p[1] for p in pairs])
    fs = [_rs_add_chips(p[0], r, pos, name=f"rs_add_chips_{nm}") for p, r, nm in zip(pairs, r2, names)]
    return _rs_join_layers(fs)


def _allreduce_small(buf):
    R, W = buf.shape

    def body(b_ref, o_ref, gather, send_sems, recv_sems):
        x, y, c = _my_pos()
        me = 4 * x + 2 * y + c
        gather[me] = b_ref[...]
        cps = []
        for d in range(1, 8):
            peer = (x ^ (d >> 2), y ^ ((d >> 1) & 1), c ^ (d & 1))
            cps.append(pltpu.make_async_remote_copy(
                src_ref=b_ref, dst_ref=gather.at[me], send_sem=send_sems.at[d - 1], recv_sem=recv_sems.at[d - 1],
                device_id=peer, device_id_type=MESH))
        for cp in cps:
            cp.start()
        for d in range(1, 8):
            pltpu.make_async_remote_copy(
                src_ref=b_ref, dst_ref=gather.at[me ^ d], send_sem=send_sems.at[d - 1], recv_sem=recv_sems.at[d - 1],
                device_id=(x, y, c), device_id_type=MESH).wait_recv()
        for cp in cps:
            cp.wait_send()
        acc = gather[0]
        for d in range(1, 8):
            acc = acc + gather[d]
        o_ref[...] = acc

    vm = pl.BlockSpec(memory_space=pltpu.VMEM)
    return pl.pallas_call(
        body, name="allreduce_small", in_specs=[vm], out_specs=vm, out_shape=jax.ShapeDtypeStruct((R, W), F32),
        scratch_shapes=[pltpu.VMEM((8, R, W), F32), pltpu.SemaphoreType.DMA((7,)), pltpu.SemaphoreType.DMA((7,))],
    )(buf)


def _heads(a, h, d):
    return a.reshape(a.shape[0], h, d).transpose(1, 0, 2)


def _unheads(a):
    h, L, d = a.shape
    return a.transpose(1, 0, 2).reshape(L, h * d)


def _rope_tables():
    pos = jnp.maximum(jnp.arange(LP, dtype=F32) - PAD_ROWS, 0.0)
    inv_freq = 1.0 / (ROPE_THETA ** (jnp.arange(0, MLA_ROPE, 2, dtype=F32) / MLA_ROPE))
    ang = pos[:, None] * inv_freq[None, :]
    cos, sin = jnp.tile(jnp.cos(ang), (1, MLA_HEADS)), jnp.tile(jnp.sin(ang), (1, MLA_HEADS))
    return jnp.concatenate([cos, cos], axis=1), jnp.concatenate([-sin, sin], axis=1)


def _lane_pad(a, width=BLOCK):
    return jnp.pad(a, ((0, 0), (0, width - a.shape[1])))


def _pad_in_proj(w):
    sl = lambda start, size: w[:, start:start + size]
    return jnp.concatenate([
        sl(OC_Z, 512), sl(OC_XBC, 768), sl(OC_FQ, 256), sl(OC_FK, 256), sl(OC_FV, 256), sl(OC_CQ, 256), sl(OC_CKV, 128),
        _lane_pad(sl(OC_DT, SSD_HEADS)), _lane_pad(sl(OC_FR, FOX_HEADS)),
        jnp.tile(sl(OC_KR, ROPE_HALF), (1, MLA_HEADS)), jnp.tile(sl(OC_KR + ROPE_HALF, ROPE_HALF), (1, MLA_HEADS))], axis=1)


def _unpad_in_proj(wp):
    sl = lambda start, size: wp[:, start:start + size]
    rope = lambda start: sl(start, 64).reshape(wp.shape[0], MLA_HEADS, ROPE_HALF).sum(axis=1)
    return jnp.concatenate([
        sl(PC_Z, 512), sl(PC_XBC, 768), sl(PC_DT, SSD_HEADS), sl(PC_FQ, 256), sl(PC_FK, 256), sl(PC_FV, 256),
        sl(PC_FR, FOX_HEADS), sl(PC_CQ, 256), sl(PC_CKV, 128), rope(PC_KR), rope(PC_KR + 64)], axis=1)


def _regroup_uq(w):
    w3 = w.reshape(w.shape[0], MLA_HEADS, MLA_NOPE + MLA_ROPE)
    return jnp.concatenate([w3[:, :, :MLA_NOPE].reshape(w.shape[0], -1),
                            w3[:, :, MLA_NOPE:MLA_NOPE + ROPE_HALF].reshape(w.shape[0], -1),
                            w3[:, :, MLA_NOPE + ROPE_HALF:].reshape(w.shape[0], -1)], axis=1)


def _ungroup_uq(wp):
    n = wp.shape[0]
    return jnp.concatenate([wp[:, :256].reshape(n, MLA_HEADS, MLA_NOPE), wp[:, 256:320].reshape(n, MLA_HEADS, ROPE_HALF),
                            wp[:, 320:].reshape(n, MLA_HEADS, ROPE_HALF)], axis=2).reshape(n, -1)


def _regroup_ukv(w):
    w3 = w.reshape(w.shape[0], MLA_HEADS, MLA_NOPE + MLA_V)
    return jnp.concatenate([w3[:, :, :MLA_NOPE].reshape(w.shape[0], -1), w3[:, :, MLA_NOPE:].reshape(w.shape[0], -1)],
                           axis=1)


def _ungroup_ukv(wp):
    n = wp.shape[0]
    return jnp.concatenate([wp[:, :256].reshape(n, MLA_HEADS, MLA_NOPE), wp[:, 256:].reshape(n, MLA_HEADS, MLA_V)],
                           axis=2).reshape(n, -1)


TMF = 1088
N_IF = LP // TMF


def _chunk_cols_mm(a, w, l, chunk_w, *, name):
    K = a.shape[1]
    return _mm_core(a, w, a_spec=_bs((TMF, K), lambda i, j, k: (i, 0)),
                    b_spec=_bs((None, None, K, chunk_w), lambda i, j, k: (j, l, 0, 0)),
                    o_spec=_bs((TMF, chunk_w), lambda i, j, k: (i, j)), grid=(N_IF, N_CHIPS, 1),
                    out_shape=(LP, N_CHIPS * chunk_w), ca=1, cb=0, name=name)


def _chunk_cols_dx(g, w, l, chunk_w, add, *, name):
    K = w.shape[2]
    return _mm_core(g, w, a_spec=_bs((TMF, chunk_w), lambda i, j, k: (i, k)),
                    b_spec=_bs((None, None, K, chunk_w), lambda i, j, k: (k, l, 0, 0)),
                    o_spec=_bs((TMF, K), lambda i, j, k: (i, 0)), grid=(N_IF, 1, N_CHIPS),
                    out_shape=(LP, K), ca=1, cb=1, name=name, add=add)


def _chunk_cols_dw(a, g, l, chunk_w, into, *, name):
    K = a.shape[1]
    return _mm_core(a, g, a_spec=_bs((LP, K), lambda i, j, k: (0, 0)), b_spec=_bs((LP, chunk_w), lambda i, j, k: (0, j)),
                    o_spec=_bs((None, None, K, chunk_w), lambda i, j, k: (j, l, 0, 0)), grid=(1, N_CHIPS, 1),
                    out_shape=(N_CHIPS, DEPTH, K, chunk_w), ca=0, cb=0, name=name, into=into)


def _chunk_rows_mm(a, w, l, chunk_h, *, name):
    N = w.shape[3]
    return _mm_core(a, w, a_spec=_bs((TMF, chunk_h), lambda i, j, k: (i, k)),
                    b_spec=_bs((None, None, chunk_h, N), lambda i, j, k: (k, l, 0, 0)),
                    o_spec=_bs((TMF, N), lambda i, j, k: (i, 0)), grid=(N_IF, 1, N_CHIPS),
                    out_shape=(LP, N), ca=1, cb=0, name=name)


def _chunk_rows_dx(g, w, l, chunk_h, *, name):
    N = w.shape[3]
    return _mm_core(g, w, a_spec=_bs((TMF, N), lambda i, j, k: (i, 0)),
                    b_spec=_bs((None, None, chunk_h, N), lambda i, j, k: (j, l, 0, 0)),
                    o_spec=_bs((TMF, chunk_h), lambda i, j, k: (i, j)), grid=(N_IF, N_CHIPS, 1),
                    out_shape=(LP, N_CHIPS * chunk_h), ca=1, cb=1, name=name)


def _chunk_rows_dw(a, g, l, chunk_h, into, *, name):
    N = g.shape[1]
    return _mm_core(a, g, a_spec=_bs((LP, chunk_h), lambda i, j, k: (0, i)), b_spec=_bs((LP, N), lambda i, j, k: (0, 0)),
                    o_spec=_bs((None, None, chunk_h, N), lambda i, j, k: (i, l, 0, 0)), grid=(N_CHIPS, 1, 1),
                    out_shape=(N_CHIPS, DEPTH, chunk_h, N), ca=0, cb=0, name=name, into=into)


def _ffn_fwd(h, W, pre, l, gam, bet, tag):
    g = _chunk_cols_mm(h, W[pre + "_w_gate"], l, HP, name=f"{tag}_gate")
    u = _chunk_cols_mm(h, W[pre + "_w_up"], l, HP, name=f"{tag}_up")
    (act,) = _rowwise(_swiglu_fn, [g, u], [], [FP], name=f"{tag}_swiglu", tile=TM, ncol=N_CHIPS, out_dtypes=[BF16])
    o = _chunk_rows_mm(act, W[pre + "_w_down"], l, HP, name=f"{tag}_down")
    (out,) = _rowwise(_make_res_ln_fn(0.5), [h, o], [gam, bet], [D_MODEL], name=f"{tag}_ln", tile=272)
    return out, (h, g, u, act, o)


def _ffn_bwd(dout, saved, W, pre, l, gam, bet, GB, tag):
    h, g, u, act, o = saved
    (dh_a, do), (dgam, dbet) = _rowwise_bwd(_make_res_ln_fn(0.5), [h, o], [gam, bet], [dout], name=f"{tag}_ln_bwd",
                                            tile=272, grad_dtypes=[F32, BF16])
    dact = _chunk_rows_dx(do, W[pre + "_w_down"], l, HP, name=f"{tag}_down_dx")
    GB[pre + "_w_down"] = _chunk_rows_dw(act, do, l, HP, GB.get(pre + "_w_down"), name=f"{tag}_down_dw")
    (dg, du), _ = _rowwise_bwd(_swiglu_fn, [g, u], [], [dact], name=f"{tag}_swiglu_bwd", tile=TM, ncol=N_CHIPS,
                               grad_dtypes=[BF16, BF16])
    GB[pre + "_w_gate"] = _chunk_cols_dw(h, dg, l, HP, GB.get(pre + "_w_gate"), name=f"{tag}_gate_dw")
    GB[pre + "_w_up"] = _chunk_cols_dw(h, du, l, HP, GB.get(pre + "_w_up"), name=f"{tag}_up_dw")
    dh = _chunk_cols_dx(dg, W[pre + "_w_gate"], l, HP, dh_a, name=f"{tag}_gate_dx")
    dh = _chunk_cols_dx(du, W[pre + "_w_up"], l, HP, dh, name=f"{tag}_up_dx")
    return dh, dgam, dbet


def _mixer_fwd(h1, W, l, cosf, sins):
    tag = f"l{l}"
    proj = _mm(h1, W["w_in_p"][l], name=f"{tag}_in_proj")
    sv = {"h1": h1, "proj": proj}
    conv_w, conv_b = W["conv_w"][l], W["conv_b"][l][None]
    xc = _conv_fwd(proj, PC_XBC // BLOCK, conv_w, conv_b, name=f"{tag}_conv")
    dt_bias = _lane_pad(W["dt_bias"][l][None])
    (dt,) = _rowwise(_ssd_pre_fn, [(proj, BLOCK, PC_DT // BLOCK)], [dt_bias], [BLOCK], name=f"{tag}_ssd_dt", tile=272)
    xh = _heads(xc[:, :SSD_D], SSD_HEADS, SSD_HD)
    bm = _heads(xc[:, SSD_D:SSD_D + 128], SSD_GROUPS, SSD_STATE)
    cm = _heads(xc[:, SSD_D + 128:], SSD_GROUPS, SSD_STATE)
    dt8 = dt[:, :SSD_HEADS].T
    dtc, dtr = dt8[:, :, None], dt8[:, None, :]
    alog = jnp.broadcast_to(W["a_log"][l][:, None, None], (SSD_HEADS, 1, BLOCK))
    yh, prevs = _ssd_fwd(xh, bm, cm, dtc, dtr, alog, name=f"{tag}_ssd")
    y_raw = _unheads(yh)
    dskip = jnp.repeat(W["d_skip"][l], SSD_HD)[None]
    normg = W["ssd_norm_g"][l][None]
    post_rows = [y_raw, (xc, 256, 0), (proj, 256, PC_Z // 256)]
    (y_ssd,) = _rowwise(_ssd_post_fn, post_rows, [dskip, normg], [SSD_D], name=f"{tag}_ssd_post", tile=272,
                        ncol=SSD_GROUPS)
    sv.update(conv_w=conv_w, conv_b=conv_b, dt_bias=dt_bias, xh=xh, bm=bm, cm=cm, dtc=dtc, dtr=dtr, alog=alog,
              prevs=prevs, post_rows=post_rows, dskip=dskip, normg=normg)
    f_b = _lane_pad(W["fox_f_b"][l][None])
    cg, cgt = _fox_gate_fwd(proj, PC_FR // BLOCK, f_b, name=f"{tag}_fox_gate")
    fox_qkv = ((proj, PC_FQ // ATT_W), (proj, PC_FK // ATT_W), (proj, PC_FV // ATT_W))
    y_fox, lse_f = _attn_fwd(*fox_qkv, scale=FOX_HD ** -0.5, name=f"{tag}_fox_attn", bias=(cg, cgt))
    sv.update(f_b=f_b, cg=cg, cgt=cgt, fox_qkv=fox_qkv, y_fox=y_fox, lse_f=lse_f)
    gq, gkv = W["mla_q_norm_g"][l][None], W["mla_kv_norm_g"][l][None]
    norm_rows = [(proj, 256, PC_CQ // 256), (proj, BLOCK, PC_CKV // BLOCK)]
    qn, cn = _rowwise(_mla_norm_fn, norm_rows, [gq, gkv], [MLA_Q_LORA, MLA_KV_LORA], name=f"{tag}_mla_norm", tile=272,
                      out_dtypes=[BF16, BF16])
    qh = _mm(qn, W["mla_w_uq_p"][l], name=f"{tag}_mla_uq")
    kvh = _mm(cn, W["mla_w_ukv_p"][l], name=f"{tag}_mla_ukv")
    qr, kr = _rowwise(_rope_fn, [(qh, BLOCK, 2), (proj, BLOCK, PC_KR // BLOCK), cosf, sins], [], [BLOCK, BLOCK],
                      name=f"{tag}_rope", tile=272)
    mla_qkv = ((qh, 0), (kvh, 0), (kvh, 1))
    y_mla, lse_m = _attn_fwd(*mla_qkv, scale=(MLA_NOPE + MLA_ROPE) ** -0.5, name=f"{tag}_mla_attn",
                             rope=((qr, 0), (kr, 0)))
    sv.update(gq=gq, gkv=gkv, norm_rows=norm_rows, qn=qn, cn=cn, qr=qr, kr=kr, mla_qkv=mla_qkv, y_mla=y_mla, lse_m=lse_m)
    ycat = jnp.concatenate([y_ssd, y_fox, y_mla], axis=1).astype(BF16)
    mix = _chunk_rows_mm(ycat, W["w_out"], l, 256, name=f"{tag}_out_proj")
    (h2,) = _rowwise(_make_res_ln_fn(1.0), [h1, mix], [W["ln2_g"][l][None], W["ln2_b"][l][None]], [D_MODEL],
                     name=f"{tag}_ln2", tile=272)
    sv.update(mix=mix, ycat=ycat)
    return h2, sv


def _mixer_bwd(dh2, sv, W, l, cosf, sins, GB):
    tag = f"l{l}"
    G = {}
    proj = sv["proj"]
    ln2g, ln2b = W["ln2_g"][l][None], W["ln2_b"][l][None]
    (dh1_a, dmix), (dln2g, dln2b) = _rowwise_bwd(
        _make_res_ln_fn(1.0), [sv["h1"], sv["mix"]], [ln2g, ln2b], [dh2], name=f"{tag}_ln2_bwd", tile=272,
        grad_dtypes=[F32, BF16])
    G["ln2_g"], G["ln2_b"] = dln2g[0], dln2b[0]
    dycat = _chunk_rows_dx(dmix, W["w_out"], l, 256, name=f"{tag}_out_proj_dx")
    GB["w_out"] = _chunk_rows_dw(sv["ycat"], dmix, l, 256, GB.get("w_out"), name=f"{tag}_out_proj_dw")
    (dy_raw, dxs_a, dz), (ddskip, dnormg) = _rowwise_bwd(
        _ssd_post_fn, sv["post_rows"], [sv["dskip"], sv["normg"]], [dycat[:, :SSD_D]],
        name=f"{tag}_ssd_post_bwd", tile=272, ncol=SSD_GROUPS)
    G["ssd_norm_g"] = dnormg[0]
    G["d_skip"] = ddskip.reshape(SSD_HEADS, SSD_HD).sum(axis=1)
    dxh, dbm, dcm, ddtc, ddtr, dal = _ssd_bwd(sv["xh"], sv["bm"], sv["cm"], sv["dtc"], sv["dtr"], sv["alog"],
                                              sv["prevs"], _heads(dy_raw, SSD_HEADS, SSD_HD), name=f"{tag}_ssd_bwd")
    G["a_log"] = dal[:, 0, 0]
    dxc = jnp.concatenate([dxs_a + _unheads(dxh), _unheads(dbm), _unheads(dcm)], axis=1)
    dxbc, G["conv_w"], dconv_b = _conv_bwd(proj, PC_XBC // BLOCK, sv["conv_w"], sv["conv_b"], dxc,
                                           name=f"{tag}_conv_bwd")
    G["conv_b"] = dconv_b[0]
    ddt = _lane_pad((ddtc[:, :, 0] + ddtr[:, 0, :]).T)
    (ddt_raw,), (ddt_bias,) = _rowwise_bwd(_ssd_pre_fn, [(proj, BLOCK, PC_DT // BLOCK)], [sv["dt_bias"]], [ddt],
                                           name=f"{tag}_ssd_dt_bwd", tile=272)
    G["dt_bias"] = ddt_bias[0, :SSD_HEADS]
    dfq, dfk, dfv, dcg, dcgt = _attn_bwd(*sv["fox_qkv"], sv["y_fox"], sv["lse_f"], (dycat, SSD_D // ATT_W),
                                         scale=FOX_HD ** -0.5, name=f"{tag}_fox_attn_bwd", bias=(sv["cg"], sv["cgt"]))
    df_raw, dfb = _fox_gate_bwd(proj, PC_FR // BLOCK, sv["f_b"], dcg, dcgt, name=f"{tag}_fox_gate_bwd")
    G["fox_f_b"] = dfb[0, :FOX_HEADS]
    dqn_h, dkn_h, dv_h, dqr, dkr = _attn_bwd(
        *sv["mla_qkv"], sv["y_mla"], sv["lse_m"], (dycat, (SSD_D + FOX_D) // ATT_W),
        scale=(MLA_NOPE + MLA_ROPE) ** -0.5, name=f"{tag}_mla_attn_bwd", rope=((sv["qr"], 0), (sv["kr"], 0)))
    dq_rope, dk_rope = _rowwise(_rope_t_fn, [dqr, dkr, cosf, sins], [], [BLOCK, BLOCK], name=f"{tag}_rope_bwd",
                                tile=272)
    dqh = jnp.concatenate([dqn_h, dq_rope], axis=1).astype(BF16)
    dkvh = jnp.concatenate([dkn_h, dv_h], axis=1).astype(BF16)
    dqn = _mm(dqh, W["mla_w_uq_p"][l], tb=True, name=f"{tag}_mla_uq_dx")
    G["mla_w_uq_p"] = _mm(sv["qn"], dqh, ta=True, name=f"{tag}_mla_uq_dw")
    dcn = _mm(dkvh, W["mla_w_ukv_p"][l], tb=True, name=f"{tag}_mla_ukv_dx")
    G["mla_w_ukv_p"] = _mm(sv["cn"], dkvh, ta=True, name=f"{tag}_mla_ukv_dw")
    (dcq, dckv), (dgq, dgkv) = _rowwise_bwd(_mla_norm_fn, sv["norm_rows"], [sv["gq"], sv["gkv"]], [dqn, dcn],
                                            name=f"{tag}_mla_norm_bwd", tile=272)
    G["mla_q_norm_g"], G["mla_kv_norm_g"] = dgq[0], dgkv[0]
    dproj = jnp.concatenate([dz, dxbc, dfq, dfk, dfv, dcq, dckv, ddt_raw, df_raw, dk_rope], axis=1).astype(BF16)
    dh1 = _mm(dproj, W["w_in_p"][l], tb=True, add=dh1_a, name=f"{tag}_in_proj_dx")
    G["w_in_p"] = _mm(sv["h1"], dproj, ta=True, name=f"{tag}_in_proj_dw")
    return dh1, G


def _local_step(x, target, W):
    h = jnp.concatenate([jnp.zeros((PAD_ROWS, D_MODEL), F32), W["meta"], x], axis=0)
    tgt = jnp.concatenate([jnp.zeros((BLOCK, D_MODEL), F32), target], axis=0)
    cosf, sins = _rope_tables()
    ln = lambda n, l: W[n][l][None]
    saved = []
    for l in range(DEPTH):
        h1, s1 = _ffn_fwd(h, W, "ffn1", l, ln("ln1_g", l), ln("ln1_b", l), f"l{l}_ffn1")
        h2, sm = _mixer_fwd(h1, W, l, cosf, sins)
        h, s2 = _ffn_fwd(h2, W, "ffn2", l, ln("ln3_g", l), ln("ln3_b", l), f"l{l}_ffn2")
        saved.append((s1, sm, s2))
    dh, loss = _loss_head(h, tgt, name="loss_head")
    grads = [None] * DEPTH
    GB = {}
    for l in reversed(range(DEPTH)):
        s1, sm, s2 = saved[l]
        G = {}
        dh, dg, db = _ffn_bwd(dh, s2, W, "ffn2", l, ln("ln3_g", l), ln("ln3_b", l), GB, f"l{l}_ffn2")
        G["ln3_g"], G["ln3_b"] = dg[0], db[0]
        dh, Gm = _mixer_bwd(dh, sm, W, l, cosf, sins, GB)
        G.update(Gm)
        dh, dg, db = _ffn_bwd(dh, s1, W, "ffn1", l, ln("ln1_g", l), ln("ln1_b", l), GB, f"l{l}_ffn1")
        G["ln1_g"], G["ln1_b"] = dg[0], db[0]
        grads[l] = G
    return loss, dh, grads, GB


WEIGHTS = ['meta', 'ffn1_w_gate', 'ffn1_w_up', 'ffn1_w_down', 'ln1_g', 'ln1_b', 'w_in', 'conv_w', 'conv_b', 'dt_bias',
           'a_log', 'd_skip', 'ssd_norm_g', 'fox_f_b', 'mla_q_norm_g', 'mla_w_uq', 'mla_kv_norm_g', 'mla_w_ukv',
           'w_out', 'ln2_g', 'ln2_b', 'ffn2_w_gate', 'ffn2_w_up', 'ffn2_w_down', 'ln3_g', 'ln3_b']
SMALL = ["ln1_g", "ln1_b", "conv_b", "dt_bias", "a_log", "d_skip", "ssd_norm_g", "fox_f_b", "mla_q_norm_g",
         "mla_kv_norm_g", "ln2_g", "ln2_b", "ln3_g", "ln3_b"]
MATMUL_W = ["ffn1_w_gate", "ffn1_w_up", "ffn1_w_down", "w_in", "mla_w_uq", "mla_w_ukv", "w_out", "ffn2_w_gate",
            "ffn2_w_up", "ffn2_w_down"]
SMALL_ROWS = 312


def _pad_to(a, axis, size):
    pads = [(0, 0)] * a.ndim
    pads[axis] = (0, size - a.shape[axis])
    return jnp.pad(a, pads)


def _chip_cols(full, chip, width):
    return lax.dynamic_slice_in_dim(full, chip * width, width, axis=full.ndim - 1)


def kernel(x, meta, ffn1_w_gate, ffn1_w_up, ffn1_w_down, ln1_g, ln1_b, w_in, conv_w, conv_b, dt_bias, a_log, d_skip, ssd_norm_g, fox_f_b, mla_q_norm_g, mla_w_uq, mla_kv_norm_g, mla_w_ukv, w_out, ln2_g, ln2_b, ffn2_w_gate, ffn2_w_up, ffn2_w_down, ln3_g, ln3_b, loss_target, m_meta, m_ffn1_w_gate, m_ffn1_w_up, m_ffn1_w_down, m_ln1_g, m_ln1_b, m_w_in, m_conv_w, m_conv_b, m_dt_bias, m_a_log, m_d_skip, m_ssd_norm_g, m_fox_f_b, m_mla_q_norm_g, m_mla_w_uq, m_mla_kv_norm_g, m_mla_w_ukv, m_w_out, m_ln2_g, m_ln2_b, m_ffn2_w_gate, m_ffn2_w_up, m_ffn2_w_down, m_ln3_g, m_ln3_b, v_meta, v_ffn1_w_gate, v_ffn1_w_up, v_ffn1_w_down, v_ln1_g, v_ln1_b, v_w_in, v_conv_w, v_conv_b, v_dt_bias, v_a_log, v_d_skip, v_ssd_norm_g, v_fox_f_b, v_mla_q_norm_g, v_mla_w_uq, v_mla_kv_norm_g, v_mla_w_ukv, v_w_out, v_ln2_g, v_ln2_b, v_ffn2_w_gate, v_ffn2_w_up, v_ffn2_w_down, v_ln3_g, v_ln3_b):
    args = dict(locals())
    w = {n: args[n] for n in WEIGHTS}
    m = {n: args["m_" + n] for n in WEIGHTS}
    v = {n: args["v_" + n] for n in WEIGHTS}
    xcoord, ycoord, _ = _my_pos()
    chip = 2 * xcoord + ycoord

    send = {}
    for pre in ("ffn1", "ffn2"):
        send[pre + "_w_gate"] = _pad_to(w[pre + "_w_gate"], 2, HP).astype(BF16)
        send[pre + "_w_up"] = _pad_to(w[pre + "_w_up"], 2, HP).astype(BF16)
        send[pre + "_w_down"] = _pad_to(w[pre + "_w_down"], 1, HP).astype(BF16)
    send["w_in"] = _pad_to(w["w_in"], 2, IN_SHARD_P).astype(BF16)
    for n in ("mla_w_uq", "mla_w_ukv", "w_out"):
        send[n] = w[n].astype(BF16)
    send["meta"] = w["meta"].reshape(2, N_META // 2, D_MODEL // N_CHIPS)
    send["conv_w"] = w["conv_w"]
    order = ["ffn1_w_gate", "ffn1_w_up", "ffn1_w_down", "meta", "w_in", "conv_w", "mla_w_uq", "mla_w_ukv", "w_out",
             "ffn2_w_gate", "ffn2_w_up", "ffn2_w_down"]
    got = dict(zip(order, _allgather_chips([send[n] for n in order])))

    W = {n: got[n] for n in got if n.startswith("ffn") or n == "w_out"}
    cat = lambda n, cut=None: jnp.concatenate([got[n][k][..., :cut] for k in range(N_CHIPS)], axis=-1)
    w_in_full = cat("w_in", IN_SHARD)
    W["w_in_p"] = [_pad_in_proj(w_in_full[l]) for l in range(DEPTH)]
    W["mla_w_uq_p"] = [_regroup_uq(cat("mla_w_uq")[l]) for l in range(DEPTH)]
    W["mla_w_ukv_p"] = [_regroup_ukv(cat("mla_w_ukv")[l]) for l in range(DEPTH)]
    W["conv_w"] = cat("conv_w")
    W["meta"] = jnp.concatenate([got["meta"][k].reshape(N_META, D_MODEL // N_CHIPS) for k in range(N_CHIPS)], axis=1)
    for n in SMALL:
        W[n] = w[n]

    loss, dh0, G, GB = _local_step(x[0], loss_target[0], W)

    def chunked(name, ungroup, width, pad):
        per_layer = []
        for l in range(DEPTH):
            full = ungroup(G[l][name])
            sh = jnp.moveaxis(full.reshape(full.shape[0], N_CHIPS, width), 1, 0)
            per_layer.append(_pad_to(sh, 2, pad))
        return jnp.stack(per_layer, axis=1)

    GB["w_in"] = chunked("w_in_p", _unpad_in_proj, IN_SHARD, IN_SHARD_P)
    GB["mla_w_uq"] = chunked("mla_w_uq_p", _ungroup_uq, MLA_NOPE + MLA_ROPE, MLA_NOPE + MLA_ROPE)
    GB["mla_w_ukv"] = chunked("mla_w_ukv_p", _ungroup_ukv, MLA_NOPE + MLA_V, MLA_NOPE + MLA_V)
    reduced = dict(zip(MATMUL_W, _reduce_scatter([GB[n] for n in MATMUL_W], MATMUL_W)))

    small_parts = [jnp.stack([G[l][n] for l in range(DEPTH)]).reshape(-1) for n in SMALL]
    small_parts += [jnp.stack([G[l]["conv_w"] for l in range(DEPTH)]).reshape(-1), dh0[PAD_ROWS:BLOCK].reshape(-1),
                    loss[0, :1]]
    flat = jnp.concatenate(small_parts)
    flat = jnp.pad(flat, (0, SMALL_ROWS * BLOCK - flat.shape[0]))
    red = _allreduce_small(flat.reshape(SMALL_ROWS, BLOCK)).reshape(-1)
    grads, off = {}, 0
    for n in SMALL:
        size = int(np.prod(w[n].shape))
        grads[n] = red[off:off + size].reshape(w[n].shape)
        off += size
    conv_full = red[off:off + DEPTH * SSD_CONV * 768].reshape(DEPTH, SSD_CONV, 768)
    off += DEPTH * SSD_CONV * 768
    meta_full = red[off:off + N_META * D_MODEL].reshape(N_META, D_MODEL)
    off += N_META * D_MODEL
    loss_out = red[off]
    grads["conv_w"] = _chip_cols(conv_full, chip, 768 // N_CHIPS)
    grads["meta"] = _chip_cols(meta_full, chip, D_MODEL // N_CHIPS)

    delta, new_m, new_v = {}, {}, {}
    for n in MATMUL_W:
        grads[n], delta[n], new_m[n], new_v[n] = _adamw(w[n], reduced[n], m[n], v[n], name=f"adamw_{n}")
    rest = [n for n in WEIGHTS if n not in MATMUL_W]

    def pack_small(d):
        f = jnp.concatenate([d[n].reshape(-1) for n in rest])
        tot = -(-f.shape[0] // (8 * BLOCK)) * 8 * BLOCK
        return jnp.pad(f, (0, tot - f.shape[0])).reshape(-1, BLOCK)

    _, d2, m2, v2 = _adamw(pack_small(w), pack_small(grads), pack_small(m), pack_small(v), name="adamw_small")
    off = 0
    for n in rest:
        size = int(np.prod(w[n].shape))
        for dst, src in ((delta, d2), (new_m, m2), (new_v, v2)):
            dst[n] = src.reshape(-1)[off:off + size].reshape(w[n].shape)
        off += size

    grad_x = dh0[BLOCK:][None]
    return (loss_out, grad_x, *[grads[n] for n in WEIGHTS], *[delta[n] for n in WEIGHTS],
            *[new_m[n] for n in WEIGHTS], *[new_v[n] for n in WEIGHTS])
```

```python
import functools

import numpy as np
import jax
import jax.numpy as jnp
from jax import lax
from jax.experimental import pallas as pl
from jax.experimental.pallas import tpu as pltpu

F32 = jnp.float32
BF16 = jnp.bfloat16
MESH = pl.DeviceIdType.MESH

D_MODEL = 1024
SEQ = 2048
N_META = 16
BLOCK = 128
PAD_ROWS = 112
LP = PAD_ROWS + N_META + SEQ
N_CHUNK = LP // BLOCK
DEPTH = 2
D_FF = 2816
N_CHIPS = 4
FF_SHARD = D_FF // N_CHIPS
HP = 768
FP = N_CHIPS * HP
SSD_HEADS, SSD_HD, SSD_D, SSD_GROUPS, SSD_STATE, SSD_CONV = 8, 64, 512, 2, 64, 4
FOX_HEADS, FOX_HD, FOX_D = 4, 64, 256
MLA_HEADS, MLA_Q_LORA, MLA_KV_LORA, MLA_NOPE, MLA_ROPE, MLA_V, MLA_D = 4, 256, 128, 64, 32, 64, 256
ROPE_HALF = MLA_ROPE // 2
ROPE_THETA = 10000.0
N_IN = 2476
IN_SHARD = N_IN // N_CHIPS
IN_SHARD_P = 640
ALPHA = (2 * DEPTH) ** 0.25
EPS = 1e-5
ADAM_LR, ADAM_B1, ADAM_B2, ADAM_EPS, ADAM_WD, ADAM_STEP = 0.001, 0.9, 0.999, 1e-08, 0.01, 10
NEG = -1e30
TM = 544

VMEM_LIMIT_BYTES = 56 * 1024 * 1024

PC_Z, PC_XBC, PC_FQ, PC_FK, PC_FV, PC_CQ, PC_CKV, PC_DT, PC_FR, PC_KR, PC_END = (
    0, 512, 1280, 1536, 1792, 2048, 2304, 2432, 2560, 2688, 2816)
OC_Z, OC_XBC, OC_DT, OC_FQ, OC_FK, OC_FV, OC_FR, OC_CQ, OC_CKV, OC_KR = (
    0, 512, 1280, 1288, 1544, 1800, 2056, 2060, 2316, 2444)


def _cparams(sem=None):
    return pltpu.CompilerParams(dimension_semantics=sem, vmem_limit_bytes=VMEM_LIMIT_BYTES)


def _tile(n, cap, mult):
    best = None
    for t in range(mult, min(n, cap) + 1, mult):
        if n % t == 0:
            best = t
    return best if best is not None else n


def _bs(shape, fn):
    return pl.BlockSpec(shape, fn)


ANY = pl.BlockSpec(memory_space=pl.ANY)


def _dims(ca, cb):
    return (((ca,), (cb,)), ((), ()))


def _raw_bdot(a, b, ca, cb):
    return lax.dot_general(a.astype(BF16), b.astype(BF16), _dims(ca, cb), preferred_element_type=F32)


@functools.partial(jax.custom_vjp, nondiff_argnums=(2, 3))
def _bdot(a, b, ca, cb):
    return _raw_bdot(a, b, ca, cb)


def _bdot_fwd(a, b, ca, cb):
    return _raw_bdot(a, b, ca, cb), (a, b)


def _bdot_bwd(ca, cb, res, g):
    a, b = res
    if (ca, cb) == (1, 0):
        return _raw_bdot(g, b, 1, 1), _raw_bdot(a, g, 0, 0)
    if (ca, cb) == (1, 1):
        return _raw_bdot(g, b, 1, 0), _raw_bdot(g, a, 0, 0)
    if (ca, cb) == (0, 0):
        return _raw_bdot(b, g, 1, 1), _raw_bdot(a, g, 1, 0)
    raise NotImplementedError((ca, cb))


_bdot.defvjp(_bdot_fwd, _bdot_bwd)


def _mm_core(a, b, *, a_spec, b_spec, o_spec, grid, out_shape, ca, cb, name, add=None, into=None):
    nk = grid[2]
    has_add, has_into = add is not None, into is not None
    acc_shape = tuple(d for d in o_spec.block_shape if d is not None)

    def body(*refs):
        a_ref, b_ref = refs[0], refs[1]
        p = 2
        add_ref = refs[p] if has_add else None
        p += int(has_add) + int(has_into)
        o_ref, acc_ref = refs[p], refs[p + 1]
        k = pl.program_id(2)

        @pl.when(k == 0)
        def _():
            acc_ref[...] = jnp.zeros_like(acc_ref)

        acc_ref[...] += _raw_bdot(a_ref[...], b_ref[...], ca, cb)

        @pl.when(k == nk - 1)
        def _():
            r = acc_ref[...]
            if has_add:
                r = r + add_ref[...]
            o_ref[...] = r

    ins = [a, b] + ([add] if has_add else []) + ([into] if has_into else [])
    in_specs = [a_spec, b_spec] + ([o_spec] if has_add else []) + ([ANY] if has_into else [])
    return pl.pallas_call(
        body, name=name, grid=grid, in_specs=in_specs, out_specs=o_spec,
        out_shape=jax.ShapeDtypeStruct(out_shape, F32), scratch_shapes=[pltpu.VMEM(acc_shape, F32)],
        input_output_aliases=({len(ins) - 1: 0} if has_into else {}),
        compiler_params=_cparams(("parallel", "parallel", "arbitrary")),
    )(*ins)


MM_VMEM_BUDGET = 40 * 1024 * 1024


def _divisors(n, mult):
    return [t for t in range(mult, n + 1, mult) if n % t == 0] or [n]


def _pick_tiles(M, N, K, a_bytes, b_bytes, ta, has_add):
    best = None
    for tm in _divisors(M, 128 if ta else 16):
        for tn in _divisors(N, 128):
            vmem = 2 * tm * K * a_bytes + 2 * K * tn * b_bytes + (3 + 2 * int(has_add)) * tm * tn * 4
            if vmem <= MM_VMEM_BUDGET:
                key = ((M // tm) * (N // tn), -tn)
                if best is None or key < best[0]:
                    best = (key, tm, tn)
    assert best is not None, (M, N, K)
    return best[1], best[2], K


def _mm(a, b, *, ta=False, tb=False, add=None, name):
    if ta:
        K, M = a.shape
    else:
        M, K = a.shape
    if tb:
        N, Kb = b.shape
    else:
        Kb, N = b.shape
    assert K == Kb, (a.shape, b.shape, ta, tb)
    tm, tn, tk = _pick_tiles(M, N, K, a.dtype.itemsize, b.dtype.itemsize, ta, add is not None)
    a_spec = _bs((tk, tm), lambda i, j, k: (k, i)) if ta else _bs((tm, tk), lambda i, j, k: (i, k))
    b_spec = _bs((tn, tk), lambda i, j, k: (j, k)) if tb else _bs((tk, tn), lambda i, j, k: (k, j))
    return _mm_core(a, b, a_spec=a_spec, b_spec=b_spec, o_spec=_bs((tm, tn), lambda i, j, k: (i, j)),
                    grid=(M // tm, N // tn, K // tk), out_shape=(M, N), ca=0 if ta else 1, cb=1 if tb else 0,
                    name=name, add=add)


def _row_entry(r, ncol):
    if isinstance(r, tuple):
        return r
    return r, r.shape[1] // ncol, 0


def _rowwise(fn, rows, pars, out_cols, *, name, tile, ncol=1, out_dtypes=None):
    rows = [_row_entry(r, ncol) for r in rows]
    L = rows[0][0].shape[0]
    nr, npar = len(rows), len(pars)
    in_specs = [_bs((tile, w), lambda g, i, o=o: (i, o + g)) for _, w, o in rows]
    in_specs += [_bs((p.shape[0], p.shape[1] // ncol), lambda g, i: (0, g)) for p in pars]
    out_specs = [_bs((tile, c // ncol), lambda g, i: (i, g)) for c in out_cols]

    def body(*refs):
        ins, outs = refs[:nr + npar], refs[nr + npar:]
        row0 = pl.program_id(1) * tile
        res = fn(row0, *[r[...] for r in ins])
        for o, v in zip(outs, res):
            o[...] = v.astype(o.dtype)

    return pl.pallas_call(
        body, name=name, grid=(ncol, L // tile), in_specs=in_specs, out_specs=out_specs,
        out_shape=[jax.ShapeDtypeStruct((L, c), d) for c, d in zip(out_cols, out_dtypes or [F32] * len(out_cols))],
        compiler_params=_cparams(("parallel", "parallel")),
    )(*[r[0] for r in rows], *pars)


def _rowwise_bwd(fn, rows, pars, douts, *, name, tile, ncol=1, row_grad=None, grad_dtypes=None):
    rows = [_row_entry(r, ncol) for r in rows]
    L = rows[0][0].shape[0]
    nr, npar, nd = len(rows), len(pars), len(douts)
    row_grad = [True] * nr if row_grad is None else row_grad
    in_specs = [_bs((tile, w), lambda g, i, o=o: (i, o + g)) for _, w, o in rows]
    in_specs += [_bs((p.shape[0], p.shape[1] // ncol), lambda g, i: (0, g)) for p in pars]
    in_specs += [_bs((tile, d.shape[1] // ncol), lambda g, i: (i, g)) for d in douts]
    g_widths = [w * ncol for (_, w, _), f in zip(rows, row_grad) if f]
    out_specs = [_bs((tile, w // ncol), lambda g, i: (i, g)) for w in g_widths]
    out_specs += [_bs((p.shape[0], p.shape[1] // ncol), lambda g, i: (0, g)) for p in pars]
    out_shape = [jax.ShapeDtypeStruct((L, w), d) for w, d in zip(g_widths, grad_dtypes or [F32] * len(g_widths))]
    out_shape += [jax.ShapeDtypeStruct(p.shape, F32) for p in pars]

    def body(*refs):
        ins = refs[:nr + npar]
        dos = refs[nr + npar:nr + npar + nd]
        outs = refs[nr + npar + nd:]
        i = pl.program_id(1)
        row0 = i * tile
        _, vjp = jax.vjp(lambda *a: tuple(fn(row0, *a)), *[r[...] for r in ins])
        grads = vjp(tuple(d[...].astype(F32) for d in dos))
        o = 0
        for j in range(nr):
            if row_grad[j]:
                outs[o][...] = grads[j].astype(outs[o].dtype)
                o += 1
        for j in range(npar):
            g, ref = grads[nr + j], outs[o + j]

            @pl.when(i == 0)
            def _(g=g, ref=ref):
                ref[...] = g

            @pl.when(i > 0)
            def _(g=g, ref=ref):
                ref[...] += g

    res = pl.pallas_call(
        body, name=name, grid=(ncol, L // tile), in_specs=in_specs, out_specs=out_specs, out_shape=out_shape,
        compiler_params=_cparams(("parallel", "arbitrary")),
    )(*[r[0] for r in rows], *pars, *douts)
    return res[:len(g_widths)], res[len(g_widths):]


def _row_ids(row0, shape):
    return row0 + lax.broadcasted_iota(jnp.int32, shape, 0)


def _sigmoid(x):
    return 1.0 / (1.0 + jnp.exp(-x))


@jax.custom_vjp
def _softplus(x):
    return jnp.maximum(x, 0.0) + jnp.log(1.0 + jnp.exp(-jnp.abs(x)))


def _softplus_fwd(x):
    return _softplus(x), x


def _softplus_bwd(x, g):
    return (g * _sigmoid(x),)


_softplus.defvjp(_softplus_fwd, _softplus_bwd)


def _silu(x):
    return x * _sigmoid(x)


def _swiglu_fn(row0, g, u):
    return (_silu(g) * u,)


def _make_res_ln_fn(scale):
    def fn(row0, h, o, gam, bet):
        pre = ALPHA * h + scale * o
        mu = jnp.mean(pre, axis=-1, keepdims=True)
        xc = pre - mu
        var = jnp.mean(xc * xc, axis=-1, keepdims=True)
        return (xc * lax.rsqrt(var + EPS) * gam + bet,)
    return fn


def _ssd_pre_fn(row0, raw, bias):
    dt = _softplus(raw + bias)
    return (jnp.where(_row_ids(row0, raw.shape) >= PAD_ROWS, dt, 0.0),)


def _ssd_post_fn(row0, y, xs, z, dskip, normg):
    v = (y + dskip * xs) * _silu(z)
    v = v * lax.rsqrt(jnp.mean(v * v, axis=-1, keepdims=True) + EPS)
    return (v * normg,)


def _mla_norm_fn(row0, cq, ckv, gq, gkv):
    qn = cq * lax.rsqrt(jnp.mean(cq * cq, axis=-1, keepdims=True) + EPS) * gq
    cn = ckv * lax.rsqrt(jnp.mean(ckv * ckv, axis=-1, keepdims=True) + EPS) * gkv
    return qn, cn


def _rope_fn(row0, q, k, cosf, sins):
    return (q * cosf + pltpu.roll(q, 64, 1) * sins, k * cosf + pltpu.roll(k, 64, 1) * sins)


def _rope_t_fn(row0, gq, gk, cosf, sins):
    return (gq * cosf + pltpu.roll(gq * sins, 64, 1), gk * cosf + pltpu.roll(gk * sins, 64, 1))


def _conv_fwd(x, x_off, w, b, *, name):
    C = w.shape[1]

    def body(x_ref, w_ref, b_ref, o_ref):
        rows = lax.broadcasted_iota(jnp.int32, (LP, BLOCK), 0)
        xv = jnp.where(rows >= PAD_ROWS, x_ref[...], 0.0)
        acc = b_ref[...] + w_ref[3:4, :] * xv
        for k in range(SSD_CONV - 1):
            acc = acc + w_ref[k:k + 1, :] * pltpu.roll(xv, SSD_CONV - 1 - k, 0)
        o_ref[...] = _silu(acc)

    return pl.pallas_call(
        body, name=name, grid=(C // BLOCK,),
        in_specs=[_bs((LP, BLOCK), lambda j: (0, j + x_off)), _bs((SSD_CONV, BLOCK), lambda j: (0, j)),
                  _bs((1, BLOCK), lambda j: (0, j))],
        out_specs=_bs((LP, BLOCK), lambda j: (0, j)),
        out_shape=jax.ShapeDtypeStruct((LP, C), F32), compiler_params=_cparams(("parallel",)),
    )(x, w, b)


def _conv_bwd(x, x_off, w, b, dout, *, name):
    C = w.shape[1]

    def body(x_ref, w_ref, b_ref, do_ref, dx_ref, dw_ref, db_ref):
        rows = lax.broadcasted_iota(jnp.int32, (LP, BLOCK), 0)
        real = rows >= PAD_ROWS
        xv = jnp.where(real, x_ref[...], 0.0)
        shifted = [pltpu.roll(xv, SSD_CONV - 1 - k, 0) for k in range(SSD_CONV - 1)] + [xv]
        acc = b_ref[...]
        for k in range(SSD_CONV):
            acc = acc + w_ref[k:k + 1, :] * shifted[k]
        sig = _sigmoid(acc)
        dacc = jnp.where(real, do_ref[...] * (sig * (1.0 + acc * (1.0 - sig))), 0.0)
        db_ref[...] = jnp.sum(dacc, axis=0, keepdims=True)
        dx = w_ref[3:4, :] * dacc
        for k in range(SSD_CONV):
            dw_ref[k:k + 1, :] = jnp.sum(dacc * shifted[k], axis=0, keepdims=True)
            if k < SSD_CONV - 1:
                dx = dx + w_ref[k:k + 1, :] * pltpu.roll(dacc, LP - (SSD_CONV - 1 - k), 0)
        dx_ref[...] = jnp.where(real, dx, 0.0)

    return pl.pallas_call(
        body, name=name, grid=(C // BLOCK,),
        in_specs=[_bs((LP, BLOCK), lambda j: (0, j + x_off)), _bs((SSD_CONV, BLOCK), lambda j: (0, j)),
                  _bs((1, BLOCK), lambda j: (0, j)), _bs((LP, BLOCK), lambda j: (0, j))],
        out_specs=[_bs((LP, BLOCK), lambda j: (0, j)), _bs((SSD_CONV, BLOCK), lambda j: (0, j)),
                   _bs((1, BLOCK), lambda j: (0, j))],
        out_shape=[jax.ShapeDtypeStruct((LP, C), F32), jax.ShapeDtypeStruct((SSD_CONV, C), F32),
                   jax.ShapeDtypeStruct((1, C), F32)],
        compiler_params=_cparams(("parallel",)),
    )(x, w, b, dout)


_BDIMS = {"nn": (((2,), (1,)), ((0,), (0,))), "nt": (((2,), (2,)), ((0,), (0,))), "tn": (((1,), (1,)), ((0,), (0,)))}


def _raw_bdot3(a, b, mode):
    return lax.dot_general(a.astype(BF16), b.astype(BF16), _BDIMS[mode], preferred_element_type=F32)


@functools.partial(jax.custom_vjp, nondiff_argnums=(2,))
def _bdot3(a, b, mode):
    return _raw_bdot3(a, b, mode)


def _bdot3_fwd(a, b, mode):
    return _raw_bdot3(a, b, mode), (a, b)


def _bdot3_bwd(mode, res, g):
    a, b = res
    if mode == "nn":
        return _raw_bdot3(g, b, "nt"), _raw_bdot3(a, g, "tn")
    if mode == "nt":
        return _raw_bdot3(g, b, "nn"), _raw_bdot3(g, a, "tn")
    return _raw_bdot3(b, g, "nt"), _raw_bdot3(a, g, "nn")


_bdot3.defvjp(_bdot3_fwd, _bdot3_bwd)


def _ssd_chunk(x, bm, cm, dtc, dtr, alog, prev):
    rep = SSD_HEADS // SSD_GROUPS
    per_head = lambda t: jnp.broadcast_to(t[:, None], (SSD_GROUPS, rep) + t.shape[1:]).reshape((SSD_HEADS,) + t.shape[1:])
    bm, cm = per_head(bm), per_head(cm)
    lane = lax.broadcasted_iota(jnp.int32, alog.shape, 2)
    a_neg = -jnp.exp(jnp.sum(jnp.where(lane == 0, alog, 0.0), axis=2, keepdims=True))
    ac_in = dtc * a_neg
    ar_in = dtr * a_neg
    li = lax.broadcasted_iota(jnp.int32, (1, BLOCK, BLOCK), 1)
    si = lax.broadcasted_iota(jnp.int32, (1, BLOCK, BLOCK), 2)
    causal = li >= si
    acum_c = jnp.sum(jnp.where(causal, ar_in, 0.0), axis=2, keepdims=True)
    acum_r = jnp.sum(jnp.where(li <= si, ac_in, 0.0), axis=1, keepdims=True)
    total = jnp.sum(ar_in, axis=2, keepdims=True)
    seg = jnp.exp(jnp.where(causal, acum_c - acum_r, NEG))
    xdt = x * dtc
    cb = _bdot3(cm, bm, "nt")
    y = _bdot3(cb * seg, xdt, "nn") + _bdot3(cm, prev, "nt") * jnp.exp(acum_c)
    st = _bdot3(xdt, bm * jnp.exp(total - acum_c), "tn")
    return y, prev * jnp.exp(total) + st


def _ssd_specs(rev):
    ci = (lambda c: N_CHUNK - 1 - c) if rev else (lambda c: c)
    x_spec = _bs((SSD_HEADS, BLOCK, SSD_HD), lambda c: (0, ci(c), 0))
    g_spec = _bs((SSD_GROUPS, BLOCK, SSD_STATE), lambda c: (0, ci(c), 0))
    dtc_spec = _bs((SSD_HEADS, BLOCK, 1), lambda c: (0, ci(c), 0))
    dtr_spec = _bs((SSD_HEADS, 1, BLOCK), lambda c: (0, 0, ci(c)))
    al_spec = _bs((SSD_HEADS, 1, BLOCK), lambda c: (0, 0, 0))
    st_spec = _bs((None, SSD_HEADS, SSD_HD, SSD_STATE), lambda c: (ci(c), 0, 0, 0))
    return x_spec, g_spec, dtc_spec, dtr_spec, al_spec, st_spec


def _ssd_fwd(x, bm, cm, dtc, dtr, alog, *, name):
    x_spec, g_spec, dtc_spec, dtr_spec, al_spec, st_spec = _ssd_specs(False)

    def body(x_ref, b_ref, c_ref, dtc_ref, dtr_ref, al_ref, y_ref, prev_ref, state):
        @pl.when(pl.program_id(0) == 0)
        def _():
            state[...] = jnp.zeros_like(state)

        prev = state[...]
        prev_ref[...] = prev
        y, new = _ssd_chunk(x_ref[...], b_ref[...], c_ref[...], dtc_ref[...], dtr_ref[...], al_ref[...], prev)
        y_ref[...] = y
        state[...] = new

    return pl.pallas_call(
        body, name=name, grid=(N_CHUNK,),
        in_specs=[x_spec, g_spec, g_spec, dtc_spec, dtr_spec, al_spec], out_specs=[x_spec, st_spec],
        out_shape=[jax.ShapeDtypeStruct((SSD_HEADS, LP, SSD_HD), F32),
                   jax.ShapeDtypeStruct((N_CHUNK, SSD_HEADS, SSD_HD, SSD_STATE), F32)],
        scratch_shapes=[pltpu.VMEM((SSD_HEADS, SSD_HD, SSD_STATE), F32)],
        compiler_params=_cparams(("arbitrary",)),
    )(x, bm, cm, dtc, dtr, alog)


def _ssd_bwd(x, bm, cm, dtc, dtr, alog, prevs, dy, *, name):
    x_spec, g_spec, dtc_spec, dtr_spec, al_spec, st_spec = _ssd_specs(True)

    def body(x_ref, b_ref, c_ref, dtc_ref, dtr_ref, al_ref, prev_ref, dy_ref,
             dx_ref, db_ref, dc_ref, ddtc_ref, ddtr_ref, dal_ref, dstate):
        c = pl.program_id(0)

        @pl.when(c == 0)
        def _():
            dstate[...] = jnp.zeros_like(dstate)

        _, vjp = jax.vjp(_ssd_chunk, x_ref[...], b_ref[...], c_ref[...], dtc_ref[...], dtr_ref[...], al_ref[...],
                         prev_ref[...])
        dx, db, dc, ddtc, ddtr, dal, dprev = vjp((dy_ref[...], dstate[...]))
        dx_ref[...] = dx
        db_ref[...] = db
        dc_ref[...] = dc
        ddtc_ref[...] = ddtc
        ddtr_ref[...] = ddtr
        dstate[...] = dprev

        @pl.when(c == 0)
        def _():
            dal_ref[...] = dal

        @pl.when(c > 0)
        def _():
            dal_ref[...] += dal

    hs = jax.ShapeDtypeStruct((SSD_HEADS, LP, SSD_HD), F32)
    gs = jax.ShapeDtypeStruct((SSD_GROUPS, LP, SSD_STATE), F32)
    return pl.pallas_call(
        body, name=name, grid=(N_CHUNK,),
        in_specs=[x_spec, g_spec, g_spec, dtc_spec, dtr_spec, al_spec, st_spec, x_spec],
        out_specs=[x_spec, g_spec, g_spec, dtc_spec, dtr_spec, al_spec],
        out_shape=[hs, gs, gs, jax.ShapeDtypeStruct((SSD_HEADS, LP, 1), F32),
                   jax.ShapeDtypeStruct((SSD_HEADS, 1, LP), F32), jax.ShapeDtypeStruct((SSD_HEADS, 1, BLOCK), F32)],
        scratch_shapes=[pltpu.VMEM((SSD_HEADS, SSD_HD, SSD_STATE), F32)],
        compiler_params=_cparams(("arbitrary",)),
    )(x, bm, cm, dtc, dtr, alog, prevs, dy)


def _tri_dot(tri, v):
    hi = v.astype(BF16)
    r1 = v - hi.astype(F32)
    mid = r1.astype(BF16)
    lo = (r1 - mid.astype(F32)).astype(BF16)
    t = tri.astype(BF16)
    d = lambda p: lax.dot_general(t, p, _dims(1, 0), preferred_element_type=F32)
    return d(hi) + d(mid) + d(lo)


def _fox_gate_fwd(raw, raw_blk, bias, *, name):
    def body(raw_ref, b_ref, c_ref, ct_ref, carry):
        j = pl.program_id(0)

        @pl.when(j == 0)
        def _():
            carry[...] = jnp.zeros_like(carry)

        rows = j * BLOCK + lax.broadcasted_iota(jnp.int32, (BLOCK, BLOCK), 0)
        lf = jnp.where(rows >= PAD_ROWS, -_softplus(-(raw_ref[...] + b_ref[...])), 0.0)
        li = lax.broadcasted_iota(jnp.int32, (BLOCK, BLOCK), 0)
        si = lax.broadcasted_iota(jnp.int32, (BLOCK, BLOCK), 1)
        cv = _tri_dot(jnp.where(li >= si, 1.0, 0.0), lf) + carry[...]
        c_ref[...] = cv
        ct_ref[...] = cv.T
        carry[...] += jnp.sum(lf, axis=0, keepdims=True)

    return pl.pallas_call(
        body, name=name, grid=(N_CHUNK,),
        in_specs=[_bs((BLOCK, BLOCK), lambda j: (j, raw_blk)), _bs((1, BLOCK), lambda j: (0, 0))],
        out_specs=[_bs((BLOCK, BLOCK), lambda j: (j, 0)), _bs((BLOCK, BLOCK), lambda j: (0, j))],
        out_shape=[jax.ShapeDtypeStruct((LP, BLOCK), F32), jax.ShapeDtypeStruct((BLOCK, LP), F32)],
        scratch_shapes=[pltpu.VMEM((1, BLOCK), F32)], compiler_params=_cparams(("arbitrary",)),
    )(raw, bias)


def _fox_gate_bwd(raw, raw_blk, bias, dc, dct, *, name):
    rj = lambda j: N_CHUNK - 1 - j

    def body(raw_ref, b_ref, dc_ref, dct_ref, draw_ref, db_ref, carry):
        j = pl.program_id(0)

        @pl.when(j == 0)
        def _():
            carry[...] = jnp.zeros_like(carry)

        rows = (N_CHUNK - 1 - j) * BLOCK + lax.broadcasted_iota(jnp.int32, (BLOCK, BLOCK), 0)
        li = lax.broadcasted_iota(jnp.int32, (BLOCK, BLOCK), 0)
        si = lax.broadcasted_iota(jnp.int32, (BLOCK, BLOCK), 1)
        dcv = dc_ref[...] + dct_ref[...].T
        dlf = _tri_dot(jnp.where(li <= si, 1.0, 0.0), dcv) + carry[...]
        carry[...] += jnp.sum(dcv, axis=0, keepdims=True)
        draw = jnp.where(rows >= PAD_ROWS, dlf * (1.0 - _sigmoid(raw_ref[...] + b_ref[...])), 0.0)
        draw_ref[...] = draw
        dsum = jnp.sum(draw, axis=0, keepdims=True)

        @pl.when(j == 0)
        def _():
            db_ref[...] = dsum

        @pl.when(j > 0)
        def _():
            db_ref[...] += dsum

    return pl.pallas_call(
        body, name=name, grid=(N_CHUNK,),
        in_specs=[_bs((BLOCK, BLOCK), lambda j: (rj(j), raw_blk)), _bs((1, BLOCK), lambda j: (0, 0)),
                  _bs((BLOCK, BLOCK), lambda j: (rj(j), 0)), _bs((BLOCK, BLOCK), lambda j: (0, rj(j)))],
        out_specs=[_bs((BLOCK, BLOCK), lambda j: (rj(j), 0)), _bs((1, BLOCK), lambda j: (0, 0))],
        out_shape=[jax.ShapeDtypeStruct((LP, BLOCK), F32), jax.ShapeDtypeStruct((1, BLOCK), F32)],
        scratch_shapes=[pltpu.VMEM((1, BLOCK), F32)], compiler_params=_cparams(("arbitrary",)),
    )(raw, bias, dc, dct)


ATT_W = 256
ATT_QB = 272
ATT_STEPS = LP // ATT_QB
ATT_KEYS = (640, 1152, 1664, LP)


def _lane_head(width, per, mod=None):
    lane = lax.broadcasted_iota(jnp.int32, (1, width), 1)
    if mod is not None:
        lane = lane % mod
    return lane // per


def _attn_mask(i, kw):
    r = i * ATT_QB + lax.broadcasted_iota(jnp.int32, (ATT_QB, kw), 0)
    c = lax.broadcasted_iota(jnp.int32, (ATT_QB, kw), 1)
    return (c <= r) & ((c >= PAD_ROWS) | (r < PAD_ROWS))


def _attn_by_key_class(i, fn):
    for p, kw in enumerate(ATT_KEYS):
        @pl.when(i // 2 == p)
        def _(kw=kw):
            fn(kw)


def _attn_specs(q, k, v, bias, rope):
    qspec = lambda blk, w=ATT_W: _bs((ATT_QB, w), lambda i: (i, blk))
    fspec = lambda blk, w=ATT_W: _bs((LP, w), lambda i: (0, blk))
    ins = [q[0], k[0], v[0]]
    specs = [qspec(q[1]), fspec(k[1]), fspec(v[1])]
    if bias is not None:
        ins += [bias[0], bias[1]]
        specs += [qspec(0, BLOCK), _bs((BLOCK, LP), lambda i: (0, 0))]
    if rope is not None:
        ins += [rope[0][0], rope[1][0]]
        specs += [qspec(rope[0][1], BLOCK), fspec(rope[1][1], BLOCK)]
    return ins, specs, qspec, fspec


def _attn_fwd(q, k, v, *, scale, name, bias=None, rope=None):
    ins, specs, qspec, fspec = _attn_specs(q, k, v, bias, rope)
    has_bias, has_rope = bias is not None, rope is not None

    def body(*refs):
        it = iter(refs)
        q_ref, k_ref, v_ref = next(it), next(it), next(it)
        if has_bias:
            c_ref, ct_ref = next(it), next(it)
        if has_rope:
            qr_ref, kr_ref = next(it), next(it)
        o_ref, lse_ref = next(it), next(it)
        i = pl.program_id(0)

        def block(kw):
            ok = _attn_mask(i, kw)
            qv, kv, vv = q_ref[...], k_ref[0:kw, :], v_ref[0:kw, :]
            hid, l128 = _lane_head(ATT_W, FOX_HD), _lane_head(BLOCK, 1)
            if has_rope:
                rid = _lane_head(BLOCK, ROPE_HALF, 64)
                qrv, krv = qr_ref[...], kr_ref[0:kw, :]
            o_acc = jnp.zeros((ATT_QB, ATT_W), F32)
            lse_acc = jnp.zeros((ATT_QB, BLOCK), F32)
            for h in range(FOX_HEADS):
                s = _raw_bdot(jnp.where(hid == h, qv, 0.0), kv, 1, 1)
                if has_rope:
                    s = s + _raw_bdot(jnp.where(rid == h, qrv, 0.0), krv, 1, 1)
                s = s * scale
                if has_bias:
                    cq = jnp.sum(jnp.where(l128 == h, c_ref[...], 0.0), axis=1, keepdims=True)
                    s = s + (cq - ct_ref[h:h + 1, 0:kw])
                s = jnp.where(ok, s, NEG)
                m = jnp.max(s, axis=1, keepdims=True)
                p = jnp.exp(s - m)
                l = jnp.sum(p, axis=1, keepdims=True)
                o_acc = jnp.where(hid == h, _raw_bdot(p, vv, 1, 0) / l, o_acc)
                lse_acc = jnp.where(l128 == h, m + jnp.log(l), lse_acc)
            o_ref[...] = o_acc
            lse_ref[...] = lse_acc

        _attn_by_key_class(i, block)

    return pl.pallas_call(
        body, name=name, grid=(ATT_STEPS,), in_specs=specs, out_specs=[qspec(0), qspec(0, BLOCK)],
        out_shape=[jax.ShapeDtypeStruct((LP, ATT_W), F32), jax.ShapeDtypeStruct((LP, BLOCK), F32)],
        compiler_params=_cparams(("parallel",)),
    )(*ins)


def _attn_bwd(q, k, v, o, lse, do, *, scale, name, bias=None, rope=None):
    ins, specs, qspec, fspec = _attn_specs(q, k, v, bias, rope)
    has_bias, has_rope = bias is not None, rope is not None
    ins += [o, lse, do[0]]
    specs += [qspec(0), qspec(0, BLOCK), qspec(do[1])]

    def body(*refs):
        it = iter(refs)
        q_ref, k_ref, v_ref = next(it), next(it), next(it)
        if has_bias:
            c_ref, ct_ref = next(it), next(it)
        if has_rope:
            qr_ref, kr_ref = next(it), next(it)
        o_ref, lse_ref, do_ref = next(it), next(it), next(it)
        dq_ref, dk_ref, dv_ref = next(it), next(it), next(it)
        if has_bias:
            dc_ref, dct_ref = next(it), next(it)
        if has_rope:
            dqr_ref, dkr_ref = next(it), next(it)
        i = pl.program_id(0)

        @pl.when(i == 0)
        def _():
            dk_ref[...] = jnp.zeros_like(dk_ref)
            dv_ref[...] = jnp.zeros_like(dv_ref)
            if has_rope:
                dkr_ref[...] = jnp.zeros_like(dkr_ref)
            if has_bias:
                dct_ref[...] = jnp.zeros_like(dct_ref)

        def block(kw):
            ok = _attn_mask(i, kw)
            qv, kv, vv = q_ref[...], k_ref[0:kw, :], v_ref[0:kw, :]
            ov, dov, lsev = o_ref[...], do_ref[...], lse_ref[...]
            hid, l128 = _lane_head(ATT_W, FOX_HD), _lane_head(BLOCK, 1)
            if has_rope:
                rid = _lane_head(BLOCK, ROPE_HALF, 64)
                qrv, krv = qr_ref[...], kr_ref[0:kw, :]
                dqr_acc = jnp.zeros((ATT_QB, BLOCK), F32)
                dkr_acc = jnp.zeros((kw, BLOCK), F32)
            dq_acc = jnp.zeros((ATT_QB, ATT_W), F32)
            dk_acc = jnp.zeros((kw, ATT_W), F32)
            dv_acc = jnp.zeros((kw, ATT_W), F32)
            dc_acc = jnp.zeros((ATT_QB, BLOCK), F32)
            for h in range(FOX_HEADS):
                qm = jnp.where(hid == h, qv, 0.0)
                s = _raw_bdot(qm, kv, 1, 1)
                if has_rope:
                    qrm = jnp.where(rid == h, qrv, 0.0)
                    s = s + _raw_bdot(qrm, krv, 1, 1)
                s = s * scale
                if has_bias:
                    cq = jnp.sum(jnp.where(l128 == h, c_ref[...], 0.0), axis=1, keepdims=True)
                    s = s + (cq - ct_ref[h:h + 1, 0:kw])
                s = jnp.where(ok, s, NEG)
                p = jnp.exp(s - jnp.sum(jnp.where(l128 == h, lsev, 0.0), axis=1, keepdims=True))
                dom = jnp.where(hid == h, dov, 0.0)
                dp = _raw_bdot(dom, vv, 1, 1)
                delta = jnp.sum(dom * ov, axis=1, keepdims=True)
                ds = p * (dp - delta)
                dq_acc = jnp.where(hid == h, _raw_bdot(ds, kv, 1, 0) * scale, dq_acc)
                dk_acc = dk_acc + _raw_bdot(ds, qm, 0, 0) * scale
                dv_acc = dv_acc + _raw_bdot(p, dom, 0, 0)
                if has_rope:
                    dqr_acc = jnp.where(rid == h, _raw_bdot(ds, krv, 1, 0) * scale, dqr_acc)
                    dkr_acc = dkr_acc + _raw_bdot(ds, qrm, 0, 0) * scale
                if has_bias:
                    dc_acc = jnp.where(l128 == h, jnp.sum(ds, axis=1, keepdims=True), dc_acc)
                    dct_ref[h:h + 1, 0:kw] -= jnp.sum(ds, axis=0, keepdims=True)
            dq_ref[...] = dq_acc
            dk_ref[0:kw, :] += dk_acc
            dv_ref[0:kw, :] += dv_acc
            if has_bias:
                dc_ref[...] = dc_acc
            if has_rope:
                dqr_ref[...] = dqr_acc
                dkr_ref[0:kw, :] += dkr_acc

        _attn_by_key_class(i, block)

    wide = jax.ShapeDtypeStruct((LP, ATT_W), F32)
    narrow = jax.ShapeDtypeStruct((LP, BLOCK), F32)
    out_specs = [qspec(0), fspec(0), fspec(0)]
    out_shape = [wide, wide, wide]
    if has_bias:
        out_specs += [qspec(0, BLOCK), _bs((BLOCK, LP), lambda i: (0, 0))]
        out_shape += [narrow, jax.ShapeDtypeStruct((BLOCK, LP), F32)]
    if has_rope:
        out_specs += [qspec(0, BLOCK), fspec(0, BLOCK)]
        out_shape += [narrow, narrow]
    return pl.pallas_call(
        body, name=name, grid=(ATT_STEPS,), in_specs=specs, out_specs=out_specs, out_shape=out_shape,
        compiler_params=_cparams(("arbitrary",)),
    )(*ins)


def _loss_head(y, target, *, name):
    tile = 272

    def body(y_ref, t_ref, dy_ref, loss_ref):
        i = pl.program_id(0)
        rows = i * tile + lax.broadcasted_iota(jnp.int32, (tile, D_MODEL), 0)
        err = jnp.where(rows >= BLOCK, y_ref[...] - t_ref[...], 0.0)
        dy_ref[...] = err * (1.0 / D_MODEL)
        part = 0.5 * jnp.sum(jnp.sum(err * err, axis=1, keepdims=True) * (1.0 / D_MODEL), axis=0, keepdims=True)
        part = jnp.broadcast_to(part, (1, BLOCK))

        @pl.when(i == 0)
        def _():
            loss_ref[...] = part

        @pl.when(i > 0)
        def _():
            loss_ref[...] += part

    return pl.pallas_call(
        body, name=name, grid=(LP // tile,),
        in_specs=[_bs((tile, D_MODEL), lambda i: (i, 0)), _bs((tile, D_MODEL), lambda i: (i, 0))],
        out_specs=[_bs((tile, D_MODEL), lambda i: (i, 0)), _bs((1, BLOCK), lambda i: (0, 0))],
        out_shape=[jax.ShapeDtypeStruct((LP, D_MODEL), F32), jax.ShapeDtypeStruct((1, BLOCK), F32)],
        compiler_params=_cparams(("arbitrary",)),
    )(y, target)


def _adamw(w, g, m, v, *, name):
    if w.ndim == 2:
        w, g, m, v = w[None], g[None], m[None], v[None]
        squeeze = True
    else:
        squeeze = False
    NL, R, C = w.shape
    CG = g.shape[2]
    tile = _tile(R, 256, 8)

    def body(w_ref, g_ref, m_ref, v_ref, go_ref, d_ref, nm_ref, nv_ref):
        gv = g_ref[:, :C]
        nm = ADAM_B1 * m_ref[...] + (1.0 - ADAM_B1) * gv
        nv = ADAM_B2 * v_ref[...] + (1.0 - ADAM_B2) * (gv * gv)
        m_hat = nm / (1.0 - ADAM_B1 ** ADAM_STEP)
        v_hat = nv / (1.0 - ADAM_B2 ** ADAM_STEP)
        go_ref[...] = gv
        d_ref[...] = -ADAM_LR * (m_hat / (jnp.sqrt(v_hat) + ADAM_EPS) + ADAM_WD * w_ref[...])
        nm_ref[...] = nm
        nv_ref[...] = nv

    spec = _bs((None, tile, C), lambda l, i: (l, i, 0))
    gspec = _bs((None, tile, CG), lambda l, i: (l, i, 0))
    res = pl.pallas_call(
        body, name=name, grid=(NL, R // tile), in_specs=[spec, gspec, spec, spec], out_specs=[spec] * 4,
        out_shape=[jax.ShapeDtypeStruct((NL, R, C), F32)] * 4, compiler_params=_cparams(("parallel", "parallel")),
    )(w, g, m, v)
    return [r[0] for r in res] if squeeze else res


def _my_pos():
    return lax.axis_index("x"), lax.axis_index("y"), lax.axis_index("c")


def _other_chips(x, y):
    return [(1 - x, y), (x, 1 - y), (1 - x, 1 - y)]


def _allgather_chips(shards):
    n = len(shards)
    per = 7

    def body(*refs):
        ins, outs = refs[:n], refs[n:2 * n]
        send_sems, recv_sems = refs[2 * n], refs[2 * n + 1]
        x, y, c = _my_pos()
        chips = _other_chips(x, y)
        sibling, me = (x, y, 1 - c), 2 * x + y

        def cp(a, kk, src, dst, to):
            return pltpu.make_async_remote_copy(src_ref=src, dst_ref=dst, send_sem=send_sems.at[per * a + kk],
                                                recv_sem=recv_sems.at[per * a + kk], device_id=to, device_id_type=MESH)

        sends = []
        for a in range(n):
            for j, chip in enumerate(chips):
                sends.append(cp(a, j, ins[a].at[c], outs[a].at[me, c], (*chip, c)))
            sends.append(cp(a, 3, ins[a], outs[a].at[me], sibling))
        for s in sends:
            s.start()
        for a in range(n):
            for j, chip in enumerate(chips):
                slab = outs[a].at[2 * chip[0] + chip[1], c]
                cp(a, j, slab, slab, (x, y, c)).wait_recv()
                fwd = cp(a, 4 + j, slab, slab, sibling)
                fwd.start()
                sends.append(fwd)
        for a in range(n):
            cp(a, 3, ins[a], outs[a].at[me], (x, y, c)).wait_recv()
            for j, chip in enumerate(chips):
                slab = outs[a].at[2 * chip[0] + chip[1], 1 - c]
                cp(a, 4 + j, slab, slab, (x, y, c)).wait_recv()
        for s in sends:
            s.wait_send()

    return pl.pallas_call(
        body, name="allgather_chips", in_specs=[ANY] * n, out_specs=[ANY] * n,
        out_shape=[jax.ShapeDtypeStruct((N_CHIPS,) + s.shape, s.dtype) for s in shards],
        scratch_shapes=[pltpu.SemaphoreType.DMA((per * n,)), pltpu.SemaphoreType.DMA((per * n,))],
    )(*shards)


def _rs_swap_layers(gs):
    n = len(gs)

    def body(*refs):
        ins, outs = refs[:n], refs[n:2 * n]
        send_sems, recv_sems = refs[2 * n], refs[2 * n + 1]
        x, y, c = _my_pos()
        cps = [pltpu.make_async_remote_copy(
            src_ref=ins[a].at[:, 1 - c], dst_ref=outs[a], send_sem=send_sems.at[a], recv_sem=recv_sems.at[a],
            device_id=(x, y, 1 - c), device_id_type=MESH) for a in range(n)]
        for cp in cps:
            cp.start()
        for cp in cps:
            cp.wait()

    return pl.pallas_call(
        body, name="rs_swap_layers", in_specs=[ANY] * n, out_specs=[ANY] * n,
        out_shape=[jax.ShapeDtypeStruct((N_CHIPS,) + g.shape[2:], g.dtype) for g in gs],
        scratch_shapes=[pltpu.SemaphoreType.DMA((n,)), pltpu.SemaphoreType.DMA((n,))],
    )(*gs)


def _rs_add_pair(g, r, pos, *, name):
    _, _, R, C = g.shape
    tile = _tile(R, 512, 16)

    def body(pos_ref, g_ref, r_ref, o32_ref, o16_ref):
        s = g_ref[...] + r_ref[...]
        o32_ref[...] = s
        o16_ref[...] = s.astype(BF16)

    spec = _bs((None, tile, C), lambda k, i, pos_ref: (k, i, 0))
    grid_spec = pltpu.PrefetchScalarGridSpec(
        num_scalar_prefetch=1, grid=(N_CHIPS, R // tile),
        in_specs=[_bs((None, None, tile, C), lambda k, i, pos_ref: (k, pos_ref[1], i, 0)), spec],
        out_specs=[spec, spec])
    return pl.pallas_call(
        body, name=name, grid_spec=grid_spec,
        out_shape=[jax.ShapeDtypeStruct((N_CHIPS, R, C), F32), jax.ShapeDtypeStruct((N_CHIPS, R, C), BF16)],
        compiler_params=_cparams(("parallel", "parallel")),
    )(pos, g, r)


def _rs_exchange_chips(ps):
    n = len(ps)

    def body(*refs):
        ins, outs = refs[:n], refs[n:2 * n]
        send_sems, recv_sems = refs[2 * n], refs[2 * n + 1]
        x, y, c = _my_pos()
        cps = []
        for a in range(n):
            for j, chip in enumerate(_other_chips(x, y)):
                cps.append(pltpu.make_async_remote_copy(
                    src_ref=ins[a].at[2 * chip[0] + chip[1]], dst_ref=outs[a].at[j], send_sem=send_sems.at[3 * a + j],
                    recv_sem=recv_sems.at[3 * a + j], device_id=(*chip, c), device_id_type=MESH))
        for cp in cps:
            cp.start()
        for cp in cps:
            cp.wait()

    return pl.pallas_call(
        body, name="rs_exchange_chips", in_specs=[ANY] * n, out_specs=[ANY] * n,
        out_shape=[jax.ShapeDtypeStruct((3,) + p.shape[1:], p.dtype) for p in ps],
        scratch_shapes=[pltpu.SemaphoreType.DMA((3 * n,)), pltpu.SemaphoreType.DMA((3 * n,))],
    )(*ps)


def _rs_add_chips(p32, r16, pos, *, name):
    _, R, C = p32.shape
    tile = _tile(R, 512, 16)

    def body(pos_ref, p_ref, r_ref, o_ref):
        o_ref[...] = ((p_ref[...] + r_ref[0].astype(F32)) + r_ref[1].astype(F32)) + r_ref[2].astype(F32)

    grid_spec = pltpu.PrefetchScalarGridSpec(
        num_scalar_prefetch=1, grid=(R // tile,),
        in_specs=[_bs((None, tile, C), lambda i, pos_ref: (pos_ref[0], i, 0)),
                  _bs((3, tile, C), lambda i, pos_ref: (0, i, 0))],
        out_specs=_bs((None, tile, C), lambda i, pos_ref: (pos_ref[1], i, 0)))
    return pl.pallas_call(
        body, name=name, grid_spec=grid_spec, out_shape=jax.ShapeDtypeStruct((2, R, C), F32),
        compiler_params=_cparams(("parallel",)),
    )(pos, p32, r16)


def _rs_join_layers(fs):
    n = len(fs)

    def body(*refs):
        outs = refs[n:2 * n]
        send_sems, recv_sems = refs[2 * n], refs[2 * n + 1]
        x, y, c = _my_pos()
        for a in range(n):
            pltpu.make_async_remote_copy(src_ref=outs[a].at[c], dst_ref=outs[a].at[c], send_sem=send_sems.at[a],
                                         recv_sem=recv_sems.at[a], device_id=(x, y, 1 - c), device_id_type=MESH).start()
        for a in range(n):
            pltpu.make_async_remote_copy(src_ref=outs[a].at[c], dst_ref=outs[a].at[1 - c], send_sem=send_sems.at[a],
                                         recv_sem=recv_sems.at[a], device_id=(x, y, 1 - c), device_id_type=MESH).wait()

    return pl.pallas_call(
        body, name="rs_join_layers", in_specs=[ANY] * n, out_specs=[ANY] * n,
        out_shape=[jax.ShapeDtypeStruct(f.shape, f.dtype) for f in fs],
        input_output_aliases={a: a for a in range(n)},
        scratch_shapes=[pltpu.SemaphoreType.DMA((n,)), pltpu.SemaphoreType.DMA((n,))],
    )(*fs)


def _reduce_scatter(gs, names):
    x, y, c = _my_pos()
    pos = jnp.stack([2 * x + y, c]).astype(jnp.int32)
    r1 = _rs_swap_layers(gs)
    pairs = [_rs_add_pair(g, r, pos, name=f"rs_add_pair_{nm}") for g, r, nm in zip(gs, r1, names)]
    r2 = _rs_exchange_chips([p[1] for p in pairs])
    fs = [_rs_add_chips(p[0], r, pos, name=f"rs_add_chips_{nm}") for p, r, nm in zip(pairs, r2, names)]
    return _rs_join_layers(fs)


def _allreduce_small(buf):
    R, W = buf.shape

    def body(b_ref, o_ref, gather, send_sems, recv_sems):
        x, y, c = _my_pos()
        me = 4 * x + 2 * y + c
        gather[me] = b_ref[...]
        cps = []
        for d in range(1, 8):
            peer = (x ^ (d >> 2), y ^ ((d >> 1) & 1), c ^ (d & 1))
            cps.append(pltpu.make_async_remote_copy(
                src_ref=b_ref, dst_ref=gather.at[me], send_sem=send_sems.at[d - 1], recv_sem=recv_sems.at[d - 1],
                device_id=peer, device_id_type=MESH))
        for cp in cps:
            cp.start()
        for d in range(1, 8):
            pltpu.make_async_remote_copy(
                src_ref=b_ref, dst_ref=gather.at[me ^ d], send_sem=send_sems.at[d - 1], recv_sem=recv_sems.at[d - 1],
                device_id=(x, y, c), device_id_type=MESH).wait_recv()
        for cp in cps:
            cp.wait_send()
        acc = gather[0]
        for d in range(1, 8):
            acc = acc + gather[d]
        o_ref[...] = acc

    vm = pl.BlockSpec(memory_space=pltpu.VMEM)
    return pl.pallas_call(
        body, name="allreduce_small", in_specs=[vm], out_specs=vm, out_shape=jax.ShapeDtypeStruct((R, W), F32),
        scratch_shapes=[pltpu.VMEM((8, R, W), F32), pltpu.SemaphoreType.DMA((7,)), pltpu.SemaphoreType.DMA((7,))],
    )(buf)


def _heads(a, h, d):
    return a.reshape(a.shape[0], h, d).transpose(1, 0, 2)


def _unheads(a):
    h, L, d = a.shape
    return a.transpose(1, 0, 2).reshape(L, h * d)


def _rope_tables():
    pos = jnp.maximum(jnp.arange(LP, dtype=F32) - PAD_ROWS, 0.0)
    inv_freq = 1.0 / (ROPE_THETA ** (jnp.arange(0, MLA_ROPE, 2, dtype=F32) / MLA_ROPE))
    ang = pos[:, None] * inv_freq[None, :]
    cos, sin = jnp.tile(jnp.cos(ang), (1, MLA_HEADS)), jnp.tile(jnp.sin(ang), (1, MLA_HEADS))
    return jnp.concatenate([cos, cos], axis=1), jnp.concatenate([-sin, sin], axis=1)


def _lane_pad(a, width=BLOCK):
    return jnp.pad(a, ((0, 0), (0, width - a.shape[1])))


def _pad_in_proj(w):
    sl = lambda start, size: w[:, start:start + size]
    return jnp.concatenate([
        sl(OC_Z, 512), sl(OC_XBC, 768), sl(OC_FQ, 256), sl(OC_FK, 256), sl(OC_FV, 256), sl(OC_CQ, 256), sl(OC_CKV, 128),
        _lane_pad(sl(OC_DT, SSD_HEADS)), _lane_pad(sl(OC_FR, FOX_HEADS)),
        jnp.tile(sl(OC_KR, ROPE_HALF), (1, MLA_HEADS)), jnp.tile(sl(OC_KR + ROPE_HALF, ROPE_HALF), (1, MLA_HEADS))], axis=1)


def _unpad_in_proj(wp):
    sl = lambda start, size: wp[:, start:start + size]
    rope = lambda start: sl(start, 64).reshape(wp.shape[0], MLA_HEADS, ROPE_HALF).sum(axis=1)
    return jnp.concatenate([
        sl(PC_Z, 512), sl(PC_XBC, 768), sl(PC_DT, SSD_HEADS), sl(PC_FQ, 256), sl(PC_FK, 256), sl(PC_FV, 256),
        sl(PC_FR, FOX_HEADS), sl(PC_CQ, 256), sl(PC_CKV, 128), rope(PC_KR), rope(PC_KR + 64)], axis=1)


def _regroup_uq(w):
    w3 = w.reshape(w.shape[0], MLA_HEADS, MLA_NOPE + MLA_ROPE)
    return jnp.concatenate([w3[:, :, :MLA_NOPE].reshape(w.shape[0], -1),
                            w3[:, :, MLA_NOPE:MLA_NOPE + ROPE_HALF].reshape(w.shape[0], -1),
                            w3[:, :, MLA_NOPE + ROPE_HALF:].reshape(w.shape[0], -1)], axis=1)


def _ungroup_uq(wp):
    n = wp.shape[0]
    return jnp.concatenate([wp[:, :256].reshape(n, MLA_HEADS, MLA_NOPE), wp[:, 256:320].reshape(n, MLA_HEADS, ROPE_HALF),
                            wp[:, 320:].reshape(n, MLA_HEADS, ROPE_HALF)], axis=2).reshape(n, -1)


def _regroup_ukv(w):
    w3 = w.reshape(w.shape[0], MLA_HEADS, MLA_NOPE + MLA_V)
    return jnp.concatenate([w3[:, :, :MLA_NOPE].reshape(w.shape[0], -1), w3[:, :, MLA_NOPE:].reshape(w.shape[0], -1)],
                           axis=1)


def _ungroup_ukv(wp):
    n = wp.shape[0]
    return jnp.concatenate([wp[:, :256].reshape(n, MLA_HEADS, MLA_NOPE), wp[:, 256:].reshape(n, MLA_HEADS, MLA_V)],
                           axis=2).reshape(n, -1)


TMF = 1088
N_IF = LP // TMF


def _chunk_cols_mm(a, w, l, chunk_w, *, name):
    K = a.shape[1]
    return _mm_core(a, w, a_spec=_bs((TMF, K), lambda i, j, k: (i, 0)),
                    b_spec=_bs((None, None, K, chunk_w), lambda i, j, k: (j, l, 0, 0)),
                    o_spec=_bs((TMF, chunk_w), lambda i, j, k: (i, j)), grid=(N_IF, N_CHIPS, 1),
                    out_shape=(LP, N_CHIPS * chunk_w), ca=1, cb=0, name=name)


def _chunk_cols_dx(g, w, l, chunk_w, add, *, name):
    K = w.shape[2]
    return _mm_core(g, w, a_spec=_bs((TMF, chunk_w), lambda i, j, k: (i, k)),
                    b_spec=_bs((None, None, K, chunk_w), lambda i, j, k: (k, l, 0, 0)),
                    o_spec=_bs((TMF, K), lambda i, j, k: (i, 0)), grid=(N_IF, 1, N_CHIPS),
                    out_shape=(LP, K), ca=1, cb=1, name=name, add=add)


def _chunk_cols_dw(a, g, l, chunk_w, into, *, name):
    K = a.shape[1]
    return _mm_core(a, g, a_spec=_bs((LP, K), lambda i, j, k: (0, 0)), b_spec=_bs((LP, chunk_w), lambda i, j, k: (0, j)),
                    o_spec=_bs((None, None, K, chunk_w), lambda i, j, k: (j, l, 0, 0)), grid=(1, N_CHIPS, 1),
                    out_shape=(N_CHIPS, DEPTH, K, chunk_w), ca=0, cb=0, name=name, into=into)


def _chunk_rows_mm(a, w, l, chunk_h, *, name):
    N = w.shape[3]
    return _mm_core(a, w, a_spec=_bs((TMF, chunk_h), lambda i, j, k: (i, k)),
                    b_spec=_bs((None, None, chunk_h, N), lambda i, j, k: (k, l, 0, 0)),
                    o_spec=_bs((TMF, N), lambda i, j, k: (i, 0)), grid=(N_IF, 1, N_CHIPS),
                    out_shape=(LP, N), ca=1, cb=0, name=name)


def _chunk_rows_dx(g, w, l, chunk_h, *, name):
    N = w.shape[3]
    return _mm_core(g, w, a_spec=_bs((TMF, N), lambda i, j, k: (i, 0)),
                    b_spec=_bs((None, None, chunk_h, N), lambda i, j, k: (j, l, 0, 0)),
                    o_spec=_bs((TMF, chunk_h), lambda i, j, k: (i, j)), grid=(N_IF, N_CHIPS, 1),
                    out_shape=(LP, N_CHIPS * chunk_h), ca=1, cb=1, name=name)


def _chunk_rows_dw(a, g, l, chunk_h, into, *, name):
    N = g.shape[1]
    return _mm_core(a, g, a_spec=_bs((LP, chunk_h), lambda i, j, k: (0, i)), b_spec=_bs((LP, N), lambda i, j, k: (0, 0)),
                    o_spec=_bs((None, None, chunk_h, N), lambda i, j, k: (i, l, 0, 0)), grid=(N_CHIPS, 1, 1),
                    out_shape=(N_CHIPS, DEPTH, chunk_h, N), ca=0, cb=0, name=name, into=into)


def _ffn_fwd(h, W, pre, l, gam, bet, tag):
    g = _chunk_cols_mm(h, W[pre + "_w_gate"], l, HP, name=f"{tag}_gate")
    u = _chunk_cols_mm(h, W[pre + "_w_up"], l, HP, name=f"{tag}_up")
    (act,) = _rowwise(_swiglu_fn, [g, u], [], [FP], name=f"{tag}_swiglu", tile=TM, ncol=N_CHIPS, out_dtypes=[BF16])
    o = _chunk_rows_mm(act, W[pre + "_w_down"], l, HP, name=f"{tag}_down")
    (out,) = _rowwise(_make_res_ln_fn(0.5), [h, o], [gam, bet], [D_MODEL], name=f"{tag}_ln", tile=272)
    return out, (h, g, u, act, o)


def _ffn_bwd(dout, saved, W, pre, l, gam, bet, GB, tag):
    h, g, u, act, o = saved
    (dh_a, do), (dgam, dbet) = _rowwise_bwd(_make_res_ln_fn(0.5), [h, o], [gam, bet], [dout], name=f"{tag}_ln_bwd",
                                            tile=272, grad_dtypes=[F32, BF16])
    dact = _chunk_rows_dx(do, W[pre + "_w_down"], l, HP, name=f"{tag}_down_dx")
    GB[pre + "_w_down"] = _chunk_rows_dw(act, do, l, HP, GB.get(pre + "_w_down"), name=f"{tag}_down_dw")
    (dg, du), _ = _rowwise_bwd(_swiglu_fn, [g, u], [], [dact], name=f"{tag}_swiglu_bwd", tile=TM, ncol=N_CHIPS,
                               grad_dtypes=[BF16, BF16])
    GB[pre + "_w_gate"] = _chunk_cols_dw(h, dg, l, HP, GB.get(pre + "_w_gate"), name=f"{tag}_gate_dw")
    GB[pre + "_w_up"] = _chunk_cols_dw(h, du, l, HP, GB.get(pre + "_w_up"), name=f"{tag}_up_dw")
    dh = _chunk_cols_dx(dg, W[pre + "_w_gate"], l, HP, dh_a, name=f"{tag}_gate_dx")
    dh = _chunk_cols_dx(du, W[pre + "_w_up"], l, HP, dh, name=f"{tag}_up_dx")
    return dh, dgam, dbet


def _mixer_fwd(h1, W, l, cosf, sins):
    tag = f"l{l}"
    proj = _mm(h1, W["w_in_p"][l], name=f"{tag}_in_proj")
    sv = {"h1": h1, "proj": proj}
    conv_w, conv_b = W["conv_w"][l], W["conv_b"][l][None]
    xc = _conv_fwd(proj, PC_XBC // BLOCK, conv_w, conv_b, name=f"{tag}_conv")
    dt_bias = _lane_pad(W["dt_bias"][l][None])
    (dt,) = _rowwise(_ssd_pre_fn, [(proj, BLOCK, PC_DT // BLOCK)], [dt_bias], [BLOCK], name=f"{tag}_ssd_dt", tile=272)
    xh = _heads(xc[:, :SSD_D], SSD_HEADS, SSD_HD)
    bm = _heads(xc[:, SSD_D:SSD_D + 128], SSD_GROUPS, SSD_STATE)
    cm = _heads(xc[:, SSD_D + 128:], SSD_GROUPS, SSD_STATE)
    dt8 = dt[:, :SSD_HEADS].T
    dtc, dtr = dt8[:, :, None], dt8[:, None, :]
    alog = jnp.broadcast_to(W["a_log"][l][:, None, None], (SSD_HEADS, 1, BLOCK))
    yh, prevs = _ssd_fwd(xh, bm, cm, dtc, dtr, alog, name=f"{tag}_ssd")
    y_raw = _unheads(yh)
    dskip = jnp.repeat(W["d_skip"][l], SSD_HD)[None]
    normg = W["ssd_norm_g"][l][None]
    post_rows = [y_raw, (xc, 256, 0), (proj, 256, PC_Z // 256)]
    (y_ssd,) = _rowwise(_ssd_post_fn, post_rows, [dskip, normg], [SSD_D], name=f"{tag}_ssd_post", tile=272,
                        ncol=SSD_GROUPS)
    sv.update(conv_w=conv_w, conv_b=conv_b, dt_bias=dt_bias, xh=xh, bm=bm, cm=cm, dtc=dtc, dtr=dtr, alog=alog,
              prevs=prevs, post_rows=post_rows, dskip=dskip, normg=normg)
    f_b = _lane_pad(W["fox_f_b"][l][None])
    cg, cgt = _fox_gate_fwd(proj, PC_FR // BLOCK, f_b, name=f"{tag}_fox_gate")
    fox_qkv = ((proj, PC_FQ // ATT_W), (proj, PC_FK // ATT_W), (proj, PC_FV // ATT_W))
    y_fox, lse_f = _attn_fwd(*fox_qkv, scale=FOX_HD ** -0.5, name=f"{tag}_fox_attn", bias=(cg, cgt))
    sv.update(f_b=f_b, cg=cg, cgt=cgt, fox_qkv=fox_qkv, y_fox=y_fox, lse_f=lse_f)
    gq, gkv = W["mla_q_norm_g"][l][None], W["mla_kv_norm_g"][l][None]
    norm_rows = [(proj, 256, PC_CQ // 256), (proj, BLOCK, PC_CKV // BLOCK)]
    qn, cn = _rowwise(_mla_norm_fn, norm_rows, [gq, gkv], [MLA_Q_LORA, MLA_KV_LORA], name=f"{tag}_mla_norm", tile=272,
                      out_dtypes=[BF16, BF16])
    qh = _mm(qn, W["mla_w_uq_p"][l], name=f"{tag}_mla_uq")
    kvh = _mm(cn, W["mla_w_ukv_p"][l], name=f"{tag}_mla_ukv")
    qr, kr = _rowwise(_rope_fn, [(qh, BLOCK, 2), (proj, BLOCK, PC_KR // BLOCK), cosf, sins], [], [BLOCK, BLOCK],
                      name=f"{tag}_rope", tile=272)
    mla_qkv = ((qh, 0), (kvh, 0), (kvh, 1))
    y_mla, lse_m = _attn_fwd(*mla_qkv, scale=(MLA_NOPE + MLA_ROPE) ** -0.5, name=f"{tag}_mla_attn",
                             rope=((qr, 0), (kr, 0)))
    sv.update(gq=gq, gkv=gkv, norm_rows=norm_rows, qn=qn, cn=cn, qr=qr, kr=kr, mla_qkv=mla_qkv, y_mla=y_mla, lse_m=lse_m)
    ycat = jnp.concatenate([y_ssd, y_fox, y_mla], axis=1).astype(BF16)
    mix = _chunk_rows_mm(ycat, W["w_out"], l, 256, name=f"{tag}_out_proj")
    (h2,) = _rowwise(_make_res_ln_fn(1.0), [h1, mix], [W["ln2_g"][l][None], W["ln2_b"][l][None]], [D_MODEL],
                     name=f"{tag}_ln2", tile=272)
    sv.update(mix=mix, ycat=ycat)
    return h2, sv


def _mixer_bwd(dh2, sv, W, l, cosf, sins, GB):
    tag = f"l{l}"
    G = {}
    proj = sv["proj"]
    ln2g, ln2b = W["ln2_g"][l][None], W["ln2_b"][l][None]
    (dh1_a, dmix), (dln2g, dln2b) = _rowwise_bwd(
        _make_res_ln_fn(1.0), [sv["h1"], sv["mix"]], [ln2g, ln2b], [dh2], name=f"{tag}_ln2_bwd", tile=272,
        grad_dtypes=[F32, BF16])
    G["ln2_g"], G["ln2_b"] = dln2g[0], dln2b[0]
    dycat = _chunk_rows_dx(dmix, W["w_out"], l, 256, name=f"{tag}_out_proj_dx")
    GB["w_out"] = _chunk_rows_dw(sv["ycat"], dmix, l, 256, GB.get("w_out"), name=f"{tag}_out_proj_dw")
    (dy_raw, dxs_a, dz), (ddskip, dnormg) = _rowwise_bwd(
        _ssd_post_fn, sv["post_rows"], [sv["dskip"], sv["normg"]], [dycat[:, :SSD_D]],
        name=f"{tag}_ssd_post_bwd", tile=272, ncol=SSD_GROUPS)
    G["ssd_norm_g"] = dnormg[0]
    G["d_skip"] = ddskip.reshape(SSD_HEADS, SSD_HD).sum(axis=1)
    dxh, dbm, dcm, ddtc, ddtr, dal = _ssd_bwd(sv["xh"], sv["bm"], sv["cm"], sv["dtc"], sv["dtr"], sv["alog"],
                                              sv["prevs"], _heads(dy_raw, SSD_HEADS, SSD_HD), name=f"{tag}_ssd_bwd")
    G["a_log"] = dal[:, 0, 0]
    dxc = jnp.concatenate([dxs_a + _unheads(dxh), _unheads(dbm), _unheads(dcm)], axis=1)
    dxbc, G["conv_w"], dconv_b = _conv_bwd(proj, PC_XBC // BLOCK, sv["conv_w"], sv["conv_b"], dxc,
                                           name=f"{tag}_conv_bwd")
    G["conv_b"] = dconv_b[0]
    ddt = _lane_pad((ddtc[:, :, 0] + ddtr[:, 0, :]).T)
    (ddt_raw,), (ddt_bias,) = _rowwise_bwd(_ssd_pre_fn, [(proj, BLOCK, PC_DT // BLOCK)], [sv["dt_bias"]], [ddt],
                                           name=f"{tag}_ssd_dt_bwd", tile=272)
    G["dt_bias"] = ddt_bias[0, :SSD_HEADS]
    dfq, dfk, dfv, dcg, dcgt = _attn_bwd(*sv["fox_qkv"], sv["y_fox"], sv["lse_f"], (dycat, SSD_D // ATT_W),
                                         scale=FOX_HD ** -0.5, name=f"{tag}_fox_attn_bwd", bias=(sv["cg"], sv["cgt"]))
    df_raw, dfb = _fox_gate_bwd(proj, PC_FR // BLOCK, sv["f_b"], dcg, dcgt, name=f"{tag}_fox_gate_bwd")
    G["fox_f_b"] = dfb[0, :FOX_HEADS]
    dqn_h, dkn_h, dv_h, dqr, dkr = _attn_bwd(
        *sv["mla_qkv"], sv["y_mla"], sv["lse_m"], (dycat, (SSD_D + FOX_D) // ATT_W),
        scale=(MLA_NOPE + MLA_ROPE) ** -0.5, name=f"{tag}_mla_attn_bwd", rope=((sv["qr"], 0), (sv["kr"], 0)))
    dq_rope, dk_rope = _rowwise(_rope_t_fn, [dqr, dkr, cosf, sins], [], [BLOCK, BLOCK], name=f"{tag}_rope_bwd",
                                tile=272)
    dqh = jnp.concatenate([dqn_h, dq_rope], axis=1).astype(BF16)
    dkvh = jnp.concatenate([dkn_h, dv_h], axis=1).astype(BF16)
    dqn = _mm(dqh, W["mla_w_uq_p"][l], tb=True, name=f"{tag}_mla_uq_dx")
    G["mla_w_uq_p"] = _mm(sv["qn"], dqh, ta=True, name=f"{tag}_mla_uq_dw")
    dcn = _mm(dkvh, W["mla_w_ukv_p"][l], tb=True, name=f"{tag}_mla_ukv_dx")
    G["mla_w_ukv_p"] = _mm(sv["cn"], dkvh, ta=True, name=f"{tag}_mla_ukv_dw")
    (dcq, dckv), (dgq, dgkv) = _rowwise_bwd(_mla_norm_fn, sv["norm_rows"], [sv["gq"], sv["gkv"]], [dqn, dcn],
                                            name=f"{tag}_mla_norm_bwd", tile=272)
    G["mla_q_norm_g"], G["mla_kv_norm_g"] = dgq[0], dgkv[0]
    dproj = jnp.concatenate([dz, dxbc, dfq, dfk, dfv, dcq, dckv, ddt_raw, df_raw, dk_rope], axis=1).astype(BF16)
    dh1 = _mm(dproj, W["w_in_p"][l], tb=True, add=dh1_a, name=f"{tag}_in_proj_dx")
    G["w_in_p"] = _mm(sv["h1"], dproj, ta=True, name=f"{tag}_in_proj_dw")
    return dh1, G


def _local_step(x, target, W):
    h = jnp.concatenate([jnp.zeros((PAD_ROWS, D_MODEL), F32), W["meta"], x], axis=0)
    tgt = jnp.concatenate([jnp.zeros((BLOCK, D_MODEL), F32), target], axis=0)
    cosf, sins = _rope_tables()
    ln = lambda n, l: W[n][l][None]
    saved = []
    for l in range(DEPTH):
        h1, s1 = _ffn_fwd(h, W, "ffn1", l, ln("ln1_g", l), ln("ln1_b", l), f"l{l}_ffn1")
        h2, sm = _mixer_fwd(h1, W, l, cosf, sins)
        h, s2 = _ffn_fwd(h2, W, "ffn2", l, ln("ln3_g", l), ln("ln3_b", l), f"l{l}_ffn2")
        saved.append((s1, sm, s2))
    dh, loss = _loss_head(h, tgt, name="loss_head")
    grads = [None] * DEPTH
    GB = {}
    for l in reversed(range(DEPTH)):
        s1, sm, s2 = saved[l]
        G = {}
        dh, dg, db = _ffn_bwd(dh, s2, W, "ffn2", l, ln("ln3_g", l), ln("ln3_b", l), GB, f"l{l}_ffn2")
        G["ln3_g"], G["ln3_b"] = dg[0], db[0]
        dh, Gm = _mixer_bwd(dh, sm, W, l, cosf, sins, GB)
        G.update(Gm)
        dh, dg, db = _ffn_bwd(dh, s1, W, "ffn1", l, ln("ln1_g", l), ln("ln1_b", l), GB, f"l{l}_ffn1")
        G["ln1_g"], G["ln1_b"] = dg[0], db[0]
        grads[l] = G
    return loss, dh, grads, GB


WEIGHTS = ['meta', 'ffn1_w_gate', 'ffn1_w_up', 'ffn1_w_down', 'ln1_g', 'ln1_b', 'w_in', 'conv_w', 'conv_b', 'dt_bias',
           'a_log', 'd_skip', 'ssd_norm_g', 'fox_f_b', 'mla_q_norm_g', 'mla_w_uq', 'mla_kv_norm_g', 'mla_w_ukv',
           'w_out', 'ln2_g', 'ln2_b', 'ffn2_w_gate', 'ffn2_w_up', 'ffn2_w_down', 'ln3_g', 'ln3_b']
SMALL = ["ln1_g", "ln1_b", "conv_b", "dt_bias", "a_log", "d_skip", "ssd_norm_g", "fox_f_b", "mla_q_norm_g",
         "mla_kv_norm_g", "ln2_g", "ln2_b", "ln3_g", "ln3_b"]
MATMUL_W = ["ffn1_w_gate", "ffn1_w_up", "ffn1_w_down", "w_in", "mla_w_uq", "mla_w_ukv", "w_out", "ffn2_w_gate",
            "ffn2_w_up", "ffn2_w_down"]
SMALL_ROWS = 312


def _pad_to(a, axis, size):
    pads = [(0, 0)] * a.ndim
    pads[axis] = (0, size - a.shape[axis])
    return jnp.pad(a, pads)


def _chip_cols(full, chip, width):
    return lax.dynamic_slice_in_dim(full, chip * width, width, axis=full.ndim - 1)


def kernel(x, meta, ffn1_w_gate, ffn1_w_up, ffn1_w_down, ln1_g, ln1_b, w_in, conv_w, conv_b, dt_bias, a_log, d_skip, ssd_norm_g, fox_f_b, mla_q_norm_g, mla_w_uq, mla_kv_norm_g, mla_w_ukv, w_out, ln2_g, ln2_b, ffn2_w_gate, ffn2_w_up, ffn2_w_down, ln3_g, ln3_b, loss_target, m_meta, m_ffn1_w_gate, m_ffn1_w_up, m_ffn1_w_down, m_ln1_g, m_ln1_b, m_w_in, m_conv_w, m_conv_b, m_dt_bias, m_a_log, m_d_skip, m_ssd_norm_g, m_fox_f_b, m_mla_q_norm_g, m_mla_w_uq, m_mla_kv_norm_g, m_mla_w_ukv, m_w_out, m_ln2_g, m_ln2_b, m_ffn2_w_gate, m_ffn2_w_up, m_ffn2_w_down, m_ln3_g, m_ln3_b, v_meta, v_ffn1_w_gate, v_ffn1_w_up, v_ffn1_w_down, v_ln1_g, v_ln1_b, v_w_in, v_conv_w, v_conv_b, v_dt_bias, v_a_log, v_d_skip, v_ssd_norm_g, v_fox_f_b, v_mla_q_norm_g, v_mla_w_uq, v_mla_kv_norm_g, v_mla_w_ukv, v_w_out, v_ln2_g, v_ln2_b, v_ffn2_w_gate, v_ffn2_w_up, v_ffn2_w_down, v_ln3_g, v_ln3_b):
    args = dict(locals())
    w = {n: args[n] for n in WEIGHTS}
    m = {n: args["m_" + n] for n in WEIGHTS}
    v = {n: args["v_" + n] for n in WEIGHTS}
    xcoord, ycoord, _ = _my_pos()
    chip = 2 * xcoord + ycoord

    send = {}
    for pre in ("ffn1", "ffn2"):
        send[pre + "_w_gate"] = _pad_to(w[pre + "_w_gate"], 2, HP).astype(BF16)
        send[pre + "_w_up"] = _pad_to(w[pre + "_w_up"], 2, HP).astype(BF16)
        send[pre + "_w_down"] = _pad_to(w[pre + "_w_down"], 1, HP).astype(BF16)
    send["w_in"] = _pad_to(w["w_in"], 2, IN_SHARD_P).astype(BF16)
    for n in ("mla_w_uq", "mla_w_ukv", "w_out"):
        send[n] = w[n].astype(BF16)
    send["meta"] = w["meta"].reshape(2, N_META // 2, D_MODEL // N_CHIPS)
    send["conv_w"] = w["conv_w"]
    order = ["ffn1_w_gate", "ffn1_w_up", "ffn1_w_down", "meta", "w_in", "conv_w", "mla_w_uq", "mla_w_ukv", "w_out",
             "ffn2_w_gate", "ffn2_w_up", "ffn2_w_down"]
    got = dict(zip(order, _allgather_chips([send[n] for n in order])))

    W = {n: got[n] for n in got if n.startswith("ffn") or n == "w_out"}
    cat = lambda n, cut=None: jnp.concatenate([got[n][k][..., :cut] for k in range(N_CHIPS)], axis=-1)
    w_in_full = cat("w_in", IN_SHARD)
    W["w_in_p"] = [_pad_in_proj(w_in_full[l]) for l in range(DEPTH)]
    W["mla_w_uq_p"] = [_regroup_uq(cat("mla_w_uq")[l]) for l in range(DEPTH)]
    W["mla_w_ukv_p"] = [_regroup_ukv(cat("mla_w_ukv")[l]) for l in range(DEPTH)]
    W["conv_w"] = cat("conv_w")
    W["meta"] = jnp.concatenate([got["meta"][k].reshape(N_META, D_MODEL // N_CHIPS) for k in range(N_CHIPS)], axis=1)
    for n in SMALL:
        W[n] = w[n]

    loss, dh0, G, GB = _local_step(x[0], loss_target[0], W)

    def chunked(name, ungroup, width, pad):
        per_layer = []
        for l in range(DEPTH):
            full = ungroup(G[l][name])
            sh = jnp.moveaxis(full.reshape(full.shape[0], N_CHIPS, width), 1, 0)
            per_layer.append(_pad_to(sh, 2, pad))
        return jnp.stack(per_layer, axis=1)

    GB["w_in"] = chunked("w_in_p", _unpad_in_proj, IN_SHARD, IN_SHARD_P)
    GB["mla_w_uq"] = chunked("mla_w_uq_p", _ungroup_uq, MLA_NOPE + MLA_ROPE, MLA_NOPE + MLA_ROPE)
    GB["mla_w_ukv"] = chunked("mla_w_ukv_p", _ungroup_ukv, MLA_NOPE + MLA_V, MLA_NOPE + MLA_V)
    reduced = dict(zip(MATMUL_W, _reduce_scatter([GB[n] for n in MATMUL_W], MATMUL_W)))

    small_parts = [jnp.stack([G[l][n] for l in range(DEPTH)]).reshape(-1) for n in SMALL]
    small_parts += [jnp.stack([G[l]["conv_w"] for l in range(DEPTH)]).reshape(-1), dh0[PAD_ROWS:BLOCK].reshape(-1),
                    loss[0, :1]]
    flat = jnp.concatenate(small_parts)
    flat = jnp.pad(flat, (0, SMALL_ROWS * BLOCK - flat.shape[0]))
    red = _allreduce_small(flat.reshape(SMALL_ROWS, BLOCK)).reshape(-1)
    grads, off = {}, 0
    for n in SMALL:
        size = int(np.prod(w[n].shape))
        grads[n] = red[off:off + size].reshape(w[n].shape)
        off += size
    conv_full = red[off:off + DEPTH * SSD_CONV * 768].reshape(DEPTH, SSD_CONV, 768)
    off += DEPTH * SSD_CONV * 768
    meta_full = red[off:off + N_META * D_MODEL].reshape(N_META, D_MODEL)
    off += N_META * D_MODEL
    loss_out = red[off]
    grads["conv_w"] = _chip_cols(conv_full, chip, 768 // N_CHIPS)
    grads["meta"] = _chip_cols(meta_full, chip, D_MODEL // N_CHIPS)

    delta, new_m, new_v = {}, {}, {}
    for n in MATMUL_W:
        grads[n], delta[n], new_m[n], new_v[n] = _adamw(w[n], reduced[n], m[n], v[n], name=f"adamw_{n}")
    rest = [n for n in WEIGHTS if n not in MATMUL_W]

    def pack_small(d):
        f = jnp.concatenate([d[n].reshape(-1) for n in rest])
        tot = -(-f.shape[0] // (8 * BLOCK)) * 8 * BLOCK
        return jnp.pad(f, (0, tot - f.shape[0])).reshape(-1, BLOCK)

    _, d2, m2, v2 = _adamw(pack_small(w), pack_small(grads), pack_small(m), pack_small(v), name="adamw_small")
    off = 0
    for n in rest:
        size = int(np.prod(w[n].shape))
        for dst, src in ((delta, d2), (new_m, m2), (new_v, v2)):
            dst[n] = src.reshape(-1)[off:off + size].reshape(w[n].shape)
        off += size

    grad_x = dh0[BLOCK:][None]
    return (loss_out, grad_x, *[grads[n] for n in WEIGHTS], *[delta[n] for n in WEIGHTS],
            *[new_m[n] for n in WEIGHTS], *[new_v[n] for n in WEIGHTS])
```

```python
import functools

import numpy as np
import jax
import jax.numpy as jnp
from jax import lax
from jax.experimental import pallas as pl
from jax.experimental.pallas import tpu as pltpu

F32 = jnp.float32
BF16 = jnp.bfloat16
MESH = pl.DeviceIdType.MESH

D_MODEL = 1024
SEQ = 2048
N_META = 16
BLOCK = 128
PAD_ROWS = 112
LP = PAD_ROWS + N_META + SEQ
N_CHUNK = LP // BLOCK
DEPTH = 2
D_FF = 2816
N_CHIPS = 4
FF_SHARD = D_FF // N_CHIPS
HP = 768
FP = N_CHIPS * HP
SSD_HEADS, SSD_HD, SSD_D, SSD_GROUPS, SSD_STATE, SSD_CONV = 8, 64, 512, 2, 64, 4
FOX_HEADS, FOX_HD, FOX_D = 4, 64, 256
MLA_HEADS, MLA_Q_LORA, MLA_KV_LORA, MLA_NOPE, MLA_ROPE, MLA_V, MLA_D = 4, 256, 128, 64, 32, 64, 256
ROPE_HALF = MLA_ROPE // 2
ROPE_THETA = 10000.0
N_IN = 2476
IN_SHARD = N_IN // N_CHIPS
IN_SHARD_P = 640
ALPHA = (2 * DEPTH) ** 0.25
EPS = 1e-5
ADAM_LR, ADAM_B1, ADAM_B2, ADAM_EPS, ADAM_WD, ADAM_STEP = 0.001, 0.9, 0.999, 1e-08, 0.01, 10
NEG = -1e30
TM = 544

VMEM_LIMIT_BYTES = 56 * 1024 * 1024

PC_Z, PC_XBC, PC_FQ, PC_FK, PC_FV, PC_CQ, PC_CKV, PC_DT, PC_FR, PC_KR, PC_END = (
    0, 512, 1280, 1536, 1792, 2048, 2304, 2432, 2560, 2688, 2816)
OC_Z, OC_XBC, OC_DT, OC_FQ, OC_FK, OC_FV, OC_FR, OC_CQ, OC_CKV, OC_KR = (
    0, 512, 1280, 1288, 1544, 1800, 2056, 2060, 2316, 2444)


def _cparams(sem=None):
    return pltpu.CompilerParams(dimension_semantics=sem, vmem_limit_bytes=VMEM_LIMIT_BYTES)


def _tile(n, cap, mult):
    best = None
    for t in range(mult, min(n, cap) + 1, mult):
        if n % t == 0:
            best = t
    return best if best is not None else n


def _bs(shape, fn):
    return pl.BlockSpec(shape, fn)


ANY = pl.BlockSpec(memory_space=pl.ANY)


def _dims(ca, cb):
    return (((ca,), (cb,)), ((), ()))


def _raw_bdot(a, b, ca, cb):
    return lax.dot_general(a.astype(BF16), b.astype(BF16), _dims(ca, cb), preferred_element_type=F32)


@functools.partial(jax.custom_vjp, nondiff_argnums=(2, 3))
def _bdot(a, b, ca, cb):
    return _raw_bdot(a, b, ca, cb)


def _bdot_fwd(a, b, ca, cb):
    return _raw_bdot(a, b, ca, cb), (a, b)


def _bdot_bwd(ca, cb, res, g):
    a, b = res
    if (ca, cb) == (1, 0):
        return _raw_bdot(g, b, 1, 1), _raw_bdot(a, g, 0, 0)
    if (ca, cb) == (1, 1):
        return _raw_bdot(g, b, 1, 0), _raw_bdot(g, a, 0, 0)
    if (ca, cb) == (0, 0):
        return _raw_bdot(b, g, 1, 1), _raw_bdot(a, g, 1, 0)
    raise NotImplementedError((ca, cb))


_bdot.defvjp(_bdot_fwd, _bdot_bwd)


def _mm_core(a, b, *, a_spec, b_spec, o_spec, grid, out_shape, ca, cb, name, add=None, into=None):
    nk = grid[2]
    has_add, has_into = add is not None, into is not None
    acc_shape = tuple(d for d in o_spec.block_shape if d is not None)

    def body(*refs):
        a_ref, b_ref = refs[0], refs[1]
        p = 2
        add_ref = refs[p] if has_add else None
        p += int(has_add) + int(has_into)
        o_ref, acc_ref = refs[p], refs[p + 1]
        k = pl.program_id(2)

        @pl.when(k == 0)
        def _():
            acc_ref[...] = jnp.zeros_like(acc_ref)

        acc_ref[...] += _raw_bdot(a_ref[...], b_ref[...], ca, cb)

        @pl.when(k == nk - 1)
        def _():
            r = acc_ref[...]
            if has_add:
                r = r + add_ref[...]
            o_ref[...] = r

    ins = [a, b] + ([add] if has_add else []) + ([into] if has_into else [])
    in_specs = [a_spec, b_spec] + ([o_spec] if has_add else []) + ([ANY] if has_into else [])
    return pl.pallas_call(
        body, name=name, grid=grid, in_specs=in_specs, out_specs=o_spec,
        out_shape=jax.ShapeDtypeStruct(out_shape, F32), scratch_shapes=[pltpu.VMEM(acc_shape, F32)],
        input_output_aliases=({len(ins) - 1: 0} if has_into else {}),
        compiler_params=_cparams(("parallel", "parallel", "arbitrary")),
    )(*ins)


MM_VMEM_BUDGET = 40 * 1024 * 1024


def _divisors(n, mult):
    return [t for t in range(mult, n + 1, mult) if n % t == 0] or [n]


def _pick_tiles(M, N, K, a_bytes, b_bytes, ta, has_add):
    best = None
    for tm in _divisors(M, 128 if ta else 16):
        for tn in _divisors(N, 128):
            vmem = 2 * tm * K * a_bytes + 2 * K * tn * b_bytes + (3 + 2 * int(has_add)) * tm * tn * 4
            if vmem <= MM_VMEM_BUDGET:
                key = ((M // tm) * (N // tn), -tn)
                if best is None or key < best[0]:
                    best = (key, tm, tn)
    assert best is not None, (M, N, K)
    return best[1], best[2], K


def _mm(a, b, *, ta=False, tb=False, add=None, name):
    if ta:
        K, M = a.shape
    else:
        M, K = a.shape
    if tb:
        N, Kb = b.shape
    else:
        Kb, N = b.shape
    assert K == Kb, (a.shape, b.shape, ta, tb)
    tm, tn, tk = _pick_tiles(M, N, K, a.dtype.itemsize, b.dtype.itemsize, ta, add is not None)
    a_spec = _bs((tk, tm), lambda i, j, k: (k, i)) if ta else _bs((tm, tk), lambda i, j, k: (i, k))
    b_spec = _bs((tn, tk), lambda i, j, k: (j, k)) if tb else _bs((tk, tn), lambda i, j, k: (k, j))
    return _mm_core(a, b, a_spec=a_spec, b_spec=b_spec, o_spec=_bs((tm, tn), lambda i, j, k: (i, j)),
                    grid=(M // tm, N // tn, K // tk), out_shape=(M, N), ca=0 if ta else 1, cb=1 if tb else 0,
                    name=name, add=add)


def _row_entry(r, ncol):
    if isinstance(r, tuple):
        return r
    return r, r.shape[1] // ncol, 0


def _rowwise(fn, rows, pars, out_cols, *, name, tile, ncol=1, out_dtypes=None):
    rows = [_row_entry(r, ncol) for r in rows]
    L = rows[0][0].shape[0]
    nr, npar = len(rows), len(pars)
    in_specs = [_bs((tile, w), lambda g, i, o=o: (i, o + g)) for _, w, o in rows]
    in_specs += [_bs((p.shape[0], p.shape[1] // ncol), lambda g, i: (0, g)) for p in pars]
    out_specs = [_bs((tile, c // ncol), lambda g, i: (i, g)) for c in out_cols]

    def body(*refs):
        ins, outs = refs[:nr + npar], refs[nr + npar:]
        row0 = pl.program_id(1) * tile
        res = fn(row0, *[r[...] for r in ins])
        for o, v in zip(outs, res):
            o[...] = v.astype(o.dtype)

    return pl.pallas_call(
        body, name=name, grid=(ncol, L // tile), in_specs=in_specs, out_specs=out_specs,
        out_shape=[jax.ShapeDtypeStruct((L, c), d) for c, d in zip(out_cols, out_dtypes or [F32] * len(out_cols))],
        compiler_params=_cparams(("parallel", "parallel")),
    )(*[r[0] for r in rows], *pars)


def _rowwise_bwd(fn, rows, pars, douts, *, name, tile, ncol=1, row_grad=None, grad_dtypes=None):
    rows = [_row_entry(r, ncol) for r in rows]
    L = rows[0][0].shape[0]
    nr, npar, nd = len(rows), len(pars), len(douts)
    row_grad = [True] * nr if row_grad is None else row_grad
    in_specs = [_bs((tile, w), lambda g, i, o=o: (i, o + g)) for _, w, o in rows]
    in_specs += [_bs((p.shape[0], p.shape[1] // ncol), lambda g, i: (0, g)) for p in pars]
    in_specs += [_bs((tile, d.shape[1] // ncol), lambda g, i: (i, g)) for d in douts]
    g_widths = [w * ncol for (_, w, _), f in zip(rows, row_grad) if f]
    out_specs = [_bs((tile, w // ncol), lambda g, i: (i, g)) for w in g_widths]
    out_specs += [_bs((p.shape[0], p.shape[1] // ncol), lambda g, i: (0, g)) for p in pars]
    out_shape = [jax.ShapeDtypeStruct((L, w), d) for w, d in zip(g_widths, grad_dtypes or [F32] * len(g_widths))]
    out_shape += [jax.ShapeDtypeStruct(p.shape, F32) for p in pars]

    def body(*refs):
        ins = refs[:nr + npar]
        dos = refs[nr + npar:nr + npar + nd]
        outs = refs[nr + npar + nd:]
        i = pl.program_id(1)
        row0 = i * tile
        _, vjp = jax.vjp(lambda *a: tuple(fn(row0, *a)), *[r[...] for r in ins])
        grads = vjp(tuple(d[...].astype(F32) for d in dos))
        o = 0
        for j in range(nr):
            if row_grad[j]:
                outs[o][...] = grads[j].astype(outs[o].dtype)
                o += 1
        for j in range(npar):
            g, ref = grads[nr + j], outs[o + j]

            @pl.when(i == 0)
            def _(g=g, ref=ref):
                ref[...] = g

            @pl.when(i > 0)
            def _(g=g, ref=ref):
                ref[...] += g

    res = pl.pallas_call(
        body, name=name, grid=(ncol, L // tile), in_specs=in_specs, out_specs=out_specs, out_shape=out_shape,
        compiler_params=_cparams(("parallel", "arbitrary")),
    )(*[r[0] for r in rows], *pars, *douts)
    return res[:len(g_widths)], res[len(g_widths):]


def _row_ids(row0, shape):
    return row0 + lax.broadcasted_iota(jnp.int32, shape, 0)


def _sigmoid(x):
    return 1.0 / (1.0 + jnp.exp(-x))


@jax.custom_vjp
def _softplus(x):
    return jnp.maximum(x, 0.0) + jnp.log(1.0 + jnp.exp(-jnp.abs(x)))


def _softplus_fwd(x):
    return _softplus(x), x


def _softplus_bwd(x, g):
    return (g * _sigmoid(x),)


_softplus.defvjp(_softplus_fwd, _softplus_bwd)


def _silu(x):
    return x * _sigmoid(x)


def _swiglu_fn(row0, g, u):
    return (_silu(g) * u,)


def _make_res_ln_fn(scale):
    def fn(row0, h, o, gam, bet):
        pre = ALPHA * h + scale * o
        mu = jnp.mean(pre, axis=-1, keepdims=True)
        xc = pre - mu
        var = jnp.mean(xc * xc, axis=-1, keepdims=True)
        return (xc * lax.rsqrt(var + EPS) * gam + bet,)
    return fn


def _ssd_pre_fn(row0, raw, bias):
    dt = _softplus(raw + bias)
    return (jnp.where(_row_ids(row0, raw.shape) >= PAD_ROWS, dt, 0.0),)


def _ssd_post_fn(row0, y, xs, z, dskip, normg):
    v = (y + dskip * xs) * _silu(z)
    v = v * lax.rsqrt(jnp.mean(v * v, axis=-1, keepdims=True) + EPS)
    return (v * normg,)


def _mla_norm_fn(row0, cq, ckv, gq, gkv):
    qn = cq * lax.rsqrt(jnp.mean(cq * cq, axis=-1, keepdims=True) + EPS) * gq
    cn = ckv * lax.rsqrt(jnp.mean(ckv * ckv, axis=-1, keepdims=True) + EPS) * gkv
    return qn, cn


def _rope_fn(row0, q, k, cosf, sins):
    return (q * cosf + pltpu.roll(q, 64, 1) * sins, k * cosf + pltpu.roll(k, 64, 1) * sins)


def _rope_t_fn(row0, gq, gk, cosf, sins):
    return (gq * cosf + pltpu.roll(gq * sins, 64, 1), gk * cosf + pltpu.roll(gk * sins, 64, 1))


def _conv_fwd(x, x_off, w, b, *, name):
    C = w.shape[1]

    def body(x_ref, w_ref, b_ref, o_ref):
        rows = lax.broadcasted_iota(jnp.int32, (LP, BLOCK), 0)
        xv = jnp.where(rows >= PAD_ROWS, x_ref[...], 0.0)
        acc = b_ref[...] + w_ref[3:4, :] * xv
        for k in range(SSD_CONV - 1):
            acc = acc + w_ref[k:k + 1, :] * pltpu.roll(xv, SSD_CONV - 1 - k, 0)
        o_ref[...] = _silu(acc)

    return pl.pallas_call(
        body, name=name, grid=(C // BLOCK,),
        in_specs=[_bs((LP, BLOCK), lambda j: (0, j + x_off)), _bs((SSD_CONV, BLOCK), lambda j: (0, j)),
                  _bs((1, BLOCK), lambda j: (0, j))],
        out_specs=_bs((LP, BLOCK), lambda j: (0, j)),
        out_shape=jax.ShapeDtypeStruct((LP, C), F32), compiler_params=_cparams(("parallel",)),
    )(x, w, b)


def _conv_bwd(x, x_off, w, b, dout, *, name):
    C = w.shape[1]

    def body(x_ref, w_ref, b_ref, do_ref, dx_ref, dw_ref, db_ref):
        rows = lax.broadcasted_iota(jnp.int32, (LP, BLOCK), 0)
        real = rows >= PAD_ROWS
        xv = jnp.where(real, x_ref[...], 0.0)
        shifted = [pltpu.roll(xv, SSD_CONV - 1 - k, 0) for k in range(SSD_CONV - 1)] + [xv]
        acc = b_ref[...]
        for k in range(SSD_CONV):
            acc = acc + w_ref[k:k + 1, :] * shifted[k]
        sig = _sigmoid(acc)
        dacc = jnp.where(real, do_ref[...] * (sig * (1.0 + acc * (1.0 - sig))), 0.0)
        db_ref[...] = jnp.sum(dacc, axis=0, keepdims=True)
        dx = w_ref[3:4, :] * dacc
        for k in range(SSD_CONV):
            dw_ref[k:k + 1, :] = jnp.sum(dacc * shifted[k], axis=0, keepdims=True)
            if k < SSD_CONV - 1:
                dx = dx + w_ref[k:k + 1, :] * pltpu.roll(dacc, LP - (SSD_CONV - 1 - k), 0)
        dx_ref[...] = jnp.where(real, dx, 0.0)

    return pl.pallas_call(
        body, name=name, grid=(C // BLOCK,),
        in_specs=[_bs((LP, BLOCK), lambda j: (0, j + x_off)), _bs((SSD_CONV, BLOCK), lambda j: (0, j)),
                  _bs((1, BLOCK), lambda j: (0, j)), _bs((LP, BLOCK), lambda j: (0, j))],
        out_specs=[_bs((LP, BLOCK), lambda j: (0, j)), _bs((SSD_CONV, BLOCK), lambda j: (0, j)),
                   _bs((1, BLOCK), lambda j: (0, j))],
        out_shape=[jax.ShapeDtypeStruct((LP, C), F32), jax.ShapeDtypeStruct((SSD_CONV, C), F32),
                   jax.ShapeDtypeStruct((1, C), F32)],
        compiler_params=_cparams(("parallel",)),
    )(x, w, b, dout)


_BDIMS = {"nn": (((2,), (1,)), ((0,), (0,))), "nt": (((2,), (2,)), ((0,), (0,))), "tn": (((1,), (1,)), ((0,), (0,)))}


def _raw_bdot3(a, b, mode):
    return lax.dot_general(a.astype(BF16), b.astype(BF16), _BDIMS[mode], preferred_element_type=F32)


@functools.partial(jax.custom_vjp, nondiff_argnums=(2,))
def _bdot3(a, b, mode):
    return _raw_bdot3(a, b, mode)


def _bdot3_fwd(a, b, mode):
    return _raw_bdot3(a, b, mode), (a, b)


def _bdot3_bwd(mode, res, g):
    a, b = res
    if mode == "nn":
        return _raw_bdot3(g, b, "nt"), _raw_bdot3(a, g, "tn")
    if mode == "nt":
        return _raw_bdot3(g, b, "nn"), _raw_bdot3(g, a, "tn")
    return _raw_bdot3(b, g, "nt"), _raw_bdot3(a, g, "nn")


_bdot3.defvjp(_bdot3_fwd, _bdot3_bwd)


def _ssd_chunk(x, bm, cm, dtc, dtr, alog, prev):
    rep = SSD_HEADS // SSD_GROUPS
    per_head = lambda t: jnp.broadcast_to(t[:, None], (SSD_GROUPS, rep) + t.shape[1:]).reshape((SSD_HEADS,) + t.shape[1:])
    bm, cm = per_head(bm), per_head(cm)
    lane = lax.broadcasted_iota(jnp.int32, alog.shape, 2)
    a_neg = -jnp.exp(jnp.sum(jnp.where(lane == 0, alog, 0.0), axis=2, keepdims=True))
    ac_in = dtc * a_neg
    ar_in = dtr * a_neg
    li = lax.broadcasted_iota(jnp.int32, (1, BLOCK, BLOCK), 1)
    si = lax.broadcasted_iota(jnp.int32, (1, BLOCK, BLOCK), 2)
    causal = li >= si
    acum_c = jnp.sum(jnp.where(causal, ar_in, 0.0), axis=2, keepdims=True)
    acum_r = jnp.sum(jnp.where(li <= si, ac_in, 0.0), axis=1, keepdims=True)
    total = jnp.sum(ar_in, axis=2, keepdims=True)
    seg = jnp.exp(jnp.where(causal, acum_c - acum_r, NEG))
    xdt = x * dtc
    cb = _bdot3(cm, bm, "nt")
    y = _bdot3(cb * seg, xdt, "nn") + _bdot3(cm, prev, "nt") * jnp.exp(acum_c)
    st = _bdot3(xdt, bm * jnp.exp(total - acum_c), "tn")
    return y, prev * jnp.exp(total) + st


def _ssd_specs(rev):
    ci = (lambda c: N_CHUNK - 1 - c) if rev else (lambda c: c)
    x_spec = _bs((SSD_HEADS, BLOCK, SSD_HD), lambda c: (0, ci(c), 0))
    g_spec = _bs((SSD_GROUPS, BLOCK, SSD_STATE), lambda c: (0, ci(c), 0))
    dtc_spec = _bs((SSD_HEADS, BLOCK, 1), lambda c: (0, ci(c), 0))
    dtr_spec = _bs((SSD_HEADS, 1, BLOCK), lambda c: (0, 0, ci(c)))
    al_spec = _bs((SSD_HEADS, 1, BLOCK), lambda c: (0, 0, 0))
    st_spec = _bs((None, SSD_HEADS, SSD_HD, SSD_STATE), lambda c: (ci(c), 0, 0, 0))
    return x_spec, g_spec, dtc_spec, dtr_spec, al_spec, st_spec


def _ssd_fwd(x, bm, cm, dtc, dtr, alog, *, name):
    x_spec, g_spec, dtc_spec, dtr_spec, al_spec, st_spec = _ssd_specs(False)

    def body(x_ref, b_ref, c_ref, dtc_ref, dtr_ref, al_ref, y_ref, prev_ref, state):
        @pl.when(pl.program_id(0) == 0)
        def _():
            state[...] = jnp.zeros_like(state)

        prev = state[...]
        prev_ref[...] = prev
        y, new = _ssd_chunk(x_ref[...], b_ref[...], c_ref[...], dtc_ref[...], dtr_ref[...], al_ref[...], prev)
        y_ref[...] = y
        state[...] = new

    return pl.pallas_call(
        body, name=name, grid=(N_CHUNK,),
        in_specs=[x_spec, g_spec, g_spec, dtc_spec, dtr_spec, al_spec], out_specs=[x_spec, st_spec],
        out_shape=[jax.ShapeDtypeStruct((SSD_HEADS, LP, SSD_HD), F32),
                   jax.ShapeDtypeStruct((N_CHUNK, SSD_HEADS, SSD_HD, SSD_STATE), F32)],
        scratch_shapes=[pltpu.VMEM((SSD_HEADS, SSD_HD, SSD_STATE), F32)],
        compiler_params=_cparams(("arbitrary",)),
    )(x, bm, cm, dtc, dtr, alog)


def _ssd_bwd(x, bm, cm, dtc, dtr, alog, prevs, dy, *, name):
    x_spec, g_spec, dtc_spec, dtr_spec, al_spec, st_spec = _ssd_specs(True)

    def body(x_ref, b_ref, c_ref, dtc_ref, dtr_ref, al_ref, prev_ref, dy_ref,
             dx_ref, db_ref, dc_ref, ddtc_ref, ddtr_ref, dal_ref, dstate):
        c = pl.program_id(0)

        @pl.when(c == 0)
        def _():
            dstate[...] = jnp.zeros_like(dstate)

        _, vjp = jax.vjp(_ssd_chunk, x_ref[...], b_ref[...], c_ref[...], dtc_ref[...], dtr_ref[...], al_ref[...],
                         prev_ref[...])
        dx, db, dc, ddtc, ddtr, dal, dprev = vjp((dy_ref[...], dstate[...]))
        dx_ref[...] = dx
        db_ref[...] = db
        dc_ref[...] = dc
        ddtc_ref[...] = ddtc
        ddtr_ref[...] = ddtr
        dstate[...] = dprev

        @pl.when(c == 0)
        def _():
            dal_ref[...] = dal

        @pl.when(c > 0)
        def _():
            dal_ref[...] += dal

    hs = jax.ShapeDtypeStruct((SSD_HEADS, LP, SSD_HD), F32)
    gs = jax.ShapeDtypeStruct((SSD_GROUPS, LP, SSD_STATE), F32)
    return pl.pallas_call(
        body, name=name, grid=(N_CHUNK,),
        in_specs=[x_spec, g_spec, g_spec, dtc_spec, dtr_spec, al_spec, st_spec, x_spec],
        out_specs=[x_spec, g_spec, g_spec, dtc_spec, dtr_spec, al_spec],
        out_shape=[hs, gs, gs, jax.ShapeDtypeStruct((SSD_HEADS, LP, 1), F32),
                   jax.ShapeDtypeStruct((SSD_HEADS, 1, LP), F32), jax.ShapeDtypeStruct((SSD_HEADS, 1, BLOCK), F32)],
        scratch_shapes=[pltpu.VMEM((SSD_HEADS, SSD_HD, SSD_STATE), F32)],
        compiler_params=_cparams(("arbitrary",)),
    )(x, bm, cm, dtc, dtr, alog, prevs, dy)


def _tri_dot(tri, v):
    hi = v.astype(BF16)
    r1 = v - hi.astype(F32)
    mid = r1.astype(BF16)
    lo = (r1 - mid.astype(F32)).astype(BF16)
    t = tri.astype(BF16)
    d = lambda p: lax.dot_general(t, p, _dims(1, 0), preferred_element_type=F32)
    return d(hi) + d(mid) + d(lo)


def _fox_gate_fwd(raw, raw_blk, bias, *, name):
    def body(raw_ref, b_ref, c_ref, ct_ref, carry):
        j = pl.program_id(0)

        @pl.when(j == 0)
        def _():
            carry[...] = jnp.zeros_like(carry)

        rows = j * BLOCK + lax.broadcasted_iota(jnp.int32, (BLOCK, BLOCK), 0)
        lf = jnp.where(rows >= PAD_ROWS, -_softplus(-(raw_ref[...] + b_ref[...])), 0.0)
        li = lax.broadcasted_iota(jnp.int32, (BLOCK, BLOCK), 0)
        si = lax.broadcasted_iota(jnp.int32, (BLOCK, BLOCK), 1)
        cv = _tri_dot(jnp.where(li >= si, 1.0, 0.0), lf) + carry[...]
        c_ref[...] = cv
        ct_ref[...] = cv.T
        carry[...] += jnp.sum(lf, axis=0, keepdims=True)

    return pl.pallas_call(
        body, name=name, grid=(N_CHUNK,),
        in_specs=[_bs((BLOCK, BLOCK), lambda j: (j, raw_blk)), _bs((1, BLOCK), lambda j: (0, 0))],
        out_specs=[_bs((BLOCK, BLOCK), lambda j: (j, 0)), _bs((BLOCK, BLOCK), lambda j: (0, j))],
        out_shape=[jax.ShapeDtypeStruct((LP, BLOCK), F32), jax.ShapeDtypeStruct((BLOCK, LP), F32)],
        scratch_shapes=[pltpu.VMEM((1, BLOCK), F32)], compiler_params=_cparams(("arbitrary",)),
    )(raw, bias)


def _fox_gate_bwd(raw, raw_blk, bias, dc, dct, *, name):
    rj = lambda j: N_CHUNK - 1 - j

    def body(raw_ref, b_ref, dc_ref, dct_ref, draw_ref, db_ref, carry):
        j = pl.program_id(0)

        @pl.when(j == 0)
        def _():
            carry[...] = jnp.zeros_like(carry)

        rows = (N_CHUNK - 1 - j) * BLOCK + lax.broadcasted_iota(jnp.int32, (BLOCK, BLOCK), 0)
        li = lax.broadcasted_iota(jnp.int32, (BLOCK, BLOCK), 0)
        si = lax.broadcasted_iota(jnp.int32, (BLOCK, BLOCK), 1)
        dcv = dc_ref[...] + dct_ref[...].T
        dlf = _tri_dot(jnp.where(li <= si, 1.0, 0.0), dcv) + carry[...]
        carry[...] += jnp.sum(dcv, axis=0, keepdims=True)
        draw = jnp.where(rows >= PAD_ROWS, dlf * (1.0 - _sigmoid(raw_ref[...] + b_ref[...])), 0.0)
        draw_ref[...] = draw
        dsum = jnp.sum(draw, axis=0, keepdims=True)

        @pl.when(j == 0)
        def _():
            db_ref[...] = dsum

        @pl.when(j > 0)
        def _():
            db_ref[...] += dsum

    return pl.pallas_call(
        body, name=name, grid=(N_CHUNK,),
        in_specs=[_bs((BLOCK, BLOCK), lambda j: (rj(j), raw_blk)), _bs((1, BLOCK), lambda j: (0, 0)),
                  _bs((BLOCK, BLOCK), lambda j: (rj(j), 0)), _bs((BLOCK, BLOCK), lambda j: (0, rj(j)))],
        out_specs=[_bs((BLOCK, BLOCK), lambda j: (rj(j), 0)), _bs((1, BLOCK), lambda j: (0, 0))],
        out_shape=[jax.ShapeDtypeStruct((LP, BLOCK), F32), jax.ShapeDtypeStruct((1, BLOCK), F32)],
        scratch_shapes=[pltpu.VMEM((1, BLOCK), F32)], compiler_params=_cparams(("arbitrary",)),
    )(raw, bias, dc, dct)


ATT_W = 256
ATT_QB = 272
ATT_STEPS = LP // ATT_QB
ATT_KEYS = (640, 1152, 1664, LP)


def _lane_head(width, per, mod=None):
    lane = lax.broadcasted_iota(jnp.int32, (1, width), 1)
    if mod is not None:
        lane = lane % mod
    return lane // per


def _attn_mask(i, kw):
    r = i * ATT_QB + lax.broadcasted_iota(jnp.int32, (ATT_QB, kw), 0)
    c = lax.broadcasted_iota(jnp.int32, (ATT_QB, kw), 1)
    return (c <= r) & ((c >= PAD_ROWS) | (r < PAD_ROWS))


def _attn_by_key_class(i, fn):
    for p, kw in enumerate(ATT_KEYS):
        @pl.when(i // 2 == p)
        def _(kw=kw):
            fn(kw)


def _attn_specs(q, k, v, bias, rope):
    qspec = lambda blk, w=ATT_W: _bs((ATT_QB, w), lambda i: (i, blk))
    fspec = lambda blk, w=ATT_W: _bs((LP, w), lambda i: (0, blk))
    ins = [q[0], k[0], v[0]]
    specs = [qspec(q[1]), fspec(k[1]), fspec(v[1])]
    if bias is not None:
        ins += [bias[0], bias[1]]
        specs += [qspec(0, BLOCK), _bs((BLOCK, LP), lambda i: (0, 0))]
    if rope is not None:
        ins += [rope[0][0], rope[1][0]]
        specs += [qspec(rope[0][1], BLOCK), fspec(rope[1][1], BLOCK)]
    return ins, specs, qspec, fspec


def _attn_fwd(q, k, v, *, scale, name, bias=None, rope=None):
    ins, specs, qspec, fspec = _attn_specs(q, k, v, bias, rope)
    has_bias, has_rope = bias is not None, rope is not None

    def body(*refs):
        it = iter(refs)
        q_ref, k_ref, v_ref = next(it), next(it), next(it)
        if has_bias:
            c_ref, ct_ref = next(it), next(it)
        if has_rope:
            qr_ref, kr_ref = next(it), next(it)
        o_ref, lse_ref = next(it), next(it)
        i = pl.program_id(0)

        def block(kw):
            ok = _attn_mask(i, kw)
            qv, kv, vv = q_ref[...], k_ref[0:kw, :], v_ref[0:kw, :]
            hid, l128 = _lane_head(ATT_W, FOX_HD), _lane_head(BLOCK, 1)
            if has_rope:
                rid = _lane_head(BLOCK, ROPE_HALF, 64)
                qrv, krv = qr_ref[...], kr_ref[0:kw, :]
            def head(h, carry):
                o_acc, lse_acc = carry
                s = _raw_bdot(jnp.where(hid == h, qv, 0.0), kv, 1, 1)
                if has_rope:
                    s = s + _raw_bdot(jnp.where(rid == h, qrv, 0.0), krv, 1, 1)
                s = s * scale
                if has_bias:
                    cq = jnp.sum(jnp.where(l128 == h, c_ref[...], 0.0), axis=1, keepdims=True)
                    s = s + (cq - ct_ref[pl.ds(h, 1), 0:kw])
                s = jnp.where(ok, s, NEG)
                m = jnp.max(s, axis=1, keepdims=True)
                p = jnp.exp(s - m)
                l = jnp.sum(p, axis=1, keepdims=True)
                o_acc = jnp.where(hid == h, _raw_bdot(p, vv, 1, 0) / l, o_acc)
                lse_acc = jnp.where(l128 == h, m + jnp.log(l), lse_acc)
                return o_acc, lse_acc

            o_acc, lse_acc = lax.fori_loop(
                0, FOX_HEADS, head, (jnp.zeros((ATT_QB, ATT_W), F32), jnp.zeros((ATT_QB, BLOCK), F32)))
            o_ref[...] = o_acc
            lse_ref[...] = lse_acc

        _attn_by_key_class(i, block)

    return pl.pallas_call(
        body, name=name, grid=(ATT_STEPS,), in_specs=specs, out_specs=[qspec(0), qspec(0, BLOCK)],
        out_shape=[jax.ShapeDtypeStruct((LP, ATT_W), F32), jax.ShapeDtypeStruct((LP, BLOCK), F32)],
        compiler_params=_cparams(("parallel",)),
    )(*ins)


def _attn_bwd(q, k, v, o, lse, do, *, scale, name, bias=None, rope=None):
    ins, specs, qspec, fspec = _attn_specs(q, k, v, bias, rope)
    has_bias, has_rope = bias is not None, rope is not None
    ins += [o, lse, do[0]]
    specs += [qspec(0), qspec(0, BLOCK), qspec(do[1])]

    def body(*refs):
        it = iter(refs)
        q_ref, k_ref, v_ref = next(it), next(it), next(it)
        if has_bias:
            c_ref, ct_ref = next(it), next(it)
        if has_rope:
            qr_ref, kr_ref = next(it), next(it)
        o_ref, lse_ref, do_ref = next(it), next(it), next(it)
        dq_ref, dk_ref, dv_ref = next(it), next(it), next(it)
        if has_bias:
            dc_ref, dct_ref = next(it), next(it)
        if has_rope:
            dqr_ref, dkr_ref = next(it), next(it)
        i = pl.program_id(0)

        @pl.when(i == 0)
        def _():
            dk_ref[...] = jnp.zeros_like(dk_ref)
            dv_ref[...] = jnp.zeros_like(dv_ref)
            if has_rope:
                dkr_ref[...] = jnp.zeros_like(dkr_ref)
            if has_bias:
                dct_ref[...] = jnp.zeros_like(dct_ref)

        def block(kw):
            ok = _attn_mask(i, kw)
            qv, kv, vv = q_ref[...], k_ref[0:kw, :], v_ref[0:kw, :]
            ov, dov, lsev = o_ref[...], do_ref[...], lse_ref[...]
            hid, l128 = _lane_head(ATT_W, FOX_HD), _lane_head(BLOCK, 1)
            if has_rope:
                rid = _lane_head(BLOCK, ROPE_HALF, 64)
                qrv, krv = qr_ref[...], kr_ref[0:kw, :]

            def head(h, carry):
                dq_acc, aux_acc = carry
                qm = jnp.where(hid == h, qv, 0.0)
                s = _raw_bdot(qm, kv, 1, 1)
                if has_rope:
                    qrm = jnp.where(rid == h, qrv, 0.0)
                    s = s + _raw_bdot(qrm, krv, 1, 1)
                s = s * scale
                if has_bias:
                    cq = jnp.sum(jnp.where(l128 == h, c_ref[...], 0.0), axis=1, keepdims=True)
                    s = s + (cq - ct_ref[pl.ds(h, 1), 0:kw])
                s = jnp.where(ok, s, NEG)
                p = jnp.exp(s - jnp.sum(jnp.where(l128 == h, lsev, 0.0), axis=1, keepdims=True))
                dom = jnp.where(hid == h, dov, 0.0)
                dp = _raw_bdot(dom, vv, 1, 1)
                delta = jnp.sum(dom * ov, axis=1, keepdims=True)
                ds = p * (dp - delta)
                dq_acc = jnp.where(hid == h, _raw_bdot(ds, kv, 1, 0) * scale, dq_acc)
                dk_ref[0:kw, :] += _raw_bdot(ds, qm, 0, 0) * scale
                dv_ref[0:kw, :] += _raw_bdot(p, dom, 0, 0)
                if has_rope:
                    aux_acc = jnp.where(rid == h, _raw_bdot(ds, krv, 1, 0) * scale, aux_acc)
                    dkr_ref[0:kw, :] += _raw_bdot(ds, qrm, 0, 0) * scale
                if has_bias:
                    aux_acc = jnp.where(l128 == h, jnp.sum(ds, axis=1, keepdims=True), aux_acc)
                    dct_ref[pl.ds(h, 1), 0:kw] -= jnp.sum(ds, axis=0, keepdims=True)
                return dq_acc, aux_acc

            dq_acc, aux_acc = lax.fori_loop(
                0, FOX_HEADS, head, (jnp.zeros((ATT_QB, ATT_W), F32), jnp.zeros((ATT_QB, BLOCK), F32)))
            dq_ref[...] = dq_acc
            if has_bias:
                dc_ref[...] = aux_acc
            if has_rope:
                dqr_ref[...] = aux_acc

        _attn_by_key_class(i, block)

    wide = jax.ShapeDtypeStruct((LP, ATT_W), F32)
    narrow = jax.ShapeDtypeStruct((LP, BLOCK), F32)
    out_specs = [qspec(0), fspec(0), fspec(0)]
    out_shape = [wide, wide, wide]
    if has_bias:
        out_specs += [qspec(0, BLOCK), _bs((BLOCK, LP), lambda i: (0, 0))]
        out_shape += [narrow, jax.ShapeDtypeStruct((BLOCK, LP), F32)]
    if has_rope:
        out_specs += [qspec(0, BLOCK), fspec(0, BLOCK)]
        out_shape += [narrow, narrow]
    return pl.pallas_call(
        body, name=name, grid=(ATT_STEPS,), in_specs=specs, out_specs=out_specs, out_shape=out_shape,
        compiler_params=_cparams(("arbitrary",)),
    )(*ins)


def _loss_head(y, target, *, name):
    tile = 272

    def body(y_ref, t_ref, dy_ref, loss_ref):
        i = pl.program_id(0)
        rows = i * tile + lax.broadcasted_iota(jnp.int32, (tile, D_MODEL), 0)
        err = jnp.where(rows >= BLOCK, y_ref[...] - t_ref[...], 0.0)
        dy_ref[...] = err * (1.0 / D_MODEL)
        part = 0.5 * jnp.sum(jnp.sum(err * err, axis=1, keepdims=True) * (1.0 / D_MODEL), axis=0, keepdims=True)
        part = jnp.broadcast_to(part, (1, BLOCK))

        @pl.when(i == 0)
        def _():
            loss_ref[...] = part

        @pl.when(i > 0)
        def _():
            loss_ref[...] += part

    return pl.pallas_call(
        body, name=name, grid=(LP // tile,),
        in_specs=[_bs((tile, D_MODEL), lambda i: (i, 0)), _bs((tile, D_MODEL), lambda i: (i, 0))],
        out_specs=[_bs((tile, D_MODEL), lambda i: (i, 0)), _bs((1, BLOCK), lambda i: (0, 0))],
        out_shape=[jax.ShapeDtypeStruct((LP, D_MODEL), F32), jax.ShapeDtypeStruct((1, BLOCK), F32)],
        compiler_params=_cparams(("arbitrary",)),
    )(y, target)


def _adamw(w, g, m, v, *, name):
    if w.ndim == 2:
        w, g, m, v = w[None], g[None], m[None], v[None]
        squeeze = True
    else:
        squeeze = False
    NL, R, C = w.shape
    CG = g.shape[2]
    tile = _tile(R, 256, 8)

    def body(w_ref, g_ref, m_ref, v_ref, go_ref, d_ref, nm_ref, nv_ref):
        gv = g_ref[:, :C]
        nm = ADAM_B1 * m_ref[...] + (1.0 - ADAM_B1) * gv
        nv = ADAM_B2 * v_ref[...] + (1.0 - ADAM_B2) * (gv * gv)
        m_hat = nm / (1.0 - ADAM_B1 ** ADAM_STEP)
        v_hat = nv / (1.0 - ADAM_B2 ** ADAM_STEP)
        go_ref[...] = gv
        d_ref[...] = -ADAM_LR * (m_hat / (jnp.sqrt(v_hat) + ADAM_EPS) + ADAM_WD * w_ref[...])
        nm_ref[...] = nm
        nv_ref[...] = nv

    spec = _bs((None, tile, C), lambda l, i: (l, i, 0))
    gspec = _bs((None, tile, CG), lambda l, i: (l, i, 0))
    res = pl.pallas_call(
        body, name=name, grid=(NL, R // tile), in_specs=[spec, gspec, spec, spec], out_specs=[spec] * 4,
        out_shape=[jax.ShapeDtypeStruct((NL, R, C), F32)] * 4, compiler_params=_cparams(("parallel", "parallel")),
    )(w, g, m, v)
    return [r[0] for r in res] if squeeze else res


def _my_pos():
    return lax.axis_index("x"), lax.axis_index("y"), lax.axis_index("c")


def _other_chips(x, y):
    return [(1 - x, y), (x, 1 - y), (1 - x, 1 - y)]


def _allgather_chips(shards):
    n = len(shards)
    per = 7

    def body(*refs):
        ins, outs = refs[:n], refs[n:2 * n]
        send_sems, recv_sems = refs[2 * n], refs[2 * n + 1]
        x, y, c = _my_pos()
        chips = _other_chips(x, y)
        sibling, me = (x, y, 1 - c), 2 * x + y

        def cp(a, kk, src, dst, to):
            return pltpu.make_async_remote_copy(src_ref=src, dst_ref=dst, send_sem=send_sems.at[per * a + kk],
                                                recv_sem=recv_sems.at[per * a + kk], device_id=to, device_id_type=MESH)

        sends = []
        for a in range(n):
            for j, chip in enumerate(chips):
                sends.append(cp(a, j, ins[a].at[c], outs[a].at[me, c], (*chip, c)))
            sends.append(cp(a, 3, ins[a], outs[a].at[me], sibling))
        for s in sends:
            s.start()
        for a in range(n):
            for j, chip in enumerate(chips):
                slab = outs[a].at[2 * chip[0] + chip[1], c]
                cp(a, j, slab, slab, (x, y, c)).wait_recv()
                fwd = cp(a, 4 + j, slab, slab, sibling)
                fwd.start()
                sends.append(fwd)
        for a in range(n):
            cp(a, 3, ins[a], outs[a].at[me], (x, y, c)).wait_recv()
            for j, chip in enumerate(chips):
                slab = outs[a].at[2 * chip[0] + chip[1], 1 - c]
                cp(a, 4 + j, slab, slab, (x, y, c)).wait_recv()
        for s in sends:
            s.wait_send()

    return pl.pallas_call(
        body, name="allgather_chips", in_specs=[ANY] * n, out_specs=[ANY] * n,
        out_shape=[jax.ShapeDtypeStruct((N_CHIPS,) + s.shape, s.dtype) for s in shards],
        scratch_shapes=[pltpu.SemaphoreType.DMA((per * n,)), pltpu.SemaphoreType.DMA((per * n,))],
    )(*shards)


def _rs_swap_layers(gs):
    n = len(gs)

    def body(*refs):
        ins, outs = refs[:n], refs[n:2 * n]
        send_sems, recv_sems = refs[2 * n], refs[2 * n + 1]
        x, y, c = _my_pos()
        cps = [pltpu.make_async_remote_copy(
            src_ref=ins[a].at[:, 1 - c], dst_ref=outs[a], send_sem=send_sems.at[a], recv_sem=recv_sems.at[a],
            device_id=(x, y, 1 - c), device_id_type=MESH) for a in range(n)]
        for cp in cps:
            cp.start()
        for cp in cps:
            cp.wait()

    return pl.pallas_call(
        body, name="rs_swap_layers", in_specs=[ANY] * n, out_specs=[ANY] * n,
        out_shape=[jax.ShapeDtypeStruct((N_CHIPS,) + g.shape[2:], g.dtype) for g in gs],
        scratch_shapes=[pltpu.SemaphoreType.DMA((n,)), pltpu.SemaphoreType.DMA((n,))],
    )(*gs)


def _rs_add_pair(g, r, pos, *, name):
    _, _, R, C = g.shape
    tile = _tile(R, 512, 16)

    def body(pos_ref, g_ref, r_ref, o32_ref, o16_ref):
        s = g_ref[...] + r_ref[...]
        o32_ref[...] = s
        o16_ref[...] = s.astype(BF16)

    spec = _bs((None, tile, C), lambda k, i, pos_ref: (k, i, 0))
    grid_spec = pltpu.PrefetchScalarGridSpec(
        num_scalar_prefetch=1, grid=(N_CHIPS, R // tile),
        in_specs=[_bs((None, None, tile, C), lambda k, i, pos_ref: (k, pos_ref[1], i, 0)), spec],
        out_specs=[spec, spec])
    return pl.pallas_call(
        body, name=name, grid_spec=grid_spec,
        out_shape=[jax.ShapeDtypeStruct((N_CHIPS, R, C), F32), jax.ShapeDtypeStruct((N_CHIPS, R, C), BF16)],
        compiler_params=_cparams(("parallel", "parallel")),
    )(pos, g, r)


def _rs_exchange_chips(ps):
    n = len(ps)

    def body(*refs):
        ins, outs = refs[:n], refs[n:2 * n]
        send_sems, recv_sems = refs[2 * n], refs[2 * n + 1]
        x, y, c = _my_pos()
        cps = []
        for a in range(n):
            for j, chip in enumerate(_other_chips(x, y)):
                cps.append(pltpu.make_async_remote_copy(
                    src_ref=ins[a].at[2 * chip[0] + chip[1]], dst_ref=outs[a].at[j], send_sem=send_sems.at[3 * a + j],
                    recv_sem=recv_sems.at[3 * a + j], device_id=(*chip, c), device_id_type=MESH))
        for cp in cps:
            cp.start()
        for cp in cps:
            cp.wait()

    return pl.pallas_call(
        body, name="rs_exchange_chips", in_specs=[ANY] * n, out_specs=[ANY] * n,
        out_shape=[jax.ShapeDtypeStruct((3,) + p.shape[1:], p.dtype) for p in ps],
        scratch_shapes=[pltpu.SemaphoreType.DMA((3 * n,)), pltpu.SemaphoreType.DMA((3 * n,))],
    )(*ps)


def _rs_add_chips(p32, r16, pos, *, name):
    _, R, C = p32.shape
    tile = _tile(R, 512, 16)

    def body(pos_ref, p_ref, r_ref, o_ref):
        o_ref[...] = ((p_ref[...] + r_ref[0].astype(F32)) + r_ref[1].astype(F32)) + r_ref[2].astype(F32)

    grid_spec = pltpu.PrefetchScalarGridSpec(
        num_scalar_prefetch=1, grid=(R // tile,),
        in_specs=[_bs((None, tile, C), lambda i, pos_ref: (pos_ref[0], i, 0)),
                  _bs((3, tile, C), lambda i, pos_ref: (0, i, 0))],
        out_specs=_bs((None, tile, C), lambda i, pos_ref: (pos_ref[1], i, 0)))
    return pl.pallas_call(
        body, name=name, grid_spec=grid_spec, out_shape=jax.ShapeDtypeStruct((2, R, C), F32),
        compiler_params=_cparams(("parallel",)),
    )(pos, p32, r16)


def _rs_join_layers(fs):
    n = len(fs)

    def body(*refs):
        outs = refs[n:2 * n]
        send_sems, recv_sems = refs[2 * n], refs[2 * n + 1]
        x, y, c = _my_pos()
        for a in range(n):
            pltpu.make_async_remote_copy(src_ref=outs[a].at[c], dst_ref=outs[a].at[c], send_sem=send_sems.at[a],
                                         recv_sem=recv_sems.at[a], device_id=(x, y, 1 - c), device_id_type=MESH).start()
        for a in range(n):
            pltpu.make_async_remote_copy(src_ref=outs[a].at[c], dst_ref=outs[a].at[1 - c], send_sem=send_sems.at[a],
                                         recv_sem=recv_sems.at[a], device_id=(x, y, 1 - c), device_id_type=MESH).wait()

    return pl.pallas_call(
        body, name="rs_join_layers", in_specs=[ANY] * n, out_specs=[ANY] * n,
        out_shape=[jax.ShapeDtypeStruct(f.shape, f.dtype) for f in fs],
        input_output_aliases={a: a for a in range(n)},
        scratch_shapes=[pltpu.SemaphoreType.DMA((n,)), pltpu.SemaphoreType.DMA((n,))],
    )(*fs)


def _reduce_scatter(gs, names):
    x, y, c = _my_pos()
    pos = jnp.stack([2 * x + y, c]).astype(jnp.int32)
    r1 = _rs_swap_layers(gs)
    pairs = [_rs_add_pair(g, r, pos, name=f"rs_add_pair_{nm}") for g, r, nm in zip(gs, r1, names)]
    r2 = _rs_exchange_chips([p[1] for p in pairs])
    fs = [_rs_add_chips(p[0], r, pos, name=f"rs_add_chips_{nm}") for p, r, nm in zip(pairs, r2, names)]
    return _rs_join_layers(fs)


def _allreduce_small(buf):
    R, W = buf.shape

    def body(b_ref, o_ref, gather, send_sems, recv_sems):
        x, y, c = _my_pos()
        me = 4 * x + 2 * y + c
        gather[me] = b_ref[...]
        cps = []
        for d in range(1, 8):
            peer = (x ^ (d >> 2), y ^ ((d >> 1) & 1), c ^ (d & 1))
            cps.append(pltpu.make_async_remote_copy(
                src_ref=b_ref, dst_ref=gather.at[me], send_sem=send_sems.at[d - 1], recv_sem=recv_sems.at[d - 1],
                device_id=peer, device_id_type=MESH))
        for cp in cps:
            cp.start()
        for d in range(1, 8):
            pltpu.make_async_remote_copy(
                src_ref=b_ref, dst_ref=gather.at[me ^ d], send_sem=send_sems.at[d - 1], recv_sem=recv_sems.at[d - 1],
                device_id=(x, y, c), device_id_type=MESH).wait_recv()
        for cp in cps:
            cp.wait_send()
        acc = gather[0]
        for d in range(1, 8):
            acc = acc + gather[d]
        o_ref[...] = acc

    vm = pl.BlockSpec(memory_space=pltpu.VMEM)
    return pl.pallas_call(
        body, name="allreduce_small", in_specs=[vm], out_specs=vm, out_shape=jax.ShapeDtypeStruct((R, W), F32),
        scratch_shapes=[pltpu.VMEM((8, R, W), F32), pltpu.SemaphoreType.DMA((7,)), pltpu.SemaphoreType.DMA((7,))],
    )(buf)


def _heads(a, h, d):
    return a.reshape(a.shape[0], h, d).transpose(1, 0, 2)


def _unheads(a):
    h, L, d = a.shape
    return a.transpose(1, 0, 2).reshape(L, h * d)


def _rope_tables():
    pos = jnp.maximum(jnp.arange(LP, dtype=F32) - PAD_ROWS, 0.0)
    inv_freq = 1.0 / (ROPE_THETA ** (jnp.arange(0, MLA_ROPE, 2, dtype=F32) / MLA_ROPE))
    ang = pos[:, None] * inv_freq[None, :]
    cos, sin = jnp.tile(jnp.cos(ang), (1, MLA_HEADS)), jnp.tile(jnp.sin(ang), (1, MLA_HEADS))
    return jnp.concatenate([cos, cos], axis=1), jnp.concatenate([-sin, sin], axis=1)


def _lane_pad(a, width=BLOCK):
    return jnp.pad(a, ((0, 0), (0, width - a.shape[1])))


def _pad_in_proj(w):
    sl = lambda start, size: w[:, start:start + size]
    return jnp.concatenate([
        sl(OC_Z, 512), sl(OC_XBC, 768), sl(OC_FQ, 256), sl(OC_FK, 256), sl(OC_FV, 256), sl(OC_CQ, 256), sl(OC_CKV, 128),
        _lane_pad(sl(OC_DT, SSD_HEADS)), _lane_pad(sl(OC_FR, FOX_HEADS)),
        jnp.tile(sl(OC_KR, ROPE_HALF), (1, MLA_HEADS)), jnp.tile(sl(OC_KR + ROPE_HALF, ROPE_HALF), (1, MLA_HEADS))], axis=1)


def _unpad_in_proj(wp):
    sl = lambda start, size: wp[:, start:start + size]
    rope = lambda start: sl(start, 64).reshape(wp.shape[0], MLA_HEADS, ROPE_HALF).sum(axis=1)
    return jnp.concatenate([
        sl(PC_Z, 512), sl(PC_XBC, 768), sl(PC_DT, SSD_HEADS), sl(PC_FQ, 256), sl(PC_FK, 256), sl(PC_FV, 256),
        sl(PC_FR, FOX_HEADS), sl(PC_CQ, 256), sl(PC_CKV, 128), rope(PC_KR), rope(PC_KR + 64)], axis=1)


def _regroup_uq(w):
    w3 = w.reshape(w.shape[0], MLA_HEADS, MLA_NOPE + MLA_ROPE)
    return jnp.concatenate([w3[:, :, :MLA_NOPE].reshape(w.shape[0], -1),
                            w3[:, :, MLA_NOPE:MLA_NOPE + ROPE_HALF].reshape(w.shape[0], -1),
                            w3[:, :, MLA_NOPE + ROPE_HALF:].reshape(w.shape[0], -1)], axis=1)


def _ungroup_uq(wp):
    n = wp.shape[0]
    return jnp.concatenate([wp[:, :256].reshape(n, MLA_HEADS, MLA_NOPE), wp[:, 256:320].reshape(n, MLA_HEADS, ROPE_HALF),
                            wp[:, 320:].reshape(n, MLA_HEADS, ROPE_HALF)], axis=2).reshape(n, -1)


def _regroup_ukv(w):
    w3 = w.reshape(w.shape[0], MLA_HEADS, MLA_NOPE + MLA_V)
    return jnp.concatenate([w3[:, :, :MLA_NOPE].reshape(w.shape[0], -1), w3[:, :, MLA_NOPE:].reshape(w.shape[0], -1)],
                           axis=1)


def _ungroup_ukv(wp):
    n = wp.shape[0]
    return jnp.concatenate([wp[:, :256].reshape(n, MLA_HEADS, MLA_NOPE), wp[:, 256:].reshape(n, MLA_HEADS, MLA_V)],
                           axis=2).reshape(n, -1)


TMF = 1088
N_IF = LP // TMF


def _chunk_cols_mm(a, w, l, chunk_w, *, name):
    K = a.shape[1]
    return _mm_core(a, w, a_spec=_bs((TMF, K), lambda i, j, k: (i, 0)),
                    b_spec=_bs((None, None, K, chunk_w), lambda i, j, k: (j, l, 0, 0)),
                    o_spec=_bs((TMF, chunk_w), lambda i, j, k: (i, j)), grid=(N_IF, N_CHIPS, 1),
                    out_shape=(LP, N_CHIPS * chunk_w), ca=1, cb=0, name=name)


def _chunk_cols_dx(g, w, l, chunk_w, add, *, name):
    K = w.shape[2]
    return _mm_core(g, w, a_spec=_bs((TMF, chunk_w), lambda i, j, k: (i, k)),
                    b_spec=_bs((None, None, K, chunk_w), lambda i, j, k: (k, l, 0, 0)),
                    o_spec=_bs((TMF, K), lambda i, j, k: (i, 0)), grid=(N_IF, 1, N_CHIPS),
                    out_shape=(LP, K), ca=1, cb=1, name=name, add=add)


def _chunk_cols_dw(a, g, l, chunk_w, into, *, name):
    K = a.shape[1]
    return _mm_core(a, g, a_spec=_bs((LP, K), lambda i, j, k: (0, 0)), b_spec=_bs((LP, chunk_w), lambda i, j, k: (0, j)),
                    o_spec=_bs((None, None, K, chunk_w), lambda i, j, k: (j, l, 0, 0)), grid=(1, N_CHIPS, 1),
                    out_shape=(N_CHIPS, DEPTH, K, chunk_w), ca=0, cb=0, name=name, into=into)


def _chunk_rows_mm(a, w, l, chunk_h, *, name):
    N = w.shape[3]
    return _mm_core(a, w, a_spec=_bs((TMF, chunk_h), lambda i, j, k: (i, k)),
                    b_spec=_bs((None, None, chunk_h, N), lambda i, j, k: (k, l, 0, 0)),
                    o_spec=_bs((TMF, N), lambda i, j, k: (i, 0)), grid=(N_IF, 1, N_CHIPS),
                    out_shape=(LP, N), ca=1, cb=0, name=name)


def _chunk_rows_dx(g, w, l, chunk_h, *, name):
    N = w.shape[3]
    return _mm_core(g, w, a_spec=_bs((TMF, N), lambda i, j, k: (i, 0)),
                    b_spec=_bs((None, None, chunk_h, N), lambda i, j, k: (j, l, 0, 0)),
                    o_spec=_bs((TMF, chunk_h), lambda i, j, k: (i, j)), grid=(N_IF, N_CHIPS, 1),
                    out_shape=(LP, N_CHIPS * chunk_h), ca=1, cb=1, name=name)


def _chunk_rows_dw(a, g, l, chunk_h, into, *, name):
    N = g.shape[1]
    return _mm_core(a, g, a_spec=_bs((LP, chunk_h), lambda i, j, k: (0, i)), b_spec=_bs((LP, N), lambda i, j, k: (0, 0)),
                    o_spec=_bs((None, None, chunk_h, N), lambda i, j, k: (i, l, 0, 0)), grid=(N_CHIPS, 1, 1),
                    out_shape=(N_CHIPS, DEPTH, chunk_h, N), ca=0, cb=0, name=name, into=into)


def _ffn_fwd(h, W, pre, l, gam, bet, tag):
    g = _chunk_cols_mm(h, W[pre + "_w_gate"], l, HP, name=f"{tag}_gate")
    u = _chunk_cols_mm(h, W[pre + "_w_up"], l, HP, name=f"{tag}_up")
    (act,) = _rowwise(_swiglu_fn, [g, u], [], [FP], name=f"{tag}_swiglu", tile=TM, ncol=N_CHIPS, out_dtypes=[BF16])
    o = _chunk_rows_mm(act, W[pre + "_w_down"], l, HP, name=f"{tag}_down")
    (out,) = _rowwise(_make_res_ln_fn(0.5), [h, o], [gam, bet], [D_MODEL], name=f"{tag}_ln", tile=272)
    return out, (h, g, u, act, o)


def _ffn_bwd(dout, saved, W, pre, l, gam, bet, GB, tag):
    h, g, u, act, o = saved
    (dh_a, do), (dgam, dbet) = _rowwise_bwd(_make_res_ln_fn(0.5), [h, o], [gam, bet], [dout], name=f"{tag}_ln_bwd",
                                            tile=272, grad_dtypes=[F32, BF16])
    dact = _chunk_rows_dx(do, W[pre + "_w_down"], l, HP, name=f"{tag}_down_dx")
    GB[pre + "_w_down"] = _chunk_rows_dw(act, do, l, HP, GB.get(pre + "_w_down"), name=f"{tag}_down_dw")
    (dg, du), _ = _rowwise_bwd(_swiglu_fn, [g, u], [], [dact], name=f"{tag}_swiglu_bwd", tile=TM, ncol=N_CHIPS,
                               grad_dtypes=[BF16, BF16])
    GB[pre + "_w_gate"] = _chunk_cols_dw(h, dg, l, HP, GB.get(pre + "_w_gate"), name=f"{tag}_gate_dw")
    GB[pre + "_w_up"] = _chunk_cols_dw(h, du, l, HP, GB.get(pre + "_w_up"), name=f"{tag}_up_dw")
    dh = _chunk_cols_dx(dg, W[pre + "_w_gate"], l, HP, dh_a, name=f"{tag}_gate_dx")
    dh = _chunk_cols_dx(du, W[pre + "_w_up"], l, HP, dh, name=f"{tag}_up_dx")
    return dh, dgam, dbet


def _mixer_fwd(h1, W, l, cosf, sins):
    tag = f"l{l}"
    proj = _mm(h1, W["w_in_p"][l], name=f"{tag}_in_proj")
    sv = {"h1": h1, "proj": proj}
    conv_w, conv_b = W["conv_w"][l], W["conv_b"][l][None]
    xc = _conv_fwd(proj, PC_XBC // BLOCK, conv_w, conv_b, name=f"{tag}_conv")
    dt_bias = _lane_pad(W["dt_bias"][l][None])
    (dt,) = _rowwise(_ssd_pre_fn, [(proj, BLOCK, PC_DT // BLOCK)], [dt_bias], [BLOCK], name=f"{tag}_ssd_dt", tile=272)
    xh = _heads(xc[:, :SSD_D], SSD_HEADS, SSD_HD)
    bm = _heads(xc[:, SSD_D:SSD_D + 128], SSD_GROUPS, SSD_STATE)
    cm = _heads(xc[:, SSD_D + 128:], SSD_GROUPS, SSD_STATE)
    dt8 = dt[:, :SSD_HEADS].T
    dtc, dtr = dt8[:, :, None], dt8[:, None, :]
    alog = jnp.broadcast_to(W["a_log"][l][:, None, None], (SSD_HEADS, 1, BLOCK))
    yh, prevs = _ssd_fwd(xh, bm, cm, dtc, dtr, alog, name=f"{tag}_ssd")
    y_raw = _unheads(yh)
    dskip = jnp.repeat(W["d_skip"][l], SSD_HD)[None]
    normg = W["ssd_norm_g"][l][None]
    post_rows = [y_raw, (xc, 256, 0), (proj, 256, PC_Z // 256)]
    (y_ssd,) = _rowwise(_ssd_post_fn, post_rows, [dskip, normg], [SSD_D], name=f"{tag}_ssd_post", tile=272,
                        ncol=SSD_GROUPS)
    sv.update(conv_w=conv_w, conv_b=conv_b, dt_bias=dt_bias, xh=xh, bm=bm, cm=cm, dtc=dtc, dtr=dtr, alog=alog,
              prevs=prevs, post_rows=post_rows, dskip=dskip, normg=normg)
    f_b = _lane_pad(W["fox_f_b"][l][None])
    cg, cgt = _fox_gate_fwd(proj, PC_FR // BLOCK, f_b, name=f"{tag}_fox_gate")
    fox_qkv = ((proj, PC_FQ // ATT_W), (proj, PC_FK // ATT_W), (proj, PC_FV // ATT_W))
    y_fox, lse_f = _attn_fwd(*fox_qkv, scale=FOX_HD ** -0.5, name=f"{tag}_fox_attn", bias=(cg, cgt))
    sv.update(f_b=f_b, cg=cg, cgt=cgt, fox_qkv=fox_qkv, y_fox=y_fox, lse_f=lse_f)
    gq, gkv = W["mla_q_norm_g"][l][None], W["mla_kv_norm_g"][l][None]
    norm_rows = [(proj, 256, PC_CQ // 256), (proj, BLOCK, PC_CKV // BLOCK)]
    qn, cn = _rowwise(_mla_norm_fn, norm_rows, [gq, gkv], [MLA_Q_LORA, MLA_KV_LORA], name=f"{tag}_mla_norm", tile=272,
                      out_dtypes=[BF16, BF16])
    qh = _mm(qn, W["mla_w_uq_p"][l], name=f"{tag}_mla_uq")
    kvh = _mm(cn, W["mla_w_ukv_p"][l], name=f"{tag}_mla_ukv")
    qr, kr = _rowwise(_rope_fn, [(qh, BLOCK, 2), (proj, BLOCK, PC_KR // BLOCK), cosf, sins], [], [BLOCK, BLOCK],
                      name=f"{tag}_rope", tile=272)
    mla_qkv = ((qh, 0), (kvh, 0), (kvh, 1))
    y_mla, lse_m = _attn_fwd(*mla_qkv, scale=(MLA_NOPE + MLA_ROPE) ** -0.5, name=f"{tag}_mla_attn",
                             rope=((qr, 0), (kr, 0)))
    sv.update(gq=gq, gkv=gkv, norm_rows=norm_rows, qn=qn, cn=cn, qr=qr, kr=kr, mla_qkv=mla_qkv, y_mla=y_mla, lse_m=lse_m)
    ycat = jnp.concatenate([y_ssd, y_fox, y_mla], axis=1).astype(BF16)
    mix = _chunk_rows_mm(ycat, W["w_out"], l, 256, name=f"{tag}_out_proj")
    (h2,) = _rowwise(_make_res_ln_fn(1.0), [h1, mix], [W["ln2_g"][l][None], W["ln2_b"][l][None]], [D_MODEL],
                     name=f"{tag}_ln2", tile=272)
    sv.update(mix=mix, ycat=ycat)
    return h2, sv


def _mixer_bwd(dh2, sv, W, l, cosf, sins, GB):
    tag = f"l{l}"
    G = {}
    proj = sv["proj"]
    ln2g, ln2b = W["ln2_g"][l][None], W["ln2_b"][l][None]
    (dh1_a, dmix), (dln2g, dln2b) = _rowwise_bwd(
        _make_res_ln_fn(1.0), [sv["h1"], sv["mix"]], [ln2g, ln2b], [dh2], name=f"{tag}_ln2_bwd", tile=272,
        grad_dtypes=[F32, BF16])
    G["ln2_g"], G["ln2_b"] = dln2g[0], dln2b[0]
    dycat = _chunk_rows_dx(dmix, W["w_out"], l, 256, name=f"{tag}_out_proj_dx")
    GB["w_out"] = _chunk_rows_dw(sv["ycat"], dmix, l, 256, GB.get("w_out"), name=f"{tag}_out_proj_dw")
    (dy_raw, dxs_a, dz), (ddskip, dnormg) = _rowwise_bwd(
        _ssd_post_fn, sv["post_rows"], [sv["dskip"], sv["normg"]], [dycat[:, :SSD_D]],
        name=f"{tag}_ssd_post_bwd", tile=272, ncol=SSD_GROUPS)
    G["ssd_norm_g"] = dnormg[0]
    G["d_skip"] = ddskip.reshape(SSD_HEADS, SSD_HD).sum(axis=1)
    dxh, dbm, dcm, ddtc, ddtr, dal = _ssd_bwd(sv["xh"], sv["bm"], sv["cm"], sv["dtc"], sv["dtr"], sv["alog"],
                                              sv["prevs"], _heads(dy_raw, SSD_HEADS, SSD_HD), name=f"{tag}_ssd_bwd")
    G["a_log"] = dal[:, 0, 0]
    dxc = jnp.concatenate([dxs_a + _unheads(dxh), _unheads(dbm), _unheads(dcm)], axis=1)
    dxbc, G["conv_w"], dconv_b = _conv_bwd(proj, PC_XBC // BLOCK, sv["conv_w"], sv["conv_b"], dxc,
                                           name=f"{tag}_conv_bwd")
    G["conv_b"] = dconv_b[0]
    ddt = _lane_pad((ddtc[:, :, 0] + ddtr[:, 0, :]).T)
    (ddt_raw,), (ddt_bias,) = _rowwise_bwd(_ssd_pre_fn, [(proj, BLOCK, PC_DT // BLOCK)], [sv["dt_bias"]], [ddt],
                                           name=f"{tag}_ssd_dt_bwd", tile=272)
    G["dt_bias"] = ddt_bias[0, :SSD_HEADS]
    dfq, dfk, dfv, dcg, dcgt = _attn_bwd(*sv["fox_qkv"], sv["y_fox"], sv["lse_f"], (dycat, SSD_D // ATT_W),
                                         scale=FOX_HD ** -0.5, name=f"{tag}_fox_attn_bwd", bias=(sv["cg"], sv["cgt"]))
    df_raw, dfb = _fox_gate_bwd(proj, PC_FR // BLOCK, sv["f_b"], dcg, dcgt, name=f"{tag}_fox_gate_bwd")
    G["fox_f_b"] = dfb[0, :FOX_HEADS]
    dqn_h, dkn_h, dv_h, dqr, dkr = _attn_bwd(
        *sv["mla_qkv"], sv["y_mla"], sv["lse_m"], (dycat, (SSD_D + FOX_D) // ATT_W),
        scale=(MLA_NOPE + MLA_ROPE) ** -0.5, name=f"{tag}_mla_attn_bwd", rope=((sv["qr"], 0), (sv["kr"], 0)))
    dq_rope, dk_rope = _rowwise(_rope_t_fn, [dqr, dkr, cosf, sins], [], [BLOCK, BLOCK], name=f"{tag}_rope_bwd",
                                tile=272)
    dqh = jnp.concatenate([dqn_h, dq_rope], axis=1).astype(BF16)
    dkvh = jnp.concatenate([dkn_h, dv_h], axis=1).astype(BF16)
    dqn = _mm(dqh, W["mla_w_uq_p"][l], tb=True, name=f"{tag}_mla_uq_dx")
    G["mla_w_uq_p"] = _mm(sv["qn"], dqh, ta=True, name=f"{tag}_mla_uq_dw")
    dcn = _mm(dkvh, W["mla_w_ukv_p"][l], tb=True, name=f"{tag}_mla_ukv_dx")
    G["mla_w_ukv_p"] = _mm(sv["cn"], dkvh, ta=True, name=f"{tag}_mla_ukv_dw")
    (dcq, dckv), (dgq, dgkv) = _rowwise_bwd(_mla_norm_fn, sv["norm_rows"], [sv["gq"], sv["gkv"]], [dqn, dcn],
                                            name=f"{tag}_mla_norm_bwd", tile=272)
    G["mla_q_norm_g"], G["mla_kv_norm_g"] = dgq[0], dgkv[0]
    dproj = jnp.concatenate([dz, dxbc, dfq, dfk, dfv, dcq, dckv, ddt_raw, df_raw, dk_rope], axis=1).astype(BF16)
    dh1 = _mm(dproj, W["w_in_p"][l], tb=True, add=dh1_a, name=f"{tag}_in_proj_dx")
    G["w_in_p"] = _mm(sv["h1"], dproj, ta=True, name=f"{tag}_in_proj_dw")
    return dh1, G


def _local_step(x, target, W):
    h = jnp.concatenate([jnp.zeros((PAD_ROWS, D_MODEL), F32), W["meta"], x], axis=0)
    tgt = jnp.concatenate([jnp.zeros((BLOCK, D_MODEL), F32), target], axis=0)
    cosf, sins = _rope_tables()
    ln = lambda n, l: W[n][l][None]
    saved = []
    for l in range(DEPTH):
        h1, s1 = _ffn_fwd(h, W, "ffn1", l, ln("ln1_g", l), ln("ln1_b", l), f"l{l}_ffn1")
        h2, sm = _mixer_fwd(h1, W, l, cosf, sins)
        h, s2 = _ffn_fwd(h2, W, "ffn2", l, ln("ln3_g", l), ln("ln3_b", l), f"l{l}_ffn2")
        saved.append((s1, sm, s2))
    dh, loss = _loss_head(h, tgt, name="loss_head")
    grads = [None] * DEPTH
    GB = {}
    for l in reversed(range(DEPTH)):
        s1, sm, s2 = saved[l]
        G = {}
        dh, dg, db = _ffn_bwd(dh, s2, W, "ffn2", l, ln("ln3_g", l), ln("ln3_b", l), GB, f"l{l}_ffn2")
        G["ln3_g"], G["ln3_b"] = dg[0], db[0]
        dh, Gm = _mixer_bwd(dh, sm, W, l, cosf, sins, GB)
        G.update(Gm)
        dh, dg, db = _ffn_bwd(dh, s1, W, "ffn1", l, ln("ln1_g", l), ln("ln1_b", l), GB, f"l{l}_ffn1")
        G["ln1_g"], G["ln1_b"] = dg[0], db[0]
        grads[l] = G
    return loss, dh, grads, GB


WEIGHTS = ['meta', 'ffn1_w_gate', 'ffn1_w_up', 'ffn1_w_down', 'ln1_g', 'ln1_b', 'w_in', 'conv_w', 'conv_b', 'dt_bias',
           'a_log', 'd_skip', 'ssd_norm_g', 'fox_f_b', 'mla_q_norm_g', 'mla_w_uq', 'mla_kv_norm_g', 'mla_w_ukv',
           'w_out', 'ln2_g', 'ln2_b', 'ffn2_w_gate', 'ffn2_w_up', 'ffn2_w_down', 'ln3_g', 'ln3_b']
SMALL = ["ln1_g", "ln1_b", "conv_b", "dt_bias", "a_log", "d_skip", "ssd_norm_g", "fox_f_b", "mla_q_norm_g",
         "mla_kv_norm_g", "ln2_g", "ln2_b", "ln3_g", "ln3_b"]
MATMUL_W = ["ffn1_w_gate", "ffn1_w_up", "ffn1_w_down", "w_in", "mla_w_uq", "mla_w_ukv", "w_out", "ffn2_w_gate",
            "ffn2_w_up", "ffn2_w_down"]
SMALL_ROWS = 312


def _pad_to(a, axis, size):
    pads = [(0, 0)] * a.ndim
    pads[axis] = (0, size - a.shape[axis])
    return jnp.pad(a, pads)


def _chip_cols(full, chip, width):
    return lax.dynamic_slice_in_dim(full, chip * width, width, axis=full.ndim - 1)


def kernel(x, meta, ffn1_w_gate, ffn1_w_up, ffn1_w_down, ln1_g, ln1_b, w_in, conv_w, conv_b, dt_bias, a_log, d_skip, ssd_norm_g, fox_f_b, mla_q_norm_g, mla_w_uq, mla_kv_norm_g, mla_w_ukv, w_out, ln2_g, ln2_b, ffn2_w_gate, ffn2_w_up, ffn2_w_down, ln3_g, ln3_b, loss_target, m_meta, m_ffn1_w_gate, m_ffn1_w_up, m_ffn1_w_down, m_ln1_g, m_ln1_b, m_w_in, m_conv_w, m_conv_b, m_dt_bias, m_a_log, m_d_skip, m_ssd_norm_g, m_fox_f_b, m_mla_q_norm_g, m_mla_w_uq, m_mla_kv_norm_g, m_mla_w_ukv, m_w_out, m_ln2_g, m_ln2_b, m_ffn2_w_gate, m_ffn2_w_up, m_ffn2_w_down, m_ln3_g, m_ln3_b, v_meta, v_ffn1_w_gate, v_ffn1_w_up, v_ffn1_w_down, v_ln1_g, v_ln1_b, v_w_in, v_conv_w, v_conv_b, v_dt_bias, v_a_log, v_d_skip, v_ssd_norm_g, v_fox_f_b, v_mla_q_norm_g, v_mla_w_uq, v_mla_kv_norm_g, v_mla_w_ukv, v_w_out, v_ln2_g, v_ln2_b, v_ffn2_w_gate, v_ffn2_w_up, v_ffn2_w_down, v_ln3_g, v_ln3_b):
    args = dict(locals())
    w = {n: args[n] for n in WEIGHTS}
    m = {n: args["m_" + n] for n in WEIGHTS}
    v = {n: args["v_" + n] for n in WEIGHTS}
    xcoord, ycoord, _ = _my_pos()
    chip = 2 * xcoord + ycoord

    send = {}
    for pre in ("ffn1", "ffn2"):
        send[pre + "_w_gate"] = _pad_to(w[pre + "_w_gate"], 2, HP).astype(BF16)
        send[pre + "_w_up"] = _pad_to(w[pre + "_w_up"], 2, HP).astype(BF16)
        send[pre + "_w_down"] = _pad_to(w[pre + "_w_down"], 1, HP).astype(BF16)
    send["w_in"] = _pad_to(w["w_in"], 2, IN_SHARD_P).astype(BF16)
    for n in ("mla_w_uq", "mla_w_ukv", "w_out"):
        send[n] = w[n].astype(BF16)
    send["meta"] = w["meta"].reshape(2, N_META // 2, D_MODEL // N_CHIPS)
    send["conv_w"] = w["conv_w"]
    order = ["ffn1_w_gate", "ffn1_w_up", "ffn1_w_down", "meta", "w_in", "conv_w", "mla_w_uq", "mla_w_ukv", "w_out",
             "ffn2_w_gate", "ffn2_w_up", "ffn2_w_down"]
    got = dict(zip(order, _allgather_chips([send[n] for n in order])))

    W = {n: got[n] for n in got if n.startswith("ffn") or n == "w_out"}
    cat = lambda n, cut=None: jnp.concatenate([got[n][k][..., :cut] for k in range(N_CHIPS)], axis=-1)
    w_in_full = cat("w_in", IN_SHARD)
    W["w_in_p"] = [_pad_in_proj(w_in_full[l]) for l in range(DEPTH)]
    W["mla_w_uq_p"] = [_regroup_uq(cat("mla_w_uq")[l]) for l in range(DEPTH)]
    W["mla_w_ukv_p"] = [_regroup_ukv(cat("mla_w_ukv")[l]) for l in range(DEPTH)]
    W["conv_w"] = cat("conv_w")
    W["meta"] = jnp.concatenate([got["meta"][k].reshape(N_META, D_MODEL // N_CHIPS) for k in range(N_CHIPS)], axis=1)
    for n in SMALL:
        W[n] = w[n]

    loss, dh0, G, GB = _local_step(x[0], loss_target[0], W)

    def chunked(name, ungroup, width, pad):
        per_layer = []
        for l in range(DEPTH):
            full = ungroup(G[l][name])
            sh = jnp.moveaxis(full.reshape(full.shape[0], N_CHIPS, width), 1, 0)
            per_layer.append(_pad_to(sh, 2, pad))
        return jnp.stack(per_layer, axis=1)

    GB["w_in"] = chunked("w_in_p", _unpad_in_proj, IN_SHARD, IN_SHARD_P)
    GB["mla_w_uq"] = chunked("mla_w_uq_p", _ungroup_uq, MLA_NOPE + MLA_ROPE, MLA_NOPE + MLA_ROPE)
    GB["mla_w_ukv"] = chunked("mla_w_ukv_p", _ungroup_ukv, MLA_NOPE + MLA_V, MLA_NOPE + MLA_V)
    reduced = dict(zip(MATMUL_W, _reduce_scatter([GB[n] for n in MATMUL_W], MATMUL_W)))

    small_parts = [jnp.stack([G[l][n] for l in range(DEPTH)]).reshape(-1) for n in SMALL]
    small_parts += [jnp.stack([G[l]["conv_w"] for l in range(DEPTH)]).reshape(-1), dh0[PAD_ROWS:BLOCK].reshape(-1),
                    loss[0, :1]]
    flat = jnp.concatenate(small_parts)
    flat = jnp.pad(flat, (0, SMALL_ROWS * BLOCK - flat.shape[0]))
    red = _allreduce_small(flat.reshape(SMALL_ROWS, BLOCK)).reshape(-1)
    grads, off = {}, 0
    for n in SMALL:
        size = int(np.prod(w[n].shape))
        grads[n] = red[off:off + size].reshape(w[n].shape)
        off += size
    conv_full = red[off:off + DEPTH * SSD_CONV * 768].reshape(DEPTH, SSD_CONV, 768)
    off += DEPTH * SSD_CONV * 768
    meta_full = red[off:off + N_META * D_MODEL].reshape(N_META, D_MODEL)
    off += N_META * D_MODEL
    loss_out = red[off]
    grads["conv_w"] = _chip_cols(conv_full, chip, 768 // N_CHIPS)
    grads["meta"] = _chip_cols(meta_full, chip, D_MODEL // N_CHIPS)

    delta, new_m, new_v = {}, {}, {}
    for n in MATMUL_W:
        grads[n], delta[n], new_m[n], new_v[n] = _adamw(w[n], reduced[n], m[n], v[n], name=f"adamw_{n}")
    rest = [n for n in WEIGHTS if n not in MATMUL_W]

    def pack_small(d):
        f = jnp.concatenate([d[n].reshape(-1) for n in rest])
        tot = -(-f.shape[0] // (8 * BLOCK)) * 8 * BLOCK
        return jnp.pad(f, (0, tot - f.shape[0])).reshape(-1, BLOCK)

    _, d2, m2, v2 = _adamw(pack_small(w), pack_small(grads), pack_small(m), pack_small(v), name="adamw_small")
    off = 0
    for n in rest:
        size = int(np.prod(w[n].shape))
        for dst, src in ((delta, d2), (new_m, m2), (new_v, v2)):
            dst[n] = src.reshape(-1)[off:off + size].reshape(w[n].shape)
        off += size

    grad_x = dh0[BLOCK:][None]
    return (loss_out, grad_x, *[grads[n] for n in WEIGHTS], *[delta[n] for n in WEIGHTS],
            *[new_m[n] for n in WEIGHTS], *[new_v[n] for n in WEIGHTS])
```

```python
import functools

import numpy as np
import jax
import jax.numpy as jnp
from jax import lax
from jax.experimental import pallas as pl
from jax.experimental.pallas import tpu as pltpu

F32 = jnp.float32
BF16 = jnp.bfloat16
MESH = pl.DeviceIdType.MESH

D_MODEL = 1024
SEQ = 2048
N_META = 16
BLOCK = 128
PAD_ROWS = 112
LP = PAD_ROWS + N_META + SEQ
N_CHUNK = LP // BLOCK
DEPTH = 2
D_FF = 2816
N_CHIPS = 4
FF_SHARD = D_FF // N_CHIPS
HP = 768
FP = N_CHIPS * HP
SSD_HEADS, SSD_HD, SSD_D, SSD_GROUPS, SSD_STATE, SSD_CONV = 8, 64, 512, 2, 64, 4
FOX_HEADS, FOX_HD, FOX_D = 4, 64, 256
MLA_HEADS, MLA_Q_LORA, MLA_KV_LORA, MLA_NOPE, MLA_ROPE, MLA_V, MLA_D = 4, 256, 128, 64, 32, 64, 256
ROPE_HALF = MLA_ROPE // 2
ROPE_THETA = 10000.0
N_IN = 2476
IN_SHARD = N_IN // N_CHIPS
IN_SHARD_P = 640
ALPHA = (2 * DEPTH) ** 0.25
EPS = 1e-5
ADAM_LR, ADAM_B1, ADAM_B2, ADAM_EPS, ADAM_WD, ADAM_STEP = 0.001, 0.9, 0.999, 1e-08, 0.01, 10
NEG = -1e30
TM = 544

VMEM_LIMIT_BYTES = 56 * 1024 * 1024

PC_Z, PC_XBC, PC_FQ, PC_FK, PC_FV, PC_CQ, PC_CKV, PC_DT, PC_FR, PC_KR, PC_END = (
    0, 512, 1280, 1536, 1792, 2048, 2304, 2432, 2560, 2688, 2816)
OC_Z, OC_XBC, OC_DT, OC_FQ, OC_FK, OC_FV, OC_FR, OC_CQ, OC_CKV, OC_KR = (
    0, 512, 1280, 1288, 1544, 1800, 2056, 2060, 2316, 2444)


def _cparams(sem=None):
    return pltpu.CompilerParams(dimension_semantics=sem, vmem_limit_bytes=VMEM_LIMIT_BYTES)


def _tile(n, cap, mult):
    best = None
    for t in range(mult, min(n, cap) + 1, mult):
        if n % t == 0:
            best = t
    return best if best is not None else n


def _bs(shape, fn):
    return pl.BlockSpec(shape, fn)


ANY = pl.BlockSpec(memory_space=pl.ANY)


def _dims(ca, cb):
    return (((ca,), (cb,)), ((), ()))


def _raw_bdot(a, b, ca, cb):
    return lax.dot_general(a.astype(BF16), b.astype(BF16), _dims(ca, cb), preferred_element_type=F32)


@functools.partial(jax.custom_vjp, nondiff_argnums=(2, 3))
def _bdot(a, b, ca, cb):
    return _raw_bdot(a, b, ca, cb)


def _bdot_fwd(a, b, ca, cb):
    return _raw_bdot(a, b, ca, cb), (a, b)


def _bdot_bwd(ca, cb, res, g):
    a, b = res
    if (ca, cb) == (1, 0):
        return _raw_bdot(g, b, 1, 1), _raw_bdot(a, g, 0, 0)
    if (ca, cb) == (1, 1):
        return _raw_bdot(g, b, 1, 0), _raw_bdot(g, a, 0, 0)
    if (ca, cb) == (0, 0):
        return _raw_bdot(b, g, 1, 1), _raw_bdot(a, g, 1, 0)
    raise NotImplementedError((ca, cb))


_bdot.defvjp(_bdot_fwd, _bdot_bwd)


def _mm_core(a, b, *, a_spec, b_spec, o_spec, grid, out_shape, ca, cb, name, add=None):
    nk = grid[2]
    has_add = add is not None
    acc_shape = tuple(d for d in o_spec.block_shape if d is not None)

    def body(*refs):
        a_ref, b_ref = refs[0], refs[1]
        add_ref = refs[2] if has_add else None
        o_ref, acc_ref = refs[-2], refs[-1]
        k = pl.program_id(2)

        @pl.when(k == 0)
        def _():
            acc_ref[...] = jnp.zeros_like(acc_ref)

        acc_ref[...] += _raw_bdot(a_ref[...], b_ref[...], ca, cb)

        @pl.when(k == nk - 1)
        def _():
            r = acc_ref[...]
            if has_add:
                r = r + add_ref[...]
            o_ref[...] = r

    ins = [a, b] + ([add] if has_add else [])
    in_specs = [a_spec, b_spec] + ([o_spec] if has_add else [])
    return pl.pallas_call(
        body, name=name, grid=grid, in_specs=in_specs, out_specs=o_spec,
        out_shape=jax.ShapeDtypeStruct(out_shape, F32), scratch_shapes=[pltpu.VMEM(acc_shape, F32)],
        compiler_params=_cparams(("parallel", "parallel", "arbitrary")),
    )(*ins)


MM_VMEM_BUDGET = 40 * 1024 * 1024


def _divisors(n, mult):
    return [t for t in range(mult, n + 1, mult) if n % t == 0] or [n]


def _pick_tiles(M, N, K, a_bytes, b_bytes, ta, has_add):
    best = None
    for tm in _divisors(M, 128 if ta else 16):
        for tn in _divisors(N, 128):
            vmem = 2 * tm * K * a_bytes + 2 * K * tn * b_bytes + (3 + 2 * int(has_add)) * tm * tn * 4
            if vmem <= MM_VMEM_BUDGET:
                key = ((M // tm) * (N // tn), -tn)
                if best is None or key < best[0]:
                    best = (key, tm, tn)
    assert best is not None, (M, N, K)
    return best[1], best[2], K


def _mm(a, b, *, ta=False, tb=False, add=None, name):
    if ta:
        K, M = a.shape
    else:
        M, K = a.shape
    if tb:
        N, Kb = b.shape
    else:
        Kb, N = b.shape
    assert K == Kb, (a.shape, b.shape, ta, tb)
    tm, tn, tk = _pick_tiles(M, N, K, a.dtype.itemsize, b.dtype.itemsize, ta, add is not None)
    a_spec = _bs((tk, tm), lambda i, j, k: (k, i)) if ta else _bs((tm, tk), lambda i, j, k: (i, k))
    b_spec = _bs((tn, tk), lambda i, j, k: (j, k)) if tb else _bs((tk, tn), lambda i, j, k: (k, j))
    return _mm_core(a, b, a_spec=a_spec, b_spec=b_spec, o_spec=_bs((tm, tn), lambda i, j, k: (i, j)),
                    grid=(M // tm, N // tn, K // tk), out_shape=(M, N), ca=0 if ta else 1, cb=1 if tb else 0,
                    name=name, add=add)


def _row_entry(r, ncol):
    if isinstance(r, tuple):
        return r
    return r, r.shape[1] // ncol, 0


def _rowwise(fn, rows, pars, out_cols, *, name, tile, ncol=1, out_dtypes=None):
    rows = [_row_entry(r, ncol) for r in rows]
    L = rows[0][0].shape[0]
    nr, npar = len(rows), len(pars)
    in_specs = [_bs((tile, w), lambda g, i, o=o: (i, o + g)) for _, w, o in rows]
    in_specs += [_bs((p.shape[0], p.shape[1] // ncol), lambda g, i: (0, g)) for p in pars]
    out_specs = [_bs((tile, c // ncol), lambda g, i: (i, g)) for c in out_cols]

    def body(*refs):
        ins, outs = refs[:nr + npar], refs[nr + npar:]
        row0 = pl.program_id(1) * tile
        res = fn(row0, *[r[...] for r in ins])
        for o, v in zip(outs, res):
            o[...] = v.astype(o.dtype)

    return pl.pallas_call(
        body, name=name, grid=(ncol, L // tile), in_specs=in_specs, out_specs=out_specs,
        out_shape=[jax.ShapeDtypeStruct((L, c), d) for c, d in zip(out_cols, out_dtypes or [F32] * len(out_cols))],
        compiler_params=_cparams(("parallel", "parallel")),
    )(*[r[0] for r in rows], *pars)


def _rowwise_bwd(fn, rows, pars, douts, *, name, tile, ncol=1, row_grad=None, grad_dtypes=None):
    rows = [_row_entry(r, ncol) for r in rows]
    L = rows[0][0].shape[0]
    nr, npar, nd = len(rows), len(pars), len(douts)
    row_grad = [True] * nr if row_grad is None else row_grad
    in_specs = [_bs((tile, w), lambda g, i, o=o: (i, o + g)) for _, w, o in rows]
    in_specs += [_bs((p.shape[0], p.shape[1] // ncol), lambda g, i: (0, g)) for p in pars]
    in_specs += [_bs((tile, d.shape[1] // ncol), lambda g, i: (i, g)) for d in douts]
    g_widths = [w * ncol for (_, w, _), f in zip(rows, row_grad) if f]
    out_specs = [_bs((tile, w // ncol), lambda g, i: (i, g)) for w in g_widths]
    out_specs += [_bs((p.shape[0], p.shape[1] // ncol), lambda g, i: (0, g)) for p in pars]
    out_shape = [jax.ShapeDtypeStruct((L, w), d) for w, d in zip(g_widths, grad_dtypes or [F32] * len(g_widths))]
    out_shape += [jax.ShapeDtypeStruct(p.shape, F32) for p in pars]

    def body(*refs):
        ins = refs[:nr + npar]
        dos = refs[nr + npar:nr + npar + nd]
        outs = refs[nr + npar + nd:]
        i = pl.program_id(1)
        row0 = i * tile
        _, vjp = jax.vjp(lambda *a: tuple(fn(row0, *a)), *[r[...] for r in ins])
        grads = vjp(tuple(d[...].astype(F32) for d in dos))
        o = 0
        for j in range(nr):
            if row_grad[j]:
                outs[o][...] = grads[j].astype(outs[o].dtype)
                o += 1
        for j in range(npar):
            g, ref = grads[nr + j], outs[o + j]

            @pl.when(i == 0)
            def _(g=g, ref=ref):
                ref[...] = g

            @pl.when(i > 0)
            def _(g=g, ref=ref):
                ref[...] += g

    res = pl.pallas_call(
        body, name=name, grid=(ncol, L // tile), in_specs=in_specs, out_specs=out_specs, out_shape=out_shape,
        compiler_params=_cparams(("parallel", "arbitrary")),
    )(*[r[0] for r in rows], *pars, *douts)
    return res[:len(g_widths)], res[len(g_widths):]


def _row_ids(row0, shape):
    return row0 + lax.broadcasted_iota(jnp.int32, shape, 0)


def _sigmoid(x):
    return 1.0 / (1.0 + jnp.exp(-x))


@jax.custom_vjp
def _softplus(x):
    return jnp.maximum(x, 0.0) + jnp.log(1.0 + jnp.exp(-jnp.abs(x)))


def _softplus_fwd(x):
    return _softplus(x), x


def _softplus_bwd(x, g):
    return (g * _sigmoid(x),)


_softplus.defvjp(_softplus_fwd, _softplus_bwd)


def _silu(x):
    return x * _sigmoid(x)


def _swiglu_fn(row0, g, u):
    return (_silu(g) * u,)


def _make_res_ln_fn(scale):
    def fn(row0, h, o, gam, bet):
        pre = ALPHA * h + scale * o
        mu = jnp.mean(pre, axis=-1, keepdims=True)
        xc = pre - mu
        var = jnp.mean(xc * xc, axis=-1, keepdims=True)
        return (xc * lax.rsqrt(var + EPS) * gam + bet,)
    return fn


def _ssd_pre_fn(row0, raw, bias):
    dt = _softplus(raw + bias)
    return (jnp.where(_row_ids(row0, raw.shape) >= PAD_ROWS, dt, 0.0),)


def _ssd_post_fn(row0, y, xs, z, dskip, normg):
    v = (y + dskip * xs) * _silu(z)
    v = v * lax.rsqrt(jnp.mean(v * v, axis=-1, keepdims=True) + EPS)
    return (v * normg,)


def _mla_norm_fn(row0, cq, ckv, gq, gkv):
    qn = cq * lax.rsqrt(jnp.mean(cq * cq, axis=-1, keepdims=True) + EPS) * gq
    cn = ckv * lax.rsqrt(jnp.mean(ckv * ckv, axis=-1, keepdims=True) + EPS) * gkv
    return qn, cn


def _rope_fn(row0, q, k, cosf, sins):
    return (q * cosf + pltpu.roll(q, 64, 1) * sins, k * cosf + pltpu.roll(k, 64, 1) * sins)


def _rope_t_fn(row0, gq, gk, cosf, sins):
    return (gq * cosf + pltpu.roll(gq * sins, 64, 1), gk * cosf + pltpu.roll(gk * sins, 64, 1))


def _conv_fwd(x, x_off, w, b, *, name):
    C = w.shape[1]

    def body(x_ref, w_ref, b_ref, o_ref):
        rows = lax.broadcasted_iota(jnp.int32, (LP, BLOCK), 0)
        xv = jnp.where(rows >= PAD_ROWS, x_ref[...], 0.0)
        acc = b_ref[...] + w_ref[3:4, :] * xv
        for k in range(SSD_CONV - 1):
            acc = acc + w_ref[k:k + 1, :] * pltpu.roll(xv, SSD_CONV - 1 - k, 0)
        o_ref[...] = _silu(acc)

    return pl.pallas_call(
        body, name=name, grid=(C // BLOCK,),
        in_specs=[_bs((LP, BLOCK), lambda j: (0, j + x_off)), _bs((SSD_CONV, BLOCK), lambda j: (0, j)),
                  _bs((1, BLOCK), lambda j: (0, j))],
        out_specs=_bs((LP, BLOCK), lambda j: (0, j)),
        out_shape=jax.ShapeDtypeStruct((LP, C), F32), compiler_params=_cparams(("parallel",)),
    )(x, w, b)


def _conv_bwd(x, x_off, w, b, dout, *, name):
    C = w.shape[1]

    def body(x_ref, w_ref, b_ref, do_ref, dx_ref, dw_ref, db_ref):
        rows = lax.broadcasted_iota(jnp.int32, (LP, BLOCK), 0)
        real = rows >= PAD_ROWS
        xv = jnp.where(real, x_ref[...], 0.0)
        shifted = [pltpu.roll(xv, SSD_CONV - 1 - k, 0) for k in range(SSD_CONV - 1)] + [xv]
        acc = b_ref[...]
        for k in range(SSD_CONV):
            acc = acc + w_ref[k:k + 1, :] * shifted[k]
        sig = _sigmoid(acc)
        dacc = jnp.where(real, do_ref[...] * (sig * (1.0 + acc * (1.0 - sig))), 0.0)
        db_ref[...] = jnp.sum(dacc, axis=0, keepdims=True)
        dx = w_ref[3:4, :] * dacc
        for k in range(SSD_CONV):
            dw_ref[k:k + 1, :] = jnp.sum(dacc * shifted[k], axis=0, keepdims=True)
            if k < SSD_CONV - 1:
                dx = dx + w_ref[k:k + 1, :] * pltpu.roll(dacc, LP - (SSD_CONV - 1 - k), 0)
        dx_ref[...] = jnp.where(real, dx, 0.0)

    return pl.pallas_call(
        body, name=name, grid=(C // BLOCK,),
        in_specs=[_bs((LP, BLOCK), lambda j: (0, j + x_off)), _bs((SSD_CONV, BLOCK), lambda j: (0, j)),
                  _bs((1, BLOCK), lambda j: (0, j)), _bs((LP, BLOCK), lambda j: (0, j))],
        out_specs=[_bs((LP, BLOCK), lambda j: (0, j)), _bs((SSD_CONV, BLOCK), lambda j: (0, j)),
                   _bs((1, BLOCK), lambda j: (0, j))],
        out_shape=[jax.ShapeDtypeStruct((LP, C), F32), jax.ShapeDtypeStruct((SSD_CONV, C), F32),
                   jax.ShapeDtypeStruct((1, C), F32)],
        compiler_params=_cparams(("parallel",)),
    )(x, w, b, dout)


_BDIMS = {"nn": (((2,), (1,)), ((0,), (0,))), "nt": (((2,), (2,)), ((0,), (0,))), "tn": (((1,), (1,)), ((0,), (0,)))}


def _raw_bdot3(a, b, mode):
    return lax.dot_general(a.astype(BF16), b.astype(BF16), _BDIMS[mode], preferred_element_type=F32)


@functools.partial(jax.custom_vjp, nondiff_argnums=(2,))
def _bdot3(a, b, mode):
    return _raw_bdot3(a, b, mode)


def _bdot3_fwd(a, b, mode):
    return _raw_bdot3(a, b, mode), (a, b)


def _bdot3_bwd(mode, res, g):
    a, b = res
    if mode == "nn":
        return _raw_bdot3(g, b, "nt"), _raw_bdot3(a, g, "tn")
    if mode == "nt":
        return _raw_bdot3(g, b, "nn"), _raw_bdot3(g, a, "tn")
    return _raw_bdot3(b, g, "nt"), _raw_bdot3(a, g, "nn")


_bdot3.defvjp(_bdot3_fwd, _bdot3_bwd)


def _ssd_chunk(x, bm, cm, dtc, dtr, alog, prev):
    rep = SSD_HEADS // SSD_GROUPS
    per_head = lambda t: jnp.broadcast_to(t[:, None], (SSD_GROUPS, rep) + t.shape[1:]).reshape((SSD_HEADS,) + t.shape[1:])
    bm, cm = per_head(bm), per_head(cm)
    lane = lax.broadcasted_iota(jnp.int32, alog.shape, 2)
    a_neg = -jnp.exp(jnp.sum(jnp.where(lane == 0, alog, 0.0), axis=2, keepdims=True))
    ac_in = dtc * a_neg
    ar_in = dtr * a_neg
    li = lax.broadcasted_iota(jnp.int32, (1, BLOCK, BLOCK), 1)
    si = lax.broadcasted_iota(jnp.int32, (1, BLOCK, BLOCK), 2)
    causal = li >= si
    acum_c = jnp.sum(jnp.where(causal, ar_in, 0.0), axis=2, keepdims=True)
    acum_r = jnp.sum(jnp.where(li <= si, ac_in, 0.0), axis=1, keepdims=True)
    total = jnp.sum(ar_in, axis=2, keepdims=True)
    seg = jnp.exp(jnp.where(causal, acum_c - acum_r, NEG))
    xdt = x * dtc
    cb = _bdot3(cm, bm, "nt")
    y = _bdot3(cb * seg, xdt, "nn") + _bdot3(cm, prev, "nt") * jnp.exp(acum_c)
    st = _bdot3(xdt, bm * jnp.exp(total - acum_c), "tn")
    return y, prev * jnp.exp(total) + st


def _ssd_specs(rev):
    ci = (lambda c: N_CHUNK - 1 - c) if rev else (lambda c: c)
    x_spec = _bs((SSD_HEADS, BLOCK, SSD_HD), lambda c: (0, ci(c), 0))
    g_spec = _bs((SSD_GROUPS, BLOCK, SSD_STATE), lambda c: (0, ci(c), 0))
    dtc_spec = _bs((SSD_HEADS, BLOCK, 1), lambda c: (0, ci(c), 0))
    dtr_spec = _bs((SSD_HEADS, 1, BLOCK), lambda c: (0, 0, ci(c)))
    al_spec = _bs((SSD_HEADS, 1, BLOCK), lambda c: (0, 0, 0))
    st_spec = _bs((None, SSD_HEADS, SSD_HD, SSD_STATE), lambda c: (ci(c), 0, 0, 0))
    return x_spec, g_spec, dtc_spec, dtr_spec, al_spec, st_spec


def _ssd_fwd(x, bm, cm, dtc, dtr, alog, *, name):
    x_spec, g_spec, dtc_spec, dtr_spec, al_spec, st_spec = _ssd_specs(False)

    def body(x_ref, b_ref, c_ref, dtc_ref, dtr_ref, al_ref, y_ref, prev_ref, state):
        @pl.when(pl.program_id(0) == 0)
        def _():
            state[...] = jnp.zeros_like(state)

        prev = state[...]
        prev_ref[...] = prev
        y, new = _ssd_chunk(x_ref[...], b_ref[...], c_ref[...], dtc_ref[...], dtr_ref[...], al_ref[...], prev)
        y_ref[...] = y
        state[...] = new

    return pl.pallas_call(
        body, name=name, grid=(N_CHUNK,),
        in_specs=[x_spec, g_spec, g_spec, dtc_spec, dtr_spec, al_spec], out_specs=[x_spec, st_spec],
        out_shape=[jax.ShapeDtypeStruct((SSD_HEADS, LP, SSD_HD), F32),
                   jax.ShapeDtypeStruct((N_CHUNK, SSD_HEADS, SSD_HD, SSD_STATE), F32)],
        scratch_shapes=[pltpu.VMEM((SSD_HEADS, SSD_HD, SSD_STATE), F32)],
        compiler_params=_cparams(("arbitrary",)),
    )(x, bm, cm, dtc, dtr, alog)


def _ssd_bwd(x, bm, cm, dtc, dtr, alog, prevs, dy, *, name):
    x_spec, g_spec, dtc_spec, dtr_spec, al_spec, st_spec = _ssd_specs(True)

    def body(x_ref, b_ref, c_ref, dtc_ref, dtr_ref, al_ref, prev_ref, dy_ref,
             dx_ref, db_ref, dc_ref, ddtc_ref, ddtr_ref, dal_ref, dstate):
        c = pl.program_id(0)

        @pl.when(c == 0)
        def _():
            dstate[...] = jnp.zeros_like(dstate)

        _, vjp = jax.vjp(_ssd_chunk, x_ref[...], b_ref[...], c_ref[...], dtc_ref[...], dtr_ref[...], al_ref[...],
                         prev_ref[...])
        dx, db, dc, ddtc, ddtr, dal, dprev = vjp((dy_ref[...], dstate[...]))
        dx_ref[...] = dx
        db_ref[...] = db
        dc_ref[...] = dc
        ddtc_ref[...] = ddtc
        ddtr_ref[...] = ddtr
        dstate[...] = dprev

        @pl.when(c == 0)
        def _():
            dal_ref[...] = dal

        @pl.when(c > 0)
        def _():
            dal_ref[...] += dal

    hs = jax.ShapeDtypeStruct((SSD_HEADS, LP, SSD_HD), F32)
    gs = jax.ShapeDtypeStruct((SSD_GROUPS, LP, SSD_STATE), F32)
    return pl.pallas_call(
        body, name=name, grid=(N_CHUNK,),
        in_specs=[x_spec, g_spec, g_spec, dtc_spec, dtr_spec, al_spec, st_spec, x_spec],
        out_specs=[x_spec, g_spec, g_spec, dtc_spec, dtr_spec, al_spec],
        out_shape=[hs, gs, gs, jax.ShapeDtypeStruct((SSD_HEADS, LP, 1), F32),
                   jax.ShapeDtypeStruct((SSD_HEADS, 1, LP), F32), jax.ShapeDtypeStruct((SSD_HEADS, 1, BLOCK), F32)],
        scratch_shapes=[pltpu.VMEM((SSD_HEADS, SSD_HD, SSD_STATE), F32)],
        compiler_params=_cparams(("arbitrary",)),
    )(x, bm, cm, dtc, dtr, alog, prevs, dy)


def _tri_dot(tri, v):
    hi = v.astype(BF16)
    r1 = v - hi.astype(F32)
    mid = r1.astype(BF16)
    lo = (r1 - mid.astype(F32)).astype(BF16)
    t = tri.astype(BF16)
    d = lambda p: lax.dot_general(t, p, _dims(1, 0), preferred_element_type=F32)
    return d(hi) + d(mid) + d(lo)


def _fox_gate_fwd(raw, raw_blk, bias, *, name):
    def body(raw_ref, b_ref, c_ref, ct_ref, carry):
        j = pl.program_id(0)

        @pl.when(j == 0)
        def _():
            carry[...] = jnp.zeros_like(carry)

        rows = j * BLOCK + lax.broadcasted_iota(jnp.int32, (BLOCK, BLOCK), 0)
        lf = jnp.where(rows >= PAD_ROWS, -_softplus(-(raw_ref[...] + b_ref[...])), 0.0)
        li = lax.broadcasted_iota(jnp.int32, (BLOCK, BLOCK), 0)
        si = lax.broadcasted_iota(jnp.int32, (BLOCK, BLOCK), 1)
        cv = _tri_dot(jnp.where(li >= si, 1.0, 0.0), lf) + carry[...]
        c_ref[...] = cv
        ct_ref[...] = cv.T
        carry[...] += jnp.sum(lf, axis=0, keepdims=True)

    return pl.pallas_call(
        body, name=name, grid=(N_CHUNK,),
        in_specs=[_bs((BLOCK, BLOCK), lambda j: (j, raw_blk)), _bs((1, BLOCK), lambda j: (0, 0))],
        out_specs=[_bs((BLOCK, BLOCK), lambda j: (j, 0)), _bs((BLOCK, BLOCK), lambda j: (0, j))],
        out_shape=[jax.ShapeDtypeStruct((LP, BLOCK), F32), jax.ShapeDtypeStruct((BLOCK, LP), F32)],
        scratch_shapes=[pltpu.VMEM((1, BLOCK), F32)], compiler_params=_cparams(("arbitrary",)),
    )(raw, bias)


def _fox_gate_bwd(raw, raw_blk, bias, dc, dct, *, name):
    rj = lambda j: N_CHUNK - 1 - j

    def body(raw_ref, b_ref, dc_ref, dct_ref, draw_ref, db_ref, carry):
        j = pl.program_id(0)

        @pl.when(j == 0)
        def _():
            carry[...] = jnp.zeros_like(carry)

        rows = (N_CHUNK - 1 - j) * BLOCK + lax.broadcasted_iota(jnp.int32, (BLOCK, BLOCK), 0)
        li = lax.broadcasted_iota(jnp.int32, (BLOCK, BLOCK), 0)
        si = lax.broadcasted_iota(jnp.int32, (BLOCK, BLOCK), 1)
        dcv = dc_ref[...] + dct_ref[...].T
        dlf = _tri_dot(jnp.where(li <= si, 1.0, 0.0), dcv) + carry[...]
        carry[...] += jnp.sum(dcv, axis=0, keepdims=True)
        draw = jnp.where(rows >= PAD_ROWS, dlf * (1.0 - _sigmoid(raw_ref[...] + b_ref[...])), 0.0)
        draw_ref[...] = draw
        dsum = jnp.sum(draw, axis=0, keepdims=True)

        @pl.when(j == 0)
        def _():
            db_ref[...] = dsum

        @pl.when(j > 0)
        def _():
            db_ref[...] += dsum

    return pl.pallas_call(
        body, name=name, grid=(N_CHUNK,),
        in_specs=[_bs((BLOCK, BLOCK), lambda j: (rj(j), raw_blk)), _bs((1, BLOCK), lambda j: (0, 0)),
                  _bs((BLOCK, BLOCK), lambda j: (rj(j), 0)), _bs((BLOCK, BLOCK), lambda j: (0, rj(j)))],
        out_specs=[_bs((BLOCK, BLOCK), lambda j: (rj(j), 0)), _bs((1, BLOCK), lambda j: (0, 0))],
        out_shape=[jax.ShapeDtypeStruct((LP, BLOCK), F32), jax.ShapeDtypeStruct((1, BLOCK), F32)],
        scratch_shapes=[pltpu.VMEM((1, BLOCK), F32)], compiler_params=_cparams(("arbitrary",)),
    )(raw, bias, dc, dct)


ATT_W = 256
ATT_QB = 272
ATT_STEPS = LP // ATT_QB
ATT_KEYS = (640, 1152, 1664, LP)


def _lane_head(width, per, mod=None):
    lane = lax.broadcasted_iota(jnp.int32, (1, width), 1)
    if mod is not None:
        lane = lane % mod
    return lane // per


def _attn_mask(i, kw):
    r = i * ATT_QB + lax.broadcasted_iota(jnp.int32, (ATT_QB, kw), 0)
    c = lax.broadcasted_iota(jnp.int32, (ATT_QB, kw), 1)
    return (c <= r) & ((c >= PAD_ROWS) | (r < PAD_ROWS))


def _attn_by_key_class(i, fn):
    for p, kw in enumerate(ATT_KEYS):
        @pl.when(i // 2 == p)
        def _(kw=kw):
            fn(kw)


def _attn_specs(q, k, v, bias, rope):
    qspec = lambda blk, w=ATT_W: _bs((ATT_QB, w), lambda i: (i, blk))
    fspec = lambda blk, w=ATT_W: _bs((LP, w), lambda i: (0, blk))
    ins = [q[0], k[0], v[0]]
    specs = [qspec(q[1]), fspec(k[1]), fspec(v[1])]
    if bias is not None:
        ins += [bias[0], bias[1]]
        specs += [qspec(0, BLOCK), _bs((BLOCK, LP), lambda i: (0, 0))]
    if rope is not None:
        ins += [rope[0][0], rope[1][0]]
        specs += [qspec(rope[0][1], BLOCK), fspec(rope[1][1], BLOCK)]
    return ins, specs, qspec, fspec


def _attn_fwd(q, k, v, *, scale, name, bias=None, rope=None):
    ins, specs, qspec, fspec = _attn_specs(q, k, v, bias, rope)
    has_bias, has_rope = bias is not None, rope is not None

    def body(*refs):
        it = iter(refs)
        q_ref, k_ref, v_ref = next(it), next(it), next(it)
        if has_bias:
            c_ref, ct_ref = next(it), next(it)
        if has_rope:
            qr_ref, kr_ref = next(it), next(it)
        o_ref, lse_ref = next(it), next(it)
        i = pl.program_id(0)

        def block(kw):
            ok = _attn_mask(i, kw)
            qv, kv, vv = q_ref[...], k_ref[0:kw, :], v_ref[0:kw, :]
            hid, l128 = _lane_head(ATT_W, FOX_HD), _lane_head(BLOCK, 1)
            if has_rope:
                rid = _lane_head(BLOCK, ROPE_HALF, 64)
                qrv, krv = qr_ref[...], kr_ref[0:kw, :]
            def head(h, carry):
                o_acc, lse_acc = carry
                s = _raw_bdot(jnp.where(hid == h, qv, 0.0), kv, 1, 1)
                if has_rope:
                    s = s + _raw_bdot(jnp.where(rid == h, qrv, 0.0), krv, 1, 1)
                s = s * scale
                if has_bias:
                    cq = jnp.sum(jnp.where(l128 == h, c_ref[...], 0.0), axis=1, keepdims=True)
                    s = s + (cq - ct_ref[pl.ds(h, 1), 0:kw])
                s = jnp.where(ok, s, NEG)
                m = jnp.max(s, axis=1, keepdims=True)
                p = jnp.exp(s - m)
                l = jnp.sum(p, axis=1, keepdims=True)
                o_acc = jnp.where(hid == h, _raw_bdot(p, vv, 1, 0) / l, o_acc)
                lse_acc = jnp.where(l128 == h, m + jnp.log(l), lse_acc)
                return o_acc, lse_acc

            o_acc, lse_acc = lax.fori_loop(
                0, FOX_HEADS, head, (jnp.zeros((ATT_QB, ATT_W), F32), jnp.zeros((ATT_QB, BLOCK), F32)), unroll=True)
            o_ref[...] = o_acc
            lse_ref[...] = lse_acc

        _attn_by_key_class(i, block)

    return pl.pallas_call(
        body, name=name, grid=(ATT_STEPS,), in_specs=specs, out_specs=[qspec(0), qspec(0, BLOCK)],
        out_shape=[jax.ShapeDtypeStruct((LP, ATT_W), F32), jax.ShapeDtypeStruct((LP, BLOCK), F32)],
        compiler_params=_cparams(("parallel",)),
    )(*ins)


def _attn_bwd(q, k, v, o, lse, do, *, scale, name, bias=None, rope=None):
    ins, specs, qspec, fspec = _attn_specs(q, k, v, bias, rope)
    has_bias, has_rope = bias is not None, rope is not None
    ins += [o, lse, do[0]]
    specs += [qspec(0), qspec(0, BLOCK), qspec(do[1])]

    def body(*refs):
        it = iter(refs)
        q_ref, k_ref, v_ref = next(it), next(it), next(it)
        if has_bias:
            c_ref, ct_ref = next(it), next(it)
        if has_rope:
            qr_ref, kr_ref = next(it), next(it)
        o_ref, lse_ref, do_ref = next(it), next(it), next(it)
        dq_ref, dk_ref, dv_ref = next(it), next(it), next(it)
        if has_bias:
            dc_ref, dct_ref = next(it), next(it)
        if has_rope:
            dqr_ref, dkr_ref = next(it), next(it)
        i = pl.program_id(0)

        @pl.when(i == 0)
        def _():
            dk_ref[...] = jnp.zeros_like(dk_ref)
            dv_ref[...] = jnp.zeros_like(dv_ref)
            if has_rope:
                dkr_ref[...] = jnp.zeros_like(dkr_ref)
            if has_bias:
                dct_ref[...] = jnp.zeros_like(dct_ref)

        def block(kw):
            ok = _attn_mask(i, kw)
            qv, kv, vv = q_ref[...], k_ref[0:kw, :], v_ref[0:kw, :]
            ov, dov, lsev = o_ref[...], do_ref[...], lse_ref[...]
            hid, l128 = _lane_head(ATT_W, FOX_HD), _lane_head(BLOCK, 1)
            if has_rope:
                rid = _lane_head(BLOCK, ROPE_HALF, 64)
                qrv, krv = qr_ref[...], kr_ref[0:kw, :]

            def head(h, carry):
                dq_acc, aux_acc = carry
                qm = jnp.where(hid == h, qv, 0.0)
                s = _raw_bdot(qm, kv, 1, 1)
                if has_rope:
                    qrm = jnp.where(rid == h, qrv, 0.0)
                    s = s + _raw_bdot(qrm, krv, 1, 1)
                s = s * scale
                if has_bias:
                    cq = jnp.sum(jnp.where(l128 == h, c_ref[...], 0.0), axis=1, keepdims=True)
                    s = s + (cq - ct_ref[pl.ds(h, 1), 0:kw])
                s = jnp.where(ok, s, NEG)
                p = jnp.exp(s - jnp.sum(jnp.where(l128 == h, lsev, 0.0), axis=1, keepdims=True))
                dom = jnp.where(hid == h, dov, 0.0)
                dp = _raw_bdot(dom, vv, 1, 1)
                delta = jnp.sum(dom * ov, axis=1, keepdims=True)
                ds = p * (dp - delta)
                dq_acc = jnp.where(hid == h, _raw_bdot(ds, kv, 1, 0) * scale, dq_acc)
                dk_ref[0:kw, :] += _raw_bdot(ds, qm, 0, 0) * scale
                dv_ref[0:kw, :] += _raw_bdot(p, dom, 0, 0)
                if has_rope:
                    aux_acc = jnp.where(rid == h, _raw_bdot(ds, krv, 1, 0) * scale, aux_acc)
                    dkr_ref[0:kw, :] += _raw_bdot(ds, qrm, 0, 0) * scale
                if has_bias:
                    aux_acc = jnp.where(l128 == h, jnp.sum(ds, axis=1, keepdims=True), aux_acc)
                    dct_ref[pl.ds(h, 1), 0:kw] -= jnp.sum(ds, axis=0, keepdims=True)
                return dq_acc, aux_acc

            dq_acc, aux_acc = lax.fori_loop(
                0, FOX_HEADS, head, (jnp.zeros((ATT_QB, ATT_W), F32), jnp.zeros((ATT_QB, BLOCK), F32)))
            dq_ref[...] = dq_acc
            if has_bias:
                dc_ref[...] = aux_acc
            if has_rope:
                dqr_ref[...] = aux_acc

        _attn_by_key_class(i, block)

    wide = jax.ShapeDtypeStruct((LP, ATT_W), F32)
    narrow = jax.ShapeDtypeStruct((LP, BLOCK), F32)
    out_specs = [qspec(0), fspec(0), fspec(0)]
    out_shape = [wide, wide, wide]
    if has_bias:
        out_specs += [qspec(0, BLOCK), _bs((BLOCK, LP), lambda i: (0, 0))]
        out_shape += [narrow, jax.ShapeDtypeStruct((BLOCK, LP), F32)]
    if has_rope:
        out_specs += [qspec(0, BLOCK), fspec(0, BLOCK)]
        out_shape += [narrow, narrow]
    return pl.pallas_call(
        body, name=name, grid=(ATT_STEPS,), in_specs=specs, out_specs=out_specs, out_shape=out_shape,
        compiler_params=_cparams(("arbitrary",)),
    )(*ins)


def _loss_head(y, target, *, name):
    tile = 272

    def body(y_ref, t_ref, dy_ref, loss_ref):
        i = pl.program_id(0)
        rows = i * tile + lax.broadcasted_iota(jnp.int32, (tile, D_MODEL), 0)
        err = jnp.where(rows >= BLOCK, y_ref[...] - t_ref[...], 0.0)
        dy_ref[...] = err * (1.0 / D_MODEL)
        part = 0.5 * jnp.sum(jnp.sum(err * err, axis=1, keepdims=True) * (1.0 / D_MODEL), axis=0, keepdims=True)
        part = jnp.broadcast_to(part, (1, BLOCK))

        @pl.when(i == 0)
        def _():
            loss_ref[...] = part

        @pl.when(i > 0)
        def _():
            loss_ref[...] += part

    return pl.pallas_call(
        body, name=name, grid=(LP // tile,),
        in_specs=[_bs((tile, D_MODEL), lambda i: (i, 0)), _bs((tile, D_MODEL), lambda i: (i, 0))],
        out_specs=[_bs((tile, D_MODEL), lambda i: (i, 0)), _bs((1, BLOCK), lambda i: (0, 0))],
        out_shape=[jax.ShapeDtypeStruct((LP, D_MODEL), F32), jax.ShapeDtypeStruct((1, BLOCK), F32)],
        compiler_params=_cparams(("arbitrary",)),
    )(y, target)


def _adamw(w, gs, m, v, *, name):
    if w.ndim == 2:
        w, m, v = w[None], m[None], v[None]
        squeeze = True
    else:
        squeeze = False
    NL, R, C = w.shape
    assert len(gs) == NL
    CG = gs[0].shape[1]
    tile = _tile(R, 256, 8)

    def body(*refs):
        w_ref, g_refs = refs[0], refs[1:1 + NL]
        m_ref, v_ref, go_ref, d_ref, nm_ref, nv_ref = refs[1 + NL:]
        gv = g_refs[0][:, :C]
        for j in range(1, NL):
            gv = jnp.where(pl.program_id(0) == j, g_refs[j][:, :C], gv)
        nm = ADAM_B1 * m_ref[...] + (1.0 - ADAM_B1) * gv
        nv = ADAM_B2 * v_ref[...] + (1.0 - ADAM_B2) * (gv * gv)
        m_hat = nm / (1.0 - ADAM_B1 ** ADAM_STEP)
        v_hat = nv / (1.0 - ADAM_B2 ** ADAM_STEP)
        go_ref[...] = gv
        d_ref[...] = -ADAM_LR * (m_hat / (jnp.sqrt(v_hat) + ADAM_EPS) + ADAM_WD * w_ref[...])
        nm_ref[...] = nm
        nv_ref[...] = nv

    spec = _bs((None, tile, C), lambda l, i: (l, i, 0))
    gspecs = [_bs((tile, CG), lambda l, i, j=j: (jnp.where(l == j, i, 0), 0)) for j in range(NL)]
    res = pl.pallas_call(
        body, name=name, grid=(NL, R // tile), in_specs=[spec, *gspecs, spec, spec], out_specs=[spec] * 4,
        out_shape=[jax.ShapeDtypeStruct((NL, R, C), F32)] * 4, compiler_params=_cparams(("parallel", "parallel")),
    )(w, *gs, m, v)
    return [r[0] for r in res] if squeeze else res


def _my_pos():
    return lax.axis_index("x"), lax.axis_index("y"), lax.axis_index("c")


def _other_chips(x, y):
    return [(1 - x, y), (x, 1 - y), (1 - x, 1 - y)]


def _allgather_chips(shards):
    n = len(shards)
    per = 7

    def body(*refs):
        ins, outs = refs[:n], refs[n:2 * n]
        send_sems, recv_sems = refs[2 * n], refs[2 * n + 1]
        x, y, c = _my_pos()
        chips = _other_chips(x, y)
        sibling, me = (x, y, 1 - c), 2 * x + y

        def cp(a, kk, src, dst, to):
            return pltpu.make_async_remote_copy(src_ref=src, dst_ref=dst, send_sem=send_sems.at[per * a + kk],
                                                recv_sem=recv_sems.at[per * a + kk], device_id=to, device_id_type=MESH)

        sends = []
        for a in range(n):
            for j, chip in enumerate(chips):
                sends.append(cp(a, j, ins[a].at[c], outs[a].at[me, c], (*chip, c)))
            sends.append(cp(a, 3, ins[a], outs[a].at[me], sibling))
        for s in sends:
            s.start()
        for a in range(n):
            for j, chip in enumerate(chips):
                slab = outs[a].at[2 * chip[0] + chip[1], c]
                cp(a, j, slab, slab, (x, y, c)).wait_recv()
                fwd = cp(a, 4 + j, slab, slab, sibling)
                fwd.start()
                sends.append(fwd)
        for a in range(n):
            cp(a, 3, ins[a], outs[a].at[me], (x, y, c)).wait_recv()
            for j, chip in enumerate(chips):
                slab = outs[a].at[2 * chip[0] + chip[1], 1 - c]
                cp(a, 4 + j, slab, slab, (x, y, c)).wait_recv()
        for s in sends:
            s.wait_send()

    return pl.pallas_call(
        body, name="allgather_chips", in_specs=[ANY] * n, out_specs=[ANY] * n,
        out_shape=[jax.ShapeDtypeStruct((N_CHIPS,) + s.shape, s.dtype) for s in shards],
        scratch_shapes=[pltpu.SemaphoreType.DMA((per * n,)), pltpu.SemaphoreType.DMA((per * n,))],
    )(*shards)


def _rs_swap_rows(gs, tag):
    n = len(gs)

    def body(*refs):
        ins, outs = refs[:n], refs[n:2 * n]
        send_sems, recv_sems = refs[2 * n], refs[2 * n + 1]
        x, y, c = _my_pos()
        cps = []
        for a in range(n):
            half = ins[a].shape[1] // 2
            cps.append(pltpu.make_async_remote_copy(
                src_ref=ins[a].at[:, pl.ds((1 - c) * half, half)], dst_ref=outs[a], send_sem=send_sems.at[a],
                recv_sem=recv_sems.at[a], device_id=(x, y, 1 - c), device_id_type=MESH))
        for cp in cps:
            cp.start()
        for cp in cps:
            cp.wait()

    return pl.pallas_call(
        body, name=f"rs_swap_rows_{tag}", in_specs=[ANY] * n, out_specs=[ANY] * n,
        out_shape=[jax.ShapeDtypeStruct((N_CHIPS, g.shape[1] // 2, g.shape[2]), g.dtype) for g in gs],
        scratch_shapes=[pltpu.SemaphoreType.DMA((n,)), pltpu.SemaphoreType.DMA((n,))],
    )(*gs)


def _rs_add_pair(g, r, pos, *, name):
    _, H, C = r.shape
    tile = _tile(H, 512, 16)
    nt = H // tile

    def body(pos_ref, g_ref, r_ref, o32_ref, o16_ref):
        s = g_ref[...] + r_ref[...]
        o32_ref[...] = s
        o16_ref[...] = s.astype(BF16)

    spec = _bs((None, tile, C), lambda k, i, pos_ref: (k, i, 0))
    grid_spec = pltpu.PrefetchScalarGridSpec(
        num_scalar_prefetch=1, grid=(N_CHIPS, nt),
        in_specs=[_bs((None, tile, C), lambda k, i, pos_ref: (k, pos_ref[1] * nt + i, 0)), spec],
        out_specs=[spec, spec])
    return pl.pallas_call(
        body, name=name, grid_spec=grid_spec,
        out_shape=[jax.ShapeDtypeStruct((N_CHIPS, H, C), F32), jax.ShapeDtypeStruct((N_CHIPS, H, C), BF16)],
        compiler_params=_cparams(("parallel", "parallel")),
    )(pos, g, r)


def _exchange_copies(srcs, lands, send_sems, recv_sems):
    x, y, c = _my_pos()
    starts, landing = [], []
    for a in range(len(srcs)):
        for j, chip in enumerate(_other_chips(x, y)):
            sems = dict(send_sem=send_sems.at[3 * a + j], recv_sem=recv_sems.at[3 * a + j], device_id_type=MESH)
            starts.append(pltpu.make_async_remote_copy(
                src_ref=srcs[a].at[2 * chip[0] + chip[1]], dst_ref=lands[a].at[j], device_id=(*chip, c), **sems))
            landing.append(pltpu.make_async_remote_copy(
                src_ref=lands[a].at[j], dst_ref=lands[a].at[j], device_id=(x, y, c), **sems))
    return starts, landing


def _gather_copies(srcs, lands, l, send_sems, recv_sems):
    x, y, c = _my_pos()
    me = 2 * x + y
    starts, landing = [], []
    for a in range(len(srcs)):
        half = srcs[a].shape[1] // 2
        mine = pl.ds(c * half, half)
        for j, chip in enumerate(_other_chips(x, y)):
            sems = dict(send_sem=send_sems.at[3 * a + j], recv_sem=recv_sems.at[3 * a + j], device_id_type=MESH)
            starts.append(pltpu.make_async_remote_copy(
                src_ref=srcs[a].at[l, mine], dst_ref=lands[a].at[me, mine], device_id=(*chip, c), **sems))
            slab = lands[a].at[2 * chip[0] + chip[1], mine]
            landing.append(pltpu.make_async_remote_copy(src_ref=slab, dst_ref=slab, device_id=(x, y, c), **sems))
    return starts, landing


HBM = pl.BlockSpec(memory_space=pltpu.HBM)
SEM = pl.BlockSpec(memory_space=pltpu.SEMAPHORE)


def _ici_blocking(copies_fn, srcs, land_shapes, *, name):
    n = len(srcs)

    def body(*refs):
        starts, landing = copies_fn(refs[:n], refs[n:2 * n], refs[2 * n], refs[2 * n + 1])
        for cp in starts:
            cp.start()
        for cp in landing:
            cp.wait_recv()
        for cp in starts:
            cp.wait_send()

    return pl.pallas_call(
        body, name=name, in_specs=[ANY] * n, out_specs=[ANY] * n, out_shape=land_shapes,
        scratch_shapes=[pltpu.SemaphoreType.DMA((3 * n,)), pltpu.SemaphoreType.DMA((3 * n,))],
    )(*srcs)


def _ici_start(copies_fn, srcs, land_shapes, *, name):
    n = len(srcs)

    def body(*refs):
        starts, _ = copies_fn(refs[:n], refs[n:2 * n], refs[2 * n], refs[2 * n + 1])
        for cp in starts:
            cp.start()
        refs[-1][...] = jnp.zeros_like(refs[-1])

    sems = pltpu.SemaphoreType.DMA((3 * n,))
    hbm = lambda s: pltpu.HBM(s.shape, s.dtype)
    lands = [pltpu.with_memory_space_constraint(lax.empty(s.shape, s.dtype), pltpu.HBM) for s in land_shapes]
    res = pl.pallas_call(
        body, name=name, in_specs=[HBM] * (2 * n),
        out_specs=(SEM, SEM, *[HBM] * (2 * n), pl.BlockSpec(memory_space=pltpu.VMEM)),
        out_shape=(sems, sems, *[hbm(s) for s in srcs], *[hbm(s) for s in land_shapes],
                   jax.ShapeDtypeStruct((8, BLOCK), F32)),
        input_output_aliases={i: 2 + i for i in range(2 * n)},
        compiler_params=pltpu.CompilerParams(has_side_effects=pltpu.SideEffectType.DATAFLOW_SIDE_EFFECTING),
    )(*[pltpu.with_memory_space_constraint(s, pltpu.HBM) for s in srcs], *lands)
    return res[0], res[1], list(res[2:2 + n]), list(res[2 + n:2 + 2 * n]), res[-1]


def _ici_wait(copies_fn, send_sems, recv_sems, srcs, lands, after, *, name):
    n = len(srcs)

    def body(*refs):
        starts, landing = copies_fn(refs[:n], refs[n:2 * n], refs[2 * n], refs[2 * n + 1])
        for cp in starts:
            cp.wait_send()
        for cp in landing:
            cp.wait_recv()

    hbm = lambda s: pltpu.HBM(s.shape, s.dtype)
    res = pl.pallas_call(
        body, name=name, in_specs=[*[HBM] * (2 * n), SEM, SEM, ANY], out_specs=[HBM] * (2 * n),
        out_shape=[*[hbm(s) for s in srcs], *[hbm(s) for s in lands]],
        input_output_aliases={i: i for i in range(2 * n)},
        compiler_params=pltpu.CompilerParams(has_side_effects=pltpu.SideEffectType.DATAFLOW_SIDE_EFFECTING),
    )(*srcs, *lands, send_sems, recv_sems, after)
    return list(res[n:])


def _gather_d2d(shards, lands, l, tag):
    n = len(shards)

    def body(*refs):
        ins, outs = refs[:n], refs[2 * n:3 * n]
        send_sems, recv_sems = refs[3 * n], refs[3 * n + 1]
        x, y, c = _my_pos()
        me, sibling = 2 * x + y, (x, y, 1 - c)
        starts, landing = [], []
        for a in range(n):
            half = ins[a].shape[1] // 2
            mine, theirs = pl.ds(c * half, half), pl.ds((1 - c) * half, half)
            pairs = [(ins[a].at[l], outs[a].at[me], outs[a].at[me])]
            for chip in _other_chips(x, y):
                k = 2 * chip[0] + chip[1]
                pairs.append((outs[a].at[k, mine], outs[a].at[k, mine], outs[a].at[k, theirs]))
            for j, (src, dst, lands_here) in enumerate(pairs):
                sems = dict(send_sem=send_sems.at[4 * a + j], recv_sem=recv_sems.at[4 * a + j], device_id_type=MESH)
                starts.append(pltpu.make_async_remote_copy(src_ref=src, dst_ref=dst, device_id=sibling, **sems))
                landing.append(pltpu.make_async_remote_copy(src_ref=lands_here, dst_ref=lands_here, device_id=(x, y, c),
                                                            **sems))
        for cp in starts:
            cp.start()
        for cp in landing:
            cp.wait_recv()
        for cp in starts:
            cp.wait_send()

    return pl.pallas_call(
        body, name=f"gather_d2d_{tag}", in_specs=[ANY] * (2 * n), out_specs=[ANY] * n,
        out_shape=[jax.ShapeDtypeStruct(s.shape, s.dtype) for s in lands],
        input_output_aliases={n + a: a for a in range(n)},
        scratch_shapes=[pltpu.SemaphoreType.DMA((4 * n,)), pltpu.SemaphoreType.DMA((4 * n,))],
    )(*shards, *lands)


def _rs_add_chips(p32, r16, pos, *, name):
    _, H, C = p32.shape
    tile = _tile(H, 512, 16)
    nt = H // tile

    def body(pos_ref, p_ref, r_ref, o_ref):
        o_ref[...] = ((p_ref[...] + r_ref[0].astype(F32)) + r_ref[1].astype(F32)) + r_ref[2].astype(F32)

    grid_spec = pltpu.PrefetchScalarGridSpec(
        num_scalar_prefetch=1, grid=(nt,),
        in_specs=[_bs((None, tile, C), lambda i, pos_ref: (pos_ref[0], i, 0)),
                  _bs((3, tile, C), lambda i, pos_ref: (0, i, 0))],
        out_specs=_bs((tile, C), lambda i, pos_ref: (pos_ref[1] * nt + i, 0)))
    return pl.pallas_call(
        body, name=name, grid_spec=grid_spec, out_shape=jax.ShapeDtypeStruct((2 * H, C), F32),
        compiler_params=_cparams(("parallel",)),
    )(pos, p32, r16)


def _rs_join_rows(fs, tag):
    n = len(fs)

    def body(*refs):
        outs = refs[n:2 * n]
        send_sems, recv_sems = refs[2 * n], refs[2 * n + 1]
        x, y, c = _my_pos()
        for a in range(n):
            half = outs[a].shape[0] // 2
            mine = outs[a].at[pl.ds(c * half, half)]
            pltpu.make_async_remote_copy(src_ref=mine, dst_ref=mine, send_sem=send_sems.at[a],
                                         recv_sem=recv_sems.at[a], device_id=(x, y, 1 - c), device_id_type=MESH).start()
        for a in range(n):
            half = outs[a].shape[0] // 2
            pltpu.make_async_remote_copy(
                src_ref=outs[a].at[pl.ds(c * half, half)], dst_ref=outs[a].at[pl.ds((1 - c) * half, half)],
                send_sem=send_sems.at[a], recv_sem=recv_sems.at[a], device_id=(x, y, 1 - c), device_id_type=MESH).wait()

    return pl.pallas_call(
        body, name=f"rs_join_rows_{tag}", in_specs=[ANY] * n, out_specs=[ANY] * n,
        out_shape=[jax.ShapeDtypeStruct(f.shape, f.dtype) for f in fs],
        input_output_aliases={a: a for a in range(n)},
        scratch_shapes=[pltpu.SemaphoreType.DMA((n,)), pltpu.SemaphoreType.DMA((n,))],
    )(*fs)


def _pos_vector():
    x, y, c = _my_pos()
    return jnp.stack([2 * x + y, c]).astype(jnp.int32)


def _rs_pair_sums(gs, names, tag):
    pos = _pos_vector()
    r1 = _rs_swap_rows(gs, tag)
    return [_rs_add_pair(g, r, pos, name=f"rs_add_pair_{tag}_{nm}") for g, r, nm in zip(gs, r1, names)]


def _rs_finish(pairs, r2, names, tag):
    pos = _pos_vector()
    fs = [_rs_add_chips(p[0], r, pos, name=f"rs_add_chips_{tag}_{nm}") for p, r, nm in zip(pairs, r2, names)]
    return _rs_join_rows(fs, tag)


def _exchange_land_shapes(pairs):
    return [jax.ShapeDtypeStruct((3,) + p[1].shape[1:], p[1].dtype) for p in pairs]


def _allreduce_small(buf):
    R, W = buf.shape

    def body(b_ref, o_ref, gather, send_sems, recv_sems):
        x, y, c = _my_pos()
        me = 4 * x + 2 * y + c
        gather[me] = b_ref[...]
        cps = []
        for d in range(1, 8):
            peer = (x ^ (d >> 2), y ^ ((d >> 1) & 1), c ^ (d & 1))
            cps.append(pltpu.make_async_remote_copy(
                src_ref=b_ref, dst_ref=gather.at[me], send_sem=send_sems.at[d - 1], recv_sem=recv_sems.at[d - 1],
                device_id=peer, device_id_type=MESH))
        for cp in cps:
            cp.start()
        for d in range(1, 8):
            pltpu.make_async_remote_copy(
                src_ref=b_ref, dst_ref=gather.at[me ^ d], send_sem=send_sems.at[d - 1], recv_sem=recv_sems.at[d - 1],
                device_id=(x, y, c), device_id_type=MESH).wait_recv()
        for cp in cps:
            cp.wait_send()
        acc = gather[0]
        for d in range(1, 8):
            acc = acc + gather[d]
        o_ref[...] = acc

    vm = pl.BlockSpec(memory_space=pltpu.VMEM)
    return pl.pallas_call(
        body, name="allreduce_small", in_specs=[vm], out_specs=vm, out_shape=jax.ShapeDtypeStruct((R, W), F32),
        scratch_shapes=[pltpu.VMEM((8, R, W), F32), pltpu.SemaphoreType.DMA((7,)), pltpu.SemaphoreType.DMA((7,))],
    )(buf)


def _heads(a, h, d):
    return a.reshape(a.shape[0], h, d).transpose(1, 0, 2)


def _unheads(a):
    h, L, d = a.shape
    return a.transpose(1, 0, 2).reshape(L, h * d)


def _rope_tables():
    pos = jnp.maximum(jnp.arange(LP, dtype=F32) - PAD_ROWS, 0.0)
    inv_freq = 1.0 / (ROPE_THETA ** (jnp.arange(0, MLA_ROPE, 2, dtype=F32) / MLA_ROPE))
    ang = pos[:, None] * inv_freq[None, :]
    cos, sin = jnp.tile(jnp.cos(ang), (1, MLA_HEADS)), jnp.tile(jnp.sin(ang), (1, MLA_HEADS))
    return jnp.concatenate([cos, cos], axis=1), jnp.concatenate([-sin, sin], axis=1)


def _lane_pad(a, width=BLOCK):
    return jnp.pad(a, ((0, 0), (0, width - a.shape[1])))


def _pad_in_proj(w):
    sl = lambda start, size: w[:, start:start + size]
    return jnp.concatenate([
        sl(OC_Z, 512), sl(OC_XBC, 768), sl(OC_FQ, 256), sl(OC_FK, 256), sl(OC_FV, 256), sl(OC_CQ, 256), sl(OC_CKV, 128),
        _lane_pad(sl(OC_DT, SSD_HEADS)), _lane_pad(sl(OC_FR, FOX_HEADS)),
        jnp.tile(sl(OC_KR, ROPE_HALF), (1, MLA_HEADS)), jnp.tile(sl(OC_KR + ROPE_HALF, ROPE_HALF), (1, MLA_HEADS))], axis=1)


def _unpad_in_proj(wp):
    sl = lambda start, size: wp[:, start:start + size]
    rope = lambda start: sl(start, 64).reshape(wp.shape[0], MLA_HEADS, ROPE_HALF).sum(axis=1)
    return jnp.concatenate([
        sl(PC_Z, 512), sl(PC_XBC, 768), sl(PC_DT, SSD_HEADS), sl(PC_FQ, 256), sl(PC_FK, 256), sl(PC_FV, 256),
        sl(PC_FR, FOX_HEADS), sl(PC_CQ, 256), sl(PC_CKV, 128), rope(PC_KR), rope(PC_KR + 64)], axis=1)


def _regroup_uq(w):
    w3 = w.reshape(w.shape[0], MLA_HEADS, MLA_NOPE + MLA_ROPE)
    return jnp.concatenate([w3[:, :, :MLA_NOPE].reshape(w.shape[0], -1),
                            w3[:, :, MLA_NOPE:MLA_NOPE + ROPE_HALF].reshape(w.shape[0], -1),
                            w3[:, :, MLA_NOPE + ROPE_HALF:].reshape(w.shape[0], -1)], axis=1)


def _ungroup_uq(wp):
    n = wp.shape[0]
    return jnp.concatenate([wp[:, :256].reshape(n, MLA_HEADS, MLA_NOPE), wp[:, 256:320].reshape(n, MLA_HEADS, ROPE_HALF),
                            wp[:, 320:].reshape(n, MLA_HEADS, ROPE_HALF)], axis=2).reshape(n, -1)


def _regroup_ukv(w):
    w3 = w.reshape(w.shape[0], MLA_HEADS, MLA_NOPE + MLA_V)
    return jnp.concatenate([w3[:, :, :MLA_NOPE].reshape(w.shape[0], -1), w3[:, :, MLA_NOPE:].reshape(w.shape[0], -1)],
                           axis=1)


def _ungroup_ukv(wp):
    n = wp.shape[0]
    return jnp.concatenate([wp[:, :256].reshape(n, MLA_HEADS, MLA_NOPE), wp[:, 256:].reshape(n, MLA_HEADS, MLA_V)],
                           axis=2).reshape(n, -1)


TMF = 1088
N_IF = LP // TMF


def _chunk_cols_mm(a, w, l, chunk_w, *, name):
    K = a.shape[1]
    return _mm_core(a, w, a_spec=_bs((TMF, K), lambda i, j, k: (i, 0)),
                    b_spec=_bs((None, K, chunk_w), lambda i, j, k: (j, 0, 0)),
                    o_spec=_bs((TMF, chunk_w), lambda i, j, k: (i, j)), grid=(N_IF, N_CHIPS, 1),
                    out_shape=(LP, N_CHIPS * chunk_w), ca=1, cb=0, name=name)


def _chunk_cols_dx(g, w, l, chunk_w, add, *, name):
    K = w.shape[1]
    return _mm_core(g, w, a_spec=_bs((TMF, chunk_w), lambda i, j, k: (i, k)),
                    b_spec=_bs((None, K, chunk_w), lambda i, j, k: (k, 0, 0)),
                    o_spec=_bs((TMF, K), lambda i, j, k: (i, 0)), grid=(N_IF, 1, N_CHIPS),
                    out_shape=(LP, K), ca=1, cb=1, name=name, add=add)


def _chunk_cols_dw(a, g, chunk_w, *, name):
    K = a.shape[1]
    return _mm_core(a, g, a_spec=_bs((LP, K), lambda i, j, k: (0, 0)), b_spec=_bs((LP, chunk_w), lambda i, j, k: (0, j)),
                    o_spec=_bs((None, K, chunk_w), lambda i, j, k: (j, 0, 0)), grid=(1, N_CHIPS, 1),
                    out_shape=(N_CHIPS, K, chunk_w), ca=0, cb=0, name=name)


def _chunk_rows_mm(a, w, l, chunk_h, *, name):
    N = w.shape[2]
    return _mm_core(a, w, a_spec=_bs((TMF, chunk_h), lambda i, j, k: (i, k)),
                    b_spec=_bs((None, chunk_h, N), lambda i, j, k: (k, 0, 0)),
                    o_spec=_bs((TMF, N), lambda i, j, k: (i, 0)), grid=(N_IF, 1, N_CHIPS),
                    out_shape=(LP, N), ca=1, cb=0, name=name)


def _chunk_rows_dx(g, w, l, chunk_h, *, name):
    N = w.shape[2]
    return _mm_core(g, w, a_spec=_bs((TMF, N), lambda i, j, k: (i, 0)),
                    b_spec=_bs((None, chunk_h, N), lambda i, j, k: (j, 0, 0)),
                    o_spec=_bs((TMF, chunk_h), lambda i, j, k: (i, j)), grid=(N_IF, N_CHIPS, 1),
                    out_shape=(LP, N_CHIPS * chunk_h), ca=1, cb=1, name=name)


def _chunk_rows_dw(a, g, chunk_h, *, name):
    N = g.shape[1]
    return _mm_core(a, g, a_spec=_bs((LP, chunk_h), lambda i, j, k: (0, i)), b_spec=_bs((LP, N), lambda i, j, k: (0, 0)),
                    o_spec=_bs((None, chunk_h, N), lambda i, j, k: (i, 0, 0)), grid=(N_CHIPS, 1, 1),
                    out_shape=(N_CHIPS, chunk_h, N), ca=0, cb=0, name=name)


def _ffn_fwd(h, W, pre, l, gam, bet, tag):
    g = _chunk_cols_mm(h, W[pre + "_w_gate"][l], l, HP, name=f"{tag}_gate")
    u = _chunk_cols_mm(h, W[pre + "_w_up"][l], l, HP, name=f"{tag}_up")
    (act,) = _rowwise(_swiglu_fn, [g, u], [], [FP], name=f"{tag}_swiglu", tile=TM, ncol=N_CHIPS, out_dtypes=[BF16])
    o = _chunk_rows_mm(act, W[pre + "_w_down"][l], l, HP, name=f"{tag}_down")
    (out,) = _rowwise(_make_res_ln_fn(0.5), [h, o], [gam, bet], [D_MODEL], name=f"{tag}_ln", tile=272)
    return out, (h, g, u, act, o)


def _ffn_bwd(dout, saved, W, pre, l, gam, bet, GB, tag):
    h, g, u, act, o = saved
    (dh_a, do), (dgam, dbet) = _rowwise_bwd(_make_res_ln_fn(0.5), [h, o], [gam, bet], [dout], name=f"{tag}_ln_bwd",
                                            tile=272, grad_dtypes=[F32, BF16])
    dact = _chunk_rows_dx(do, W[pre + "_w_down"][l], l, HP, name=f"{tag}_down_dx")
    GB[pre + "_w_down"] = _chunk_rows_dw(act, do, HP, name=f"{tag}_down_dw")
    (dg, du), _ = _rowwise_bwd(_swiglu_fn, [g, u], [], [dact], name=f"{tag}_swiglu_bwd", tile=TM, ncol=N_CHIPS,
                               grad_dtypes=[BF16, BF16])
    GB[pre + "_w_gate"] = _chunk_cols_dw(h, dg, HP, name=f"{tag}_gate_dw")
    GB[pre + "_w_up"] = _chunk_cols_dw(h, du, HP, name=f"{tag}_up_dw")
    dh = _chunk_cols_dx(dg, W[pre + "_w_gate"][l], l, HP, dh_a, name=f"{tag}_gate_dx")
    dh = _chunk_cols_dx(du, W[pre + "_w_up"][l], l, HP, dh, name=f"{tag}_up_dx")
    return dh, dgam, dbet


def _mixer_fwd(h1, W, l, cosf, sins):
    tag = f"l{l}"
    proj = _mm(h1, W["w_in_p"][l], name=f"{tag}_in_proj")
    sv = {"h1": h1, "proj": proj}
    conv_w, conv_b = W["conv_w"][l], W["conv_b"][l][None]
    xc = _conv_fwd(proj, PC_XBC // BLOCK, conv_w, conv_b, name=f"{tag}_conv")
    dt_bias = _lane_pad(W["dt_bias"][l][None])
    (dt,) = _rowwise(_ssd_pre_fn, [(proj, BLOCK, PC_DT // BLOCK)], [dt_bias], [BLOCK], name=f"{tag}_ssd_dt", tile=272)
    xh = _heads(xc[:, :SSD_D], SSD_HEADS, SSD_HD)
    bm = _heads(xc[:, SSD_D:SSD_D + 128], SSD_GROUPS, SSD_STATE)
    cm = _heads(xc[:, SSD_D + 128:], SSD_GROUPS, SSD_STATE)
    dt8 = dt[:, :SSD_HEADS].T
    dtc, dtr = dt8[:, :, None], dt8[:, None, :]
    alog = jnp.broadcast_to(W["a_log"][l][:, None, None], (SSD_HEADS, 1, BLOCK))
    yh, prevs = _ssd_fwd(xh, bm, cm, dtc, dtr, alog, name=f"{tag}_ssd")
    y_raw = _unheads(yh)
    dskip = jnp.repeat(W["d_skip"][l], SSD_HD)[None]
    normg = W["ssd_norm_g"][l][None]
    post_rows = [y_raw, (xc, 256, 0), (proj, 256, PC_Z // 256)]
    (y_ssd,) = _rowwise(_ssd_post_fn, post_rows, [dskip, normg], [SSD_D], name=f"{tag}_ssd_post", tile=272,
                        ncol=SSD_GROUPS)
    sv.update(conv_w=conv_w, conv_b=conv_b, dt_bias=dt_bias, xh=xh, bm=bm, cm=cm, dtc=dtc, dtr=dtr, alog=alog,
              prevs=prevs, post_rows=post_rows, dskip=dskip, normg=normg)
    f_b = _lane_pad(W["fox_f_b"][l][None])
    cg, cgt = _fox_gate_fwd(proj, PC_FR // BLOCK, f_b, name=f"{tag}_fox_gate")
    fox_qkv = ((proj, PC_FQ // ATT_W), (proj, PC_FK // ATT_W), (proj, PC_FV // ATT_W))
    y_fox, lse_f = _attn_fwd(*fox_qkv, scale=FOX_HD ** -0.5, name=f"{tag}_fox_attn", bias=(cg, cgt))
    sv.update(f_b=f_b, cg=cg, cgt=cgt, fox_qkv=fox_qkv, y_fox=y_fox, lse_f=lse_f)
    gq, gkv = W["mla_q_norm_g"][l][None], W["mla_kv_norm_g"][l][None]
    norm_rows = [(proj, 256, PC_CQ // 256), (proj, BLOCK, PC_CKV // BLOCK)]
    qn, cn = _rowwise(_mla_norm_fn, norm_rows, [gq, gkv], [MLA_Q_LORA, MLA_KV_LORA], name=f"{tag}_mla_norm", tile=272,
                      out_dtypes=[BF16, BF16])
    qh = _mm(qn, W["mla_w_uq_p"][l], name=f"{tag}_mla_uq")
    kvh = _mm(cn, W["mla_w_ukv_p"][l], name=f"{tag}_mla_ukv")
    qr, kr = _rowwise(_rope_fn, [(qh, BLOCK, 2), (proj, BLOCK, PC_KR // BLOCK), cosf, sins], [], [BLOCK, BLOCK],
                      name=f"{tag}_rope", tile=272)
    mla_qkv = ((qh, 0), (kvh, 0), (kvh, 1))
    y_mla, lse_m = _attn_fwd(*mla_qkv, scale=(MLA_NOPE + MLA_ROPE) ** -0.5, name=f"{tag}_mla_attn",
                             rope=((qr, 0), (kr, 0)))
    sv.update(gq=gq, gkv=gkv, norm_rows=norm_rows, qn=qn, cn=cn, qr=qr, kr=kr, mla_qkv=mla_qkv, y_mla=y_mla, lse_m=lse_m)
    ycat = jnp.concatenate([y_ssd, y_fox, y_mla], axis=1).astype(BF16)
    mix = _chunk_rows_mm(ycat, W["w_out"][l], l, 256, name=f"{tag}_out_proj")
    (h2,) = _rowwise(_make_res_ln_fn(1.0), [h1, mix], [W["ln2_g"][l][None], W["ln2_b"][l][None]], [D_MODEL],
                     name=f"{tag}_ln2", tile=272)
    sv.update(mix=mix, ycat=ycat)
    return h2, sv


def _mixer_bwd(dh2, sv, W, l, cosf, sins, GB):
    tag = f"l{l}"
    G = {}
    proj = sv["proj"]
    ln2g, ln2b = W["ln2_g"][l][None], W["ln2_b"][l][None]
    (dh1_a, dmix), (dln2g, dln2b) = _rowwise_bwd(
        _make_res_ln_fn(1.0), [sv["h1"], sv["mix"]], [ln2g, ln2b], [dh2], name=f"{tag}_ln2_bwd", tile=272,
        grad_dtypes=[F32, BF16])
    G["ln2_g"], G["ln2_b"] = dln2g[0], dln2b[0]
    dycat = _chunk_rows_dx(dmix, W["w_out"][l], l, 256, name=f"{tag}_out_proj_dx")
    GB["w_out"] = _chunk_rows_dw(sv["ycat"], dmix, 256, name=f"{tag}_out_proj_dw")
    (dy_raw, dxs_a, dz), (ddskip, dnormg) = _rowwise_bwd(
        _ssd_post_fn, sv["post_rows"], [sv["dskip"], sv["normg"]], [dycat[:, :SSD_D]],
        name=f"{tag}_ssd_post_bwd", tile=272, ncol=SSD_GROUPS)
    G["ssd_norm_g"] = dnormg[0]
    G["d_skip"] = ddskip.reshape(SSD_HEADS, SSD_HD).sum(axis=1)
    dxh, dbm, dcm, ddtc, ddtr, dal = _ssd_bwd(sv["xh"], sv["bm"], sv["cm"], sv["dtc"], sv["dtr"], sv["alog"],
                                              sv["prevs"], _heads(dy_raw, SSD_HEADS, SSD_HD), name=f"{tag}_ssd_bwd")
    G["a_log"] = dal[:, 0, 0]
    dxc = jnp.concatenate([dxs_a + _unheads(dxh), _unheads(dbm), _unheads(dcm)], axis=1)
    dxbc, G["conv_w"], dconv_b = _conv_bwd(proj, PC_XBC // BLOCK, sv["conv_w"], sv["conv_b"], dxc,
                                           name=f"{tag}_conv_bwd")
    G["conv_b"] = dconv_b[0]
    ddt = _lane_pad((ddtc[:, :, 0] + ddtr[:, 0, :]).T)
    (ddt_raw,), (ddt_bias,) = _rowwise_bwd(_ssd_pre_fn, [(proj, BLOCK, PC_DT // BLOCK)], [sv["dt_bias"]], [ddt],
                                           name=f"{tag}_ssd_dt_bwd", tile=272)
    G["dt_bias"] = ddt_bias[0, :SSD_HEADS]
    dfq, dfk, dfv, dcg, dcgt = _attn_bwd(*sv["fox_qkv"], sv["y_fox"], sv["lse_f"], (dycat, SSD_D // ATT_W),
                                         scale=FOX_HD ** -0.5, name=f"{tag}_fox_attn_bwd", bias=(sv["cg"], sv["cgt"]))
    df_raw, dfb = _fox_gate_bwd(proj, PC_FR // BLOCK, sv["f_b"], dcg, dcgt, name=f"{tag}_fox_gate_bwd")
    G["fox_f_b"] = dfb[0, :FOX_HEADS]
    dqn_h, dkn_h, dv_h, dqr, dkr = _attn_bwd(
        *sv["mla_qkv"], sv["y_mla"], sv["lse_m"], (dycat, (SSD_D + FOX_D) // ATT_W),
        scale=(MLA_NOPE + MLA_ROPE) ** -0.5, name=f"{tag}_mla_attn_bwd", rope=((sv["qr"], 0), (sv["kr"], 0)))
    dq_rope, dk_rope = _rowwise(_rope_t_fn, [dqr, dkr, cosf, sins], [], [BLOCK, BLOCK], name=f"{tag}_rope_bwd",
                                tile=272)
    dqh = jnp.concatenate([dqn_h, dq_rope], axis=1).astype(BF16)
    dkvh = jnp.concatenate([dkn_h, dv_h], axis=1).astype(BF16)
    dqn = _mm(dqh, W["mla_w_uq_p"][l], tb=True, name=f"{tag}_mla_uq_dx")
    G["mla_w_uq_p"] = _mm(sv["qn"], dqh, ta=True, name=f"{tag}_mla_uq_dw")
    dcn = _mm(dkvh, W["mla_w_ukv_p"][l], tb=True, name=f"{tag}_mla_ukv_dx")
    G["mla_w_ukv_p"] = _mm(sv["cn"], dkvh, ta=True, name=f"{tag}_mla_ukv_dw")
    (dcq, dckv), (dgq, dgkv) = _rowwise_bwd(_mla_norm_fn, sv["norm_rows"], [sv["gq"], sv["gkv"]], [dqn, dcn],
                                            name=f"{tag}_mla_norm_bwd", tile=272)
    G["mla_q_norm_g"], G["mla_kv_norm_g"] = dgq[0], dgkv[0]
    dproj = jnp.concatenate([dz, dxbc, dfq, dfk, dfv, dcq, dckv, ddt_raw, df_raw, dk_rope], axis=1).astype(BF16)
    dh1 = _mm(dproj, W["w_in_p"][l], tb=True, add=dh1_a, name=f"{tag}_in_proj_dx")
    G["w_in_p"] = _mm(sv["h1"], dproj, ta=True, name=f"{tag}_in_proj_dw")
    return dh1, G


def _embed(x, meta):
    return jnp.concatenate([jnp.zeros((PAD_ROWS, D_MODEL), F32), meta, x], axis=0)


def _layer_fwd(h, W, l, cosf, sins):
    ln = lambda n: W[n][l][None]
    h1, s1 = _ffn_fwd(h, W, "ffn1", l, ln("ln1_g"), ln("ln1_b"), f"l{l}_ffn1")
    h2, sm = _mixer_fwd(h1, W, l, cosf, sins)
    h3, s2 = _ffn_fwd(h2, W, "ffn2", l, ln("ln3_g"), ln("ln3_b"), f"l{l}_ffn2")
    return h3, (s1, sm, s2)


def _layer_bwd(dh, saved, W, l, cosf, sins):
    ln = lambda n: W[n][l][None]
    s1, sm, s2 = saved
    G = {}
    dh, dg, db = _ffn_bwd(dh, s2, W, "ffn2", l, ln("ln3_g"), ln("ln3_b"), G, f"l{l}_ffn2")
    G["ln3_g"], G["ln3_b"] = dg[0], db[0]
    dh, Gm = _mixer_bwd(dh, sm, W, l, cosf, sins, G)
    G.update(Gm)
    dh, dg, db = _ffn_bwd(dh, s1, W, "ffn1", l, ln("ln1_g"), ln("ln1_b"), G, f"l{l}_ffn1")
    G["ln1_g"], G["ln1_b"] = dg[0], db[0]
    return dh, G


def _local_step(x, target, W):
    h = _embed(x, W["meta"])
    tgt = jnp.concatenate([jnp.zeros((BLOCK, D_MODEL), F32), target], axis=0)
    cosf, sins = _rope_tables()
    saved = []
    for l in range(DEPTH):
        h, sv = _layer_fwd(h, W, l, cosf, sins)
        saved.append(sv)
    dh, loss = _loss_head(h, tgt, name="loss_head")
    grads = [None] * DEPTH
    for l in reversed(range(DEPTH)):
        dh, grads[l] = _layer_bwd(dh, saved[l], W, l, cosf, sins)
    return loss, dh, grads


WEIGHTS = ['meta', 'ffn1_w_gate', 'ffn1_w_up', 'ffn1_w_down', 'ln1_g', 'ln1_b', 'w_in', 'conv_w', 'conv_b', 'dt_bias',
           'a_log', 'd_skip', 'ssd_norm_g', 'fox_f_b', 'mla_q_norm_g', 'mla_w_uq', 'mla_kv_norm_g', 'mla_w_ukv',
           'w_out', 'ln2_g', 'ln2_b', 'ffn2_w_gate', 'ffn2_w_up', 'ffn2_w_down', 'ln3_g', 'ln3_b']
SMALL = ["ln1_g", "ln1_b", "conv_b", "dt_bias", "a_log", "d_skip", "ssd_norm_g", "fox_f_b", "mla_q_norm_g",
         "mla_kv_norm_g", "ln2_g", "ln2_b", "ln3_g", "ln3_b"]
MATMUL_W = ["ffn1_w_gate", "ffn1_w_up", "ffn1_w_down", "w_in", "mla_w_uq", "mla_w_ukv", "w_out", "ffn2_w_gate",
            "ffn2_w_up", "ffn2_w_down"]
SMALL_ROWS = 312


def _pad_to(a, axis, size):
    pads = [(0, 0)] * a.ndim
    pads[axis] = (0, size - a.shape[axis])
    return jnp.pad(a, pads)


def _chip_cols(full, chip, width):
    return lax.dynamic_slice_in_dim(full, chip * width, width, axis=full.ndim - 1)


def kernel(x, meta, ffn1_w_gate, ffn1_w_up, ffn1_w_down, ln1_g, ln1_b, w_in, conv_w, conv_b, dt_bias, a_log, d_skip, ssd_norm_g, fox_f_b, mla_q_norm_g, mla_w_uq, mla_kv_norm_g, mla_w_ukv, w_out, ln2_g, ln2_b, ffn2_w_gate, ffn2_w_up, ffn2_w_down, ln3_g, ln3_b, loss_target, m_meta, m_ffn1_w_gate, m_ffn1_w_up, m_ffn1_w_down, m_ln1_g, m_ln1_b, m_w_in, m_conv_w, m_conv_b, m_dt_bias, m_a_log, m_d_skip, m_ssd_norm_g, m_fox_f_b, m_mla_q_norm_g, m_mla_w_uq, m_mla_kv_norm_g, m_mla_w_ukv, m_w_out, m_ln2_g, m_ln2_b, m_ffn2_w_gate, m_ffn2_w_up, m_ffn2_w_down, m_ln3_g, m_ln3_b, v_meta, v_ffn1_w_gate, v_ffn1_w_up, v_ffn1_w_down, v_ln1_g, v_ln1_b, v_w_in, v_conv_w, v_conv_b, v_dt_bias, v_a_log, v_d_skip, v_ssd_norm_g, v_fox_f_b, v_mla_q_norm_g, v_mla_w_uq, v_mla_kv_norm_g, v_mla_w_ukv, v_w_out, v_ln2_g, v_ln2_b, v_ffn2_w_gate, v_ffn2_w_up, v_ffn2_w_down, v_ln3_g, v_ln3_b):
    args = dict(locals())
    w = {n: args[n] for n in WEIGHTS}
    m = {n: args["m_" + n] for n in WEIGHTS}
    v = {n: args["v_" + n] for n in WEIGHTS}
    xcoord, ycoord, _ = _my_pos()
    chip = 2 * xcoord + ycoord

    send = {}
    for pre in ("ffn1", "ffn2"):
        send[pre + "_w_gate"] = _pad_to(w[pre + "_w_gate"], 2, HP).astype(BF16)
        send[pre + "_w_up"] = _pad_to(w[pre + "_w_up"], 2, HP).astype(BF16)
        send[pre + "_w_down"] = _pad_to(w[pre + "_w_down"], 1, HP).astype(BF16)
    send["w_in"] = _pad_to(w["w_in"], 2, IN_SHARD_P).astype(BF16)
    for n in ("mla_w_uq", "mla_w_ukv", "w_out"):
        send[n] = w[n].astype(BF16)
    tiny = _allgather_chips([w["meta"].reshape(2, N_META // 2, D_MODEL // N_CHIPS), w["conv_w"]])
    meta_full = jnp.concatenate([tiny[0][k].reshape(N_META, D_MODEL // N_CHIPS) for k in range(N_CHIPS)], axis=1)
    shards = [send[n] for n in MATMUL_W]
    land_shapes = [jax.ShapeDtypeStruct((N_CHIPS,) + s.shape[1:], s.dtype) for s in shards]
    gather_l = lambda l: (lambda srcs, lands, ss, rs: _gather_copies(srcs, lands, l, ss, rs))

    W = {n: [None] * DEPTH for n in MATMUL_W + ["w_in_p", "mla_w_uq_p", "mla_w_ukv_p"]}
    W["conv_w"] = jnp.concatenate([tiny[1][k] for k in range(N_CHIPS)], axis=-1)
    W["meta"] = meta_full
    for n in SMALL:
        W[n] = w[n]

    def use_gathered(l, lands):
        got = dict(zip(MATMUL_W, lands))
        cat = lambda n, cut=None: jnp.concatenate([got[n][k][..., :cut] for k in range(N_CHIPS)], axis=-1)
        for n in MATMUL_W:
            W[n][l] = got[n]
        W["w_in_p"][l] = _pad_in_proj(cat("w_in", IN_SHARD))
        W["mla_w_uq_p"][l] = _regroup_uq(cat("mla_w_uq"))
        W["mla_w_ukv_p"][l] = _regroup_ukv(cat("mla_w_ukv"))

    def chunk_grads(G):
        def chunked(name, ungroup, width, pad):
            full = ungroup(G[name])
            return _pad_to(jnp.moveaxis(full.reshape(full.shape[0], N_CHIPS, width), 1, 0), 2, pad)
        G = dict(G)
        G["w_in"] = chunked("w_in_p", _unpad_in_proj, IN_SHARD, IN_SHARD_P)
        G["mla_w_uq"] = chunked("mla_w_uq_p", _ungroup_uq, MLA_NOPE + MLA_ROPE, MLA_NOPE + MLA_ROPE)
        G["mla_w_ukv"] = chunked("mla_w_ukv_p", _ungroup_ukv, MLA_NOPE + MLA_V, MLA_NOPE + MLA_V)
        return [G[n] for n in MATMUL_W]

    lands0 = _ici_blocking(gather_l(0), shards, land_shapes, name="gather_ici_l0")
    use_gathered(0, _gather_d2d(shards, lands0, 0, "l0"))
    g_send, g_recv, g_srcs, g_lands, token = _ici_start(gather_l(1), shards, land_shapes, name="gather_ici_l1_start")
    cosf, sins = _rope_tables()
    h = _embed(x[0] + token[0, 0], meta_full)
    h, saved0 = _layer_fwd(h, W, 0, cosf, sins)
    lands1 = _ici_wait(gather_l(1), g_send, g_recv, g_srcs, g_lands, h, name="gather_ici_l1_wait")
    use_gathered(1, _gather_d2d(shards, lands1, 1, "l1"))
    h, saved1 = _layer_fwd(h, W, 1, cosf, sins)
    tgt = jnp.concatenate([jnp.zeros((BLOCK, D_MODEL), F32), loss_target[0]], axis=0)
    dh, loss = _loss_head(h, tgt, name="loss_head")
    G = [None] * DEPTH
    dh, G[1] = _layer_bwd(dh, saved1, W, 1, cosf, sins)
    pairs1 = _rs_pair_sums(chunk_grads(G[1]), MATMUL_W, "l1")
    x_send, x_recv, x_srcs, x_lands, token = _ici_start(_exchange_copies, [p[1] for p in pairs1],
                                                        _exchange_land_shapes(pairs1), name="rs_exchange_l1_start")
    dh0, G[0] = _layer_bwd(dh + token[0, 0], saved0, W, 0, cosf, sins)
    r2 = _ici_wait(_exchange_copies, x_send, x_recv, x_srcs, x_lands, dh0, name="rs_exchange_l1_wait")
    reduced1 = _rs_finish(pairs1, r2, MATMUL_W, "l1")
    pairs0 = _rs_pair_sums(chunk_grads(G[0]), MATMUL_W, "l0")
    r2 = _ici_blocking(_exchange_copies, [p[1] for p in pairs0], _exchange_land_shapes(pairs0), name="rs_exchange_l0")
    reduced0 = _rs_finish(pairs0, r2, MATMUL_W, "l0")
    reduced = {n: [reduced0[i], reduced1[i]] for i, n in enumerate(MATMUL_W)}

    small_parts = [jnp.stack([G[l][n] for l in range(DEPTH)]).reshape(-1) for n in SMALL]
    small_parts += [jnp.stack([G[l]["conv_w"] for l in range(DEPTH)]).reshape(-1), dh0[PAD_ROWS:BLOCK].reshape(-1),
                    loss[0, :1]]
    flat = jnp.concatenate(small_parts)
    flat = jnp.pad(flat, (0, SMALL_ROWS * BLOCK - flat.shape[0]))
    red = _allreduce_small(flat.reshape(SMALL_ROWS, BLOCK)).reshape(-1)
    grads, off = {}, 0
    for n in SMALL:
        size = int(np.prod(w[n].shape))
        grads[n] = red[off:off + size].reshape(w[n].shape)
        off += size
    conv_full = red[off:off + DEPTH * SSD_CONV * 768].reshape(DEPTH, SSD_CONV, 768)
    off += DEPTH * SSD_CONV * 768
    dmeta_full = red[off:off + N_META * D_MODEL].reshape(N_META, D_MODEL)
    off += N_META * D_MODEL
    loss_out = red[off]
    grads["conv_w"] = _chip_cols(conv_full, chip, 768 // N_CHIPS)
    grads["meta"] = _chip_cols(dmeta_full, chip, D_MODEL // N_CHIPS)

    delta, new_m, new_v = {}, {}, {}
    for n in MATMUL_W:
        grads[n], delta[n], new_m[n], new_v[n] = _adamw(w[n], reduced[n], m[n], v[n], name=f"adamw_{n}")
    rest = [n for n in WEIGHTS if n not in MATMUL_W]

    def pack_small(d):
        f = jnp.concatenate([d[n].reshape(-1) for n in rest])
        tot = -(-f.shape[0] // (8 * BLOCK)) * 8 * BLOCK
        return jnp.pad(f, (0, tot - f.shape[0])).reshape(-1, BLOCK)

    _, d2, m2, v2 = _adamw(pack_small(w), [pack_small(grads)], pack_small(m), pack_small(v), name="adamw_small")
    off = 0
    for n in rest:
        size = int(np.prod(w[n].shape))
        for dst, src in ((delta, d2), (new_m, m2), (new_v, v2)):
            dst[n] = src.reshape(-1)[off:off + size].reshape(w[n].shape)
        off += size

    grad_x = dh0[BLOCK:][None]
    return (loss_out, grad_x, *[grads[n] for n in WEIGHTS], *[delta[n] for n in WEIGHTS],
            *[new_m[n] for n in WEIGHTS], *[new_v[n] for n in WEIGHTS])
```

```python
import functools

import numpy as np
import jax
import jax.numpy as jnp
from jax import lax
from jax.experimental import pallas as pl
from jax.experimental.pallas import tpu as pltpu

F32 = jnp.float32
BF16 = jnp.bfloat16
MESH = pl.DeviceIdType.MESH

D_MODEL = 1024
SEQ = 2048
N_META = 16
BLOCK = 128
PAD_ROWS = 112
LP = PAD_ROWS + N_META + SEQ
N_CHUNK = LP // BLOCK
DEPTH = 2
D_FF = 2816
N_CHIPS = 4
FF_SHARD = D_FF // N_CHIPS
HP = 768
FP = N_CHIPS * HP
SSD_HEADS, SSD_HD, SSD_D, SSD_GROUPS, SSD_STATE, SSD_CONV = 8, 64, 512, 2, 64, 4
FOX_HEADS, FOX_HD, FOX_D = 4, 64, 256
MLA_HEADS, MLA_Q_LORA, MLA_KV_LORA, MLA_NOPE, MLA_ROPE, MLA_V, MLA_D = 4, 256, 128, 64, 32, 64, 256
ROPE_HALF = MLA_ROPE // 2
ROPE_THETA = 10000.0
N_IN = 2476
IN_SHARD = N_IN // N_CHIPS
IN_SHARD_P = 640
ALPHA = (2 * DEPTH) ** 0.25
EPS = 1e-5
ADAM_LR, ADAM_B1, ADAM_B2, ADAM_EPS, ADAM_WD, ADAM_STEP = 0.001, 0.9, 0.999, 1e-08, 0.01, 10
NEG = -1e30
TM = 544

VMEM_LIMIT_BYTES = 56 * 1024 * 1024

PC_Z, PC_XBC, PC_FQ, PC_FK, PC_FV, PC_CQ, PC_CKV, PC_DT, PC_FR, PC_KR, PC_END = (
    0, 512, 1280, 1536, 1792, 2048, 2304, 2432, 2560, 2688, 2816)
OC_Z, OC_XBC, OC_DT, OC_FQ, OC_FK, OC_FV, OC_FR, OC_CQ, OC_CKV, OC_KR = (
    0, 512, 1280, 1288, 1544, 1800, 2056, 2060, 2316, 2444)


def _cparams(sem=None):
    return pltpu.CompilerParams(dimension_semantics=sem, vmem_limit_bytes=VMEM_LIMIT_BYTES)


def _tile(n, cap, mult):
    best = None
    for t in range(mult, min(n, cap) + 1, mult):
        if n % t == 0:
            best = t
    return best if best is not None else n


def _bs(shape, fn):
    return pl.BlockSpec(shape, fn)


ANY = pl.BlockSpec(memory_space=pl.ANY)


def _dims(ca, cb):
    return (((ca,), (cb,)), ((), ()))


def _raw_bdot(a, b, ca, cb):
    return lax.dot_general(a.astype(BF16), b.astype(BF16), _dims(ca, cb), preferred_element_type=F32)


@functools.partial(jax.custom_vjp, nondiff_argnums=(2, 3))
def _bdot(a, b, ca, cb):
    return _raw_bdot(a, b, ca, cb)


def _bdot_fwd(a, b, ca, cb):
    return _raw_bdot(a, b, ca, cb), (a, b)


def _bdot_bwd(ca, cb, res, g):
    a, b = res
    if (ca, cb) == (1, 0):
        return _raw_bdot(g, b, 1, 1), _raw_bdot(a, g, 0, 0)
    if (ca, cb) == (1, 1):
        return _raw_bdot(g, b, 1, 0), _raw_bdot(g, a, 0, 0)
    if (ca, cb) == (0, 0):
        return _raw_bdot(b, g, 1, 1), _raw_bdot(a, g, 1, 0)
    raise NotImplementedError((ca, cb))


_bdot.defvjp(_bdot_fwd, _bdot_bwd)


def _mm_core(a, b, *, a_spec, b_spec, o_spec, grid, out_shape, ca, cb, name, add=None):
    nk = grid[2]
    has_add = add is not None
    acc_shape = tuple(d for d in o_spec.block_shape if d is not None)

    def body(*refs):
        a_ref, b_ref = refs[0], refs[1]
        add_ref = refs[2] if has_add else None
        o_ref, acc_ref = refs[-2], refs[-1]
        k = pl.program_id(2)

        @pl.when(k == 0)
        def _():
            acc_ref[...] = jnp.zeros_like(acc_ref)

        acc_ref[...] += _raw_bdot(a_ref[...], b_ref[...], ca, cb)

        @pl.when(k == nk - 1)
        def _():
            r = acc_ref[...]
            if has_add:
                r = r + add_ref[...]
            o_ref[...] = r

    ins = [a, b] + ([add] if has_add else [])
    in_specs = [a_spec, b_spec] + ([o_spec] if has_add else [])
    return pl.pallas_call(
        body, name=name, grid=grid, in_specs=in_specs, out_specs=o_spec,
        out_shape=jax.ShapeDtypeStruct(out_shape, F32), scratch_shapes=[pltpu.VMEM(acc_shape, F32)],
        compiler_params=_cparams(("parallel", "parallel", "arbitrary")),
    )(*ins)


MM_VMEM_BUDGET = 40 * 1024 * 1024


def _divisors(n, mult):
    return [t for t in range(mult, n + 1, mult) if n % t == 0] or [n]


def _pick_tiles(M, N, K, a_bytes, b_bytes, ta, has_add):
    best = None
    for tm in _divisors(M, 128 if ta else 16):
        for tn in _divisors(N, 128):
            vmem = 2 * tm * K * a_bytes + 2 * K * tn * b_bytes + (3 + 2 * int(has_add)) * tm * tn * 4
            if vmem <= MM_VMEM_BUDGET:
                key = ((M // tm) * (N // tn), -tn)
                if best is None or key < best[0]:
                    best = (key, tm, tn)
    assert best is not None, (M, N, K)
    return best[1], best[2], K


def _mm(a, b, *, ta=False, tb=False, add=None, name):
    if ta:
        K, M = a.shape
    else:
        M, K = a.shape
    if tb:
        N, Kb = b.shape
    else:
        Kb, N = b.shape
    assert K == Kb, (a.shape, b.shape, ta, tb)
    tm, tn, tk = _pick_tiles(M, N, K, a.dtype.itemsize, b.dtype.itemsize, ta, add is not None)
    a_spec = _bs((tk, tm), lambda i, j, k: (k, i)) if ta else _bs((tm, tk), lambda i, j, k: (i, k))
    b_spec = _bs((tn, tk), lambda i, j, k: (j, k)) if tb else _bs((tk, tn), lambda i, j, k: (k, j))
    return _mm_core(a, b, a_spec=a_spec, b_spec=b_spec, o_spec=_bs((tm, tn), lambda i, j, k: (i, j)),
                    grid=(M // tm, N // tn, K // tk), out_shape=(M, N), ca=0 if ta else 1, cb=1 if tb else 0,
                    name=name, add=add)


def _row_entry(r, ncol):
    if isinstance(r, tuple):
        return r
    return r, r.shape[1] // ncol, 0


def _rowwise(fn, rows, pars, out_cols, *, name, tile, ncol=1, out_dtypes=None):
    rows = [_row_entry(r, ncol) for r in rows]
    L = rows[0][0].shape[0]
    nr, npar = len(rows), len(pars)
    in_specs = [_bs((tile, w), lambda g, i, o=o: (i, o + g)) for _, w, o in rows]
    in_specs += [_bs((p.shape[0], p.shape[1] // ncol), lambda g, i: (0, g)) for p in pars]
    out_specs = [_bs((tile, c // ncol), lambda g, i: (i, g)) for c in out_cols]

    def body(*refs):
        ins, outs = refs[:nr + npar], refs[nr + npar:]
        row0 = pl.program_id(1) * tile
        res = fn(row0, *[r[...] for r in ins])
        for o, v in zip(outs, res):
            o[...] = v.astype(o.dtype)

    return pl.pallas_call(
        body, name=name, grid=(ncol, L // tile), in_specs=in_specs, out_specs=out_specs,
        out_shape=[jax.ShapeDtypeStruct((L, c), d) for c, d in zip(out_cols, out_dtypes or [F32] * len(out_cols))],
        compiler_params=_cparams(("parallel", "parallel")),
    )(*[r[0] for r in rows], *pars)


def _rowwise_bwd(fn, rows, pars, douts, *, name, tile, ncol=1, row_grad=None, grad_dtypes=None):
    rows = [_row_entry(r, ncol) for r in rows]
    L = rows[0][0].shape[0]
    nr, npar, nd = len(rows), len(pars), len(douts)
    row_grad = [True] * nr if row_grad is None else row_grad
    in_specs = [_bs((tile, w), lambda g, i, o=o: (i, o + g)) for _, w, o in rows]
    in_specs += [_bs((p.shape[0], p.shape[1] // ncol), lambda g, i: (0, g)) for p in pars]
    in_specs += [_bs((tile, d.shape[1] // ncol), lambda g, i: (i, g)) for d in douts]
    g_widths = [w * ncol for (_, w, _), f in zip(rows, row_grad) if f]
    out_specs = [_bs((tile, w // ncol), lambda g, i: (i, g)) for w in g_widths]
    out_specs += [_bs((p.shape[0], p.shape[1] // ncol), lambda g, i: (0, g)) for p in pars]
    out_shape = [jax.ShapeDtypeStruct((L, w), d) for w, d in zip(g_widths, grad_dtypes or [F32] * len(g_widths))]
    out_shape += [jax.ShapeDtypeStruct(p.shape, F32) for p in pars]

    def body(*refs):
        ins = refs[:nr + npar]
        dos = refs[nr + npar:nr + npar + nd]
        outs = refs[nr + npar + nd:]
        i = pl.program_id(1)
        row0 = i * tile
        _, vjp = jax.vjp(lambda *a: tuple(fn(row0, *a)), *[r[...] for r in ins])
        grads = vjp(tuple(d[...].astype(F32) for d in dos))
        o = 0
        for j in range(nr):
            if row_grad[j]:
                outs[o][...] = grads[j].astype(outs[o].dtype)
                o += 1
        for j in range(npar):
            g, ref = grads[nr + j], outs[o + j]

            @pl.when(i == 0)
            def _(g=g, ref=ref):
                ref[...] = g

            @pl.when(i > 0)
            def _(g=g, ref=ref):
                ref[...] += g

    res = pl.pallas_call(
        body, name=name, grid=(ncol, L // tile), in_specs=in_specs, out_specs=out_specs, out_shape=out_shape,
        compiler_params=_cparams(("parallel", "arbitrary")),
    )(*[r[0] for r in rows], *pars, *douts)
    return res[:len(g_widths)], res[len(g_widths):]


def _row_ids(row0, shape):
    return row0 + lax.broadcasted_iota(jnp.int32, shape, 0)


def _sigmoid(x):
    return 1.0 / (1.0 + jnp.exp(-x))


@jax.custom_vjp
def _softplus(x):
    return jnp.maximum(x, 0.0) + jnp.log(1.0 + jnp.exp(-jnp.abs(x)))


def _softplus_fwd(x):
    return _softplus(x), x


def _softplus_bwd(x, g):
    return (g * _sigmoid(x),)


_softplus.defvjp(_softplus_fwd, _softplus_bwd)


def _silu(x):
    return x * _sigmoid(x)


def _swiglu_fn(row0, g, u):
    return (_silu(g) * u,)


def _make_res_ln_fn(scale):
    def fn(row0, h, o, gam, bet):
        pre = ALPHA * h + scale * o
        mu = jnp.mean(pre, axis=-1, keepdims=True)
        xc = pre - mu
        var = jnp.mean(xc * xc, axis=-1, keepdims=True)
        return (xc * lax.rsqrt(var + EPS) * gam + bet,)
    return fn


def _ssd_pre_fn(row0, raw, bias):
    dt = _softplus(raw + bias)
    return (jnp.where(_row_ids(row0, raw.shape) >= PAD_ROWS, dt, 0.0),)


def _ssd_post_fn(row0, y, xs, z, dskip, normg):
    v = (y + dskip * xs) * _silu(z)
    v = v * lax.rsqrt(jnp.mean(v * v, axis=-1, keepdims=True) + EPS)
    return (v * normg,)


def _mla_norm_fn(row0, cq, ckv, gq, gkv):
    qn = cq * lax.rsqrt(jnp.mean(cq * cq, axis=-1, keepdims=True) + EPS) * gq
    cn = ckv * lax.rsqrt(jnp.mean(ckv * ckv, axis=-1, keepdims=True) + EPS) * gkv
    return qn, cn


def _rope_fn(row0, q, k, cosf, sins):
    return (q * cosf + pltpu.roll(q, 64, 1) * sins, k * cosf + pltpu.roll(k, 64, 1) * sins)


def _rope_t_fn(row0, gq, gk, cosf, sins):
    return (gq * cosf + pltpu.roll(gq * sins, 64, 1), gk * cosf + pltpu.roll(gk * sins, 64, 1))


def _conv_fwd(x, x_off, w, b, *, name):
    C = w.shape[1]

    def body(x_ref, w_ref, b_ref, o_ref):
        rows = lax.broadcasted_iota(jnp.int32, (LP, BLOCK), 0)
        xv = jnp.where(rows >= PAD_ROWS, x_ref[...], 0.0)
        acc = b_ref[...] + w_ref[3:4, :] * xv
        for k in range(SSD_CONV - 1):
            acc = acc + w_ref[k:k + 1, :] * pltpu.roll(xv, SSD_CONV - 1 - k, 0)
        o_ref[...] = _silu(acc)

    return pl.pallas_call(
        body, name=name, grid=(C // BLOCK,),
        in_specs=[_bs((LP, BLOCK), lambda j: (0, j + x_off)), _bs((SSD_CONV, BLOCK), lambda j: (0, j)),
                  _bs((1, BLOCK), lambda j: (0, j))],
        out_specs=_bs((LP, BLOCK), lambda j: (0, j)),
        out_shape=jax.ShapeDtypeStruct((LP, C), F32), compiler_params=_cparams(("parallel",)),
    )(x, w, b)


def _conv_bwd(x, x_off, w, b, dout, *, name):
    C = w.shape[1]

    def body(x_ref, w_ref, b_ref, do_ref, dx_ref, dw_ref, db_ref):
        rows = lax.broadcasted_iota(jnp.int32, (LP, BLOCK), 0)
        real = rows >= PAD_ROWS
        xv = jnp.where(real, x_ref[...], 0.0)
        shifted = [pltpu.roll(xv, SSD_CONV - 1 - k, 0) for k in range(SSD_CONV - 1)] + [xv]
        acc = b_ref[...]
        for k in range(SSD_CONV):
            acc = acc + w_ref[k:k + 1, :] * shifted[k]
        sig = _sigmoid(acc)
        dacc = jnp.where(real, do_ref[...] * (sig * (1.0 + acc * (1.0 - sig))), 0.0)
        db_ref[...] = jnp.sum(dacc, axis=0, keepdims=True)
        dx = w_ref[3:4, :] * dacc
        for k in range(SSD_CONV):
            dw_ref[k:k + 1, :] = jnp.sum(dacc * shifted[k], axis=0, keepdims=True)
            if k < SSD_CONV - 1:
                dx = dx + w_ref[k:k + 1, :] * pltpu.roll(dacc, LP - (SSD_CONV - 1 - k), 0)
        dx_ref[...] = jnp.where(real, dx, 0.0)

    return pl.pallas_call(
        body, name=name, grid=(C // BLOCK,),
        in_specs=[_bs((LP, BLOCK), lambda j: (0, j + x_off)), _bs((SSD_CONV, BLOCK), lambda j: (0, j)),
                  _bs((1, BLOCK), lambda j: (0, j)), _bs((LP, BLOCK), lambda j: (0, j))],
        out_specs=[_bs((LP, BLOCK), lambda j: (0, j)), _bs((SSD_CONV, BLOCK), lambda j: (0, j)),
                   _bs((1, BLOCK), lambda j: (0, j))],
        out_shape=[jax.ShapeDtypeStruct((LP, C), F32), jax.ShapeDtypeStruct((SSD_CONV, C), F32),
                   jax.ShapeDtypeStruct((1, C), F32)],
        compiler_params=_cparams(("parallel",)),
    )(x, w, b, dout)


_BDIMS = {"nn": (((2,), (1,)), ((0,), (0,))), "nt": (((2,), (2,)), ((0,), (0,))), "tn": (((1,), (1,)), ((0,), (0,)))}


def _raw_bdot3(a, b, mode):
    return lax.dot_general(a.astype(BF16), b.astype(BF16), _BDIMS[mode], preferred_element_type=F32)


@functools.partial(jax.custom_vjp, nondiff_argnums=(2,))
def _bdot3(a, b, mode):
    return _raw_bdot3(a, b, mode)


def _bdot3_fwd(a, b, mode):
    return _raw_bdot3(a, b, mode), (a, b)


def _bdot3_bwd(mode, res, g):
    a, b = res
    if mode == "nn":
        return _raw_bdot3(g, b, "nt"), _raw_bdot3(a, g, "tn")
    if mode == "nt":
        return _raw_bdot3(g, b, "nn"), _raw_bdot3(g, a, "tn")
    return _raw_bdot3(b, g, "nt"), _raw_bdot3(a, g, "nn")


_bdot3.defvjp(_bdot3_fwd, _bdot3_bwd)


def _ssd_chunk(x, bm, cm, dtc, dtr, alog, prev):
    rep = SSD_HEADS // SSD_GROUPS
    per_head = lambda t: jnp.broadcast_to(t[:, None], (SSD_GROUPS, rep) + t.shape[1:]).reshape((SSD_HEADS,) + t.shape[1:])
    bm, cm = per_head(bm), per_head(cm)
    lane = lax.broadcasted_iota(jnp.int32, alog.shape, 2)
    a_neg = -jnp.exp(jnp.sum(jnp.where(lane == 0, alog, 0.0), axis=2, keepdims=True))
    ac_in = dtc * a_neg
    ar_in = dtr * a_neg
    li = lax.broadcasted_iota(jnp.int32, (1, BLOCK, BLOCK), 1)
    si = lax.broadcasted_iota(jnp.int32, (1, BLOCK, BLOCK), 2)
    causal = li >= si
    acum_c = jnp.sum(jnp.where(causal, ar_in, 0.0), axis=2, keepdims=True)
    acum_r = jnp.sum(jnp.where(li <= si, ac_in, 0.0), axis=1, keepdims=True)
    total = jnp.sum(ar_in, axis=2, keepdims=True)
    seg = jnp.exp(jnp.where(causal, acum_c - acum_r, NEG))
    xdt = x * dtc
    cb = _bdot3(cm, bm, "nt")
    y = _bdot3(cb * seg, xdt, "nn") + _bdot3(cm, prev, "nt") * jnp.exp(acum_c)
    st = _bdot3(xdt, bm * jnp.exp(total - acum_c), "tn")
    return y, prev * jnp.exp(total) + st


def _ssd_specs(rev):
    ci = (lambda c: N_CHUNK - 1 - c) if rev else (lambda c: c)
    x_spec = _bs((SSD_HEADS, BLOCK, SSD_HD), lambda c: (0, ci(c), 0))
    g_spec = _bs((SSD_GROUPS, BLOCK, SSD_STATE), lambda c: (0, ci(c), 0))
    dtc_spec = _bs((SSD_HEADS, BLOCK, 1), lambda c: (0, ci(c), 0))
    dtr_spec = _bs((SSD_HEADS, 1, BLOCK), lambda c: (0, 0, ci(c)))
    al_spec = _bs((SSD_HEADS, 1, BLOCK), lambda c: (0, 0, 0))
    st_spec = _bs((None, SSD_HEADS, SSD_HD, SSD_STATE), lambda c: (ci(c), 0, 0, 0))
    return x_spec, g_spec, dtc_spec, dtr_spec, al_spec, st_spec


def _ssd_fwd(x, bm, cm, dtc, dtr, alog, *, name):
    x_spec, g_spec, dtc_spec, dtr_spec, al_spec, st_spec = _ssd_specs(False)

    def body(x_ref, b_ref, c_ref, dtc_ref, dtr_ref, al_ref, y_ref, prev_ref, state):
        @pl.when(pl.program_id(0) == 0)
        def _():
            state[...] = jnp.zeros_like(state)

        prev = state[...]
        prev_ref[...] = prev
        y, new = _ssd_chunk(x_ref[...], b_ref[...], c_ref[...], dtc_ref[...], dtr_ref[...], al_ref[...], prev)
        y_ref[...] = y
        state[...] = new

    return pl.pallas_call(
        body, name=name, grid=(N_CHUNK,),
        in_specs=[x_spec, g_spec, g_spec, dtc_spec, dtr_spec, al_spec], out_specs=[x_spec, st_spec],
        out_shape=[jax.ShapeDtypeStruct((SSD_HEADS, LP, SSD_HD), F32),
                   jax.ShapeDtypeStruct((N_CHUNK, SSD_HEADS, SSD_HD, SSD_STATE), F32)],
        scratch_shapes=[pltpu.VMEM((SSD_HEADS, SSD_HD, SSD_STATE), F32)],
        compiler_params=_cparams(("arbitrary",)),
    )(x, bm, cm, dtc, dtr, alog)


def _ssd_bwd(x, bm, cm, dtc, dtr, alog, prevs, dy, *, name):
    x_spec, g_spec, dtc_spec, dtr_spec, al_spec, st_spec = _ssd_specs(True)

    def body(x_ref, b_ref, c_ref, dtc_ref, dtr_ref, al_ref, prev_ref, dy_ref,
             dx_ref, db_ref, dc_ref, ddtc_ref, ddtr_ref, dal_ref, dstate):
        c = pl.program_id(0)

        @pl.when(c == 0)
        def _():
            dstate[...] = jnp.zeros_like(dstate)

        _, vjp = jax.vjp(_ssd_chunk, x_ref[...], b_ref[...], c_ref[...], dtc_ref[...], dtr_ref[...], al_ref[...],
                         prev_ref[...])
        dx, db, dc, ddtc, ddtr, dal, dprev = vjp((dy_ref[...], dstate[...]))
        dx_ref[...] = dx
        db_ref[...] = db
        dc_ref[...] = dc
        ddtc_ref[...] = ddtc
        ddtr_ref[...] = ddtr
        dstate[...] = dprev

        @pl.when(c == 0)
        def _():
            dal_ref[...] = dal

        @pl.when(c > 0)
        def _():
            dal_ref[...] += dal

    hs = jax.ShapeDtypeStruct((SSD_HEADS, LP, SSD_HD), F32)
    gs = jax.ShapeDtypeStruct((SSD_GROUPS, LP, SSD_STATE), F32)
    return pl.pallas_call(
        body, name=name, grid=(N_CHUNK,),
        in_specs=[x_spec, g_spec, g_spec, dtc_spec, dtr_spec, al_spec, st_spec, x_spec],
        out_specs=[x_spec, g_spec, g_spec, dtc_spec, dtr_spec, al_spec],
        out_shape=[hs, gs, gs, jax.ShapeDtypeStruct((SSD_HEADS, LP, 1), F32),
                   jax.ShapeDtypeStruct((SSD_HEADS, 1, LP), F32), jax.ShapeDtypeStruct((SSD_HEADS, 1, BLOCK), F32)],
        scratch_shapes=[pltpu.VMEM((SSD_HEADS, SSD_HD, SSD_STATE), F32)],
        compiler_params=_cparams(("arbitrary",)),
    )(x, bm, cm, dtc, dtr, alog, prevs, dy)


def _tri_dot(tri, v):
    hi = v.astype(BF16)
    r1 = v - hi.astype(F32)
    mid = r1.astype(BF16)
    lo = (r1 - mid.astype(F32)).astype(BF16)
    t = tri.astype(BF16)
    d = lambda p: lax.dot_general(t, p, _dims(1, 0), preferred_element_type=F32)
    return d(hi) + d(mid) + d(lo)


def _fox_gate_fwd(raw, raw_blk, bias, *, name):
    def body(raw_ref, b_ref, c_ref, ct_ref, carry):
        j = pl.program_id(0)

        @pl.when(j == 0)
        def _():
            carry[...] = jnp.zeros_like(carry)

        rows = j * BLOCK + lax.broadcasted_iota(jnp.int32, (BLOCK, BLOCK), 0)
        lf = jnp.where(rows >= PAD_ROWS, -_softplus(-(raw_ref[...] + b_ref[...])), 0.0)
        li = lax.broadcasted_iota(jnp.int32, (BLOCK, BLOCK), 0)
        si = lax.broadcasted_iota(jnp.int32, (BLOCK, BLOCK), 1)
        cv = _tri_dot(jnp.where(li >= si, 1.0, 0.0), lf) + carry[...]
        c_ref[...] = cv
        ct_ref[...] = cv.T
        carry[...] += jnp.sum(lf, axis=0, keepdims=True)

    return pl.pallas_call(
        body, name=name, grid=(N_CHUNK,),
        in_specs=[_bs((BLOCK, BLOCK), lambda j: (j, raw_blk)), _bs((1, BLOCK), lambda j: (0, 0))],
        out_specs=[_bs((BLOCK, BLOCK), lambda j: (j, 0)), _bs((BLOCK, BLOCK), lambda j: (0, j))],
        out_shape=[jax.ShapeDtypeStruct((LP, BLOCK), F32), jax.ShapeDtypeStruct((BLOCK, LP), F32)],
        scratch_shapes=[pltpu.VMEM((1, BLOCK), F32)], compiler_params=_cparams(("arbitrary",)),
    )(raw, bias)


def _fox_gate_bwd(raw, raw_blk, bias, dc, dct, *, name):
    rj = lambda j: N_CHUNK - 1 - j

    def body(raw_ref, b_ref, dc_ref, dct_ref, draw_ref, db_ref, carry):
        j = pl.program_id(0)

        @pl.when(j == 0)
        def _():
            carry[...] = jnp.zeros_like(carry)

        rows = (N_CHUNK - 1 - j) * BLOCK + lax.broadcasted_iota(jnp.int32, (BLOCK, BLOCK), 0)
        li = lax.broadcasted_iota(jnp.int32, (BLOCK, BLOCK), 0)
        si = lax.broadcasted_iota(jnp.int32, (BLOCK, BLOCK), 1)
        dcv = dc_ref[...] + dct_ref[...].T
        dlf = _tri_dot(jnp.where(li <= si, 1.0, 0.0), dcv) + carry[...]
        carry[...] += jnp.sum(dcv, axis=0, keepdims=True)
        draw = jnp.where(rows >= PAD_ROWS, dlf * (1.0 - _sigmoid(raw_ref[...] + b_ref[...])), 0.0)
        draw_ref[...] = draw
        dsum = jnp.sum(draw, axis=0, keepdims=True)

        @pl.when(j == 0)
        def _():
            db_ref[...] = dsum

        @pl.when(j > 0)
        def _():
            db_ref[...] += dsum

    return pl.pallas_call(
        body, name=name, grid=(N_CHUNK,),
        in_specs=[_bs((BLOCK, BLOCK), lambda j: (rj(j), raw_blk)), _bs((1, BLOCK), lambda j: (0, 0)),
                  _bs((BLOCK, BLOCK), lambda j: (rj(j), 0)), _bs((BLOCK, BLOCK), lambda j: (0, rj(j)))],
        out_specs=[_bs((BLOCK, BLOCK), lambda j: (rj(j), 0)), _bs((1, BLOCK), lambda j: (0, 0))],
        out_shape=[jax.ShapeDtypeStruct((LP, BLOCK), F32), jax.ShapeDtypeStruct((1, BLOCK), F32)],
        scratch_shapes=[pltpu.VMEM((1, BLOCK), F32)], compiler_params=_cparams(("arbitrary",)),
    )(raw, bias, dc, dct)


ATT_W = 256
ATT_QB = 272
ATT_STEPS = LP // ATT_QB
ATT_KEYS = (640, 1152, 1664, LP)


def _lane_head(width, per, mod=None):
    lane = lax.broadcasted_iota(jnp.int32, (1, width), 1)
    if mod is not None:
        lane = lane % mod
    return lane // per


def _attn_mask(i, kw):
    r = i * ATT_QB + lax.broadcasted_iota(jnp.int32, (ATT_QB, kw), 0)
    c = lax.broadcasted_iota(jnp.int32, (ATT_QB, kw), 1)
    return (c <= r) & ((c >= PAD_ROWS) | (r < PAD_ROWS))


def _attn_by_key_class(i, fn):
    for p, kw in enumerate(ATT_KEYS):
        @pl.when(i // 2 == p)
        def _(kw=kw):
            fn(kw)


def _attn_specs(q, k, v, bias, rope):
    qspec = lambda blk, w=ATT_W: _bs((ATT_QB, w), lambda i: (i, blk))
    fspec = lambda blk, w=ATT_W: _bs((LP, w), lambda i: (0, blk))
    ins = [q[0], k[0], v[0]]
    specs = [qspec(q[1]), fspec(k[1]), fspec(v[1])]
    if bias is not None:
        ins += [bias[0], bias[1]]
        specs += [qspec(0, BLOCK), _bs((BLOCK, LP), lambda i: (0, 0))]
    if rope is not None:
        ins += [rope[0][0], rope[1][0]]
        specs += [qspec(rope[0][1], BLOCK), fspec(rope[1][1], BLOCK)]
    return ins, specs, qspec, fspec


def _attn_fwd(q, k, v, *, scale, name, bias=None, rope=None):
    ins, specs, qspec, fspec = _attn_specs(q, k, v, bias, rope)
    has_bias, has_rope = bias is not None, rope is not None

    def body(*refs):
        it = iter(refs)
        q_ref, k_ref, v_ref = next(it), next(it), next(it)
        if has_bias:
            c_ref, ct_ref = next(it), next(it)
        if has_rope:
            qr_ref, kr_ref = next(it), next(it)
        o_ref, lse_ref = next(it), next(it)
        i = pl.program_id(0)

        def block(kw):
            ok = _attn_mask(i, kw)
            qv, kv, vv = q_ref[...], k_ref[0:kw, :], v_ref[0:kw, :]
            hid, l128 = _lane_head(ATT_W, FOX_HD), _lane_head(BLOCK, 1)
            if has_rope:
                rid = _lane_head(BLOCK, ROPE_HALF, 64)
                qrv, krv = qr_ref[...], kr_ref[0:kw, :]
            def head(h, carry):
                o_acc, lse_acc = carry
                s = _raw_bdot(jnp.where(hid == h, qv, 0.0), kv, 1, 1)
                if has_rope:
                    s = s + _raw_bdot(jnp.where(rid == h, qrv, 0.0), krv, 1, 1)
                s = s * scale
                if has_bias:
                    cq = jnp.sum(jnp.where(l128 == h, c_ref[...], 0.0), axis=1, keepdims=True)
                    s = s + (cq - ct_ref[pl.ds(h, 1), 0:kw])
                s = jnp.where(ok, s, NEG)
                m = jnp.max(s, axis=1, keepdims=True)
                p = jnp.exp(s - m)
                l = jnp.sum(p, axis=1, keepdims=True)
                o_acc = jnp.where(hid == h, _raw_bdot(p, vv, 1, 0) / l, o_acc)
                lse_acc = jnp.where(l128 == h, m + jnp.log(l), lse_acc)
                return o_acc, lse_acc

            o_acc, lse_acc = lax.fori_loop(
                0, FOX_HEADS, head, (jnp.zeros((ATT_QB, ATT_W), F32), jnp.zeros((ATT_QB, BLOCK), F32)), unroll=True)
            o_ref[...] = o_acc
            lse_ref[...] = lse_acc

        _attn_by_key_class(i, block)

    return pl.pallas_call(
        body, name=name, grid=(ATT_STEPS,), in_specs=specs, out_specs=[qspec(0), qspec(0, BLOCK)],
        out_shape=[jax.ShapeDtypeStruct((LP, ATT_W), F32), jax.ShapeDtypeStruct((LP, BLOCK), F32)],
        compiler_params=_cparams(("parallel",)),
    )(*ins)


def _attn_bwd(q, k, v, o, lse, do, *, scale, name, bias=None, rope=None):
    ins, specs, qspec, fspec = _attn_specs(q, k, v, bias, rope)
    has_bias, has_rope = bias is not None, rope is not None
    ins += [o, lse, do[0]]
    specs += [qspec(0), qspec(0, BLOCK), qspec(do[1])]

    def body(*refs):
        it = iter(refs)
        q_ref, k_ref, v_ref = next(it), next(it), next(it)
        if has_bias:
            c_ref, ct_ref = next(it), next(it)
        if has_rope:
            qr_ref, kr_ref = next(it), next(it)
        o_ref, lse_ref, do_ref = next(it), next(it), next(it)
        dq_ref, dk_ref, dv_ref = next(it), next(it), next(it)
        if has_bias:
            dc_ref, dct_ref = next(it), next(it)
        if has_rope:
            dqr_ref, dkr_ref = next(it), next(it)
        i = pl.program_id(0)

        @pl.when(i == 0)
        def _():
            dk_ref[...] = jnp.zeros_like(dk_ref)
            dv_ref[...] = jnp.zeros_like(dv_ref)
            if has_rope:
                dkr_ref[...] = jnp.zeros_like(dkr_ref)
            if has_bias:
                dct_ref[...] = jnp.zeros_like(dct_ref)

        def block(kw):
            ok = _attn_mask(i, kw)
            qv, kv, vv = q_ref[...], k_ref[0:kw, :], v_ref[0:kw, :]
            ov, dov, lsev = o_ref[...], do_ref[...], lse_ref[...]
            hid, l128 = _lane_head(ATT_W, FOX_HD), _lane_head(BLOCK, 1)
            if has_rope:
                rid = _lane_head(BLOCK, ROPE_HALF, 64)
                qrv, krv = qr_ref[...], kr_ref[0:kw, :]

            def head(h, carry):
                dq_acc, aux_acc = carry
                qm = jnp.where(hid == h, qv, 0.0)
                s = _raw_bdot(qm, kv, 1, 1)
                if has_rope:
                    qrm = jnp.where(rid == h, qrv, 0.0)
                    s = s + _raw_bdot(qrm, krv, 1, 1)
                s = s * scale
                if has_bias:
                    cq = jnp.sum(jnp.where(l128 == h, c_ref[...], 0.0), axis=1, keepdims=True)
                    s = s + (cq - ct_ref[pl.ds(h, 1), 0:kw])
                s = jnp.where(ok, s, NEG)
                p = jnp.exp(s - jnp.sum(jnp.where(l128 == h, lsev, 0.0), axis=1, keepdims=True))
                dom = jnp.where(hid == h, dov, 0.0)
                dp = _raw_bdot(dom, vv, 1, 1)
                delta = jnp.sum(dom * ov, axis=1, keepdims=True)
                ds = p * (dp - delta)
                dq_acc = jnp.where(hid == h, _raw_bdot(ds, kv, 1, 0) * scale, dq_acc)
                dk_ref[0:kw, :] += _raw_bdot(ds, qm, 0, 0) * scale
                dv_ref[0:kw, :] += _raw_bdot(p, dom, 0, 0)
                if has_rope:
                    aux_acc = jnp.where(rid == h, _raw_bdot(ds, krv, 1, 0) * scale, aux_acc)
                    dkr_ref[0:kw, :] += _raw_bdot(ds, qrm, 0, 0) * scale
                if has_bias:
                    aux_acc = jnp.where(l128 == h, jnp.sum(ds, axis=1, keepdims=True), aux_acc)
                    dct_ref[pl.ds(h, 1), 0:kw] -= jnp.sum(ds, axis=0, keepdims=True)
                return dq_acc, aux_acc

            dq_acc, aux_acc = lax.fori_loop(
                0, FOX_HEADS, head, (jnp.zeros((ATT_QB, ATT_W), F32), jnp.zeros((ATT_QB, BLOCK), F32)))
            dq_ref[...] = dq_acc
            if has_bias:
                dc_ref[...] = aux_acc
            if has_rope:
                dqr_ref[...] = aux_acc

        _attn_by_key_class(i, block)

    wide = jax.ShapeDtypeStruct((LP, ATT_W), F32)
    narrow = jax.ShapeDtypeStruct((LP, BLOCK), F32)
    out_specs = [qspec(0), fspec(0), fspec(0)]
    out_shape = [wide, wide, wide]
    if has_bias:
        out_specs += [qspec(0, BLOCK), _bs((BLOCK, LP), lambda i: (0, 0))]
        out_shape += [narrow, jax.ShapeDtypeStruct((BLOCK, LP), F32)]
    if has_rope:
        out_specs += [qspec(0, BLOCK), fspec(0, BLOCK)]
        out_shape += [narrow, narrow]
    return pl.pallas_call(
        body, name=name, grid=(ATT_STEPS,), in_specs=specs, out_specs=out_specs, out_shape=out_shape,
        compiler_params=_cparams(("arbitrary",)),
    )(*ins)


def _loss_head(y, target, *, name):
    tile = 272

    def body(y_ref, t_ref, dy_ref, loss_ref):
        i = pl.program_id(0)
        rows = i * tile + lax.broadcasted_iota(jnp.int32, (tile, D_MODEL), 0)
        err = jnp.where(rows >= BLOCK, y_ref[...] - t_ref[...], 0.0)
        dy_ref[...] = err * (1.0 / D_MODEL)
        part = 0.5 * jnp.sum(jnp.sum(err * err, axis=1, keepdims=True) * (1.0 / D_MODEL), axis=0, keepdims=True)
        part = jnp.broadcast_to(part, (1, BLOCK))

        @pl.when(i == 0)
        def _():
            loss_ref[...] = part

        @pl.when(i > 0)
        def _():
            loss_ref[...] += part

    return pl.pallas_call(
        body, name=name, grid=(LP // tile,),
        in_specs=[_bs((tile, D_MODEL), lambda i: (i, 0)), _bs((tile, D_MODEL), lambda i: (i, 0))],
        out_specs=[_bs((tile, D_MODEL), lambda i: (i, 0)), _bs((1, BLOCK), lambda i: (0, 0))],
        out_shape=[jax.ShapeDtypeStruct((LP, D_MODEL), F32), jax.ShapeDtypeStruct((1, BLOCK), F32)],
        compiler_params=_cparams(("arbitrary",)),
    )(y, target)


def _adamw(w, gs, m, v, *, name):
    if w.ndim == 2:
        w, m, v = w[None], m[None], v[None]
        squeeze = True
    else:
        squeeze = False
    NL, R, C = w.shape
    assert len(gs) == NL
    CG = gs[0].shape[1]
    tile = _tile(R, 256, 8)

    def body(*refs):
        w_ref, g_refs = refs[0], refs[1:1 + NL]
        m_ref, v_ref, go_ref, d_ref, nm_ref, nv_ref = refs[1 + NL:]
        gv = g_refs[0][:, :C]
        for j in range(1, NL):
            gv = jnp.where(pl.program_id(0) == j, g_refs[j][:, :C], gv)
        nm = ADAM_B1 * m_ref[...] + (1.0 - ADAM_B1) * gv
        nv = ADAM_B2 * v_ref[...] + (1.0 - ADAM_B2) * (gv * gv)
        m_hat = nm / (1.0 - ADAM_B1 ** ADAM_STEP)
        v_hat = nv / (1.0 - ADAM_B2 ** ADAM_STEP)
        go_ref[...] = gv
        d_ref[...] = -ADAM_LR * (m_hat / (jnp.sqrt(v_hat) + ADAM_EPS) + ADAM_WD * w_ref[...])
        nm_ref[...] = nm
        nv_ref[...] = nv

    spec = _bs((None, tile, C), lambda l, i: (l, i, 0))
    gspecs = [_bs((tile, CG), lambda l, i, j=j: (jnp.where(l == j, i, 0), 0)) for j in range(NL)]
    res = pl.pallas_call(
        body, name=name, grid=(NL, R // tile), in_specs=[spec, *gspecs, spec, spec], out_specs=[spec] * 4,
        out_shape=[jax.ShapeDtypeStruct((NL, R, C), F32)] * 4, compiler_params=_cparams(("parallel", "parallel")),
    )(w, *gs, m, v)
    return [r[0] for r in res] if squeeze else res


def _my_pos():
    return lax.axis_index("x"), lax.axis_index("y"), lax.axis_index("c")


def _other_chips(x, y):
    return [(1 - x, y), (x, 1 - y), (1 - x, 1 - y)]


def _allgather_chips(shards):
    n = len(shards)
    per = 7

    def body(*refs):
        ins, outs = refs[:n], refs[n:2 * n]
        send_sems, recv_sems = refs[2 * n], refs[2 * n + 1]
        x, y, c = _my_pos()
        chips = _other_chips(x, y)
        sibling, me = (x, y, 1 - c), 2 * x + y

        def cp(a, kk, src, dst, to):
            return pltpu.make_async_remote_copy(src_ref=src, dst_ref=dst, send_sem=send_sems.at[per * a + kk],
                                                recv_sem=recv_sems.at[per * a + kk], device_id=to, device_id_type=MESH)

        sends = []
        for a in range(n):
            for j, chip in enumerate(chips):
                sends.append(cp(a, j, ins[a].at[c], outs[a].at[me, c], (*chip, c)))
            sends.append(cp(a, 3, ins[a], outs[a].at[me], sibling))
        for s in sends:
            s.start()
        for a in range(n):
            for j, chip in enumerate(chips):
                slab = outs[a].at[2 * chip[0] + chip[1], c]
                cp(a, j, slab, slab, (x, y, c)).wait_recv()
                fwd = cp(a, 4 + j, slab, slab, sibling)
                fwd.start()
                sends.append(fwd)
        for a in range(n):
            cp(a, 3, ins[a], outs[a].at[me], (x, y, c)).wait_recv()
            for j, chip in enumerate(chips):
                slab = outs[a].at[2 * chip[0] + chip[1], 1 - c]
                cp(a, 4 + j, slab, slab, (x, y, c)).wait_recv()
        for s in sends:
            s.wait_send()

    return pl.pallas_call(
        body, name="allgather_chips", in_specs=[ANY] * n, out_specs=[ANY] * n,
        out_shape=[jax.ShapeDtypeStruct((N_CHIPS,) + s.shape, s.dtype) for s in shards],
        scratch_shapes=[pltpu.SemaphoreType.DMA((per * n,)), pltpu.SemaphoreType.DMA((per * n,))],
    )(*shards)


def _rs_swap_rows(gs, tag):
    n = len(gs)

    def body(*refs):
        ins, outs = refs[:n], refs[n:2 * n]
        send_sems, recv_sems = refs[2 * n], refs[2 * n + 1]
        x, y, c = _my_pos()
        cps = []
        for a in range(n):
            half = ins[a].shape[1] // 2
            cps.append(pltpu.make_async_remote_copy(
                src_ref=ins[a].at[:, pl.ds((1 - c) * half, half)], dst_ref=outs[a], send_sem=send_sems.at[a],
                recv_sem=recv_sems.at[a], device_id=(x, y, 1 - c), device_id_type=MESH))
        for cp in cps:
            cp.start()
        for cp in cps:
            cp.wait()

    return pl.pallas_call(
        body, name=f"rs_swap_rows_{tag}", in_specs=[ANY] * n, out_specs=[ANY] * n,
        out_shape=[jax.ShapeDtypeStruct((N_CHIPS, g.shape[1] // 2, g.shape[2]), g.dtype) for g in gs],
        scratch_shapes=[pltpu.SemaphoreType.DMA((n,)), pltpu.SemaphoreType.DMA((n,))],
    )(*gs)


def _rs_add_pair(g, r, pos, *, name):
    _, H, C = r.shape
    tile = _tile(H, 512, 16)
    nt = H // tile

    def body(pos_ref, g_ref, r_ref, o32_ref, o16_ref):
        s = g_ref[...] + r_ref[...]
        o32_ref[...] = s
        o16_ref[...] = s.astype(BF16)

    spec = _bs((None, tile, C), lambda k, i, pos_ref: (k, i, 0))
    grid_spec = pltpu.PrefetchScalarGridSpec(
        num_scalar_prefetch=1, grid=(N_CHIPS, nt),
        in_specs=[_bs((None, tile, C), lambda k, i, pos_ref: (k, pos_ref[1] * nt + i, 0)), spec],
        out_specs=[spec, spec])
    return pl.pallas_call(
        body, name=name, grid_spec=grid_spec,
        out_shape=[jax.ShapeDtypeStruct((N_CHIPS, H, C), F32), jax.ShapeDtypeStruct((N_CHIPS, H, C), BF16)],
        compiler_params=_cparams(("parallel", "parallel")),
    )(pos, g, r)


def _exchange_copies(srcs, lands, send_sems, recv_sems):
    x, y, c = _my_pos()
    starts, landing = [], []
    for a in range(len(srcs)):
        for j, chip in enumerate(_other_chips(x, y)):
            sems = dict(send_sem=send_sems.at[3 * a + j], recv_sem=recv_sems.at[3 * a + j], device_id_type=MESH)
            starts.append(pltpu.make_async_remote_copy(
                src_ref=srcs[a].at[2 * chip[0] + chip[1]], dst_ref=lands[a].at[j], device_id=(*chip, c), **sems))
            landing.append(pltpu.make_async_remote_copy(
                src_ref=lands[a].at[j], dst_ref=lands[a].at[j], device_id=(x, y, c), **sems))
    return starts, landing


def _gather_copies(srcs, lands, l, send_sems, recv_sems):
    x, y, c = _my_pos()
    me = 2 * x + y
    starts, landing = [], []
    for a in range(len(srcs)):
        half = srcs[a].shape[1] // 2
        mine = pl.ds(c * half, half)
        for j, chip in enumerate(_other_chips(x, y)):
            sems = dict(send_sem=send_sems.at[3 * a + j], recv_sem=recv_sems.at[3 * a + j], device_id_type=MESH)
            starts.append(pltpu.make_async_remote_copy(
                src_ref=srcs[a].at[l, mine], dst_ref=lands[a].at[me, mine], device_id=(*chip, c), **sems))
            slab = lands[a].at[2 * chip[0] + chip[1], mine]
            landing.append(pltpu.make_async_remote_copy(src_ref=slab, dst_ref=slab, device_id=(x, y, c), **sems))
    return starts, landing


HBM = pl.BlockSpec(memory_space=pltpu.HBM)
SEM = pl.BlockSpec(memory_space=pltpu.SEMAPHORE)


def _ici_blocking(copies_fn, srcs, land_shapes, *, name, after=()):
    n, na = len(srcs), len(after)

    def body(*refs):
        starts, landing = copies_fn(refs[:n], refs[n + na:2 * n + na], refs[2 * n + na], refs[2 * n + na + 1])
        for cp in starts:
            cp.start()
        for cp in landing:
            cp.wait_recv()
        for cp in starts:
            cp.wait_send()

    return pl.pallas_call(
        body, name=name, in_specs=[ANY] * (n + na), out_specs=[ANY] * n, out_shape=land_shapes,
        scratch_shapes=[pltpu.SemaphoreType.DMA((3 * n,)), pltpu.SemaphoreType.DMA((3 * n,))],
    )(*srcs, *after)


def _ici_start(copies_fn, srcs, land_shapes, *, name, after=()):
    n, na = len(srcs), len(after)

    def body(*refs):
        starts, _ = copies_fn(refs[:n], refs[n:2 * n], refs[2 * n + na], refs[2 * n + na + 1])
        for cp in starts:
            cp.start()
        refs[-1][...] = jnp.zeros_like(refs[-1])

    sems = pltpu.SemaphoreType.DMA((3 * n,))
    hbm = lambda s: pltpu.HBM(s.shape, s.dtype)
    lands = [pltpu.with_memory_space_constraint(lax.empty(s.shape, s.dtype), pltpu.HBM) for s in land_shapes]
    res = pl.pallas_call(
        body, name=name, in_specs=[HBM] * (2 * n) + [ANY] * na,
        out_specs=(SEM, SEM, *[HBM] * (2 * n), pl.BlockSpec(memory_space=pltpu.VMEM)),
        out_shape=(sems, sems, *[hbm(s) for s in srcs], *[hbm(s) for s in land_shapes],
                   jax.ShapeDtypeStruct((8, BLOCK), F32)),
        input_output_aliases={i: 2 + i for i in range(2 * n)},
        compiler_params=pltpu.CompilerParams(has_side_effects=pltpu.SideEffectType.DATAFLOW_SIDE_EFFECTING),
    )(*[pltpu.with_memory_space_constraint(s, pltpu.HBM) for s in srcs], *lands, *after)
    return res[0], res[1], list(res[2:2 + n]), list(res[2 + n:2 + 2 * n]), res[-1]


def _ici_wait(copies_fn, send_sems, recv_sems, srcs, lands, after, *, name):
    n = len(srcs)

    def body(*refs):
        starts, landing = copies_fn(refs[:n], refs[n:2 * n], refs[2 * n], refs[2 * n + 1])
        for cp in starts:
            cp.wait_send()
        for cp in landing:
            cp.wait_recv()

    hbm = lambda s: pltpu.HBM(s.shape, s.dtype)
    res = pl.pallas_call(
        body, name=name, in_specs=[*[HBM] * (2 * n), SEM, SEM, ANY], out_specs=[HBM] * (2 * n),
        out_shape=[*[hbm(s) for s in srcs], *[hbm(s) for s in lands]],
        input_output_aliases={i: i for i in range(2 * n)},
        compiler_params=pltpu.CompilerParams(has_side_effects=pltpu.SideEffectType.DATAFLOW_SIDE_EFFECTING),
    )(*srcs, *lands, send_sems, recv_sems, after)
    return list(res[:n]), list(res[n:])


def _gather_d2d(shards, lands, l, tag):
    n = len(shards)

    def body(*refs):
        ins, outs = refs[:n], refs[2 * n:3 * n]
        send_sems, recv_sems = refs[3 * n], refs[3 * n + 1]
        x, y, c = _my_pos()
        me, sibling = 2 * x + y, (x, y, 1 - c)
        starts, landing = [], []
        for a in range(n):
            half = ins[a].shape[1] // 2
            mine, theirs = pl.ds(c * half, half), pl.ds((1 - c) * half, half)
            pairs = [(ins[a].at[l], outs[a].at[me], outs[a].at[me])]
            for chip in _other_chips(x, y):
                k = 2 * chip[0] + chip[1]
                pairs.append((outs[a].at[k, mine], outs[a].at[k, mine], outs[a].at[k, theirs]))
            for j, (src, dst, lands_here) in enumerate(pairs):
                sems = dict(send_sem=send_sems.at[4 * a + j], recv_sem=recv_sems.at[4 * a + j], device_id_type=MESH)
                starts.append(pltpu.make_async_remote_copy(src_ref=src, dst_ref=dst, device_id=sibling, **sems))
                landing.append(pltpu.make_async_remote_copy(src_ref=lands_here, dst_ref=lands_here, device_id=(x, y, c),
                                                            **sems))
        for cp in starts:
            cp.start()
        for cp in landing:
            cp.wait_recv()
        for cp in starts:
            cp.wait_send()

    return pl.pallas_call(
        body, name=f"gather_d2d_{tag}", in_specs=[ANY] * (2 * n), out_specs=[ANY] * n,
        out_shape=[jax.ShapeDtypeStruct(s.shape, s.dtype) for s in lands],
        input_output_aliases={n + a: a for a in range(n)},
        scratch_shapes=[pltpu.SemaphoreType.DMA((4 * n,)), pltpu.SemaphoreType.DMA((4 * n,))],
    )(*shards, *lands)


def _rs_add_chips(p32, r16, pos, *, name):
    _, H, C = p32.shape
    tile = _tile(H, 512, 16)
    nt = H // tile

    def body(pos_ref, p_ref, r_ref, o_ref):
        o_ref[...] = ((p_ref[...] + r_ref[0].astype(F32)) + r_ref[1].astype(F32)) + r_ref[2].astype(F32)

    grid_spec = pltpu.PrefetchScalarGridSpec(
        num_scalar_prefetch=1, grid=(nt,),
        in_specs=[_bs((None, tile, C), lambda i, pos_ref: (pos_ref[0], i, 0)),
                  _bs((3, tile, C), lambda i, pos_ref: (0, i, 0))],
        out_specs=_bs((tile, C), lambda i, pos_ref: (pos_ref[1] * nt + i, 0)))
    return pl.pallas_call(
        body, name=name, grid_spec=grid_spec, out_shape=jax.ShapeDtypeStruct((2 * H, C), F32),
        compiler_params=_cparams(("parallel",)),
    )(pos, p32, r16)


def _rs_join_rows(fs, tag):
    n = len(fs)

    def body(*refs):
        outs = refs[n:2 * n]
        send_sems, recv_sems = refs[2 * n], refs[2 * n + 1]
        x, y, c = _my_pos()
        for a in range(n):
            half = outs[a].shape[0] // 2
            mine = outs[a].at[pl.ds(c * half, half)]
            pltpu.make_async_remote_copy(src_ref=mine, dst_ref=mine, send_sem=send_sems.at[a],
                                         recv_sem=recv_sems.at[a], device_id=(x, y, 1 - c), device_id_type=MESH).start()
        for a in range(n):
            half = outs[a].shape[0] // 2
            pltpu.make_async_remote_copy(
                src_ref=outs[a].at[pl.ds(c * half, half)], dst_ref=outs[a].at[pl.ds((1 - c) * half, half)],
                send_sem=send_sems.at[a], recv_sem=recv_sems.at[a], device_id=(x, y, 1 - c), device_id_type=MESH).wait()

    return pl.pallas_call(
        body, name=f"rs_join_rows_{tag}", in_specs=[ANY] * n, out_specs=[ANY] * n,
        out_shape=[jax.ShapeDtypeStruct(f.shape, f.dtype) for f in fs],
        input_output_aliases={a: a for a in range(n)},
        scratch_shapes=[pltpu.SemaphoreType.DMA((n,)), pltpu.SemaphoreType.DMA((n,))],
    )(*fs)


def _pos_vector():
    x, y, c = _my_pos()
    return jnp.stack([2 * x + y, c]).astype(jnp.int32)


def _rs_pair_sums(gs, names, tag):
    pos = _pos_vector()
    r1 = _rs_swap_rows(gs, tag)
    return [_rs_add_pair(g, r, pos, name=f"rs_add_pair_{tag}_{nm}") for g, r, nm in zip(gs, r1, names)]


def _rs_finish(pairs, r2, names, tag):
    pos = _pos_vector()
    fs = [_rs_add_chips(p[0], r, pos, name=f"rs_add_chips_{tag}_{nm}") for p, r, nm in zip(pairs, r2, names)]
    return _rs_join_rows(fs, tag)


def _exchange_land_shapes(pairs):
    return [jax.ShapeDtypeStruct((3,) + p[1].shape[1:], p[1].dtype) for p in pairs]


def _allreduce_small(buf):
    R, W = buf.shape

    def body(b_ref, o_ref, gather, send_sems, recv_sems):
        x, y, c = _my_pos()
        me = 4 * x + 2 * y + c
        gather[me] = b_ref[...]
        cps = []
        for d in range(1, 8):
            peer = (x ^ (d >> 2), y ^ ((d >> 1) & 1), c ^ (d & 1))
            cps.append(pltpu.make_async_remote_copy(
                src_ref=b_ref, dst_ref=gather.at[me], send_sem=send_sems.at[d - 1], recv_sem=recv_sems.at[d - 1],
                device_id=peer, device_id_type=MESH))
        for cp in cps:
            cp.start()
        for d in range(1, 8):
            pltpu.make_async_remote_copy(
                src_ref=b_ref, dst_ref=gather.at[me ^ d], send_sem=send_sems.at[d - 1], recv_sem=recv_sems.at[d - 1],
                device_id=(x, y, c), device_id_type=MESH).wait_recv()
        for cp in cps:
            cp.wait_send()
        acc = gather[0]
        for d in range(1, 8):
            acc = acc + gather[d]
        o_ref[...] = acc

    vm = pl.BlockSpec(memory_space=pltpu.VMEM)
    return pl.pallas_call(
        body, name="allreduce_small", in_specs=[vm], out_specs=vm, out_shape=jax.ShapeDtypeStruct((R, W), F32),
        scratch_shapes=[pltpu.VMEM((8, R, W), F32), pltpu.SemaphoreType.DMA((7,)), pltpu.SemaphoreType.DMA((7,))],
    )(buf)


def _heads(a, h, d):
    return a.reshape(a.shape[0], h, d).transpose(1, 0, 2)


def _unheads(a):
    h, L, d = a.shape
    return a.transpose(1, 0, 2).reshape(L, h * d)


def _rope_tables():
    pos = jnp.maximum(jnp.arange(LP, dtype=F32) - PAD_ROWS, 0.0)
    inv_freq = 1.0 / (ROPE_THETA ** (jnp.arange(0, MLA_ROPE, 2, dtype=F32) / MLA_ROPE))
    ang = pos[:, None] * inv_freq[None, :]
    cos, sin = jnp.tile(jnp.cos(ang), (1, MLA_HEADS)), jnp.tile(jnp.sin(ang), (1, MLA_HEADS))
    return jnp.concatenate([cos, cos], axis=1), jnp.concatenate([-sin, sin], axis=1)


def _lane_pad(a, width=BLOCK):
    return jnp.pad(a, ((0, 0), (0, width - a.shape[1])))


def _pad_in_proj(w):
    sl = lambda start, size: w[:, start:start + size]
    return jnp.concatenate([
        sl(OC_Z, 512), sl(OC_XBC, 768), sl(OC_FQ, 256), sl(OC_FK, 256), sl(OC_FV, 256), sl(OC_CQ, 256), sl(OC_CKV, 128),
        _lane_pad(sl(OC_DT, SSD_HEADS)), _lane_pad(sl(OC_FR, FOX_HEADS)),
        jnp.tile(sl(OC_KR, ROPE_HALF), (1, MLA_HEADS)), jnp.tile(sl(OC_KR + ROPE_HALF, ROPE_HALF), (1, MLA_HEADS))], axis=1)


def _unpad_in_proj(wp):
    sl = lambda start, size: wp[:, start:start + size]
    rope = lambda start: sl(start, 64).reshape(wp.shape[0], MLA_HEADS, ROPE_HALF).sum(axis=1)
    return jnp.concatenate([
        sl(PC_Z, 512), sl(PC_XBC, 768), sl(PC_DT, SSD_HEADS), sl(PC_FQ, 256), sl(PC_FK, 256), sl(PC_FV, 256),
        sl(PC_FR, FOX_HEADS), sl(PC_CQ, 256), sl(PC_CKV, 128), rope(PC_KR), rope(PC_KR + 64)], axis=1)


def _regroup_uq(w):
    w3 = w.reshape(w.shape[0], MLA_HEADS, MLA_NOPE + MLA_ROPE)
    return jnp.concatenate([w3[:, :, :MLA_NOPE].reshape(w.shape[0], -1),
                            w3[:, :, MLA_NOPE:MLA_NOPE + ROPE_HALF].reshape(w.shape[0], -1),
                            w3[:, :, MLA_NOPE + ROPE_HALF:].reshape(w.shape[0], -1)], axis=1)


def _ungroup_uq(wp):
    n = wp.shape[0]
    return jnp.concatenate([wp[:, :256].reshape(n, MLA_HEADS, MLA_NOPE), wp[:, 256:320].reshape(n, MLA_HEADS, ROPE_HALF),
                            wp[:, 320:].reshape(n, MLA_HEADS, ROPE_HALF)], axis=2).reshape(n, -1)


def _regroup_ukv(w):
    w3 = w.reshape(w.shape[0], MLA_HEADS, MLA_NOPE + MLA_V)
    return jnp.concatenate([w3[:, :, :MLA_NOPE].reshape(w.shape[0], -1), w3[:, :, MLA_NOPE:].reshape(w.shape[0], -1)],
                           axis=1)


def _ungroup_ukv(wp):
    n = wp.shape[0]
    return jnp.concatenate([wp[:, :256].reshape(n, MLA_HEADS, MLA_NOPE), wp[:, 256:].reshape(n, MLA_HEADS, MLA_V)],
                           axis=2).reshape(n, -1)


TMF = 1088
N_IF = LP // TMF


def _chunk_rows_mm(a, w, l, chunk_h, *, name, add=None):
    N = w.shape[2]
    return _mm_core(a, w, a_spec=_bs((TMF, chunk_h), lambda i, j, k: (i, k)),
                    b_spec=_bs((None, chunk_h, N), lambda i, j, k: (k, 0, 0)),
                    o_spec=_bs((TMF, N), lambda i, j, k: (i, 0)), grid=(N_IF, 1, N_CHIPS),
                    out_shape=(LP, N), ca=1, cb=0, name=name, add=add)


def _chunk_rows_dx(g, w, l, chunk_h, *, name):
    N = w.shape[2]
    return _mm_core(g, w, a_spec=_bs((TMF, N), lambda i, j, k: (i, 0)),
                    b_spec=_bs((None, chunk_h, N), lambda i, j, k: (j, 0, 0)),
                    o_spec=_bs((TMF, chunk_h), lambda i, j, k: (i, j)), grid=(N_IF, N_CHIPS, 1),
                    out_shape=(LP, N_CHIPS * chunk_h), ca=1, cb=1, name=name)


def _chunk_rows_dw(a, g, chunk_h, *, name):
    N = g.shape[1]
    return _mm_core(a, g, a_spec=_bs((LP, chunk_h), lambda i, j, k: (0, i)), b_spec=_bs((LP, N), lambda i, j, k: (0, 0)),
                    o_spec=_bs((None, chunk_h, N), lambda i, j, k: (i, 0, 0)), grid=(N_CHIPS, 1, 1),
                    out_shape=(N_CHIPS, chunk_h, N), ca=0, cb=0, name=name)


def _ffn_fwd(h, W, pre, l, gam, bet, tag):
    g = _chunk_rows_dx(h, W[pre + "_w_gate"][l], l, HP, name=f"{tag}_gate")
    u = _chunk_rows_dx(h, W[pre + "_w_up"][l], l, HP, name=f"{tag}_up")
    (act,) = _rowwise(_swiglu_fn, [g, u], [], [FP], name=f"{tag}_swiglu", tile=TM, ncol=N_CHIPS, out_dtypes=[BF16])
    o = _chunk_rows_mm(act, W[pre + "_w_down"][l], l, HP, name=f"{tag}_down")
    (out,) = _rowwise(_make_res_ln_fn(0.5), [h, o], [gam, bet], [D_MODEL], name=f"{tag}_ln", tile=272)
    return out, (h, g, u, act, o)


def _ffn_bwd(dout, saved, W, pre, l, gam, bet, GB, tag):
    h, g, u, act, o = saved
    (dh_a, do), (dgam, dbet) = _rowwise_bwd(_make_res_ln_fn(0.5), [h, o], [gam, bet], [dout], name=f"{tag}_ln_bwd",
                                            tile=272, grad_dtypes=[F32, BF16])
    dact = _chunk_rows_dx(do, W[pre + "_w_down"][l], l, HP, name=f"{tag}_down_dx")
    GB[pre + "_w_down"] = _chunk_rows_dw(act, do, HP, name=f"{tag}_down_dw")
    (dg, du), _ = _rowwise_bwd(_swiglu_fn, [g, u], [], [dact], name=f"{tag}_swiglu_bwd", tile=TM, ncol=N_CHIPS,
                               grad_dtypes=[BF16, BF16])
    GB[pre + "_w_gate"] = _chunk_rows_dw(dg, h, HP, name=f"{tag}_gate_dw")
    GB[pre + "_w_up"] = _chunk_rows_dw(du, h, HP, name=f"{tag}_up_dw")
    dh = _chunk_rows_mm(dg, W[pre + "_w_gate"][l], l, HP, add=dh_a, name=f"{tag}_gate_dx")
    dh = _chunk_rows_mm(du, W[pre + "_w_up"][l], l, HP, add=dh, name=f"{tag}_up_dx")
    return dh, dgam, dbet


def _mixer_fwd(h1, W, l, cosf, sins):
    tag = f"l{l}"
    proj = _mm(h1, W["w_in_p"][l], name=f"{tag}_in_proj")
    sv = {"h1": h1, "proj": proj}
    conv_w, conv_b = W["conv_w"][l], W["conv_b"][l][None]
    xc = _conv_fwd(proj, PC_XBC // BLOCK, conv_w, conv_b, name=f"{tag}_conv")
    dt_bias = _lane_pad(W["dt_bias"][l][None])
    (dt,) = _rowwise(_ssd_pre_fn, [(proj, BLOCK, PC_DT // BLOCK)], [dt_bias], [BLOCK], name=f"{tag}_ssd_dt", tile=272)
    xh = _heads(xc[:, :SSD_D], SSD_HEADS, SSD_HD)
    bm = _heads(xc[:, SSD_D:SSD_D + 128], SSD_GROUPS, SSD_STATE)
    cm = _heads(xc[:, SSD_D + 128:], SSD_GROUPS, SSD_STATE)
    dt8 = dt[:, :SSD_HEADS].T
    dtc, dtr = dt8[:, :, None], dt8[:, None, :]
    alog = jnp.broadcast_to(W["a_log"][l][:, None, None], (SSD_HEADS, 1, BLOCK))
    yh, prevs = _ssd_fwd(xh, bm, cm, dtc, dtr, alog, name=f"{tag}_ssd")
    y_raw = _unheads(yh)
    dskip = jnp.repeat(W["d_skip"][l], SSD_HD)[None]
    normg = W["ssd_norm_g"][l][None]
    post_rows = [y_raw, (xc, 256, 0), (proj, 256, PC_Z // 256)]
    (y_ssd,) = _rowwise(_ssd_post_fn, post_rows, [dskip, normg], [SSD_D], name=f"{tag}_ssd_post", tile=272,
                        ncol=SSD_GROUPS)
    sv.update(conv_w=conv_w, conv_b=conv_b, dt_bias=dt_bias, xh=xh, bm=bm, cm=cm, dtc=dtc, dtr=dtr, alog=alog,
              prevs=prevs, post_rows=post_rows, dskip=dskip, normg=normg)
    f_b = _lane_pad(W["fox_f_b"][l][None])
    cg, cgt = _fox_gate_fwd(proj, PC_FR // BLOCK, f_b, name=f"{tag}_fox_gate")
    fox_qkv = ((proj, PC_FQ // ATT_W), (proj, PC_FK // ATT_W), (proj, PC_FV // ATT_W))
    y_fox, lse_f = _attn_fwd(*fox_qkv, scale=FOX_HD ** -0.5, name=f"{tag}_fox_attn", bias=(cg, cgt))
    sv.update(f_b=f_b, cg=cg, cgt=cgt, fox_qkv=fox_qkv, y_fox=y_fox, lse_f=lse_f)
    gq, gkv = W["mla_q_norm_g"][l][None], W["mla_kv_norm_g"][l][None]
    norm_rows = [(proj, 256, PC_CQ // 256), (proj, BLOCK, PC_CKV // BLOCK)]
    qn, cn = _rowwise(_mla_norm_fn, norm_rows, [gq, gkv], [MLA_Q_LORA, MLA_KV_LORA], name=f"{tag}_mla_norm", tile=272,
                      out_dtypes=[BF16, BF16])
    qh = _mm(qn, W["mla_w_uq_p"][l], name=f"{tag}_mla_uq")
    kvh = _mm(cn, W["mla_w_ukv_p"][l], name=f"{tag}_mla_ukv")
    qr, kr = _rowwise(_rope_fn, [(qh, BLOCK, 2), (proj, BLOCK, PC_KR // BLOCK), cosf, sins], [], [BLOCK, BLOCK],
                      name=f"{tag}_rope", tile=272)
    mla_qkv = ((qh, 0), (kvh, 0), (kvh, 1))
    y_mla, lse_m = _attn_fwd(*mla_qkv, scale=(MLA_NOPE + MLA_ROPE) ** -0.5, name=f"{tag}_mla_attn",
                             rope=((qr, 0), (kr, 0)))
    sv.update(gq=gq, gkv=gkv, norm_rows=norm_rows, qn=qn, cn=cn, qr=qr, kr=kr, mla_qkv=mla_qkv, y_mla=y_mla, lse_m=lse_m)
    ycat = jnp.concatenate([y_ssd, y_fox, y_mla], axis=1).astype(BF16)
    mix = _chunk_rows_mm(ycat, W["w_out"][l], l, 256, name=f"{tag}_out_proj")
    (h2,) = _rowwise(_make_res_ln_fn(1.0), [h1, mix], [W["ln2_g"][l][None], W["ln2_b"][l][None]], [D_MODEL],
                     name=f"{tag}_ln2", tile=272)
    sv.update(mix=mix, ycat=ycat)
    return h2, sv


def _mixer_bwd(dh2, sv, W, l, cosf, sins, GB):
    tag = f"l{l}"
    G = {}
    proj = sv["proj"]
    ln2g, ln2b = W["ln2_g"][l][None], W["ln2_b"][l][None]
    (dh1_a, dmix), (dln2g, dln2b) = _rowwise_bwd(
        _make_res_ln_fn(1.0), [sv["h1"], sv["mix"]], [ln2g, ln2b], [dh2], name=f"{tag}_ln2_bwd", tile=272,
        grad_dtypes=[F32, BF16])
    G["ln2_g"], G["ln2_b"] = dln2g[0], dln2b[0]
    dycat = _chunk_rows_dx(dmix, W["w_out"][l], l, 256, name=f"{tag}_out_proj_dx")
    GB["w_out"] = _chunk_rows_dw(sv["ycat"], dmix, 256, name=f"{tag}_out_proj_dw")
    (dy_raw, dxs_a, dz), (ddskip, dnormg) = _rowwise_bwd(
        _ssd_post_fn, sv["post_rows"], [sv["dskip"], sv["normg"]], [dycat[:, :SSD_D]],
        name=f"{tag}_ssd_post_bwd", tile=272, ncol=SSD_GROUPS)
    G["ssd_norm_g"] = dnormg[0]
    G["d_skip"] = ddskip.reshape(SSD_HEADS, SSD_HD).sum(axis=1)
    dxh, dbm, dcm, ddtc, ddtr, dal = _ssd_bwd(sv["xh"], sv["bm"], sv["cm"], sv["dtc"], sv["dtr"], sv["alog"],
                                              sv["prevs"], _heads(dy_raw, SSD_HEADS, SSD_HD), name=f"{tag}_ssd_bwd")
    G["a_log"] = dal[:, 0, 0]
    dxc = jnp.concatenate([dxs_a + _unheads(dxh), _unheads(dbm), _unheads(dcm)], axis=1)
    dxbc, G["conv_w"], dconv_b = _conv_bwd(proj, PC_XBC // BLOCK, sv["conv_w"], sv["conv_b"], dxc,
                                           name=f"{tag}_conv_bwd")
    G["conv_b"] = dconv_b[0]
    ddt = _lane_pad((ddtc[:, :, 0] + ddtr[:, 0, :]).T)
    (ddt_raw,), (ddt_bias,) = _rowwise_bwd(_ssd_pre_fn, [(proj, BLOCK, PC_DT // BLOCK)], [sv["dt_bias"]], [ddt],
                                           name=f"{tag}_ssd_dt_bwd", tile=272)
    G["dt_bias"] = ddt_bias[0, :SSD_HEADS]
    dfq, dfk, dfv, dcg, dcgt = _attn_bwd(*sv["fox_qkv"], sv["y_fox"], sv["lse_f"], (dycat, SSD_D // ATT_W),
                                         scale=FOX_HD ** -0.5, name=f"{tag}_fox_attn_bwd", bias=(sv["cg"], sv["cgt"]))
    df_raw, dfb = _fox_gate_bwd(proj, PC_FR // BLOCK, sv["f_b"], dcg, dcgt, name=f"{tag}_fox_gate_bwd")
    G["fox_f_b"] = dfb[0, :FOX_HEADS]
    dqn_h, dkn_h, dv_h, dqr, dkr = _attn_bwd(
        *sv["mla_qkv"], sv["y_mla"], sv["lse_m"], (dycat, (SSD_D + FOX_D) // ATT_W),
        scale=(MLA_NOPE + MLA_ROPE) ** -0.5, name=f"{tag}_mla_attn_bwd", rope=((sv["qr"], 0), (sv["kr"], 0)))
    dq_rope, dk_rope = _rowwise(_rope_t_fn, [dqr, dkr, cosf, sins], [], [BLOCK, BLOCK], name=f"{tag}_rope_bwd",
                                tile=272)
    dqh = jnp.concatenate([dqn_h, dq_rope], axis=1).astype(BF16)
    dkvh = jnp.concatenate([dkn_h, dv_h], axis=1).astype(BF16)
    dqn = _mm(dqh, W["mla_w_uq_p"][l], tb=True, name=f"{tag}_mla_uq_dx")
    G["mla_w_uq_p"] = _mm(sv["qn"], dqh, ta=True, name=f"{tag}_mla_uq_dw")
    dcn = _mm(dkvh, W["mla_w_ukv_p"][l], tb=True, name=f"{tag}_mla_ukv_dx")
    G["mla_w_ukv_p"] = _mm(sv["cn"], dkvh, ta=True, name=f"{tag}_mla_ukv_dw")
    (dcq, dckv), (dgq, dgkv) = _rowwise_bwd(_mla_norm_fn, sv["norm_rows"], [sv["gq"], sv["gkv"]], [dqn, dcn],
                                            name=f"{tag}_mla_norm_bwd", tile=272)
    G["mla_q_norm_g"], G["mla_kv_norm_g"] = dgq[0], dgkv[0]
    dproj = jnp.concatenate([dz, dxbc, dfq, dfk, dfv, dcq, dckv, ddt_raw, df_raw, dk_rope], axis=1).astype(BF16)
    dh1 = _mm(dproj, W["w_in_p"][l], tb=True, add=dh1_a, name=f"{tag}_in_proj_dx")
    G["w_in_p"] = _mm(sv["h1"], dproj, ta=True, name=f"{tag}_in_proj_dw")
    return dh1, G


def _embed(x, meta):
    return jnp.concatenate([jnp.zeros((PAD_ROWS, D_MODEL), F32), meta, x], axis=0)


def _layer_fwd(h, W, l, cosf, sins):
    ln = lambda n: W[n][l][None]
    h1, s1 = _ffn_fwd(h, W, "ffn1", l, ln("ln1_g"), ln("ln1_b"), f"l{l}_ffn1")
    h2, sm = _mixer_fwd(h1, W, l, cosf, sins)
    h3, s2 = _ffn_fwd(h2, W, "ffn2", l, ln("ln3_g"), ln("ln3_b"), f"l{l}_ffn2")
    return h3, (s1, sm, s2)


def _layer_bwd(dh, saved, W, l, cosf, sins):
    ln = lambda n: W[n][l][None]
    s1, sm, s2 = saved
    G = {}
    dh, dg, db = _ffn_bwd(dh, s2, W, "ffn2", l, ln("ln3_g"), ln("ln3_b"), G, f"l{l}_ffn2")
    G["ln3_g"], G["ln3_b"] = dg[0], db[0]
    dh, Gm = _mixer_bwd(dh, sm, W, l, cosf, sins, G)
    G.update(Gm)
    dh, dg, db = _ffn_bwd(dh, s1, W, "ffn1", l, ln("ln1_g"), ln("ln1_b"), G, f"l{l}_ffn1")
    G["ln1_g"], G["ln1_b"] = dg[0], db[0]
    return dh, G


def _local_step(x, target, W):
    h = _embed(x, W["meta"])
    tgt = jnp.concatenate([jnp.zeros((BLOCK, D_MODEL), F32), target], axis=0)
    cosf, sins = _rope_tables()
    saved = []
    for l in range(DEPTH):
        h, sv = _layer_fwd(h, W, l, cosf, sins)
        saved.append(sv)
    dh, loss = _loss_head(h, tgt, name="loss_head")
    grads = [None] * DEPTH
    for l in reversed(range(DEPTH)):
        dh, grads[l] = _layer_bwd(dh, saved[l], W, l, cosf, sins)
    return loss, dh, grads


WEIGHTS = ['meta', 'ffn1_w_gate', 'ffn1_w_up', 'ffn1_w_down', 'ln1_g', 'ln1_b', 'w_in', 'conv_w', 'conv_b', 'dt_bias',
           'a_log', 'd_skip', 'ssd_norm_g', 'fox_f_b', 'mla_q_norm_g', 'mla_w_uq', 'mla_kv_norm_g', 'mla_w_ukv',
           'w_out', 'ln2_g', 'ln2_b', 'ffn2_w_gate', 'ffn2_w_up', 'ffn2_w_down', 'ln3_g', 'ln3_b']
SMALL = ["ln1_g", "ln1_b", "conv_b", "dt_bias", "a_log", "d_skip", "ssd_norm_g", "fox_f_b", "mla_q_norm_g",
         "mla_kv_norm_g", "ln2_g", "ln2_b", "ln3_g", "ln3_b"]
MATMUL_W = ["ffn1_w_gate", "ffn1_w_up", "ffn1_w_down", "w_in", "mla_w_uq", "mla_w_ukv", "w_out", "ffn2_w_gate",
            "ffn2_w_up", "ffn2_w_down"]
SMALL_ROWS = 312


def _pad_to(a, axis, size):
    pads = [(0, 0)] * a.ndim
    pads[axis] = (0, size - a.shape[axis])
    return jnp.pad(a, pads)


def _chip_cols(full, chip, width):
    return lax.dynamic_slice_in_dim(full, chip * width, width, axis=full.ndim - 1)


def kernel(x, meta, ffn1_w_gate, ffn1_w_up, ffn1_w_down, ln1_g, ln1_b, w_in, conv_w, conv_b, dt_bias, a_log, d_skip, ssd_norm_g, fox_f_b, mla_q_norm_g, mla_w_uq, mla_kv_norm_g, mla_w_ukv, w_out, ln2_g, ln2_b, ffn2_w_gate, ffn2_w_up, ffn2_w_down, ln3_g, ln3_b, loss_target, m_meta, m_ffn1_w_gate, m_ffn1_w_up, m_ffn1_w_down, m_ln1_g, m_ln1_b, m_w_in, m_conv_w, m_conv_b, m_dt_bias, m_a_log, m_d_skip, m_ssd_norm_g, m_fox_f_b, m_mla_q_norm_g, m_mla_w_uq, m_mla_kv_norm_g, m_mla_w_ukv, m_w_out, m_ln2_g, m_ln2_b, m_ffn2_w_gate, m_ffn2_w_up, m_ffn2_w_down, m_ln3_g, m_ln3_b, v_meta, v_ffn1_w_gate, v_ffn1_w_up, v_ffn1_w_down, v_ln1_g, v_ln1_b, v_w_in, v_conv_w, v_conv_b, v_dt_bias, v_a_log, v_d_skip, v_ssd_norm_g, v_fox_f_b, v_mla_q_norm_g, v_mla_w_uq, v_mla_kv_norm_g, v_mla_w_ukv, v_w_out, v_ln2_g, v_ln2_b, v_ffn2_w_gate, v_ffn2_w_up, v_ffn2_w_down, v_ln3_g, v_ln3_b):
    args = dict(locals())
    w = {n: args[n] for n in WEIGHTS}
    m = {n: args["m_" + n] for n in WEIGHTS}
    v = {n: args["v_" + n] for n in WEIGHTS}
    xcoord, ycoord, _ = _my_pos()
    chip = 2 * xcoord + ycoord

    send = {}
    tr = lambda a: jnp.swapaxes(a, 1, 2)
    for pre in ("ffn1", "ffn2"):
        send[pre + "_w_gate"] = _pad_to(tr(w[pre + "_w_gate"]), 1, HP).astype(BF16)
        send[pre + "_w_up"] = _pad_to(tr(w[pre + "_w_up"]), 1, HP).astype(BF16)
        send[pre + "_w_down"] = _pad_to(w[pre + "_w_down"], 1, HP).astype(BF16)
    send["w_in"] = _pad_to(w["w_in"], 2, IN_SHARD_P).astype(BF16)
    for n in ("mla_w_uq", "mla_w_ukv", "w_out"):
        send[n] = w[n].astype(BF16)
    tiny = _allgather_chips([w["meta"].reshape(2, N_META // 2, D_MODEL // N_CHIPS), w["conv_w"]])
    meta_full = jnp.concatenate([tiny[0][k].reshape(N_META, D_MODEL // N_CHIPS) for k in range(N_CHIPS)], axis=1)
    shards = [send[n] for n in MATMUL_W]
    land_shapes = [jax.ShapeDtypeStruct((N_CHIPS,) + s.shape[1:], s.dtype) for s in shards]
    gather_l = lambda l: (lambda srcs, lands, ss, rs: _gather_copies(srcs, lands, l, ss, rs))

    W = {n: [None] * DEPTH for n in MATMUL_W + ["w_in_p", "mla_w_uq_p", "mla_w_ukv_p"]}
    W["conv_w"] = jnp.concatenate([tiny[1][k] for k in range(N_CHIPS)], axis=-1)
    W["meta"] = meta_full
    for n in SMALL:
        W[n] = w[n]

    def use_gathered(l, names, lands):
        got = dict(zip(names, lands))
        cat = lambda n, cut=None: jnp.concatenate([got[n][k][..., :cut] for k in range(N_CHIPS)], axis=-1)
        for n in names:
            W[n][l] = got[n]
        if "w_in" in got:
            W["w_in_p"][l] = _pad_in_proj(cat("w_in", IN_SHARD))
            W["mla_w_uq_p"][l] = _regroup_uq(cat("mla_w_uq"))
            W["mla_w_ukv_p"][l] = _regroup_ukv(cat("mla_w_ukv"))

    def chunk_grads(G):
        def chunked(name, ungroup, width, pad):
            full = ungroup(G[name])
            return _pad_to(jnp.moveaxis(full.reshape(full.shape[0], N_CHIPS, width), 1, 0), 2, pad)
        G = dict(G)
        G["w_in"] = chunked("w_in_p", _unpad_in_proj, IN_SHARD, IN_SHARD_P)
        G["mla_w_uq"] = chunked("mla_w_uq_p", _ungroup_uq, MLA_NOPE + MLA_ROPE, MLA_NOPE + MLA_ROPE)
        G["mla_w_ukv"] = chunked("mla_w_ukv_p", _ungroup_ukv, MLA_NOPE + MLA_V, MLA_NOPE + MLA_V)
        return [G[n] for n in MATMUL_W]

    na = 3
    first, rest = MATMUL_W[:na], MATMUL_W[na:]
    lands = _ici_blocking(gather_l(0), shards[:na], land_shapes[:na], name="gather_ici_l0_ffn1", after=[tiny[0]])
    got = _gather_d2d(shards[:na], lands, 0, "l0_ffn1")
    use_gathered(0, first, got)
    g_send, g_recv, g_srcs, g_lands, token = _ici_start(gather_l(0), shards[na:], land_shapes[na:],
                                                        name="gather_ici_l0_rest_start", after=[got[0]])
    cosf, sins = _rope_tables()
    ln = lambda n, l: W[n][l][None]
    h = _embed(x[0] + token[0, 0], meta_full)
    h1, s1 = _ffn_fwd(h, W, "ffn1", 0, ln("ln1_g", 0), ln("ln1_b", 0), "l0_ffn1")
    rest_shards, lands = _ici_wait(gather_l(0), g_send, g_recv, g_srcs, g_lands, h1, name="gather_ici_l0_rest_wait")
    shards = shards[:na] + rest_shards
    got = _gather_d2d(shards[na:], lands, 0, "l0_rest")
    use_gathered(0, rest, got)
    g_send, g_recv, g_srcs, g_lands, token = _ici_start(gather_l(1), shards, land_shapes, name="gather_ici_l1_start",
                                                        after=[got[0]])
    h2, sm = _mixer_fwd(h1 + token[0, 0], W, 0, cosf, sins)
    h, s2 = _ffn_fwd(h2, W, "ffn2", 0, ln("ln3_g", 0), ln("ln3_b", 0), "l0_ffn2")
    saved0 = (s1, sm, s2)
    shards, lands = _ici_wait(gather_l(1), g_send, g_recv, g_srcs, g_lands, h, name="gather_ici_l1_wait")
    use_gathered(1, MATMUL_W, _gather_d2d(shards, lands, 1, "l1"))
    h, saved1 = _layer_fwd(h, W, 1, cosf, sins)
    tgt = jnp.concatenate([jnp.zeros((BLOCK, D_MODEL), F32), loss_target[0]], axis=0)
    dh, loss = _loss_head(h, tgt, name="loss_head")
    G = [None] * DEPTH
    dh, G[1] = _layer_bwd(dh, saved1, W, 1, cosf, sins)
    pairs1 = _rs_pair_sums(chunk_grads(G[1]), MATMUL_W, "l1")
    x_send, x_recv, x_srcs, x_lands, token = _ici_start(_exchange_copies, [p[1] for p in pairs1],
                                                        _exchange_land_shapes(pairs1), name="rs_exchange_l1_start")
    dh0, G[0] = _layer_bwd(dh + token[0, 0], saved0, W, 0, cosf, sins)
    _, r2 = _ici_wait(_exchange_copies, x_send, x_recv, x_srcs, x_lands, dh0, name="rs_exchange_l1_wait")
    reduced1 = _rs_finish(pairs1, r2, MATMUL_W, "l1")
    pairs0 = _rs_pair_sums(chunk_grads(G[0]), MATMUL_W, "l0")
    r2 = _ici_blocking(_exchange_copies, [p[1] for p in pairs0], _exchange_land_shapes(pairs0), name="rs_exchange_l0")
    reduced0 = _rs_finish(pairs0, r2, MATMUL_W, "l0")
    reduced = {n: [reduced0[i], reduced1[i]] for i, n in enumerate(MATMUL_W)}

    small_parts = [jnp.stack([G[l][n] for l in range(DEPTH)]).reshape(-1) for n in SMALL]
    small_parts += [jnp.stack([G[l]["conv_w"] for l in range(DEPTH)]).reshape(-1), dh0[PAD_ROWS:BLOCK].reshape(-1),
                    loss[0, :1]]
    flat = jnp.concatenate(small_parts)
    flat = jnp.pad(flat, (0, SMALL_ROWS * BLOCK - flat.shape[0]))
    red = _allreduce_small(flat.reshape(SMALL_ROWS, BLOCK)).reshape(-1)
    grads, off = {}, 0
    for n in SMALL:
        size = int(np.prod(w[n].shape))
        grads[n] = red[off:off + size].reshape(w[n].shape)
        off += size
    conv_full = red[off:off + DEPTH * SSD_CONV * 768].reshape(DEPTH, SSD_CONV, 768)
    off += DEPTH * SSD_CONV * 768
    dmeta_full = red[off:off + N_META * D_MODEL].reshape(N_META, D_MODEL)
    off += N_META * D_MODEL
    loss_out = red[off]
    grads["conv_w"] = _chip_cols(conv_full, chip, 768 // N_CHIPS)
    grads["meta"] = _chip_cols(dmeta_full, chip, D_MODEL // N_CHIPS)

    delta, new_m, new_v = {}, {}, {}
    for n in MATMUL_W:
        if n.endswith("w_gate") or n.endswith("w_up"):
            res = _adamw(tr(w[n]), reduced[n], tr(m[n]), tr(v[n]), name=f"adamw_{n}")
            grads[n], delta[n], new_m[n], new_v[n] = [tr(r) for r in res]
        else:
            grads[n], delta[n], new_m[n], new_v[n] = _adamw(w[n], reduced[n], m[n], v[n], name=f"adamw_{n}")
    rest =[n for n in WEIGHTS if n not in MATMUL_W]

    def pack_small(d):
        f = jnp.concatenate([d[n].reshape(-1) for n in rest])
        tot = -(-f.shape[0] // (8 * BLOCK)) * 8 * BLOCK
        return jnp.pad(f, (0, tot - f.shape[0])).reshape(-1, BLOCK)

    _, d2, m2, v2 = _adamw(pack_small(w), [pack_small(grads)], pack_small(m), pack_small(v), name="adamw_small")
    off = 0
    for n in rest:
        size = int(np.prod(w[n].shape))
        for dst, src in ((delta, d2), (new_m, m2), (new_v, v2)):
            dst[n] = src.reshape(-1)[off:off + size].reshape(w[n].shape)
        off += size

    grad_x = dh0[BLOCK:][None]
    return (loss_out, grad_x, *[grads[n] for n in WEIGHTS], *[delta[n] for n in WEIGHTS],
            *[new_m[n] for n in WEIGHTS], *[new_v[n] for n in WEIGHTS])
```

```python
import functools

import numpy as np
import jax
import jax.numpy as jnp
from jax import lax
from jax.experimental import pallas as pl
from jax.experimental.pallas import tpu as pltpu

F32 = jnp.float32
BF16 = jnp.bfloat16
MESH = pl.DeviceIdType.MESH

D_MODEL = 1024
SEQ = 2048
N_META = 16
BLOCK = 128
PAD_ROWS = 112
LP = PAD_ROWS + N_META + SEQ
N_CHUNK = LP // BLOCK
DEPTH = 2
D_FF = 2816
N_CHIPS = 4
FF_SHARD = D_FF // N_CHIPS
HP = 768
FP = N_CHIPS * HP
SSD_HEADS, SSD_HD, SSD_D, SSD_GROUPS, SSD_STATE, SSD_CONV = 8, 64, 512, 2, 64, 4
FOX_HEADS, FOX_HD, FOX_D = 4, 64, 256
MLA_HEADS, MLA_Q_LORA, MLA_KV_LORA, MLA_NOPE, MLA_ROPE, MLA_V, MLA_D = 4, 256, 128, 64, 32, 64, 256
ROPE_HALF = MLA_ROPE // 2
ROPE_THETA = 10000.0
N_IN = 2476
IN_SHARD = N_IN // N_CHIPS
IN_SHARD_P = 640
ALPHA = (2 * DEPTH) ** 0.25
EPS = 1e-5
ADAM_LR, ADAM_B1, ADAM_B2, ADAM_EPS, ADAM_WD, ADAM_STEP = 0.001, 0.9, 0.999, 1e-08, 0.01, 10
NEG = -1e30
TM = 544

VMEM_LIMIT_BYTES = 56 * 1024 * 1024

PC_Z, PC_XBC, PC_FQ, PC_FK, PC_FV, PC_CQ, PC_CKV, PC_DT, PC_FR, PC_KR, PC_END = (
    0, 512, 1280, 1536, 1792, 2048, 2304, 2432, 2560, 2688, 2816)
OC_Z, OC_XBC, OC_DT, OC_FQ, OC_FK, OC_FV, OC_FR, OC_CQ, OC_CKV, OC_KR = (
    0, 512, 1280, 1288, 1544, 1800, 2056, 2060, 2316, 2444)


def _cparams(sem=None):
    return pltpu.CompilerParams(dimension_semantics=sem, vmem_limit_bytes=VMEM_LIMIT_BYTES)


def _tile(n, cap, mult):
    best = None
    for t in range(mult, min(n, cap) + 1, mult):
        if n % t == 0:
            best = t
    return best if best is not None else n


def _bs(shape, fn):
    return pl.BlockSpec(shape, fn)


ANY = pl.BlockSpec(memory_space=pl.ANY)


def _dims(ca, cb):
    return (((ca,), (cb,)), ((), ()))


def _raw_bdot(a, b, ca, cb):
    return lax.dot_general(a.astype(BF16), b.astype(BF16), _dims(ca, cb), preferred_element_type=F32)


@functools.partial(jax.custom_vjp, nondiff_argnums=(2, 3))
def _bdot(a, b, ca, cb):
    return _raw_bdot(a, b, ca, cb)


def _bdot_fwd(a, b, ca, cb):
    return _raw_bdot(a, b, ca, cb), (a, b)


def _bdot_bwd(ca, cb, res, g):
    a, b = res
    if (ca, cb) == (1, 0):
        return _raw_bdot(g, b, 1, 1), _raw_bdot(a, g, 0, 0)
    if (ca, cb) == (1, 1):
        return _raw_bdot(g, b, 1, 0), _raw_bdot(g, a, 0, 0)
    if (ca, cb) == (0, 0):
        return _raw_bdot(b, g, 1, 1), _raw_bdot(a, g, 1, 0)
    raise NotImplementedError((ca, cb))


_bdot.defvjp(_bdot_fwd, _bdot_bwd)


def _mm_core(a, b, *, a_spec, b_spec, o_spec, grid, out_shape, ca, cb, name, add=None):
    nk = grid[2]
    has_add = add is not None
    acc_shape = tuple(d for d in o_spec.block_shape if d is not None)

    def body(*refs):
        a_ref, b_ref = refs[0], refs[1]
        add_ref = refs[2] if has_add else None
        o_ref, acc_ref = refs[-2], refs[-1]
        k = pl.program_id(2)

        @pl.when(k == 0)
        def _():
            acc_ref[...] = jnp.zeros_like(acc_ref)

        acc_ref[...] += _raw_bdot(a_ref[...], b_ref[...], ca, cb)

        @pl.when(k == nk - 1)
        def _():
            r = acc_ref[...]
            if has_add:
                r = r + add_ref[...]
            o_ref[...] = r

    ins = [a, b] + ([add] if has_add else [])
    in_specs = [a_spec, b_spec] + ([o_spec] if has_add else [])
    return pl.pallas_call(
        body, name=name, grid=grid, in_specs=in_specs, out_specs=o_spec,
        out_shape=jax.ShapeDtypeStruct(out_shape, F32), scratch_shapes=[pltpu.VMEM(acc_shape, F32)],
        compiler_params=_cparams(("parallel", "parallel", "arbitrary")),
    )(*ins)


MM_VMEM_BUDGET = 40 * 1024 * 1024


def _divisors(n, mult):
    return [t for t in range(mult, n + 1, mult) if n % t == 0] or [n]


def _pick_tiles(M, N, K, a_bytes, b_bytes, ta, has_add):
    best = None
    for tm in _divisors(M, 128 if ta else 16):
        for tn in _divisors(N, 128):
            vmem = 2 * tm * K * a_bytes + 2 * K * tn * b_bytes + (3 + 2 * int(has_add)) * tm * tn * 4
            if vmem <= MM_VMEM_BUDGET:
                key = ((M // tm) * (N // tn), -tn)
                if best is None or key < best[0]:
                    best = (key, tm, tn)
    assert best is not None, (M, N, K)
    return best[1], best[2], K


def _mm(a, b, *, ta=False, tb=False, add=None, name):
    if ta:
        K, M = a.shape
    else:
        M, K = a.shape
    if tb:
        N, Kb = b.shape
    else:
        Kb, N = b.shape
    assert K == Kb, (a.shape, b.shape, ta, tb)
    tm, tn, tk = _pick_tiles(M, N, K, a.dtype.itemsize, b.dtype.itemsize, ta, add is not None)
    a_spec = _bs((tk, tm), lambda i, j, k: (k, i)) if ta else _bs((tm, tk), lambda i, j, k: (i, k))
    b_spec = _bs((tn, tk), lambda i, j, k: (j, k)) if tb else _bs((tk, tn), lambda i, j, k: (k, j))
    return _mm_core(a, b, a_spec=a_spec, b_spec=b_spec, o_spec=_bs((tm, tn), lambda i, j, k: (i, j)),
                    grid=(M // tm, N // tn, K // tk), out_shape=(M, N), ca=0 if ta else 1, cb=1 if tb else 0,
                    name=name, add=add)


def _row_entry(r, ncol):
    if isinstance(r, tuple):
        return r
    return r, r.shape[1] // ncol, 0


def _rowwise(fn, rows, pars, out_cols, *, name, tile, ncol=1, out_dtypes=None):
    rows = [_row_entry(r, ncol) for r in rows]
    L = rows[0][0].shape[0]
    nr, npar = len(rows), len(pars)
    in_specs = [_bs((tile, w), lambda g, i, o=o: (i, o + g)) for _, w, o in rows]
    in_specs += [_bs((p.shape[0], p.shape[1] // ncol), lambda g, i: (0, g)) for p in pars]
    out_specs = [_bs((tile, c // ncol), lambda g, i: (i, g)) for c in out_cols]

    def body(*refs):
        ins, outs = refs[:nr + npar], refs[nr + npar:]
        row0 = pl.program_id(1) * tile
        res = fn(row0, *[r[...] for r in ins])
        for o, v in zip(outs, res):
            o[...] = v.astype(o.dtype)

    return pl.pallas_call(
        body, name=name, grid=(ncol, L // tile), in_specs=in_specs, out_specs=out_specs,
        out_shape=[jax.ShapeDtypeStruct((L, c), d) for c, d in zip(out_cols, out_dtypes or [F32] * len(out_cols))],
        compiler_params=_cparams(("parallel", "parallel")),
    )(*[r[0] for r in rows], *pars)


def _rowwise_bwd(fn, rows, pars, douts, *, name, tile, ncol=1, row_grad=None, grad_dtypes=None):
    rows = [_row_entry(r, ncol) for r in rows]
    L = rows[0][0].shape[0]
    nr, npar, nd = len(rows), len(pars), len(douts)
    row_grad = [True] * nr if row_grad is None else row_grad
    in_specs = [_bs((tile, w), lambda g, i, o=o: (i, o + g)) for _, w, o in rows]
    in_specs += [_bs((p.shape[0], p.shape[1] // ncol), lambda g, i: (0, g)) for p in pars]
    in_specs += [_bs((tile, d.shape[1] // ncol), lambda g, i: (i, g)) for d in douts]
    g_widths = [w * ncol for (_, w, _), f in zip(rows, row_grad) if f]
    out_specs = [_bs((tile, w // ncol), lambda g, i: (i, g)) for w in g_widths]
    out_specs += [_bs((p.shape[0], p.shape[1] // ncol), lambda g, i: (0, g)) for p in pars]
    out_shape = [jax.ShapeDtypeStruct((L, w), d) for w, d in zip(g_widths, grad_dtypes or [F32] * len(g_widths))]
    out_shape += [jax.ShapeDtypeStruct(p.shape, F32) for p in pars]

    def body(*refs):
        ins = refs[:nr + npar]
        dos = refs[nr + npar:nr + npar + nd]
        outs = refs[nr + npar + nd:]
        i = pl.program_id(1)
        row0 = i * tile
        _, vjp = jax.vjp(lambda *a: tuple(fn(row0, *a)), *[r[...] for r in ins])
        grads = vjp(tuple(d[...].astype(F32) for d in dos))
        o = 0
        for j in range(nr):
            if row_grad[j]:
                outs[o][...] = grads[j].astype(outs[o].dtype)
                o += 1
        for j in range(npar):
            g, ref = grads[nr + j], outs[o + j]

            @pl.when(i == 0)
            def _(g=g, ref=ref):
                ref[...] = g

            @pl.when(i > 0)
            def _(g=g, ref=ref):
                ref[...] += g

    res = pl.pallas_call(
        body, name=name, grid=(ncol, L // tile), in_specs=in_specs, out_specs=out_specs, out_shape=out_shape,
        compiler_params=_cparams(("parallel", "arbitrary")),
    )(*[r[0] for r in rows], *pars, *douts)
    return res[:len(g_widths)], res[len(g_widths):]


def _row_ids(row0, shape):
    return row0 + lax.broadcasted_iota(jnp.int32, shape, 0)


def _sigmoid(x):
    return 1.0 / (1.0 + jnp.exp(-x))


@jax.custom_vjp
def _softplus(x):
    return jnp.maximum(x, 0.0) + jnp.log(1.0 + jnp.exp(-jnp.abs(x)))


def _softplus_fwd(x):
    return _softplus(x), x


def _softplus_bwd(x, g):
    return (g * _sigmoid(x),)


_softplus.defvjp(_softplus_fwd, _softplus_bwd)


def _silu(x):
    return x * _sigmoid(x)


def _swiglu_fn(row0, g, u):
    return (_silu(g) * u,)


def _make_res_ln_fn(scale):
    def fn(row0, h, o, gam, bet):
        pre = ALPHA * h + scale * o
        mu = jnp.mean(pre, axis=-1, keepdims=True)
        xc = pre - mu
        var = jnp.mean(xc * xc, axis=-1, keepdims=True)
        return (xc * lax.rsqrt(var + EPS) * gam + bet,)
    return fn


def _ssd_pre_fn(row0, raw, bias):
    dt = _softplus(raw + bias)
    return (jnp.where(_row_ids(row0, raw.shape) >= PAD_ROWS, dt, 0.0),)


def _ssd_post_fn(row0, y, xs, z, dskip, normg):
    v = (y + dskip * xs) * _silu(z)
    v = v * lax.rsqrt(jnp.mean(v * v, axis=-1, keepdims=True) + EPS)
    return (v * normg,)


def _mla_norm_fn(row0, cq, ckv, gq, gkv):
    qn = cq * lax.rsqrt(jnp.mean(cq * cq, axis=-1, keepdims=True) + EPS) * gq
    cn = ckv * lax.rsqrt(jnp.mean(ckv * ckv, axis=-1, keepdims=True) + EPS) * gkv
    return qn, cn


def _rope_fn(row0, q, k, cosf, sins):
    return (q * cosf + pltpu.roll(q, 64, 1) * sins, k * cosf + pltpu.roll(k, 64, 1) * sins)


def _rope_t_fn(row0, gq, gk, cosf, sins):
    return (gq * cosf + pltpu.roll(gq * sins, 64, 1), gk * cosf + pltpu.roll(gk * sins, 64, 1))


def _conv_fwd(x, x_off, w, b, *, name):
    C = w.shape[1]

    def body(x_ref, w_ref, b_ref, o_ref):
        rows = lax.broadcasted_iota(jnp.int32, (LP, BLOCK), 0)
        xv = jnp.where(rows >= PAD_ROWS, x_ref[...], 0.0)
        acc = b_ref[...] + w_ref[3:4, :] * xv
        for k in range(SSD_CONV - 1):
            acc = acc + w_ref[k:k + 1, :] * pltpu.roll(xv, SSD_CONV - 1 - k, 0)
        o_ref[...] = _silu(acc)

    return pl.pallas_call(
        body, name=name, grid=(C // BLOCK,),
        in_specs=[_bs((LP, BLOCK), lambda j: (0, j + x_off)), _bs((SSD_CONV, BLOCK), lambda j: (0, j)),
                  _bs((1, BLOCK), lambda j: (0, j))],
        out_specs=_bs((LP, BLOCK), lambda j: (0, j)),
        out_shape=jax.ShapeDtypeStruct((LP, C), F32), compiler_params=_cparams(("parallel",)),
    )(x, w, b)


def _conv_bwd(x, x_off, w, b, dout, *, name):
    C = w.shape[1]

    def body(x_ref, w_ref, b_ref, do_ref, dx_ref, dw_ref, db_ref):
        rows = lax.broadcasted_iota(jnp.int32, (LP, BLOCK), 0)
        real = rows >= PAD_ROWS
        xv = jnp.where(real, x_ref[...], 0.0)
        shifted = [pltpu.roll(xv, SSD_CONV - 1 - k, 0) for k in range(SSD_CONV - 1)] + [xv]
        acc = b_ref[...]
        for k in range(SSD_CONV):
            acc = acc + w_ref[k:k + 1, :] * shifted[k]
        sig = _sigmoid(acc)
        dacc = jnp.where(real, do_ref[...] * (sig * (1.0 + acc * (1.0 - sig))), 0.0)
        db_ref[...] = jnp.sum(dacc, axis=0, keepdims=True)
        dx = w_ref[3:4, :] * dacc
        for k in range(SSD_CONV):
            dw_ref[k:k + 1, :] = jnp.sum(dacc * shifted[k], axis=0, keepdims=True)
            if k < SSD_CONV - 1:
                dx = dx + w_ref[k:k + 1, :] * pltpu.roll(dacc, LP - (SSD_CONV - 1 - k), 0)
        dx_ref[...] = jnp.where(real, dx, 0.0)

    return pl.pallas_call(
        body, name=name, grid=(C // BLOCK,),
        in_specs=[_bs((LP, BLOCK), lambda j: (0, j + x_off)), _bs((SSD_CONV, BLOCK), lambda j: (0, j)),
                  _bs((1, BLOCK), lambda j: (0, j)), _bs((LP, BLOCK), lambda j: (0, j))],
        out_specs=[_bs((LP, BLOCK), lambda j: (0, j)), _bs((SSD_CONV, BLOCK), lambda j: (0, j)),
                   _bs((1, BLOCK), lambda j: (0, j))],
        out_shape=[jax.ShapeDtypeStruct((LP, C), F32), jax.ShapeDtypeStruct((SSD_CONV, C), F32),
                   jax.ShapeDtypeStruct((1, C), F32)],
        compiler_params=_cparams(("parallel",)),
    )(x, w, b, dout)


_BDIMS = {"nn": (((2,), (1,)), ((0,), (0,))), "nt": (((2,), (2,)), ((0,), (0,))), "tn": (((1,), (1,)), ((0,), (0,)))}


def _raw_bdot3(a, b, mode):
    return lax.dot_general(a.astype(BF16), b.astype(BF16), _BDIMS[mode], preferred_element_type=F32)


@functools.partial(jax.custom_vjp, nondiff_argnums=(2,))
def _bdot3(a, b, mode):
    return _raw_bdot3(a, b, mode)


def _bdot3_fwd(a, b, mode):
    return _raw_bdot3(a, b, mode), (a, b)


def _bdot3_bwd(mode, res, g):
    a, b = res
    if mode == "nn":
        return _raw_bdot3(g, b, "nt"), _raw_bdot3(a, g, "tn")
    if mode == "nt":
        return _raw_bdot3(g, b, "nn"), _raw_bdot3(g, a, "tn")
    return _raw_bdot3(b, g, "nt"), _raw_bdot3(a, g, "nn")


_bdot3.defvjp(_bdot3_fwd, _bdot3_bwd)


def _ssd_chunk(x, bm, cm, dtc, dtr, alog, prev):
    rep = SSD_HEADS // SSD_GROUPS
    per_head = lambda t: jnp.broadcast_to(t[:, None], (SSD_GROUPS, rep) + t.shape[1:]).reshape((SSD_HEADS,) + t.shape[1:])
    bm, cm = per_head(bm), per_head(cm)
    lane = lax.broadcasted_iota(jnp.int32, alog.shape, 2)
    a_neg = -jnp.exp(jnp.sum(jnp.where(lane == 0, alog, 0.0), axis=2, keepdims=True))
    ac_in = dtc * a_neg
    ar_in = dtr * a_neg
    li = lax.broadcasted_iota(jnp.int32, (1, BLOCK, BLOCK), 1)
    si = lax.broadcasted_iota(jnp.int32, (1, BLOCK, BLOCK), 2)
    causal = li >= si
    acum_c = jnp.sum(jnp.where(causal, ar_in, 0.0), axis=2, keepdims=True)
    acum_r = jnp.sum(jnp.where(li <= si, ac_in, 0.0), axis=1, keepdims=True)
    total = jnp.sum(ar_in, axis=2, keepdims=True)
    seg = jnp.exp(jnp.where(causal, acum_c - acum_r, NEG))
    xdt = x * dtc
    cb = _bdot3(cm, bm, "nt")
    y = _bdot3(cb * seg, xdt, "nn") + _bdot3(cm, prev, "nt") * jnp.exp(acum_c)
    st = _bdot3(xdt, bm * jnp.exp(total - acum_c), "tn")
    return y, prev * jnp.exp(total) + st


def _ssd_specs(rev):
    ci = (lambda c: N_CHUNK - 1 - c) if rev else (lambda c: c)
    x_spec = _bs((SSD_HEADS, BLOCK, SSD_HD), lambda c: (0, ci(c), 0))
    g_spec = _bs((SSD_GROUPS, BLOCK, SSD_STATE), lambda c: (0, ci(c), 0))
    dtc_spec = _bs((SSD_HEADS, BLOCK, 1), lambda c: (0, ci(c), 0))
    dtr_spec = _bs((SSD_HEADS, 1, BLOCK), lambda c: (0, 0, ci(c)))
    al_spec = _bs((SSD_HEADS, 1, BLOCK), lambda c: (0, 0, 0))
    st_spec = _bs((None, SSD_HEADS, SSD_HD, SSD_STATE), lambda c: (ci(c), 0, 0, 0))
    return x_spec, g_spec, dtc_spec, dtr_spec, al_spec, st_spec


def _ssd_fwd(x, bm, cm, dtc, dtr, alog, *, name):
    x_spec, g_spec, dtc_spec, dtr_spec, al_spec, st_spec = _ssd_specs(False)

    def body(x_ref, b_ref, c_ref, dtc_ref, dtr_ref, al_ref, y_ref, prev_ref, state):
        @pl.when(pl.program_id(0) == 0)
        def _():
            state[...] = jnp.zeros_like(state)

        prev = state[...]
        prev_ref[...] = prev
        y, new = _ssd_chunk(x_ref[...], b_ref[...], c_ref[...], dtc_ref[...], dtr_ref[...], al_ref[...], prev)
        y_ref[...] = y
        state[...] = new

    return pl.pallas_call(
        body, name=name, grid=(N_CHUNK,),
        in_specs=[x_spec, g_spec, g_spec, dtc_spec, dtr_spec, al_spec], out_specs=[x_spec, st_spec],
        out_shape=[jax.ShapeDtypeStruct((SSD_HEADS, LP, SSD_HD), F32),
                   jax.ShapeDtypeStruct((N_CHUNK, SSD_HEADS, SSD_HD, SSD_STATE), F32)],
        scratch_shapes=[pltpu.VMEM((SSD_HEADS, SSD_HD, SSD_STATE), F32)],
        compiler_params=_cparams(("arbitrary",)),
    )(x, bm, cm, dtc, dtr, alog)


def _ssd_bwd(x, bm, cm, dtc, dtr, alog, prevs, dy, *, name):
    x_spec, g_spec, dtc_spec, dtr_spec, al_spec, st_spec = _ssd_specs(True)

    def body(x_ref, b_ref, c_ref, dtc_ref, dtr_ref, al_ref, prev_ref, dy_ref,
             dx_ref, db_ref, dc_ref, ddtc_ref, ddtr_ref, dal_ref, dstate):
        c = pl.program_id(0)

        @pl.when(c == 0)
        def _():
            dstate[...] = jnp.zeros_like(dstate)

        _, vjp = jax.vjp(_ssd_chunk, x_ref[...], b_ref[...], c_ref[...], dtc_ref[...], dtr_ref[...], al_ref[...],
                         prev_ref[...])
        dx, db, dc, ddtc, ddtr, dal, dprev = vjp((dy_ref[...], dstate[...]))
        dx_ref[...] = dx
        db_ref[...] = db
        dc_ref[...] = dc
        ddtc_ref[...] = ddtc
        ddtr_ref[...] = ddtr
        dstate[...] = dprev

        @pl.when(c == 0)
        def _():
            dal_ref[...] = dal

        @pl.when(c > 0)
        def _():
            dal_ref[...] += dal

    hs = jax.ShapeDtypeStruct((SSD_HEADS, LP, SSD_HD), F32)
    gs = jax.ShapeDtypeStruct((SSD_GROUPS, LP, SSD_STATE), F32)
    return pl.pallas_call(
        body, name=name, grid=(N_CHUNK,),
        in_specs=[x_spec, g_spec, g_spec, dtc_spec, dtr_spec, al_spec, st_spec, x_spec],
        out_specs=[x_spec, g_spec, g_spec, dtc_spec, dtr_spec, al_spec],
        out_shape=[hs, gs, gs, jax.ShapeDtypeStruct((SSD_HEADS, LP, 1), F32),
                   jax.ShapeDtypeStruct((SSD_HEADS, 1, LP), F32), jax.ShapeDtypeStruct((SSD_HEADS, 1, BLOCK), F32)],
        scratch_shapes=[pltpu.VMEM((SSD_HEADS, SSD_HD, SSD_STATE), F32)],
        compiler_params=_cparams(("arbitrary",)),
    )(x, bm, cm, dtc, dtr, alog, prevs, dy)


def _tri_dot(tri, v):
    hi = v.astype(BF16)
    r1 = v - hi.astype(F32)
    mid = r1.astype(BF16)
    lo = (r1 - mid.astype(F32)).astype(BF16)
    t = tri.astype(BF16)
    d = lambda p: lax.dot_general(t, p, _dims(1, 0), preferred_element_type=F32)
    return d(hi) + d(mid) + d(lo)


def _fox_gate_fwd(raw, raw_blk, bias, *, name):
    def body(raw_ref, b_ref, c_ref, ct_ref, carry):
        j = pl.program_id(0)

        @pl.when(j == 0)
        def _():
            carry[...] = jnp.zeros_like(carry)

        rows = j * BLOCK + lax.broadcasted_iota(jnp.int32, (BLOCK, BLOCK), 0)
        lf = jnp.where(rows >= PAD_ROWS, -_softplus(-(raw_ref[...] + b_ref[...])), 0.0)
        li = lax.broadcasted_iota(jnp.int32, (BLOCK, BLOCK), 0)
        si = lax.broadcasted_iota(jnp.int32, (BLOCK, BLOCK), 1)
        cv = _tri_dot(jnp.where(li >= si, 1.0, 0.0), lf) + carry[...]
        c_ref[...] = cv
        ct_ref[...] = cv.T
        carry[...] += jnp.sum(lf, axis=0, keepdims=True)

    return pl.pallas_call(
        body, name=name, grid=(N_CHUNK,),
        in_specs=[_bs((BLOCK, BLOCK), lambda j: (j, raw_blk)), _bs((1, BLOCK), lambda j: (0, 0))],
        out_specs=[_bs((BLOCK, BLOCK), lambda j: (j, 0)), _bs((BLOCK, BLOCK), lambda j: (0, j))],
        out_shape=[jax.ShapeDtypeStruct((LP, BLOCK), F32), jax.ShapeDtypeStruct((BLOCK, LP), F32)],
        scratch_shapes=[pltpu.VMEM((1, BLOCK), F32)], compiler_params=_cparams(("arbitrary",)),
    )(raw, bias)


def _fox_gate_bwd(raw, raw_blk, bias, dc, dct, *, name):
    rj = lambda j: N_CHUNK - 1 - j

    def body(raw_ref, b_ref, dc_ref, dct_ref, draw_ref, db_ref, carry):
        j = pl.program_id(0)

        @pl.when(j == 0)
        def _():
            carry[...] = jnp.zeros_like(carry)

        rows = (N_CHUNK - 1 - j) * BLOCK + lax.broadcasted_iota(jnp.int32, (BLOCK, BLOCK), 0)
        li = lax.broadcasted_iota(jnp.int32, (BLOCK, BLOCK), 0)
        si = lax.broadcasted_iota(jnp.int32, (BLOCK, BLOCK), 1)
        dcv = dc_ref[...] + dct_ref[...].T
        dlf = _tri_dot(jnp.where(li <= si, 1.0, 0.0), dcv) + carry[...]
        carry[...] += jnp.sum(dcv, axis=0, keepdims=True)
        draw = jnp.where(rows >= PAD_ROWS, dlf * (1.0 - _sigmoid(raw_ref[...] + b_ref[...])), 0.0)
        draw_ref[...] = draw
        dsum = jnp.sum(draw, axis=0, keepdims=True)

        @pl.when(j == 0)
        def _():
            db_ref[...] = dsum

        @pl.when(j > 0)
        def _():
            db_ref[...] += dsum

    return pl.pallas_call(
        body, name=name, grid=(N_CHUNK,),
        in_specs=[_bs((BLOCK, BLOCK), lambda j: (rj(j), raw_blk)), _bs((1, BLOCK), lambda j: (0, 0)),
                  _bs((BLOCK, BLOCK), lambda j: (rj(j), 0)), _bs((BLOCK, BLOCK), lambda j: (0, rj(j)))],
        out_specs=[_bs((BLOCK, BLOCK), lambda j: (rj(j), 0)), _bs((1, BLOCK), lambda j: (0, 0))],
        out_shape=[jax.ShapeDtypeStruct((LP, BLOCK), F32), jax.ShapeDtypeStruct((1, BLOCK), F32)],
        scratch_shapes=[pltpu.VMEM((1, BLOCK), F32)], compiler_params=_cparams(("arbitrary",)),
    )(raw, bias, dc, dct)


ATT_W = 256
ATT_QB = 272
ATT_STEPS = LP // ATT_QB
ATT_KEYS = (640, 1152, 1664, LP)


def _lane_head(width, per, mod=None):
    lane = lax.broadcasted_iota(jnp.int32, (1, width), 1)
    if mod is not None:
        lane = lane % mod
    return lane // per


def _attn_mask(i, kw):
    r = i * ATT_QB + lax.broadcasted_iota(jnp.int32, (ATT_QB, kw), 0)
    c = lax.broadcasted_iota(jnp.int32, (ATT_QB, kw), 1)
    return (c <= r) & ((c >= PAD_ROWS) | (r < PAD_ROWS))


def _attn_by_key_class(i, fn):
    for p, kw in enumerate(ATT_KEYS):
        @pl.when(i // 2 == p)
        def _(kw=kw):
            fn(kw)


def _attn_specs(q, k, v, bias, rope):
    qspec = lambda blk, w=ATT_W: _bs((ATT_QB, w), lambda i: (i, blk))
    fspec = lambda blk, w=ATT_W: _bs((LP, w), lambda i: (0, blk))
    ins = [q[0], k[0], v[0]]
    specs = [qspec(q[1]), fspec(k[1]), fspec(v[1])]
    if bias is not None:
        ins += [bias[0], bias[1]]
        specs += [qspec(0, BLOCK), _bs((BLOCK, LP), lambda i: (0, 0))]
    if rope is not None:
        ins += [rope[0][0], rope[1][0]]
        specs += [qspec(rope[0][1], BLOCK), fspec(rope[1][1], BLOCK)]
    return ins, specs, qspec, fspec


def _attn_fwd(q, k, v, *, scale, name, bias=None, rope=None):
    ins, specs, qspec, fspec = _attn_specs(q, k, v, bias, rope)
    has_bias, has_rope = bias is not None, rope is not None

    def body(*refs):
        it = iter(refs)
        q_ref, k_ref, v_ref = next(it), next(it), next(it)
        if has_bias:
            c_ref, ct_ref = next(it), next(it)
        if has_rope:
            qr_ref, kr_ref = next(it), next(it)
        o_ref, lse_ref = next(it), next(it)
        i = pl.program_id(0)

        def block(kw):
            ok = _attn_mask(i, kw)
            qv, kv, vv = q_ref[...], k_ref[0:kw, :], v_ref[0:kw, :]
            hid, l128 = _lane_head(ATT_W, FOX_HD), _lane_head(BLOCK, 1)
            if has_rope:
                rid = _lane_head(BLOCK, ROPE_HALF, 64)
                qrv, krv = qr_ref[...], kr_ref[0:kw, :]
            def head(h, carry):
                o_acc, lse_acc = carry
                s = _raw_bdot(jnp.where(hid == h, qv, 0.0), kv, 1, 1)
                if has_rope:
                    s = s + _raw_bdot(jnp.where(rid == h, qrv, 0.0), krv, 1, 1)
                s = s * scale
                if has_bias:
                    cq = jnp.sum(jnp.where(l128 == h, c_ref[...], 0.0), axis=1, keepdims=True)
                    s = s + (cq - ct_ref[pl.ds(h, 1), 0:kw])
                s = jnp.where(ok, s, NEG)
                m = jnp.max(s, axis=1, keepdims=True)
                p = jnp.exp(s - m)
                l = jnp.sum(p, axis=1, keepdims=True)
                o_acc = jnp.where(hid == h, _raw_bdot(p, vv, 1, 0) / l, o_acc)
                lse_acc = jnp.where(l128 == h, m + jnp.log(l), lse_acc)
                return o_acc, lse_acc

            o_acc, lse_acc = lax.fori_loop(
                0, FOX_HEADS, head, (jnp.zeros((ATT_QB, ATT_W), F32), jnp.zeros((ATT_QB, BLOCK), F32)), unroll=True)
            o_ref[...] = o_acc
            lse_ref[...] = lse_acc

        _attn_by_key_class(i, block)

    return pl.pallas_call(
        body, name=name, grid=(ATT_STEPS,), in_specs=specs, out_specs=[qspec(0), qspec(0, BLOCK)],
        out_shape=[jax.ShapeDtypeStruct((LP, ATT_W), F32), jax.ShapeDtypeStruct((LP, BLOCK), F32)],
        compiler_params=_cparams(("parallel",)),
    )(*ins)


def _attn_bwd(q, k, v, o, lse, do, *, scale, name, bias=None, rope=None):
    ins, specs, qspec, fspec = _attn_specs(q, k, v, bias, rope)
    has_bias, has_rope = bias is not None, rope is not None
    ins += [o, lse, do[0]]
    specs += [qspec(0), qspec(0, BLOCK), qspec(do[1])]

    def body(*refs):
        it = iter(refs)
        q_ref, k_ref, v_ref = next(it), next(it), next(it)
        if has_bias:
            c_ref, ct_ref = next(it), next(it)
        if has_rope:
            qr_ref, kr_ref = next(it), next(it)
        o_ref, lse_ref, do_ref = next(it), next(it), next(it)
        dq_ref, dk_ref, dv_ref = next(it), next(it), next(it)
        if has_bias:
            dc_ref, dct_ref = next(it), next(it)
        if has_rope:
            dqr_ref, dkr_ref = next(it), next(it)
        i = pl.program_id(0)

        @pl.when(i == 0)
        def _():
            dk_ref[...] = jnp.zeros_like(dk_ref)
            dv_ref[...] = jnp.zeros_like(dv_ref)
            if has_rope:
                dkr_ref[...] = jnp.zeros_like(dkr_ref)
            if has_bias:
                dct_ref[...] = jnp.zeros_like(dct_ref)

        def block(kw):
            ok = _attn_mask(i, kw)
            qv, kv, vv = q_ref[...], k_ref[0:kw, :], v_ref[0:kw, :]
            ov, dov, lsev = o_ref[...], do_ref[...], lse_ref[...]
            hid, l128 = _lane_head(ATT_W, FOX_HD), _lane_head(BLOCK, 1)
            if has_rope:
                rid = _lane_head(BLOCK, ROPE_HALF, 64)
                qrv, krv = qr_ref[...], kr_ref[0:kw, :]

            def head(h, carry):
                dq_acc, aux_acc = carry
                qm = jnp.where(hid == h, qv, 0.0)
                s = _raw_bdot(qm, kv, 1, 1)
                if has_rope:
                    qrm = jnp.where(rid == h, qrv, 0.0)
                    s = s + _raw_bdot(qrm, krv, 1, 1)
                s = s * scale
                if has_bias:
                    cq = jnp.sum(jnp.where(l128 == h, c_ref[...], 0.0), axis=1, keepdims=True)
                    s = s + (cq - ct_ref[pl.ds(h, 1), 0:kw])
                s = jnp.where(ok, s, NEG)
                p = jnp.exp(s - jnp.sum(jnp.where(l128 == h, lsev, 0.0), axis=1, keepdims=True))
                dom = jnp.where(hid == h, dov, 0.0)
                dp = _raw_bdot(dom, vv, 1, 1)
                delta = jnp.sum(dom * ov, axis=1, keepdims=True)
                ds = p * (dp - delta)
                dq_acc = jnp.where(hid == h, _raw_bdot(ds, kv, 1, 0) * scale, dq_acc)
                dk_ref[0:kw, :] += _raw_bdot(ds, qm, 0, 0) * scale
                dv_ref[0:kw, :] += _raw_bdot(p, dom, 0, 0)
                if has_rope:
                    aux_acc = jnp.where(rid == h, _raw_bdot(ds, krv, 1, 0) * scale, aux_acc)
                    dkr_ref[0:kw, :] += _raw_bdot(ds, qrm, 0, 0) * scale
                if has_bias:
                    aux_acc = jnp.where(l128 == h, jnp.sum(ds, axis=1, keepdims=True), aux_acc)
                    dct_ref[pl.ds(h, 1), 0:kw] -= jnp.sum(ds, axis=0, keepdims=True)
                return dq_acc, aux_acc

            dq_acc, aux_acc = lax.fori_loop(
                0, FOX_HEADS, head, (jnp.zeros((ATT_QB, ATT_W), F32), jnp.zeros((ATT_QB, BLOCK), F32)))
            dq_ref[...] = dq_acc
            if has_bias:
                dc_ref[...] = aux_acc
            if has_rope:
                dqr_ref[...] = aux_acc

        _attn_by_key_class(i, block)

    wide = jax.ShapeDtypeStruct((LP, ATT_W), F32)
    narrow = jax.ShapeDtypeStruct((LP, BLOCK), F32)
    out_specs = [qspec(0), fspec(0), fspec(0)]
    out_shape = [wide, wide, wide]
    if has_bias:
        out_specs += [qspec(0, BLOCK), _bs((BLOCK, LP), lambda i: (0, 0))]
        out_shape += [narrow, jax.ShapeDtypeStruct((BLOCK, LP), F32)]
    if has_rope:
        out_specs += [qspec(0, BLOCK), fspec(0, BLOCK)]
        out_shape += [narrow, narrow]
    return pl.pallas_call(
        body, name=name, grid=(ATT_STEPS,), in_specs=specs, out_specs=out_specs, out_shape=out_shape,
        compiler_params=_cparams(("arbitrary",)),
    )(*ins)


def _loss_head(y, target, *, name):
    tile = 272

    def body(y_ref, t_ref, dy_ref, loss_ref):
        i = pl.program_id(0)
        rows = i * tile + lax.broadcasted_iota(jnp.int32, (tile, D_MODEL), 0)
        err = jnp.where(rows >= BLOCK, y_ref[...] - t_ref[...], 0.0)
        dy_ref[...] = err * (1.0 / D_MODEL)
        part = 0.5 * jnp.sum(jnp.sum(err * err, axis=1, keepdims=True) * (1.0 / D_MODEL), axis=0, keepdims=True)
        part = jnp.broadcast_to(part, (1, BLOCK))

        @pl.when(i == 0)
        def _():
            loss_ref[...] = part

        @pl.when(i > 0)
        def _():
            loss_ref[...] += part

    return pl.pallas_call(
        body, name=name, grid=(LP // tile,),
        in_specs=[_bs((tile, D_MODEL), lambda i: (i, 0)), _bs((tile, D_MODEL), lambda i: (i, 0))],
        out_specs=[_bs((tile, D_MODEL), lambda i: (i, 0)), _bs((1, BLOCK), lambda i: (0, 0))],
        out_shape=[jax.ShapeDtypeStruct((LP, D_MODEL), F32), jax.ShapeDtypeStruct((1, BLOCK), F32)],
        compiler_params=_cparams(("arbitrary",)),
    )(y, target)


def _adamw(w, gs, m, v, *, name):
    if w.ndim == 2:
        w, m, v = w[None], m[None], v[None]
        squeeze = True
    else:
        squeeze = False
    NL, R, C = w.shape
    assert len(gs) == NL
    CG = gs[0].shape[1]
    tile = _tile(R, 256, 8)

    def body(*refs):
        w_ref, g_refs = refs[0], refs[1:1 + NL]
        m_ref, v_ref, go_ref, d_ref, nm_ref, nv_ref = refs[1 + NL:]
        gv = g_refs[0][:, :C]
        for j in range(1, NL):
            gv = jnp.where(pl.program_id(0) == j, g_refs[j][:, :C], gv)
        nm = ADAM_B1 * m_ref[...] + (1.0 - ADAM_B1) * gv
        nv = ADAM_B2 * v_ref[...] + (1.0 - ADAM_B2) * (gv * gv)
        m_hat = nm / (1.0 - ADAM_B1 ** ADAM_STEP)
        v_hat = nv / (1.0 - ADAM_B2 ** ADAM_STEP)
        go_ref[...] = gv
        d_ref[...] = -ADAM_LR * (m_hat / (jnp.sqrt(v_hat) + ADAM_EPS) + ADAM_WD * w_ref[...])
        nm_ref[...] = nm
        nv_ref[...] = nv

    spec = _bs((None, tile, C), lambda l, i: (l, i, 0))
    gspecs = [_bs((tile, CG), lambda l, i, j=j: (jnp.where(l == j, i, 0), 0)) for j in range(NL)]
    res = pl.pallas_call(
        body, name=name, grid=(NL, R // tile), in_specs=[spec, *gspecs, spec, spec], out_specs=[spec] * 4,
        out_shape=[jax.ShapeDtypeStruct((NL, R, C), F32)] * 4, compiler_params=_cparams(("parallel", "parallel")),
    )(w, *gs, m, v)
    return [r[0] for r in res] if squeeze else res


def _my_pos():
    return lax.axis_index("x"), lax.axis_index("y"), lax.axis_index("c")


def _other_chips(x, y):
    return [(1 - x, y), (x, 1 - y), (1 - x, 1 - y)]


def _allgather_chips(shards):
    n = len(shards)
    per = 7

    def body(*refs):
        ins, outs = refs[:n], refs[n:2 * n]
        send_sems, recv_sems = refs[2 * n], refs[2 * n + 1]
        x, y, c = _my_pos()
        chips = _other_chips(x, y)
        sibling, me = (x, y, 1 - c), 2 * x + y

        def cp(a, kk, src, dst, to):
            return pltpu.make_async_remote_copy(src_ref=src, dst_ref=dst, send_sem=send_sems.at[per * a + kk],
                                                recv_sem=recv_sems.at[per * a + kk], device_id=to, device_id_type=MESH)

        sends = []
        for a in range(n):
            for j, chip in enumerate(chips):
                sends.append(cp(a, j, ins[a].at[c], outs[a].at[me, c], (*chip, c)))
            sends.append(cp(a, 3, ins[a], outs[a].at[me], sibling))
        for s in sends:
            s.start()
        for a in range(n):
            for j, chip in enumerate(chips):
                slab = outs[a].at[2 * chip[0] + chip[1], c]
                cp(a, j, slab, slab, (x, y, c)).wait_recv()
                fwd = cp(a, 4 + j, slab, slab, sibling)
                fwd.start()
                sends.append(fwd)
        for a in range(n):
            cp(a, 3, ins[a], outs[a].at[me], (x, y, c)).wait_recv()
            for j, chip in enumerate(chips):
                slab = outs[a].at[2 * chip[0] + chip[1], 1 - c]
                cp(a, 4 + j, slab, slab, (x, y, c)).wait_recv()
        for s in sends:
            s.wait_send()

    return pl.pallas_call(
        body, name="allgather_chips", in_specs=[ANY] * n, out_specs=[ANY] * n,
        out_shape=[jax.ShapeDtypeStruct((N_CHIPS,) + s.shape, s.dtype) for s in shards],
        scratch_shapes=[pltpu.SemaphoreType.DMA((per * n,)), pltpu.SemaphoreType.DMA((per * n,))],
    )(*shards)


def _rs_swap_rows(gs, tag):
    n = len(gs)

    def body(*refs):
        ins, outs = refs[:n], refs[n:2 * n]
        send_sems, recv_sems = refs[2 * n], refs[2 * n + 1]
        x, y, c = _my_pos()
        cps = []
        for a in range(n):
            half = ins[a].shape[1] // 2
            cps.append(pltpu.make_async_remote_copy(
                src_ref=ins[a].at[:, pl.ds((1 - c) * half, half)], dst_ref=outs[a], send_sem=send_sems.at[a],
                recv_sem=recv_sems.at[a], device_id=(x, y, 1 - c), device_id_type=MESH))
        for cp in cps:
            cp.start()
        for cp in cps:
            cp.wait()

    return pl.pallas_call(
        body, name=f"rs_swap_rows_{tag}", in_specs=[ANY] * n, out_specs=[ANY] * n,
        out_shape=[jax.ShapeDtypeStruct((N_CHIPS, g.shape[1] // 2, g.shape[2]), g.dtype) for g in gs],
        scratch_shapes=[pltpu.SemaphoreType.DMA((n,)), pltpu.SemaphoreType.DMA((n,))],
    )(*gs)


def _rs_add_pair(g, r, pos, *, name):
    _, H, C = r.shape
    tile = _tile(H, 512, 16)
    nt = H // tile

    def body(pos_ref, g_ref, r_ref, o32_ref, o16_ref):
        s = g_ref[...] + r_ref[...]
        o32_ref[...] = s
        o16_ref[...] = s.astype(BF16)

    spec = _bs((None, tile, C), lambda k, i, pos_ref: (k, i, 0))
    grid_spec = pltpu.PrefetchScalarGridSpec(
        num_scalar_prefetch=1, grid=(N_CHIPS, nt),
        in_specs=[_bs((None, tile, C), lambda k, i, pos_ref: (k, pos_ref[1] * nt + i, 0)), spec],
        out_specs=[spec, spec])
    return pl.pallas_call(
        body, name=name, grid_spec=grid_spec,
        out_shape=[jax.ShapeDtypeStruct((N_CHIPS, H, C), F32), jax.ShapeDtypeStruct((N_CHIPS, H, C), BF16)],
        compiler_params=_cparams(("parallel", "parallel")),
    )(pos, g, r)


def _exchange_copies(srcs, lands, send_sems, recv_sems):
    x, y, c = _my_pos()
    starts, landing = [], []
    for a in range(len(srcs)):
        for j, chip in enumerate(_other_chips(x, y)):
            sems = dict(send_sem=send_sems.at[3 * a + j], recv_sem=recv_sems.at[3 * a + j], device_id_type=MESH)
            starts.append(pltpu.make_async_remote_copy(
                src_ref=srcs[a].at[2 * chip[0] + chip[1]], dst_ref=lands[a].at[j], device_id=(*chip, c), **sems))
            landing.append(pltpu.make_async_remote_copy(
                src_ref=lands[a].at[j], dst_ref=lands[a].at[j], device_id=(x, y, c), **sems))
    return starts, landing


def _gather_copies(srcs, lands, l, send_sems, recv_sems):
    x, y, c = _my_pos()
    me = 2 * x + y
    starts, landing = [], []
    for a in range(len(srcs)):
        half = srcs[a].shape[1] // 2
        mine = pl.ds(c * half, half)
        for j, chip in enumerate(_other_chips(x, y)):
            sems = dict(send_sem=send_sems.at[3 * a + j], recv_sem=recv_sems.at[3 * a + j], device_id_type=MESH)
            starts.append(pltpu.make_async_remote_copy(
                src_ref=srcs[a].at[l, mine], dst_ref=lands[a].at[me, mine], device_id=(*chip, c), **sems))
            slab = lands[a].at[2 * chip[0] + chip[1], mine]
            landing.append(pltpu.make_async_remote_copy(src_ref=slab, dst_ref=slab, device_id=(x, y, c), **sems))
    return starts, landing


HBM = pl.BlockSpec(memory_space=pltpu.HBM)
SEM = pl.BlockSpec(memory_space=pltpu.SEMAPHORE)


def _ici_blocking(copies_fn, srcs, land_shapes, *, name, after=()):
    n, na = len(srcs), len(after)

    def body(*refs):
        starts, landing = copies_fn(refs[:n], refs[n + na:2 * n + na], refs[2 * n + na], refs[2 * n + na + 1])
        for cp in starts:
            cp.start()
        for cp in landing:
            cp.wait_recv()
        for cp in starts:
            cp.wait_send()

    return pl.pallas_call(
        body, name=name, in_specs=[ANY] * (n + na), out_specs=[ANY] * n, out_shape=land_shapes,
        scratch_shapes=[pltpu.SemaphoreType.DMA((3 * n,)), pltpu.SemaphoreType.DMA((3 * n,))],
    )(*srcs, *after)


def _ici_start(copies_fn, srcs, land_shapes, *, name, after=()):
    n, na = len(srcs), len(after)

    def body(*refs):
        starts, _ = copies_fn(refs[:n], refs[n:2 * n], refs[2 * n + na], refs[2 * n + na + 1])
        for cp in starts:
            cp.start()
        refs[-1][...] = jnp.zeros_like(refs[-1])

    sems = pltpu.SemaphoreType.DMA((3 * n,))
    hbm = lambda s: pltpu.HBM(s.shape, s.dtype)
    lands = [pltpu.with_memory_space_constraint(lax.empty(s.shape, s.dtype), pltpu.HBM) for s in land_shapes]
    res = pl.pallas_call(
        body, name=name, in_specs=[HBM] * (2 * n) + [ANY] * na,
        out_specs=(SEM, SEM, *[HBM] * (2 * n), pl.BlockSpec(memory_space=pltpu.VMEM)),
        out_shape=(sems, sems, *[hbm(s) for s in srcs], *[hbm(s) for s in land_shapes],
                   jax.ShapeDtypeStruct((8, BLOCK), F32)),
        input_output_aliases={i: 2 + i for i in range(2 * n)},
        compiler_params=pltpu.CompilerParams(has_side_effects=pltpu.SideEffectType.DATAFLOW_SIDE_EFFECTING),
    )(*[pltpu.with_memory_space_constraint(s, pltpu.HBM) for s in srcs], *lands, *after)
    return res[0], res[1], list(res[2:2 + n]), list(res[2 + n:2 + 2 * n]), res[-1]


def _ici_wait(copies_fn, send_sems, recv_sems, srcs, lands, after, *, name):
    n = len(srcs)

    def body(*refs):
        starts, landing = copies_fn(refs[:n], refs[n:2 * n], refs[2 * n], refs[2 * n + 1])
        for cp in starts:
            cp.wait_send()
        for cp in landing:
            cp.wait_recv()

    hbm = lambda s: pltpu.HBM(s.shape, s.dtype)
    res = pl.pallas_call(
        body, name=name, in_specs=[*[HBM] * (2 * n), SEM, SEM, ANY], out_specs=[HBM] * (2 * n),
        out_shape=[*[hbm(s) for s in srcs], *[hbm(s) for s in lands]],
        input_output_aliases={i: i for i in range(2 * n)},
        compiler_params=pltpu.CompilerParams(has_side_effects=pltpu.SideEffectType.DATAFLOW_SIDE_EFFECTING),
    )(*srcs, *lands, send_sems, recv_sems, after)
    return list(res[:n]), list(res[n:])


def _gather_d2d(shards, lands, l, tag):
    n = len(shards)

    def body(*refs):
        ins, outs = refs[:n], refs[2 * n:3 * n]
        send_sems, recv_sems = refs[3 * n], refs[3 * n + 1]
        x, y, c = _my_pos()
        me, sibling = 2 * x + y, (x, y, 1 - c)
        starts, landing = [], []
        for a in range(n):
            half = ins[a].shape[1] // 2
            mine, theirs = pl.ds(c * half, half), pl.ds((1 - c) * half, half)
            pairs = [(ins[a].at[l], outs[a].at[me], outs[a].at[me])]
            for chip in _other_chips(x, y):
                k = 2 * chip[0] + chip[1]
                pairs.append((outs[a].at[k, mine], outs[a].at[k, mine], outs[a].at[k, theirs]))
            for j, (src, dst, lands_here) in enumerate(pairs):
                sems = dict(send_sem=send_sems.at[4 * a + j], recv_sem=recv_sems.at[4 * a + j], device_id_type=MESH)
                starts.append(pltpu.make_async_remote_copy(src_ref=src, dst_ref=dst, device_id=sibling, **sems))
                landing.append(pltpu.make_async_remote_copy(src_ref=lands_here, dst_ref=lands_here, device_id=(x, y, c),
                                                            **sems))
        for cp in starts:
            cp.start()
        for cp in landing:
            cp.wait_recv()
        for cp in starts:
            cp.wait_send()

    return pl.pallas_call(
        body, name=f"gather_d2d_{tag}", in_specs=[ANY] * (2 * n), out_specs=[ANY] * n,
        out_shape=[jax.ShapeDtypeStruct(s.shape, s.dtype) for s in lands],
        input_output_aliases={n + a: a for a in range(n)},
        scratch_shapes=[pltpu.SemaphoreType.DMA((4 * n,)), pltpu.SemaphoreType.DMA((4 * n,))],
    )(*shards, *lands)


def _rs_add_chips(p32, r16, pos, *, name):
    _, H, C = p32.shape
    tile = _tile(H, 512, 16)
    nt = H // tile

    def body(pos_ref, p_ref, r_ref, o_ref):
        o_ref[...] = ((p_ref[...] + r_ref[0].astype(F32)) + r_ref[1].astype(F32)) + r_ref[2].astype(F32)

    grid_spec = pltpu.PrefetchScalarGridSpec(
        num_scalar_prefetch=1, grid=(nt,),
        in_specs=[_bs((None, tile, C), lambda i, pos_ref: (pos_ref[0], i, 0)),
                  _bs((3, tile, C), lambda i, pos_ref: (0, i, 0))],
        out_specs=_bs((tile, C), lambda i, pos_ref: (pos_ref[1] * nt + i, 0)))
    return pl.pallas_call(
        body, name=name, grid_spec=grid_spec, out_shape=jax.ShapeDtypeStruct((2 * H, C), F32),
        compiler_params=_cparams(("parallel",)),
    )(pos, p32, r16)


def _rs_join_rows(fs, tag):
    n = len(fs)

    def body(*refs):
        outs = refs[n:2 * n]
        send_sems, recv_sems = refs[2 * n], refs[2 * n + 1]
        x, y, c = _my_pos()
        for a in range(n):
            half = outs[a].shape[0] // 2
            mine = outs[a].at[pl.ds(c * half, half)]
            pltpu.make_async_remote_copy(src_ref=mine, dst_ref=mine, send_sem=send_sems.at[a],
                                         recv_sem=recv_sems.at[a], device_id=(x, y, 1 - c), device_id_type=MESH).start()
        for a in range(n):
            half = outs[a].shape[0] // 2
            pltpu.make_async_remote_copy(
                src_ref=outs[a].at[pl.ds(c * half, half)], dst_ref=outs[a].at[pl.ds((1 - c) * half, half)],
                send_sem=send_sems.at[a], recv_sem=recv_sems.at[a], device_id=(x, y, 1 - c), device_id_type=MESH).wait()

    return pl.pallas_call(
        body, name=f"rs_join_rows_{tag}", in_specs=[ANY] * n, out_specs=[ANY] * n,
        out_shape=[jax.ShapeDtypeStruct(f.shape, f.dtype) for f in fs],
        input_output_aliases={a: a for a in range(n)},
        scratch_shapes=[pltpu.SemaphoreType.DMA((n,)), pltpu.SemaphoreType.DMA((n,))],
    )(*fs)


def _pos_vector():
    x, y, c = _my_pos()
    return jnp.stack([2 * x + y, c]).astype(jnp.int32)


def _rs_pair_sums(gs, names, tag):
    pos = _pos_vector()
    r1 = _rs_swap_rows(gs, tag)
    return [_rs_add_pair(g, r, pos, name=f"rs_add_pair_{tag}_{nm}") for g, r, nm in zip(gs, r1, names)]


def _rs_finish(pairs, r2, names, tag):
    pos = _pos_vector()
    fs = [_rs_add_chips(p[0], r, pos, name=f"rs_add_chips_{tag}_{nm}") for p, r, nm in zip(pairs, r2, names)]
    return _rs_join_rows(fs, tag)


def _exchange_land_shapes(pairs):
    return [jax.ShapeDtypeStruct((3,) + p[1].shape[1:], p[1].dtype) for p in pairs]


def _allreduce_small(buf):
    R, W = buf.shape

    def body(b_ref, o_ref, gather, send_sems, recv_sems):
        x, y, c = _my_pos()
        me = 4 * x + 2 * y + c
        gather[me] = b_ref[...]
        cps = []
        for d in range(1, 8):
            peer = (x ^ (d >> 2), y ^ ((d >> 1) & 1), c ^ (d & 1))
            cps.append(pltpu.make_async_remote_copy(
                src_ref=b_ref, dst_ref=gather.at[me], send_sem=send_sems.at[d - 1], recv_sem=recv_sems.at[d - 1],
                device_id=peer, device_id_type=MESH))
        for cp in cps:
            cp.start()
        for d in range(1, 8):
            pltpu.make_async_remote_copy(
                src_ref=b_ref, dst_ref=gather.at[me ^ d], send_sem=send_sems.at[d - 1], recv_sem=recv_sems.at[d - 1],
                device_id=(x, y, c), device_id_type=MESH).wait_recv()
        for cp in cps:
            cp.wait_send()
        acc = gather[0]
        for d in range(1, 8):
            acc = acc + gather[d]
        o_ref[...] = acc

    vm = pl.BlockSpec(memory_space=pltpu.VMEM)
    return pl.pallas_call(
        body, name="allreduce_small", in_specs=[vm], out_specs=vm, out_shape=jax.ShapeDtypeStruct((R, W), F32),
        scratch_shapes=[pltpu.VMEM((8, R, W), F32), pltpu.SemaphoreType.DMA((7,)), pltpu.SemaphoreType.DMA((7,))],
    )(buf)


def _heads(a, h, d):
    return a.reshape(a.shape[0], h, d).transpose(1, 0, 2)


def _unheads(a):
    h, L, d = a.shape
    return a.transpose(1, 0, 2).reshape(L, h * d)


def _rope_tables():
    pos = jnp.maximum(jnp.arange(LP, dtype=F32) - PAD_ROWS, 0.0)
    inv_freq = 1.0 / (ROPE_THETA ** (jnp.arange(0, MLA_ROPE, 2, dtype=F32) / MLA_ROPE))
    ang = pos[:, None] * inv_freq[None, :]
    cos, sin = jnp.tile(jnp.cos(ang), (1, MLA_HEADS)), jnp.tile(jnp.sin(ang), (1, MLA_HEADS))
    return jnp.concatenate([cos, cos], axis=1), jnp.concatenate([-sin, sin], axis=1)


def _lane_pad(a, width=BLOCK):
    return jnp.pad(a, ((0, 0), (0, width - a.shape[1])))


def _pad_in_proj(w):
    sl = lambda start, size: w[:, start:start + size]
    return jnp.concatenate([
        sl(OC_Z, 512), sl(OC_XBC, 768), sl(OC_FQ, 256), sl(OC_FK, 256), sl(OC_FV, 256), sl(OC_CQ, 256), sl(OC_CKV, 128),
        _lane_pad(sl(OC_DT, SSD_HEADS)), _lane_pad(sl(OC_FR, FOX_HEADS)),
        jnp.tile(sl(OC_KR, ROPE_HALF), (1, MLA_HEADS)), jnp.tile(sl(OC_KR + ROPE_HALF, ROPE_HALF), (1, MLA_HEADS))], axis=1)


def _unpad_in_proj(wp):
    sl = lambda start, size: wp[:, start:start + size]
    rope = lambda start: sl(start, 64).reshape(wp.shape[0], MLA_HEADS, ROPE_HALF).sum(axis=1)
    return jnp.concatenate([
        sl(PC_Z, 512), sl(PC_XBC, 768), sl(PC_DT, SSD_HEADS), sl(PC_FQ, 256), sl(PC_FK, 256), sl(PC_FV, 256),
        sl(PC_FR, FOX_HEADS), sl(PC_CQ, 256), sl(PC_CKV, 128), rope(PC_KR), rope(PC_KR + 64)], axis=1)


def _regroup_uq(w):
    w3 = w.reshape(w.shape[0], MLA_HEADS, MLA_NOPE + MLA_ROPE)
    return jnp.concatenate([w3[:, :, :MLA_NOPE].reshape(w.shape[0], -1),
                            w3[:, :, MLA_NOPE:MLA_NOPE + ROPE_HALF].reshape(w.shape[0], -1),
                            w3[:, :, MLA_NOPE + ROPE_HALF:].reshape(w.shape[0], -1)], axis=1)


def _ungroup_uq(wp):
    n = wp.shape[0]
    return jnp.concatenate([wp[:, :256].reshape(n, MLA_HEADS, MLA_NOPE), wp[:, 256:320].reshape(n, MLA_HEADS, ROPE_HALF),
                            wp[:, 320:].reshape(n, MLA_HEADS, ROPE_HALF)], axis=2).reshape(n, -1)


def _regroup_ukv(w):
    w3 = w.reshape(w.shape[0], MLA_HEADS, MLA_NOPE + MLA_V)
    return jnp.concatenate([w3[:, :, :MLA_NOPE].reshape(w.shape[0], -1), w3[:, :, MLA_NOPE:].reshape(w.shape[0], -1)],
                           axis=1)


def _ungroup_ukv(wp):
    n = wp.shape[0]
    return jnp.concatenate([wp[:, :256].reshape(n, MLA_HEADS, MLA_NOPE), wp[:, 256:].reshape(n, MLA_HEADS, MLA_V)],
                           axis=2).reshape(n, -1)


TMF = 1088
N_IF = LP // TMF


def _chunk_rows_mm(a, w, l, chunk_h, *, name, add=None):
    N = w.shape[2]
    return _mm_core(a, w, a_spec=_bs((TMF, chunk_h), lambda i, j, k: (i, k)),
                    b_spec=_bs((None, chunk_h, N), lambda i, j, k: (k, 0, 0)),
                    o_spec=_bs((TMF, N), lambda i, j, k: (i, 0)), grid=(N_IF, 1, N_CHIPS),
                    out_shape=(LP, N), ca=1, cb=0, name=name, add=add)


def _chunk_rows_dx(g, w, l, chunk_h, *, name):
    N = w.shape[2]
    return _mm_core(g, w, a_spec=_bs((TMF, N), lambda i, j, k: (i, 0)),
                    b_spec=_bs((None, chunk_h, N), lambda i, j, k: (j, 0, 0)),
                    o_spec=_bs((TMF, chunk_h), lambda i, j, k: (i, j)), grid=(N_IF, N_CHIPS, 1),
                    out_shape=(LP, N_CHIPS * chunk_h), ca=1, cb=1, name=name)


def _chunk_rows_dw(a, g, chunk_h, *, name):
    N = g.shape[1]
    return _mm_core(a, g, a_spec=_bs((LP, chunk_h), lambda i, j, k: (0, i)), b_spec=_bs((LP, N), lambda i, j, k: (0, 0)),
                    o_spec=_bs((None, chunk_h, N), lambda i, j, k: (i, 0, 0)), grid=(N_CHIPS, 1, 1),
                    out_shape=(N_CHIPS, chunk_h, N), ca=0, cb=0, name=name)


def _ffn_fwd(h, W, pre, l, gam, bet, tag):
    g = _chunk_rows_dx(h, W[pre + "_w_gate"][l], l, HP, name=f"{tag}_gate")
    u = _chunk_rows_dx(h, W[pre + "_w_up"][l], l, HP, name=f"{tag}_up")
    (act,) = _rowwise(_swiglu_fn, [g, u], [], [FP], name=f"{tag}_swiglu", tile=TM, ncol=N_CHIPS, out_dtypes=[BF16])
    o = _chunk_rows_mm(act, W[pre + "_w_down"][l], l, HP, name=f"{tag}_down")
    (out,) = _rowwise(_make_res_ln_fn(0.5), [h, o], [gam, bet], [D_MODEL], name=f"{tag}_ln", tile=272)
    return out, (h, g, u, act, o)


def _ffn_bwd(dout, saved, W, pre, l, gam, bet, GB, tag):
    h, g, u, act, o = saved
    (dh_a, do), (dgam, dbet) = _rowwise_bwd(_make_res_ln_fn(0.5), [h, o], [gam, bet], [dout], name=f"{tag}_ln_bwd",
                                            tile=272, grad_dtypes=[F32, BF16])
    dact = _chunk_rows_dx(do, W[pre + "_w_down"][l], l, HP, name=f"{tag}_down_dx")
    GB[pre + "_w_down"] = _chunk_rows_dw(act, do, HP, name=f"{tag}_down_dw")
    (dg, du), _ = _rowwise_bwd(_swiglu_fn, [g, u], [], [dact], name=f"{tag}_swiglu_bwd", tile=TM, ncol=N_CHIPS,
                               grad_dtypes=[BF16, BF16])
    GB[pre + "_w_gate"] = _chunk_rows_dw(dg, h, HP, name=f"{tag}_gate_dw")
    GB[pre + "_w_up"] = _chunk_rows_dw(du, h, HP, name=f"{tag}_up_dw")
    dh = _chunk_rows_mm(dg, W[pre + "_w_gate"][l], l, HP, add=dh_a, name=f"{tag}_gate_dx")
    dh = _chunk_rows_mm(du, W[pre + "_w_up"][l], l, HP, add=dh, name=f"{tag}_up_dx")
    return dh, dgam, dbet


def _mixer_fwd(h1, W, l, cosf, sins):
    tag = f"l{l}"
    proj = _mm(h1, W["w_in_p"][l], name=f"{tag}_in_proj")
    sv = {"h1": h1, "proj": proj}
    conv_w, conv_b = W["conv_w"][l], W["conv_b"][l][None]
    xc = _conv_fwd(proj, PC_XBC // BLOCK, conv_w, conv_b, name=f"{tag}_conv")
    dt_bias = _lane_pad(W["dt_bias"][l][None])
    (dt,) = _rowwise(_ssd_pre_fn, [(proj, BLOCK, PC_DT // BLOCK)], [dt_bias], [BLOCK], name=f"{tag}_ssd_dt", tile=272)
    xh = _heads(xc[:, :SSD_D], SSD_HEADS, SSD_HD)
    bm = _heads(xc[:, SSD_D:SSD_D + 128], SSD_GROUPS, SSD_STATE)
    cm = _heads(xc[:, SSD_D + 128:], SSD_GROUPS, SSD_STATE)
    dt8 = dt[:, :SSD_HEADS].T
    dtc, dtr = dt8[:, :, None], dt8[:, None, :]
    alog = jnp.broadcast_to(W["a_log"][l][:, None, None], (SSD_HEADS, 1, BLOCK))
    yh, prevs = _ssd_fwd(xh, bm, cm, dtc, dtr, alog, name=f"{tag}_ssd")
    y_raw = _unheads(yh)
    dskip = jnp.repeat(W["d_skip"][l], SSD_HD)[None]
    normg = W["ssd_norm_g"][l][None]
    post_rows = [y_raw, (xc, 256, 0), (proj, 256, PC_Z // 256)]
    (y_ssd,) = _rowwise(_ssd_post_fn, post_rows, [dskip, normg], [SSD_D], name=f"{tag}_ssd_post", tile=272,
                        ncol=SSD_GROUPS)
    sv.update(conv_w=conv_w, conv_b=conv_b, dt_bias=dt_bias, xh=xh, bm=bm, cm=cm, dtc=dtc, dtr=dtr, alog=alog,
              prevs=prevs, post_rows=post_rows, dskip=dskip, normg=normg)
    f_b = _lane_pad(W["fox_f_b"][l][None])
    cg, cgt = _fox_gate_fwd(proj, PC_FR // BLOCK, f_b, name=f"{tag}_fox_gate")
    fox_qkv = ((proj, PC_FQ // ATT_W), (proj, PC_FK // ATT_W), (proj, PC_FV // ATT_W))
    y_fox, lse_f = _attn_fwd(*fox_qkv, scale=FOX_HD ** -0.5, name=f"{tag}_fox_attn", bias=(cg, cgt))
    sv.update(f_b=f_b, cg=cg, cgt=cgt, fox_qkv=fox_qkv, y_fox=y_fox, lse_f=lse_f)
    gq, gkv = W["mla_q_norm_g"][l][None], W["mla_kv_norm_g"][l][None]
    norm_rows = [(proj, 256, PC_CQ // 256), (proj, BLOCK, PC_CKV // BLOCK)]
    qn, cn = _rowwise(_mla_norm_fn, norm_rows, [gq, gkv], [MLA_Q_LORA, MLA_KV_LORA], name=f"{tag}_mla_norm", tile=272,
                      out_dtypes=[BF16, BF16])
    qh = _mm(qn, W["mla_w_uq_p"][l], name=f"{tag}_mla_uq")
    kvh = _mm(cn, W["mla_w_ukv_p"][l], name=f"{tag}_mla_ukv")
    qr, kr = _rowwise(_rope_fn, [(qh, BLOCK, 2), (proj, BLOCK, PC_KR // BLOCK), cosf, sins], [], [BLOCK, BLOCK],
                      name=f"{tag}_rope", tile=272)
    mla_qkv = ((qh, 0), (kvh, 0), (kvh, 1))
    y_mla, lse_m = _attn_fwd(*mla_qkv, scale=(MLA_NOPE + MLA_ROPE) ** -0.5, name=f"{tag}_mla_attn",
                             rope=((qr, 0), (kr, 0)))
    sv.update(gq=gq, gkv=gkv, norm_rows=norm_rows, qn=qn, cn=cn, qr=qr, kr=kr, mla_qkv=mla_qkv, y_mla=y_mla, lse_m=lse_m)
    ycat = jnp.concatenate([y_ssd, y_fox, y_mla], axis=1).astype(BF16)
    mix = _chunk_rows_mm(ycat, W["w_out"][l], l, 256, name=f"{tag}_out_proj")
    (h2,) = _rowwise(_make_res_ln_fn(1.0), [h1, mix], [W["ln2_g"][l][None], W["ln2_b"][l][None]], [D_MODEL],
                     name=f"{tag}_ln2", tile=272)
    sv.update(mix=mix, ycat=ycat)
    return h2, sv


def _mixer_bwd(dh2, sv, W, l, cosf, sins, GB):
    tag = f"l{l}"
    G = {}
    proj = sv["proj"]
    ln2g, ln2b = W["ln2_g"][l][None], W["ln2_b"][l][None]
    (dh1_a, dmix), (dln2g, dln2b) = _rowwise_bwd(
        _make_res_ln_fn(1.0), [sv["h1"], sv["mix"]], [ln2g, ln2b], [dh2], name=f"{tag}_ln2_bwd", tile=272,
        grad_dtypes=[F32, BF16])
    G["ln2_g"], G["ln2_b"] = dln2g[0], dln2b[0]
    dycat = _chunk_rows_dx(dmix, W["w_out"][l], l, 256, name=f"{tag}_out_proj_dx")
    GB["w_out"] = _chunk_rows_dw(sv["ycat"], dmix, 256, name=f"{tag}_out_proj_dw")
    (dy_raw, dxs_a, dz), (ddskip, dnormg) = _rowwise_bwd(
        _ssd_post_fn, sv["post_rows"], [sv["dskip"], sv["normg"]], [dycat[:, :SSD_D]],
        name=f"{tag}_ssd_post_bwd", tile=272, ncol=SSD_GROUPS)
    G["ssd_norm_g"] = dnormg[0]
    G["d_skip"] = ddskip.reshape(SSD_HEADS, SSD_HD).sum(axis=1)
    dxh, dbm, dcm, ddtc, ddtr, dal = _ssd_bwd(sv["xh"], sv["bm"], sv["cm"], sv["dtc"], sv["dtr"], sv["alog"],
                                              sv["prevs"], _heads(dy_raw, SSD_HEADS, SSD_HD), name=f"{tag}_ssd_bwd")
    G["a_log"] = dal[:, 0, 0]
    dxc = jnp.concatenate([dxs_a + _unheads(dxh), _unheads(dbm), _unheads(dcm)], axis=1)
    dxbc, G["conv_w"], dconv_b = _conv_bwd(proj, PC_XBC // BLOCK, sv["conv_w"], sv["conv_b"], dxc,
                                           name=f"{tag}_conv_bwd")
    G["conv_b"] = dconv_b[0]
    ddt = _lane_pad((ddtc[:, :, 0] + ddtr[:, 0, :]).T)
    (ddt_raw,), (ddt_bias,) = _rowwise_bwd(_ssd_pre_fn, [(proj, BLOCK, PC_DT // BLOCK)], [sv["dt_bias"]], [ddt],
                                           name=f"{tag}_ssd_dt_bwd", tile=272)
    G["dt_bias"] = ddt_bias[0, :SSD_HEADS]
    dfq, dfk, dfv, dcg, dcgt = _attn_bwd(*sv["fox_qkv"], sv["y_fox"], sv["lse_f"], (dycat, SSD_D // ATT_W),
                                         scale=FOX_HD ** -0.5, name=f"{tag}_fox_attn_bwd", bias=(sv["cg"], sv["cgt"]))
    df_raw, dfb = _fox_gate_bwd(proj, PC_FR // BLOCK, sv["f_b"], dcg, dcgt, name=f"{tag}_fox_gate_bwd")
    G["fox_f_b"] = dfb[0, :FOX_HEADS]
    dqn_h, dkn_h, dv_h, dqr, dkr = _attn_bwd(
        *sv["mla_qkv"], sv["y_mla"], sv["lse_m"], (dycat, (SSD_D + FOX_D) // ATT_W),
        scale=(MLA_NOPE + MLA_ROPE) ** -0.5, name=f"{tag}_mla_attn_bwd", rope=((sv["qr"], 0), (sv["kr"], 0)))
    dq_rope, dk_rope = _rowwise(_rope_t_fn, [dqr, dkr, cosf, sins], [], [BLOCK, BLOCK], name=f"{tag}_rope_bwd",
                                tile=272)
    dqh = jnp.concatenate([dqn_h, dq_rope], axis=1).astype(BF16)
    dkvh = jnp.concatenate([dkn_h, dv_h], axis=1).astype(BF16)
    dqn = _mm(dqh, W["mla_w_uq_p"][l], tb=True, name=f"{tag}_mla_uq_dx")
    G["mla_w_uq_p"] = _mm(sv["qn"], dqh, ta=True, name=f"{tag}_mla_uq_dw")
    dcn = _mm(dkvh, W["mla_w_ukv_p"][l], tb=True, name=f"{tag}_mla_ukv_dx")
    G["mla_w_ukv_p"] = _mm(sv["cn"], dkvh, ta=True, name=f"{tag}_mla_ukv_dw")
    (dcq, dckv), (dgq, dgkv) = _rowwise_bwd(_mla_norm_fn, sv["norm_rows"], [sv["gq"], sv["gkv"]], [dqn, dcn],
                                            name=f"{tag}_mla_norm_bwd", tile=272)
    G["mla_q_norm_g"], G["mla_kv_norm_g"] = dgq[0], dgkv[0]
    dproj = jnp.concatenate([dz, dxbc, dfq, dfk, dfv, dcq, dckv, ddt_raw, df_raw, dk_rope], axis=1).astype(BF16)
    dh1 = _mm(dproj, W["w_in_p"][l], tb=True, add=dh1_a, name=f"{tag}_in_proj_dx")
    G["w_in_p"] = _mm(sv["h1"], dproj, ta=True, name=f"{tag}_in_proj_dw")
    return dh1, G


def _embed(x, meta):
    return jnp.concatenate([jnp.zeros((PAD_ROWS, D_MODEL), F32), meta, x], axis=0)


def _layer_fwd(h, W, l, cosf, sins):
    ln = lambda n: W[n][l][None]
    h1, s1 = _ffn_fwd(h, W, "ffn1", l, ln("ln1_g"), ln("ln1_b"), f"l{l}_ffn1")
    h2, sm = _mixer_fwd(h1, W, l, cosf, sins)
    h3, s2 = _ffn_fwd(h2, W, "ffn2", l, ln("ln3_g"), ln("ln3_b"), f"l{l}_ffn2")
    return h3, (s1, sm, s2)


def _layer_bwd(dh, saved, W, l, cosf, sins):
    ln = lambda n: W[n][l][None]
    s1, sm, s2 = saved
    G = {}
    dh, dg, db = _ffn_bwd(dh, s2, W, "ffn2", l, ln("ln3_g"), ln("ln3_b"), G, f"l{l}_ffn2")
    G["ln3_g"], G["ln3_b"] = dg[0], db[0]
    dh, Gm = _mixer_bwd(dh, sm, W, l, cosf, sins, G)
    G.update(Gm)
    dh, dg, db = _ffn_bwd(dh, s1, W, "ffn1", l, ln("ln1_g"), ln("ln1_b"), G, f"l{l}_ffn1")
    G["ln1_g"], G["ln1_b"] = dg[0], db[0]
    return dh, G


def _local_step(x, target, W):
    h = _embed(x, W["meta"])
    tgt = jnp.concatenate([jnp.zeros((BLOCK, D_MODEL), F32), target], axis=0)
    cosf, sins = _rope_tables()
    saved = []
    for l in range(DEPTH):
        h, sv = _layer_fwd(h, W, l, cosf, sins)
        saved.append(sv)
    dh, loss = _loss_head(h, tgt, name="loss_head")
    grads = [None] * DEPTH
    for l in reversed(range(DEPTH)):
        dh, grads[l] = _layer_bwd(dh, saved[l], W, l, cosf, sins)
    return loss, dh, grads


WEIGHTS = ['meta', 'ffn1_w_gate', 'ffn1_w_up', 'ffn1_w_down', 'ln1_g', 'ln1_b', 'w_in', 'conv_w', 'conv_b', 'dt_bias',
           'a_log', 'd_skip', 'ssd_norm_g', 'fox_f_b', 'mla_q_norm_g', 'mla_w_uq', 'mla_kv_norm_g', 'mla_w_ukv',
           'w_out', 'ln2_g', 'ln2_b', 'ffn2_w_gate', 'ffn2_w_up', 'ffn2_w_down', 'ln3_g', 'ln3_b']
SMALL = ["ln1_g", "ln1_b", "conv_b", "dt_bias", "a_log", "d_skip", "ssd_norm_g", "fox_f_b", "mla_q_norm_g",
         "mla_kv_norm_g", "ln2_g", "ln2_b", "ln3_g", "ln3_b"]
MATMUL_W = ["ffn1_w_gate", "ffn1_w_up", "ffn1_w_down", "w_in", "mla_w_uq", "mla_w_ukv", "w_out", "ffn2_w_gate",
            "ffn2_w_up", "ffn2_w_down"]
SMALL_ROWS = 312


def _pad_to(a, axis, size):
    pads = [(0, 0)] * a.ndim
    pads[axis] = (0, size - a.shape[axis])
    return jnp.pad(a, pads)


def _chip_cols(full, chip, width):
    return lax.dynamic_slice_in_dim(full, chip * width, width, axis=full.ndim - 1)


def kernel(x, meta, ffn1_w_gate, ffn1_w_up, ffn1_w_down, ln1_g, ln1_b, w_in, conv_w, conv_b, dt_bias, a_log, d_skip, ssd_norm_g, fox_f_b, mla_q_norm_g, mla_w_uq, mla_kv_norm_g, mla_w_ukv, w_out, ln2_g, ln2_b, ffn2_w_gate, ffn2_w_up, ffn2_w_down, ln3_g, ln3_b, loss_target, m_meta, m_ffn1_w_gate, m_ffn1_w_up, m_ffn1_w_down, m_ln1_g, m_ln1_b, m_w_in, m_conv_w, m_conv_b, m_dt_bias, m_a_log, m_d_skip, m_ssd_norm_g, m_fox_f_b, m_mla_q_norm_g, m_mla_w_uq, m_mla_kv_norm_g, m_mla_w_ukv, m_w_out, m_ln2_g, m_ln2_b, m_ffn2_w_gate, m_ffn2_w_up, m_ffn2_w_down, m_ln3_g, m_ln3_b, v_meta, v_ffn1_w_gate, v_ffn1_w_up, v_ffn1_w_down, v_ln1_g, v_ln1_b, v_w_in, v_conv_w, v_conv_b, v_dt_bias, v_a_log, v_d_skip, v_ssd_norm_g, v_fox_f_b, v_mla_q_norm_g, v_mla_w_uq, v_mla_kv_norm_g, v_mla_w_ukv, v_w_out, v_ln2_g, v_ln2_b, v_ffn2_w_gate, v_ffn2_w_up, v_ffn2_w_down, v_ln3_g, v_ln3_b):
    args = dict(locals())
    w = {n: args[n] for n in WEIGHTS}
    m = {n: args["m_" + n] for n in WEIGHTS}
    v = {n: args["v_" + n] for n in WEIGHTS}
    xcoord, ycoord, _ = _my_pos()
    chip = 2 * xcoord + ycoord

    send = {}
    tr = lambda a: jnp.swapaxes(a, 1, 2)
    for pre in ("ffn1", "ffn2"):
        send[pre + "_w_gate"] = _pad_to(tr(w[pre + "_w_gate"]), 1, HP).astype(BF16)
        send[pre + "_w_up"] = _pad_to(tr(w[pre + "_w_up"]), 1, HP).astype(BF16)
        send[pre + "_w_down"] = _pad_to(w[pre + "_w_down"], 1, HP).astype(BF16)
    send["w_in"] = _pad_to(w["w_in"], 2, IN_SHARD_P).astype(BF16)
    for n in ("mla_w_uq", "mla_w_ukv", "w_out"):
        send[n] = w[n].astype(BF16)
    tiny = _allgather_chips([w["meta"].reshape(2, N_META // 2, D_MODEL // N_CHIPS), w["conv_w"]])
    meta_full = jnp.concatenate([tiny[0][k].reshape(N_META, D_MODEL // N_CHIPS) for k in range(N_CHIPS)], axis=1)
    shards = [send[n] for n in MATMUL_W]
    land_shapes = [jax.ShapeDtypeStruct((N_CHIPS,) + s.shape[1:], s.dtype) for s in shards]
    gather_l = lambda l: (lambda srcs, lands, ss, rs: _gather_copies(srcs, lands, l, ss, rs))

    W = {n: [None] * DEPTH for n in MATMUL_W + ["w_in_p", "mla_w_uq_p", "mla_w_ukv_p"]}
    W["conv_w"] = jnp.concatenate([tiny[1][k] for k in range(N_CHIPS)], axis=-1)
    W["meta"] = meta_full
    for n in SMALL:
        W[n] = w[n]

    def use_gathered(l, names, lands):
        got = dict(zip(names, lands))
        cat = lambda n, cut=None: jnp.concatenate([got[n][k][..., :cut] for k in range(N_CHIPS)], axis=-1)
        for n in names:
            W[n][l] = got[n]
        if "w_in" in got:
            W["w_in_p"][l] = _pad_in_proj(cat("w_in", IN_SHARD))
            W["mla_w_uq_p"][l] = _regroup_uq(cat("mla_w_uq"))
            W["mla_w_ukv_p"][l] = _regroup_ukv(cat("mla_w_ukv"))

    def chunk_grads(G, names):
        def chunked(name, ungroup, width, pad):
            full = ungroup(G[name])
            return _pad_to(jnp.moveaxis(full.reshape(full.shape[0], N_CHIPS, width), 1, 0), 2, pad)
        special = {"w_in": ("w_in_p", _unpad_in_proj, IN_SHARD, IN_SHARD_P),
                   "mla_w_uq": ("mla_w_uq_p", _ungroup_uq, MLA_NOPE + MLA_ROPE, MLA_NOPE + MLA_ROPE),
                   "mla_w_ukv": ("mla_w_ukv_p", _ungroup_ukv, MLA_NOPE + MLA_V, MLA_NOPE + MLA_V)}
        return [chunked(*special[n]) if n in special else G[n] for n in names]

    def rs_start(G, names, tag):
        pairs = _rs_pair_sums(chunk_grads(G, names), names, tag)
        handle = _ici_start(_exchange_copies, [p[1] for p in pairs], _exchange_land_shapes(pairs),
                            name=f"rs_exchange_{tag}_start")
        return pairs, handle

    def rs_end(pairs, handle, names, tag, after):
        _, r2 = _ici_wait(_exchange_copies, *handle[:4], after, name=f"rs_exchange_{tag}_wait")
        return dict(zip(names, _rs_finish(pairs, r2, names, tag)))

    na = 3
    first, rest = MATMUL_W[:na], MATMUL_W[na:]
    lands = _ici_blocking(gather_l(0), shards[:na], land_shapes[:na], name="gather_ici_l0_ffn1", after=[tiny[0]])
    got = _gather_d2d(shards[:na], lands, 0, "l0_ffn1")
    use_gathered(0, first, got)
    g_send, g_recv, g_srcs, g_lands, token = _ici_start(gather_l(0), shards[na:], land_shapes[na:],
                                                        name="gather_ici_l0_rest_start", after=[got[0]])
    cosf, sins = _rope_tables()
    ln = lambda n, l: W[n][l][None]
    h = _embed(x[0] + token[0, 0], meta_full)
    h1, s1 = _ffn_fwd(h, W, "ffn1", 0, ln("ln1_g", 0), ln("ln1_b", 0), "l0_ffn1")
    rest_shards, lands = _ici_wait(gather_l(0), g_send, g_recv, g_srcs, g_lands, h1, name="gather_ici_l0_rest_wait")
    shards = shards[:na] + rest_shards
    got = _gather_d2d(shards[na:], lands, 0, "l0_rest")
    use_gathered(0, rest, got)
    g_send, g_recv, g_srcs, g_lands, token = _ici_start(gather_l(1), shards, land_shapes, name="gather_ici_l1_start",
                                                        after=[got[0]])
    h2, sm = _mixer_fwd(h1 + token[0, 0], W, 0, cosf, sins)
    h, s2 = _ffn_fwd(h2, W, "ffn2", 0, ln("ln3_g", 0), ln("ln3_b", 0), "l0_ffn2")
    saved0 = (s1, sm, s2)
    shards, lands = _ici_wait(gather_l(1), g_send, g_recv, g_srcs, g_lands, h, name="gather_ici_l1_wait")
    use_gathered(1, MATMUL_W, _gather_d2d(shards, lands, 1, "l1"))
    h, saved1 = _layer_fwd(h, W, 1, cosf, sins)
    tgt = jnp.concatenate([jnp.zeros((BLOCK, D_MODEL), F32), loss_target[0]], axis=0)
    dh, loss = _loss_head(h, tgt, name="loss_head")
    G = [None] * DEPTH
    dh, G[1] = _layer_bwd(dh, saved1, W, 1, cosf, sins)
    ffn2_w, mix_w, ffn1_w = MATMUL_W[7:], MATMUL_W[3:7], MATMUL_W[:3]
    pairs_l1, x_l1 = rs_start(G[1], MATMUL_W, "l1")
    G0 = {}
    dh, dg, db = _ffn_bwd(dh + x_l1[4][0, 0], s2, W, "ffn2", 0, ln("ln3_g", 0), ln("ln3_b", 0), G0, "l0_ffn2")
    G0["ln3_g"], G0["ln3_b"] = dg[0], db[0]
    pairs_a, x_a = rs_start(G0, ffn2_w, "l0_ffn2")
    dh, Gm = _mixer_bwd(dh + x_a[4][0, 0], sm, W, 0, cosf, sins, G0)
    G0.update(Gm)
    reduced1 = rs_end(pairs_l1, x_l1, MATMUL_W, "l1", dh)
    pairs_b, x_b = rs_start(G0, mix_w, "l0_mix")
    dh0, dg, db = _ffn_bwd(dh + x_b[4][0, 0], s1, W, "ffn1", 0, ln("ln1_g", 0), ln("ln1_b", 0), G0, "l0_ffn1")
    G0["ln1_g"], G0["ln1_b"] = dg[0], db[0]
    G[0] = G0
    reduced0 = rs_end(pairs_a, x_a, ffn2_w, "l0_ffn2", dh0)
    pairs_c = _rs_pair_sums(chunk_grads(G0, ffn1_w), ffn1_w, "l0_ffn1")
    r2 = _ici_blocking(_exchange_copies, [p[1] for p in pairs_c], _exchange_land_shapes(pairs_c),
                       name="rs_exchange_l0_ffn1")
    reduced0.update(zip(ffn1_w, _rs_finish(pairs_c, r2, ffn1_w, "l0_ffn1")))
    reduced0.update(rs_end(pairs_b, x_b, mix_w, "l0_mix", reduced0[ffn1_w[0]]))
    reduced = {n: [reduced0[n], reduced1[n]] for n in MATMUL_W}

    small_parts = [jnp.stack([G[l][n] for l in range(DEPTH)]).reshape(-1) for n in SMALL]
    small_parts += [jnp.stack([G[l]["conv_w"] for l in range(DEPTH)]).reshape(-1), dh0[PAD_ROWS:BLOCK].reshape(-1),
                    loss[0, :1]]
    flat = jnp.concatenate(small_parts)
    flat = jnp.pad(flat, (0, SMALL_ROWS * BLOCK - flat.shape[0]))
    red = _allreduce_small(flat.reshape(SMALL_ROWS, BLOCK)).reshape(-1)
    grads, off = {}, 0
    for n in SMALL:
        size = int(np.prod(w[n].shape))
        grads[n] = red[off:off + size].reshape(w[n].shape)
        off += size
    conv_full = red[off:off + DEPTH * SSD_CONV * 768].reshape(DEPTH, SSD_CONV, 768)
    off += DEPTH * SSD_CONV * 768
    dmeta_full = red[off:off + N_META * D_MODEL].reshape(N_META, D_MODEL)
    off += N_META * D_MODEL
    loss_out = red[off]
    grads["conv_w"] = _chip_cols(conv_full, chip, 768 // N_CHIPS)
    grads["meta"] = _chip_cols(dmeta_full, chip, D_MODEL // N_CHIPS)

    delta, new_m, new_v = {}, {}, {}
    for n in MATMUL_W:
        if n.endswith("w_gate") or n.endswith("w_up"):
            res = _adamw(tr(w[n]), reduced[n], tr(m[n]), tr(v[n]), name=f"adamw_{n}")
            grads[n], delta[n], new_m[n], new_v[n] = [tr(r) for r in res]
        else:
            grads[n], delta[n], new_m[n], new_v[n] = _adamw(w[n], reduced[n], m[n], v[n], name=f"adamw_{n}")
    rest =[n for n in WEIGHTS if n not in MATMUL_W]

    def pack_small(d):
        f = jnp.concatenate([d[n].reshape(-1) for n in rest])
        tot = -(-f.shape[0] // (8 * BLOCK)) * 8 * BLOCK
        return jnp.pad(f, (0, tot - f.shape[0])).reshape(-1, BLOCK)

    _, d2, m2, v2 = _adamw(pack_small(w), [pack_small(grads)], pack_small(m), pack_small(v), name="adamw_small")
    off = 0
    for n in rest:
        size = int(np.prod(w[n].shape))
        for dst, src in ((delta, d2), (new_m, m2), (new_v, v2)):
            dst[n] = src.reshape(-1)[off:off + size].reshape(w[n].shape)
        off += size

    grad_x = dh0[BLOCK:][None]
    return (loss_out, grad_x, *[grads[n] for n in WEIGHTS], *[delta[n] for n in WEIGHTS],
            *[new_m[n] for n in WEIGHTS], *[new_v[n] for n in WEIGHTS])
```

```python
import functools

import numpy as np
import jax
import jax.numpy as jnp
from jax import lax
from jax.experimental import pallas as pl
from jax.experimental.pallas import tpu as pltpu

F32 = jnp.float32
BF16 = jnp.bfloat16
MESH = pl.DeviceIdType.MESH

D_MODEL = 1024
SEQ = 2048
N_META = 16
BLOCK = 128
PAD_ROWS = 112
LP = PAD_ROWS + N_META + SEQ
N_CHUNK = LP // BLOCK
DEPTH = 2
D_FF = 2816
N_CHIPS = 4
FF_SHARD = D_FF // N_CHIPS
HP = 768
FP = N_CHIPS * HP
SSD_HEADS, SSD_HD, SSD_D, SSD_GROUPS, SSD_STATE, SSD_CONV = 8, 64, 512, 2, 64, 4
FOX_HEADS, FOX_HD, FOX_D = 4, 64, 256
MLA_HEADS, MLA_Q_LORA, MLA_KV_LORA, MLA_NOPE, MLA_ROPE, MLA_V, MLA_D = 4, 256, 128, 64, 32, 64, 256
ROPE_HALF = MLA_ROPE // 2
ROPE_THETA = 10000.0
N_IN = 2476
IN_SHARD = N_IN // N_CHIPS
IN_SHARD_P = 640
ALPHA = (2 * DEPTH) ** 0.25
EPS = 1e-5
ADAM_LR, ADAM_B1, ADAM_B2, ADAM_EPS, ADAM_WD, ADAM_STEP = 0.001, 0.9, 0.999, 1e-08, 0.01, 10
NEG = -1e30
TM = 544

VMEM_LIMIT_BYTES = 56 * 1024 * 1024

PC_Z, PC_XBC, PC_FQ, PC_FK, PC_FV, PC_CQ, PC_CKV, PC_DT, PC_FR, PC_KR, PC_END = (
    0, 512, 1280, 1536, 1792, 2048, 2304, 2432, 2560, 2688, 2816)
OC_Z, OC_XBC, OC_DT, OC_FQ, OC_FK, OC_FV, OC_FR, OC_CQ, OC_CKV, OC_KR = (
    0, 512, 1280, 1288, 1544, 1800, 2056, 2060, 2316, 2444)


def _cparams(sem=None):
    return pltpu.CompilerParams(dimension_semantics=sem, vmem_limit_bytes=VMEM_LIMIT_BYTES)


def _tile(n, cap, mult):
    best = None
    for t in range(mult, min(n, cap) + 1, mult):
        if n % t == 0:
            best = t
    return best if best is not None else n


def _bs(shape, fn):
    return pl.BlockSpec(shape, fn)


ANY = pl.BlockSpec(memory_space=pl.ANY)


def _dims(ca, cb):
    return (((ca,), (cb,)), ((), ()))


def _raw_bdot(a, b, ca, cb):
    return lax.dot_general(a.astype(BF16), b.astype(BF16), _dims(ca, cb), preferred_element_type=F32)


@functools.partial(jax.custom_vjp, nondiff_argnums=(2, 3))
def _bdot(a, b, ca, cb):
    return _raw_bdot(a, b, ca, cb)


def _bdot_fwd(a, b, ca, cb):
    return _raw_bdot(a, b, ca, cb), (a, b)


def _bdot_bwd(ca, cb, res, g):
    a, b = res
    if (ca, cb) == (1, 0):
        return _raw_bdot(g, b, 1, 1), _raw_bdot(a, g, 0, 0)
    if (ca, cb) == (1, 1):
        return _raw_bdot(g, b, 1, 0), _raw_bdot(g, a, 0, 0)
    if (ca, cb) == (0, 0):
        return _raw_bdot(b, g, 1, 1), _raw_bdot(a, g, 1, 0)
    raise NotImplementedError((ca, cb))


_bdot.defvjp(_bdot_fwd, _bdot_bwd)


def _mm_core(a, b, *, a_spec, b_spec, o_spec, grid, out_shape, ca, cb, name, add=None):
    nk = grid[2]
    has_add = add is not None
    acc_shape = tuple(d for d in o_spec.block_shape if d is not None)

    def body(*refs):
        a_ref, b_ref = refs[0], refs[1]
        add_ref = refs[2] if has_add else None
        o_ref, acc_ref = refs[-2], refs[-1]
        k = pl.program_id(2)

        @pl.when(k == 0)
        def _():
            acc_ref[...] = jnp.zeros_like(acc_ref)

        acc_ref[...] += _raw_bdot(a_ref[...], b_ref[...], ca, cb)

        @pl.when(k == nk - 1)
        def _():
            r = acc_ref[...]
            if has_add:
                r = r + add_ref[...]
            o_ref[...] = r

    ins = [a, b] + ([add] if has_add else [])
    in_specs = [a_spec, b_spec] + ([o_spec] if has_add else [])
    return pl.pallas_call(
        body, name=name, grid=grid, in_specs=in_specs, out_specs=o_spec,
        out_shape=jax.ShapeDtypeStruct(out_shape, F32), scratch_shapes=[pltpu.VMEM(acc_shape, F32)],
        compiler_params=_cparams(("parallel", "parallel", "arbitrary")),
    )(*ins)


MM_VMEM_BUDGET = 40 * 1024 * 1024


def _divisors(n, mult):
    return [t for t in range(mult, n + 1, mult) if n % t == 0] or [n]


def _pick_tiles(M, N, K, a_bytes, b_bytes, ta, has_add):
    best = None
    for tm in _divisors(M, 128 if ta else 16):
        for tn in _divisors(N, 128):
            vmem = 2 * tm * K * a_bytes + 2 * K * tn * b_bytes + (3 + 2 * int(has_add)) * tm * tn * 4
            if vmem <= MM_VMEM_BUDGET:
                key = ((M // tm) * (N // tn), -tn)
                if best is None or key < best[0]:
                    best = (key, tm, tn)
    assert best is not None, (M, N, K)
    return best[1], best[2], K


def _mm(a, b, *, ta=False, tb=False, add=None, name):
    if ta:
        K, M = a.shape
    else:
        M, K = a.shape
    if tb:
        N, Kb = b.shape
    else:
        Kb, N = b.shape
    assert K == Kb, (a.shape, b.shape, ta, tb)
    tm, tn, tk = _pick_tiles(M, N, K, a.dtype.itemsize, b.dtype.itemsize, ta, add is not None)
    a_spec = _bs((tk, tm), lambda i, j, k: (k, i)) if ta else _bs((tm, tk), lambda i, j, k: (i, k))
    b_spec = _bs((tn, tk), lambda i, j, k: (j, k)) if tb else _bs((tk, tn), lambda i, j, k: (k, j))
    return _mm_core(a, b, a_spec=a_spec, b_spec=b_spec, o_spec=_bs((tm, tn), lambda i, j, k: (i, j)),
                    grid=(M // tm, N // tn, K // tk), out_shape=(M, N), ca=0 if ta else 1, cb=1 if tb else 0,
                    name=name, add=add)


def _row_entry(r, ncol):
    if isinstance(r, tuple):
        return r
    return r, r.shape[1] // ncol, 0


def _rowwise(fn, rows, pars, out_cols, *, name, tile, ncol=1, out_dtypes=None):
    rows = [_row_entry(r, ncol) for r in rows]
    L = rows[0][0].shape[0]
    nr, npar = len(rows), len(pars)
    in_specs = [_bs((tile, w), lambda g, i, o=o: (i, o + g)) for _, w, o in rows]
    in_specs += [_bs((p.shape[0], p.shape[1] // ncol), lambda g, i: (0, g)) for p in pars]
    out_specs = [_bs((tile, c // ncol), lambda g, i: (i, g)) for c in out_cols]

    def body(*refs):
        ins, outs = refs[:nr + npar], refs[nr + npar:]
        row0 = pl.program_id(1) * tile
        res = fn(row0, *[r[...] for r in ins])
        for o, v in zip(outs, res):
            o[...] = v.astype(o.dtype)

    return pl.pallas_call(
        body, name=name, grid=(ncol, L // tile), in_specs=in_specs, out_specs=out_specs,
        out_shape=[jax.ShapeDtypeStruct((L, c), d) for c, d in zip(out_cols, out_dtypes or [F32] * len(out_cols))],
        compiler_params=_cparams(("parallel", "parallel")),
    )(*[r[0] for r in rows], *pars)


def _rowwise_bwd(fn, rows, pars, douts, *, name, tile, ncol=1, row_grad=None, grad_dtypes=None):
    rows = [_row_entry(r, ncol) for r in rows]
    L = rows[0][0].shape[0]
    nr, npar, nd = len(rows), len(pars), len(douts)
    row_grad = [True] * nr if row_grad is None else row_grad
    in_specs = [_bs((tile, w), lambda g, i, o=o: (i, o + g)) for _, w, o in rows]
    in_specs += [_bs((p.shape[0], p.shape[1] // ncol), lambda g, i: (0, g)) for p in pars]
    in_specs += [_bs((tile, d.shape[1] // ncol), lambda g, i: (i, g)) for d in douts]
    g_widths = [w * ncol for (_, w, _), f in zip(rows, row_grad) if f]
    out_specs = [_bs((tile, w // ncol), lambda g, i: (i, g)) for w in g_widths]
    out_specs += [_bs((p.shape[0], p.shape[1] // ncol), lambda g, i: (0, g)) for p in pars]
    out_shape = [jax.ShapeDtypeStruct((L, w), d) for w, d in zip(g_widths, grad_dtypes or [F32] * len(g_widths))]
    out_shape += [jax.ShapeDtypeStruct(p.shape, F32) for p in pars]

    def body(*refs):
        ins = refs[:nr + npar]
        dos = refs[nr + npar:nr + npar + nd]
        outs = refs[nr + npar + nd:]
        i = pl.program_id(1)
        row0 = i * tile
        _, vjp = jax.vjp(lambda *a: tuple(fn(row0, *a)), *[r[...] for r in ins])
        grads = vjp(tuple(d[...].astype(F32) for d in dos))
        o = 0
        for j in range(nr):
            if row_grad[j]:
                outs[o][...] = grads[j].astype(outs[o].dtype)
                o += 1
        for j in range(npar):
            g, ref = grads[nr + j], outs[o + j]

            @pl.when(i == 0)
            def _(g=g, ref=ref):
                ref[...] = g

            @pl.when(i > 0)
            def _(g=g, ref=ref):
                ref[...] += g

    res = pl.pallas_call(
        body, name=name, grid=(ncol, L // tile), in_specs=in_specs, out_specs=out_specs, out_shape=out_shape,
        compiler_params=_cparams(("parallel", "arbitrary")),
    )(*[r[0] for r in rows], *pars, *douts)
    return res[:len(g_widths)], res[len(g_widths):]


def _row_ids(row0, shape):
    return row0 + lax.broadcasted_iota(jnp.int32, shape, 0)


def _sigmoid(x):
    return 1.0 / (1.0 + jnp.exp(-x))


@jax.custom_vjp
def _softplus(x):
    return jnp.maximum(x, 0.0) + jnp.log(1.0 + jnp.exp(-jnp.abs(x)))


def _softplus_fwd(x):
    return _softplus(x), x


def _softplus_bwd(x, g):
    return (g * _sigmoid(x),)


_softplus.defvjp(_softplus_fwd, _softplus_bwd)


def _silu(x):
    return x * _sigmoid(x)


def _swiglu_fn(row0, g, u):
    return (_silu(g) * u,)


def _make_res_ln_fn(scale):
    def fn(row0, h, o, gam, bet):
        pre = ALPHA * h + scale * o
        mu = jnp.mean(pre, axis=-1, keepdims=True)
        xc = pre - mu
        var = jnp.mean(xc * xc, axis=-1, keepdims=True)
        return (xc * lax.rsqrt(var + EPS) * gam + bet,)
    return fn


def _ssd_pre_fn(row0, raw, bias):
    dt = _softplus(raw + bias)
    return (jnp.where(_row_ids(row0, raw.shape) >= PAD_ROWS, dt, 0.0),)


def _ssd_post_fn(row0, y, xs, z, dskip, normg):
    v = (y + dskip * xs) * _silu(z)
    v = v * lax.rsqrt(jnp.mean(v * v, axis=-1, keepdims=True) + EPS)
    return (v * normg,)


def _mla_norm_fn(row0, cq, ckv, gq, gkv):
    qn = cq * lax.rsqrt(jnp.mean(cq * cq, axis=-1, keepdims=True) + EPS) * gq
    cn = ckv * lax.rsqrt(jnp.mean(ckv * ckv, axis=-1, keepdims=True) + EPS) * gkv
    return qn, cn


def _rope_fn(row0, q, k, cosf, sins):
    return (q * cosf + pltpu.roll(q, 64, 1) * sins, k * cosf + pltpu.roll(k, 64, 1) * sins)


def _rope_t_fn(row0, gq, gk, cosf, sins):
    return (gq * cosf + pltpu.roll(gq * sins, 64, 1), gk * cosf + pltpu.roll(gk * sins, 64, 1))


def _conv_fwd(x, x_off, w, b, *, name):
    C = w.shape[1]

    def body(x_ref, w_ref, b_ref, o_ref):
        rows = lax.broadcasted_iota(jnp.int32, (LP, BLOCK), 0)
        xv = jnp.where(rows >= PAD_ROWS, x_ref[...], 0.0)
        acc = b_ref[...] + w_ref[3:4, :] * xv
        for k in range(SSD_CONV - 1):
            acc = acc + w_ref[k:k + 1, :] * pltpu.roll(xv, SSD_CONV - 1 - k, 0)
        o_ref[...] = _silu(acc)

    return pl.pallas_call(
        body, name=name, grid=(C // BLOCK,),
        in_specs=[_bs((LP, BLOCK), lambda j: (0, j + x_off)), _bs((SSD_CONV, BLOCK), lambda j: (0, j)),
                  _bs((1, BLOCK), lambda j: (0, j))],
        out_specs=_bs((LP, BLOCK), lambda j: (0, j)),
        out_shape=jax.ShapeDtypeStruct((LP, C), F32), compiler_params=_cparams(("parallel",)),
    )(x, w, b)


def _conv_bwd(x, x_off, w, b, dout, *, name):
    C = w.shape[1]

    def body(x_ref, w_ref, b_ref, do_ref, dx_ref, dw_ref, db_ref):
        rows = lax.broadcasted_iota(jnp.int32, (LP, BLOCK), 0)
        real = rows >= PAD_ROWS
        xv = jnp.where(real, x_ref[...], 0.0)
        shifted = [pltpu.roll(xv, SSD_CONV - 1 - k, 0) for k in range(SSD_CONV - 1)] + [xv]
        acc = b_ref[...]
        for k in range(SSD_CONV):
            acc = acc + w_ref[k:k + 1, :] * shifted[k]
        sig = _sigmoid(acc)
        dacc = jnp.where(real, do_ref[...] * (sig * (1.0 + acc * (1.0 - sig))), 0.0)
        db_ref[...] = jnp.sum(dacc, axis=0, keepdims=True)
        dx = w_ref[3:4, :] * dacc
        for k in range(SSD_CONV):
            dw_ref[k:k + 1, :] = jnp.sum(dacc * shifted[k], axis=0, keepdims=True)
            if k < SSD_CONV - 1:
                dx = dx + w_ref[k:k + 1, :] * pltpu.roll(dacc, LP - (SSD_CONV - 1 - k), 0)
        dx_ref[...] = jnp.where(real, dx, 0.0)

    return pl.pallas_call(
        body, name=name, grid=(C // BLOCK,),
        in_specs=[_bs((LP, BLOCK), lambda j: (0, j + x_off)), _bs((SSD_CONV, BLOCK), lambda j: (0, j)),
                  _bs((1, BLOCK), lambda j: (0, j)), _bs((LP, BLOCK), lambda j: (0, j))],
        out_specs=[_bs((LP, BLOCK), lambda j: (0, j)), _bs((SSD_CONV, BLOCK), lambda j: (0, j)),
                   _bs((1, BLOCK), lambda j: (0, j))],
        out_shape=[jax.ShapeDtypeStruct((LP, C), F32), jax.ShapeDtypeStruct((SSD_CONV, C), F32),
                   jax.ShapeDtypeStruct((1, C), F32)],
        compiler_params=_cparams(("parallel",)),
    )(x, w, b, dout)


_BDIMS = {"nn": (((2,), (1,)), ((0,), (0,))), "nt": (((2,), (2,)), ((0,), (0,))), "tn": (((1,), (1,)), ((0,), (0,)))}


def _raw_bdot3(a, b, mode):
    return lax.dot_general(a.astype(BF16), b.astype(BF16), _BDIMS[mode], preferred_element_type=F32)


@functools.partial(jax.custom_vjp, nondiff_argnums=(2,))
def _bdot3(a, b, mode):
    return _raw_bdot3(a, b, mode)


def _bdot3_fwd(a, b, mode):
    return _raw_bdot3(a, b, mode), (a, b)


def _bdot3_bwd(mode, res, g):
    a, b = res
    if mode == "nn":
        return _raw_bdot3(g, b, "nt"), _raw_bdot3(a, g, "tn")
    if mode == "nt":
        return _raw_bdot3(g, b, "nn"), _raw_bdot3(g, a, "tn")
    return _raw_bdot3(b, g, "nt"), _raw_bdot3(a, g, "nn")


_bdot3.defvjp(_bdot3_fwd, _bdot3_bwd)


def _ssd_chunk(x, bm, cm, dtc, dtr, alog, prev):
    rep = SSD_HEADS // SSD_GROUPS
    per_head = lambda t: jnp.broadcast_to(t[:, None], (SSD_GROUPS, rep) + t.shape[1:]).reshape((SSD_HEADS,) + t.shape[1:])
    bm, cm = per_head(bm), per_head(cm)
    lane = lax.broadcasted_iota(jnp.int32, alog.shape, 2)
    a_neg = -jnp.exp(jnp.sum(jnp.where(lane == 0, alog, 0.0), axis=2, keepdims=True))
    ac_in = dtc * a_neg
    ar_in = dtr * a_neg
    li = lax.broadcasted_iota(jnp.int32, (1, BLOCK, BLOCK), 1)
    si = lax.broadcasted_iota(jnp.int32, (1, BLOCK, BLOCK), 2)
    causal = li >= si
    acum_c = jnp.sum(jnp.where(causal, ar_in, 0.0), axis=2, keepdims=True)
    acum_r = jnp.sum(jnp.where(li <= si, ac_in, 0.0), axis=1, keepdims=True)
    total = jnp.sum(ar_in, axis=2, keepdims=True)
    seg = jnp.exp(jnp.where(causal, acum_c - acum_r, NEG))
    xdt = x * dtc
    cb = _bdot3(cm, bm, "nt")
    y = _bdot3(cb * seg, xdt, "nn") + _bdot3(cm, prev, "nt") * jnp.exp(acum_c)
    st = _bdot3(xdt, bm * jnp.exp(total - acum_c), "tn")
    return y, prev * jnp.exp(total) + st


def _ssd_specs(rev):
    ci = (lambda c: N_CHUNK - 1 - c) if rev else (lambda c: c)
    x_spec = _bs((SSD_HEADS, BLOCK, SSD_HD), lambda c: (0, ci(c), 0))
    g_spec = _bs((SSD_GROUPS, BLOCK, SSD_STATE), lambda c: (0, ci(c), 0))
    dtc_spec = _bs((SSD_HEADS, BLOCK, 1), lambda c: (0, ci(c), 0))
    dtr_spec = _bs((SSD_HEADS, 1, BLOCK), lambda c: (0, 0, ci(c)))
    al_spec = _bs((SSD_HEADS, 1, BLOCK), lambda c: (0, 0, 0))
    st_spec = _bs((None, SSD_HEADS, SSD_HD, SSD_STATE), lambda c: (ci(c), 0, 0, 0))
    return x_spec, g_spec, dtc_spec, dtr_spec, al_spec, st_spec


def _ssd_fwd(x, bm, cm, dtc, dtr, alog, *, name):
    x_spec, g_spec, dtc_spec, dtr_spec, al_spec, st_spec = _ssd_specs(False)

    def body(x_ref, b_ref, c_ref, dtc_ref, dtr_ref, al_ref, y_ref, prev_ref, state):
        @pl.when(pl.program_id(0) == 0)
        def _():
            state[...] = jnp.zeros_like(state)

        prev = state[...]
        prev_ref[...] = prev
        y, new = _ssd_chunk(x_ref[...], b_ref[...], c_ref[...], dtc_ref[...], dtr_ref[...], al_ref[...], prev)
        y_ref[...] = y
        state[...] = new

    return pl.pallas_call(
        body, name=name, grid=(N_CHUNK,),
        in_specs=[x_spec, g_spec, g_spec, dtc_spec, dtr_spec, al_spec], out_specs=[x_spec, st_spec],
        out_shape=[jax.ShapeDtypeStruct((SSD_HEADS, LP, SSD_HD), F32),
                   jax.ShapeDtypeStruct((N_CHUNK, SSD_HEADS, SSD_HD, SSD_STATE), F32)],
        scratch_shapes=[pltpu.VMEM((SSD_HEADS, SSD_HD, SSD_STATE), F32)],
        compiler_params=_cparams(("arbitrary",)),
    )(x, bm, cm, dtc, dtr, alog)


def _ssd_bwd(x, bm, cm, dtc, dtr, alog, prevs, dy, *, name):
    x_spec, g_spec, dtc_spec, dtr_spec, al_spec, st_spec = _ssd_specs(True)

    def body(x_ref, b_ref, c_ref, dtc_ref, dtr_ref, al_ref, prev_ref, dy_ref,
             dx_ref, db_ref, dc_ref, ddtc_ref, ddtr_ref, dal_ref, dstate):
        c = pl.program_id(0)

        @pl.when(c == 0)
        def _():
            dstate[...] = jnp.zeros_like(dstate)

        _, vjp = jax.vjp(_ssd_chunk, x_ref[...], b_ref[...], c_ref[...], dtc_ref[...], dtr_ref[...], al_ref[...],
                         prev_ref[...])
        dx, db, dc, ddtc, ddtr, dal, dprev = vjp((dy_ref[...], dstate[...]))
        dx_ref[...] = dx
        db_ref[...] = db
        dc_ref[...] = dc
        ddtc_ref[...] = ddtc
        ddtr_ref[...] = ddtr
        dstate[...] = dprev

        @pl.when(c == 0)
        def _():
            dal_ref[...] = dal

        @pl.when(c > 0)
        def _():
            dal_ref[...] += dal

    hs = jax.ShapeDtypeStruct((SSD_HEADS, LP, SSD_HD), F32)
    gs = jax.ShapeDtypeStruct((SSD_GROUPS, LP, SSD_STATE), F32)
    return pl.pallas_call(
        body, name=name, grid=(N_CHUNK,),
        in_specs=[x_spec, g_spec, g_spec, dtc_spec, dtr_spec, al_spec, st_spec, x_spec],
        out_specs=[x_spec, g_spec, g_spec, dtc_spec, dtr_spec, al_spec],
        out_shape=[hs, gs, gs, jax.ShapeDtypeStruct((SSD_HEADS, LP, 1), F32),
                   jax.ShapeDtypeStruct((SSD_HEADS, 1, LP), F32), jax.ShapeDtypeStruct((SSD_HEADS, 1, BLOCK), F32)],
        scratch_shapes=[pltpu.VMEM((SSD_HEADS, SSD_HD, SSD_STATE), F32)],
        compiler_params=_cparams(("arbitrary",)),
    )(x, bm, cm, dtc, dtr, alog, prevs, dy)


def _tri_dot(tri, v):
    hi = v.astype(BF16)
    r1 = v - hi.astype(F32)
    mid = r1.astype(BF16)
    lo = (r1 - mid.astype(F32)).astype(BF16)
    t = tri.astype(BF16)
    d = lambda p: lax.dot_general(t, p, _dims(1, 0), preferred_element_type=F32)
    return d(hi) + d(mid) + d(lo)


def _fox_gate_fwd(raw, raw_blk, bias, *, name):
    def body(raw_ref, b_ref, c_ref, ct_ref, carry):
        j = pl.program_id(0)

        @pl.when(j == 0)
        def _():
            carry[...] = jnp.zeros_like(carry)

        rows = j * BLOCK + lax.broadcasted_iota(jnp.int32, (BLOCK, BLOCK), 0)
        lf = jnp.where(rows >= PAD_ROWS, -_softplus(-(raw_ref[...] + b_ref[...])), 0.0)
        li = lax.broadcasted_iota(jnp.int32, (BLOCK, BLOCK), 0)
        si = lax.broadcasted_iota(jnp.int32, (BLOCK, BLOCK), 1)
        cv = _tri_dot(jnp.where(li >= si, 1.0, 0.0), lf) + carry[...]
        c_ref[...] = cv
        ct_ref[...] = cv.T
        carry[...] += jnp.sum(lf, axis=0, keepdims=True)

    return pl.pallas_call(
        body, name=name, grid=(N_CHUNK,),
        in_specs=[_bs((BLOCK, BLOCK), lambda j: (j, raw_blk)), _bs((1, BLOCK), lambda j: (0, 0))],
        out_specs=[_bs((BLOCK, BLOCK), lambda j: (j, 0)), _bs((BLOCK, BLOCK), lambda j: (0, j))],
        out_shape=[jax.ShapeDtypeStruct((LP, BLOCK), F32), jax.ShapeDtypeStruct((BLOCK, LP), F32)],
        scratch_shapes=[pltpu.VMEM((1, BLOCK), F32)], compiler_params=_cparams(("arbitrary",)),
    )(raw, bias)


def _fox_gate_bwd(raw, raw_blk, bias, dc, dct, *, name):
    rj = lambda j: N_CHUNK - 1 - j

    def body(raw_ref, b_ref, dc_ref, dct_ref, draw_ref, db_ref, carry):
        j = pl.program_id(0)

        @pl.when(j == 0)
        def _():
            carry[...] = jnp.zeros_like(carry)

        rows = (N_CHUNK - 1 - j) * BLOCK + lax.broadcasted_iota(jnp.int32, (BLOCK, BLOCK), 0)
        li = lax.broadcasted_iota(jnp.int32, (BLOCK, BLOCK), 0)
        si = lax.broadcasted_iota(jnp.int32, (BLOCK, BLOCK), 1)
        dcv = dc_ref[...] + dct_ref[...].T
        dlf = _tri_dot(jnp.where(li <= si, 1.0, 0.0), dcv) + carry[...]
        carry[...] += jnp.sum(dcv, axis=0, keepdims=True)
        draw = jnp.where(rows >= PAD_ROWS, dlf * (1.0 - _sigmoid(raw_ref[...] + b_ref[...])), 0.0)
        draw_ref[...] = draw
        dsum = jnp.sum(draw, axis=0, keepdims=True)

        @pl.when(j == 0)
        def _():
            db_ref[...] = dsum

        @pl.when(j > 0)
        def _():
            db_ref[...] += dsum

    return pl.pallas_call(
        body, name=name, grid=(N_CHUNK,),
        in_specs=[_bs((BLOCK, BLOCK), lambda j: (rj(j), raw_blk)), _bs((1, BLOCK), lambda j: (0, 0)),
                  _bs((BLOCK, BLOCK), lambda j: (rj(j), 0)), _bs((BLOCK, BLOCK), lambda j: (0, rj(j)))],
        out_specs=[_bs((BLOCK, BLOCK), lambda j: (rj(j), 0)), _bs((1, BLOCK), lambda j: (0, 0))],
        out_shape=[jax.ShapeDtypeStruct((LP, BLOCK), F32), jax.ShapeDtypeStruct((1, BLOCK), F32)],
        scratch_shapes=[pltpu.VMEM((1, BLOCK), F32)], compiler_params=_cparams(("arbitrary",)),
    )(raw, bias, dc, dct)


ATT_W = 256
ATT_QB = 272
ATT_STEPS = LP // ATT_QB
ATT_KEYS = (640, 1152, 1664, LP)


def _lane_head(width, per, mod=None):
    lane = lax.broadcasted_iota(jnp.int32, (1, width), 1)
    if mod is not None:
        lane = lane % mod
    return lane // per


def _attn_mask(i, kw):
    r = i * ATT_QB + lax.broadcasted_iota(jnp.int32, (ATT_QB, kw), 0)
    c = lax.broadcasted_iota(jnp.int32, (ATT_QB, kw), 1)
    return (c <= r) & ((c >= PAD_ROWS) | (r < PAD_ROWS))


def _attn_by_key_class(i, fn):
    for p, kw in enumerate(ATT_KEYS):
        @pl.when(i // 2 == p)
        def _(kw=kw):
            fn(kw)


def _attn_specs(q, k, v, bias, rope):
    qspec = lambda blk, w=ATT_W: _bs((ATT_QB, w), lambda i: (i, blk))
    fspec = lambda blk, w=ATT_W: _bs((LP, w), lambda i: (0, blk))
    ins = [q[0], k[0], v[0]]
    specs = [qspec(q[1]), fspec(k[1]), fspec(v[1])]
    if bias is not None:
        ins += [bias[0], bias[1]]
        specs += [qspec(0, BLOCK), _bs((BLOCK, LP), lambda i: (0, 0))]
    if rope is not None:
        ins += [rope[0][0], rope[1][0]]
        specs += [qspec(rope[0][1], BLOCK), fspec(rope[1][1], BLOCK)]
    return ins, specs, qspec, fspec


def _attn_fwd(q, k, v, *, scale, name, bias=None, rope=None):
    ins, specs, qspec, fspec = _attn_specs(q, k, v, bias, rope)
    has_bias, has_rope = bias is not None, rope is not None

    def body(*refs):
        it = iter(refs)
        q_ref, k_ref, v_ref = next(it), next(it), next(it)
        if has_bias:
            c_ref, ct_ref = next(it), next(it)
        if has_rope:
            qr_ref, kr_ref = next(it), next(it)
        o_ref, lse_ref = next(it), next(it)
        i = pl.program_id(0)

        def block(kw):
            ok = _attn_mask(i, kw)
            qv, kv, vv = q_ref[...], k_ref[0:kw, :], v_ref[0:kw, :]
            hid, l128 = _lane_head(ATT_W, FOX_HD), _lane_head(BLOCK, 1)
            if has_rope:
                rid = _lane_head(BLOCK, ROPE_HALF, 64)
                qrv, krv = qr_ref[...], kr_ref[0:kw, :]
            def head(h, carry):
                o_acc, lse_acc = carry
                s = _raw_bdot(jnp.where(hid == h, qv, 0.0), kv, 1, 1)
                if has_rope:
                    s = s + _raw_bdot(jnp.where(rid == h, qrv, 0.0), krv, 1, 1)
                s = s * scale
                if has_bias:
                    cq = jnp.sum(jnp.where(l128 == h, c_ref[...], 0.0), axis=1, keepdims=True)
                    s = s + (cq - ct_ref[pl.ds(h, 1), 0:kw])
                s = jnp.where(ok, s, NEG)
                m = jnp.max(s, axis=1, keepdims=True)
                p = jnp.exp(s - m)
                l = jnp.sum(p, axis=1, keepdims=True)
                o_acc = jnp.where(hid == h, _raw_bdot(p, vv, 1, 0) / l, o_acc)
                lse_acc = jnp.where(l128 == h, m + jnp.log(l), lse_acc)
                return o_acc, lse_acc

            o_acc, lse_acc = lax.fori_loop(
                0, FOX_HEADS, head, (jnp.zeros((ATT_QB, ATT_W), F32), jnp.zeros((ATT_QB, BLOCK), F32)), unroll=True)
            o_ref[...] = o_acc
            lse_ref[...] = lse_acc

        _attn_by_key_class(i, block)

    return pl.pallas_call(
        body, name=name, grid=(ATT_STEPS,), in_specs=specs, out_specs=[qspec(0), qspec(0, BLOCK)],
        out_shape=[jax.ShapeDtypeStruct((LP, ATT_W), F32), jax.ShapeDtypeStruct((LP, BLOCK), F32)],
        compiler_params=_cparams(("parallel",)),
    )(*ins)


def _attn_bwd(q, k, v, o, lse, do, *, scale, name, bias=None, rope=None):
    ins, specs, qspec, fspec = _attn_specs(q, k, v, bias, rope)
    has_bias, has_rope = bias is not None, rope is not None
    ins += [o, lse, do[0]]
    specs += [qspec(0), qspec(0, BLOCK), qspec(do[1])]

    def body(*refs):
        it = iter(refs)
        q_ref, k_ref, v_ref = next(it), next(it), next(it)
        if has_bias:
            c_ref, ct_ref = next(it), next(it)
        if has_rope:
            qr_ref, kr_ref = next(it), next(it)
        o_ref, lse_ref, do_ref = next(it), next(it), next(it)
        dq_ref, dk_ref, dv_ref = next(it), next(it), next(it)
        if has_bias:
            dc_ref, dct_ref = next(it), next(it)
        if has_rope:
            dqr_ref, dkr_ref = next(it), next(it)
        i = pl.program_id(0)

        @pl.when(i == 0)
        def _():
            dk_ref[...] = jnp.zeros_like(dk_ref)
            dv_ref[...] = jnp.zeros_like(dv_ref)
            if has_rope:
                dkr_ref[...] = jnp.zeros_like(dkr_ref)
            if has_bias:
                dct_ref[...] = jnp.zeros_like(dct_ref)

        def block(kw):
            ok = _attn_mask(i, kw)
            qv, kv, vv = q_ref[...], k_ref[0:kw, :], v_ref[0:kw, :]
            ov, dov, lsev = o_ref[...], do_ref[...], lse_ref[...]
            hid, l128 = _lane_head(ATT_W, FOX_HD), _lane_head(BLOCK, 1)
            if has_rope:
                rid = _lane_head(BLOCK, ROPE_HALF, 64)
                qrv, krv = qr_ref[...], kr_ref[0:kw, :]

            def head(h, carry):
                dq_acc, aux_acc = carry
                qm = jnp.where(hid == h, qv, 0.0)
                s = _raw_bdot(qm, kv, 1, 1)
                if has_rope:
                    qrm = jnp.where(rid == h, qrv, 0.0)
                    s = s + _raw_bdot(qrm, krv, 1, 1)
                s = s * scale
                if has_bias:
                    cq = jnp.sum(jnp.where(l128 == h, c_ref[...], 0.0), axis=1, keepdims=True)
                    s = s + (cq - ct_ref[pl.ds(h, 1), 0:kw])
                s = jnp.where(ok, s, NEG)
                p = jnp.exp(s - jnp.sum(jnp.where(l128 == h, lsev, 0.0), axis=1, keepdims=True))
                dom = jnp.where(hid == h, dov, 0.0)
                dp = _raw_bdot(dom, vv, 1, 1)
                delta = jnp.sum(dom * ov, axis=1, keepdims=True)
                ds = p * (dp - delta)
                dq_acc = jnp.where(hid == h, _raw_bdot(ds, kv, 1, 0) * scale, dq_acc)
                dk_ref[0:kw, :] += _raw_bdot(ds, qm, 0, 0) * scale
                dv_ref[0:kw, :] += _raw_bdot(p, dom, 0, 0)
                if has_rope:
                    aux_acc = jnp.where(rid == h, _raw_bdot(ds, krv, 1, 0) * scale, aux_acc)
                    dkr_ref[0:kw, :] += _raw_bdot(ds, qrm, 0, 0) * scale
                if has_bias:
                    aux_acc = jnp.where(l128 == h, jnp.sum(ds, axis=1, keepdims=True), aux_acc)
                    dct_ref[pl.ds(h, 1), 0:kw] -= jnp.sum(ds, axis=0, keepdims=True)
                return dq_acc, aux_acc

            dq_acc, aux_acc = lax.fori_loop(
                0, FOX_HEADS, head, (jnp.zeros((ATT_QB, ATT_W), F32), jnp.zeros((ATT_QB, BLOCK), F32)))
            dq_ref[...] = dq_acc
            if has_bias:
                dc_ref[...] = aux_acc
            if has_rope:
                dqr_ref[...] = aux_acc

        _attn_by_key_class(i, block)

    wide = jax.ShapeDtypeStruct((LP, ATT_W), F32)
    narrow = jax.ShapeDtypeStruct((LP, BLOCK), F32)
    out_specs = [qspec(0), fspec(0), fspec(0)]
    out_shape = [wide, wide, wide]
    if has_bias:
        out_specs += [qspec(0, BLOCK), _bs((BLOCK, LP), lambda i: (0, 0))]
        out_shape += [narrow, jax.ShapeDtypeStruct((BLOCK, LP), F32)]
    if has_rope:
        out_specs += [qspec(0, BLOCK), fspec(0, BLOCK)]
        out_shape += [narrow, narrow]
    return pl.pallas_call(
        body, name=name, grid=(ATT_STEPS,), in_specs=specs, out_specs=out_specs, out_shape=out_shape,
        compiler_params=_cparams(("arbitrary",)),
    )(*ins)


def _loss_head(y, target, *, name):
    tile = 272

    def body(y_ref, t_ref, dy_ref, loss_ref):
        i = pl.program_id(0)
        rows = i * tile + lax.broadcasted_iota(jnp.int32, (tile, D_MODEL), 0)
        err = jnp.where(rows >= BLOCK, y_ref[...] - t_ref[...], 0.0)
        dy_ref[...] = err * (1.0 / D_MODEL)
        part = 0.5 * jnp.sum(jnp.sum(err * err, axis=1, keepdims=True) * (1.0 / D_MODEL), axis=0, keepdims=True)
        part = jnp.broadcast_to(part, (1, BLOCK))

        @pl.when(i == 0)
        def _():
            loss_ref[...] = part

        @pl.when(i > 0)
        def _():
            loss_ref[...] += part

    return pl.pallas_call(
        body, name=name, grid=(LP // tile,),
        in_specs=[_bs((tile, D_MODEL), lambda i: (i, 0)), _bs((tile, D_MODEL), lambda i: (i, 0))],
        out_specs=[_bs((tile, D_MODEL), lambda i: (i, 0)), _bs((1, BLOCK), lambda i: (0, 0))],
        out_shape=[jax.ShapeDtypeStruct((LP, D_MODEL), F32), jax.ShapeDtypeStruct((1, BLOCK), F32)],
        compiler_params=_cparams(("arbitrary",)),
    )(y, target)


def _adamw(w, gs, m, v, *, name, after=()):
    if w.ndim == 2:
        w, m, v = w[None], m[None], v[None]
        squeeze = True
    else:
        squeeze = False
    NL, R, C = w.shape
    assert len(gs) == NL
    CG = gs[0].shape[1]
    tile = _tile(R, 256, 8)

    def body(*refs):
        w_ref, g_refs = refs[0], refs[1:1 + NL]
        m_ref, v_ref = refs[1 + NL:3 + NL]
        go_ref, d_ref, nm_ref, nv_ref = refs[3 + NL + len(after):]
        gv = g_refs[0][:, :C]
        for j in range(1, NL):
            gv = jnp.where(pl.program_id(0) == j, g_refs[j][:, :C], gv)
        nm = ADAM_B1 * m_ref[...] + (1.0 - ADAM_B1) * gv
        nv = ADAM_B2 * v_ref[...] + (1.0 - ADAM_B2) * (gv * gv)
        m_hat = nm / (1.0 - ADAM_B1 ** ADAM_STEP)
        v_hat = nv / (1.0 - ADAM_B2 ** ADAM_STEP)
        go_ref[...] = gv
        d_ref[...] = -ADAM_LR * (m_hat / (jnp.sqrt(v_hat) + ADAM_EPS) + ADAM_WD * w_ref[...])
        nm_ref[...] = nm
        nv_ref[...] = nv

    spec = _bs((None, tile, C), lambda l, i: (l, i, 0))
    gspecs = [_bs((tile, CG), lambda l, i, j=j: (jnp.where(l == j, i, 0), 0)) for j in range(NL)]
    res = pl.pallas_call(
        body, name=name, grid=(NL, R // tile), in_specs=[spec, *gspecs, spec, spec, *[ANY] * len(after)],
        out_specs=[spec] * 4, out_shape=[jax.ShapeDtypeStruct((NL, R, C), F32)] * 4,
        compiler_params=_cparams(("parallel", "parallel")),
    )(w, *gs, m, v, *after)
    return [r[0] for r in res] if squeeze else res


def _my_pos():
    return lax.axis_index("x"), lax.axis_index("y"), lax.axis_index("c")


def _other_chips(x, y):
    return [(1 - x, y), (x, 1 - y), (1 - x, 1 - y)]


def _allgather_chips(shards):
    n = len(shards)
    per = 7

    def body(*refs):
        ins, outs = refs[:n], refs[n:2 * n]
        send_sems, recv_sems = refs[2 * n], refs[2 * n + 1]
        x, y, c = _my_pos()
        chips = _other_chips(x, y)
        sibling, me = (x, y, 1 - c), 2 * x + y

        def cp(a, kk, src, dst, to):
            return pltpu.make_async_remote_copy(src_ref=src, dst_ref=dst, send_sem=send_sems.at[per * a + kk],
                                                recv_sem=recv_sems.at[per * a + kk], device_id=to, device_id_type=MESH)

        sends = []
        for a in range(n):
            for j, chip in enumerate(chips):
                sends.append(cp(a, j, ins[a].at[c], outs[a].at[me, c], (*chip, c)))
            sends.append(cp(a, 3, ins[a], outs[a].at[me], sibling))
        for s in sends:
            s.start()
        for a in range(n):
            for j, chip in enumerate(chips):
                slab = outs[a].at[2 * chip[0] + chip[1], c]
                cp(a, j, slab, slab, (x, y, c)).wait_recv()
                fwd = cp(a, 4 + j, slab, slab, sibling)
                fwd.start()
                sends.append(fwd)
        for a in range(n):
            cp(a, 3, ins[a], outs[a].at[me], (x, y, c)).wait_recv()
            for j, chip in enumerate(chips):
                slab = outs[a].at[2 * chip[0] + chip[1], 1 - c]
                cp(a, 4 + j, slab, slab, (x, y, c)).wait_recv()
        for s in sends:
            s.wait_send()

    return pl.pallas_call(
        body, name="allgather_chips", in_specs=[ANY] * n, out_specs=[ANY] * n,
        out_shape=[jax.ShapeDtypeStruct((N_CHIPS,) + s.shape, s.dtype) for s in shards],
        scratch_shapes=[pltpu.SemaphoreType.DMA((per * n,)), pltpu.SemaphoreType.DMA((per * n,))],
    )(*shards)


def _rs_swap_rows(gs, tag):
    n = len(gs)

    def body(*refs):
        ins, outs = refs[:n], refs[n:2 * n]
        send_sems, recv_sems = refs[2 * n], refs[2 * n + 1]
        x, y, c = _my_pos()
        cps = []
        for a in range(n):
            half = ins[a].shape[1] // 2
            cps.append(pltpu.make_async_remote_copy(
                src_ref=ins[a].at[:, pl.ds((1 - c) * half, half)], dst_ref=outs[a], send_sem=send_sems.at[a],
                recv_sem=recv_sems.at[a], device_id=(x, y, 1 - c), device_id_type=MESH))
        for cp in cps:
            cp.start()
        for cp in cps:
            cp.wait()

    return pl.pallas_call(
        body, name=f"rs_swap_rows_{tag}", in_specs=[ANY] * n, out_specs=[ANY] * n,
        out_shape=[jax.ShapeDtypeStruct((N_CHIPS, g.shape[1] // 2, g.shape[2]), g.dtype) for g in gs],
        scratch_shapes=[pltpu.SemaphoreType.DMA((n,)), pltpu.SemaphoreType.DMA((n,))],
    )(*gs)


def _rs_add_pair(g, r, pos, *, name):
    _, H, C = r.shape
    tile = _tile(H, 512, 16)
    nt = H // tile

    def body(pos_ref, g_ref, r_ref, o32_ref, o16_ref):
        s = g_ref[...] + r_ref[...]
        o32_ref[...] = s
        o16_ref[...] = s.astype(BF16)

    spec = _bs((None, tile, C), lambda k, i, pos_ref: (k, i, 0))
    grid_spec = pltpu.PrefetchScalarGridSpec(
        num_scalar_prefetch=1, grid=(N_CHIPS, nt),
        in_specs=[_bs((None, tile, C), lambda k, i, pos_ref: (k, pos_ref[1] * nt + i, 0)), spec],
        out_specs=[spec, spec])
    return pl.pallas_call(
        body, name=name, grid_spec=grid_spec,
        out_shape=[jax.ShapeDtypeStruct((N_CHIPS, H, C), F32), jax.ShapeDtypeStruct((N_CHIPS, H, C), BF16)],
        compiler_params=_cparams(("parallel", "parallel")),
    )(pos, g, r)


def _exchange_copies(srcs, lands, send_sems, recv_sems):
    x, y, c = _my_pos()
    starts, landing = [], []
    for a in range(len(srcs)):
        for j, chip in enumerate(_other_chips(x, y)):
            sems = dict(send_sem=send_sems.at[3 * a + j], recv_sem=recv_sems.at[3 * a + j], device_id_type=MESH)
            starts.append(pltpu.make_async_remote_copy(
                src_ref=srcs[a].at[2 * chip[0] + chip[1]], dst_ref=lands[a].at[j], device_id=(*chip, c), **sems))
            landing.append(pltpu.make_async_remote_copy(
                src_ref=lands[a].at[j], dst_ref=lands[a].at[j], device_id=(x, y, c), **sems))
    return starts, landing


def _gather_copies(srcs, lands, l, send_sems, recv_sems):
    x, y, c = _my_pos()
    me = 2 * x + y
    starts, landing = [], []
    for a in range(len(srcs)):
        half = srcs[a].shape[1] // 2
        mine = pl.ds(c * half, half)
        for j, chip in enumerate(_other_chips(x, y)):
            sems = dict(send_sem=send_sems.at[3 * a + j], recv_sem=recv_sems.at[3 * a + j], device_id_type=MESH)
            starts.append(pltpu.make_async_remote_copy(
                src_ref=srcs[a].at[l, mine], dst_ref=lands[a].at[me, mine], device_id=(*chip, c), **sems))
            slab = lands[a].at[2 * chip[0] + chip[1], mine]
            landing.append(pltpu.make_async_remote_copy(src_ref=slab, dst_ref=slab, device_id=(x, y, c), **sems))
    return starts, landing


HBM = pl.BlockSpec(memory_space=pltpu.HBM)
SEM = pl.BlockSpec(memory_space=pltpu.SEMAPHORE)


def _ici_blocking(copies_fn, srcs, land_shapes, *, name, after=()):
    n, na = len(srcs), len(after)

    def body(*refs):
        starts, landing = copies_fn(refs[:n], refs[n + na:2 * n + na], refs[2 * n + na], refs[2 * n + na + 1])
        for cp in starts:
            cp.start()
        for cp in landing:
            cp.wait_recv()
        for cp in starts:
            cp.wait_send()

    return pl.pallas_call(
        body, name=name, in_specs=[ANY] * (n + na), out_specs=[ANY] * n, out_shape=land_shapes,
        scratch_shapes=[pltpu.SemaphoreType.DMA((3 * n,)), pltpu.SemaphoreType.DMA((3 * n,))],
    )(*srcs, *after)


def _ici_start(copies_fn, srcs, land_shapes, *, name, after=()):
    n, na = len(srcs), len(after)

    def body(*refs):
        starts, _ = copies_fn(refs[:n], refs[n:2 * n], refs[2 * n + na], refs[2 * n + na + 1])
        for cp in starts:
            cp.start()
        refs[-1][...] = jnp.zeros_like(refs[-1])

    sems = pltpu.SemaphoreType.DMA((3 * n,))
    hbm = lambda s: pltpu.HBM(s.shape, s.dtype)
    lands = [pltpu.with_memory_space_constraint(lax.empty(s.shape, s.dtype), pltpu.HBM) for s in land_shapes]
    res = pl.pallas_call(
        body, name=name, in_specs=[HBM] * (2 * n) + [ANY] * na,
        out_specs=(SEM, SEM, *[HBM] * (2 * n), pl.BlockSpec(memory_space=pltpu.VMEM)),
        out_shape=(sems, sems, *[hbm(s) for s in srcs], *[hbm(s) for s in land_shapes],
                   jax.ShapeDtypeStruct((8, BLOCK), F32)),
        input_output_aliases={i: 2 + i for i in range(2 * n)},
        compiler_params=pltpu.CompilerParams(has_side_effects=pltpu.SideEffectType.DATAFLOW_SIDE_EFFECTING),
    )(*[pltpu.with_memory_space_constraint(s, pltpu.HBM) for s in srcs], *lands, *after)
    return res[0], res[1], list(res[2:2 + n]), list(res[2 + n:2 + 2 * n]), res[-1]


def _ici_wait(copies_fn, send_sems, recv_sems, srcs, lands, after, *, name):
    n = len(srcs)

    def body(*refs):
        starts, landing = copies_fn(refs[:n], refs[n:2 * n], refs[2 * n], refs[2 * n + 1])
        for cp in starts:
            cp.wait_send()
        for cp in landing:
            cp.wait_recv()

    hbm = lambda s: pltpu.HBM(s.shape, s.dtype)
    res = pl.pallas_call(
        body, name=name, in_specs=[*[HBM] * (2 * n), SEM, SEM, ANY], out_specs=[HBM] * (2 * n),
        out_shape=[*[hbm(s) for s in srcs], *[hbm(s) for s in lands]],
        input_output_aliases={i: i for i in range(2 * n)},
        compiler_params=pltpu.CompilerParams(has_side_effects=pltpu.SideEffectType.DATAFLOW_SIDE_EFFECTING),
    )(*srcs, *lands, send_sems, recv_sems, after)
    return list(res[:n]), list(res[n:])


def _gather_d2d(shards, lands, l, tag):
    n = len(shards)

    def body(*refs):
        ins, outs = refs[:n], refs[2 * n:3 * n]
        send_sems, recv_sems = refs[3 * n], refs[3 * n + 1]
        x, y, c = _my_pos()
        me, sibling = 2 * x + y, (x, y, 1 - c)
        starts, landing = [], []
        for a in range(n):
            half = ins[a].shape[1] // 2
            mine, theirs = pl.ds(c * half, half), pl.ds((1 - c) * half, half)
            pairs = [(ins[a].at[l], outs[a].at[me], outs[a].at[me])]
            for chip in _other_chips(x, y):
                k = 2 * chip[0] + chip[1]
                pairs.append((outs[a].at[k, mine], outs[a].at[k, mine], outs[a].at[k, theirs]))
            for j, (src, dst, lands_here) in enumerate(pairs):
                sems = dict(send_sem=send_sems.at[4 * a + j], recv_sem=recv_sems.at[4 * a + j], device_id_type=MESH)
                starts.append(pltpu.make_async_remote_copy(src_ref=src, dst_ref=dst, device_id=sibling, **sems))
                landing.append(pltpu.make_async_remote_copy(src_ref=lands_here, dst_ref=lands_here, device_id=(x, y, c),
                                                            **sems))
        for cp in starts:
            cp.start()
        for cp in landing:
            cp.wait_recv()
        for cp in starts:
            cp.wait_send()

    return pl.pallas_call(
        body, name=f"gather_d2d_{tag}", in_specs=[ANY] * (2 * n), out_specs=[ANY] * n,
        out_shape=[jax.ShapeDtypeStruct(s.shape, s.dtype) for s in lands],
        input_output_aliases={n + a: a for a in range(n)},
        scratch_shapes=[pltpu.SemaphoreType.DMA((4 * n,)), pltpu.SemaphoreType.DMA((4 * n,))],
    )(*shards, *lands)


def _rs_add_chips(p32, r16, pos, *, name):
    _, H, C = p32.shape
    tile = _tile(H, 512, 16)
    nt = H // tile

    def body(pos_ref, p_ref, r_ref, o_ref):
        o_ref[...] = ((p_ref[...] + r_ref[0].astype(F32)) + r_ref[1].astype(F32)) + r_ref[2].astype(F32)

    grid_spec = pltpu.PrefetchScalarGridSpec(
        num_scalar_prefetch=1, grid=(nt,),
        in_specs=[_bs((None, tile, C), lambda i, pos_ref: (pos_ref[0], i, 0)),
                  _bs((3, tile, C), lambda i, pos_ref: (0, i, 0))],
        out_specs=_bs((tile, C), lambda i, pos_ref: (pos_ref[1] * nt + i, 0)))
    return pl.pallas_call(
        body, name=name, grid_spec=grid_spec, out_shape=jax.ShapeDtypeStruct((2 * H, C), F32),
        compiler_params=_cparams(("parallel",)),
    )(pos, p32, r16)


def _rs_join_rows(fs, tag):
    n = len(fs)

    def body(*refs):
        outs = refs[n:2 * n]
        send_sems, recv_sems = refs[2 * n], refs[2 * n + 1]
        x, y, c = _my_pos()
        for a in range(n):
            half = outs[a].shape[0] // 2
            mine = outs[a].at[pl.ds(c * half, half)]
            pltpu.make_async_remote_copy(src_ref=mine, dst_ref=mine, send_sem=send_sems.at[a],
                                         recv_sem=recv_sems.at[a], device_id=(x, y, 1 - c), device_id_type=MESH).start()
        for a in range(n):
            half = outs[a].shape[0] // 2
            pltpu.make_async_remote_copy(
                src_ref=outs[a].at[pl.ds(c * half, half)], dst_ref=outs[a].at[pl.ds((1 - c) * half, half)],
                send_sem=send_sems.at[a], recv_sem=recv_sems.at[a], device_id=(x, y, 1 - c), device_id_type=MESH).wait()

    return pl.pallas_call(
        body, name=f"rs_join_rows_{tag}", in_specs=[ANY] * n, out_specs=[ANY] * n,
        out_shape=[jax.ShapeDtypeStruct(f.shape, f.dtype) for f in fs],
        input_output_aliases={a: a for a in range(n)},
        scratch_shapes=[pltpu.SemaphoreType.DMA((n,)), pltpu.SemaphoreType.DMA((n,))],
    )(*fs)


def _pos_vector():
    x, y, c = _my_pos()
    return jnp.stack([2 * x + y, c]).astype(jnp.int32)


def _swap_copies(srcs, lands, send_sems, recv_sems):
    x, y, c = _my_pos()
    starts, landing = [], []
    for a in range(len(srcs)):
        half = srcs[a].shape[1] // 2
        sems = dict(send_sem=send_sems.at[3 * a], recv_sem=recv_sems.at[3 * a], device_id_type=MESH)
        starts.append(pltpu.make_async_remote_copy(
            src_ref=srcs[a].at[:, pl.ds((1 - c) * half, half)], dst_ref=lands[a], device_id=(x, y, 1 - c), **sems))
        landing.append(pltpu.make_async_remote_copy(src_ref=lands[a], dst_ref=lands[a], device_id=(x, y, c), **sems))
    return starts, landing


def _swap_land_shapes(gs):
    return [jax.ShapeDtypeStruct((N_CHIPS, g.shape[1] // 2, g.shape[2]), g.dtype) for g in gs]


def _rs_add_pairs(gs, r1, names, tag):
    pos = _pos_vector()
    return [_rs_add_pair(g, r, pos, name=f"rs_add_pair_{tag}_{nm}") for g, r, nm in zip(gs, r1, names)]


def _rs_pair_sums(gs, names, tag):
    return _rs_add_pairs(gs, _rs_swap_rows(gs, tag), names, tag)


def _rs_finish(pairs, r2, names, tag):
    pos = _pos_vector()
    fs = [_rs_add_chips(p[0], r, pos, name=f"rs_add_chips_{tag}_{nm}") for p, r, nm in zip(pairs, r2, names)]
    return _rs_join_rows(fs, tag)


def _exchange_land_shapes(pairs):
    return [jax.ShapeDtypeStruct((3,) + p[1].shape[1:], p[1].dtype) for p in pairs]


def _allreduce_small(buf):
    R, W = buf.shape

    def body(b_ref, o_ref, gather, send_sems, recv_sems):
        x, y, c = _my_pos()
        me = 4 * x + 2 * y + c
        gather[me] = b_ref[...]
        cps = []
        for d in range(1, 8):
            peer = (x ^ (d >> 2), y ^ ((d >> 1) & 1), c ^ (d & 1))
            cps.append(pltpu.make_async_remote_copy(
                src_ref=b_ref, dst_ref=gather.at[me], send_sem=send_sems.at[d - 1], recv_sem=recv_sems.at[d - 1],
                device_id=peer, device_id_type=MESH))
        for cp in cps:
            cp.start()
        for d in range(1, 8):
            pltpu.make_async_remote_copy(
                src_ref=b_ref, dst_ref=gather.at[me ^ d], send_sem=send_sems.at[d - 1], recv_sem=recv_sems.at[d - 1],
                device_id=(x, y, c), device_id_type=MESH).wait_recv()
        for cp in cps:
            cp.wait_send()
        acc = gather[0]
        for d in range(1, 8):
            acc = acc + gather[d]
        o_ref[...] = acc

    vm = pl.BlockSpec(memory_space=pltpu.VMEM)
    return pl.pallas_call(
        body, name="allreduce_small", in_specs=[vm], out_specs=vm, out_shape=jax.ShapeDtypeStruct((R, W), F32),
        scratch_shapes=[pltpu.VMEM((8, R, W), F32), pltpu.SemaphoreType.DMA((7,)), pltpu.SemaphoreType.DMA((7,))],
    )(buf)


def _heads(a, h, d):
    return a.reshape(a.shape[0], h, d).transpose(1, 0, 2)


def _unheads(a):
    h, L, d = a.shape
    return a.transpose(1, 0, 2).reshape(L, h * d)


def _rope_tables():
    pos = jnp.maximum(jnp.arange(LP, dtype=F32) - PAD_ROWS, 0.0)
    inv_freq = 1.0 / (ROPE_THETA ** (jnp.arange(0, MLA_ROPE, 2, dtype=F32) / MLA_ROPE))
    ang = pos[:, None] * inv_freq[None, :]
    cos, sin = jnp.tile(jnp.cos(ang), (1, MLA_HEADS)), jnp.tile(jnp.sin(ang), (1, MLA_HEADS))
    return jnp.concatenate([cos, cos], axis=1), jnp.concatenate([-sin, sin], axis=1)


def _lane_pad(a, width=BLOCK):
    return jnp.pad(a, ((0, 0), (0, width - a.shape[1])))


def _pad_in_proj(w):
    sl = lambda start, size: w[:, start:start + size]
    return jnp.concatenate([
        sl(OC_Z, 512), sl(OC_XBC, 768), sl(OC_FQ, 256), sl(OC_FK, 256), sl(OC_FV, 256), sl(OC_CQ, 256), sl(OC_CKV, 128),
        _lane_pad(sl(OC_DT, SSD_HEADS)), _lane_pad(sl(OC_FR, FOX_HEADS)),
        jnp.tile(sl(OC_KR, ROPE_HALF), (1, MLA_HEADS)), jnp.tile(sl(OC_KR + ROPE_HALF, ROPE_HALF), (1, MLA_HEADS))], axis=1)


def _unpad_in_proj(wp):
    sl = lambda start, size: wp[:, start:start + size]
    rope = lambda start: sl(start, 64).reshape(wp.shape[0], MLA_HEADS, ROPE_HALF).sum(axis=1)
    return jnp.concatenate([
        sl(PC_Z, 512), sl(PC_XBC, 768), sl(PC_DT, SSD_HEADS), sl(PC_FQ, 256), sl(PC_FK, 256), sl(PC_FV, 256),
        sl(PC_FR, FOX_HEADS), sl(PC_CQ, 256), sl(PC_CKV, 128), rope(PC_KR), rope(PC_KR + 64)], axis=1)


def _regroup_uq(w):
    w3 = w.reshape(w.shape[0], MLA_HEADS, MLA_NOPE + MLA_ROPE)
    return jnp.concatenate([w3[:, :, :MLA_NOPE].reshape(w.shape[0], -1),
                            w3[:, :, MLA_NOPE:MLA_NOPE + ROPE_HALF].reshape(w.shape[0], -1),
                            w3[:, :, MLA_NOPE + ROPE_HALF:].reshape(w.shape[0], -1)], axis=1)


def _ungroup_uq(wp):
    n = wp.shape[0]
    return jnp.concatenate([wp[:, :256].reshape(n, MLA_HEADS, MLA_NOPE), wp[:, 256:320].reshape(n, MLA_HEADS, ROPE_HALF),
                            wp[:, 320:].reshape(n, MLA_HEADS, ROPE_HALF)], axis=2).reshape(n, -1)


def _regroup_ukv(w):
    w3 = w.reshape(w.shape[0], MLA_HEADS, MLA_NOPE + MLA_V)
    return jnp.concatenate([w3[:, :, :MLA_NOPE].reshape(w.shape[0], -1), w3[:, :, MLA_NOPE:].reshape(w.shape[0], -1)],
                           axis=1)


def _ungroup_ukv(wp):
    n = wp.shape[0]
    return jnp.concatenate([wp[:, :256].reshape(n, MLA_HEADS, MLA_NOPE), wp[:, 256:].reshape(n, MLA_HEADS, MLA_V)],
                           axis=2).reshape(n, -1)


TMF = 1088
N_IF = LP // TMF


def _chunk_rows_mm(a, w, l, chunk_h, *, name, add=None):
    N = w.shape[2]
    return _mm_core(a, w, a_spec=_bs((TMF, chunk_h), lambda i, j, k: (i, k)),
                    b_spec=_bs((None, chunk_h, N), lambda i, j, k: (k, 0, 0)),
                    o_spec=_bs((TMF, N), lambda i, j, k: (i, 0)), grid=(N_IF, 1, N_CHIPS),
                    out_shape=(LP, N), ca=1, cb=0, name=name, add=add)


def _chunk_rows_dx(g, w, l, chunk_h, *, name):
    N = w.shape[2]
    return _mm_core(g, w, a_spec=_bs((TMF, N), lambda i, j, k: (i, 0)),
                    b_spec=_bs((None, chunk_h, N), lambda i, j, k: (j, 0, 0)),
                    o_spec=_bs((TMF, chunk_h), lambda i, j, k: (i, j)), grid=(N_IF, N_CHIPS, 1),
                    out_shape=(LP, N_CHIPS * chunk_h), ca=1, cb=1, name=name)


def _chunk_rows_dw(a, g, chunk_h, *, name):
    N = g.shape[1]
    return _mm_core(a, g, a_spec=_bs((LP, chunk_h), lambda i, j, k: (0, i)), b_spec=_bs((LP, N), lambda i, j, k: (0, 0)),
                    o_spec=_bs((None, chunk_h, N), lambda i, j, k: (i, 0, 0)), grid=(N_CHIPS, 1, 1),
                    out_shape=(N_CHIPS, chunk_h, N), ca=0, cb=0, name=name)


def _ffn_fwd(h, W, pre, l, gam, bet, tag):
    g = _chunk_rows_dx(h, W[pre + "_w_gate"][l], l, HP, name=f"{tag}_gate")
    u = _chunk_rows_dx(h, W[pre + "_w_up"][l], l, HP, name=f"{tag}_up")
    (act,) = _rowwise(_swiglu_fn, [g, u], [], [FP], name=f"{tag}_swiglu", tile=TM, ncol=N_CHIPS, out_dtypes=[BF16])
    o = _chunk_rows_mm(act, W[pre + "_w_down"][l], l, HP, name=f"{tag}_down")
    (out,) = _rowwise(_make_res_ln_fn(0.5), [h, o], [gam, bet], [D_MODEL], name=f"{tag}_ln", tile=272)
    return out, (h, g, u, act, o)


def _ffn_bwd(dout, saved, W, pre, l, gam, bet, GB, tag):
    h, g, u, act, o = saved
    (dh_a, do), (dgam, dbet) = _rowwise_bwd(_make_res_ln_fn(0.5), [h, o], [gam, bet], [dout], name=f"{tag}_ln_bwd",
                                            tile=272, grad_dtypes=[F32, BF16])
    dact = _chunk_rows_dx(do, W[pre + "_w_down"][l], l, HP, name=f"{tag}_down_dx")
    GB[pre + "_w_down"] = _chunk_rows_dw(act, do, HP, name=f"{tag}_down_dw")
    (dg, du), _ = _rowwise_bwd(_swiglu_fn, [g, u], [], [dact], name=f"{tag}_swiglu_bwd", tile=TM, ncol=N_CHIPS,
                               grad_dtypes=[BF16, BF16])
    GB[pre + "_w_gate"] = _chunk_rows_dw(dg, h, HP, name=f"{tag}_gate_dw")
    GB[pre + "_w_up"] = _chunk_rows_dw(du, h, HP, name=f"{tag}_up_dw")
    dh = _chunk_rows_mm(dg, W[pre + "_w_gate"][l], l, HP, add=dh_a, name=f"{tag}_gate_dx")
    dh = _chunk_rows_mm(du, W[pre + "_w_up"][l], l, HP, add=dh, name=f"{tag}_up_dx")
    return dh, dgam, dbet


def _mixer_fwd(h1, W, l, cosf, sins):
    tag = f"l{l}"
    proj = _mm(h1, W["w_in_p"][l], name=f"{tag}_in_proj")
    sv = {"h1": h1, "proj": proj}
    conv_w, conv_b = W["conv_w"][l], W["conv_b"][l][None]
    xc = _conv_fwd(proj, PC_XBC // BLOCK, conv_w, conv_b, name=f"{tag}_conv")
    dt_bias = _lane_pad(W["dt_bias"][l][None])
    (dt,) = _rowwise(_ssd_pre_fn, [(proj, BLOCK, PC_DT // BLOCK)], [dt_bias], [BLOCK], name=f"{tag}_ssd_dt", tile=272)
    xh = _heads(xc[:, :SSD_D], SSD_HEADS, SSD_HD)
    bm = _heads(xc[:, SSD_D:SSD_D + 128], SSD_GROUPS, SSD_STATE)
    cm = _heads(xc[:, SSD_D + 128:], SSD_GROUPS, SSD_STATE)
    dt8 = dt[:, :SSD_HEADS].T
    dtc, dtr = dt8[:, :, None], dt8[:, None, :]
    alog = jnp.broadcast_to(W["a_log"][l][:, None, None], (SSD_HEADS, 1, BLOCK))
    yh, prevs = _ssd_fwd(xh, bm, cm, dtc, dtr, alog, name=f"{tag}_ssd")
    y_raw = _unheads(yh)
    dskip = jnp.repeat(W["d_skip"][l], SSD_HD)[None]
    normg = W["ssd_norm_g"][l][None]
    post_rows = [y_raw, (xc, 256, 0), (proj, 256, PC_Z // 256)]
    (y_ssd,) = _rowwise(_ssd_post_fn, post_rows, [dskip, normg], [SSD_D], name=f"{tag}_ssd_post", tile=272,
                        ncol=SSD_GROUPS)
    sv.update(conv_w=conv_w, conv_b=conv_b, dt_bias=dt_bias, xh=xh, bm=bm, cm=cm, dtc=dtc, dtr=dtr, alog=alog,
              prevs=prevs, post_rows=post_rows, dskip=dskip, normg=normg)
    f_b = _lane_pad(W["fox_f_b"][l][None])
    cg, cgt = _fox_gate_fwd(proj, PC_FR // BLOCK, f_b, name=f"{tag}_fox_gate")
    fox_qkv = ((proj, PC_FQ // ATT_W), (proj, PC_FK // ATT_W), (proj, PC_FV // ATT_W))
    y_fox, lse_f = _attn_fwd(*fox_qkv, scale=FOX_HD ** -0.5, name=f"{tag}_fox_attn", bias=(cg, cgt))
    sv.update(f_b=f_b, cg=cg, cgt=cgt, fox_qkv=fox_qkv, y_fox=y_fox, lse_f=lse_f)
    gq, gkv = W["mla_q_norm_g"][l][None], W["mla_kv_norm_g"][l][None]
    norm_rows = [(proj, 256, PC_CQ // 256), (proj, BLOCK, PC_CKV // BLOCK)]
    qn, cn = _rowwise(_mla_norm_fn, norm_rows, [gq, gkv], [MLA_Q_LORA, MLA_KV_LORA], name=f"{tag}_mla_norm", tile=272,
                      out_dtypes=[BF16, BF16])
    qh = _mm(qn, W["mla_w_uq_p"][l], name=f"{tag}_mla_uq")
    kvh = _mm(cn, W["mla_w_ukv_p"][l], name=f"{tag}_mla_ukv")
    qr, kr = _rowwise(_rope_fn, [(qh, BLOCK, 2), (proj, BLOCK, PC_KR // BLOCK), cosf, sins], [], [BLOCK, BLOCK],
                      name=f"{tag}_rope", tile=272)
    mla_qkv = ((qh, 0), (kvh, 0), (kvh, 1))
    y_mla, lse_m = _attn_fwd(*mla_qkv, scale=(MLA_NOPE + MLA_ROPE) ** -0.5, name=f"{tag}_mla_attn",
                             rope=((qr, 0), (kr, 0)))
    sv.update(gq=gq, gkv=gkv, norm_rows=norm_rows, qn=qn, cn=cn, qr=qr, kr=kr, mla_qkv=mla_qkv, y_mla=y_mla, lse_m=lse_m)
    ycat = jnp.concatenate([y_ssd, y_fox, y_mla], axis=1).astype(BF16)
    mix = _chunk_rows_mm(ycat, W["w_out"][l], l, 256, name=f"{tag}_out_proj")
    (h2,) = _rowwise(_make_res_ln_fn(1.0), [h1, mix], [W["ln2_g"][l][None], W["ln2_b"][l][None]], [D_MODEL],
                     name=f"{tag}_ln2", tile=272)
    sv.update(mix=mix, ycat=ycat)
    return h2, sv


def _mixer_bwd(dh2, sv, W, l, cosf, sins, GB):
    tag = f"l{l}"
    G = {}
    proj = sv["proj"]
    ln2g, ln2b = W["ln2_g"][l][None], W["ln2_b"][l][None]
    (dh1_a, dmix), (dln2g, dln2b) = _rowwise_bwd(
        _make_res_ln_fn(1.0), [sv["h1"], sv["mix"]], [ln2g, ln2b], [dh2], name=f"{tag}_ln2_bwd", tile=272,
        grad_dtypes=[F32, BF16])
    G["ln2_g"], G["ln2_b"] = dln2g[0], dln2b[0]
    dycat = _chunk_rows_dx(dmix, W["w_out"][l], l, 256, name=f"{tag}_out_proj_dx")
    GB["w_out"] = _chunk_rows_dw(sv["ycat"], dmix, 256, name=f"{tag}_out_proj_dw")
    (dy_raw, dxs_a, dz), (ddskip, dnormg) = _rowwise_bwd(
        _ssd_post_fn, sv["post_rows"], [sv["dskip"], sv["normg"]], [dycat[:, :SSD_D]],
        name=f"{tag}_ssd_post_bwd", tile=272, ncol=SSD_GROUPS)
    G["ssd_norm_g"] = dnormg[0]
    G["d_skip"] = ddskip.reshape(SSD_HEADS, SSD_HD).sum(axis=1)
    dxh, dbm, dcm, ddtc, ddtr, dal = _ssd_bwd(sv["xh"], sv["bm"], sv["cm"], sv["dtc"], sv["dtr"], sv["alog"],
                                              sv["prevs"], _heads(dy_raw, SSD_HEADS, SSD_HD), name=f"{tag}_ssd_bwd")
    G["a_log"] = dal[:, 0, 0]
    dxc = jnp.concatenate([dxs_a + _unheads(dxh), _unheads(dbm), _unheads(dcm)], axis=1)
    dxbc, G["conv_w"], dconv_b = _conv_bwd(proj, PC_XBC // BLOCK, sv["conv_w"], sv["conv_b"], dxc,
                                           name=f"{tag}_conv_bwd")
    G["conv_b"] = dconv_b[0]
    ddt = _lane_pad((ddtc[:, :, 0] + ddtr[:, 0, :]).T)
    (ddt_raw,), (ddt_bias,) = _rowwise_bwd(_ssd_pre_fn, [(proj, BLOCK, PC_DT // BLOCK)], [sv["dt_bias"]], [ddt],
                                           name=f"{tag}_ssd_dt_bwd", tile=272)
    G["dt_bias"] = ddt_bias[0, :SSD_HEADS]
    dfq, dfk, dfv, dcg, dcgt = _attn_bwd(*sv["fox_qkv"], sv["y_fox"], sv["lse_f"], (dycat, SSD_D // ATT_W),
                                         scale=FOX_HD ** -0.5, name=f"{tag}_fox_attn_bwd", bias=(sv["cg"], sv["cgt"]))
    df_raw, dfb = _fox_gate_bwd(proj, PC_FR // BLOCK, sv["f_b"], dcg, dcgt, name=f"{tag}_fox_gate_bwd")
    G["fox_f_b"] = dfb[0, :FOX_HEADS]
    dqn_h, dkn_h, dv_h, dqr, dkr = _attn_bwd(
        *sv["mla_qkv"], sv["y_mla"], sv["lse_m"], (dycat, (SSD_D + FOX_D) // ATT_W),
        scale=(MLA_NOPE + MLA_ROPE) ** -0.5, name=f"{tag}_mla_attn_bwd", rope=((sv["qr"], 0), (sv["kr"], 0)))
    dq_rope, dk_rope = _rowwise(_rope_t_fn, [dqr, dkr, cosf, sins], [], [BLOCK, BLOCK], name=f"{tag}_rope_bwd",
                                tile=272)
    dqh = jnp.concatenate([dqn_h, dq_rope], axis=1).astype(BF16)
    dkvh = jnp.concatenate([dkn_h, dv_h], axis=1).astype(BF16)
    dqn = _mm(dqh, W["mla_w_uq_p"][l], tb=True, name=f"{tag}_mla_uq_dx")
    G["mla_w_uq_p"] = _mm(sv["qn"], dqh, ta=True, name=f"{tag}_mla_uq_dw")
    dcn = _mm(dkvh, W["mla_w_ukv_p"][l], tb=True, name=f"{tag}_mla_ukv_dx")
    G["mla_w_ukv_p"] = _mm(sv["cn"], dkvh, ta=True, name=f"{tag}_mla_ukv_dw")
    (dcq, dckv), (dgq, dgkv) = _rowwise_bwd(_mla_norm_fn, sv["norm_rows"], [sv["gq"], sv["gkv"]], [dqn, dcn],
                                            name=f"{tag}_mla_norm_bwd", tile=272)
    G["mla_q_norm_g"], G["mla_kv_norm_g"] = dgq[0], dgkv[0]
    dproj = jnp.concatenate([dz, dxbc, dfq, dfk, dfv, dcq, dckv, ddt_raw, df_raw, dk_rope], axis=1).astype(BF16)
    dh1 = _mm(dproj, W["w_in_p"][l], tb=True, add=dh1_a, name=f"{tag}_in_proj_dx")
    G["w_in_p"] = _mm(sv["h1"], dproj, ta=True, name=f"{tag}_in_proj_dw")
    return dh1, G


def _embed(x, meta):
    return jnp.concatenate([jnp.zeros((PAD_ROWS, D_MODEL), F32), meta, x], axis=0)


def _layer_fwd(h, W, l, cosf, sins):
    ln = lambda n: W[n][l][None]
    h1, s1 = _ffn_fwd(h, W, "ffn1", l, ln("ln1_g"), ln("ln1_b"), f"l{l}_ffn1")
    h2, sm = _mixer_fwd(h1, W, l, cosf, sins)
    h3, s2 = _ffn_fwd(h2, W, "ffn2", l, ln("ln3_g"), ln("ln3_b"), f"l{l}_ffn2")
    return h3, (s1, sm, s2)


def _layer_bwd(dh, saved, W, l, cosf, sins):
    ln = lambda n: W[n][l][None]
    s1, sm, s2 = saved
    G = {}
    dh, dg, db = _ffn_bwd(dh, s2, W, "ffn2", l, ln("ln3_g"), ln("ln3_b"), G, f"l{l}_ffn2")
    G["ln3_g"], G["ln3_b"] = dg[0], db[0]
    dh, Gm = _mixer_bwd(dh, sm, W, l, cosf, sins, G)
    G.update(Gm)
    dh, dg, db = _ffn_bwd(dh, s1, W, "ffn1", l, ln("ln1_g"), ln("ln1_b"), G, f"l{l}_ffn1")
    G["ln1_g"], G["ln1_b"] = dg[0], db[0]
    return dh, G


def _local_step(x, target, W):
    h = _embed(x, W["meta"])
    tgt = jnp.concatenate([jnp.zeros((BLOCK, D_MODEL), F32), target], axis=0)
    cosf, sins = _rope_tables()
    saved = []
    for l in range(DEPTH):
        h, sv = _layer_fwd(h, W, l, cosf, sins)
        saved.append(sv)
    dh, loss = _loss_head(h, tgt, name="loss_head")
    grads = [None] * DEPTH
    for l in reversed(range(DEPTH)):
        dh, grads[l] = _layer_bwd(dh, saved[l], W, l, cosf, sins)
    return loss, dh, grads


WEIGHTS = ['meta', 'ffn1_w_gate', 'ffn1_w_up', 'ffn1_w_down', 'ln1_g', 'ln1_b', 'w_in', 'conv_w', 'conv_b', 'dt_bias',
           'a_log', 'd_skip', 'ssd_norm_g', 'fox_f_b', 'mla_q_norm_g', 'mla_w_uq', 'mla_kv_norm_g', 'mla_w_ukv',
           'w_out', 'ln2_g', 'ln2_b', 'ffn2_w_gate', 'ffn2_w_up', 'ffn2_w_down', 'ln3_g', 'ln3_b']
SMALL = ["ln1_g", "ln1_b", "conv_b", "dt_bias", "a_log", "d_skip", "ssd_norm_g", "fox_f_b", "mla_q_norm_g",
         "mla_kv_norm_g", "ln2_g", "ln2_b", "ln3_g", "ln3_b"]
MATMUL_W = ["ffn1_w_gate", "ffn1_w_up", "ffn1_w_down", "w_in", "mla_w_uq", "mla_w_ukv", "w_out", "ffn2_w_gate",
            "ffn2_w_up", "ffn2_w_down"]
SMALL_ROWS = 312


def _pad_to(a, axis, size):
    pads = [(0, 0)] * a.ndim
    pads[axis] = (0, size - a.shape[axis])
    return jnp.pad(a, pads)


def _chip_cols(full, chip, width):
    return lax.dynamic_slice_in_dim(full, chip * width, width, axis=full.ndim - 1)


def kernel(x, meta, ffn1_w_gate, ffn1_w_up, ffn1_w_down, ln1_g, ln1_b, w_in, conv_w, conv_b, dt_bias, a_log, d_skip, ssd_norm_g, fox_f_b, mla_q_norm_g, mla_w_uq, mla_kv_norm_g, mla_w_ukv, w_out, ln2_g, ln2_b, ffn2_w_gate, ffn2_w_up, ffn2_w_down, ln3_g, ln3_b, loss_target, m_meta, m_ffn1_w_gate, m_ffn1_w_up, m_ffn1_w_down, m_ln1_g, m_ln1_b, m_w_in, m_conv_w, m_conv_b, m_dt_bias, m_a_log, m_d_skip, m_ssd_norm_g, m_fox_f_b, m_mla_q_norm_g, m_mla_w_uq, m_mla_kv_norm_g, m_mla_w_ukv, m_w_out, m_ln2_g, m_ln2_b, m_ffn2_w_gate, m_ffn2_w_up, m_ffn2_w_down, m_ln3_g, m_ln3_b, v_meta, v_ffn1_w_gate, v_ffn1_w_up, v_ffn1_w_down, v_ln1_g, v_ln1_b, v_w_in, v_conv_w, v_conv_b, v_dt_bias, v_a_log, v_d_skip, v_ssd_norm_g, v_fox_f_b, v_mla_q_norm_g, v_mla_w_uq, v_mla_kv_norm_g, v_mla_w_ukv, v_w_out, v_ln2_g, v_ln2_b, v_ffn2_w_gate, v_ffn2_w_up, v_ffn2_w_down, v_ln3_g, v_ln3_b):
    args = dict(locals())
    w = {n: args[n] for n in WEIGHTS}
    m = {n: args["m_" + n] for n in WEIGHTS}
    v = {n: args["v_" + n] for n in WEIGHTS}
    xcoord, ycoord, _ = _my_pos()
    chip = 2 * xcoord + ycoord

    tr = lambda a: jnp.swapaxes(a, 1, 2)

    def bf16_shard(n, zero=None):
        a = w[n] if zero is None else w[n] + zero
        if n.endswith("w_gate") or n.endswith("w_up"):
            a = _pad_to(tr(a), 1, HP)
        elif n.endswith("w_down"):
            a = _pad_to(a, 1, HP)
        elif n == "w_in":
            a = _pad_to(a, 2, IN_SHARD_P)
        return a.astype(BF16)

    land_shape = lambda s: jax.ShapeDtypeStruct((N_CHIPS,) + s.shape[1:], s.dtype)
    gather_l = lambda l: (lambda srcs, lands, ss, rs: _gather_copies(srcs, lands, l, ss, rs))
    tiny = _allgather_chips([w["meta"].reshape(2, N_META // 2, D_MODEL // N_CHIPS), w["conv_w"]])
    meta_full = jnp.concatenate([tiny[0][k].reshape(N_META, D_MODEL // N_CHIPS) for k in range(N_CHIPS)], axis=1)

    W = {n: [None] * DEPTH for n in MATMUL_W + ["w_in_p", "mla_w_uq_p", "mla_w_ukv_p"]}
    W["conv_w"] = jnp.concatenate([tiny[1][k] for k in range(N_CHIPS)], axis=-1)
    W["meta"] = meta_full
    for n in SMALL:
        W[n] = w[n]

    def use_gathered(l, names, lands):
        got = dict(zip(names, lands))
        cat = lambda n, cut=None: jnp.concatenate([got[n][k][..., :cut] for k in range(N_CHIPS)], axis=-1)
        for n in names:
            W[n][l] = got[n]
        if "w_in" in got:
            W["w_in_p"][l] = _pad_in_proj(cat("w_in", IN_SHARD))
            W["mla_w_uq_p"][l] = _regroup_uq(cat("mla_w_uq"))
            W["mla_w_ukv_p"][l] = _regroup_ukv(cat("mla_w_ukv"))

    def chunk_grads(G, names):
        def chunked(name, ungroup, width, pad):
            full = ungroup(G[name])
            return _pad_to(jnp.moveaxis(full.reshape(full.shape[0], N_CHIPS, width), 1, 0), 2, pad)
        special = {"w_in": ("w_in_p", _unpad_in_proj, IN_SHARD, IN_SHARD_P),
                   "mla_w_uq": ("mla_w_uq_p", _ungroup_uq, MLA_NOPE + MLA_ROPE, MLA_NOPE + MLA_ROPE),
                   "mla_w_ukv": ("mla_w_ukv_p", _ungroup_ukv, MLA_NOPE + MLA_V, MLA_NOPE + MLA_V)}
        return [chunked(*special[n]) if n in special else G[n] for n in names]

    def rs_start(G, names, tag):
        pairs = _rs_pair_sums(chunk_grads(G, names), names, tag)
        handle = _ici_start(_exchange_copies, [p[1] for p in pairs], _exchange_land_shapes(pairs),
                            name=f"rs_exchange_{tag}_start")
        return pairs, handle

    def swap_start(G, names, tag):
        gs = chunk_grads(G, names)
        return _ici_start(_swap_copies, gs, _swap_land_shapes(gs), name=f"rs_swap_{tag}_start")

    def exchange_start(swap_handle, names, tag, after):
        gs, r1 = _ici_wait(_swap_copies, *swap_handle[:4], after, name=f"rs_swap_{tag}_wait")
        pairs = _rs_add_pairs(gs, r1, names, tag)
        handle = _ici_start(_exchange_copies, [p[1] for p in pairs], _exchange_land_shapes(pairs),
                            name=f"rs_exchange_{tag}_start")
        return pairs, handle

    def rs_end(pairs, handle, names, tag, after):
        _, r2 = _ici_wait(_exchange_copies, *handle[:4], after, name=f"rs_exchange_{tag}_wait")
        return dict(zip(names, _rs_finish(pairs, r2, names, tag)))

    na = 3
    first, rest = MATMUL_W[:na], MATMUL_W[na:]
    shards = [bf16_shard(n) for n in first]
    g_send, g_recv, g_srcs, g_lands, token = _ici_start(gather_l(0), shards, [land_shape(s) for s in shards],
                                                        name="gather_ici_l0_ffn1_start", after=[tiny[0]])
    shards_rest = [bf16_shard(n, token[0, 0]) for n in rest]
    shards, lands = _ici_wait(gather_l(0), g_send, g_recv, g_srcs, g_lands, shards_rest[0],
                              name="gather_ici_l0_ffn1_wait")
    shards = shards + shards_rest
    land_shapes = [land_shape(s) for s in shards]
    got = _gather_d2d(shards[:na], lands, 0, "l0_ffn1")
    use_gathered(0, first, got)
    g_send, g_recv, g_srcs, g_lands, token = _ici_start(gather_l(0), shards[na:], land_shapes[na:],
                                                        name="gather_ici_l0_rest_start", after=[got[0]])
    cosf, sins = _rope_tables()
    ln = lambda n, l: W[n][l][None]
    h = _embed(x[0] + token[0, 0], meta_full)
    h1, s1 = _ffn_fwd(h, W, "ffn1", 0, ln("ln1_g", 0), ln("ln1_b", 0), "l0_ffn1")
    rest_shards, lands = _ici_wait(gather_l(0), g_send, g_recv, g_srcs, g_lands, h1, name="gather_ici_l0_rest_wait")
    shards = shards[:na] + rest_shards
    got = _gather_d2d(shards[na:], lands, 0, "l0_rest")
    use_gathered(0, rest, got)
    g_send, g_recv, g_srcs, g_lands, token = _ici_start(gather_l(1), shards, land_shapes, name="gather_ici_l1_start",
                                                        after=[got[0]])
    h2, sm = _mixer_fwd(h1 + token[0, 0], W, 0, cosf, sins)
    h, s2 = _ffn_fwd(h2, W, "ffn2", 0, ln("ln3_g", 0), ln("ln3_b", 0), "l0_ffn2")
    saved0 = (s1, sm, s2)
    shards, lands = _ici_wait(gather_l(1), g_send, g_recv, g_srcs, g_lands, h, name="gather_ici_l1_wait")
    use_gathered(1, MATMUL_W, _gather_d2d(shards, lands, 1, "l1"))
    h, saved1 = _layer_fwd(h, W, 1, cosf, sins)
    tgt = jnp.concatenate([jnp.zeros((BLOCK, D_MODEL), F32), loss_target[0]], axis=0)
    dh, loss = _loss_head(h, tgt, name="loss_head")
    G = [None] * DEPTH
    dh, G[1] = _layer_bwd(dh, saved1, W, 1, cosf, sins)
    ffn2_w, mix_w, ffn1_w = MATMUL_W[7:], MATMUL_W[3:7], MATMUL_W[:3]
    sw_l1 = swap_start(G[1], MATMUL_W, "l1")
    G0 = {}
    dh, dg, db = _ffn_bwd(dh + sw_l1[4][0, 0], s2, W, "ffn2", 0, ln("ln3_g", 0), ln("ln3_b", 0), G0, "l0_ffn2")
    G0["ln3_g"], G0["ln3_b"] = dg[0], db[0]
    pairs_l1, x_l1 = exchange_start(sw_l1, MATMUL_W, "l1", dh)
    sw_a = swap_start(G0, ffn2_w, "l0_ffn2")
    dh, Gm = _mixer_bwd(dh + (x_l1[4][0, 0] + sw_a[4][0, 0]), sm, W, 0, cosf, sins, G0)
    G0.update(Gm)
    pairs_a, x_a = exchange_start(sw_a, ffn2_w, "l0_ffn2", dh)
    reduced1 = rs_end(pairs_l1, x_l1, MATMUL_W, "l1", dh)
    pairs_b, x_b = rs_start(G0, mix_w, "l0_mix")
    dh0, dg, db = _ffn_bwd(dh + (x_a[4][0, 0] + x_b[4][0, 0]), s1, W, "ffn1", 0, ln("ln1_g", 0), ln("ln1_b", 0), G0,
                           "l0_ffn1")
    G0["ln1_g"], G0["ln1_b"] = dg[0], db[0]
    G[0] = G0
    reduced0 = rs_end(pairs_a, x_a, ffn2_w, "l0_ffn2", dh0)
    reduced0.update(rs_end(pairs_b, x_b, mix_w, "l0_mix", dh0))

    small_parts = [jnp.stack([G[l][n] for l in range(DEPTH)]).reshape(-1) for n in SMALL]
    small_parts += [jnp.stack([G[l]["conv_w"] for l in range(DEPTH)]).reshape(-1), dh0[PAD_ROWS:BLOCK].reshape(-1),
                    loss[0, :1]]
    flat = jnp.concatenate(small_parts)
    flat = jnp.pad(flat, (0, SMALL_ROWS * BLOCK - flat.shape[0]))
    red2d = _allreduce_small(flat.reshape(SMALL_ROWS, BLOCK))
    red = red2d.reshape(-1)

    pairs_c = _rs_pair_sums(chunk_grads(G0, ffn1_w), ffn1_w, "l0_ffn1")
    x_c = _ici_start(_exchange_copies, [p[1] for p in pairs_c], _exchange_land_shapes(pairs_c),
                     name="rs_exchange_l0_ffn1_start", after=[red2d])
    grads, off = {}, 0
    for n in SMALL:
        size = int(np.prod(w[n].shape))
        grads[n] = red[off:off + size].reshape(w[n].shape)
        off += size
    conv_full = red[off:off + DEPTH * SSD_CONV * 768].reshape(DEPTH, SSD_CONV, 768)
    off += DEPTH * SSD_CONV * 768
    dmeta_full = red[off:off + N_META * D_MODEL].reshape(N_META, D_MODEL)
    off += N_META * D_MODEL
    loss_out = red[off]
    grads["conv_w"] = _chip_cols(conv_full, chip, 768 // N_CHIPS)
    grads["meta"] = _chip_cols(dmeta_full, chip, D_MODEL // N_CHIPS)

    delta, new_m, new_v = {}, {}, {}

    def adamw_matmul_weights(names, after):
        for n in names:
            gs = [reduced0[n], reduced1[n]]
            if n.endswith("w_gate") or n.endswith("w_up"):
                res = _adamw(tr(w[n]), gs, tr(m[n]), tr(v[n]), name=f"adamw_{n}", after=after)
                grads[n], delta[n], new_m[n], new_v[n] = [tr(r) for r in res]
            else:
                grads[n], delta[n], new_m[n], new_v[n] = _adamw(w[n], gs, m[n], v[n], name=f"adamw_{n}", after=after)

    adamw_matmul_weights(ffn2_w + mix_w, [x_c[4]])
    rest = [n for n in WEIGHTS if n not in MATMUL_W]

    def pack_small(d):
        f = jnp.concatenate([d[n].reshape(-1) for n in rest])
        tot = -(-f.shape[0] // (8 * BLOCK)) * 8 * BLOCK
        return jnp.pad(f, (0, tot - f.shape[0])).reshape(-1, BLOCK)

    _, d2, m2, v2 = _adamw(pack_small(w), [pack_small(grads)], pack_small(m), pack_small(v), name="adamw_small",
                           after=[x_c[4]])
    reduced0.update(rs_end(pairs_c, x_c, ffn1_w, "l0_ffn1", d2))
    adamw_matmul_weights(ffn1_w, [])
    off = 0
    for n in rest:
        size = int(np.prod(w[n].shape))
        for dst, src in ((delta, d2), (new_m, m2), (new_v, v2)):
            dst[n] = src.reshape(-1)[off:off + size].reshape(w[n].shape)
        off += size

    grad_x = dh0[BLOCK:][None]
    return (loss_out, grad_x, *[grads[n] for n in WEIGHTS], *[delta[n] for n in WEIGHTS],
            *[new_m[n] for n in WEIGHTS], *[new_v[n] for n in WEIGHTS])
```

```python
import functools

import numpy as np
import jax
import jax.numpy as jnp
from jax import lax
from jax.experimental import pallas as pl
from jax.experimental.pallas import tpu as pltpu

F32 = jnp.float32
BF16 = jnp.bfloat16
MESH = pl.DeviceIdType.MESH

D_MODEL = 1024
SEQ = 2048
N_META = 16
BLOCK = 128
PAD_ROWS = 112
LP = PAD_ROWS + N_META + SEQ
N_CHUNK = LP // BLOCK
DEPTH = 2
D_FF = 2816
N_CHIPS = 4
FF_SHARD = D_FF // N_CHIPS
HP = 768
FP = N_CHIPS * HP
SSD_HEADS, SSD_HD, SSD_D, SSD_GROUPS, SSD_STATE, SSD_CONV = 8, 64, 512, 2, 64, 4
FOX_HEADS, FOX_HD, FOX_D = 4, 64, 256
MLA_HEADS, MLA_Q_LORA, MLA_KV_LORA, MLA_NOPE, MLA_ROPE, MLA_V, MLA_D = 4, 256, 128, 64, 32, 64, 256
ROPE_HALF = MLA_ROPE // 2
ROPE_THETA = 10000.0
N_IN = 2476
IN_SHARD = N_IN // N_CHIPS
IN_SHARD_P = 640
ALPHA = (2 * DEPTH) ** 0.25
EPS = 1e-5
ADAM_LR, ADAM_B1, ADAM_B2, ADAM_EPS, ADAM_WD, ADAM_STEP = 0.001, 0.9, 0.999, 1e-08, 0.01, 10
NEG = -1e30
TM = 544

VMEM_LIMIT_BYTES = 56 * 1024 * 1024

PC_Z, PC_XBC, PC_FQ, PC_FK, PC_FV, PC_CQ, PC_CKV, PC_DT, PC_FR, PC_KR, PC_END = (
    0, 512, 1280, 1536, 1792, 2048, 2304, 2432, 2560, 2688, 2816)
OC_Z, OC_XBC, OC_DT, OC_FQ, OC_FK, OC_FV, OC_FR, OC_CQ, OC_CKV, OC_KR = (
    0, 512, 1280, 1288, 1544, 1800, 2056, 2060, 2316, 2444)


def _cparams(sem=None):
    return pltpu.CompilerParams(dimension_semantics=sem, vmem_limit_bytes=VMEM_LIMIT_BYTES)


def _tile(n, cap, mult):
    best = None
    for t in range(mult, min(n, cap) + 1, mult):
        if n % t == 0:
            best = t
    return best if best is not None else n


def _bs(shape, fn):
    return pl.BlockSpec(shape, fn)


ANY = pl.BlockSpec(memory_space=pl.ANY)


def _dims(ca, cb):
    return (((ca,), (cb,)), ((), ()))


def _raw_bdot(a, b, ca, cb):
    return lax.dot_general(a.astype(BF16), b.astype(BF16), _dims(ca, cb), preferred_element_type=F32)


def _mm_core(a, b, *, a_spec, b_spec, o_spec, grid, out_shape, ca, cb, name, add=None):
    nk = grid[2]
    has_add = add is not None
    acc_shape = tuple(d for d in o_spec.block_shape if d is not None)

    def body(*refs):
        a_ref, b_ref = refs[0], refs[1]
        add_ref = refs[2] if has_add else None
        o_ref, acc_ref = refs[-2], refs[-1]
        k = pl.program_id(2)

        @pl.when(k == 0)
        def _():
            acc_ref[...] = jnp.zeros_like(acc_ref)

        acc_ref[...] += _raw_bdot(a_ref[...], b_ref[...], ca, cb)

        @pl.when(k == nk - 1)
        def _():
            r = acc_ref[...]
            if has_add:
                r = r + add_ref[...]
            o_ref[...] = r

    ins = [a, b] + ([add] if has_add else [])
    in_specs = [a_spec, b_spec] + ([o_spec] if has_add else [])
    return pl.pallas_call(
        body, name=name, grid=grid, in_specs=in_specs, out_specs=o_spec,
        out_shape=jax.ShapeDtypeStruct(out_shape, F32), scratch_shapes=[pltpu.VMEM(acc_shape, F32)],
        compiler_params=_cparams(("parallel", "parallel", "arbitrary")),
    )(*ins)


MM_VMEM_BUDGET = 40 * 1024 * 1024


def _divisors(n, mult):
    return [t for t in range(mult, n + 1, mult) if n % t == 0] or [n]


def _pick_tiles(M, N, K, a_bytes, b_bytes, ta, has_add):
    best = None
    for tm in _divisors(M, 128 if ta else 16):
        for tn in _divisors(N, 128):
            vmem = 2 * tm * K * a_bytes + 2 * K * tn * b_bytes + (3 + 2 * int(has_add)) * tm * tn * 4
            if vmem <= MM_VMEM_BUDGET:
                key = ((M // tm) * (N // tn), -tn)
                if best is None or key < best[0]:
                    best = (key, tm, tn)
    assert best is not None, (M, N, K)
    return best[1], best[2], K


def _mm(a, b, *, ta=False, tb=False, add=None, name):
    if ta:
        K, M = a.shape
    else:
        M, K = a.shape
    if tb:
        N, Kb = b.shape
    else:
        Kb, N = b.shape
    assert K == Kb, (a.shape, b.shape, ta, tb)
    tm, tn, tk = _pick_tiles(M, N, K, a.dtype.itemsize, b.dtype.itemsize, ta, add is not None)
    a_spec = _bs((tk, tm), lambda i, j, k: (k, i)) if ta else _bs((tm, tk), lambda i, j, k: (i, k))
    b_spec = _bs((tn, tk), lambda i, j, k: (j, k)) if tb else _bs((tk, tn), lambda i, j, k: (k, j))
    return _mm_core(a, b, a_spec=a_spec, b_spec=b_spec, o_spec=_bs((tm, tn), lambda i, j, k: (i, j)),
                    grid=(M // tm, N // tn, K // tk), out_shape=(M, N), ca=0 if ta else 1, cb=1 if tb else 0,
                    name=name, add=add)


def _row_entry(r, ncol):
    if isinstance(r, tuple):
        return r
    return r, r.shape[1] // ncol, 0


def _rowwise(fn, rows, pars, out_cols, *, name, tile, ncol=1, out_dtypes=None):
    rows = [_row_entry(r, ncol) for r in rows]
    L = rows[0][0].shape[0]
    nr, npar = len(rows), len(pars)
    in_specs = [_bs((tile, w), lambda g, i, o=o: (i, o + g)) for _, w, o in rows]
    in_specs += [_bs((p.shape[0], p.shape[1] // ncol), lambda g, i: (0, g)) for p in pars]
    out_specs = [_bs((tile, c // ncol), lambda g, i: (i, g)) for c in out_cols]

    def body(*refs):
        ins, outs = refs[:nr + npar], refs[nr + npar:]
        row0 = pl.program_id(1) * tile
        res = fn(row0, *[r[...] for r in ins])
        for o, v in zip(outs, res):
            o[...] = v.astype(o.dtype)

    return pl.pallas_call(
        body, name=name, grid=(ncol, L // tile), in_specs=in_specs, out_specs=out_specs,
        out_shape=[jax.ShapeDtypeStruct((L, c), d) for c, d in zip(out_cols, out_dtypes or [F32] * len(out_cols))],
        compiler_params=_cparams(("parallel", "parallel")),
    )(*[r[0] for r in rows], *pars)


def _rowwise_bwd(fn, rows, pars, douts, *, name, tile, ncol=1, row_grad=None, grad_dtypes=None):
    rows = [_row_entry(r, ncol) for r in rows]
    L = rows[0][0].shape[0]
    nr, npar, nd = len(rows), len(pars), len(douts)
    row_grad = [True] * nr if row_grad is None else row_grad
    in_specs = [_bs((tile, w), lambda g, i, o=o: (i, o + g)) for _, w, o in rows]
    in_specs += [_bs((p.shape[0], p.shape[1] // ncol), lambda g, i: (0, g)) for p in pars]
    in_specs += [_bs((tile, d.shape[1] // ncol), lambda g, i: (i, g)) for d in douts]
    g_widths = [w * ncol for (_, w, _), f in zip(rows, row_grad) if f]
    out_specs = [_bs((tile, w // ncol), lambda g, i: (i, g)) for w in g_widths]
    out_specs += [_bs((p.shape[0], p.shape[1] // ncol), lambda g, i: (0, g)) for p in pars]
    out_shape = [jax.ShapeDtypeStruct((L, w), d) for w, d in zip(g_widths, grad_dtypes or [F32] * len(g_widths))]
    out_shape += [jax.ShapeDtypeStruct(p.shape, F32) for p in pars]

    def body(*refs):
        ins = refs[:nr + npar]
        dos = refs[nr + npar:nr + npar + nd]
        outs = refs[nr + npar + nd:]
        i = pl.program_id(1)
        row0 = i * tile
        _, vjp = jax.vjp(lambda *a: tuple(fn(row0, *a)), *[r[...] for r in ins])
        grads = vjp(tuple(d[...].astype(F32) for d in dos))
        o = 0
        for j in range(nr):
            if row_grad[j]:
                outs[o][...] = grads[j].astype(outs[o].dtype)
                o += 1
        for j in range(npar):
            g, ref = grads[nr + j], outs[o + j]

            @pl.when(i == 0)
            def _(g=g, ref=ref):
                ref[...] = g

            @pl.when(i > 0)
            def _(g=g, ref=ref):
                ref[...] += g

    res = pl.pallas_call(
        body, name=name, grid=(ncol, L // tile), in_specs=in_specs, out_specs=out_specs, out_shape=out_shape,
        compiler_params=_cparams(("parallel", "arbitrary")),
    )(*[r[0] for r in rows], *pars, *douts)
    return res[:len(g_widths)], res[len(g_widths):]


def _row_ids(row0, shape):
    return row0 + lax.broadcasted_iota(jnp.int32, shape, 0)


def _sigmoid(x):
    return 1.0 / (1.0 + jnp.exp(-x))


@jax.custom_vjp
def _softplus(x):
    return jnp.maximum(x, 0.0) + jnp.log(1.0 + jnp.exp(-jnp.abs(x)))


def _softplus_fwd(x):
    return _softplus(x), x


def _softplus_bwd(x, g):
    return (g * _sigmoid(x),)


_softplus.defvjp(_softplus_fwd, _softplus_bwd)


def _silu(x):
    return x * _sigmoid(x)


def _make_res_ln_fn(scale):
    def fn(row0, h, o, gam, bet):
        pre = ALPHA * h + scale * o
        mu = jnp.mean(pre, axis=-1, keepdims=True)
        xc = pre - mu
        var = jnp.mean(xc * xc, axis=-1, keepdims=True)
        return (xc * lax.rsqrt(var + EPS) * gam + bet,)
    return fn


def _ssd_pre_fn(row0, raw, bias):
    dt = _softplus(raw + bias)
    return (jnp.where(_row_ids(row0, raw.shape) >= PAD_ROWS, dt, 0.0),)


def _ssd_post_fn(row0, y, xs, z, dskip, normg):
    v = (y + dskip * xs) * _silu(z)
    v = v * lax.rsqrt(jnp.mean(v * v, axis=-1, keepdims=True) + EPS)
    return (v * normg,)


def _mla_norm_fn(row0, cq, ckv, gq, gkv):
    qn = cq * lax.rsqrt(jnp.mean(cq * cq, axis=-1, keepdims=True) + EPS) * gq
    cn = ckv * lax.rsqrt(jnp.mean(ckv * ckv, axis=-1, keepdims=True) + EPS) * gkv
    return qn, cn


def _rope_fn(row0, q, k, cosf, sins):
    return (q * cosf + pltpu.roll(q, 64, 1) * sins, k * cosf + pltpu.roll(k, 64, 1) * sins)


def _rope_t_fn(row0, gq, gk, cosf, sins):
    return (gq * cosf + pltpu.roll(gq * sins, 64, 1), gk * cosf + pltpu.roll(gk * sins, 64, 1))


def _conv_fwd(x, x_off, w, b, *, name):
    C = w.shape[1]

    def body(x_ref, w_ref, b_ref, o_ref):
        rows = lax.broadcasted_iota(jnp.int32, (LP, BLOCK), 0)
        xv = jnp.where(rows >= PAD_ROWS, x_ref[...], 0.0)
        acc = b_ref[...] + w_ref[3:4, :] * xv
        for k in range(SSD_CONV - 1):
            acc = acc + w_ref[k:k + 1, :] * pltpu.roll(xv, SSD_CONV - 1 - k, 0)
        o_ref[...] = _silu(acc)

    return pl.pallas_call(
        body, name=name, grid=(C // BLOCK,),
        in_specs=[_bs((LP, BLOCK), lambda j: (0, j + x_off)), _bs((SSD_CONV, BLOCK), lambda j: (0, j)),
                  _bs((1, BLOCK), lambda j: (0, j))],
        out_specs=_bs((LP, BLOCK), lambda j: (0, j)),
        out_shape=jax.ShapeDtypeStruct((LP, C), F32), compiler_params=_cparams(("parallel",)),
    )(x, w, b)


def _conv_bwd(x, x_off, w, b, dout, *, name):
    C = w.shape[1]

    def body(x_ref, w_ref, b_ref, do_ref, dx_ref, dw_ref, db_ref):
        rows = lax.broadcasted_iota(jnp.int32, (LP, BLOCK), 0)
        real = rows >= PAD_ROWS
        xv = jnp.where(real, x_ref[...], 0.0)
        shifted = [pltpu.roll(xv, SSD_CONV - 1 - k, 0) for k in range(SSD_CONV - 1)] + [xv]
        acc = b_ref[...]
        for k in range(SSD_CONV):
            acc = acc + w_ref[k:k + 1, :] * shifted[k]
        sig = _sigmoid(acc)
        dacc = jnp.where(real, do_ref[...] * (sig * (1.0 + acc * (1.0 - sig))), 0.0)
        db_ref[...] = jnp.sum(dacc, axis=0, keepdims=True)
        dx = w_ref[3:4, :] * dacc
        for k in range(SSD_CONV):
            dw_ref[k:k + 1, :] = jnp.sum(dacc * shifted[k], axis=0, keepdims=True)
            if k < SSD_CONV - 1:
                dx = dx + w_ref[k:k + 1, :] * pltpu.roll(dacc, LP - (SSD_CONV - 1 - k), 0)
        dx_ref[...] = jnp.where(real, dx, 0.0)

    return pl.pallas_call(
        body, name=name, grid=(C // BLOCK,),
        in_specs=[_bs((LP, BLOCK), lambda j: (0, j + x_off)), _bs((SSD_CONV, BLOCK), lambda j: (0, j)),
                  _bs((1, BLOCK), lambda j: (0, j)), _bs((LP, BLOCK), lambda j: (0, j))],
        out_specs=[_bs((LP, BLOCK), lambda j: (0, j)), _bs((SSD_CONV, BLOCK), lambda j: (0, j)),
                   _bs((1, BLOCK), lambda j: (0, j))],
        out_shape=[jax.ShapeDtypeStruct((LP, C), F32), jax.ShapeDtypeStruct((SSD_CONV, C), F32),
                   jax.ShapeDtypeStruct((1, C), F32)],
        compiler_params=_cparams(("parallel",)),
    )(x, w, b, dout)


_BDIMS = {"nn": (((2,), (1,)), ((0,), (0,))), "nt": (((2,), (2,)), ((0,), (0,))), "tn": (((1,), (1,)), ((0,), (0,)))}


def _raw_bdot3(a, b, mode):
    return lax.dot_general(a.astype(BF16), b.astype(BF16), _BDIMS[mode], preferred_element_type=F32)


@functools.partial(jax.custom_vjp, nondiff_argnums=(2,))
def _bdot3(a, b, mode):
    return _raw_bdot3(a, b, mode)


def _bdot3_fwd(a, b, mode):
    return _raw_bdot3(a, b, mode), (a, b)


def _bdot3_bwd(mode, res, g):
    a, b = res
    if mode == "nn":
        return _raw_bdot3(g, b, "nt"), _raw_bdot3(a, g, "tn")
    if mode == "nt":
        return _raw_bdot3(g, b, "nn"), _raw_bdot3(g, a, "tn")
    return _raw_bdot3(b, g, "nt"), _raw_bdot3(a, g, "nn")


_bdot3.defvjp(_bdot3_fwd, _bdot3_bwd)


def _ssd_chunk(x, bm, cm, dtc, dtr, alog, prev):
    rep = SSD_HEADS // SSD_GROUPS
    per_head = lambda t: jnp.broadcast_to(t[:, None], (SSD_GROUPS, rep) + t.shape[1:]).reshape((SSD_HEADS,) + t.shape[1:])
    bm, cm = per_head(bm), per_head(cm)
    lane = lax.broadcasted_iota(jnp.int32, alog.shape, 2)
    a_neg = -jnp.exp(jnp.sum(jnp.where(lane == 0, alog, 0.0), axis=2, keepdims=True))
    ac_in = dtc * a_neg
    ar_in = dtr * a_neg
    li = lax.broadcasted_iota(jnp.int32, (1, BLOCK, BLOCK), 1)
    si = lax.broadcasted_iota(jnp.int32, (1, BLOCK, BLOCK), 2)
    causal = li >= si
    acum_c = jnp.sum(jnp.where(causal, ar_in, 0.0), axis=2, keepdims=True)
    acum_r = jnp.sum(jnp.where(li <= si, ac_in, 0.0), axis=1, keepdims=True)
    total = jnp.sum(ar_in, axis=2, keepdims=True)
    seg = jnp.exp(jnp.where(causal, acum_c - acum_r, NEG))
    xdt = x * dtc
    cb = _bdot3(cm, bm, "nt")
    y = _bdot3(cb * seg, xdt, "nn") + _bdot3(cm, prev, "nt") * jnp.exp(acum_c)
    st = _bdot3(xdt, bm * jnp.exp(total - acum_c), "tn")
    return y, prev * jnp.exp(total) + st


def _ssd_specs(rev):
    ci = (lambda c: N_CHUNK - 1 - c) if rev else (lambda c: c)
    x_spec = _bs((SSD_HEADS, BLOCK, SSD_HD), lambda c: (0, ci(c), 0))
    g_spec = _bs((SSD_GROUPS, BLOCK, SSD_STATE), lambda c: (0, ci(c), 0))
    dtc_spec = _bs((SSD_HEADS, BLOCK, 1), lambda c: (0, ci(c), 0))
    dtr_spec = _bs((SSD_HEADS, 1, BLOCK), lambda c: (0, 0, ci(c)))
    al_spec = _bs((SSD_HEADS, 1, BLOCK), lambda c: (0, 0, 0))
    st_spec = _bs((None, SSD_HEADS, SSD_HD, SSD_STATE), lambda c: (ci(c), 0, 0, 0))
    return x_spec, g_spec, dtc_spec, dtr_spec, al_spec, st_spec


def _ssd_fwd(x, bm, cm, dtc, dtr, alog, *, name):
    x_spec, g_spec, dtc_spec, dtr_spec, al_spec, st_spec = _ssd_specs(False)

    def body(x_ref, b_ref, c_ref, dtc_ref, dtr_ref, al_ref, y_ref, prev_ref, state):
        @pl.when(pl.program_id(0) == 0)
        def _():
            state[...] = jnp.zeros_like(state)

        prev = state[...]
        prev_ref[...] = prev
        y, new = _ssd_chunk(x_ref[...], b_ref[...], c_ref[...], dtc_ref[...], dtr_ref[...], al_ref[...], prev)
        y_ref[...] = y
        state[...] = new

    return pl.pallas_call(
        body, name=name, grid=(N_CHUNK,),
        in_specs=[x_spec, g_spec, g_spec, dtc_spec, dtr_spec, al_spec], out_specs=[x_spec, st_spec],
        out_shape=[jax.ShapeDtypeStruct((SSD_HEADS, LP, SSD_HD), F32),
                   jax.ShapeDtypeStruct((N_CHUNK, SSD_HEADS, SSD_HD, SSD_STATE), F32)],
        scratch_shapes=[pltpu.VMEM((SSD_HEADS, SSD_HD, SSD_STATE), F32)],
        compiler_params=_cparams(("arbitrary",)),
    )(x, bm, cm, dtc, dtr, alog)


def _ssd_bwd(x, bm, cm, dtc, dtr, alog, prevs, dy, *, name):
    x_spec, g_spec, dtc_spec, dtr_spec, al_spec, st_spec = _ssd_specs(True)

    def body(x_ref, b_ref, c_ref, dtc_ref, dtr_ref, al_ref, prev_ref, dy_ref,
             dx_ref, db_ref, dc_ref, ddtc_ref, ddtr_ref, dal_ref, dstate):
        c = pl.program_id(0)

        @pl.when(c == 0)
        def _():
            dstate[...] = jnp.zeros_like(dstate)

        _, vjp = jax.vjp(_ssd_chunk, x_ref[...], b_ref[...], c_ref[...], dtc_ref[...], dtr_ref[...], al_ref[...],
                         prev_ref[...])
        dx, db, dc, ddtc, ddtr, dal, dprev = vjp((dy_ref[...], dstate[...]))
        dx_ref[...] = dx
        db_ref[...] = db
        dc_ref[...] = dc
        ddtc_ref[...] = ddtc
        ddtr_ref[...] = ddtr
        dstate[...] = dprev

        @pl.when(c == 0)
        def _():
            dal_ref[...] = dal

        @pl.when(c > 0)
        def _():
            dal_ref[...] += dal

    hs = jax.ShapeDtypeStruct((SSD_HEADS, LP, SSD_HD), F32)
    gs = jax.ShapeDtypeStruct((SSD_GROUPS, LP, SSD_STATE), F32)
    return pl.pallas_call(
        body, name=name, grid=(N_CHUNK,),
        in_specs=[x_spec, g_spec, g_spec, dtc_spec, dtr_spec, al_spec, st_spec, x_spec],
        out_specs=[x_spec, g_spec, g_spec, dtc_spec, dtr_spec, al_spec],
        out_shape=[hs, gs, gs, jax.ShapeDtypeStruct((SSD_HEADS, LP, 1), F32),
                   jax.ShapeDtypeStruct((SSD_HEADS, 1, LP), F32), jax.ShapeDtypeStruct((SSD_HEADS, 1, BLOCK), F32)],
        scratch_shapes=[pltpu.VMEM((SSD_HEADS, SSD_HD, SSD_STATE), F32)],
        compiler_params=_cparams(("arbitrary",)),
    )(x, bm, cm, dtc, dtr, alog, prevs, dy)


def _tri_dot(tri, v):
    hi = v.astype(BF16)
    r1 = v - hi.astype(F32)
    mid = r1.astype(BF16)
    lo = (r1 - mid.astype(F32)).astype(BF16)
    t = tri.astype(BF16)
    d = lambda p: lax.dot_general(t, p, _dims(1, 0), preferred_element_type=F32)
    return d(hi) + d(mid) + d(lo)


def _fox_gate_fwd(raw, raw_blk, bias, *, name):
    def body(raw_ref, b_ref, c_ref, ct_ref, carry):
        j = pl.program_id(0)

        @pl.when(j == 0)
        def _():
            carry[...] = jnp.zeros_like(carry)

        rows = j * BLOCK + lax.broadcasted_iota(jnp.int32, (BLOCK, BLOCK), 0)
        lf = jnp.where(rows >= PAD_ROWS, -_softplus(-(raw_ref[...] + b_ref[...])), 0.0)
        li = lax.broadcasted_iota(jnp.int32, (BLOCK, BLOCK), 0)
        si = lax.broadcasted_iota(jnp.int32, (BLOCK, BLOCK), 1)
        cv = _tri_dot(jnp.where(li >= si, 1.0, 0.0), lf) + carry[...]
        c_ref[...] = cv
        ct_ref[...] = cv.T
        carry[...] += jnp.sum(lf, axis=0, keepdims=True)

    return pl.pallas_call(
        body, name=name, grid=(N_CHUNK,),
        in_specs=[_bs((BLOCK, BLOCK), lambda j: (j, raw_blk)), _bs((1, BLOCK), lambda j: (0, 0))],
        out_specs=[_bs((BLOCK, BLOCK), lambda j: (j, 0)), _bs((BLOCK, BLOCK), lambda j: (0, j))],
        out_shape=[jax.ShapeDtypeStruct((LP, BLOCK), F32), jax.ShapeDtypeStruct((BLOCK, LP), F32)],
        scratch_shapes=[pltpu.VMEM((1, BLOCK), F32)], compiler_params=_cparams(("arbitrary",)),
    )(raw, bias)


def _fox_gate_bwd(raw, raw_blk, bias, dc, dct, *, name):
    rj = lambda j: N_CHUNK - 1 - j

    def body(raw_ref, b_ref, dc_ref, dct_ref, draw_ref, db_ref, carry):
        j = pl.program_id(0)

        @pl.when(j == 0)
        def _():
            carry[...] = jnp.zeros_like(carry)

        rows = (N_CHUNK - 1 - j) * BLOCK + lax.broadcasted_iota(jnp.int32, (BLOCK, BLOCK), 0)
        li = lax.broadcasted_iota(jnp.int32, (BLOCK, BLOCK), 0)
        si = lax.broadcasted_iota(jnp.int32, (BLOCK, BLOCK), 1)
        dcv = dc_ref[...] + dct_ref[...].T
        dlf = _tri_dot(jnp.where(li <= si, 1.0, 0.0), dcv) + carry[...]
        carry[...] += jnp.sum(dcv, axis=0, keepdims=True)
        draw = jnp.where(rows >= PAD_ROWS, dlf * (1.0 - _sigmoid(raw_ref[...] + b_ref[...])), 0.0)
        draw_ref[...] = draw
        dsum = jnp.sum(draw, axis=0, keepdims=True)

        @pl.when(j == 0)
        def _():
            db_ref[...] = dsum

        @pl.when(j > 0)
        def _():
            db_ref[...] += dsum

    return pl.pallas_call(
        body, name=name, grid=(N_CHUNK,),
        in_specs=[_bs((BLOCK, BLOCK), lambda j: (rj(j), raw_blk)), _bs((1, BLOCK), lambda j: (0, 0)),
                  _bs((BLOCK, BLOCK), lambda j: (rj(j), 0)), _bs((BLOCK, BLOCK), lambda j: (0, rj(j)))],
        out_specs=[_bs((BLOCK, BLOCK), lambda j: (rj(j), 0)), _bs((1, BLOCK), lambda j: (0, 0))],
        out_shape=[jax.ShapeDtypeStruct((LP, BLOCK), F32), jax.ShapeDtypeStruct((1, BLOCK), F32)],
        scratch_shapes=[pltpu.VMEM((1, BLOCK), F32)], compiler_params=_cparams(("arbitrary",)),
    )(raw, bias, dc, dct)


ATT_W = 256
ATT_QB = 272
ATT_STEPS = LP // ATT_QB
ATT_KEYS = (640, 1152, 1664, LP)


def _lane_head(width, per, mod=None):
    lane = lax.broadcasted_iota(jnp.int32, (1, width), 1)
    if mod is not None:
        lane = lane % mod
    return lane // per


def _attn_mask(i, kw):
    r = i * ATT_QB + lax.broadcasted_iota(jnp.int32, (ATT_QB, kw), 0)
    c = lax.broadcasted_iota(jnp.int32, (ATT_QB, kw), 1)
    return (c <= r) & ((c >= PAD_ROWS) | (r < PAD_ROWS))


def _attn_by_key_class(i, fn):
    for p, kw in enumerate(ATT_KEYS):
        @pl.when(i // 2 == p)
        def _(kw=kw):
            fn(kw)


def _attn_specs(q, k, v, bias, rope):
    qspec = lambda blk, w=ATT_W: _bs((ATT_QB, w), lambda i: (i, blk))
    fspec = lambda blk, w=ATT_W: _bs((LP, w), lambda i: (0, blk))
    ins = [q[0], k[0], v[0]]
    specs = [qspec(q[1]), fspec(k[1]), fspec(v[1])]
    if bias is not None:
        ins += [bias[0], bias[1]]
        specs += [qspec(0, BLOCK), _bs((BLOCK, LP), lambda i: (0, 0))]
    if rope is not None:
        ins += [rope[0][0], rope[1][0]]
        specs += [qspec(rope[0][1], BLOCK), fspec(rope[1][1], BLOCK)]
    return ins, specs, qspec, fspec


def _attn_fwd(q, k, v, *, scale, name, bias=None, rope=None):
    ins, specs, qspec, fspec = _attn_specs(q, k, v, bias, rope)
    has_bias, has_rope = bias is not None, rope is not None

    def body(*refs):
        it = iter(refs)
        q_ref, k_ref, v_ref = next(it), next(it), next(it)
        if has_bias:
            c_ref, ct_ref = next(it), next(it)
        if has_rope:
            qr_ref, kr_ref = next(it), next(it)
        o_ref, lse_ref = next(it), next(it)
        i = pl.program_id(0)

        def block(kw):
            ok = _attn_mask(i, kw)
            qv, kv, vv = q_ref[...], k_ref[0:kw, :], v_ref[0:kw, :]
            hid, l128 = _lane_head(ATT_W, FOX_HD), _lane_head(BLOCK, 1)
            if has_rope:
                rid = _lane_head(BLOCK, ROPE_HALF, 64)
                qrv, krv = qr_ref[...], kr_ref[0:kw, :]
            def head(h, carry):
                o_acc, lse_acc = carry
                s = _raw_bdot(jnp.where(hid == h, qv, 0.0), kv, 1, 1)
                if has_rope:
                    s = s + _raw_bdot(jnp.where(rid == h, qrv, 0.0), krv, 1, 1)
                s = s * scale
                if has_bias:
                    cq = jnp.sum(jnp.where(l128 == h, c_ref[...], 0.0), axis=1, keepdims=True)
                    s = s + (cq - ct_ref[pl.ds(h, 1), 0:kw])
                s = jnp.where(ok, s, NEG)
                m = jnp.max(s, axis=1, keepdims=True)
                p = jnp.exp(s - m)
                l = jnp.sum(p, axis=1, keepdims=True)
                o_acc = jnp.where(hid == h, _raw_bdot(p, vv, 1, 0) / l, o_acc)
                lse_acc = jnp.where(l128 == h, m + jnp.log(l), lse_acc)
                return o_acc, lse_acc

            o_acc, lse_acc = lax.fori_loop(
                0, FOX_HEADS, head, (jnp.zeros((ATT_QB, ATT_W), F32), jnp.zeros((ATT_QB, BLOCK), F32)), unroll=True)
            o_ref[...] = o_acc
            lse_ref[...] = lse_acc

        _attn_by_key_class(i, block)

    return pl.pallas_call(
        body, name=name, grid=(ATT_STEPS,), in_specs=specs, out_specs=[qspec(0), qspec(0, BLOCK)],
        out_shape=[jax.ShapeDtypeStruct((LP, ATT_W), F32), jax.ShapeDtypeStruct((LP, BLOCK), F32)],
        compiler_params=_cparams(("parallel",)),
    )(*ins)


def _attn_bwd(q, k, v, o, lse, do, *, scale, name, bias=None, rope=None):
    ins, specs, qspec, fspec = _attn_specs(q, k, v, bias, rope)
    has_bias, has_rope = bias is not None, rope is not None
    ins += [o, lse, do[0]]
    specs += [qspec(0), qspec(0, BLOCK), qspec(do[1])]

    def body(*refs):
        it = iter(refs)
        q_ref, k_ref, v_ref = next(it), next(it), next(it)
        if has_bias:
            c_ref, ct_ref = next(it), next(it)
        if has_rope:
            qr_ref, kr_ref = next(it), next(it)
        o_ref, lse_ref, do_ref = next(it), next(it), next(it)
        dq_ref, dk_ref, dv_ref = next(it), next(it), next(it)
        if has_bias:
            dc_ref, dct_ref = next(it), next(it)
        if has_rope:
            dqr_ref, dkr_ref = next(it), next(it)
        i = pl.program_id(0)

        @pl.when(i == 0)
        def _():
            dk_ref[...] = jnp.zeros_like(dk_ref)
            dv_ref[...] = jnp.zeros_like(dv_ref)
            if has_rope:
                dkr_ref[...] = jnp.zeros_like(dkr_ref)
            if has_bias:
                dct_ref[...] = jnp.zeros_like(dct_ref)

        def block(kw):
            ok = _attn_mask(i, kw)
            qv, kv, vv = q_ref[...], k_ref[0:kw, :], v_ref[0:kw, :]
            ov, dov, lsev = o_ref[...], do_ref[...], lse_ref[...]
            hid, l128 = _lane_head(ATT_W, FOX_HD), _lane_head(BLOCK, 1)
            if has_rope:
                rid = _lane_head(BLOCK, ROPE_HALF, 64)
                qrv, krv = qr_ref[...], kr_ref[0:kw, :]

            def head(h, carry):
                dq_acc, aux_acc = carry
                qm = jnp.where(hid == h, qv, 0.0)
                s = _raw_bdot(qm, kv, 1, 1)
                if has_rope:
                    qrm = jnp.where(rid == h, qrv, 0.0)
                    s = s + _raw_bdot(qrm, krv, 1, 1)
                s = s * scale
                if has_bias:
                    cq = jnp.sum(jnp.where(l128 == h, c_ref[...], 0.0), axis=1, keepdims=True)
                    s = s + (cq - ct_ref[pl.ds(h, 1), 0:kw])
                s = jnp.where(ok, s, NEG)
                p = jnp.exp(s - jnp.sum(jnp.where(l128 == h, lsev, 0.0), axis=1, keepdims=True))
                dom = jnp.where(hid == h, dov, 0.0)
                dp = _raw_bdot(dom, vv, 1, 1)
                delta = jnp.sum(dom * ov, axis=1, keepdims=True)
                ds = p * (dp - delta)
                dq_acc = jnp.where(hid == h, _raw_bdot(ds, kv, 1, 0) * scale, dq_acc)
                dk_ref[0:kw, :] += _raw_bdot(ds, qm, 0, 0) * scale
                dv_ref[0:kw, :] += _raw_bdot(p, dom, 0, 0)
                if has_rope:
                    aux_acc = jnp.where(rid == h, _raw_bdot(ds, krv, 1, 0) * scale, aux_acc)
                    dkr_ref[0:kw, :] += _raw_bdot(ds, qrm, 0, 0) * scale
                if has_bias:
                    aux_acc = jnp.where(l128 == h, jnp.sum(ds, axis=1, keepdims=True), aux_acc)
                    dct_ref[pl.ds(h, 1), 0:kw] -= jnp.sum(ds, axis=0, keepdims=True)
                return dq_acc, aux_acc

            dq_acc, aux_acc = lax.fori_loop(
                0, FOX_HEADS, head, (jnp.zeros((ATT_QB, ATT_W), F32), jnp.zeros((ATT_QB, BLOCK), F32)))
            dq_ref[...] = dq_acc
            if has_bias:
                dc_ref[...] = aux_acc
            if has_rope:
                dqr_ref[...] = aux_acc

        _attn_by_key_class(i, block)

    wide = jax.ShapeDtypeStruct((LP, ATT_W), F32)
    narrow = jax.ShapeDtypeStruct((LP, BLOCK), F32)
    out_specs = [qspec(0), fspec(0), fspec(0)]
    out_shape = [wide, wide, wide]
    if has_bias:
        out_specs += [qspec(0, BLOCK), _bs((BLOCK, LP), lambda i: (0, 0))]
        out_shape += [narrow, jax.ShapeDtypeStruct((BLOCK, LP), F32)]
    if has_rope:
        out_specs += [qspec(0, BLOCK), fspec(0, BLOCK)]
        out_shape += [narrow, narrow]
    return pl.pallas_call(
        body, name=name, grid=(ATT_STEPS,), in_specs=specs, out_specs=out_specs, out_shape=out_shape,
        compiler_params=_cparams(("arbitrary",)),
    )(*ins)


def _loss_head(y, target, *, name):
    tile = 272

    def body(y_ref, t_ref, dy_ref, loss_ref):
        i = pl.program_id(0)
        rows = i * tile + lax.broadcasted_iota(jnp.int32, (tile, D_MODEL), 0)
        err = jnp.where(rows >= BLOCK, y_ref[...] - t_ref[...], 0.0)
        dy_ref[...] = err * (1.0 / D_MODEL)
        part = 0.5 * jnp.sum(jnp.sum(err * err, axis=1, keepdims=True) * (1.0 / D_MODEL), axis=0, keepdims=True)
        part = jnp.broadcast_to(part, (1, BLOCK))

        @pl.when(i == 0)
        def _():
            loss_ref[...] = part

        @pl.when(i > 0)
        def _():
            loss_ref[...] += part

    return pl.pallas_call(
        body, name=name, grid=(LP // tile,),
        in_specs=[_bs((tile, D_MODEL), lambda i: (i, 0)), _bs((tile, D_MODEL), lambda i: (i, 0))],
        out_specs=[_bs((tile, D_MODEL), lambda i: (i, 0)), _bs((1, BLOCK), lambda i: (0, 0))],
        out_shape=[jax.ShapeDtypeStruct((LP, D_MODEL), F32), jax.ShapeDtypeStruct((1, BLOCK), F32)],
        compiler_params=_cparams(("arbitrary",)),
    )(y, target)


def _adamw(w, gs, m, v, *, name, after=()):
    if w.ndim == 2:
        w, m, v = w[None], m[None], v[None]
        squeeze = True
    else:
        squeeze = False
    NL, R, C = w.shape
    assert len(gs) == NL
    CG = gs[0].shape[1]
    tile = _tile(R, 256, 8)

    def body(*refs):
        w_ref, g_refs = refs[0], refs[1:1 + NL]
        m_ref, v_ref = refs[1 + NL:3 + NL]
        go_ref, d_ref, nm_ref, nv_ref = refs[3 + NL + len(after):]
        gv = g_refs[0][:, :C]
        for j in range(1, NL):
            gv = jnp.where(pl.program_id(0) == j, g_refs[j][:, :C], gv)
        nm = ADAM_B1 * m_ref[...] + (1.0 - ADAM_B1) * gv
        nv = ADAM_B2 * v_ref[...] + (1.0 - ADAM_B2) * (gv * gv)
        m_hat = nm / (1.0 - ADAM_B1 ** ADAM_STEP)
        v_hat = nv / (1.0 - ADAM_B2 ** ADAM_STEP)
        go_ref[...] = gv
        d_ref[...] = -ADAM_LR * (m_hat / (jnp.sqrt(v_hat) + ADAM_EPS) + ADAM_WD * w_ref[...])
        nm_ref[...] = nm
        nv_ref[...] = nv

    spec = _bs((None, tile, C), lambda l, i: (l, i, 0))
    gspecs = [_bs((tile, CG), lambda l, i, j=j: (jnp.where(l == j, i, 0), 0)) for j in range(NL)]
    res = pl.pallas_call(
        body, name=name, grid=(NL, R // tile), in_specs=[spec, *gspecs, spec, spec, *[ANY] * len(after)],
        out_specs=[spec] * 4, out_shape=[jax.ShapeDtypeStruct((NL, R, C), F32)] * 4,
        compiler_params=_cparams(("parallel", "parallel")),
    )(w, *gs, m, v, *after)
    return [r[0] for r in res] if squeeze else res


def _my_pos():
    return lax.axis_index("x"), lax.axis_index("y"), lax.axis_index("c")


def _other_chips(x, y):
    return [(1 - x, y), (x, 1 - y), (1 - x, 1 - y)]


def _allgather_chips(shards):
    n = len(shards)
    per = 7

    def body(*refs):
        ins, outs = refs[:n], refs[n:2 * n]
        send_sems, recv_sems = refs[2 * n], refs[2 * n + 1]
        x, y, c = _my_pos()
        chips = _other_chips(x, y)
        sibling, me = (x, y, 1 - c), 2 * x + y

        def cp(a, kk, src, dst, to):
            return pltpu.make_async_remote_copy(src_ref=src, dst_ref=dst, send_sem=send_sems.at[per * a + kk],
                                                recv_sem=recv_sems.at[per * a + kk], device_id=to, device_id_type=MESH)

        sends = []
        for a in range(n):
            for j, chip in enumerate(chips):
                sends.append(cp(a, j, ins[a].at[c], outs[a].at[me, c], (*chip, c)))
            sends.append(cp(a, 3, ins[a], outs[a].at[me], sibling))
        for s in sends:
            s.start()
        for a in range(n):
            for j, chip in enumerate(chips):
                slab = outs[a].at[2 * chip[0] + chip[1], c]
                cp(a, j, slab, slab, (x, y, c)).wait_recv()
                fwd = cp(a, 4 + j, slab, slab, sibling)
                fwd.start()
                sends.append(fwd)
        for a in range(n):
            cp(a, 3, ins[a], outs[a].at[me], (x, y, c)).wait_recv()
            for j, chip in enumerate(chips):
                slab = outs[a].at[2 * chip[0] + chip[1], 1 - c]
                cp(a, 4 + j, slab, slab, (x, y, c)).wait_recv()
        for s in sends:
            s.wait_send()

    return pl.pallas_call(
        body, name="allgather_chips", in_specs=[ANY] * n, out_specs=[ANY] * n,
        out_shape=[jax.ShapeDtypeStruct((N_CHIPS,) + s.shape, s.dtype) for s in shards],
        scratch_shapes=[pltpu.SemaphoreType.DMA((per * n,)), pltpu.SemaphoreType.DMA((per * n,))],
    )(*shards)


def _rs_swap_rows(gs, tag):
    n = len(gs)

    def body(*refs):
        ins, outs = refs[:n], refs[n:2 * n]
        send_sems, recv_sems = refs[2 * n], refs[2 * n + 1]
        x, y, c = _my_pos()
        cps = []
        for a in range(n):
            half = ins[a].shape[1] // 2
            cps.append(pltpu.make_async_remote_copy(
                src_ref=ins[a].at[:, pl.ds((1 - c) * half, half)], dst_ref=outs[a], send_sem=send_sems.at[a],
                recv_sem=recv_sems.at[a], device_id=(x, y, 1 - c), device_id_type=MESH))
        for cp in cps:
            cp.start()
        for cp in cps:
            cp.wait()

    return pl.pallas_call(
        body, name=f"rs_swap_rows_{tag}", in_specs=[ANY] * n, out_specs=[ANY] * n,
        out_shape=[jax.ShapeDtypeStruct((N_CHIPS, g.shape[1] // 2, g.shape[2]), g.dtype) for g in gs],
        scratch_shapes=[pltpu.SemaphoreType.DMA((n,)), pltpu.SemaphoreType.DMA((n,))],
    )(*gs)


def _rs_add_pair(g, r, pos, *, name):
    _, H, C = r.shape
    tile = _tile(H, 512, 16)
    nt = H // tile

    def body(pos_ref, g_ref, r_ref, o32_ref, o16_ref):
        s = g_ref[...] + r_ref[...]
        o32_ref[...] = s
        o16_ref[...] = s.astype(BF16)

    spec = _bs((None, tile, C), lambda k, i, pos_ref: (k, i, 0))
    grid_spec = pltpu.PrefetchScalarGridSpec(
        num_scalar_prefetch=1, grid=(N_CHIPS, nt),
        in_specs=[_bs((None, tile, C), lambda k, i, pos_ref: (k, pos_ref[1] * nt + i, 0)), spec],
        out_specs=[spec, spec])
    return pl.pallas_call(
        body, name=name, grid_spec=grid_spec,
        out_shape=[jax.ShapeDtypeStruct((N_CHIPS, H, C), F32), jax.ShapeDtypeStruct((N_CHIPS, H, C), BF16)],
        compiler_params=_cparams(("parallel", "parallel")),
    )(pos, g, r)


def _exchange_copies(srcs, lands, send_sems, recv_sems):
    x, y, c = _my_pos()
    starts, landing = [], []
    for a in range(len(srcs)):
        for j, chip in enumerate(_other_chips(x, y)):
            sems = dict(send_sem=send_sems.at[3 * a + j], recv_sem=recv_sems.at[3 * a + j], device_id_type=MESH)
            starts.append(pltpu.make_async_remote_copy(
                src_ref=srcs[a].at[2 * chip[0] + chip[1]], dst_ref=lands[a].at[j], device_id=(*chip, c), **sems))
            landing.append(pltpu.make_async_remote_copy(
                src_ref=lands[a].at[j], dst_ref=lands[a].at[j], device_id=(x, y, c), **sems))
    return starts, landing


def _gather_copies(srcs, lands, l, send_sems, recv_sems):
    x, y, c = _my_pos()
    me = 2 * x + y
    starts, landing = [], []
    for a in range(len(srcs)):
        half = srcs[a].shape[1] // 2
        mine = pl.ds(c * half, half)
        for j, chip in enumerate(_other_chips(x, y)):
            sems = dict(send_sem=send_sems.at[3 * a + j], recv_sem=recv_sems.at[3 * a + j], device_id_type=MESH)
            starts.append(pltpu.make_async_remote_copy(
                src_ref=srcs[a].at[l, mine], dst_ref=lands[a].at[me, mine], device_id=(*chip, c), **sems))
            slab = lands[a].at[2 * chip[0] + chip[1], mine]
            landing.append(pltpu.make_async_remote_copy(src_ref=slab, dst_ref=slab, device_id=(x, y, c), **sems))
    return starts, landing


HBM = pl.BlockSpec(memory_space=pltpu.HBM)
SEM = pl.BlockSpec(memory_space=pltpu.SEMAPHORE)


def _ici_start(copies_fn, srcs, land_shapes, *, name, after=()):
    n, na = len(srcs), len(after)

    def body(*refs):
        starts, _ = copies_fn(refs[:n], refs[n:2 * n], refs[2 * n + na], refs[2 * n + na + 1])
        for cp in starts:
            cp.start()
        refs[-1][...] = jnp.zeros_like(refs[-1])

    sems = pltpu.SemaphoreType.DMA((3 * n,))
    hbm = lambda s: pltpu.HBM(s.shape, s.dtype)
    lands = [pltpu.with_memory_space_constraint(lax.empty(s.shape, s.dtype), pltpu.HBM) for s in land_shapes]
    res = pl.pallas_call(
        body, name=name, in_specs=[HBM] * (2 * n) + [ANY] * na,
        out_specs=(SEM, SEM, *[HBM] * (2 * n), pl.BlockSpec(memory_space=pltpu.VMEM)),
        out_shape=(sems, sems, *[hbm(s) for s in srcs], *[hbm(s) for s in land_shapes],
                   jax.ShapeDtypeStruct((8, BLOCK), F32)),
        input_output_aliases={i: 2 + i for i in range(2 * n)},
        compiler_params=pltpu.CompilerParams(has_side_effects=pltpu.SideEffectType.DATAFLOW_SIDE_EFFECTING),
    )(*[pltpu.with_memory_space_constraint(s, pltpu.HBM) for s in srcs], *lands, *after)
    return res[0], res[1], list(res[2:2 + n]), list(res[2 + n:2 + 2 * n]), res[-1]


def _ici_wait(copies_fn, send_sems, recv_sems, srcs, lands, after, *, name):
    n = len(srcs)

    def body(*refs):
        starts, landing = copies_fn(refs[:n], refs[n:2 * n], refs[2 * n], refs[2 * n + 1])
        for cp in starts:
            cp.wait_send()
        for cp in landing:
            cp.wait_recv()

    hbm = lambda s: pltpu.HBM(s.shape, s.dtype)
    res = pl.pallas_call(
        body, name=name, in_specs=[*[HBM] * (2 * n), SEM, SEM, ANY], out_specs=[HBM] * (2 * n),
        out_shape=[*[hbm(s) for s in srcs], *[hbm(s) for s in lands]],
        input_output_aliases={i: i for i in range(2 * n)},
        compiler_params=pltpu.CompilerParams(has_side_effects=pltpu.SideEffectType.DATAFLOW_SIDE_EFFECTING),
    )(*srcs, *lands, send_sems, recv_sems, after)
    return list(res[:n]), list(res[n:])


def _gather_d2d(shards, lands, l, tag):
    n = len(shards)

    def body(*refs):
        ins, outs = refs[:n], refs[2 * n:3 * n]
        send_sems, recv_sems = refs[3 * n], refs[3 * n + 1]
        x, y, c = _my_pos()
        me, sibling = 2 * x + y, (x, y, 1 - c)
        starts, landing = [], []
        for a in range(n):
            half = ins[a].shape[1] // 2
            mine, theirs = pl.ds(c * half, half), pl.ds((1 - c) * half, half)
            pairs = [(ins[a].at[l], outs[a].at[me], outs[a].at[me])]
            for chip in _other_chips(x, y):
                k = 2 * chip[0] + chip[1]
                pairs.append((outs[a].at[k, mine], outs[a].at[k, mine], outs[a].at[k, theirs]))
            for j, (src, dst, lands_here) in enumerate(pairs):
                sems = dict(send_sem=send_sems.at[4 * a + j], recv_sem=recv_sems.at[4 * a + j], device_id_type=MESH)
                starts.append(pltpu.make_async_remote_copy(src_ref=src, dst_ref=dst, device_id=sibling, **sems))
                landing.append(pltpu.make_async_remote_copy(src_ref=lands_here, dst_ref=lands_here, device_id=(x, y, c),
                                                            **sems))
        for cp in starts:
            cp.start()
        for cp in landing:
            cp.wait_recv()
        for cp in starts:
            cp.wait_send()

    return pl.pallas_call(
        body, name=f"gather_d2d_{tag}", in_specs=[ANY] * (2 * n), out_specs=[ANY] * n,
        out_shape=[jax.ShapeDtypeStruct(s.shape, s.dtype) for s in lands],
        input_output_aliases={n + a: a for a in range(n)},
        scratch_shapes=[pltpu.SemaphoreType.DMA((4 * n,)), pltpu.SemaphoreType.DMA((4 * n,))],
    )(*shards, *lands)


def _rs_add_chips(p32, r16, pos, *, name):
    _, H, C = p32.shape
    tile = _tile(H, 512, 16)
    nt = H // tile

    def body(pos_ref, p_ref, r_ref, o_ref):
        o_ref[...] = ((p_ref[...] + r_ref[0].astype(F32)) + r_ref[1].astype(F32)) + r_ref[2].astype(F32)

    grid_spec = pltpu.PrefetchScalarGridSpec(
        num_scalar_prefetch=1, grid=(nt,),
        in_specs=[_bs((None, tile, C), lambda i, pos_ref: (pos_ref[0], i, 0)),
                  _bs((3, tile, C), lambda i, pos_ref: (0, i, 0))],
        out_specs=_bs((tile, C), lambda i, pos_ref: (pos_ref[1] * nt + i, 0)))
    return pl.pallas_call(
        body, name=name, grid_spec=grid_spec, out_shape=jax.ShapeDtypeStruct((2 * H, C), F32),
        compiler_params=_cparams(("parallel",)),
    )(pos, p32, r16)


def _rs_join_rows(fs, tag):
    n = len(fs)

    def body(*refs):
        outs = refs[n:2 * n]
        send_sems, recv_sems = refs[2 * n], refs[2 * n + 1]
        x, y, c = _my_pos()
        for a in range(n):
            half = outs[a].shape[0] // 2
            mine = outs[a].at[pl.ds(c * half, half)]
            pltpu.make_async_remote_copy(src_ref=mine, dst_ref=mine, send_sem=send_sems.at[a],
                                         recv_sem=recv_sems.at[a], device_id=(x, y, 1 - c), device_id_type=MESH).start()
        for a in range(n):
            half = outs[a].shape[0] // 2
            pltpu.make_async_remote_copy(
                src_ref=outs[a].at[pl.ds(c * half, half)], dst_ref=outs[a].at[pl.ds((1 - c) * half, half)],
                send_sem=send_sems.at[a], recv_sem=recv_sems.at[a], device_id=(x, y, 1 - c), device_id_type=MESH).wait()

    return pl.pallas_call(
        body, name=f"rs_join_rows_{tag}", in_specs=[ANY] * n, out_specs=[ANY] * n,
        out_shape=[jax.ShapeDtypeStruct(f.shape, f.dtype) for f in fs],
        input_output_aliases={a: a for a in range(n)},
        scratch_shapes=[pltpu.SemaphoreType.DMA((n,)), pltpu.SemaphoreType.DMA((n,))],
    )(*fs)


def _pos_vector():
    x, y, c = _my_pos()
    return jnp.stack([2 * x + y, c]).astype(jnp.int32)


def _swap_copies(srcs, lands, send_sems, recv_sems):
    x, y, c = _my_pos()
    starts, landing = [], []
    for a in range(len(srcs)):
        half = srcs[a].shape[1] // 2
        sems = dict(send_sem=send_sems.at[3 * a], recv_sem=recv_sems.at[3 * a], device_id_type=MESH)
        starts.append(pltpu.make_async_remote_copy(
            src_ref=srcs[a].at[:, pl.ds((1 - c) * half, half)], dst_ref=lands[a], device_id=(x, y, 1 - c), **sems))
        landing.append(pltpu.make_async_remote_copy(src_ref=lands[a], dst_ref=lands[a], device_id=(x, y, c), **sems))
    return starts, landing


def _swap_land_shapes(gs):
    return [jax.ShapeDtypeStruct((N_CHIPS, g.shape[1] // 2, g.shape[2]), g.dtype) for g in gs]


def _rs_add_pairs(gs, r1, names, tag):
    pos = _pos_vector()
    return [_rs_add_pair(g, r, pos, name=f"rs_add_pair_{tag}_{nm}") for g, r, nm in zip(gs, r1, names)]


def _rs_pair_sums(gs, names, tag):
    return _rs_add_pairs(gs, _rs_swap_rows(gs, tag), names, tag)


def _rs_finish(pairs, r2, names, tag):
    pos = _pos_vector()
    fs = [_rs_add_chips(p[0], r, pos, name=f"rs_add_chips_{tag}_{nm}") for p, r, nm in zip(pairs, r2, names)]
    return _rs_join_rows(fs, tag)


def _exchange_land_shapes(pairs):
    return [jax.ShapeDtypeStruct((3,) + p[1].shape[1:], p[1].dtype) for p in pairs]


def _allreduce_small(buf):
    R, W = buf.shape

    def body(b_ref, o_ref, gather, send_sems, recv_sems):
        x, y, c = _my_pos()
        me = 4 * x + 2 * y + c
        gather[me] = b_ref[...]
        cps = []
        for d in range(1, 8):
            peer = (x ^ (d >> 2), y ^ ((d >> 1) & 1), c ^ (d & 1))
            cps.append(pltpu.make_async_remote_copy(
                src_ref=b_ref, dst_ref=gather.at[me], send_sem=send_sems.at[d - 1], recv_sem=recv_sems.at[d - 1],
                device_id=peer, device_id_type=MESH))
        for cp in cps:
            cp.start()
        for d in range(1, 8):
            pltpu.make_async_remote_copy(
                src_ref=b_ref, dst_ref=gather.at[me ^ d], send_sem=send_sems.at[d - 1], recv_sem=recv_sems.at[d - 1],
                device_id=(x, y, c), device_id_type=MESH).wait_recv()
        for cp in cps:
            cp.wait_send()
        acc = gather[0]
        for d in range(1, 8):
            acc = acc + gather[d]
        o_ref[...] = acc

    vm = pl.BlockSpec(memory_space=pltpu.VMEM)
    return pl.pallas_call(
        body, name="allreduce_small", in_specs=[vm], out_specs=vm, out_shape=jax.ShapeDtypeStruct((R, W), F32),
        scratch_shapes=[pltpu.VMEM((8, R, W), F32), pltpu.SemaphoreType.DMA((7,)), pltpu.SemaphoreType.DMA((7,))],
    )(buf)


def _heads(a, h, d):
    return a.reshape(a.shape[0], h, d).transpose(1, 0, 2)


def _unheads(a):
    h, L, d = a.shape
    return a.transpose(1, 0, 2).reshape(L, h * d)


def _rope_tables():
    pos = jnp.maximum(jnp.arange(LP, dtype=F32) - PAD_ROWS, 0.0)
    inv_freq = 1.0 / (ROPE_THETA ** (jnp.arange(0, MLA_ROPE, 2, dtype=F32) / MLA_ROPE))
    ang = pos[:, None] * inv_freq[None, :]
    cos, sin = jnp.tile(jnp.cos(ang), (1, MLA_HEADS)), jnp.tile(jnp.sin(ang), (1, MLA_HEADS))
    return jnp.concatenate([cos, cos], axis=1), jnp.concatenate([-sin, sin], axis=1)


def _lane_pad(a, width=BLOCK):
    return jnp.pad(a, ((0, 0), (0, width - a.shape[1])))


def _pad_in_proj(w):
    sl = lambda start, size: w[:, start:start + size]
    return jnp.concatenate([
        sl(OC_Z, 512), sl(OC_XBC, 768), sl(OC_FQ, 256), sl(OC_FK, 256), sl(OC_FV, 256), sl(OC_CQ, 256), sl(OC_CKV, 128),
        _lane_pad(sl(OC_DT, SSD_HEADS)), _lane_pad(sl(OC_FR, FOX_HEADS)),
        jnp.tile(sl(OC_KR, ROPE_HALF), (1, MLA_HEADS)), jnp.tile(sl(OC_KR + ROPE_HALF, ROPE_HALF), (1, MLA_HEADS))], axis=1)


def _unpad_in_proj(wp):
    sl = lambda start, size: wp[:, start:start + size]
    rope = lambda start: sl(start, 64).reshape(wp.shape[0], MLA_HEADS, ROPE_HALF).sum(axis=1)
    return jnp.concatenate([
        sl(PC_Z, 512), sl(PC_XBC, 768), sl(PC_DT, SSD_HEADS), sl(PC_FQ, 256), sl(PC_FK, 256), sl(PC_FV, 256),
        sl(PC_FR, FOX_HEADS), sl(PC_CQ, 256), sl(PC_CKV, 128), rope(PC_KR), rope(PC_KR + 64)], axis=1)


def _regroup_uq(w):
    w3 = w.reshape(w.shape[0], MLA_HEADS, MLA_NOPE + MLA_ROPE)
    return jnp.concatenate([w3[:, :, :MLA_NOPE].reshape(w.shape[0], -1),
                            w3[:, :, MLA_NOPE:MLA_NOPE + ROPE_HALF].reshape(w.shape[0], -1),
                            w3[:, :, MLA_NOPE + ROPE_HALF:].reshape(w.shape[0], -1)], axis=1)


def _ungroup_uq(wp):
    n = wp.shape[0]
    return jnp.concatenate([wp[:, :256].reshape(n, MLA_HEADS, MLA_NOPE), wp[:, 256:320].reshape(n, MLA_HEADS, ROPE_HALF),
                            wp[:, 320:].reshape(n, MLA_HEADS, ROPE_HALF)], axis=2).reshape(n, -1)


def _regroup_ukv(w):
    w3 = w.reshape(w.shape[0], MLA_HEADS, MLA_NOPE + MLA_V)
    return jnp.concatenate([w3[:, :, :MLA_NOPE].reshape(w.shape[0], -1), w3[:, :, MLA_NOPE:].reshape(w.shape[0], -1)],
                           axis=1)


def _ungroup_ukv(wp):
    n = wp.shape[0]
    return jnp.concatenate([wp[:, :256].reshape(n, MLA_HEADS, MLA_NOPE), wp[:, 256:].reshape(n, MLA_HEADS, MLA_V)],
                           axis=2).reshape(n, -1)


TMF = 1088
N_IF = LP // TMF


def _chunk_rows_mm(a, w, l, chunk_h, *, name, add=None):
    N = w.shape[2]
    return _mm_core(a, w, a_spec=_bs((TMF, chunk_h), lambda i, j, k: (i, k)),
                    b_spec=_bs((None, chunk_h, N), lambda i, j, k: (k, 0, 0)),
                    o_spec=_bs((TMF, N), lambda i, j, k: (i, 0)), grid=(N_IF, 1, N_CHIPS),
                    out_shape=(LP, N), ca=1, cb=0, name=name, add=add)


def _chunk_rows_dx(g, w, l, chunk_h, *, name):
    N = w.shape[2]
    return _mm_core(g, w, a_spec=_bs((TMF, N), lambda i, j, k: (i, 0)),
                    b_spec=_bs((None, chunk_h, N), lambda i, j, k: (j, 0, 0)),
                    o_spec=_bs((TMF, chunk_h), lambda i, j, k: (i, j)), grid=(N_IF, N_CHIPS, 1),
                    out_shape=(LP, N_CHIPS * chunk_h), ca=1, cb=1, name=name)


def _chunk_rows_dw(a, g, chunk_h, *, name):
    N = g.shape[1]
    return _mm_core(a, g, a_spec=_bs((LP, chunk_h), lambda i, j, k: (0, i)), b_spec=_bs((LP, N), lambda i, j, k: (0, 0)),
                    o_spec=_bs((None, chunk_h, N), lambda i, j, k: (i, 0, 0)), grid=(N_CHIPS, 1, 1),
                    out_shape=(N_CHIPS, chunk_h, N), ca=0, cb=0, name=name)


def _ffn_up_swiglu(h, wg, wu, *, name):
    def body(h_ref, wg_ref, wu_ref, g_ref, u_ref, a_ref):
        hb = h_ref[...].astype(BF16)
        g = _raw_bdot(hb, wg_ref[...], 1, 1)
        u = _raw_bdot(hb, wu_ref[...], 1, 1)
        g_ref[...] = g
        u_ref[...] = u
        a_ref[...] = (_silu(g) * u).astype(a_ref.dtype)

    w_spec = _bs((None, HP, D_MODEL), lambda i, j: (j, 0, 0))
    o_spec = _bs((TMF, HP), lambda i, j: (i, j))
    return pl.pallas_call(
        body, name=name, grid=(N_IF, N_CHIPS), in_specs=[_bs((TMF, D_MODEL), lambda i, j: (i, 0)), w_spec, w_spec],
        out_specs=[o_spec] * 3,
        out_shape=[jax.ShapeDtypeStruct((LP, FP), F32), jax.ShapeDtypeStruct((LP, FP), F32),
                   jax.ShapeDtypeStruct((LP, FP), BF16)],
        compiler_params=_cparams(("parallel", "parallel")),
    )(h, wg, wu)


def _ffn_down_dx_swiglu(do, wd, g, u, *, name):
    def body(do_ref, wd_ref, g_ref, u_ref, dg_ref, du_ref):
        dact = _raw_bdot(do_ref[...], wd_ref[...], 1, 1)
        gv = g_ref[...]
        sig = _sigmoid(gv)
        dg_ref[...] = (dact * u_ref[...] * (sig * (1.0 + gv * (1.0 - sig)))).astype(dg_ref.dtype)
        du_ref[...] = (dact * (gv * sig)).astype(du_ref.dtype)

    blk = _bs((TMF, HP), lambda i, j: (i, j))
    return pl.pallas_call(
        body, name=name, grid=(N_IF, N_CHIPS),
        in_specs=[_bs((TMF, D_MODEL), lambda i, j: (i, 0)), _bs((None, HP, D_MODEL), lambda i, j: (j, 0, 0)), blk, blk],
        out_specs=[blk, blk], out_shape=[jax.ShapeDtypeStruct((LP, FP), BF16)] * 2,
        compiler_params=_cparams(("parallel", "parallel")),
    )(do, wd, g, u)


def _ffn_gate_up_dx(dg, du, wg, wu, add, *, name):
    def body(dg_ref, du_ref, wg_ref, wu_ref, add_ref, o_ref, acc_ref):
        k = pl.program_id(1)

        @pl.when(k == 0)
        def _():
            acc_ref[...] = jnp.zeros_like(acc_ref)

        acc_ref[...] += _raw_bdot(dg_ref[...], wg_ref[...], 1, 0) + _raw_bdot(du_ref[...], wu_ref[...], 1, 0)

        @pl.when(k == N_CHIPS - 1)
        def _():
            o_ref[...] = acc_ref[...] + add_ref[...]

    a_spec = _bs((TMF, HP), lambda i, k: (i, k))
    w_spec = _bs((None, HP, D_MODEL), lambda i, k: (k, 0, 0))
    o_spec = _bs((TMF, D_MODEL), lambda i, k: (i, 0))
    return pl.pallas_call(
        body, name=name, grid=(N_IF, N_CHIPS), in_specs=[a_spec, a_spec, w_spec, w_spec, o_spec], out_specs=o_spec,
        out_shape=jax.ShapeDtypeStruct((LP, D_MODEL), F32), scratch_shapes=[pltpu.VMEM((TMF, D_MODEL), F32)],
        compiler_params=_cparams(("parallel", "arbitrary")),
    )(dg, du, wg, wu, add)


def _ffn_fwd(h, W, pre, l, gam, bet, tag):
    g, u, act = _ffn_up_swiglu(h, W[pre + "_w_gate"][l], W[pre + "_w_up"][l], name=f"{tag}_up_swiglu")
    o = _chunk_rows_mm(act, W[pre + "_w_down"][l], l, HP, name=f"{tag}_down")
    (out,) = _rowwise(_make_res_ln_fn(0.5), [h, o], [gam, bet], [D_MODEL], name=f"{tag}_ln", tile=272)
    return out, (h, g, u, act, o)


def _ffn_bwd(dout, saved, W, pre, l, gam, bet, GB, tag):
    h, g, u, act, o = saved
    (dh_a, do), (dgam, dbet) = _rowwise_bwd(_make_res_ln_fn(0.5), [h, o], [gam, bet], [dout], name=f"{tag}_ln_bwd",
                                            tile=272, grad_dtypes=[F32, BF16])
    dg, du = _ffn_down_dx_swiglu(do, W[pre + "_w_down"][l], g, u, name=f"{tag}_down_dx_swiglu")
    GB[pre + "_w_down"] = _chunk_rows_dw(act, do, HP, name=f"{tag}_down_dw")
    GB[pre + "_w_gate"] = _chunk_rows_dw(dg, h, HP, name=f"{tag}_gate_dw")
    GB[pre + "_w_up"] = _chunk_rows_dw(du, h, HP, name=f"{tag}_up_dw")
    dh = _ffn_gate_up_dx(dg, du, W[pre + "_w_gate"][l], W[pre + "_w_up"][l], dh_a, name=f"{tag}_gate_up_dx")
    return dh, dgam, dbet


def _mixer_fwd(h1, W, l, cosf, sins):
    tag = f"l{l}"
    proj = _mm(h1, W["w_in_p"][l], name=f"{tag}_in_proj")
    sv = {"h1": h1, "proj": proj}
    conv_w, conv_b = W["conv_w"][l], W["conv_b"][l][None]
    xc = _conv_fwd(proj, PC_XBC // BLOCK, conv_w, conv_b, name=f"{tag}_conv")
    dt_bias = _lane_pad(W["dt_bias"][l][None])
    (dt,) = _rowwise(_ssd_pre_fn, [(proj, BLOCK, PC_DT // BLOCK)], [dt_bias], [BLOCK], name=f"{tag}_ssd_dt", tile=272)
    xh = _heads(xc[:, :SSD_D], SSD_HEADS, SSD_HD)
    bm = _heads(xc[:, SSD_D:SSD_D + 128], SSD_GROUPS, SSD_STATE)
    cm = _heads(xc[:, SSD_D + 128:], SSD_GROUPS, SSD_STATE)
    dt8 = dt[:, :SSD_HEADS].T
    dtc, dtr = dt8[:, :, None], dt8[:, None, :]
    alog = jnp.broadcast_to(W["a_log"][l][:, None, None], (SSD_HEADS, 1, BLOCK))
    yh, prevs = _ssd_fwd(xh, bm, cm, dtc, dtr, alog, name=f"{tag}_ssd")
    y_raw = _unheads(yh)
    dskip = jnp.repeat(W["d_skip"][l], SSD_HD)[None]
    normg = W["ssd_norm_g"][l][None]
    post_rows = [y_raw, (xc, 256, 0), (proj, 256, PC_Z // 256)]
    (y_ssd,) = _rowwise(_ssd_post_fn, post_rows, [dskip, normg], [SSD_D], name=f"{tag}_ssd_post", tile=272,
                        ncol=SSD_GROUPS)
    sv.update(conv_w=conv_w, conv_b=conv_b, dt_bias=dt_bias, xh=xh, bm=bm, cm=cm, dtc=dtc, dtr=dtr, alog=alog,
              prevs=prevs, post_rows=post_rows, dskip=dskip, normg=normg)
    f_b = _lane_pad(W["fox_f_b"][l][None])
    cg, cgt = _fox_gate_fwd(proj, PC_FR // BLOCK, f_b, name=f"{tag}_fox_gate")
    fox_qkv = ((proj, PC_FQ // ATT_W), (proj, PC_FK // ATT_W), (proj, PC_FV // ATT_W))
    y_fox, lse_f = _attn_fwd(*fox_qkv, scale=FOX_HD ** -0.5, name=f"{tag}_fox_attn", bias=(cg, cgt))
    sv.update(f_b=f_b, cg=cg, cgt=cgt, fox_qkv=fox_qkv, y_fox=y_fox, lse_f=lse_f)
    gq, gkv = W["mla_q_norm_g"][l][None], W["mla_kv_norm_g"][l][None]
    norm_rows = [(proj, 256, PC_CQ // 256), (proj, BLOCK, PC_CKV // BLOCK)]
    qn, cn = _rowwise(_mla_norm_fn, norm_rows, [gq, gkv], [MLA_Q_LORA, MLA_KV_LORA], name=f"{tag}_mla_norm", tile=272,
                      out_dtypes=[BF16, BF16])
    qh = _mm(qn, W["mla_w_uq_p"][l], name=f"{tag}_mla_uq")
    kvh = _mm(cn, W["mla_w_ukv_p"][l], name=f"{tag}_mla_ukv")
    qr, kr = _rowwise(_rope_fn, [(qh, BLOCK, 2), (proj, BLOCK, PC_KR // BLOCK), cosf, sins], [], [BLOCK, BLOCK],
                      name=f"{tag}_rope", tile=272)
    mla_qkv = ((qh, 0), (kvh, 0), (kvh, 1))
    y_mla, lse_m = _attn_fwd(*mla_qkv, scale=(MLA_NOPE + MLA_ROPE) ** -0.5, name=f"{tag}_mla_attn",
                             rope=((qr, 0), (kr, 0)))
    sv.update(gq=gq, gkv=gkv, norm_rows=norm_rows, qn=qn, cn=cn, qr=qr, kr=kr, mla_qkv=mla_qkv, y_mla=y_mla, lse_m=lse_m)
    ycat = jnp.concatenate([y_ssd, y_fox, y_mla], axis=1).astype(BF16)
    mix = _chunk_rows_mm(ycat, W["w_out"][l], l, 256, name=f"{tag}_out_proj")
    (h2,) = _rowwise(_make_res_ln_fn(1.0), [h1, mix], [W["ln2_g"][l][None], W["ln2_b"][l][None]], [D_MODEL],
                     name=f"{tag}_ln2", tile=272)
    sv.update(mix=mix, ycat=ycat)
    return h2, sv


def _mixer_bwd(dh2, sv, W, l, cosf, sins, GB):
    tag = f"l{l}"
    G = {}
    proj = sv["proj"]
    ln2g, ln2b = W["ln2_g"][l][None], W["ln2_b"][l][None]
    (dh1_a, dmix), (dln2g, dln2b) = _rowwise_bwd(
        _make_res_ln_fn(1.0), [sv["h1"], sv["mix"]], [ln2g, ln2b], [dh2], name=f"{tag}_ln2_bwd", tile=272,
        grad_dtypes=[F32, BF16])
    G["ln2_g"], G["ln2_b"] = dln2g[0], dln2b[0]
    dycat = _chunk_rows_dx(dmix, W["w_out"][l], l, 256, name=f"{tag}_out_proj_dx")
    GB["w_out"] = _chunk_rows_dw(sv["ycat"], dmix, 256, name=f"{tag}_out_proj_dw")
    (dy_raw, dxs_a, dz), (ddskip, dnormg) = _rowwise_bwd(
        _ssd_post_fn, sv["post_rows"], [sv["dskip"], sv["normg"]], [dycat[:, :SSD_D]],
        name=f"{tag}_ssd_post_bwd", tile=272, ncol=SSD_GROUPS)
    G["ssd_norm_g"] = dnormg[0]
    G["d_skip"] = ddskip.reshape(SSD_HEADS, SSD_HD).sum(axis=1)
    dxh, dbm, dcm, ddtc, ddtr, dal = _ssd_bwd(sv["xh"], sv["bm"], sv["cm"], sv["dtc"], sv["dtr"], sv["alog"],
                                              sv["prevs"], _heads(dy_raw, SSD_HEADS, SSD_HD), name=f"{tag}_ssd_bwd")
    G["a_log"] = dal[:, 0, 0]
    dxc = jnp.concatenate([dxs_a + _unheads(dxh), _unheads(dbm), _unheads(dcm)], axis=1)
    dxbc, G["conv_w"], dconv_b = _conv_bwd(proj, PC_XBC // BLOCK, sv["conv_w"], sv["conv_b"], dxc,
                                           name=f"{tag}_conv_bwd")
    G["conv_b"] = dconv_b[0]
    ddt = _lane_pad((ddtc[:, :, 0] + ddtr[:, 0, :]).T)
    (ddt_raw,), (ddt_bias,) = _rowwise_bwd(_ssd_pre_fn, [(proj, BLOCK, PC_DT // BLOCK)], [sv["dt_bias"]], [ddt],
                                           name=f"{tag}_ssd_dt_bwd", tile=272)
    G["dt_bias"] = ddt_bias[0, :SSD_HEADS]
    dfq, dfk, dfv, dcg, dcgt = _attn_bwd(*sv["fox_qkv"], sv["y_fox"], sv["lse_f"], (dycat, SSD_D // ATT_W),
                                         scale=FOX_HD ** -0.5, name=f"{tag}_fox_attn_bwd", bias=(sv["cg"], sv["cgt"]))
    df_raw, dfb = _fox_gate_bwd(proj, PC_FR // BLOCK, sv["f_b"], dcg, dcgt, name=f"{tag}_fox_gate_bwd")
    G["fox_f_b"] = dfb[0, :FOX_HEADS]
    dqn_h, dkn_h, dv_h, dqr, dkr = _attn_bwd(
        *sv["mla_qkv"], sv["y_mla"], sv["lse_m"], (dycat, (SSD_D + FOX_D) // ATT_W),
        scale=(MLA_NOPE + MLA_ROPE) ** -0.5, name=f"{tag}_mla_attn_bwd", rope=((sv["qr"], 0), (sv["kr"], 0)))
    dq_rope, dk_rope = _rowwise(_rope_t_fn, [dqr, dkr, cosf, sins], [], [BLOCK, BLOCK], name=f"{tag}_rope_bwd",
                                tile=272)
    dqh = jnp.concatenate([dqn_h, dq_rope], axis=1).astype(BF16)
    dkvh = jnp.concatenate([dkn_h, dv_h], axis=1).astype(BF16)
    dqn = _mm(dqh, W["mla_w_uq_p"][l], tb=True, name=f"{tag}_mla_uq_dx")
    G["mla_w_uq_p"] = _mm(sv["qn"], dqh, ta=True, name=f"{tag}_mla_uq_dw")
    dcn = _mm(dkvh, W["mla_w_ukv_p"][l], tb=True, name=f"{tag}_mla_ukv_dx")
    G["mla_w_ukv_p"] = _mm(sv["cn"], dkvh, ta=True, name=f"{tag}_mla_ukv_dw")
    (dcq, dckv), (dgq, dgkv) = _rowwise_bwd(_mla_norm_fn, sv["norm_rows"], [sv["gq"], sv["gkv"]], [dqn, dcn],
                                            name=f"{tag}_mla_norm_bwd", tile=272)
    G["mla_q_norm_g"], G["mla_kv_norm_g"] = dgq[0], dgkv[0]
    dproj = jnp.concatenate([dz, dxbc, dfq, dfk, dfv, dcq, dckv, ddt_raw, df_raw, dk_rope], axis=1).astype(BF16)
    dh1 = _mm(dproj, W["w_in_p"][l], tb=True, add=dh1_a, name=f"{tag}_in_proj_dx")
    G["w_in_p"] = _mm(sv["h1"], dproj, ta=True, name=f"{tag}_in_proj_dw")
    return dh1, G


def _embed(x, meta):
    return jnp.concatenate([jnp.zeros((PAD_ROWS, D_MODEL), F32), meta, x], axis=0)


def _layer_fwd(h, W, l, cosf, sins):
    ln = lambda n: W[n][l][None]
    h1, s1 = _ffn_fwd(h, W, "ffn1", l, ln("ln1_g"), ln("ln1_b"), f"l{l}_ffn1")
    h2, sm = _mixer_fwd(h1, W, l, cosf, sins)
    h3, s2 = _ffn_fwd(h2, W, "ffn2", l, ln("ln3_g"), ln("ln3_b"), f"l{l}_ffn2")
    return h3, (s1, sm, s2)


def _layer_bwd(dh, saved, W, l, cosf, sins):
    ln = lambda n: W[n][l][None]
    s1, sm, s2 = saved
    G = {}
    dh, dg, db = _ffn_bwd(dh, s2, W, "ffn2", l, ln("ln3_g"), ln("ln3_b"), G, f"l{l}_ffn2")
    G["ln3_g"], G["ln3_b"] = dg[0], db[0]
    dh, Gm = _mixer_bwd(dh, sm, W, l, cosf, sins, G)
    G.update(Gm)
    dh, dg, db = _ffn_bwd(dh, s1, W, "ffn1", l, ln("ln1_g"), ln("ln1_b"), G, f"l{l}_ffn1")
    G["ln1_g"], G["ln1_b"] = dg[0], db[0]
    return dh, G


def _local_step(x, target, W):
    h = _embed(x, W["meta"])
    tgt = jnp.concatenate([jnp.zeros((BLOCK, D_MODEL), F32), target], axis=0)
    cosf, sins = _rope_tables()
    saved = []
    for l in range(DEPTH):
        h, sv = _layer_fwd(h, W, l, cosf, sins)
        saved.append(sv)
    dh, loss = _loss_head(h, tgt, name="loss_head")
    grads = [None] * DEPTH
    for l in reversed(range(DEPTH)):
        dh, grads[l] = _layer_bwd(dh, saved[l], W, l, cosf, sins)
    return loss, dh, grads


WEIGHTS = ['meta', 'ffn1_w_gate', 'ffn1_w_up', 'ffn1_w_down', 'ln1_g', 'ln1_b', 'w_in', 'conv_w', 'conv_b', 'dt_bias',
           'a_log', 'd_skip', 'ssd_norm_g', 'fox_f_b', 'mla_q_norm_g', 'mla_w_uq', 'mla_kv_norm_g', 'mla_w_ukv',
           'w_out', 'ln2_g', 'ln2_b', 'ffn2_w_gate', 'ffn2_w_up', 'ffn2_w_down', 'ln3_g', 'ln3_b']
SMALL = ["ln1_g", "ln1_b", "conv_b", "dt_bias", "a_log", "d_skip", "ssd_norm_g", "fox_f_b", "mla_q_norm_g",
         "mla_kv_norm_g", "ln2_g", "ln2_b", "ln3_g", "ln3_b"]
MATMUL_W = ["ffn1_w_gate", "ffn1_w_up", "ffn1_w_down", "w_in", "mla_w_uq", "mla_w_ukv", "w_out", "ffn2_w_gate",
            "ffn2_w_up", "ffn2_w_down"]
SMALL_ROWS = 312


def _pad_to(a, axis, size):
    pads = [(0, 0)] * a.ndim
    pads[axis] = (0, size - a.shape[axis])
    return jnp.pad(a, pads)


def _chip_cols(full, chip, width):
    return lax.dynamic_slice_in_dim(full, chip * width, width, axis=full.ndim - 1)


def kernel(x, meta, ffn1_w_gate, ffn1_w_up, ffn1_w_down, ln1_g, ln1_b, w_in, conv_w, conv_b, dt_bias, a_log, d_skip, ssd_norm_g, fox_f_b, mla_q_norm_g, mla_w_uq, mla_kv_norm_g, mla_w_ukv, w_out, ln2_g, ln2_b, ffn2_w_gate, ffn2_w_up, ffn2_w_down, ln3_g, ln3_b, loss_target, m_meta, m_ffn1_w_gate, m_ffn1_w_up, m_ffn1_w_down, m_ln1_g, m_ln1_b, m_w_in, m_conv_w, m_conv_b, m_dt_bias, m_a_log, m_d_skip, m_ssd_norm_g, m_fox_f_b, m_mla_q_norm_g, m_mla_w_uq, m_mla_kv_norm_g, m_mla_w_ukv, m_w_out, m_ln2_g, m_ln2_b, m_ffn2_w_gate, m_ffn2_w_up, m_ffn2_w_down, m_ln3_g, m_ln3_b, v_meta, v_ffn1_w_gate, v_ffn1_w_up, v_ffn1_w_down, v_ln1_g, v_ln1_b, v_w_in, v_conv_w, v_conv_b, v_dt_bias, v_a_log, v_d_skip, v_ssd_norm_g, v_fox_f_b, v_mla_q_norm_g, v_mla_w_uq, v_mla_kv_norm_g, v_mla_w_ukv, v_w_out, v_ln2_g, v_ln2_b, v_ffn2_w_gate, v_ffn2_w_up, v_ffn2_w_down, v_ln3_g, v_ln3_b):
    args = dict(locals())
    w = {n: args[n] for n in WEIGHTS}
    m = {n: args["m_" + n] for n in WEIGHTS}
    v = {n: args["v_" + n] for n in WEIGHTS}
    xcoord, ycoord, _ = _my_pos()
    chip = 2 * xcoord + ycoord

    tr = lambda a: jnp.swapaxes(a, 1, 2)

    def bf16_shard(n, zero=None):
        a = w[n] if zero is None else w[n] + zero
        if n.endswith("w_gate") or n.endswith("w_up"):
            a = _pad_to(tr(a), 1, HP)
        elif n.endswith("w_down"):
            a = _pad_to(a, 1, HP)
        elif n == "w_in":
            a = _pad_to(a, 2, IN_SHARD_P)
        return a.astype(BF16)

    land_shape = lambda s: jax.ShapeDtypeStruct((N_CHIPS,) + s.shape[1:], s.dtype)
    gather_l = lambda l: (lambda srcs, lands, ss, rs: _gather_copies(srcs, lands, l, ss, rs))
    tiny = _allgather_chips([w["meta"].reshape(2, N_META // 2, D_MODEL // N_CHIPS), w["conv_w"]])
    meta_full = jnp.concatenate([tiny[0][k].reshape(N_META, D_MODEL // N_CHIPS) for k in range(N_CHIPS)], axis=1)

    W = {n: [None] * DEPTH for n in MATMUL_W + ["w_in_p", "mla_w_uq_p", "mla_w_ukv_p"]}
    W["conv_w"] = jnp.concatenate([tiny[1][k] for k in range(N_CHIPS)], axis=-1)
    W["meta"] = meta_full
    for n in SMALL:
        W[n] = w[n]

    def use_gathered(l, names, lands):
        got = dict(zip(names, lands))
        cat = lambda n, cut=None: jnp.concatenate([got[n][k][..., :cut] for k in range(N_CHIPS)], axis=-1)
        for n in names:
            W[n][l] = got[n]
        if "w_in" in got:
            W["w_in_p"][l] = _pad_in_proj(cat("w_in", IN_SHARD))
            W["mla_w_uq_p"][l] = _regroup_uq(cat("mla_w_uq"))
            W["mla_w_ukv_p"][l] = _regroup_ukv(cat("mla_w_ukv"))

    def chunk_grads(G, names):
        def chunked(name, ungroup, width, pad):
            full = ungroup(G[name])
            return _pad_to(jnp.moveaxis(full.reshape(full.shape[0], N_CHIPS, width), 1, 0), 2, pad)
        special = {"w_in": ("w_in_p", _unpad_in_proj, IN_SHARD, IN_SHARD_P),
                   "mla_w_uq": ("mla_w_uq_p", _ungroup_uq, MLA_NOPE + MLA_ROPE, MLA_NOPE + MLA_ROPE),
                   "mla_w_ukv": ("mla_w_ukv_p", _ungroup_ukv, MLA_NOPE + MLA_V, MLA_NOPE + MLA_V)}
        return [chunked(*special[n]) if n in special else G[n] for n in names]

    def rs_start(G, names, tag):
        pairs = _rs_pair_sums(chunk_grads(G, names), names, tag)
        handle = _ici_start(_exchange_copies, [p[1] for p in pairs], _exchange_land_shapes(pairs),
                            name=f"rs_exchange_{tag}_start")
        return pairs, handle

    def swap_start(G, names, tag):
        gs = chunk_grads(G, names)
        return _ici_start(_swap_copies, gs, _swap_land_shapes(gs), name=f"rs_swap_{tag}_start")

    def exchange_start(swap_handle, names, tag, after):
        gs, r1 = _ici_wait(_swap_copies, *swap_handle[:4], after, name=f"rs_swap_{tag}_wait")
        pairs = _rs_add_pairs(gs, r1, names, tag)
        handle = _ici_start(_exchange_copies, [p[1] for p in pairs], _exchange_land_shapes(pairs),
                            name=f"rs_exchange_{tag}_start")
        return pairs, handle

    def rs_end(pairs, handle, names, tag, after):
        _, r2 = _ici_wait(_exchange_copies, *handle[:4], after, name=f"rs_exchange_{tag}_wait")
        return dict(zip(names, _rs_finish(pairs, r2, names, tag)))

    na = 3
    first, rest = MATMUL_W[:na], MATMUL_W[na:]
    shards = [bf16_shard(n) for n in first]
    g_send, g_recv, g_srcs, g_lands, token = _ici_start(gather_l(0), shards, [land_shape(s) for s in shards],
                                                        name="gather_ici_l0_ffn1_start", after=[tiny[0]])
    shards_rest = [bf16_shard(n, token[0, 0]) for n in rest]
    shards, lands = _ici_wait(gather_l(0), g_send, g_recv, g_srcs, g_lands, shards_rest[0],
                              name="gather_ici_l0_ffn1_wait")
    shards = shards + shards_rest
    land_shapes = [land_shape(s) for s in shards]
    got = _gather_d2d(shards[:na], lands, 0, "l0_ffn1")
    use_gathered(0, first, got)
    g_send, g_recv, g_srcs, g_lands, token = _ici_start(gather_l(0), shards[na:], land_shapes[na:],
                                                        name="gather_ici_l0_rest_start", after=[got[0]])
    cosf, sins = _rope_tables()
    ln = lambda n, l: W[n][l][None]
    h = _embed(x[0] + token[0, 0], meta_full)
    h1, s1 = _ffn_fwd(h, W, "ffn1", 0, ln("ln1_g", 0), ln("ln1_b", 0), "l0_ffn1")
    rest_shards, lands = _ici_wait(gather_l(0), g_send, g_recv, g_srcs, g_lands, h1, name="gather_ici_l0_rest_wait")
    shards = shards[:na] + rest_shards
    got = _gather_d2d(shards[na:], lands, 0, "l0_rest")
    use_gathered(0, rest, got)
    g_send, g_recv, g_srcs, g_lands, token = _ici_start(gather_l(1), shards, land_shapes, name="gather_ici_l1_start",
                                                        after=[got[0]])
    h2, sm = _mixer_fwd(h1 + token[0, 0], W, 0, cosf, sins)
    h, s2 = _ffn_fwd(h2, W, "ffn2", 0, ln("ln3_g", 0), ln("ln3_b", 0), "l0_ffn2")
    saved0 = (s1, sm, s2)
    shards, lands = _ici_wait(gather_l(1), g_send, g_recv, g_srcs, g_lands, h, name="gather_ici_l1_wait")
    use_gathered(1, MATMUL_W, _gather_d2d(shards, lands, 1, "l1"))
    h, saved1 = _layer_fwd(h, W, 1, cosf, sins)
    tgt = jnp.concatenate([jnp.zeros((BLOCK, D_MODEL), F32), loss_target[0]], axis=0)
    dh, loss = _loss_head(h, tgt, name="loss_head")
    G = [None] * DEPTH
    dh, G[1] = _layer_bwd(dh, saved1, W, 1, cosf, sins)
    ffn2_w, mix_w, ffn1_w = MATMUL_W[7:], MATMUL_W[3:7], MATMUL_W[:3]
    sw_l1 = swap_start(G[1], MATMUL_W, "l1")
    G0 = {}
    dh, dg, db = _ffn_bwd(dh + sw_l1[4][0, 0], s2, W, "ffn2", 0, ln("ln3_g", 0), ln("ln3_b", 0), G0, "l0_ffn2")
    G0["ln3_g"], G0["ln3_b"] = dg[0], db[0]
    pairs_l1, x_l1 = exchange_start(sw_l1, MATMUL_W, "l1", dh)
    sw_a = swap_start(G0, ffn2_w, "l0_ffn2")
    dh, Gm = _mixer_bwd(dh + (x_l1[4][0, 0] + sw_a[4][0, 0]), sm, W, 0, cosf, sins, G0)
    G0.update(Gm)
    pairs_a, x_a = exchange_start(sw_a, ffn2_w, "l0_ffn2", dh)
    reduced1 = rs_end(pairs_l1, x_l1, MATMUL_W, "l1", dh)
    pairs_b, x_b = rs_start(G0, mix_w, "l0_mix")
    dh0, dg, db = _ffn_bwd(dh + (x_a[4][0, 0] + x_b[4][0, 0]), s1, W, "ffn1", 0, ln("ln1_g", 0), ln("ln1_b", 0), G0,
                           "l0_ffn1")
    G0["ln1_g"], G0["ln1_b"] = dg[0], db[0]
    G[0] = G0
    reduced0 = rs_end(pairs_a, x_a, ffn2_w, "l0_ffn2", dh0)
    reduced0.update(rs_end(pairs_b, x_b, mix_w, "l0_mix", dh0))

    small_parts = [jnp.stack([G[l][n] for l in range(DEPTH)]).reshape(-1) for n in SMALL]
    small_parts += [jnp.stack([G[l]["conv_w"] for l in range(DEPTH)]).reshape(-1), dh0[PAD_ROWS:BLOCK].reshape(-1),
                    loss[0, :1]]
    flat = jnp.concatenate(small_parts)
    flat = jnp.pad(flat, (0, SMALL_ROWS * BLOCK - flat.shape[0]))
    red2d = _allreduce_small(flat.reshape(SMALL_ROWS, BLOCK))
    red = red2d.reshape(-1)

    pairs_c = _rs_pair_sums(chunk_grads(G0, ffn1_w), ffn1_w, "l0_ffn1")
    x_c = _ici_start(_exchange_copies, [p[1] for p in pairs_c], _exchange_land_shapes(pairs_c),
                     name="rs_exchange_l0_ffn1_start", after=[red2d])
    grads, off = {}, 0
    for n in SMALL:
        size = int(np.prod(w[n].shape))
        grads[n] = red[off:off + size].reshape(w[n].shape)
        off += size
    conv_full = red[off:off + DEPTH * SSD_CONV * 768].reshape(DEPTH, SSD_CONV, 768)
    off += DEPTH * SSD_CONV * 768
    dmeta_full = red[off:off + N_META * D_MODEL].reshape(N_META, D_MODEL)
    off += N_META * D_MODEL
    loss_out = red[off]
    grads["conv_w"] = _chip_cols(conv_full, chip, 768 // N_CHIPS)
    grads["meta"] = _chip_cols(dmeta_full, chip, D_MODEL // N_CHIPS)

    delta, new_m, new_v = {}, {}, {}

    def adamw_matmul_weights(names, after):
        for n in names:
            gs = [reduced0[n], reduced1[n]]
            if n.endswith("w_gate") or n.endswith("w_up"):
                res = _adamw(tr(w[n]), gs, tr(m[n]), tr(v[n]), name=f"adamw_{n}", after=after)
                grads[n], delta[n], new_m[n], new_v[n] = [tr(r) for r in res]
            else:
                grads[n], delta[n], new_m[n], new_v[n] = _adamw(w[n], gs, m[n], v[n], name=f"adamw_{n}", after=after)

    adamw_matmul_weights(ffn2_w + mix_w, [x_c[4]])
    rest = [n for n in WEIGHTS if n not in MATMUL_W]

    def pack_small(d):
        f = jnp.concatenate([d[n].reshape(-1) for n in rest])
        tot = -(-f.shape[0] // (8 * BLOCK)) * 8 * BLOCK
        return jnp.pad(f, (0, tot - f.shape[0])).reshape(-1, BLOCK)

    _, d2, m2, v2 = _adamw(pack_small(w), [pack_small(grads)], pack_small(m), pack_small(v), name="adamw_small",
                           after=[x_c[4]])
    reduced0.update(rs_end(pairs_c, x_c, ffn1_w, "l0_ffn1", d2))
    adamw_matmul_weights(ffn1_w, [])
    off = 0
    for n in rest:
        size = int(np.prod(w[n].shape))
        for dst, src in ((delta, d2), (new_m, m2), (new_v, v2)):
            dst[n] = src.reshape(-1)[off:off + size].reshape(w[n].shape)
        off += size

    grad_x = dh0[BLOCK:][None]
    return (loss_out, grad_x, *[grads[n] for n in WEIGHTS], *[delta[n] for n in WEIGHTS],
            *[new_m[n] for n in WEIGHTS], *[new_v[n] for n in WEIGHTS])
```

```python
import functools

import numpy as np
import jax
import jax.numpy as jnp
from jax import lax
from jax.experimental import pallas as pl
from jax.experimental.pallas import tpu as pltpu

F32 = jnp.float32
BF16 = jnp.bfloat16
MESH = pl.DeviceIdType.MESH

D_MODEL = 1024
SEQ = 2048
N_META = 16
BLOCK = 128
PAD_ROWS = 112
LP = PAD_ROWS + N_META + SEQ
N_CHUNK = LP // BLOCK
DEPTH = 2
D_FF = 2816
N_CHIPS = 4
FF_SHARD = D_FF // N_CHIPS
HP = 768
FP = N_CHIPS * HP
SSD_HEADS, SSD_HD, SSD_D, SSD_GROUPS, SSD_STATE, SSD_CONV = 8, 64, 512, 2, 64, 4
FOX_HEADS, FOX_HD, FOX_D = 4, 64, 256
MLA_HEADS, MLA_Q_LORA, MLA_KV_LORA, MLA_NOPE, MLA_ROPE, MLA_V, MLA_D = 4, 256, 128, 64, 32, 64, 256
ROPE_HALF = MLA_ROPE // 2
ROPE_THETA = 10000.0
N_IN = 2476
IN_SHARD = N_IN // N_CHIPS
IN_SHARD_P = 640
ALPHA = (2 * DEPTH) ** 0.25
EPS = 1e-5
ADAM_LR, ADAM_B1, ADAM_B2, ADAM_EPS, ADAM_WD, ADAM_STEP = 0.001, 0.9, 0.999, 1e-08, 0.01, 10
NEG = -1e30
TM = 544

VMEM_LIMIT_BYTES = 56 * 1024 * 1024

PC_Z, PC_XBC, PC_FQ, PC_FK, PC_FV, PC_CQ, PC_CKV, PC_DT, PC_FR, PC_KR, PC_END = (
    0, 512, 1280, 1536, 1792, 2048, 2304, 2432, 2560, 2688, 2816)
OC_Z, OC_XBC, OC_DT, OC_FQ, OC_FK, OC_FV, OC_FR, OC_CQ, OC_CKV, OC_KR = (
    0, 512, 1280, 1288, 1544, 1800, 2056, 2060, 2316, 2444)


def _cparams(sem=None):
    return pltpu.CompilerParams(dimension_semantics=sem, vmem_limit_bytes=VMEM_LIMIT_BYTES)


def _tile(n, cap, mult):
    best = None
    for t in range(mult, min(n, cap) + 1, mult):
        if n % t == 0:
            best = t
    return best if best is not None else n


def _bs(shape, fn):
    return pl.BlockSpec(shape, fn)


ANY = pl.BlockSpec(memory_space=pl.ANY)


def _dims(ca, cb):
    return (((ca,), (cb,)), ((), ()))


def _raw_bdot(a, b, ca, cb):
    return lax.dot_general(a.astype(BF16), b.astype(BF16), _dims(ca, cb), preferred_element_type=F32)


def _mm_core(a, b, *, a_spec, b_spec, o_spec, grid, out_shape, ca, cb, name, add=None):
    nk = grid[2]
    has_add = add is not None
    acc_shape = tuple(d for d in o_spec.block_shape if d is not None)

    def body(*refs):
        a_ref, b_ref = refs[0], refs[1]
        add_ref = refs[2] if has_add else None
        o_ref, acc_ref = refs[-2], refs[-1]
        k = pl.program_id(2)

        @pl.when(k == 0)
        def _():
            acc_ref[...] = jnp.zeros_like(acc_ref)

        acc_ref[...] += _raw_bdot(a_ref[...], b_ref[...], ca, cb)

        @pl.when(k == nk - 1)
        def _():
            r = acc_ref[...]
            if has_add:
                r = r + add_ref[...]
            o_ref[...] = r

    ins = [a, b] + ([add] if has_add else [])
    in_specs = [a_spec, b_spec] + ([o_spec] if has_add else [])
    return pl.pallas_call(
        body, name=name, grid=grid, in_specs=in_specs, out_specs=o_spec,
        out_shape=jax.ShapeDtypeStruct(out_shape, F32), scratch_shapes=[pltpu.VMEM(acc_shape, F32)],
        compiler_params=_cparams(("parallel", "parallel", "arbitrary")),
    )(*ins)


MM_VMEM_BUDGET = 40 * 1024 * 1024


def _divisors(n, mult):
    return [t for t in range(mult, n + 1, mult) if n % t == 0] or [n]


def _pick_tiles(M, N, K, a_bytes, b_bytes, ta, has_add):
    best = None
    for tm in _divisors(M, 128 if ta else 16):
        for tn in _divisors(N, 128):
            vmem = 2 * tm * K * a_bytes + 2 * K * tn * b_bytes + (3 + 2 * int(has_add)) * tm * tn * 4
            if vmem <= MM_VMEM_BUDGET:
                key = ((M // tm) * (N // tn), -tn)
                if best is None or key < best[0]:
                    best = (key, tm, tn)
    assert best is not None, (M, N, K)
    return best[1], best[2], K


def _mm(a, b, *, ta=False, tb=False, add=None, name):
    if ta:
        K, M = a.shape
    else:
        M, K = a.shape
    if tb:
        N, Kb = b.shape
    else:
        Kb, N = b.shape
    assert K == Kb, (a.shape, b.shape, ta, tb)
    tm, tn, tk = _pick_tiles(M, N, K, a.dtype.itemsize, b.dtype.itemsize, ta, add is not None)
    a_spec = _bs((tk, tm), lambda i, j, k: (k, i)) if ta else _bs((tm, tk), lambda i, j, k: (i, k))
    b_spec = _bs((tn, tk), lambda i, j, k: (j, k)) if tb else _bs((tk, tn), lambda i, j, k: (k, j))
    return _mm_core(a, b, a_spec=a_spec, b_spec=b_spec, o_spec=_bs((tm, tn), lambda i, j, k: (i, j)),
                    grid=(M // tm, N // tn, K // tk), out_shape=(M, N), ca=0 if ta else 1, cb=1 if tb else 0,
                    name=name, add=add)


def _row_entry(r, ncol):
    if isinstance(r, tuple):
        return r
    return r, r.shape[1] // ncol, 0


def _rowwise(fn, rows, pars, out_cols, *, name, tile, ncol=1, out_dtypes=None):
    rows = [_row_entry(r, ncol) for r in rows]
    L = rows[0][0].shape[0]
    nr, npar = len(rows), len(pars)
    in_specs = [_bs((tile, w), lambda g, i, o=o: (i, o + g)) for _, w, o in rows]
    in_specs += [_bs((p.shape[0], p.shape[1] // ncol), lambda g, i: (0, g)) for p in pars]
    out_specs = [_bs((tile, c // ncol), lambda g, i: (i, g)) for c in out_cols]

    def body(*refs):
        ins, outs = refs[:nr + npar], refs[nr + npar:]
        row0 = pl.program_id(1) * tile
        res = fn(row0, *[r[...] for r in ins])
        for o, v in zip(outs, res):
            o[...] = v.astype(o.dtype)

    return pl.pallas_call(
        body, name=name, grid=(ncol, L // tile), in_specs=in_specs, out_specs=out_specs,
        out_shape=[jax.ShapeDtypeStruct((L, c), d) for c, d in zip(out_cols, out_dtypes or [F32] * len(out_cols))],
        compiler_params=_cparams(("parallel", "parallel")),
    )(*[r[0] for r in rows], *pars)


def _rowwise_bwd(fn, rows, pars, douts, *, name, tile, ncol=1, row_grad=None, grad_dtypes=None):
    rows = [_row_entry(r, ncol) for r in rows]
    L = rows[0][0].shape[0]
    nr, npar, nd = len(rows), len(pars), len(douts)
    row_grad = [True] * nr if row_grad is None else row_grad
    in_specs = [_bs((tile, w), lambda g, i, o=o: (i, o + g)) for _, w, o in rows]
    in_specs += [_bs((p.shape[0], p.shape[1] // ncol), lambda g, i: (0, g)) for p in pars]
    in_specs += [_bs((tile, d.shape[1] // ncol), lambda g, i: (i, g)) for d in douts]
    g_widths = [w * ncol for (_, w, _), f in zip(rows, row_grad) if f]
    out_specs = [_bs((tile, w // ncol), lambda g, i: (i, g)) for w in g_widths]
    out_specs += [_bs((p.shape[0], p.shape[1] // ncol), lambda g, i: (0, g)) for p in pars]
    out_shape = [jax.ShapeDtypeStruct((L, w), d) for w, d in zip(g_widths, grad_dtypes or [F32] * len(g_widths))]
    out_shape += [jax.ShapeDtypeStruct(p.shape, F32) for p in pars]

    def body(*refs):
        ins = refs[:nr + npar]
        dos = refs[nr + npar:nr + npar + nd]
        outs = refs[nr + npar + nd:]
        i = pl.program_id(1)
        row0 = i * tile
        _, vjp = jax.vjp(lambda *a: tuple(fn(row0, *a)), *[r[...] for r in ins])
        grads = vjp(tuple(d[...].astype(F32) for d in dos))
        o = 0
        for j in range(nr):
            if row_grad[j]:
                outs[o][...] = grads[j].astype(outs[o].dtype)
                o += 1
        for j in range(npar):
            g, ref = grads[nr + j], outs[o + j]

            @pl.when(i == 0)
            def _(g=g, ref=ref):
                ref[...] = g

            @pl.when(i > 0)
            def _(g=g, ref=ref):
                ref[...] += g

    res = pl.pallas_call(
        body, name=name, grid=(ncol, L // tile), in_specs=in_specs, out_specs=out_specs, out_shape=out_shape,
        compiler_params=_cparams(("parallel", "arbitrary")),
    )(*[r[0] for r in rows], *pars, *douts)
    return res[:len(g_widths)], res[len(g_widths):]


def _row_ids(row0, shape):
    return row0 + lax.broadcasted_iota(jnp.int32, shape, 0)


def _sigmoid(x):
    return 1.0 / (1.0 + jnp.exp(-x))


@jax.custom_vjp
def _softplus(x):
    return jnp.maximum(x, 0.0) + jnp.log(1.0 + jnp.exp(-jnp.abs(x)))


def _softplus_fwd(x):
    return _softplus(x), x


def _softplus_bwd(x, g):
    return (g * _sigmoid(x),)


_softplus.defvjp(_softplus_fwd, _softplus_bwd)


def _silu(x):
    return x * _sigmoid(x)


def _make_res_ln_fn(scale):
    def fn(row0, h, o, gam, bet):
        pre = ALPHA * h + scale * o
        mu = jnp.mean(pre, axis=-1, keepdims=True)
        xc = pre - mu
        var = jnp.mean(xc * xc, axis=-1, keepdims=True)
        return (xc * lax.rsqrt(var + EPS) * gam + bet,)
    return fn


def _ssd_pre_fn(row0, raw, bias):
    dt = _softplus(raw + bias)
    return (jnp.where(_row_ids(row0, raw.shape) >= PAD_ROWS, dt, 0.0),)


def _ssd_post_fn(row0, y, xs, z, dskip, normg):
    v = (y + dskip * xs) * _silu(z)
    v = v * lax.rsqrt(jnp.mean(v * v, axis=-1, keepdims=True) + EPS)
    return (v * normg,)


def _mla_norm_fn(row0, cq, ckv, gq, gkv):
    qn = cq * lax.rsqrt(jnp.mean(cq * cq, axis=-1, keepdims=True) + EPS) * gq
    cn = ckv * lax.rsqrt(jnp.mean(ckv * ckv, axis=-1, keepdims=True) + EPS) * gkv
    return qn, cn


def _rope_fn(row0, q, k, cosf, sins):
    return (q * cosf + pltpu.roll(q, 64, 1) * sins, k * cosf + pltpu.roll(k, 64, 1) * sins)


def _rope_t_fn(row0, gq, gk, cosf, sins):
    return (gq * cosf + pltpu.roll(gq * sins, 64, 1), gk * cosf + pltpu.roll(gk * sins, 64, 1))


def _conv_fwd(x, x_off, w, b, *, name):
    C = w.shape[1]

    def body(x_ref, w_ref, b_ref, o_ref):
        rows = lax.broadcasted_iota(jnp.int32, (LP, BLOCK), 0)
        xv = jnp.where(rows >= PAD_ROWS, x_ref[...], 0.0)
        acc = b_ref[...] + w_ref[3:4, :] * xv
        for k in range(SSD_CONV - 1):
            acc = acc + w_ref[k:k + 1, :] * pltpu.roll(xv, SSD_CONV - 1 - k, 0)
        o_ref[...] = _silu(acc)

    return pl.pallas_call(
        body, name=name, grid=(C // BLOCK,),
        in_specs=[_bs((LP, BLOCK), lambda j: (0, j + x_off)), _bs((SSD_CONV, BLOCK), lambda j: (0, j)),
                  _bs((1, BLOCK), lambda j: (0, j))],
        out_specs=_bs((LP, BLOCK), lambda j: (0, j)),
        out_shape=jax.ShapeDtypeStruct((LP, C), F32), compiler_params=_cparams(("parallel",)),
    )(x, w, b)


def _conv_bwd(x, x_off, w, b, dout, *, name):
    C = w.shape[1]

    def body(x_ref, w_ref, b_ref, do_ref, dx_ref, dw_ref, db_ref):
        rows = lax.broadcasted_iota(jnp.int32, (LP, BLOCK), 0)
        real = rows >= PAD_ROWS
        xv = jnp.where(real, x_ref[...], 0.0)
        shifted = [pltpu.roll(xv, SSD_CONV - 1 - k, 0) for k in range(SSD_CONV - 1)] + [xv]
        acc = b_ref[...]
        for k in range(SSD_CONV):
            acc = acc + w_ref[k:k + 1, :] * shifted[k]
        sig = _sigmoid(acc)
        dacc = jnp.where(real, do_ref[...] * (sig * (1.0 + acc * (1.0 - sig))), 0.0)
        db_ref[...] = jnp.sum(dacc, axis=0, keepdims=True)
        dx = w_ref[3:4, :] * dacc
        for k in range(SSD_CONV):
            dw_ref[k:k + 1, :] = jnp.sum(dacc * shifted[k], axis=0, keepdims=True)
            if k < SSD_CONV - 1:
                dx = dx + w_ref[k:k + 1, :] * pltpu.roll(dacc, LP - (SSD_CONV - 1 - k), 0)
        dx_ref[...] = jnp.where(real, dx, 0.0)

    return pl.pallas_call(
        body, name=name, grid=(C // BLOCK,),
        in_specs=[_bs((LP, BLOCK), lambda j: (0, j + x_off)), _bs((SSD_CONV, BLOCK), lambda j: (0, j)),
                  _bs((1, BLOCK), lambda j: (0, j)), _bs((LP, BLOCK), lambda j: (0, j))],
        out_specs=[_bs((LP, BLOCK), lambda j: (0, j)), _bs((SSD_CONV, BLOCK), lambda j: (0, j)),
                   _bs((1, BLOCK), lambda j: (0, j))],
        out_shape=[jax.ShapeDtypeStruct((LP, C), F32), jax.ShapeDtypeStruct((SSD_CONV, C), F32),
                   jax.ShapeDtypeStruct((1, C), F32)],
        compiler_params=_cparams(("parallel",)),
    )(x, w, b, dout)


_BDIMS = {"nn": (((2,), (1,)), ((0,), (0,))), "nt": (((2,), (2,)), ((0,), (0,))), "tn": (((1,), (1,)), ((0,), (0,)))}


def _raw_bdot3(a, b, mode):
    return lax.dot_general(a.astype(BF16), b.astype(BF16), _BDIMS[mode], preferred_element_type=F32)


@functools.partial(jax.custom_vjp, nondiff_argnums=(2,))
def _bdot3(a, b, mode):
    return _raw_bdot3(a, b, mode)


def _bdot3_fwd(a, b, mode):
    return _raw_bdot3(a, b, mode), (a, b)


def _bdot3_bwd(mode, res, g):
    a, b = res
    if mode == "nn":
        return _raw_bdot3(g, b, "nt"), _raw_bdot3(a, g, "tn")
    if mode == "nt":
        return _raw_bdot3(g, b, "nn"), _raw_bdot3(g, a, "tn")
    return _raw_bdot3(b, g, "nt"), _raw_bdot3(a, g, "nn")


_bdot3.defvjp(_bdot3_fwd, _bdot3_bwd)


def _ssd_chunk(x, bm, cm, dtc, dtr, alog, prev):
    rep = SSD_HEADS // SSD_GROUPS
    per_head = lambda t: jnp.broadcast_to(t[:, None], (SSD_GROUPS, rep) + t.shape[1:]).reshape((SSD_HEADS,) + t.shape[1:])
    bm, cm = per_head(bm), per_head(cm)
    lane = lax.broadcasted_iota(jnp.int32, alog.shape, 2)
    a_neg = -jnp.exp(jnp.sum(jnp.where(lane == 0, alog, 0.0), axis=2, keepdims=True))
    ac_in = dtc * a_neg
    ar_in = dtr * a_neg
    li = lax.broadcasted_iota(jnp.int32, (1, BLOCK, BLOCK), 1)
    si = lax.broadcasted_iota(jnp.int32, (1, BLOCK, BLOCK), 2)
    causal = li >= si
    acum_c = jnp.sum(jnp.where(causal, ar_in, 0.0), axis=2, keepdims=True)
    acum_r = jnp.sum(jnp.where(li <= si, ac_in, 0.0), axis=1, keepdims=True)
    total = jnp.sum(ar_in, axis=2, keepdims=True)
    seg = jnp.exp(jnp.where(causal, acum_c - acum_r, NEG))
    xdt = x * dtc
    cb = _bdot3(cm, bm, "nt")
    y = _bdot3(cb * seg, xdt, "nn") + _bdot3(cm, prev, "nt") * jnp.exp(acum_c)
    st = _bdot3(xdt, bm * jnp.exp(total - acum_c), "tn")
    return y, prev * jnp.exp(total) + st


def _ssd_specs(rev):
    ci = (lambda c: N_CHUNK - 1 - c) if rev else (lambda c: c)
    x_spec = _bs((SSD_HEADS, BLOCK, SSD_HD), lambda c: (0, ci(c), 0))
    g_spec = _bs((SSD_GROUPS, BLOCK, SSD_STATE), lambda c: (0, ci(c), 0))
    dtc_spec = _bs((SSD_HEADS, BLOCK, 1), lambda c: (0, ci(c), 0))
    dtr_spec = _bs((SSD_HEADS, 1, BLOCK), lambda c: (0, 0, ci(c)))
    al_spec = _bs((SSD_HEADS, 1, BLOCK), lambda c: (0, 0, 0))
    st_spec = _bs((None, SSD_HEADS, SSD_HD, SSD_STATE), lambda c: (ci(c), 0, 0, 0))
    return x_spec, g_spec, dtc_spec, dtr_spec, al_spec, st_spec


def _ssd_fwd(x, bm, cm, dtc, dtr, alog, *, name):
    x_spec, g_spec, dtc_spec, dtr_spec, al_spec, st_spec = _ssd_specs(False)

    def body(x_ref, b_ref, c_ref, dtc_ref, dtr_ref, al_ref, y_ref, prev_ref, state):
        @pl.when(pl.program_id(0) == 0)
        def _():
            state[...] = jnp.zeros_like(state)

        prev = state[...]
        prev_ref[...] = prev
        y, new = _ssd_chunk(x_ref[...], b_ref[...], c_ref[...], dtc_ref[...], dtr_ref[...], al_ref[...], prev)
        y_ref[...] = y
        state[...] = new

    return pl.pallas_call(
        body, name=name, grid=(N_CHUNK,),
        in_specs=[x_spec, g_spec, g_spec, dtc_spec, dtr_spec, al_spec], out_specs=[x_spec, st_spec],
        out_shape=[jax.ShapeDtypeStruct((SSD_HEADS, LP, SSD_HD), F32),
                   jax.ShapeDtypeStruct((N_CHUNK, SSD_HEADS, SSD_HD, SSD_STATE), F32)],
        scratch_shapes=[pltpu.VMEM((SSD_HEADS, SSD_HD, SSD_STATE), F32)],
        compiler_params=_cparams(("arbitrary",)),
    )(x, bm, cm, dtc, dtr, alog)


def _ssd_bwd(x, bm, cm, dtc, dtr, alog, prevs, dy, *, name):
    x_spec, g_spec, dtc_spec, dtr_spec, al_spec, st_spec = _ssd_specs(True)

    def body(x_ref, b_ref, c_ref, dtc_ref, dtr_ref, al_ref, prev_ref, dy_ref,
             dx_ref, db_ref, dc_ref, ddtc_ref, ddtr_ref, dal_ref, dstate):
        c = pl.program_id(0)

        @pl.when(c == 0)
        def _():
            dstate[...] = jnp.zeros_like(dstate)

        _, vjp = jax.vjp(_ssd_chunk, x_ref[...], b_ref[...], c_ref[...], dtc_ref[...], dtr_ref[...], al_ref[...],
                         prev_ref[...])
        dx, db, dc, ddtc, ddtr, dal, dprev = vjp((dy_ref[...], dstate[...]))
        dx_ref[...] = dx
        db_ref[...] = db
        dc_ref[...] = dc
        ddtc_ref[...] = ddtc
        ddtr_ref[...] = ddtr
        dstate[...] = dprev

        @pl.when(c == 0)
        def _():
            dal_ref[...] = dal

        @pl.when(c > 0)
        def _():
            dal_ref[...] += dal

    hs = jax.ShapeDtypeStruct((SSD_HEADS, LP, SSD_HD), F32)
    gs = jax.ShapeDtypeStruct((SSD_GROUPS, LP, SSD_STATE), F32)
    return pl.pallas_call(
        body, name=name, grid=(N_CHUNK,),
        in_specs=[x_spec, g_spec, g_spec, dtc_spec, dtr_spec, al_spec, st_spec, x_spec],
        out_specs=[x_spec, g_spec, g_spec, dtc_spec, dtr_spec, al_spec],
        out_shape=[hs, gs, gs, jax.ShapeDtypeStruct((SSD_HEADS, LP, 1), F32),
                   jax.ShapeDtypeStruct((SSD_HEADS, 1, LP), F32), jax.ShapeDtypeStruct((SSD_HEADS, 1, BLOCK), F32)],
        scratch_shapes=[pltpu.VMEM((SSD_HEADS, SSD_HD, SSD_STATE), F32)],
        compiler_params=_cparams(("arbitrary",)),
    )(x, bm, cm, dtc, dtr, alog, prevs, dy)


def _tri_dot(tri, v):
    hi = v.astype(BF16)
    r1 = v - hi.astype(F32)
    mid = r1.astype(BF16)
    lo = (r1 - mid.astype(F32)).astype(BF16)
    t = tri.astype(BF16)
    d = lambda p: lax.dot_general(t, p, _dims(1, 0), preferred_element_type=F32)
    return d(hi) + d(mid) + d(lo)


def _fox_gate_fwd(raw, raw_blk, bias, *, name):
    def body(raw_ref, b_ref, c_ref, ct_ref, carry):
        j = pl.program_id(0)

        @pl.when(j == 0)
        def _():
            carry[...] = jnp.zeros_like(carry)

        rows = j * BLOCK + lax.broadcasted_iota(jnp.int32, (BLOCK, BLOCK), 0)
        lf = jnp.where(rows >= PAD_ROWS, -_softplus(-(raw_ref[...] + b_ref[...])), 0.0)
        li = lax.broadcasted_iota(jnp.int32, (BLOCK, BLOCK), 0)
        si = lax.broadcasted_iota(jnp.int32, (BLOCK, BLOCK), 1)
        cv = _tri_dot(jnp.where(li >= si, 1.0, 0.0), lf) + carry[...]
        c_ref[...] = cv
        ct_ref[...] = cv.T
        carry[...] += jnp.sum(lf, axis=0, keepdims=True)

    return pl.pallas_call(
        body, name=name, grid=(N_CHUNK,),
        in_specs=[_bs((BLOCK, BLOCK), lambda j: (j, raw_blk)), _bs((1, BLOCK), lambda j: (0, 0))],
        out_specs=[_bs((BLOCK, BLOCK), lambda j: (j, 0)), _bs((BLOCK, BLOCK), lambda j: (0, j))],
        out_shape=[jax.ShapeDtypeStruct((LP, BLOCK), F32), jax.ShapeDtypeStruct((BLOCK, LP), F32)],
        scratch_shapes=[pltpu.VMEM((1, BLOCK), F32)], compiler_params=_cparams(("arbitrary",)),
    )(raw, bias)


def _fox_gate_bwd(raw, raw_blk, bias, dc, dct, *, name):
    rj = lambda j: N_CHUNK - 1 - j

    def body(raw_ref, b_ref, dc_ref, dct_ref, draw_ref, db_ref, carry):
        j = pl.program_id(0)

        @pl.when(j == 0)
        def _():
            carry[...] = jnp.zeros_like(carry)

        rows = (N_CHUNK - 1 - j) * BLOCK + lax.broadcasted_iota(jnp.int32, (BLOCK, BLOCK), 0)
        li = lax.broadcasted_iota(jnp.int32, (BLOCK, BLOCK), 0)
        si = lax.broadcasted_iota(jnp.int32, (BLOCK, BLOCK), 1)
        dcv = dc_ref[...] + dct_ref[...].T
        dlf = _tri_dot(jnp.where(li <= si, 1.0, 0.0), dcv) + carry[...]
        carry[...] += jnp.sum(dcv, axis=0, keepdims=True)
        draw = jnp.where(rows >= PAD_ROWS, dlf * (1.0 - _sigmoid(raw_ref[...] + b_ref[...])), 0.0)
        draw_ref[...] = draw
        dsum = jnp.sum(draw, axis=0, keepdims=True)

        @pl.when(j == 0)
        def _():
            db_ref[...] = dsum

        @pl.when(j > 0)
        def _():
            db_ref[...] += dsum

    return pl.pallas_call(
        body, name=name, grid=(N_CHUNK,),
        in_specs=[_bs((BLOCK, BLOCK), lambda j: (rj(j), raw_blk)), _bs((1, BLOCK), lambda j: (0, 0)),
                  _bs((BLOCK, BLOCK), lambda j: (rj(j), 0)), _bs((BLOCK, BLOCK), lambda j: (0, rj(j)))],
        out_specs=[_bs((BLOCK, BLOCK), lambda j: (rj(j), 0)), _bs((1, BLOCK), lambda j: (0, 0))],
        out_shape=[jax.ShapeDtypeStruct((LP, BLOCK), F32), jax.ShapeDtypeStruct((1, BLOCK), F32)],
        scratch_shapes=[pltpu.VMEM((1, BLOCK), F32)], compiler_params=_cparams(("arbitrary",)),
    )(raw, bias, dc, dct)


ATT_W = 256
ATT_QB = 272
ATT_STEPS = LP // ATT_QB
ATT_KEYS = (640, 1152, 1664, LP)


def _lane_head(width, per, mod=None):
    lane = lax.broadcasted_iota(jnp.int32, (1, width), 1)
    if mod is not None:
        lane = lane % mod
    return lane // per


def _attn_mask(i, kw):
    r = i * ATT_QB + lax.broadcasted_iota(jnp.int32, (ATT_QB, kw), 0)
    c = lax.broadcasted_iota(jnp.int32, (ATT_QB, kw), 1)
    return (c <= r) & ((c >= PAD_ROWS) | (r < PAD_ROWS))


def _attn_by_key_class(i, fn):
    for p, kw in enumerate(ATT_KEYS):
        @pl.when(i // 2 == p)
        def _(kw=kw):
            fn(kw)


def _attn_specs(q, k, v, bias, rope):
    qspec = lambda blk, w=ATT_W: _bs((ATT_QB, w), lambda i: (i, blk))
    fspec = lambda blk, w=ATT_W: _bs((LP, w), lambda i: (0, blk))
    ins = [q[0], k[0], v[0]]
    specs = [qspec(q[1]), fspec(k[1]), fspec(v[1])]
    if bias is not None:
        ins += [bias[0], bias[1]]
        specs += [qspec(0, BLOCK), _bs((BLOCK, LP), lambda i: (0, 0))]
    if rope is not None:
        ins += [rope[0][0], rope[1][0]]
        specs += [qspec(rope[0][1], BLOCK), fspec(rope[1][1], BLOCK)]
    return ins, specs, qspec, fspec


def _attn_fwd(q, k, v, *, scale, name, bias=None, rope=None):
    ins, specs, qspec, fspec = _attn_specs(q, k, v, bias, rope)
    has_bias, has_rope = bias is not None, rope is not None

    def body(*refs):
        it = iter(refs)
        q_ref, k_ref, v_ref = next(it), next(it), next(it)
        if has_bias:
            c_ref, ct_ref = next(it), next(it)
        if has_rope:
            qr_ref, kr_ref = next(it), next(it)
        o_ref, lse_ref = next(it), next(it)
        i = pl.program_id(0)

        def block(kw):
            ok = _attn_mask(i, kw)
            qv, kv, vv = q_ref[...], k_ref[0:kw, :], v_ref[0:kw, :]
            hid, l128 = _lane_head(ATT_W, FOX_HD), _lane_head(BLOCK, 1)
            if has_rope:
                rid = _lane_head(BLOCK, ROPE_HALF, 64)
                qrv, krv = qr_ref[...], kr_ref[0:kw, :]
            def head(h, carry):
                o_acc, lse_acc = carry
                s = _raw_bdot(jnp.where(hid == h, qv, 0.0), kv, 1, 1)
                if has_rope:
                    s = s + _raw_bdot(jnp.where(rid == h, qrv, 0.0), krv, 1, 1)
                s = s * scale
                if has_bias:
                    cq = jnp.sum(jnp.where(l128 == h, c_ref[...], 0.0), axis=1, keepdims=True)
                    s = s + (cq - ct_ref[pl.ds(h, 1), 0:kw])
                s = jnp.where(ok, s, NEG)
                m = jnp.max(s, axis=1, keepdims=True)
                p = jnp.exp(s - m)
                l = jnp.sum(p, axis=1, keepdims=True)
                o_acc = jnp.where(hid == h, _raw_bdot(p, vv, 1, 0) / l, o_acc)
                lse_acc = jnp.where(l128 == h, m + jnp.log(l), lse_acc)
                return o_acc, lse_acc

            o_acc, lse_acc = lax.fori_loop(
                0, FOX_HEADS, head, (jnp.zeros((ATT_QB, ATT_W), F32), jnp.zeros((ATT_QB, BLOCK), F32)), unroll=True)
            o_ref[...] = o_acc
            lse_ref[...] = lse_acc

        _attn_by_key_class(i, block)

    return pl.pallas_call(
        body, name=name, grid=(ATT_STEPS,), in_specs=specs, out_specs=[qspec(0), qspec(0, BLOCK)],
        out_shape=[jax.ShapeDtypeStruct((LP, ATT_W), F32), jax.ShapeDtypeStruct((LP, BLOCK), F32)],
        compiler_params=_cparams(("parallel",)),
    )(*ins)


def _attn_bwd(q, k, v, o, lse, do, *, scale, name, bias=None, rope=None):
    ins, specs, qspec, fspec = _attn_specs(q, k, v, bias, rope)
    has_bias, has_rope = bias is not None, rope is not None
    ins += [o, lse, do[0]]
    specs += [qspec(0), qspec(0, BLOCK), qspec(do[1])]

    def body(*refs):
        it = iter(refs)
        q_ref, k_ref, v_ref = next(it), next(it), next(it)
        if has_bias:
            c_ref, ct_ref = next(it), next(it)
        if has_rope:
            qr_ref, kr_ref = next(it), next(it)
        o_ref, lse_ref, do_ref = next(it), next(it), next(it)
        dq_ref, dk_ref, dv_ref = next(it), next(it), next(it)
        if has_bias:
            dc_ref, dct_ref = next(it), next(it)
        if has_rope:
            dqr_ref, dkr_ref = next(it), next(it)
        i = pl.program_id(0)

        @pl.when(i == 0)
        def _():
            dk_ref[...] = jnp.zeros_like(dk_ref)
            dv_ref[...] = jnp.zeros_like(dv_ref)
            if has_rope:
                dkr_ref[...] = jnp.zeros_like(dkr_ref)
            if has_bias:
                dct_ref[...] = jnp.zeros_like(dct_ref)

        def block(kw):
            ok = _attn_mask(i, kw)
            qv, kv, vv = q_ref[...], k_ref[0:kw, :], v_ref[0:kw, :]
            ov, dov, lsev = o_ref[...], do_ref[...], lse_ref[...]
            hid, l128 = _lane_head(ATT_W, FOX_HD), _lane_head(BLOCK, 1)
            if has_rope:
                rid = _lane_head(BLOCK, ROPE_HALF, 64)
                qrv, krv = qr_ref[...], kr_ref[0:kw, :]

            def head(h, carry):
                dq_acc, aux_acc = carry
                qm = jnp.where(hid == h, qv, 0.0)
                s = _raw_bdot(qm, kv, 1, 1)
                if has_rope:
                    qrm = jnp.where(rid == h, qrv, 0.0)
                    s = s + _raw_bdot(qrm, krv, 1, 1)
                s = s * scale
                if has_bias:
                    cq = jnp.sum(jnp.where(l128 == h, c_ref[...], 0.0), axis=1, keepdims=True)
                    s = s + (cq - ct_ref[pl.ds(h, 1), 0:kw])
                s = jnp.where(ok, s, NEG)
                p = jnp.exp(s - jnp.sum(jnp.where(l128 == h, lsev, 0.0), axis=1, keepdims=True))
                dom = jnp.where(hid == h, dov, 0.0)
                dp = _raw_bdot(dom, vv, 1, 1)
                delta = jnp.sum(dom * ov, axis=1, keepdims=True)
                ds = p * (dp - delta)
                dq_acc = jnp.where(hid == h, _raw_bdot(ds, kv, 1, 0) * scale, dq_acc)
                dk_ref[0:kw, :] += _raw_bdot(ds, qm, 0, 0) * scale
                dv_ref[0:kw, :] += _raw_bdot(p, dom, 0, 0)
                if has_rope:
                    aux_acc = jnp.where(rid == h, _raw_bdot(ds, krv, 1, 0) * scale, aux_acc)
                    dkr_ref[0:kw, :] += _raw_bdot(ds, qrm, 0, 0) * scale
                if has_bias:
                    aux_acc = jnp.where(l128 == h, jnp.sum(ds, axis=1, keepdims=True), aux_acc)
                    dct_ref[pl.ds(h, 1), 0:kw] -= jnp.sum(ds, axis=0, keepdims=True)
                return dq_acc, aux_acc

            dq_acc, aux_acc = lax.fori_loop(
                0, FOX_HEADS, head, (jnp.zeros((ATT_QB, ATT_W), F32), jnp.zeros((ATT_QB, BLOCK), F32)))
            dq_ref[...] = dq_acc
            if has_bias:
                dc_ref[...] = aux_acc
            if has_rope:
                dqr_ref[...] = aux_acc

        _attn_by_key_class(i, block)

    wide = jax.ShapeDtypeStruct((LP, ATT_W), F32)
    narrow = jax.ShapeDtypeStruct((LP, BLOCK), F32)
    out_specs = [qspec(0), fspec(0), fspec(0)]
    out_shape = [wide, wide, wide]
    if has_bias:
        out_specs += [qspec(0, BLOCK), _bs((BLOCK, LP), lambda i: (0, 0))]
        out_shape += [narrow, jax.ShapeDtypeStruct((BLOCK, LP), F32)]
    if has_rope:
        out_specs += [qspec(0, BLOCK), fspec(0, BLOCK)]
        out_shape += [narrow, narrow]
    return pl.pallas_call(
        body, name=name, grid=(ATT_STEPS,), in_specs=specs, out_specs=out_specs, out_shape=out_shape,
        compiler_params=_cparams(("arbitrary",)),
    )(*ins)


def _loss_head(y, target, *, name):
    tile = 272

    def body(y_ref, t_ref, dy_ref, loss_ref):
        i = pl.program_id(0)
        rows = i * tile + lax.broadcasted_iota(jnp.int32, (tile, D_MODEL), 0)
        err = jnp.where(rows >= BLOCK, y_ref[...] - t_ref[...], 0.0)
        dy_ref[...] = err * (1.0 / D_MODEL)
        part = 0.5 * jnp.sum(jnp.sum(err * err, axis=1, keepdims=True) * (1.0 / D_MODEL), axis=0, keepdims=True)
        part = jnp.broadcast_to(part, (1, BLOCK))

        @pl.when(i == 0)
        def _():
            loss_ref[...] = part

        @pl.when(i > 0)
        def _():
            loss_ref[...] += part

    return pl.pallas_call(
        body, name=name, grid=(LP // tile,),
        in_specs=[_bs((tile, D_MODEL), lambda i: (i, 0)), _bs((tile, D_MODEL), lambda i: (i, 0))],
        out_specs=[_bs((tile, D_MODEL), lambda i: (i, 0)), _bs((1, BLOCK), lambda i: (0, 0))],
        out_shape=[jax.ShapeDtypeStruct((LP, D_MODEL), F32), jax.ShapeDtypeStruct((1, BLOCK), F32)],
        compiler_params=_cparams(("arbitrary",)),
    )(y, target)


def _adamw(w, gs, m, v, *, name, after=()):
    if w.ndim == 2:
        w, m, v = w[None], m[None], v[None]
        squeeze = True
    else:
        squeeze = False
    NL, R, C = w.shape
    assert len(gs) == NL
    CG = gs[0].shape[1]
    tile = _tile(R, 256, 8)

    def body(*refs):
        w_ref, g_refs = refs[0], refs[1:1 + NL]
        m_ref, v_ref = refs[1 + NL:3 + NL]
        go_ref, d_ref, nm_ref, nv_ref = refs[3 + NL + len(after):]
        gv = g_refs[0][:, :C]
        for j in range(1, NL):
            gv = jnp.where(pl.program_id(0) == j, g_refs[j][:, :C], gv)
        nm = ADAM_B1 * m_ref[...] + (1.0 - ADAM_B1) * gv
        nv = ADAM_B2 * v_ref[...] + (1.0 - ADAM_B2) * (gv * gv)
        m_hat = nm / (1.0 - ADAM_B1 ** ADAM_STEP)
        v_hat = nv / (1.0 - ADAM_B2 ** ADAM_STEP)
        go_ref[...] = gv
        d_ref[...] = -ADAM_LR * (m_hat / (jnp.sqrt(v_hat) + ADAM_EPS) + ADAM_WD * w_ref[...])
        nm_ref[...] = nm
        nv_ref[...] = nv

    spec = _bs((None, tile, C), lambda l, i: (l, i, 0))
    gspecs = [_bs((tile, CG), lambda l, i, j=j: (jnp.where(l == j, i, 0), 0)) for j in range(NL)]
    res = pl.pallas_call(
        body, name=name, grid=(NL, R // tile), in_specs=[spec, *gspecs, spec, spec, *[ANY] * len(after)],
        out_specs=[spec] * 4, out_shape=[jax.ShapeDtypeStruct((NL, R, C), F32)] * 4,
        compiler_params=_cparams(("parallel", "parallel")),
    )(w, *gs, m, v, *after)
    return [r[0] for r in res] if squeeze else res


def _my_pos():
    return lax.axis_index("x"), lax.axis_index("y"), lax.axis_index("c")


def _other_chips(x, y):
    return [(1 - x, y), (x, 1 - y), (1 - x, 1 - y)]


def _allgather_chips(shards):
    n = len(shards)
    per = 7

    def body(*refs):
        ins, outs = refs[:n], refs[n:2 * n]
        send_sems, recv_sems = refs[2 * n], refs[2 * n + 1]
        x, y, c = _my_pos()
        chips = _other_chips(x, y)
        sibling, me = (x, y, 1 - c), 2 * x + y

        def cp(a, kk, src, dst, to):
            return pltpu.make_async_remote_copy(src_ref=src, dst_ref=dst, send_sem=send_sems.at[per * a + kk],
                                                recv_sem=recv_sems.at[per * a + kk], device_id=to, device_id_type=MESH)

        sends = []
        for a in range(n):
            for j, chip in enumerate(chips):
                sends.append(cp(a, j, ins[a].at[c], outs[a].at[me, c], (*chip, c)))
            sends.append(cp(a, 3, ins[a], outs[a].at[me], sibling))
        for s in sends:
            s.start()
        for a in range(n):
            for j, chip in enumerate(chips):
                slab = outs[a].at[2 * chip[0] + chip[1], c]
                cp(a, j, slab, slab, (x, y, c)).wait_recv()
                fwd = cp(a, 4 + j, slab, slab, sibling)
                fwd.start()
                sends.append(fwd)
        for a in range(n):
            cp(a, 3, ins[a], outs[a].at[me], (x, y, c)).wait_recv()
            for j, chip in enumerate(chips):
                slab = outs[a].at[2 * chip[0] + chip[1], 1 - c]
                cp(a, 4 + j, slab, slab, (x, y, c)).wait_recv()
        for s in sends:
            s.wait_send()

    return pl.pallas_call(
        body, name="allgather_chips", in_specs=[ANY] * n, out_specs=[ANY] * n,
        out_shape=[jax.ShapeDtypeStruct((N_CHIPS,) + s.shape, s.dtype) for s in shards],
        scratch_shapes=[pltpu.SemaphoreType.DMA((per * n,)), pltpu.SemaphoreType.DMA((per * n,))],
    )(*shards)


def _rs_swap_rows(gs, tag):
    n = len(gs)

    def body(*refs):
        ins, outs = refs[:n], refs[n:2 * n]
        send_sems, recv_sems = refs[2 * n], refs[2 * n + 1]
        x, y, c = _my_pos()
        cps = []
        for a in range(n):
            half = ins[a].shape[1] // 2
            cps.append(pltpu.make_async_remote_copy(
                src_ref=ins[a].at[:, pl.ds((1 - c) * half, half)], dst_ref=outs[a], send_sem=send_sems.at[a],
                recv_sem=recv_sems.at[a], device_id=(x, y, 1 - c), device_id_type=MESH))
        for cp in cps:
            cp.start()
        for cp in cps:
            cp.wait()

    return pl.pallas_call(
        body, name=f"rs_swap_rows_{tag}", in_specs=[ANY] * n, out_specs=[ANY] * n,
        out_shape=[jax.ShapeDtypeStruct((N_CHIPS, g.shape[1] // 2, g.shape[2]), g.dtype) for g in gs],
        scratch_shapes=[pltpu.SemaphoreType.DMA((n,)), pltpu.SemaphoreType.DMA((n,))],
    )(*gs)


RS_ADD_VMEM_BYTES = 24 * 1024 * 1024


def _rs_tile(H, C, n):
    return _tile(H, max(16, RS_ADD_VMEM_BYTES // (28 * n * C)), 16)


def _rs_add_pair(gs, rs, pos, *, name):
    n = len(gs)
    _, H, C = rs[0].shape
    tile = _rs_tile(H, C, n)
    nt = H // tile

    def body(pos_ref, *refs):
        for a in range(n):
            s = refs[a][...] + refs[n + a][...]
            refs[2 * n + 2 * a][...] = s
            refs[2 * n + 2 * a + 1][...] = s.astype(BF16)

    spec = _bs((None, tile, C), lambda k, i, pos_ref: (k, i, 0))
    g_spec = _bs((None, tile, C), lambda k, i, pos_ref: (k, pos_ref[1] * nt + i, 0))
    grid_spec = pltpu.PrefetchScalarGridSpec(
        num_scalar_prefetch=1, grid=(N_CHIPS, nt), in_specs=[g_spec] * n + [spec] * n, out_specs=[spec] * (2 * n))
    res = pl.pallas_call(
        body, name=name, grid_spec=grid_spec,
        out_shape=[jax.ShapeDtypeStruct((N_CHIPS, H, C), F32), jax.ShapeDtypeStruct((N_CHIPS, H, C), BF16)] * n,
        compiler_params=_cparams(("parallel", "parallel")),
    )(pos, *gs, *rs)
    return [(res[2 * a], res[2 * a + 1]) for a in range(n)]


def _exchange_copies(srcs, lands, send_sems, recv_sems):
    x, y, c = _my_pos()
    starts, landing = [], []
    for a in range(len(srcs)):
        for j, chip in enumerate(_other_chips(x, y)):
            sems = dict(send_sem=send_sems.at[3 * a + j], recv_sem=recv_sems.at[3 * a + j], device_id_type=MESH)
            starts.append(pltpu.make_async_remote_copy(
                src_ref=srcs[a].at[2 * chip[0] + chip[1]], dst_ref=lands[a].at[j], device_id=(*chip, c), **sems))
            landing.append(pltpu.make_async_remote_copy(
                src_ref=lands[a].at[j], dst_ref=lands[a].at[j], device_id=(x, y, c), **sems))
    return starts, landing


def _gather_copies(srcs, lands, l, send_sems, recv_sems):
    x, y, c = _my_pos()
    me = 2 * x + y
    starts, landing = [], []
    for a in range(len(srcs)):
        half = srcs[a].shape[1] // 2
        mine = pl.ds(c * half, half)
        for j, chip in enumerate(_other_chips(x, y)):
            sems = dict(send_sem=send_sems.at[3 * a + j], recv_sem=recv_sems.at[3 * a + j], device_id_type=MESH)
            starts.append(pltpu.make_async_remote_copy(
                src_ref=srcs[a].at[l, mine], dst_ref=lands[a].at[me, mine], device_id=(*chip, c), **sems))
            slab = lands[a].at[2 * chip[0] + chip[1], mine]
            landing.append(pltpu.make_async_remote_copy(src_ref=slab, dst_ref=slab, device_id=(x, y, c), **sems))
    return starts, landing


HBM = pl.BlockSpec(memory_space=pltpu.HBM)
SEM = pl.BlockSpec(memory_space=pltpu.SEMAPHORE)


def _ici_start(copies_fn, srcs, land_shapes, *, name, after=()):
    n, na = len(srcs), len(after)

    def body(*refs):
        starts, _ = copies_fn(refs[:n], refs[n:2 * n], refs[2 * n + na], refs[2 * n + na + 1])
        for cp in starts:
            cp.start()
        refs[-1][...] = jnp.zeros_like(refs[-1])

    sems = pltpu.SemaphoreType.DMA((3 * n,))
    hbm = lambda s: pltpu.HBM(s.shape, s.dtype)
    lands = [pltpu.with_memory_space_constraint(lax.empty(s.shape, s.dtype), pltpu.HBM) for s in land_shapes]
    res = pl.pallas_call(
        body, name=name, in_specs=[HBM] * (2 * n) + [ANY] * na,
        out_specs=(SEM, SEM, *[HBM] * (2 * n), pl.BlockSpec(memory_space=pltpu.VMEM)),
        out_shape=(sems, sems, *[hbm(s) for s in srcs], *[hbm(s) for s in land_shapes],
                   jax.ShapeDtypeStruct((8, BLOCK), F32)),
        input_output_aliases={i: 2 + i for i in range(2 * n)},
        compiler_params=pltpu.CompilerParams(has_side_effects=pltpu.SideEffectType.DATAFLOW_SIDE_EFFECTING),
    )(*[pltpu.with_memory_space_constraint(s, pltpu.HBM) for s in srcs], *lands, *after)
    return res[0], res[1], list(res[2:2 + n]), list(res[2 + n:2 + 2 * n]), res[-1]


def _ici_wait(copies_fn, send_sems, recv_sems, srcs, lands, after, *, name):
    n = len(srcs)

    def body(*refs):
        starts, landing = copies_fn(refs[:n], refs[n:2 * n], refs[2 * n], refs[2 * n + 1])
        for cp in starts:
            cp.wait_send()
        for cp in landing:
            cp.wait_recv()

    hbm = lambda s: pltpu.HBM(s.shape, s.dtype)
    res = pl.pallas_call(
        body, name=name, in_specs=[*[HBM] * (2 * n), SEM, SEM, ANY], out_specs=[HBM] * (2 * n),
        out_shape=[*[hbm(s) for s in srcs], *[hbm(s) for s in lands]],
        input_output_aliases={i: i for i in range(2 * n)},
        compiler_params=pltpu.CompilerParams(has_side_effects=pltpu.SideEffectType.DATAFLOW_SIDE_EFFECTING),
    )(*srcs, *lands, send_sems, recv_sems, after)
    return list(res[:n]), list(res[n:])


def _gather_d2d(shards, lands, l, tag):
    n = len(shards)

    def body(*refs):
        ins, outs = refs[:n], refs[2 * n:3 * n]
        send_sems, recv_sems = refs[3 * n], refs[3 * n + 1]
        x, y, c = _my_pos()
        me, sibling = 2 * x + y, (x, y, 1 - c)
        starts, landing = [], []
        for a in range(n):
            half = ins[a].shape[1] // 2
            mine, theirs = pl.ds(c * half, half), pl.ds((1 - c) * half, half)
            pairs = [(ins[a].at[l], outs[a].at[me], outs[a].at[me])]
            for chip in _other_chips(x, y):
                k = 2 * chip[0] + chip[1]
                pairs.append((outs[a].at[k, mine], outs[a].at[k, mine], outs[a].at[k, theirs]))
            for j, (src, dst, lands_here) in enumerate(pairs):
                sems = dict(send_sem=send_sems.at[4 * a + j], recv_sem=recv_sems.at[4 * a + j], device_id_type=MESH)
                starts.append(pltpu.make_async_remote_copy(src_ref=src, dst_ref=dst, device_id=sibling, **sems))
                landing.append(pltpu.make_async_remote_copy(src_ref=lands_here, dst_ref=lands_here, device_id=(x, y, c),
                                                            **sems))
        for cp in starts:
            cp.start()
        for cp in landing:
            cp.wait_recv()
        for cp in starts:
            cp.wait_send()

    return pl.pallas_call(
        body, name=f"gather_d2d_{tag}", in_specs=[ANY] * (2 * n), out_specs=[ANY] * n,
        out_shape=[jax.ShapeDtypeStruct(s.shape, s.dtype) for s in lands],
        input_output_aliases={n + a: a for a in range(n)},
        scratch_shapes=[pltpu.SemaphoreType.DMA((4 * n,)), pltpu.SemaphoreType.DMA((4 * n,))],
    )(*shards, *lands)


def _rs_add_chips(p32s, r16s, pos, *, name):
    n = len(p32s)
    _, H, C = p32s[0].shape
    tile = _rs_tile(H, C, n)
    nt = H // tile

    def body(pos_ref, *refs):
        for a in range(n):
            p_ref, r_ref = refs[a], refs[n + a]
            refs[2 * n + a][...] = ((p_ref[...] + r_ref[0].astype(F32)) + r_ref[1].astype(F32)) + r_ref[2].astype(F32)

    grid_spec = pltpu.PrefetchScalarGridSpec(
        num_scalar_prefetch=1, grid=(nt,),
        in_specs=[_bs((None, tile, C), lambda i, pos_ref: (pos_ref[0], i, 0))] * n
        + [_bs((3, tile, C), lambda i, pos_ref: (0, i, 0))] * n,
        out_specs=[_bs((tile, C), lambda i, pos_ref: (pos_ref[1] * nt + i, 0))] * n)
    return pl.pallas_call(
        body, name=name, grid_spec=grid_spec, out_shape=[jax.ShapeDtypeStruct((2 * H, C), F32)] * n,
        compiler_params=_cparams(("parallel",)),
    )(pos, *p32s, *r16s)


def _rs_join_rows(fs, tag):
    n = len(fs)

    def body(*refs):
        outs = refs[n:2 * n]
        send_sems, recv_sems = refs[2 * n], refs[2 * n + 1]
        x, y, c = _my_pos()
        for a in range(n):
            half = outs[a].shape[0] // 2
            mine = outs[a].at[pl.ds(c * half, half)]
            pltpu.make_async_remote_copy(src_ref=mine, dst_ref=mine, send_sem=send_sems.at[a],
                                         recv_sem=recv_sems.at[a], device_id=(x, y, 1 - c), device_id_type=MESH).start()
        for a in range(n):
            half = outs[a].shape[0] // 2
            pltpu.make_async_remote_copy(
                src_ref=outs[a].at[pl.ds(c * half, half)], dst_ref=outs[a].at[pl.ds((1 - c) * half, half)],
                send_sem=send_sems.at[a], recv_sem=recv_sems.at[a], device_id=(x, y, 1 - c), device_id_type=MESH).wait()

    return pl.pallas_call(
        body, name=f"rs_join_rows_{tag}", in_specs=[ANY] * n, out_specs=[ANY] * n,
        out_shape=[jax.ShapeDtypeStruct(f.shape, f.dtype) for f in fs],
        input_output_aliases={a: a for a in range(n)},
        scratch_shapes=[pltpu.SemaphoreType.DMA((n,)), pltpu.SemaphoreType.DMA((n,))],
    )(*fs)


def _pos_vector():
    x, y, c = _my_pos()
    return jnp.stack([2 * x + y, c]).astype(jnp.int32)


def _swap_copies(srcs, lands, send_sems, recv_sems):
    x, y, c = _my_pos()
    starts, landing = [], []
    for a in range(len(srcs)):
        half = srcs[a].shape[1] // 2
        sems = dict(send_sem=send_sems.at[3 * a], recv_sem=recv_sems.at[3 * a], device_id_type=MESH)
        starts.append(pltpu.make_async_remote_copy(
            src_ref=srcs[a].at[:, pl.ds((1 - c) * half, half)], dst_ref=lands[a], device_id=(x, y, 1 - c), **sems))
        landing.append(pltpu.make_async_remote_copy(src_ref=lands[a], dst_ref=lands[a], device_id=(x, y, c), **sems))
    return starts, landing


def _swap_land_shapes(gs):
    return [jax.ShapeDtypeStruct((N_CHIPS, g.shape[1] // 2, g.shape[2]), g.dtype) for g in gs]


def _same_shape_runs(arrays):
    runs, start = [], 0
    for i in range(1, len(arrays) + 1):
        if i == len(arrays) or arrays[i].shape != arrays[start].shape:
            runs.append((start, i))
            start = i
    return runs


def _rs_add_pairs(gs, r1, names, tag):
    pos = _pos_vector()
    out = []
    for a, b in _same_shape_runs(gs):
        out += _rs_add_pair(gs[a:b], r1[a:b], pos, name=f"rs_add_pair_{tag}_{names[a]}")
    return out


def _rs_pair_sums(gs, names, tag):
    return _rs_add_pairs(gs, _rs_swap_rows(gs, tag), names, tag)


def _rs_finish(pairs, r2, names, tag):
    pos = _pos_vector()
    p32s = [p[0] for p in pairs]
    fs = []
    for a, b in _same_shape_runs(p32s):
        fs += _rs_add_chips(p32s[a:b], r2[a:b], pos, name=f"rs_add_chips_{tag}_{names[a]}")
    return _rs_join_rows(fs, tag)


def _exchange_land_shapes(pairs):
    return [jax.ShapeDtypeStruct((3,) + p[1].shape[1:], p[1].dtype) for p in pairs]


def _allreduce_small(buf):
    R, W = buf.shape

    def body(b_ref, o_ref, gather, send_sems, recv_sems):
        x, y, c = _my_pos()
        me = 4 * x + 2 * y + c
        gather[me] = b_ref[...]
        cps = []
        for d in range(1, 8):
            peer = (x ^ (d >> 2), y ^ ((d >> 1) & 1), c ^ (d & 1))
            cps.append(pltpu.make_async_remote_copy(
                src_ref=b_ref, dst_ref=gather.at[me], send_sem=send_sems.at[d - 1], recv_sem=recv_sems.at[d - 1],
                device_id=peer, device_id_type=MESH))
        for cp in cps:
            cp.start()
        for d in range(1, 8):
            pltpu.make_async_remote_copy(
                src_ref=b_ref, dst_ref=gather.at[me ^ d], send_sem=send_sems.at[d - 1], recv_sem=recv_sems.at[d - 1],
                device_id=(x, y, c), device_id_type=MESH).wait_recv()
        for cp in cps:
            cp.wait_send()
        acc = gather[0]
        for d in range(1, 8):
            acc = acc + gather[d]
        o_ref[...] = acc

    vm = pl.BlockSpec(memory_space=pltpu.VMEM)
    return pl.pallas_call(
        body, name="allreduce_small", in_specs=[vm], out_specs=vm, out_shape=jax.ShapeDtypeStruct((R, W), F32),
        scratch_shapes=[pltpu.VMEM((8, R, W), F32), pltpu.SemaphoreType.DMA((7,)), pltpu.SemaphoreType.DMA((7,))],
    )(buf)


def _heads(a, h, d):
    return a.reshape(a.shape[0], h, d).transpose(1, 0, 2)


def _unheads(a):
    h, L, d = a.shape
    return a.transpose(1, 0, 2).reshape(L, h * d)


def _rope_tables():
    pos = jnp.maximum(jnp.arange(LP, dtype=F32) - PAD_ROWS, 0.0)
    inv_freq = 1.0 / (ROPE_THETA ** (jnp.arange(0, MLA_ROPE, 2, dtype=F32) / MLA_ROPE))
    ang = pos[:, None] * inv_freq[None, :]
    cos, sin = jnp.tile(jnp.cos(ang), (1, MLA_HEADS)), jnp.tile(jnp.sin(ang), (1, MLA_HEADS))
    return jnp.concatenate([cos, cos], axis=1), jnp.concatenate([-sin, sin], axis=1)


def _lane_pad(a, width=BLOCK):
    return jnp.pad(a, ((0, 0), (0, width - a.shape[1])))


def _pad_in_proj(w):
    sl = lambda start, size: w[:, start:start + size]
    return jnp.concatenate([
        sl(OC_Z, 512), sl(OC_XBC, 768), sl(OC_FQ, 256), sl(OC_FK, 256), sl(OC_FV, 256), sl(OC_CQ, 256), sl(OC_CKV, 128),
        _lane_pad(sl(OC_DT, SSD_HEADS)), _lane_pad(sl(OC_FR, FOX_HEADS)),
        jnp.tile(sl(OC_KR, ROPE_HALF), (1, MLA_HEADS)), jnp.tile(sl(OC_KR + ROPE_HALF, ROPE_HALF), (1, MLA_HEADS))], axis=1)


def _unpad_in_proj(wp):
    sl = lambda start, size: wp[:, start:start + size]
    rope = lambda start: sl(start, 64).reshape(wp.shape[0], MLA_HEADS, ROPE_HALF).sum(axis=1)
    return jnp.concatenate([
        sl(PC_Z, 512), sl(PC_XBC, 768), sl(PC_DT, SSD_HEADS), sl(PC_FQ, 256), sl(PC_FK, 256), sl(PC_FV, 256),
        sl(PC_FR, FOX_HEADS), sl(PC_CQ, 256), sl(PC_CKV, 128), rope(PC_KR), rope(PC_KR + 64)], axis=1)


def _regroup_uq(w):
    w3 = w.reshape(w.shape[0], MLA_HEADS, MLA_NOPE + MLA_ROPE)
    return jnp.concatenate([w3[:, :, :MLA_NOPE].reshape(w.shape[0], -1),
                            w3[:, :, MLA_NOPE:MLA_NOPE + ROPE_HALF].reshape(w.shape[0], -1),
                            w3[:, :, MLA_NOPE + ROPE_HALF:].reshape(w.shape[0], -1)], axis=1)


def _ungroup_uq(wp):
    n = wp.shape[0]
    return jnp.concatenate([wp[:, :256].reshape(n, MLA_HEADS, MLA_NOPE), wp[:, 256:320].reshape(n, MLA_HEADS, ROPE_HALF),
                            wp[:, 320:].reshape(n, MLA_HEADS, ROPE_HALF)], axis=2).reshape(n, -1)


def _regroup_ukv(w):
    w3 = w.reshape(w.shape[0], MLA_HEADS, MLA_NOPE + MLA_V)
    return jnp.concatenate([w3[:, :, :MLA_NOPE].reshape(w.shape[0], -1), w3[:, :, MLA_NOPE:].reshape(w.shape[0], -1)],
                           axis=1)


def _ungroup_ukv(wp):
    n = wp.shape[0]
    return jnp.concatenate([wp[:, :256].reshape(n, MLA_HEADS, MLA_NOPE), wp[:, 256:].reshape(n, MLA_HEADS, MLA_V)],
                           axis=2).reshape(n, -1)


TMF = 1088
N_IF = LP // TMF


def _chunk_rows_dx(g, w, l, chunk_h, *, name):
    N = w.shape[2]
    return _mm_core(g, w, a_spec=_bs((TMF, N), lambda i, j, k: (i, 0)),
                    b_spec=_bs((None, chunk_h, N), lambda i, j, k: (j, 0, 0)),
                    o_spec=_bs((TMF, chunk_h), lambda i, j, k: (i, j)), grid=(N_IF, N_CHIPS, 1),
                    out_shape=(LP, N_CHIPS * chunk_h), ca=1, cb=1, name=name)


def _chunk_rows_dw(a, g, chunk_h, *, name):
    N = g.shape[1]
    return _mm_core(a, g, a_spec=_bs((LP, chunk_h), lambda i, j, k: (0, i)), b_spec=_bs((LP, N), lambda i, j, k: (0, 0)),
                    o_spec=_bs((None, chunk_h, N), lambda i, j, k: (i, 0, 0)), grid=(N_CHIPS, 1, 1),
                    out_shape=(N_CHIPS, chunk_h, N), ca=0, cb=0, name=name)


def _ffn_up_swiglu(h, wg, wu, *, name):
    def body(h_ref, wg_ref, wu_ref, g_ref, u_ref, a_ref):
        hb = h_ref[...].astype(BF16)
        g = _raw_bdot(hb, wg_ref[...], 1, 1)
        u = _raw_bdot(hb, wu_ref[...], 1, 1)
        g_ref[...] = g
        u_ref[...] = u
        a_ref[...] = (_silu(g) * u).astype(a_ref.dtype)

    w_spec = _bs((None, HP, D_MODEL), lambda i, j: (j, 0, 0))
    o_spec = _bs((TMF, HP), lambda i, j: (i, j))
    return pl.pallas_call(
        body, name=name, grid=(N_IF, N_CHIPS), in_specs=[_bs((TMF, D_MODEL), lambda i, j: (i, 0)), w_spec, w_spec],
        out_specs=[o_spec] * 3,
        out_shape=[jax.ShapeDtypeStruct((LP, FP), F32), jax.ShapeDtypeStruct((LP, FP), F32),
                   jax.ShapeDtypeStruct((LP, FP), BF16)],
        compiler_params=_cparams(("parallel", "parallel")),
    )(h, wg, wu)


def _ffn_down_dx_swiglu(do, wd, g, u, *, name):
    def body(do_ref, wd_ref, g_ref, u_ref, dg_ref, du_ref):
        dact = _raw_bdot(do_ref[...], wd_ref[...], 1, 1)
        gv = g_ref[...]
        sig = _sigmoid(gv)
        dg_ref[...] = (dact * u_ref[...] * (sig * (1.0 + gv * (1.0 - sig)))).astype(dg_ref.dtype)
        du_ref[...] = (dact * (gv * sig)).astype(du_ref.dtype)

    blk = _bs((TMF, HP), lambda i, j: (i, j))
    return pl.pallas_call(
        body, name=name, grid=(N_IF, N_CHIPS),
        in_specs=[_bs((TMF, D_MODEL), lambda i, j: (i, 0)), _bs((None, HP, D_MODEL), lambda i, j: (j, 0, 0)), blk, blk],
        out_specs=[blk, blk], out_shape=[jax.ShapeDtypeStruct((LP, FP), BF16)] * 2,
        compiler_params=_cparams(("parallel", "parallel")),
    )(do, wd, g, u)


def _ffn_gate_up_dx(dg, du, wg, wu, add, *, name):
    def body(dg_ref, du_ref, wg_ref, wu_ref, add_ref, o_ref, acc_ref):
        k = pl.program_id(1)

        @pl.when(k == 0)
        def _():
            acc_ref[...] = jnp.zeros_like(acc_ref)

        acc_ref[...] += _raw_bdot(dg_ref[...], wg_ref[...], 1, 0) + _raw_bdot(du_ref[...], wu_ref[...], 1, 0)

        @pl.when(k == N_CHIPS - 1)
        def _():
            o_ref[...] = acc_ref[...] + add_ref[...]

    a_spec = _bs((TMF, HP), lambda i, k: (i, k))
    w_spec = _bs((None, HP, D_MODEL), lambda i, k: (k, 0, 0))
    o_spec = _bs((TMF, D_MODEL), lambda i, k: (i, 0))
    return pl.pallas_call(
        body, name=name, grid=(N_IF, N_CHIPS), in_specs=[a_spec, a_spec, w_spec, w_spec, o_spec], out_specs=o_spec,
        out_shape=jax.ShapeDtypeStruct((LP, D_MODEL), F32), scratch_shapes=[pltpu.VMEM((TMF, D_MODEL), F32)],
        compiler_params=_cparams(("parallel", "arbitrary")),
    )(dg, du, wg, wu, add)


def _chunk_rows_mm_res_ln(a, w, chunk_h, h, gam, bet, scale, *, name):
    res_ln = _make_res_ln_fn(scale)

    def body(a_ref, w_ref, h_ref, g_ref, b_ref, o_ref, y_ref, acc_ref):
        k = pl.program_id(1)

        @pl.when(k == 0)
        def _():
            acc_ref[...] = jnp.zeros_like(acc_ref)

        acc_ref[...] += _raw_bdot(a_ref[...], w_ref[...], 1, 0)

        @pl.when(k == N_CHIPS - 1)
        def _():
            o = acc_ref[...]
            o_ref[...] = o
            (y_ref[...],) = res_ln(0, h_ref[...], o, g_ref[...], b_ref[...])

    row = _bs((TMF, D_MODEL), lambda i, k: (i, 0))
    par = _bs((1, D_MODEL), lambda i, k: (0, 0))
    return pl.pallas_call(
        body, name=name, grid=(N_IF, N_CHIPS),
        in_specs=[_bs((TMF, chunk_h), lambda i, k: (i, k)), _bs((None, chunk_h, D_MODEL), lambda i, k: (k, 0, 0)), row,
                  par, par],
        out_specs=[row, row], out_shape=[jax.ShapeDtypeStruct((LP, D_MODEL), F32)] * 2,
        scratch_shapes=[pltpu.VMEM((TMF, D_MODEL), F32)], compiler_params=_cparams(("parallel", "arbitrary")),
    )(a, w, h, gam, bet)


def _ffn_fwd(h, W, pre, l, gam, bet, tag):
    g, u, act = _ffn_up_swiglu(h, W[pre + "_w_gate"][l], W[pre + "_w_up"][l], name=f"{tag}_up_swiglu")
    o, out = _chunk_rows_mm_res_ln(act, W[pre + "_w_down"][l], HP, h, gam, bet, 0.5, name=f"{tag}_down_ln")
    return out, (h, g, u, act, o)


def _ffn_bwd(dout, saved, W, pre, l, gam, bet, GB, tag):
    h, g, u, act, o = saved
    (dh_a, do), (dgam, dbet) = _rowwise_bwd(_make_res_ln_fn(0.5), [h, o], [gam, bet], [dout], name=f"{tag}_ln_bwd",
                                            tile=272, grad_dtypes=[F32, BF16])
    dg, du = _ffn_down_dx_swiglu(do, W[pre + "_w_down"][l], g, u, name=f"{tag}_down_dx_swiglu")
    GB[pre + "_w_down"] = _chunk_rows_dw(act, do, HP, name=f"{tag}_down_dw")
    GB[pre + "_w_gate"] = _chunk_rows_dw(dg, h, HP, name=f"{tag}_gate_dw")
    GB[pre + "_w_up"] = _chunk_rows_dw(du, h, HP, name=f"{tag}_up_dw")
    dh = _ffn_gate_up_dx(dg, du, W[pre + "_w_gate"][l], W[pre + "_w_up"][l], dh_a, name=f"{tag}_gate_up_dx")
    return dh, dgam, dbet


def _mixer_fwd(h1, W, l, cosf, sins):
    tag = f"l{l}"
    proj = _mm(h1, W["w_in_p"][l], name=f"{tag}_in_proj")
    sv = {"h1": h1, "proj": proj}
    conv_w, conv_b = W["conv_w"][l], W["conv_b"][l][None]
    xc = _conv_fwd(proj, PC_XBC // BLOCK, conv_w, conv_b, name=f"{tag}_conv")
    dt_bias = _lane_pad(W["dt_bias"][l][None])
    (dt,) = _rowwise(_ssd_pre_fn, [(proj, BLOCK, PC_DT // BLOCK)], [dt_bias], [BLOCK], name=f"{tag}_ssd_dt", tile=272)
    xh = _heads(xc[:, :SSD_D], SSD_HEADS, SSD_HD)
    bm = _heads(xc[:, SSD_D:SSD_D + 128], SSD_GROUPS, SSD_STATE)
    cm = _heads(xc[:, SSD_D + 128:], SSD_GROUPS, SSD_STATE)
    dt8 = dt[:, :SSD_HEADS].T
    dtc, dtr = dt8[:, :, None], dt8[:, None, :]
    alog = jnp.broadcast_to(W["a_log"][l][:, None, None], (SSD_HEADS, 1, BLOCK))
    yh, prevs = _ssd_fwd(xh, bm, cm, dtc, dtr, alog, name=f"{tag}_ssd")
    y_raw = _unheads(yh)
    dskip = jnp.repeat(W["d_skip"][l], SSD_HD)[None]
    normg = W["ssd_norm_g"][l][None]
    post_rows = [y_raw, (xc, 256, 0), (proj, 256, PC_Z // 256)]
    (y_ssd,) = _rowwise(_ssd_post_fn, post_rows, [dskip, normg], [SSD_D], name=f"{tag}_ssd_post", tile=272,
                        ncol=SSD_GROUPS)
    sv.update(conv_w=conv_w, conv_b=conv_b, dt_bias=dt_bias, xh=xh, bm=bm, cm=cm, dtc=dtc, dtr=dtr, alog=alog,
              prevs=prevs, post_rows=post_rows, dskip=dskip, normg=normg)
    f_b = _lane_pad(W["fox_f_b"][l][None])
    cg, cgt = _fox_gate_fwd(proj, PC_FR // BLOCK, f_b, name=f"{tag}_fox_gate")
    fox_qkv = ((proj, PC_FQ // ATT_W), (proj, PC_FK // ATT_W), (proj, PC_FV // ATT_W))
    y_fox, lse_f = _attn_fwd(*fox_qkv, scale=FOX_HD ** -0.5, name=f"{tag}_fox_attn", bias=(cg, cgt))
    sv.update(f_b=f_b, cg=cg, cgt=cgt, fox_qkv=fox_qkv, y_fox=y_fox, lse_f=lse_f)
    gq, gkv = W["mla_q_norm_g"][l][None], W["mla_kv_norm_g"][l][None]
    norm_rows = [(proj, 256, PC_CQ // 256), (proj, BLOCK, PC_CKV // BLOCK)]
    qn, cn = _rowwise(_mla_norm_fn, norm_rows, [gq, gkv], [MLA_Q_LORA, MLA_KV_LORA], name=f"{tag}_mla_norm", tile=272,
                      out_dtypes=[BF16, BF16])
    qh = _mm(qn, W["mla_w_uq_p"][l], name=f"{tag}_mla_uq")
    kvh = _mm(cn, W["mla_w_ukv_p"][l], name=f"{tag}_mla_ukv")
    qr, kr = _rowwise(_rope_fn, [(qh, BLOCK, 2), (proj, BLOCK, PC_KR // BLOCK), cosf, sins], [], [BLOCK, BLOCK],
                      name=f"{tag}_rope", tile=272)
    mla_qkv = ((qh, 0), (kvh, 0), (kvh, 1))
    y_mla, lse_m = _attn_fwd(*mla_qkv, scale=(MLA_NOPE + MLA_ROPE) ** -0.5, name=f"{tag}_mla_attn",
                             rope=((qr, 0), (kr, 0)))
    sv.update(gq=gq, gkv=gkv, norm_rows=norm_rows, qn=qn, cn=cn, qr=qr, kr=kr, mla_qkv=mla_qkv, y_mla=y_mla, lse_m=lse_m)
    ycat = jnp.concatenate([y_ssd, y_fox, y_mla], axis=1).astype(BF16)
    mix, h2 = _chunk_rows_mm_res_ln(ycat, W["w_out"][l], 256, h1, W["ln2_g"][l][None], W["ln2_b"][l][None], 1.0,
                                    name=f"{tag}_out_proj_ln2")
    sv.update(mix=mix, ycat=ycat)
    return h2, sv


def _mixer_bwd(dh2, sv, W, l, cosf, sins, GB):
    tag = f"l{l}"
    G = {}
    proj = sv["proj"]
    ln2g, ln2b = W["ln2_g"][l][None], W["ln2_b"][l][None]
    (dh1_a, dmix), (dln2g, dln2b) = _rowwise_bwd(
        _make_res_ln_fn(1.0), [sv["h1"], sv["mix"]], [ln2g, ln2b], [dh2], name=f"{tag}_ln2_bwd", tile=272,
        grad_dtypes=[F32, BF16])
    G["ln2_g"], G["ln2_b"] = dln2g[0], dln2b[0]
    dycat = _chunk_rows_dx(dmix, W["w_out"][l], l, 256, name=f"{tag}_out_proj_dx")
    GB["w_out"] = _chunk_rows_dw(sv["ycat"], dmix, 256, name=f"{tag}_out_proj_dw")
    (dy_raw, dxs_a, dz), (ddskip, dnormg) = _rowwise_bwd(
        _ssd_post_fn, sv["post_rows"], [sv["dskip"], sv["normg"]], [dycat[:, :SSD_D]],
        name=f"{tag}_ssd_post_bwd", tile=272, ncol=SSD_GROUPS)
    G["ssd_norm_g"] = dnormg[0]
    G["d_skip"] = ddskip.reshape(SSD_HEADS, SSD_HD).sum(axis=1)
    dxh, dbm, dcm, ddtc, ddtr, dal = _ssd_bwd(sv["xh"], sv["bm"], sv["cm"], sv["dtc"], sv["dtr"], sv["alog"],
                                              sv["prevs"], _heads(dy_raw, SSD_HEADS, SSD_HD), name=f"{tag}_ssd_bwd")
    G["a_log"] = dal[:, 0, 0]
    dxc = jnp.concatenate([dxs_a + _unheads(dxh), _unheads(dbm), _unheads(dcm)], axis=1)
    dxbc, G["conv_w"], dconv_b = _conv_bwd(proj, PC_XBC // BLOCK, sv["conv_w"], sv["conv_b"], dxc,
                                           name=f"{tag}_conv_bwd")
    G["conv_b"] = dconv_b[0]
    ddt = _lane_pad((ddtc[:, :, 0] + ddtr[:, 0, :]).T)
    (ddt_raw,), (ddt_bias,) = _rowwise_bwd(_ssd_pre_fn, [(proj, BLOCK, PC_DT // BLOCK)], [sv["dt_bias"]], [ddt],
                                           name=f"{tag}_ssd_dt_bwd", tile=272)
    G["dt_bias"] = ddt_bias[0, :SSD_HEADS]
    dfq, dfk, dfv, dcg, dcgt = _attn_bwd(*sv["fox_qkv"], sv["y_fox"], sv["lse_f"], (dycat, SSD_D // ATT_W),
                                         scale=FOX_HD ** -0.5, name=f"{tag}_fox_attn_bwd", bias=(sv["cg"], sv["cgt"]))
    df_raw, dfb = _fox_gate_bwd(proj, PC_FR // BLOCK, sv["f_b"], dcg, dcgt, name=f"{tag}_fox_gate_bwd")
    G["fox_f_b"] = dfb[0, :FOX_HEADS]
    dqn_h, dkn_h, dv_h, dqr, dkr = _attn_bwd(
        *sv["mla_qkv"], sv["y_mla"], sv["lse_m"], (dycat, (SSD_D + FOX_D) // ATT_W),
        scale=(MLA_NOPE + MLA_ROPE) ** -0.5, name=f"{tag}_mla_attn_bwd", rope=((sv["qr"], 0), (sv["kr"], 0)))
    dq_rope, dk_rope = _rowwise(_rope_t_fn, [dqr, dkr, cosf, sins], [], [BLOCK, BLOCK], name=f"{tag}_rope_bwd",
                                tile=272)
    dqh = jnp.concatenate([dqn_h, dq_rope], axis=1).astype(BF16)
    dkvh = jnp.concatenate([dkn_h, dv_h], axis=1).astype(BF16)
    dqn = _mm(dqh, W["mla_w_uq_p"][l], tb=True, name=f"{tag}_mla_uq_dx")
    G["mla_w_uq_p"] = _mm(sv["qn"], dqh, ta=True, name=f"{tag}_mla_uq_dw")
    dcn = _mm(dkvh, W["mla_w_ukv_p"][l], tb=True, name=f"{tag}_mla_ukv_dx")
    G["mla_w_ukv_p"] = _mm(sv["cn"], dkvh, ta=True, name=f"{tag}_mla_ukv_dw")
    (dcq, dckv), (dgq, dgkv) = _rowwise_bwd(_mla_norm_fn, sv["norm_rows"], [sv["gq"], sv["gkv"]], [dqn, dcn],
                                            name=f"{tag}_mla_norm_bwd", tile=272)
    G["mla_q_norm_g"], G["mla_kv_norm_g"] = dgq[0], dgkv[0]
    dproj = jnp.concatenate([dz, dxbc, dfq, dfk, dfv, dcq, dckv, ddt_raw, df_raw, dk_rope], axis=1).astype(BF16)
    dh1 = _mm(dproj, W["w_in_p"][l], tb=True, add=dh1_a, name=f"{tag}_in_proj_dx")
    G["w_in_p"] = _mm(sv["h1"], dproj, ta=True, name=f"{tag}_in_proj_dw")
    return dh1, G


def _embed(x, meta):
    return jnp.concatenate([jnp.zeros((PAD_ROWS, D_MODEL), F32), meta, x], axis=0)


def _layer_fwd(h, W, l, cosf, sins):
    ln = lambda n: W[n][l][None]
    h1, s1 = _ffn_fwd(h, W, "ffn1", l, ln("ln1_g"), ln("ln1_b"), f"l{l}_ffn1")
    h2, sm = _mixer_fwd(h1, W, l, cosf, sins)
    h3, s2 = _ffn_fwd(h2, W, "ffn2", l, ln("ln3_g"), ln("ln3_b"), f"l{l}_ffn2")
    return h3, (s1, sm, s2)


def _layer_bwd(dh, saved, W, l, cosf, sins):
    ln = lambda n: W[n][l][None]
    s1, sm, s2 = saved
    G = {}
    dh, dg, db = _ffn_bwd(dh, s2, W, "ffn2", l, ln("ln3_g"), ln("ln3_b"), G, f"l{l}_ffn2")
    G["ln3_g"], G["ln3_b"] = dg[0], db[0]
    dh, Gm = _mixer_bwd(dh, sm, W, l, cosf, sins, G)
    G.update(Gm)
    dh, dg, db = _ffn_bwd(dh, s1, W, "ffn1", l, ln("ln1_g"), ln("ln1_b"), G, f"l{l}_ffn1")
    G["ln1_g"], G["ln1_b"] = dg[0], db[0]
    return dh, G


def _local_step(x, target, W):
    h = _embed(x, W["meta"])
    tgt = jnp.concatenate([jnp.zeros((BLOCK, D_MODEL), F32), target], axis=0)
    cosf, sins = _rope_tables()
    saved = []
    for l in range(DEPTH):
        h, sv = _layer_fwd(h, W, l, cosf, sins)
        saved.append(sv)
    dh, loss = _loss_head(h, tgt, name="loss_head")
    grads = [None] * DEPTH
    for l in reversed(range(DEPTH)):
        dh, grads[l] = _layer_bwd(dh, saved[l], W, l, cosf, sins)
    return loss, dh, grads


WEIGHTS = ['meta', 'ffn1_w_gate', 'ffn1_w_up', 'ffn1_w_down', 'ln1_g', 'ln1_b', 'w_in', 'conv_w', 'conv_b', 'dt_bias',
           'a_log', 'd_skip', 'ssd_norm_g', 'fox_f_b', 'mla_q_norm_g', 'mla_w_uq', 'mla_kv_norm_g', 'mla_w_ukv',
           'w_out', 'ln2_g', 'ln2_b', 'ffn2_w_gate', 'ffn2_w_up', 'ffn2_w_down', 'ln3_g', 'ln3_b']
SMALL = ["ln1_g", "ln1_b", "conv_b", "dt_bias", "a_log", "d_skip", "ssd_norm_g", "fox_f_b", "mla_q_norm_g",
         "mla_kv_norm_g", "ln2_g", "ln2_b", "ln3_g", "ln3_b"]
MATMUL_W = ["ffn1_w_gate", "ffn1_w_up", "ffn1_w_down", "w_in", "mla_w_uq", "mla_w_ukv", "w_out", "ffn2_w_gate",
            "ffn2_w_up", "ffn2_w_down"]
SMALL_ROWS = 312


def _pad_to(a, axis, size):
    pads = [(0, 0)] * a.ndim
    pads[axis] = (0, size - a.shape[axis])
    return jnp.pad(a, pads)


def _chip_cols(full, chip, width):
    return lax.dynamic_slice_in_dim(full, chip * width, width, axis=full.ndim - 1)


def kernel(x, meta, ffn1_w_gate, ffn1_w_up, ffn1_w_down, ln1_g, ln1_b, w_in, conv_w, conv_b, dt_bias, a_log, d_skip, ssd_norm_g, fox_f_b, mla_q_norm_g, mla_w_uq, mla_kv_norm_g, mla_w_ukv, w_out, ln2_g, ln2_b, ffn2_w_gate, ffn2_w_up, ffn2_w_down, ln3_g, ln3_b, loss_target, m_meta, m_ffn1_w_gate, m_ffn1_w_up, m_ffn1_w_down, m_ln1_g, m_ln1_b, m_w_in, m_conv_w, m_conv_b, m_dt_bias, m_a_log, m_d_skip, m_ssd_norm_g, m_fox_f_b, m_mla_q_norm_g, m_mla_w_uq, m_mla_kv_norm_g, m_mla_w_ukv, m_w_out, m_ln2_g, m_ln2_b, m_ffn2_w_gate, m_ffn2_w_up, m_ffn2_w_down, m_ln3_g, m_ln3_b, v_meta, v_ffn1_w_gate, v_ffn1_w_up, v_ffn1_w_down, v_ln1_g, v_ln1_b, v_w_in, v_conv_w, v_conv_b, v_dt_bias, v_a_log, v_d_skip, v_ssd_norm_g, v_fox_f_b, v_mla_q_norm_g, v_mla_w_uq, v_mla_kv_norm_g, v_mla_w_ukv, v_w_out, v_ln2_g, v_ln2_b, v_ffn2_w_gate, v_ffn2_w_up, v_ffn2_w_down, v_ln3_g, v_ln3_b):
    args = dict(locals())
    w = {n: args[n] for n in WEIGHTS}
    m = {n: args["m_" + n] for n in WEIGHTS}
    v = {n: args["v_" + n] for n in WEIGHTS}
    xcoord, ycoord, _ = _my_pos()
    chip = 2 * xcoord + ycoord

    tr = lambda a: jnp.swapaxes(a, 1, 2)

    def bf16_shard(n, zero=None):
        a = w[n] if zero is None else w[n] + zero
        if n.endswith("w_gate") or n.endswith("w_up"):
            a = _pad_to(tr(a), 1, HP)
        elif n.endswith("w_down"):
            a = _pad_to(a, 1, HP)
        elif n == "w_in":
            a = _pad_to(a, 2, IN_SHARD_P)
        return a.astype(BF16)

    land_shape = lambda s: jax.ShapeDtypeStruct((N_CHIPS,) + s.shape[1:], s.dtype)
    gather_l = lambda l: (lambda srcs, lands, ss, rs: _gather_copies(srcs, lands, l, ss, rs))
    tiny = _allgather_chips([w["meta"].reshape(2, N_META // 2, D_MODEL // N_CHIPS), w["conv_w"]])
    meta_full = jnp.concatenate([tiny[0][k].reshape(N_META, D_MODEL // N_CHIPS) for k in range(N_CHIPS)], axis=1)

    W = {n: [None] * DEPTH for n in MATMUL_W + ["w_in_p", "mla_w_uq_p", "mla_w_ukv_p"]}
    W["conv_w"] = jnp.concatenate([tiny[1][k] for k in range(N_CHIPS)], axis=-1)
    W["meta"] = meta_full
    for n in SMALL:
        W[n] = w[n]

    def use_gathered(l, names, lands):
        got = dict(zip(names, lands))
        cat = lambda n, cut=None: jnp.concatenate([got[n][k][..., :cut] for k in range(N_CHIPS)], axis=-1)
        for n in names:
            W[n][l] = got[n]
        if "w_in" in got:
            W["w_in_p"][l] = _pad_in_proj(cat("w_in", IN_SHARD))
            W["mla_w_uq_p"][l] = _regroup_uq(cat("mla_w_uq"))
            W["mla_w_ukv_p"][l] = _regroup_ukv(cat("mla_w_ukv"))

    def chunk_grads(G, names):
        def chunked(name, ungroup, width, pad):
            full = ungroup(G[name])
            return _pad_to(jnp.moveaxis(full.reshape(full.shape[0], N_CHIPS, width), 1, 0), 2, pad)
        special = {"w_in": ("w_in_p", _unpad_in_proj, IN_SHARD, IN_SHARD_P),
                   "mla_w_uq": ("mla_w_uq_p", _ungroup_uq, MLA_NOPE + MLA_ROPE, MLA_NOPE + MLA_ROPE),
                   "mla_w_ukv": ("mla_w_ukv_p", _ungroup_ukv, MLA_NOPE + MLA_V, MLA_NOPE + MLA_V)}
        return [chunked(*special[n]) if n in special else G[n] for n in names]

    def rs_start(G, names, tag):
        pairs = _rs_pair_sums(chunk_grads(G, names), names, tag)
        handle = _ici_start(_exchange_copies, [p[1] for p in pairs], _exchange_land_shapes(pairs),
                            name=f"rs_exchange_{tag}_start")
        return pairs, handle

    def swap_start(G, names, tag):
        gs = chunk_grads(G, names)
        return _ici_start(_swap_copies, gs, _swap_land_shapes(gs), name=f"rs_swap_{tag}_start")

    def exchange_start(swap_handle, names, tag, after):
        gs, r1 = _ici_wait(_swap_copies, *swap_handle[:4], after, name=f"rs_swap_{tag}_wait")
        pairs = _rs_add_pairs(gs, r1, names, tag)
        handle = _ici_start(_exchange_copies, [p[1] for p in pairs], _exchange_land_shapes(pairs),
                            name=f"rs_exchange_{tag}_start")
        return pairs, handle

    def rs_end(pairs, handle, names, tag, after):
        _, r2 = _ici_wait(_exchange_copies, *handle[:4], after, name=f"rs_exchange_{tag}_wait")
        return dict(zip(names, _rs_finish(pairs, r2, names, tag)))

    na = 3
    first, rest = MATMUL_W[:na], MATMUL_W[na:]
    shards = [bf16_shard(n) for n in first]
    g_send, g_recv, g_srcs, g_lands, token = _ici_start(gather_l(0), shards, [land_shape(s) for s in shards],
                                                        name="gather_ici_l0_ffn1_start", after=[tiny[0]])
    shards_rest = [bf16_shard(n, token[0, 0]) for n in rest]
    shards, lands = _ici_wait(gather_l(0), g_send, g_recv, g_srcs, g_lands, shards_rest[0],
                              name="gather_ici_l0_ffn1_wait")
    shards = shards + shards_rest
    land_shapes = [land_shape(s) for s in shards]
    got = _gather_d2d(shards[:na], lands, 0, "l0_ffn1")
    use_gathered(0, first, got)
    g_send, g_recv, g_srcs, g_lands, token = _ici_start(gather_l(0), shards[na:], land_shapes[na:],
                                                        name="gather_ici_l0_rest_start", after=[got[0]])
    cosf, sins = _rope_tables()
    ln = lambda n, l: W[n][l][None]
    h = _embed(x[0] + token[0, 0], meta_full)
    h1, s1 = _ffn_fwd(h, W, "ffn1", 0, ln("ln1_g", 0), ln("ln1_b", 0), "l0_ffn1")
    rest_shards, lands = _ici_wait(gather_l(0), g_send, g_recv, g_srcs, g_lands, h1, name="gather_ici_l0_rest_wait")
    shards = shards[:na] + rest_shards
    got = _gather_d2d(shards[na:], lands, 0, "l0_rest")
    use_gathered(0, rest, got)
    g_send, g_recv, g_srcs, g_lands, token = _ici_start(gather_l(1), shards, land_shapes, name="gather_ici_l1_start",
                                                        after=[got[0]])
    h2, sm = _mixer_fwd(h1 + token[0, 0], W, 0, cosf, sins)
    h, s2 = _ffn_fwd(h2, W, "ffn2", 0, ln("ln3_g", 0), ln("ln3_b", 0), "l0_ffn2")
    saved0 = (s1, sm, s2)
    shards, lands = _ici_wait(gather_l(1), g_send, g_recv, g_srcs, g_lands, h, name="gather_ici_l1_wait")
    use_gathered(1, MATMUL_W, _gather_d2d(shards, lands, 1, "l1"))
    h, saved1 = _layer_fwd(h, W, 1, cosf, sins)
    tgt = jnp.concatenate([jnp.zeros((BLOCK, D_MODEL), F32), loss_target[0]], axis=0)
    dh, loss = _loss_head(h, tgt, name="loss_head")
    G = [None] * DEPTH
    dh, G[1] = _layer_bwd(dh, saved1, W, 1, cosf, sins)
    ffn2_w, mix_w, ffn1_w = MATMUL_W[7:], MATMUL_W[3:7], MATMUL_W[:3]
    sw_l1 = swap_start(G[1], MATMUL_W, "l1")
    G0 = {}
    dh, dg, db = _ffn_bwd(dh + sw_l1[4][0, 0], s2, W, "ffn2", 0, ln("ln3_g", 0), ln("ln3_b", 0), G0, "l0_ffn2")
    G0["ln3_g"], G0["ln3_b"] = dg[0], db[0]
    pairs_l1, x_l1 = exchange_start(sw_l1, MATMUL_W, "l1", dh)
    sw_a = swap_start(G0, ffn2_w, "l0_ffn2")
    dh, Gm = _mixer_bwd(dh + (x_l1[4][0, 0] + sw_a[4][0, 0]), sm, W, 0, cosf, sins, G0)
    G0.update(Gm)
    pairs_a, x_a = exchange_start(sw_a, ffn2_w, "l0_ffn2", dh)
    reduced1 = rs_end(pairs_l1, x_l1, MATMUL_W, "l1", dh)
    pairs_b, x_b = rs_start(G0, mix_w, "l0_mix")
    dh0, dg, db = _ffn_bwd(dh + (x_a[4][0, 0] + x_b[4][0, 0]), s1, W, "ffn1", 0, ln("ln1_g", 0), ln("ln1_b", 0), G0,
                           "l0_ffn1")
    G0["ln1_g"], G0["ln1_b"] = dg[0], db[0]
    G[0] = G0
    reduced0 = rs_end(pairs_a, x_a, ffn2_w, "l0_ffn2", dh0)
    reduced0.update(rs_end(pairs_b, x_b, mix_w, "l0_mix", dh0))

    small_parts = [jnp.stack([G[l][n] for l in range(DEPTH)]).reshape(-1) for n in SMALL]
    small_parts += [jnp.stack([G[l]["conv_w"] for l in range(DEPTH)]).reshape(-1), dh0[PAD_ROWS:BLOCK].reshape(-1),
                    loss[0, :1]]
    flat = jnp.concatenate(small_parts)
    flat = jnp.pad(flat, (0, SMALL_ROWS * BLOCK - flat.shape[0]))
    red2d = _allreduce_small(flat.reshape(SMALL_ROWS, BLOCK))
    red = red2d.reshape(-1)

    pairs_c = _rs_pair_sums(chunk_grads(G0, ffn1_w), ffn1_w, "l0_ffn1")
    x_c = _ici_start(_exchange_copies, [p[1] for p in pairs_c], _exchange_land_shapes(pairs_c),
                     name="rs_exchange_l0_ffn1_start", after=[red2d])
    grads, off = {}, 0
    for n in SMALL:
        size = int(np.prod(w[n].shape))
        grads[n] = red[off:off + size].reshape(w[n].shape)
        off += size
    conv_full = red[off:off + DEPTH * SSD_CONV * 768].reshape(DEPTH, SSD_CONV, 768)
    off += DEPTH * SSD_CONV * 768
    dmeta_full = red[off:off + N_META * D_MODEL].reshape(N_META, D_MODEL)
    off += N_META * D_MODEL
    loss_out = red[off]
    grads["conv_w"] = _chip_cols(conv_full, chip, 768 // N_CHIPS)
    grads["meta"] = _chip_cols(dmeta_full, chip, D_MODEL // N_CHIPS)

    delta, new_m, new_v = {}, {}, {}

    def adamw_matmul_weights(names, after):
        for n in names:
            gs = [reduced0[n], reduced1[n]]
            if n.endswith("w_gate") or n.endswith("w_up"):
                res = _adamw(tr(w[n]), gs, tr(m[n]), tr(v[n]), name=f"adamw_{n}", after=after)
                grads[n], delta[n], new_m[n], new_v[n] = [tr(r) for r in res]
            else:
                grads[n], delta[n], new_m[n], new_v[n] = _adamw(w[n], gs, m[n], v[n], name=f"adamw_{n}", after=after)

    adamw_matmul_weights(ffn2_w + mix_w, [x_c[4]])
    rest = [n for n in WEIGHTS if n not in MATMUL_W]

    def pack_small(d):
        f = jnp.concatenate([d[n].reshape(-1) for n in rest])
        tot = -(-f.shape[0] // (8 * BLOCK)) * 8 * BLOCK
        return jnp.pad(f, (0, tot - f.shape[0])).reshape(-1, BLOCK)

    _, d2, m2, v2 = _adamw(pack_small(w), [pack_small(grads)], pack_small(m), pack_small(v), name="adamw_small",
                           after=[x_c[4]])
    reduced0.update(rs_end(pairs_c, x_c, ffn1_w, "l0_ffn1", d2))
    adamw_matmul_weights(ffn1_w, [])
    off = 0
    for n in rest:
        size = int(np.prod(w[n].shape))
        for dst, src in ((delta, d2), (new_m, m2), (new_v, v2)):
            dst[n] = src.reshape(-1)[off:off + size].reshape(w[n].shape)
        off += size

    grad_x = dh0[BLOCK:][None]
    return (loss_out, grad_x, *[grads[n] for n in WEIGHTS], *[delta[n] for n in WEIGHTS],
            *[new_m[n] for n in WEIGHTS], *[new_v[n] for n in WEIGHTS])
```

```python
import functools

import numpy as np
import jax
import jax.numpy as jnp
from jax import lax
from jax.experimental import pallas as pl
from jax.experimental.pallas import tpu as pltpu

F32 = jnp.float32
BF16 = jnp.bfloat16
MESH = pl.DeviceIdType.MESH

D_MODEL = 1024
SEQ = 2048
N_META = 16
BLOCK = 128
PAD_ROWS = 112
LP = PAD_ROWS + N_META + SEQ
N_CHUNK = LP // BLOCK
DEPTH = 2
D_FF = 2816
N_CHIPS = 4
FF_SHARD = D_FF // N_CHIPS
HP = 768
FP = N_CHIPS * HP
SSD_HEADS, SSD_HD, SSD_D, SSD_GROUPS, SSD_STATE, SSD_CONV = 8, 64, 512, 2, 64, 4
FOX_HEADS, FOX_HD, FOX_D = 4, 64, 256
MLA_HEADS, MLA_Q_LORA, MLA_KV_LORA, MLA_NOPE, MLA_ROPE, MLA_V, MLA_D = 4, 256, 128, 64, 32, 64, 256
ROPE_HALF = MLA_ROPE // 2
ROPE_THETA = 10000.0
N_IN = 2476
IN_SHARD = N_IN // N_CHIPS
IN_SHARD_P = 640
ALPHA = (2 * DEPTH) ** 0.25
EPS = 1e-5
ADAM_LR, ADAM_B1, ADAM_B2, ADAM_EPS, ADAM_WD, ADAM_STEP = 0.001, 0.9, 0.999, 1e-08, 0.01, 10
NEG = -1e30
TM = 544

VMEM_LIMIT_BYTES = 56 * 1024 * 1024

PC_Z, PC_XBC, PC_FQ, PC_FK, PC_FV, PC_CQ, PC_CKV, PC_DT, PC_FR, PC_KR, PC_END = (
    0, 512, 1280, 1536, 1792, 2048, 2304, 2432, 2560, 2688, 2816)
OC_Z, OC_XBC, OC_DT, OC_FQ, OC_FK, OC_FV, OC_FR, OC_CQ, OC_CKV, OC_KR = (
    0, 512, 1280, 1288, 1544, 1800, 2056, 2060, 2316, 2444)


def _cparams(sem=None):
    return pltpu.CompilerParams(dimension_semantics=sem, vmem_limit_bytes=VMEM_LIMIT_BYTES)


def _tile(n, cap, mult):
    best = None
    for t in range(mult, min(n, cap) + 1, mult):
        if n % t == 0:
            best = t
    return best if best is not None else n


def _bs(shape, fn):
    return pl.BlockSpec(shape, fn)


ANY = pl.BlockSpec(memory_space=pl.ANY)


def _dims(ca, cb):
    return (((ca,), (cb,)), ((), ()))


def _raw_bdot(a, b, ca, cb):
    return lax.dot_general(a.astype(BF16), b.astype(BF16), _dims(ca, cb), preferred_element_type=F32)


def _mm_core(a, b, *, a_spec, b_spec, o_spec, grid, out_shape, ca, cb, name, add=None, after=()):
    nk = grid[2]
    has_add = add is not None
    acc_shape = tuple(d for d in o_spec.block_shape if d is not None)

    def body(*refs):
        a_ref, b_ref = refs[0], refs[1]
        add_ref = refs[2] if has_add else None
        o_ref, acc_ref = refs[-2], refs[-1]
        k = pl.program_id(2)

        @pl.when(k == 0)
        def _():
            acc_ref[...] = jnp.zeros_like(acc_ref)

        acc_ref[...] += _raw_bdot(a_ref[...], b_ref[...], ca, cb)

        @pl.when(k == nk - 1)
        def _():
            r = acc_ref[...]
            if has_add:
                r = r + add_ref[...]
            o_ref[...] = r

    ins = [a, b] + ([add] if has_add else []) + list(after)
    in_specs = [a_spec, b_spec] + ([o_spec] if has_add else []) + [ANY] * len(after)
    return pl.pallas_call(
        body, name=name, grid=grid, in_specs=in_specs, out_specs=o_spec,
        out_shape=jax.ShapeDtypeStruct(out_shape, F32), scratch_shapes=[pltpu.VMEM(acc_shape, F32)],
        compiler_params=_cparams(("parallel", "parallel", "arbitrary")),
    )(*ins)


MM_VMEM_BUDGET = 40 * 1024 * 1024


def _divisors(n, mult):
    return [t for t in range(mult, n + 1, mult) if n % t == 0] or [n]


def _pick_tiles(M, N, K, a_bytes, b_bytes, ta, has_add):
    best = None
    for tm in _divisors(M, 128 if ta else 16):
        for tn in _divisors(N, 128):
            vmem = 2 * tm * K * a_bytes + 2 * K * tn * b_bytes + (3 + 2 * int(has_add)) * tm * tn * 4
            if vmem <= MM_VMEM_BUDGET:
                key = ((M // tm) * (N // tn), -tn)
                if best is None or key < best[0]:
                    best = (key, tm, tn)
    assert best is not None, (M, N, K)
    return best[1], best[2], K


def _mm(a, b, *, ta=False, tb=False, add=None, name, after=()):
    if ta:
        K, M = a.shape
    else:
        M, K = a.shape
    if tb:
        N, Kb = b.shape
    else:
        Kb, N = b.shape
    assert K == Kb, (a.shape, b.shape, ta, tb)
    tm, tn, tk = _pick_tiles(M, N, K, a.dtype.itemsize, b.dtype.itemsize, ta, add is not None)
    a_spec = _bs((tk, tm), lambda i, j, k: (k, i)) if ta else _bs((tm, tk), lambda i, j, k: (i, k))
    b_spec = _bs((tn, tk), lambda i, j, k: (j, k)) if tb else _bs((tk, tn), lambda i, j, k: (k, j))
    return _mm_core(a, b, a_spec=a_spec, b_spec=b_spec, o_spec=_bs((tm, tn), lambda i, j, k: (i, j)),
                    grid=(M // tm, N // tn, K // tk), out_shape=(M, N), ca=0 if ta else 1, cb=1 if tb else 0,
                    name=name, add=add, after=after)


def _row_entry(r, ncol):
    if isinstance(r, tuple):
        return r
    return r, r.shape[1] // ncol, 0


def _rowwise(fn, rows, pars, out_cols, *, name, tile, ncol=1, out_dtypes=None):
    rows = [_row_entry(r, ncol) for r in rows]
    L = rows[0][0].shape[0]
    nr, npar = len(rows), len(pars)
    in_specs = [_bs((tile, w), lambda g, i, o=o: (i, o + g)) for _, w, o in rows]
    in_specs += [_bs((p.shape[0], p.shape[1] // ncol), lambda g, i: (0, g)) for p in pars]
    out_specs = [_bs((tile, c // ncol), lambda g, i: (i, g)) for c in out_cols]

    def body(*refs):
        ins, outs = refs[:nr + npar], refs[nr + npar:]
        row0 = pl.program_id(1) * tile
        res = fn(row0, *[r[...] for r in ins])
        for o, v in zip(outs, res):
            o[...] = v.astype(o.dtype)

    return pl.pallas_call(
        body, name=name, grid=(ncol, L // tile), in_specs=in_specs, out_specs=out_specs,
        out_shape=[jax.ShapeDtypeStruct((L, c), d) for c, d in zip(out_cols, out_dtypes or [F32] * len(out_cols))],
        compiler_params=_cparams(("parallel", "parallel")),
    )(*[r[0] for r in rows], *pars)


def _rowwise_bwd(fn, rows, pars, douts, *, name, tile, ncol=1, row_grad=None, grad_dtypes=None):
    rows = [_row_entry(r, ncol) for r in rows]
    L = rows[0][0].shape[0]
    nr, npar, nd = len(rows), len(pars), len(douts)
    row_grad = [True] * nr if row_grad is None else row_grad
    in_specs = [_bs((tile, w), lambda g, i, o=o: (i, o + g)) for _, w, o in rows]
    in_specs += [_bs((p.shape[0], p.shape[1] // ncol), lambda g, i: (0, g)) for p in pars]
    in_specs += [_bs((tile, d.shape[1] // ncol), lambda g, i: (i, g)) for d in douts]
    g_widths = [w * ncol for (_, w, _), f in zip(rows, row_grad) if f]
    out_specs = [_bs((tile, w // ncol), lambda g, i: (i, g)) for w in g_widths]
    out_specs += [_bs((p.shape[0], p.shape[1] // ncol), lambda g, i: (0, g)) for p in pars]
    out_shape = [jax.ShapeDtypeStruct((L, w), d) for w, d in zip(g_widths, grad_dtypes or [F32] * len(g_widths))]
    out_shape += [jax.ShapeDtypeStruct(p.shape, F32) for p in pars]

    def body(*refs):
        ins = refs[:nr + npar]
        dos = refs[nr + npar:nr + npar + nd]
        outs = refs[nr + npar + nd:]
        i = pl.program_id(1)
        row0 = i * tile
        _, vjp = jax.vjp(lambda *a: tuple(fn(row0, *a)), *[r[...] for r in ins])
        grads = vjp(tuple(d[...].astype(F32) for d in dos))
        o = 0
        for j in range(nr):
            if row_grad[j]:
                outs[o][...] = grads[j].astype(outs[o].dtype)
                o += 1
        for j in range(npar):
            g, ref = grads[nr + j], outs[o + j]

            @pl.when(i == 0)
            def _(g=g, ref=ref):
                ref[...] = g

            @pl.when(i > 0)
            def _(g=g, ref=ref):
                ref[...] += g

    res = pl.pallas_call(
        body, name=name, grid=(ncol, L // tile), in_specs=in_specs, out_specs=out_specs, out_shape=out_shape,
        compiler_params=_cparams(("parallel", "arbitrary")),
    )(*[r[0] for r in rows], *pars, *douts)
    return res[:len(g_widths)], res[len(g_widths):]


def _row_ids(row0, shape):
    return row0 + lax.broadcasted_iota(jnp.int32, shape, 0)


def _sigmoid(x):
    return 1.0 / (1.0 + jnp.exp(-x))


@jax.custom_vjp
def _softplus(x):
    return jnp.maximum(x, 0.0) + jnp.log(1.0 + jnp.exp(-jnp.abs(x)))


def _softplus_fwd(x):
    return _softplus(x), x


def _softplus_bwd(x, g):
    return (g * _sigmoid(x),)


_softplus.defvjp(_softplus_fwd, _softplus_bwd)


def _silu(x):
    return x * _sigmoid(x)


def _make_res_ln_fn(scale):
    def fn(row0, h, o, gam, bet):
        pre = ALPHA * h + scale * o
        mu = jnp.mean(pre, axis=-1, keepdims=True)
        xc = pre - mu
        var = jnp.mean(xc * xc, axis=-1, keepdims=True)
        return (xc * lax.rsqrt(var + EPS) * gam + bet,)
    return fn


def _ssd_pre_fn(row0, raw, bias):
    dt = _softplus(raw + bias)
    return (jnp.where(_row_ids(row0, raw.shape) >= PAD_ROWS, dt, 0.0),)


def _ssd_post_fn(row0, y, xs, z, dskip, normg):
    v = (y + dskip * xs) * _silu(z)
    v = v * lax.rsqrt(jnp.mean(v * v, axis=-1, keepdims=True) + EPS)
    return (v * normg,)


def _mla_norm_fn(row0, cq, ckv, gq, gkv):
    qn = cq * lax.rsqrt(jnp.mean(cq * cq, axis=-1, keepdims=True) + EPS) * gq
    cn = ckv * lax.rsqrt(jnp.mean(ckv * ckv, axis=-1, keepdims=True) + EPS) * gkv
    return qn, cn


def _rope_fn(row0, q, k, cosf, sins):
    return (q * cosf + pltpu.roll(q, 64, 1) * sins, k * cosf + pltpu.roll(k, 64, 1) * sins)


def _rope_t_fn(row0, gq, gk, cosf, sins):
    return (gq * cosf + pltpu.roll(gq * sins, 64, 1), gk * cosf + pltpu.roll(gk * sins, 64, 1))


def _conv_fwd(x, x_off, w, b, *, name):
    C = w.shape[1]

    def body(x_ref, w_ref, b_ref, o_ref):
        rows = lax.broadcasted_iota(jnp.int32, (LP, BLOCK), 0)
        xv = jnp.where(rows >= PAD_ROWS, x_ref[...], 0.0)
        acc = b_ref[...] + w_ref[3:4, :] * xv
        for k in range(SSD_CONV - 1):
            acc = acc + w_ref[k:k + 1, :] * pltpu.roll(xv, SSD_CONV - 1 - k, 0)
        o_ref[...] = _silu(acc)

    return pl.pallas_call(
        body, name=name, grid=(C // BLOCK,),
        in_specs=[_bs((LP, BLOCK), lambda j: (0, j + x_off)), _bs((SSD_CONV, BLOCK), lambda j: (0, j)),
                  _bs((1, BLOCK), lambda j: (0, j))],
        out_specs=_bs((LP, BLOCK), lambda j: (0, j)),
        out_shape=jax.ShapeDtypeStruct((LP, C), F32), compiler_params=_cparams(("parallel",)),
    )(x, w, b)


def _conv_bwd(x, x_off, w, b, dout, *, name):
    C = w.shape[1]

    def body(x_ref, w_ref, b_ref, do_ref, dx_ref, dw_ref, db_ref):
        rows = lax.broadcasted_iota(jnp.int32, (LP, BLOCK), 0)
        real = rows >= PAD_ROWS
        xv = jnp.where(real, x_ref[...], 0.0)
        shifted = [pltpu.roll(xv, SSD_CONV - 1 - k, 0) for k in range(SSD_CONV - 1)] + [xv]
        acc = b_ref[...]
        for k in range(SSD_CONV):
            acc = acc + w_ref[k:k + 1, :] * shifted[k]
        sig = _sigmoid(acc)
        dacc = jnp.where(real, do_ref[...] * (sig * (1.0 + acc * (1.0 - sig))), 0.0)
        db_ref[...] = jnp.sum(dacc, axis=0, keepdims=True)
        dx = w_ref[3:4, :] * dacc
        for k in range(SSD_CONV):
            dw_ref[k:k + 1, :] = jnp.sum(dacc * shifted[k], axis=0, keepdims=True)
            if k < SSD_CONV - 1:
                dx = dx + w_ref[k:k + 1, :] * pltpu.roll(dacc, LP - (SSD_CONV - 1 - k), 0)
        dx_ref[...] = jnp.where(real, dx, 0.0)

    return pl.pallas_call(
        body, name=name, grid=(C // BLOCK,),
        in_specs=[_bs((LP, BLOCK), lambda j: (0, j + x_off)), _bs((SSD_CONV, BLOCK), lambda j: (0, j)),
                  _bs((1, BLOCK), lambda j: (0, j)), _bs((LP, BLOCK), lambda j: (0, j))],
        out_specs=[_bs((LP, BLOCK), lambda j: (0, j)), _bs((SSD_CONV, BLOCK), lambda j: (0, j)),
                   _bs((1, BLOCK), lambda j: (0, j))],
        out_shape=[jax.ShapeDtypeStruct((LP, C), F32), jax.ShapeDtypeStruct((SSD_CONV, C), F32),
                   jax.ShapeDtypeStruct((1, C), F32)],
        compiler_params=_cparams(("parallel",)),
    )(x, w, b, dout)


_BDIMS = {"nn": (((2,), (1,)), ((0,), (0,))), "nt": (((2,), (2,)), ((0,), (0,))), "tn": (((1,), (1,)), ((0,), (0,)))}


def _raw_bdot3(a, b, mode):
    return lax.dot_general(a.astype(BF16), b.astype(BF16), _BDIMS[mode], preferred_element_type=F32)


@functools.partial(jax.custom_vjp, nondiff_argnums=(2,))
def _bdot3(a, b, mode):
    return _raw_bdot3(a, b, mode)


def _bdot3_fwd(a, b, mode):
    return _raw_bdot3(a, b, mode), (a, b)


def _bdot3_bwd(mode, res, g):
    a, b = res
    if mode == "nn":
        return _raw_bdot3(g, b, "nt"), _raw_bdot3(a, g, "tn")
    if mode == "nt":
        return _raw_bdot3(g, b, "nn"), _raw_bdot3(g, a, "tn")
    return _raw_bdot3(b, g, "nt"), _raw_bdot3(a, g, "nn")


_bdot3.defvjp(_bdot3_fwd, _bdot3_bwd)


def _ssd_chunk(x, bm, cm, dtc, dtr, alog, prev):
    rep = SSD_HEADS // SSD_GROUPS
    per_head = lambda t: jnp.broadcast_to(t[:, None], (SSD_GROUPS, rep) + t.shape[1:]).reshape((SSD_HEADS,) + t.shape[1:])
    bm, cm = per_head(bm), per_head(cm)
    lane = lax.broadcasted_iota(jnp.int32, alog.shape, 2)
    a_neg = -jnp.exp(jnp.sum(jnp.where(lane == 0, alog, 0.0), axis=2, keepdims=True))
    ac_in = dtc * a_neg
    ar_in = dtr * a_neg
    li = lax.broadcasted_iota(jnp.int32, (1, BLOCK, BLOCK), 1)
    si = lax.broadcasted_iota(jnp.int32, (1, BLOCK, BLOCK), 2)
    causal = li >= si
    acum_c = jnp.sum(jnp.where(causal, ar_in, 0.0), axis=2, keepdims=True)
    acum_r = jnp.sum(jnp.where(li <= si, ac_in, 0.0), axis=1, keepdims=True)
    total = jnp.sum(ar_in, axis=2, keepdims=True)
    seg = jnp.exp(jnp.where(causal, acum_c - acum_r, NEG))
    xdt = x * dtc
    cb = _bdot3(cm, bm, "nt")
    y = _bdot3(cb * seg, xdt, "nn") + _bdot3(cm, prev, "nt") * jnp.exp(acum_c)
    st = _bdot3(xdt, bm * jnp.exp(total - acum_c), "tn")
    return y, prev * jnp.exp(total) + st


def _ssd_specs(rev):
    ci = (lambda c: N_CHUNK - 1 - c) if rev else (lambda c: c)
    x_spec = _bs((SSD_HEADS, BLOCK, SSD_HD), lambda c: (0, ci(c), 0))
    g_spec = _bs((SSD_GROUPS, BLOCK, SSD_STATE), lambda c: (0, ci(c), 0))
    dtc_spec = _bs((SSD_HEADS, BLOCK, 1), lambda c: (0, ci(c), 0))
    dtr_spec = _bs((SSD_HEADS, 1, BLOCK), lambda c: (0, 0, ci(c)))
    al_spec = _bs((SSD_HEADS, 1, BLOCK), lambda c: (0, 0, 0))
    st_spec = _bs((None, SSD_HEADS, SSD_HD, SSD_STATE), lambda c: (ci(c), 0, 0, 0))
    return x_spec, g_spec, dtc_spec, dtr_spec, al_spec, st_spec


def _ssd_fwd(x, bm, cm, dtc, dtr, alog, *, name):
    x_spec, g_spec, dtc_spec, dtr_spec, al_spec, st_spec = _ssd_specs(False)

    def body(x_ref, b_ref, c_ref, dtc_ref, dtr_ref, al_ref, y_ref, prev_ref, state):
        @pl.when(pl.program_id(0) == 0)
        def _():
            state[...] = jnp.zeros_like(state)

        prev = state[...]
        prev_ref[...] = prev
        y, new = _ssd_chunk(x_ref[...], b_ref[...], c_ref[...], dtc_ref[...], dtr_ref[...], al_ref[...], prev)
        y_ref[...] = y
        state[...] = new

    return pl.pallas_call(
        body, name=name, grid=(N_CHUNK,),
        in_specs=[x_spec, g_spec, g_spec, dtc_spec, dtr_spec, al_spec], out_specs=[x_spec, st_spec],
        out_shape=[jax.ShapeDtypeStruct((SSD_HEADS, LP, SSD_HD), F32),
                   jax.ShapeDtypeStruct((N_CHUNK, SSD_HEADS, SSD_HD, SSD_STATE), F32)],
        scratch_shapes=[pltpu.VMEM((SSD_HEADS, SSD_HD, SSD_STATE), F32)],
        compiler_params=_cparams(("arbitrary",)),
    )(x, bm, cm, dtc, dtr, alog)


def _ssd_bwd(x, bm, cm, dtc, dtr, alog, prevs, dy, *, name):
    x_spec, g_spec, dtc_spec, dtr_spec, al_spec, st_spec = _ssd_specs(True)

    def body(x_ref, b_ref, c_ref, dtc_ref, dtr_ref, al_ref, prev_ref, dy_ref,
             dx_ref, db_ref, dc_ref, ddtc_ref, ddtr_ref, dal_ref, dstate):
        c = pl.program_id(0)

        @pl.when(c == 0)
        def _():
            dstate[...] = jnp.zeros_like(dstate)

        _, vjp = jax.vjp(_ssd_chunk, x_ref[...], b_ref[...], c_ref[...], dtc_ref[...], dtr_ref[...], al_ref[...],
                         prev_ref[...])
        dx, db, dc, ddtc, ddtr, dal, dprev = vjp((dy_ref[...], dstate[...]))
        dx_ref[...] = dx
        db_ref[...] = db
        dc_ref[...] = dc
        ddtc_ref[...] = ddtc
        ddtr_ref[...] = ddtr
        dstate[...] = dprev

        @pl.when(c == 0)
        def _():
            dal_ref[...] = dal

        @pl.when(c > 0)
        def _():
            dal_ref[...] += dal

    hs = jax.ShapeDtypeStruct((SSD_HEADS, LP, SSD_HD), F32)
    gs = jax.ShapeDtypeStruct((SSD_GROUPS, LP, SSD_STATE), F32)
    return pl.pallas_call(
        body, name=name, grid=(N_CHUNK,),
        in_specs=[x_spec, g_spec, g_spec, dtc_spec, dtr_spec, al_spec, st_spec, x_spec],
        out_specs=[x_spec, g_spec, g_spec, dtc_spec, dtr_spec, al_spec],
        out_shape=[hs, gs, gs, jax.ShapeDtypeStruct((SSD_HEADS, LP, 1), F32),
                   jax.ShapeDtypeStruct((SSD_HEADS, 1, LP), F32), jax.ShapeDtypeStruct((SSD_HEADS, 1, BLOCK), F32)],
        scratch_shapes=[pltpu.VMEM((SSD_HEADS, SSD_HD, SSD_STATE), F32)],
        compiler_params=_cparams(("arbitrary",)),
    )(x, bm, cm, dtc, dtr, alog, prevs, dy)


def _tri_dot(tri, v):
    hi = v.astype(BF16)
    r1 = v - hi.astype(F32)
    mid = r1.astype(BF16)
    lo = (r1 - mid.astype(F32)).astype(BF16)
    t = tri.astype(BF16)
    d = lambda p: lax.dot_general(t, p, _dims(1, 0), preferred_element_type=F32)
    return d(hi) + d(mid) + d(lo)


def _fox_gate_fwd(raw, raw_blk, bias, *, name):
    def body(raw_ref, b_ref, c_ref, ct_ref, carry):
        j = pl.program_id(0)

        @pl.when(j == 0)
        def _():
            carry[...] = jnp.zeros_like(carry)

        rows = j * BLOCK + lax.broadcasted_iota(jnp.int32, (BLOCK, BLOCK), 0)
        lf = jnp.where(rows >= PAD_ROWS, -_softplus(-(raw_ref[...] + b_ref[...])), 0.0)
        li = lax.broadcasted_iota(jnp.int32, (BLOCK, BLOCK), 0)
        si = lax.broadcasted_iota(jnp.int32, (BLOCK, BLOCK), 1)
        cv = _tri_dot(jnp.where(li >= si, 1.0, 0.0), lf) + carry[...]
        c_ref[...] = cv
        ct_ref[...] = cv.T
        carry[...] += jnp.sum(lf, axis=0, keepdims=True)

    return pl.pallas_call(
        body, name=name, grid=(N_CHUNK,),
        in_specs=[_bs((BLOCK, BLOCK), lambda j: (j, raw_blk)), _bs((1, BLOCK), lambda j: (0, 0))],
        out_specs=[_bs((BLOCK, BLOCK), lambda j: (j, 0)), _bs((BLOCK, BLOCK), lambda j: (0, j))],
        out_shape=[jax.ShapeDtypeStruct((LP, BLOCK), F32), jax.ShapeDtypeStruct((BLOCK, LP), F32)],
        scratch_shapes=[pltpu.VMEM((1, BLOCK), F32)], compiler_params=_cparams(("arbitrary",)),
    )(raw, bias)


def _fox_gate_bwd(raw, raw_blk, bias, dc, dct, *, name):
    rj = lambda j: N_CHUNK - 1 - j

    def body(raw_ref, b_ref, dc_ref, dct_ref, draw_ref, db_ref, carry):
        j = pl.program_id(0)

        @pl.when(j == 0)
        def _():
            carry[...] = jnp.zeros_like(carry)

        rows = (N_CHUNK - 1 - j) * BLOCK + lax.broadcasted_iota(jnp.int32, (BLOCK, BLOCK), 0)
        li = lax.broadcasted_iota(jnp.int32, (BLOCK, BLOCK), 0)
        si = lax.broadcasted_iota(jnp.int32, (BLOCK, BLOCK), 1)
        dcv = dc_ref[...] + dct_ref[...].T
        dlf = _tri_dot(jnp.where(li <= si, 1.0, 0.0), dcv) + carry[...]
        carry[...] += jnp.sum(dcv, axis=0, keepdims=True)
        draw = jnp.where(rows >= PAD_ROWS, dlf * (1.0 - _sigmoid(raw_ref[...] + b_ref[...])), 0.0)
        draw_ref[...] = draw
        dsum = jnp.sum(draw, axis=0, keepdims=True)

        @pl.when(j == 0)
        def _():
            db_ref[...] = dsum

        @pl.when(j > 0)
        def _():
            db_ref[...] += dsum

    return pl.pallas_call(
        body, name=name, grid=(N_CHUNK,),
        in_specs=[_bs((BLOCK, BLOCK), lambda j: (rj(j), raw_blk)), _bs((1, BLOCK), lambda j: (0, 0)),
                  _bs((BLOCK, BLOCK), lambda j: (rj(j), 0)), _bs((BLOCK, BLOCK), lambda j: (0, rj(j)))],
        out_specs=[_bs((BLOCK, BLOCK), lambda j: (rj(j), 0)), _bs((1, BLOCK), lambda j: (0, 0))],
        out_shape=[jax.ShapeDtypeStruct((LP, BLOCK), F32), jax.ShapeDtypeStruct((1, BLOCK), F32)],
        scratch_shapes=[pltpu.VMEM((1, BLOCK), F32)], compiler_params=_cparams(("arbitrary",)),
    )(raw, bias, dc, dct)


ATT_W = 256
ATT_QB = 272
ATT_STEPS = LP // ATT_QB
ATT_KEYS = (640, 1152, 1664, LP)


def _lane_head(width, per, mod=None):
    lane = lax.broadcasted_iota(jnp.int32, (1, width), 1)
    if mod is not None:
        lane = lane % mod
    return lane // per


def _attn_mask(i, kw):
    r = i * ATT_QB + lax.broadcasted_iota(jnp.int32, (ATT_QB, kw), 0)
    c = lax.broadcasted_iota(jnp.int32, (ATT_QB, kw), 1)
    return (c <= r) & ((c >= PAD_ROWS) | (r < PAD_ROWS))


def _attn_by_key_class(i, fn):
    for p, kw in enumerate(ATT_KEYS):
        @pl.when(i // 2 == p)
        def _(kw=kw):
            fn(kw)


def _attn_specs(q, k, v, bias, rope):
    qspec = lambda blk, w=ATT_W: _bs((ATT_QB, w), lambda i: (i, blk))
    fspec = lambda blk, w=ATT_W: _bs((LP, w), lambda i: (0, blk))
    ins = [q[0], k[0], v[0]]
    specs = [qspec(q[1]), fspec(k[1]), fspec(v[1])]
    if bias is not None:
        ins += [bias[0], bias[1]]
        specs += [qspec(0, BLOCK), _bs((BLOCK, LP), lambda i: (0, 0))]
    if rope is not None:
        ins += [rope[0][0], rope[1][0]]
        specs += [qspec(rope[0][1], BLOCK), fspec(rope[1][1], BLOCK)]
    return ins, specs, qspec, fspec


def _attn_fwd(q, k, v, *, scale, name, bias=None, rope=None):
    ins, specs, qspec, fspec = _attn_specs(q, k, v, bias, rope)
    has_bias, has_rope = bias is not None, rope is not None

    def body(*refs):
        it = iter(refs)
        q_ref, k_ref, v_ref = next(it), next(it), next(it)
        if has_bias:
            c_ref, ct_ref = next(it), next(it)
        if has_rope:
            qr_ref, kr_ref = next(it), next(it)
        o_ref, lse_ref = next(it), next(it)
        i = pl.program_id(0)

        def block(kw):
            ok = _attn_mask(i, kw)
            qv, kv, vv = q_ref[...], k_ref[0:kw, :], v_ref[0:kw, :]
            hid, l128 = _lane_head(ATT_W, FOX_HD), _lane_head(BLOCK, 1)
            if has_rope:
                rid = _lane_head(BLOCK, ROPE_HALF, 64)
                qrv, krv = qr_ref[...], kr_ref[0:kw, :]
            def head(h, carry):
                o_acc, lse_acc = carry
                s = _raw_bdot(jnp.where(hid == h, qv, 0.0), kv, 1, 1)
                if has_rope:
                    s = s + _raw_bdot(jnp.where(rid == h, qrv, 0.0), krv, 1, 1)
                s = s * scale
                if has_bias:
                    cq = jnp.sum(jnp.where(l128 == h, c_ref[...], 0.0), axis=1, keepdims=True)
                    s = s + (cq - ct_ref[pl.ds(h, 1), 0:kw])
                s = jnp.where(ok, s, NEG)
                m = jnp.max(s, axis=1, keepdims=True)
                p = jnp.exp(s - m)
                l = jnp.sum(p, axis=1, keepdims=True)
                o_acc = jnp.where(hid == h, _raw_bdot(p, vv, 1, 0) / l, o_acc)
                lse_acc = jnp.where(l128 == h, m + jnp.log(l), lse_acc)
                return o_acc, lse_acc

            o_acc, lse_acc = lax.fori_loop(
                0, FOX_HEADS, head, (jnp.zeros((ATT_QB, ATT_W), F32), jnp.zeros((ATT_QB, BLOCK), F32)), unroll=True)
            o_ref[...] = o_acc
            lse_ref[...] = lse_acc

        _attn_by_key_class(i, block)

    return pl.pallas_call(
        body, name=name, grid=(ATT_STEPS,), in_specs=specs, out_specs=[qspec(0), qspec(0, BLOCK)],
        out_shape=[jax.ShapeDtypeStruct((LP, ATT_W), F32), jax.ShapeDtypeStruct((LP, BLOCK), F32)],
        compiler_params=_cparams(("parallel",)),
    )(*ins)


def _attn_bwd(q, k, v, o, lse, do, *, scale, name, bias=None, rope=None):
    ins, specs, qspec, fspec = _attn_specs(q, k, v, bias, rope)
    has_bias, has_rope = bias is not None, rope is not None
    ins += [o, lse, do[0]]
    specs += [qspec(0), qspec(0, BLOCK), qspec(do[1])]

    def body(*refs):
        it = iter(refs)
        q_ref, k_ref, v_ref = next(it), next(it), next(it)
        if has_bias:
            c_ref, ct_ref = next(it), next(it)
        if has_rope:
            qr_ref, kr_ref = next(it), next(it)
        o_ref, lse_ref, do_ref = next(it), next(it), next(it)
        dq_ref, dk_ref, dv_ref = next(it), next(it), next(it)
        if has_bias:
            dc_ref, dct_ref = next(it), next(it)
        if has_rope:
            dqr_ref, dkr_ref = next(it), next(it)
        i = pl.program_id(0)

        @pl.when(i == 0)
        def _():
            dk_ref[...] = jnp.zeros_like(dk_ref)
            dv_ref[...] = jnp.zeros_like(dv_ref)
            if has_rope:
                dkr_ref[...] = jnp.zeros_like(dkr_ref)
            if has_bias:
                dct_ref[...] = jnp.zeros_like(dct_ref)

        def block(kw):
            ok = _attn_mask(i, kw)
            qv, kv, vv = q_ref[...], k_ref[0:kw, :], v_ref[0:kw, :]
            ov, dov, lsev = o_ref[...], do_ref[...], lse_ref[...]
            hid, l128 = _lane_head(ATT_W, FOX_HD), _lane_head(BLOCK, 1)
            if has_rope:
                rid = _lane_head(BLOCK, ROPE_HALF, 64)
                qrv, krv = qr_ref[...], kr_ref[0:kw, :]

            def head(h, carry):
                dq_acc, aux_acc = carry
                qm = jnp.where(hid == h, qv, 0.0)
                s = _raw_bdot(qm, kv, 1, 1)
                if has_rope:
                    qrm = jnp.where(rid == h, qrv, 0.0)
                    s = s + _raw_bdot(qrm, krv, 1, 1)
                s = s * scale
                if has_bias:
                    cq = jnp.sum(jnp.where(l128 == h, c_ref[...], 0.0), axis=1, keepdims=True)
                    s = s + (cq - ct_ref[pl.ds(h, 1), 0:kw])
                s = jnp.where(ok, s, NEG)
                p = jnp.exp(s - jnp.sum(jnp.where(l128 == h, lsev, 0.0), axis=1, keepdims=True))
                dom = jnp.where(hid == h, dov, 0.0)
                dp = _raw_bdot(dom, vv, 1, 1)
                delta = jnp.sum(dom * ov, axis=1, keepdims=True)
                ds = p * (dp - delta)
                dq_acc = jnp.where(hid == h, _raw_bdot(ds, kv, 1, 0) * scale, dq_acc)
                dk_ref[0:kw, :] += _raw_bdot(ds, qm, 0, 0) * scale
                dv_ref[0:kw, :] += _raw_bdot(p, dom, 0, 0)
                if has_rope:
                    aux_acc = jnp.where(rid == h, _raw_bdot(ds, krv, 1, 0) * scale, aux_acc)
                    dkr_ref[0:kw, :] += _raw_bdot(ds, qrm, 0, 0) * scale
                if has_bias:
                    aux_acc = jnp.where(l128 == h, jnp.sum(ds, axis=1, keepdims=True), aux_acc)
                    dct_ref[pl.ds(h, 1), 0:kw] -= jnp.sum(ds, axis=0, keepdims=True)
                return dq_acc, aux_acc

            dq_acc, aux_acc = lax.fori_loop(
                0, FOX_HEADS, head, (jnp.zeros((ATT_QB, ATT_W), F32), jnp.zeros((ATT_QB, BLOCK), F32)))
            dq_ref[...] = dq_acc
            if has_bias:
                dc_ref[...] = aux_acc
            if has_rope:
                dqr_ref[...] = aux_acc

        _attn_by_key_class(i, block)

    wide = jax.ShapeDtypeStruct((LP, ATT_W), F32)
    narrow = jax.ShapeDtypeStruct((LP, BLOCK), F32)
    out_specs = [qspec(0), fspec(0), fspec(0)]
    out_shape = [wide, wide, wide]
    if has_bias:
        out_specs += [qspec(0, BLOCK), _bs((BLOCK, LP), lambda i: (0, 0))]
        out_shape += [narrow, jax.ShapeDtypeStruct((BLOCK, LP), F32)]
    if has_rope:
        out_specs += [qspec(0, BLOCK), fspec(0, BLOCK)]
        out_shape += [narrow, narrow]
    return pl.pallas_call(
        body, name=name, grid=(ATT_STEPS,), in_specs=specs, out_specs=out_specs, out_shape=out_shape,
        compiler_params=_cparams(("arbitrary",)),
    )(*ins)


def _loss_head(y, target, *, name):
    tile = 272

    def body(y_ref, t_ref, dy_ref, loss_ref):
        i = pl.program_id(0)
        rows = i * tile + lax.broadcasted_iota(jnp.int32, (tile, D_MODEL), 0)
        err = jnp.where(rows >= BLOCK, y_ref[...] - t_ref[...], 0.0)
        dy_ref[...] = err * (1.0 / D_MODEL)
        part = 0.5 * jnp.sum(jnp.sum(err * err, axis=1, keepdims=True) * (1.0 / D_MODEL), axis=0, keepdims=True)
        part = jnp.broadcast_to(part, (1, BLOCK))

        @pl.when(i == 0)
        def _():
            loss_ref[...] = part

        @pl.when(i > 0)
        def _():
            loss_ref[...] += part

    return pl.pallas_call(
        body, name=name, grid=(LP // tile,),
        in_specs=[_bs((tile, D_MODEL), lambda i: (i, 0)), _bs((tile, D_MODEL), lambda i: (i, 0))],
        out_specs=[_bs((tile, D_MODEL), lambda i: (i, 0)), _bs((1, BLOCK), lambda i: (0, 0))],
        out_shape=[jax.ShapeDtypeStruct((LP, D_MODEL), F32), jax.ShapeDtypeStruct((1, BLOCK), F32)],
        compiler_params=_cparams(("arbitrary",)),
    )(y, target)


def _adamw(w, gs, m, v, *, name, after=()):
    if w.ndim == 2:
        w, m, v = w[None], m[None], v[None]
        squeeze = True
    else:
        squeeze = False
    NL, R, C = w.shape
    assert len(gs) == NL
    CG = gs[0].shape[1]
    tile = _tile(R, 256, 8)

    def body(*refs):
        w_ref, g_refs = refs[0], refs[1:1 + NL]
        m_ref, v_ref = refs[1 + NL:3 + NL]
        go_ref, d_ref, nm_ref, nv_ref = refs[3 + NL + len(after):]
        gv = g_refs[0][:, :C]
        for j in range(1, NL):
            gv = jnp.where(pl.program_id(0) == j, g_refs[j][:, :C], gv)
        nm = ADAM_B1 * m_ref[...] + (1.0 - ADAM_B1) * gv
        nv = ADAM_B2 * v_ref[...] + (1.0 - ADAM_B2) * (gv * gv)
        m_hat = nm / (1.0 - ADAM_B1 ** ADAM_STEP)
        v_hat = nv / (1.0 - ADAM_B2 ** ADAM_STEP)
        go_ref[...] = gv
        d_ref[...] = -ADAM_LR * (m_hat / (jnp.sqrt(v_hat) + ADAM_EPS) + ADAM_WD * w_ref[...])
        nm_ref[...] = nm
        nv_ref[...] = nv

    spec = _bs((None, tile, C), lambda l, i: (l, i, 0))
    gspecs = [_bs((tile, CG), lambda l, i, j=j: (jnp.where(l == j, i, 0), 0)) for j in range(NL)]
    res = pl.pallas_call(
        body, name=name, grid=(NL, R // tile), in_specs=[spec, *gspecs, spec, spec, *[ANY] * len(after)],
        out_specs=[spec] * 4, out_shape=[jax.ShapeDtypeStruct((NL, R, C), F32)] * 4,
        compiler_params=_cparams(("parallel", "parallel")),
    )(w, *gs, m, v, *after)
    return [r[0] for r in res] if squeeze else res


def _my_pos():
    return lax.axis_index("x"), lax.axis_index("y"), lax.axis_index("c")


def _other_chips(x, y):
    return [(1 - x, y), (x, 1 - y), (1 - x, 1 - y)]


def _allgather_chips(shards):
    n = len(shards)
    per = 7

    def body(*refs):
        ins, outs = refs[:n], refs[n:2 * n]
        send_sems, recv_sems = refs[2 * n], refs[2 * n + 1]
        x, y, c = _my_pos()
        chips = _other_chips(x, y)
        sibling, me = (x, y, 1 - c), 2 * x + y

        def cp(a, kk, src, dst, to):
            return pltpu.make_async_remote_copy(src_ref=src, dst_ref=dst, send_sem=send_sems.at[per * a + kk],
                                                recv_sem=recv_sems.at[per * a + kk], device_id=to, device_id_type=MESH)

        sends = []
        for a in range(n):
            for j, chip in enumerate(chips):
                sends.append(cp(a, j, ins[a].at[c], outs[a].at[me, c], (*chip, c)))
            sends.append(cp(a, 3, ins[a], outs[a].at[me], sibling))
        for s in sends:
            s.start()
        for a in range(n):
            for j, chip in enumerate(chips):
                slab = outs[a].at[2 * chip[0] + chip[1], c]
                cp(a, j, slab, slab, (x, y, c)).wait_recv()
                fwd = cp(a, 4 + j, slab, slab, sibling)
                fwd.start()
                sends.append(fwd)
        for a in range(n):
            cp(a, 3, ins[a], outs[a].at[me], (x, y, c)).wait_recv()
            for j, chip in enumerate(chips):
                slab = outs[a].at[2 * chip[0] + chip[1], 1 - c]
                cp(a, 4 + j, slab, slab, (x, y, c)).wait_recv()
        for s in sends:
            s.wait_send()

    return pl.pallas_call(
        body, name="allgather_chips", in_specs=[ANY] * n, out_specs=[ANY] * n,
        out_shape=[jax.ShapeDtypeStruct((N_CHIPS,) + s.shape, s.dtype) for s in shards],
        scratch_shapes=[pltpu.SemaphoreType.DMA((per * n,)), pltpu.SemaphoreType.DMA((per * n,))],
    )(*shards)


def _rs_swap_rows(gs, tag):
    n = len(gs)

    def body(*refs):
        ins, outs = refs[:n], refs[n:2 * n]
        send_sems, recv_sems = refs[2 * n], refs[2 * n + 1]
        x, y, c = _my_pos()
        cps = []
        for a in range(n):
            half = ins[a].shape[1] // 2
            cps.append(pltpu.make_async_remote_copy(
                src_ref=ins[a].at[:, pl.ds((1 - c) * half, half)], dst_ref=outs[a], send_sem=send_sems.at[a],
                recv_sem=recv_sems.at[a], device_id=(x, y, 1 - c), device_id_type=MESH))
        for cp in cps:
            cp.start()
        for cp in cps:
            cp.wait()

    return pl.pallas_call(
        body, name=f"rs_swap_rows_{tag}", in_specs=[ANY] * n, out_specs=[ANY] * n,
        out_shape=[jax.ShapeDtypeStruct((N_CHIPS, g.shape[1] // 2, g.shape[2]), g.dtype) for g in gs],
        scratch_shapes=[pltpu.SemaphoreType.DMA((n,)), pltpu.SemaphoreType.DMA((n,))],
    )(*gs)


RS_ADD_VMEM_BYTES = 24 * 1024 * 1024


def _rs_tile(H, C, n):
    return _tile(H, max(16, RS_ADD_VMEM_BYTES // (28 * n * C)), 16)


def _rs_add_pair(gs, rs, pos, *, name):
    n = len(gs)
    _, H, C = rs[0].shape
    tile = _rs_tile(H, C, n)
    nt = H // tile

    def body(pos_ref, *refs):
        for a in range(n):
            s = refs[a][...] + refs[n + a][...]
            refs[2 * n + 2 * a][...] = s
            refs[2 * n + 2 * a + 1][...] = s.astype(BF16)

    spec = _bs((None, tile, C), lambda k, i, pos_ref: (k, i, 0))
    g_spec = _bs((None, tile, C), lambda k, i, pos_ref: (k, pos_ref[1] * nt + i, 0))
    grid_spec = pltpu.PrefetchScalarGridSpec(
        num_scalar_prefetch=1, grid=(N_CHIPS, nt), in_specs=[g_spec] * n + [spec] * n, out_specs=[spec] * (2 * n))
    res = pl.pallas_call(
        body, name=name, grid_spec=grid_spec,
        out_shape=[jax.ShapeDtypeStruct((N_CHIPS, H, C), F32), jax.ShapeDtypeStruct((N_CHIPS, H, C), BF16)] * n,
        compiler_params=_cparams(("parallel", "parallel")),
    )(pos, *gs, *rs)
    return [(res[2 * a], res[2 * a + 1]) for a in range(n)]


def _exchange_copies(srcs, lands, send_sems, recv_sems):
    x, y, c = _my_pos()
    starts, landing = [], []
    for a in range(len(srcs)):
        for j, chip in enumerate(_other_chips(x, y)):
            sems = dict(send_sem=send_sems.at[3 * a + j], recv_sem=recv_sems.at[3 * a + j], device_id_type=MESH)
            starts.append(pltpu.make_async_remote_copy(
                src_ref=srcs[a].at[2 * chip[0] + chip[1]], dst_ref=lands[a].at[j], device_id=(*chip, c), **sems))
            landing.append(pltpu.make_async_remote_copy(
                src_ref=lands[a].at[j], dst_ref=lands[a].at[j], device_id=(x, y, c), **sems))
    return starts, landing


def _gather_copies(srcs, lands, l, send_sems, recv_sems):
    x, y, c = _my_pos()
    me = 2 * x + y
    starts, landing = [], []
    for a in range(len(srcs)):
        half = srcs[a].shape[1] // 2
        mine = pl.ds(c * half, half)
        for j, chip in enumerate(_other_chips(x, y)):
            sems = dict(send_sem=send_sems.at[3 * a + j], recv_sem=recv_sems.at[3 * a + j], device_id_type=MESH)
            starts.append(pltpu.make_async_remote_copy(
                src_ref=srcs[a].at[l, mine], dst_ref=lands[a].at[me, mine], device_id=(*chip, c), **sems))
            slab = lands[a].at[2 * chip[0] + chip[1], mine]
            landing.append(pltpu.make_async_remote_copy(src_ref=slab, dst_ref=slab, device_id=(x, y, c), **sems))
    return starts, landing


HBM = pl.BlockSpec(memory_space=pltpu.HBM)
SEM = pl.BlockSpec(memory_space=pltpu.SEMAPHORE)


def _ici_start(copies_fn, srcs, land_shapes, *, name, after=()):
    n, na = len(srcs), len(after)

    def body(*refs):
        starts, _ = copies_fn(refs[:n], refs[n:2 * n], refs[2 * n + na], refs[2 * n + na + 1])
        for cp in starts:
            cp.start()
        refs[-1][...] = jnp.zeros_like(refs[-1])

    sems = pltpu.SemaphoreType.DMA((3 * n,))
    hbm = lambda s: pltpu.HBM(s.shape, s.dtype)
    lands = [pltpu.with_memory_space_constraint(lax.empty(s.shape, s.dtype), pltpu.HBM) for s in land_shapes]
    res = pl.pallas_call(
        body, name=name, in_specs=[HBM] * (2 * n) + [ANY] * na,
        out_specs=(SEM, SEM, *[HBM] * (2 * n), pl.BlockSpec(memory_space=pltpu.VMEM)),
        out_shape=(sems, sems, *[hbm(s) for s in srcs], *[hbm(s) for s in land_shapes],
                   jax.ShapeDtypeStruct((8, BLOCK), F32)),
        input_output_aliases={i: 2 + i for i in range(2 * n)},
        compiler_params=pltpu.CompilerParams(has_side_effects=pltpu.SideEffectType.DATAFLOW_SIDE_EFFECTING),
    )(*[pltpu.with_memory_space_constraint(s, pltpu.HBM) for s in srcs], *lands, *after)
    return res[0], res[1], list(res[2:2 + n]), list(res[2 + n:2 + 2 * n]), res[-1]


def _ici_wait(copies_fn, send_sems, recv_sems, srcs, lands, after, *, name):
    n = len(srcs)

    def body(*refs):
        starts, landing = copies_fn(refs[:n], refs[n:2 * n], refs[2 * n], refs[2 * n + 1])
        for cp in starts:
            cp.wait_send()
        for cp in landing:
            cp.wait_recv()

    hbm = lambda s: pltpu.HBM(s.shape, s.dtype)
    res = pl.pallas_call(
        body, name=name, in_specs=[*[HBM] * (2 * n), SEM, SEM, ANY], out_specs=[HBM] * (2 * n),
        out_shape=[*[hbm(s) for s in srcs], *[hbm(s) for s in lands]],
        input_output_aliases={i: i for i in range(2 * n)},
        compiler_params=pltpu.CompilerParams(has_side_effects=pltpu.SideEffectType.DATAFLOW_SIDE_EFFECTING),
    )(*srcs, *lands, send_sems, recv_sems, after)
    return list(res[:n]), list(res[n:])


def _gather_d2d(shards, lands, l, tag):
    n = len(shards)

    def body(*refs):
        ins, outs = refs[:n], refs[2 * n:3 * n]
        send_sems, recv_sems = refs[3 * n], refs[3 * n + 1]
        x, y, c = _my_pos()
        me, sibling = 2 * x + y, (x, y, 1 - c)
        starts, landing = [], []
        for a in range(n):
            half = ins[a].shape[1] // 2
            mine, theirs = pl.ds(c * half, half), pl.ds((1 - c) * half, half)
            pairs = [(ins[a].at[l], outs[a].at[me], outs[a].at[me])]
            for chip in _other_chips(x, y):
                k = 2 * chip[0] + chip[1]
                pairs.append((outs[a].at[k, mine], outs[a].at[k, mine], outs[a].at[k, theirs]))
            for j, (src, dst, lands_here) in enumerate(pairs):
                sems = dict(send_sem=send_sems.at[4 * a + j], recv_sem=recv_sems.at[4 * a + j], device_id_type=MESH)
                starts.append(pltpu.make_async_remote_copy(src_ref=src, dst_ref=dst, device_id=sibling, **sems))
                landing.append(pltpu.make_async_remote_copy(src_ref=lands_here, dst_ref=lands_here, device_id=(x, y, c),
                                                            **sems))
        for cp in starts:
            cp.start()
        for cp in landing:
            cp.wait_recv()
        for cp in starts:
            cp.wait_send()

    return pl.pallas_call(
        body, name=f"gather_d2d_{tag}", in_specs=[ANY] * (2 * n), out_specs=[ANY] * n,
        out_shape=[jax.ShapeDtypeStruct(s.shape, s.dtype) for s in lands],
        input_output_aliases={n + a: a for a in range(n)},
        scratch_shapes=[pltpu.SemaphoreType.DMA((4 * n,)), pltpu.SemaphoreType.DMA((4 * n,))],
    )(*shards, *lands)


def _rs_add_chips(p32s, r16s, pos, *, name):
    n = len(p32s)
    _, H, C = p32s[0].shape
    tile = _rs_tile(H, C, n)
    nt = H // tile

    def body(pos_ref, *refs):
        for a in range(n):
            p_ref, r_ref = refs[a], refs[n + a]
            refs[2 * n + a][...] = ((p_ref[...] + r_ref[0].astype(F32)) + r_ref[1].astype(F32)) + r_ref[2].astype(F32)

    grid_spec = pltpu.PrefetchScalarGridSpec(
        num_scalar_prefetch=1, grid=(nt,),
        in_specs=[_bs((None, tile, C), lambda i, pos_ref: (pos_ref[0], i, 0))] * n
        + [_bs((3, tile, C), lambda i, pos_ref: (0, i, 0))] * n,
        out_specs=[_bs((tile, C), lambda i, pos_ref: (pos_ref[1] * nt + i, 0))] * n)
    return pl.pallas_call(
        body, name=name, grid_spec=grid_spec, out_shape=[jax.ShapeDtypeStruct((2 * H, C), F32)] * n,
        compiler_params=_cparams(("parallel",)),
    )(pos, *p32s, *r16s)


def _rs_join_rows(fs, tag):
    n = len(fs)

    def body(*refs):
        outs = refs[n:2 * n]
        send_sems, recv_sems = refs[2 * n], refs[2 * n + 1]
        x, y, c = _my_pos()
        for a in range(n):
            half = outs[a].shape[0] // 2
            mine = outs[a].at[pl.ds(c * half, half)]
            pltpu.make_async_remote_copy(src_ref=mine, dst_ref=mine, send_sem=send_sems.at[a],
                                         recv_sem=recv_sems.at[a], device_id=(x, y, 1 - c), device_id_type=MESH).start()
        for a in range(n):
            half = outs[a].shape[0] // 2
            pltpu.make_async_remote_copy(
                src_ref=outs[a].at[pl.ds(c * half, half)], dst_ref=outs[a].at[pl.ds((1 - c) * half, half)],
                send_sem=send_sems.at[a], recv_sem=recv_sems.at[a], device_id=(x, y, 1 - c), device_id_type=MESH).wait()

    return pl.pallas_call(
        body, name=f"rs_join_rows_{tag}", in_specs=[ANY] * n, out_specs=[ANY] * n,
        out_shape=[jax.ShapeDtypeStruct(f.shape, f.dtype) for f in fs],
        input_output_aliases={a: a for a in range(n)},
        scratch_shapes=[pltpu.SemaphoreType.DMA((n,)), pltpu.SemaphoreType.DMA((n,))],
    )(*fs)


def _pos_vector():
    x, y, c = _my_pos()
    return jnp.stack([2 * x + y, c]).astype(jnp.int32)


def _swap_copies(srcs, lands, send_sems, recv_sems):
    x, y, c = _my_pos()
    starts, landing = [], []
    for a in range(len(srcs)):
        half = srcs[a].shape[1] // 2
        sems = dict(send_sem=send_sems.at[3 * a], recv_sem=recv_sems.at[3 * a], device_id_type=MESH)
        starts.append(pltpu.make_async_remote_copy(
            src_ref=srcs[a].at[:, pl.ds((1 - c) * half, half)], dst_ref=lands[a], device_id=(x, y, 1 - c), **sems))
        landing.append(pltpu.make_async_remote_copy(src_ref=lands[a], dst_ref=lands[a], device_id=(x, y, c), **sems))
    return starts, landing


def _swap_land_shapes(gs):
    return [jax.ShapeDtypeStruct((N_CHIPS, g.shape[1] // 2, g.shape[2]), g.dtype) for g in gs]


def _same_shape_runs(arrays):
    runs, start = [], 0
    for i in range(1, len(arrays) + 1):
        if i == len(arrays) or arrays[i].shape != arrays[start].shape:
            runs.append((start, i))
            start = i
    return runs


def _rs_add_pairs(gs, r1, names, tag):
    pos = _pos_vector()
    out = []
    for a, b in _same_shape_runs(gs):
        out += _rs_add_pair(gs[a:b], r1[a:b], pos, name=f"rs_add_pair_{tag}_{names[a]}")
    return out


def _rs_pair_sums(gs, names, tag):
    return _rs_add_pairs(gs, _rs_swap_rows(gs, tag), names, tag)


def _rs_finish(pairs, r2, names, tag):
    pos = _pos_vector()
    p32s = [p[0] for p in pairs]
    fs = []
    for a, b in _same_shape_runs(p32s):
        fs += _rs_add_chips(p32s[a:b], r2[a:b], pos, name=f"rs_add_chips_{tag}_{names[a]}")
    return _rs_join_rows(fs, tag)


def _exchange_land_shapes(pairs):
    return [jax.ShapeDtypeStruct((3,) + p[1].shape[1:], p[1].dtype) for p in pairs]


def _allreduce_small(buf):
    R, W = buf.shape

    def body(b_ref, o_ref, gather, send_sems, recv_sems):
        x, y, c = _my_pos()
        me = 4 * x + 2 * y + c
        gather[me] = b_ref[...]
        cps = []
        for d in range(1, 8):
            peer = (x ^ (d >> 2), y ^ ((d >> 1) & 1), c ^ (d & 1))
            cps.append(pltpu.make_async_remote_copy(
                src_ref=b_ref, dst_ref=gather.at[me], send_sem=send_sems.at[d - 1], recv_sem=recv_sems.at[d - 1],
                device_id=peer, device_id_type=MESH))
        for cp in cps:
            cp.start()
        for d in range(1, 8):
            pltpu.make_async_remote_copy(
                src_ref=b_ref, dst_ref=gather.at[me ^ d], send_sem=send_sems.at[d - 1], recv_sem=recv_sems.at[d - 1],
                device_id=(x, y, c), device_id_type=MESH).wait_recv()
        for cp in cps:
            cp.wait_send()
        acc = gather[0]
        for d in range(1, 8):
            acc = acc + gather[d]
        o_ref[...] = acc

    vm = pl.BlockSpec(memory_space=pltpu.VMEM)
    return pl.pallas_call(
        body, name="allreduce_small", in_specs=[vm], out_specs=vm, out_shape=jax.ShapeDtypeStruct((R, W), F32),
        scratch_shapes=[pltpu.VMEM((8, R, W), F32), pltpu.SemaphoreType.DMA((7,)), pltpu.SemaphoreType.DMA((7,))],
    )(buf)


def _heads(a, h, d):
    return a.reshape(a.shape[0], h, d).transpose(1, 0, 2)


def _unheads(a):
    h, L, d = a.shape
    return a.transpose(1, 0, 2).reshape(L, h * d)


def _rope_tables():
    pos = jnp.maximum(jnp.arange(LP, dtype=F32) - PAD_ROWS, 0.0)
    inv_freq = 1.0 / (ROPE_THETA ** (jnp.arange(0, MLA_ROPE, 2, dtype=F32) / MLA_ROPE))
    ang = pos[:, None] * inv_freq[None, :]
    cos, sin = jnp.tile(jnp.cos(ang), (1, MLA_HEADS)), jnp.tile(jnp.sin(ang), (1, MLA_HEADS))
    return jnp.concatenate([cos, cos], axis=1), jnp.concatenate([-sin, sin], axis=1)


def _lane_pad(a, width=BLOCK):
    return jnp.pad(a, ((0, 0), (0, width - a.shape[1])))


def _pad_in_proj(w):
    sl = lambda start, size: w[:, start:start + size]
    return jnp.concatenate([
        sl(OC_Z, 512), sl(OC_XBC, 768), sl(OC_FQ, 256), sl(OC_FK, 256), sl(OC_FV, 256), sl(OC_CQ, 256), sl(OC_CKV, 128),
        _lane_pad(sl(OC_DT, SSD_HEADS)), _lane_pad(sl(OC_FR, FOX_HEADS)),
        jnp.tile(sl(OC_KR, ROPE_HALF), (1, MLA_HEADS)), jnp.tile(sl(OC_KR + ROPE_HALF, ROPE_HALF), (1, MLA_HEADS))], axis=1)


def _unpad_in_proj(wp):
    sl = lambda start, size: wp[:, start:start + size]
    rope = lambda start: sl(start, 64).reshape(wp.shape[0], MLA_HEADS, ROPE_HALF).sum(axis=1)
    return jnp.concatenate([
        sl(PC_Z, 512), sl(PC_XBC, 768), sl(PC_DT, SSD_HEADS), sl(PC_FQ, 256), sl(PC_FK, 256), sl(PC_FV, 256),
        sl(PC_FR, FOX_HEADS), sl(PC_CQ, 256), sl(PC_CKV, 128), rope(PC_KR), rope(PC_KR + 64)], axis=1)


def _regroup_uq(w):
    w3 = w.reshape(w.shape[0], MLA_HEADS, MLA_NOPE + MLA_ROPE)
    return jnp.concatenate([w3[:, :, :MLA_NOPE].reshape(w.shape[0], -1),
                            w3[:, :, MLA_NOPE:MLA_NOPE + ROPE_HALF].reshape(w.shape[0], -1),
                            w3[:, :, MLA_NOPE + ROPE_HALF:].reshape(w.shape[0], -1)], axis=1)


def _ungroup_uq(wp):
    n = wp.shape[0]
    return jnp.concatenate([wp[:, :256].reshape(n, MLA_HEADS, MLA_NOPE), wp[:, 256:320].reshape(n, MLA_HEADS, ROPE_HALF),
                            wp[:, 320:].reshape(n, MLA_HEADS, ROPE_HALF)], axis=2).reshape(n, -1)


def _regroup_ukv(w):
    w3 = w.reshape(w.shape[0], MLA_HEADS, MLA_NOPE + MLA_V)
    return jnp.concatenate([w3[:, :, :MLA_NOPE].reshape(w.shape[0], -1), w3[:, :, MLA_NOPE:].reshape(w.shape[0], -1)],
                           axis=1)


def _ungroup_ukv(wp):
    n = wp.shape[0]
    return jnp.concatenate([wp[:, :256].reshape(n, MLA_HEADS, MLA_NOPE), wp[:, 256:].reshape(n, MLA_HEADS, MLA_V)],
                           axis=2).reshape(n, -1)


TMF = 1088
N_IF = LP // TMF


def _chunk_rows_dx(g, w, l, chunk_h, *, name):
    N = w.shape[2]
    return _mm_core(g, w, a_spec=_bs((TMF, N), lambda i, j, k: (i, 0)),
                    b_spec=_bs((None, chunk_h, N), lambda i, j, k: (j, 0, 0)),
                    o_spec=_bs((TMF, chunk_h), lambda i, j, k: (i, j)), grid=(N_IF, N_CHIPS, 1),
                    out_shape=(LP, N_CHIPS * chunk_h), ca=1, cb=1, name=name)


def _chunk_rows_dw(a, g, chunk_h, *, name):
    N = g.shape[1]
    return _mm_core(a, g, a_spec=_bs((LP, chunk_h), lambda i, j, k: (0, i)), b_spec=_bs((LP, N), lambda i, j, k: (0, 0)),
                    o_spec=_bs((None, chunk_h, N), lambda i, j, k: (i, 0, 0)), grid=(N_CHIPS, 1, 1),
                    out_shape=(N_CHIPS, chunk_h, N), ca=0, cb=0, name=name)


def _ffn_up_swiglu(h, wg, wu, *, name):
    def body(h_ref, wg_ref, wu_ref, g_ref, u_ref, a_ref):
        hb = h_ref[...].astype(BF16)
        g = _raw_bdot(hb, wg_ref[...], 1, 1)
        u = _raw_bdot(hb, wu_ref[...], 1, 1)
        g_ref[...] = g
        u_ref[...] = u
        a_ref[...] = (_silu(g) * u).astype(a_ref.dtype)

    w_spec = _bs((None, HP, D_MODEL), lambda i, j: (j, 0, 0))
    o_spec = _bs((TMF, HP), lambda i, j: (i, j))
    return pl.pallas_call(
        body, name=name, grid=(N_IF, N_CHIPS), in_specs=[_bs((TMF, D_MODEL), lambda i, j: (i, 0)), w_spec, w_spec],
        out_specs=[o_spec] * 3,
        out_shape=[jax.ShapeDtypeStruct((LP, FP), F32), jax.ShapeDtypeStruct((LP, FP), F32),
                   jax.ShapeDtypeStruct((LP, FP), BF16)],
        compiler_params=_cparams(("parallel", "parallel")),
    )(h, wg, wu)


def _ffn_down_dx_swiglu(do, wd, g, u, *, name):
    def body(do_ref, wd_ref, g_ref, u_ref, dg_ref, du_ref):
        dact = _raw_bdot(do_ref[...], wd_ref[...], 1, 1)
        gv = g_ref[...]
        sig = _sigmoid(gv)
        dg_ref[...] = (dact * u_ref[...] * (sig * (1.0 + gv * (1.0 - sig)))).astype(dg_ref.dtype)
        du_ref[...] = (dact * (gv * sig)).astype(du_ref.dtype)

    blk = _bs((TMF, HP), lambda i, j: (i, j))
    return pl.pallas_call(
        body, name=name, grid=(N_IF, N_CHIPS),
        in_specs=[_bs((TMF, D_MODEL), lambda i, j: (i, 0)), _bs((None, HP, D_MODEL), lambda i, j: (j, 0, 0)), blk, blk],
        out_specs=[blk, blk], out_shape=[jax.ShapeDtypeStruct((LP, FP), BF16)] * 2,
        compiler_params=_cparams(("parallel", "parallel")),
    )(do, wd, g, u)


def _ffn_gate_up_dw(dg, du, h, *, name):
    def body(dg_ref, du_ref, h_ref, wg_ref, wu_ref):
        hb = h_ref[...].astype(BF16)
        wg_ref[...] = _raw_bdot(dg_ref[...], hb, 0, 0)
        wu_ref[...] = _raw_bdot(du_ref[...], hb, 0, 0)

    a_spec = _bs((LP, HP), lambda k: (0, k))
    o_spec = _bs((None, HP, D_MODEL), lambda k: (k, 0, 0))
    return pl.pallas_call(
        body, name=name, grid=(N_CHIPS,), in_specs=[a_spec, a_spec, _bs((LP, D_MODEL), lambda k: (0, 0))],
        out_specs=[o_spec, o_spec], out_shape=[jax.ShapeDtypeStruct((N_CHIPS, HP, D_MODEL), F32)] * 2,
        compiler_params=_cparams(("parallel",)),
    )(dg, du, h)


def _ffn_gate_up_dx(dg, du, wg, wu, add, *, name):
    def body(dg_ref, du_ref, wg_ref, wu_ref, add_ref, o_ref, acc_ref):
        k = pl.program_id(1)

        @pl.when(k == 0)
        def _():
            acc_ref[...] = jnp.zeros_like(acc_ref)

        acc_ref[...] += _raw_bdot(dg_ref[...], wg_ref[...], 1, 0) + _raw_bdot(du_ref[...], wu_ref[...], 1, 0)

        @pl.when(k == N_CHIPS - 1)
        def _():
            o_ref[...] = acc_ref[...] + add_ref[...]

    a_spec = _bs((TMF, HP), lambda i, k: (i, k))
    w_spec = _bs((None, HP, D_MODEL), lambda i, k: (k, 0, 0))
    o_spec = _bs((TMF, D_MODEL), lambda i, k: (i, 0))
    return pl.pallas_call(
        body, name=name, grid=(N_IF, N_CHIPS), in_specs=[a_spec, a_spec, w_spec, w_spec, o_spec], out_specs=o_spec,
        out_shape=jax.ShapeDtypeStruct((LP, D_MODEL), F32), scratch_shapes=[pltpu.VMEM((TMF, D_MODEL), F32)],
        compiler_params=_cparams(("parallel", "arbitrary")),
    )(dg, du, wg, wu, add)


def _chunk_rows_mm_res_ln(a, w, chunk_h, h, gam, bet, scale, *, name):
    res_ln = _make_res_ln_fn(scale)

    def body(a_ref, w_ref, h_ref, g_ref, b_ref, o_ref, y_ref, yb_ref, acc_ref):
        k = pl.program_id(1)

        @pl.when(k == 0)
        def _():
            acc_ref[...] = jnp.zeros_like(acc_ref)

        acc_ref[...] += _raw_bdot(a_ref[...], w_ref[...], 1, 0)

        @pl.when(k == N_CHIPS - 1)
        def _():
            o = acc_ref[...]
            o_ref[...] = o
            (y,) = res_ln(0, h_ref[...], o, g_ref[...], b_ref[...])
            y_ref[...] = y
            yb_ref[...] = y.astype(yb_ref.dtype)

    row = _bs((TMF, D_MODEL), lambda i, k: (i, 0))
    par = _bs((1, D_MODEL), lambda i, k: (0, 0))
    return pl.pallas_call(
        body, name=name, grid=(N_IF, N_CHIPS),
        in_specs=[_bs((TMF, chunk_h), lambda i, k: (i, k)), _bs((None, chunk_h, D_MODEL), lambda i, k: (k, 0, 0)), row,
                  par, par],
        out_specs=[row, row, row],
        out_shape=[jax.ShapeDtypeStruct((LP, D_MODEL), F32)] * 2 + [jax.ShapeDtypeStruct((LP, D_MODEL), BF16)],
        scratch_shapes=[pltpu.VMEM((TMF, D_MODEL), F32)], compiler_params=_cparams(("parallel", "arbitrary")),
    )(a, w, h, gam, bet)


def _ffn_fwd(hp, W, pre, l, gam, bet, tag):
    h, hb = hp
    g, u, act = _ffn_up_swiglu(hb, W[pre + "_w_gate"][l], W[pre + "_w_up"][l], name=f"{tag}_up_swiglu")
    o, out, outb = _chunk_rows_mm_res_ln(act, W[pre + "_w_down"][l], HP, h, gam, bet, 0.5, name=f"{tag}_down_ln")
    return (out, outb), (h, hb, g, u, act, o)


def _ffn_bwd(dout, saved, W, pre, l, gam, bet, GB, tag):
    h, hb, g, u, act, o = saved
    (dh_a, do), (dgam, dbet) = _rowwise_bwd(_make_res_ln_fn(0.5), [h, o], [gam, bet], [dout], name=f"{tag}_ln_bwd",
                                            tile=272, grad_dtypes=[F32, BF16])
    dg, du = _ffn_down_dx_swiglu(do, W[pre + "_w_down"][l], g, u, name=f"{tag}_down_dx_swiglu")
    GB[pre + "_w_down"] = _chunk_rows_dw(act, do, HP, name=f"{tag}_down_dw")
    GB[pre + "_w_gate"], GB[pre + "_w_up"] = _ffn_gate_up_dw(dg, du, hb, name=f"{tag}_gate_up_dw")
    dh = _ffn_gate_up_dx(dg, du, W[pre + "_w_gate"][l], W[pre + "_w_up"][l], dh_a, name=f"{tag}_gate_up_dx")
    return dh, dgam, dbet


def _mixer_fwd(hp1, W, l, cosf, sins, after=()):
    h1, h1b = hp1
    tag = f"l{l}"
    proj = _mm(h1b, W["w_in_p"][l], name=f"{tag}_in_proj", after=after)
    sv = {"h1": h1, "h1b": h1b, "proj": proj}
    conv_w, conv_b = W["conv_w"][l], W["conv_b"][l][None]
    xc = _conv_fwd(proj, PC_XBC // BLOCK, conv_w, conv_b, name=f"{tag}_conv")
    dt_bias = _lane_pad(W["dt_bias"][l][None])
    (dt,) = _rowwise(_ssd_pre_fn, [(proj, BLOCK, PC_DT // BLOCK)], [dt_bias], [BLOCK], name=f"{tag}_ssd_dt", tile=272)
    xh = _heads(xc[:, :SSD_D], SSD_HEADS, SSD_HD)
    bm = _heads(xc[:, SSD_D:SSD_D + 128], SSD_GROUPS, SSD_STATE)
    cm = _heads(xc[:, SSD_D + 128:], SSD_GROUPS, SSD_STATE)
    dt8 = dt[:, :SSD_HEADS].T
    dtc, dtr = dt8[:, :, None], dt8[:, None, :]
    alog = jnp.broadcast_to(W["a_log"][l][:, None, None], (SSD_HEADS, 1, BLOCK))
    yh, prevs = _ssd_fwd(xh, bm, cm, dtc, dtr, alog, name=f"{tag}_ssd")
    y_raw = _unheads(yh)
    dskip = jnp.repeat(W["d_skip"][l], SSD_HD)[None]
    normg = W["ssd_norm_g"][l][None]
    post_rows = [y_raw, (xc, 256, 0), (proj, 256, PC_Z // 256)]
    (y_ssd,) = _rowwise(_ssd_post_fn, post_rows, [dskip, normg], [SSD_D], name=f"{tag}_ssd_post", tile=272,
                        ncol=SSD_GROUPS)
    sv.update(conv_w=conv_w, conv_b=conv_b, dt_bias=dt_bias, xh=xh, bm=bm, cm=cm, dtc=dtc, dtr=dtr, alog=alog,
              prevs=prevs, post_rows=post_rows, dskip=dskip, normg=normg)
    f_b = _lane_pad(W["fox_f_b"][l][None])
    cg, cgt = _fox_gate_fwd(proj, PC_FR // BLOCK, f_b, name=f"{tag}_fox_gate")
    fox_qkv = ((proj, PC_FQ // ATT_W), (proj, PC_FK // ATT_W), (proj, PC_FV // ATT_W))
    y_fox, lse_f = _attn_fwd(*fox_qkv, scale=FOX_HD ** -0.5, name=f"{tag}_fox_attn", bias=(cg, cgt))
    sv.update(f_b=f_b, cg=cg, cgt=cgt, fox_qkv=fox_qkv, y_fox=y_fox, lse_f=lse_f)
    gq, gkv = W["mla_q_norm_g"][l][None], W["mla_kv_norm_g"][l][None]
    norm_rows = [(proj, 256, PC_CQ // 256), (proj, BLOCK, PC_CKV // BLOCK)]
    qn, cn = _rowwise(_mla_norm_fn, norm_rows, [gq, gkv], [MLA_Q_LORA, MLA_KV_LORA], name=f"{tag}_mla_norm", tile=272,
                      out_dtypes=[BF16, BF16])
    qh = _mm(qn, W["mla_w_uq_p"][l], name=f"{tag}_mla_uq")
    kvh = _mm(cn, W["mla_w_ukv_p"][l], name=f"{tag}_mla_ukv")
    qr, kr = _rowwise(_rope_fn, [(qh, BLOCK, 2), (proj, BLOCK, PC_KR // BLOCK), cosf, sins], [], [BLOCK, BLOCK],
                      name=f"{tag}_rope", tile=272)
    mla_qkv = ((qh, 0), (kvh, 0), (kvh, 1))
    y_mla, lse_m = _attn_fwd(*mla_qkv, scale=(MLA_NOPE + MLA_ROPE) ** -0.5, name=f"{tag}_mla_attn",
                             rope=((qr, 0), (kr, 0)))
    sv.update(gq=gq, gkv=gkv, norm_rows=norm_rows, qn=qn, cn=cn, qr=qr, kr=kr, mla_qkv=mla_qkv, y_mla=y_mla, lse_m=lse_m)
    ycat = jnp.concatenate([y_ssd, y_fox, y_mla], axis=1).astype(BF16)
    mix, h2, h2b = _chunk_rows_mm_res_ln(ycat, W["w_out"][l], 256, h1, W["ln2_g"][l][None], W["ln2_b"][l][None], 1.0,
                                    name=f"{tag}_out_proj_ln2")
    sv.update(mix=mix, ycat=ycat)
    return (h2, h2b), sv


def _mixer_bwd(dh2, sv, W, l, cosf, sins, GB, zero=0.0):
    tag = f"l{l}"
    G = {}
    proj = sv["proj"]
    ln2g, ln2b = W["ln2_g"][l][None] + zero, W["ln2_b"][l][None]
    (dh1_a, dmix), (dln2g, dln2b) = _rowwise_bwd(
        _make_res_ln_fn(1.0), [sv["h1"], sv["mix"]], [ln2g, ln2b], [dh2], name=f"{tag}_ln2_bwd", tile=272,
        grad_dtypes=[F32, BF16])
    G["ln2_g"], G["ln2_b"] = dln2g[0], dln2b[0]
    dycat = _chunk_rows_dx(dmix, W["w_out"][l], l, 256, name=f"{tag}_out_proj_dx")
    GB["w_out"] = _chunk_rows_dw(sv["ycat"], dmix, 256, name=f"{tag}_out_proj_dw")
    (dy_raw, dxs_a, dz), (ddskip, dnormg) = _rowwise_bwd(
        _ssd_post_fn, sv["post_rows"], [sv["dskip"], sv["normg"]], [dycat[:, :SSD_D]],
        name=f"{tag}_ssd_post_bwd", tile=272, ncol=SSD_GROUPS)
    G["ssd_norm_g"] = dnormg[0]
    G["d_skip"] = ddskip.reshape(SSD_HEADS, SSD_HD).sum(axis=1)
    dxh, dbm, dcm, ddtc, ddtr, dal = _ssd_bwd(sv["xh"], sv["bm"], sv["cm"], sv["dtc"], sv["dtr"], sv["alog"],
                                              sv["prevs"], _heads(dy_raw, SSD_HEADS, SSD_HD), name=f"{tag}_ssd_bwd")
    G["a_log"] = dal[:, 0, 0]
    dxc = jnp.concatenate([dxs_a + _unheads(dxh), _unheads(dbm), _unheads(dcm)], axis=1)
    dxbc, G["conv_w"], dconv_b = _conv_bwd(proj, PC_XBC // BLOCK, sv["conv_w"], sv["conv_b"], dxc,
                                           name=f"{tag}_conv_bwd")
    G["conv_b"] = dconv_b[0]
    ddt = _lane_pad((ddtc[:, :, 0] + ddtr[:, 0, :]).T)
    (ddt_raw,), (ddt_bias,) = _rowwise_bwd(_ssd_pre_fn, [(proj, BLOCK, PC_DT // BLOCK)], [sv["dt_bias"]], [ddt],
                                           name=f"{tag}_ssd_dt_bwd", tile=272)
    G["dt_bias"] = ddt_bias[0, :SSD_HEADS]
    dfq, dfk, dfv, dcg, dcgt = _attn_bwd(*sv["fox_qkv"], sv["y_fox"], sv["lse_f"], (dycat, SSD_D // ATT_W),
                                         scale=FOX_HD ** -0.5, name=f"{tag}_fox_attn_bwd", bias=(sv["cg"], sv["cgt"]))
    df_raw, dfb = _fox_gate_bwd(proj, PC_FR // BLOCK, sv["f_b"], dcg, dcgt, name=f"{tag}_fox_gate_bwd")
    G["fox_f_b"] = dfb[0, :FOX_HEADS]
    dqn_h, dkn_h, dv_h, dqr, dkr = _attn_bwd(
        *sv["mla_qkv"], sv["y_mla"], sv["lse_m"], (dycat, (SSD_D + FOX_D) // ATT_W),
        scale=(MLA_NOPE + MLA_ROPE) ** -0.5, name=f"{tag}_mla_attn_bwd", rope=((sv["qr"], 0), (sv["kr"], 0)))
    dq_rope, dk_rope = _rowwise(_rope_t_fn, [dqr, dkr, cosf, sins], [], [BLOCK, BLOCK], name=f"{tag}_rope_bwd",
                                tile=272)
    dqh = jnp.concatenate([dqn_h, dq_rope], axis=1).astype(BF16)
    dkvh = jnp.concatenate([dkn_h, dv_h], axis=1).astype(BF16)
    dqn = _mm(dqh, W["mla_w_uq_p"][l], tb=True, name=f"{tag}_mla_uq_dx")
    G["mla_w_uq_p"] = _mm(sv["qn"], dqh, ta=True, name=f"{tag}_mla_uq_dw")
    dcn = _mm(dkvh, W["mla_w_ukv_p"][l], tb=True, name=f"{tag}_mla_ukv_dx")
    G["mla_w_ukv_p"] = _mm(sv["cn"], dkvh, ta=True, name=f"{tag}_mla_ukv_dw")
    (dcq, dckv), (dgq, dgkv) = _rowwise_bwd(_mla_norm_fn, sv["norm_rows"], [sv["gq"], sv["gkv"]], [dqn, dcn],
                                            name=f"{tag}_mla_norm_bwd", tile=272)
    G["mla_q_norm_g"], G["mla_kv_norm_g"] = dgq[0], dgkv[0]
    dproj = jnp.concatenate([dz, dxbc, dfq, dfk, dfv, dcq, dckv, ddt_raw, df_raw, dk_rope], axis=1).astype(BF16)
    dh1 = _mm(dproj, W["w_in_p"][l], tb=True, add=dh1_a, name=f"{tag}_in_proj_dx")
    G["w_in_p"] = _mm(sv["h1b"], dproj, ta=True, name=f"{tag}_in_proj_dw")
    return dh1, G


def _embed(x, meta):
    return jnp.concatenate([jnp.zeros((PAD_ROWS, D_MODEL), F32), meta, x], axis=0)


def _layer_fwd(h, W, l, cosf, sins):
    ln = lambda n: W[n][l][None]
    h1, s1 = _ffn_fwd(h, W, "ffn1", l, ln("ln1_g"), ln("ln1_b"), f"l{l}_ffn1")
    h2, sm = _mixer_fwd(h1, W, l, cosf, sins)
    h3, s2 = _ffn_fwd(h2, W, "ffn2", l, ln("ln3_g"), ln("ln3_b"), f"l{l}_ffn2")
    return h3, (s1, sm, s2)


def _layer_bwd(dh, saved, W, l, cosf, sins):
    ln = lambda n: W[n][l][None]
    s1, sm, s2 = saved
    G = {}
    dh, dg, db = _ffn_bwd(dh, s2, W, "ffn2", l, ln("ln3_g"), ln("ln3_b"), G, f"l{l}_ffn2")
    G["ln3_g"], G["ln3_b"] = dg[0], db[0]
    dh, Gm = _mixer_bwd(dh, sm, W, l, cosf, sins, G)
    G.update(Gm)
    dh, dg, db = _ffn_bwd(dh, s1, W, "ffn1", l, ln("ln1_g"), ln("ln1_b"), G, f"l{l}_ffn1")
    G["ln1_g"], G["ln1_b"] = dg[0], db[0]
    return dh, G


def _local_step(x, target, W):
    h = _embed(x, W["meta"])
    h = (h, h.astype(BF16))
    tgt = jnp.concatenate([jnp.zeros((BLOCK, D_MODEL), F32), target], axis=0)
    cosf, sins = _rope_tables()
    saved = []
    for l in range(DEPTH):
        h, sv = _layer_fwd(h, W, l, cosf, sins)
        saved.append(sv)
    dh, loss = _loss_head(h[0], tgt, name="loss_head")
    grads = [None] * DEPTH
    for l in reversed(range(DEPTH)):
        dh, grads[l] = _layer_bwd(dh, saved[l], W, l, cosf, sins)
    return loss, dh, grads


WEIGHTS = ['meta', 'ffn1_w_gate', 'ffn1_w_up', 'ffn1_w_down', 'ln1_g', 'ln1_b', 'w_in', 'conv_w', 'conv_b', 'dt_bias',
           'a_log', 'd_skip', 'ssd_norm_g', 'fox_f_b', 'mla_q_norm_g', 'mla_w_uq', 'mla_kv_norm_g', 'mla_w_ukv',
           'w_out', 'ln2_g', 'ln2_b', 'ffn2_w_gate', 'ffn2_w_up', 'ffn2_w_down', 'ln3_g', 'ln3_b']
SMALL = ["ln1_g", "ln1_b", "conv_b", "dt_bias", "a_log", "d_skip", "ssd_norm_g", "fox_f_b", "mla_q_norm_g",
         "mla_kv_norm_g", "ln2_g", "ln2_b", "ln3_g", "ln3_b"]
MATMUL_W = ["ffn1_w_gate", "ffn1_w_up", "ffn1_w_down", "w_in", "mla_w_uq", "mla_w_ukv", "w_out", "ffn2_w_gate",
            "ffn2_w_up", "ffn2_w_down"]
SMALL_ROWS = 312


def _pad_to(a, axis, size):
    pads = [(0, 0)] * a.ndim
    pads[axis] = (0, size - a.shape[axis])
    return jnp.pad(a, pads)


def _chip_cols(full, chip, width):
    return lax.dynamic_slice_in_dim(full, chip * width, width, axis=full.ndim - 1)


def kernel(x, meta, ffn1_w_gate, ffn1_w_up, ffn1_w_down, ln1_g, ln1_b, w_in, conv_w, conv_b, dt_bias, a_log, d_skip, ssd_norm_g, fox_f_b, mla_q_norm_g, mla_w_uq, mla_kv_norm_g, mla_w_ukv, w_out, ln2_g, ln2_b, ffn2_w_gate, ffn2_w_up, ffn2_w_down, ln3_g, ln3_b, loss_target, m_meta, m_ffn1_w_gate, m_ffn1_w_up, m_ffn1_w_down, m_ln1_g, m_ln1_b, m_w_in, m_conv_w, m_conv_b, m_dt_bias, m_a_log, m_d_skip, m_ssd_norm_g, m_fox_f_b, m_mla_q_norm_g, m_mla_w_uq, m_mla_kv_norm_g, m_mla_w_ukv, m_w_out, m_ln2_g, m_ln2_b, m_ffn2_w_gate, m_ffn2_w_up, m_ffn2_w_down, m_ln3_g, m_ln3_b, v_meta, v_ffn1_w_gate, v_ffn1_w_up, v_ffn1_w_down, v_ln1_g, v_ln1_b, v_w_in, v_conv_w, v_conv_b, v_dt_bias, v_a_log, v_d_skip, v_ssd_norm_g, v_fox_f_b, v_mla_q_norm_g, v_mla_w_uq, v_mla_kv_norm_g, v_mla_w_ukv, v_w_out, v_ln2_g, v_ln2_b, v_ffn2_w_gate, v_ffn2_w_up, v_ffn2_w_down, v_ln3_g, v_ln3_b):
    args = dict(locals())
    w = {n: args[n] for n in WEIGHTS}
    m = {n: args["m_" + n] for n in WEIGHTS}
    v = {n: args["v_" + n] for n in WEIGHTS}
    xcoord, ycoord, _ = _my_pos()
    chip = 2 * xcoord + ycoord

    tr = lambda a: jnp.swapaxes(a, 1, 2)

    def bf16_shard(n, zero=None):
        a = w[n] if zero is None else w[n] + zero
        if n.endswith("w_gate") or n.endswith("w_up"):
            a = _pad_to(tr(a), 1, HP)
        elif n.endswith("w_down"):
            a = _pad_to(a, 1, HP)
        elif n == "w_in":
            a = _pad_to(a, 2, IN_SHARD_P)
        return a.astype(BF16)

    land_shape = lambda s: jax.ShapeDtypeStruct((N_CHIPS,) + s.shape[1:], s.dtype)
    gather_l = lambda l: (lambda srcs, lands, ss, rs: _gather_copies(srcs, lands, l, ss, rs))
    tiny = _allgather_chips([w["meta"].reshape(2, N_META // 2, D_MODEL // N_CHIPS), w["conv_w"]])
    meta_full = jnp.concatenate([tiny[0][k].reshape(N_META, D_MODEL // N_CHIPS) for k in range(N_CHIPS)], axis=1)

    W = {n: [None] * DEPTH for n in MATMUL_W + ["w_in_p", "mla_w_uq_p", "mla_w_ukv_p"]}
    W["conv_w"] = jnp.concatenate([tiny[1][k] for k in range(N_CHIPS)], axis=-1)
    W["meta"] = meta_full
    for n in SMALL:
        W[n] = w[n]

    def use_gathered(l, names, lands):
        got = dict(zip(names, lands))
        cat = lambda n, cut=None: jnp.concatenate([got[n][k][..., :cut] for k in range(N_CHIPS)], axis=-1)
        for n in names:
            W[n][l] = got[n]
        if "w_in" in got:
            W["w_in_p"][l] = _pad_in_proj(cat("w_in", IN_SHARD))
            W["mla_w_uq_p"][l] = _regroup_uq(cat("mla_w_uq"))
            W["mla_w_ukv_p"][l] = _regroup_ukv(cat("mla_w_ukv"))

    def chunk_grads(G, names):
        def chunked(name, ungroup, width, pad):
            full = ungroup(G[name])
            return _pad_to(jnp.moveaxis(full.reshape(full.shape[0], N_CHIPS, width), 1, 0), 2, pad)
        special = {"w_in": ("w_in_p", _unpad_in_proj, IN_SHARD, IN_SHARD_P),
                   "mla_w_uq": ("mla_w_uq_p", _ungroup_uq, MLA_NOPE + MLA_ROPE, MLA_NOPE + MLA_ROPE),
                   "mla_w_ukv": ("mla_w_ukv_p", _ungroup_ukv, MLA_NOPE + MLA_V, MLA_NOPE + MLA_V)}
        return [chunked(*special[n]) if n in special else G[n] for n in names]

    def rs_start(G, names, tag):
        pairs = _rs_pair_sums(chunk_grads(G, names), names, tag)
        handle = _ici_start(_exchange_copies, [p[1] for p in pairs], _exchange_land_shapes(pairs),
                            name=f"rs_exchange_{tag}_start")
        return pairs, handle

    def swap_start(G, names, tag):
        gs = chunk_grads(G, names)
        return _ici_start(_swap_copies, gs, _swap_land_shapes(gs), name=f"rs_swap_{tag}_start")

    def exchange_start(swap_handle, names, tag, after):
        gs, r1 = _ici_wait(_swap_copies, *swap_handle[:4], after, name=f"rs_swap_{tag}_wait")
        pairs = _rs_add_pairs(gs, r1, names, tag)
        handle = _ici_start(_exchange_copies, [p[1] for p in pairs], _exchange_land_shapes(pairs),
                            name=f"rs_exchange_{tag}_start")
        return pairs, handle

    def rs_end(pairs, handle, names, tag, after):
        _, r2 = _ici_wait(_exchange_copies, *handle[:4], after, name=f"rs_exchange_{tag}_wait")
        return dict(zip(names, _rs_finish(pairs, r2, names, tag)))

    na = 3
    first, rest = MATMUL_W[:na], MATMUL_W[na:]
    shards = [bf16_shard(n) for n in first]
    g_send, g_recv, g_srcs, g_lands, token = _ici_start(gather_l(0), shards, [land_shape(s) for s in shards],
                                                        name="gather_ici_l0_ffn1_start", after=[tiny[0]])
    shards_rest = [bf16_shard(n, token[0, 0]) for n in rest]
    shards, lands = _ici_wait(gather_l(0), g_send, g_recv, g_srcs, g_lands, shards_rest[0],
                              name="gather_ici_l0_ffn1_wait")
    shards = shards + shards_rest
    land_shapes = [land_shape(s) for s in shards]
    got = _gather_d2d(shards[:na], lands, 0, "l0_ffn1")
    use_gathered(0, first, got)
    g_send, g_recv, g_srcs, g_lands, token = _ici_start(gather_l(0), shards[na:], land_shapes[na:],
                                                        name="gather_ici_l0_rest_start", after=[got[0]])
    cosf, sins = _rope_tables()
    ln = lambda n, l: W[n][l][None]
    h = _embed(x[0] + token[0, 0], meta_full)
    h = (h, h.astype(BF16))
    h1, s1 = _ffn_fwd(h, W, "ffn1", 0, ln("ln1_g", 0), ln("ln1_b", 0), "l0_ffn1")
    rest_shards, lands = _ici_wait(gather_l(0), g_send, g_recv, g_srcs, g_lands, h1[0], name="gather_ici_l0_rest_wait")
    shards = shards[:na] + rest_shards
    got = _gather_d2d(shards[na:], lands, 0, "l0_rest")
    use_gathered(0, rest, got)
    g_send, g_recv, g_srcs, g_lands, token = _ici_start(gather_l(1), shards, land_shapes, name="gather_ici_l1_start",
                                                        after=[got[0]])
    h2, sm = _mixer_fwd(h1, W, 0, cosf, sins, after=[token])
    h, s2 = _ffn_fwd(h2, W, "ffn2", 0, ln("ln3_g", 0), ln("ln3_b", 0), "l0_ffn2")
    saved0 = (s1, sm, s2)
    shards, lands = _ici_wait(gather_l(1), g_send, g_recv, g_srcs, g_lands, h[0], name="gather_ici_l1_wait")
    use_gathered(1, MATMUL_W, _gather_d2d(shards, lands, 1, "l1"))
    h, saved1 = _layer_fwd(h, W, 1, cosf, sins)
    tgt = jnp.concatenate([jnp.zeros((BLOCK, D_MODEL), F32), loss_target[0]], axis=0)
    dh, loss = _loss_head(h[0], tgt, name="loss_head")
    G = [None] * DEPTH
    dh, G[1] = _layer_bwd(dh, saved1, W, 1, cosf, sins)
    ffn2_w, mix_w, ffn1_w = MATMUL_W[7:], MATMUL_W[3:7], MATMUL_W[:3]
    sw_l1 = swap_start(G[1], MATMUL_W, "l1")
    G0 = {}
    dh, dg, db = _ffn_bwd(dh, s2, W, "ffn2", 0, ln("ln3_g", 0) + sw_l1[4][0, 0], ln("ln3_b", 0), G0, "l0_ffn2")
    G0["ln3_g"], G0["ln3_b"] = dg[0], db[0]
    pairs_l1, x_l1 = exchange_start(sw_l1, MATMUL_W, "l1", dh)
    sw_a = swap_start(G0, ffn2_w, "l0_ffn2")
    dh, Gm = _mixer_bwd(dh, sm, W, 0, cosf, sins, G0, zero=x_l1[4][0, 0] + sw_a[4][0, 0])
    G0.update(Gm)
    pairs_a, x_a = exchange_start(sw_a, ffn2_w, "l0_ffn2", dh)
    reduced1 = rs_end(pairs_l1, x_l1, MATMUL_W, "l1", dh)
    pairs_b, x_b = rs_start(G0, mix_w, "l0_mix")
    dh0, dg, db = _ffn_bwd(dh, s1, W, "ffn1", 0, ln("ln1_g", 0) + (x_a[4][0, 0] + x_b[4][0, 0]), ln("ln1_b", 0), G0,
                           "l0_ffn1")
    G0["ln1_g"], G0["ln1_b"] = dg[0], db[0]
    G[0] = G0
    reduced0 = rs_end(pairs_a, x_a, ffn2_w, "l0_ffn2", dh0)
    reduced0.update(rs_end(pairs_b, x_b, mix_w, "l0_mix", dh0))

    small_parts = [jnp.stack([G[l][n] for l in range(DEPTH)]).reshape(-1) for n in SMALL]
    small_parts += [jnp.stack([G[l]["conv_w"] for l in range(DEPTH)]).reshape(-1), dh0[PAD_ROWS:BLOCK].reshape(-1),
                    loss[0, :1]]
    flat = jnp.concatenate(small_parts)
    flat = jnp.pad(flat, (0, SMALL_ROWS * BLOCK - flat.shape[0]))
    red2d = _allreduce_small(flat.reshape(SMALL_ROWS, BLOCK))
    red = red2d.reshape(-1)

    pairs_c = _rs_pair_sums(chunk_grads(G0, ffn1_w), ffn1_w, "l0_ffn1")
    x_c = _ici_start(_exchange_copies, [p[1] for p in pairs_c], _exchange_land_shapes(pairs_c),
                     name="rs_exchange_l0_ffn1_start", after=[red2d])
    grads, off = {}, 0
    for n in SMALL:
        size = int(np.prod(w[n].shape))
        grads[n] = red[off:off + size].reshape(w[n].shape)
        off += size
    conv_full = red[off:off + DEPTH * SSD_CONV * 768].reshape(DEPTH, SSD_CONV, 768)
    off += DEPTH * SSD_CONV * 768
    dmeta_full = red[off:off + N_META * D_MODEL].reshape(N_META, D_MODEL)
    off += N_META * D_MODEL
    loss_out = red[off]
    grads["conv_w"] = _chip_cols(conv_full, chip, 768 // N_CHIPS)
    grads["meta"] = _chip_cols(dmeta_full, chip, D_MODEL // N_CHIPS)

    delta, new_m, new_v = {}, {}, {}

    def adamw_matmul_weights(names, after):
        for n in names:
            gs = [reduced0[n], reduced1[n]]
            if n.endswith("w_gate") or n.endswith("w_up"):
                res = _adamw(tr(w[n]), gs, tr(m[n]), tr(v[n]), name=f"adamw_{n}", after=after)
                grads[n], delta[n], new_m[n], new_v[n] = [tr(r) for r in res]
            else:
                grads[n], delta[n], new_m[n], new_v[n] = _adamw(w[n], gs, m[n], v[n], name=f"adamw_{n}", after=after)

    adamw_matmul_weights(ffn2_w + mix_w, [x_c[4]])
    rest = [n for n in WEIGHTS if n not in MATMUL_W]

    def pack_small(d):
        f = jnp.concatenate([d[n].reshape(-1) for n in rest])
        tot = -(-f.shape[0] // (8 * BLOCK)) * 8 * BLOCK
        return jnp.pad(f, (0, tot - f.shape[0])).reshape(-1, BLOCK)

    _, d2, m2, v2 = _adamw(pack_small(w), [pack_small(grads)], pack_small(m), pack_small(v), name="adamw_small",
                           after=[x_c[4]])
    reduced0.update(rs_end(pairs_c, x_c, ffn1_w, "l0_ffn1", d2))
    adamw_matmul_weights(ffn1_w, [])
    off = 0
    for n in rest:
        size = int(np.prod(w[n].shape))
        for dst, src in ((delta, d2), (new_m, m2), (new_v, v2)):
            dst[n] = src.reshape(-1)[off:off + size].reshape(w[n].shape)
        off += size

    grad_x = dh0[BLOCK:][None]
    return (loss_out, grad_x, *[grads[n] for n in WEIGHTS], *[delta[n] for n in WEIGHTS],
            *[new_m[n] for n in WEIGHTS], *[new_v[n] for n in WEIGHTS])
```

```python
import functools

import numpy as np
import jax
import jax.numpy as jnp
from jax import lax
from jax.experimental import pallas as pl
from jax.experimental.pallas import tpu as pltpu

F32 = jnp.float32
BF16 = jnp.bfloat16
MESH = pl.DeviceIdType.MESH

D_MODEL = 1024
SEQ = 2048
N_META = 16
BLOCK = 128
PAD_ROWS = 112
LP = PAD_ROWS + N_META + SEQ
N_CHUNK = LP // BLOCK
DEPTH = 2
D_FF = 2816
N_CHIPS = 4
FF_SHARD = D_FF // N_CHIPS
HP = 768
FP = N_CHIPS * HP
SSD_HEADS, SSD_HD, SSD_D, SSD_GROUPS, SSD_STATE, SSD_CONV = 8, 64, 512, 2, 64, 4
FOX_HEADS, FOX_HD, FOX_D = 4, 64, 256
MLA_HEADS, MLA_Q_LORA, MLA_KV_LORA, MLA_NOPE, MLA_ROPE, MLA_V, MLA_D = 4, 256, 128, 64, 32, 64, 256
ROPE_HALF = MLA_ROPE // 2
ROPE_THETA = 10000.0
N_IN = 2476
IN_SHARD = N_IN // N_CHIPS
IN_SHARD_P = 640
ALPHA = (2 * DEPTH) ** 0.25
EPS = 1e-5
ADAM_LR, ADAM_B1, ADAM_B2, ADAM_EPS, ADAM_WD, ADAM_STEP = 0.001, 0.9, 0.999, 1e-08, 0.01, 10
NEG = -1e30
TM = 544

VMEM_LIMIT_BYTES = 56 * 1024 * 1024

PC_Z, PC_XBC, PC_FQ, PC_FK, PC_FV, PC_CQ, PC_CKV, PC_DT, PC_FR, PC_KR, PC_END = (
    0, 512, 1280, 1536, 1792, 2048, 2304, 2432, 2560, 2688, 2816)
OC_Z, OC_XBC, OC_DT, OC_FQ, OC_FK, OC_FV, OC_FR, OC_CQ, OC_CKV, OC_KR = (
    0, 512, 1280, 1288, 1544, 1800, 2056, 2060, 2316, 2444)


def _cparams(sem=None):
    return pltpu.CompilerParams(dimension_semantics=sem, vmem_limit_bytes=VMEM_LIMIT_BYTES)


def _tile(n, cap, mult):
    best = None
    for t in range(mult, min(n, cap) + 1, mult):
        if n % t == 0:
            best = t
    return best if best is not None else n


def _bs(shape, fn):
    return pl.BlockSpec(shape, fn)


ANY = pl.BlockSpec(memory_space=pl.ANY)


def _dims(ca, cb):
    return (((ca,), (cb,)), ((), ()))


def _raw_bdot(a, b, ca, cb):
    return lax.dot_general(a.astype(BF16), b.astype(BF16), _dims(ca, cb), preferred_element_type=F32)


def _mm_core(a, b, *, a_spec, b_spec, o_spec, grid, out_shape, ca, cb, name, add=None, after=()):
    nk = grid[2]
    has_add = add is not None
    acc_shape = tuple(d for d in o_spec.block_shape if d is not None)

    def body(*refs):
        a_ref, b_ref = refs[0], refs[1]
        add_ref = refs[2] if has_add else None
        o_ref, acc_ref = refs[-2], refs[-1]
        k = pl.program_id(2)

        @pl.when(k == 0)
        def _():
            acc_ref[...] = jnp.zeros_like(acc_ref)

        acc_ref[...] += _raw_bdot(a_ref[...], b_ref[...], ca, cb)

        @pl.when(k == nk - 1)
        def _():
            r = acc_ref[...]
            if has_add:
                r = r + add_ref[...]
            o_ref[...] = r

    ins = [a, b] + ([add] if has_add else []) + list(after)
    in_specs = [a_spec, b_spec] + ([o_spec] if has_add else []) + [ANY] * len(after)
    return pl.pallas_call(
        body, name=name, grid=grid, in_specs=in_specs, out_specs=o_spec,
        out_shape=jax.ShapeDtypeStruct(out_shape, F32), scratch_shapes=[pltpu.VMEM(acc_shape, F32)],
        compiler_params=_cparams(("parallel", "parallel", "arbitrary")),
    )(*ins)


MM_VMEM_BUDGET = 40 * 1024 * 1024


def _divisors(n, mult):
    return [t for t in range(mult, n + 1, mult) if n % t == 0] or [n]


def _pick_tiles(M, N, K, a_bytes, b_bytes, ta, has_add):
    best = None
    for tm in _divisors(M, 128 if ta else 16):
        for tn in _divisors(N, 128):
            vmem = 2 * tm * K * a_bytes + 2 * K * tn * b_bytes + (3 + 2 * int(has_add)) * tm * tn * 4
            if vmem <= MM_VMEM_BUDGET:
                key = ((M // tm) * (N // tn), -tn)
                if best is None or key < best[0]:
                    best = (key, tm, tn)
    assert best is not None, (M, N, K)
    return best[1], best[2], K


def _mm(a, b, *, ta=False, tb=False, add=None, name, after=()):
    if ta:
        K, M = a.shape
    else:
        M, K = a.shape
    if tb:
        N, Kb = b.shape
    else:
        Kb, N = b.shape
    assert K == Kb, (a.shape, b.shape, ta, tb)
    tm, tn, tk = _pick_tiles(M, N, K, a.dtype.itemsize, b.dtype.itemsize, ta, add is not None)
    a_spec = _bs((tk, tm), lambda i, j, k: (k, i)) if ta else _bs((tm, tk), lambda i, j, k: (i, k))
    b_spec = _bs((tn, tk), lambda i, j, k: (j, k)) if tb else _bs((tk, tn), lambda i, j, k: (k, j))
    return _mm_core(a, b, a_spec=a_spec, b_spec=b_spec, o_spec=_bs((tm, tn), lambda i, j, k: (i, j)),
                    grid=(M // tm, N // tn, K // tk), out_shape=(M, N), ca=0 if ta else 1, cb=1 if tb else 0,
                    name=name, add=add, after=after)


def _row_entry(r, ncol):
    if isinstance(r, tuple):
        return r
    return r, r.shape[1] // ncol, 0


def _rowwise(fn, rows, pars, out_cols, *, name, tile, ncol=1, out_dtypes=None):
    rows = [_row_entry(r, ncol) for r in rows]
    L = rows[0][0].shape[0]
    nr, npar = len(rows), len(pars)
    in_specs = [_bs((tile, w), lambda g, i, o=o: (i, o + g)) for _, w, o in rows]
    in_specs += [_bs((p.shape[0], p.shape[1] // ncol), lambda g, i: (0, g)) for p in pars]
    out_specs = [_bs((tile, c // ncol), lambda g, i: (i, g)) for c in out_cols]

    def body(*refs):
        ins, outs = refs[:nr + npar], refs[nr + npar:]
        row0 = pl.program_id(1) * tile
        res = fn(row0, *[r[...] for r in ins])
        for o, v in zip(outs, res):
            o[...] = v.astype(o.dtype)

    return pl.pallas_call(
        body, name=name, grid=(ncol, L // tile), in_specs=in_specs, out_specs=out_specs,
        out_shape=[jax.ShapeDtypeStruct((L, c), d) for c, d in zip(out_cols, out_dtypes or [F32] * len(out_cols))],
        compiler_params=_cparams(("parallel", "parallel")),
    )(*[r[0] for r in rows], *pars)


def _rowwise_bwd(fn, rows, pars, douts, *, name, tile, ncol=1, row_grad=None, grad_dtypes=None):
    rows = [_row_entry(r, ncol) for r in rows]
    L = rows[0][0].shape[0]
    nr, npar, nd = len(rows), len(pars), len(douts)
    row_grad = [True] * nr if row_grad is None else row_grad
    in_specs = [_bs((tile, w), lambda g, i, o=o: (i, o + g)) for _, w, o in rows]
    in_specs += [_bs((p.shape[0], p.shape[1] // ncol), lambda g, i: (0, g)) for p in pars]
    in_specs += [_bs((tile, d.shape[1] // ncol), lambda g, i: (i, g)) for d in douts]
    g_widths = [w * ncol for (_, w, _), f in zip(rows, row_grad) if f]
    out_specs = [_bs((tile, w // ncol), lambda g, i: (i, g)) for w in g_widths]
    out_specs += [_bs((p.shape[0], p.shape[1] // ncol), lambda g, i: (0, g)) for p in pars]
    out_shape = [jax.ShapeDtypeStruct((L, w), d) for w, d in zip(g_widths, grad_dtypes or [F32] * len(g_widths))]
    out_shape += [jax.ShapeDtypeStruct(p.shape, F32) for p in pars]

    def body(*refs):
        ins = refs[:nr + npar]
        dos = refs[nr + npar:nr + npar + nd]
        outs = refs[nr + npar + nd:]
        i = pl.program_id(1)
        row0 = i * tile
        _, vjp = jax.vjp(lambda *a: tuple(fn(row0, *a)), *[r[...] for r in ins])
        grads = vjp(tuple(d[...].astype(F32) for d in dos))
        o = 0
        for j in range(nr):
            if row_grad[j]:
                outs[o][...] = grads[j].astype(outs[o].dtype)
                o += 1
        for j in range(npar):
            g, ref = grads[nr + j], outs[o + j]

            @pl.when(i == 0)
            def _(g=g, ref=ref):
                ref[...] = g

            @pl.when(i > 0)
            def _(g=g, ref=ref):
                ref[...] += g

    res = pl.pallas_call(
        body, name=name, grid=(ncol, L // tile), in_specs=in_specs, out_specs=out_specs, out_shape=out_shape,
        compiler_params=_cparams(("parallel", "arbitrary")),
    )(*[r[0] for r in rows], *pars, *douts)
    return res[:len(g_widths)], res[len(g_widths):]


def _sigmoid(x):
    return 1.0 / (1.0 + jnp.exp(-x))


def _softplus(x):
    return jnp.maximum(x, 0.0) + jnp.log(1.0 + jnp.exp(-jnp.abs(x)))


def _silu(x):
    return x * _sigmoid(x)


def _make_res_ln_fn(scale):
    def fn(row0, h, o, gam, bet):
        pre = ALPHA * h + scale * o
        mu = jnp.mean(pre, axis=-1, keepdims=True)
        xc = pre - mu
        var = jnp.mean(xc * xc, axis=-1, keepdims=True)
        return (xc * lax.rsqrt(var + EPS) * gam + bet,)
    return fn


def _ssd_post_fn(row0, y, xs, z, dskip, normg):
    v = (y + dskip * xs) * _silu(z)
    v = v * lax.rsqrt(jnp.mean(v * v, axis=-1, keepdims=True) + EPS)
    return (v * normg,)


def _mla_norm_fn(row0, cq, ckv, gq, gkv):
    qn = cq * lax.rsqrt(jnp.mean(cq * cq, axis=-1, keepdims=True) + EPS) * gq
    cn = ckv * lax.rsqrt(jnp.mean(ckv * ckv, axis=-1, keepdims=True) + EPS) * gkv
    return qn, cn


def _rope_fn(row0, q, k, cosf, sins):
    return (q * cosf + pltpu.roll(q, 64, 1) * sins, k * cosf + pltpu.roll(k, 64, 1) * sins)


def _rope_t_fn(row0, gq, gk, cosf, sins):
    return (gq * cosf + pltpu.roll(gq * sins, 64, 1), gk * cosf + pltpu.roll(gk * sins, 64, 1))


def _conv_fwd(x, x_off, w, b, *, name):
    C = w.shape[1]

    def body(x_ref, w_ref, b_ref, o_ref):
        rows = lax.broadcasted_iota(jnp.int32, (LP, BLOCK), 0)
        xv = jnp.where(rows >= PAD_ROWS, x_ref[...], 0.0)
        acc = b_ref[...] + w_ref[3:4, :] * xv
        for k in range(SSD_CONV - 1):
            acc = acc + w_ref[k:k + 1, :] * pltpu.roll(xv, SSD_CONV - 1 - k, 0)
        o_ref[...] = _silu(acc)

    return pl.pallas_call(
        body, name=name, grid=(C // BLOCK,),
        in_specs=[_bs((LP, BLOCK), lambda j: (0, j + x_off)), _bs((SSD_CONV, BLOCK), lambda j: (0, j)),
                  _bs((1, BLOCK), lambda j: (0, j))],
        out_specs=_bs((LP, BLOCK), lambda j: (0, j)),
        out_shape=jax.ShapeDtypeStruct((LP, C), F32), compiler_params=_cparams(("parallel",)),
    )(x, w, b)


def _conv_bwd(x, x_off, w, b, dout, *, name):
    C = w.shape[1]

    def body(x_ref, w_ref, b_ref, do_ref, dx_ref, dw_ref, db_ref):
        rows = lax.broadcasted_iota(jnp.int32, (LP, BLOCK), 0)
        real = rows >= PAD_ROWS
        xv = jnp.where(real, x_ref[...], 0.0)
        shifted = [pltpu.roll(xv, SSD_CONV - 1 - k, 0) for k in range(SSD_CONV - 1)] + [xv]
        acc = b_ref[...]
        for k in range(SSD_CONV):
            acc = acc + w_ref[k:k + 1, :] * shifted[k]
        sig = _sigmoid(acc)
        dacc = jnp.where(real, do_ref[...] * (sig * (1.0 + acc * (1.0 - sig))), 0.0)
        db_ref[...] = jnp.sum(dacc, axis=0, keepdims=True)
        dx = w_ref[3:4, :] * dacc
        for k in range(SSD_CONV):
            dw_ref[k:k + 1, :] = jnp.sum(dacc * shifted[k], axis=0, keepdims=True)
            if k < SSD_CONV - 1:
                dx = dx + w_ref[k:k + 1, :] * pltpu.roll(dacc, LP - (SSD_CONV - 1 - k), 0)
        dx_ref[...] = jnp.where(real, dx, 0.0)

    return pl.pallas_call(
        body, name=name, grid=(C // BLOCK,),
        in_specs=[_bs((LP, BLOCK), lambda j: (0, j + x_off)), _bs((SSD_CONV, BLOCK), lambda j: (0, j)),
                  _bs((1, BLOCK), lambda j: (0, j)), _bs((LP, BLOCK), lambda j: (0, j))],
        out_specs=[_bs((LP, BLOCK), lambda j: (0, j)), _bs((SSD_CONV, BLOCK), lambda j: (0, j)),
                   _bs((1, BLOCK), lambda j: (0, j))],
        out_shape=[jax.ShapeDtypeStruct((LP, C), F32), jax.ShapeDtypeStruct((SSD_CONV, C), F32),
                   jax.ShapeDtypeStruct((1, C), F32)],
        compiler_params=_cparams(("parallel",)),
    )(x, w, b, dout)


_BDIMS = {"nn": (((2,), (1,)), ((0,), (0,))), "nt": (((2,), (2,)), ((0,), (0,))), "tn": (((1,), (1,)), ((0,), (0,)))}


def _raw_bdot3(a, b, mode):
    return lax.dot_general(a.astype(BF16), b.astype(BF16), _BDIMS[mode], preferred_element_type=F32)


@functools.partial(jax.custom_vjp, nondiff_argnums=(2,))
def _bdot3(a, b, mode):
    return _raw_bdot3(a, b, mode)


def _bdot3_fwd(a, b, mode):
    return _raw_bdot3(a, b, mode), (a, b)


def _bdot3_bwd(mode, res, g):
    a, b = res
    if mode == "nn":
        return _raw_bdot3(g, b, "nt"), _raw_bdot3(a, g, "tn")
    if mode == "nt":
        return _raw_bdot3(g, b, "nn"), _raw_bdot3(g, a, "tn")
    return _raw_bdot3(b, g, "nt"), _raw_bdot3(a, g, "nn")


_bdot3.defvjp(_bdot3_fwd, _bdot3_bwd)


def _ssd_chunk(x, bm, cm, dt, dtt, alog, prev):
    rep = SSD_HEADS // SSD_GROUPS
    per_head = lambda t: jnp.broadcast_to(t[:, None], (SSD_GROUPS, rep) + t.shape[1:]).reshape((SSD_HEADS,) + t.shape[1:])
    bm, cm = per_head(bm), per_head(cm)
    lane_h = lax.broadcasted_iota(jnp.int32, (1, BLOCK), 1)
    row_h = lax.broadcasted_iota(jnp.int32, (BLOCK, 1), 0)
    dtc = jnp.stack([jnp.sum(jnp.where(lane_h == h, dt, 0.0), axis=1, keepdims=True) for h in range(SSD_HEADS)])
    dtr = jnp.stack([jnp.sum(jnp.where(row_h == h, dtt, 0.0), axis=0, keepdims=True) for h in range(SSD_HEADS)])
    lane = lax.broadcasted_iota(jnp.int32, alog.shape, 2)
    a_neg = -jnp.exp(jnp.sum(jnp.where(lane == 0, alog, 0.0), axis=2, keepdims=True))
    ac_in = dtc * a_neg
    ar_in = dtr * a_neg
    li = lax.broadcasted_iota(jnp.int32, (1, BLOCK, BLOCK), 1)
    si = lax.broadcasted_iota(jnp.int32, (1, BLOCK, BLOCK), 2)
    causal = li >= si
    acum_c = jnp.sum(jnp.where(causal, ar_in, 0.0), axis=2, keepdims=True)
    acum_r = jnp.sum(jnp.where(li <= si, ac_in, 0.0), axis=1, keepdims=True)
    total = jnp.sum(ar_in, axis=2, keepdims=True)
    seg = jnp.exp(jnp.where(causal, acum_c - acum_r, NEG))
    xdt = x * dtc
    cb = _bdot3(cm, bm, "nt")
    y = _bdot3(cb * seg, xdt, "nn") + _bdot3(cm, prev, "nt") * jnp.exp(acum_c)
    st = _bdot3(xdt, bm * jnp.exp(total - acum_c), "tn")
    return y, prev * jnp.exp(total) + st


def _ssd_dt_fwd(raw, raw_blk, bias, *, name):
    def body(raw_ref, b_ref, dt_ref, dtt_ref):
        rows = pl.program_id(0) * BLOCK + lax.broadcasted_iota(jnp.int32, (BLOCK, BLOCK), 0)
        dt = jnp.where(rows >= PAD_ROWS, _softplus(raw_ref[...] + b_ref[...]), 0.0)
        dt_ref[...] = dt
        dtt_ref[...] = dt.T

    return pl.pallas_call(
        body, name=name, grid=(N_CHUNK,),
        in_specs=[_bs((BLOCK, BLOCK), lambda j: (j, raw_blk)), _bs((1, BLOCK), lambda j: (0, 0))],
        out_specs=[_bs((BLOCK, BLOCK), lambda j: (j, 0)), _bs((BLOCK, BLOCK), lambda j: (0, j))],
        out_shape=[jax.ShapeDtypeStruct((LP, BLOCK), F32), jax.ShapeDtypeStruct((BLOCK, LP), F32)],
        compiler_params=_cparams(("parallel",)),
    )(raw, bias)


def _ssd_dt_bwd(raw, raw_blk, bias, ddt, ddtt, *, name):
    def body(raw_ref, b_ref, ddt_ref, ddtt_ref, draw_ref, db_ref):
        j = pl.program_id(0)
        rows = j * BLOCK + lax.broadcasted_iota(jnp.int32, (BLOCK, BLOCK), 0)
        g = ddt_ref[...] + ddtt_ref[...].T
        draw = jnp.where(rows >= PAD_ROWS, g * _sigmoid(raw_ref[...] + b_ref[...]), 0.0)
        draw_ref[...] = draw
        dsum = jnp.sum(draw, axis=0, keepdims=True)

        @pl.when(j == 0)
        def _():
            db_ref[...] = dsum

        @pl.when(j > 0)
        def _():
            db_ref[...] += dsum

    return pl.pallas_call(
        body, name=name, grid=(N_CHUNK,),
        in_specs=[_bs((BLOCK, BLOCK), lambda j: (j, raw_blk)), _bs((1, BLOCK), lambda j: (0, 0)),
                  _bs((BLOCK, BLOCK), lambda j: (j, 0)), _bs((BLOCK, BLOCK), lambda j: (0, j))],
        out_specs=[_bs((BLOCK, BLOCK), lambda j: (j, 0)), _bs((1, BLOCK), lambda j: (0, 0))],
        out_shape=[jax.ShapeDtypeStruct((LP, BLOCK), F32), jax.ShapeDtypeStruct((1, BLOCK), F32)],
        compiler_params=_cparams(("arbitrary",)),
    )(raw, bias, ddt, ddtt)


def _ssd_specs(rev):
    ci = (lambda c: N_CHUNK - 1 - c) if rev else (lambda c: c)
    x_spec = _bs((SSD_HEADS, BLOCK, SSD_HD), lambda c: (0, ci(c), 0))
    g_spec = _bs((SSD_GROUPS, BLOCK, SSD_STATE), lambda c: (0, ci(c), 0))
    dtc_spec = _bs((BLOCK, BLOCK), lambda c: (ci(c), 0))
    dtr_spec = _bs((BLOCK, BLOCK), lambda c: (0, ci(c)))
    al_spec = _bs((SSD_HEADS, 1, BLOCK), lambda c: (0, 0, 0))
    st_spec = _bs((None, SSD_HEADS, SSD_HD, SSD_STATE), lambda c: (ci(c), 0, 0, 0))
    return x_spec, g_spec, dtc_spec, dtr_spec, al_spec, st_spec


def _ssd_fwd(x, bm, cm, dtc, dtr, alog, *, name):
    x_spec, g_spec, dtc_spec, dtr_spec, al_spec, st_spec = _ssd_specs(False)

    def body(x_ref, b_ref, c_ref, dtc_ref, dtr_ref, al_ref, y_ref, prev_ref, state):
        @pl.when(pl.program_id(0) == 0)
        def _():
            state[...] = jnp.zeros_like(state)

        prev = state[...]
        prev_ref[...] = prev
        y, new = _ssd_chunk(x_ref[...], b_ref[...], c_ref[...], dtc_ref[...], dtr_ref[...], al_ref[...], prev)
        y_ref[...] = y
        state[...] = new

    return pl.pallas_call(
        body, name=name, grid=(N_CHUNK,),
        in_specs=[x_spec, g_spec, g_spec, dtc_spec, dtr_spec, al_spec], out_specs=[x_spec, st_spec],
        out_shape=[jax.ShapeDtypeStruct((SSD_HEADS, LP, SSD_HD), F32),
                   jax.ShapeDtypeStruct((N_CHUNK, SSD_HEADS, SSD_HD, SSD_STATE), F32)],
        scratch_shapes=[pltpu.VMEM((SSD_HEADS, SSD_HD, SSD_STATE), F32)],
        compiler_params=_cparams(("arbitrary",)),
    )(x, bm, cm, dtc, dtr, alog)


def _ssd_bwd(x, bm, cm, dtc, dtr, alog, prevs, dy, *, name):
    x_spec, g_spec, dtc_spec, dtr_spec, al_spec, st_spec = _ssd_specs(True)

    def body(x_ref, b_ref, c_ref, dtc_ref, dtr_ref, al_ref, prev_ref, dy_ref,
             dx_ref, db_ref, dc_ref, ddtc_ref, ddtr_ref, dal_ref, dstate):
        c = pl.program_id(0)

        @pl.when(c == 0)
        def _():
            dstate[...] = jnp.zeros_like(dstate)

        _, vjp = jax.vjp(_ssd_chunk, x_ref[...], b_ref[...], c_ref[...], dtc_ref[...], dtr_ref[...], al_ref[...],
                         prev_ref[...])
        dx, db, dc, ddtc, ddtr, dal, dprev = vjp((dy_ref[...], dstate[...]))
        dx_ref[...] = dx
        db_ref[...] = db
        dc_ref[...] = dc
        ddtc_ref[...] = ddtc
        ddtr_ref[...] = ddtr
        dstate[...] = dprev

        @pl.when(c == 0)
        def _():
            dal_ref[...] = dal

        @pl.when(c > 0)
        def _():
            dal_ref[...] += dal

    hs = jax.ShapeDtypeStruct((SSD_HEADS, LP, SSD_HD), F32)
    gs = jax.ShapeDtypeStruct((SSD_GROUPS, LP, SSD_STATE), F32)
    return pl.pallas_call(
        body, name=name, grid=(N_CHUNK,),
        in_specs=[x_spec, g_spec, g_spec, dtc_spec, dtr_spec, al_spec, st_spec, x_spec],
        out_specs=[x_spec, g_spec, g_spec, dtc_spec, dtr_spec, al_spec],
        out_shape=[hs, gs, gs, jax.ShapeDtypeStruct((LP, BLOCK), F32),
                   jax.ShapeDtypeStruct((BLOCK, LP), F32), jax.ShapeDtypeStruct((SSD_HEADS, 1, BLOCK), F32)],
        scratch_shapes=[pltpu.VMEM((SSD_HEADS, SSD_HD, SSD_STATE), F32)],
        compiler_params=_cparams(("arbitrary",)),
    )(x, bm, cm, dtc, dtr, alog, prevs, dy)


def _tri_dot(tri, v):
    hi = v.astype(BF16)
    r1 = v - hi.astype(F32)
    mid = r1.astype(BF16)
    lo = (r1 - mid.astype(F32)).astype(BF16)
    t = tri.astype(BF16)
    d = lambda p: lax.dot_general(t, p, _dims(1, 0), preferred_element_type=F32)
    return d(hi) + d(mid) + d(lo)


def _fox_gate_fwd(raw, raw_blk, bias, *, name):
    def body(raw_ref, b_ref, c_ref, ct_ref, carry):
        j = pl.program_id(0)

        @pl.when(j == 0)
        def _():
            carry[...] = jnp.zeros_like(carry)

        rows = j * BLOCK + lax.broadcasted_iota(jnp.int32, (BLOCK, BLOCK), 0)
        lf = jnp.where(rows >= PAD_ROWS, -_softplus(-(raw_ref[...] + b_ref[...])), 0.0)
        li = lax.broadcasted_iota(jnp.int32, (BLOCK, BLOCK), 0)
        si = lax.broadcasted_iota(jnp.int32, (BLOCK, BLOCK), 1)
        cv = _tri_dot(jnp.where(li >= si, 1.0, 0.0), lf) + carry[...]
        c_ref[...] = cv
        ct_ref[...] = cv.T
        carry[...] += jnp.sum(lf, axis=0, keepdims=True)

    return pl.pallas_call(
        body, name=name, grid=(N_CHUNK,),
        in_specs=[_bs((BLOCK, BLOCK), lambda j: (j, raw_blk)), _bs((1, BLOCK), lambda j: (0, 0))],
        out_specs=[_bs((BLOCK, BLOCK), lambda j: (j, 0)), _bs((BLOCK, BLOCK), lambda j: (0, j))],
        out_shape=[jax.ShapeDtypeStruct((LP, BLOCK), F32), jax.ShapeDtypeStruct((BLOCK, LP), F32)],
        scratch_shapes=[pltpu.VMEM((1, BLOCK), F32)], compiler_params=_cparams(("arbitrary",)),
    )(raw, bias)


def _fox_gate_bwd(raw, raw_blk, bias, dc, dct, *, name):
    rj = lambda j: N_CHUNK - 1 - j

    def body(raw_ref, b_ref, dc_ref, dct_ref, draw_ref, db_ref, carry):
        j = pl.program_id(0)

        @pl.when(j == 0)
        def _():
            carry[...] = jnp.zeros_like(carry)

        rows = (N_CHUNK - 1 - j) * BLOCK + lax.broadcasted_iota(jnp.int32, (BLOCK, BLOCK), 0)
        li = lax.broadcasted_iota(jnp.int32, (BLOCK, BLOCK), 0)
        si = lax.broadcasted_iota(jnp.int32, (BLOCK, BLOCK), 1)
        dcv = dc_ref[...] + dct_ref[...].T
        dlf = _tri_dot(jnp.where(li <= si, 1.0, 0.0), dcv) + carry[...]
        carry[...] += jnp.sum(dcv, axis=0, keepdims=True)
        draw = jnp.where(rows >= PAD_ROWS, dlf * (1.0 - _sigmoid(raw_ref[...] + b_ref[...])), 0.0)
        draw_ref[...] = draw
        dsum = jnp.sum(draw, axis=0, keepdims=True)

        @pl.when(j == 0)
        def _():
            db_ref[...] = dsum

        @pl.when(j > 0)
        def _():
            db_ref[...] += dsum

    return pl.pallas_call(
        body, name=name, grid=(N_CHUNK,),
        in_specs=[_bs((BLOCK, BLOCK), lambda j: (rj(j), raw_blk)), _bs((1, BLOCK), lambda j: (0, 0)),
                  _bs((BLOCK, BLOCK), lambda j: (rj(j), 0)), _bs((BLOCK, BLOCK), lambda j: (0, rj(j)))],
        out_specs=[_bs((BLOCK, BLOCK), lambda j: (rj(j), 0)), _bs((1, BLOCK), lambda j: (0, 0))],
        out_shape=[jax.ShapeDtypeStruct((LP, BLOCK), F32), jax.ShapeDtypeStruct((1, BLOCK), F32)],
        scratch_shapes=[pltpu.VMEM((1, BLOCK), F32)], compiler_params=_cparams(("arbitrary",)),
    )(raw, bias, dc, dct)


ATT_W = 256
ATT_QB = 272
ATT_STEPS = LP // ATT_QB
ATT_KEYS = (640, 1152, 1664, LP)


def _lane_head(width, per, mod=None):
    lane = lax.broadcasted_iota(jnp.int32, (1, width), 1)
    if mod is not None:
        lane = lane % mod
    return lane // per


def _attn_mask(i, kw):
    r = i * ATT_QB + lax.broadcasted_iota(jnp.int32, (ATT_QB, kw), 0)
    c = lax.broadcasted_iota(jnp.int32, (ATT_QB, kw), 1)
    return (c <= r) & ((c >= PAD_ROWS) | (r < PAD_ROWS))


def _attn_by_key_class(i, fn):
    for p, kw in enumerate(ATT_KEYS):
        @pl.when(i // 2 == p)
        def _(kw=kw):
            fn(kw)


def _attn_specs(q, k, v, bias, rope):
    qspec = lambda blk, w=ATT_W: _bs((ATT_QB, w), lambda i: (i, blk))
    fspec = lambda blk, w=ATT_W: _bs((LP, w), lambda i: (0, blk))
    ins = [q[0], k[0], v[0]]
    specs = [qspec(q[1]), fspec(k[1]), fspec(v[1])]
    if bias is not None:
        ins += [bias[0], bias[1]]
        specs += [qspec(0, BLOCK), _bs((BLOCK, LP), lambda i: (0, 0))]
    if rope is not None:
        ins += [rope[0][0], rope[1][0]]
        specs += [qspec(rope[0][1], BLOCK), fspec(rope[1][1], BLOCK)]
    return ins, specs, qspec, fspec


def _attn_fwd(q, k, v, *, scale, name, bias=None, rope=None):
    ins, specs, qspec, fspec = _attn_specs(q, k, v, bias, rope)
    has_bias, has_rope = bias is not None, rope is not None

    def body(*refs):
        it = iter(refs)
        q_ref, k_ref, v_ref = next(it), next(it), next(it)
        if has_bias:
            c_ref, ct_ref = next(it), next(it)
        if has_rope:
            qr_ref, kr_ref = next(it), next(it)
        o_ref, lse_ref = next(it), next(it)
        i = pl.program_id(0)

        def block(kw):
            ok = _attn_mask(i, kw)
            qv, kv, vv = q_ref[...].astype(BF16), k_ref[0:kw, :].astype(BF16), v_ref[0:kw, :].astype(BF16)
            hid, l128 = _lane_head(ATT_W, FOX_HD), _lane_head(BLOCK, 1)
            if has_rope:
                rid = _lane_head(BLOCK, ROPE_HALF, 64)
                qrv, krv = qr_ref[...].astype(BF16), kr_ref[0:kw, :].astype(BF16)
            def head(h, carry):
                o_acc, lse_acc = carry
                s = _raw_bdot(jnp.where(hid == h, qv, 0.0), kv, 1, 1)
                if has_rope:
                    s = s + _raw_bdot(jnp.where(rid == h, qrv, 0.0), krv, 1, 1)
                s = s * scale
                if has_bias:
                    cq = jnp.sum(jnp.where(l128 == h, c_ref[...], 0.0), axis=1, keepdims=True)
                    s = s + (cq - ct_ref[pl.ds(h, 1), 0:kw])
                s = jnp.where(ok, s, NEG)
                m = jnp.max(s, axis=1, keepdims=True)
                p = jnp.exp(s - m)
                l = jnp.sum(p, axis=1, keepdims=True)
                o_acc = jnp.where(hid == h, _raw_bdot(p, vv, 1, 0) / l, o_acc)
                lse_acc = jnp.where(l128 == h, m + jnp.log(l), lse_acc)
                return o_acc, lse_acc

            o_acc, lse_acc = lax.fori_loop(
                0, FOX_HEADS, head, (jnp.zeros((ATT_QB, ATT_W), F32), jnp.zeros((ATT_QB, BLOCK), F32)), unroll=True)
            o_ref[...] = o_acc
            lse_ref[...] = lse_acc

        _attn_by_key_class(i, block)

    return pl.pallas_call(
        body, name=name, grid=(ATT_STEPS,), in_specs=specs, out_specs=[qspec(0), qspec(0, BLOCK)],
        out_shape=[jax.ShapeDtypeStruct((LP, ATT_W), F32), jax.ShapeDtypeStruct((LP, BLOCK), F32)],
        compiler_params=_cparams(("parallel",)),
    )(*ins)


def _attn_bwd(q, k, v, o, lse, do, *, scale, name, bias=None, rope=None):
    ins, specs, qspec, fspec = _attn_specs(q, k, v, bias, rope)
    has_bias, has_rope = bias is not None, rope is not None
    ins += [o, lse, do[0]]
    specs += [qspec(0), qspec(0, BLOCK), qspec(do[1])]

    def body(*refs):
        it = iter(refs)
        q_ref, k_ref, v_ref = next(it), next(it), next(it)
        if has_bias:
            c_ref, ct_ref = next(it), next(it)
        if has_rope:
            qr_ref, kr_ref = next(it), next(it)
        o_ref, lse_ref, do_ref = next(it), next(it), next(it)
        dq_ref, dk_ref, dv_ref = next(it), next(it), next(it)
        if has_bias:
            dc_ref, dct_ref = next(it), next(it)
        if has_rope:
            dqr_ref, dkr_ref = next(it), next(it)
        i = pl.program_id(0)

        @pl.when(i == 0)
        def _():
            dk_ref[...] = jnp.zeros_like(dk_ref)
            dv_ref[...] = jnp.zeros_like(dv_ref)
            if has_rope:
                dkr_ref[...] = jnp.zeros_like(dkr_ref)
            if has_bias:
                dct_ref[...] = jnp.zeros_like(dct_ref)

        def block(kw):
            ok = _attn_mask(i, kw)
            qv, kv, vv = q_ref[...].astype(BF16), k_ref[0:kw, :].astype(BF16), v_ref[0:kw, :].astype(BF16)
            dov, lsev = do_ref[...], lse_ref[...]
            dov_ov = dov * o_ref[...]
            dov = dov.astype(BF16)
            hid, l128 = _lane_head(ATT_W, FOX_HD), _lane_head(BLOCK, 1)
            if has_rope:
                rid = _lane_head(BLOCK, ROPE_HALF, 64)
                qrv, krv = qr_ref[...].astype(BF16), kr_ref[0:kw, :].astype(BF16)

            def head(h, carry):
                dq_acc, aux_acc = carry
                qm = jnp.where(hid == h, qv, 0.0)
                s = _raw_bdot(qm, kv, 1, 1)
                if has_rope:
                    qrm = jnp.where(rid == h, qrv, 0.0)
                    s = s + _raw_bdot(qrm, krv, 1, 1)
                s = s * scale
                if has_bias:
                    cq = jnp.sum(jnp.where(l128 == h, c_ref[...], 0.0), axis=1, keepdims=True)
                    s = s + (cq - ct_ref[pl.ds(h, 1), 0:kw])
                s = jnp.where(ok, s, NEG)
                p = jnp.exp(s - jnp.sum(jnp.where(l128 == h, lsev, 0.0), axis=1, keepdims=True))
                dom = jnp.where(hid == h, dov, 0.0)
                dp = _raw_bdot(dom, vv, 1, 1)
                delta = jnp.sum(jnp.where(hid == h, dov_ov, 0.0), axis=1, keepdims=True)
                ds = p * (dp - delta)
                dsb, pb = ds.astype(BF16), p.astype(BF16)
                dq_acc = jnp.where(hid == h, _raw_bdot(dsb, kv, 1, 0) * scale, dq_acc)
                dk_ref[0:kw, :] += _raw_bdot(dsb, qm, 0, 0) * scale
                dv_ref[0:kw, :] += _raw_bdot(pb, dom, 0, 0)
                if has_rope:
                    aux_acc = jnp.where(rid == h, _raw_bdot(dsb, krv, 1, 0) * scale, aux_acc)
                    dkr_ref[0:kw, :] += _raw_bdot(dsb, qrm, 0, 0) * scale
                if has_bias:
                    aux_acc = jnp.where(l128 == h, jnp.sum(ds, axis=1, keepdims=True), aux_acc)
                    dct_ref[pl.ds(h, 1), 0:kw] -= jnp.sum(ds, axis=0, keepdims=True)
                return dq_acc, aux_acc

            dq_acc, aux_acc = lax.fori_loop(
                0, FOX_HEADS, head, (jnp.zeros((ATT_QB, ATT_W), F32), jnp.zeros((ATT_QB, BLOCK), F32)))
            dq_ref[...] = dq_acc
            if has_bias:
                dc_ref[...] = aux_acc
            if has_rope:
                dqr_ref[...] = aux_acc

        _attn_by_key_class(i, block)

    wide = jax.ShapeDtypeStruct((LP, ATT_W), F32)
    narrow = jax.ShapeDtypeStruct((LP, BLOCK), F32)
    out_specs = [qspec(0), fspec(0), fspec(0)]
    out_shape = [wide, wide, wide]
    if has_bias:
        out_specs += [qspec(0, BLOCK), _bs((BLOCK, LP), lambda i: (0, 0))]
        out_shape += [narrow, jax.ShapeDtypeStruct((BLOCK, LP), F32)]
    if has_rope:
        out_specs += [qspec(0, BLOCK), fspec(0, BLOCK)]
        out_shape += [narrow, narrow]
    return pl.pallas_call(
        body, name=name, grid=(ATT_STEPS,), in_specs=specs, out_specs=out_specs, out_shape=out_shape,
        compiler_params=_cparams(("arbitrary",)),
    )(*ins)


def _loss_head(y, target, *, name):
    tile = 272

    def body(y_ref, t_ref, dy_ref, loss_ref):
        i = pl.program_id(0)
        rows = i * tile + lax.broadcasted_iota(jnp.int32, (tile, D_MODEL), 0)
        err = jnp.where(rows >= BLOCK, y_ref[...] - t_ref[...], 0.0)
        dy_ref[...] = err * (1.0 / D_MODEL)
        part = 0.5 * jnp.sum(jnp.sum(err * err, axis=1, keepdims=True) * (1.0 / D_MODEL), axis=0, keepdims=True)
        part = jnp.broadcast_to(part, (1, BLOCK))

        @pl.when(i == 0)
        def _():
            loss_ref[...] = part

        @pl.when(i > 0)
        def _():
            loss_ref[...] += part

    return pl.pallas_call(
        body, name=name, grid=(LP // tile,),
        in_specs=[_bs((tile, D_MODEL), lambda i: (i, 0)), _bs((tile, D_MODEL), lambda i: (i, 0))],
        out_specs=[_bs((tile, D_MODEL), lambda i: (i, 0)), _bs((1, BLOCK), lambda i: (0, 0))],
        out_shape=[jax.ShapeDtypeStruct((LP, D_MODEL), F32), jax.ShapeDtypeStruct((1, BLOCK), F32)],
        compiler_params=_cparams(("arbitrary",)),
    )(y, target)


def _adamw(w, gs, m, v, *, name, after=()):
    if w.ndim == 2:
        w, m, v = w[None], m[None], v[None]
        squeeze = True
    else:
        squeeze = False
    NL, R, C = w.shape
    assert len(gs) == NL
    CG = gs[0].shape[1]
    tile = _tile(R, 256, 8)

    def body(*refs):
        w_ref, g_refs = refs[0], refs[1:1 + NL]
        m_ref, v_ref = refs[1 + NL:3 + NL]
        go_ref, d_ref, nm_ref, nv_ref = refs[3 + NL + len(after):]
        gv = g_refs[0][:, :C]
        for j in range(1, NL):
            gv = jnp.where(pl.program_id(0) == j, g_refs[j][:, :C], gv)
        nm = ADAM_B1 * m_ref[...] + (1.0 - ADAM_B1) * gv
        nv = ADAM_B2 * v_ref[...] + (1.0 - ADAM_B2) * (gv * gv)
        m_hat = nm / (1.0 - ADAM_B1 ** ADAM_STEP)
        v_hat = nv / (1.0 - ADAM_B2 ** ADAM_STEP)
        go_ref[...] = gv
        d_ref[...] = -ADAM_LR * (m_hat / (jnp.sqrt(v_hat) + ADAM_EPS) + ADAM_WD * w_ref[...])
        nm_ref[...] = nm
        nv_ref[...] = nv

    spec = _bs((None, tile, C), lambda l, i: (l, i, 0))
    gspecs = [_bs((tile, CG), lambda l, i, j=j: (jnp.where(l == j, i, 0), 0)) for j in range(NL)]
    res = pl.pallas_call(
        body, name=name, grid=(NL, R // tile), in_specs=[spec, *gspecs, spec, spec, *[ANY] * len(after)],
        out_specs=[spec] * 4, out_shape=[jax.ShapeDtypeStruct((NL, R, C), F32)] * 4,
        compiler_params=_cparams(("parallel", "parallel")),
    )(w, *gs, m, v, *after)
    return [r[0] for r in res] if squeeze else res


def _my_pos():
    return lax.axis_index("x"), lax.axis_index("y"), lax.axis_index("c")


def _other_chips(x, y):
    return [(1 - x, y), (x, 1 - y), (1 - x, 1 - y)]


def _allgather_chips(shards):
    n = len(shards)
    per = 7

    def body(*refs):
        ins, outs = refs[:n], refs[n:2 * n]
        send_sems, recv_sems = refs[2 * n], refs[2 * n + 1]
        x, y, c = _my_pos()
        chips = _other_chips(x, y)
        sibling, me = (x, y, 1 - c), 2 * x + y

        def cp(a, kk, src, dst, to):
            return pltpu.make_async_remote_copy(src_ref=src, dst_ref=dst, send_sem=send_sems.at[per * a + kk],
                                                recv_sem=recv_sems.at[per * a + kk], device_id=to, device_id_type=MESH)

        sends = []
        for a in range(n):
            for j, chip in enumerate(chips):
                sends.append(cp(a, j, ins[a].at[c], outs[a].at[me, c], (*chip, c)))
            sends.append(cp(a, 3, ins[a], outs[a].at[me], sibling))
        for s in sends:
            s.start()
        for a in range(n):
            for j, chip in enumerate(chips):
                slab = outs[a].at[2 * chip[0] + chip[1], c]
                cp(a, j, slab, slab, (x, y, c)).wait_recv()
                fwd = cp(a, 4 + j, slab, slab, sibling)
                fwd.start()
                sends.append(fwd)
        for a in range(n):
            cp(a, 3, ins[a], outs[a].at[me], (x, y, c)).wait_recv()
            for j, chip in enumerate(chips):
                slab = outs[a].at[2 * chip[0] + chip[1], 1 - c]
                cp(a, 4 + j, slab, slab, (x, y, c)).wait_recv()
        for s in sends:
            s.wait_send()

    return pl.pallas_call(
        body, name="allgather_chips", in_specs=[ANY] * n, out_specs=[ANY] * n,
        out_shape=[jax.ShapeDtypeStruct((N_CHIPS,) + s.shape, s.dtype) for s in shards],
        scratch_shapes=[pltpu.SemaphoreType.DMA((per * n,)), pltpu.SemaphoreType.DMA((per * n,))],
    )(*shards)


def _rs_swap_rows(gs, tag):
    n = len(gs)

    def body(*refs):
        ins, outs = refs[:n], refs[n:2 * n]
        send_sems, recv_sems = refs[2 * n], refs[2 * n + 1]
        x, y, c = _my_pos()
        cps = []
        for a in range(n):
            half = ins[a].shape[1] // 2
            cps.append(pltpu.make_async_remote_copy(
                src_ref=ins[a].at[:, pl.ds((1 - c) * half, half)], dst_ref=outs[a], send_sem=send_sems.at[a],
                recv_sem=recv_sems.at[a], device_id=(x, y, 1 - c), device_id_type=MESH))
        for cp in cps:
            cp.start()
        for cp in cps:
            cp.wait()

    return pl.pallas_call(
        body, name=f"rs_swap_rows_{tag}", in_specs=[ANY] * n, out_specs=[ANY] * n,
        out_shape=[jax.ShapeDtypeStruct((N_CHIPS, g.shape[1] // 2, g.shape[2]), g.dtype) for g in gs],
        scratch_shapes=[pltpu.SemaphoreType.DMA((n,)), pltpu.SemaphoreType.DMA((n,))],
    )(*gs)


RS_ADD_VMEM_BYTES = 24 * 1024 * 1024


def _rs_tile(H, C, n):
    return _tile(H, max(16, RS_ADD_VMEM_BYTES // (28 * n * C)), 16)


def _rs_add_pair(gs, rs, pos, *, name):
    n = len(gs)
    _, H, C = rs[0].shape
    tile = _rs_tile(H, C, n)
    nt = H // tile

    def body(pos_ref, *refs):
        for a in range(n):
            s = refs[a][...] + refs[n + a][...]
            refs[2 * n + 2 * a][...] = s
            refs[2 * n + 2 * a + 1][...] = s.astype(BF16)

    spec = _bs((None, tile, C), lambda k, i, pos_ref: (k, i, 0))
    g_spec = _bs((None, tile, C), lambda k, i, pos_ref: (k, pos_ref[1] * nt + i, 0))
    grid_spec = pltpu.PrefetchScalarGridSpec(
        num_scalar_prefetch=1, grid=(N_CHIPS, nt), in_specs=[g_spec] * n + [spec] * n, out_specs=[spec] * (2 * n))
    res = pl.pallas_call(
        body, name=name, grid_spec=grid_spec,
        out_shape=[jax.ShapeDtypeStruct((N_CHIPS, H, C), F32), jax.ShapeDtypeStruct((N_CHIPS, H, C), BF16)] * n,
        compiler_params=_cparams(("parallel", "parallel")),
    )(pos, *gs, *rs)
    return [(res[2 * a], res[2 * a + 1]) for a in range(n)]


def _exchange_copies(srcs, lands, send_sems, recv_sems):
    x, y, c = _my_pos()
    starts, landing = [], []
    for a in range(len(srcs)):
        for j, chip in enumerate(_other_chips(x, y)):
            sems = dict(send_sem=send_sems.at[3 * a + j], recv_sem=recv_sems.at[3 * a + j], device_id_type=MESH)
            starts.append(pltpu.make_async_remote_copy(
                src_ref=srcs[a].at[2 * chip[0] + chip[1]], dst_ref=lands[a].at[j], device_id=(*chip, c), **sems))
            landing.append(pltpu.make_async_remote_copy(
                src_ref=lands[a].at[j], dst_ref=lands[a].at[j], device_id=(x, y, c), **sems))
    return starts, landing


def _gather_copies(srcs, lands, l, send_sems, recv_sems):
    x, y, c = _my_pos()
    me = 2 * x + y
    starts, landing = [], []
    for a in range(len(srcs)):
        half = srcs[a].shape[1] // 2
        mine = pl.ds(c * half, half)
        for j, chip in enumerate(_other_chips(x, y)):
            sems = dict(send_sem=send_sems.at[3 * a + j], recv_sem=recv_sems.at[3 * a + j], device_id_type=MESH)
            starts.append(pltpu.make_async_remote_copy(
                src_ref=srcs[a].at[l, mine], dst_ref=lands[a].at[me, mine], device_id=(*chip, c), **sems))
            slab = lands[a].at[2 * chip[0] + chip[1], mine]
            landing.append(pltpu.make_async_remote_copy(src_ref=slab, dst_ref=slab, device_id=(x, y, c), **sems))
    return starts, landing


HBM = pl.BlockSpec(memory_space=pltpu.HBM)
SEM = pl.BlockSpec(memory_space=pltpu.SEMAPHORE)


def _ici_start(copies_fn, srcs, land_shapes, *, name, after=()):
    n, na = len(srcs), len(after)

    def body(*refs):
        starts, _ = copies_fn(refs[:n], refs[n:2 * n], refs[2 * n + na], refs[2 * n + na + 1])
        for cp in starts:
            cp.start()
        refs[-1][...] = jnp.zeros_like(refs[-1])

    sems = pltpu.SemaphoreType.DMA((3 * n,))
    hbm = lambda s: pltpu.HBM(s.shape, s.dtype)
    lands = [pltpu.with_memory_space_constraint(lax.empty(s.shape, s.dtype), pltpu.HBM) for s in land_shapes]
    res = pl.pallas_call(
        body, name=name, in_specs=[HBM] * (2 * n) + [ANY] * na,
        out_specs=(SEM, SEM, *[HBM] * (2 * n), pl.BlockSpec(memory_space=pltpu.VMEM)),
        out_shape=(sems, sems, *[hbm(s) for s in srcs], *[hbm(s) for s in land_shapes],
                   jax.ShapeDtypeStruct((8, BLOCK), F32)),
        input_output_aliases={i: 2 + i for i in range(2 * n)},
        compiler_params=pltpu.CompilerParams(has_side_effects=pltpu.SideEffectType.DATAFLOW_SIDE_EFFECTING),
    )(*[pltpu.with_memory_space_constraint(s, pltpu.HBM) for s in srcs], *lands, *after)
    return res[0], res[1], list(res[2:2 + n]), list(res[2 + n:2 + 2 * n]), res[-1]


def _ici_wait(copies_fn, send_sems, recv_sems, srcs, lands, after, *, name):
    n = len(srcs)

    def body(*refs):
        starts, landing = copies_fn(refs[:n], refs[n:2 * n], refs[2 * n], refs[2 * n + 1])
        for cp in starts:
            cp.wait_send()
        for cp in landing:
            cp.wait_recv()

    hbm = lambda s: pltpu.HBM(s.shape, s.dtype)
    res = pl.pallas_call(
        body, name=name, in_specs=[*[HBM] * (2 * n), SEM, SEM, ANY], out_specs=[HBM] * (2 * n),
        out_shape=[*[hbm(s) for s in srcs], *[hbm(s) for s in lands]],
        input_output_aliases={i: i for i in range(2 * n)},
        compiler_params=pltpu.CompilerParams(has_side_effects=pltpu.SideEffectType.DATAFLOW_SIDE_EFFECTING),
    )(*srcs, *lands, send_sems, recv_sems, after)
    return list(res[:n]), list(res[n:])


def _gather_d2d(shards, lands, l, tag):
    n = len(shards)

    def body(*refs):
        ins, outs = refs[:n], refs[2 * n:3 * n]
        send_sems, recv_sems = refs[3 * n], refs[3 * n + 1]
        x, y, c = _my_pos()
        me, sibling = 2 * x + y, (x, y, 1 - c)
        starts, landing = [], []
        for a in range(n):
            half = ins[a].shape[1] // 2
            mine, theirs = pl.ds(c * half, half), pl.ds((1 - c) * half, half)
            pairs = [(ins[a].at[l], outs[a].at[me], outs[a].at[me])]
            for chip in _other_chips(x, y):
                k = 2 * chip[0] + chip[1]
                pairs.append((outs[a].at[k, mine], outs[a].at[k, mine], outs[a].at[k, theirs]))
            for j, (src, dst, lands_here) in enumerate(pairs):
                sems = dict(send_sem=send_sems.at[4 * a + j], recv_sem=recv_sems.at[4 * a + j], device_id_type=MESH)
                starts.append(pltpu.make_async_remote_copy(src_ref=src, dst_ref=dst, device_id=sibling, **sems))
                landing.append(pltpu.make_async_remote_copy(src_ref=lands_here, dst_ref=lands_here, device_id=(x, y, c),
                                                            **sems))
        for cp in starts:
            cp.start()
        for cp in landing:
            cp.wait_recv()
        for cp in starts:
            cp.wait_send()

    return pl.pallas_call(
        body, name=f"gather_d2d_{tag}", in_specs=[ANY] * (2 * n), out_specs=[ANY] * n,
        out_shape=[jax.ShapeDtypeStruct(s.shape, s.dtype) for s in lands],
        input_output_aliases={n + a: a for a in range(n)},
        scratch_shapes=[pltpu.SemaphoreType.DMA((4 * n,)), pltpu.SemaphoreType.DMA((4 * n,))],
    )(*shards, *lands)


def _rs_add_chips(p32s, r16s, pos, *, name):
    n = len(p32s)
    _, H, C = p32s[0].shape
    tile = _rs_tile(H, C, n)
    nt = H // tile

    def body(pos_ref, *refs):
        for a in range(n):
            p_ref, r_ref = refs[a], refs[n + a]
            refs[2 * n + a][...] = ((p_ref[...] + r_ref[0].astype(F32)) + r_ref[1].astype(F32)) + r_ref[2].astype(F32)

    grid_spec = pltpu.PrefetchScalarGridSpec(
        num_scalar_prefetch=1, grid=(nt,),
        in_specs=[_bs((None, tile, C), lambda i, pos_ref: (pos_ref[0], i, 0))] * n
        + [_bs((3, tile, C), lambda i, pos_ref: (0, i, 0))] * n,
        out_specs=[_bs((tile, C), lambda i, pos_ref: (pos_ref[1] * nt + i, 0))] * n)
    return pl.pallas_call(
        body, name=name, grid_spec=grid_spec, out_shape=[jax.ShapeDtypeStruct((2 * H, C), F32)] * n,
        compiler_params=_cparams(("parallel",)),
    )(pos, *p32s, *r16s)


def _rs_join_rows(fs, tag):
    n = len(fs)

    def body(*refs):
        outs = refs[n:2 * n]
        send_sems, recv_sems = refs[2 * n], refs[2 * n + 1]
        x, y, c = _my_pos()
        for a in range(n):
            half = outs[a].shape[0] // 2
            mine = outs[a].at[pl.ds(c * half, half)]
            pltpu.make_async_remote_copy(src_ref=mine, dst_ref=mine, send_sem=send_sems.at[a],
                                         recv_sem=recv_sems.at[a], device_id=(x, y, 1 - c), device_id_type=MESH).start()
        for a in range(n):
            half = outs[a].shape[0] // 2
            pltpu.make_async_remote_copy(
                src_ref=outs[a].at[pl.ds(c * half, half)], dst_ref=outs[a].at[pl.ds((1 - c) * half, half)],
                send_sem=send_sems.at[a], recv_sem=recv_sems.at[a], device_id=(x, y, 1 - c), device_id_type=MESH).wait()

    return pl.pallas_call(
        body, name=f"rs_join_rows_{tag}", in_specs=[ANY] * n, out_specs=[ANY] * n,
        out_shape=[jax.ShapeDtypeStruct(f.shape, f.dtype) for f in fs],
        input_output_aliases={a: a for a in range(n)},
        scratch_shapes=[pltpu.SemaphoreType.DMA((n,)), pltpu.SemaphoreType.DMA((n,))],
    )(*fs)


def _pos_vector():
    x, y, c = _my_pos()
    return jnp.stack([2 * x + y, c]).astype(jnp.int32)


def _swap_copies(srcs, lands, send_sems, recv_sems):
    x, y, c = _my_pos()
    starts, landing = [], []
    for a in range(len(srcs)):
        half = srcs[a].shape[1] // 2
        sems = dict(send_sem=send_sems.at[3 * a], recv_sem=recv_sems.at[3 * a], device_id_type=MESH)
        starts.append(pltpu.make_async_remote_copy(
            src_ref=srcs[a].at[:, pl.ds((1 - c) * half, half)], dst_ref=lands[a], device_id=(x, y, 1 - c), **sems))
        landing.append(pltpu.make_async_remote_copy(src_ref=lands[a], dst_ref=lands[a], device_id=(x, y, c), **sems))
    return starts, landing


def _swap_land_shapes(gs):
    return [jax.ShapeDtypeStruct((N_CHIPS, g.shape[1] // 2, g.shape[2]), g.dtype) for g in gs]


def _same_shape_runs(arrays):
    runs, start = [], 0
    for i in range(1, len(arrays) + 1):
        if i == len(arrays) or arrays[i].shape != arrays[start].shape:
            runs.append((start, i))
            start = i
    return runs


def _rs_add_pairs(gs, r1, names, tag):
    pos = _pos_vector()
    out = []
    for a, b in _same_shape_runs(gs):
        out += _rs_add_pair(gs[a:b], r1[a:b], pos, name=f"rs_add_pair_{tag}_{names[a]}")
    return out


def _rs_pair_sums(gs, names, tag):
    return _rs_add_pairs(gs, _rs_swap_rows(gs, tag), names, tag)


def _rs_finish(pairs, r2, names, tag):
    pos = _pos_vector()
    p32s = [p[0] for p in pairs]
    fs = []
    for a, b in _same_shape_runs(p32s):
        fs += _rs_add_chips(p32s[a:b], r2[a:b], pos, name=f"rs_add_chips_{tag}_{names[a]}")
    return _rs_join_rows(fs, tag)


def _exchange_land_shapes(pairs):
    return [jax.ShapeDtypeStruct((3,) + p[1].shape[1:], p[1].dtype) for p in pairs]


def _allreduce_small(buf):
    R, W = buf.shape

    def body(b_ref, o_ref, gather, send_sems, recv_sems):
        x, y, c = _my_pos()
        me = 4 * x + 2 * y + c
        gather[me] = b_ref[...]
        cps = []
        for d in range(1, 8):
            peer = (x ^ (d >> 2), y ^ ((d >> 1) & 1), c ^ (d & 1))
            cps.append(pltpu.make_async_remote_copy(
                src_ref=b_ref, dst_ref=gather.at[me], send_sem=send_sems.at[d - 1], recv_sem=recv_sems.at[d - 1],
                device_id=peer, device_id_type=MESH))
        for cp in cps:
            cp.start()
        for d in range(1, 8):
            pltpu.make_async_remote_copy(
                src_ref=b_ref, dst_ref=gather.at[me ^ d], send_sem=send_sems.at[d - 1], recv_sem=recv_sems.at[d - 1],
                device_id=(x, y, c), device_id_type=MESH).wait_recv()
        for cp in cps:
            cp.wait_send()
        acc = gather[0]
        for d in range(1, 8):
            acc = acc + gather[d]
        o_ref[...] = acc

    vm = pl.BlockSpec(memory_space=pltpu.VMEM)
    return pl.pallas_call(
        body, name="allreduce_small", in_specs=[vm], out_specs=vm, out_shape=jax.ShapeDtypeStruct((R, W), F32),
        scratch_shapes=[pltpu.VMEM((8, R, W), F32), pltpu.SemaphoreType.DMA((7,)), pltpu.SemaphoreType.DMA((7,))],
    )(buf)


def _heads(a, h, d):
    return a.reshape(a.shape[0], h, d).transpose(1, 0, 2)


def _unheads(a):
    h, L, d = a.shape
    return a.transpose(1, 0, 2).reshape(L, h * d)


def _rope_tables():
    pos = jnp.maximum(jnp.arange(LP, dtype=F32) - PAD_ROWS, 0.0)
    inv_freq = 1.0 / (ROPE_THETA ** (jnp.arange(0, MLA_ROPE, 2, dtype=F32) / MLA_ROPE))
    ang = pos[:, None] * inv_freq[None, :]
    cos, sin = jnp.tile(jnp.cos(ang), (1, MLA_HEADS)), jnp.tile(jnp.sin(ang), (1, MLA_HEADS))
    return jnp.concatenate([cos, cos], axis=1), jnp.concatenate([-sin, sin], axis=1)


def _lane_pad(a, width=BLOCK):
    return jnp.pad(a, ((0, 0), (0, width - a.shape[1])))


def _pad_in_proj(w):
    sl = lambda start, size: w[:, start:start + size]
    return jnp.concatenate([
        sl(OC_Z, 512), sl(OC_XBC, 768), sl(OC_FQ, 256), sl(OC_FK, 256), sl(OC_FV, 256), sl(OC_CQ, 256), sl(OC_CKV, 128),
        _lane_pad(sl(OC_DT, SSD_HEADS)), _lane_pad(sl(OC_FR, FOX_HEADS)),
        jnp.tile(sl(OC_KR, ROPE_HALF), (1, MLA_HEADS)), jnp.tile(sl(OC_KR + ROPE_HALF, ROPE_HALF), (1, MLA_HEADS))], axis=1)


def _unpad_in_proj(wp):
    sl = lambda start, size: wp[:, start:start + size]
    rope = lambda start: sl(start, 64).reshape(wp.shape[0], MLA_HEADS, ROPE_HALF).sum(axis=1)
    return jnp.concatenate([
        sl(PC_Z, 512), sl(PC_XBC, 768), sl(PC_DT, SSD_HEADS), sl(PC_FQ, 256), sl(PC_FK, 256), sl(PC_FV, 256),
        sl(PC_FR, FOX_HEADS), sl(PC_CQ, 256), sl(PC_CKV, 128), rope(PC_KR), rope(PC_KR + 64)], axis=1)


def _regroup_uq(w):
    w3 = w.reshape(w.shape[0], MLA_HEADS, MLA_NOPE + MLA_ROPE)
    return jnp.concatenate([w3[:, :, :MLA_NOPE].reshape(w.shape[0], -1),
                            w3[:, :, MLA_NOPE:MLA_NOPE + ROPE_HALF].reshape(w.shape[0], -1),
                            w3[:, :, MLA_NOPE + ROPE_HALF:].reshape(w.shape[0], -1)], axis=1)


def _ungroup_uq(wp):
    n = wp.shape[0]
    return jnp.concatenate([wp[:, :256].reshape(n, MLA_HEADS, MLA_NOPE), wp[:, 256:320].reshape(n, MLA_HEADS, ROPE_HALF),
                            wp[:, 320:].reshape(n, MLA_HEADS, ROPE_HALF)], axis=2).reshape(n, -1)


def _regroup_ukv(w):
    w3 = w.reshape(w.shape[0], MLA_HEADS, MLA_NOPE + MLA_V)
    return jnp.concatenate([w3[:, :, :MLA_NOPE].reshape(w.shape[0], -1), w3[:, :, MLA_NOPE:].reshape(w.shape[0], -1)],
                           axis=1)


def _ungroup_ukv(wp):
    n = wp.shape[0]
    return jnp.concatenate([wp[:, :256].reshape(n, MLA_HEADS, MLA_NOPE), wp[:, 256:].reshape(n, MLA_HEADS, MLA_V)],
                           axis=2).reshape(n, -1)


TMF = 1088
N_IF = LP // TMF


def _chunk_rows_dx(g, w, l, chunk_h, *, name):
    N = w.shape[2]
    return _mm_core(g, w, a_spec=_bs((TMF, N), lambda i, j, k: (i, 0)),
                    b_spec=_bs((None, chunk_h, N), lambda i, j, k: (j, 0, 0)),
                    o_spec=_bs((TMF, chunk_h), lambda i, j, k: (i, j)), grid=(N_IF, N_CHIPS, 1),
                    out_shape=(LP, N_CHIPS * chunk_h), ca=1, cb=1, name=name)


def _chunk_rows_dw(a, g, chunk_h, *, name):
    N = g.shape[1]
    return _mm_core(a, g, a_spec=_bs((LP, chunk_h), lambda i, j, k: (0, i)), b_spec=_bs((LP, N), lambda i, j, k: (0, 0)),
                    o_spec=_bs((None, chunk_h, N), lambda i, j, k: (i, 0, 0)), grid=(N_CHIPS, 1, 1),
                    out_shape=(N_CHIPS, chunk_h, N), ca=0, cb=0, name=name)


def _ffn_up_swiglu(h, wg, wu, *, name):
    def body(h_ref, wg_ref, wu_ref, g_ref, u_ref, a_ref):
        hb = h_ref[...].astype(BF16)
        g = _raw_bdot(hb, wg_ref[...], 1, 1)
        u = _raw_bdot(hb, wu_ref[...], 1, 1)
        g_ref[...] = g
        u_ref[...] = u
        a_ref[...] = (_silu(g) * u).astype(a_ref.dtype)

    w_spec = _bs((None, HP, D_MODEL), lambda i, j: (j, 0, 0))
    o_spec = _bs((TMF, HP), lambda i, j: (i, j))
    return pl.pallas_call(
        body, name=name, grid=(N_IF, N_CHIPS), in_specs=[_bs((TMF, D_MODEL), lambda i, j: (i, 0)), w_spec, w_spec],
        out_specs=[o_spec] * 3,
        out_shape=[jax.ShapeDtypeStruct((LP, FP), F32), jax.ShapeDtypeStruct((LP, FP), F32),
                   jax.ShapeDtypeStruct((LP, FP), BF16)],
        compiler_params=_cparams(("parallel", "parallel")),
    )(h, wg, wu)


def _ffn_down_dx_swiglu(do, wd, g, u, *, name):
    def body(do_ref, wd_ref, g_ref, u_ref, dg_ref, du_ref):
        dact = _raw_bdot(do_ref[...], wd_ref[...], 1, 1)
        gv = g_ref[...]
        sig = _sigmoid(gv)
        dg_ref[...] = (dact * u_ref[...] * (sig * (1.0 + gv * (1.0 - sig)))).astype(dg_ref.dtype)
        du_ref[...] = (dact * (gv * sig)).astype(du_ref.dtype)

    blk = _bs((TMF, HP), lambda i, j: (i, j))
    return pl.pallas_call(
        body, name=name, grid=(N_IF, N_CHIPS),
        in_specs=[_bs((TMF, D_MODEL), lambda i, j: (i, 0)), _bs((None, HP, D_MODEL), lambda i, j: (j, 0, 0)), blk, blk],
        out_specs=[blk, blk], out_shape=[jax.ShapeDtypeStruct((LP, FP), BF16)] * 2,
        compiler_params=_cparams(("parallel", "parallel")),
    )(do, wd, g, u)


def _ffn_gate_up_dw(dg, du, h, *, name):
    def body(dg_ref, du_ref, h_ref, wg_ref, wu_ref):
        hb = h_ref[...].astype(BF16)
        wg_ref[...] = _raw_bdot(dg_ref[...], hb, 0, 0)
        wu_ref[...] = _raw_bdot(du_ref[...], hb, 0, 0)

    a_spec = _bs((LP, HP), lambda k: (0, k))
    o_spec = _bs((None, HP, D_MODEL), lambda k: (k, 0, 0))
    return pl.pallas_call(
        body, name=name, grid=(N_CHIPS,), in_specs=[a_spec, a_spec, _bs((LP, D_MODEL), lambda k: (0, 0))],
        out_specs=[o_spec, o_spec], out_shape=[jax.ShapeDtypeStruct((N_CHIPS, HP, D_MODEL), F32)] * 2,
        compiler_params=_cparams(("parallel",)),
    )(dg, du, h)


def _ffn_gate_up_dx(dg, du, wg, wu, add, *, name):
    def body(dg_ref, du_ref, wg_ref, wu_ref, add_ref, o_ref, acc_ref):
        k = pl.program_id(1)

        @pl.when(k == 0)
        def _():
            acc_ref[...] = jnp.zeros_like(acc_ref)

        acc_ref[...] += _raw_bdot(dg_ref[...], wg_ref[...], 1, 0) + _raw_bdot(du_ref[...], wu_ref[...], 1, 0)

        @pl.when(k == N_CHIPS - 1)
        def _():
            o_ref[...] = acc_ref[...] + add_ref[...]

    a_spec = _bs((TMF, HP), lambda i, k: (i, k))
    w_spec = _bs((None, HP, D_MODEL), lambda i, k: (k, 0, 0))
    o_spec = _bs((TMF, D_MODEL), lambda i, k: (i, 0))
    return pl.pallas_call(
        body, name=name, grid=(N_IF, N_CHIPS), in_specs=[a_spec, a_spec, w_spec, w_spec, o_spec], out_specs=o_spec,
        out_shape=jax.ShapeDtypeStruct((LP, D_MODEL), F32), scratch_shapes=[pltpu.VMEM((TMF, D_MODEL), F32)],
        compiler_params=_cparams(("parallel", "arbitrary")),
    )(dg, du, wg, wu, add)


def _chunk_rows_mm_res_ln(a, w, chunk_h, h, gam, bet, scale, *, name):
    res_ln = _make_res_ln_fn(scale)

    def body(a_ref, w_ref, h_ref, g_ref, b_ref, o_ref, y_ref, yb_ref, acc_ref):
        k = pl.program_id(1)

        @pl.when(k == 0)
        def _():
            acc_ref[...] = jnp.zeros_like(acc_ref)

        acc_ref[...] += _raw_bdot(a_ref[...], w_ref[...], 1, 0)

        @pl.when(k == N_CHIPS - 1)
        def _():
            o = acc_ref[...]
            o_ref[...] = o
            (y,) = res_ln(0, h_ref[...], o, g_ref[...], b_ref[...])
            y_ref[...] = y
            yb_ref[...] = y.astype(yb_ref.dtype)

    row = _bs((TMF, D_MODEL), lambda i, k: (i, 0))
    par = _bs((1, D_MODEL), lambda i, k: (0, 0))
    return pl.pallas_call(
        body, name=name, grid=(N_IF, N_CHIPS),
        in_specs=[_bs((TMF, chunk_h), lambda i, k: (i, k)), _bs((None, chunk_h, D_MODEL), lambda i, k: (k, 0, 0)), row,
                  par, par],
        out_specs=[row, row, row],
        out_shape=[jax.ShapeDtypeStruct((LP, D_MODEL), F32)] * 2 + [jax.ShapeDtypeStruct((LP, D_MODEL), BF16)],
        scratch_shapes=[pltpu.VMEM((TMF, D_MODEL), F32)], compiler_params=_cparams(("parallel", "arbitrary")),
    )(a, w, h, gam, bet)


def _ffn_fwd(hp, W, pre, l, gam, bet, tag):
    h, hb = hp
    g, u, act = _ffn_up_swiglu(hb, W[pre + "_w_gate"][l], W[pre + "_w_up"][l], name=f"{tag}_up_swiglu")
    o, out, outb = _chunk_rows_mm_res_ln(act, W[pre + "_w_down"][l], HP, h, gam, bet, 0.5, name=f"{tag}_down_ln")
    return (out, outb), (h, hb, g, u, act, o)


def _ffn_bwd(dout, saved, W, pre, l, gam, bet, GB, tag):
    h, hb, g, u, act, o = saved
    (dh_a, do), (dgam, dbet) = _rowwise_bwd(_make_res_ln_fn(0.5), [h, o], [gam, bet], [dout], name=f"{tag}_ln_bwd",
                                            tile=272, grad_dtypes=[F32, BF16])
    dg, du = _ffn_down_dx_swiglu(do, W[pre + "_w_down"][l], g, u, name=f"{tag}_down_dx_swiglu")
    GB[pre + "_w_down"] = _chunk_rows_dw(act, do, HP, name=f"{tag}_down_dw")
    GB[pre + "_w_gate"], GB[pre + "_w_up"] = _ffn_gate_up_dw(dg, du, hb, name=f"{tag}_gate_up_dw")
    dh = _ffn_gate_up_dx(dg, du, W[pre + "_w_gate"][l], W[pre + "_w_up"][l], dh_a, name=f"{tag}_gate_up_dx")
    return dh, dgam, dbet


def _mixer_fwd(hp1, W, l, cosf, sins, after=()):
    h1, h1b = hp1
    tag = f"l{l}"
    proj = _mm(h1b, W["w_in_p"][l], name=f"{tag}_in_proj", after=after)
    sv = {"h1": h1, "h1b": h1b, "proj": proj}
    conv_w, conv_b = W["conv_w"][l], W["conv_b"][l][None]
    xc = _conv_fwd(proj, PC_XBC // BLOCK, conv_w, conv_b, name=f"{tag}_conv")
    dt_bias = _lane_pad(W["dt_bias"][l][None])
    dtc, dtr = _ssd_dt_fwd(proj, PC_DT // BLOCK, dt_bias, name=f"{tag}_ssd_dt")
    xh = _heads(xc[:, :SSD_D], SSD_HEADS, SSD_HD)
    bm = _heads(xc[:, SSD_D:SSD_D + 128], SSD_GROUPS, SSD_STATE)
    cm = _heads(xc[:, SSD_D + 128:], SSD_GROUPS, SSD_STATE)
    alog = jnp.broadcast_to(W["a_log"][l][:, None, None], (SSD_HEADS, 1, BLOCK))
    yh, prevs = _ssd_fwd(xh, bm, cm, dtc, dtr, alog, name=f"{tag}_ssd")
    y_raw = _unheads(yh)
    dskip = jnp.repeat(W["d_skip"][l], SSD_HD)[None]
    normg = W["ssd_norm_g"][l][None]
    post_rows = [y_raw, (xc, 256, 0), (proj, 256, PC_Z // 256)]
    (y_ssd,) = _rowwise(_ssd_post_fn, post_rows, [dskip, normg], [SSD_D], name=f"{tag}_ssd_post", tile=272,
                        ncol=SSD_GROUPS)
    sv.update(conv_w=conv_w, conv_b=conv_b, dt_bias=dt_bias, xh=xh, bm=bm, cm=cm, dtc=dtc, dtr=dtr, alog=alog,
              prevs=prevs, post_rows=post_rows, dskip=dskip, normg=normg)
    f_b = _lane_pad(W["fox_f_b"][l][None])
    cg, cgt = _fox_gate_fwd(proj, PC_FR // BLOCK, f_b, name=f"{tag}_fox_gate")
    fox_qkv = ((proj, PC_FQ // ATT_W), (proj, PC_FK // ATT_W), (proj, PC_FV // ATT_W))
    y_fox, lse_f = _attn_fwd(*fox_qkv, scale=FOX_HD ** -0.5, name=f"{tag}_fox_attn", bias=(cg, cgt))
    sv.update(f_b=f_b, cg=cg, cgt=cgt, fox_qkv=fox_qkv, y_fox=y_fox, lse_f=lse_f)
    gq, gkv = W["mla_q_norm_g"][l][None], W["mla_kv_norm_g"][l][None]
    norm_rows = [(proj, 256, PC_CQ // 256), (proj, BLOCK, PC_CKV // BLOCK)]
    qn, cn = _rowwise(_mla_norm_fn, norm_rows, [gq, gkv], [MLA_Q_LORA, MLA_KV_LORA], name=f"{tag}_mla_norm", tile=272,
                      out_dtypes=[BF16, BF16])
    qh = _mm(qn, W["mla_w_uq_p"][l], name=f"{tag}_mla_uq")
    kvh = _mm(cn, W["mla_w_ukv_p"][l], name=f"{tag}_mla_ukv")
    qr, kr = _rowwise(_rope_fn, [(qh, BLOCK, 2), (proj, BLOCK, PC_KR // BLOCK), cosf, sins], [], [BLOCK, BLOCK],
                      name=f"{tag}_rope", tile=272)
    mla_qkv = ((qh, 0), (kvh, 0), (kvh, 1))
    y_mla, lse_m = _attn_fwd(*mla_qkv, scale=(MLA_NOPE + MLA_ROPE) ** -0.5, name=f"{tag}_mla_attn",
                             rope=((qr, 0), (kr, 0)))
    sv.update(gq=gq, gkv=gkv, norm_rows=norm_rows, qn=qn, cn=cn, qr=qr, kr=kr, mla_qkv=mla_qkv, y_mla=y_mla, lse_m=lse_m)
    ycat = jnp.concatenate([y_ssd, y_fox, y_mla], axis=1).astype(BF16)
    mix, h2, h2b = _chunk_rows_mm_res_ln(ycat, W["w_out"][l], 256, h1, W["ln2_g"][l][None], W["ln2_b"][l][None], 1.0,
                                    name=f"{tag}_out_proj_ln2")
    sv.update(mix=mix, ycat=ycat)
    return (h2, h2b), sv


def _mixer_bwd(dh2, sv, W, l, cosf, sins, GB, zero=0.0):
    tag = f"l{l}"
    G = {}
    proj = sv["proj"]
    ln2g, ln2b = W["ln2_g"][l][None] + zero, W["ln2_b"][l][None]
    (dh1_a, dmix), (dln2g, dln2b) = _rowwise_bwd(
        _make_res_ln_fn(1.0), [sv["h1"], sv["mix"]], [ln2g, ln2b], [dh2], name=f"{tag}_ln2_bwd", tile=272,
        grad_dtypes=[F32, BF16])
    G["ln2_g"], G["ln2_b"] = dln2g[0], dln2b[0]
    dycat = _chunk_rows_dx(dmix, W["w_out"][l], l, 256, name=f"{tag}_out_proj_dx")
    GB["w_out"] = _chunk_rows_dw(sv["ycat"], dmix, 256, name=f"{tag}_out_proj_dw")
    (dy_raw, dxs_a, dz), (ddskip, dnormg) = _rowwise_bwd(
        _ssd_post_fn, sv["post_rows"], [sv["dskip"], sv["normg"]], [dycat[:, :SSD_D]],
        name=f"{tag}_ssd_post_bwd", tile=272, ncol=SSD_GROUPS)
    G["ssd_norm_g"] = dnormg[0]
    G["d_skip"] = ddskip.reshape(SSD_HEADS, SSD_HD).sum(axis=1)
    dxh, dbm, dcm, ddtc, ddtr, dal = _ssd_bwd(sv["xh"], sv["bm"], sv["cm"], sv["dtc"], sv["dtr"], sv["alog"],
                                              sv["prevs"], _heads(dy_raw, SSD_HEADS, SSD_HD), name=f"{tag}_ssd_bwd")
    G["a_log"] = dal[:, 0, 0]
    dxc = jnp.concatenate([dxs_a + _unheads(dxh), _unheads(dbm), _unheads(dcm)], axis=1)
    dxbc, G["conv_w"], dconv_b = _conv_bwd(proj, PC_XBC // BLOCK, sv["conv_w"], sv["conv_b"], dxc,
                                           name=f"{tag}_conv_bwd")
    G["conv_b"] = dconv_b[0]
    ddt_raw, ddt_bias = _ssd_dt_bwd(proj, PC_DT // BLOCK, sv["dt_bias"], ddtc, ddtr, name=f"{tag}_ssd_dt_bwd")
    G["dt_bias"] = ddt_bias[0, :SSD_HEADS]
    dfq, dfk, dfv, dcg, dcgt = _attn_bwd(*sv["fox_qkv"], sv["y_fox"], sv["lse_f"], (dycat, SSD_D // ATT_W),
                                         scale=FOX_HD ** -0.5, name=f"{tag}_fox_attn_bwd", bias=(sv["cg"], sv["cgt"]))
    df_raw, dfb = _fox_gate_bwd(proj, PC_FR // BLOCK, sv["f_b"], dcg, dcgt, name=f"{tag}_fox_gate_bwd")
    G["fox_f_b"] = dfb[0, :FOX_HEADS]
    dqn_h, dkn_h, dv_h, dqr, dkr = _attn_bwd(
        *sv["mla_qkv"], sv["y_mla"], sv["lse_m"], (dycat, (SSD_D + FOX_D) // ATT_W),
        scale=(MLA_NOPE + MLA_ROPE) ** -0.5, name=f"{tag}_mla_attn_bwd", rope=((sv["qr"], 0), (sv["kr"], 0)))
    dq_rope, dk_rope = _rowwise(_rope_t_fn, [dqr, dkr, cosf, sins], [], [BLOCK, BLOCK], name=f"{tag}_rope_bwd",
                                tile=272)
    dqh = jnp.concatenate([dqn_h, dq_rope], axis=1).astype(BF16)
    dkvh = jnp.concatenate([dkn_h, dv_h], axis=1).astype(BF16)
    dqn = _mm(dqh, W["mla_w_uq_p"][l], tb=True, name=f"{tag}_mla_uq_dx")
    G["mla_w_uq_p"] = _mm(sv["qn"], dqh, ta=True, name=f"{tag}_mla_uq_dw")
    dcn = _mm(dkvh, W["mla_w_ukv_p"][l], tb=True, name=f"{tag}_mla_ukv_dx")
    G["mla_w_ukv_p"] = _mm(sv["cn"], dkvh, ta=True, name=f"{tag}_mla_ukv_dw")
    (dcq, dckv), (dgq, dgkv) = _rowwise_bwd(_mla_norm_fn, sv["norm_rows"], [sv["gq"], sv["gkv"]], [dqn, dcn],
                                            name=f"{tag}_mla_norm_bwd", tile=272)
    G["mla_q_norm_g"], G["mla_kv_norm_g"] = dgq[0], dgkv[0]
    dproj = jnp.concatenate([dz, dxbc, dfq, dfk, dfv, dcq, dckv, ddt_raw, df_raw, dk_rope], axis=1).astype(BF16)
    dh1 = _mm(dproj, W["w_in_p"][l], tb=True, add=dh1_a, name=f"{tag}_in_proj_dx")
    G["w_in_p"] = _mm(sv["h1b"], dproj, ta=True, name=f"{tag}_in_proj_dw")
    return dh1, G


def _embed(x, meta):
    return jnp.concatenate([jnp.zeros((PAD_ROWS, D_MODEL), F32), meta, x], axis=0)


def _layer_fwd(h, W, l, cosf, sins):
    ln = lambda n: W[n][l][None]
    h1, s1 = _ffn_fwd(h, W, "ffn1", l, ln("ln1_g"), ln("ln1_b"), f"l{l}_ffn1")
    h2, sm = _mixer_fwd(h1, W, l, cosf, sins)
    h3, s2 = _ffn_fwd(h2, W, "ffn2", l, ln("ln3_g"), ln("ln3_b"), f"l{l}_ffn2")
    return h3, (s1, sm, s2)


def _layer_bwd(dh, saved, W, l, cosf, sins):
    ln = lambda n: W[n][l][None]
    s1, sm, s2 = saved
    G = {}
    dh, dg, db = _ffn_bwd(dh, s2, W, "ffn2", l, ln("ln3_g"), ln("ln3_b"), G, f"l{l}_ffn2")
    G["ln3_g"], G["ln3_b"] = dg[0], db[0]
    dh, Gm = _mixer_bwd(dh, sm, W, l, cosf, sins, G)
    G.update(Gm)
    dh, dg, db = _ffn_bwd(dh, s1, W, "ffn1", l, ln("ln1_g"), ln("ln1_b"), G, f"l{l}_ffn1")
    G["ln1_g"], G["ln1_b"] = dg[0], db[0]
    return dh, G


def _local_step(x, target, W):
    h = _embed(x, W["meta"])
    h = (h, h.astype(BF16))
    tgt = jnp.concatenate([jnp.zeros((BLOCK, D_MODEL), F32), target], axis=0)
    cosf, sins = _rope_tables()
    saved = []
    for l in range(DEPTH):
        h, sv = _layer_fwd(h, W, l, cosf, sins)
        saved.append(sv)
    dh, loss = _loss_head(h[0], tgt, name="loss_head")
    grads = [None] * DEPTH
    for l in reversed(range(DEPTH)):
        dh, grads[l] = _layer_bwd(dh, saved[l], W, l, cosf, sins)
    return loss, dh, grads


WEIGHTS = ['meta', 'ffn1_w_gate', 'ffn1_w_up', 'ffn1_w_down', 'ln1_g', 'ln1_b', 'w_in', 'conv_w', 'conv_b', 'dt_bias',
           'a_log', 'd_skip', 'ssd_norm_g', 'fox_f_b', 'mla_q_norm_g', 'mla_w_uq', 'mla_kv_norm_g', 'mla_w_ukv',
           'w_out', 'ln2_g', 'ln2_b', 'ffn2_w_gate', 'ffn2_w_up', 'ffn2_w_down', 'ln3_g', 'ln3_b']
SMALL = ["ln1_g", "ln1_b", "conv_b", "dt_bias", "a_log", "d_skip", "ssd_norm_g", "fox_f_b", "mla_q_norm_g",
         "mla_kv_norm_g", "ln2_g", "ln2_b", "ln3_g", "ln3_b"]
MATMUL_W = ["ffn1_w_gate", "ffn1_w_up", "ffn1_w_down", "w_in", "mla_w_uq", "mla_w_ukv", "w_out", "ffn2_w_gate",
            "ffn2_w_up", "ffn2_w_down"]
SMALL_ROWS = 312


def _pad_to(a, axis, size):
    pads = [(0, 0)] * a.ndim
    pads[axis] = (0, size - a.shape[axis])
    return jnp.pad(a, pads)


def _chip_cols(full, chip, width):
    return lax.dynamic_slice_in_dim(full, chip * width, width, axis=full.ndim - 1)


def kernel(x, meta, ffn1_w_gate, ffn1_w_up, ffn1_w_down, ln1_g, ln1_b, w_in, conv_w, conv_b, dt_bias, a_log, d_skip, ssd_norm_g, fox_f_b, mla_q_norm_g, mla_w_uq, mla_kv_norm_g, mla_w_ukv, w_out, ln2_g, ln2_b, ffn2_w_gate, ffn2_w_up, ffn2_w_down, ln3_g, ln3_b, loss_target, m_meta, m_ffn1_w_gate, m_ffn1_w_up, m_ffn1_w_down, m_ln1_g, m_ln1_b, m_w_in, m_conv_w, m_conv_b, m_dt_bias, m_a_log, m_d_skip, m_ssd_norm_g, m_fox_f_b, m_mla_q_norm_g, m_mla_w_uq, m_mla_kv_norm_g, m_mla_w_ukv, m_w_out, m_ln2_g, m_ln2_b, m_ffn2_w_gate, m_ffn2_w_up, m_ffn2_w_down, m_ln3_g, m_ln3_b, v_meta, v_ffn1_w_gate, v_ffn1_w_up, v_ffn1_w_down, v_ln1_g, v_ln1_b, v_w_in, v_conv_w, v_conv_b, v_dt_bias, v_a_log, v_d_skip, v_ssd_norm_g, v_fox_f_b, v_mla_q_norm_g, v_mla_w_uq, v_mla_kv_norm_g, v_mla_w_ukv, v_w_out, v_ln2_g, v_ln2_b, v_ffn2_w_gate, v_ffn2_w_up, v_ffn2_w_down, v_ln3_g, v_ln3_b):
    args = dict(locals())
    w = {n: args[n] for n in WEIGHTS}
    m = {n: args["m_" + n] for n in WEIGHTS}
    v = {n: args["v_" + n] for n in WEIGHTS}
    xcoord, ycoord, _ = _my_pos()
    chip = 2 * xcoord + ycoord

    tr = lambda a: jnp.swapaxes(a, 1, 2)

    def bf16_shard(n, zero=None):
        a = w[n] if zero is None else w[n] + zero
        if n.endswith("w_gate") or n.endswith("w_up"):
            a = _pad_to(tr(a), 1, HP)
        elif n.endswith("w_down"):
            a = _pad_to(a, 1, HP)
        elif n == "w_in":
            a = _pad_to(a, 2, IN_SHARD_P)
        return a.astype(BF16)

    land_shape = lambda s: jax.ShapeDtypeStruct((N_CHIPS,) + s.shape[1:], s.dtype)
    gather_l = lambda l: (lambda srcs, lands, ss, rs: _gather_copies(srcs, lands, l, ss, rs))
    tiny = _allgather_chips([w["meta"].reshape(2, N_META // 2, D_MODEL // N_CHIPS), w["conv_w"]])
    meta_full = jnp.concatenate([tiny[0][k].reshape(N_META, D_MODEL // N_CHIPS) for k in range(N_CHIPS)], axis=1)

    W = {n: [None] * DEPTH for n in MATMUL_W + ["w_in_p", "mla_w_uq_p", "mla_w_ukv_p"]}
    W["conv_w"] = jnp.concatenate([tiny[1][k] for k in range(N_CHIPS)], axis=-1)
    W["meta"] = meta_full
    for n in SMALL:
        W[n] = w[n]

    def use_gathered(l, names, lands):
        got = dict(zip(names, lands))
        cat = lambda n, cut=None: jnp.concatenate([got[n][k][..., :cut] for k in range(N_CHIPS)], axis=-1)
        for n in names:
            W[n][l] = got[n]
        if "w_in" in got:
            W["w_in_p"][l] = _pad_in_proj(cat("w_in", IN_SHARD))
            W["mla_w_uq_p"][l] = _regroup_uq(cat("mla_w_uq"))
            W["mla_w_ukv_p"][l] = _regroup_ukv(cat("mla_w_ukv"))

    def chunk_grads(G, names):
        def chunked(name, ungroup, width, pad):
            full = ungroup(G[name])
            return _pad_to(jnp.moveaxis(full.reshape(full.shape[0], N_CHIPS, width), 1, 0), 2, pad)
        special = {"w_in": ("w_in_p", _unpad_in_proj, IN_SHARD, IN_SHARD_P),
                   "mla_w_uq": ("mla_w_uq_p", _ungroup_uq, MLA_NOPE + MLA_ROPE, MLA_NOPE + MLA_ROPE),
                   "mla_w_ukv": ("mla_w_ukv_p", _ungroup_ukv, MLA_NOPE + MLA_V, MLA_NOPE + MLA_V)}
        return [chunked(*special[n]) if n in special else G[n] for n in names]

    def rs_start(G, names, tag):
        pairs = _rs_pair_sums(chunk_grads(G, names), names, tag)
        handle = _ici_start(_exchange_copies, [p[1] for p in pairs], _exchange_land_shapes(pairs),
                            name=f"rs_exchange_{tag}_start")
        return pairs, handle

    def swap_start(G, names, tag):
        gs = chunk_grads(G, names)
        return _ici_start(_swap_copies, gs, _swap_land_shapes(gs), name=f"rs_swap_{tag}_start")

    def exchange_start(swap_handle, names, tag, after):
        gs, r1 = _ici_wait(_swap_copies, *swap_handle[:4], after, name=f"rs_swap_{tag}_wait")
        pairs = _rs_add_pairs(gs, r1, names, tag)
        handle = _ici_start(_exchange_copies, [p[1] for p in pairs], _exchange_land_shapes(pairs),
                            name=f"rs_exchange_{tag}_start")
        return pairs, handle

    def rs_end(pairs, handle, names, tag, after):
        _, r2 = _ici_wait(_exchange_copies, *handle[:4], after, name=f"rs_exchange_{tag}_wait")
        return dict(zip(names, _rs_finish(pairs, r2, names, tag)))

    na = 3
    first, rest = MATMUL_W[:na], MATMUL_W[na:]
    shards = [bf16_shard(n) for n in first]
    g_send, g_recv, g_srcs, g_lands, token = _ici_start(gather_l(0), shards, [land_shape(s) for s in shards],
                                                        name="gather_ici_l0_ffn1_start", after=[tiny[0]])
    shards_rest = [bf16_shard(n, token[0, 0]) for n in rest]
    shards, lands = _ici_wait(gather_l(0), g_send, g_recv, g_srcs, g_lands, shards_rest[0],
                              name="gather_ici_l0_ffn1_wait")
    shards = shards + shards_rest
    land_shapes = [land_shape(s) for s in shards]
    got = _gather_d2d(shards[:na], lands, 0, "l0_ffn1")
    use_gathered(0, first, got)
    g_send, g_recv, g_srcs, g_lands, token = _ici_start(gather_l(0), shards[na:], land_shapes[na:],
                                                        name="gather_ici_l0_rest_start", after=[got[0]])
    cosf, sins = _rope_tables()
    ln = lambda n, l: W[n][l][None]
    h = _embed(x[0] + token[0, 0], meta_full)
    h = (h, h.astype(BF16))
    h1, s1 = _ffn_fwd(h, W, "ffn1", 0, ln("ln1_g", 0), ln("ln1_b", 0), "l0_ffn1")
    rest_shards, lands = _ici_wait(gather_l(0), g_send, g_recv, g_srcs, g_lands, h1[0], name="gather_ici_l0_rest_wait")
    shards = shards[:na] + rest_shards
    got = _gather_d2d(shards[na:], lands, 0, "l0_rest")
    use_gathered(0, rest, got)
    g_send, g_recv, g_srcs, g_lands, token = _ici_start(gather_l(1), shards, land_shapes, name="gather_ici_l1_start",
                                                        after=[got[0]])
    h2, sm = _mixer_fwd(h1, W, 0, cosf, sins, after=[token])
    h, s2 = _ffn_fwd(h2, W, "ffn2", 0, ln("ln3_g", 0), ln("ln3_b", 0), "l0_ffn2")
    saved0 = (s1, sm, s2)
    shards, lands = _ici_wait(gather_l(1), g_send, g_recv, g_srcs, g_lands, h[0], name="gather_ici_l1_wait")
    use_gathered(1, MATMUL_W, _gather_d2d(shards, lands, 1, "l1"))
    h, saved1 = _layer_fwd(h, W, 1, cosf, sins)
    tgt = jnp.concatenate([jnp.zeros((BLOCK, D_MODEL), F32), loss_target[0]], axis=0)
    dh, loss = _loss_head(h[0], tgt, name="loss_head")
    G = [None] * DEPTH
    dh, G[1] = _layer_bwd(dh, saved1, W, 1, cosf, sins)
    ffn2_w, mix_w, ffn1_w = MATMUL_W[7:], MATMUL_W[3:7], MATMUL_W[:3]
    sw_l1 = swap_start(G[1], MATMUL_W, "l1")
    G0 = {}
    dh, dg, db = _ffn_bwd(dh, s2, W, "ffn2", 0, ln("ln3_g", 0) + sw_l1[4][0, 0], ln("ln3_b", 0), G0, "l0_ffn2")
    G0["ln3_g"], G0["ln3_b"] = dg[0], db[0]
    pairs_l1, x_l1 = exchange_start(sw_l1, MATMUL_W, "l1", dh)
    sw_a = swap_start(G0, ffn2_w, "l0_ffn2")
    dh, Gm = _mixer_bwd(dh, sm, W, 0, cosf, sins, G0, zero=x_l1[4][0, 0] + sw_a[4][0, 0])
    G0.update(Gm)
    pairs_a, x_a = exchange_start(sw_a, ffn2_w, "l0_ffn2", dh)
    reduced1 = rs_end(pairs_l1, x_l1, MATMUL_W, "l1", dh)
    pairs_b, x_b = rs_start(G0, mix_w, "l0_mix")
    dh0, dg, db = _ffn_bwd(dh, s1, W, "ffn1", 0, ln("ln1_g", 0) + (x_a[4][0, 0] + x_b[4][0, 0]), ln("ln1_b", 0), G0,
                           "l0_ffn1")
    G0["ln1_g"], G0["ln1_b"] = dg[0], db[0]
    G[0] = G0
    reduced0 = rs_end(pairs_a, x_a, ffn2_w, "l0_ffn2", dh0)
    reduced0.update(rs_end(pairs_b, x_b, mix_w, "l0_mix", dh0))

    small_parts = [jnp.stack([G[l][n] for l in range(DEPTH)]).reshape(-1) for n in SMALL]
    small_parts += [jnp.stack([G[l]["conv_w"] for l in range(DEPTH)]).reshape(-1), dh0[PAD_ROWS:BLOCK].reshape(-1),
                    loss[0, :1]]
    flat = jnp.concatenate(small_parts)
    flat = jnp.pad(flat, (0, SMALL_ROWS * BLOCK - flat.shape[0]))
    red2d = _allreduce_small(flat.reshape(SMALL_ROWS, BLOCK))
    red = red2d.reshape(-1)

    pairs_c = _rs_pair_sums(chunk_grads(G0, ffn1_w), ffn1_w, "l0_ffn1")
    x_c = _ici_start(_exchange_copies, [p[1] for p in pairs_c], _exchange_land_shapes(pairs_c),
                     name="rs_exchange_l0_ffn1_start", after=[red2d])
    grads, off = {}, 0
    for n in SMALL:
        size = int(np.prod(w[n].shape))
        grads[n] = red[off:off + size].reshape(w[n].shape)
        off += size
    conv_full = red[off:off + DEPTH * SSD_CONV * 768].reshape(DEPTH, SSD_CONV, 768)
    off += DEPTH * SSD_CONV * 768
    dmeta_full = red[off:off + N_META * D_MODEL].reshape(N_META, D_MODEL)
    off += N_META * D_MODEL
    loss_out = red[off]
    grads["conv_w"] = _chip_cols(conv_full, chip, 768 // N_CHIPS)
    grads["meta"] = _chip_cols(dmeta_full, chip, D_MODEL // N_CHIPS)

    delta, new_m, new_v = {}, {}, {}

    def adamw_matmul_weights(names, after):
        for n in names:
            gs = [reduced0[n], reduced1[n]]
            if n.endswith("w_gate") or n.endswith("w_up"):
                res = _adamw(tr(w[n]), gs, tr(m[n]), tr(v[n]), name=f"adamw_{n}", after=after)
                grads[n], delta[n], new_m[n], new_v[n] = [tr(r) for r in res]
            else:
                grads[n], delta[n], new_m[n], new_v[n] = _adamw(w[n], gs, m[n], v[n], name=f"adamw_{n}", after=after)

    adamw_matmul_weights(ffn2_w + mix_w, [x_c[4]])
    rest = [n for n in WEIGHTS if n not in MATMUL_W]

    def pack_small(d):
        f = jnp.concatenate([d[n].reshape(-1) for n in rest])
        tot = -(-f.shape[0] // (8 * BLOCK)) * 8 * BLOCK
        return jnp.pad(f, (0, tot - f.shape[0])).reshape(-1, BLOCK)

    _, d2, m2, v2 = _adamw(pack_small(w), [pack_small(grads)], pack_small(m), pack_small(v), name="adamw_small",
                           after=[x_c[4]])
    reduced0.update(rs_end(pairs_c, x_c, ffn1_w, "l0_ffn1", d2))
    adamw_matmul_weights(ffn1_w, [])
    off = 0
    for n in rest:
        size = int(np.prod(w[n].shape))
        for dst, src in ((delta, d2), (new_m, m2), (new_v, v2)):
            dst[n] = src.reshape(-1)[off:off + size].reshape(w[n].shape)
        off += size

    grad_x = dh0[BLOCK:][None]
    return (loss_out, grad_x, *[grads[n] for n in WEIGHTS], *[delta[n] for n in WEIGHTS],
            *[new_m[n] for n in WEIGHTS], *[new_v[n] for n in WEIGHTS])
```

```python
import functools

import numpy as np
import jax
import jax.numpy as jnp
from jax import lax
from jax.experimental import pallas as pl
from jax.experimental.pallas import tpu as pltpu

F32 = jnp.float32
BF16 = jnp.bfloat16
MESH = pl.DeviceIdType.MESH

D_MODEL = 1024
SEQ = 2048
N_META = 16
BLOCK = 128
PAD_ROWS = 112
LP = PAD_ROWS + N_META + SEQ
N_CHUNK = LP // BLOCK
DEPTH = 2
D_FF = 2816
N_CHIPS = 4
FF_SHARD = D_FF // N_CHIPS
HP = 768
FP = N_CHIPS * HP
SSD_HEADS, SSD_HD, SSD_D, SSD_GROUPS, SSD_STATE, SSD_CONV = 8, 64, 512, 2, 64, 4
FOX_HEADS, FOX_HD, FOX_D = 4, 64, 256
MLA_HEADS, MLA_Q_LORA, MLA_KV_LORA, MLA_NOPE, MLA_ROPE, MLA_V, MLA_D = 4, 256, 128, 64, 32, 64, 256
ROPE_HALF = MLA_ROPE // 2
ROPE_THETA = 10000.0
N_IN = 2476
IN_SHARD = N_IN // N_CHIPS
IN_SHARD_P = 640
ALPHA = (2 * DEPTH) ** 0.25
EPS = 1e-5
ADAM_LR, ADAM_B1, ADAM_B2, ADAM_EPS, ADAM_WD, ADAM_STEP = 0.001, 0.9, 0.999, 1e-08, 0.01, 10
NEG = -1e30
TM = 544

VMEM_LIMIT_BYTES = 56 * 1024 * 1024

PC_Z, PC_XBC, PC_FQ, PC_FK, PC_FV, PC_CQ, PC_CKV, PC_DT, PC_FR, PC_KR, PC_END = (
    0, 512, 1280, 1536, 1792, 2048, 2304, 2432, 2560, 2688, 2816)
OC_Z, OC_XBC, OC_DT, OC_FQ, OC_FK, OC_FV, OC_FR, OC_CQ, OC_CKV, OC_KR = (
    0, 512, 1280, 1288, 1544, 1800, 2056, 2060, 2316, 2444)


def _cparams(sem=None):
    return pltpu.CompilerParams(dimension_semantics=sem, vmem_limit_bytes=VMEM_LIMIT_BYTES)


def _tile(n, cap, mult):
    best = None
    for t in range(mult, min(n, cap) + 1, mult):
        if n % t == 0:
            best = t
    return best if best is not None else n


def _bs(shape, fn):
    return pl.BlockSpec(shape, fn)


ANY = pl.BlockSpec(memory_space=pl.ANY)


def _dims(ca, cb):
    return (((ca,), (cb,)), ((), ()))


def _raw_bdot(a, b, ca, cb):
    return lax.dot_general(a.astype(BF16), b.astype(BF16), _dims(ca, cb), preferred_element_type=F32)


def _mm_core(a, b, *, a_spec, b_spec, o_spec, grid, out_shape, ca, cb, name, add=None, after=()):
    nk = grid[2]
    has_add = add is not None
    acc_shape = tuple(d for d in o_spec.block_shape if d is not None)

    def body(*refs):
        a_ref, b_ref = refs[0], refs[1]
        add_ref = refs[2] if has_add else None
        o_ref, acc_ref = refs[-2], refs[-1]
        k = pl.program_id(2)

        @pl.when(k == 0)
        def _():
            acc_ref[...] = jnp.zeros_like(acc_ref)

        acc_ref[...] += _raw_bdot(a_ref[...], b_ref[...], ca, cb)

        @pl.when(k == nk - 1)
        def _():
            r = acc_ref[...]
            if has_add:
                r = r + add_ref[...]
            o_ref[...] = r

    ins = [a, b] + ([add] if has_add else []) + list(after)
    in_specs = [a_spec, b_spec] + ([o_spec] if has_add else []) + [ANY] * len(after)
    return pl.pallas_call(
        body, name=name, grid=grid, in_specs=in_specs, out_specs=o_spec,
        out_shape=jax.ShapeDtypeStruct(out_shape, F32), scratch_shapes=[pltpu.VMEM(acc_shape, F32)],
        compiler_params=_cparams(("parallel", "parallel", "arbitrary")),
    )(*ins)


MM_VMEM_BUDGET = 40 * 1024 * 1024


def _divisors(n, mult):
    return [t for t in range(mult, n + 1, mult) if n % t == 0] or [n]


def _pick_tiles(M, N, K, a_bytes, b_bytes, ta, has_add):
    best = None
    for tm in _divisors(M, 128 if ta else 16):
        for tn in _divisors(N, 128):
            vmem = 2 * tm * K * a_bytes + 2 * K * tn * b_bytes + (3 + 2 * int(has_add)) * tm * tn * 4
            if vmem <= MM_VMEM_BUDGET:
                key = ((M // tm) * (N // tn), -tn)
                if best is None or key < best[0]:
                    best = (key, tm, tn)
    assert best is not None, (M, N, K)
    return best[1], best[2], K


def _mm(a, b, *, ta=False, tb=False, add=None, name, after=()):
    if ta:
        K, M = a.shape
    else:
        M, K = a.shape
    if tb:
        N, Kb = b.shape
    else:
        Kb, N = b.shape
    assert K == Kb, (a.shape, b.shape, ta, tb)
    tm, tn, tk = _pick_tiles(M, N, K, a.dtype.itemsize, b.dtype.itemsize, ta, add is not None)
    a_spec = _bs((tk, tm), lambda i, j, k: (k, i)) if ta else _bs((tm, tk), lambda i, j, k: (i, k))
    b_spec = _bs((tn, tk), lambda i, j, k: (j, k)) if tb else _bs((tk, tn), lambda i, j, k: (k, j))
    return _mm_core(a, b, a_spec=a_spec, b_spec=b_spec, o_spec=_bs((tm, tn), lambda i, j, k: (i, j)),
                    grid=(M // tm, N // tn, K // tk), out_shape=(M, N), ca=0 if ta else 1, cb=1 if tb else 0,
                    name=name, add=add, after=after)


def _row_entry(r, ncol):
    if isinstance(r, tuple):
        return r
    return r, r.shape[1] // ncol, 0


def _rowwise(fn, rows, pars, out_cols, *, name, tile, ncol=1, out_dtypes=None):
    rows = [_row_entry(r, ncol) for r in rows]
    L = rows[0][0].shape[0]
    nr, npar = len(rows), len(pars)
    in_specs = [_bs((tile, w), lambda g, i, o=o: (i, o + g)) for _, w, o in rows]
    in_specs += [_bs((p.shape[0], p.shape[1] // ncol), lambda g, i: (0, g)) for p in pars]
    out_specs = [_bs((tile, c // ncol), lambda g, i: (i, g)) for c in out_cols]

    def body(*refs):
        ins, outs = refs[:nr + npar], refs[nr + npar:]
        row0 = pl.program_id(1) * tile
        res = fn(row0, *[r[...] for r in ins])
        for o, v in zip(outs, res):
            o[...] = v.astype(o.dtype)

    return pl.pallas_call(
        body, name=name, grid=(ncol, L // tile), in_specs=in_specs, out_specs=out_specs,
        out_shape=[jax.ShapeDtypeStruct((L, c), d) for c, d in zip(out_cols, out_dtypes or [F32] * len(out_cols))],
        compiler_params=_cparams(("parallel", "parallel")),
    )(*[r[0] for r in rows], *pars)


def _rowwise_bwd(fn, rows, pars, douts, *, name, tile, ncol=1, row_grad=None, grad_dtypes=None):
    rows = [_row_entry(r, ncol) for r in rows]
    L = rows[0][0].shape[0]
    nr, npar, nd = len(rows), len(pars), len(douts)
    row_grad = [True] * nr if row_grad is None else row_grad
    in_specs = [_bs((tile, w), lambda g, i, o=o: (i, o + g)) for _, w, o in rows]
    in_specs += [_bs((p.shape[0], p.shape[1] // ncol), lambda g, i: (0, g)) for p in pars]
    in_specs += [_bs((tile, d.shape[1] // ncol), lambda g, i: (i, g)) for d in douts]
    g_widths = [w * ncol for (_, w, _), f in zip(rows, row_grad) if f]
    out_specs = [_bs((tile, w // ncol), lambda g, i: (i, g)) for w in g_widths]
    out_specs += [_bs((p.shape[0], p.shape[1] // ncol), lambda g, i: (0, g)) for p in pars]
    out_shape = [jax.ShapeDtypeStruct((L, w), d) for w, d in zip(g_widths, grad_dtypes or [F32] * len(g_widths))]
    out_shape += [jax.ShapeDtypeStruct(p.shape, F32) for p in pars]

    def body(*refs):
        ins = refs[:nr + npar]
        dos = refs[nr + npar:nr + npar + nd]
        outs = refs[nr + npar + nd:]
        i = pl.program_id(1)
        row0 = i * tile
        _, vjp = jax.vjp(lambda *a: tuple(fn(row0, *a)), *[r[...] for r in ins])
        grads = vjp(tuple(d[...].astype(F32) for d in dos))
        o = 0
        for j in range(nr):
            if row_grad[j]:
                outs[o][...] = grads[j].astype(outs[o].dtype)
                o += 1
        for j in range(npar):
            g, ref = grads[nr + j], outs[o + j]

            @pl.when(i == 0)
            def _(g=g, ref=ref):
                ref[...] = g

            @pl.when(i > 0)
            def _(g=g, ref=ref):
                ref[...] += g

    res = pl.pallas_call(
        body, name=name, grid=(ncol, L // tile), in_specs=in_specs, out_specs=out_specs, out_shape=out_shape,
        compiler_params=_cparams(("parallel", "arbitrary")),
    )(*[r[0] for r in rows], *pars, *douts)
    return res[:len(g_widths)], res[len(g_widths):]


def _sigmoid(x):
    return 1.0 / (1.0 + jnp.exp(-x))


def _softplus(x):
    return jnp.maximum(x, 0.0) + jnp.log(1.0 + jnp.exp(-jnp.abs(x)))


def _silu(x):
    return x * _sigmoid(x)


def _make_res_ln_fn(scale):
    def fn(row0, h, o, gam, bet):
        pre = ALPHA * h + scale * o
        mu = jnp.mean(pre, axis=-1, keepdims=True)
        xc = pre - mu
        var = jnp.mean(xc * xc, axis=-1, keepdims=True)
        return (xc * lax.rsqrt(var + EPS) * gam + bet,)
    return fn


def _ssd_post_fn(row0, y, xs, z, dskip, normg):
    v = (y + dskip * xs) * _silu(z)
    v = v * lax.rsqrt(jnp.mean(v * v, axis=-1, keepdims=True) + EPS)
    return (v * normg,)


def _mla_norm_fn(row0, cq, ckv, gq, gkv):
    qn = cq * lax.rsqrt(jnp.mean(cq * cq, axis=-1, keepdims=True) + EPS) * gq
    cn = ckv * lax.rsqrt(jnp.mean(ckv * ckv, axis=-1, keepdims=True) + EPS) * gkv
    return qn, cn


def _rope_fn(row0, q, k, cosf, sins):
    return (q * cosf + pltpu.roll(q, 64, 1) * sins, k * cosf + pltpu.roll(k, 64, 1) * sins)


def _rope_t_fn(row0, gq, gk, cosf, sins):
    return (gq * cosf + pltpu.roll(gq * sins, 64, 1), gk * cosf + pltpu.roll(gk * sins, 64, 1))


def _conv_fwd(x, x_off, w, b, *, name):
    C = w.shape[1]

    def body(x_ref, w_ref, b_ref, o_ref):
        rows = lax.broadcasted_iota(jnp.int32, (LP, BLOCK), 0)
        xv = jnp.where(rows >= PAD_ROWS, x_ref[...], 0.0)
        acc = b_ref[...] + w_ref[3:4, :] * xv
        for k in range(SSD_CONV - 1):
            acc = acc + w_ref[k:k + 1, :] * pltpu.roll(xv, SSD_CONV - 1 - k, 0)
        o_ref[...] = _silu(acc)

    return pl.pallas_call(
        body, name=name, grid=(C // BLOCK,),
        in_specs=[_bs((LP, BLOCK), lambda j: (0, j + x_off)), _bs((SSD_CONV, BLOCK), lambda j: (0, j)),
                  _bs((1, BLOCK), lambda j: (0, j))],
        out_specs=_bs((LP, BLOCK), lambda j: (0, j)),
        out_shape=jax.ShapeDtypeStruct((LP, C), F32), compiler_params=_cparams(("parallel",)),
    )(x, w, b)


def _conv_bwd(x, x_off, w, b, dout, *, name):
    C = w.shape[1]

    def body(x_ref, w_ref, b_ref, do_ref, dx_ref, dw_ref, db_ref):
        rows = lax.broadcasted_iota(jnp.int32, (LP, BLOCK), 0)
        real = rows >= PAD_ROWS
        xv = jnp.where(real, x_ref[...], 0.0)
        shifted = [pltpu.roll(xv, SSD_CONV - 1 - k, 0) for k in range(SSD_CONV - 1)] + [xv]
        acc = b_ref[...]
        for k in range(SSD_CONV):
            acc = acc + w_ref[k:k + 1, :] * shifted[k]
        sig = _sigmoid(acc)
        dacc = jnp.where(real, do_ref[...] * (sig * (1.0 + acc * (1.0 - sig))), 0.0)
        db_ref[...] = jnp.sum(dacc, axis=0, keepdims=True)
        dx = w_ref[3:4, :] * dacc
        for k in range(SSD_CONV):
            dw_ref[k:k + 1, :] = jnp.sum(dacc * shifted[k], axis=0, keepdims=True)
            if k < SSD_CONV - 1:
                dx = dx + w_ref[k:k + 1, :] * pltpu.roll(dacc, LP - (SSD_CONV - 1 - k), 0)
        dx_ref[...] = jnp.where(real, dx, 0.0)

    return pl.pallas_call(
        body, name=name, grid=(C // BLOCK,),
        in_specs=[_bs((LP, BLOCK), lambda j: (0, j + x_off)), _bs((SSD_CONV, BLOCK), lambda j: (0, j)),
                  _bs((1, BLOCK), lambda j: (0, j)), _bs((LP, BLOCK), lambda j: (0, j))],
        out_specs=[_bs((LP, BLOCK), lambda j: (0, j)), _bs((SSD_CONV, BLOCK), lambda j: (0, j)),
                   _bs((1, BLOCK), lambda j: (0, j))],
        out_shape=[jax.ShapeDtypeStruct((LP, C), F32), jax.ShapeDtypeStruct((SSD_CONV, C), F32),
                   jax.ShapeDtypeStruct((1, C), F32)],
        compiler_params=_cparams(("parallel",)),
    )(x, w, b, dout)


_BDIMS = {"nn": (((2,), (1,)), ((0,), (0,))), "nt": (((2,), (2,)), ((0,), (0,))), "tn": (((1,), (1,)), ((0,), (0,)))}


def _raw_bdot3(a, b, mode):
    return lax.dot_general(a.astype(BF16), b.astype(BF16), _BDIMS[mode], preferred_element_type=F32)


@functools.partial(jax.custom_vjp, nondiff_argnums=(2,))
def _bdot3(a, b, mode):
    return _raw_bdot3(a, b, mode)


def _bdot3_fwd(a, b, mode):
    return _raw_bdot3(a, b, mode), (a, b)


def _bdot3_bwd(mode, res, g):
    a, b = res
    if mode == "nn":
        return _raw_bdot3(g, b, "nt"), _raw_bdot3(a, g, "tn")
    if mode == "nt":
        return _raw_bdot3(g, b, "nn"), _raw_bdot3(g, a, "tn")
    return _raw_bdot3(b, g, "nt"), _raw_bdot3(a, g, "nn")


_bdot3.defvjp(_bdot3_fwd, _bdot3_bwd)


def _ssd_chunk(x, bm, cm, dt, dtt, alog, prev):
    rep = SSD_HEADS // SSD_GROUPS
    per_head = lambda t: jnp.broadcast_to(t[:, None], (SSD_GROUPS, rep) + t.shape[1:]).reshape((SSD_HEADS,) + t.shape[1:])
    bm, cm = per_head(bm), per_head(cm)
    lane_h = lax.broadcasted_iota(jnp.int32, (1, BLOCK), 1)
    row_h = lax.broadcasted_iota(jnp.int32, (BLOCK, 1), 0)
    dtc = jnp.stack([jnp.sum(jnp.where(lane_h == h, dt, 0.0), axis=1, keepdims=True) for h in range(SSD_HEADS)])
    dtr = jnp.stack([jnp.sum(jnp.where(row_h == h, dtt, 0.0), axis=0, keepdims=True) for h in range(SSD_HEADS)])
    lane = lax.broadcasted_iota(jnp.int32, alog.shape, 2)
    a_neg = -jnp.exp(jnp.sum(jnp.where(lane == 0, alog, 0.0), axis=2, keepdims=True))
    ac_in = dtc * a_neg
    ar_in = dtr * a_neg
    li = lax.broadcasted_iota(jnp.int32, (1, BLOCK, BLOCK), 1)
    si = lax.broadcasted_iota(jnp.int32, (1, BLOCK, BLOCK), 2)
    causal = li >= si
    acum_c = jnp.sum(jnp.where(causal, ar_in, 0.0), axis=2, keepdims=True)
    acum_r = jnp.sum(jnp.where(li <= si, ac_in, 0.0), axis=1, keepdims=True)
    total = jnp.sum(ar_in, axis=2, keepdims=True)
    seg = jnp.exp(jnp.where(causal, acum_c - acum_r, NEG))
    xdt = x * dtc
    cb = _bdot3(cm, bm, "nt")
    y = _bdot3(cb * seg, xdt, "nn") + _bdot3(cm, prev, "nt") * jnp.exp(acum_c)
    st = _bdot3(xdt, bm * jnp.exp(total - acum_c), "tn")
    return y, prev * jnp.exp(total) + st


def _ssd_dt_fwd(raw, raw_blk, bias, *, name):
    def body(raw_ref, b_ref, dt_ref, dtt_ref):
        rows = pl.program_id(0) * BLOCK + lax.broadcasted_iota(jnp.int32, (BLOCK, BLOCK), 0)
        dt = jnp.where(rows >= PAD_ROWS, _softplus(raw_ref[...] + b_ref[...]), 0.0)
        dt_ref[...] = dt
        dtt_ref[...] = dt.T

    return pl.pallas_call(
        body, name=name, grid=(N_CHUNK,),
        in_specs=[_bs((BLOCK, BLOCK), lambda j: (j, raw_blk)), _bs((1, BLOCK), lambda j: (0, 0))],
        out_specs=[_bs((BLOCK, BLOCK), lambda j: (j, 0)), _bs((BLOCK, BLOCK), lambda j: (0, j))],
        out_shape=[jax.ShapeDtypeStruct((LP, BLOCK), F32), jax.ShapeDtypeStruct((BLOCK, LP), F32)],
        compiler_params=_cparams(("parallel",)),
    )(raw, bias)


def _ssd_dt_bwd(raw, raw_blk, bias, ddt, ddtt, *, name):
    def body(raw_ref, b_ref, ddt_ref, ddtt_ref, draw_ref, db_ref):
        j = pl.program_id(0)
        rows = j * BLOCK + lax.broadcasted_iota(jnp.int32, (BLOCK, BLOCK), 0)
        g = ddt_ref[...] + ddtt_ref[...].T
        draw = jnp.where(rows >= PAD_ROWS, g * _sigmoid(raw_ref[...] + b_ref[...]), 0.0)
        draw_ref[...] = draw
        dsum = jnp.sum(draw, axis=0, keepdims=True)

        @pl.when(j == 0)
        def _():
            db_ref[...] = dsum

        @pl.when(j > 0)
        def _():
            db_ref[...] += dsum

    return pl.pallas_call(
        body, name=name, grid=(N_CHUNK,),
        in_specs=[_bs((BLOCK, BLOCK), lambda j: (j, raw_blk)), _bs((1, BLOCK), lambda j: (0, 0)),
                  _bs((BLOCK, BLOCK), lambda j: (j, 0)), _bs((BLOCK, BLOCK), lambda j: (0, j))],
        out_specs=[_bs((BLOCK, BLOCK), lambda j: (j, 0)), _bs((1, BLOCK), lambda j: (0, 0))],
        out_shape=[jax.ShapeDtypeStruct((LP, BLOCK), F32), jax.ShapeDtypeStruct((1, BLOCK), F32)],
        compiler_params=_cparams(("arbitrary",)),
    )(raw, bias, ddt, ddtt)


def _ssd_specs(rev):
    ci = (lambda c: N_CHUNK - 1 - c) if rev else (lambda c: c)
    x_spec = _bs((SSD_HEADS, BLOCK, SSD_HD), lambda c: (0, ci(c), 0))
    g_spec = _bs((SSD_GROUPS, BLOCK, SSD_STATE), lambda c: (0, ci(c), 0))
    dtc_spec = _bs((BLOCK, BLOCK), lambda c: (ci(c), 0))
    dtr_spec = _bs((BLOCK, BLOCK), lambda c: (0, ci(c)))
    al_spec = _bs((SSD_HEADS, 1, BLOCK), lambda c: (0, 0, 0))
    st_spec = _bs((None, SSD_HEADS, SSD_HD, SSD_STATE), lambda c: (ci(c), 0, 0, 0))
    return x_spec, g_spec, dtc_spec, dtr_spec, al_spec, st_spec


def _ssd_fwd(x, bm, cm, dtc, dtr, alog, *, name):
    x_spec, g_spec, dtc_spec, dtr_spec, al_spec, st_spec = _ssd_specs(False)

    def body(x_ref, b_ref, c_ref, dtc_ref, dtr_ref, al_ref, y_ref, prev_ref, state):
        @pl.when(pl.program_id(0) == 0)
        def _():
            state[...] = jnp.zeros_like(state)

        prev = state[...]
        prev_ref[...] = prev
        y, new = _ssd_chunk(x_ref[...], b_ref[...], c_ref[...], dtc_ref[...], dtr_ref[...], al_ref[...], prev)
        y_ref[...] = y
        state[...] = new

    return pl.pallas_call(
        body, name=name, grid=(N_CHUNK,),
        in_specs=[x_spec, g_spec, g_spec, dtc_spec, dtr_spec, al_spec], out_specs=[x_spec, st_spec],
        out_shape=[jax.ShapeDtypeStruct((SSD_HEADS, LP, SSD_HD), F32),
                   jax.ShapeDtypeStruct((N_CHUNK, SSD_HEADS, SSD_HD, SSD_STATE), F32)],
        scratch_shapes=[pltpu.VMEM((SSD_HEADS, SSD_HD, SSD_STATE), F32)],
        compiler_params=_cparams(("arbitrary",)),
    )(x, bm, cm, dtc, dtr, alog)


def _ssd_bwd(x, bm, cm, dtc, dtr, alog, prevs, dy, *, name):
    x_spec, g_spec, dtc_spec, dtr_spec, al_spec, st_spec = _ssd_specs(True)

    def body(x_ref, b_ref, c_ref, dtc_ref, dtr_ref, al_ref, prev_ref, dy_ref,
             dx_ref, db_ref, dc_ref, ddtc_ref, ddtr_ref, dal_ref, dstate):
        c = pl.program_id(0)

        @pl.when(c == 0)
        def _():
            dstate[...] = jnp.zeros_like(dstate)

        _, vjp = jax.vjp(_ssd_chunk, x_ref[...], b_ref[...], c_ref[...], dtc_ref[...], dtr_ref[...], al_ref[...],
                         prev_ref[...])
        dx, db, dc, ddtc, ddtr, dal, dprev = vjp((dy_ref[...], dstate[...]))
        dx_ref[...] = dx
        db_ref[...] = db
        dc_ref[...] = dc
        ddtc_ref[...] = ddtc
        ddtr_ref[...] = ddtr
        dstate[...] = dprev

        @pl.when(c == 0)
        def _():
            dal_ref[...] = dal

        @pl.when(c > 0)
        def _():
            dal_ref[...] += dal

    hs = jax.ShapeDtypeStruct((SSD_HEADS, LP, SSD_HD), F32)
    gs = jax.ShapeDtypeStruct((SSD_GROUPS, LP, SSD_STATE), F32)
    return pl.pallas_call(
        body, name=name, grid=(N_CHUNK,),
        in_specs=[x_spec, g_spec, g_spec, dtc_spec, dtr_spec, al_spec, st_spec, x_spec],
        out_specs=[x_spec, g_spec, g_spec, dtc_spec, dtr_spec, al_spec],
        out_shape=[hs, gs, gs, jax.ShapeDtypeStruct((LP, BLOCK), F32),
                   jax.ShapeDtypeStruct((BLOCK, LP), F32), jax.ShapeDtypeStruct((SSD_HEADS, 1, BLOCK), F32)],
        scratch_shapes=[pltpu.VMEM((SSD_HEADS, SSD_HD, SSD_STATE), F32)],
        compiler_params=_cparams(("arbitrary",)),
    )(x, bm, cm, dtc, dtr, alog, prevs, dy)


def _tri_dot(tri, v):
    hi = v.astype(BF16)
    r1 = v - hi.astype(F32)
    mid = r1.astype(BF16)
    lo = (r1 - mid.astype(F32)).astype(BF16)
    t = tri.astype(BF16)
    d = lambda p: lax.dot_general(t, p, _dims(1, 0), preferred_element_type=F32)
    return d(hi) + d(mid) + d(lo)


def _fox_gate_fwd(raw, raw_blk, bias, *, name):
    def body(raw_ref, b_ref, c_ref, ct_ref, carry):
        j = pl.program_id(0)

        @pl.when(j == 0)
        def _():
            carry[...] = jnp.zeros_like(carry)

        rows = j * BLOCK + lax.broadcasted_iota(jnp.int32, (BLOCK, BLOCK), 0)
        lf = jnp.where(rows >= PAD_ROWS, -_softplus(-(raw_ref[...] + b_ref[...])), 0.0)
        li = lax.broadcasted_iota(jnp.int32, (BLOCK, BLOCK), 0)
        si = lax.broadcasted_iota(jnp.int32, (BLOCK, BLOCK), 1)
        cv = _tri_dot(jnp.where(li >= si, 1.0, 0.0), lf) + carry[...]
        c_ref[...] = cv
        ct_ref[...] = cv.T
        carry[...] += jnp.sum(lf, axis=0, keepdims=True)

    return pl.pallas_call(
        body, name=name, grid=(N_CHUNK,),
        in_specs=[_bs((BLOCK, BLOCK), lambda j: (j, raw_blk)), _bs((1, BLOCK), lambda j: (0, 0))],
        out_specs=[_bs((BLOCK, BLOCK), lambda j: (j, 0)), _bs((BLOCK, BLOCK), lambda j: (0, j))],
        out_shape=[jax.ShapeDtypeStruct((LP, BLOCK), F32), jax.ShapeDtypeStruct((BLOCK, LP), F32)],
        scratch_shapes=[pltpu.VMEM((1, BLOCK), F32)], compiler_params=_cparams(("arbitrary",)),
    )(raw, bias)


def _fox_gate_bwd(raw, raw_blk, bias, dc, dct, *, name):
    rj = lambda j: N_CHUNK - 1 - j

    def body(raw_ref, b_ref, dc_ref, dct_ref, draw_ref, db_ref, carry):
        j = pl.program_id(0)

        @pl.when(j == 0)
        def _():
            carry[...] = jnp.zeros_like(carry)

        rows = (N_CHUNK - 1 - j) * BLOCK + lax.broadcasted_iota(jnp.int32, (BLOCK, BLOCK), 0)
        li = lax.broadcasted_iota(jnp.int32, (BLOCK, BLOCK), 0)
        si = lax.broadcasted_iota(jnp.int32, (BLOCK, BLOCK), 1)
        dcv = dc_ref[...] + dct_ref[...].T
        dlf = _tri_dot(jnp.where(li <= si, 1.0, 0.0), dcv) + carry[...]
        carry[...] += jnp.sum(dcv, axis=0, keepdims=True)
        draw = jnp.where(rows >= PAD_ROWS, dlf * (1.0 - _sigmoid(raw_ref[...] + b_ref[...])), 0.0)
        draw_ref[...] = draw
        dsum = jnp.sum(draw, axis=0, keepdims=True)

        @pl.when(j == 0)
        def _():
            db_ref[...] = dsum

        @pl.when(j > 0)
        def _():
            db_ref[...] += dsum

    return pl.pallas_call(
        body, name=name, grid=(N_CHUNK,),
        in_specs=[_bs((BLOCK, BLOCK), lambda j: (rj(j), raw_blk)), _bs((1, BLOCK), lambda j: (0, 0)),
                  _bs((BLOCK, BLOCK), lambda j: (rj(j), 0)), _bs((BLOCK, BLOCK), lambda j: (0, rj(j)))],
        out_specs=[_bs((BLOCK, BLOCK), lambda j: (rj(j), 0)), _bs((1, BLOCK), lambda j: (0, 0))],
        out_shape=[jax.ShapeDtypeStruct((LP, BLOCK), F32), jax.ShapeDtypeStruct((1, BLOCK), F32)],
        scratch_shapes=[pltpu.VMEM((1, BLOCK), F32)], compiler_params=_cparams(("arbitrary",)),
    )(raw, bias, dc, dct)


ATT_W = 256
ATT_QB = 272
ATT_STEPS = LP // ATT_QB
ATT_KEYS = (640, 1152, 1664, LP)
ATT_BLOCKS_PER_CLASS = ATT_STEPS // len(ATT_KEYS)


def _lane_head(width, per, mod=None):
    lane = lax.broadcasted_iota(jnp.int32, (1, width), 1)
    if mod is not None:
        lane = lane % mod
    return lane // per


def _attn_mask(i, kw):
    r = i * ATT_QB + lax.broadcasted_iota(jnp.int32, (ATT_QB, kw), 0)
    c = lax.broadcasted_iota(jnp.int32, (ATT_QB, kw), 1)
    return (c <= r) & ((c >= PAD_ROWS) | (r < PAD_ROWS))


def _attn_by_key_class(i, fn):
    for p, kw in enumerate(ATT_KEYS):
        @pl.when(i // ATT_BLOCKS_PER_CLASS == p)
        def _(kw=kw):
            fn(kw)


def _attn_specs(q, k, v, bias, rope):
    qspec = lambda blk, w=ATT_W: _bs((ATT_QB, w), lambda i: (i, blk))
    fspec = lambda blk, w=ATT_W: _bs((LP, w), lambda i: (0, blk))
    ins = [q[0], k[0], v[0]]
    specs = [qspec(q[1]), fspec(k[1]), fspec(v[1])]
    if bias is not None:
        ins += [bias[0], bias[1]]
        specs += [qspec(0, BLOCK), _bs((BLOCK, LP), lambda i: (0, 0))]
    if rope is not None:
        ins += [rope[0][0], rope[1][0]]
        specs += [qspec(rope[0][1], BLOCK), fspec(rope[1][1], BLOCK)]
    return ins, specs, qspec, fspec


def _attn_fwd(q, k, v, *, scale, name, bias=None, rope=None):
    ins, specs, qspec, fspec = _attn_specs(q, k, v, bias, rope)
    has_bias, has_rope = bias is not None, rope is not None

    def body(*refs):
        it = iter(refs)
        q_ref, k_ref, v_ref = next(it), next(it), next(it)
        if has_bias:
            c_ref, ct_ref = next(it), next(it)
        if has_rope:
            qr_ref, kr_ref = next(it), next(it)
        o_ref, lse_ref = next(it), next(it)
        i = pl.program_id(0)

        def block(kw):
            ok = _attn_mask(i, kw)
            qv, kv, vv = q_ref[...].astype(BF16), k_ref[0:kw, :].astype(BF16), v_ref[0:kw, :].astype(BF16)
            hid, l128 = _lane_head(ATT_W, FOX_HD), _lane_head(BLOCK, 1)
            if has_rope:
                rid = _lane_head(BLOCK, ROPE_HALF, 64)
                qrv, krv = qr_ref[...].astype(BF16), kr_ref[0:kw, :].astype(BF16)
            def head(h, carry):
                o_acc, lse_acc = carry
                s = _raw_bdot(jnp.where(hid == h, qv, 0.0), kv, 1, 1)
                if has_rope:
                    s = s + _raw_bdot(jnp.where(rid == h, qrv, 0.0), krv, 1, 1)
                s = s * scale
                if has_bias:
                    cq = jnp.sum(jnp.where(l128 == h, c_ref[...], 0.0), axis=1, keepdims=True)
                    s = s + (cq - ct_ref[pl.ds(h, 1), 0:kw])
                s = jnp.where(ok, s, NEG)
                m = jnp.max(s, axis=1, keepdims=True)
                p = jnp.exp(s - m)
                l = jnp.sum(p, axis=1, keepdims=True)
                o_acc = jnp.where(hid == h, _raw_bdot(p, vv, 1, 0) / l, o_acc)
                lse_acc = jnp.where(l128 == h, m + jnp.log(l), lse_acc)
                return o_acc, lse_acc

            o_acc, lse_acc = lax.fori_loop(
                0, FOX_HEADS, head, (jnp.zeros((ATT_QB, ATT_W), F32), jnp.zeros((ATT_QB, BLOCK), F32)), unroll=True)
            o_ref[...] = o_acc
            lse_ref[...] = lse_acc

        _attn_by_key_class(i, block)

    return pl.pallas_call(
        body, name=name, grid=(ATT_STEPS,), in_specs=specs, out_specs=[qspec(0), qspec(0, BLOCK)],
        out_shape=[jax.ShapeDtypeStruct((LP, ATT_W), F32), jax.ShapeDtypeStruct((LP, BLOCK), F32)],
        compiler_params=_cparams(("parallel",)),
    )(*ins)


def _attn_bwd(q, k, v, o, lse, do, *, scale, name, bias=None, rope=None):
    ins, specs, qspec, fspec = _attn_specs(q, k, v, bias, rope)
    has_bias, has_rope = bias is not None, rope is not None
    ins += [o, lse, do[0]]
    specs += [qspec(0), qspec(0, BLOCK), qspec(do[1])]

    def body(*refs):
        it = iter(refs)
        q_ref, k_ref, v_ref = next(it), next(it), next(it)
        if has_bias:
            c_ref, ct_ref = next(it), next(it)
        if has_rope:
            qr_ref, kr_ref = next(it), next(it)
        o_ref, lse_ref, do_ref = next(it), next(it), next(it)
        dq_ref, dk_ref, dv_ref = next(it), next(it), next(it)
        if has_bias:
            dc_ref, dct_ref = next(it), next(it)
        if has_rope:
            dqr_ref, dkr_ref = next(it), next(it)
        i = pl.program_id(0)

        @pl.when(i == 0)
        def _():
            dk_ref[...] = jnp.zeros_like(dk_ref)
            dv_ref[...] = jnp.zeros_like(dv_ref)
            if has_rope:
                dkr_ref[...] = jnp.zeros_like(dkr_ref)
            if has_bias:
                dct_ref[...] = jnp.zeros_like(dct_ref)

        def block(kw):
            ok = _attn_mask(i, kw)
            qv, kv, vv = q_ref[...].astype(BF16), k_ref[0:kw, :].astype(BF16), v_ref[0:kw, :].astype(BF16)
            dov, lsev = do_ref[...], lse_ref[...]
            dov_ov = dov * o_ref[...]
            dov = dov.astype(BF16)
            hid, l128 = _lane_head(ATT_W, FOX_HD), _lane_head(BLOCK, 1)
            if has_rope:
                rid = _lane_head(BLOCK, ROPE_HALF, 64)
                qrv, krv = qr_ref[...].astype(BF16), kr_ref[0:kw, :].astype(BF16)

            def head(h, carry):
                dq_acc, aux_acc = carry
                qm = jnp.where(hid == h, qv, 0.0)
                s = _raw_bdot(qm, kv, 1, 1)
                if has_rope:
                    qrm = jnp.where(rid == h, qrv, 0.0)
                    s = s + _raw_bdot(qrm, krv, 1, 1)
                s = s * scale
                if has_bias:
                    cq = jnp.sum(jnp.where(l128 == h, c_ref[...], 0.0), axis=1, keepdims=True)
                    s = s + (cq - ct_ref[pl.ds(h, 1), 0:kw])
                s = jnp.where(ok, s, NEG)
                p = jnp.exp(s - jnp.sum(jnp.where(l128 == h, lsev, 0.0), axis=1, keepdims=True))
                dom = jnp.where(hid == h, dov, 0.0)
                dp = _raw_bdot(dom, vv, 1, 1)
                delta = jnp.sum(jnp.where(hid == h, dov_ov, 0.0), axis=1, keepdims=True)
                ds = p * (dp - delta)
                dsb, pb = ds.astype(BF16), p.astype(BF16)
                dq_acc = jnp.where(hid == h, _raw_bdot(dsb, kv, 1, 0) * scale, dq_acc)
                dk_ref[0:kw, :] += _raw_bdot(dsb, qm, 0, 0) * scale
                dv_ref[0:kw, :] += _raw_bdot(pb, dom, 0, 0)
                if has_rope:
                    aux_acc = jnp.where(rid == h, _raw_bdot(dsb, krv, 1, 0) * scale, aux_acc)
                    dkr_ref[0:kw, :] += _raw_bdot(dsb, qrm, 0, 0) * scale
                if has_bias:
                    aux_acc = jnp.where(l128 == h, jnp.sum(ds, axis=1, keepdims=True), aux_acc)
                    dct_ref[pl.ds(h, 1), 0:kw] -= jnp.sum(ds, axis=0, keepdims=True)
                return dq_acc, aux_acc

            dq_acc, aux_acc = lax.fori_loop(
                0, FOX_HEADS, head, (jnp.zeros((ATT_QB, ATT_W), F32), jnp.zeros((ATT_QB, BLOCK), F32)))
            dq_ref[...] = dq_acc
            if has_bias:
                dc_ref[...] = aux_acc
            if has_rope:
                dqr_ref[...] = aux_acc

        _attn_by_key_class(i, block)

    wide = jax.ShapeDtypeStruct((LP, ATT_W), F32)
    narrow = jax.ShapeDtypeStruct((LP, BLOCK), F32)
    out_specs = [qspec(0), fspec(0), fspec(0)]
    out_shape = [wide, wide, wide]
    if has_bias:
        out_specs += [qspec(0, BLOCK), _bs((BLOCK, LP), lambda i: (0, 0))]
        out_shape += [narrow, jax.ShapeDtypeStruct((BLOCK, LP), F32)]
    if has_rope:
        out_specs += [qspec(0, BLOCK), fspec(0, BLOCK)]
        out_shape += [narrow, narrow]
    return pl.pallas_call(
        body, name=name, grid=(ATT_STEPS,), in_specs=specs, out_specs=out_specs, out_shape=out_shape,
        compiler_params=_cparams(("arbitrary",)),
    )(*ins)


def _loss_head(y, target, *, name):
    tile = 272

    def body(y_ref, t_ref, dy_ref, loss_ref):
        i = pl.program_id(0)
        rows = i * tile + lax.broadcasted_iota(jnp.int32, (tile, D_MODEL), 0)
        err = jnp.where(rows >= BLOCK, y_ref[...] - t_ref[...], 0.0)
        dy_ref[...] = err * (1.0 / D_MODEL)
        part = 0.5 * jnp.sum(jnp.sum(err * err, axis=1, keepdims=True) * (1.0 / D_MODEL), axis=0, keepdims=True)
        part = jnp.broadcast_to(part, (1, BLOCK))

        @pl.when(i == 0)
        def _():
            loss_ref[...] = part

        @pl.when(i > 0)
        def _():
            loss_ref[...] += part

    return pl.pallas_call(
        body, name=name, grid=(LP // tile,),
        in_specs=[_bs((tile, D_MODEL), lambda i: (i, 0)), _bs((tile, D_MODEL), lambda i: (i, 0))],
        out_specs=[_bs((tile, D_MODEL), lambda i: (i, 0)), _bs((1, BLOCK), lambda i: (0, 0))],
        out_shape=[jax.ShapeDtypeStruct((LP, D_MODEL), F32), jax.ShapeDtypeStruct((1, BLOCK), F32)],
        compiler_params=_cparams(("arbitrary",)),
    )(y, target)


def _adamw(w, gs, m, v, *, name, after=()):
    if w.ndim == 2:
        w, m, v = w[None], m[None], v[None]
        squeeze = True
    else:
        squeeze = False
    NL, R, C = w.shape
    assert len(gs) == NL
    CG = gs[0].shape[1]
    tile = _tile(R, 256, 8)

    def body(*refs):
        w_ref, g_refs = refs[0], refs[1:1 + NL]
        m_ref, v_ref = refs[1 + NL:3 + NL]
        go_ref, d_ref, nm_ref, nv_ref = refs[3 + NL + len(after):]
        gv = g_refs[0][:, :C]
        for j in range(1, NL):
            gv = jnp.where(pl.program_id(0) == j, g_refs[j][:, :C], gv)
        nm = ADAM_B1 * m_ref[...] + (1.0 - ADAM_B1) * gv
        nv = ADAM_B2 * v_ref[...] + (1.0 - ADAM_B2) * (gv * gv)
        m_hat = nm / (1.0 - ADAM_B1 ** ADAM_STEP)
        v_hat = nv / (1.0 - ADAM_B2 ** ADAM_STEP)
        go_ref[...] = gv
        d_ref[...] = -ADAM_LR * (m_hat / (jnp.sqrt(v_hat) + ADAM_EPS) + ADAM_WD * w_ref[...])
        nm_ref[...] = nm
        nv_ref[...] = nv

    spec = _bs((None, tile, C), lambda l, i: (l, i, 0))
    gspecs = [_bs((tile, CG), lambda l, i, j=j: (jnp.where(l == j, i, 0), 0)) for j in range(NL)]
    res = pl.pallas_call(
        body, name=name, grid=(NL, R // tile), in_specs=[spec, *gspecs, spec, spec, *[ANY] * len(after)],
        out_specs=[spec] * 4, out_shape=[jax.ShapeDtypeStruct((NL, R, C), F32)] * 4,
        compiler_params=_cparams(("parallel", "parallel")),
    )(w, *gs, m, v, *after)
    return [r[0] for r in res] if squeeze else res


def _my_pos():
    return lax.axis_index("x"), lax.axis_index("y"), lax.axis_index("c")


def _other_chips(x, y):
    return [(1 - x, y), (x, 1 - y), (1 - x, 1 - y)]


def _allgather_chips(shards):
    n = len(shards)
    per = 7

    def body(*refs):
        ins, outs = refs[:n], refs[n:2 * n]
        send_sems, recv_sems = refs[2 * n], refs[2 * n + 1]
        x, y, c = _my_pos()
        chips = _other_chips(x, y)
        sibling, me = (x, y, 1 - c), 2 * x + y

        def cp(a, kk, src, dst, to):
            return pltpu.make_async_remote_copy(src_ref=src, dst_ref=dst, send_sem=send_sems.at[per * a + kk],
                                                recv_sem=recv_sems.at[per * a + kk], device_id=to, device_id_type=MESH)

        sends = []
        for a in range(n):
            for j, chip in enumerate(chips):
                sends.append(cp(a, j, ins[a].at[c], outs[a].at[me, c], (*chip, c)))
            sends.append(cp(a, 3, ins[a], outs[a].at[me], sibling))
        for s in sends:
            s.start()
        for a in range(n):
            for j, chip in enumerate(chips):
                slab = outs[a].at[2 * chip[0] + chip[1], c]
                cp(a, j, slab, slab, (x, y, c)).wait_recv()
                fwd = cp(a, 4 + j, slab, slab, sibling)
                fwd.start()
                sends.append(fwd)
        for a in range(n):
            cp(a, 3, ins[a], outs[a].at[me], (x, y, c)).wait_recv()
            for j, chip in enumerate(chips):
                slab = outs[a].at[2 * chip[0] + chip[1], 1 - c]
                cp(a, 4 + j, slab, slab, (x, y, c)).wait_recv()
        for s in sends:
            s.wait_send()

    return pl.pallas_call(
        body, name="allgather_chips", in_specs=[ANY] * n, out_specs=[ANY] * n,
        out_shape=[jax.ShapeDtypeStruct((N_CHIPS,) + s.shape, s.dtype) for s in shards],
        scratch_shapes=[pltpu.SemaphoreType.DMA((per * n,)), pltpu.SemaphoreType.DMA((per * n,))],
    )(*shards)


def _rs_swap_rows(gs, tag):
    n = len(gs)

    def body(*refs):
        ins, outs = refs[:n], refs[n:2 * n]
        send_sems, recv_sems = refs[2 * n], refs[2 * n + 1]
        x, y, c = _my_pos()
        cps = []
        for a in range(n):
            half = ins[a].shape[1] // 2
            cps.append(pltpu.make_async_remote_copy(
                src_ref=ins[a].at[:, pl.ds((1 - c) * half, half)], dst_ref=outs[a], send_sem=send_sems.at[a],
                recv_sem=recv_sems.at[a], device_id=(x, y, 1 - c), device_id_type=MESH))
        for cp in cps:
            cp.start()
        for cp in cps:
            cp.wait()

    return pl.pallas_call(
        body, name=f"rs_swap_rows_{tag}", in_specs=[ANY] * n, out_specs=[ANY] * n,
        out_shape=[jax.ShapeDtypeStruct((N_CHIPS, g.shape[1] // 2, g.shape[2]), g.dtype) for g in gs],
        scratch_shapes=[pltpu.SemaphoreType.DMA((n,)), pltpu.SemaphoreType.DMA((n,))],
    )(*gs)


RS_ADD_VMEM_BYTES = 24 * 1024 * 1024


def _rs_tile(H, C, n):
    return _tile(H, max(16, RS_ADD_VMEM_BYTES // (28 * n * C)), 16)


def _rs_add_pair(gs, rs, pos, *, name):
    n = len(gs)
    _, H, C = rs[0].shape
    tile = _rs_tile(H, C, n)
    nt = H // tile

    def body(pos_ref, *refs):
        for a in range(n):
            s = refs[a][...] + refs[n + a][...]
            refs[2 * n + 2 * a][...] = s
            refs[2 * n + 2 * a + 1][...] = s.astype(BF16)

    spec = _bs((None, tile, C), lambda k, i, pos_ref: (k, i, 0))
    g_spec = _bs((None, tile, C), lambda k, i, pos_ref: (k, pos_ref[1] * nt + i, 0))
    grid_spec = pltpu.PrefetchScalarGridSpec(
        num_scalar_prefetch=1, grid=(N_CHIPS, nt), in_specs=[g_spec] * n + [spec] * n, out_specs=[spec] * (2 * n))
    res = pl.pallas_call(
        body, name=name, grid_spec=grid_spec,
        out_shape=[jax.ShapeDtypeStruct((N_CHIPS, H, C), F32), jax.ShapeDtypeStruct((N_CHIPS, H, C), BF16)] * n,
        compiler_params=_cparams(("parallel", "parallel")),
    )(pos, *gs, *rs)
    return [(res[2 * a], res[2 * a + 1]) for a in range(n)]


def _exchange_copies(srcs, lands, send_sems, recv_sems):
    x, y, c = _my_pos()
    starts, landing = [], []
    for a in range(len(srcs)):
        for j, chip in enumerate(_other_chips(x, y)):
            sems = dict(send_sem=send_sems.at[3 * a + j], recv_sem=recv_sems.at[3 * a + j], device_id_type=MESH)
            starts.append(pltpu.make_async_remote_copy(
                src_ref=srcs[a].at[2 * chip[0] + chip[1]], dst_ref=lands[a].at[j], device_id=(*chip, c), **sems))
            landing.append(pltpu.make_async_remote_copy(
                src_ref=lands[a].at[j], dst_ref=lands[a].at[j], device_id=(x, y, c), **sems))
    return starts, landing


def _gather_copies(srcs, lands, send_sems, recv_sems):
    x, y, c = _my_pos()
    me = 2 * x + y
    starts, landing = [], []
    for a in range(len(srcs)):
        half = srcs[a].shape[0] // 2
        mine = pl.ds(c * half, half)
        for j, chip in enumerate(_other_chips(x, y)):
            sems = dict(send_sem=send_sems.at[3 * a + j], recv_sem=recv_sems.at[3 * a + j], device_id_type=MESH)
            starts.append(pltpu.make_async_remote_copy(
                src_ref=srcs[a].at[mine], dst_ref=lands[a].at[me, mine], device_id=(*chip, c), **sems))
            slab = lands[a].at[2 * chip[0] + chip[1], mine]
            landing.append(pltpu.make_async_remote_copy(src_ref=slab, dst_ref=slab, device_id=(x, y, c), **sems))
    return starts, landing


HBM = pl.BlockSpec(memory_space=pltpu.HBM)
SEM = pl.BlockSpec(memory_space=pltpu.SEMAPHORE)


def _ici_start(copies_fn, srcs, land_shapes, *, name, after=()):
    n, na = len(srcs), len(after)

    def body(*refs):
        starts, _ = copies_fn(refs[:n], refs[n:2 * n], refs[2 * n + na], refs[2 * n + na + 1])
        for cp in starts:
            cp.start()
        refs[-1][...] = jnp.zeros_like(refs[-1])

    sems = pltpu.SemaphoreType.DMA((3 * n,))
    hbm = lambda s: pltpu.HBM(s.shape, s.dtype)
    lands = [pltpu.with_memory_space_constraint(lax.empty(s.shape, s.dtype), pltpu.HBM) for s in land_shapes]
    res = pl.pallas_call(
        body, name=name, in_specs=[HBM] * (2 * n) + [ANY] * na,
        out_specs=(SEM, SEM, *[HBM] * (2 * n), pl.BlockSpec(memory_space=pltpu.VMEM)),
        out_shape=(sems, sems, *[hbm(s) for s in srcs], *[hbm(s) for s in land_shapes],
                   jax.ShapeDtypeStruct((8, BLOCK), F32)),
        input_output_aliases={i: 2 + i for i in range(2 * n)},
        compiler_params=pltpu.CompilerParams(has_side_effects=pltpu.SideEffectType.DATAFLOW_SIDE_EFFECTING),
    )(*[pltpu.with_memory_space_constraint(s, pltpu.HBM) for s in srcs], *lands, *after)
    return res[0], res[1], list(res[2:2 + n]), list(res[2 + n:2 + 2 * n]), res[-1]


def _ici_wait(copies_fn, send_sems, recv_sems, srcs, lands, after, *, name):
    n = len(srcs)

    def body(*refs):
        starts, landing = copies_fn(refs[:n], refs[n:2 * n], refs[2 * n], refs[2 * n + 1])
        for cp in starts:
            cp.wait_send()
        for cp in landing:
            cp.wait_recv()

    hbm = lambda s: pltpu.HBM(s.shape, s.dtype)
    res = pl.pallas_call(
        body, name=name, in_specs=[*[HBM] * (2 * n), SEM, SEM, ANY], out_specs=[HBM] * (2 * n),
        out_shape=[*[hbm(s) for s in srcs], *[hbm(s) for s in lands]],
        input_output_aliases={i: i for i in range(2 * n)},
        compiler_params=pltpu.CompilerParams(has_side_effects=pltpu.SideEffectType.DATAFLOW_SIDE_EFFECTING),
    )(*srcs, *lands, send_sems, recv_sems, after)
    return list(res[:n]), list(res[n:])


def _gather_d2d(shards, lands, tag):
    n = len(shards)

    def body(*refs):
        ins, outs = refs[:n], refs[2 * n:3 * n]
        send_sems, recv_sems = refs[3 * n], refs[3 * n + 1]
        x, y, c = _my_pos()
        me, sibling = 2 * x + y, (x, y, 1 - c)
        starts, landing = [], []
        for a in range(n):
            half = ins[a].shape[0] // 2
            mine, theirs = pl.ds(c * half, half), pl.ds((1 - c) * half, half)
            pairs = [(ins[a], outs[a].at[me], outs[a].at[me])]
            for chip in _other_chips(x, y):
                k = 2 * chip[0] + chip[1]
                pairs.append((outs[a].at[k, mine], outs[a].at[k, mine], outs[a].at[k, theirs]))
            for j, (src, dst, lands_here) in enumerate(pairs):
                sems = dict(send_sem=send_sems.at[4 * a + j], recv_sem=recv_sems.at[4 * a + j], device_id_type=MESH)
                starts.append(pltpu.make_async_remote_copy(src_ref=src, dst_ref=dst, device_id=sibling, **sems))
                landing.append(pltpu.make_async_remote_copy(src_ref=lands_here, dst_ref=lands_here, device_id=(x, y, c),
                                                            **sems))
        for cp in starts:
            cp.start()
        for cp in landing:
            cp.wait_recv()
        for cp in starts:
            cp.wait_send()

    return pl.pallas_call(
        body, name=f"gather_d2d_{tag}", in_specs=[ANY] * (2 * n), out_specs=[ANY] * n,
        out_shape=[jax.ShapeDtypeStruct(s.shape, s.dtype) for s in lands],
        input_output_aliases={n + a: a for a in range(n)},
        scratch_shapes=[pltpu.SemaphoreType.DMA((4 * n,)), pltpu.SemaphoreType.DMA((4 * n,))],
    )(*shards, *lands)


def _rs_add_chips(p32s, r16s, pos, *, name):
    n = len(p32s)
    _, H, C = p32s[0].shape
    tile = _rs_tile(H, C, n)
    nt = H // tile

    def body(pos_ref, *refs):
        for a in range(n):
            p_ref, r_ref = refs[a], refs[n + a]
            refs[2 * n + a][...] = ((p_ref[...] + r_ref[0].astype(F32)) + r_ref[1].astype(F32)) + r_ref[2].astype(F32)

    grid_spec = pltpu.PrefetchScalarGridSpec(
        num_scalar_prefetch=1, grid=(nt,),
        in_specs=[_bs((None, tile, C), lambda i, pos_ref: (pos_ref[0], i, 0))] * n
        + [_bs((3, tile, C), lambda i, pos_ref: (0, i, 0))] * n,
        out_specs=[_bs((tile, C), lambda i, pos_ref: (pos_ref[1] * nt + i, 0))] * n)
    return pl.pallas_call(
        body, name=name, grid_spec=grid_spec, out_shape=[jax.ShapeDtypeStruct((2 * H, C), F32)] * n,
        compiler_params=_cparams(("parallel",)),
    )(pos, *p32s, *r16s)


def _rs_join_rows(fs, tag):
    n = len(fs)

    def body(*refs):
        outs = refs[n:2 * n]
        send_sems, recv_sems = refs[2 * n], refs[2 * n + 1]
        x, y, c = _my_pos()
        for a in range(n):
            half = outs[a].shape[0] // 2
            mine = outs[a].at[pl.ds(c * half, half)]
            pltpu.make_async_remote_copy(src_ref=mine, dst_ref=mine, send_sem=send_sems.at[a],
                                         recv_sem=recv_sems.at[a], device_id=(x, y, 1 - c), device_id_type=MESH).start()
        for a in range(n):
            half = outs[a].shape[0] // 2
            pltpu.make_async_remote_copy(
                src_ref=outs[a].at[pl.ds(c * half, half)], dst_ref=outs[a].at[pl.ds((1 - c) * half, half)],
                send_sem=send_sems.at[a], recv_sem=recv_sems.at[a], device_id=(x, y, 1 - c), device_id_type=MESH).wait()

    return pl.pallas_call(
        body, name=f"rs_join_rows_{tag}", in_specs=[ANY] * n, out_specs=[ANY] * n,
        out_shape=[jax.ShapeDtypeStruct(f.shape, f.dtype) for f in fs],
        input_output_aliases={a: a for a in range(n)},
        scratch_shapes=[pltpu.SemaphoreType.DMA((n,)), pltpu.SemaphoreType.DMA((n,))],
    )(*fs)


def _pos_vector():
    x, y, c = _my_pos()
    return jnp.stack([2 * x + y, c]).astype(jnp.int32)


def _swap_copies(srcs, lands, send_sems, recv_sems):
    x, y, c = _my_pos()
    starts, landing = [], []
    for a in range(len(srcs)):
        half = srcs[a].shape[1] // 2
        sems = dict(send_sem=send_sems.at[3 * a], recv_sem=recv_sems.at[3 * a], device_id_type=MESH)
        starts.append(pltpu.make_async_remote_copy(
            src_ref=srcs[a].at[:, pl.ds((1 - c) * half, half)], dst_ref=lands[a], device_id=(x, y, 1 - c), **sems))
        landing.append(pltpu.make_async_remote_copy(src_ref=lands[a], dst_ref=lands[a], device_id=(x, y, c), **sems))
    return starts, landing


def _swap_land_shapes(gs):
    return [jax.ShapeDtypeStruct((N_CHIPS, g.shape[1] // 2, g.shape[2]), g.dtype) for g in gs]


def _same_shape_runs(arrays):
    runs, start = [], 0
    for i in range(1, len(arrays) + 1):
        if i == len(arrays) or arrays[i].shape != arrays[start].shape:
            runs.append((start, i))
            start = i
    return runs


def _rs_add_pairs(gs, r1, names, tag):
    pos = _pos_vector()
    out = []
    for a, b in _same_shape_runs(gs):
        out += _rs_add_pair(gs[a:b], r1[a:b], pos, name=f"rs_add_pair_{tag}_{names[a]}")
    return out


def _rs_pair_sums(gs, names, tag):
    return _rs_add_pairs(gs, _rs_swap_rows(gs, tag), names, tag)


def _rs_finish(pairs, r2, names, tag):
    pos = _pos_vector()
    p32s = [p[0] for p in pairs]
    fs = []
    for a, b in _same_shape_runs(p32s):
        fs += _rs_add_chips(p32s[a:b], r2[a:b], pos, name=f"rs_add_chips_{tag}_{names[a]}")
    return _rs_join_rows(fs, tag)


def _exchange_land_shapes(pairs):
    return [jax.ShapeDtypeStruct((3,) + p[1].shape[1:], p[1].dtype) for p in pairs]


def _allreduce_small(buf):
    R, W = buf.shape

    def body(b_ref, o_ref, gather, send_sems, recv_sems):
        x, y, c = _my_pos()
        me = 4 * x + 2 * y + c
        gather[me] = b_ref[...]
        cps = []
        for d in range(1, 8):
            peer = (x ^ (d >> 2), y ^ ((d >> 1) & 1), c ^ (d & 1))
            cps.append(pltpu.make_async_remote_copy(
                src_ref=b_ref, dst_ref=gather.at[me], send_sem=send_sems.at[d - 1], recv_sem=recv_sems.at[d - 1],
                device_id=peer, device_id_type=MESH))
        for cp in cps:
            cp.start()
        for d in range(1, 8):
            pltpu.make_async_remote_copy(
                src_ref=b_ref, dst_ref=gather.at[me ^ d], send_sem=send_sems.at[d - 1], recv_sem=recv_sems.at[d - 1],
                device_id=(x, y, c), device_id_type=MESH).wait_recv()
        for cp in cps:
            cp.wait_send()
        acc = gather[0]
        for d in range(1, 8):
            acc = acc + gather[d]
        o_ref[...] = acc

    vm = pl.BlockSpec(memory_space=pltpu.VMEM)
    return pl.pallas_call(
        body, name="allreduce_small", in_specs=[vm], out_specs=vm, out_shape=jax.ShapeDtypeStruct((R, W), F32),
        scratch_shapes=[pltpu.VMEM((8, R, W), F32), pltpu.SemaphoreType.DMA((7,)), pltpu.SemaphoreType.DMA((7,))],
    )(buf)


def _heads(a, h, d):
    return a.reshape(a.shape[0], h, d).transpose(1, 0, 2)


def _unheads(a):
    h, L, d = a.shape
    return a.transpose(1, 0, 2).reshape(L, h * d)


def _rope_tables():
    pos = jnp.maximum(jnp.arange(LP, dtype=F32) - PAD_ROWS, 0.0)
    inv_freq = 1.0 / (ROPE_THETA ** (jnp.arange(0, MLA_ROPE, 2, dtype=F32) / MLA_ROPE))
    ang = pos[:, None] * inv_freq[None, :]
    cos, sin = jnp.tile(jnp.cos(ang), (1, MLA_HEADS)), jnp.tile(jnp.sin(ang), (1, MLA_HEADS))
    return jnp.concatenate([cos, cos], axis=1), jnp.concatenate([-sin, sin], axis=1)


def _lane_pad(a, width=BLOCK):
    return jnp.pad(a, ((0, 0), (0, width - a.shape[1])))


def _pad_in_proj(w):
    sl = lambda start, size: w[:, start:start + size]
    return jnp.concatenate([
        sl(OC_Z, 512), sl(OC_XBC, 768), sl(OC_FQ, 256), sl(OC_FK, 256), sl(OC_FV, 256), sl(OC_CQ, 256), sl(OC_CKV, 128),
        _lane_pad(sl(OC_DT, SSD_HEADS)), _lane_pad(sl(OC_FR, FOX_HEADS)),
        jnp.tile(sl(OC_KR, ROPE_HALF), (1, MLA_HEADS)), jnp.tile(sl(OC_KR + ROPE_HALF, ROPE_HALF), (1, MLA_HEADS))], axis=1)


def _in_proj_grad_chunks(wp):
    rope = lambda start: wp[:, start:start + 64].reshape(wp.shape[0], MLA_HEADS, ROPE_HALF).sum(axis=1)
    segs = [(wp, PC_Z, 512), (wp, PC_XBC, 768), (wp, PC_DT, SSD_HEADS), (wp, PC_FQ, 256), (wp, PC_FK, 256),
            (wp, PC_FV, 256), (wp, PC_FR, FOX_HEADS), (wp, PC_CQ, 256), (wp, PC_CKV, 128),
            (rope(PC_KR), 0, ROPE_HALF), (rope(PC_KR + 64), 0, ROPE_HALF)]
    chunks = []
    for k in range(N_CHIPS):
        lo, hi, pos, pieces = k * IN_SHARD, (k + 1) * IN_SHARD, 0, []
        for arr, start, size in segs:
            a, b = max(lo, pos), min(hi, pos + size)
            if a < b:
                pieces.append(arr[:, start + a - pos:start + b - pos])
            pos += size
        pieces.append(jnp.zeros((wp.shape[0], IN_SHARD_P - IN_SHARD), wp.dtype))
        chunks.append(jnp.concatenate(pieces, axis=1))
    return jnp.stack(chunks)


def _regroup_uq(w):
    w3 = w.reshape(w.shape[0], MLA_HEADS, MLA_NOPE + MLA_ROPE)
    return jnp.concatenate([w3[:, :, :MLA_NOPE].reshape(w.shape[0], -1),
                            w3[:, :, MLA_NOPE:MLA_NOPE + ROPE_HALF].reshape(w.shape[0], -1),
                            w3[:, :, MLA_NOPE + ROPE_HALF:].reshape(w.shape[0], -1)], axis=1)


def _ungroup_uq(wp):
    n = wp.shape[0]
    return jnp.concatenate([wp[:, :256].reshape(n, MLA_HEADS, MLA_NOPE), wp[:, 256:320].reshape(n, MLA_HEADS, ROPE_HALF),
                            wp[:, 320:].reshape(n, MLA_HEADS, ROPE_HALF)], axis=2).reshape(n, -1)


def _regroup_ukv(w):
    w3 = w.reshape(w.shape[0], MLA_HEADS, MLA_NOPE + MLA_V)
    return jnp.concatenate([w3[:, :, :MLA_NOPE].reshape(w.shape[0], -1), w3[:, :, MLA_NOPE:].reshape(w.shape[0], -1)],
                           axis=1)


def _ungroup_ukv(wp):
    n = wp.shape[0]
    return jnp.concatenate([wp[:, :256].reshape(n, MLA_HEADS, MLA_NOPE), wp[:, 256:].reshape(n, MLA_HEADS, MLA_V)],
                           axis=2).reshape(n, -1)


TMF = 1088
N_IF = LP // TMF


def _chunk_rows_dx(g, w, l, chunk_h, *, name):
    N = w.shape[2]
    return _mm_core(g, w, a_spec=_bs((TMF, N), lambda i, j, k: (i, 0)),
                    b_spec=_bs((None, chunk_h, N), lambda i, j, k: (j, 0, 0)),
                    o_spec=_bs((TMF, chunk_h), lambda i, j, k: (i, j)), grid=(N_IF, N_CHIPS, 1),
                    out_shape=(LP, N_CHIPS * chunk_h), ca=1, cb=1, name=name)


def _chunk_rows_dw(a, g, chunk_h, *, name):
    N = g.shape[1]
    return _mm_core(a, g, a_spec=_bs((LP, chunk_h), lambda i, j, k: (0, i)), b_spec=_bs((LP, N), lambda i, j, k: (0, 0)),
                    o_spec=_bs((None, chunk_h, N), lambda i, j, k: (i, 0, 0)), grid=(N_CHIPS, 1, 1),
                    out_shape=(N_CHIPS, chunk_h, N), ca=0, cb=0, name=name)


def _ffn_up_swiglu(h, wg, wu, *, name):
    def body(h_ref, wg_ref, wu_ref, g_ref, u_ref, a_ref):
        hb = h_ref[...].astype(BF16)
        g = _raw_bdot(hb, wg_ref[...], 1, 1)
        u = _raw_bdot(hb, wu_ref[...], 1, 1)
        g_ref[...] = g
        u_ref[...] = u
        a_ref[...] = (_silu(g) * u).astype(a_ref.dtype)

    w_spec = _bs((None, HP, D_MODEL), lambda i, j: (j, 0, 0))
    o_spec = _bs((TMF, HP), lambda i, j: (i, j))
    return pl.pallas_call(
        body, name=name, grid=(N_IF, N_CHIPS), in_specs=[_bs((TMF, D_MODEL), lambda i, j: (i, 0)), w_spec, w_spec],
        out_specs=[o_spec] * 3,
        out_shape=[jax.ShapeDtypeStruct((LP, FP), F32), jax.ShapeDtypeStruct((LP, FP), F32),
                   jax.ShapeDtypeStruct((LP, FP), BF16)],
        compiler_params=_cparams(("parallel", "parallel")),
    )(h, wg, wu)


def _ffn_down_dx_swiglu(do, wd, g, u, *, name):
    def body(do_ref, wd_ref, g_ref, u_ref, dg_ref, du_ref):
        dact = _raw_bdot(do_ref[...], wd_ref[...], 1, 1)
        gv = g_ref[...]
        sig = _sigmoid(gv)
        dg_ref[...] = (dact * u_ref[...] * (sig * (1.0 + gv * (1.0 - sig)))).astype(dg_ref.dtype)
        du_ref[...] = (dact * (gv * sig)).astype(du_ref.dtype)

    blk = _bs((TMF, HP), lambda i, j: (i, j))
    return pl.pallas_call(
        body, name=name, grid=(N_IF, N_CHIPS),
        in_specs=[_bs((TMF, D_MODEL), lambda i, j: (i, 0)), _bs((None, HP, D_MODEL), lambda i, j: (j, 0, 0)), blk, blk],
        out_specs=[blk, blk], out_shape=[jax.ShapeDtypeStruct((LP, FP), BF16)] * 2,
        compiler_params=_cparams(("parallel", "parallel")),
    )(do, wd, g, u)


def _ffn_gate_up_dw(dg, du, h, *, name):
    def body(dg_ref, du_ref, h_ref, wg_ref, wu_ref):
        hb = h_ref[...].astype(BF16)
        wg_ref[...] = _raw_bdot(dg_ref[...], hb, 0, 0)
        wu_ref[...] = _raw_bdot(du_ref[...], hb, 0, 0)

    a_spec = _bs((LP, HP), lambda k: (0, k))
    o_spec = _bs((None, HP, D_MODEL), lambda k: (k, 0, 0))
    return pl.pallas_call(
        body, name=name, grid=(N_CHIPS,), in_specs=[a_spec, a_spec, _bs((LP, D_MODEL), lambda k: (0, 0))],
        out_specs=[o_spec, o_spec], out_shape=[jax.ShapeDtypeStruct((N_CHIPS, HP, D_MODEL), F32)] * 2,
        compiler_params=_cparams(("parallel",)),
    )(dg, du, h)


def _ffn_gate_up_dx(dg, du, wg, wu, add, *, name):
    def body(dg_ref, du_ref, wg_ref, wu_ref, add_ref, o_ref, acc_ref):
        k = pl.program_id(1)

        @pl.when(k == 0)
        def _():
            acc_ref[...] = jnp.zeros_like(acc_ref)

        acc_ref[...] += _raw_bdot(dg_ref[...], wg_ref[...], 1, 0) + _raw_bdot(du_ref[...], wu_ref[...], 1, 0)

        @pl.when(k == N_CHIPS - 1)
        def _():
            o_ref[...] = acc_ref[...] + add_ref[...]

    a_spec = _bs((TMF, HP), lambda i, k: (i, k))
    w_spec = _bs((None, HP, D_MODEL), lambda i, k: (k, 0, 0))
    o_spec = _bs((TMF, D_MODEL), lambda i, k: (i, 0))
    return pl.pallas_call(
        body, name=name, grid=(N_IF, N_CHIPS), in_specs=[a_spec, a_spec, w_spec, w_spec, o_spec], out_specs=o_spec,
        out_shape=jax.ShapeDtypeStruct((LP, D_MODEL), F32), scratch_shapes=[pltpu.VMEM((TMF, D_MODEL), F32)],
        compiler_params=_cparams(("parallel", "arbitrary")),
    )(dg, du, wg, wu, add)


def _chunk_rows_mm_res_ln(a, w, chunk_h, h, gam, bet, scale, *, name):
    res_ln = _make_res_ln_fn(scale)

    def body(a_ref, w_ref, h_ref, g_ref, b_ref, o_ref, y_ref, yb_ref, acc_ref):
        k = pl.program_id(1)

        @pl.when(k == 0)
        def _():
            acc_ref[...] = jnp.zeros_like(acc_ref)

        acc_ref[...] += _raw_bdot(a_ref[...], w_ref[...], 1, 0)

        @pl.when(k == N_CHIPS - 1)
        def _():
            o = acc_ref[...]
            o_ref[...] = o
            (y,) = res_ln(0, h_ref[...], o, g_ref[...], b_ref[...])
            y_ref[...] = y
            yb_ref[...] = y.astype(yb_ref.dtype)

    row = _bs((TMF, D_MODEL), lambda i, k: (i, 0))
    par = _bs((1, D_MODEL), lambda i, k: (0, 0))
    return pl.pallas_call(
        body, name=name, grid=(N_IF, N_CHIPS),
        in_specs=[_bs((TMF, chunk_h), lambda i, k: (i, k)), _bs((None, chunk_h, D_MODEL), lambda i, k: (k, 0, 0)), row,
                  par, par],
        out_specs=[row, row, row],
        out_shape=[jax.ShapeDtypeStruct((LP, D_MODEL), F32)] * 2 + [jax.ShapeDtypeStruct((LP, D_MODEL), BF16)],
        scratch_shapes=[pltpu.VMEM((TMF, D_MODEL), F32)], compiler_params=_cparams(("parallel", "arbitrary")),
    )(a, w, h, gam, bet)


def _ffn_fwd(hp, W, pre, l, gam, bet, tag):
    h, hb = hp
    g, u, act = _ffn_up_swiglu(hb, W[pre + "_w_gate"][l], W[pre + "_w_up"][l], name=f"{tag}_up_swiglu")
    o, out, outb = _chunk_rows_mm_res_ln(act, W[pre + "_w_down"][l], HP, h, gam, bet, 0.5, name=f"{tag}_down_ln")
    return (out, outb), (h, hb, g, u, act, o)


def _ffn_bwd(dout, saved, W, pre, l, gam, bet, GB, tag):
    h, hb, g, u, act, o = saved
    (dh_a, do), (dgam, dbet) = _rowwise_bwd(_make_res_ln_fn(0.5), [h, o], [gam, bet], [dout], name=f"{tag}_ln_bwd",
                                            tile=272, grad_dtypes=[F32, BF16])
    dg, du = _ffn_down_dx_swiglu(do, W[pre + "_w_down"][l], g, u, name=f"{tag}_down_dx_swiglu")
    GB[pre + "_w_down"] = _chunk_rows_dw(act, do, HP, name=f"{tag}_down_dw")
    GB[pre + "_w_gate"], GB[pre + "_w_up"] = _ffn_gate_up_dw(dg, du, hb, name=f"{tag}_gate_up_dw")
    dh = _ffn_gate_up_dx(dg, du, W[pre + "_w_gate"][l], W[pre + "_w_up"][l], dh_a, name=f"{tag}_gate_up_dx")
    return dh, dgam, dbet


def _mixer_fwd(hp1, W, l, cosf, sins, after=()):
    h1, h1b = hp1
    tag = f"l{l}"
    proj = _mm(h1b, W["w_in_p"][l], name=f"{tag}_in_proj", after=after)
    sv = {"h1": h1, "h1b": h1b, "proj": proj}
    conv_w, conv_b = W["conv_w"][l], W["conv_b"][l][None]
    xc = _conv_fwd(proj, PC_XBC // BLOCK, conv_w, conv_b, name=f"{tag}_conv")
    dt_bias = _lane_pad(W["dt_bias"][l][None])
    dtc, dtr = _ssd_dt_fwd(proj, PC_DT // BLOCK, dt_bias, name=f"{tag}_ssd_dt")
    xh = _heads(xc[:, :SSD_D], SSD_HEADS, SSD_HD)
    bm = _heads(xc[:, SSD_D:SSD_D + 128], SSD_GROUPS, SSD_STATE)
    cm = _heads(xc[:, SSD_D + 128:], SSD_GROUPS, SSD_STATE)
    alog = jnp.broadcast_to(W["a_log"][l][:, None, None], (SSD_HEADS, 1, BLOCK))
    yh, prevs = _ssd_fwd(xh, bm, cm, dtc, dtr, alog, name=f"{tag}_ssd")
    y_raw = _unheads(yh)
    dskip = jnp.repeat(W["d_skip"][l], SSD_HD)[None]
    normg = W["ssd_norm_g"][l][None]
    post_rows = [y_raw, (xc, 256, 0), (proj, 256, PC_Z // 256)]
    (y_ssd,) = _rowwise(_ssd_post_fn, post_rows, [dskip, normg], [SSD_D], name=f"{tag}_ssd_post", tile=272,
                        ncol=SSD_GROUPS)
    sv.update(conv_w=conv_w, conv_b=conv_b, dt_bias=dt_bias, xh=xh, bm=bm, cm=cm, dtc=dtc, dtr=dtr, alog=alog,
              prevs=prevs, post_rows=post_rows, dskip=dskip, normg=normg)
    f_b = _lane_pad(W["fox_f_b"][l][None])
    cg, cgt = _fox_gate_fwd(proj, PC_FR // BLOCK, f_b, name=f"{tag}_fox_gate")
    fox_qkv = ((proj, PC_FQ // ATT_W), (proj, PC_FK // ATT_W), (proj, PC_FV // ATT_W))
    y_fox, lse_f = _attn_fwd(*fox_qkv, scale=FOX_HD ** -0.5, name=f"{tag}_fox_attn", bias=(cg, cgt))
    sv.update(f_b=f_b, cg=cg, cgt=cgt, fox_qkv=fox_qkv, y_fox=y_fox, lse_f=lse_f)
    gq, gkv = W["mla_q_norm_g"][l][None], W["mla_kv_norm_g"][l][None]
    norm_rows = [(proj, 256, PC_CQ // 256), (proj, BLOCK, PC_CKV // BLOCK)]
    qn, cn = _rowwise(_mla_norm_fn, norm_rows, [gq, gkv], [MLA_Q_LORA, MLA_KV_LORA], name=f"{tag}_mla_norm", tile=272,
                      out_dtypes=[BF16, BF16])
    qh = _mm(qn, W["mla_w_uq_p"][l], name=f"{tag}_mla_uq")
    kvh = _mm(cn, W["mla_w_ukv_p"][l], name=f"{tag}_mla_ukv")
    qr, kr = _rowwise(_rope_fn, [(qh, BLOCK, 2), (proj, BLOCK, PC_KR // BLOCK), cosf, sins], [], [BLOCK, BLOCK],
                      name=f"{tag}_rope", tile=272)
    mla_qkv = ((qh, 0), (kvh, 0), (kvh, 1))
    y_mla, lse_m = _attn_fwd(*mla_qkv, scale=(MLA_NOPE + MLA_ROPE) ** -0.5, name=f"{tag}_mla_attn",
                             rope=((qr, 0), (kr, 0)))
    sv.update(gq=gq, gkv=gkv, norm_rows=norm_rows, qn=qn, cn=cn, qr=qr, kr=kr, mla_qkv=mla_qkv, y_mla=y_mla, lse_m=lse_m)
    ycat = jnp.concatenate([y_ssd, y_fox, y_mla], axis=1).astype(BF16)
    mix, h2, h2b = _chunk_rows_mm_res_ln(ycat, W["w_out"][l], 256, h1, W["ln2_g"][l][None], W["ln2_b"][l][None], 1.0,
                                    name=f"{tag}_out_proj_ln2")
    sv.update(mix=mix, ycat=ycat)
    return (h2, h2b), sv


def _mixer_bwd(dh2, sv, W, l, cosf, sins, GB, zero=0.0):
    tag = f"l{l}"
    G = {}
    proj = sv["proj"]
    ln2g, ln2b = W["ln2_g"][l][None] + zero, W["ln2_b"][l][None]
    (dh1_a, dmix), (dln2g, dln2b) = _rowwise_bwd(
        _make_res_ln_fn(1.0), [sv["h1"], sv["mix"]], [ln2g, ln2b], [dh2], name=f"{tag}_ln2_bwd", tile=272,
        grad_dtypes=[F32, BF16])
    G["ln2_g"], G["ln2_b"] = dln2g[0], dln2b[0]
    dycat = _chunk_rows_dx(dmix, W["w_out"][l], l, 256, name=f"{tag}_out_proj_dx")
    GB["w_out"] = _chunk_rows_dw(sv["ycat"], dmix, 256, name=f"{tag}_out_proj_dw")
    (dy_raw, dxs_a, dz), (ddskip, dnormg) = _rowwise_bwd(
        _ssd_post_fn, sv["post_rows"], [sv["dskip"], sv["normg"]], [dycat[:, :SSD_D]],
        name=f"{tag}_ssd_post_bwd", tile=272, ncol=SSD_GROUPS)
    G["ssd_norm_g"] = dnormg[0]
    G["d_skip"] = ddskip.reshape(SSD_HEADS, SSD_HD).sum(axis=1)
    dxh, dbm, dcm, ddtc, ddtr, dal = _ssd_bwd(sv["xh"], sv["bm"], sv["cm"], sv["dtc"], sv["dtr"], sv["alog"],
                                              sv["prevs"], _heads(dy_raw, SSD_HEADS, SSD_HD), name=f"{tag}_ssd_bwd")
    G["a_log"] = dal[:, 0, 0]
    dxc = jnp.concatenate([dxs_a + _unheads(dxh), _unheads(dbm), _unheads(dcm)], axis=1)
    dxbc, G["conv_w"], dconv_b = _conv_bwd(proj, PC_XBC // BLOCK, sv["conv_w"], sv["conv_b"], dxc,
                                           name=f"{tag}_conv_bwd")
    G["conv_b"] = dconv_b[0]
    ddt_raw, ddt_bias = _ssd_dt_bwd(proj, PC_DT // BLOCK, sv["dt_bias"], ddtc, ddtr, name=f"{tag}_ssd_dt_bwd")
    G["dt_bias"] = ddt_bias[0, :SSD_HEADS]
    dfq, dfk, dfv, dcg, dcgt = _attn_bwd(*sv["fox_qkv"], sv["y_fox"], sv["lse_f"], (dycat, SSD_D // ATT_W),
                                         scale=FOX_HD ** -0.5, name=f"{tag}_fox_attn_bwd", bias=(sv["cg"], sv["cgt"]))
    df_raw, dfb = _fox_gate_bwd(proj, PC_FR // BLOCK, sv["f_b"], dcg, dcgt, name=f"{tag}_fox_gate_bwd")
    G["fox_f_b"] = dfb[0, :FOX_HEADS]
    dqn_h, dkn_h, dv_h, dqr, dkr = _attn_bwd(
        *sv["mla_qkv"], sv["y_mla"], sv["lse_m"], (dycat, (SSD_D + FOX_D) // ATT_W),
        scale=(MLA_NOPE + MLA_ROPE) ** -0.5, name=f"{tag}_mla_attn_bwd", rope=((sv["qr"], 0), (sv["kr"], 0)))
    dq_rope, dk_rope = _rowwise(_rope_t_fn, [dqr, dkr, cosf, sins], [], [BLOCK, BLOCK], name=f"{tag}_rope_bwd",
                                tile=272)
    dqh = jnp.concatenate([dqn_h, dq_rope], axis=1).astype(BF16)
    dkvh = jnp.concatenate([dkn_h, dv_h], axis=1).astype(BF16)
    dqn = _mm(dqh, W["mla_w_uq_p"][l], tb=True, name=f"{tag}_mla_uq_dx")
    G["mla_w_uq_p"] = _mm(sv["qn"], dqh, ta=True, name=f"{tag}_mla_uq_dw")
    dcn = _mm(dkvh, W["mla_w_ukv_p"][l], tb=True, name=f"{tag}_mla_ukv_dx")
    G["mla_w_ukv_p"] = _mm(sv["cn"], dkvh, ta=True, name=f"{tag}_mla_ukv_dw")
    (dcq, dckv), (dgq, dgkv) = _rowwise_bwd(_mla_norm_fn, sv["norm_rows"], [sv["gq"], sv["gkv"]], [dqn, dcn],
                                            name=f"{tag}_mla_norm_bwd", tile=272)
    G["mla_q_norm_g"], G["mla_kv_norm_g"] = dgq[0], dgkv[0]
    dproj = jnp.concatenate([dz, dxbc, dfq, dfk, dfv, dcq, dckv, ddt_raw, df_raw, dk_rope], axis=1).astype(BF16)
    dh1 = _mm(dproj, W["w_in_p"][l], tb=True, add=dh1_a, name=f"{tag}_in_proj_dx")
    G["w_in_p"] = _mm(sv["h1b"], dproj, ta=True, name=f"{tag}_in_proj_dw")
    return dh1, G


def _embed(x, meta):
    return jnp.concatenate([jnp.zeros((PAD_ROWS, D_MODEL), F32), meta, x], axis=0)


def _layer_fwd(h, W, l, cosf, sins):
    ln = lambda n: W[n][l][None]
    h1, s1 = _ffn_fwd(h, W, "ffn1", l, ln("ln1_g"), ln("ln1_b"), f"l{l}_ffn1")
    h2, sm = _mixer_fwd(h1, W, l, cosf, sins)
    h3, s2 = _ffn_fwd(h2, W, "ffn2", l, ln("ln3_g"), ln("ln3_b"), f"l{l}_ffn2")
    return h3, (s1, sm, s2)


def _layer_bwd(dh, saved, W, l, cosf, sins):
    ln = lambda n: W[n][l][None]
    s1, sm, s2 = saved
    G = {}
    dh, dg, db = _ffn_bwd(dh, s2, W, "ffn2", l, ln("ln3_g"), ln("ln3_b"), G, f"l{l}_ffn2")
    G["ln3_g"], G["ln3_b"] = dg[0], db[0]
    dh, Gm = _mixer_bwd(dh, sm, W, l, cosf, sins, G)
    G.update(Gm)
    dh, dg, db = _ffn_bwd(dh, s1, W, "ffn1", l, ln("ln1_g"), ln("ln1_b"), G, f"l{l}_ffn1")
    G["ln1_g"], G["ln1_b"] = dg[0], db[0]
    return dh, G


def _local_step(x, target, W):
    h = _embed(x, W["meta"])
    h = (h, h.astype(BF16))
    tgt = jnp.concatenate([jnp.zeros((BLOCK, D_MODEL), F32), target], axis=0)
    cosf, sins = _rope_tables()
    saved = []
    for l in range(DEPTH):
        h, sv = _layer_fwd(h, W, l, cosf, sins)
        saved.append(sv)
    dh, loss = _loss_head(h[0], tgt, name="loss_head")
    grads = [None] * DEPTH
    for l in reversed(range(DEPTH)):
        dh, grads[l] = _layer_bwd(dh, saved[l], W, l, cosf, sins)
    return loss, dh, grads


WEIGHTS = ['meta', 'ffn1_w_gate', 'ffn1_w_up', 'ffn1_w_down', 'ln1_g', 'ln1_b', 'w_in', 'conv_w', 'conv_b', 'dt_bias',
           'a_log', 'd_skip', 'ssd_norm_g', 'fox_f_b', 'mla_q_norm_g', 'mla_w_uq', 'mla_kv_norm_g', 'mla_w_ukv',
           'w_out', 'ln2_g', 'ln2_b', 'ffn2_w_gate', 'ffn2_w_up', 'ffn2_w_down', 'ln3_g', 'ln3_b']
SMALL = ["ln1_g", "ln1_b", "conv_b", "dt_bias", "a_log", "d_skip", "ssd_norm_g", "fox_f_b", "mla_q_norm_g",
         "mla_kv_norm_g", "ln2_g", "ln2_b", "ln3_g", "ln3_b"]
MATMUL_W = ["ffn1_w_gate", "ffn1_w_up", "ffn1_w_down", "w_in", "mla_w_uq", "mla_w_ukv", "w_out", "ffn2_w_gate",
            "ffn2_w_up", "ffn2_w_down"]
SMALL_ROWS = 312


def _pad_to(a, axis, size):
    pads = [(0, 0)] * a.ndim
    pads[axis] = (0, size - a.shape[axis])
    return jnp.pad(a, pads)


def _chip_cols(full, chip, width):
    return lax.dynamic_slice_in_dim(full, chip * width, width, axis=full.ndim - 1)


def kernel(x, meta, ffn1_w_gate, ffn1_w_up, ffn1_w_down, ln1_g, ln1_b, w_in, conv_w, conv_b, dt_bias, a_log, d_skip, ssd_norm_g, fox_f_b, mla_q_norm_g, mla_w_uq, mla_kv_norm_g, mla_w_ukv, w_out, ln2_g, ln2_b, ffn2_w_gate, ffn2_w_up, ffn2_w_down, ln3_g, ln3_b, loss_target, m_meta, m_ffn1_w_gate, m_ffn1_w_up, m_ffn1_w_down, m_ln1_g, m_ln1_b, m_w_in, m_conv_w, m_conv_b, m_dt_bias, m_a_log, m_d_skip, m_ssd_norm_g, m_fox_f_b, m_mla_q_norm_g, m_mla_w_uq, m_mla_kv_norm_g, m_mla_w_ukv, m_w_out, m_ln2_g, m_ln2_b, m_ffn2_w_gate, m_ffn2_w_up, m_ffn2_w_down, m_ln3_g, m_ln3_b, v_meta, v_ffn1_w_gate, v_ffn1_w_up, v_ffn1_w_down, v_ln1_g, v_ln1_b, v_w_in, v_conv_w, v_conv_b, v_dt_bias, v_a_log, v_d_skip, v_ssd_norm_g, v_fox_f_b, v_mla_q_norm_g, v_mla_w_uq, v_mla_kv_norm_g, v_mla_w_ukv, v_w_out, v_ln2_g, v_ln2_b, v_ffn2_w_gate, v_ffn2_w_up, v_ffn2_w_down, v_ln3_g, v_ln3_b):
    args = dict(locals())
    w = {n: args[n] for n in WEIGHTS}
    m = {n: args["m_" + n] for n in WEIGHTS}
    v = {n: args["v_" + n] for n in WEIGHTS}
    xcoord, ycoord, _ = _my_pos()
    chip = 2 * xcoord + ycoord

    tr = lambda a: jnp.swapaxes(a, 1, 2)

    def bf16_shard(n, l, zero=None):
        a = w[n] if zero is None else w[n] + zero
        if n.endswith("w_gate") or n.endswith("w_up"):
            a = _pad_to(tr(a), 1, HP)
        elif n.endswith("w_down"):
            a = _pad_to(a, 1, HP)
        elif n == "w_in":
            a = _pad_to(a, 2, IN_SHARD_P)
        return a[l].astype(BF16)

    land_shape = lambda s: jax.ShapeDtypeStruct((N_CHIPS,) + s.shape, s.dtype)

    def gather_start(names, l, tag, after):
        srcs = [bf16_shard(n, l, None if after is None else after[0, 0]) for n in names]
        return _ici_start(_gather_copies, srcs, [land_shape(s) for s in srcs], name=f"gather_ici_{tag}_start",
                          after=[tiny[0]] if after is None else [after])

    def gather_finish(handle, names, l, tag, after):
        srcs, lands = _ici_wait(_gather_copies, *handle[:4], after, name=f"gather_ici_{tag}_wait")
        use_gathered(l, names, _gather_d2d(srcs, lands, tag))

    tiny = _allgather_chips([w["meta"].reshape(2, N_META // 2, D_MODEL // N_CHIPS), w["conv_w"]])
    meta_full = jnp.concatenate([tiny[0][k].reshape(N_META, D_MODEL // N_CHIPS) for k in range(N_CHIPS)], axis=1)

    W = {n: [None] * DEPTH for n in MATMUL_W + ["w_in_p", "mla_w_uq_p", "mla_w_ukv_p"]}
    W["conv_w"] = jnp.concatenate([tiny[1][k] for k in range(N_CHIPS)], axis=-1)
    W["meta"] = meta_full
    for n in SMALL:
        W[n] = w[n]

    def use_gathered(l, names, lands):
        got = dict(zip(names, lands))
        cat = lambda n, cut=None: jnp.concatenate([got[n][k][..., :cut] for k in range(N_CHIPS)], axis=-1)
        for n in names:
            W[n][l] = got[n]
        if "w_in" in got:
            W["w_in_p"][l] = _pad_in_proj(cat("w_in", IN_SHARD))
            W["mla_w_uq_p"][l] = _regroup_uq(cat("mla_w_uq"))
            W["mla_w_ukv_p"][l] = _regroup_ukv(cat("mla_w_ukv"))

    def chunk_grads(G, names):
        def chunked(name, ungroup, width, pad):
            full = ungroup(G[name])
            return _pad_to(jnp.moveaxis(full.reshape(full.shape[0], N_CHIPS, width), 1, 0), 2, pad)
        special = {"mla_w_uq": ("mla_w_uq_p", _ungroup_uq, MLA_NOPE + MLA_ROPE, MLA_NOPE + MLA_ROPE),
                   "mla_w_ukv": ("mla_w_ukv_p", _ungroup_ukv, MLA_NOPE + MLA_V, MLA_NOPE + MLA_V)}
        return [_in_proj_grad_chunks(G["w_in_p"]) if n == "w_in" else chunked(*special[n]) if n in special else G[n]
                for n in names]

    def rs_start(G, names, tag):
        pairs = _rs_pair_sums(chunk_grads(G, names), names, tag)
        handle = _ici_start(_exchange_copies, [p[1] for p in pairs], _exchange_land_shapes(pairs),
                            name=f"rs_exchange_{tag}_start")
        return pairs, handle

    def swap_start(G, names, tag):
        gs = chunk_grads(G, names)
        return _ici_start(_swap_copies, gs, _swap_land_shapes(gs), name=f"rs_swap_{tag}_start")

    def exchange_start(swap_handle, names, tag, after):
        gs, r1 = _ici_wait(_swap_copies, *swap_handle[:4], after, name=f"rs_swap_{tag}_wait")
        pairs = _rs_add_pairs(gs, r1, names, tag)
        handle = _ici_start(_exchange_copies, [p[1] for p in pairs], _exchange_land_shapes(pairs),
                            name=f"rs_exchange_{tag}_start")
        return pairs, handle

    def rs_end(pairs, handle, names, tag, after):
        _, r2 = _ici_wait(_exchange_copies, *handle[:4], after, name=f"rs_exchange_{tag}_wait")
        return dict(zip(names, _rs_finish(pairs, r2, names, tag)))

    ffn1_w, mix_w, ffn2_w = MATMUL_W[:3], MATMUL_W[3:7], MATMUL_W[7:]
    g_a = gather_start(ffn1_w, 0, "l0_ffn1", None)
    g_b = gather_start(mix_w, 0, "l0_mix", g_a[4])
    g_c = gather_start(ffn2_w, 0, "l0_ffn2", g_b[4])
    g_l1 = gather_start(MATMUL_W, 1, "l1", g_c[4])
    token = g_l1[4]
    cosf, sins = _rope_tables()
    ln = lambda n, l: W[n][l][None]
    h = _embed(x[0] + token[0, 0], meta_full)
    h = (h, h.astype(BF16))
    gather_finish(g_a, ffn1_w, 0, "l0_ffn1", h[1])
    h1, s1 = _ffn_fwd(h, W, "ffn1", 0, ln("ln1_g", 0), ln("ln1_b", 0), "l0_ffn1")
    gather_finish(g_b, mix_w, 0, "l0_mix", h1[1])
    h2, sm = _mixer_fwd(h1, W, 0, cosf, sins)
    gather_finish(g_c, ffn2_w, 0, "l0_ffn2", h2[1])
    h, s2 = _ffn_fwd(h2, W, "ffn2", 0, ln("ln3_g", 0), ln("ln3_b", 0), "l0_ffn2")
    saved0 = (s1, sm, s2)
    gather_finish(g_l1, MATMUL_W, 1, "l1", h[1])
    h, saved1 = _layer_fwd(h, W, 1, cosf, sins)
    tgt = jnp.concatenate([jnp.zeros((BLOCK, D_MODEL), F32), loss_target[0]], axis=0)
    dh, loss = _loss_head(h[0], tgt, name="loss_head")
    G = [None] * DEPTH
    dh, G[1] = _layer_bwd(dh, saved1, W, 1, cosf, sins)
    ffn2_w, mix_w, ffn1_w = MATMUL_W[7:], MATMUL_W[3:7], MATMUL_W[:3]
    sw_l1 = swap_start(G[1], MATMUL_W, "l1")
    G0 = {}
    dh, dg, db = _ffn_bwd(dh, s2, W, "ffn2", 0, ln("ln3_g", 0) + sw_l1[4][0, 0], ln("ln3_b", 0), G0, "l0_ffn2")
    G0["ln3_g"], G0["ln3_b"] = dg[0], db[0]
    pairs_l1, x_l1 = exchange_start(sw_l1, MATMUL_W, "l1", dh)
    sw_a = swap_start(G0, ffn2_w, "l0_ffn2")
    dh, Gm = _mixer_bwd(dh, sm, W, 0, cosf, sins, G0, zero=x_l1[4][0, 0] + sw_a[4][0, 0])
    G0.update(Gm)
    pairs_a, x_a = exchange_start(sw_a, ffn2_w, "l0_ffn2", dh)
    reduced1 = rs_end(pairs_l1, x_l1, MATMUL_W, "l1", dh)
    pairs_b, x_b = rs_start(G0, mix_w, "l0_mix")
    dh0, dg, db = _ffn_bwd(dh, s1, W, "ffn1", 0, ln("ln1_g", 0) + (x_a[4][0, 0] + x_b[4][0, 0]), ln("ln1_b", 0), G0,
                           "l0_ffn1")
    G0["ln1_g"], G0["ln1_b"] = dg[0], db[0]
    G[0] = G0
    reduced0 = rs_end(pairs_a, x_a, ffn2_w, "l0_ffn2", dh0)
    reduced0.update(rs_end(pairs_b, x_b, mix_w, "l0_mix", dh0))

    small_parts = [jnp.stack([G[l][n] for l in range(DEPTH)]).reshape(-1) for n in SMALL]
    small_parts += [jnp.stack([G[l]["conv_w"] for l in range(DEPTH)]).reshape(-1), dh0[PAD_ROWS:BLOCK].reshape(-1),
                    loss[0, :1]]
    flat = jnp.concatenate(small_parts)
    flat = jnp.pad(flat, (0, SMALL_ROWS * BLOCK - flat.shape[0]))
    red2d = _allreduce_small(flat.reshape(SMALL_ROWS, BLOCK))
    red = red2d.reshape(-1)

    pairs_c = _rs_pair_sums(chunk_grads(G0, ffn1_w), ffn1_w, "l0_ffn1")
    x_c = _ici_start(_exchange_copies, [p[1] for p in pairs_c], _exchange_land_shapes(pairs_c),
                     name="rs_exchange_l0_ffn1_start", after=[red2d])
    grads, off = {}, 0
    for n in SMALL:
        size = int(np.prod(w[n].shape))
        grads[n] = red[off:off + size].reshape(w[n].shape)
        off += size
    conv_full = red[off:off + DEPTH * SSD_CONV * 768].reshape(DEPTH, SSD_CONV, 768)
    off += DEPTH * SSD_CONV * 768
    dmeta_full = red[off:off + N_META * D_MODEL].reshape(N_META, D_MODEL)
    off += N_META * D_MODEL
    loss_out = red[off]
    grads["conv_w"] = _chip_cols(conv_full, chip, 768 // N_CHIPS)
    grads["meta"] = _chip_cols(dmeta_full, chip, D_MODEL // N_CHIPS)

    delta, new_m, new_v = {}, {}, {}

    def adamw_matmul_weights(names, after):
        for n in names:
            gs = [reduced0[n], reduced1[n]]
            if n.endswith("w_gate") or n.endswith("w_up"):
                res = _adamw(tr(w[n]), gs, tr(m[n]), tr(v[n]), name=f"adamw_{n}", after=after)
                grads[n], delta[n], new_m[n], new_v[n] = [tr(r) for r in res]
            else:
                grads[n], delta[n], new_m[n], new_v[n] = _adamw(w[n], gs, m[n], v[n], name=f"adamw_{n}", after=after)

    adamw_matmul_weights(ffn2_w + mix_w, [x_c[4]])
    rest = [n for n in WEIGHTS if n not in MATMUL_W]

    def pack_small(d):
        f = jnp.concatenate([d[n].reshape(-1) for n in rest])
        tot = -(-f.shape[0] // (8 * BLOCK)) * 8 * BLOCK
        return jnp.pad(f, (0, tot - f.shape[0])).reshape(-1, BLOCK)

    _, d2, m2, v2 = _adamw(pack_small(w), [pack_small(grads)], pack_small(m), pack_small(v), name="adamw_small",
                           after=[x_c[4]])
    reduced0.update(rs_end(pairs_c, x_c, ffn1_w, "l0_ffn1", d2))
    adamw_matmul_weights(ffn1_w, [])
    off = 0
    for n in rest:
        size = int(np.prod(w[n].shape))
        for dst, src in ((delta, d2), (new_m, m2), (new_v, v2)):
            dst[n] = src.reshape(-1)[off:off + size].reshape(w[n].shape)
        off += size

    grad_x = dh0[BLOCK:][None]
    return (loss_out, grad_x, *[grads[n] for n in WEIGHTS], *[delta[n] for n in WEIGHTS],
            *[new_m[n] for n in WEIGHTS], *[new_v[n] for n in WEIGHTS])
```

```python
import functools

import numpy as np
import jax
import jax.numpy as jnp
from jax import lax
from jax.experimental import pallas as pl
from jax.experimental.pallas import tpu as pltpu

F32 = jnp.float32
BF16 = jnp.bfloat16
MESH = pl.DeviceIdType.MESH

D_MODEL = 1024
SEQ = 2048
N_META = 16
BLOCK = 128
PAD_ROWS = 112
LP = PAD_ROWS + N_META + SEQ
N_CHUNK = LP // BLOCK
DEPTH = 2
D_FF = 2816
N_CHIPS = 4
FF_SHARD = D_FF // N_CHIPS
HP = 768
FP = N_CHIPS * HP
SSD_HEADS, SSD_HD, SSD_D, SSD_GROUPS, SSD_STATE, SSD_CONV = 8, 64, 512, 2, 64, 4
FOX_HEADS, FOX_HD, FOX_D = 4, 64, 256
MLA_HEADS, MLA_Q_LORA, MLA_KV_LORA, MLA_NOPE, MLA_ROPE, MLA_V, MLA_D = 4, 256, 128, 64, 32, 64, 256
ROPE_HALF = MLA_ROPE // 2
ROPE_THETA = 10000.0
N_IN = 2476
IN_SHARD = N_IN // N_CHIPS
IN_SHARD_P = 640
ALPHA = (2 * DEPTH) ** 0.25
EPS = 1e-5
ADAM_LR, ADAM_B1, ADAM_B2, ADAM_EPS, ADAM_WD, ADAM_STEP = 0.001, 0.9, 0.999, 1e-08, 0.01, 10
NEG = -1e30
TM = 544

VMEM_LIMIT_BYTES = 56 * 1024 * 1024

PC_Z, PC_XBC, PC_FQ, PC_FK, PC_FV, PC_CQ, PC_CKV, PC_DT, PC_FR, PC_KR, PC_END = (
    0, 512, 1280, 1536, 1792, 2048, 2304, 2432, 2560, 2688, 2816)
OC_Z, OC_XBC, OC_DT, OC_FQ, OC_FK, OC_FV, OC_FR, OC_CQ, OC_CKV, OC_KR = (
    0, 512, 1280, 1288, 1544, 1800, 2056, 2060, 2316, 2444)


def _cparams(sem=None):
    return pltpu.CompilerParams(dimension_semantics=sem, vmem_limit_bytes=VMEM_LIMIT_BYTES)


def _tile(n, cap, mult):
    best = None
    for t in range(mult, min(n, cap) + 1, mult):
        if n % t == 0:
            best = t
    return best if best is not None else n


def _bs(shape, fn):
    return pl.BlockSpec(shape, fn)


ANY = pl.BlockSpec(memory_space=pl.ANY)


def _dims(ca, cb):
    return (((ca,), (cb,)), ((), ()))


def _raw_bdot(a, b, ca, cb):
    return lax.dot_general(a.astype(BF16), b.astype(BF16), _dims(ca, cb), preferred_element_type=F32)


def _mm_core(a, b, *, a_spec, b_spec, o_spec, grid, out_shape, ca, cb, name, add=None):
    nk = grid[2]
    has_add = add is not None
    acc_shape = tuple(d for d in o_spec.block_shape if d is not None)

    def body(*refs):
        a_ref, b_ref = refs[0], refs[1]
        add_ref = refs[2] if has_add else None
        o_ref, acc_ref = refs[-2], refs[-1]
        k = pl.program_id(2)

        @pl.when(k == 0)
        def _():
            acc_ref[...] = jnp.zeros_like(acc_ref)

        acc_ref[...] += _raw_bdot(a_ref[...], b_ref[...], ca, cb)

        @pl.when(k == nk - 1)
        def _():
            r = acc_ref[...]
            if has_add:
                r = r + add_ref[...]
            o_ref[...] = r

    ins = [a, b] + ([add] if has_add else [])
    in_specs = [a_spec, b_spec] + ([o_spec] if has_add else [])
    return pl.pallas_call(
        body, name=name, grid=grid, in_specs=in_specs, out_specs=o_spec,
        out_shape=jax.ShapeDtypeStruct(out_shape, F32), scratch_shapes=[pltpu.VMEM(acc_shape, F32)],
        compiler_params=_cparams(("parallel", "parallel", "arbitrary")),
    )(*ins)


MM_VMEM_BUDGET = 40 * 1024 * 1024


def _divisors(n, mult):
    return [t for t in range(mult, n + 1, mult) if n % t == 0] or [n]


def _pick_tiles(M, N, K, a_bytes, b_bytes, ta, has_add):
    best = None
    for tm in _divisors(M, 128 if ta else 16):
        for tn in _divisors(N, 128):
            vmem = 2 * tm * K * a_bytes + 2 * K * tn * b_bytes + (3 + 2 * int(has_add)) * tm * tn * 4
            if vmem <= MM_VMEM_BUDGET:
                key = ((M // tm) * (N // tn), -tn)
                if best is None or key < best[0]:
                    best = (key, tm, tn)
    assert best is not None, (M, N, K)
    return best[1], best[2], K


def _mm(a, b, *, ta=False, tb=False, add=None, name):
    if ta:
        K, M = a.shape
    else:
        M, K = a.shape
    if tb:
        N, Kb = b.shape
    else:
        Kb, N = b.shape
    assert K == Kb, (a.shape, b.shape, ta, tb)
    tm, tn, tk = _pick_tiles(M, N, K, a.dtype.itemsize, b.dtype.itemsize, ta, add is not None)
    a_spec = _bs((tk, tm), lambda i, j, k: (k, i)) if ta else _bs((tm, tk), lambda i, j, k: (i, k))
    b_spec = _bs((tn, tk), lambda i, j, k: (j, k)) if tb else _bs((tk, tn), lambda i, j, k: (k, j))
    return _mm_core(a, b, a_spec=a_spec, b_spec=b_spec, o_spec=_bs((tm, tn), lambda i, j, k: (i, j)),
                    grid=(M // tm, N // tn, K // tk), out_shape=(M, N), ca=0 if ta else 1, cb=1 if tb else 0,
                    name=name, add=add)


def _row_entry(r, ncol):
    if isinstance(r, tuple):
        return r
    return r, r.shape[1] // ncol, 0


def _rowwise(fn, rows, pars, out_cols, *, name, tile, ncol=1, out_dtypes=None):
    rows = [_row_entry(r, ncol) for r in rows]
    L = rows[0][0].shape[0]
    nr, npar = len(rows), len(pars)
    in_specs = [_bs((tile, w), lambda g, i, o=o: (i, o + g)) for _, w, o in rows]
    in_specs += [_bs((p.shape[0], p.shape[1] // ncol), lambda g, i: (0, g)) for p in pars]
    out_specs = [_bs((tile, c // ncol), lambda g, i: (i, g)) for c in out_cols]

    def body(*refs):
        ins, outs = refs[:nr + npar], refs[nr + npar:]
        row0 = pl.program_id(1) * tile
        res = fn(row0, *[r[...] for r in ins])
        for o, v in zip(outs, res):
            o[...] = v.astype(o.dtype)

    return pl.pallas_call(
        body, name=name, grid=(ncol, L // tile), in_specs=in_specs, out_specs=out_specs,
        out_shape=[jax.ShapeDtypeStruct((L, c), d) for c, d in zip(out_cols, out_dtypes or [F32] * len(out_cols))],
        compiler_params=_cparams(("parallel", "parallel")),
    )(*[r[0] for r in rows], *pars)


def _rowwise_bwd(fn, rows, pars, douts, *, name, tile, ncol=1, row_grad=None, grad_dtypes=None):
    rows = [_row_entry(r, ncol) for r in rows]
    L = rows[0][0].shape[0]
    nr, npar, nd = len(rows), len(pars), len(douts)
    row_grad = [True] * nr if row_grad is None else row_grad
    in_specs = [_bs((tile, w), lambda g, i, o=o: (i, o + g)) for _, w, o in rows]
    in_specs += [_bs((p.shape[0], p.shape[1] // ncol), lambda g, i: (0, g)) for p in pars]
    in_specs += [_bs((tile, d.shape[1] // ncol), lambda g, i: (i, g)) for d in douts]
    g_widths = [w * ncol for (_, w, _), f in zip(rows, row_grad) if f]
    out_specs = [_bs((tile, w // ncol), lambda g, i: (i, g)) for w in g_widths]
    out_specs += [_bs((p.shape[0], p.shape[1] // ncol), lambda g, i: (0, g)) for p in pars]
    out_shape = [jax.ShapeDtypeStruct((L, w), d) for w, d in zip(g_widths, grad_dtypes or [F32] * len(g_widths))]
    out_shape += [jax.ShapeDtypeStruct(p.shape, F32) for p in pars]

    def body(*refs):
        ins = refs[:nr + npar]
        dos = refs[nr + npar:nr + npar + nd]
        outs = refs[nr + npar + nd:]
        i = pl.program_id(1)
        row0 = i * tile
        _, vjp = jax.vjp(lambda *a: tuple(fn(row0, *a)), *[r[...] for r in ins])
        grads = vjp(tuple(d[...].astype(F32) for d in dos))
        o = 0
        for j in range(nr):
            if row_grad[j]:
                outs[o][...] = grads[j].astype(outs[o].dtype)
                o += 1
        for j in range(npar):
            g, ref = grads[nr + j], outs[o + j]

            @pl.when(i == 0)
            def _(g=g, ref=ref):
                ref[...] = g

            @pl.when(i > 0)
            def _(g=g, ref=ref):
                ref[...] += g

    res = pl.pallas_call(
        body, name=name, grid=(ncol, L // tile), in_specs=in_specs, out_specs=out_specs, out_shape=out_shape,
        compiler_params=_cparams(("parallel", "arbitrary")),
    )(*[r[0] for r in rows], *pars, *douts)
    return res[:len(g_widths)], res[len(g_widths):]


def _sigmoid(x):
    return 1.0 / (1.0 + jnp.exp(-x))


def _softplus(x):
    return jnp.maximum(x, 0.0) + jnp.log(1.0 + jnp.exp(-jnp.abs(x)))


def _silu(x):
    return x * _sigmoid(x)


def _make_res_ln_fn(scale):
    def fn(row0, h, o, gam, bet):
        pre = ALPHA * h + scale * o
        mu = jnp.mean(pre, axis=-1, keepdims=True)
        xc = pre - mu
        var = jnp.mean(xc * xc, axis=-1, keepdims=True)
        return (xc * lax.rsqrt(var + EPS) * gam + bet,)
    return fn


def _ssd_post_fn(row0, y, xs, z, dskip, normg):
    v = (y + dskip * xs) * _silu(z)
    v = v * lax.rsqrt(jnp.mean(v * v, axis=-1, keepdims=True) + EPS)
    return (v * normg,)


def _mla_norm_fn(row0, cq, ckv, gq, gkv):
    qn = cq * lax.rsqrt(jnp.mean(cq * cq, axis=-1, keepdims=True) + EPS) * gq
    cn = ckv * lax.rsqrt(jnp.mean(ckv * ckv, axis=-1, keepdims=True) + EPS) * gkv
    return qn, cn


def _rope_fn(row0, q, k, cosf, sins):
    return (q * cosf + pltpu.roll(q, 64, 1) * sins, k * cosf + pltpu.roll(k, 64, 1) * sins)


def _rope_t_fn(row0, gq, gk, cosf, sins):
    return (gq * cosf + pltpu.roll(gq * sins, 64, 1), gk * cosf + pltpu.roll(gk * sins, 64, 1))


def _conv_fwd(x, x_off, w, b, *, name):
    C = w.shape[1]

    def body(x_ref, w_ref, b_ref, o_ref):
        rows = lax.broadcasted_iota(jnp.int32, (LP, BLOCK), 0)
        xv = jnp.where(rows >= PAD_ROWS, x_ref[...], 0.0)
        acc = b_ref[...] + w_ref[3:4, :] * xv
        for k in range(SSD_CONV - 1):
            acc = acc + w_ref[k:k + 1, :] * pltpu.roll(xv, SSD_CONV - 1 - k, 0)
        o_ref[...] = _silu(acc)

    return pl.pallas_call(
        body, name=name, grid=(C // BLOCK,),
        in_specs=[_bs((LP, BLOCK), lambda j: (0, j + x_off)), _bs((SSD_CONV, BLOCK), lambda j: (0, j)),
                  _bs((1, BLOCK), lambda j: (0, j))],
        out_specs=_bs((LP, BLOCK), lambda j: (0, j)),
        out_shape=jax.ShapeDtypeStruct((LP, C), F32), compiler_params=_cparams(("parallel",)),
    )(x, w, b)


def _conv_bwd(x, x_off, w, b, dout, *, name):
    C = w.shape[1]

    def body(x_ref, w_ref, b_ref, do_ref, dx_ref, dw_ref, db_ref):
        rows = lax.broadcasted_iota(jnp.int32, (LP, BLOCK), 0)
        real = rows >= PAD_ROWS
        xv = jnp.where(real, x_ref[...], 0.0)
        shifted = [pltpu.roll(xv, SSD_CONV - 1 - k, 0) for k in range(SSD_CONV - 1)] + [xv]
        acc = b_ref[...]
        for k in range(SSD_CONV):
            acc = acc + w_ref[k:k + 1, :] * shifted[k]
        sig = _sigmoid(acc)
        dacc = jnp.where(real, do_ref[...] * (sig * (1.0 + acc * (1.0 - sig))), 0.0)
        db_ref[...] = jnp.sum(dacc, axis=0, keepdims=True)
        dx = w_ref[3:4, :] * dacc
        for k in range(SSD_CONV):
            dw_ref[k:k + 1, :] = jnp.sum(dacc * shifted[k], axis=0, keepdims=True)
            if k < SSD_CONV - 1:
                dx = dx + w_ref[k:k + 1, :] * pltpu.roll(dacc, LP - (SSD_CONV - 1 - k), 0)
        dx_ref[...] = jnp.where(real, dx, 0.0)

    return pl.pallas_call(
        body, name=name, grid=(C // BLOCK,),
        in_specs=[_bs((LP, BLOCK), lambda j: (0, j + x_off)), _bs((SSD_CONV, BLOCK), lambda j: (0, j)),
                  _bs((1, BLOCK), lambda j: (0, j)), _bs((LP, BLOCK), lambda j: (0, j))],
        out_specs=[_bs((LP, BLOCK), lambda j: (0, j)), _bs((SSD_CONV, BLOCK), lambda j: (0, j)),
                   _bs((1, BLOCK), lambda j: (0, j))],
        out_shape=[jax.ShapeDtypeStruct((LP, C), F32), jax.ShapeDtypeStruct((SSD_CONV, C), F32),
                   jax.ShapeDtypeStruct((1, C), F32)],
        compiler_params=_cparams(("parallel",)),
    )(x, w, b, dout)


_BDIMS = {"nn": (((2,), (1,)), ((0,), (0,))), "nt": (((2,), (2,)), ((0,), (0,))), "tn": (((1,), (1,)), ((0,), (0,)))}


def _raw_bdot3(a, b, mode):
    return lax.dot_general(a.astype(BF16), b.astype(BF16), _BDIMS[mode], preferred_element_type=F32)


@functools.partial(jax.custom_vjp, nondiff_argnums=(2,))
def _bdot3(a, b, mode):
    return _raw_bdot3(a, b, mode)


def _bdot3_fwd(a, b, mode):
    return _raw_bdot3(a, b, mode), (a, b)


def _bdot3_bwd(mode, res, g):
    a, b = res
    if mode == "nn":
        return _raw_bdot3(g, b, "nt"), _raw_bdot3(a, g, "tn")
    if mode == "nt":
        return _raw_bdot3(g, b, "nn"), _raw_bdot3(g, a, "tn")
    return _raw_bdot3(b, g, "nt"), _raw_bdot3(a, g, "nn")


_bdot3.defvjp(_bdot3_fwd, _bdot3_bwd)


def _ssd_chunk(x, bm, cm, dt, dtt, alog, prev):
    rep = SSD_HEADS // SSD_GROUPS
    per_head = lambda t: jnp.broadcast_to(t[:, None], (SSD_GROUPS, rep) + t.shape[1:]).reshape((SSD_HEADS,) + t.shape[1:])
    bm, cm = per_head(bm), per_head(cm)
    lane_h = lax.broadcasted_iota(jnp.int32, (1, BLOCK), 1)
    row_h = lax.broadcasted_iota(jnp.int32, (BLOCK, 1), 0)
    dtc = jnp.stack([jnp.sum(jnp.where(lane_h == h, dt, 0.0), axis=1, keepdims=True) for h in range(SSD_HEADS)])
    dtr = jnp.stack([jnp.sum(jnp.where(row_h == h, dtt, 0.0), axis=0, keepdims=True) for h in range(SSD_HEADS)])
    lane = lax.broadcasted_iota(jnp.int32, alog.shape, 2)
    a_neg = -jnp.exp(jnp.sum(jnp.where(lane == 0, alog, 0.0), axis=2, keepdims=True))
    ac_in = dtc * a_neg
    ar_in = dtr * a_neg
    li = lax.broadcasted_iota(jnp.int32, (1, BLOCK, BLOCK), 1)
    si = lax.broadcasted_iota(jnp.int32, (1, BLOCK, BLOCK), 2)
    causal = li >= si
    acum_c = jnp.sum(jnp.where(causal, ar_in, 0.0), axis=2, keepdims=True)
    acum_r = jnp.sum(jnp.where(li <= si, ac_in, 0.0), axis=1, keepdims=True)
    total = jnp.sum(ar_in, axis=2, keepdims=True)
    seg = jnp.exp(jnp.where(causal, acum_c - acum_r, NEG))
    xdt = x * dtc
    cb = _bdot3(cm, bm, "nt")
    y = _bdot3(cb * seg, xdt, "nn") + _bdot3(cm, prev, "nt") * jnp.exp(acum_c)
    st = _bdot3(xdt, bm * jnp.exp(total - acum_c), "tn")
    return y, prev * jnp.exp(total) + st


def _ssd_dt_fwd(raw, raw_blk, bias, *, name):
    def body(raw_ref, b_ref, dt_ref, dtt_ref):
        rows = pl.program_id(0) * BLOCK + lax.broadcasted_iota(jnp.int32, (BLOCK, BLOCK), 0)
        dt = jnp.where(rows >= PAD_ROWS, _softplus(raw_ref[...] + b_ref[...]), 0.0)
        dt_ref[...] = dt
        dtt_ref[...] = dt.T

    return pl.pallas_call(
        body, name=name, grid=(N_CHUNK,),
        in_specs=[_bs((BLOCK, BLOCK), lambda j: (j, raw_blk)), _bs((1, BLOCK), lambda j: (0, 0))],
        out_specs=[_bs((BLOCK, BLOCK), lambda j: (j, 0)), _bs((BLOCK, BLOCK), lambda j: (0, j))],
        out_shape=[jax.ShapeDtypeStruct((LP, BLOCK), F32), jax.ShapeDtypeStruct((BLOCK, LP), F32)],
        compiler_params=_cparams(("parallel",)),
    )(raw, bias)


def _ssd_dt_bwd(raw, raw_blk, bias, ddt, ddtt, *, name):
    def body(raw_ref, b_ref, ddt_ref, ddtt_ref, draw_ref, db_ref):
        j = pl.program_id(0)
        rows = j * BLOCK + lax.broadcasted_iota(jnp.int32, (BLOCK, BLOCK), 0)
        g = ddt_ref[...] + ddtt_ref[...].T
        draw = jnp.where(rows >= PAD_ROWS, g * _sigmoid(raw_ref[...] + b_ref[...]), 0.0)
        draw_ref[...] = draw
        dsum = jnp.sum(draw, axis=0, keepdims=True)

        @pl.when(j == 0)
        def _():
            db_ref[...] = dsum

        @pl.when(j > 0)
        def _():
            db_ref[...] += dsum

    return pl.pallas_call(
        body, name=name, grid=(N_CHUNK,),
        in_specs=[_bs((BLOCK, BLOCK), lambda j: (j, raw_blk)), _bs((1, BLOCK), lambda j: (0, 0)),
                  _bs((BLOCK, BLOCK), lambda j: (j, 0)), _bs((BLOCK, BLOCK), lambda j: (0, j))],
        out_specs=[_bs((BLOCK, BLOCK), lambda j: (j, 0)), _bs((1, BLOCK), lambda j: (0, 0))],
        out_shape=[jax.ShapeDtypeStruct((LP, BLOCK), F32), jax.ShapeDtypeStruct((1, BLOCK), F32)],
        compiler_params=_cparams(("arbitrary",)),
    )(raw, bias, ddt, ddtt)


def _ssd_specs(rev):
    ci = (lambda c: N_CHUNK - 1 - c) if rev else (lambda c: c)
    x_spec = _bs((SSD_HEADS, BLOCK, SSD_HD), lambda c: (0, ci(c), 0))
    g_spec = _bs((SSD_GROUPS, BLOCK, SSD_STATE), lambda c: (0, ci(c), 0))
    dtc_spec = _bs((BLOCK, BLOCK), lambda c: (ci(c), 0))
    dtr_spec = _bs((BLOCK, BLOCK), lambda c: (0, ci(c)))
    al_spec = _bs((SSD_HEADS, 1, BLOCK), lambda c: (0, 0, 0))
    st_spec = _bs((None, SSD_HEADS, SSD_HD, SSD_STATE), lambda c: (ci(c), 0, 0, 0))
    return x_spec, g_spec, dtc_spec, dtr_spec, al_spec, st_spec


def _ssd_fwd(x, bm, cm, dtc, dtr, alog, *, name):
    x_spec, g_spec, dtc_spec, dtr_spec, al_spec, st_spec = _ssd_specs(False)

    def body(x_ref, b_ref, c_ref, dtc_ref, dtr_ref, al_ref, y_ref, prev_ref, state):
        @pl.when(pl.program_id(0) == 0)
        def _():
            state[...] = jnp.zeros_like(state)

        prev = state[...]
        prev_ref[...] = prev
        y, new = _ssd_chunk(x_ref[...], b_ref[...], c_ref[...], dtc_ref[...], dtr_ref[...], al_ref[...], prev)
        y_ref[...] = y
        state[...] = new

    return pl.pallas_call(
        body, name=name, grid=(N_CHUNK,),
        in_specs=[x_spec, g_spec, g_spec, dtc_spec, dtr_spec, al_spec], out_specs=[x_spec, st_spec],
        out_shape=[jax.ShapeDtypeStruct((SSD_HEADS, LP, SSD_HD), F32),
                   jax.ShapeDtypeStruct((N_CHUNK, SSD_HEADS, SSD_HD, SSD_STATE), F32)],
        scratch_shapes=[pltpu.VMEM((SSD_HEADS, SSD_HD, SSD_STATE), F32)],
        compiler_params=_cparams(("arbitrary",)),
    )(x, bm, cm, dtc, dtr, alog)


def _ssd_bwd(x, bm, cm, dtc, dtr, alog, prevs, dy, *, name):
    x_spec, g_spec, dtc_spec, dtr_spec, al_spec, st_spec = _ssd_specs(True)

    def body(x_ref, b_ref, c_ref, dtc_ref, dtr_ref, al_ref, prev_ref, dy_ref,
             dx_ref, db_ref, dc_ref, ddtc_ref, ddtr_ref, dal_ref, dstate):
        c = pl.program_id(0)

        @pl.when(c == 0)
        def _():
            dstate[...] = jnp.zeros_like(dstate)

        _, vjp = jax.vjp(_ssd_chunk, x_ref[...], b_ref[...], c_ref[...], dtc_ref[...], dtr_ref[...], al_ref[...],
                         prev_ref[...])
        dx, db, dc, ddtc, ddtr, dal, dprev = vjp((dy_ref[...], dstate[...]))
        dx_ref[...] = dx
        db_ref[...] = db
        dc_ref[...] = dc
        ddtc_ref[...] = ddtc
        ddtr_ref[...] = ddtr
        dstate[...] = dprev

        @pl.when(c == 0)
        def _():
            dal_ref[...] = dal

        @pl.when(c > 0)
        def _():
            dal_ref[...] += dal

    hs = jax.ShapeDtypeStruct((SSD_HEADS, LP, SSD_HD), F32)
    gs = jax.ShapeDtypeStruct((SSD_GROUPS, LP, SSD_STATE), F32)
    return pl.pallas_call(
        body, name=name, grid=(N_CHUNK,),
        in_specs=[x_spec, g_spec, g_spec, dtc_spec, dtr_spec, al_spec, st_spec, x_spec],
        out_specs=[x_spec, g_spec, g_spec, dtc_spec, dtr_spec, al_spec],
        out_shape=[hs, gs, gs, jax.ShapeDtypeStruct((LP, BLOCK), F32),
                   jax.ShapeDtypeStruct((BLOCK, LP), F32), jax.ShapeDtypeStruct((SSD_HEADS, 1, BLOCK), F32)],
        scratch_shapes=[pltpu.VMEM((SSD_HEADS, SSD_HD, SSD_STATE), F32)],
        compiler_params=_cparams(("arbitrary",)),
    )(x, bm, cm, dtc, dtr, alog, prevs, dy)


def _tri_dot(tri, v):
    hi = v.astype(BF16)
    r1 = v - hi.astype(F32)
    mid = r1.astype(BF16)
    lo = (r1 - mid.astype(F32)).astype(BF16)
    t = tri.astype(BF16)
    d = lambda p: lax.dot_general(t, p, _dims(1, 0), preferred_element_type=F32)
    return d(hi) + d(mid) + d(lo)


def _fox_gate_fwd(raw, raw_blk, bias, *, name):
    def body(raw_ref, b_ref, c_ref, ct_ref, carry):
        j = pl.program_id(0)

        @pl.when(j == 0)
        def _():
            carry[...] = jnp.zeros_like(carry)

        rows = j * BLOCK + lax.broadcasted_iota(jnp.int32, (BLOCK, BLOCK), 0)
        lf = jnp.where(rows >= PAD_ROWS, -_softplus(-(raw_ref[...] + b_ref[...])), 0.0)
        li = lax.broadcasted_iota(jnp.int32, (BLOCK, BLOCK), 0)
        si = lax.broadcasted_iota(jnp.int32, (BLOCK, BLOCK), 1)
        cv = _tri_dot(jnp.where(li >= si, 1.0, 0.0), lf) + carry[...]
        c_ref[...] = cv
        ct_ref[...] = cv.T
        carry[...] += jnp.sum(lf, axis=0, keepdims=True)

    return pl.pallas_call(
        body, name=name, grid=(N_CHUNK,),
        in_specs=[_bs((BLOCK, BLOCK), lambda j: (j, raw_blk)), _bs((1, BLOCK), lambda j: (0, 0))],
        out_specs=[_bs((BLOCK, BLOCK), lambda j: (j, 0)), _bs((BLOCK, BLOCK), lambda j: (0, j))],
        out_shape=[jax.ShapeDtypeStruct((LP, BLOCK), F32), jax.ShapeDtypeStruct((BLOCK, LP), F32)],
        scratch_shapes=[pltpu.VMEM((1, BLOCK), F32)], compiler_params=_cparams(("arbitrary",)),
    )(raw, bias)


def _fox_gate_bwd(raw, raw_blk, bias, dc, dct, *, name):
    rj = lambda j: N_CHUNK - 1 - j

    def body(raw_ref, b_ref, dc_ref, dct_ref, draw_ref, db_ref, carry):
        j = pl.program_id(0)

        @pl.when(j == 0)
        def _():
            carry[...] = jnp.zeros_like(carry)

        rows = (N_CHUNK - 1 - j) * BLOCK + lax.broadcasted_iota(jnp.int32, (BLOCK, BLOCK), 0)
        li = lax.broadcasted_iota(jnp.int32, (BLOCK, BLOCK), 0)
        si = lax.broadcasted_iota(jnp.int32, (BLOCK, BLOCK), 1)
        dcv = dc_ref[...] + dct_ref[...].T
        dlf = _tri_dot(jnp.where(li <= si, 1.0, 0.0), dcv) + carry[...]
        carry[...] += jnp.sum(dcv, axis=0, keepdims=True)
        draw = jnp.where(rows >= PAD_ROWS, dlf * (1.0 - _sigmoid(raw_ref[...] + b_ref[...])), 0.0)
        draw_ref[...] = draw
        dsum = jnp.sum(draw, axis=0, keepdims=True)

        @pl.when(j == 0)
        def _():
            db_ref[...] = dsum

        @pl.when(j > 0)
        def _():
            db_ref[...] += dsum

    return pl.pallas_call(
        body, name=name, grid=(N_CHUNK,),
        in_specs=[_bs((BLOCK, BLOCK), lambda j: (rj(j), raw_blk)), _bs((1, BLOCK), lambda j: (0, 0)),
                  _bs((BLOCK, BLOCK), lambda j: (rj(j), 0)), _bs((BLOCK, BLOCK), lambda j: (0, rj(j)))],
        out_specs=[_bs((BLOCK, BLOCK), lambda j: (rj(j), 0)), _bs((1, BLOCK), lambda j: (0, 0))],
        out_shape=[jax.ShapeDtypeStruct((LP, BLOCK), F32), jax.ShapeDtypeStruct((1, BLOCK), F32)],
        scratch_shapes=[pltpu.VMEM((1, BLOCK), F32)], compiler_params=_cparams(("arbitrary",)),
    )(raw, bias, dc, dct)


ATT_W = 256
ATT_QB = 272
ATT_STEPS = LP // ATT_QB
ATT_KEYS = (640, 1152, 1664, LP)
ATT_BLOCKS_PER_CLASS = ATT_STEPS // len(ATT_KEYS)


def _lane_head(width, per, mod=None):
    lane = lax.broadcasted_iota(jnp.int32, (1, width), 1)
    if mod is not None:
        lane = lane % mod
    return lane // per


def _attn_mask(i, kw):
    r = i * ATT_QB + lax.broadcasted_iota(jnp.int32, (ATT_QB, kw), 0)
    c = lax.broadcasted_iota(jnp.int32, (ATT_QB, kw), 1)
    return (c <= r) & ((c >= PAD_ROWS) | (r < PAD_ROWS))


def _attn_by_key_class(i, fn):
    for p, kw in enumerate(ATT_KEYS):
        @pl.when(i // ATT_BLOCKS_PER_CLASS == p)
        def _(kw=kw):
            fn(kw)


def _attn_specs(q, k, v, bias, rope):
    qspec = lambda blk, w=ATT_W: _bs((ATT_QB, w), lambda i: (i, blk))
    fspec = lambda blk, w=ATT_W: _bs((LP, w), lambda i: (0, blk))
    ins = [q[0], k[0], v[0]]
    specs = [qspec(q[1]), fspec(k[1]), fspec(v[1])]
    if bias is not None:
        ins += [bias[0], bias[1]]
        specs += [qspec(0, BLOCK), _bs((BLOCK, LP), lambda i: (0, 0))]
    if rope is not None:
        ins += [rope[0][0], rope[1][0]]
        specs += [qspec(rope[0][1], BLOCK), fspec(rope[1][1], BLOCK)]
    return ins, specs, qspec, fspec


def _attn_fwd(q, k, v, *, scale, name, bias=None, rope=None):
    ins, specs, qspec, fspec = _attn_specs(q, k, v, bias, rope)
    has_bias, has_rope = bias is not None, rope is not None

    def body(*refs):
        it = iter(refs)
        q_ref, k_ref, v_ref = next(it), next(it), next(it)
        if has_bias:
            c_ref, ct_ref = next(it), next(it)
        if has_rope:
            qr_ref, kr_ref = next(it), next(it)
        o_ref, lse_ref = next(it), next(it)
        i = pl.program_id(0)

        def block(kw):
            ok = _attn_mask(i, kw)
            qv, kv, vv = q_ref[...].astype(BF16), k_ref[0:kw, :].astype(BF16), v_ref[0:kw, :].astype(BF16)
            hid, l128 = _lane_head(ATT_W, FOX_HD), _lane_head(BLOCK, 1)
            if has_rope:
                rid = _lane_head(BLOCK, ROPE_HALF, 64)
                qrv, krv = qr_ref[...].astype(BF16), kr_ref[0:kw, :].astype(BF16)
            def head(h, carry):
                o_acc, lse_acc = carry
                s = _raw_bdot(jnp.where(hid == h, qv, 0.0), kv, 1, 1)
                if has_rope:
                    s = s + _raw_bdot(jnp.where(rid == h, qrv, 0.0), krv, 1, 1)
                s = s * scale
                if has_bias:
                    cq = jnp.sum(jnp.where(l128 == h, c_ref[...], 0.0), axis=1, keepdims=True)
                    s = s + (cq - ct_ref[pl.ds(h, 1), 0:kw])
                s = jnp.where(ok, s, NEG)
                m = jnp.max(s, axis=1, keepdims=True)
                p = jnp.exp(s - m)
                l = jnp.sum(p, axis=1, keepdims=True)
                o_acc = jnp.where(hid == h, _raw_bdot(p, vv, 1, 0) / l, o_acc)
                lse_acc = jnp.where(l128 == h, m + jnp.log(l), lse_acc)
                return o_acc, lse_acc

            o_acc, lse_acc = lax.fori_loop(
                0, FOX_HEADS, head, (jnp.zeros((ATT_QB, ATT_W), F32), jnp.zeros((ATT_QB, BLOCK), F32)), unroll=True)
            o_ref[...] = o_acc
            lse_ref[...] = lse_acc

        _attn_by_key_class(i, block)

    return pl.pallas_call(
        body, name=name, grid=(ATT_STEPS,), in_specs=specs, out_specs=[qspec(0), qspec(0, BLOCK)],
        out_shape=[jax.ShapeDtypeStruct((LP, ATT_W), F32), jax.ShapeDtypeStruct((LP, BLOCK), F32)],
        compiler_params=_cparams(("parallel",)),
    )(*ins)


def _attn_bwd(q, k, v, o, lse, do, *, scale, name, bias=None, rope=None):
    ins, specs, qspec, fspec = _attn_specs(q, k, v, bias, rope)
    has_bias, has_rope = bias is not None, rope is not None
    ins += [o, lse, do[0]]
    specs += [qspec(0), qspec(0, BLOCK), qspec(do[1])]

    def body(*refs):
        it = iter(refs)
        q_ref, k_ref, v_ref = next(it), next(it), next(it)
        if has_bias:
            c_ref, ct_ref = next(it), next(it)
        if has_rope:
            qr_ref, kr_ref = next(it), next(it)
        o_ref, lse_ref, do_ref = next(it), next(it), next(it)
        dq_ref, dk_ref, dv_ref = next(it), next(it), next(it)
        if has_bias:
            dc_ref, dct_ref = next(it), next(it)
        if has_rope:
            dqr_ref, dkr_ref = next(it), next(it)
        i = pl.program_id(0)

        @pl.when(i == 0)
        def _():
            dk_ref[...] = jnp.zeros_like(dk_ref)
            dv_ref[...] = jnp.zeros_like(dv_ref)
            if has_rope:
                dkr_ref[...] = jnp.zeros_like(dkr_ref)
            if has_bias:
                dct_ref[...] = jnp.zeros_like(dct_ref)

        def block(kw):
            ok = _attn_mask(i, kw)
            qv, kv, vv = q_ref[...].astype(BF16), k_ref[0:kw, :].astype(BF16), v_ref[0:kw, :].astype(BF16)
            dov, lsev = do_ref[...], lse_ref[...]
            dov_ov = dov * o_ref[...]
            dov = dov.astype(BF16)
            hid, l128 = _lane_head(ATT_W, FOX_HD), _lane_head(BLOCK, 1)
            if has_rope:
                rid = _lane_head(BLOCK, ROPE_HALF, 64)
                qrv, krv = qr_ref[...].astype(BF16), kr_ref[0:kw, :].astype(BF16)

            def head(h, carry):
                dq_acc, aux_acc = carry
                qm = jnp.where(hid == h, qv, 0.0)
                s = _raw_bdot(qm, kv, 1, 1)
                if has_rope:
                    qrm = jnp.where(rid == h, qrv, 0.0)
                    s = s + _raw_bdot(qrm, krv, 1, 1)
                s = s * scale
                if has_bias:
                    cq = jnp.sum(jnp.where(l128 == h, c_ref[...], 0.0), axis=1, keepdims=True)
                    s = s + (cq - ct_ref[pl.ds(h, 1), 0:kw])
                s = jnp.where(ok, s, NEG)
                p = jnp.exp(s - jnp.sum(jnp.where(l128 == h, lsev, 0.0), axis=1, keepdims=True))
                dom = jnp.where(hid == h, dov, 0.0)
                dp = _raw_bdot(dom, vv, 1, 1)
                delta = jnp.sum(jnp.where(hid == h, dov_ov, 0.0), axis=1, keepdims=True)
                ds = p * (dp - delta)
                dsb, pb = ds.astype(BF16), p.astype(BF16)
                dq_acc = jnp.where(hid == h, _raw_bdot(dsb, kv, 1, 0) * scale, dq_acc)
                dk_ref[0:kw, :] += _raw_bdot(dsb, qm, 0, 0) * scale
                dv_ref[0:kw, :] += _raw_bdot(pb, dom, 0, 0)
                if has_rope:
                    aux_acc = jnp.where(rid == h, _raw_bdot(dsb, krv, 1, 0) * scale, aux_acc)
                    dkr_ref[0:kw, :] += _raw_bdot(dsb, qrm, 0, 0) * scale
                if has_bias:
                    aux_acc = jnp.where(l128 == h, jnp.sum(ds, axis=1, keepdims=True), aux_acc)
                    dct_ref[pl.ds(h, 1), 0:kw] -= jnp.sum(ds, axis=0, keepdims=True)
                return dq_acc, aux_acc

            dq_acc, aux_acc = lax.fori_loop(
                0, FOX_HEADS, head, (jnp.zeros((ATT_QB, ATT_W), F32), jnp.zeros((ATT_QB, BLOCK), F32)))
            dq_ref[...] = dq_acc
            if has_bias:
                dc_ref[...] = aux_acc
            if has_rope:
                dqr_ref[...] = aux_acc

        _attn_by_key_class(i, block)

    wide = jax.ShapeDtypeStruct((LP, ATT_W), F32)
    narrow = jax.ShapeDtypeStruct((LP, BLOCK), F32)
    out_specs = [qspec(0), fspec(0), fspec(0)]
    out_shape = [wide, wide, wide]
    if has_bias:
        out_specs += [qspec(0, BLOCK), _bs((BLOCK, LP), lambda i: (0, 0))]
        out_shape += [narrow, jax.ShapeDtypeStruct((BLOCK, LP), F32)]
    if has_rope:
        out_specs += [qspec(0, BLOCK), fspec(0, BLOCK)]
        out_shape += [narrow, narrow]
    return pl.pallas_call(
        body, name=name, grid=(ATT_STEPS,), in_specs=specs, out_specs=out_specs, out_shape=out_shape,
        compiler_params=_cparams(("arbitrary",)),
    )(*ins)


def _loss_head(y, target, *, name):
    tile = 272

    def body(y_ref, t_ref, dy_ref, loss_ref):
        i = pl.program_id(0)
        rows = i * tile + lax.broadcasted_iota(jnp.int32, (tile, D_MODEL), 0)
        err = jnp.where(rows >= BLOCK, y_ref[...] - t_ref[...], 0.0)
        dy_ref[...] = err * (1.0 / D_MODEL)
        part = 0.5 * jnp.sum(jnp.sum(err * err, axis=1, keepdims=True) * (1.0 / D_MODEL), axis=0, keepdims=True)
        part = jnp.broadcast_to(part, (1, BLOCK))

        @pl.when(i == 0)
        def _():
            loss_ref[...] = part

        @pl.when(i > 0)
        def _():
            loss_ref[...] += part

    return pl.pallas_call(
        body, name=name, grid=(LP // tile,),
        in_specs=[_bs((tile, D_MODEL), lambda i: (i, 0)), _bs((tile, D_MODEL), lambda i: (i, 0))],
        out_specs=[_bs((tile, D_MODEL), lambda i: (i, 0)), _bs((1, BLOCK), lambda i: (0, 0))],
        out_shape=[jax.ShapeDtypeStruct((LP, D_MODEL), F32), jax.ShapeDtypeStruct((1, BLOCK), F32)],
        compiler_params=_cparams(("arbitrary",)),
    )(y, target)


def _adamw(w, gs, m, v, *, name, after=()):
    if w.ndim == 2:
        w, m, v = w[None], m[None], v[None]
        squeeze = True
    else:
        squeeze = False
    NL, R, C = w.shape
    assert len(gs) == NL
    CG = gs[0].shape[1]
    tile = _tile(R, 256, 8)

    def body(*refs):
        w_ref, g_refs = refs[0], refs[1:1 + NL]
        m_ref, v_ref = refs[1 + NL:3 + NL]
        go_ref, d_ref, nm_ref, nv_ref = refs[3 + NL + len(after):]
        gv = g_refs[0][:, :C]
        for j in range(1, NL):
            gv = jnp.where(pl.program_id(0) == j, g_refs[j][:, :C], gv)
        nm = ADAM_B1 * m_ref[...] + (1.0 - ADAM_B1) * gv
        nv = ADAM_B2 * v_ref[...] + (1.0 - ADAM_B2) * (gv * gv)
        m_hat = nm / (1.0 - ADAM_B1 ** ADAM_STEP)
        v_hat = nv / (1.0 - ADAM_B2 ** ADAM_STEP)
        go_ref[...] = gv
        d_ref[...] = -ADAM_LR * (m_hat / (jnp.sqrt(v_hat) + ADAM_EPS) + ADAM_WD * w_ref[...])
        nm_ref[...] = nm
        nv_ref[...] = nv

    spec = _bs((None, tile, C), lambda l, i: (l, i, 0))
    gspecs = [_bs((tile, CG), lambda l, i, j=j: (jnp.where(l == j, i, 0), 0)) for j in range(NL)]
    res = pl.pallas_call(
        body, name=name, grid=(NL, R // tile), in_specs=[spec, *gspecs, spec, spec, *[ANY] * len(after)],
        out_specs=[spec] * 4, out_shape=[jax.ShapeDtypeStruct((NL, R, C), F32)] * 4,
        compiler_params=_cparams(("parallel", "parallel")),
    )(w, *gs, m, v, *after)
    return [r[0] for r in res] if squeeze else res


def _my_pos():
    return lax.axis_index("x"), lax.axis_index("y"), lax.axis_index("c")


def _other_chips(x, y):
    return [(1 - x, y), (x, 1 - y), (1 - x, 1 - y)]


def _allgather_chips(shards):
    n = len(shards)
    per = 7

    def body(*refs):
        ins, outs = refs[:n], refs[n:2 * n]
        send_sems, recv_sems = refs[2 * n], refs[2 * n + 1]
        x, y, c = _my_pos()
        chips = _other_chips(x, y)
        sibling, me = (x, y, 1 - c), 2 * x + y

        def cp(a, kk, src, dst, to):
            return pltpu.make_async_remote_copy(src_ref=src, dst_ref=dst, send_sem=send_sems.at[per * a + kk],
                                                recv_sem=recv_sems.at[per * a + kk], device_id=to, device_id_type=MESH)

        sends = []
        for a in range(n):
            for j, chip in enumerate(chips):
                sends.append(cp(a, j, ins[a].at[c], outs[a].at[me, c], (*chip, c)))
            sends.append(cp(a, 3, ins[a], outs[a].at[me], sibling))
        for s in sends:
            s.start()
        for a in range(n):
            for j, chip in enumerate(chips):
                slab = outs[a].at[2 * chip[0] + chip[1], c]
                cp(a, j, slab, slab, (x, y, c)).wait_recv()
                fwd = cp(a, 4 + j, slab, slab, sibling)
                fwd.start()
                sends.append(fwd)
        for a in range(n):
            cp(a, 3, ins[a], outs[a].at[me], (x, y, c)).wait_recv()
            for j, chip in enumerate(chips):
                slab = outs[a].at[2 * chip[0] + chip[1], 1 - c]
                cp(a, 4 + j, slab, slab, (x, y, c)).wait_recv()
        for s in sends:
            s.wait_send()

    return pl.pallas_call(
        body, name="allgather_chips", in_specs=[ANY] * n, out_specs=[ANY] * n,
        out_shape=[jax.ShapeDtypeStruct((N_CHIPS,) + s.shape, s.dtype) for s in shards],
        scratch_shapes=[pltpu.SemaphoreType.DMA((per * n,)), pltpu.SemaphoreType.DMA((per * n,))],
    )(*shards)


def _rs_swap_rows(gs, tag):
    n = len(gs)

    def body(*refs):
        ins, outs = refs[:n], refs[n:2 * n]
        send_sems, recv_sems = refs[2 * n], refs[2 * n + 1]
        x, y, c = _my_pos()
        cps = []
        for a in range(n):
            half = ins[a].shape[1] // 2
            cps.append(pltpu.make_async_remote_copy(
                src_ref=ins[a].at[:, pl.ds((1 - c) * half, half)], dst_ref=outs[a], send_sem=send_sems.at[a],
                recv_sem=recv_sems.at[a], device_id=(x, y, 1 - c), device_id_type=MESH))
        for cp in cps:
            cp.start()
        for cp in cps:
            cp.wait()

    return pl.pallas_call(
        body, name=f"rs_swap_rows_{tag}", in_specs=[ANY] * n, out_specs=[ANY] * n,
        out_shape=[jax.ShapeDtypeStruct((N_CHIPS, g.shape[1] // 2, g.shape[2]), g.dtype) for g in gs],
        scratch_shapes=[pltpu.SemaphoreType.DMA((n,)), pltpu.SemaphoreType.DMA((n,))],
    )(*gs)


RS_ADD_VMEM_BYTES = 24 * 1024 * 1024


def _rs_tile(H, C, n):
    return _tile(H, max(16, RS_ADD_VMEM_BYTES // (28 * n * C)), 16)


def _rs_add_pair(gs, rs, pos, *, name):
    n = len(gs)
    _, H, C = rs[0].shape
    tile = _rs_tile(H, C, n)
    nt = H // tile

    def body(pos_ref, *refs):
        for a in range(n):
            s = refs[a][...] + refs[n + a][...]
            refs[2 * n + 2 * a][...] = s
            refs[2 * n + 2 * a + 1][...] = s.astype(BF16)

    spec = _bs((None, tile, C), lambda k, i, pos_ref: (k, i, 0))
    g_spec = _bs((None, tile, C), lambda k, i, pos_ref: (k, pos_ref[1] * nt + i, 0))
    grid_spec = pltpu.PrefetchScalarGridSpec(
        num_scalar_prefetch=1, grid=(N_CHIPS, nt), in_specs=[g_spec] * n + [spec] * n, out_specs=[spec] * (2 * n))
    res = pl.pallas_call(
        body, name=name, grid_spec=grid_spec,
        out_shape=[jax.ShapeDtypeStruct((N_CHIPS, H, C), F32), jax.ShapeDtypeStruct((N_CHIPS, H, C), BF16)] * n,
        compiler_params=_cparams(("parallel", "parallel")),
    )(pos, *gs, *rs)
    return [(res[2 * a], res[2 * a + 1]) for a in range(n)]


def _exchange_copies(srcs, lands, send_sems, recv_sems):
    x, y, c = _my_pos()
    starts, landing = [], []
    for a in range(len(srcs)):
        for j, chip in enumerate(_other_chips(x, y)):
            sems = dict(send_sem=send_sems.at[3 * a + j], recv_sem=recv_sems.at[3 * a + j], device_id_type=MESH)
            starts.append(pltpu.make_async_remote_copy(
                src_ref=srcs[a].at[2 * chip[0] + chip[1]], dst_ref=lands[a].at[j], device_id=(*chip, c), **sems))
            landing.append(pltpu.make_async_remote_copy(
                src_ref=lands[a].at[j], dst_ref=lands[a].at[j], device_id=(x, y, c), **sems))
    return starts, landing


def _gather_copies(srcs, lands, send_sems, recv_sems):
    x, y, c = _my_pos()
    me = 2 * x + y
    starts, landing = [], []
    for a in range(len(srcs)):
        half = srcs[a].shape[0] // 2
        mine = pl.ds(c * half, half)
        for j, chip in enumerate(_other_chips(x, y)):
            sems = dict(send_sem=send_sems.at[3 * a + j], recv_sem=recv_sems.at[3 * a + j], device_id_type=MESH)
            starts.append(pltpu.make_async_remote_copy(
                src_ref=srcs[a].at[mine], dst_ref=lands[a].at[me, mine], device_id=(*chip, c), **sems))
            slab = lands[a].at[2 * chip[0] + chip[1], mine]
            landing.append(pltpu.make_async_remote_copy(src_ref=slab, dst_ref=slab, device_id=(x, y, c), **sems))
    return starts, landing


HBM = pl.BlockSpec(memory_space=pltpu.HBM)
SEM = pl.BlockSpec(memory_space=pltpu.SEMAPHORE)


def _ici_start(copies_fn, srcs, land_shapes, *, name, after=()):
    n, na = len(srcs), len(after)

    def body(*refs):
        starts, _ = copies_fn(refs[:n], refs[n:2 * n], refs[2 * n + na], refs[2 * n + na + 1])
        for cp in starts:
            cp.start()
        refs[-1][...] = jnp.zeros_like(refs[-1])

    sems = pltpu.SemaphoreType.DMA((3 * n,))
    hbm = lambda s: pltpu.HBM(s.shape, s.dtype)
    lands = [pltpu.with_memory_space_constraint(lax.empty(s.shape, s.dtype), pltpu.HBM) for s in land_shapes]
    res = pl.pallas_call(
        body, name=name, in_specs=[HBM] * (2 * n) + [ANY] * na,
        out_specs=(SEM, SEM, *[HBM] * (2 * n), pl.BlockSpec(memory_space=pltpu.VMEM)),
        out_shape=(sems, sems, *[hbm(s) for s in srcs], *[hbm(s) for s in land_shapes],
                   jax.ShapeDtypeStruct((8, BLOCK), F32)),
        input_output_aliases={i: 2 + i for i in range(2 * n)},
        compiler_params=pltpu.CompilerParams(has_side_effects=pltpu.SideEffectType.DATAFLOW_SIDE_EFFECTING),
    )(*[pltpu.with_memory_space_constraint(s, pltpu.HBM) for s in srcs], *lands, *after)
    return res[0], res[1], list(res[2:2 + n]), list(res[2 + n:2 + 2 * n]), res[-1]


def _ici_wait(copies_fn, send_sems, recv_sems, srcs, lands, after, *, name):
    n = len(srcs)
    after = list(after) if isinstance(after, (list, tuple)) else [after]

    def body(*refs):
        starts, landing = copies_fn(refs[:n], refs[n:2 * n], refs[2 * n], refs[2 * n + 1])
        for cp in starts:
            cp.wait_send()
        for cp in landing:
            cp.wait_recv()

    hbm = lambda s: pltpu.HBM(s.shape, s.dtype)
    res = pl.pallas_call(
        body, name=name, in_specs=[*[HBM] * (2 * n), SEM, SEM, *[ANY] * len(after)], out_specs=[HBM] * (2 * n),
        out_shape=[*[hbm(s) for s in srcs], *[hbm(s) for s in lands]],
        input_output_aliases={i: i for i in range(2 * n)},
        compiler_params=pltpu.CompilerParams(has_side_effects=pltpu.SideEffectType.DATAFLOW_SIDE_EFFECTING),
    )(*srcs, *lands, send_sems, recv_sems, *after)
    return list(res[:n]), list(res[n:])


def _gather_d2d(shards, lands, tag):
    n = len(shards)

    def body(*refs):
        ins, outs = refs[:n], refs[2 * n:3 * n]
        send_sems, recv_sems = refs[3 * n], refs[3 * n + 1]
        x, y, c = _my_pos()
        me, sibling = 2 * x + y, (x, y, 1 - c)
        starts, landing = [], []
        for a in range(n):
            half = ins[a].shape[0] // 2
            mine, theirs = pl.ds(c * half, half), pl.ds((1 - c) * half, half)
            pairs = [(ins[a], outs[a].at[me], outs[a].at[me])]
            for chip in _other_chips(x, y):
                k = 2 * chip[0] + chip[1]
                pairs.append((outs[a].at[k, mine], outs[a].at[k, mine], outs[a].at[k, theirs]))
            for j, (src, dst, lands_here) in enumerate(pairs):
                sems = dict(send_sem=send_sems.at[4 * a + j], recv_sem=recv_sems.at[4 * a + j], device_id_type=MESH)
                starts.append(pltpu.make_async_remote_copy(src_ref=src, dst_ref=dst, device_id=sibling, **sems))
                landing.append(pltpu.make_async_remote_copy(src_ref=lands_here, dst_ref=lands_here, device_id=(x, y, c),
                                                            **sems))
        for cp in starts:
            cp.start()
        for cp in landing:
            cp.wait_recv()
        for cp in starts:
            cp.wait_send()

    return pl.pallas_call(
        body, name=f"gather_d2d_{tag}", in_specs=[ANY] * (2 * n), out_specs=[ANY] * n,
        out_shape=[jax.ShapeDtypeStruct(s.shape, s.dtype) for s in lands],
        input_output_aliases={n + a: a for a in range(n)},
        scratch_shapes=[pltpu.SemaphoreType.DMA((4 * n,)), pltpu.SemaphoreType.DMA((4 * n,))],
    )(*shards, *lands)


def _rs_add_chips(p32s, r16s, pos, *, name):
    n = len(p32s)
    _, H, C = p32s[0].shape
    tile = _rs_tile(H, C, n)
    nt = H // tile

    def body(pos_ref, *refs):
        for a in range(n):
            p_ref, r_ref = refs[a], refs[n + a]
            refs[2 * n + a][...] = ((p_ref[...] + r_ref[0].astype(F32)) + r_ref[1].astype(F32)) + r_ref[2].astype(F32)

    grid_spec = pltpu.PrefetchScalarGridSpec(
        num_scalar_prefetch=1, grid=(nt,),
        in_specs=[_bs((None, tile, C), lambda i, pos_ref: (pos_ref[0], i, 0))] * n
        + [_bs((3, tile, C), lambda i, pos_ref: (0, i, 0))] * n,
        out_specs=[_bs((tile, C), lambda i, pos_ref: (pos_ref[1] * nt + i, 0))] * n)
    return pl.pallas_call(
        body, name=name, grid_spec=grid_spec, out_shape=[jax.ShapeDtypeStruct((2 * H, C), F32)] * n,
        compiler_params=_cparams(("parallel",)),
    )(pos, *p32s, *r16s)


def _rs_join_rows(fs, tag):
    n = len(fs)

    def body(*refs):
        outs = refs[n:2 * n]
        send_sems, recv_sems = refs[2 * n], refs[2 * n + 1]
        x, y, c = _my_pos()
        for a in range(n):
            half = outs[a].shape[0] // 2
            mine = outs[a].at[pl.ds(c * half, half)]
            pltpu.make_async_remote_copy(src_ref=mine, dst_ref=mine, send_sem=send_sems.at[a],
                                         recv_sem=recv_sems.at[a], device_id=(x, y, 1 - c), device_id_type=MESH).start()
        for a in range(n):
            half = outs[a].shape[0] // 2
            pltpu.make_async_remote_copy(
                src_ref=outs[a].at[pl.ds(c * half, half)], dst_ref=outs[a].at[pl.ds((1 - c) * half, half)],
                send_sem=send_sems.at[a], recv_sem=recv_sems.at[a], device_id=(x, y, 1 - c), device_id_type=MESH).wait()

    return pl.pallas_call(
        body, name=f"rs_join_rows_{tag}", in_specs=[ANY] * n, out_specs=[ANY] * n,
        out_shape=[jax.ShapeDtypeStruct(f.shape, f.dtype) for f in fs],
        input_output_aliases={a: a for a in range(n)},
        scratch_shapes=[pltpu.SemaphoreType.DMA((n,)), pltpu.SemaphoreType.DMA((n,))],
    )(*fs)


def _pos_vector():
    x, y, c = _my_pos()
    return jnp.stack([2 * x + y, c]).astype(jnp.int32)


def _swap_copies(srcs, lands, send_sems, recv_sems):
    x, y, c = _my_pos()
    starts, landing = [], []
    for a in range(len(srcs)):
        half = srcs[a].shape[1] // 2
        sems = dict(send_sem=send_sems.at[3 * a], recv_sem=recv_sems.at[3 * a], device_id_type=MESH)
        starts.append(pltpu.make_async_remote_copy(
            src_ref=srcs[a].at[:, pl.ds((1 - c) * half, half)], dst_ref=lands[a], device_id=(x, y, 1 - c), **sems))
        landing.append(pltpu.make_async_remote_copy(src_ref=lands[a], dst_ref=lands[a], device_id=(x, y, c), **sems))
    return starts, landing


def _swap_land_shapes(gs):
    return [jax.ShapeDtypeStruct((N_CHIPS, g.shape[1] // 2, g.shape[2]), g.dtype) for g in gs]


def _same_shape_runs(arrays):
    runs, start = [], 0
    for i in range(1, len(arrays) + 1):
        if i == len(arrays) or arrays[i].shape != arrays[start].shape:
            runs.append((start, i))
            start = i
    return runs


def _rs_add_pairs(gs, r1, names, tag):
    pos = _pos_vector()
    out = []
    for a, b in _same_shape_runs(gs):
        out += _rs_add_pair(gs[a:b], r1[a:b], pos, name=f"rs_add_pair_{tag}_{names[a]}")
    return out


def _rs_pair_sums(gs, names, tag):
    return _rs_add_pairs(gs, _rs_swap_rows(gs, tag), names, tag)


def _rs_finish(pairs, r2, names, tag):
    pos = _pos_vector()
    p32s = [p[0] for p in pairs]
    fs = []
    for a, b in _same_shape_runs(p32s):
        fs += _rs_add_chips(p32s[a:b], r2[a:b], pos, name=f"rs_add_chips_{tag}_{names[a]}")
    return _rs_join_rows(fs, tag)


def _exchange_land_shapes(pairs):
    return [jax.ShapeDtypeStruct((3,) + p[1].shape[1:], p[1].dtype) for p in pairs]


def _allreduce_small(buf):
    R, W = buf.shape

    def body(b_ref, o_ref, gather, send_sems, recv_sems):
        x, y, c = _my_pos()
        me = 4 * x + 2 * y + c
        gather[me] = b_ref[...]
        cps = []
        for d in range(1, 8):
            peer = (x ^ (d >> 2), y ^ ((d >> 1) & 1), c ^ (d & 1))
            cps.append(pltpu.make_async_remote_copy(
                src_ref=b_ref, dst_ref=gather.at[me], send_sem=send_sems.at[d - 1], recv_sem=recv_sems.at[d - 1],
                device_id=peer, device_id_type=MESH))
        for cp in cps:
            cp.start()
        for d in range(1, 8):
            pltpu.make_async_remote_copy(
                src_ref=b_ref, dst_ref=gather.at[me ^ d], send_sem=send_sems.at[d - 1], recv_sem=recv_sems.at[d - 1],
                device_id=(x, y, c), device_id_type=MESH).wait_recv()
        for cp in cps:
            cp.wait_send()
        acc = gather[0]
        for d in range(1, 8):
            acc = acc + gather[d]
        o_ref[...] = acc

    vm = pl.BlockSpec(memory_space=pltpu.VMEM)
    return pl.pallas_call(
        body, name="allreduce_small", in_specs=[vm], out_specs=vm, out_shape=jax.ShapeDtypeStruct((R, W), F32),
        scratch_shapes=[pltpu.VMEM((8, R, W), F32), pltpu.SemaphoreType.DMA((7,)), pltpu.SemaphoreType.DMA((7,))],
    )(buf)


def _heads(a, h, d):
    return a.reshape(a.shape[0], h, d).transpose(1, 0, 2)


def _unheads(a):
    h, L, d = a.shape
    return a.transpose(1, 0, 2).reshape(L, h * d)


def _rope_tables():
    pos = jnp.maximum(jnp.arange(LP, dtype=F32) - PAD_ROWS, 0.0)
    inv_freq = 1.0 / (ROPE_THETA ** (jnp.arange(0, MLA_ROPE, 2, dtype=F32) / MLA_ROPE))
    ang = pos[:, None] * inv_freq[None, :]
    cos, sin = jnp.tile(jnp.cos(ang), (1, MLA_HEADS)), jnp.tile(jnp.sin(ang), (1, MLA_HEADS))
    return jnp.concatenate([cos, cos], axis=1), jnp.concatenate([-sin, sin], axis=1)


def _lane_pad(a, width=BLOCK):
    return jnp.pad(a, ((0, 0), (0, width - a.shape[1])))


def _pad_in_proj(w):
    sl = lambda start, size: w[:, start:start + size]
    return jnp.concatenate([
        sl(OC_Z, 512), sl(OC_XBC, 768), sl(OC_FQ, 256), sl(OC_FK, 256), sl(OC_FV, 256), sl(OC_CQ, 256), sl(OC_CKV, 128),
        _lane_pad(sl(OC_DT, SSD_HEADS)), _lane_pad(sl(OC_FR, FOX_HEADS)),
        jnp.tile(sl(OC_KR, ROPE_HALF), (1, MLA_HEADS)), jnp.tile(sl(OC_KR + ROPE_HALF, ROPE_HALF), (1, MLA_HEADS))], axis=1)


def _in_proj_grad_chunks(wp):
    rope = lambda start: wp[:, start:start + 64].reshape(wp.shape[0], MLA_HEADS, ROPE_HALF).sum(axis=1)
    segs = [(wp, PC_Z, 512), (wp, PC_XBC, 768), (wp, PC_DT, SSD_HEADS), (wp, PC_FQ, 256), (wp, PC_FK, 256),
            (wp, PC_FV, 256), (wp, PC_FR, FOX_HEADS), (wp, PC_CQ, 256), (wp, PC_CKV, 128),
            (rope(PC_KR), 0, ROPE_HALF), (rope(PC_KR + 64), 0, ROPE_HALF)]
    chunks = []
    for k in range(N_CHIPS):
        lo, hi, pos, pieces = k * IN_SHARD, (k + 1) * IN_SHARD, 0, []
        for arr, start, size in segs:
            a, b = max(lo, pos), min(hi, pos + size)
            if a < b:
                pieces.append(arr[:, start + a - pos:start + b - pos])
            pos += size
        pieces.append(jnp.zeros((wp.shape[0], IN_SHARD_P - IN_SHARD), wp.dtype))
        chunks.append(jnp.concatenate(pieces, axis=1))
    return jnp.stack(chunks)


def _regroup_uq(w):
    w3 = w.reshape(w.shape[0], MLA_HEADS, MLA_NOPE + MLA_ROPE)
    return jnp.concatenate([w3[:, :, :MLA_NOPE].reshape(w.shape[0], -1),
                            w3[:, :, MLA_NOPE:MLA_NOPE + ROPE_HALF].reshape(w.shape[0], -1),
                            w3[:, :, MLA_NOPE + ROPE_HALF:].reshape(w.shape[0], -1)], axis=1)


def _ungroup_uq(wp):
    n = wp.shape[0]
    return jnp.concatenate([wp[:, :256].reshape(n, MLA_HEADS, MLA_NOPE), wp[:, 256:320].reshape(n, MLA_HEADS, ROPE_HALF),
                            wp[:, 320:].reshape(n, MLA_HEADS, ROPE_HALF)], axis=2).reshape(n, -1)


def _regroup_ukv(w):
    w3 = w.reshape(w.shape[0], MLA_HEADS, MLA_NOPE + MLA_V)
    return jnp.concatenate([w3[:, :, :MLA_NOPE].reshape(w.shape[0], -1), w3[:, :, MLA_NOPE:].reshape(w.shape[0], -1)],
                           axis=1)


def _ungroup_ukv(wp):
    n = wp.shape[0]
    return jnp.concatenate([wp[:, :256].reshape(n, MLA_HEADS, MLA_NOPE), wp[:, 256:].reshape(n, MLA_HEADS, MLA_V)],
                           axis=2).reshape(n, -1)


TMF = 1088
N_IF = LP // TMF


def _chunk_rows_dx(g, w, l, chunk_h, *, name):
    N = w.shape[2]
    return _mm_core(g, w, a_spec=_bs((TMF, N), lambda i, j, k: (i, 0)),
                    b_spec=_bs((None, chunk_h, N), lambda i, j, k: (j, 0, 0)),
                    o_spec=_bs((TMF, chunk_h), lambda i, j, k: (i, j)), grid=(N_IF, N_CHIPS, 1),
                    out_shape=(LP, N_CHIPS * chunk_h), ca=1, cb=1, name=name)


def _chunk_rows_dw(a, g, chunk_h, *, name):
    N = g.shape[1]
    return _mm_core(a, g, a_spec=_bs((LP, chunk_h), lambda i, j, k: (0, i)), b_spec=_bs((LP, N), lambda i, j, k: (0, 0)),
                    o_spec=_bs((None, chunk_h, N), lambda i, j, k: (i, 0, 0)), grid=(N_CHIPS, 1, 1),
                    out_shape=(N_CHIPS, chunk_h, N), ca=0, cb=0, name=name)


def _ffn_up_swiglu(h, wg, wu, *, name):
    def body(h_ref, wg_ref, wu_ref, g_ref, u_ref, a_ref):
        hb = h_ref[...].astype(BF16)
        g = _raw_bdot(hb, wg_ref[...], 1, 1)
        u = _raw_bdot(hb, wu_ref[...], 1, 1)
        g_ref[...] = g
        u_ref[...] = u
        a_ref[...] = (_silu(g) * u).astype(a_ref.dtype)

    w_spec = _bs((None, HP, D_MODEL), lambda i, j: (j, 0, 0))
    o_spec = _bs((TMF, HP), lambda i, j: (i, j))
    return pl.pallas_call(
        body, name=name, grid=(N_IF, N_CHIPS), in_specs=[_bs((TMF, D_MODEL), lambda i, j: (i, 0)), w_spec, w_spec],
        out_specs=[o_spec] * 3,
        out_shape=[jax.ShapeDtypeStruct((LP, FP), F32), jax.ShapeDtypeStruct((LP, FP), F32),
                   jax.ShapeDtypeStruct((LP, FP), BF16)],
        compiler_params=_cparams(("parallel", "parallel")),
    )(h, wg, wu)


def _ffn_down_dx_swiglu(do, wd, g, u, *, name):
    def body(do_ref, wd_ref, g_ref, u_ref, dg_ref, du_ref):
        dact = _raw_bdot(do_ref[...], wd_ref[...], 1, 1)
        gv = g_ref[...]
        sig = _sigmoid(gv)
        dg_ref[...] = (dact * u_ref[...] * (sig * (1.0 + gv * (1.0 - sig)))).astype(dg_ref.dtype)
        du_ref[...] = (dact * (gv * sig)).astype(du_ref.dtype)

    blk = _bs((TMF, HP), lambda i, j: (i, j))
    return pl.pallas_call(
        body, name=name, grid=(N_IF, N_CHIPS),
        in_specs=[_bs((TMF, D_MODEL), lambda i, j: (i, 0)), _bs((None, HP, D_MODEL), lambda i, j: (j, 0, 0)), blk, blk],
        out_specs=[blk, blk], out_shape=[jax.ShapeDtypeStruct((LP, FP), BF16)] * 2,
        compiler_params=_cparams(("parallel", "parallel")),
    )(do, wd, g, u)


def _ffn_gate_up_dw(dg, du, h, *, name):
    def body(dg_ref, du_ref, h_ref, wg_ref, wu_ref):
        hb = h_ref[...].astype(BF16)
        wg_ref[...] = _raw_bdot(dg_ref[...], hb, 0, 0)
        wu_ref[...] = _raw_bdot(du_ref[...], hb, 0, 0)

    a_spec = _bs((LP, HP), lambda k: (0, k))
    o_spec = _bs((None, HP, D_MODEL), lambda k: (k, 0, 0))
    return pl.pallas_call(
        body, name=name, grid=(N_CHIPS,), in_specs=[a_spec, a_spec, _bs((LP, D_MODEL), lambda k: (0, 0))],
        out_specs=[o_spec, o_spec], out_shape=[jax.ShapeDtypeStruct((N_CHIPS, HP, D_MODEL), F32)] * 2,
        compiler_params=_cparams(("parallel",)),
    )(dg, du, h)


def _ffn_gate_up_dx(dg, du, wg, wu, add, *, name):
    def body(dg_ref, du_ref, wg_ref, wu_ref, add_ref, o_ref, acc_ref):
        k = pl.program_id(1)

        @pl.when(k == 0)
        def _():
            acc_ref[...] = jnp.zeros_like(acc_ref)

        acc_ref[...] += _raw_bdot(dg_ref[...], wg_ref[...], 1, 0) + _raw_bdot(du_ref[...], wu_ref[...], 1, 0)

        @pl.when(k == N_CHIPS - 1)
        def _():
            o_ref[...] = acc_ref[...] + add_ref[...]

    a_spec = _bs((TMF, HP), lambda i, k: (i, k))
    w_spec = _bs((None, HP, D_MODEL), lambda i, k: (k, 0, 0))
    o_spec = _bs((TMF, D_MODEL), lambda i, k: (i, 0))
    return pl.pallas_call(
        body, name=name, grid=(N_IF, N_CHIPS), in_specs=[a_spec, a_spec, w_spec, w_spec, o_spec], out_specs=o_spec,
        out_shape=jax.ShapeDtypeStruct((LP, D_MODEL), F32), scratch_shapes=[pltpu.VMEM((TMF, D_MODEL), F32)],
        compiler_params=_cparams(("parallel", "arbitrary")),
    )(dg, du, wg, wu, add)


def _chunk_rows_mm_res_ln(a, w, chunk_h, h, gam, bet, scale, *, name):
    res_ln = _make_res_ln_fn(scale)

    def body(a_ref, w_ref, h_ref, g_ref, b_ref, o_ref, y_ref, yb_ref, acc_ref):
        k = pl.program_id(1)

        @pl.when(k == 0)
        def _():
            acc_ref[...] = jnp.zeros_like(acc_ref)

        acc_ref[...] += _raw_bdot(a_ref[...], w_ref[...], 1, 0)

        @pl.when(k == N_CHIPS - 1)
        def _():
            o = acc_ref[...]
            o_ref[...] = o
            (y,) = res_ln(0, h_ref[...], o, g_ref[...], b_ref[...])
            y_ref[...] = y
            yb_ref[...] = y.astype(yb_ref.dtype)

    row = _bs((TMF, D_MODEL), lambda i, k: (i, 0))
    par = _bs((1, D_MODEL), lambda i, k: (0, 0))
    return pl.pallas_call(
        body, name=name, grid=(N_IF, N_CHIPS),
        in_specs=[_bs((TMF, chunk_h), lambda i, k: (i, k)), _bs((None, chunk_h, D_MODEL), lambda i, k: (k, 0, 0)), row,
                  par, par],
        out_specs=[row, row, row],
        out_shape=[jax.ShapeDtypeStruct((LP, D_MODEL), F32)] * 2 + [jax.ShapeDtypeStruct((LP, D_MODEL), BF16)],
        scratch_shapes=[pltpu.VMEM((TMF, D_MODEL), F32)], compiler_params=_cparams(("parallel", "arbitrary")),
    )(a, w, h, gam, bet)


def _ffn_fwd(hp, W, pre, l, gam, bet, tag):
    h, hb = hp
    g, u, act = _ffn_up_swiglu(hb, W[pre + "_w_gate"][l], W[pre + "_w_up"][l], name=f"{tag}_up_swiglu")
    o, out, outb = _chunk_rows_mm_res_ln(act, W[pre + "_w_down"][l], HP, h, gam, bet, 0.5, name=f"{tag}_down_ln")
    return (out, outb), (h, hb, g, u, act, o)


def _ffn_bwd(dout, saved, W, pre, l, gam, bet, GB, tag):
    h, hb, g, u, act, o = saved
    (dh_a, do), (dgam, dbet) = _rowwise_bwd(_make_res_ln_fn(0.5), [h, o], [gam, bet], [dout], name=f"{tag}_ln_bwd",
                                            tile=272, grad_dtypes=[F32, BF16])
    dg, du = _ffn_down_dx_swiglu(do, W[pre + "_w_down"][l], g, u, name=f"{tag}_down_dx_swiglu")
    GB[pre + "_w_down"] = _chunk_rows_dw(act, do, HP, name=f"{tag}_down_dw")
    GB[pre + "_w_gate"], GB[pre + "_w_up"] = _ffn_gate_up_dw(dg, du, hb, name=f"{tag}_gate_up_dw")
    dh = _ffn_gate_up_dx(dg, du, W[pre + "_w_gate"][l], W[pre + "_w_up"][l], dh_a, name=f"{tag}_gate_up_dx")
    return dh, dgam, dbet


def _mixer_fwd(hp1, W, l, cosf, sins):
    h1, h1b = hp1
    tag = f"l{l}"
    proj = _mm(h1b, W["w_in_p"][l], name=f"{tag}_in_proj")
    sv = {"h1": h1, "h1b": h1b, "proj": proj}
    conv_w, conv_b = W["conv_w"][l], W["conv_b"][l][None]
    xc = _conv_fwd(proj, PC_XBC // BLOCK, conv_w, conv_b, name=f"{tag}_conv")
    dt_bias = _lane_pad(W["dt_bias"][l][None])
    dtc, dtr = _ssd_dt_fwd(proj, PC_DT // BLOCK, dt_bias, name=f"{tag}_ssd_dt")
    xh = _heads(xc[:, :SSD_D], SSD_HEADS, SSD_HD)
    bm = _heads(xc[:, SSD_D:SSD_D + 128], SSD_GROUPS, SSD_STATE)
    cm = _heads(xc[:, SSD_D + 128:], SSD_GROUPS, SSD_STATE)
    alog = jnp.broadcast_to(W["a_log"][l][:, None, None], (SSD_HEADS, 1, BLOCK))
    yh, prevs = _ssd_fwd(xh, bm, cm, dtc, dtr, alog, name=f"{tag}_ssd")
    y_raw = _unheads(yh)
    dskip = jnp.repeat(W["d_skip"][l], SSD_HD)[None]
    normg = W["ssd_norm_g"][l][None]
    post_rows = [y_raw, (xc, 256, 0), (proj, 256, PC_Z // 256)]
    (y_ssd,) = _rowwise(_ssd_post_fn, post_rows, [dskip, normg], [SSD_D], name=f"{tag}_ssd_post", tile=272,
                        ncol=SSD_GROUPS)
    sv.update(conv_w=conv_w, conv_b=conv_b, dt_bias=dt_bias, xh=xh, bm=bm, cm=cm, dtc=dtc, dtr=dtr, alog=alog,
              prevs=prevs, post_rows=post_rows, dskip=dskip, normg=normg)
    f_b = _lane_pad(W["fox_f_b"][l][None])
    cg, cgt = _fox_gate_fwd(proj, PC_FR // BLOCK, f_b, name=f"{tag}_fox_gate")
    fox_qkv = ((proj, PC_FQ // ATT_W), (proj, PC_FK // ATT_W), (proj, PC_FV // ATT_W))
    y_fox, lse_f = _attn_fwd(*fox_qkv, scale=FOX_HD ** -0.5, name=f"{tag}_fox_attn", bias=(cg, cgt))
    sv.update(f_b=f_b, cg=cg, cgt=cgt, fox_qkv=fox_qkv, y_fox=y_fox, lse_f=lse_f)
    gq, gkv = W["mla_q_norm_g"][l][None], W["mla_kv_norm_g"][l][None]
    norm_rows = [(proj, 256, PC_CQ // 256), (proj, BLOCK, PC_CKV // BLOCK)]
    qn, cn = _rowwise(_mla_norm_fn, norm_rows, [gq, gkv], [MLA_Q_LORA, MLA_KV_LORA], name=f"{tag}_mla_norm", tile=272,
                      out_dtypes=[BF16, BF16])
    qh = _mm(qn, W["mla_w_uq_p"][l], name=f"{tag}_mla_uq")
    kvh = _mm(cn, W["mla_w_ukv_p"][l], name=f"{tag}_mla_ukv")
    qr, kr = _rowwise(_rope_fn, [(qh, BLOCK, 2), (proj, BLOCK, PC_KR // BLOCK), cosf, sins], [], [BLOCK, BLOCK],
                      name=f"{tag}_rope", tile=272)
    mla_qkv = ((qh, 0), (kvh, 0), (kvh, 1))
    y_mla, lse_m = _attn_fwd(*mla_qkv, scale=(MLA_NOPE + MLA_ROPE) ** -0.5, name=f"{tag}_mla_attn",
                             rope=((qr, 0), (kr, 0)))
    sv.update(gq=gq, gkv=gkv, norm_rows=norm_rows, qn=qn, cn=cn, qr=qr, kr=kr, mla_qkv=mla_qkv, y_mla=y_mla, lse_m=lse_m)
    ycat = jnp.concatenate([y_ssd, y_fox, y_mla], axis=1).astype(BF16)
    mix, h2, h2b = _chunk_rows_mm_res_ln(ycat, W["w_out"][l], 256, h1, W["ln2_g"][l][None], W["ln2_b"][l][None], 1.0,
                                    name=f"{tag}_out_proj_ln2")
    sv.update(mix=mix, ycat=ycat)
    return (h2, h2b), sv


def _mixer_bwd(dh2, sv, W, l, cosf, sins, GB, zero=0.0):
    tag = f"l{l}"
    G = {}
    proj = sv["proj"]
    ln2g, ln2b = W["ln2_g"][l][None] + zero, W["ln2_b"][l][None]
    (dh1_a, dmix), (dln2g, dln2b) = _rowwise_bwd(
        _make_res_ln_fn(1.0), [sv["h1"], sv["mix"]], [ln2g, ln2b], [dh2], name=f"{tag}_ln2_bwd", tile=272,
        grad_dtypes=[F32, BF16])
    G["ln2_g"], G["ln2_b"] = dln2g[0], dln2b[0]
    dycat = _chunk_rows_dx(dmix, W["w_out"][l], l, 256, name=f"{tag}_out_proj_dx")
    GB["w_out"] = _chunk_rows_dw(sv["ycat"], dmix, 256, name=f"{tag}_out_proj_dw")
    (dy_raw, dxs_a, dz), (ddskip, dnormg) = _rowwise_bwd(
        _ssd_post_fn, sv["post_rows"], [sv["dskip"], sv["normg"]], [dycat[:, :SSD_D]],
        name=f"{tag}_ssd_post_bwd", tile=272, ncol=SSD_GROUPS)
    G["ssd_norm_g"] = dnormg[0]
    G["d_skip"] = ddskip.reshape(SSD_HEADS, SSD_HD).sum(axis=1)
    dxh, dbm, dcm, ddtc, ddtr, dal = _ssd_bwd(sv["xh"], sv["bm"], sv["cm"], sv["dtc"], sv["dtr"], sv["alog"],
                                              sv["prevs"], _heads(dy_raw, SSD_HEADS, SSD_HD), name=f"{tag}_ssd_bwd")
    G["a_log"] = dal[:, 0, 0]
    dxc = jnp.concatenate([dxs_a + _unheads(dxh), _unheads(dbm), _unheads(dcm)], axis=1)
    dxbc, G["conv_w"], dconv_b = _conv_bwd(proj, PC_XBC // BLOCK, sv["conv_w"], sv["conv_b"], dxc,
                                           name=f"{tag}_conv_bwd")
    G["conv_b"] = dconv_b[0]
    ddt_raw, ddt_bias = _ssd_dt_bwd(proj, PC_DT // BLOCK, sv["dt_bias"], ddtc, ddtr, name=f"{tag}_ssd_dt_bwd")
    G["dt_bias"] = ddt_bias[0, :SSD_HEADS]
    dfq, dfk, dfv, dcg, dcgt = _attn_bwd(*sv["fox_qkv"], sv["y_fox"], sv["lse_f"], (dycat, SSD_D // ATT_W),
                                         scale=FOX_HD ** -0.5, name=f"{tag}_fox_attn_bwd", bias=(sv["cg"], sv["cgt"]))
    df_raw, dfb = _fox_gate_bwd(proj, PC_FR // BLOCK, sv["f_b"], dcg, dcgt, name=f"{tag}_fox_gate_bwd")
    G["fox_f_b"] = dfb[0, :FOX_HEADS]
    dqn_h, dkn_h, dv_h, dqr, dkr = _attn_bwd(
        *sv["mla_qkv"], sv["y_mla"], sv["lse_m"], (dycat, (SSD_D + FOX_D) // ATT_W),
        scale=(MLA_NOPE + MLA_ROPE) ** -0.5, name=f"{tag}_mla_attn_bwd", rope=((sv["qr"], 0), (sv["kr"], 0)))
    dq_rope, dk_rope = _rowwise(_rope_t_fn, [dqr, dkr, cosf, sins], [], [BLOCK, BLOCK], name=f"{tag}_rope_bwd",
                                tile=272)
    dqh = jnp.concatenate([dqn_h, dq_rope], axis=1).astype(BF16)
    dkvh = jnp.concatenate([dkn_h, dv_h], axis=1).astype(BF16)
    dqn = _mm(dqh, W["mla_w_uq_p"][l], tb=True, name=f"{tag}_mla_uq_dx")
    G["mla_w_uq_p"] = _mm(sv["qn"], dqh, ta=True, name=f"{tag}_mla_uq_dw")
    dcn = _mm(dkvh, W["mla_w_ukv_p"][l], tb=True, name=f"{tag}_mla_ukv_dx")
    G["mla_w_ukv_p"] = _mm(sv["cn"], dkvh, ta=True, name=f"{tag}_mla_ukv_dw")
    (dcq, dckv), (dgq, dgkv) = _rowwise_bwd(_mla_norm_fn, sv["norm_rows"], [sv["gq"], sv["gkv"]], [dqn, dcn],
                                            name=f"{tag}_mla_norm_bwd", tile=272)
    G["mla_q_norm_g"], G["mla_kv_norm_g"] = dgq[0], dgkv[0]
    dproj = jnp.concatenate([dz, dxbc, dfq, dfk, dfv, dcq, dckv, ddt_raw, df_raw, dk_rope], axis=1).astype(BF16)
    dh1 = _mm(dproj, W["w_in_p"][l], tb=True, add=dh1_a, name=f"{tag}_in_proj_dx")
    G["w_in_p"] = _mm(sv["h1b"], dproj, ta=True, name=f"{tag}_in_proj_dw")
    return dh1, G


def _embed(x, meta):
    return jnp.concatenate([jnp.zeros((PAD_ROWS, D_MODEL), F32), meta, x], axis=0)


def _layer_fwd(h, W, l, cosf, sins):
    ln = lambda n: W[n][l][None]
    h1, s1 = _ffn_fwd(h, W, "ffn1", l, ln("ln1_g"), ln("ln1_b"), f"l{l}_ffn1")
    h2, sm = _mixer_fwd(h1, W, l, cosf, sins)
    h3, s2 = _ffn_fwd(h2, W, "ffn2", l, ln("ln3_g"), ln("ln3_b"), f"l{l}_ffn2")
    return h3, (s1, sm, s2)


def _layer_bwd(dh, saved, W, l, cosf, sins):
    ln = lambda n: W[n][l][None]
    s1, sm, s2 = saved
    G = {}
    dh, dg, db = _ffn_bwd(dh, s2, W, "ffn2", l, ln("ln3_g"), ln("ln3_b"), G, f"l{l}_ffn2")
    G["ln3_g"], G["ln3_b"] = dg[0], db[0]
    dh, Gm = _mixer_bwd(dh, sm, W, l, cosf, sins, G)
    G.update(Gm)
    dh, dg, db = _ffn_bwd(dh, s1, W, "ffn1", l, ln("ln1_g"), ln("ln1_b"), G, f"l{l}_ffn1")
    G["ln1_g"], G["ln1_b"] = dg[0], db[0]
    return dh, G


def _local_step(x, target, W):
    h = _embed(x, W["meta"])
    h = (h, h.astype(BF16))
    tgt = jnp.concatenate([jnp.zeros((BLOCK, D_MODEL), F32), target], axis=0)
    cosf, sins = _rope_tables()
    saved = []
    for l in range(DEPTH):
        h, sv = _layer_fwd(h, W, l, cosf, sins)
        saved.append(sv)
    dh, loss = _loss_head(h[0], tgt, name="loss_head")
    grads = [None] * DEPTH
    for l in reversed(range(DEPTH)):
        dh, grads[l] = _layer_bwd(dh, saved[l], W, l, cosf, sins)
    return loss, dh, grads


WEIGHTS = ['meta', 'ffn1_w_gate', 'ffn1_w_up', 'ffn1_w_down', 'ln1_g', 'ln1_b', 'w_in', 'conv_w', 'conv_b', 'dt_bias',
           'a_log', 'd_skip', 'ssd_norm_g', 'fox_f_b', 'mla_q_norm_g', 'mla_w_uq', 'mla_kv_norm_g', 'mla_w_ukv',
           'w_out', 'ln2_g', 'ln2_b', 'ffn2_w_gate', 'ffn2_w_up', 'ffn2_w_down', 'ln3_g', 'ln3_b']
SMALL = ["ln1_g", "ln1_b", "conv_b", "dt_bias", "a_log", "d_skip", "ssd_norm_g", "fox_f_b", "mla_q_norm_g",
         "mla_kv_norm_g", "ln2_g", "ln2_b", "ln3_g", "ln3_b"]
MATMUL_W = ["ffn1_w_gate", "ffn1_w_up", "ffn1_w_down", "w_in", "mla_w_uq", "mla_w_ukv", "w_out", "ffn2_w_gate",
            "ffn2_w_up", "ffn2_w_down"]
SMALL_ROWS = 312


def _pad_to(a, axis, size):
    pads = [(0, 0)] * a.ndim
    pads[axis] = (0, size - a.shape[axis])
    return jnp.pad(a, pads)


def _chip_cols(full, chip, width):
    return lax.dynamic_slice_in_dim(full, chip * width, width, axis=full.ndim - 1)


def kernel(x, meta, ffn1_w_gate, ffn1_w_up, ffn1_w_down, ln1_g, ln1_b, w_in, conv_w, conv_b, dt_bias, a_log, d_skip, ssd_norm_g, fox_f_b, mla_q_norm_g, mla_w_uq, mla_kv_norm_g, mla_w_ukv, w_out, ln2_g, ln2_b, ffn2_w_gate, ffn2_w_up, ffn2_w_down, ln3_g, ln3_b, loss_target, m_meta, m_ffn1_w_gate, m_ffn1_w_up, m_ffn1_w_down, m_ln1_g, m_ln1_b, m_w_in, m_conv_w, m_conv_b, m_dt_bias, m_a_log, m_d_skip, m_ssd_norm_g, m_fox_f_b, m_mla_q_norm_g, m_mla_w_uq, m_mla_kv_norm_g, m_mla_w_ukv, m_w_out, m_ln2_g, m_ln2_b, m_ffn2_w_gate, m_ffn2_w_up, m_ffn2_w_down, m_ln3_g, m_ln3_b, v_meta, v_ffn1_w_gate, v_ffn1_w_up, v_ffn1_w_down, v_ln1_g, v_ln1_b, v_w_in, v_conv_w, v_conv_b, v_dt_bias, v_a_log, v_d_skip, v_ssd_norm_g, v_fox_f_b, v_mla_q_norm_g, v_mla_w_uq, v_mla_kv_norm_g, v_mla_w_ukv, v_w_out, v_ln2_g, v_ln2_b, v_ffn2_w_gate, v_ffn2_w_up, v_ffn2_w_down, v_ln3_g, v_ln3_b):
    args = dict(locals())
    w = {n: args[n] for n in WEIGHTS}
    m = {n: args["m_" + n] for n in WEIGHTS}
    v = {n: args["v_" + n] for n in WEIGHTS}
    xcoord, ycoord, _ = _my_pos()
    chip = 2 * xcoord + ycoord

    tr = lambda a: jnp.swapaxes(a, 1, 2)

    def bf16_shard(n, l, zero=None):
        a = w[n] if zero is None else w[n] + zero
        if n.endswith("w_gate") or n.endswith("w_up"):
            a = _pad_to(tr(a), 1, HP)
        elif n.endswith("w_down"):
            a = _pad_to(a, 1, HP)
        elif n == "w_in":
            a = _pad_to(a, 2, IN_SHARD_P)
        return a[l].astype(BF16)

    land_shape = lambda s: jax.ShapeDtypeStruct((N_CHIPS,) + s.shape, s.dtype)

    def gather_start(names, l, tag, after):
        srcs = [bf16_shard(n, l, None if after is None else after[0, 0]) for n in names]
        return _ici_start(_gather_copies, srcs, [land_shape(s) for s in srcs], name=f"gather_ici_{tag}_start",
                          after=[tiny[0]] if after is None else [after])

    def gather_finish(handle, names, l, tag, after):
        srcs, lands = _ici_wait(_gather_copies, *handle[:4], after, name=f"gather_ici_{tag}_wait")
        use_gathered(l, names, _gather_d2d(srcs, lands, tag))

    tiny = _allgather_chips([w["meta"].reshape(2, N_META // 2, D_MODEL // N_CHIPS), w["conv_w"]])
    meta_full = jnp.concatenate([tiny[0][k].reshape(N_META, D_MODEL // N_CHIPS) for k in range(N_CHIPS)], axis=1)

    W = {n: [None] * DEPTH for n in MATMUL_W + ["w_in_p", "mla_w_uq_p", "mla_w_ukv_p"]}
    W["conv_w"] = jnp.concatenate([tiny[1][k] for k in range(N_CHIPS)], axis=-1)
    W["meta"] = meta_full
    for n in SMALL:
        W[n] = w[n]

    def use_gathered(l, names, lands):
        got = dict(zip(names, lands))
        cat = lambda n, cut=None: jnp.concatenate([got[n][k][..., :cut] for k in range(N_CHIPS)], axis=-1)
        for n in names:
            W[n][l] = got[n]
        if "w_in" in got:
            W["w_in_p"][l] = _pad_in_proj(cat("w_in", IN_SHARD))
            W["mla_w_uq_p"][l] = _regroup_uq(cat("mla_w_uq"))
            W["mla_w_ukv_p"][l] = _regroup_ukv(cat("mla_w_ukv"))

    def chunk_grads(G, names):
        def chunked(name, ungroup, width, pad):
            full = ungroup(G[name])
            return _pad_to(jnp.moveaxis(full.reshape(full.shape[0], N_CHIPS, width), 1, 0), 2, pad)
        special = {"mla_w_uq": ("mla_w_uq_p", _ungroup_uq, MLA_NOPE + MLA_ROPE, MLA_NOPE + MLA_ROPE),
                   "mla_w_ukv": ("mla_w_ukv_p", _ungroup_ukv, MLA_NOPE + MLA_V, MLA_NOPE + MLA_V)}
        return [_in_proj_grad_chunks(G["w_in_p"]) if n == "w_in" else chunked(*special[n]) if n in special else G[n]
                for n in names]

    def rs_start(G, names, tag):
        pairs = _rs_pair_sums(chunk_grads(G, names), names, tag)
        handle = _ici_start(_exchange_copies, [p[1] for p in pairs], _exchange_land_shapes(pairs),
                            name=f"rs_exchange_{tag}_start")
        return pairs, handle

    def swap_start(G, names, tag):
        gs = chunk_grads(G, names)
        return _ici_start(_swap_copies, gs, _swap_land_shapes(gs), name=f"rs_swap_{tag}_start")

    def exchange_start(swap_handle, names, tag, after):
        gs, r1 = _ici_wait(_swap_copies, *swap_handle[:4], after, name=f"rs_swap_{tag}_wait")
        pairs = _rs_add_pairs(gs, r1, names, tag)
        handle = _ici_start(_exchange_copies, [p[1] for p in pairs], _exchange_land_shapes(pairs),
                            name=f"rs_exchange_{tag}_start")
        return pairs, handle

    def rs_end(pairs, handle, names, tag, after):
        _, r2 = _ici_wait(_exchange_copies, *handle[:4], after, name=f"rs_exchange_{tag}_wait")
        return dict(zip(names, _rs_finish(pairs, r2, names, tag)))

    ffn1_w, mix_w, ffn2_w = MATMUL_W[:3], MATMUL_W[3:7], MATMUL_W[7:]
    g_a = gather_start(ffn1_w, 0, "l0_ffn1", None)
    g_b = gather_start(mix_w, 0, "l0_mix", g_a[4])
    g_c = gather_start(ffn2_w, 0, "l0_ffn2", g_b[4])
    g_l1 = gather_start(MATMUL_W, 1, "l1", g_c[4])
    token = g_l1[4]
    cosf, sins = _rope_tables()
    ln = lambda n, l: W[n][l][None]
    h = _embed(x[0] + token[0, 0], meta_full)
    h = (h, h.astype(BF16))
    gather_finish(g_a, ffn1_w, 0, "l0_ffn1", h[1])
    h1, s1 = _ffn_fwd(h, W, "ffn1", 0, ln("ln1_g", 0), ln("ln1_b", 0), "l0_ffn1")
    gather_finish(g_b, mix_w, 0, "l0_mix", h1[1])
    h2, sm = _mixer_fwd(h1, W, 0, cosf, sins)
    gather_finish(g_c, ffn2_w, 0, "l0_ffn2", h2[1])
    h, s2 = _ffn_fwd(h2, W, "ffn2", 0, ln("ln3_g", 0), ln("ln3_b", 0), "l0_ffn2")
    saved0 = (s1, sm, s2)
    gather_finish(g_l1, MATMUL_W, 1, "l1", h[1])
    h, saved1 = _layer_fwd(h, W, 1, cosf, sins)
    tgt = jnp.concatenate([jnp.zeros((BLOCK, D_MODEL), F32), loss_target[0]], axis=0)
    dh, loss = _loss_head(h[0], tgt, name="loss_head")
    G = [None] * DEPTH
    dh, G[1] = _layer_bwd(dh, saved1, W, 1, cosf, sins)
    ffn2_w, mix_w, ffn1_w = MATMUL_W[7:], MATMUL_W[3:7], MATMUL_W[:3]
    sw_l1 = swap_start(G[1], MATMUL_W, "l1")
    G0 = {}
    dh, dg, db = _ffn_bwd(dh, s2, W, "ffn2", 0, ln("ln3_g", 0) + sw_l1[4][0, 0], ln("ln3_b", 0), G0, "l0_ffn2")
    G0["ln3_g"], G0["ln3_b"] = dg[0], db[0]
    pairs_l1, x_l1 = exchange_start(sw_l1, MATMUL_W, "l1", dh)
    sw_a = swap_start(G0, ffn2_w, "l0_ffn2")
    dh, Gm = _mixer_bwd(dh, sm, W, 0, cosf, sins, G0, zero=x_l1[4][0, 0] + sw_a[4][0, 0])
    G0.update(Gm)
    pairs_a, x_a = exchange_start(sw_a, ffn2_w, "l0_ffn2", dh)
    reduced1 = rs_end(pairs_l1, x_l1, MATMUL_W, "l1", dh)
    pairs_b, x_b = rs_start(G0, mix_w, "l0_mix")
    dh0, dg, db = _ffn_bwd(dh, s1, W, "ffn1", 0, ln("ln1_g", 0) + (x_a[4][0, 0] + x_b[4][0, 0]), ln("ln1_b", 0), G0,
                           "l0_ffn1")
    G0["ln1_g"], G0["ln1_b"] = dg[0], db[0]
    G[0] = G0
    reduced0 = rs_end(pairs_a, x_a, ffn2_w, "l0_ffn2", dh0)
    reduced0.update(rs_end(pairs_b, x_b, mix_w, "l0_mix", dh0))

    small_parts = [jnp.stack([G[l][n] for l in range(DEPTH)]).reshape(-1) for n in SMALL]
    small_parts += [jnp.stack([G[l]["conv_w"] for l in range(DEPTH)]).reshape(-1), dh0[PAD_ROWS:BLOCK].reshape(-1),
                    loss[0, :1]]
    flat = jnp.concatenate(small_parts)
    flat = jnp.pad(flat, (0, SMALL_ROWS * BLOCK - flat.shape[0]))
    red2d = _allreduce_small(flat.reshape(SMALL_ROWS, BLOCK))
    red = red2d.reshape(-1)

    pairs_c = _rs_pair_sums(chunk_grads(G0, ffn1_w), ffn1_w, "l0_ffn1")
    x_c = _ici_start(_exchange_copies, [p[1] for p in pairs_c], _exchange_land_shapes(pairs_c),
                     name="rs_exchange_l0_ffn1_start", after=[red2d])
    grads, off = {}, 0
    for n in SMALL:
        size = int(np.prod(w[n].shape))
        grads[n] = red[off:off + size].reshape(w[n].shape)
        off += size
    conv_full = red[off:off + DEPTH * SSD_CONV * 768].reshape(DEPTH, SSD_CONV, 768)
    off += DEPTH * SSD_CONV * 768
    dmeta_full = red[off:off + N_META * D_MODEL].reshape(N_META, D_MODEL)
    off += N_META * D_MODEL
    loss_out = red[off]
    grads["conv_w"] = _chip_cols(conv_full, chip, 768 // N_CHIPS)
    grads["meta"] = _chip_cols(dmeta_full, chip, D_MODEL // N_CHIPS)

    delta, new_m, new_v = {}, {}, {}

    def adamw_matmul_weights(names, after):
        done = []
        for n in names:
            gs = [reduced0[n], reduced1[n]]
            if n.endswith("w_gate") or n.endswith("w_up"):
                res = _adamw(tr(w[n]), gs, tr(m[n]), tr(v[n]), name=f"adamw_{n}", after=after)
                grads[n], delta[n], new_m[n], new_v[n] = [tr(r) for r in res]
            else:
                res = _adamw(w[n], gs, m[n], v[n], name=f"adamw_{n}", after=after)
                grads[n], delta[n], new_m[n], new_v[n] = res
            done.append(res[1])
        return done

    early_done = adamw_matmul_weights(ffn2_w + mix_w, [x_c[4]])
    rest = [n for n in WEIGHTS if n not in MATMUL_W]

    def pack_small(d):
        f = jnp.concatenate([d[n].reshape(-1) for n in rest])
        tot = -(-f.shape[0] // (8 * BLOCK)) * 8 * BLOCK
        return jnp.pad(f, (0, tot - f.shape[0])).reshape(-1, BLOCK)

    _, d2, m2, v2 = _adamw(pack_small(w), [pack_small(grads)], pack_small(m), pack_small(v), name="adamw_small",
                           after=[x_c[4]])
    reduced0.update(rs_end(pairs_c, x_c, ffn1_w, "l0_ffn1", [d2] + early_done))
    adamw_matmul_weights(ffn1_w, [])
    off = 0
    for n in rest:
        size = int(np.prod(w[n].shape))
        for dst, src in ((delta, d2), (new_m, m2), (new_v, v2)):
            dst[n] = src.reshape(-1)[off:off + size].reshape(w[n].shape)
        off += size

    grad_x = dh0[BLOCK:][None]
    return (loss_out, grad_x, *[grads[n] for n in WEIGHTS], *[delta[n] for n in WEIGHTS],
            *[new_m[n] for n in WEIGHTS], *[new_v[n] for n in WEIGHTS])
```

```python
import functools

import numpy as np
import jax
import jax.numpy as jnp
from jax import lax
from jax.experimental import pallas as pl
from jax.experimental.pallas import tpu as pltpu

F32 = jnp.float32
BF16 = jnp.bfloat16
MESH = pl.DeviceIdType.MESH

D_MODEL = 1024
SEQ = 2048
N_META = 16
BLOCK = 128
PAD_ROWS = 112
LP = PAD_ROWS + N_META + SEQ
N_CHUNK = LP // BLOCK
DEPTH = 2
D_FF = 2816
N_CHIPS = 4
FF_SHARD = D_FF // N_CHIPS
HP = 768
FP = N_CHIPS * HP
SSD_HEADS, SSD_HD, SSD_D, SSD_GROUPS, SSD_STATE, SSD_CONV = 8, 64, 512, 2, 64, 4
FOX_HEADS, FOX_HD, FOX_D = 4, 64, 256
MLA_HEADS, MLA_Q_LORA, MLA_KV_LORA, MLA_NOPE, MLA_ROPE, MLA_V, MLA_D = 4, 256, 128, 64, 32, 64, 256
ROPE_HALF = MLA_ROPE // 2
ROPE_THETA = 10000.0
N_IN = 2476
IN_SHARD = N_IN // N_CHIPS
IN_SHARD_P = 640
ALPHA = (2 * DEPTH) ** 0.25
EPS = 1e-5
ADAM_LR, ADAM_B1, ADAM_B2, ADAM_EPS, ADAM_WD, ADAM_STEP = 0.001, 0.9, 0.999, 1e-08, 0.01, 10
NEG = -1e30
TM = 544

VMEM_LIMIT_BYTES = 56 * 1024 * 1024

PC_Z, PC_XBC, PC_FQ, PC_FK, PC_FV, PC_CQ, PC_CKV, PC_DT, PC_FR, PC_KR, PC_END = (
    0, 512, 1280, 1536, 1792, 2048, 2304, 2432, 2560, 2688, 2816)
OC_Z, OC_XBC, OC_DT, OC_FQ, OC_FK, OC_FV, OC_FR, OC_CQ, OC_CKV, OC_KR = (
    0, 512, 1280, 1288, 1544, 1800, 2056, 2060, 2316, 2444)


def _cparams(sem=None):
    return pltpu.CompilerParams(dimension_semantics=sem, vmem_limit_bytes=VMEM_LIMIT_BYTES)


def _tile(n, cap, mult):
    best = None
    for t in range(mult, min(n, cap) + 1, mult):
        if n % t == 0:
            best = t
    return best if best is not None else n


def _bs(shape, fn):
    return pl.BlockSpec(shape, fn)


ANY = pl.BlockSpec(memory_space=pl.ANY)


def _dims(ca, cb):
    return (((ca,), (cb,)), ((), ()))


def _raw_bdot(a, b, ca, cb):
    return lax.dot_general(a.astype(BF16), b.astype(BF16), _dims(ca, cb), preferred_element_type=F32)


def _mm_core(a, b, *, a_spec, b_spec, o_spec, grid, out_shape, ca, cb, name, add=None, after=()):
    nk = grid[2]
    has_add = add is not None
    acc_shape = tuple(d for d in o_spec.block_shape if d is not None)

    def body(*refs):
        a_ref, b_ref = refs[0], refs[1]
        add_ref = refs[2] if has_add else None
        o_ref, acc_ref = refs[-2], refs[-1]
        k = pl.program_id(2)

        @pl.when(k == 0)
        def _():
            acc_ref[...] = jnp.zeros_like(acc_ref)

        acc_ref[...] += _raw_bdot(a_ref[...], b_ref[...], ca, cb)

        @pl.when(k == nk - 1)
        def _():
            r = acc_ref[...]
            if has_add:
                r = r + add_ref[...]
            o_ref[...] = r

    ins = [a, b] + ([add] if has_add else []) + list(after)
    in_specs = [a_spec, b_spec] + ([o_spec] if has_add else []) + [ANY] * len(after)
    return pl.pallas_call(
        body, name=name, grid=grid, in_specs=in_specs, out_specs=o_spec,
        out_shape=jax.ShapeDtypeStruct(out_shape, F32), scratch_shapes=[pltpu.VMEM(acc_shape, F32)],
        compiler_params=_cparams(("parallel", "parallel", "arbitrary")),
    )(*ins)


MM_VMEM_BUDGET = 40 * 1024 * 1024


def _divisors(n, mult):
    return [t for t in range(mult, n + 1, mult) if n % t == 0] or [n]


def _pick_tiles(M, N, K, a_bytes, b_bytes, ta, has_add):
    best = None
    for tm in _divisors(M, 128 if ta else 16):
        for tn in _divisors(N, 128):
            vmem = 2 * tm * K * a_bytes + 2 * K * tn * b_bytes + (3 + 2 * int(has_add)) * tm * tn * 4
            if vmem <= MM_VMEM_BUDGET:
                key = ((M // tm) * (N // tn), -tn)
                if best is None or key < best[0]:
                    best = (key, tm, tn)
    assert best is not None, (M, N, K)
    return best[1], best[2], K


def _mm(a, b, *, ta=False, tb=False, add=None, name, after=()):
    if ta:
        K, M = a.shape
    else:
        M, K = a.shape
    if tb:
        N, Kb = b.shape
    else:
        Kb, N = b.shape
    assert K == Kb, (a.shape, b.shape, ta, tb)
    tm, tn, tk = _pick_tiles(M, N, K, a.dtype.itemsize, b.dtype.itemsize, ta, add is not None)
    a_spec = _bs((tk, tm), lambda i, j, k: (k, i)) if ta else _bs((tm, tk), lambda i, j, k: (i, k))
    b_spec = _bs((tn, tk), lambda i, j, k: (j, k)) if tb else _bs((tk, tn), lambda i, j, k: (k, j))
    return _mm_core(a, b, a_spec=a_spec, b_spec=b_spec, o_spec=_bs((tm, tn), lambda i, j, k: (i, j)),
                    grid=(M // tm, N // tn, K // tk), out_shape=(M, N), ca=0 if ta else 1, cb=1 if tb else 0,
                    name=name, add=add, after=after)


def _row_entry(r, ncol):
    if isinstance(r, tuple):
        return r
    return r, r.shape[1] // ncol, 0


def _rowwise(fn, rows, pars, out_cols, *, name, tile, ncol=1, out_dtypes=None):
    rows = [_row_entry(r, ncol) for r in rows]
    L = rows[0][0].shape[0]
    nr, npar = len(rows), len(pars)
    in_specs = [_bs((tile, w), lambda g, i, o=o: (i, o + g)) for _, w, o in rows]
    in_specs += [_bs((p.shape[0], p.shape[1] // ncol), lambda g, i: (0, g)) for p in pars]
    out_specs = [_bs((tile, c // ncol), lambda g, i: (i, g)) for c in out_cols]

    def body(*refs):
        ins, outs = refs[:nr + npar], refs[nr + npar:]
        row0 = pl.program_id(1) * tile
        res = fn(row0, *[r[...] for r in ins])
        for o, v in zip(outs, res):
            o[...] = v.astype(o.dtype)

    return pl.pallas_call(
        body, name=name, grid=(ncol, L // tile), in_specs=in_specs, out_specs=out_specs,
        out_shape=[jax.ShapeDtypeStruct((L, c), d) for c, d in zip(out_cols, out_dtypes or [F32] * len(out_cols))],
        compiler_params=_cparams(("parallel", "parallel")),
    )(*[r[0] for r in rows], *pars)


def _rowwise_bwd(fn, rows, pars, douts, *, name, tile, ncol=1, row_grad=None, grad_dtypes=None):
    rows = [_row_entry(r, ncol) for r in rows]
    L = rows[0][0].shape[0]
    nr, npar, nd = len(rows), len(pars), len(douts)
    row_grad = [True] * nr if row_grad is None else row_grad
    in_specs = [_bs((tile, w), lambda g, i, o=o: (i, o + g)) for _, w, o in rows]
    in_specs += [_bs((p.shape[0], p.shape[1] // ncol), lambda g, i: (0, g)) for p in pars]
    in_specs += [_bs((tile, d.shape[1] // ncol), lambda g, i: (i, g)) for d in douts]
    g_widths = [w * ncol for (_, w, _), f in zip(rows, row_grad) if f]
    out_specs = [_bs((tile, w // ncol), lambda g, i: (i, g)) for w in g_widths]
    out_specs += [_bs((p.shape[0], p.shape[1] // ncol), lambda g, i: (0, g)) for p in pars]
    out_shape = [jax.ShapeDtypeStruct((L, w), d) for w, d in zip(g_widths, grad_dtypes or [F32] * len(g_widths))]
    out_shape += [jax.ShapeDtypeStruct(p.shape, F32) for p in pars]

    def body(*refs):
        ins = refs[:nr + npar]
        dos = refs[nr + npar:nr + npar + nd]
        outs = refs[nr + npar + nd:]
        i = pl.program_id(1)
        row0 = i * tile
        _, vjp = jax.vjp(lambda *a: tuple(fn(row0, *a)), *[r[...] for r in ins])
        grads = vjp(tuple(d[...].astype(F32) for d in dos))
        o = 0
        for j in range(nr):
            if row_grad[j]:
                outs[o][...] = grads[j].astype(outs[o].dtype)
                o += 1
        for j in range(npar):
            g, ref = grads[nr + j], outs[o + j]

            @pl.when(i == 0)
            def _(g=g, ref=ref):
                ref[...] = g

            @pl.when(i > 0)
            def _(g=g, ref=ref):
                ref[...] += g

    res = pl.pallas_call(
        body, name=name, grid=(ncol, L // tile), in_specs=in_specs, out_specs=out_specs, out_shape=out_shape,
        compiler_params=_cparams(("parallel", "arbitrary")),
    )(*[r[0] for r in rows], *pars, *douts)
    return res[:len(g_widths)], res[len(g_widths):]


def _sigmoid(x):
    return 1.0 / (1.0 + jnp.exp(-x))


def _softplus(x):
    return jnp.maximum(x, 0.0) + jnp.log(1.0 + jnp.exp(-jnp.abs(x)))


def _silu(x):
    return x * _sigmoid(x)


def _make_res_ln_fn(scale):
    def fn(row0, h, o, gam, bet):
        pre = ALPHA * h + scale * o
        mu = jnp.mean(pre, axis=-1, keepdims=True)
        xc = pre - mu
        var = jnp.mean(xc * xc, axis=-1, keepdims=True)
        return (xc * lax.rsqrt(var + EPS) * gam + bet,)
    return fn


def _ssd_post_fn(row0, y, xs, z, dskip, normg):
    v = (y + dskip * xs) * _silu(z)
    v = v * lax.rsqrt(jnp.mean(v * v, axis=-1, keepdims=True) + EPS)
    return (v * normg,)


def _mla_norm_fn(row0, cq, ckv, gq, gkv):
    qn = cq * lax.rsqrt(jnp.mean(cq * cq, axis=-1, keepdims=True) + EPS) * gq
    cn = ckv * lax.rsqrt(jnp.mean(ckv * ckv, axis=-1, keepdims=True) + EPS) * gkv
    return qn, cn


def _rope_fn(row0, q, k, cosf, sins):
    return (q * cosf + pltpu.roll(q, 64, 1) * sins, k * cosf + pltpu.roll(k, 64, 1) * sins)


def _rope_t_fn(row0, gq, gk, cosf, sins):
    return (gq * cosf + pltpu.roll(gq * sins, 64, 1), gk * cosf + pltpu.roll(gk * sins, 64, 1))


def _conv_fwd(x, x_off, w, b, *, name):
    C = w.shape[1]

    def body(x_ref, w_ref, b_ref, o_ref):
        rows = lax.broadcasted_iota(jnp.int32, (LP, BLOCK), 0)
        xv = jnp.where(rows >= PAD_ROWS, x_ref[...], 0.0)
        acc = b_ref[...] + w_ref[3:4, :] * xv
        for k in range(SSD_CONV - 1):
            acc = acc + w_ref[k:k + 1, :] * pltpu.roll(xv, SSD_CONV - 1 - k, 0)
        o_ref[...] = _silu(acc)

    return pl.pallas_call(
        body, name=name, grid=(C // BLOCK,),
        in_specs=[_bs((LP, BLOCK), lambda j: (0, j + x_off)), _bs((SSD_CONV, BLOCK), lambda j: (0, j)),
                  _bs((1, BLOCK), lambda j: (0, j))],
        out_specs=_bs((LP, BLOCK), lambda j: (0, j)),
        out_shape=jax.ShapeDtypeStruct((LP, C), F32), compiler_params=_cparams(("parallel",)),
    )(x, w, b)


def _conv_bwd(x, x_off, w, b, dout, *, name):
    C = w.shape[1]

    def body(x_ref, w_ref, b_ref, do_ref, dx_ref, dw_ref, db_ref):
        rows = lax.broadcasted_iota(jnp.int32, (LP, BLOCK), 0)
        real = rows >= PAD_ROWS
        xv = jnp.where(real, x_ref[...], 0.0)
        shifted = [pltpu.roll(xv, SSD_CONV - 1 - k, 0) for k in range(SSD_CONV - 1)] + [xv]
        acc = b_ref[...]
        for k in range(SSD_CONV):
            acc = acc + w_ref[k:k + 1, :] * shifted[k]
        sig = _sigmoid(acc)
        dacc = jnp.where(real, do_ref[...] * (sig * (1.0 + acc * (1.0 - sig))), 0.0)
        db_ref[...] = jnp.sum(dacc, axis=0, keepdims=True)
        dx = w_ref[3:4, :] * dacc
        for k in range(SSD_CONV):
            dw_ref[k:k + 1, :] = jnp.sum(dacc * shifted[k], axis=0, keepdims=True)
            if k < SSD_CONV - 1:
                dx = dx + w_ref[k:k + 1, :] * pltpu.roll(dacc, LP - (SSD_CONV - 1 - k), 0)
        dx_ref[...] = jnp.where(real, dx, 0.0)

    return pl.pallas_call(
        body, name=name, grid=(C // BLOCK,),
        in_specs=[_bs((LP, BLOCK), lambda j: (0, j + x_off)), _bs((SSD_CONV, BLOCK), lambda j: (0, j)),
                  _bs((1, BLOCK), lambda j: (0, j)), _bs((LP, BLOCK), lambda j: (0, j))],
        out_specs=[_bs((LP, BLOCK), lambda j: (0, j)), _bs((SSD_CONV, BLOCK), lambda j: (0, j)),
                   _bs((1, BLOCK), lambda j: (0, j))],
        out_shape=[jax.ShapeDtypeStruct((LP, C), F32), jax.ShapeDtypeStruct((SSD_CONV, C), F32),
                   jax.ShapeDtypeStruct((1, C), F32)],
        compiler_params=_cparams(("parallel",)),
    )(x, w, b, dout)


_BDIMS = {"nn": (((2,), (1,)), ((0,), (0,))), "nt": (((2,), (2,)), ((0,), (0,))), "tn": (((1,), (1,)), ((0,), (0,)))}


def _raw_bdot3(a, b, mode):
    return lax.dot_general(a.astype(BF16), b.astype(BF16), _BDIMS[mode], preferred_element_type=F32)


@functools.partial(jax.custom_vjp, nondiff_argnums=(2,))
def _bdot3(a, b, mode):
    return _raw_bdot3(a, b, mode)


def _bdot3_fwd(a, b, mode):
    return _raw_bdot3(a, b, mode), (a, b)


def _bdot3_bwd(mode, res, g):
    a, b = res
    if mode == "nn":
        return _raw_bdot3(g, b, "nt"), _raw_bdot3(a, g, "tn")
    if mode == "nt":
        return _raw_bdot3(g, b, "nn"), _raw_bdot3(g, a, "tn")
    return _raw_bdot3(b, g, "nt"), _raw_bdot3(a, g, "nn")


_bdot3.defvjp(_bdot3_fwd, _bdot3_bwd)


def _ssd_chunk(x, bm, cm, dt, dtt, alog, prev):
    rep = SSD_HEADS // SSD_GROUPS
    per_head = lambda t: jnp.broadcast_to(t[:, None], (SSD_GROUPS, rep) + t.shape[1:]).reshape((SSD_HEADS,) + t.shape[1:])
    bm, cm = per_head(bm), per_head(cm)
    lane_h = lax.broadcasted_iota(jnp.int32, (1, BLOCK), 1)
    row_h = lax.broadcasted_iota(jnp.int32, (BLOCK, 1), 0)
    dtc = jnp.stack([jnp.sum(jnp.where(lane_h == h, dt, 0.0), axis=1, keepdims=True) for h in range(SSD_HEADS)])
    dtr = jnp.stack([jnp.sum(jnp.where(row_h == h, dtt, 0.0), axis=0, keepdims=True) for h in range(SSD_HEADS)])
    lane = lax.broadcasted_iota(jnp.int32, alog.shape, 2)
    a_neg = -jnp.exp(jnp.sum(jnp.where(lane == 0, alog, 0.0), axis=2, keepdims=True))
    ac_in = dtc * a_neg
    ar_in = dtr * a_neg
    li = lax.broadcasted_iota(jnp.int32, (1, BLOCK, BLOCK), 1)
    si = lax.broadcasted_iota(jnp.int32, (1, BLOCK, BLOCK), 2)
    causal = li >= si
    acum_c = jnp.sum(jnp.where(causal, ar_in, 0.0), axis=2, keepdims=True)
    acum_r = jnp.sum(jnp.where(li <= si, ac_in, 0.0), axis=1, keepdims=True)
    total = jnp.sum(ar_in, axis=2, keepdims=True)
    seg = jnp.exp(jnp.where(causal, acum_c - acum_r, NEG))
    xdt = x * dtc
    cb = _bdot3(cm, bm, "nt")
    y = _bdot3(cb * seg, xdt, "nn") + _bdot3(cm, prev, "nt") * jnp.exp(acum_c)
    st = _bdot3(xdt, bm * jnp.exp(total - acum_c), "tn")
    return y, prev * jnp.exp(total) + st


def _ssd_dt_fwd(raw, raw_blk, bias, *, name):
    def body(raw_ref, b_ref, dt_ref, dtt_ref):
        rows = pl.program_id(0) * BLOCK + lax.broadcasted_iota(jnp.int32, (BLOCK, BLOCK), 0)
        dt = jnp.where(rows >= PAD_ROWS, _softplus(raw_ref[...] + b_ref[...]), 0.0)
        dt_ref[...] = dt
        dtt_ref[...] = dt.T

    return pl.pallas_call(
        body, name=name, grid=(N_CHUNK,),
        in_specs=[_bs((BLOCK, BLOCK), lambda j: (j, raw_blk)), _bs((1, BLOCK), lambda j: (0, 0))],
        out_specs=[_bs((BLOCK, BLOCK), lambda j: (j, 0)), _bs((BLOCK, BLOCK), lambda j: (0, j))],
        out_shape=[jax.ShapeDtypeStruct((LP, BLOCK), F32), jax.ShapeDtypeStruct((BLOCK, LP), F32)],
        compiler_params=_cparams(("parallel",)),
    )(raw, bias)


def _ssd_dt_bwd(raw, raw_blk, bias, ddt, ddtt, *, name):
    def body(raw_ref, b_ref, ddt_ref, ddtt_ref, draw_ref, db_ref):
        j = pl.program_id(0)
        rows = j * BLOCK + lax.broadcasted_iota(jnp.int32, (BLOCK, BLOCK), 0)
        g = ddt_ref[...] + ddtt_ref[...].T
        draw = jnp.where(rows >= PAD_ROWS, g * _sigmoid(raw_ref[...] + b_ref[...]), 0.0)
        draw_ref[...] = draw
        dsum = jnp.sum(draw, axis=0, keepdims=True)

        @pl.when(j == 0)
        def _():
            db_ref[...] = dsum

        @pl.when(j > 0)
        def _():
            db_ref[...] += dsum

    return pl.pallas_call(
        body, name=name, grid=(N_CHUNK,),
        in_specs=[_bs((BLOCK, BLOCK), lambda j: (j, raw_blk)), _bs((1, BLOCK), lambda j: (0, 0)),
                  _bs((BLOCK, BLOCK), lambda j: (j, 0)), _bs((BLOCK, BLOCK), lambda j: (0, j))],
        out_specs=[_bs((BLOCK, BLOCK), lambda j: (j, 0)), _bs((1, BLOCK), lambda j: (0, 0))],
        out_shape=[jax.ShapeDtypeStruct((LP, BLOCK), F32), jax.ShapeDtypeStruct((1, BLOCK), F32)],
        compiler_params=_cparams(("arbitrary",)),
    )(raw, bias, ddt, ddtt)


def _ssd_specs(rev):
    ci = (lambda c: N_CHUNK - 1 - c) if rev else (lambda c: c)
    x_spec = _bs((SSD_HEADS, BLOCK, SSD_HD), lambda c: (0, ci(c), 0))
    g_spec = _bs((SSD_GROUPS, BLOCK, SSD_STATE), lambda c: (0, ci(c), 0))
    dtc_spec = _bs((BLOCK, BLOCK), lambda c: (ci(c), 0))
    dtr_spec = _bs((BLOCK, BLOCK), lambda c: (0, ci(c)))
    al_spec = _bs((SSD_HEADS, 1, BLOCK), lambda c: (0, 0, 0))
    st_spec = _bs((None, SSD_HEADS, SSD_HD, SSD_STATE), lambda c: (ci(c), 0, 0, 0))
    return x_spec, g_spec, dtc_spec, dtr_spec, al_spec, st_spec


def _ssd_fwd(x, bm, cm, dtc, dtr, alog, *, name):
    x_spec, g_spec, dtc_spec, dtr_spec, al_spec, st_spec = _ssd_specs(False)

    def body(x_ref, b_ref, c_ref, dtc_ref, dtr_ref, al_ref, y_ref, prev_ref, state):
        @pl.when(pl.program_id(0) == 0)
        def _():
            state[...] = jnp.zeros_like(state)

        prev = state[...]
        prev_ref[...] = prev
        y, new = _ssd_chunk(x_ref[...], b_ref[...], c_ref[...], dtc_ref[...], dtr_ref[...], al_ref[...], prev)
        y_ref[...] = y
        state[...] = new

    return pl.pallas_call(
        body, name=name, grid=(N_CHUNK,),
        in_specs=[x_spec, g_spec, g_spec, dtc_spec, dtr_spec, al_spec], out_specs=[x_spec, st_spec],
        out_shape=[jax.ShapeDtypeStruct((SSD_HEADS, LP, SSD_HD), F32),
                   jax.ShapeDtypeStruct((N_CHUNK, SSD_HEADS, SSD_HD, SSD_STATE), F32)],
        scratch_shapes=[pltpu.VMEM((SSD_HEADS, SSD_HD, SSD_STATE), F32)],
        compiler_params=_cparams(("arbitrary",)),
    )(x, bm, cm, dtc, dtr, alog)


def _ssd_bwd(x, bm, cm, dtc, dtr, alog, prevs, dy, *, name):
    x_spec, g_spec, dtc_spec, dtr_spec, al_spec, st_spec = _ssd_specs(True)

    def body(x_ref, b_ref, c_ref, dtc_ref, dtr_ref, al_ref, prev_ref, dy_ref,
             dx_ref, db_ref, dc_ref, ddtc_ref, ddtr_ref, dal_ref, dstate):
        c = pl.program_id(0)

        @pl.when(c == 0)
        def _():
            dstate[...] = jnp.zeros_like(dstate)

        _, vjp = jax.vjp(_ssd_chunk, x_ref[...], b_ref[...], c_ref[...], dtc_ref[...], dtr_ref[...], al_ref[...],
                         prev_ref[...])
        dx, db, dc, ddtc, ddtr, dal, dprev = vjp((dy_ref[...], dstate[...]))
        dx_ref[...] = dx
        db_ref[...] = db
        dc_ref[...] = dc
        ddtc_ref[...] = ddtc
        ddtr_ref[...] = ddtr
        dstate[...] = dprev

        @pl.when(c == 0)
        def _():
            dal_ref[...] = dal

        @pl.when(c > 0)
        def _():
            dal_ref[...] += dal

    hs = jax.ShapeDtypeStruct((SSD_HEADS, LP, SSD_HD), F32)
    gs = jax.ShapeDtypeStruct((SSD_GROUPS, LP, SSD_STATE), F32)
    return pl.pallas_call(
        body, name=name, grid=(N_CHUNK,),
        in_specs=[x_spec, g_spec, g_spec, dtc_spec, dtr_spec, al_spec, st_spec, x_spec],
        out_specs=[x_spec, g_spec, g_spec, dtc_spec, dtr_spec, al_spec],
        out_shape=[hs, gs, gs, jax.ShapeDtypeStruct((LP, BLOCK), F32),
                   jax.ShapeDtypeStruct((BLOCK, LP), F32), jax.ShapeDtypeStruct((SSD_HEADS, 1, BLOCK), F32)],
        scratch_shapes=[pltpu.VMEM((SSD_HEADS, SSD_HD, SSD_STATE), F32)],
        compiler_params=_cparams(("arbitrary",)),
    )(x, bm, cm, dtc, dtr, alog, prevs, dy)


def _tri_dot(tri, v):
    hi = v.astype(BF16)
    r1 = v - hi.astype(F32)
    mid = r1.astype(BF16)
    lo = (r1 - mid.astype(F32)).astype(BF16)
    t = tri.astype(BF16)
    d = lambda p: lax.dot_general(t, p, _dims(1, 0), preferred_element_type=F32)
    return d(hi) + d(mid) + d(lo)


def _fox_gate_fwd(raw, raw_blk, bias, *, name):
    def body(raw_ref, b_ref, c_ref, ct_ref, carry):
        j = pl.program_id(0)

        @pl.when(j == 0)
        def _():
            carry[...] = jnp.zeros_like(carry)

        rows = j * BLOCK + lax.broadcasted_iota(jnp.int32, (BLOCK, BLOCK), 0)
        lf = jnp.where(rows >= PAD_ROWS, -_softplus(-(raw_ref[...] + b_ref[...])), 0.0)
        li = lax.broadcasted_iota(jnp.int32, (BLOCK, BLOCK), 0)
        si = lax.broadcasted_iota(jnp.int32, (BLOCK, BLOCK), 1)
        cv = _tri_dot(jnp.where(li >= si, 1.0, 0.0), lf) + carry[...]
        c_ref[...] = cv
        ct_ref[...] = cv.T
        carry[...] += jnp.sum(lf, axis=0, keepdims=True)

    return pl.pallas_call(
        body, name=name, grid=(N_CHUNK,),
        in_specs=[_bs((BLOCK, BLOCK), lambda j: (j, raw_blk)), _bs((1, BLOCK), lambda j: (0, 0))],
        out_specs=[_bs((BLOCK, BLOCK), lambda j: (j, 0)), _bs((BLOCK, BLOCK), lambda j: (0, j))],
        out_shape=[jax.ShapeDtypeStruct((LP, BLOCK), F32), jax.ShapeDtypeStruct((BLOCK, LP), F32)],
        scratch_shapes=[pltpu.VMEM((1, BLOCK), F32)], compiler_params=_cparams(("arbitrary",)),
    )(raw, bias)


def _fox_gate_bwd(raw, raw_blk, bias, dc, dct, *, name):
    rj = lambda j: N_CHUNK - 1 - j

    def body(raw_ref, b_ref, dc_ref, dct_ref, draw_ref, db_ref, carry):
        j = pl.program_id(0)

        @pl.when(j == 0)
        def _():
            carry[...] = jnp.zeros_like(carry)

        rows = (N_CHUNK - 1 - j) * BLOCK + lax.broadcasted_iota(jnp.int32, (BLOCK, BLOCK), 0)
        li = lax.broadcasted_iota(jnp.int32, (BLOCK, BLOCK), 0)
        si = lax.broadcasted_iota(jnp.int32, (BLOCK, BLOCK), 1)
        dcv = dc_ref[...] + dct_ref[...].T
        dlf = _tri_dot(jnp.where(li <= si, 1.0, 0.0), dcv) + carry[...]
        carry[...] += jnp.sum(dcv, axis=0, keepdims=True)
        draw = jnp.where(rows >= PAD_ROWS, dlf * (1.0 - _sigmoid(raw_ref[...] + b_ref[...])), 0.0)
        draw_ref[...] = draw
        dsum = jnp.sum(draw, axis=0, keepdims=True)

        @pl.when(j == 0)
        def _():
            db_ref[...] = dsum

        @pl.when(j > 0)
        def _():
            db_ref[...] += dsum

    return pl.pallas_call(
        body, name=name, grid=(N_CHUNK,),
        in_specs=[_bs((BLOCK, BLOCK), lambda j: (rj(j), raw_blk)), _bs((1, BLOCK), lambda j: (0, 0)),
                  _bs((BLOCK, BLOCK), lambda j: (rj(j), 0)), _bs((BLOCK, BLOCK), lambda j: (0, rj(j)))],
        out_specs=[_bs((BLOCK, BLOCK), lambda j: (rj(j), 0)), _bs((1, BLOCK), lambda j: (0, 0))],
        out_shape=[jax.ShapeDtypeStruct((LP, BLOCK), F32), jax.ShapeDtypeStruct((1, BLOCK), F32)],
        scratch_shapes=[pltpu.VMEM((1, BLOCK), F32)], compiler_params=_cparams(("arbitrary",)),
    )(raw, bias, dc, dct)


ATT_W = 256
ATT_QB = 272
ATT_STEPS = LP // ATT_QB
ATT_KEYS = (640, 1152, 1664, LP)
ATT_BLOCKS_PER_CLASS = ATT_STEPS // len(ATT_KEYS)


def _lane_head(width, per, mod=None):
    lane = lax.broadcasted_iota(jnp.int32, (1, width), 1)
    if mod is not None:
        lane = lane % mod
    return lane // per


def _attn_mask(i, kw):
    r = i * ATT_QB + lax.broadcasted_iota(jnp.int32, (ATT_QB, kw), 0)
    c = lax.broadcasted_iota(jnp.int32, (ATT_QB, kw), 1)
    return (c <= r) & ((c >= PAD_ROWS) | (r < PAD_ROWS))


def _attn_by_key_class(i, fn):
    for p, kw in enumerate(ATT_KEYS):
        @pl.when(i // ATT_BLOCKS_PER_CLASS == p)
        def _(kw=kw):
            fn(kw)


def _attn_specs(q, k, v, bias, rope):
    qspec = lambda blk, w=ATT_W: _bs((ATT_QB, w), lambda i: (i, blk))
    fspec = lambda blk, w=ATT_W: _bs((LP, w), lambda i: (0, blk))
    ins = [q[0], k[0], v[0]]
    specs = [qspec(q[1]), fspec(k[1]), fspec(v[1])]
    if bias is not None:
        ins += [bias[0], bias[1]]
        specs += [qspec(0, BLOCK), _bs((BLOCK, LP), lambda i: (0, 0))]
    if rope is not None:
        ins += [rope[0][0], rope[1][0]]
        specs += [qspec(rope[0][1], BLOCK), fspec(rope[1][1], BLOCK)]
    return ins, specs, qspec, fspec


def _attn_fwd(q, k, v, *, scale, name, bias=None, rope=None):
    ins, specs, qspec, fspec = _attn_specs(q, k, v, bias, rope)
    has_bias, has_rope = bias is not None, rope is not None

    def body(*refs):
        it = iter(refs)
        q_ref, k_ref, v_ref = next(it), next(it), next(it)
        if has_bias:
            c_ref, ct_ref = next(it), next(it)
        if has_rope:
            qr_ref, kr_ref = next(it), next(it)
        o_ref, lse_ref = next(it), next(it)
        i = pl.program_id(0)

        def block(kw):
            ok = _attn_mask(i, kw)
            qv, kv, vv = q_ref[...].astype(BF16), k_ref[0:kw, :].astype(BF16), v_ref[0:kw, :].astype(BF16)
            hid, l128 = _lane_head(ATT_W, FOX_HD), _lane_head(BLOCK, 1)
            if has_rope:
                rid = _lane_head(BLOCK, ROPE_HALF, 64)
                qrv, krv = qr_ref[...].astype(BF16), kr_ref[0:kw, :].astype(BF16)
            def head(h, carry):
                o_acc, lse_acc = carry
                s = _raw_bdot(jnp.where(hid == h, qv, 0.0), kv, 1, 1)
                if has_rope:
                    s = s + _raw_bdot(jnp.where(rid == h, qrv, 0.0), krv, 1, 1)
                s = s * scale
                if has_bias:
                    cq = jnp.sum(jnp.where(l128 == h, c_ref[...], 0.0), axis=1, keepdims=True)
                    s = s + (cq - ct_ref[pl.ds(h, 1), 0:kw])
                s = jnp.where(ok, s, NEG)
                m = jnp.max(s, axis=1, keepdims=True)
                p = jnp.exp(s - m)
                l = jnp.sum(p, axis=1, keepdims=True)
                o_acc = jnp.where(hid == h, _raw_bdot(p, vv, 1, 0) / l, o_acc)
                lse_acc = jnp.where(l128 == h, m + jnp.log(l), lse_acc)
                return o_acc, lse_acc

            o_acc, lse_acc = lax.fori_loop(
                0, FOX_HEADS, head, (jnp.zeros((ATT_QB, ATT_W), F32), jnp.zeros((ATT_QB, BLOCK), F32)), unroll=True)
            o_ref[...] = o_acc
            lse_ref[...] = lse_acc

        _attn_by_key_class(i, block)

    return pl.pallas_call(
        body, name=name, grid=(ATT_STEPS,), in_specs=specs, out_specs=[qspec(0), qspec(0, BLOCK)],
        out_shape=[jax.ShapeDtypeStruct((LP, ATT_W), F32), jax.ShapeDtypeStruct((LP, BLOCK), F32)],
        compiler_params=_cparams(("parallel",)),
    )(*ins)


def _attn_bwd(q, k, v, o, lse, do, *, scale, name, bias=None, rope=None):
    ins, specs, qspec, fspec = _attn_specs(q, k, v, bias, rope)
    has_bias, has_rope = bias is not None, rope is not None
    ins += [o, lse, do[0]]
    specs += [qspec(0), qspec(0, BLOCK), qspec(do[1])]

    def body(*refs):
        it = iter(refs)
        q_ref, k_ref, v_ref = next(it), next(it), next(it)
        if has_bias:
            c_ref, ct_ref = next(it), next(it)
        if has_rope:
            qr_ref, kr_ref = next(it), next(it)
        o_ref, lse_ref, do_ref = next(it), next(it), next(it)
        dq_ref, dk_ref, dv_ref = next(it), next(it), next(it)
        if has_bias:
            dc_ref, dct_ref = next(it), next(it)
        if has_rope:
            dqr_ref, dkr_ref = next(it), next(it)
        i = pl.program_id(0)

        @pl.when(i == 0)
        def _():
            dk_ref[...] = jnp.zeros_like(dk_ref)
            dv_ref[...] = jnp.zeros_like(dv_ref)
            if has_rope:
                dkr_ref[...] = jnp.zeros_like(dkr_ref)
            if has_bias:
                dct_ref[...] = jnp.zeros_like(dct_ref)

        def block(kw):
            ok = _attn_mask(i, kw)
            qv, kv, vv = q_ref[...].astype(BF16), k_ref[0:kw, :].astype(BF16), v_ref[0:kw, :].astype(BF16)
            dov, lsev = do_ref[...], lse_ref[...]
            dov_ov = dov * o_ref[...]
            dov = dov.astype(BF16)
            hid, l128 = _lane_head(ATT_W, FOX_HD), _lane_head(BLOCK, 1)
            if has_rope:
                rid = _lane_head(BLOCK, ROPE_HALF, 64)
                qrv, krv = qr_ref[...].astype(BF16), kr_ref[0:kw, :].astype(BF16)

            def head(h, carry):
                dq_acc, aux_acc = carry
                qm = jnp.where(hid == h, qv, 0.0)
                s = _raw_bdot(qm, kv, 1, 1)
                if has_rope:
                    qrm = jnp.where(rid == h, qrv, 0.0)
                    s = s + _raw_bdot(qrm, krv, 1, 1)
                s = s * scale
                if has_bias:
                    cq = jnp.sum(jnp.where(l128 == h, c_ref[...], 0.0), axis=1, keepdims=True)
                    s = s + (cq - ct_ref[pl.ds(h, 1), 0:kw])
                s = jnp.where(ok, s, NEG)
                p = jnp.exp(s - jnp.sum(jnp.where(l128 == h, lsev, 0.0), axis=1, keepdims=True))
                dom = jnp.where(hid == h, dov, 0.0)
                dp = _raw_bdot(dom, vv, 1, 1)
                delta = jnp.sum(jnp.where(hid == h, dov_ov, 0.0), axis=1, keepdims=True)
                ds = p * (dp - delta)
                dsb, pb = ds.astype(BF16), p.astype(BF16)
                dq_acc = jnp.where(hid == h, _raw_bdot(dsb, kv, 1, 0) * scale, dq_acc)
                dk_ref[0:kw, :] += _raw_bdot(dsb, qm, 0, 0) * scale
                dv_ref[0:kw, :] += _raw_bdot(pb, dom, 0, 0)
                if has_rope:
                    aux_acc = jnp.where(rid == h, _raw_bdot(dsb, krv, 1, 0) * scale, aux_acc)
                    dkr_ref[0:kw, :] += _raw_bdot(dsb, qrm, 0, 0) * scale
                if has_bias:
                    aux_acc = jnp.where(l128 == h, jnp.sum(ds, axis=1, keepdims=True), aux_acc)
                    dct_ref[pl.ds(h, 1), 0:kw] -= jnp.sum(ds, axis=0, keepdims=True)
                return dq_acc, aux_acc

            dq_acc, aux_acc = lax.fori_loop(
                0, FOX_HEADS, head, (jnp.zeros((ATT_QB, ATT_W), F32), jnp.zeros((ATT_QB, BLOCK), F32)))
            dq_ref[...] = dq_acc
            if has_bias:
                dc_ref[...] = aux_acc
            if has_rope:
                dqr_ref[...] = aux_acc

        _attn_by_key_class(i, block)

    wide = jax.ShapeDtypeStruct((LP, ATT_W), F32)
    narrow = jax.ShapeDtypeStruct((LP, BLOCK), F32)
    out_specs = [qspec(0), fspec(0), fspec(0)]
    out_shape = [wide, wide, wide]
    if has_bias:
        out_specs += [qspec(0, BLOCK), _bs((BLOCK, LP), lambda i: (0, 0))]
        out_shape += [narrow, jax.ShapeDtypeStruct((BLOCK, LP), F32)]
    if has_rope:
        out_specs += [qspec(0, BLOCK), fspec(0, BLOCK)]
        out_shape += [narrow, narrow]
    return pl.pallas_call(
        body, name=name, grid=(ATT_STEPS,), in_specs=specs, out_specs=out_specs, out_shape=out_shape,
        compiler_params=_cparams(("arbitrary",)),
    )(*ins)


def _loss_head(y, target, *, name):
    tile = 272

    def body(y_ref, t_ref, dy_ref, loss_ref):
        i = pl.program_id(0)
        rows = i * tile + lax.broadcasted_iota(jnp.int32, (tile, D_MODEL), 0)
        err = jnp.where(rows >= BLOCK, y_ref[...] - t_ref[...], 0.0)
        dy_ref[...] = err * (1.0 / D_MODEL)
        part = 0.5 * jnp.sum(jnp.sum(err * err, axis=1, keepdims=True) * (1.0 / D_MODEL), axis=0, keepdims=True)
        part = jnp.broadcast_to(part, (1, BLOCK))

        @pl.when(i == 0)
        def _():
            loss_ref[...] = part

        @pl.when(i > 0)
        def _():
            loss_ref[...] += part

    return pl.pallas_call(
        body, name=name, grid=(LP // tile,),
        in_specs=[_bs((tile, D_MODEL), lambda i: (i, 0)), _bs((tile, D_MODEL), lambda i: (i, 0))],
        out_specs=[_bs((tile, D_MODEL), lambda i: (i, 0)), _bs((1, BLOCK), lambda i: (0, 0))],
        out_shape=[jax.ShapeDtypeStruct((LP, D_MODEL), F32), jax.ShapeDtypeStruct((1, BLOCK), F32)],
        compiler_params=_cparams(("arbitrary",)),
    )(y, target)


def _adamw(w, gs, m, v, *, name, after=()):
    if w.ndim == 2:
        w, m, v = w[None], m[None], v[None]
        squeeze = True
    else:
        squeeze = False
    NL, R, C = w.shape
    assert len(gs) == NL
    CG = gs[0].shape[1]
    tile = _tile(R, 256, 8)

    def body(*refs):
        w_ref, g_refs = refs[0], refs[1:1 + NL]
        m_ref, v_ref = refs[1 + NL:3 + NL]
        go_ref, d_ref, nm_ref, nv_ref = refs[3 + NL + len(after):]
        gv = g_refs[0][:, :C]
        for j in range(1, NL):
            gv = jnp.where(pl.program_id(0) == j, g_refs[j][:, :C], gv)
        nm = ADAM_B1 * m_ref[...] + (1.0 - ADAM_B1) * gv
        nv = ADAM_B2 * v_ref[...] + (1.0 - ADAM_B2) * (gv * gv)
        m_hat = nm / (1.0 - ADAM_B1 ** ADAM_STEP)
        v_hat = nv / (1.0 - ADAM_B2 ** ADAM_STEP)
        go_ref[...] = gv
        d_ref[...] = -ADAM_LR * (m_hat / (jnp.sqrt(v_hat) + ADAM_EPS) + ADAM_WD * w_ref[...])
        nm_ref[...] = nm
        nv_ref[...] = nv

    spec = _bs((None, tile, C), lambda l, i: (l, i, 0))
    gspecs = [_bs((tile, CG), lambda l, i, j=j: (jnp.where(l == j, i, 0), 0)) for j in range(NL)]
    res = pl.pallas_call(
        body, name=name, grid=(NL, R // tile), in_specs=[spec, *gspecs, spec, spec, *[ANY] * len(after)],
        out_specs=[spec] * 4, out_shape=[jax.ShapeDtypeStruct((NL, R, C), F32)] * 4,
        compiler_params=_cparams(("parallel", "parallel")),
    )(w, *gs, m, v, *after)
    return [r[0] for r in res] if squeeze else res


def _my_pos():
    return lax.axis_index("x"), lax.axis_index("y"), lax.axis_index("c")


def _other_chips(x, y):
    return [(1 - x, y), (x, 1 - y), (1 - x, 1 - y)]


def _allgather_chips(shards):
    n = len(shards)
    per = 7

    def body(*refs):
        ins, outs = refs[:n], refs[n:2 * n]
        send_sems, recv_sems = refs[2 * n], refs[2 * n + 1]
        x, y, c = _my_pos()
        chips = _other_chips(x, y)
        sibling, me = (x, y, 1 - c), 2 * x + y

        def cp(a, kk, src, dst, to):
            return pltpu.make_async_remote_copy(src_ref=src, dst_ref=dst, send_sem=send_sems.at[per * a + kk],
                                                recv_sem=recv_sems.at[per * a + kk], device_id=to, device_id_type=MESH)

        sends = []
        for a in range(n):
            for j, chip in enumerate(chips):
                sends.append(cp(a, j, ins[a].at[c], outs[a].at[me, c], (*chip, c)))
            sends.append(cp(a, 3, ins[a], outs[a].at[me], sibling))
        for s in sends:
            s.start()
        for a in range(n):
            for j, chip in enumerate(chips):
                slab = outs[a].at[2 * chip[0] + chip[1], c]
                cp(a, j, slab, slab, (x, y, c)).wait_recv()
                fwd = cp(a, 4 + j, slab, slab, sibling)
                fwd.start()
                sends.append(fwd)
        for a in range(n):
            cp(a, 3, ins[a], outs[a].at[me], (x, y, c)).wait_recv()
            for j, chip in enumerate(chips):
                slab = outs[a].at[2 * chip[0] + chip[1], 1 - c]
                cp(a, 4 + j, slab, slab, (x, y, c)).wait_recv()
        for s in sends:
            s.wait_send()

    return pl.pallas_call(
        body, name="allgather_chips", in_specs=[ANY] * n, out_specs=[ANY] * n,
        out_shape=[jax.ShapeDtypeStruct((N_CHIPS,) + s.shape, s.dtype) for s in shards],
        scratch_shapes=[pltpu.SemaphoreType.DMA((per * n,)), pltpu.SemaphoreType.DMA((per * n,))],
    )(*shards)


def _rs_swap_rows(gs, tag):
    n = len(gs)

    def body(*refs):
        ins, outs = refs[:n], refs[n:2 * n]
        send_sems, recv_sems = refs[2 * n], refs[2 * n + 1]
        x, y, c = _my_pos()
        cps = []
        for a in range(n):
            half = ins[a].shape[1] // 2
            cps.append(pltpu.make_async_remote_copy(
                src_ref=ins[a].at[:, pl.ds((1 - c) * half, half)], dst_ref=outs[a], send_sem=send_sems.at[a],
                recv_sem=recv_sems.at[a], device_id=(x, y, 1 - c), device_id_type=MESH))
        for cp in cps:
            cp.start()
        for cp in cps:
            cp.wait()

    return pl.pallas_call(
        body, name=f"rs_swap_rows_{tag}", in_specs=[ANY] * n, out_specs=[ANY] * n,
        out_shape=[jax.ShapeDtypeStruct((N_CHIPS, g.shape[1] // 2, g.shape[2]), g.dtype) for g in gs],
        scratch_shapes=[pltpu.SemaphoreType.DMA((n,)), pltpu.SemaphoreType.DMA((n,))],
    )(*gs)


RS_ADD_VMEM_BYTES = 24 * 1024 * 1024


def _rs_tile(H, C, n):
    return _tile(H, max(16, RS_ADD_VMEM_BYTES // (28 * n * C)), 16)


def _rs_add_pair(gs, rs, pos, *, name):
    n = len(gs)
    _, H, C = rs[0].shape
    tile = _rs_tile(H, C, n)
    nt = H // tile

    def body(pos_ref, *refs):
        for a in range(n):
            s = refs[a][...] + refs[n + a][...]
            refs[2 * n + 2 * a][...] = s
            refs[2 * n + 2 * a + 1][...] = s.astype(BF16)

    spec = _bs((None, tile, C), lambda k, i, pos_ref: (k, i, 0))
    g_spec = _bs((None, tile, C), lambda k, i, pos_ref: (k, pos_ref[1] * nt + i, 0))
    grid_spec = pltpu.PrefetchScalarGridSpec(
        num_scalar_prefetch=1, grid=(N_CHIPS, nt), in_specs=[g_spec] * n + [spec] * n, out_specs=[spec] * (2 * n))
    res = pl.pallas_call(
        body, name=name, grid_spec=grid_spec,
        out_shape=[jax.ShapeDtypeStruct((N_CHIPS, H, C), F32), jax.ShapeDtypeStruct((N_CHIPS, H, C), BF16)] * n,
        compiler_params=_cparams(("parallel", "parallel")),
    )(pos, *gs, *rs)
    return [(res[2 * a], res[2 * a + 1]) for a in range(n)]


def _exchange_copies(srcs, lands, send_sems, recv_sems):
    x, y, c = _my_pos()
    starts, landing = [], []
    for a in range(len(srcs)):
        for j, chip in enumerate(_other_chips(x, y)):
            sems = dict(send_sem=send_sems.at[3 * a + j], recv_sem=recv_sems.at[3 * a + j], device_id_type=MESH)
            starts.append(pltpu.make_async_remote_copy(
                src_ref=srcs[a].at[2 * chip[0] + chip[1]], dst_ref=lands[a].at[j], device_id=(*chip, c), **sems))
            landing.append(pltpu.make_async_remote_copy(
                src_ref=lands[a].at[j], dst_ref=lands[a].at[j], device_id=(x, y, c), **sems))
    return starts, landing


def _gather_copies(srcs, lands, send_sems, recv_sems):
    x, y, c = _my_pos()
    me = 2 * x + y
    starts, landing = [], []
    for a in range(len(srcs)):
        half = srcs[a].shape[0] // 2
        mine = pl.ds(c * half, half)
        for j, chip in enumerate(_other_chips(x, y)):
            sems = dict(send_sem=send_sems.at[3 * a + j], recv_sem=recv_sems.at[3 * a + j], device_id_type=MESH)
            starts.append(pltpu.make_async_remote_copy(
                src_ref=srcs[a].at[mine], dst_ref=lands[a].at[me, mine], device_id=(*chip, c), **sems))
            slab = lands[a].at[2 * chip[0] + chip[1], mine]
            landing.append(pltpu.make_async_remote_copy(src_ref=slab, dst_ref=slab, device_id=(x, y, c), **sems))
    return starts, landing


HBM = pl.BlockSpec(memory_space=pltpu.HBM)
SEM = pl.BlockSpec(memory_space=pltpu.SEMAPHORE)


def _ici_start(copies_fn, srcs, land_shapes, *, name, after=(), sems_per_array=3):
    n, na = len(srcs), len(after)

    def body(*refs):
        starts, _ = copies_fn(refs[:n], refs[n:2 * n], refs[2 * n + na], refs[2 * n + na + 1])
        for cp in starts:
            cp.start()
        refs[-1][...] = jnp.zeros_like(refs[-1])

    sems = pltpu.SemaphoreType.DMA((sems_per_array * n,))
    hbm = lambda s: pltpu.HBM(s.shape, s.dtype)
    lands = [pltpu.with_memory_space_constraint(
        lax.empty(s.shape, s.dtype) if isinstance(s, jax.ShapeDtypeStruct) else s, pltpu.HBM) for s in land_shapes]
    res = pl.pallas_call(
        body, name=name, in_specs=[HBM] * (2 * n) + [ANY] * na,
        out_specs=(SEM, SEM, *[HBM] * (2 * n), pl.BlockSpec(memory_space=pltpu.VMEM)),
        out_shape=(sems, sems, *[hbm(s) for s in srcs], *[hbm(s) for s in land_shapes],
                   jax.ShapeDtypeStruct((8, BLOCK), F32)),
        input_output_aliases={i: 2 + i for i in range(2 * n)},
        compiler_params=pltpu.CompilerParams(has_side_effects=pltpu.SideEffectType.DATAFLOW_SIDE_EFFECTING),
    )(*[pltpu.with_memory_space_constraint(s, pltpu.HBM) for s in srcs], *lands, *after)
    return res[0], res[1], list(res[2:2 + n]), list(res[2 + n:2 + 2 * n]), res[-1]


def _ici_wait(copies_fn, send_sems, recv_sems, srcs, lands, after, *, name):
    n = len(srcs)
    after = list(after) if isinstance(after, (list, tuple)) else [after]

    def body(*refs):
        starts, landing = copies_fn(refs[:n], refs[n:2 * n], refs[2 * n], refs[2 * n + 1])
        for cp in starts:
            cp.wait_send()
        for cp in landing:
            cp.wait_recv()

    hbm = lambda s: pltpu.HBM(s.shape, s.dtype)
    res = pl.pallas_call(
        body, name=name, in_specs=[*[HBM] * (2 * n), SEM, SEM, *[ANY] * len(after)], out_specs=[HBM] * (2 * n),
        out_shape=[*[hbm(s) for s in srcs], *[hbm(s) for s in lands]],
        input_output_aliases={i: i for i in range(2 * n)},
        compiler_params=pltpu.CompilerParams(has_side_effects=pltpu.SideEffectType.DATAFLOW_SIDE_EFFECTING),
    )(*srcs, *lands, send_sems, recv_sems, *after)
    return list(res[:n]), list(res[n:])


D2D_COPIES = 4


def _d2d_copies(ins, outs, send_sems, recv_sems):
    x, y, c = _my_pos()
    me, sibling = 2 * x + y, (x, y, 1 - c)
    starts, landing = [], []
    for a in range(len(ins)):
        half = ins[a].shape[0] // 2
        mine, theirs = pl.ds(c * half, half), pl.ds((1 - c) * half, half)
        pairs = [(ins[a], outs[a].at[me], outs[a].at[me])]
        for chip in _other_chips(x, y):
            k = 2 * chip[0] + chip[1]
            pairs.append((outs[a].at[k, mine], outs[a].at[k, mine], outs[a].at[k, theirs]))
        for j, (src, dst, lands_here) in enumerate(pairs):
            sems = dict(send_sem=send_sems.at[D2D_COPIES * a + j], recv_sem=recv_sems.at[D2D_COPIES * a + j],
                        device_id_type=MESH)
            starts.append(pltpu.make_async_remote_copy(src_ref=src, dst_ref=dst, device_id=sibling, **sems))
            landing.append(pltpu.make_async_remote_copy(src_ref=lands_here, dst_ref=lands_here, device_id=(x, y, c),
                                                        **sems))
    return starts, landing


def _gather_d2d(shards, lands, tag):
    n = len(shards)

    def body(*refs):
        starts, landing = _d2d_copies(refs[:n], refs[2 * n:3 * n], refs[3 * n], refs[3 * n + 1])
        for cp in starts:
            cp.start()
        for cp in landing:
            cp.wait_recv()
        for cp in starts:
            cp.wait_send()

    return pl.pallas_call(
        body, name=f"gather_d2d_{tag}", in_specs=[ANY] * (2 * n), out_specs=[ANY] * n,
        out_shape=[jax.ShapeDtypeStruct(s.shape, s.dtype) for s in lands],
        input_output_aliases={n + a: a for a in range(n)},
        scratch_shapes=[pltpu.SemaphoreType.DMA((D2D_COPIES * n,)), pltpu.SemaphoreType.DMA((D2D_COPIES * n,))],
    )(*shards, *lands)


def _rs_add_chips(p32s, r16s, pos, *, name):
    n = len(p32s)
    _, H, C = p32s[0].shape
    tile = _rs_tile(H, C, n)
    nt = H // tile

    def body(pos_ref, *refs):
        for a in range(n):
            p_ref, r_ref = refs[a], refs[n + a]
            refs[2 * n + a][...] = ((p_ref[...] + r_ref[0].astype(F32)) + r_ref[1].astype(F32)) + r_ref[2].astype(F32)

    grid_spec = pltpu.PrefetchScalarGridSpec(
        num_scalar_prefetch=1, grid=(nt,),
        in_specs=[_bs((None, tile, C), lambda i, pos_ref: (pos_ref[0], i, 0))] * n
        + [_bs((3, tile, C), lambda i, pos_ref: (0, i, 0))] * n,
        out_specs=[_bs((tile, C), lambda i, pos_ref: (pos_ref[1] * nt + i, 0))] * n)
    return pl.pallas_call(
        body, name=name, grid_spec=grid_spec, out_shape=[jax.ShapeDtypeStruct((2 * H, C), F32)] * n,
        compiler_params=_cparams(("parallel",)),
    )(pos, *p32s, *r16s)


def _rs_join_rows(fs, tag):
    n = len(fs)

    def body(*refs):
        outs = refs[n:2 * n]
        send_sems, recv_sems = refs[2 * n], refs[2 * n + 1]
        x, y, c = _my_pos()
        for a in range(n):
            half = outs[a].shape[0] // 2
            mine = outs[a].at[pl.ds(c * half, half)]
            pltpu.make_async_remote_copy(src_ref=mine, dst_ref=mine, send_sem=send_sems.at[a],
                                         recv_sem=recv_sems.at[a], device_id=(x, y, 1 - c), device_id_type=MESH).start()
        for a in range(n):
            half = outs[a].shape[0] // 2
            pltpu.make_async_remote_copy(
                src_ref=outs[a].at[pl.ds(c * half, half)], dst_ref=outs[a].at[pl.ds((1 - c) * half, half)],
                send_sem=send_sems.at[a], recv_sem=recv_sems.at[a], device_id=(x, y, 1 - c), device_id_type=MESH).wait()

    return pl.pallas_call(
        body, name=f"rs_join_rows_{tag}", in_specs=[ANY] * n, out_specs=[ANY] * n,
        out_shape=[jax.ShapeDtypeStruct(f.shape, f.dtype) for f in fs],
        input_output_aliases={a: a for a in range(n)},
        scratch_shapes=[pltpu.SemaphoreType.DMA((n,)), pltpu.SemaphoreType.DMA((n,))],
    )(*fs)


def _pos_vector():
    x, y, c = _my_pos()
    return jnp.stack([2 * x + y, c]).astype(jnp.int32)


def _swap_copies(srcs, lands, send_sems, recv_sems):
    x, y, c = _my_pos()
    starts, landing = [], []
    for a in range(len(srcs)):
        half = srcs[a].shape[1] // 2
        sems = dict(send_sem=send_sems.at[3 * a], recv_sem=recv_sems.at[3 * a], device_id_type=MESH)
        starts.append(pltpu.make_async_remote_copy(
            src_ref=srcs[a].at[:, pl.ds((1 - c) * half, half)], dst_ref=lands[a], device_id=(x, y, 1 - c), **sems))
        landing.append(pltpu.make_async_remote_copy(src_ref=lands[a], dst_ref=lands[a], device_id=(x, y, c), **sems))
    return starts, landing


def _swap_land_shapes(gs):
    return [jax.ShapeDtypeStruct((N_CHIPS, g.shape[1] // 2, g.shape[2]), g.dtype) for g in gs]


def _same_shape_runs(arrays):
    runs, start = [], 0
    for i in range(1, len(arrays) + 1):
        if i == len(arrays) or arrays[i].shape != arrays[start].shape:
            runs.append((start, i))
            start = i
    return runs


def _rs_add_pairs(gs, r1, names, tag):
    pos = _pos_vector()
    out = []
    for a, b in _same_shape_runs(gs):
        out += _rs_add_pair(gs[a:b], r1[a:b], pos, name=f"rs_add_pair_{tag}_{names[a]}")
    return out


def _rs_pair_sums(gs, names, tag):
    return _rs_add_pairs(gs, _rs_swap_rows(gs, tag), names, tag)


def _rs_finish(pairs, r2, names, tag):
    pos = _pos_vector()
    p32s = [p[0] for p in pairs]
    fs = []
    for a, b in _same_shape_runs(p32s):
        fs += _rs_add_chips(p32s[a:b], r2[a:b], pos, name=f"rs_add_chips_{tag}_{names[a]}")
    return _rs_join_rows(fs, tag)


def _exchange_land_shapes(pairs):
    return [jax.ShapeDtypeStruct((3,) + p[1].shape[1:], p[1].dtype) for p in pairs]


def _allreduce_small(buf):
    R, W = buf.shape

    def body(b_ref, o_ref, gather, send_sems, recv_sems):
        x, y, c = _my_pos()
        me = 4 * x + 2 * y + c
        gather[me] = b_ref[...]
        cps = []
        for d in range(1, 8):
            peer = (x ^ (d >> 2), y ^ ((d >> 1) & 1), c ^ (d & 1))
            cps.append(pltpu.make_async_remote_copy(
                src_ref=b_ref, dst_ref=gather.at[me], send_sem=send_sems.at[d - 1], recv_sem=recv_sems.at[d - 1],
                device_id=peer, device_id_type=MESH))
        for cp in cps:
            cp.start()
        for d in range(1, 8):
            pltpu.make_async_remote_copy(
                src_ref=b_ref, dst_ref=gather.at[me ^ d], send_sem=send_sems.at[d - 1], recv_sem=recv_sems.at[d - 1],
                device_id=(x, y, c), device_id_type=MESH).wait_recv()
        for cp in cps:
            cp.wait_send()
        acc = gather[0]
        for d in range(1, 8):
            acc = acc + gather[d]
        o_ref[...] = acc

    vm = pl.BlockSpec(memory_space=pltpu.VMEM)
    return pl.pallas_call(
        body, name="allreduce_small", in_specs=[vm], out_specs=vm, out_shape=jax.ShapeDtypeStruct((R, W), F32),
        scratch_shapes=[pltpu.VMEM((8, R, W), F32), pltpu.SemaphoreType.DMA((7,)), pltpu.SemaphoreType.DMA((7,))],
    )(buf)


def _heads(a, h, d):
    return a.reshape(a.shape[0], h, d).transpose(1, 0, 2)


def _unheads(a):
    h, L, d = a.shape
    return a.transpose(1, 0, 2).reshape(L, h * d)


def _rope_tables():
    pos = jnp.maximum(jnp.arange(LP, dtype=F32) - PAD_ROWS, 0.0)
    inv_freq = 1.0 / (ROPE_THETA ** (jnp.arange(0, MLA_ROPE, 2, dtype=F32) / MLA_ROPE))
    ang = pos[:, None] * inv_freq[None, :]
    cos, sin = jnp.tile(jnp.cos(ang), (1, MLA_HEADS)), jnp.tile(jnp.sin(ang), (1, MLA_HEADS))
    return jnp.concatenate([cos, cos], axis=1), jnp.concatenate([-sin, sin], axis=1)


def _lane_pad(a, width=BLOCK):
    return jnp.pad(a, ((0, 0), (0, width - a.shape[1])))


def _pad_in_proj(w):
    sl = lambda start, size: w[:, start:start + size]
    return jnp.concatenate([
        sl(OC_Z, 512), sl(OC_XBC, 768), sl(OC_FQ, 256), sl(OC_FK, 256), sl(OC_FV, 256), sl(OC_CQ, 256), sl(OC_CKV, 128),
        _lane_pad(sl(OC_DT, SSD_HEADS)), _lane_pad(sl(OC_FR, FOX_HEADS)),
        jnp.tile(sl(OC_KR, ROPE_HALF), (1, MLA_HEADS)), jnp.tile(sl(OC_KR + ROPE_HALF, ROPE_HALF), (1, MLA_HEADS))], axis=1)


def _in_proj_grad_chunks(wp):
    rope = lambda start: wp[:, start:start + 64].reshape(wp.shape[0], MLA_HEADS, ROPE_HALF).sum(axis=1)
    segs = [(wp, PC_Z, 512), (wp, PC_XBC, 768), (wp, PC_DT, SSD_HEADS), (wp, PC_FQ, 256), (wp, PC_FK, 256),
            (wp, PC_FV, 256), (wp, PC_FR, FOX_HEADS), (wp, PC_CQ, 256), (wp, PC_CKV, 128),
            (rope(PC_KR), 0, ROPE_HALF), (rope(PC_KR + 64), 0, ROPE_HALF)]
    chunks = []
    for k in range(N_CHIPS):
        lo, hi, pos, pieces = k * IN_SHARD, (k + 1) * IN_SHARD, 0, []
        for arr, start, size in segs:
            a, b = max(lo, pos), min(hi, pos + size)
            if a < b:
                pieces.append(arr[:, start + a - pos:start + b - pos])
            pos += size
        pieces.append(jnp.zeros((wp.shape[0], IN_SHARD_P - IN_SHARD), wp.dtype))
        chunks.append(jnp.concatenate(pieces, axis=1))
    return jnp.stack(chunks)


def _regroup_uq(w):
    w3 = w.reshape(w.shape[0], MLA_HEADS, MLA_NOPE + MLA_ROPE)
    return jnp.concatenate([w3[:, :, :MLA_NOPE].reshape(w.shape[0], -1),
                            w3[:, :, MLA_NOPE:MLA_NOPE + ROPE_HALF].reshape(w.shape[0], -1),
                            w3[:, :, MLA_NOPE + ROPE_HALF:].reshape(w.shape[0], -1)], axis=1)


def _ungroup_uq(wp):
    n = wp.shape[0]
    return jnp.concatenate([wp[:, :256].reshape(n, MLA_HEADS, MLA_NOPE), wp[:, 256:320].reshape(n, MLA_HEADS, ROPE_HALF),
                            wp[:, 320:].reshape(n, MLA_HEADS, ROPE_HALF)], axis=2).reshape(n, -1)


def _regroup_ukv(w):
    w3 = w.reshape(w.shape[0], MLA_HEADS, MLA_NOPE + MLA_V)
    return jnp.concatenate([w3[:, :, :MLA_NOPE].reshape(w.shape[0], -1), w3[:, :, MLA_NOPE:].reshape(w.shape[0], -1)],
                           axis=1)


def _ungroup_ukv(wp):
    n = wp.shape[0]
    return jnp.concatenate([wp[:, :256].reshape(n, MLA_HEADS, MLA_NOPE), wp[:, 256:].reshape(n, MLA_HEADS, MLA_V)],
                           axis=2).reshape(n, -1)


TMF = 1088
N_IF = LP // TMF


def _chunk_rows_dx(g, w, l, chunk_h, *, name):
    N = w.shape[2]
    return _mm_core(g, w, a_spec=_bs((TMF, N), lambda i, j, k: (i, 0)),
                    b_spec=_bs((None, chunk_h, N), lambda i, j, k: (j, 0, 0)),
                    o_spec=_bs((TMF, chunk_h), lambda i, j, k: (i, j)), grid=(N_IF, N_CHIPS, 1),
                    out_shape=(LP, N_CHIPS * chunk_h), ca=1, cb=1, name=name)


def _chunk_rows_dw(a, g, chunk_h, *, name):
    N = g.shape[1]
    return _mm_core(a, g, a_spec=_bs((LP, chunk_h), lambda i, j, k: (0, i)), b_spec=_bs((LP, N), lambda i, j, k: (0, 0)),
                    o_spec=_bs((None, chunk_h, N), lambda i, j, k: (i, 0, 0)), grid=(N_CHIPS, 1, 1),
                    out_shape=(N_CHIPS, chunk_h, N), ca=0, cb=0, name=name)


def _ffn_up_swiglu(h, wg, wu, *, name, after=()):
    def body(h_ref, wg_ref, wu_ref, *refs):
        g_ref, u_ref, a_ref = refs[len(after):]
        hb = h_ref[...].astype(BF16)
        g = _raw_bdot(hb, wg_ref[...], 1, 1)
        u = _raw_bdot(hb, wu_ref[...], 1, 1)
        g_ref[...] = g
        u_ref[...] = u
        a_ref[...] = (_silu(g) * u).astype(a_ref.dtype)

    w_spec = _bs((None, HP, D_MODEL), lambda i, j: (j, 0, 0))
    o_spec = _bs((TMF, HP), lambda i, j: (i, j))
    return pl.pallas_call(
        body, name=name, grid=(N_IF, N_CHIPS),
        in_specs=[_bs((TMF, D_MODEL), lambda i, j: (i, 0)), w_spec, w_spec, *[ANY] * len(after)],
        out_specs=[o_spec] * 3,
        out_shape=[jax.ShapeDtypeStruct((LP, FP), F32), jax.ShapeDtypeStruct((LP, FP), F32),
                   jax.ShapeDtypeStruct((LP, FP), BF16)],
        compiler_params=_cparams(("parallel", "parallel")),
    )(h, wg, wu, *after)


def _ffn_down_dx_swiglu(do, wd, g, u, *, name):
    def body(do_ref, wd_ref, g_ref, u_ref, dg_ref, du_ref):
        dact = _raw_bdot(do_ref[...], wd_ref[...], 1, 1)
        gv = g_ref[...]
        sig = _sigmoid(gv)
        dg_ref[...] = (dact * u_ref[...] * (sig * (1.0 + gv * (1.0 - sig)))).astype(dg_ref.dtype)
        du_ref[...] = (dact * (gv * sig)).astype(du_ref.dtype)

    blk = _bs((TMF, HP), lambda i, j: (i, j))
    return pl.pallas_call(
        body, name=name, grid=(N_IF, N_CHIPS),
        in_specs=[_bs((TMF, D_MODEL), lambda i, j: (i, 0)), _bs((None, HP, D_MODEL), lambda i, j: (j, 0, 0)), blk, blk],
        out_specs=[blk, blk], out_shape=[jax.ShapeDtypeStruct((LP, FP), BF16)] * 2,
        compiler_params=_cparams(("parallel", "parallel")),
    )(do, wd, g, u)


def _ffn_gate_up_dw(dg, du, h, *, name):
    def body(dg_ref, du_ref, h_ref, wg_ref, wu_ref):
        hb = h_ref[...].astype(BF16)
        wg_ref[...] = _raw_bdot(dg_ref[...], hb, 0, 0)
        wu_ref[...] = _raw_bdot(du_ref[...], hb, 0, 0)

    a_spec = _bs((LP, HP), lambda k: (0, k))
    o_spec = _bs((None, HP, D_MODEL), lambda k: (k, 0, 0))
    return pl.pallas_call(
        body, name=name, grid=(N_CHIPS,), in_specs=[a_spec, a_spec, _bs((LP, D_MODEL), lambda k: (0, 0))],
        out_specs=[o_spec, o_spec], out_shape=[jax.ShapeDtypeStruct((N_CHIPS, HP, D_MODEL), F32)] * 2,
        compiler_params=_cparams(("parallel",)),
    )(dg, du, h)


def _ffn_gate_up_dx(dg, du, wg, wu, add, *, name):
    def body(dg_ref, du_ref, wg_ref, wu_ref, add_ref, o_ref, acc_ref):
        k = pl.program_id(1)

        @pl.when(k == 0)
        def _():
            acc_ref[...] = jnp.zeros_like(acc_ref)

        acc_ref[...] += _raw_bdot(dg_ref[...], wg_ref[...], 1, 0) + _raw_bdot(du_ref[...], wu_ref[...], 1, 0)

        @pl.when(k == N_CHIPS - 1)
        def _():
            o_ref[...] = acc_ref[...] + add_ref[...]

    a_spec = _bs((TMF, HP), lambda i, k: (i, k))
    w_spec = _bs((None, HP, D_MODEL), lambda i, k: (k, 0, 0))
    o_spec = _bs((TMF, D_MODEL), lambda i, k: (i, 0))
    return pl.pallas_call(
        body, name=name, grid=(N_IF, N_CHIPS), in_specs=[a_spec, a_spec, w_spec, w_spec, o_spec], out_specs=o_spec,
        out_shape=jax.ShapeDtypeStruct((LP, D_MODEL), F32), scratch_shapes=[pltpu.VMEM((TMF, D_MODEL), F32)],
        compiler_params=_cparams(("parallel", "arbitrary")),
    )(dg, du, wg, wu, add)


def _chunk_rows_mm_res_ln(a, w, chunk_h, h, gam, bet, scale, *, name):
    res_ln = _make_res_ln_fn(scale)

    def body(a_ref, w_ref, h_ref, g_ref, b_ref, o_ref, y_ref, yb_ref, acc_ref):
        k = pl.program_id(1)

        @pl.when(k == 0)
        def _():
            acc_ref[...] = jnp.zeros_like(acc_ref)

        acc_ref[...] += _raw_bdot(a_ref[...], w_ref[...], 1, 0)

        @pl.when(k == N_CHIPS - 1)
        def _():
            o = acc_ref[...]
            o_ref[...] = o
            (y,) = res_ln(0, h_ref[...], o, g_ref[...], b_ref[...])
            y_ref[...] = y
            yb_ref[...] = y.astype(yb_ref.dtype)

    row = _bs((TMF, D_MODEL), lambda i, k: (i, 0))
    par = _bs((1, D_MODEL), lambda i, k: (0, 0))
    return pl.pallas_call(
        body, name=name, grid=(N_IF, N_CHIPS),
        in_specs=[_bs((TMF, chunk_h), lambda i, k: (i, k)), _bs((None, chunk_h, D_MODEL), lambda i, k: (k, 0, 0)), row,
                  par, par],
        out_specs=[row, row, row],
        out_shape=[jax.ShapeDtypeStruct((LP, D_MODEL), F32)] * 2 + [jax.ShapeDtypeStruct((LP, D_MODEL), BF16)],
        scratch_shapes=[pltpu.VMEM((TMF, D_MODEL), F32)], compiler_params=_cparams(("parallel", "arbitrary")),
    )(a, w, h, gam, bet)


def _ffn_fwd(hp, W, pre, l, gam, bet, tag, after=()):
    h, hb = hp
    g, u, act = _ffn_up_swiglu(hb, W[pre + "_w_gate"][l], W[pre + "_w_up"][l], name=f"{tag}_up_swiglu", after=after)
    o, out, outb = _chunk_rows_mm_res_ln(act, W[pre + "_w_down"][l], HP, h, gam, bet, 0.5, name=f"{tag}_down_ln")
    return (out, outb), (h, hb, g, u, act, o)


def _ffn_bwd(dout, saved, W, pre, l, gam, bet, GB, tag):
    h, hb, g, u, act, o = saved
    (dh_a, do), (dgam, dbet) = _rowwise_bwd(_make_res_ln_fn(0.5), [h, o], [gam, bet], [dout], name=f"{tag}_ln_bwd",
                                            tile=272, grad_dtypes=[F32, BF16])
    dg, du = _ffn_down_dx_swiglu(do, W[pre + "_w_down"][l], g, u, name=f"{tag}_down_dx_swiglu")
    GB[pre + "_w_down"] = _chunk_rows_dw(act, do, HP, name=f"{tag}_down_dw")
    GB[pre + "_w_gate"], GB[pre + "_w_up"] = _ffn_gate_up_dw(dg, du, hb, name=f"{tag}_gate_up_dw")
    dh = _ffn_gate_up_dx(dg, du, W[pre + "_w_gate"][l], W[pre + "_w_up"][l], dh_a, name=f"{tag}_gate_up_dx")
    return dh, dgam, dbet


def _mixer_fwd(hp1, W, l, cosf, sins, after=()):
    h1, h1b = hp1
    tag = f"l{l}"
    proj = _mm(h1b, W["w_in_p"][l], name=f"{tag}_in_proj", after=after)
    sv = {"h1": h1, "h1b": h1b, "proj": proj}
    conv_w, conv_b = W["conv_w"][l], W["conv_b"][l][None]
    xc = _conv_fwd(proj, PC_XBC // BLOCK, conv_w, conv_b, name=f"{tag}_conv")
    dt_bias = _lane_pad(W["dt_bias"][l][None])
    dtc, dtr = _ssd_dt_fwd(proj, PC_DT // BLOCK, dt_bias, name=f"{tag}_ssd_dt")
    xh = _heads(xc[:, :SSD_D], SSD_HEADS, SSD_HD)
    bm = _heads(xc[:, SSD_D:SSD_D + 128], SSD_GROUPS, SSD_STATE)
    cm = _heads(xc[:, SSD_D + 128:], SSD_GROUPS, SSD_STATE)
    alog = jnp.broadcast_to(W["a_log"][l][:, None, None], (SSD_HEADS, 1, BLOCK))
    yh, prevs = _ssd_fwd(xh, bm, cm, dtc, dtr, alog, name=f"{tag}_ssd")
    y_raw = _unheads(yh)
    dskip = jnp.repeat(W["d_skip"][l], SSD_HD)[None]
    normg = W["ssd_norm_g"][l][None]
    post_rows = [y_raw, (xc, 256, 0), (proj, 256, PC_Z // 256)]
    (y_ssd,) = _rowwise(_ssd_post_fn, post_rows, [dskip, normg], [SSD_D], name=f"{tag}_ssd_post", tile=272,
                        ncol=SSD_GROUPS)
    sv.update(conv_w=conv_w, conv_b=conv_b, dt_bias=dt_bias, xh=xh, bm=bm, cm=cm, dtc=dtc, dtr=dtr, alog=alog,
              prevs=prevs, post_rows=post_rows, dskip=dskip, normg=normg)
    f_b = _lane_pad(W["fox_f_b"][l][None])
    cg, cgt = _fox_gate_fwd(proj, PC_FR // BLOCK, f_b, name=f"{tag}_fox_gate")
    fox_qkv = ((proj, PC_FQ // ATT_W), (proj, PC_FK // ATT_W), (proj, PC_FV // ATT_W))
    y_fox, lse_f = _attn_fwd(*fox_qkv, scale=FOX_HD ** -0.5, name=f"{tag}_fox_attn", bias=(cg, cgt))
    sv.update(f_b=f_b, cg=cg, cgt=cgt, fox_qkv=fox_qkv, y_fox=y_fox, lse_f=lse_f)
    gq, gkv = W["mla_q_norm_g"][l][None], W["mla_kv_norm_g"][l][None]
    norm_rows = [(proj, 256, PC_CQ // 256), (proj, BLOCK, PC_CKV // BLOCK)]
    qn, cn = _rowwise(_mla_norm_fn, norm_rows, [gq, gkv], [MLA_Q_LORA, MLA_KV_LORA], name=f"{tag}_mla_norm", tile=272,
                      out_dtypes=[BF16, BF16])
    qh = _mm(qn, W["mla_w_uq_p"][l], name=f"{tag}_mla_uq")
    kvh = _mm(cn, W["mla_w_ukv_p"][l], name=f"{tag}_mla_ukv")
    qr, kr = _rowwise(_rope_fn, [(qh, BLOCK, 2), (proj, BLOCK, PC_KR // BLOCK), cosf, sins], [], [BLOCK, BLOCK],
                      name=f"{tag}_rope", tile=272)
    mla_qkv = ((qh, 0), (kvh, 0), (kvh, 1))
    y_mla, lse_m = _attn_fwd(*mla_qkv, scale=(MLA_NOPE + MLA_ROPE) ** -0.5, name=f"{tag}_mla_attn",
                             rope=((qr, 0), (kr, 0)))
    sv.update(gq=gq, gkv=gkv, norm_rows=norm_rows, qn=qn, cn=cn, qr=qr, kr=kr, mla_qkv=mla_qkv, y_mla=y_mla, lse_m=lse_m)
    ycat = jnp.concatenate([y_ssd, y_fox, y_mla], axis=1).astype(BF16)
    mix, h2, h2b = _chunk_rows_mm_res_ln(ycat, W["w_out"][l], 256, h1, W["ln2_g"][l][None], W["ln2_b"][l][None], 1.0,
                                    name=f"{tag}_out_proj_ln2")
    sv.update(mix=mix, ycat=ycat)
    return (h2, h2b), sv


def _mixer_bwd(dh2, sv, W, l, cosf, sins, GB, zero=0.0):
    tag = f"l{l}"
    G = {}
    proj = sv["proj"]
    ln2g, ln2b = W["ln2_g"][l][None] + zero, W["ln2_b"][l][None]
    (dh1_a, dmix), (dln2g, dln2b) = _rowwise_bwd(
        _make_res_ln_fn(1.0), [sv["h1"], sv["mix"]], [ln2g, ln2b], [dh2], name=f"{tag}_ln2_bwd", tile=272,
        grad_dtypes=[F32, BF16])
    G["ln2_g"], G["ln2_b"] = dln2g[0], dln2b[0]
    dycat = _chunk_rows_dx(dmix, W["w_out"][l], l, 256, name=f"{tag}_out_proj_dx")
    GB["w_out"] = _chunk_rows_dw(sv["ycat"], dmix, 256, name=f"{tag}_out_proj_dw")
    (dy_raw, dxs_a, dz), (ddskip, dnormg) = _rowwise_bwd(
        _ssd_post_fn, sv["post_rows"], [sv["dskip"], sv["normg"]], [dycat[:, :SSD_D]],
        name=f"{tag}_ssd_post_bwd", tile=272, ncol=SSD_GROUPS)
    G["ssd_norm_g"] = dnormg[0]
    G["d_skip"] = ddskip.reshape(SSD_HEADS, SSD_HD).sum(axis=1)
    dxh, dbm, dcm, ddtc, ddtr, dal = _ssd_bwd(sv["xh"], sv["bm"], sv["cm"], sv["dtc"], sv["dtr"], sv["alog"],
                                              sv["prevs"], _heads(dy_raw, SSD_HEADS, SSD_HD), name=f"{tag}_ssd_bwd")
    G["a_log"] = dal[:, 0, 0]
    dxc = jnp.concatenate([dxs_a + _unheads(dxh), _unheads(dbm), _unheads(dcm)], axis=1)
    dxbc, G["conv_w"], dconv_b = _conv_bwd(proj, PC_XBC // BLOCK, sv["conv_w"], sv["conv_b"], dxc,
                                           name=f"{tag}_conv_bwd")
    G["conv_b"] = dconv_b[0]
    ddt_raw, ddt_bias = _ssd_dt_bwd(proj, PC_DT // BLOCK, sv["dt_bias"], ddtc, ddtr, name=f"{tag}_ssd_dt_bwd")
    G["dt_bias"] = ddt_bias[0, :SSD_HEADS]
    dfq, dfk, dfv, dcg, dcgt = _attn_bwd(*sv["fox_qkv"], sv["y_fox"], sv["lse_f"], (dycat, SSD_D // ATT_W),
                                         scale=FOX_HD ** -0.5, name=f"{tag}_fox_attn_bwd", bias=(sv["cg"], sv["cgt"]))
    df_raw, dfb = _fox_gate_bwd(proj, PC_FR // BLOCK, sv["f_b"], dcg, dcgt, name=f"{tag}_fox_gate_bwd")
    G["fox_f_b"] = dfb[0, :FOX_HEADS]
    dqn_h, dkn_h, dv_h, dqr, dkr = _attn_bwd(
        *sv["mla_qkv"], sv["y_mla"], sv["lse_m"], (dycat, (SSD_D + FOX_D) // ATT_W),
        scale=(MLA_NOPE + MLA_ROPE) ** -0.5, name=f"{tag}_mla_attn_bwd", rope=((sv["qr"], 0), (sv["kr"], 0)))
    dq_rope, dk_rope = _rowwise(_rope_t_fn, [dqr, dkr, cosf, sins], [], [BLOCK, BLOCK], name=f"{tag}_rope_bwd",
                                tile=272)
    dqh = jnp.concatenate([dqn_h, dq_rope], axis=1).astype(BF16)
    dkvh = jnp.concatenate([dkn_h, dv_h], axis=1).astype(BF16)
    dqn = _mm(dqh, W["mla_w_uq_p"][l], tb=True, name=f"{tag}_mla_uq_dx")
    G["mla_w_uq_p"] = _mm(sv["qn"], dqh, ta=True, name=f"{tag}_mla_uq_dw")
    dcn = _mm(dkvh, W["mla_w_ukv_p"][l], tb=True, name=f"{tag}_mla_ukv_dx")
    G["mla_w_ukv_p"] = _mm(sv["cn"], dkvh, ta=True, name=f"{tag}_mla_ukv_dw")
    (dcq, dckv), (dgq, dgkv) = _rowwise_bwd(_mla_norm_fn, sv["norm_rows"], [sv["gq"], sv["gkv"]], [dqn, dcn],
                                            name=f"{tag}_mla_norm_bwd", tile=272)
    G["mla_q_norm_g"], G["mla_kv_norm_g"] = dgq[0], dgkv[0]
    dproj = jnp.concatenate([dz, dxbc, dfq, dfk, dfv, dcq, dckv, ddt_raw, df_raw, dk_rope], axis=1).astype(BF16)
    dh1 = _mm(dproj, W["w_in_p"][l], tb=True, add=dh1_a, name=f"{tag}_in_proj_dx")
    G["w_in_p"] = _mm(sv["h1b"], dproj, ta=True, name=f"{tag}_in_proj_dw")
    return dh1, G


def _embed(x, meta):
    return jnp.concatenate([jnp.zeros((PAD_ROWS, D_MODEL), F32), meta, x], axis=0)


def _layer_fwd(h, W, l, cosf, sins):
    ln = lambda n: W[n][l][None]
    h1, s1 = _ffn_fwd(h, W, "ffn1", l, ln("ln1_g"), ln("ln1_b"), f"l{l}_ffn1")
    h2, sm = _mixer_fwd(h1, W, l, cosf, sins)
    h3, s2 = _ffn_fwd(h2, W, "ffn2", l, ln("ln3_g"), ln("ln3_b"), f"l{l}_ffn2")
    return h3, (s1, sm, s2)


def _layer_bwd(dh, saved, W, l, cosf, sins):
    ln = lambda n: W[n][l][None]
    s1, sm, s2 = saved
    G = {}
    dh, dg, db = _ffn_bwd(dh, s2, W, "ffn2", l, ln("ln3_g"), ln("ln3_b"), G, f"l{l}_ffn2")
    G["ln3_g"], G["ln3_b"] = dg[0], db[0]
    dh, Gm = _mixer_bwd(dh, sm, W, l, cosf, sins, G)
    G.update(Gm)
    dh, dg, db = _ffn_bwd(dh, s1, W, "ffn1", l, ln("ln1_g"), ln("ln1_b"), G, f"l{l}_ffn1")
    G["ln1_g"], G["ln1_b"] = dg[0], db[0]
    return dh, G


def _local_step(x, target, W):
    h = _embed(x, W["meta"])
    h = (h, h.astype(BF16))
    tgt = jnp.concatenate([jnp.zeros((BLOCK, D_MODEL), F32), target], axis=0)
    cosf, sins = _rope_tables()
    saved = []
    for l in range(DEPTH):
        h, sv = _layer_fwd(h, W, l, cosf, sins)
        saved.append(sv)
    dh, loss = _loss_head(h[0], tgt, name="loss_head")
    grads = [None] * DEPTH
    for l in reversed(range(DEPTH)):
        dh, grads[l] = _layer_bwd(dh, saved[l], W, l, cosf, sins)
    return loss, dh, grads


WEIGHTS = ['meta', 'ffn1_w_gate', 'ffn1_w_up', 'ffn1_w_down', 'ln1_g', 'ln1_b', 'w_in', 'conv_w', 'conv_b', 'dt_bias',
           'a_log', 'd_skip', 'ssd_norm_g', 'fox_f_b', 'mla_q_norm_g', 'mla_w_uq', 'mla_kv_norm_g', 'mla_w_ukv',
           'w_out', 'ln2_g', 'ln2_b', 'ffn2_w_gate', 'ffn2_w_up', 'ffn2_w_down', 'ln3_g', 'ln3_b']
SMALL = ["ln1_g", "ln1_b", "conv_b", "dt_bias", "a_log", "d_skip", "ssd_norm_g", "fox_f_b", "mla_q_norm_g",
         "mla_kv_norm_g", "ln2_g", "ln2_b", "ln3_g", "ln3_b"]
MATMUL_W = ["ffn1_w_gate", "ffn1_w_up", "ffn1_w_down", "w_in", "mla_w_uq", "mla_w_ukv", "w_out", "ffn2_w_gate",
            "ffn2_w_up", "ffn2_w_down"]
SMALL_ROWS = 312


def _pad_to(a, axis, size):
    pads = [(0, 0)] * a.ndim
    pads[axis] = (0, size - a.shape[axis])
    return jnp.pad(a, pads)


def _chip_cols(full, chip, width):
    return lax.dynamic_slice_in_dim(full, chip * width, width, axis=full.ndim - 1)


def kernel(x, meta, ffn1_w_gate, ffn1_w_up, ffn1_w_down, ln1_g, ln1_b, w_in, conv_w, conv_b, dt_bias, a_log, d_skip, ssd_norm_g, fox_f_b, mla_q_norm_g, mla_w_uq, mla_kv_norm_g, mla_w_ukv, w_out, ln2_g, ln2_b, ffn2_w_gate, ffn2_w_up, ffn2_w_down, ln3_g, ln3_b, loss_target, m_meta, m_ffn1_w_gate, m_ffn1_w_up, m_ffn1_w_down, m_ln1_g, m_ln1_b, m_w_in, m_conv_w, m_conv_b, m_dt_bias, m_a_log, m_d_skip, m_ssd_norm_g, m_fox_f_b, m_mla_q_norm_g, m_mla_w_uq, m_mla_kv_norm_g, m_mla_w_ukv, m_w_out, m_ln2_g, m_ln2_b, m_ffn2_w_gate, m_ffn2_w_up, m_ffn2_w_down, m_ln3_g, m_ln3_b, v_meta, v_ffn1_w_gate, v_ffn1_w_up, v_ffn1_w_down, v_ln1_g, v_ln1_b, v_w_in, v_conv_w, v_conv_b, v_dt_bias, v_a_log, v_d_skip, v_ssd_norm_g, v_fox_f_b, v_mla_q_norm_g, v_mla_w_uq, v_mla_kv_norm_g, v_mla_w_ukv, v_w_out, v_ln2_g, v_ln2_b, v_ffn2_w_gate, v_ffn2_w_up, v_ffn2_w_down, v_ln3_g, v_ln3_b):
    args = dict(locals())
    w = {n: args[n] for n in WEIGHTS}
    m = {n: args["m_" + n] for n in WEIGHTS}
    v = {n: args["v_" + n] for n in WEIGHTS}
    xcoord, ycoord, _ = _my_pos()
    chip = 2 * xcoord + ycoord

    tr = lambda a: jnp.swapaxes(a, 1, 2)

    def bf16_shard(n, l, zero=None):
        a = w[n] if zero is None else w[n] + zero
        if n.endswith("w_gate") or n.endswith("w_up"):
            a = _pad_to(tr(a), 1, HP)
        elif n.endswith("w_down"):
            a = _pad_to(a, 1, HP)
        elif n == "w_in":
            a = _pad_to(a, 2, IN_SHARD_P)
        return a[l].astype(BF16)

    land_shape = lambda s: jax.ShapeDtypeStruct((N_CHIPS,) + s.shape, s.dtype)

    def gather_start(names, l, tag, after):
        srcs = [bf16_shard(n, l, None if after is None else after[0, 0]) for n in names]
        return _ici_start(_gather_copies, srcs, [land_shape(s) for s in srcs], name=f"gather_ici_{tag}_start",
                          after=[tiny[0]] if after is None else [after])

    def gather_finish(handle, names, l, tag, after):
        srcs, lands = _ici_wait(_gather_copies, *handle[:4], after, name=f"gather_ici_{tag}_wait")
        use_gathered(l, names, _gather_d2d(srcs, lands, tag))

    def gather_d2d_start(handle, tag, after):
        srcs, lands = _ici_wait(_gather_copies, *handle[:4], after, name=f"gather_ici_{tag}_wait")
        return _ici_start(_d2d_copies, srcs, lands, name=f"gather_d2d_{tag}_start", sems_per_array=D2D_COPIES)

    def gather_d2d_finish(handle, names, l, tag, after):
        _, lands = _ici_wait(_d2d_copies, *handle[:4], after, name=f"gather_d2d_{tag}_wait")
        use_gathered(l, names, lands)

    tiny = _allgather_chips([w["meta"].reshape(2, N_META // 2, D_MODEL // N_CHIPS), w["conv_w"]])
    meta_full = jnp.concatenate([tiny[0][k].reshape(N_META, D_MODEL // N_CHIPS) for k in range(N_CHIPS)], axis=1)

    W = {n: [None] * DEPTH for n in MATMUL_W + ["w_in_p", "mla_w_uq_p", "mla_w_ukv_p"]}
    W["conv_w"] = jnp.concatenate([tiny[1][k] for k in range(N_CHIPS)], axis=-1)
    W["meta"] = meta_full
    for n in SMALL:
        W[n] = w[n]

    def use_gathered(l, names, lands):
        got = dict(zip(names, lands))
        cat = lambda n, cut=None: jnp.concatenate([got[n][k][..., :cut] for k in range(N_CHIPS)], axis=-1)
        for n in names:
            W[n][l] = got[n]
        if "w_in" in got:
            W["w_in_p"][l] = _pad_in_proj(cat("w_in", IN_SHARD))
            W["mla_w_uq_p"][l] = _regroup_uq(cat("mla_w_uq"))
            W["mla_w_ukv_p"][l] = _regroup_ukv(cat("mla_w_ukv"))

    def chunk_grads(G, names):
        def chunked(name, ungroup, width, pad):
            full = ungroup(G[name])
            return _pad_to(jnp.moveaxis(full.reshape(full.shape[0], N_CHIPS, width), 1, 0), 2, pad)
        special = {"mla_w_uq": ("mla_w_uq_p", _ungroup_uq, MLA_NOPE + MLA_ROPE, MLA_NOPE + MLA_ROPE),
                   "mla_w_ukv": ("mla_w_ukv_p", _ungroup_ukv, MLA_NOPE + MLA_V, MLA_NOPE + MLA_V)}
        return [_in_proj_grad_chunks(G["w_in_p"]) if n == "w_in" else chunked(*special[n]) if n in special else G[n]
                for n in names]

    def rs_start(G, names, tag):
        pairs = _rs_pair_sums(chunk_grads(G, names), names, tag)
        handle = _ici_start(_exchange_copies, [p[1] for p in pairs], _exchange_land_shapes(pairs),
                            name=f"rs_exchange_{tag}_start")
        return pairs, handle

    def swap_start(G, names, tag):
        gs = chunk_grads(G, names)
        return _ici_start(_swap_copies, gs, _swap_land_shapes(gs), name=f"rs_swap_{tag}_start")

    def exchange_start(swap_handle, names, tag, after):
        gs, r1 = _ici_wait(_swap_copies, *swap_handle[:4], after, name=f"rs_swap_{tag}_wait")
        pairs = _rs_add_pairs(gs, r1, names, tag)
        handle = _ici_start(_exchange_copies, [p[1] for p in pairs], _exchange_land_shapes(pairs),
                            name=f"rs_exchange_{tag}_start")
        return pairs, handle

    def rs_end(pairs, handle, names, tag, after):
        _, r2 = _ici_wait(_exchange_copies, *handle[:4], after, name=f"rs_exchange_{tag}_wait")
        return dict(zip(names, _rs_finish(pairs, r2, names, tag)))

    ffn1_w, mix_w, ffn2_w = MATMUL_W[:3], MATMUL_W[3:7], MATMUL_W[7:]
    g_a = gather_start(ffn1_w, 0, "l0_ffn1", None)
    g_b = gather_start(mix_w, 0, "l0_mix", g_a[4])
    g_c = gather_start(ffn2_w, 0, "l0_ffn2", g_b[4])
    g_l1 = gather_start(MATMUL_W, 1, "l1", g_c[4])
    token = g_l1[4]
    cosf, sins = _rope_tables()
    ln = lambda n, l: W[n][l][None]
    h = _embed(x[0] + token[0, 0], meta_full)
    h = (h, h.astype(BF16))
    gather_finish(g_a, ffn1_w, 0, "l0_ffn1", h[1])
    h1, s1 = _ffn_fwd(h, W, "ffn1", 0, ln("ln1_g", 0), ln("ln1_b", 0), "l0_ffn1")
    gather_finish(g_b, mix_w, 0, "l0_mix", h1[1])
    d_c = gather_d2d_start(g_c, "l0_ffn2", W["w_in_p"][0])
    h2, sm = _mixer_fwd(h1, W, 0, cosf, sins, after=[d_c[4]])
    gather_d2d_finish(d_c, ffn2_w, 0, "l0_ffn2", h2[1])
    d_l1 = gather_d2d_start(g_l1, "l1", W["ffn2_w_gate"][0])
    h, s2 = _ffn_fwd(h2, W, "ffn2", 0, ln("ln3_g", 0), ln("ln3_b", 0), "l0_ffn2", after=[d_l1[4]])
    saved0 = (s1, sm, s2)
    gather_d2d_finish(d_l1, MATMUL_W, 1, "l1", h[1])
    h, saved1 = _layer_fwd(h, W, 1, cosf, sins)
    tgt = jnp.concatenate([jnp.zeros((BLOCK, D_MODEL), F32), loss_target[0]], axis=0)
    dh, loss = _loss_head(h[0], tgt, name="loss_head")
    G = [None] * DEPTH
    dh, G[1] = _layer_bwd(dh, saved1, W, 1, cosf, sins)
    ffn2_w, mix_w, ffn1_w = MATMUL_W[7:], MATMUL_W[3:7], MATMUL_W[:3]
    sw_l1 = swap_start(G[1], MATMUL_W, "l1")
    G0 = {}
    dh, dg, db = _ffn_bwd(dh, s2, W, "ffn2", 0, ln("ln3_g", 0) + sw_l1[4][0, 0], ln("ln3_b", 0), G0, "l0_ffn2")
    G0["ln3_g"], G0["ln3_b"] = dg[0], db[0]
    pairs_l1, x_l1 = exchange_start(sw_l1, MATMUL_W, "l1", dh)
    sw_a = swap_start(G0, ffn2_w, "l0_ffn2")
    dh, Gm = _mixer_bwd(dh, sm, W, 0, cosf, sins, G0, zero=x_l1[4][0, 0] + sw_a[4][0, 0])
    G0.update(Gm)
    pairs_a, x_a = exchange_start(sw_a, ffn2_w, "l0_ffn2", dh)
    reduced1 = rs_end(pairs_l1, x_l1, MATMUL_W, "l1", dh)
    pairs_b, x_b = rs_start(G0, mix_w, "l0_mix")
    dh0, dg, db = _ffn_bwd(dh, s1, W, "ffn1", 0, ln("ln1_g", 0) + (x_a[4][0, 0] + x_b[4][0, 0]), ln("ln1_b", 0), G0,
                           "l0_ffn1")
    G0["ln1_g"], G0["ln1_b"] = dg[0], db[0]
    G[0] = G0
    reduced0 = rs_end(pairs_a, x_a, ffn2_w, "l0_ffn2", dh0)
    reduced0.update(rs_end(pairs_b, x_b, mix_w, "l0_mix", dh0))

    small_parts = [jnp.stack([G[l][n] for l in range(DEPTH)]).reshape(-1) for n in SMALL]
    small_parts += [jnp.stack([G[l]["conv_w"] for l in range(DEPTH)]).reshape(-1), dh0[PAD_ROWS:BLOCK].reshape(-1),
                    loss[0, :1]]
    sw_c = swap_start(G0, ffn1_w, "l0_ffn1")
    flat = jnp.concatenate(small_parts) + sw_c[4][0, 0]
    flat = jnp.pad(flat, (0, SMALL_ROWS * BLOCK - flat.shape[0]))
    red2d = _allreduce_small(flat.reshape(SMALL_ROWS, BLOCK))
    red = red2d.reshape(-1)
    pairs_c, x_c = exchange_start(sw_c, ffn1_w, "l0_ffn1", red2d)
    grads, off = {}, 0
    for n in SMALL:
        size = int(np.prod(w[n].shape))
        grads[n] = red[off:off + size].reshape(w[n].shape)
        off += size
    conv_full = red[off:off + DEPTH * SSD_CONV * 768].reshape(DEPTH, SSD_CONV, 768)
    off += DEPTH * SSD_CONV * 768
    dmeta_full = red[off:off + N_META * D_MODEL].reshape(N_META, D_MODEL)
    off += N_META * D_MODEL
    loss_out = red[off]
    grads["conv_w"] = _chip_cols(conv_full, chip, 768 // N_CHIPS)
    grads["meta"] = _chip_cols(dmeta_full, chip, D_MODEL // N_CHIPS)

    delta, new_m, new_v = {}, {}, {}

    def adamw_matmul_weights(names, after):
        done = []
        for n in names:
            gs = [reduced0[n], reduced1[n]]
            if n.endswith("w_gate") or n.endswith("w_up"):
                res = _adamw(tr(w[n]), gs, tr(m[n]), tr(v[n]), name=f"adamw_{n}", after=after)
                grads[n], delta[n], new_m[n], new_v[n] = [tr(r) for r in res]
            else:
                res = _adamw(w[n], gs, m[n], v[n], name=f"adamw_{n}", after=after)
                grads[n], delta[n], new_m[n], new_v[n] = res
            done.append(res[1])
        return done

    early_done = adamw_matmul_weights(ffn2_w + mix_w, [x_c[4]])
    rest = [n for n in WEIGHTS if n not in MATMUL_W]

    def pack_small(d):
        f = jnp.concatenate([d[n].reshape(-1) for n in rest])
        tot = -(-f.shape[0] // (8 * BLOCK)) * 8 * BLOCK
        return jnp.pad(f, (0, tot - f.shape[0])).reshape(-1, BLOCK)

    _, d2, m2, v2 = _adamw(pack_small(w), [pack_small(grads)], pack_small(m), pack_small(v), name="adamw_small",
                           after=[x_c[4]])
    reduced0.update(rs_end(pairs_c, x_c, ffn1_w, "l0_ffn1", [d2] + early_done))
    adamw_matmul_weights(ffn1_w, [])
    off = 0
    for n in rest:
        size = int(np.prod(w[n].shape))
        for dst, src in ((delta, d2), (new_m, m2), (new_v, v2)):
            dst[n] = src.reshape(-1)[off:off + size].reshape(w[n].shape)
        off += size

    grad_x = dh0[BLOCK:][None]
    return (loss_out, grad_x, *[grads[n] for n in WEIGHTS], *[delta[n] for n in WEIGHTS],
            *[new_m[n] for n in WEIGHTS], *[new_v[n] for n in WEIGHTS])
```

```python
import functools

import numpy as np
import jax
import jax.numpy as jnp
from jax import lax
from jax.experimental import pallas as pl
from jax.experimental.pallas import tpu as pltpu

F32 = jnp.float32
BF16 = jnp.bfloat16
MESH = pl.DeviceIdType.MESH

D_MODEL = 1024
SEQ = 2048
N_META = 16
BLOCK = 128
PAD_ROWS = 112
LP = PAD_ROWS + N_META + SEQ
N_CHUNK = LP // BLOCK
DEPTH = 2
D_FF = 2816
N_CHIPS = 4
FF_SHARD = D_FF // N_CHIPS
HP = 768
FP = N_CHIPS * HP
SSD_HEADS, SSD_HD, SSD_D, SSD_GROUPS, SSD_STATE, SSD_CONV = 8, 64, 512, 2, 64, 4
FOX_HEADS, FOX_HD, FOX_D = 4, 64, 256
MLA_HEADS, MLA_Q_LORA, MLA_KV_LORA, MLA_NOPE, MLA_ROPE, MLA_V, MLA_D = 4, 256, 128, 64, 32, 64, 256
ROPE_HALF = MLA_ROPE // 2
ROPE_THETA = 10000.0
N_IN = 2476
IN_SHARD = N_IN // N_CHIPS
IN_SHARD_P = 640
ALPHA = (2 * DEPTH) ** 0.25
EPS = 1e-5
ADAM_LR, ADAM_B1, ADAM_B2, ADAM_EPS, ADAM_WD, ADAM_STEP = 0.001, 0.9, 0.999, 1e-08, 0.01, 10
NEG = -1e30
TM = 544

VMEM_LIMIT_BYTES = 56 * 1024 * 1024

PC_Z, PC_XBC, PC_FQ, PC_FK, PC_FV, PC_CQ, PC_CKV, PC_DT, PC_FR, PC_KR, PC_END = (
    0, 512, 1280, 1536, 1792, 2048, 2304, 2432, 2560, 2688, 2816)
OC_Z, OC_XBC, OC_DT, OC_FQ, OC_FK, OC_FV, OC_FR, OC_CQ, OC_CKV, OC_KR = (
    0, 512, 1280, 1288, 1544, 1800, 2056, 2060, 2316, 2444)


def _cparams(sem=None):
    return pltpu.CompilerParams(dimension_semantics=sem, vmem_limit_bytes=VMEM_LIMIT_BYTES)


def _tile(n, cap, mult):
    best = None
    for t in range(mult, min(n, cap) + 1, mult):
        if n % t == 0:
            best = t
    return best if best is not None else n


def _bs(shape, fn):
    return pl.BlockSpec(shape, fn)


ANY = pl.BlockSpec(memory_space=pl.ANY)


def _dims(ca, cb):
    return (((ca,), (cb,)), ((), ()))


def _raw_bdot(a, b, ca, cb):
    return lax.dot_general(a.astype(BF16), b.astype(BF16), _dims(ca, cb), preferred_element_type=F32)


def _mm_core(a, b, *, a_spec, b_spec, o_spec, grid, out_shape, ca, cb, name, add=None, after=()):
    nk = grid[2]
    has_add = add is not None
    acc_shape = tuple(d for d in o_spec.block_shape if d is not None)

    def body(*refs):
        a_ref, b_ref = refs[0], refs[1]
        add_ref = refs[2] if has_add else None
        o_ref, acc_ref = refs[-2], refs[-1]
        k = pl.program_id(2)

        @pl.when(k == 0)
        def _():
            acc_ref[...] = jnp.zeros_like(acc_ref)

        acc_ref[...] += _raw_bdot(a_ref[...], b_ref[...], ca, cb)

        @pl.when(k == nk - 1)
        def _():
            r = acc_ref[...]
            if has_add:
                r = r + add_ref[...]
            o_ref[...] = r

    ins = [a, b] + ([add] if has_add else []) + list(after)
    in_specs = [a_spec, b_spec] + ([o_spec] if has_add else []) + [ANY] * len(after)
    return pl.pallas_call(
        body, name=name, grid=grid, in_specs=in_specs, out_specs=o_spec,
        out_shape=jax.ShapeDtypeStruct(out_shape, F32), scratch_shapes=[pltpu.VMEM(acc_shape, F32)],
        compiler_params=_cparams(("parallel", "parallel", "arbitrary")),
    )(*ins)


MM_VMEM_BUDGET = 40 * 1024 * 1024


def _divisors(n, mult):
    return [t for t in range(mult, n + 1, mult) if n % t == 0] or [n]


def _pick_tiles(M, N, K, a_bytes, b_bytes, ta, has_add):
    best = None
    for tm in _divisors(M, 128 if ta else 16):
        for tn in _divisors(N, 128):
            vmem = 2 * tm * K * a_bytes + 2 * K * tn * b_bytes + (3 + 2 * int(has_add)) * tm * tn * 4
            if vmem <= MM_VMEM_BUDGET:
                key = ((M // tm) * (N // tn), -tn)
                if best is None or key < best[0]:
                    best = (key, tm, tn)
    assert best is not None, (M, N, K)
    return best[1], best[2], K


def _mm(a, b, *, ta=False, tb=False, add=None, name, after=()):
    if ta:
        K, M = a.shape
    else:
        M, K = a.shape
    if tb:
        N, Kb = b.shape
    else:
        Kb, N = b.shape
    assert K == Kb, (a.shape, b.shape, ta, tb)
    tm, tn, tk = _pick_tiles(M, N, K, a.dtype.itemsize, b.dtype.itemsize, ta, add is not None)
    a_spec = _bs((tk, tm), lambda i, j, k: (k, i)) if ta else _bs((tm, tk), lambda i, j, k: (i, k))
    b_spec = _bs((tn, tk), lambda i, j, k: (j, k)) if tb else _bs((tk, tn), lambda i, j, k: (k, j))
    return _mm_core(a, b, a_spec=a_spec, b_spec=b_spec, o_spec=_bs((tm, tn), lambda i, j, k: (i, j)),
                    grid=(M // tm, N // tn, K // tk), out_shape=(M, N), ca=0 if ta else 1, cb=1 if tb else 0,
                    name=name, add=add, after=after)


def _row_entry(r, ncol):
    if isinstance(r, tuple):
        return r
    return r, r.shape[1] // ncol, 0


def _rowwise(fn, rows, pars, out_cols, *, name, tile, ncol=1, out_dtypes=None):
    rows = [_row_entry(r, ncol) for r in rows]
    L = rows[0][0].shape[0]
    nr, npar = len(rows), len(pars)
    in_specs = [_bs((tile, w), lambda g, i, o=o: (i, o + g)) for _, w, o in rows]
    in_specs += [_bs((p.shape[0], p.shape[1] // ncol), lambda g, i: (0, g)) for p in pars]
    out_specs = [_bs((tile, c // ncol), lambda g, i: (i, g)) for c in out_cols]

    def body(*refs):
        ins, outs = refs[:nr + npar], refs[nr + npar:]
        row0 = pl.program_id(1) * tile
        res = fn(row0, *[r[...] for r in ins])
        for o, v in zip(outs, res):
            o[...] = v.astype(o.dtype)

    return pl.pallas_call(
        body, name=name, grid=(ncol, L // tile), in_specs=in_specs, out_specs=out_specs,
        out_shape=[jax.ShapeDtypeStruct((L, c), d) for c, d in zip(out_cols, out_dtypes or [F32] * len(out_cols))],
        compiler_params=_cparams(("parallel", "parallel")),
    )(*[r[0] for r in rows], *pars)


def _rowwise_bwd(fn, rows, pars, douts, *, name, tile, ncol=1, row_grad=None, grad_dtypes=None):
    rows = [_row_entry(r, ncol) for r in rows]
    L = rows[0][0].shape[0]
    nr, npar, nd = len(rows), len(pars), len(douts)
    row_grad = [True] * nr if row_grad is None else row_grad
    in_specs = [_bs((tile, w), lambda g, i, o=o: (i, o + g)) for _, w, o in rows]
    in_specs += [_bs((p.shape[0], p.shape[1] // ncol), lambda g, i: (0, g)) for p in pars]
    in_specs += [_bs((tile, d.shape[1] // ncol), lambda g, i: (i, g)) for d in douts]
    g_widths = [w * ncol for (_, w, _), f in zip(rows, row_grad) if f]
    out_specs = [_bs((tile, w // ncol), lambda g, i: (i, g)) for w in g_widths]
    out_specs += [_bs((p.shape[0], p.shape[1] // ncol), lambda g, i: (0, g)) for p in pars]
    out_shape = [jax.ShapeDtypeStruct((L, w), d) for w, d in zip(g_widths, grad_dtypes or [F32] * len(g_widths))]
    out_shape += [jax.ShapeDtypeStruct(p.shape, F32) for p in pars]

    def body(*refs):
        ins = refs[:nr + npar]
        dos = refs[nr + npar:nr + npar + nd]
        outs = refs[nr + npar + nd:]
        i = pl.program_id(1)
        row0 = i * tile
        _, vjp = jax.vjp(lambda *a: tuple(fn(row0, *a)), *[r[...] for r in ins])
        grads = vjp(tuple(d[...].astype(F32) for d in dos))
        o = 0
        for j in range(nr):
            if row_grad[j]:
                outs[o][...] = grads[j].astype(outs[o].dtype)
                o += 1
        for j in range(npar):
            g, ref = grads[nr + j], outs[o + j]

            @pl.when(i == 0)
            def _(g=g, ref=ref):
                ref[...] = g

            @pl.when(i > 0)
            def _(g=g, ref=ref):
                ref[...] += g

    res = pl.pallas_call(
        body, name=name, grid=(ncol, L // tile), in_specs=in_specs, out_specs=out_specs, out_shape=out_shape,
        compiler_params=_cparams(("parallel", "arbitrary")),
    )(*[r[0] for r in rows], *pars, *douts)
    return res[:len(g_widths)], res[len(g_widths):]


def _sigmoid(x):
    return 1.0 / (1.0 + jnp.exp(-x))


def _softplus(x):
    return jnp.maximum(x, 0.0) + jnp.log(1.0 + jnp.exp(-jnp.abs(x)))


def _silu(x):
    return x * _sigmoid(x)


def _make_res_ln_fn(scale):
    def fn(row0, h, o, gam, bet):
        pre = ALPHA * h + scale * o
        mu = jnp.mean(pre, axis=-1, keepdims=True)
        xc = pre - mu
        var = jnp.mean(xc * xc, axis=-1, keepdims=True)
        return (xc * lax.rsqrt(var + EPS) * gam + bet,)
    return fn


def _ssd_post_fn(row0, y, xs, z, dskip, normg):
    v = (y + dskip * xs) * _silu(z)
    v = v * lax.rsqrt(jnp.mean(v * v, axis=-1, keepdims=True) + EPS)
    return (v * normg,)


def _mla_norm_fn(row0, cq, ckv, gq, gkv):
    qn = cq * lax.rsqrt(jnp.mean(cq * cq, axis=-1, keepdims=True) + EPS) * gq
    cn = ckv * lax.rsqrt(jnp.mean(ckv * ckv, axis=-1, keepdims=True) + EPS) * gkv
    return qn, cn


def _rope_fn(row0, q, k, cosf, sins):
    return (q * cosf + pltpu.roll(q, 64, 1) * sins, k * cosf + pltpu.roll(k, 64, 1) * sins)


def _rope_t_fn(row0, gq, gk, cosf, sins):
    return (gq * cosf + pltpu.roll(gq * sins, 64, 1), gk * cosf + pltpu.roll(gk * sins, 64, 1))


def _conv_fwd(x, x_off, w, b, *, name):
    C = w.shape[1]

    def body(x_ref, w_ref, b_ref, o_ref):
        rows = lax.broadcasted_iota(jnp.int32, (LP, BLOCK), 0)
        xv = jnp.where(rows >= PAD_ROWS, x_ref[...], 0.0)
        acc = b_ref[...] + w_ref[3:4, :] * xv
        for k in range(SSD_CONV - 1):
            acc = acc + w_ref[k:k + 1, :] * pltpu.roll(xv, SSD_CONV - 1 - k, 0)
        o_ref[...] = _silu(acc)

    return pl.pallas_call(
        body, name=name, grid=(C // BLOCK,),
        in_specs=[_bs((LP, BLOCK), lambda j: (0, j + x_off)), _bs((SSD_CONV, BLOCK), lambda j: (0, j)),
                  _bs((1, BLOCK), lambda j: (0, j))],
        out_specs=_bs((LP, BLOCK), lambda j: (0, j)),
        out_shape=jax.ShapeDtypeStruct((LP, C), F32), compiler_params=_cparams(("parallel",)),
    )(x, w, b)


def _conv_bwd(x, x_off, w, b, dout, *, name):
    C = w.shape[1]

    def body(x_ref, w_ref, b_ref, do_ref, dx_ref, dw_ref, db_ref):
        rows = lax.broadcasted_iota(jnp.int32, (LP, BLOCK), 0)
        real = rows >= PAD_ROWS
        xv = jnp.where(real, x_ref[...], 0.0)
        shifted = [pltpu.roll(xv, SSD_CONV - 1 - k, 0) for k in range(SSD_CONV - 1)] + [xv]
        acc = b_ref[...]
        for k in range(SSD_CONV):
            acc = acc + w_ref[k:k + 1, :] * shifted[k]
        sig = _sigmoid(acc)
        dacc = jnp.where(real, do_ref[...] * (sig * (1.0 + acc * (1.0 - sig))), 0.0)
        db_ref[...] = jnp.sum(dacc, axis=0, keepdims=True)
        dx = w_ref[3:4, :] * dacc
        for k in range(SSD_CONV):
            dw_ref[k:k + 1, :] = jnp.sum(dacc * shifted[k], axis=0, keepdims=True)
            if k < SSD_CONV - 1:
                dx = dx + w_ref[k:k + 1, :] * pltpu.roll(dacc, LP - (SSD_CONV - 1 - k), 0)
        dx_ref[...] = jnp.where(real, dx, 0.0)

    return pl.pallas_call(
        body, name=name, grid=(C // BLOCK,),
        in_specs=[_bs((LP, BLOCK), lambda j: (0, j + x_off)), _bs((SSD_CONV, BLOCK), lambda j: (0, j)),
                  _bs((1, BLOCK), lambda j: (0, j)), _bs((LP, BLOCK), lambda j: (0, j))],
        out_specs=[_bs((LP, BLOCK), lambda j: (0, j)), _bs((SSD_CONV, BLOCK), lambda j: (0, j)),
                   _bs((1, BLOCK), lambda j: (0, j))],
        out_shape=[jax.ShapeDtypeStruct((LP, C), F32), jax.ShapeDtypeStruct((SSD_CONV, C), F32),
                   jax.ShapeDtypeStruct((1, C), F32)],
        compiler_params=_cparams(("parallel",)),
    )(x, w, b, dout)


_BDIMS = {"nn": (((2,), (1,)), ((0,), (0,))), "nt": (((2,), (2,)), ((0,), (0,))), "tn": (((1,), (1,)), ((0,), (0,)))}


def _raw_bdot3(a, b, mode):
    return lax.dot_general(a.astype(BF16), b.astype(BF16), _BDIMS[mode], preferred_element_type=F32)


@functools.partial(jax.custom_vjp, nondiff_argnums=(2,))
def _bdot3(a, b, mode):
    return _raw_bdot3(a, b, mode)


def _bdot3_fwd(a, b, mode):
    return _raw_bdot3(a, b, mode), (a, b)


def _bdot3_bwd(mode, res, g):
    a, b = res
    if mode == "nn":
        return _raw_bdot3(g, b, "nt"), _raw_bdot3(a, g, "tn")
    if mode == "nt":
        return _raw_bdot3(g, b, "nn"), _raw_bdot3(g, a, "tn")
    return _raw_bdot3(b, g, "nt"), _raw_bdot3(a, g, "nn")


_bdot3.defvjp(_bdot3_fwd, _bdot3_bwd)


def _ssd_chunk(x, bm, cm, dt, dtt, alog, prev):
    rep = SSD_HEADS // SSD_GROUPS
    per_head = lambda t: jnp.broadcast_to(t[:, None], (SSD_GROUPS, rep) + t.shape[1:]).reshape((SSD_HEADS,) + t.shape[1:])
    bm, cm = per_head(bm), per_head(cm)
    lane_h = lax.broadcasted_iota(jnp.int32, (1, BLOCK), 1)
    row_h = lax.broadcasted_iota(jnp.int32, (BLOCK, 1), 0)
    dtc = jnp.stack([jnp.sum(jnp.where(lane_h == h, dt, 0.0), axis=1, keepdims=True) for h in range(SSD_HEADS)])
    dtr = jnp.stack([jnp.sum(jnp.where(row_h == h, dtt, 0.0), axis=0, keepdims=True) for h in range(SSD_HEADS)])
    lane = lax.broadcasted_iota(jnp.int32, alog.shape, 2)
    a_neg = -jnp.exp(jnp.sum(jnp.where(lane == 0, alog, 0.0), axis=2, keepdims=True))
    ac_in = dtc * a_neg
    ar_in = dtr * a_neg
    li = lax.broadcasted_iota(jnp.int32, (1, BLOCK, BLOCK), 1)
    si = lax.broadcasted_iota(jnp.int32, (1, BLOCK, BLOCK), 2)
    causal = li >= si
    acum_c = jnp.sum(jnp.where(causal, ar_in, 0.0), axis=2, keepdims=True)
    acum_r = jnp.sum(jnp.where(li <= si, ac_in, 0.0), axis=1, keepdims=True)
    total = jnp.sum(ar_in, axis=2, keepdims=True)
    seg = jnp.exp(jnp.where(causal, acum_c - acum_r, NEG))
    xdt = x * dtc
    cb = _bdot3(cm, bm, "nt")
    y = _bdot3(cb * seg, xdt, "nn") + _bdot3(cm, prev, "nt") * jnp.exp(acum_c)
    st = _bdot3(xdt, bm * jnp.exp(total - acum_c), "tn")
    return y, prev * jnp.exp(total) + st


def _ssd_dt_fwd(raw, raw_blk, bias, *, name):
    def body(raw_ref, b_ref, dt_ref, dtt_ref):
        rows = lax.broadcasted_iota(jnp.int32, (LP, BLOCK), 0)
        dt = jnp.where(rows >= PAD_ROWS, _softplus(raw_ref[...] + b_ref[...]), 0.0)
        dt_ref[...] = dt
        dtt_ref[...] = dt.T

    return pl.pallas_call(
        body, name=name, grid=(1,),
        in_specs=[_bs((LP, BLOCK), lambda j: (0, raw_blk)), _bs((1, BLOCK), lambda j: (0, 0))],
        out_specs=[_bs((LP, BLOCK), lambda j: (0, 0)), _bs((BLOCK, LP), lambda j: (0, 0))],
        out_shape=[jax.ShapeDtypeStruct((LP, BLOCK), F32), jax.ShapeDtypeStruct((BLOCK, LP), F32)],
        compiler_params=_cparams(("arbitrary",)),
    )(raw, bias)


def _ssd_dt_bwd(raw, raw_blk, bias, ddt, ddtt, *, name):
    def body(raw_ref, b_ref, ddt_ref, ddtt_ref, draw_ref, db_ref):
        rows = lax.broadcasted_iota(jnp.int32, (LP, BLOCK), 0)
        g = ddt_ref[...] + ddtt_ref[...].T
        draw = jnp.where(rows >= PAD_ROWS, g * _sigmoid(raw_ref[...] + b_ref[...]), 0.0)
        draw_ref[...] = draw
        db_ref[...] = jnp.sum(draw, axis=0, keepdims=True)

    return pl.pallas_call(
        body, name=name, grid=(1,),
        in_specs=[_bs((LP, BLOCK), lambda j: (0, raw_blk)), _bs((1, BLOCK), lambda j: (0, 0)),
                  _bs((LP, BLOCK), lambda j: (0, 0)), _bs((BLOCK, LP), lambda j: (0, 0))],
        out_specs=[_bs((LP, BLOCK), lambda j: (0, 0)), _bs((1, BLOCK), lambda j: (0, 0))],
        out_shape=[jax.ShapeDtypeStruct((LP, BLOCK), F32), jax.ShapeDtypeStruct((1, BLOCK), F32)],
        compiler_params=_cparams(("arbitrary",)),
    )(raw, bias, ddt, ddtt)


def _ssd_specs(rev):
    ci = (lambda c: N_CHUNK - 1 - c) if rev else (lambda c: c)
    x_spec = _bs((SSD_HEADS, BLOCK, SSD_HD), lambda c: (0, ci(c), 0))
    g_spec = _bs((SSD_GROUPS, BLOCK, SSD_STATE), lambda c: (0, ci(c), 0))
    dtc_spec = _bs((BLOCK, BLOCK), lambda c: (ci(c), 0))
    dtr_spec = _bs((BLOCK, BLOCK), lambda c: (0, ci(c)))
    al_spec = _bs((SSD_HEADS, 1, BLOCK), lambda c: (0, 0, 0))
    st_spec = _bs((None, SSD_HEADS, SSD_HD, SSD_STATE), lambda c: (ci(c), 0, 0, 0))
    return x_spec, g_spec, dtc_spec, dtr_spec, al_spec, st_spec


def _ssd_fwd(x, bm, cm, dtc, dtr, alog, *, name):
    x_spec, g_spec, dtc_spec, dtr_spec, al_spec, st_spec = _ssd_specs(False)

    def body(x_ref, b_ref, c_ref, dtc_ref, dtr_ref, al_ref, y_ref, prev_ref, state):
        @pl.when(pl.program_id(0) == 0)
        def _():
            state[...] = jnp.zeros_like(state)

        prev = state[...]
        prev_ref[...] = prev
        y, new = _ssd_chunk(x_ref[...], b_ref[...], c_ref[...], dtc_ref[...], dtr_ref[...], al_ref[...], prev)
        y_ref[...] = y
        state[...] = new

    return pl.pallas_call(
        body, name=name, grid=(N_CHUNK,),
        in_specs=[x_spec, g_spec, g_spec, dtc_spec, dtr_spec, al_spec], out_specs=[x_spec, st_spec],
        out_shape=[jax.ShapeDtypeStruct((SSD_HEADS, LP, SSD_HD), F32),
                   jax.ShapeDtypeStruct((N_CHUNK, SSD_HEADS, SSD_HD, SSD_STATE), F32)],
        scratch_shapes=[pltpu.VMEM((SSD_HEADS, SSD_HD, SSD_STATE), F32)],
        compiler_params=_cparams(("arbitrary",)),
    )(x, bm, cm, dtc, dtr, alog)


def _ssd_bwd(x, bm, cm, dtc, dtr, alog, prevs, dy, *, name):
    x_spec, g_spec, dtc_spec, dtr_spec, al_spec, st_spec = _ssd_specs(True)

    def body(x_ref, b_ref, c_ref, dtc_ref, dtr_ref, al_ref, prev_ref, dy_ref,
             dx_ref, db_ref, dc_ref, ddtc_ref, ddtr_ref, dal_ref, dstate):
        c = pl.program_id(0)

        @pl.when(c == 0)
        def _():
            dstate[...] = jnp.zeros_like(dstate)

        _, vjp = jax.vjp(_ssd_chunk, x_ref[...], b_ref[...], c_ref[...], dtc_ref[...], dtr_ref[...], al_ref[...],
                         prev_ref[...])
        dx, db, dc, ddtc, ddtr, dal, dprev = vjp((dy_ref[...], dstate[...]))
        dx_ref[...] = dx
        db_ref[...] = db
        dc_ref[...] = dc
        ddtc_ref[...] = ddtc
        ddtr_ref[...] = ddtr
        dstate[...] = dprev

        @pl.when(c == 0)
        def _():
            dal_ref[...] = dal

        @pl.when(c > 0)
        def _():
            dal_ref[...] += dal

    hs = jax.ShapeDtypeStruct((SSD_HEADS, LP, SSD_HD), F32)
    gs = jax.ShapeDtypeStruct((SSD_GROUPS, LP, SSD_STATE), F32)
    return pl.pallas_call(
        body, name=name, grid=(N_CHUNK,),
        in_specs=[x_spec, g_spec, g_spec, dtc_spec, dtr_spec, al_spec, st_spec, x_spec],
        out_specs=[x_spec, g_spec, g_spec, dtc_spec, dtr_spec, al_spec],
        out_shape=[hs, gs, gs, jax.ShapeDtypeStruct((LP, BLOCK), F32),
                   jax.ShapeDtypeStruct((BLOCK, LP), F32), jax.ShapeDtypeStruct((SSD_HEADS, 1, BLOCK), F32)],
        scratch_shapes=[pltpu.VMEM((SSD_HEADS, SSD_HD, SSD_STATE), F32)],
        compiler_params=_cparams(("arbitrary",)),
    )(x, bm, cm, dtc, dtr, alog, prevs, dy)


def _tri_dot(tri, v):
    hi = v.astype(BF16)
    r1 = v - hi.astype(F32)
    mid = r1.astype(BF16)
    lo = (r1 - mid.astype(F32)).astype(BF16)
    t = tri.astype(BF16)
    d = lambda p: lax.dot_general(t, p, _dims(1, 0), preferred_element_type=F32)
    return d(hi) + d(mid) + d(lo)


def _fox_gate_fwd(raw, raw_blk, bias, *, name):
    def body(raw_ref, b_ref, c_ref, ct_ref, carry):
        j = pl.program_id(0)

        @pl.when(j == 0)
        def _():
            carry[...] = jnp.zeros_like(carry)

        rows = j * BLOCK + lax.broadcasted_iota(jnp.int32, (BLOCK, BLOCK), 0)
        lf = jnp.where(rows >= PAD_ROWS, -_softplus(-(raw_ref[...] + b_ref[...])), 0.0)
        li = lax.broadcasted_iota(jnp.int32, (BLOCK, BLOCK), 0)
        si = lax.broadcasted_iota(jnp.int32, (BLOCK, BLOCK), 1)
        cv = _tri_dot(jnp.where(li >= si, 1.0, 0.0), lf) + carry[...]
        c_ref[...] = cv
        ct_ref[...] = cv.T
        carry[...] += jnp.sum(lf, axis=0, keepdims=True)

    return pl.pallas_call(
        body, name=name, grid=(N_CHUNK,),
        in_specs=[_bs((BLOCK, BLOCK), lambda j: (j, raw_blk)), _bs((1, BLOCK), lambda j: (0, 0))],
        out_specs=[_bs((BLOCK, BLOCK), lambda j: (j, 0)), _bs((BLOCK, BLOCK), lambda j: (0, j))],
        out_shape=[jax.ShapeDtypeStruct((LP, BLOCK), F32), jax.ShapeDtypeStruct((BLOCK, LP), F32)],
        scratch_shapes=[pltpu.VMEM((1, BLOCK), F32)], compiler_params=_cparams(("arbitrary",)),
    )(raw, bias)


def _fox_gate_bwd(raw, raw_blk, bias, dc, dct, *, name):
    rj = lambda j: N_CHUNK - 1 - j

    def body(raw_ref, b_ref, dc_ref, dct_ref, draw_ref, db_ref, carry):
        j = pl.program_id(0)

        @pl.when(j == 0)
        def _():
            carry[...] = jnp.zeros_like(carry)

        rows = (N_CHUNK - 1 - j) * BLOCK + lax.broadcasted_iota(jnp.int32, (BLOCK, BLOCK), 0)
        li = lax.broadcasted_iota(jnp.int32, (BLOCK, BLOCK), 0)
        si = lax.broadcasted_iota(jnp.int32, (BLOCK, BLOCK), 1)
        dcv = dc_ref[...] + dct_ref[...].T
        dlf = _tri_dot(jnp.where(li <= si, 1.0, 0.0), dcv) + carry[...]
        carry[...] += jnp.sum(dcv, axis=0, keepdims=True)
        draw = jnp.where(rows >= PAD_ROWS, dlf * (1.0 - _sigmoid(raw_ref[...] + b_ref[...])), 0.0)
        draw_ref[...] = draw
        dsum = jnp.sum(draw, axis=0, keepdims=True)

        @pl.when(j == 0)
        def _():
            db_ref[...] = dsum

        @pl.when(j > 0)
        def _():
            db_ref[...] += dsum

    return pl.pallas_call(
        body, name=name, grid=(N_CHUNK,),
        in_specs=[_bs((BLOCK, BLOCK), lambda j: (rj(j), raw_blk)), _bs((1, BLOCK), lambda j: (0, 0)),
                  _bs((BLOCK, BLOCK), lambda j: (rj(j), 0)), _bs((BLOCK, BLOCK), lambda j: (0, rj(j)))],
        out_specs=[_bs((BLOCK, BLOCK), lambda j: (rj(j), 0)), _bs((1, BLOCK), lambda j: (0, 0))],
        out_shape=[jax.ShapeDtypeStruct((LP, BLOCK), F32), jax.ShapeDtypeStruct((1, BLOCK), F32)],
        scratch_shapes=[pltpu.VMEM((1, BLOCK), F32)], compiler_params=_cparams(("arbitrary",)),
    )(raw, bias, dc, dct)


ATT_W = 256
ATT_QB = 272
ATT_STEPS = LP // ATT_QB
ATT_KEYS = (640, 1152, 1664, LP)
ATT_BLOCKS_PER_CLASS = ATT_STEPS // len(ATT_KEYS)


def _lane_head(width, per, mod=None):
    lane = lax.broadcasted_iota(jnp.int32, (1, width), 1)
    if mod is not None:
        lane = lane % mod
    return lane // per


def _attn_mask(i, kw):
    r = i * ATT_QB + lax.broadcasted_iota(jnp.int32, (ATT_QB, kw), 0)
    c = lax.broadcasted_iota(jnp.int32, (ATT_QB, kw), 1)
    return (c <= r) & ((c >= PAD_ROWS) | (r < PAD_ROWS))


def _attn_by_key_class(i, fn):
    for p, kw in enumerate(ATT_KEYS):
        @pl.when(i // ATT_BLOCKS_PER_CLASS == p)
        def _(kw=kw):
            fn(kw)


def _attn_specs(q, k, v, bias, rope):
    qspec = lambda blk, w=ATT_W: _bs((ATT_QB, w), lambda i: (i, blk))
    fspec = lambda blk, w=ATT_W: _bs((LP, w), lambda i: (0, blk))
    ins = [q[0], k[0], v[0]]
    specs = [qspec(q[1]), fspec(k[1]), fspec(v[1])]
    if bias is not None:
        ins += [bias[0], bias[1]]
        specs += [qspec(0, BLOCK), _bs((BLOCK, LP), lambda i: (0, 0))]
    if rope is not None:
        ins += [rope[0][0], rope[1][0]]
        specs += [qspec(rope[0][1], BLOCK), fspec(rope[1][1], BLOCK)]
    return ins, specs, qspec, fspec


def _attn_fwd(q, k, v, *, scale, name, bias=None, rope=None):
    ins, specs, qspec, fspec = _attn_specs(q, k, v, bias, rope)
    has_bias, has_rope = bias is not None, rope is not None

    def body(*refs):
        it = iter(refs)
        q_ref, k_ref, v_ref = next(it), next(it), next(it)
        if has_bias:
            c_ref, ct_ref = next(it), next(it)
        if has_rope:
            qr_ref, kr_ref = next(it), next(it)
        o_ref, lse_ref = next(it), next(it)
        i = pl.program_id(0)

        def block(kw):
            ok = _attn_mask(i, kw)
            qv, kv, vv = q_ref[...].astype(BF16), k_ref[0:kw, :].astype(BF16), v_ref[0:kw, :].astype(BF16)
            hid, l128 = _lane_head(ATT_W, FOX_HD), _lane_head(BLOCK, 1)
            if has_rope:
                rid = _lane_head(BLOCK, ROPE_HALF, 64)
                qrv, krv = qr_ref[...].astype(BF16), kr_ref[0:kw, :].astype(BF16)
            def head(h, carry):
                o_acc, lse_acc = carry
                s = _raw_bdot(jnp.where(hid == h, qv, 0.0), kv, 1, 1)
                if has_rope:
                    s = s + _raw_bdot(jnp.where(rid == h, qrv, 0.0), krv, 1, 1)
                s = s * scale
                if has_bias:
                    cq = jnp.sum(jnp.where(l128 == h, c_ref[...], 0.0), axis=1, keepdims=True)
                    s = s + (cq - ct_ref[pl.ds(h, 1), 0:kw])
                s = jnp.where(ok, s, NEG)
                m = jnp.max(s, axis=1, keepdims=True)
                p = jnp.exp(s - m)
                l = jnp.sum(p, axis=1, keepdims=True)
                o_acc = jnp.where(hid == h, _raw_bdot(p, vv, 1, 0) / l, o_acc)
                lse_acc = jnp.where(l128 == h, m + jnp.log(l), lse_acc)
                return o_acc, lse_acc

            o_acc, lse_acc = lax.fori_loop(
                0, FOX_HEADS, head, (jnp.zeros((ATT_QB, ATT_W), F32), jnp.zeros((ATT_QB, BLOCK), F32)), unroll=True)
            o_ref[...] = o_acc
            lse_ref[...] = lse_acc

        _attn_by_key_class(i, block)

    return pl.pallas_call(
        body, name=name, grid=(ATT_STEPS,), in_specs=specs, out_specs=[qspec(0), qspec(0, BLOCK)],
        out_shape=[jax.ShapeDtypeStruct((LP, ATT_W), F32), jax.ShapeDtypeStruct((LP, BLOCK), F32)],
        compiler_params=_cparams(("parallel",)),
    )(*ins)


def _attn_bwd(q, k, v, o, lse, do, *, scale, name, bias=None, rope=None):
    ins, specs, qspec, fspec = _attn_specs(q, k, v, bias, rope)
    has_bias, has_rope = bias is not None, rope is not None
    ins += [o, lse, do[0]]
    specs += [qspec(0), qspec(0, BLOCK), qspec(do[1])]

    def body(*refs):
        it = iter(refs)
        q_ref, k_ref, v_ref = next(it), next(it), next(it)
        if has_bias:
            c_ref, ct_ref = next(it), next(it)
        if has_rope:
            qr_ref, kr_ref = next(it), next(it)
        o_ref, lse_ref, do_ref = next(it), next(it), next(it)
        dq_ref, dk_ref, dv_ref = next(it), next(it), next(it)
        if has_bias:
            dc_ref, dct_ref = next(it), next(it)
        if has_rope:
            dqr_ref, dkr_ref = next(it), next(it)
        i = pl.program_id(0)

        @pl.when(i == 0)
        def _():
            dk_ref[...] = jnp.zeros_like(dk_ref)
            dv_ref[...] = jnp.zeros_like(dv_ref)
            if has_rope:
                dkr_ref[...] = jnp.zeros_like(dkr_ref)
            if has_bias:
                dct_ref[...] = jnp.zeros_like(dct_ref)

        def block(kw):
            ok = _attn_mask(i, kw)
            qv, kv, vv = q_ref[...].astype(BF16), k_ref[0:kw, :].astype(BF16), v_ref[0:kw, :].astype(BF16)
            dov, lsev = do_ref[...], lse_ref[...]
            dov_ov = dov * o_ref[...]
            dov = dov.astype(BF16)
            hid, l128 = _lane_head(ATT_W, FOX_HD), _lane_head(BLOCK, 1)
            if has_rope:
                rid = _lane_head(BLOCK, ROPE_HALF, 64)
                qrv, krv = qr_ref[...].astype(BF16), kr_ref[0:kw, :].astype(BF16)

            def head(h, carry):
                dq_acc, aux_acc = carry
                qm = jnp.where(hid == h, qv, 0.0)
                s = _raw_bdot(qm, kv, 1, 1)
                if has_rope:
                    qrm = jnp.where(rid == h, qrv, 0.0)
                    s = s + _raw_bdot(qrm, krv, 1, 1)
                s = s * scale
                if has_bias:
                    cq = jnp.sum(jnp.where(l128 == h, c_ref[...], 0.0), axis=1, keepdims=True)
                    s = s + (cq - ct_ref[pl.ds(h, 1), 0:kw])
                s = jnp.where(ok, s, NEG)
                p = jnp.exp(s - jnp.sum(jnp.where(l128 == h, lsev, 0.0), axis=1, keepdims=True))
                dom = jnp.where(hid == h, dov, 0.0)
                dp = _raw_bdot(dom, vv, 1, 1)
                delta = jnp.sum(jnp.where(hid == h, dov_ov, 0.0), axis=1, keepdims=True)
                ds = p * (dp - delta)
                dsb, pb = ds.astype(BF16), p.astype(BF16)
                dq_acc = jnp.where(hid == h, _raw_bdot(dsb, kv, 1, 0) * scale, dq_acc)
                dk_ref[0:kw, :] += _raw_bdot(dsb, qm, 0, 0) * scale
                dv_ref[0:kw, :] += _raw_bdot(pb, dom, 0, 0)
                if has_rope:
                    aux_acc = jnp.where(rid == h, _raw_bdot(dsb, krv, 1, 0) * scale, aux_acc)
                    dkr_ref[0:kw, :] += _raw_bdot(dsb, qrm, 0, 0) * scale
                if has_bias:
                    aux_acc = jnp.where(l128 == h, jnp.sum(ds, axis=1, keepdims=True), aux_acc)
                    dct_ref[pl.ds(h, 1), 0:kw] -= jnp.sum(ds, axis=0, keepdims=True)
                return dq_acc, aux_acc

            dq_acc, aux_acc = lax.fori_loop(
                0, FOX_HEADS, head, (jnp.zeros((ATT_QB, ATT_W), F32), jnp.zeros((ATT_QB, BLOCK), F32)))
            dq_ref[...] = dq_acc
            if has_bias:
                dc_ref[...] = aux_acc
            if has_rope:
                dqr_ref[...] = aux_acc

        _attn_by_key_class(i, block)

    wide = jax.ShapeDtypeStruct((LP, ATT_W), F32)
    narrow = jax.ShapeDtypeStruct((LP, BLOCK), F32)
    out_specs = [qspec(0), fspec(0), fspec(0)]
    out_shape = [wide, wide, wide]
    if has_bias:
        out_specs += [qspec(0, BLOCK), _bs((BLOCK, LP), lambda i: (0, 0))]
        out_shape += [narrow, jax.ShapeDtypeStruct((BLOCK, LP), F32)]
    if has_rope:
        out_specs += [qspec(0, BLOCK), fspec(0, BLOCK)]
        out_shape += [narrow, narrow]
    return pl.pallas_call(
        body, name=name, grid=(ATT_STEPS,), in_specs=specs, out_specs=out_specs, out_shape=out_shape,
        compiler_params=_cparams(("arbitrary",)),
    )(*ins)


def _loss_head(y, target, *, name):
    tile = 272

    def body(y_ref, t_ref, dy_ref, loss_ref):
        i = pl.program_id(0)
        rows = i * tile + lax.broadcasted_iota(jnp.int32, (tile, D_MODEL), 0)
        err = jnp.where(rows >= BLOCK, y_ref[...] - t_ref[...], 0.0)
        dy_ref[...] = err * (1.0 / D_MODEL)
        part = 0.5 * jnp.sum(jnp.sum(err * err, axis=1, keepdims=True) * (1.0 / D_MODEL), axis=0, keepdims=True)
        part = jnp.broadcast_to(part, (1, BLOCK))

        @pl.when(i == 0)
        def _():
            loss_ref[...] = part

        @pl.when(i > 0)
        def _():
            loss_ref[...] += part

    return pl.pallas_call(
        body, name=name, grid=(LP // tile,),
        in_specs=[_bs((tile, D_MODEL), lambda i: (i, 0)), _bs((tile, D_MODEL), lambda i: (i, 0))],
        out_specs=[_bs((tile, D_MODEL), lambda i: (i, 0)), _bs((1, BLOCK), lambda i: (0, 0))],
        out_shape=[jax.ShapeDtypeStruct((LP, D_MODEL), F32), jax.ShapeDtypeStruct((1, BLOCK), F32)],
        compiler_params=_cparams(("arbitrary",)),
    )(y, target)


def _adamw(w, gs, m, v, *, name, after=()):
    if w.ndim == 2:
        w, m, v = w[None], m[None], v[None]
        squeeze = True
    else:
        squeeze = False
    NL, R, C = w.shape
    assert len(gs) == NL
    CG = gs[0].shape[1]
    tile = _tile(R, 256, 8)

    def body(*refs):
        w_ref, g_refs = refs[0], refs[1:1 + NL]
        m_ref, v_ref = refs[1 + NL:3 + NL]
        go_ref, d_ref, nm_ref, nv_ref = refs[3 + NL + len(after):]
        gv = g_refs[0][:, :C]
        for j in range(1, NL):
            gv = jnp.where(pl.program_id(0) == j, g_refs[j][:, :C], gv)
        nm = ADAM_B1 * m_ref[...] + (1.0 - ADAM_B1) * gv
        nv = ADAM_B2 * v_ref[...] + (1.0 - ADAM_B2) * (gv * gv)
        m_hat = nm / (1.0 - ADAM_B1 ** ADAM_STEP)
        v_hat = nv / (1.0 - ADAM_B2 ** ADAM_STEP)
        go_ref[...] = gv
        d_ref[...] = -ADAM_LR * (m_hat / (jnp.sqrt(v_hat) + ADAM_EPS) + ADAM_WD * w_ref[...])
        nm_ref[...] = nm
        nv_ref[...] = nv

    spec = _bs((None, tile, C), lambda l, i: (l, i, 0))
    gspecs = [_bs((tile, CG), lambda l, i, j=j: (jnp.where(l == j, i, 0), 0)) for j in range(NL)]
    res = pl.pallas_call(
        body, name=name, grid=(NL, R // tile), in_specs=[spec, *gspecs, spec, spec, *[ANY] * len(after)],
        out_specs=[spec] * 4, out_shape=[jax.ShapeDtypeStruct((NL, R, C), F32)] * 4,
        compiler_params=_cparams(("parallel", "parallel")),
    )(w, *gs, m, v, *after)
    return [r[0] for r in res] if squeeze else res


def _my_pos():
    return lax.axis_index("x"), lax.axis_index("y"), lax.axis_index("c")


def _other_chips(x, y):
    return [(1 - x, y), (x, 1 - y), (1 - x, 1 - y)]


def _allgather_chips(shards):
    n = len(shards)
    per = 7

    def body(*refs):
        ins, outs = refs[:n], refs[n:2 * n]
        send_sems, recv_sems = refs[2 * n], refs[2 * n + 1]
        x, y, c = _my_pos()
        chips = _other_chips(x, y)
        sibling, me = (x, y, 1 - c), 2 * x + y

        def cp(a, kk, src, dst, to):
            return pltpu.make_async_remote_copy(src_ref=src, dst_ref=dst, send_sem=send_sems.at[per * a + kk],
                                                recv_sem=recv_sems.at[per * a + kk], device_id=to, device_id_type=MESH)

        sends = []
        for a in range(n):
            for j, chip in enumerate(chips):
                sends.append(cp(a, j, ins[a].at[c], outs[a].at[me, c], (*chip, c)))
            sends.append(cp(a, 3, ins[a], outs[a].at[me], sibling))
        for s in sends:
            s.start()
        for a in range(n):
            for j, chip in enumerate(chips):
                slab = outs[a].at[2 * chip[0] + chip[1], c]
                cp(a, j, slab, slab, (x, y, c)).wait_recv()
                fwd = cp(a, 4 + j, slab, slab, sibling)
                fwd.start()
                sends.append(fwd)
        for a in range(n):
            cp(a, 3, ins[a], outs[a].at[me], (x, y, c)).wait_recv()
            for j, chip in enumerate(chips):
                slab = outs[a].at[2 * chip[0] + chip[1], 1 - c]
                cp(a, 4 + j, slab, slab, (x, y, c)).wait_recv()
        for s in sends:
            s.wait_send()

    return pl.pallas_call(
        body, name="allgather_chips", in_specs=[ANY] * n, out_specs=[ANY] * n,
        out_shape=[jax.ShapeDtypeStruct((N_CHIPS,) + s.shape, s.dtype) for s in shards],
        scratch_shapes=[pltpu.SemaphoreType.DMA((per * n,)), pltpu.SemaphoreType.DMA((per * n,))],
    )(*shards)


def _rs_swap_rows(gs, tag):
    n = len(gs)

    def body(*refs):
        ins, outs = refs[:n], refs[n:2 * n]
        send_sems, recv_sems = refs[2 * n], refs[2 * n + 1]
        x, y, c = _my_pos()
        cps = []
        for a in range(n):
            half = ins[a].shape[1] // 2
            cps.append(pltpu.make_async_remote_copy(
                src_ref=ins[a].at[:, pl.ds((1 - c) * half, half)], dst_ref=outs[a], send_sem=send_sems.at[a],
                recv_sem=recv_sems.at[a], device_id=(x, y, 1 - c), device_id_type=MESH))
        for cp in cps:
            cp.start()
        for cp in cps:
            cp.wait()

    return pl.pallas_call(
        body, name=f"rs_swap_rows_{tag}", in_specs=[ANY] * n, out_specs=[ANY] * n,
        out_shape=[jax.ShapeDtypeStruct((N_CHIPS, g.shape[1] // 2, g.shape[2]), g.dtype) for g in gs],
        scratch_shapes=[pltpu.SemaphoreType.DMA((n,)), pltpu.SemaphoreType.DMA((n,))],
    )(*gs)


RS_ADD_VMEM_BYTES = 24 * 1024 * 1024


def _rs_tile(H, C, n):
    return _tile(H, max(16, RS_ADD_VMEM_BYTES // (28 * n * C)), 16)


def _rs_add_pair(gs, rs, pos, *, name):
    n = len(gs)
    _, H, C = rs[0].shape
    tile = _rs_tile(H, C, n)
    nt = H // tile

    def body(pos_ref, *refs):
        for a in range(n):
            s = refs[a][...] + refs[n + a][...]
            refs[2 * n + 2 * a][...] = s
            refs[2 * n + 2 * a + 1][...] = s.astype(BF16)

    spec = _bs((None, tile, C), lambda k, i, pos_ref: (k, i, 0))
    g_spec = _bs((None, tile, C), lambda k, i, pos_ref: (k, pos_ref[1] * nt + i, 0))
    grid_spec = pltpu.PrefetchScalarGridSpec(
        num_scalar_prefetch=1, grid=(N_CHIPS, nt), in_specs=[g_spec] * n + [spec] * n, out_specs=[spec] * (2 * n))
    res = pl.pallas_call(
        body, name=name, grid_spec=grid_spec,
        out_shape=[jax.ShapeDtypeStruct((N_CHIPS, H, C), F32), jax.ShapeDtypeStruct((N_CHIPS, H, C), BF16)] * n,
        compiler_params=_cparams(("parallel", "parallel")),
    )(pos, *gs, *rs)
    return [(res[2 * a], res[2 * a + 1]) for a in range(n)]


def _exchange_copies(srcs, lands, send_sems, recv_sems):
    x, y, c = _my_pos()
    starts, landing = [], []
    for a in range(len(srcs)):
        for j, chip in enumerate(_other_chips(x, y)):
            sems = dict(send_sem=send_sems.at[3 * a + j], recv_sem=recv_sems.at[3 * a + j], device_id_type=MESH)
            starts.append(pltpu.make_async_remote_copy(
                src_ref=srcs[a].at[2 * chip[0] + chip[1]], dst_ref=lands[a].at[j], device_id=(*chip, c), **sems))
            landing.append(pltpu.make_async_remote_copy(
                src_ref=lands[a].at[j], dst_ref=lands[a].at[j], device_id=(x, y, c), **sems))
    return starts, landing


def _gather_copies(srcs, lands, send_sems, recv_sems):
    x, y, c = _my_pos()
    me = 2 * x + y
    starts, landing = [], []
    for a in range(len(srcs)):
        half = srcs[a].shape[0] // 2
        mine = pl.ds(c * half, half)
        for j, chip in enumerate(_other_chips(x, y)):
            sems = dict(send_sem=send_sems.at[3 * a + j], recv_sem=recv_sems.at[3 * a + j], device_id_type=MESH)
            starts.append(pltpu.make_async_remote_copy(
                src_ref=srcs[a].at[mine], dst_ref=lands[a].at[me, mine], device_id=(*chip, c), **sems))
            slab = lands[a].at[2 * chip[0] + chip[1], mine]
            landing.append(pltpu.make_async_remote_copy(src_ref=slab, dst_ref=slab, device_id=(x, y, c), **sems))
    return starts, landing


HBM = pl.BlockSpec(memory_space=pltpu.HBM)
SEM = pl.BlockSpec(memory_space=pltpu.SEMAPHORE)


def _ici_start(copies_fn, srcs, land_shapes, *, name, after=(), sems_per_array=3):
    n, na = len(srcs), len(after)

    def body(*refs):
        starts, _ = copies_fn(refs[:n], refs[n:2 * n], refs[2 * n + na], refs[2 * n + na + 1])
        for cp in starts:
            cp.start()
        refs[-1][...] = jnp.zeros_like(refs[-1])

    sems = pltpu.SemaphoreType.DMA((sems_per_array * n,))
    hbm = lambda s: pltpu.HBM(s.shape, s.dtype)
    lands = [pltpu.with_memory_space_constraint(
        lax.empty(s.shape, s.dtype) if isinstance(s, jax.ShapeDtypeStruct) else s, pltpu.HBM) for s in land_shapes]
    res = pl.pallas_call(
        body, name=name, in_specs=[HBM] * (2 * n) + [ANY] * na,
        out_specs=(SEM, SEM, *[HBM] * (2 * n), pl.BlockSpec(memory_space=pltpu.VMEM)),
        out_shape=(sems, sems, *[hbm(s) for s in srcs], *[hbm(s) for s in land_shapes],
                   jax.ShapeDtypeStruct((8, BLOCK), F32)),
        input_output_aliases={i: 2 + i for i in range(2 * n)},
        compiler_params=pltpu.CompilerParams(has_side_effects=pltpu.SideEffectType.DATAFLOW_SIDE_EFFECTING),
    )(*[pltpu.with_memory_space_constraint(s, pltpu.HBM) for s in srcs], *lands, *after)
    return res[0], res[1], list(res[2:2 + n]), list(res[2 + n:2 + 2 * n]), res[-1]


def _ici_wait(copies_fn, send_sems, recv_sems, srcs, lands, after, *, name):
    n = len(srcs)
    after = list(after) if isinstance(after, (list, tuple)) else [after]

    def body(*refs):
        starts, landing = copies_fn(refs[:n], refs[n:2 * n], refs[2 * n], refs[2 * n + 1])
        for cp in starts:
            cp.wait_send()
        for cp in landing:
            cp.wait_recv()

    hbm = lambda s: pltpu.HBM(s.shape, s.dtype)
    res = pl.pallas_call(
        body, name=name, in_specs=[*[HBM] * (2 * n), SEM, SEM, *[ANY] * len(after)], out_specs=[HBM] * (2 * n),
        out_shape=[*[hbm(s) for s in srcs], *[hbm(s) for s in lands]],
        input_output_aliases={i: i for i in range(2 * n)},
        compiler_params=pltpu.CompilerParams(has_side_effects=pltpu.SideEffectType.DATAFLOW_SIDE_EFFECTING),
    )(*srcs, *lands, send_sems, recv_sems, *after)
    return list(res[:n]), list(res[n:])


D2D_COPIES = 4


def _d2d_copies(ins, outs, send_sems, recv_sems):
    x, y, c = _my_pos()
    me, sibling = 2 * x + y, (x, y, 1 - c)
    starts, landing = [], []
    for a in range(len(ins)):
        half = ins[a].shape[0] // 2
        mine, theirs = pl.ds(c * half, half), pl.ds((1 - c) * half, half)
        pairs = [(ins[a], outs[a].at[me], outs[a].at[me])]
        for chip in _other_chips(x, y):
            k = 2 * chip[0] + chip[1]
            pairs.append((outs[a].at[k, mine], outs[a].at[k, mine], outs[a].at[k, theirs]))
        for j, (src, dst, lands_here) in enumerate(pairs):
            sems = dict(send_sem=send_sems.at[D2D_COPIES * a + j], recv_sem=recv_sems.at[D2D_COPIES * a + j],
                        device_id_type=MESH)
            starts.append(pltpu.make_async_remote_copy(src_ref=src, dst_ref=dst, device_id=sibling, **sems))
            landing.append(pltpu.make_async_remote_copy(src_ref=lands_here, dst_ref=lands_here, device_id=(x, y, c),
                                                        **sems))
    return starts, landing


def _gather_d2d(shards, lands, tag):
    n = len(shards)

    def body(*refs):
        starts, landing = _d2d_copies(refs[:n], refs[2 * n:3 * n], refs[3 * n], refs[3 * n + 1])
        for cp in starts:
            cp.start()
        for cp in landing:
            cp.wait_recv()
        for cp in starts:
            cp.wait_send()

    return pl.pallas_call(
        body, name=f"gather_d2d_{tag}", in_specs=[ANY] * (2 * n), out_specs=[ANY] * n,
        out_shape=[jax.ShapeDtypeStruct(s.shape, s.dtype) for s in lands],
        input_output_aliases={n + a: a for a in range(n)},
        scratch_shapes=[pltpu.SemaphoreType.DMA((D2D_COPIES * n,)), pltpu.SemaphoreType.DMA((D2D_COPIES * n,))],
    )(*shards, *lands)


def _rs_add_chips(p32s, r16s, pos, *, name):
    n = len(p32s)
    _, H, C = p32s[0].shape
    tile = _rs_tile(H, C, n)
    nt = H // tile

    def body(pos_ref, *refs):
        for a in range(n):
            p_ref, r_ref = refs[a], refs[n + a]
            refs[2 * n + a][...] = ((p_ref[...] + r_ref[0].astype(F32)) + r_ref[1].astype(F32)) + r_ref[2].astype(F32)

    grid_spec = pltpu.PrefetchScalarGridSpec(
        num_scalar_prefetch=1, grid=(nt,),
        in_specs=[_bs((None, tile, C), lambda i, pos_ref: (pos_ref[0], i, 0))] * n
        + [_bs((3, tile, C), lambda i, pos_ref: (0, i, 0))] * n,
        out_specs=[_bs((tile, C), lambda i, pos_ref: (pos_ref[1] * nt + i, 0))] * n)
    return pl.pallas_call(
        body, name=name, grid_spec=grid_spec, out_shape=[jax.ShapeDtypeStruct((2 * H, C), F32)] * n,
        compiler_params=_cparams(("parallel",)),
    )(pos, *p32s, *r16s)


def _rs_join_rows(fs, tag):
    n = len(fs)

    def body(*refs):
        outs = refs[n:2 * n]
        send_sems, recv_sems = refs[2 * n], refs[2 * n + 1]
        x, y, c = _my_pos()
        for a in range(n):
            half = outs[a].shape[0] // 2
            mine = outs[a].at[pl.ds(c * half, half)]
            pltpu.make_async_remote_copy(src_ref=mine, dst_ref=mine, send_sem=send_sems.at[a],
                                         recv_sem=recv_sems.at[a], device_id=(x, y, 1 - c), device_id_type=MESH).start()
        for a in range(n):
            half = outs[a].shape[0] // 2
            pltpu.make_async_remote_copy(
                src_ref=outs[a].at[pl.ds(c * half, half)], dst_ref=outs[a].at[pl.ds((1 - c) * half, half)],
                send_sem=send_sems.at[a], recv_sem=recv_sems.at[a], device_id=(x, y, 1 - c), device_id_type=MESH).wait()

    return pl.pallas_call(
        body, name=f"rs_join_rows_{tag}", in_specs=[ANY] * n, out_specs=[ANY] * n,
        out_shape=[jax.ShapeDtypeStruct(f.shape, f.dtype) for f in fs],
        input_output_aliases={a: a for a in range(n)},
        scratch_shapes=[pltpu.SemaphoreType.DMA((n,)), pltpu.SemaphoreType.DMA((n,))],
    )(*fs)


def _pos_vector():
    x, y, c = _my_pos()
    return jnp.stack([2 * x + y, c]).astype(jnp.int32)


def _swap_copies(srcs, lands, send_sems, recv_sems):
    x, y, c = _my_pos()
    starts, landing = [], []
    for a in range(len(srcs)):
        half = srcs[a].shape[1] // 2
        sems = dict(send_sem=send_sems.at[3 * a], recv_sem=recv_sems.at[3 * a], device_id_type=MESH)
        starts.append(pltpu.make_async_remote_copy(
            src_ref=srcs[a].at[:, pl.ds((1 - c) * half, half)], dst_ref=lands[a], device_id=(x, y, 1 - c), **sems))
        landing.append(pltpu.make_async_remote_copy(src_ref=lands[a], dst_ref=lands[a], device_id=(x, y, c), **sems))
    return starts, landing


def _swap_land_shapes(gs):
    return [jax.ShapeDtypeStruct((N_CHIPS, g.shape[1] // 2, g.shape[2]), g.dtype) for g in gs]


def _same_shape_runs(arrays):
    runs, start = [], 0
    for i in range(1, len(arrays) + 1):
        if i == len(arrays) or arrays[i].shape != arrays[start].shape:
            runs.append((start, i))
            start = i
    return runs


def _rs_add_pairs(gs, r1, names, tag):
    pos = _pos_vector()
    out = []
    for a, b in _same_shape_runs(gs):
        out += _rs_add_pair(gs[a:b], r1[a:b], pos, name=f"rs_add_pair_{tag}_{names[a]}")
    return out


def _rs_pair_sums(gs, names, tag):
    return _rs_add_pairs(gs, _rs_swap_rows(gs, tag), names, tag)


def _rs_finish(pairs, r2, names, tag):
    pos = _pos_vector()
    p32s = [p[0] for p in pairs]
    fs = []
    for a, b in _same_shape_runs(p32s):
        fs += _rs_add_chips(p32s[a:b], r2[a:b], pos, name=f"rs_add_chips_{tag}_{names[a]}")
    return _rs_join_rows(fs, tag)


def _exchange_land_shapes(pairs):
    return [jax.ShapeDtypeStruct((3,) + p[1].shape[1:], p[1].dtype) for p in pairs]


def _allreduce_small(buf):
    R, W = buf.shape

    def body(b_ref, o_ref, gather, send_sems, recv_sems):
        x, y, c = _my_pos()
        me = 4 * x + 2 * y + c
        gather[me] = b_ref[...]
        cps = []
        for d in range(1, 8):
            peer = (x ^ (d >> 2), y ^ ((d >> 1) & 1), c ^ (d & 1))
            cps.append(pltpu.make_async_remote_copy(
                src_ref=b_ref, dst_ref=gather.at[me], send_sem=send_sems.at[d - 1], recv_sem=recv_sems.at[d - 1],
                device_id=peer, device_id_type=MESH))
        for cp in cps:
            cp.start()
        for d in range(1, 8):
            pltpu.make_async_remote_copy(
                src_ref=b_ref, dst_ref=gather.at[me ^ d], send_sem=send_sems.at[d - 1], recv_sem=recv_sems.at[d - 1],
                device_id=(x, y, c), device_id_type=MESH).wait_recv()
        for cp in cps:
            cp.wait_send()
        acc = gather[0]
        for d in range(1, 8):
            acc = acc + gather[d]
        o_ref[...] = acc

    vm = pl.BlockSpec(memory_space=pltpu.VMEM)
    return pl.pallas_call(
        body, name="allreduce_small", in_specs=[vm], out_specs=vm, out_shape=jax.ShapeDtypeStruct((R, W), F32),
        scratch_shapes=[pltpu.VMEM((8, R, W), F32), pltpu.SemaphoreType.DMA((7,)), pltpu.SemaphoreType.DMA((7,))],
    )(buf)


def _heads(a, h, d):
    return a.reshape(a.shape[0], h, d).transpose(1, 0, 2)


def _unheads(a):
    h, L, d = a.shape
    return a.transpose(1, 0, 2).reshape(L, h * d)


def _rope_tables():
    pos = jnp.maximum(jnp.arange(LP, dtype=F32) - PAD_ROWS, 0.0)
    inv_freq = 1.0 / (ROPE_THETA ** (jnp.arange(0, MLA_ROPE, 2, dtype=F32) / MLA_ROPE))
    ang = pos[:, None] * inv_freq[None, :]
    cos, sin = jnp.tile(jnp.cos(ang), (1, MLA_HEADS)), jnp.tile(jnp.sin(ang), (1, MLA_HEADS))
    return jnp.concatenate([cos, cos], axis=1), jnp.concatenate([-sin, sin], axis=1)


def _lane_pad(a, width=BLOCK):
    return jnp.pad(a, ((0, 0), (0, width - a.shape[1])))


def _pad_in_proj(w):
    sl = lambda start, size: w[:, start:start + size]
    return jnp.concatenate([
        sl(OC_Z, 512), sl(OC_XBC, 768), sl(OC_FQ, 256), sl(OC_FK, 256), sl(OC_FV, 256), sl(OC_CQ, 256), sl(OC_CKV, 128),
        _lane_pad(sl(OC_DT, SSD_HEADS)), _lane_pad(sl(OC_FR, FOX_HEADS)),
        jnp.tile(sl(OC_KR, ROPE_HALF), (1, MLA_HEADS)), jnp.tile(sl(OC_KR + ROPE_HALF, ROPE_HALF), (1, MLA_HEADS))], axis=1)


def _in_proj_grad_chunks(wp):
    rope = lambda start: wp[:, start:start + 64].reshape(wp.shape[0], MLA_HEADS, ROPE_HALF).sum(axis=1)
    segs = [(wp, PC_Z, 512), (wp, PC_XBC, 768), (wp, PC_DT, SSD_HEADS), (wp, PC_FQ, 256), (wp, PC_FK, 256),
            (wp, PC_FV, 256), (wp, PC_FR, FOX_HEADS), (wp, PC_CQ, 256), (wp, PC_CKV, 128),
            (rope(PC_KR), 0, ROPE_HALF), (rope(PC_KR + 64), 0, ROPE_HALF)]
    chunks = []
    for k in range(N_CHIPS):
        lo, hi, pos, pieces = k * IN_SHARD, (k + 1) * IN_SHARD, 0, []
        for arr, start, size in segs:
            a, b = max(lo, pos), min(hi, pos + size)
            if a < b:
                pieces.append(arr[:, start + a - pos:start + b - pos])
            pos += size
        pieces.append(jnp.zeros((wp.shape[0], IN_SHARD_P - IN_SHARD), wp.dtype))
        chunks.append(jnp.concatenate(pieces, axis=1))
    return jnp.stack(chunks)


def _regroup_uq(w):
    w3 = w.reshape(w.shape[0], MLA_HEADS, MLA_NOPE + MLA_ROPE)
    return jnp.concatenate([w3[:, :, :MLA_NOPE].reshape(w.shape[0], -1),
                            w3[:, :, MLA_NOPE:MLA_NOPE + ROPE_HALF].reshape(w.shape[0], -1),
                            w3[:, :, MLA_NOPE + ROPE_HALF:].reshape(w.shape[0], -1)], axis=1)


def _ungroup_uq(wp):
    n = wp.shape[0]
    return jnp.concatenate([wp[:, :256].reshape(n, MLA_HEADS, MLA_NOPE), wp[:, 256:320].reshape(n, MLA_HEADS, ROPE_HALF),
                            wp[:, 320:].reshape(n, MLA_HEADS, ROPE_HALF)], axis=2).reshape(n, -1)


def _regroup_ukv(w):
    w3 = w.reshape(w.shape[0], MLA_HEADS, MLA_NOPE + MLA_V)
    return jnp.concatenate([w3[:, :, :MLA_NOPE].reshape(w.shape[0], -1), w3[:, :, MLA_NOPE:].reshape(w.shape[0], -1)],
                           axis=1)


def _ungroup_ukv(wp):
    n = wp.shape[0]
    return jnp.concatenate([wp[:, :256].reshape(n, MLA_HEADS, MLA_NOPE), wp[:, 256:].reshape(n, MLA_HEADS, MLA_V)],
                           axis=2).reshape(n, -1)


TMF = 1088
N_IF = LP // TMF


def _chunk_rows_dx(g, w, l, chunk_h, *, name):
    N = w.shape[2]
    return _mm_core(g, w, a_spec=_bs((TMF, N), lambda i, j, k: (i, 0)),
                    b_spec=_bs((None, chunk_h, N), lambda i, j, k: (j, 0, 0)),
                    o_spec=_bs((TMF, chunk_h), lambda i, j, k: (i, j)), grid=(N_IF, N_CHIPS, 1),
                    out_shape=(LP, N_CHIPS * chunk_h), ca=1, cb=1, name=name)


def _chunk_rows_dw(a, g, chunk_h, *, name):
    N = g.shape[1]
    return _mm_core(a, g, a_spec=_bs((LP, chunk_h), lambda i, j, k: (0, i)), b_spec=_bs((LP, N), lambda i, j, k: (0, 0)),
                    o_spec=_bs((None, chunk_h, N), lambda i, j, k: (i, 0, 0)), grid=(N_CHIPS, 1, 1),
                    out_shape=(N_CHIPS, chunk_h, N), ca=0, cb=0, name=name)


def _ffn_up_swiglu(h, wg, wu, *, name, after=()):
    def body(h_ref, wg_ref, wu_ref, *refs):
        g_ref, u_ref, a_ref = refs[len(after):]
        hb = h_ref[...].astype(BF16)
        g = _raw_bdot(hb, wg_ref[...], 1, 1)
        u = _raw_bdot(hb, wu_ref[...], 1, 1)
        g_ref[...] = g
        u_ref[...] = u
        a_ref[...] = (_silu(g) * u).astype(a_ref.dtype)

    w_spec = _bs((None, HP, D_MODEL), lambda i, j: (j, 0, 0))
    o_spec = _bs((TMF, HP), lambda i, j: (i, j))
    return pl.pallas_call(
        body, name=name, grid=(N_IF, N_CHIPS),
        in_specs=[_bs((TMF, D_MODEL), lambda i, j: (i, 0)), w_spec, w_spec, *[ANY] * len(after)],
        out_specs=[o_spec] * 3,
        out_shape=[jax.ShapeDtypeStruct((LP, FP), F32), jax.ShapeDtypeStruct((LP, FP), F32),
                   jax.ShapeDtypeStruct((LP, FP), BF16)],
        compiler_params=_cparams(("parallel", "parallel")),
    )(h, wg, wu, *after)


def _ffn_down_dx_swiglu(do, wd, g, u, *, name):
    def body(do_ref, wd_ref, g_ref, u_ref, dg_ref, du_ref):
        dact = _raw_bdot(do_ref[...], wd_ref[...], 1, 1)
        gv = g_ref[...]
        sig = _sigmoid(gv)
        dg_ref[...] = (dact * u_ref[...] * (sig * (1.0 + gv * (1.0 - sig)))).astype(dg_ref.dtype)
        du_ref[...] = (dact * (gv * sig)).astype(du_ref.dtype)

    blk = _bs((TMF, HP), lambda i, j: (i, j))
    return pl.pallas_call(
        body, name=name, grid=(N_IF, N_CHIPS),
        in_specs=[_bs((TMF, D_MODEL), lambda i, j: (i, 0)), _bs((None, HP, D_MODEL), lambda i, j: (j, 0, 0)), blk, blk],
        out_specs=[blk, blk], out_shape=[jax.ShapeDtypeStruct((LP, FP), BF16)] * 2,
        compiler_params=_cparams(("parallel", "parallel")),
    )(do, wd, g, u)


def _ffn_gate_up_dw(dg, du, h, *, name):
    def body(dg_ref, du_ref, h_ref, wg_ref, wu_ref):
        hb = h_ref[...].astype(BF16)
        wg_ref[...] = _raw_bdot(dg_ref[...], hb, 0, 0)
        wu_ref[...] = _raw_bdot(du_ref[...], hb, 0, 0)

    a_spec = _bs((LP, HP), lambda k: (0, k))
    o_spec = _bs((None, HP, D_MODEL), lambda k: (k, 0, 0))
    return pl.pallas_call(
        body, name=name, grid=(N_CHIPS,), in_specs=[a_spec, a_spec, _bs((LP, D_MODEL), lambda k: (0, 0))],
        out_specs=[o_spec, o_spec], out_shape=[jax.ShapeDtypeStruct((N_CHIPS, HP, D_MODEL), F32)] * 2,
        compiler_params=_cparams(("parallel",)),
    )(dg, du, h)


def _ffn_gate_up_dx(dg, du, wg, wu, add, *, name):
    def body(dg_ref, du_ref, wg_ref, wu_ref, add_ref, o_ref, acc_ref):
        k = pl.program_id(1)

        @pl.when(k == 0)
        def _():
            acc_ref[...] = jnp.zeros_like(acc_ref)

        acc_ref[...] += _raw_bdot(dg_ref[...], wg_ref[...], 1, 0) + _raw_bdot(du_ref[...], wu_ref[...], 1, 0)

        @pl.when(k == N_CHIPS - 1)
        def _():
            o_ref[...] = acc_ref[...] + add_ref[...]

    a_spec = _bs((TMF, HP), lambda i, k: (i, k))
    w_spec = _bs((None, HP, D_MODEL), lambda i, k: (k, 0, 0))
    o_spec = _bs((TMF, D_MODEL), lambda i, k: (i, 0))
    return pl.pallas_call(
        body, name=name, grid=(N_IF, N_CHIPS), in_specs=[a_spec, a_spec, w_spec, w_spec, o_spec], out_specs=o_spec,
        out_shape=jax.ShapeDtypeStruct((LP, D_MODEL), F32), scratch_shapes=[pltpu.VMEM((TMF, D_MODEL), F32)],
        compiler_params=_cparams(("parallel", "arbitrary")),
    )(dg, du, wg, wu, add)


def _chunk_rows_mm_res_ln(a, w, chunk_h, h, gam, bet, scale, *, name):
    res_ln = _make_res_ln_fn(scale)

    def body(a_ref, w_ref, h_ref, g_ref, b_ref, o_ref, y_ref, yb_ref, acc_ref):
        k = pl.program_id(1)

        @pl.when(k == 0)
        def _():
            acc_ref[...] = jnp.zeros_like(acc_ref)

        acc_ref[...] += _raw_bdot(a_ref[...], w_ref[...], 1, 0)

        @pl.when(k == N_CHIPS - 1)
        def _():
            o = acc_ref[...]
            o_ref[...] = o
            (y,) = res_ln(0, h_ref[...], o, g_ref[...], b_ref[...])
            y_ref[...] = y
            yb_ref[...] = y.astype(yb_ref.dtype)

    row = _bs((TMF, D_MODEL), lambda i, k: (i, 0))
    par = _bs((1, D_MODEL), lambda i, k: (0, 0))
    return pl.pallas_call(
        body, name=name, grid=(N_IF, N_CHIPS),
        in_specs=[_bs((TMF, chunk_h), lambda i, k: (i, k)), _bs((None, chunk_h, D_MODEL), lambda i, k: (k, 0, 0)), row,
                  par, par],
        out_specs=[row, row, row],
        out_shape=[jax.ShapeDtypeStruct((LP, D_MODEL), F32)] * 2 + [jax.ShapeDtypeStruct((LP, D_MODEL), BF16)],
        scratch_shapes=[pltpu.VMEM((TMF, D_MODEL), F32)], compiler_params=_cparams(("parallel", "arbitrary")),
    )(a, w, h, gam, bet)


def _ffn_fwd(hp, W, pre, l, gam, bet, tag, after=()):
    h, hb = hp
    g, u, act = _ffn_up_swiglu(hb, W[pre + "_w_gate"][l], W[pre + "_w_up"][l], name=f"{tag}_up_swiglu", after=after)
    o, out, outb = _chunk_rows_mm_res_ln(act, W[pre + "_w_down"][l], HP, h, gam, bet, 0.5, name=f"{tag}_down_ln")
    return (out, outb), (h, hb, g, u, act, o)


def _ffn_bwd(dout, saved, W, pre, l, gam, bet, GB, tag):
    h, hb, g, u, act, o = saved
    (dh_a, do), (dgam, dbet) = _rowwise_bwd(_make_res_ln_fn(0.5), [h, o], [gam, bet], [dout], name=f"{tag}_ln_bwd",
                                            tile=272, grad_dtypes=[F32, BF16])
    dg, du = _ffn_down_dx_swiglu(do, W[pre + "_w_down"][l], g, u, name=f"{tag}_down_dx_swiglu")
    GB[pre + "_w_down"] = _chunk_rows_dw(act, do, HP, name=f"{tag}_down_dw")
    GB[pre + "_w_gate"], GB[pre + "_w_up"] = _ffn_gate_up_dw(dg, du, hb, name=f"{tag}_gate_up_dw")
    dh = _ffn_gate_up_dx(dg, du, W[pre + "_w_gate"][l], W[pre + "_w_up"][l], dh_a, name=f"{tag}_gate_up_dx")
    return dh, dgam, dbet


def _mixer_fwd(hp1, W, l, cosf, sins, after=()):
    h1, h1b = hp1
    tag = f"l{l}"
    proj = _mm(h1b, W["w_in_p"][l], name=f"{tag}_in_proj", after=after)
    sv = {"h1": h1, "h1b": h1b, "proj": proj}
    conv_w, conv_b = W["conv_w"][l], W["conv_b"][l][None]
    xc = _conv_fwd(proj, PC_XBC // BLOCK, conv_w, conv_b, name=f"{tag}_conv")
    dt_bias = _lane_pad(W["dt_bias"][l][None])
    dtc, dtr = _ssd_dt_fwd(proj, PC_DT // BLOCK, dt_bias, name=f"{tag}_ssd_dt")
    xh = _heads(xc[:, :SSD_D], SSD_HEADS, SSD_HD)
    bm = _heads(xc[:, SSD_D:SSD_D + 128], SSD_GROUPS, SSD_STATE)
    cm = _heads(xc[:, SSD_D + 128:], SSD_GROUPS, SSD_STATE)
    alog = jnp.broadcast_to(W["a_log"][l][:, None, None], (SSD_HEADS, 1, BLOCK))
    yh, prevs = _ssd_fwd(xh, bm, cm, dtc, dtr, alog, name=f"{tag}_ssd")
    y_raw = _unheads(yh)
    dskip = jnp.repeat(W["d_skip"][l], SSD_HD)[None]
    normg = W["ssd_norm_g"][l][None]
    post_rows = [y_raw, (xc, 256, 0), (proj, 256, PC_Z // 256)]
    (y_ssd,) = _rowwise(_ssd_post_fn, post_rows, [dskip, normg], [SSD_D], name=f"{tag}_ssd_post", tile=272,
                        ncol=SSD_GROUPS)
    sv.update(conv_w=conv_w, conv_b=conv_b, dt_bias=dt_bias, xh=xh, bm=bm, cm=cm, dtc=dtc, dtr=dtr, alog=alog,
              prevs=prevs, post_rows=post_rows, dskip=dskip, normg=normg)
    f_b = _lane_pad(W["fox_f_b"][l][None])
    cg, cgt = _fox_gate_fwd(proj, PC_FR // BLOCK, f_b, name=f"{tag}_fox_gate")
    fox_qkv = ((proj, PC_FQ // ATT_W), (proj, PC_FK // ATT_W), (proj, PC_FV // ATT_W))
    y_fox, lse_f = _attn_fwd(*fox_qkv, scale=FOX_HD ** -0.5, name=f"{tag}_fox_attn", bias=(cg, cgt))
    sv.update(f_b=f_b, cg=cg, cgt=cgt, fox_qkv=fox_qkv, y_fox=y_fox, lse_f=lse_f)
    gq, gkv = W["mla_q_norm_g"][l][None], W["mla_kv_norm_g"][l][None]
    norm_rows = [(proj, 256, PC_CQ // 256), (proj, BLOCK, PC_CKV // BLOCK)]
    qn, cn = _rowwise(_mla_norm_fn, norm_rows, [gq, gkv], [MLA_Q_LORA, MLA_KV_LORA], name=f"{tag}_mla_norm", tile=272,
                      out_dtypes=[BF16, BF16])
    qh = _mm(qn, W["mla_w_uq_p"][l], name=f"{tag}_mla_uq")
    kvh = _mm(cn, W["mla_w_ukv_p"][l], name=f"{tag}_mla_ukv")
    qr, kr = _rowwise(_rope_fn, [(qh, BLOCK, 2), (proj, BLOCK, PC_KR // BLOCK), cosf, sins], [], [BLOCK, BLOCK],
                      name=f"{tag}_rope", tile=272)
    mla_qkv = ((qh, 0), (kvh, 0), (kvh, 1))
    y_mla, lse_m = _attn_fwd(*mla_qkv, scale=(MLA_NOPE + MLA_ROPE) ** -0.5, name=f"{tag}_mla_attn",
                             rope=((qr, 0), (kr, 0)))
    sv.update(gq=gq, gkv=gkv, norm_rows=norm_rows, qn=qn, cn=cn, qr=qr, kr=kr, mla_qkv=mla_qkv, y_mla=y_mla, lse_m=lse_m)
    ycat = jnp.concatenate([y_ssd, y_fox, y_mla], axis=1).astype(BF16)
    mix, h2, h2b = _chunk_rows_mm_res_ln(ycat, W["w_out"][l], 256, h1, W["ln2_g"][l][None], W["ln2_b"][l][None], 1.0,
                                    name=f"{tag}_out_proj_ln2")
    sv.update(mix=mix, ycat=ycat)
    return (h2, h2b), sv


def _mixer_bwd(dh2, sv, W, l, cosf, sins, GB, zero=0.0):
    tag = f"l{l}"
    G = {}
    proj = sv["proj"]
    ln2g, ln2b = W["ln2_g"][l][None] + zero, W["ln2_b"][l][None]
    (dh1_a, dmix), (dln2g, dln2b) = _rowwise_bwd(
        _make_res_ln_fn(1.0), [sv["h1"], sv["mix"]], [ln2g, ln2b], [dh2], name=f"{tag}_ln2_bwd", tile=272,
        grad_dtypes=[F32, BF16])
    G["ln2_g"], G["ln2_b"] = dln2g[0], dln2b[0]
    dycat = _chunk_rows_dx(dmix, W["w_out"][l], l, 256, name=f"{tag}_out_proj_dx")
    GB["w_out"] = _chunk_rows_dw(sv["ycat"], dmix, 256, name=f"{tag}_out_proj_dw")
    (dy_raw, dxs_a, dz), (ddskip, dnormg) = _rowwise_bwd(
        _ssd_post_fn, sv["post_rows"], [sv["dskip"], sv["normg"]], [dycat[:, :SSD_D]],
        name=f"{tag}_ssd_post_bwd", tile=272, ncol=SSD_GROUPS)
    G["ssd_norm_g"] = dnormg[0]
    G["d_skip"] = ddskip.reshape(SSD_HEADS, SSD_HD).sum(axis=1)
    dxh, dbm, dcm, ddtc, ddtr, dal = _ssd_bwd(sv["xh"], sv["bm"], sv["cm"], sv["dtc"], sv["dtr"], sv["alog"],
                                              sv["prevs"], _heads(dy_raw, SSD_HEADS, SSD_HD), name=f"{tag}_ssd_bwd")
    G["a_log"] = dal[:, 0, 0]
    dxc = jnp.concatenate([dxs_a + _unheads(dxh), _unheads(dbm), _unheads(dcm)], axis=1)
    dxbc, G["conv_w"], dconv_b = _conv_bwd(proj, PC_XBC // BLOCK, sv["conv_w"], sv["conv_b"], dxc,
                                           name=f"{tag}_conv_bwd")
    G["conv_b"] = dconv_b[0]
    ddt_raw, ddt_bias = _ssd_dt_bwd(proj, PC_DT // BLOCK, sv["dt_bias"], ddtc, ddtr, name=f"{tag}_ssd_dt_bwd")
    G["dt_bias"] = ddt_bias[0, :SSD_HEADS]
    dfq, dfk, dfv, dcg, dcgt = _attn_bwd(*sv["fox_qkv"], sv["y_fox"], sv["lse_f"], (dycat, SSD_D // ATT_W),
                                         scale=FOX_HD ** -0.5, name=f"{tag}_fox_attn_bwd", bias=(sv["cg"], sv["cgt"]))
    df_raw, dfb = _fox_gate_bwd(proj, PC_FR // BLOCK, sv["f_b"], dcg, dcgt, name=f"{tag}_fox_gate_bwd")
    G["fox_f_b"] = dfb[0, :FOX_HEADS]
    dqn_h, dkn_h, dv_h, dqr, dkr = _attn_bwd(
        *sv["mla_qkv"], sv["y_mla"], sv["lse_m"], (dycat, (SSD_D + FOX_D) // ATT_W),
        scale=(MLA_NOPE + MLA_ROPE) ** -0.5, name=f"{tag}_mla_attn_bwd", rope=((sv["qr"], 0), (sv["kr"], 0)))
    dq_rope, dk_rope = _rowwise(_rope_t_fn, [dqr, dkr, cosf, sins], [], [BLOCK, BLOCK], name=f"{tag}_rope_bwd",
                                tile=272)
    dqh = jnp.concatenate([dqn_h, dq_rope], axis=1).astype(BF16)
    dkvh = jnp.concatenate([dkn_h, dv_h], axis=1).astype(BF16)
    dqn = _mm(dqh, W["mla_w_uq_p"][l], tb=True, name=f"{tag}_mla_uq_dx")
    G["mla_w_uq_p"] = _mm(sv["qn"], dqh, ta=True, name=f"{tag}_mla_uq_dw")
    dcn = _mm(dkvh, W["mla_w_ukv_p"][l], tb=True, name=f"{tag}_mla_ukv_dx")
    G["mla_w_ukv_p"] = _mm(sv["cn"], dkvh, ta=True, name=f"{tag}_mla_ukv_dw")
    (dcq, dckv), (dgq, dgkv) = _rowwise_bwd(_mla_norm_fn, sv["norm_rows"], [sv["gq"], sv["gkv"]], [dqn, dcn],
                                            name=f"{tag}_mla_norm_bwd", tile=272)
    G["mla_q_norm_g"], G["mla_kv_norm_g"] = dgq[0], dgkv[0]
    dproj = jnp.concatenate([dz, dxbc, dfq, dfk, dfv, dcq, dckv, ddt_raw, df_raw, dk_rope], axis=1).astype(BF16)
    dh1 = _mm(dproj, W["w_in_p"][l], tb=True, add=dh1_a, name=f"{tag}_in_proj_dx")
    G["w_in_p"] = _mm(sv["h1b"], dproj, ta=True, name=f"{tag}_in_proj_dw")
    return dh1, G


def _embed(x, meta):
    return jnp.concatenate([jnp.zeros((PAD_ROWS, D_MODEL), F32), meta, x], axis=0)


def _layer_fwd(h, W, l, cosf, sins):
    ln = lambda n: W[n][l][None]
    h1, s1 = _ffn_fwd(h, W, "ffn1", l, ln("ln1_g"), ln("ln1_b"), f"l{l}_ffn1")
    h2, sm = _mixer_fwd(h1, W, l, cosf, sins)
    h3, s2 = _ffn_fwd(h2, W, "ffn2", l, ln("ln3_g"), ln("ln3_b"), f"l{l}_ffn2")
    return h3, (s1, sm, s2)


def _layer_bwd(dh, saved, W, l, cosf, sins):
    ln = lambda n: W[n][l][None]
    s1, sm, s2 = saved
    G = {}
    dh, dg, db = _ffn_bwd(dh, s2, W, "ffn2", l, ln("ln3_g"), ln("ln3_b"), G, f"l{l}_ffn2")
    G["ln3_g"], G["ln3_b"] = dg[0], db[0]
    dh, Gm = _mixer_bwd(dh, sm, W, l, cosf, sins, G)
    G.update(Gm)
    dh, dg, db = _ffn_bwd(dh, s1, W, "ffn1", l, ln("ln1_g"), ln("ln1_b"), G, f"l{l}_ffn1")
    G["ln1_g"], G["ln1_b"] = dg[0], db[0]
    return dh, G


def _local_step(x, target, W):
    h = _embed(x, W["meta"])
    h = (h, h.astype(BF16))
    tgt = jnp.concatenate([jnp.zeros((BLOCK, D_MODEL), F32), target], axis=0)
    cosf, sins = _rope_tables()
    saved = []
    for l in range(DEPTH):
        h, sv = _layer_fwd(h, W, l, cosf, sins)
        saved.append(sv)
    dh, loss = _loss_head(h[0], tgt, name="loss_head")
    grads = [None] * DEPTH
    for l in reversed(range(DEPTH)):
        dh, grads[l] = _layer_bwd(dh, saved[l], W, l, cosf, sins)
    return loss, dh, grads


WEIGHTS = ['meta', 'ffn1_w_gate', 'ffn1_w_up', 'ffn1_w_down', 'ln1_g', 'ln1_b', 'w_in', 'conv_w', 'conv_b', 'dt_bias',
           'a_log', 'd_skip', 'ssd_norm_g', 'fox_f_b', 'mla_q_norm_g', 'mla_w_uq', 'mla_kv_norm_g', 'mla_w_ukv',
           'w_out', 'ln2_g', 'ln2_b', 'ffn2_w_gate', 'ffn2_w_up', 'ffn2_w_down', 'ln3_g', 'ln3_b']
SMALL = ["ln1_g", "ln1_b", "conv_b", "dt_bias", "a_log", "d_skip", "ssd_norm_g", "fox_f_b", "mla_q_norm_g",
         "mla_kv_norm_g", "ln2_g", "ln2_b", "ln3_g", "ln3_b"]
MATMUL_W = ["ffn1_w_gate", "ffn1_w_up", "ffn1_w_down", "w_in", "mla_w_uq", "mla_w_ukv", "w_out", "ffn2_w_gate",
            "ffn2_w_up", "ffn2_w_down"]
SMALL_ROWS = 312


def _pad_to(a, axis, size):
    pads = [(0, 0)] * a.ndim
    pads[axis] = (0, size - a.shape[axis])
    return jnp.pad(a, pads)


def _chip_cols(full, chip, width):
    return lax.dynamic_slice_in_dim(full, chip * width, width, axis=full.ndim - 1)


def kernel(x, meta, ffn1_w_gate, ffn1_w_up, ffn1_w_down, ln1_g, ln1_b, w_in, conv_w, conv_b, dt_bias, a_log, d_skip, ssd_norm_g, fox_f_b, mla_q_norm_g, mla_w_uq, mla_kv_norm_g, mla_w_ukv, w_out, ln2_g, ln2_b, ffn2_w_gate, ffn2_w_up, ffn2_w_down, ln3_g, ln3_b, loss_target, m_meta, m_ffn1_w_gate, m_ffn1_w_up, m_ffn1_w_down, m_ln1_g, m_ln1_b, m_w_in, m_conv_w, m_conv_b, m_dt_bias, m_a_log, m_d_skip, m_ssd_norm_g, m_fox_f_b, m_mla_q_norm_g, m_mla_w_uq, m_mla_kv_norm_g, m_mla_w_ukv, m_w_out, m_ln2_g, m_ln2_b, m_ffn2_w_gate, m_ffn2_w_up, m_ffn2_w_down, m_ln3_g, m_ln3_b, v_meta, v_ffn1_w_gate, v_ffn1_w_up, v_ffn1_w_down, v_ln1_g, v_ln1_b, v_w_in, v_conv_w, v_conv_b, v_dt_bias, v_a_log, v_d_skip, v_ssd_norm_g, v_fox_f_b, v_mla_q_norm_g, v_mla_w_uq, v_mla_kv_norm_g, v_mla_w_ukv, v_w_out, v_ln2_g, v_ln2_b, v_ffn2_w_gate, v_ffn2_w_up, v_ffn2_w_down, v_ln3_g, v_ln3_b):
    args = dict(locals())
    w = {n: args[n] for n in WEIGHTS}
    m = {n: args["m_" + n] for n in WEIGHTS}
    v = {n: args["v_" + n] for n in WEIGHTS}
    xcoord, ycoord, _ = _my_pos()
    chip = 2 * xcoord + ycoord

    tr = lambda a: jnp.swapaxes(a, 1, 2)

    def bf16_shard(n, l, zero=None):
        a = w[n] if zero is None else w[n] + zero
        if n.endswith("w_gate") or n.endswith("w_up"):
            a = _pad_to(tr(a), 1, HP)
        elif n.endswith("w_down"):
            a = _pad_to(a, 1, HP)
        elif n == "w_in":
            a = _pad_to(a, 2, IN_SHARD_P)
        return a[l].astype(BF16)

    land_shape = lambda s: jax.ShapeDtypeStruct((N_CHIPS,) + s.shape, s.dtype)

    def gather_start(names, l, tag, after):
        srcs = [bf16_shard(n, l, None if after is None else after[0, 0]) for n in names]
        return _ici_start(_gather_copies, srcs, [land_shape(s) for s in srcs], name=f"gather_ici_{tag}_start",
                          after=[tiny[0]] if after is None else [after])

    def gather_finish(handle, names, l, tag, after):
        srcs, lands = _ici_wait(_gather_copies, *handle[:4], after, name=f"gather_ici_{tag}_wait")
        use_gathered(l, names, _gather_d2d(srcs, lands, tag))

    def gather_d2d_start(handle, tag, after):
        srcs, lands = _ici_wait(_gather_copies, *handle[:4], after, name=f"gather_ici_{tag}_wait")
        return _ici_start(_d2d_copies, srcs, lands, name=f"gather_d2d_{tag}_start", sems_per_array=D2D_COPIES)

    def gather_d2d_finish(handle, names, l, tag, after):
        _, lands = _ici_wait(_d2d_copies, *handle[:4], after, name=f"gather_d2d_{tag}_wait")
        use_gathered(l, names, lands)

    tiny = _allgather_chips([w["meta"].reshape(2, N_META // 2, D_MODEL // N_CHIPS), w["conv_w"]])
    meta_full = jnp.concatenate([tiny[0][k].reshape(N_META, D_MODEL // N_CHIPS) for k in range(N_CHIPS)], axis=1)

    W = {n: [None] * DEPTH for n in MATMUL_W + ["w_in_p", "mla_w_uq_p", "mla_w_ukv_p"]}
    W["conv_w"] = jnp.concatenate([tiny[1][k] for k in range(N_CHIPS)], axis=-1)
    W["meta"] = meta_full
    for n in SMALL:
        W[n] = w[n]

    def use_gathered(l, names, lands):
        got = dict(zip(names, lands))
        cat = lambda n, cut=None: jnp.concatenate([got[n][k][..., :cut] for k in range(N_CHIPS)], axis=-1)
        for n in names:
            W[n][l] = got[n]
        if "w_in" in got:
            W["w_in_p"][l] = _pad_in_proj(cat("w_in", IN_SHARD))
            W["mla_w_uq_p"][l] = _regroup_uq(cat("mla_w_uq"))
            W["mla_w_ukv_p"][l] = _regroup_ukv(cat("mla_w_ukv"))

    def chunk_grads(G, names):
        def chunked(name, ungroup, width, pad):
            full = ungroup(G[name])
            return _pad_to(jnp.moveaxis(full.reshape(full.shape[0], N_CHIPS, width), 1, 0), 2, pad)
        special = {"mla_w_uq": ("mla_w_uq_p", _ungroup_uq, MLA_NOPE + MLA_ROPE, MLA_NOPE + MLA_ROPE),
                   "mla_w_ukv": ("mla_w_ukv_p", _ungroup_ukv, MLA_NOPE + MLA_V, MLA_NOPE + MLA_V)}
        return [_in_proj_grad_chunks(G["w_in_p"]) if n == "w_in" else chunked(*special[n]) if n in special else G[n]
                for n in names]

    def rs_start(G, names, tag):
        pairs = _rs_pair_sums(chunk_grads(G, names), names, tag)
        handle = _ici_start(_exchange_copies, [p[1] for p in pairs], _exchange_land_shapes(pairs),
                            name=f"rs_exchange_{tag}_start")
        return pairs, handle

    def swap_start(G, names, tag):
        gs = chunk_grads(G, names)
        return _ici_start(_swap_copies, gs, _swap_land_shapes(gs), name=f"rs_swap_{tag}_start")

    def exchange_start(swap_handle, names, tag, after):
        gs, r1 = _ici_wait(_swap_copies, *swap_handle[:4], after, name=f"rs_swap_{tag}_wait")
        pairs = _rs_add_pairs(gs, r1, names, tag)
        handle = _ici_start(_exchange_copies, [p[1] for p in pairs], _exchange_land_shapes(pairs),
                            name=f"rs_exchange_{tag}_start")
        return pairs, handle

    def rs_end(pairs, handle, names, tag, after):
        _, r2 = _ici_wait(_exchange_copies, *handle[:4], after, name=f"rs_exchange_{tag}_wait")
        return dict(zip(names, _rs_finish(pairs, r2, names, tag)))

    ffn1_w, mix_w, ffn2_w = MATMUL_W[:3], MATMUL_W[3:7], MATMUL_W[7:]
    g_a = gather_start(ffn1_w, 0, "l0_ffn1", None)
    g_b = gather_start(mix_w, 0, "l0_mix", g_a[4])
    g_c = gather_start(ffn2_w, 0, "l0_ffn2", g_b[4])
    g_l1 = gather_start(MATMUL_W, 1, "l1", g_c[4])
    token = g_l1[4]
    cosf, sins = _rope_tables()
    ln = lambda n, l: W[n][l][None]
    h = _embed(x[0] + token[0, 0], meta_full)
    h = (h, h.astype(BF16))
    gather_finish(g_a, ffn1_w, 0, "l0_ffn1", h[1])
    h1, s1 = _ffn_fwd(h, W, "ffn1", 0, ln("ln1_g", 0), ln("ln1_b", 0), "l0_ffn1")
    gather_finish(g_b, mix_w, 0, "l0_mix", h1[1])
    d_c = gather_d2d_start(g_c, "l0_ffn2", W["w_in_p"][0])
    h2, sm = _mixer_fwd(h1, W, 0, cosf, sins, after=[d_c[4]])
    gather_d2d_finish(d_c, ffn2_w, 0, "l0_ffn2", h2[1])
    d_l1 = gather_d2d_start(g_l1, "l1", W["ffn2_w_gate"][0])
    h, s2 = _ffn_fwd(h2, W, "ffn2", 0, ln("ln3_g", 0), ln("ln3_b", 0), "l0_ffn2", after=[d_l1[4]])
    saved0 = (s1, sm, s2)
    gather_d2d_finish(d_l1, MATMUL_W, 1, "l1", h[1])
    h, saved1 = _layer_fwd(h, W, 1, cosf, sins)
    tgt = jnp.concatenate([jnp.zeros((BLOCK, D_MODEL), F32), loss_target[0]], axis=0)
    dh, loss = _loss_head(h[0], tgt, name="loss_head")
    G = [None] * DEPTH
    dh, G[1] = _layer_bwd(dh, saved1, W, 1, cosf, sins)
    ffn2_w, mix_w, ffn1_w = MATMUL_W[7:], MATMUL_W[3:7], MATMUL_W[:3]
    sw_l1 = swap_start(G[1], MATMUL_W, "l1")
    G0 = {}
    dh, dg, db = _ffn_bwd(dh, s2, W, "ffn2", 0, ln("ln3_g", 0) + sw_l1[4][0, 0], ln("ln3_b", 0), G0, "l0_ffn2")
    G0["ln3_g"], G0["ln3_b"] = dg[0], db[0]
    pairs_l1, x_l1 = exchange_start(sw_l1, MATMUL_W, "l1", dh)
    sw_a = swap_start(G0, ffn2_w, "l0_ffn2")
    dh, Gm = _mixer_bwd(dh, sm, W, 0, cosf, sins, G0, zero=x_l1[4][0, 0] + sw_a[4][0, 0])
    G0.update(Gm)
    pairs_a, x_a = exchange_start(sw_a, ffn2_w, "l0_ffn2", dh)
    reduced1 = rs_end(pairs_l1, x_l1, MATMUL_W, "l1", dh)
    pairs_b, x_b = rs_start(G0, mix_w, "l0_mix")
    dh0, dg, db = _ffn_bwd(dh, s1, W, "ffn1", 0, ln("ln1_g", 0) + (x_a[4][0, 0] + x_b[4][0, 0]), ln("ln1_b", 0), G0,
                           "l0_ffn1")
    G0["ln1_g"], G0["ln1_b"] = dg[0], db[0]
    G[0] = G0
    reduced0 = rs_end(pairs_a, x_a, ffn2_w, "l0_ffn2", dh0)
    reduced0.update(rs_end(pairs_b, x_b, mix_w, "l0_mix", dh0))

    small_parts = [jnp.stack([G[l][n] for l in range(DEPTH)]).reshape(-1) for n in SMALL]
    small_parts += [jnp.stack([G[l]["conv_w"] for l in range(DEPTH)]).reshape(-1), dh0[PAD_ROWS:BLOCK].reshape(-1),
                    loss[0, :1]]
    sw_c = swap_start(G0, ffn1_w, "l0_ffn1")
    flat = jnp.concatenate(small_parts) + sw_c[4][0, 0]
    flat = jnp.pad(flat, (0, SMALL_ROWS * BLOCK - flat.shape[0]))
    red2d = _allreduce_small(flat.reshape(SMALL_ROWS, BLOCK))
    red = red2d.reshape(-1)
    pairs_c, x_c = exchange_start(sw_c, ffn1_w, "l0_ffn1", red2d)
    grads, off = {}, 0
    for n in SMALL:
        size = int(np.prod(w[n].shape))
        grads[n] = red[off:off + size].reshape(w[n].shape)
        off += size
    conv_full = red[off:off + DEPTH * SSD_CONV * 768].reshape(DEPTH, SSD_CONV, 768)
    off += DEPTH * SSD_CONV * 768
    dmeta_full = red[off:off + N_META * D_MODEL].reshape(N_META, D_MODEL)
    off += N_META * D_MODEL
    loss_out = red[off]
    grads["conv_w"] = _chip_cols(conv_full, chip, 768 // N_CHIPS)
    grads["meta"] = _chip_cols(dmeta_full, chip, D_MODEL // N_CHIPS)

    delta, new_m, new_v = {}, {}, {}

    def adamw_matmul_weights(names, after):
        done = []
        for n in names:
            gs = [reduced0[n], reduced1[n]]
            if n.endswith("w_gate") or n.endswith("w_up"):
                res = _adamw(tr(w[n]), gs, tr(m[n]), tr(v[n]), name=f"adamw_{n}", after=after)
                grads[n], delta[n], new_m[n], new_v[n] = [tr(r) for r in res]
            else:
                res = _adamw(w[n], gs, m[n], v[n], name=f"adamw_{n}", after=after)
                grads[n], delta[n], new_m[n], new_v[n] = res
            done.append(res[1])
        return done

    early_done = adamw_matmul_weights(ffn2_w + mix_w, [x_c[4]])
    rest = [n for n in WEIGHTS if n not in MATMUL_W]

    def pack_small(d):
        f = jnp.concatenate([d[n].reshape(-1) for n in rest])
        tot = -(-f.shape[0] // (8 * BLOCK)) * 8 * BLOCK
        return jnp.pad(f, (0, tot - f.shape[0])).reshape(-1, BLOCK)

    _, d2, m2, v2 = _adamw(pack_small(w), [pack_small(grads)], pack_small(m), pack_small(v), name="adamw_small",
                           after=[x_c[4]])
    reduced0.update(rs_end(pairs_c, x_c, ffn1_w, "l0_ffn1", [d2] + early_done))
    adamw_matmul_weights(ffn1_w, [])
    off = 0
    for n in rest:
        size = int(np.prod(w[n].shape))
        for dst, src in ((delta, d2), (new_m, m2), (new_v, v2)):
            dst[n] = src.reshape(-1)[off:off + size].reshape(w[n].shape)
        off += size

    grad_x = dh0[BLOCK:][None]
    return (loss_out, grad_x, *[grads[n] for n in WEIGHTS], *[delta[n] for n in WEIGHTS],
            *[new_m[n] for n in WEIGHTS], *[new_v[n] for n in WEIGHTS])
```

```python
import functools

import numpy as np
import jax
import jax.numpy as jnp
from jax import lax
from jax.experimental import pallas as pl
from jax.experimental.pallas import tpu as pltpu

F32 = jnp.float32
BF16 = jnp.bfloat16
MESH = pl.DeviceIdType.MESH

D_MODEL = 1024
SEQ = 2048
N_META = 16
BLOCK = 128
PAD_ROWS = 112
LP = PAD_ROWS + N_META + SEQ
N_CHUNK = LP // BLOCK
DEPTH = 2
D_FF = 2816
N_CHIPS = 4
FF_SHARD = D_FF // N_CHIPS
HP = 768
FP = N_CHIPS * HP
SSD_HEADS, SSD_HD, SSD_D, SSD_GROUPS, SSD_STATE, SSD_CONV = 8, 64, 512, 2, 64, 4
FOX_HEADS, FOX_HD, FOX_D = 4, 64, 256
MLA_HEADS, MLA_Q_LORA, MLA_KV_LORA, MLA_NOPE, MLA_ROPE, MLA_V, MLA_D = 4, 256, 128, 64, 32, 64, 256
ROPE_HALF = MLA_ROPE // 2
ROPE_THETA = 10000.0
N_IN = 2476
IN_SHARD = N_IN // N_CHIPS
IN_SHARD_P = 640
ALPHA = (2 * DEPTH) ** 0.25
EPS = 1e-5
ADAM_LR, ADAM_B1, ADAM_B2, ADAM_EPS, ADAM_WD, ADAM_STEP = 0.001, 0.9, 0.999, 1e-08, 0.01, 10
NEG = -1e30
TM = 544

VMEM_LIMIT_BYTES = 56 * 1024 * 1024

PC_Z, PC_XBC, PC_FQ, PC_FK, PC_FV, PC_CQ, PC_CKV, PC_DT, PC_FR, PC_KR, PC_END = (
    0, 512, 1280, 1536, 1792, 2048, 2304, 2432, 2560, 2688, 2816)
OC_Z, OC_XBC, OC_DT, OC_FQ, OC_FK, OC_FV, OC_FR, OC_CQ, OC_CKV, OC_KR = (
    0, 512, 1280, 1288, 1544, 1800, 2056, 2060, 2316, 2444)


def _cparams(sem=None):
    return pltpu.CompilerParams(dimension_semantics=sem, vmem_limit_bytes=VMEM_LIMIT_BYTES)


def _tile(n, cap, mult):
    best = None
    for t in range(mult, min(n, cap) + 1, mult):
        if n % t == 0:
            best = t
    return best if best is not None else n


def _bs(shape, fn):
    return pl.BlockSpec(shape, fn)


ANY = pl.BlockSpec(memory_space=pl.ANY)


def _dims(ca, cb):
    return (((ca,), (cb,)), ((), ()))


def _raw_bdot(a, b, ca, cb):
    return lax.dot_general(a.astype(BF16), b.astype(BF16), _dims(ca, cb), preferred_element_type=F32)


def _mm_core(a, b, *, a_spec, b_spec, o_spec, grid, out_shape, ca, cb, name, add=None, after=()):
    nk = grid[2]
    has_add = add is not None
    acc_shape = tuple(d for d in o_spec.block_shape if d is not None)

    def body(*refs):
        a_ref, b_ref = refs[0], refs[1]
        add_ref = refs[2] if has_add else None
        o_ref, acc_ref = refs[-2], refs[-1]
        k = pl.program_id(2)

        @pl.when(k == 0)
        def _():
            acc_ref[...] = jnp.zeros_like(acc_ref)

        acc_ref[...] += _raw_bdot(a_ref[...], b_ref[...], ca, cb)

        @pl.when(k == nk - 1)
        def _():
            r = acc_ref[...]
            if has_add:
                r = r + add_ref[...]
            o_ref[...] = r

    ins = [a, b] + ([add] if has_add else []) + list(after)
    in_specs = [a_spec, b_spec] + ([o_spec] if has_add else []) + [ANY] * len(after)
    return pl.pallas_call(
        body, name=name, grid=grid, in_specs=in_specs, out_specs=o_spec,
        out_shape=jax.ShapeDtypeStruct(out_shape, F32), scratch_shapes=[pltpu.VMEM(acc_shape, F32)],
        compiler_params=_cparams(("parallel", "parallel", "arbitrary")),
    )(*ins)


MM_VMEM_BUDGET = 40 * 1024 * 1024


def _divisors(n, mult):
    return [t for t in range(mult, n + 1, mult) if n % t == 0] or [n]


def _pick_tiles(M, N, K, a_bytes, b_bytes, ta, has_add):
    best = None
    for tm in _divisors(M, 128 if ta else 16):
        for tn in _divisors(N, 128):
            vmem = 2 * tm * K * a_bytes + 2 * K * tn * b_bytes + (3 + 2 * int(has_add)) * tm * tn * 4
            if vmem <= MM_VMEM_BUDGET:
                key = ((M // tm) * (N // tn), -tn)
                if best is None or key < best[0]:
                    best = (key, tm, tn)
    assert best is not None, (M, N, K)
    return best[1], best[2], K


def _mm(a, b, *, ta=False, tb=False, add=None, name, after=()):
    if ta:
        K, M = a.shape
    else:
        M, K = a.shape
    if tb:
        N, Kb = b.shape
    else:
        Kb, N = b.shape
    assert K == Kb, (a.shape, b.shape, ta, tb)
    tm, tn, tk = _pick_tiles(M, N, K, a.dtype.itemsize, b.dtype.itemsize, ta, add is not None)
    a_spec = _bs((tk, tm), lambda i, j, k: (k, i)) if ta else _bs((tm, tk), lambda i, j, k: (i, k))
    b_spec = _bs((tn, tk), lambda i, j, k: (j, k)) if tb else _bs((tk, tn), lambda i, j, k: (k, j))
    return _mm_core(a, b, a_spec=a_spec, b_spec=b_spec, o_spec=_bs((tm, tn), lambda i, j, k: (i, j)),
                    grid=(M // tm, N // tn, K // tk), out_shape=(M, N), ca=0 if ta else 1, cb=1 if tb else 0,
                    name=name, add=add, after=after)


def _row_entry(r, ncol):
    if isinstance(r, tuple):
        return r
    return r, r.shape[1] // ncol, 0


def _rowwise(fn, rows, pars, out_cols, *, name, tile, ncol=1, out_dtypes=None):
    rows = [_row_entry(r, ncol) for r in rows]
    L = rows[0][0].shape[0]
    nr, npar = len(rows), len(pars)
    in_specs = [_bs((tile, w), lambda g, i, o=o: (i, o + g)) for _, w, o in rows]
    in_specs += [_bs((p.shape[0], p.shape[1] // ncol), lambda g, i: (0, g)) for p in pars]
    out_specs = [_bs((tile, c // ncol), lambda g, i: (i, g)) for c in out_cols]

    def body(*refs):
        ins, outs = refs[:nr + npar], refs[nr + npar:]
        row0 = pl.program_id(1) * tile
        res = fn(row0, *[r[...] for r in ins])
        for o, v in zip(outs, res):
            o[...] = v.astype(o.dtype)

    return pl.pallas_call(
        body, name=name, grid=(ncol, L // tile), in_specs=in_specs, out_specs=out_specs,
        out_shape=[jax.ShapeDtypeStruct((L, c), d) for c, d in zip(out_cols, out_dtypes or [F32] * len(out_cols))],
        compiler_params=_cparams(("parallel", "parallel")),
    )(*[r[0] for r in rows], *pars)


def _rowwise_bwd(fn, rows, pars, douts, *, name, tile, ncol=1, row_grad=None, grad_dtypes=None):
    rows = [_row_entry(r, ncol) for r in rows]
    L = rows[0][0].shape[0]
    nr, npar, nd = len(rows), len(pars), len(douts)
    row_grad = [True] * nr if row_grad is None else row_grad
    in_specs = [_bs((tile, w), lambda g, i, o=o: (i, o + g)) for _, w, o in rows]
    in_specs += [_bs((p.shape[0], p.shape[1] // ncol), lambda g, i: (0, g)) for p in pars]
    in_specs += [_bs((tile, d.shape[1] // ncol), lambda g, i: (i, g)) for d in douts]
    g_widths = [w * ncol for (_, w, _), f in zip(rows, row_grad) if f]
    out_specs = [_bs((tile, w // ncol), lambda g, i: (i, g)) for w in g_widths]
    out_specs += [_bs((p.shape[0], p.shape[1] // ncol), lambda g, i: (0, g)) for p in pars]
    out_shape = [jax.ShapeDtypeStruct((L, w), d) for w, d in zip(g_widths, grad_dtypes or [F32] * len(g_widths))]
    out_shape += [jax.ShapeDtypeStruct(p.shape, F32) for p in pars]

    def body(*refs):
        ins = refs[:nr + npar]
        dos = refs[nr + npar:nr + npar + nd]
        outs = refs[nr + npar + nd:]
        i = pl.program_id(1)
        row0 = i * tile
        _, vjp = jax.vjp(lambda *a: tuple(fn(row0, *a)), *[r[...] for r in ins])
        grads = vjp(tuple(d[...].astype(F32) for d in dos))
        o = 0
        for j in range(nr):
            if row_grad[j]:
                outs[o][...] = grads[j].astype(outs[o].dtype)
                o += 1
        for j in range(npar):
            g, ref = grads[nr + j], outs[o + j]

            @pl.when(i == 0)
            def _(g=g, ref=ref):
                ref[...] = g

            @pl.when(i > 0)
            def _(g=g, ref=ref):
                ref[...] += g

    res = pl.pallas_call(
        body, name=name, grid=(ncol, L // tile), in_specs=in_specs, out_specs=out_specs, out_shape=out_shape,
        compiler_params=_cparams(("parallel", "arbitrary")),
    )(*[r[0] for r in rows], *pars, *douts)
    return res[:len(g_widths)], res[len(g_widths):]


def _sigmoid(x):
    return 1.0 / (1.0 + jnp.exp(-x))


def _softplus(x):
    return jnp.maximum(x, 0.0) + jnp.log(1.0 + jnp.exp(-jnp.abs(x)))


def _silu(x):
    return x * _sigmoid(x)


def _make_res_ln_fn(scale):
    def fn(row0, h, o, gam, bet):
        pre = ALPHA * h + scale * o
        mu = jnp.mean(pre, axis=-1, keepdims=True)
        xc = pre - mu
        var = jnp.mean(xc * xc, axis=-1, keepdims=True)
        return (xc * lax.rsqrt(var + EPS) * gam + bet,)
    return fn


def _ssd_post_fn(row0, y, xs, z, dskip, normg):
    v = (y + dskip * xs) * _silu(z)
    v = v * lax.rsqrt(jnp.mean(v * v, axis=-1, keepdims=True) + EPS)
    return (v * normg,)


def _mla_norm_fn(row0, cq, ckv, gq, gkv):
    qn = cq * lax.rsqrt(jnp.mean(cq * cq, axis=-1, keepdims=True) + EPS) * gq
    cn = ckv * lax.rsqrt(jnp.mean(ckv * ckv, axis=-1, keepdims=True) + EPS) * gkv
    return qn, cn


def _rope_fn(row0, q, k, cosf, sins):
    return (q * cosf + pltpu.roll(q, 64, 1) * sins, k * cosf + pltpu.roll(k, 64, 1) * sins)


def _rope_t_fn(row0, gq, gk, cosf, sins):
    return (gq * cosf + pltpu.roll(gq * sins, 64, 1), gk * cosf + pltpu.roll(gk * sins, 64, 1))


def _conv_fwd(x, x_off, w, b, *, name):
    C = w.shape[1]

    def body(x_ref, w_ref, b_ref, o_ref):
        rows = lax.broadcasted_iota(jnp.int32, (LP, BLOCK), 0)
        xv = jnp.where(rows >= PAD_ROWS, x_ref[...], 0.0)
        acc = b_ref[...] + w_ref[3:4, :] * xv
        for k in range(SSD_CONV - 1):
            acc = acc + w_ref[k:k + 1, :] * pltpu.roll(xv, SSD_CONV - 1 - k, 0)
        o_ref[...] = _silu(acc)

    return pl.pallas_call(
        body, name=name, grid=(C // BLOCK,),
        in_specs=[_bs((LP, BLOCK), lambda j: (0, j + x_off)), _bs((SSD_CONV, BLOCK), lambda j: (0, j)),
                  _bs((1, BLOCK), lambda j: (0, j))],
        out_specs=_bs((LP, BLOCK), lambda j: (0, j)),
        out_shape=jax.ShapeDtypeStruct((LP, C), F32), compiler_params=_cparams(("parallel",)),
    )(x, w, b)


def _conv_bwd(x, x_off, w, b, dout, *, name):
    C = w.shape[1]

    def body(x_ref, w_ref, b_ref, do_ref, dx_ref, dw_ref, db_ref):
        rows = lax.broadcasted_iota(jnp.int32, (LP, BLOCK), 0)
        real = rows >= PAD_ROWS
        xv = jnp.where(real, x_ref[...], 0.0)
        shifted = [pltpu.roll(xv, SSD_CONV - 1 - k, 0) for k in range(SSD_CONV - 1)] + [xv]
        acc = b_ref[...]
        for k in range(SSD_CONV):
            acc = acc + w_ref[k:k + 1, :] * shifted[k]
        sig = _sigmoid(acc)
        dacc = jnp.where(real, do_ref[...] * (sig * (1.0 + acc * (1.0 - sig))), 0.0)
        db_ref[...] = jnp.sum(dacc, axis=0, keepdims=True)
        dx = w_ref[3:4, :] * dacc
        for k in range(SSD_CONV):
            dw_ref[k:k + 1, :] = jnp.sum(dacc * shifted[k], axis=0, keepdims=True)
            if k < SSD_CONV - 1:
                dx = dx + w_ref[k:k + 1, :] * pltpu.roll(dacc, LP - (SSD_CONV - 1 - k), 0)
        dx_ref[...] = jnp.where(real, dx, 0.0)

    return pl.pallas_call(
        body, name=name, grid=(C // BLOCK,),
        in_specs=[_bs((LP, BLOCK), lambda j: (0, j + x_off)), _bs((SSD_CONV, BLOCK), lambda j: (0, j)),
                  _bs((1, BLOCK), lambda j: (0, j)), _bs((LP, BLOCK), lambda j: (0, j))],
        out_specs=[_bs((LP, BLOCK), lambda j: (0, j)), _bs((SSD_CONV, BLOCK), lambda j: (0, j)),
                   _bs((1, BLOCK), lambda j: (0, j))],
        out_shape=[jax.ShapeDtypeStruct((LP, C), F32), jax.ShapeDtypeStruct((SSD_CONV, C), F32),
                   jax.ShapeDtypeStruct((1, C), F32)],
        compiler_params=_cparams(("parallel",)),
    )(x, w, b, dout)


_BDIMS = {"nn": (((2,), (1,)), ((0,), (0,))), "nt": (((2,), (2,)), ((0,), (0,))), "tn": (((1,), (1,)), ((0,), (0,)))}


def _raw_bdot3(a, b, mode):
    return lax.dot_general(a.astype(BF16), b.astype(BF16), _BDIMS[mode], preferred_element_type=F32)


@functools.partial(jax.custom_vjp, nondiff_argnums=(2,))
def _bdot3(a, b, mode):
    return _raw_bdot3(a, b, mode)


def _bdot3_fwd(a, b, mode):
    return _raw_bdot3(a, b, mode), (a, b)


def _bdot3_bwd(mode, res, g):
    a, b = res
    if mode == "nn":
        return _raw_bdot3(g, b, "nt"), _raw_bdot3(a, g, "tn")
    if mode == "nt":
        return _raw_bdot3(g, b, "nn"), _raw_bdot3(g, a, "tn")
    return _raw_bdot3(b, g, "nt"), _raw_bdot3(a, g, "nn")


_bdot3.defvjp(_bdot3_fwd, _bdot3_bwd)


def _ssd_chunk(x, bm, cm, dt, dtt, alog, prev):
    rep = SSD_HEADS // SSD_GROUPS
    per_head = lambda t: jnp.broadcast_to(t[:, None], (SSD_GROUPS, rep) + t.shape[1:]).reshape((SSD_HEADS,) + t.shape[1:])
    bm, cm = per_head(bm), per_head(cm)
    lane_h = lax.broadcasted_iota(jnp.int32, (1, BLOCK), 1)
    row_h = lax.broadcasted_iota(jnp.int32, (BLOCK, 1), 0)
    dtc = jnp.stack([jnp.sum(jnp.where(lane_h == h, dt, 0.0), axis=1, keepdims=True) for h in range(SSD_HEADS)])
    dtr = jnp.stack([jnp.sum(jnp.where(row_h == h, dtt, 0.0), axis=0, keepdims=True) for h in range(SSD_HEADS)])
    lane = lax.broadcasted_iota(jnp.int32, alog.shape, 2)
    a_neg = -jnp.exp(jnp.sum(jnp.where(lane == 0, alog, 0.0), axis=2, keepdims=True))
    ac_in = dtc * a_neg
    ar_in = dtr * a_neg
    li = lax.broadcasted_iota(jnp.int32, (1, BLOCK, BLOCK), 1)
    si = lax.broadcasted_iota(jnp.int32, (1, BLOCK, BLOCK), 2)
    causal = li >= si
    acum_c = jnp.sum(jnp.where(causal, ar_in, 0.0), axis=2, keepdims=True)
    acum_r = jnp.sum(jnp.where(li <= si, ac_in, 0.0), axis=1, keepdims=True)
    total = jnp.sum(ar_in, axis=2, keepdims=True)
    seg = jnp.exp(jnp.where(causal, acum_c - acum_r, NEG))
    xdt = x * dtc
    cb = _bdot3(cm, bm, "nt")
    y = _bdot3(cb * seg, xdt, "nn") + _bdot3(cm, prev, "nt") * jnp.exp(acum_c)
    st = _bdot3(xdt, bm * jnp.exp(total - acum_c), "tn")
    return y, prev * jnp.exp(total) + st


def _ssd_dt_fwd(raw, raw_blk, bias, *, name):
    def body(raw_ref, b_ref, dt_ref, dtt_ref):
        rows = lax.broadcasted_iota(jnp.int32, (LP, BLOCK), 0)
        dt = jnp.where(rows >= PAD_ROWS, _softplus(raw_ref[...] + b_ref[...]), 0.0)
        dt_ref[...] = dt
        dtt_ref[...] = dt.T

    return pl.pallas_call(
        body, name=name, grid=(1,),
        in_specs=[_bs((LP, BLOCK), lambda j: (0, raw_blk)), _bs((1, BLOCK), lambda j: (0, 0))],
        out_specs=[_bs((LP, BLOCK), lambda j: (0, 0)), _bs((BLOCK, LP), lambda j: (0, 0))],
        out_shape=[jax.ShapeDtypeStruct((LP, BLOCK), F32), jax.ShapeDtypeStruct((BLOCK, LP), F32)],
        compiler_params=_cparams(("arbitrary",)),
    )(raw, bias)


def _ssd_dt_bwd(raw, raw_blk, bias, ddt, ddtt, *, name):
    def body(raw_ref, b_ref, ddt_ref, ddtt_ref, draw_ref, db_ref):
        rows = lax.broadcasted_iota(jnp.int32, (LP, BLOCK), 0)
        g = ddt_ref[...] + ddtt_ref[...].T
        draw = jnp.where(rows >= PAD_ROWS, g * _sigmoid(raw_ref[...] + b_ref[...]), 0.0)
        draw_ref[...] = draw
        db_ref[...] = jnp.sum(draw, axis=0, keepdims=True)

    return pl.pallas_call(
        body, name=name, grid=(1,),
        in_specs=[_bs((LP, BLOCK), lambda j: (0, raw_blk)), _bs((1, BLOCK), lambda j: (0, 0)),
                  _bs((LP, BLOCK), lambda j: (0, 0)), _bs((BLOCK, LP), lambda j: (0, 0))],
        out_specs=[_bs((LP, BLOCK), lambda j: (0, 0)), _bs((1, BLOCK), lambda j: (0, 0))],
        out_shape=[jax.ShapeDtypeStruct((LP, BLOCK), F32), jax.ShapeDtypeStruct((1, BLOCK), F32)],
        compiler_params=_cparams(("arbitrary",)),
    )(raw, bias, ddt, ddtt)


def _ssd_specs(rev):
    ci = (lambda c: N_CHUNK - 1 - c) if rev else (lambda c: c)
    x_spec = _bs((SSD_HEADS, BLOCK, SSD_HD), lambda c: (0, ci(c), 0))
    g_spec = _bs((SSD_GROUPS, BLOCK, SSD_STATE), lambda c: (0, ci(c), 0))
    dtc_spec = _bs((BLOCK, BLOCK), lambda c: (ci(c), 0))
    dtr_spec = _bs((BLOCK, BLOCK), lambda c: (0, ci(c)))
    al_spec = _bs((SSD_HEADS, 1, BLOCK), lambda c: (0, 0, 0))
    st_spec = _bs((None, SSD_HEADS, SSD_HD, SSD_STATE), lambda c: (ci(c), 0, 0, 0))
    return x_spec, g_spec, dtc_spec, dtr_spec, al_spec, st_spec


def _ssd_fwd(x, bm, cm, dtc, dtr, alog, *, name):
    x_spec, g_spec, dtc_spec, dtr_spec, al_spec, st_spec = _ssd_specs(False)

    def body(x_ref, b_ref, c_ref, dtc_ref, dtr_ref, al_ref, y_ref, prev_ref, state):
        @pl.when(pl.program_id(0) == 0)
        def _():
            state[...] = jnp.zeros_like(state)

        prev = state[...]
        prev_ref[...] = prev
        y, new = _ssd_chunk(x_ref[...], b_ref[...], c_ref[...], dtc_ref[...], dtr_ref[...], al_ref[...], prev)
        y_ref[...] = y
        state[...] = new

    return pl.pallas_call(
        body, name=name, grid=(N_CHUNK,),
        in_specs=[x_spec, g_spec, g_spec, dtc_spec, dtr_spec, al_spec], out_specs=[x_spec, st_spec],
        out_shape=[jax.ShapeDtypeStruct((SSD_HEADS, LP, SSD_HD), F32),
                   jax.ShapeDtypeStruct((N_CHUNK, SSD_HEADS, SSD_HD, SSD_STATE), F32)],
        scratch_shapes=[pltpu.VMEM((SSD_HEADS, SSD_HD, SSD_STATE), F32)],
        compiler_params=_cparams(("arbitrary",)),
    )(x, bm, cm, dtc, dtr, alog)


def _ssd_bwd(x, bm, cm, dtc, dtr, alog, prevs, dy, *, name):
    x_spec, g_spec, dtc_spec, dtr_spec, al_spec, st_spec = _ssd_specs(True)

    def body(x_ref, b_ref, c_ref, dtc_ref, dtr_ref, al_ref, prev_ref, dy_ref,
             dx_ref, db_ref, dc_ref, ddtc_ref, ddtr_ref, dal_ref, dstate):
        c = pl.program_id(0)

        @pl.when(c == 0)
        def _():
            dstate[...] = jnp.zeros_like(dstate)

        _, vjp = jax.vjp(_ssd_chunk, x_ref[...], b_ref[...], c_ref[...], dtc_ref[...], dtr_ref[...], al_ref[...],
                         prev_ref[...])
        dx, db, dc, ddtc, ddtr, dal, dprev = vjp((dy_ref[...], dstate[...]))
        dx_ref[...] = dx
        db_ref[...] = db
        dc_ref[...] = dc
        ddtc_ref[...] = ddtc
        ddtr_ref[...] = ddtr
        dstate[...] = dprev

        @pl.when(c == 0)
        def _():
            dal_ref[...] = dal

        @pl.when(c > 0)
        def _():
            dal_ref[...] += dal

    hs = jax.ShapeDtypeStruct((SSD_HEADS, LP, SSD_HD), F32)
    gs = jax.ShapeDtypeStruct((SSD_GROUPS, LP, SSD_STATE), F32)
    return pl.pallas_call(
        body, name=name, grid=(N_CHUNK,),
        in_specs=[x_spec, g_spec, g_spec, dtc_spec, dtr_spec, al_spec, st_spec, x_spec],
        out_specs=[x_spec, g_spec, g_spec, dtc_spec, dtr_spec, al_spec],
        out_shape=[hs, gs, gs, jax.ShapeDtypeStruct((LP, BLOCK), F32),
                   jax.ShapeDtypeStruct((BLOCK, LP), F32), jax.ShapeDtypeStruct((SSD_HEADS, 1, BLOCK), F32)],
        scratch_shapes=[pltpu.VMEM((SSD_HEADS, SSD_HD, SSD_STATE), F32)],
        compiler_params=_cparams(("arbitrary",)),
    )(x, bm, cm, dtc, dtr, alog, prevs, dy)


def _tri_dot(tri, v):
    hi = v.astype(BF16)
    r1 = v - hi.astype(F32)
    mid = r1.astype(BF16)
    lo = (r1 - mid.astype(F32)).astype(BF16)
    t = tri.astype(BF16)
    d = lambda p: lax.dot_general(t, p, _dims(1, 0), preferred_element_type=F32)
    return d(hi) + d(mid) + d(lo)


def _fox_gate_fwd(raw, raw_blk, bias, *, name):
    def body(raw_ref, b_ref, c_ref, ct_ref):
        li = lax.broadcasted_iota(jnp.int32, (BLOCK, BLOCK), 0)
        si = lax.broadcasted_iota(jnp.int32, (BLOCK, BLOCK), 1)
        tri = jnp.where(li >= si, 1.0, 0.0)
        carry = jnp.zeros((1, BLOCK), F32)
        for j in range(N_CHUNK):
            r = slice(j * BLOCK, (j + 1) * BLOCK)
            lf = jnp.where(j * BLOCK + li >= PAD_ROWS, -_softplus(-(raw_ref[r, :] + b_ref[...])), 0.0)
            cv = _tri_dot(tri, lf) + carry
            c_ref[r, :] = cv
            ct_ref[:, r] = cv.T
            carry = carry + jnp.sum(lf, axis=0, keepdims=True)

    return pl.pallas_call(
        body, name=name, grid=(1,),
        in_specs=[_bs((LP, BLOCK), lambda j: (0, raw_blk)), _bs((1, BLOCK), lambda j: (0, 0))],
        out_specs=[_bs((LP, BLOCK), lambda j: (0, 0)), _bs((BLOCK, LP), lambda j: (0, 0))],
        out_shape=[jax.ShapeDtypeStruct((LP, BLOCK), F32), jax.ShapeDtypeStruct((BLOCK, LP), F32)],
        compiler_params=_cparams(("arbitrary",)),
    )(raw, bias)


def _fox_gate_bwd(raw, raw_blk, bias, dc, dct, *, name):
    def body(raw_ref, b_ref, dc_ref, dct_ref, draw_ref, db_ref):
        li = lax.broadcasted_iota(jnp.int32, (BLOCK, BLOCK), 0)
        si = lax.broadcasted_iota(jnp.int32, (BLOCK, BLOCK), 1)
        tri_t = jnp.where(li <= si, 1.0, 0.0)
        carry = jnp.zeros((1, BLOCK), F32)
        dsum = jnp.zeros((1, BLOCK), F32)
        for j in reversed(range(N_CHUNK)):
            r = slice(j * BLOCK, (j + 1) * BLOCK)
            dcv = dc_ref[r, :] + dct_ref[:, r].T
            dlf = _tri_dot(tri_t, dcv) + carry
            carry = carry + jnp.sum(dcv, axis=0, keepdims=True)
            draw = jnp.where(j * BLOCK + li >= PAD_ROWS, dlf * (1.0 - _sigmoid(raw_ref[r, :] + b_ref[...])), 0.0)
            draw_ref[r, :] = draw
            dsum = dsum + jnp.sum(draw, axis=0, keepdims=True)
        db_ref[...] = dsum

    return pl.pallas_call(
        body, name=name, grid=(1,),
        in_specs=[_bs((LP, BLOCK), lambda j: (0, raw_blk)), _bs((1, BLOCK), lambda j: (0, 0)),
                  _bs((LP, BLOCK), lambda j: (0, 0)), _bs((BLOCK, LP), lambda j: (0, 0))],
        out_specs=[_bs((LP, BLOCK), lambda j: (0, 0)), _bs((1, BLOCK), lambda j: (0, 0))],
        out_shape=[jax.ShapeDtypeStruct((LP, BLOCK), F32), jax.ShapeDtypeStruct((1, BLOCK), F32)],
        compiler_params=_cparams(("arbitrary",)),
    )(raw, bias, dc, dct)


ATT_W = 256
ATT_QB = 272
ATT_STEPS = LP // ATT_QB
ATT_KEYS = (640, 1152, 1664, LP)
ATT_BLOCKS_PER_CLASS = ATT_STEPS // len(ATT_KEYS)


def _lane_head(width, per, mod=None):
    lane = lax.broadcasted_iota(jnp.int32, (1, width), 1)
    if mod is not None:
        lane = lane % mod
    return lane // per


def _attn_mask(i, kw):
    r = i * ATT_QB + lax.broadcasted_iota(jnp.int32, (ATT_QB, kw), 0)
    c = lax.broadcasted_iota(jnp.int32, (ATT_QB, kw), 1)
    return (c <= r) & ((c >= PAD_ROWS) | (r < PAD_ROWS))


def _attn_by_key_class(i, fn):
    for p, kw in enumerate(ATT_KEYS):
        @pl.when(i // ATT_BLOCKS_PER_CLASS == p)
        def _(kw=kw):
            fn(kw)


def _attn_specs(q, k, v, bias, rope):
    qspec = lambda blk, w=ATT_W: _bs((ATT_QB, w), lambda i: (i, blk))
    fspec = lambda blk, w=ATT_W: _bs((LP, w), lambda i: (0, blk))
    ins = [q[0], k[0], v[0]]
    specs = [qspec(q[1]), fspec(k[1]), fspec(v[1])]
    if bias is not None:
        ins += [bias[0], bias[1]]
        specs += [qspec(0, BLOCK), _bs((BLOCK, LP), lambda i: (0, 0))]
    if rope is not None:
        ins += [rope[0][0], rope[1][0]]
        specs += [qspec(rope[0][1], BLOCK), fspec(rope[1][1], BLOCK)]
    return ins, specs, qspec, fspec


def _attn_fwd(q, k, v, *, scale, name, bias=None, rope=None):
    ins, specs, qspec, fspec = _attn_specs(q, k, v, bias, rope)
    has_bias, has_rope = bias is not None, rope is not None

    def body(*refs):
        it = iter(refs)
        q_ref, k_ref, v_ref = next(it), next(it), next(it)
        if has_bias:
            c_ref, ct_ref = next(it), next(it)
        if has_rope:
            qr_ref, kr_ref = next(it), next(it)
        o_ref, lse_ref = next(it), next(it)
        i = pl.program_id(0)

        def block(kw):
            ok = _attn_mask(i, kw)
            qv, kv, vv = q_ref[...].astype(BF16), k_ref[0:kw, :].astype(BF16), v_ref[0:kw, :].astype(BF16)
            hid, l128 = _lane_head(ATT_W, FOX_HD), _lane_head(BLOCK, 1)
            if has_rope:
                rid = _lane_head(BLOCK, ROPE_HALF, 64)
                qrv, krv = qr_ref[...].astype(BF16), kr_ref[0:kw, :].astype(BF16)
            def head(h, carry):
                o_acc, lse_acc = carry
                s = _raw_bdot(jnp.where(hid == h, qv, 0.0), kv, 1, 1)
                if has_rope:
                    s = s + _raw_bdot(jnp.where(rid == h, qrv, 0.0), krv, 1, 1)
                s = s * scale
                if has_bias:
                    cq = jnp.sum(jnp.where(l128 == h, c_ref[...], 0.0), axis=1, keepdims=True)
                    s = s + (cq - ct_ref[pl.ds(h, 1), 0:kw])
                s = jnp.where(ok, s, NEG)
                m = jnp.max(s, axis=1, keepdims=True)
                p = jnp.exp(s - m)
                l = jnp.sum(p, axis=1, keepdims=True)
                o_acc = jnp.where(hid == h, _raw_bdot(p, vv, 1, 0) / l, o_acc)
                lse_acc = jnp.where(l128 == h, m + jnp.log(l), lse_acc)
                return o_acc, lse_acc

            o_acc, lse_acc = lax.fori_loop(
                0, FOX_HEADS, head, (jnp.zeros((ATT_QB, ATT_W), F32), jnp.zeros((ATT_QB, BLOCK), F32)), unroll=True)
            o_ref[...] = o_acc
            lse_ref[...] = lse_acc

        _attn_by_key_class(i, block)

    return pl.pallas_call(
        body, name=name, grid=(ATT_STEPS,), in_specs=specs, out_specs=[qspec(0), qspec(0, BLOCK)],
        out_shape=[jax.ShapeDtypeStruct((LP, ATT_W), F32), jax.ShapeDtypeStruct((LP, BLOCK), F32)],
        compiler_params=_cparams(("parallel",)),
    )(*ins)


def _attn_bwd(q, k, v, o, lse, do, *, scale, name, bias=None, rope=None):
    ins, specs, qspec, fspec = _attn_specs(q, k, v, bias, rope)
    has_bias, has_rope = bias is not None, rope is not None
    ins += [o, lse, do[0]]
    specs += [qspec(0), qspec(0, BLOCK), qspec(do[1])]

    def body(*refs):
        it = iter(refs)
        q_ref, k_ref, v_ref = next(it), next(it), next(it)
        if has_bias:
            c_ref, ct_ref = next(it), next(it)
        if has_rope:
            qr_ref, kr_ref = next(it), next(it)
        o_ref, lse_ref, do_ref = next(it), next(it), next(it)
        dq_ref, dk_ref, dv_ref = next(it), next(it), next(it)
        if has_bias:
            dc_ref, dct_ref = next(it), next(it)
        if has_rope:
            dqr_ref, dkr_ref = next(it), next(it)
        i = pl.program_id(0)

        @pl.when(i == 0)
        def _():
            dk_ref[...] = jnp.zeros_like(dk_ref)
            dv_ref[...] = jnp.zeros_like(dv_ref)
            if has_rope:
                dkr_ref[...] = jnp.zeros_like(dkr_ref)
            if has_bias:
                dct_ref[...] = jnp.zeros_like(dct_ref)

        def block(kw):
            ok = _attn_mask(i, kw)
            qv, kv, vv = q_ref[...].astype(BF16), k_ref[0:kw, :].astype(BF16), v_ref[0:kw, :].astype(BF16)
            dov, lsev = do_ref[...], lse_ref[...]
            dov_ov = dov * o_ref[...]
            dov = dov.astype(BF16)
            hid, l128 = _lane_head(ATT_W, FOX_HD), _lane_head(BLOCK, 1)
            if has_rope:
                rid = _lane_head(BLOCK, ROPE_HALF, 64)
                qrv, krv = qr_ref[...].astype(BF16), kr_ref[0:kw, :].astype(BF16)

            def head(h, carry):
                dq_acc, aux_acc = carry
                qm = jnp.where(hid == h, qv, 0.0)
                s = _raw_bdot(qm, kv, 1, 1)
                if has_rope:
                    qrm = jnp.where(rid == h, qrv, 0.0)
                    s = s + _raw_bdot(qrm, krv, 1, 1)
                s = s * scale
                if has_bias:
                    cq = jnp.sum(jnp.where(l128 == h, c_ref[...], 0.0), axis=1, keepdims=True)
                    s = s + (cq - ct_ref[pl.ds(h, 1), 0:kw])
                s = jnp.where(ok, s, NEG)
                p = jnp.exp(s - jnp.sum(jnp.where(l128 == h, lsev, 0.0), axis=1, keepdims=True))
                dom = jnp.where(hid == h, dov, 0.0)
                dp = _raw_bdot(dom, vv, 1, 1)
                delta = jnp.sum(jnp.where(hid == h, dov_ov, 0.0), axis=1, keepdims=True)
                ds = p * (dp - delta)
                dsb, pb = ds.astype(BF16), p.astype(BF16)
                dq_acc = jnp.where(hid == h, _raw_bdot(dsb, kv, 1, 0) * scale, dq_acc)
                dk_ref[0:kw, :] += _raw_bdot(dsb, qm, 0, 0) * scale
                dv_ref[0:kw, :] += _raw_bdot(pb, dom, 0, 0)
                if has_rope:
                    aux_acc = jnp.where(rid == h, _raw_bdot(dsb, krv, 1, 0) * scale, aux_acc)
                    dkr_ref[0:kw, :] += _raw_bdot(dsb, qrm, 0, 0) * scale
                if has_bias:
                    aux_acc = jnp.where(l128 == h, jnp.sum(ds, axis=1, keepdims=True), aux_acc)
                    dct_ref[pl.ds(h, 1), 0:kw] -= jnp.sum(ds, axis=0, keepdims=True)
                return dq_acc, aux_acc

            dq_acc, aux_acc = lax.fori_loop(
                0, FOX_HEADS, head, (jnp.zeros((ATT_QB, ATT_W), F32), jnp.zeros((ATT_QB, BLOCK), F32)))
            dq_ref[...] = dq_acc
            if has_bias:
                dc_ref[...] = aux_acc
            if has_rope:
                dqr_ref[...] = aux_acc

        _attn_by_key_class(i, block)

    wide = jax.ShapeDtypeStruct((LP, ATT_W), F32)
    narrow = jax.ShapeDtypeStruct((LP, BLOCK), F32)
    out_specs = [qspec(0), fspec(0), fspec(0)]
    out_shape = [wide, wide, wide]
    if has_bias:
        out_specs += [qspec(0, BLOCK), _bs((BLOCK, LP), lambda i: (0, 0))]
        out_shape += [narrow, jax.ShapeDtypeStruct((BLOCK, LP), F32)]
    if has_rope:
        out_specs += [qspec(0, BLOCK), fspec(0, BLOCK)]
        out_shape += [narrow, narrow]
    return pl.pallas_call(
        body, name=name, grid=(ATT_STEPS,), in_specs=specs, out_specs=out_specs, out_shape=out_shape,
        compiler_params=_cparams(("arbitrary",)),
    )(*ins)


def _loss_head(y, target, *, name):
    tile = 272

    def body(y_ref, t_ref, dy_ref, loss_ref):
        i = pl.program_id(0)
        rows = i * tile + lax.broadcasted_iota(jnp.int32, (tile, D_MODEL), 0)
        err = jnp.where(rows >= BLOCK, y_ref[...] - t_ref[...], 0.0)
        dy_ref[...] = err * (1.0 / D_MODEL)
        part = 0.5 * jnp.sum(jnp.sum(err * err, axis=1, keepdims=True) * (1.0 / D_MODEL), axis=0, keepdims=True)
        part = jnp.broadcast_to(part, (1, BLOCK))

        @pl.when(i == 0)
        def _():
            loss_ref[...] = part

        @pl.when(i > 0)
        def _():
            loss_ref[...] += part

    return pl.pallas_call(
        body, name=name, grid=(LP // tile,),
        in_specs=[_bs((tile, D_MODEL), lambda i: (i, 0)), _bs((tile, D_MODEL), lambda i: (i, 0))],
        out_specs=[_bs((tile, D_MODEL), lambda i: (i, 0)), _bs((1, BLOCK), lambda i: (0, 0))],
        out_shape=[jax.ShapeDtypeStruct((LP, D_MODEL), F32), jax.ShapeDtypeStruct((1, BLOCK), F32)],
        compiler_params=_cparams(("arbitrary",)),
    )(y, target)


def _adamw(w, gs, m, v, *, name, after=()):
    if w.ndim == 2:
        w, m, v = w[None], m[None], v[None]
        squeeze = True
    else:
        squeeze = False
    NL, R, C = w.shape
    assert len(gs) == NL
    CG = gs[0].shape[1]
    tile = _tile(R, 256, 8)

    def body(*refs):
        w_ref, g_refs = refs[0], refs[1:1 + NL]
        m_ref, v_ref = refs[1 + NL:3 + NL]
        go_ref, d_ref, nm_ref, nv_ref = refs[3 + NL + len(after):]
        gv = g_refs[0][:, :C]
        for j in range(1, NL):
            gv = jnp.where(pl.program_id(0) == j, g_refs[j][:, :C], gv)
        nm = ADAM_B1 * m_ref[...] + (1.0 - ADAM_B1) * gv
        nv = ADAM_B2 * v_ref[...] + (1.0 - ADAM_B2) * (gv * gv)
        m_hat = nm / (1.0 - ADAM_B1 ** ADAM_STEP)
        v_hat = nv / (1.0 - ADAM_B2 ** ADAM_STEP)
        go_ref[...] = gv
        d_ref[...] = -ADAM_LR * (m_hat / (jnp.sqrt(v_hat) + ADAM_EPS) + ADAM_WD * w_ref[...])
        nm_ref[...] = nm
        nv_ref[...] = nv

    spec = _bs((None, tile, C), lambda l, i: (l, i, 0))
    gspecs = [_bs((tile, CG), lambda l, i, j=j: (jnp.where(l == j, i, 0), 0)) for j in range(NL)]
    res = pl.pallas_call(
        body, name=name, grid=(NL, R // tile), in_specs=[spec, *gspecs, spec, spec, *[ANY] * len(after)],
        out_specs=[spec] * 4, out_shape=[jax.ShapeDtypeStruct((NL, R, C), F32)] * 4,
        compiler_params=_cparams(("parallel", "parallel")),
    )(w, *gs, m, v, *after)
    return [r[0] for r in res] if squeeze else res


def _my_pos():
    return lax.axis_index("x"), lax.axis_index("y"), lax.axis_index("c")


def _other_chips(x, y):
    return [(1 - x, y), (x, 1 - y), (1 - x, 1 - y)]


def _allgather_chips(shards):
    n = len(shards)
    per = 7

    def body(*refs):
        ins, outs = refs[:n], refs[n:2 * n]
        send_sems, recv_sems = refs[2 * n], refs[2 * n + 1]
        x, y, c = _my_pos()
        chips = _other_chips(x, y)
        sibling, me = (x, y, 1 - c), 2 * x + y

        def cp(a, kk, src, dst, to):
            return pltpu.make_async_remote_copy(src_ref=src, dst_ref=dst, send_sem=send_sems.at[per * a + kk],
                                                recv_sem=recv_sems.at[per * a + kk], device_id=to, device_id_type=MESH)

        sends = []
        for a in range(n):
            for j, chip in enumerate(chips):
                sends.append(cp(a, j, ins[a].at[c], outs[a].at[me, c], (*chip, c)))
            sends.append(cp(a, 3, ins[a], outs[a].at[me], sibling))
        for s in sends:
            s.start()
        for a in range(n):
            for j, chip in enumerate(chips):
                slab = outs[a].at[2 * chip[0] + chip[1], c]
                cp(a, j, slab, slab, (x, y, c)).wait_recv()
                fwd = cp(a, 4 + j, slab, slab, sibling)
                fwd.start()
                sends.append(fwd)
        for a in range(n):
            cp(a, 3, ins[a], outs[a].at[me], (x, y, c)).wait_recv()
            for j, chip in enumerate(chips):
                slab = outs[a].at[2 * chip[0] + chip[1], 1 - c]
                cp(a, 4 + j, slab, slab, (x, y, c)).wait_recv()
        for s in sends:
            s.wait_send()

    return pl.pallas_call(
        body, name="allgather_chips", in_specs=[ANY] * n, out_specs=[ANY] * n,
        out_shape=[jax.ShapeDtypeStruct((N_CHIPS,) + s.shape, s.dtype) for s in shards],
        scratch_shapes=[pltpu.SemaphoreType.DMA((per * n,)), pltpu.SemaphoreType.DMA((per * n,))],
    )(*shards)


def _rs_swap_rows(gs, tag):
    n = len(gs)

    def body(*refs):
        ins, outs = refs[:n], refs[n:2 * n]
        send_sems, recv_sems = refs[2 * n], refs[2 * n + 1]
        x, y, c = _my_pos()
        cps = []
        for a in range(n):
            half = ins[a].shape[1] // 2
            cps.append(pltpu.make_async_remote_copy(
                src_ref=ins[a].at[:, pl.ds((1 - c) * half, half)], dst_ref=outs[a], send_sem=send_sems.at[a],
                recv_sem=recv_sems.at[a], device_id=(x, y, 1 - c), device_id_type=MESH))
        for cp in cps:
            cp.start()
        for cp in cps:
            cp.wait()

    return pl.pallas_call(
        body, name=f"rs_swap_rows_{tag}", in_specs=[ANY] * n, out_specs=[ANY] * n,
        out_shape=[jax.ShapeDtypeStruct((N_CHIPS, g.shape[1] // 2, g.shape[2]), g.dtype) for g in gs],
        scratch_shapes=[pltpu.SemaphoreType.DMA((n,)), pltpu.SemaphoreType.DMA((n,))],
    )(*gs)


RS_ADD_VMEM_BYTES = 24 * 1024 * 1024


def _rs_tile(H, C, n):
    return _tile(H, max(16, RS_ADD_VMEM_BYTES // (28 * n * C)), 16)


def _rs_add_pair(gs, rs, pos, *, name):
    n = len(gs)
    _, H, C = rs[0].shape
    tile = _rs_tile(H, C, n)
    nt = H // tile

    def body(pos_ref, *refs):
        for a in range(n):
            s = refs[a][...] + refs[n + a][...]
            refs[2 * n + 2 * a][...] = s
            refs[2 * n + 2 * a + 1][...] = s.astype(BF16)

    spec = _bs((None, tile, C), lambda k, i, pos_ref: (k, i, 0))
    g_spec = _bs((None, tile, C), lambda k, i, pos_ref: (k, pos_ref[1] * nt + i, 0))
    grid_spec = pltpu.PrefetchScalarGridSpec(
        num_scalar_prefetch=1, grid=(N_CHIPS, nt), in_specs=[g_spec] * n + [spec] * n, out_specs=[spec] * (2 * n))
    res = pl.pallas_call(
        body, name=name, grid_spec=grid_spec,
        out_shape=[jax.ShapeDtypeStruct((N_CHIPS, H, C), F32), jax.ShapeDtypeStruct((N_CHIPS, H, C), BF16)] * n,
        compiler_params=_cparams(("parallel", "parallel")),
    )(pos, *gs, *rs)
    return [(res[2 * a], res[2 * a + 1]) for a in range(n)]


def _exchange_copies(srcs, lands, send_sems, recv_sems):
    x, y, c = _my_pos()
    starts, landing = [], []
    for a in range(len(srcs)):
        for j, chip in enumerate(_other_chips(x, y)):
            sems = dict(send_sem=send_sems.at[3 * a + j], recv_sem=recv_sems.at[3 * a + j], device_id_type=MESH)
            starts.append(pltpu.make_async_remote_copy(
                src_ref=srcs[a].at[2 * chip[0] + chip[1]], dst_ref=lands[a].at[j], device_id=(*chip, c), **sems))
            landing.append(pltpu.make_async_remote_copy(
                src_ref=lands[a].at[j], dst_ref=lands[a].at[j], device_id=(x, y, c), **sems))
    return starts, landing


def _gather_copies(srcs, lands, send_sems, recv_sems):
    x, y, c = _my_pos()
    me = 2 * x + y
    starts, landing = [], []
    for a in range(len(srcs)):
        half = srcs[a].shape[0] // 2
        mine = pl.ds(c * half, half)
        for j, chip in enumerate(_other_chips(x, y)):
            sems = dict(send_sem=send_sems.at[3 * a + j], recv_sem=recv_sems.at[3 * a + j], device_id_type=MESH)
            starts.append(pltpu.make_async_remote_copy(
                src_ref=srcs[a].at[mine], dst_ref=lands[a].at[me, mine], device_id=(*chip, c), **sems))
            slab = lands[a].at[2 * chip[0] + chip[1], mine]
            landing.append(pltpu.make_async_remote_copy(src_ref=slab, dst_ref=slab, device_id=(x, y, c), **sems))
    return starts, landing


HBM = pl.BlockSpec(memory_space=pltpu.HBM)
SEM = pl.BlockSpec(memory_space=pltpu.SEMAPHORE)


def _ici_start(copies_fn, srcs, land_shapes, *, name, after=(), sems_per_array=3):
    n, na = len(srcs), len(after)

    def body(*refs):
        starts, _ = copies_fn(refs[:n], refs[n:2 * n], refs[2 * n + na], refs[2 * n + na + 1])
        for cp in starts:
            cp.start()
        refs[-1][...] = jnp.zeros_like(refs[-1])

    sems = pltpu.SemaphoreType.DMA((sems_per_array * n,))
    hbm = lambda s: pltpu.HBM(s.shape, s.dtype)
    lands = [pltpu.with_memory_space_constraint(
        lax.empty(s.shape, s.dtype) if isinstance(s, jax.ShapeDtypeStruct) else s, pltpu.HBM) for s in land_shapes]
    res = pl.pallas_call(
        body, name=name, in_specs=[HBM] * (2 * n) + [ANY] * na,
        out_specs=(SEM, SEM, *[HBM] * (2 * n), pl.BlockSpec(memory_space=pltpu.VMEM)),
        out_shape=(sems, sems, *[hbm(s) for s in srcs], *[hbm(s) for s in land_shapes],
                   jax.ShapeDtypeStruct((8, BLOCK), F32)),
        input_output_aliases={i: 2 + i for i in range(2 * n)},
        compiler_params=pltpu.CompilerParams(has_side_effects=pltpu.SideEffectType.DATAFLOW_SIDE_EFFECTING),
    )(*[pltpu.with_memory_space_constraint(s, pltpu.HBM) for s in srcs], *lands, *after)
    return res[0], res[1], list(res[2:2 + n]), list(res[2 + n:2 + 2 * n]), res[-1]


def _ici_wait(copies_fn, send_sems, recv_sems, srcs, lands, after, *, name):
    n = len(srcs)
    after = list(after) if isinstance(after, (list, tuple)) else [after]

    def body(*refs):
        starts, landing = copies_fn(refs[:n], refs[n:2 * n], refs[2 * n], refs[2 * n + 1])
        for cp in starts:
            cp.wait_send()
        for cp in landing:
            cp.wait_recv()

    hbm = lambda s: pltpu.HBM(s.shape, s.dtype)
    res = pl.pallas_call(
        body, name=name, in_specs=[*[HBM] * (2 * n), SEM, SEM, *[ANY] * len(after)], out_specs=[HBM] * (2 * n),
        out_shape=[*[hbm(s) for s in srcs], *[hbm(s) for s in lands]],
        input_output_aliases={i: i for i in range(2 * n)},
        compiler_params=pltpu.CompilerParams(has_side_effects=pltpu.SideEffectType.DATAFLOW_SIDE_EFFECTING),
    )(*srcs, *lands, send_sems, recv_sems, *after)
    return list(res[:n]), list(res[n:])


D2D_COPIES = 4


def _d2d_copies(ins, outs, send_sems, recv_sems):
    x, y, c = _my_pos()
    me, sibling = 2 * x + y, (x, y, 1 - c)
    starts, landing = [], []
    for a in range(len(ins)):
        half = ins[a].shape[0] // 2
        mine, theirs = pl.ds(c * half, half), pl.ds((1 - c) * half, half)
        pairs = [(ins[a], outs[a].at[me], outs[a].at[me])]
        for chip in _other_chips(x, y):
            k = 2 * chip[0] + chip[1]
            pairs.append((outs[a].at[k, mine], outs[a].at[k, mine], outs[a].at[k, theirs]))
        for j, (src, dst, lands_here) in enumerate(pairs):
            sems = dict(send_sem=send_sems.at[D2D_COPIES * a + j], recv_sem=recv_sems.at[D2D_COPIES * a + j],
                        device_id_type=MESH)
            starts.append(pltpu.make_async_remote_copy(src_ref=src, dst_ref=dst, device_id=sibling, **sems))
            landing.append(pltpu.make_async_remote_copy(src_ref=lands_here, dst_ref=lands_here, device_id=(x, y, c),
                                                        **sems))
    return starts, landing


def _gather_d2d(shards, lands, tag):
    n = len(shards)

    def body(*refs):
        starts, landing = _d2d_copies(refs[:n], refs[2 * n:3 * n], refs[3 * n], refs[3 * n + 1])
        for cp in starts:
            cp.start()
        for cp in landing:
            cp.wait_recv()
        for cp in starts:
            cp.wait_send()

    return pl.pallas_call(
        body, name=f"gather_d2d_{tag}", in_specs=[ANY] * (2 * n), out_specs=[ANY] * n,
        out_shape=[jax.ShapeDtypeStruct(s.shape, s.dtype) for s in lands],
        input_output_aliases={n + a: a for a in range(n)},
        scratch_shapes=[pltpu.SemaphoreType.DMA((D2D_COPIES * n,)), pltpu.SemaphoreType.DMA((D2D_COPIES * n,))],
    )(*shards, *lands)


def _rs_add_chips(p32s, r16s, pos, *, name):
    n = len(p32s)
    _, H, C = p32s[0].shape
    tile = _rs_tile(H, C, n)
    nt = H // tile

    def body(pos_ref, *refs):
        for a in range(n):
            p_ref, r_ref = refs[a], refs[n + a]
            refs[2 * n + a][...] = ((p_ref[...] + r_ref[0].astype(F32)) + r_ref[1].astype(F32)) + r_ref[2].astype(F32)

    grid_spec = pltpu.PrefetchScalarGridSpec(
        num_scalar_prefetch=1, grid=(nt,),
        in_specs=[_bs((None, tile, C), lambda i, pos_ref: (pos_ref[0], i, 0))] * n
        + [_bs((3, tile, C), lambda i, pos_ref: (0, i, 0))] * n,
        out_specs=[_bs((tile, C), lambda i, pos_ref: (pos_ref[1] * nt + i, 0))] * n)
    return pl.pallas_call(
        body, name=name, grid_spec=grid_spec, out_shape=[jax.ShapeDtypeStruct((2 * H, C), F32)] * n,
        compiler_params=_cparams(("parallel",)),
    )(pos, *p32s, *r16s)


def _rs_join_rows(fs, tag):
    n = len(fs)

    def body(*refs):
        outs = refs[n:2 * n]
        send_sems, recv_sems = refs[2 * n], refs[2 * n + 1]
        x, y, c = _my_pos()
        for a in range(n):
            half = outs[a].shape[0] // 2
            mine = outs[a].at[pl.ds(c * half, half)]
            pltpu.make_async_remote_copy(src_ref=mine, dst_ref=mine, send_sem=send_sems.at[a],
                                         recv_sem=recv_sems.at[a], device_id=(x, y, 1 - c), device_id_type=MESH).start()
        for a in range(n):
            half = outs[a].shape[0] // 2
            pltpu.make_async_remote_copy(
                src_ref=outs[a].at[pl.ds(c * half, half)], dst_ref=outs[a].at[pl.ds((1 - c) * half, half)],
                send_sem=send_sems.at[a], recv_sem=recv_sems.at[a], device_id=(x, y, 1 - c), device_id_type=MESH).wait()

    return pl.pallas_call(
        body, name=f"rs_join_rows_{tag}", in_specs=[ANY] * n, out_specs=[ANY] * n,
        out_shape=[jax.ShapeDtypeStruct(f.shape, f.dtype) for f in fs],
        input_output_aliases={a: a for a in range(n)},
        scratch_shapes=[pltpu.SemaphoreType.DMA((n,)), pltpu.SemaphoreType.DMA((n,))],
    )(*fs)


def _pos_vector():
    x, y, c = _my_pos()
    return jnp.stack([2 * x + y, c]).astype(jnp.int32)


def _swap_copies(srcs, lands, send_sems, recv_sems):
    x, y, c = _my_pos()
    starts, landing = [], []
    for a in range(len(srcs)):
        half = srcs[a].shape[1] // 2
        sems = dict(send_sem=send_sems.at[3 * a], recv_sem=recv_sems.at[3 * a], device_id_type=MESH)
        starts.append(pltpu.make_async_remote_copy(
            src_ref=srcs[a].at[:, pl.ds((1 - c) * half, half)], dst_ref=lands[a], device_id=(x, y, 1 - c), **sems))
        landing.append(pltpu.make_async_remote_copy(src_ref=lands[a], dst_ref=lands[a], device_id=(x, y, c), **sems))
    return starts, landing


def _swap_land_shapes(gs):
    return [jax.ShapeDtypeStruct((N_CHIPS, g.shape[1] // 2, g.shape[2]), g.dtype) for g in gs]


def _same_shape_runs(arrays):
    runs, start = [], 0
    for i in range(1, len(arrays) + 1):
        if i == len(arrays) or arrays[i].shape != arrays[start].shape:
            runs.append((start, i))
            start = i
    return runs


def _rs_add_pairs(gs, r1, names, tag):
    pos = _pos_vector()
    out = []
    for a, b in _same_shape_runs(gs):
        out += _rs_add_pair(gs[a:b], r1[a:b], pos, name=f"rs_add_pair_{tag}_{names[a]}")
    return out


def _rs_pair_sums(gs, names, tag):
    return _rs_add_pairs(gs, _rs_swap_rows(gs, tag), names, tag)


def _rs_finish(pairs, r2, names, tag):
    pos = _pos_vector()
    p32s = [p[0] for p in pairs]
    fs = []
    for a, b in _same_shape_runs(p32s):
        fs += _rs_add_chips(p32s[a:b], r2[a:b], pos, name=f"rs_add_chips_{tag}_{names[a]}")
    return _rs_join_rows(fs, tag)


def _exchange_land_shapes(pairs):
    return [jax.ShapeDtypeStruct((3,) + p[1].shape[1:], p[1].dtype) for p in pairs]


def _allreduce_small(buf):
    R, W = buf.shape

    def body(b_ref, o_ref, gather, send_sems, recv_sems):
        x, y, c = _my_pos()
        me = 4 * x + 2 * y + c
        gather[me] = b_ref[...]
        cps = []
        for d in range(1, 8):
            peer = (x ^ (d >> 2), y ^ ((d >> 1) & 1), c ^ (d & 1))
            cps.append(pltpu.make_async_remote_copy(
                src_ref=b_ref, dst_ref=gather.at[me], send_sem=send_sems.at[d - 1], recv_sem=recv_sems.at[d - 1],
                device_id=peer, device_id_type=MESH))
        for cp in cps:
            cp.start()
        for d in range(1, 8):
            pltpu.make_async_remote_copy(
                src_ref=b_ref, dst_ref=gather.at[me ^ d], send_sem=send_sems.at[d - 1], recv_sem=recv_sems.at[d - 1],
                device_id=(x, y, c), device_id_type=MESH).wait_recv()
        for cp in cps:
            cp.wait_send()
        acc = gather[0]
        for d in range(1, 8):
            acc = acc + gather[d]
        o_ref[...] = acc

    vm = pl.BlockSpec(memory_space=pltpu.VMEM)
    return pl.pallas_call(
        body, name="allreduce_small", in_specs=[vm], out_specs=vm, out_shape=jax.ShapeDtypeStruct((R, W), F32),
        scratch_shapes=[pltpu.VMEM((8, R, W), F32), pltpu.SemaphoreType.DMA((7,)), pltpu.SemaphoreType.DMA((7,))],
    )(buf)


def _heads(a, h, d):
    return a.reshape(a.shape[0], h, d).transpose(1, 0, 2)


def _unheads(a):
    h, L, d = a.shape
    return a.transpose(1, 0, 2).reshape(L, h * d)


def _rope_tables():
    pos = jnp.maximum(jnp.arange(LP, dtype=F32) - PAD_ROWS, 0.0)
    inv_freq = 1.0 / (ROPE_THETA ** (jnp.arange(0, MLA_ROPE, 2, dtype=F32) / MLA_ROPE))
    ang = pos[:, None] * inv_freq[None, :]
    cos, sin = jnp.tile(jnp.cos(ang), (1, MLA_HEADS)), jnp.tile(jnp.sin(ang), (1, MLA_HEADS))
    return jnp.concatenate([cos, cos], axis=1), jnp.concatenate([-sin, sin], axis=1)


def _lane_pad(a, width=BLOCK):
    return jnp.pad(a, ((0, 0), (0, width - a.shape[1])))


def _pad_in_proj(w):
    sl = lambda start, size: w[:, start:start + size]
    return jnp.concatenate([
        sl(OC_Z, 512), sl(OC_XBC, 768), sl(OC_FQ, 256), sl(OC_FK, 256), sl(OC_FV, 256), sl(OC_CQ, 256), sl(OC_CKV, 128),
        _lane_pad(sl(OC_DT, SSD_HEADS)), _lane_pad(sl(OC_FR, FOX_HEADS)),
        jnp.tile(sl(OC_KR, ROPE_HALF), (1, MLA_HEADS)), jnp.tile(sl(OC_KR + ROPE_HALF, ROPE_HALF), (1, MLA_HEADS))], axis=1)


def _in_proj_grad_chunks(wp):
    rope = lambda start: wp[:, start:start + 64].reshape(wp.shape[0], MLA_HEADS, ROPE_HALF).sum(axis=1)
    segs = [(wp, PC_Z, 512), (wp, PC_XBC, 768), (wp, PC_DT, SSD_HEADS), (wp, PC_FQ, 256), (wp, PC_FK, 256),
            (wp, PC_FV, 256), (wp, PC_FR, FOX_HEADS), (wp, PC_CQ, 256), (wp, PC_CKV, 128),
            (rope(PC_KR), 0, ROPE_HALF), (rope(PC_KR + 64), 0, ROPE_HALF)]
    chunks = []
    for k in range(N_CHIPS):
        lo, hi, pos, pieces = k * IN_SHARD, (k + 1) * IN_SHARD, 0, []
        for arr, start, size in segs:
            a, b = max(lo, pos), min(hi, pos + size)
            if a < b:
                pieces.append(arr[:, start + a - pos:start + b - pos])
            pos += size
        pieces.append(jnp.zeros((wp.shape[0], IN_SHARD_P - IN_SHARD), wp.dtype))
        chunks.append(jnp.concatenate(pieces, axis=1))
    return jnp.stack(chunks)


def _regroup_uq(w):
    w3 = w.reshape(w.shape[0], MLA_HEADS, MLA_NOPE + MLA_ROPE)
    return jnp.concatenate([w3[:, :, :MLA_NOPE].reshape(w.shape[0], -1),
                            w3[:, :, MLA_NOPE:MLA_NOPE + ROPE_HALF].reshape(w.shape[0], -1),
                            w3[:, :, MLA_NOPE + ROPE_HALF:].reshape(w.shape[0], -1)], axis=1)


def _ungroup_uq(wp):
    n = wp.shape[0]
    return jnp.concatenate([wp[:, :256].reshape(n, MLA_HEADS, MLA_NOPE), wp[:, 256:320].reshape(n, MLA_HEADS, ROPE_HALF),
                            wp[:, 320:].reshape(n, MLA_HEADS, ROPE_HALF)], axis=2).reshape(n, -1)


def _regroup_ukv(w):
    w3 = w.reshape(w.shape[0], MLA_HEADS, MLA_NOPE + MLA_V)
    return jnp.concatenate([w3[:, :, :MLA_NOPE].reshape(w.shape[0], -1), w3[:, :, MLA_NOPE:].reshape(w.shape[0], -1)],
                           axis=1)


def _ungroup_ukv(wp):
    n = wp.shape[0]
    return jnp.concatenate([wp[:, :256].reshape(n, MLA_HEADS, MLA_NOPE), wp[:, 256:].reshape(n, MLA_HEADS, MLA_V)],
                           axis=2).reshape(n, -1)


TMF = 1088
N_IF = LP // TMF


def _chunk_rows_dx(g, w, l, chunk_h, *, name):
    N = w.shape[2]
    return _mm_core(g, w, a_spec=_bs((TMF, N), lambda i, j, k: (i, 0)),
                    b_spec=_bs((None, chunk_h, N), lambda i, j, k: (j, 0, 0)),
                    o_spec=_bs((TMF, chunk_h), lambda i, j, k: (i, j)), grid=(N_IF, N_CHIPS, 1),
                    out_shape=(LP, N_CHIPS * chunk_h), ca=1, cb=1, name=name)


def _chunk_rows_dw(a, g, chunk_h, *, name):
    N = g.shape[1]
    return _mm_core(a, g, a_spec=_bs((LP, chunk_h), lambda i, j, k: (0, i)), b_spec=_bs((LP, N), lambda i, j, k: (0, 0)),
                    o_spec=_bs((None, chunk_h, N), lambda i, j, k: (i, 0, 0)), grid=(N_CHIPS, 1, 1),
                    out_shape=(N_CHIPS, chunk_h, N), ca=0, cb=0, name=name)


def _ffn_up_swiglu(h, wg, wu, *, name, after=()):
    def body(h_ref, wg_ref, wu_ref, *refs):
        g_ref, u_ref, a_ref = refs[len(after):]
        hb = h_ref[...].astype(BF16)
        g = _raw_bdot(hb, wg_ref[...], 1, 1)
        u = _raw_bdot(hb, wu_ref[...], 1, 1)
        g_ref[...] = g
        u_ref[...] = u
        a_ref[...] = (_silu(g) * u).astype(a_ref.dtype)

    w_spec = _bs((None, HP, D_MODEL), lambda i, j: (j, 0, 0))
    o_spec = _bs((TMF, HP), lambda i, j: (i, j))
    return pl.pallas_call(
        body, name=name, grid=(N_IF, N_CHIPS),
        in_specs=[_bs((TMF, D_MODEL), lambda i, j: (i, 0)), w_spec, w_spec, *[ANY] * len(after)],
        out_specs=[o_spec] * 3,
        out_shape=[jax.ShapeDtypeStruct((LP, FP), F32), jax.ShapeDtypeStruct((LP, FP), F32),
                   jax.ShapeDtypeStruct((LP, FP), BF16)],
        compiler_params=_cparams(("parallel", "parallel")),
    )(h, wg, wu, *after)


def _ffn_down_dx_swiglu(do, wd, g, u, *, name):
    def body(do_ref, wd_ref, g_ref, u_ref, dg_ref, du_ref):
        dact = _raw_bdot(do_ref[...], wd_ref[...], 1, 1)
        gv = g_ref[...]
        sig = _sigmoid(gv)
        dg_ref[...] = (dact * u_ref[...] * (sig * (1.0 + gv * (1.0 - sig)))).astype(dg_ref.dtype)
        du_ref[...] = (dact * (gv * sig)).astype(du_ref.dtype)

    blk = _bs((TMF, HP), lambda i, j: (i, j))
    return pl.pallas_call(
        body, name=name, grid=(N_IF, N_CHIPS),
        in_specs=[_bs((TMF, D_MODEL), lambda i, j: (i, 0)), _bs((None, HP, D_MODEL), lambda i, j: (j, 0, 0)), blk, blk],
        out_specs=[blk, blk], out_shape=[jax.ShapeDtypeStruct((LP, FP), BF16)] * 2,
        compiler_params=_cparams(("parallel", "parallel")),
    )(do, wd, g, u)


def _ffn_gate_up_dw(dg, du, h, *, name):
    def body(dg_ref, du_ref, h_ref, wg_ref, wu_ref):
        hb = h_ref[...].astype(BF16)
        wg_ref[...] = _raw_bdot(dg_ref[...], hb, 0, 0)
        wu_ref[...] = _raw_bdot(du_ref[...], hb, 0, 0)

    a_spec = _bs((LP, HP), lambda k: (0, k))
    o_spec = _bs((None, HP, D_MODEL), lambda k: (k, 0, 0))
    return pl.pallas_call(
        body, name=name, grid=(N_CHIPS,), in_specs=[a_spec, a_spec, _bs((LP, D_MODEL), lambda k: (0, 0))],
        out_specs=[o_spec, o_spec], out_shape=[jax.ShapeDtypeStruct((N_CHIPS, HP, D_MODEL), F32)] * 2,
        compiler_params=_cparams(("parallel",)),
    )(dg, du, h)


def _ffn_gate_up_dx(dg, du, wg, wu, add, *, name):
    def body(dg_ref, du_ref, wg_ref, wu_ref, add_ref, o_ref, acc_ref):
        k = pl.program_id(1)

        @pl.when(k == 0)
        def _():
            acc_ref[...] = jnp.zeros_like(acc_ref)

        acc_ref[...] += _raw_bdot(dg_ref[...], wg_ref[...], 1, 0) + _raw_bdot(du_ref[...], wu_ref[...], 1, 0)

        @pl.when(k == N_CHIPS - 1)
        def _():
            o_ref[...] = acc_ref[...] + add_ref[...]

    a_spec = _bs((TMF, HP), lambda i, k: (i, k))
    w_spec = _bs((None, HP, D_MODEL), lambda i, k: (k, 0, 0))
    o_spec = _bs((TMF, D_MODEL), lambda i, k: (i, 0))
    return pl.pallas_call(
        body, name=name, grid=(N_IF, N_CHIPS), in_specs=[a_spec, a_spec, w_spec, w_spec, o_spec], out_specs=o_spec,
        out_shape=jax.ShapeDtypeStruct((LP, D_MODEL), F32), scratch_shapes=[pltpu.VMEM((TMF, D_MODEL), F32)],
        compiler_params=_cparams(("parallel", "arbitrary")),
    )(dg, du, wg, wu, add)


def _chunk_rows_mm_res_ln(a, w, chunk_h, h, gam, bet, scale, *, name):
    res_ln = _make_res_ln_fn(scale)

    def body(a_ref, w_ref, h_ref, g_ref, b_ref, o_ref, y_ref, yb_ref, acc_ref):
        k = pl.program_id(1)

        @pl.when(k == 0)
        def _():
            acc_ref[...] = jnp.zeros_like(acc_ref)

        acc_ref[...] += _raw_bdot(a_ref[...], w_ref[...], 1, 0)

        @pl.when(k == N_CHIPS - 1)
        def _():
            o = acc_ref[...]
            o_ref[...] = o
            (y,) = res_ln(0, h_ref[...], o, g_ref[...], b_ref[...])
            y_ref[...] = y
            yb_ref[...] = y.astype(yb_ref.dtype)

    row = _bs((TMF, D_MODEL), lambda i, k: (i, 0))
    par = _bs((1, D_MODEL), lambda i, k: (0, 0))
    return pl.pallas_call(
        body, name=name, grid=(N_IF, N_CHIPS),
        in_specs=[_bs((TMF, chunk_h), lambda i, k: (i, k)), _bs((None, chunk_h, D_MODEL), lambda i, k: (k, 0, 0)), row,
                  par, par],
        out_specs=[row, row, row],
        out_shape=[jax.ShapeDtypeStruct((LP, D_MODEL), F32)] * 2 + [jax.ShapeDtypeStruct((LP, D_MODEL), BF16)],
        scratch_shapes=[pltpu.VMEM((TMF, D_MODEL), F32)], compiler_params=_cparams(("parallel", "arbitrary")),
    )(a, w, h, gam, bet)


def _ffn_fwd(hp, W, pre, l, gam, bet, tag, after=()):
    h, hb = hp
    g, u, act = _ffn_up_swiglu(hb, W[pre + "_w_gate"][l], W[pre + "_w_up"][l], name=f"{tag}_up_swiglu", after=after)
    o, out, outb = _chunk_rows_mm_res_ln(act, W[pre + "_w_down"][l], HP, h, gam, bet, 0.5, name=f"{tag}_down_ln")
    return (out, outb), (h, hb, g, u, act, o)


def _ffn_bwd(dout, saved, W, pre, l, gam, bet, GB, tag):
    h, hb, g, u, act, o = saved
    (dh_a, do), (dgam, dbet) = _rowwise_bwd(_make_res_ln_fn(0.5), [h, o], [gam, bet], [dout], name=f"{tag}_ln_bwd",
                                            tile=272, grad_dtypes=[F32, BF16])
    dg, du = _ffn_down_dx_swiglu(do, W[pre + "_w_down"][l], g, u, name=f"{tag}_down_dx_swiglu")
    GB[pre + "_w_down"] = _chunk_rows_dw(act, do, HP, name=f"{tag}_down_dw")
    GB[pre + "_w_gate"], GB[pre + "_w_up"] = _ffn_gate_up_dw(dg, du, hb, name=f"{tag}_gate_up_dw")
    dh = _ffn_gate_up_dx(dg, du, W[pre + "_w_gate"][l], W[pre + "_w_up"][l], dh_a, name=f"{tag}_gate_up_dx")
    return dh, dgam, dbet


def _mixer_fwd(hp1, W, l, cosf, sins, after=()):
    h1, h1b = hp1
    tag = f"l{l}"
    proj = _mm(h1b, W["w_in_p"][l], name=f"{tag}_in_proj", after=after)
    sv = {"h1": h1, "h1b": h1b, "proj": proj}
    conv_w, conv_b = W["conv_w"][l], W["conv_b"][l][None]
    xc = _conv_fwd(proj, PC_XBC // BLOCK, conv_w, conv_b, name=f"{tag}_conv")
    dt_bias = _lane_pad(W["dt_bias"][l][None])
    dtc, dtr = _ssd_dt_fwd(proj, PC_DT // BLOCK, dt_bias, name=f"{tag}_ssd_dt")
    xh = _heads(xc[:, :SSD_D], SSD_HEADS, SSD_HD)
    bm = _heads(xc[:, SSD_D:SSD_D + 128], SSD_GROUPS, SSD_STATE)
    cm = _heads(xc[:, SSD_D + 128:], SSD_GROUPS, SSD_STATE)
    alog = jnp.broadcast_to(W["a_log"][l][:, None, None], (SSD_HEADS, 1, BLOCK))
    yh, prevs = _ssd_fwd(xh, bm, cm, dtc, dtr, alog, name=f"{tag}_ssd")
    y_raw = _unheads(yh)
    dskip = jnp.repeat(W["d_skip"][l], SSD_HD)[None]
    normg = W["ssd_norm_g"][l][None]
    post_rows = [y_raw, (xc, 256, 0), (proj, 256, PC_Z // 256)]
    (y_ssd,) = _rowwise(_ssd_post_fn, post_rows, [dskip, normg], [SSD_D], name=f"{tag}_ssd_post", tile=272,
                        ncol=SSD_GROUPS)
    sv.update(conv_w=conv_w, conv_b=conv_b, dt_bias=dt_bias, xh=xh, bm=bm, cm=cm, dtc=dtc, dtr=dtr, alog=alog,
              prevs=prevs, post_rows=post_rows, dskip=dskip, normg=normg)
    f_b = _lane_pad(W["fox_f_b"][l][None])
    cg, cgt = _fox_gate_fwd(proj, PC_FR // BLOCK, f_b, name=f"{tag}_fox_gate")
    fox_qkv = ((proj, PC_FQ // ATT_W), (proj, PC_FK // ATT_W), (proj, PC_FV // ATT_W))
    y_fox, lse_f = _attn_fwd(*fox_qkv, scale=FOX_HD ** -0.5, name=f"{tag}_fox_attn", bias=(cg, cgt))
    sv.update(f_b=f_b, cg=cg, cgt=cgt, fox_qkv=fox_qkv, y_fox=y_fox, lse_f=lse_f)
    gq, gkv = W["mla_q_norm_g"][l][None], W["mla_kv_norm_g"][l][None]
    norm_rows = [(proj, 256, PC_CQ // 256), (proj, BLOCK, PC_CKV // BLOCK)]
    qn, cn = _rowwise(_mla_norm_fn, norm_rows, [gq, gkv], [MLA_Q_LORA, MLA_KV_LORA], name=f"{tag}_mla_norm", tile=272,
                      out_dtypes=[BF16, BF16])
    qh = _mm(qn, W["mla_w_uq_p"][l], name=f"{tag}_mla_uq")
    kvh = _mm(cn, W["mla_w_ukv_p"][l], name=f"{tag}_mla_ukv")
    qr, kr = _rowwise(_rope_fn, [(qh, BLOCK, 2), (proj, BLOCK, PC_KR // BLOCK), cosf, sins], [], [BLOCK, BLOCK],
                      name=f"{tag}_rope", tile=272)
    mla_qkv = ((qh, 0), (kvh, 0), (kvh, 1))
    y_mla, lse_m = _attn_fwd(*mla_qkv, scale=(MLA_NOPE + MLA_ROPE) ** -0.5, name=f"{tag}_mla_attn",
                             rope=((qr, 0), (kr, 0)))
    sv.update(gq=gq, gkv=gkv, norm_rows=norm_rows, qn=qn, cn=cn, qr=qr, kr=kr, mla_qkv=mla_qkv, y_mla=y_mla, lse_m=lse_m)
    ycat = jnp.concatenate([y_ssd, y_fox, y_mla], axis=1).astype(BF16)
    mix, h2, h2b = _chunk_rows_mm_res_ln(ycat, W["w_out"][l], 256, h1, W["ln2_g"][l][None], W["ln2_b"][l][None], 1.0,
                                    name=f"{tag}_out_proj_ln2")
    sv.update(mix=mix, ycat=ycat)
    return (h2, h2b), sv


def _mixer_bwd(dh2, sv, W, l, cosf, sins, GB, zero=0.0):
    tag = f"l{l}"
    G = {}
    proj = sv["proj"]
    ln2g, ln2b = W["ln2_g"][l][None] + zero, W["ln2_b"][l][None]
    (dh1_a, dmix), (dln2g, dln2b) = _rowwise_bwd(
        _make_res_ln_fn(1.0), [sv["h1"], sv["mix"]], [ln2g, ln2b], [dh2], name=f"{tag}_ln2_bwd", tile=272,
        grad_dtypes=[F32, BF16])
    G["ln2_g"], G["ln2_b"] = dln2g[0], dln2b[0]
    dycat = _chunk_rows_dx(dmix, W["w_out"][l], l, 256, name=f"{tag}_out_proj_dx")
    GB["w_out"] = _chunk_rows_dw(sv["ycat"], dmix, 256, name=f"{tag}_out_proj_dw")
    (dy_raw, dxs_a, dz), (ddskip, dnormg) = _rowwise_bwd(
        _ssd_post_fn, sv["post_rows"], [sv["dskip"], sv["normg"]], [dycat[:, :SSD_D]],
        name=f"{tag}_ssd_post_bwd", tile=272, ncol=SSD_GROUPS)
    G["ssd_norm_g"] = dnormg[0]
    G["d_skip"] = ddskip.reshape(SSD_HEADS, SSD_HD).sum(axis=1)
    dxh, dbm, dcm, ddtc, ddtr, dal = _ssd_bwd(sv["xh"], sv["bm"], sv["cm"], sv["dtc"], sv["dtr"], sv["alog"],
                                              sv["prevs"], _heads(dy_raw, SSD_HEADS, SSD_HD), name=f"{tag}_ssd_bwd")
    G["a_log"] = dal[:, 0, 0]
    dxc = jnp.concatenate([dxs_a + _unheads(dxh), _unheads(dbm), _unheads(dcm)], axis=1)
    dxbc, G["conv_w"], dconv_b = _conv_bwd(proj, PC_XBC // BLOCK, sv["conv_w"], sv["conv_b"], dxc,
                                           name=f"{tag}_conv_bwd")
    G["conv_b"] = dconv_b[0]
    ddt_raw, ddt_bias = _ssd_dt_bwd(proj, PC_DT // BLOCK, sv["dt_bias"], ddtc, ddtr, name=f"{tag}_ssd_dt_bwd")
    G["dt_bias"] = ddt_bias[0, :SSD_HEADS]
    dfq, dfk, dfv, dcg, dcgt = _attn_bwd(*sv["fox_qkv"], sv["y_fox"], sv["lse_f"], (dycat, SSD_D // ATT_W),
                                         scale=FOX_HD ** -0.5, name=f"{tag}_fox_attn_bwd", bias=(sv["cg"], sv["cgt"]))
    df_raw, dfb = _fox_gate_bwd(proj, PC_FR // BLOCK, sv["f_b"], dcg, dcgt, name=f"{tag}_fox_gate_bwd")
    G["fox_f_b"] = dfb[0, :FOX_HEADS]
    dqn_h, dkn_h, dv_h, dqr, dkr = _attn_bwd(
        *sv["mla_qkv"], sv["y_mla"], sv["lse_m"], (dycat, (SSD_D + FOX_D) // ATT_W),
        scale=(MLA_NOPE + MLA_ROPE) ** -0.5, name=f"{tag}_mla_attn_bwd", rope=((sv["qr"], 0), (sv["kr"], 0)))
    dq_rope, dk_rope = _rowwise(_rope_t_fn, [dqr, dkr, cosf, sins], [], [BLOCK, BLOCK], name=f"{tag}_rope_bwd",
                                tile=272)
    dqh = jnp.concatenate([dqn_h, dq_rope], axis=1).astype(BF16)
    dkvh = jnp.concatenate([dkn_h, dv_h], axis=1).astype(BF16)
    dqn = _mm(dqh, W["mla_w_uq_p"][l], tb=True, name=f"{tag}_mla_uq_dx")
    G["mla_w_uq_p"] = _mm(sv["qn"], dqh, ta=True, name=f"{tag}_mla_uq_dw")
    dcn = _mm(dkvh, W["mla_w_ukv_p"][l], tb=True, name=f"{tag}_mla_ukv_dx")
    G["mla_w_ukv_p"] = _mm(sv["cn"], dkvh, ta=True, name=f"{tag}_mla_ukv_dw")
    (dcq, dckv), (dgq, dgkv) = _rowwise_bwd(_mla_norm_fn, sv["norm_rows"], [sv["gq"], sv["gkv"]], [dqn, dcn],
                                            name=f"{tag}_mla_norm_bwd", tile=272)
    G["mla_q_norm_g"], G["mla_kv_norm_g"] = dgq[0], dgkv[0]
    dproj = jnp.concatenate([dz, dxbc, dfq, dfk, dfv, dcq, dckv, ddt_raw, df_raw, dk_rope], axis=1).astype(BF16)
    dh1 = _mm(dproj, W["w_in_p"][l], tb=True, add=dh1_a, name=f"{tag}_in_proj_dx")
    G["w_in_p"] = _mm(sv["h1b"], dproj, ta=True, name=f"{tag}_in_proj_dw")
    return dh1, G


def _embed(x, meta):
    return jnp.concatenate([jnp.zeros((PAD_ROWS, D_MODEL), F32), meta, x], axis=0)


def _layer_fwd(h, W, l, cosf, sins):
    ln = lambda n: W[n][l][None]
    h1, s1 = _ffn_fwd(h, W, "ffn1", l, ln("ln1_g"), ln("ln1_b"), f"l{l}_ffn1")
    h2, sm = _mixer_fwd(h1, W, l, cosf, sins)
    h3, s2 = _ffn_fwd(h2, W, "ffn2", l, ln("ln3_g"), ln("ln3_b"), f"l{l}_ffn2")
    return h3, (s1, sm, s2)


def _layer_bwd(dh, saved, W, l, cosf, sins):
    ln = lambda n: W[n][l][None]
    s1, sm, s2 = saved
    G = {}
    dh, dg, db = _ffn_bwd(dh, s2, W, "ffn2", l, ln("ln3_g"), ln("ln3_b"), G, f"l{l}_ffn2")
    G["ln3_g"], G["ln3_b"] = dg[0], db[0]
    dh, Gm = _mixer_bwd(dh, sm, W, l, cosf, sins, G)
    G.update(Gm)
    dh, dg, db = _ffn_bwd(dh, s1, W, "ffn1", l, ln("ln1_g"), ln("ln1_b"), G, f"l{l}_ffn1")
    G["ln1_g"], G["ln1_b"] = dg[0], db[0]
    return dh, G


def _local_step(x, target, W):
    h = _embed(x, W["meta"])
    h = (h, h.astype(BF16))
    tgt = jnp.concatenate([jnp.zeros((BLOCK, D_MODEL), F32), target], axis=0)
    cosf, sins = _rope_tables()
    saved = []
    for l in range(DEPTH):
        h, sv = _layer_fwd(h, W, l, cosf, sins)
        saved.append(sv)
    dh, loss = _loss_head(h[0], tgt, name="loss_head")
    grads = [None] * DEPTH
    for l in reversed(range(DEPTH)):
        dh, grads[l] = _layer_bwd(dh, saved[l], W, l, cosf, sins)
    return loss, dh, grads


WEIGHTS = ['meta', 'ffn1_w_gate', 'ffn1_w_up', 'ffn1_w_down', 'ln1_g', 'ln1_b', 'w_in', 'conv_w', 'conv_b', 'dt_bias',
           'a_log', 'd_skip', 'ssd_norm_g', 'fox_f_b', 'mla_q_norm_g', 'mla_w_uq', 'mla_kv_norm_g', 'mla_w_ukv',
           'w_out', 'ln2_g', 'ln2_b', 'ffn2_w_gate', 'ffn2_w_up', 'ffn2_w_down', 'ln3_g', 'ln3_b']
SMALL = ["ln1_g", "ln1_b", "conv_b", "dt_bias", "a_log", "d_skip", "ssd_norm_g", "fox_f_b", "mla_q_norm_g",
         "mla_kv_norm_g", "ln2_g", "ln2_b", "ln3_g", "ln3_b"]
MATMUL_W = ["ffn1_w_gate", "ffn1_w_up", "ffn1_w_down", "w_in", "mla_w_uq", "mla_w_ukv", "w_out", "ffn2_w_gate",
            "ffn2_w_up", "ffn2_w_down"]
SMALL_ROWS = 312


def _pad_to(a, axis, size):
    pads = [(0, 0)] * a.ndim
    pads[axis] = (0, size - a.shape[axis])
    return jnp.pad(a, pads)


def _chip_cols(full, chip, width):
    return lax.dynamic_slice_in_dim(full, chip * width, width, axis=full.ndim - 1)


def kernel(x, meta, ffn1_w_gate, ffn1_w_up, ffn1_w_down, ln1_g, ln1_b, w_in, conv_w, conv_b, dt_bias, a_log, d_skip, ssd_norm_g, fox_f_b, mla_q_norm_g, mla_w_uq, mla_kv_norm_g, mla_w_ukv, w_out, ln2_g, ln2_b, ffn2_w_gate, ffn2_w_up, ffn2_w_down, ln3_g, ln3_b, loss_target, m_meta, m_ffn1_w_gate, m_ffn1_w_up, m_ffn1_w_down, m_ln1_g, m_ln1_b, m_w_in, m_conv_w, m_conv_b, m_dt_bias, m_a_log, m_d_skip, m_ssd_norm_g, m_fox_f_b, m_mla_q_norm_g, m_mla_w_uq, m_mla_kv_norm_g, m_mla_w_ukv, m_w_out, m_ln2_g, m_ln2_b, m_ffn2_w_gate, m_ffn2_w_up, m_ffn2_w_down, m_ln3_g, m_ln3_b, v_meta, v_ffn1_w_gate, v_ffn1_w_up, v_ffn1_w_down, v_ln1_g, v_ln1_b, v_w_in, v_conv_w, v_conv_b, v_dt_bias, v_a_log, v_d_skip, v_ssd_norm_g, v_fox_f_b, v_mla_q_norm_g, v_mla_w_uq, v_mla_kv_norm_g, v_mla_w_ukv, v_w_out, v_ln2_g, v_ln2_b, v_ffn2_w_gate, v_ffn2_w_up, v_ffn2_w_down, v_ln3_g, v_ln3_b):
    args = dict(locals())
    w = {n: args[n] for n in WEIGHTS}
    m = {n: args["m_" + n] for n in WEIGHTS}
    v = {n: args["v_" + n] for n in WEIGHTS}
    xcoord, ycoord, _ = _my_pos()
    chip = 2 * xcoord + ycoord

    tr = lambda a: jnp.swapaxes(a, 1, 2)

    def bf16_shard(n, l, zero=None):
        a = w[n] if zero is None else w[n] + zero
        if n.endswith("w_gate") or n.endswith("w_up"):
            a = _pad_to(tr(a), 1, HP)
        elif n.endswith("w_down"):
            a = _pad_to(a, 1, HP)
        elif n == "w_in":
            a = _pad_to(a, 2, IN_SHARD_P)
        return a[l].astype(BF16)

    land_shape = lambda s: jax.ShapeDtypeStruct((N_CHIPS,) + s.shape, s.dtype)

    def gather_start(names, l, tag, after):
        srcs = [bf16_shard(n, l, None if after is None else after[0, 0]) for n in names]
        return _ici_start(_gather_copies, srcs, [land_shape(s) for s in srcs], name=f"gather_ici_{tag}_start",
                          after=[tiny[0]] if after is None else [after])

    def gather_finish(handle, names, l, tag, after):
        srcs, lands = _ici_wait(_gather_copies, *handle[:4], after, name=f"gather_ici_{tag}_wait")
        use_gathered(l, names, _gather_d2d(srcs, lands, tag))

    def gather_d2d_start(handle, tag, after):
        srcs, lands = _ici_wait(_gather_copies, *handle[:4], after, name=f"gather_ici_{tag}_wait")
        return _ici_start(_d2d_copies, srcs, lands, name=f"gather_d2d_{tag}_start", sems_per_array=D2D_COPIES)

    def gather_d2d_finish(handle, names, l, tag, after):
        _, lands = _ici_wait(_d2d_copies, *handle[:4], after, name=f"gather_d2d_{tag}_wait")
        use_gathered(l, names, lands)

    tiny = _allgather_chips([w["meta"].reshape(2, N_META // 2, D_MODEL // N_CHIPS), w["conv_w"]])
    meta_full = jnp.concatenate([tiny[0][k].reshape(N_META, D_MODEL // N_CHIPS) for k in range(N_CHIPS)], axis=1)

    W = {n: [None] * DEPTH for n in MATMUL_W + ["w_in_p", "mla_w_uq_p", "mla_w_ukv_p"]}
    W["conv_w"] = jnp.concatenate([tiny[1][k] for k in range(N_CHIPS)], axis=-1)
    W["meta"] = meta_full
    for n in SMALL:
        W[n] = w[n]

    def use_gathered(l, names, lands):
        got = dict(zip(names, lands))
        cat = lambda n, cut=None: jnp.concatenate([got[n][k][..., :cut] for k in range(N_CHIPS)], axis=-1)
        for n in names:
            W[n][l] = got[n]
        if "w_in" in got:
            W["w_in_p"][l] = _pad_in_proj(cat("w_in", IN_SHARD))
            W["mla_w_uq_p"][l] = _regroup_uq(cat("mla_w_uq"))
            W["mla_w_ukv_p"][l] = _regroup_ukv(cat("mla_w_ukv"))

    def chunk_grads(G, names):
        def chunked(name, ungroup, width, pad):
            full = ungroup(G[name])
            return _pad_to(jnp.moveaxis(full.reshape(full.shape[0], N_CHIPS, width), 1, 0), 2, pad)
        special = {"mla_w_uq": ("mla_w_uq_p", _ungroup_uq, MLA_NOPE + MLA_ROPE, MLA_NOPE + MLA_ROPE),
                   "mla_w_ukv": ("mla_w_ukv_p", _ungroup_ukv, MLA_NOPE + MLA_V, MLA_NOPE + MLA_V)}
        return [_in_proj_grad_chunks(G["w_in_p"]) if n == "w_in" else chunked(*special[n]) if n in special else G[n]
                for n in names]

    def rs_start(G, names, tag):
        pairs = _rs_pair_sums(chunk_grads(G, names), names, tag)
        handle = _ici_start(_exchange_copies, [p[1] for p in pairs], _exchange_land_shapes(pairs),
                            name=f"rs_exchange_{tag}_start")
        return pairs, handle

    def swap_start(G, names, tag):
        gs = chunk_grads(G, names)
        return _ici_start(_swap_copies, gs, _swap_land_shapes(gs), name=f"rs_swap_{tag}_start")

    def exchange_start(swap_handle, names, tag, after):
        gs, r1 = _ici_wait(_swap_copies, *swap_handle[:4], after, name=f"rs_swap_{tag}_wait")
        pairs = _rs_add_pairs(gs, r1, names, tag)
        handle = _ici_start(_exchange_copies, [p[1] for p in pairs], _exchange_land_shapes(pairs),
                            name=f"rs_exchange_{tag}_start")
        return pairs, handle

    def rs_end(pairs, handle, names, tag, after):
        _, r2 = _ici_wait(_exchange_copies, *handle[:4], after, name=f"rs_exchange_{tag}_wait")
        return dict(zip(names, _rs_finish(pairs, r2, names, tag)))

    ffn1_w, mix_w, ffn2_w = MATMUL_W[:3], MATMUL_W[3:7], MATMUL_W[7:]
    g_a = gather_start(ffn1_w, 0, "l0_ffn1", None)
    g_b = gather_start(mix_w, 0, "l0_mix", g_a[4])
    g_c = gather_start(ffn2_w, 0, "l0_ffn2", g_b[4])
    g_l1 = gather_start(MATMUL_W, 1, "l1", g_c[4])
    token = g_l1[4]
    cosf, sins = _rope_tables()
    ln = lambda n, l: W[n][l][None]
    h = _embed(x[0] + token[0, 0], meta_full)
    h = (h, h.astype(BF16))
    gather_finish(g_a, ffn1_w, 0, "l0_ffn1", h[1])
    h1, s1 = _ffn_fwd(h, W, "ffn1", 0, ln("ln1_g", 0), ln("ln1_b", 0), "l0_ffn1")
    gather_finish(g_b, mix_w, 0, "l0_mix", h1[1])
    d_c = gather_d2d_start(g_c, "l0_ffn2", W["w_in_p"][0])
    h2, sm = _mixer_fwd(h1, W, 0, cosf, sins, after=[d_c[4]])
    gather_d2d_finish(d_c, ffn2_w, 0, "l0_ffn2", h2[1])
    d_l1 = gather_d2d_start(g_l1, "l1", W["ffn2_w_gate"][0])
    h, s2 = _ffn_fwd(h2, W, "ffn2", 0, ln("ln3_g", 0), ln("ln3_b", 0), "l0_ffn2", after=[d_l1[4]])
    saved0 = (s1, sm, s2)
    gather_d2d_finish(d_l1, MATMUL_W, 1, "l1", h[1])
    h, saved1 = _layer_fwd(h, W, 1, cosf, sins)
    tgt = jnp.concatenate([jnp.zeros((BLOCK, D_MODEL), F32), loss_target[0]], axis=0)
    dh, loss = _loss_head(h[0], tgt, name="loss_head")
    G = [None] * DEPTH
    dh, G[1] = _layer_bwd(dh, saved1, W, 1, cosf, sins)
    ffn2_w, mix_w, ffn1_w = MATMUL_W[7:], MATMUL_W[3:7], MATMUL_W[:3]
    sw_l1 = swap_start(G[1], MATMUL_W, "l1")
    G0 = {}
    dh, dg, db = _ffn_bwd(dh, s2, W, "ffn2", 0, ln("ln3_g", 0) + sw_l1[4][0, 0], ln("ln3_b", 0), G0, "l0_ffn2")
    G0["ln3_g"], G0["ln3_b"] = dg[0], db[0]
    pairs_l1, x_l1 = exchange_start(sw_l1, MATMUL_W, "l1", dh)
    sw_a = swap_start(G0, ffn2_w, "l0_ffn2")
    dh, Gm = _mixer_bwd(dh, sm, W, 0, cosf, sins, G0, zero=x_l1[4][0, 0] + sw_a[4][0, 0])
    G0.update(Gm)
    pairs_a, x_a = exchange_start(sw_a, ffn2_w, "l0_ffn2", dh)
    reduced1 = rs_end(pairs_l1, x_l1, MATMUL_W, "l1", dh)
    pairs_b, x_b = rs_start(G0, mix_w, "l0_mix")
    dh0, dg, db = _ffn_bwd(dh, s1, W, "ffn1", 0, ln("ln1_g", 0) + (x_a[4][0, 0] + x_b[4][0, 0]), ln("ln1_b", 0), G0,
                           "l0_ffn1")
    G0["ln1_g"], G0["ln1_b"] = dg[0], db[0]
    G[0] = G0
    reduced0 = rs_end(pairs_a, x_a, ffn2_w, "l0_ffn2", dh0)
    reduced0.update(rs_end(pairs_b, x_b, mix_w, "l0_mix", dh0))

    small_parts = [jnp.stack([G[l][n] for l in range(DEPTH)]).reshape(-1) for n in SMALL]
    small_parts += [jnp.stack([G[l]["conv_w"] for l in range(DEPTH)]).reshape(-1), dh0[PAD_ROWS:BLOCK].reshape(-1),
                    loss[0, :1]]
    sw_c = swap_start(G0, ffn1_w, "l0_ffn1")
    flat = jnp.concatenate(small_parts) + sw_c[4][0, 0]
    flat = jnp.pad(flat, (0, SMALL_ROWS * BLOCK - flat.shape[0]))
    red2d = _allreduce_small(flat.reshape(SMALL_ROWS, BLOCK))
    red = red2d.reshape(-1)
    pairs_c, x_c = exchange_start(sw_c, ffn1_w, "l0_ffn1", red2d)
    grads, off = {}, 0
    for n in SMALL:
        size = int(np.prod(w[n].shape))
        grads[n] = red[off:off + size].reshape(w[n].shape)
        off += size
    conv_full = red[off:off + DEPTH * SSD_CONV * 768].reshape(DEPTH, SSD_CONV, 768)
    off += DEPTH * SSD_CONV * 768
    dmeta_full = red[off:off + N_META * D_MODEL].reshape(N_META, D_MODEL)
    off += N_META * D_MODEL
    loss_out = red[off]
    grads["conv_w"] = _chip_cols(conv_full, chip, 768 // N_CHIPS)
    grads["meta"] = _chip_cols(dmeta_full, chip, D_MODEL // N_CHIPS)

    delta, new_m, new_v = {}, {}, {}

    def adamw_matmul_weights(names, after):
        done = []
        for n in names:
            gs = [reduced0[n], reduced1[n]]
            if n.endswith("w_gate") or n.endswith("w_up"):
                res = _adamw(tr(w[n]), gs, tr(m[n]), tr(v[n]), name=f"adamw_{n}", after=after)
                grads[n], delta[n], new_m[n], new_v[n] = [tr(r) for r in res]
            else:
                res = _adamw(w[n], gs, m[n], v[n], name=f"adamw_{n}", after=after)
                grads[n], delta[n], new_m[n], new_v[n] = res
            done.append(res[1])
        return done

    early_done = adamw_matmul_weights(ffn2_w + mix_w, [x_c[4]])
    rest = [n for n in WEIGHTS if n not in MATMUL_W]

    def pack_small(d):
        f = jnp.concatenate([d[n].reshape(-1) for n in rest])
        tot = -(-f.shape[0] // (8 * BLOCK)) * 8 * BLOCK
        return jnp.pad(f, (0, tot - f.shape[0])).reshape(-1, BLOCK)

    _, d2, m2, v2 = _adamw(pack_small(w), [pack_small(grads)], pack_small(m), pack_small(v), name="adamw_small",
                           after=[x_c[4]])
    reduced0.update(rs_end(pairs_c, x_c, ffn1_w, "l0_ffn1", [d2] + early_done))
    adamw_matmul_weights(ffn1_w, [])
    off = 0
    for n in rest:
        size = int(np.prod(w[n].shape))
        for dst, src in ((delta, d2), (new_m, m2), (new_v, v2)):
            dst[n] = src.reshape(-1)[off:off + size].reshape(w[n].shape)
        off += size

    grad_x = dh0[BLOCK:][None]
    return (loss_out, grad_x, *[grads[n] for n in WEIGHTS], *[delta[n] for n in WEIGHTS],
            *[new_m[n] for n in WEIGHTS], *[new_v[n] for n in WEIGHTS])
```

```python
import functools

import numpy as np
import jax
import jax.numpy as jnp
from jax import lax
from jax.experimental import pallas as pl
from jax.experimental.pallas import tpu as pltpu

F32 = jnp.float32
BF16 = jnp.bfloat16
MESH = pl.DeviceIdType.MESH

D_MODEL = 1024
SEQ = 2048
N_META = 16
BLOCK = 128
PAD_ROWS = 112
LP = PAD_ROWS + N_META + SEQ
N_CHUNK = LP // BLOCK
DEPTH = 2
D_FF = 2816
N_CHIPS = 4
FF_SHARD = D_FF // N_CHIPS
HP = 768
FP = N_CHIPS * HP
SSD_HEADS, SSD_HD, SSD_D, SSD_GROUPS, SSD_STATE, SSD_CONV = 8, 64, 512, 2, 64, 4
FOX_HEADS, FOX_HD, FOX_D = 4, 64, 256
MLA_HEADS, MLA_Q_LORA, MLA_KV_LORA, MLA_NOPE, MLA_ROPE, MLA_V, MLA_D = 4, 256, 128, 64, 32, 64, 256
ROPE_HALF = MLA_ROPE // 2
ROPE_THETA = 10000.0
N_IN = 2476
IN_SHARD = N_IN // N_CHIPS
IN_SHARD_P = 640
ALPHA = (2 * DEPTH) ** 0.25
EPS = 1e-5
ADAM_LR, ADAM_B1, ADAM_B2, ADAM_EPS, ADAM_WD, ADAM_STEP = 0.001, 0.9, 0.999, 1e-08, 0.01, 10
NEG = -1e30
TM = 544

VMEM_LIMIT_BYTES = 56 * 1024 * 1024

PC_Z, PC_XBC, PC_FQ, PC_FK, PC_FV, PC_CQ, PC_CKV, PC_DT, PC_FR, PC_KR, PC_END = (
    0, 512, 1280, 1536, 1792, 2048, 2304, 2432, 2560, 2688, 2816)
OC_Z, OC_XBC, OC_DT, OC_FQ, OC_FK, OC_FV, OC_FR, OC_CQ, OC_CKV, OC_KR = (
    0, 512, 1280, 1288, 1544, 1800, 2056, 2060, 2316, 2444)


def _cparams(sem=None):
    return pltpu.CompilerParams(dimension_semantics=sem, vmem_limit_bytes=VMEM_LIMIT_BYTES)


def _tile(n, cap, mult):
    best = None
    for t in range(mult, min(n, cap) + 1, mult):
        if n % t == 0:
            best = t
    return best if best is not None else n


def _bs(shape, fn):
    return pl.BlockSpec(shape, fn)


ANY = pl.BlockSpec(memory_space=pl.ANY)


def _dims(ca, cb):
    return (((ca,), (cb,)), ((), ()))


def _raw_bdot(a, b, ca, cb):
    return lax.dot_general(a.astype(BF16), b.astype(BF16), _dims(ca, cb), preferred_element_type=F32)


def _mm_core(a, b, *, a_spec, b_spec, o_spec, grid, out_shape, ca, cb, name, add=None, after=()):
    nk = grid[2]
    has_add = add is not None
    acc_shape = tuple(d for d in o_spec.block_shape if d is not None)

    def body(*refs):
        a_ref, b_ref = refs[0], refs[1]
        add_ref = refs[2] if has_add else None
        o_ref, acc_ref = refs[-2], refs[-1]
        k = pl.program_id(2)

        @pl.when(k == 0)
        def _():
            acc_ref[...] = jnp.zeros_like(acc_ref)

        acc_ref[...] += _raw_bdot(a_ref[...], b_ref[...], ca, cb)

        @pl.when(k == nk - 1)
        def _():
            r = acc_ref[...]
            if has_add:
                r = r + add_ref[...]
            o_ref[...] = r

    ins = [a, b] + ([add] if has_add else []) + list(after)
    in_specs = [a_spec, b_spec] + ([o_spec] if has_add else []) + [ANY] * len(after)
    return pl.pallas_call(
        body, name=name, grid=grid, in_specs=in_specs, out_specs=o_spec,
        out_shape=jax.ShapeDtypeStruct(out_shape, F32), scratch_shapes=[pltpu.VMEM(acc_shape, F32)],
        compiler_params=_cparams(("parallel", "parallel", "arbitrary")),
    )(*ins)


MM_VMEM_BUDGET = 40 * 1024 * 1024


def _divisors(n, mult):
    return [t for t in range(mult, n + 1, mult) if n % t == 0] or [n]


def _pick_tiles(M, N, K, a_bytes, b_bytes, ta, has_add):
    best = None
    for tm in _divisors(M, 128 if ta else 16):
        for tn in _divisors(N, 128):
            vmem = 2 * tm * K * a_bytes + 2 * K * tn * b_bytes + (3 + 2 * int(has_add)) * tm * tn * 4
            if vmem <= MM_VMEM_BUDGET:
                key = ((M // tm) * (N // tn), -tn)
                if best is None or key < best[0]:
                    best = (key, tm, tn)
    assert best is not None, (M, N, K)
    return best[1], best[2], K


def _mm(a, b, *, ta=False, tb=False, add=None, name, after=()):
    if ta:
        K, M = a.shape
    else:
        M, K = a.shape
    if tb:
        N, Kb = b.shape
    else:
        Kb, N = b.shape
    assert K == Kb, (a.shape, b.shape, ta, tb)
    tm, tn, tk = _pick_tiles(M, N, K, a.dtype.itemsize, b.dtype.itemsize, ta, add is not None)
    a_spec = _bs((tk, tm), lambda i, j, k: (k, i)) if ta else _bs((tm, tk), lambda i, j, k: (i, k))
    b_spec = _bs((tn, tk), lambda i, j, k: (j, k)) if tb else _bs((tk, tn), lambda i, j, k: (k, j))
    return _mm_core(a, b, a_spec=a_spec, b_spec=b_spec, o_spec=_bs((tm, tn), lambda i, j, k: (i, j)),
                    grid=(M // tm, N // tn, K // tk), out_shape=(M, N), ca=0 if ta else 1, cb=1 if tb else 0,
                    name=name, add=add, after=after)


def _row_entry(r, ncol):
    if isinstance(r, tuple):
        return r
    return r, r.shape[1] // ncol, 0


def _rowwise(fn, rows, pars, out_cols, *, name, tile, ncol=1, out_dtypes=None):
    rows = [_row_entry(r, ncol) for r in rows]
    L = rows[0][0].shape[0]
    nr, npar = len(rows), len(pars)
    in_specs = [_bs((tile, w), lambda g, i, o=o: (i, o + g)) for _, w, o in rows]
    in_specs += [_bs((p.shape[0], p.shape[1] // ncol), lambda g, i: (0, g)) for p in pars]
    out_specs = [_bs((tile, c // ncol), lambda g, i: (i, g)) for c in out_cols]

    def body(*refs):
        ins, outs = refs[:nr + npar], refs[nr + npar:]
        row0 = pl.program_id(1) * tile
        res = fn(row0, *[r[...] for r in ins])
        for o, v in zip(outs, res):
            o[...] = v.astype(o.dtype)

    return pl.pallas_call(
        body, name=name, grid=(ncol, L // tile), in_specs=in_specs, out_specs=out_specs,
        out_shape=[jax.ShapeDtypeStruct((L, c), d) for c, d in zip(out_cols, out_dtypes or [F32] * len(out_cols))],
        compiler_params=_cparams(("parallel", "parallel")),
    )(*[r[0] for r in rows], *pars)


def _rowwise_bwd(fn, rows, pars, douts, *, name, tile, ncol=1, row_grad=None, grad_dtypes=None):
    rows = [_row_entry(r, ncol) for r in rows]
    L = rows[0][0].shape[0]
    nr, npar, nd = len(rows), len(pars), len(douts)
    row_grad = [True] * nr if row_grad is None else row_grad
    in_specs = [_bs((tile, w), lambda g, i, o=o: (i, o + g)) for _, w, o in rows]
    in_specs += [_bs((p.shape[0], p.shape[1] // ncol), lambda g, i: (0, g)) for p in pars]
    in_specs += [_bs((tile, d.shape[1] // ncol), lambda g, i: (i, g)) for d in douts]
    g_widths = [w * ncol for (_, w, _), f in zip(rows, row_grad) if f]
    out_specs = [_bs((tile, w // ncol), lambda g, i: (i, g)) for w in g_widths]
    out_specs += [_bs((p.shape[0], p.shape[1] // ncol), lambda g, i: (0, g)) for p in pars]
    out_shape = [jax.ShapeDtypeStruct((L, w), d) for w, d in zip(g_widths, grad_dtypes or [F32] * len(g_widths))]
    out_shape += [jax.ShapeDtypeStruct(p.shape, F32) for p in pars]

    def body(*refs):
        ins = refs[:nr + npar]
        dos = refs[nr + npar:nr + npar + nd]
        outs = refs[nr + npar + nd:]
        i = pl.program_id(1)
        row0 = i * tile
        _, vjp = jax.vjp(lambda *a: tuple(fn(row0, *a)), *[r[...] for r in ins])
        grads = vjp(tuple(d[...].astype(F32) for d in dos))
        o = 0
        for j in range(nr):
            if row_grad[j]:
                outs[o][...] = grads[j].astype(outs[o].dtype)
                o += 1
        for j in range(npar):
            g, ref = grads[nr + j], outs[o + j]

            @pl.when(i == 0)
            def _(g=g, ref=ref):
                ref[...] = g

            @pl.when(i > 0)
            def _(g=g, ref=ref):
                ref[...] += g

    res = pl.pallas_call(
        body, name=name, grid=(ncol, L // tile), in_specs=in_specs, out_specs=out_specs, out_shape=out_shape,
        compiler_params=_cparams(("parallel", "arbitrary")),
    )(*[r[0] for r in rows], *pars, *douts)
    return res[:len(g_widths)], res[len(g_widths):]


def _sigmoid(x):
    return 1.0 / (1.0 + jnp.exp(-x))


def _softplus(x):
    return jnp.maximum(x, 0.0) + jnp.log(1.0 + jnp.exp(-jnp.abs(x)))


def _silu(x):
    return x * _sigmoid(x)


def _make_res_ln_fn(scale):
    def fn(row0, h, o, gam, bet):
        pre = ALPHA * h + scale * o
        mu = jnp.mean(pre, axis=-1, keepdims=True)
        xc = pre - mu
        var = jnp.mean(xc * xc, axis=-1, keepdims=True)
        return (xc * lax.rsqrt(var + EPS) * gam + bet,)
    return fn


def _ssd_post_fn(row0, y, xs, z, dskip, normg):
    v = (y + dskip * xs) * _silu(z)
    v = v * lax.rsqrt(jnp.mean(v * v, axis=-1, keepdims=True) + EPS)
    return (v * normg,)


def _mla_norm_fn(row0, cq, ckv, gq, gkv):
    qn = cq * lax.rsqrt(jnp.mean(cq * cq, axis=-1, keepdims=True) + EPS) * gq
    cn = ckv * lax.rsqrt(jnp.mean(ckv * ckv, axis=-1, keepdims=True) + EPS) * gkv
    return qn, cn


def _rope_fn(row0, q, k, cosf, sins):
    return (q * cosf + pltpu.roll(q, 64, 1) * sins, k * cosf + pltpu.roll(k, 64, 1) * sins)


def _rope_t_fn(row0, gq, gk, cosf, sins):
    return (gq * cosf + pltpu.roll(gq * sins, 64, 1), gk * cosf + pltpu.roll(gk * sins, 64, 1))


def _conv_fwd(x, x_off, w, b, *, name):
    C = w.shape[1]

    def body(x_ref, w_ref, b_ref, o_ref):
        rows = lax.broadcasted_iota(jnp.int32, (LP, BLOCK), 0)
        xv = jnp.where(rows >= PAD_ROWS, x_ref[...], 0.0)
        acc = b_ref[...] + w_ref[3:4, :] * xv
        for k in range(SSD_CONV - 1):
            acc = acc + w_ref[k:k + 1, :] * pltpu.roll(xv, SSD_CONV - 1 - k, 0)
        o_ref[...] = _silu(acc)

    return pl.pallas_call(
        body, name=name, grid=(C // BLOCK,),
        in_specs=[_bs((LP, BLOCK), lambda j: (0, j + x_off)), _bs((SSD_CONV, BLOCK), lambda j: (0, j)),
                  _bs((1, BLOCK), lambda j: (0, j))],
        out_specs=_bs((LP, BLOCK), lambda j: (0, j)),
        out_shape=jax.ShapeDtypeStruct((LP, C), F32), compiler_params=_cparams(("parallel",)),
    )(x, w, b)


def _conv_bwd(x, x_off, w, b, dout, *, name):
    C = w.shape[1]

    def body(x_ref, w_ref, b_ref, do_ref, dx_ref, dw_ref, db_ref):
        rows = lax.broadcasted_iota(jnp.int32, (LP, BLOCK), 0)
        real = rows >= PAD_ROWS
        xv = jnp.where(real, x_ref[...], 0.0)
        shifted = [pltpu.roll(xv, SSD_CONV - 1 - k, 0) for k in range(SSD_CONV - 1)] + [xv]
        acc = b_ref[...]
        for k in range(SSD_CONV):
            acc = acc + w_ref[k:k + 1, :] * shifted[k]
        sig = _sigmoid(acc)
        dacc = jnp.where(real, do_ref[...] * (sig * (1.0 + acc * (1.0 - sig))), 0.0)
        db_ref[...] = jnp.sum(dacc, axis=0, keepdims=True)
        dx = w_ref[3:4, :] * dacc
        for k in range(SSD_CONV):
            dw_ref[k:k + 1, :] = jnp.sum(dacc * shifted[k], axis=0, keepdims=True)
            if k < SSD_CONV - 1:
                dx = dx + w_ref[k:k + 1, :] * pltpu.roll(dacc, LP - (SSD_CONV - 1 - k), 0)
        dx_ref[...] = jnp.where(real, dx, 0.0)

    return pl.pallas_call(
        body, name=name, grid=(C // BLOCK,),
        in_specs=[_bs((LP, BLOCK), lambda j: (0, j + x_off)), _bs((SSD_CONV, BLOCK), lambda j: (0, j)),
                  _bs((1, BLOCK), lambda j: (0, j)), _bs((LP, BLOCK), lambda j: (0, j))],
        out_specs=[_bs((LP, BLOCK), lambda j: (0, j)), _bs((SSD_CONV, BLOCK), lambda j: (0, j)),
                   _bs((1, BLOCK), lambda j: (0, j))],
        out_shape=[jax.ShapeDtypeStruct((LP, C), F32), jax.ShapeDtypeStruct((SSD_CONV, C), F32),
                   jax.ShapeDtypeStruct((1, C), F32)],
        compiler_params=_cparams(("parallel",)),
    )(x, w, b, dout)


_BDIMS = {"nn": (((2,), (1,)), ((0,), (0,))), "nt": (((2,), (2,)), ((0,), (0,))), "tn": (((1,), (1,)), ((0,), (0,)))}


def _raw_bdot3(a, b, mode):
    return lax.dot_general(a.astype(BF16), b.astype(BF16), _BDIMS[mode], preferred_element_type=F32)


@functools.partial(jax.custom_vjp, nondiff_argnums=(2,))
def _bdot3(a, b, mode):
    return _raw_bdot3(a, b, mode)


def _bdot3_fwd(a, b, mode):
    return _raw_bdot3(a, b, mode), (a, b)


def _bdot3_bwd(mode, res, g):
    a, b = res
    if mode == "nn":
        return _raw_bdot3(g, b, "nt"), _raw_bdot3(a, g, "tn")
    if mode == "nt":
        return _raw_bdot3(g, b, "nn"), _raw_bdot3(g, a, "tn")
    return _raw_bdot3(b, g, "nt"), _raw_bdot3(a, g, "nn")


_bdot3.defvjp(_bdot3_fwd, _bdot3_bwd)


def _ssd_chunk(x, bm, cm, dt, dtt, alog, prev):
    rep = SSD_HEADS // SSD_GROUPS
    per_head = lambda t: jnp.broadcast_to(t[:, None], (SSD_GROUPS, rep) + t.shape[1:]).reshape((SSD_HEADS,) + t.shape[1:])
    bm, cm = per_head(bm), per_head(cm)
    lane_h = lax.broadcasted_iota(jnp.int32, (1, BLOCK), 1)
    row_h = lax.broadcasted_iota(jnp.int32, (BLOCK, 1), 0)
    dtc = jnp.stack([jnp.sum(jnp.where(lane_h == h, dt, 0.0), axis=1, keepdims=True) for h in range(SSD_HEADS)])
    dtr = jnp.stack([jnp.sum(jnp.where(row_h == h, dtt, 0.0), axis=0, keepdims=True) for h in range(SSD_HEADS)])
    lane = lax.broadcasted_iota(jnp.int32, alog.shape, 2)
    a_neg = -jnp.exp(jnp.sum(jnp.where(lane == 0, alog, 0.0), axis=2, keepdims=True))
    ac_in = dtc * a_neg
    ar_in = dtr * a_neg
    li = lax.broadcasted_iota(jnp.int32, (1, BLOCK, BLOCK), 1)
    si = lax.broadcasted_iota(jnp.int32, (1, BLOCK, BLOCK), 2)
    causal = li >= si
    acum_c = jnp.sum(jnp.where(causal, ar_in, 0.0), axis=2, keepdims=True)
    acum_r = jnp.sum(jnp.where(li <= si, ac_in, 0.0), axis=1, keepdims=True)
    total = jnp.sum(ar_in, axis=2, keepdims=True)
    seg = jnp.exp(jnp.where(causal, acum_c - acum_r, NEG))
    xdt = x * dtc
    cb = _bdot3(cm, bm, "nt")
    y = _bdot3(cb * seg, xdt, "nn") + _bdot3(cm, prev, "nt") * jnp.exp(acum_c)
    st = _bdot3(xdt, bm * jnp.exp(total - acum_c), "tn")
    return y, prev * jnp.exp(total) + st


def _ssd_dt_fwd(raw, raw_blk, bias, *, name):
    def body(raw_ref, b_ref, dt_ref, dtt_ref):
        rows = lax.broadcasted_iota(jnp.int32, (LP, BLOCK), 0)
        dt = jnp.where(rows >= PAD_ROWS, _softplus(raw_ref[...] + b_ref[...]), 0.0)
        dt_ref[...] = dt
        dtt_ref[...] = dt.T

    return pl.pallas_call(
        body, name=name, grid=(1,),
        in_specs=[_bs((LP, BLOCK), lambda j: (0, raw_blk)), _bs((1, BLOCK), lambda j: (0, 0))],
        out_specs=[_bs((LP, BLOCK), lambda j: (0, 0)), _bs((BLOCK, LP), lambda j: (0, 0))],
        out_shape=[jax.ShapeDtypeStruct((LP, BLOCK), F32), jax.ShapeDtypeStruct((BLOCK, LP), F32)],
        compiler_params=_cparams(("arbitrary",)),
    )(raw, bias)


def _ssd_dt_bwd(raw, raw_blk, bias, ddt, ddtt, *, name):
    def body(raw_ref, b_ref, ddt_ref, ddtt_ref, draw_ref, db_ref):
        rows = lax.broadcasted_iota(jnp.int32, (LP, BLOCK), 0)
        g = ddt_ref[...] + ddtt_ref[...].T
        draw = jnp.where(rows >= PAD_ROWS, g * _sigmoid(raw_ref[...] + b_ref[...]), 0.0)
        draw_ref[...] = draw
        db_ref[...] = jnp.sum(draw, axis=0, keepdims=True)

    return pl.pallas_call(
        body, name=name, grid=(1,),
        in_specs=[_bs((LP, BLOCK), lambda j: (0, raw_blk)), _bs((1, BLOCK), lambda j: (0, 0)),
                  _bs((LP, BLOCK), lambda j: (0, 0)), _bs((BLOCK, LP), lambda j: (0, 0))],
        out_specs=[_bs((LP, BLOCK), lambda j: (0, 0)), _bs((1, BLOCK), lambda j: (0, 0))],
        out_shape=[jax.ShapeDtypeStruct((LP, BLOCK), F32), jax.ShapeDtypeStruct((1, BLOCK), F32)],
        compiler_params=_cparams(("arbitrary",)),
    )(raw, bias, ddt, ddtt)


def _ssd_specs(rev):
    ci = (lambda c: N_CHUNK - 1 - c) if rev else (lambda c: c)
    x_spec = _bs((SSD_HEADS, BLOCK, SSD_HD), lambda c: (0, ci(c), 0))
    g_spec = _bs((SSD_GROUPS, BLOCK, SSD_STATE), lambda c: (0, ci(c), 0))
    dtc_spec = _bs((BLOCK, BLOCK), lambda c: (ci(c), 0))
    dtr_spec = _bs((BLOCK, BLOCK), lambda c: (0, ci(c)))
    al_spec = _bs((SSD_HEADS, 1, BLOCK), lambda c: (0, 0, 0))
    st_spec = _bs((None, SSD_HEADS, SSD_HD, SSD_STATE), lambda c: (ci(c), 0, 0, 0))
    return x_spec, g_spec, dtc_spec, dtr_spec, al_spec, st_spec


def _ssd_fwd(x, bm, cm, dtc, dtr, alog, *, name):
    x_spec, g_spec, dtc_spec, dtr_spec, al_spec, st_spec = _ssd_specs(False)

    def body(x_ref, b_ref, c_ref, dtc_ref, dtr_ref, al_ref, y_ref, prev_ref, state):
        @pl.when(pl.program_id(0) == 0)
        def _():
            state[...] = jnp.zeros_like(state)

        prev = state[...]
        prev_ref[...] = prev
        y, new = _ssd_chunk(x_ref[...], b_ref[...], c_ref[...], dtc_ref[...], dtr_ref[...], al_ref[...], prev)
        y_ref[...] = y
        state[...] = new

    return pl.pallas_call(
        body, name=name, grid=(N_CHUNK,),
        in_specs=[x_spec, g_spec, g_spec, dtc_spec, dtr_spec, al_spec], out_specs=[x_spec, st_spec],
        out_shape=[jax.ShapeDtypeStruct((SSD_HEADS, LP, SSD_HD), F32),
                   jax.ShapeDtypeStruct((N_CHUNK, SSD_HEADS, SSD_HD, SSD_STATE), F32)],
        scratch_shapes=[pltpu.VMEM((SSD_HEADS, SSD_HD, SSD_STATE), F32)],
        compiler_params=_cparams(("arbitrary",)),
    )(x, bm, cm, dtc, dtr, alog)


def _ssd_bwd(x, bm, cm, dtc, dtr, alog, prevs, dy, *, name):
    x_spec, g_spec, dtc_spec, dtr_spec, al_spec, st_spec = _ssd_specs(True)

    def body(x_ref, b_ref, c_ref, dtc_ref, dtr_ref, al_ref, prev_ref, dy_ref,
             dx_ref, db_ref, dc_ref, ddtc_ref, ddtr_ref, dal_ref, dstate):
        c = pl.program_id(0)

        @pl.when(c == 0)
        def _():
            dstate[...] = jnp.zeros_like(dstate)

        _, vjp = jax.vjp(_ssd_chunk, x_ref[...], b_ref[...], c_ref[...], dtc_ref[...], dtr_ref[...], al_ref[...],
                         prev_ref[...])
        dx, db, dc, ddtc, ddtr, dal, dprev = vjp((dy_ref[...], dstate[...]))
        dx_ref[...] = dx
        db_ref[...] = db
        dc_ref[...] = dc
        ddtc_ref[...] = ddtc
        ddtr_ref[...] = ddtr
        dstate[...] = dprev

        @pl.when(c == 0)
        def _():
            dal_ref[...] = dal

        @pl.when(c > 0)
        def _():
            dal_ref[...] += dal

    hs = jax.ShapeDtypeStruct((SSD_HEADS, LP, SSD_HD), F32)
    gs = jax.ShapeDtypeStruct((SSD_GROUPS, LP, SSD_STATE), F32)
    return pl.pallas_call(
        body, name=name, grid=(N_CHUNK,),
        in_specs=[x_spec, g_spec, g_spec, dtc_spec, dtr_spec, al_spec, st_spec, x_spec],
        out_specs=[x_spec, g_spec, g_spec, dtc_spec, dtr_spec, al_spec],
        out_shape=[hs, gs, gs, jax.ShapeDtypeStruct((LP, BLOCK), F32),
                   jax.ShapeDtypeStruct((BLOCK, LP), F32), jax.ShapeDtypeStruct((SSD_HEADS, 1, BLOCK), F32)],
        scratch_shapes=[pltpu.VMEM((SSD_HEADS, SSD_HD, SSD_STATE), F32)],
        compiler_params=_cparams(("arbitrary",)),
    )(x, bm, cm, dtc, dtr, alog, prevs, dy)


def _tri_dot(tri, v):
    hi = v.astype(BF16)
    r1 = v - hi.astype(F32)
    mid = r1.astype(BF16)
    lo = (r1 - mid.astype(F32)).astype(BF16)
    t = tri.astype(BF16)
    d = lambda p: lax.dot_general(t, p, _dims(1, 0), preferred_element_type=F32)
    return d(hi) + d(mid) + d(lo)


def _fox_gate_fwd(raw, raw_blk, bias, *, name):
    def body(raw_ref, b_ref, c_ref, ct_ref):
        li = lax.broadcasted_iota(jnp.int32, (BLOCK, BLOCK), 0)
        si = lax.broadcasted_iota(jnp.int32, (BLOCK, BLOCK), 1)
        tri = jnp.where(li >= si, 1.0, 0.0)
        carry = jnp.zeros((1, BLOCK), F32)
        for j in range(N_CHUNK):
            r = slice(j * BLOCK, (j + 1) * BLOCK)
            lf = jnp.where(j * BLOCK + li >= PAD_ROWS, -_softplus(-(raw_ref[r, :] + b_ref[...])), 0.0)
            cv = _tri_dot(tri, lf) + carry
            c_ref[r, :] = cv
            ct_ref[:, r] = cv.T
            carry = carry + jnp.sum(lf, axis=0, keepdims=True)

    return pl.pallas_call(
        body, name=name, grid=(1,),
        in_specs=[_bs((LP, BLOCK), lambda j: (0, raw_blk)), _bs((1, BLOCK), lambda j: (0, 0))],
        out_specs=[_bs((LP, BLOCK), lambda j: (0, 0)), _bs((BLOCK, LP), lambda j: (0, 0))],
        out_shape=[jax.ShapeDtypeStruct((LP, BLOCK), F32), jax.ShapeDtypeStruct((BLOCK, LP), F32)],
        compiler_params=_cparams(("arbitrary",)),
    )(raw, bias)


def _fox_gate_bwd(raw, raw_blk, bias, dc, dct, *, name):
    def body(raw_ref, b_ref, dc_ref, dct_ref, draw_ref, db_ref):
        li = lax.broadcasted_iota(jnp.int32, (BLOCK, BLOCK), 0)
        si = lax.broadcasted_iota(jnp.int32, (BLOCK, BLOCK), 1)
        tri_t = jnp.where(li <= si, 1.0, 0.0)
        carry = jnp.zeros((1, BLOCK), F32)
        dsum = jnp.zeros((1, BLOCK), F32)
        for j in reversed(range(N_CHUNK)):
            r = slice(j * BLOCK, (j + 1) * BLOCK)
            dcv = dc_ref[r, :] + dct_ref[:, r].T
            dlf = _tri_dot(tri_t, dcv) + carry
            carry = carry + jnp.sum(dcv, axis=0, keepdims=True)
            draw = jnp.where(j * BLOCK + li >= PAD_ROWS, dlf * (1.0 - _sigmoid(raw_ref[r, :] + b_ref[...])), 0.0)
            draw_ref[r, :] = draw
            dsum = dsum + jnp.sum(draw, axis=0, keepdims=True)
        db_ref[...] = dsum

    return pl.pallas_call(
        body, name=name, grid=(1,),
        in_specs=[_bs((LP, BLOCK), lambda j: (0, raw_blk)), _bs((1, BLOCK), lambda j: (0, 0)),
                  _bs((LP, BLOCK), lambda j: (0, 0)), _bs((BLOCK, LP), lambda j: (0, 0))],
        out_specs=[_bs((LP, BLOCK), lambda j: (0, 0)), _bs((1, BLOCK), lambda j: (0, 0))],
        out_shape=[jax.ShapeDtypeStruct((LP, BLOCK), F32), jax.ShapeDtypeStruct((1, BLOCK), F32)],
        compiler_params=_cparams(("arbitrary",)),
    )(raw, bias, dc, dct)


ATT_W = 256
ATT_QB = 272
ATT_STEPS = LP // ATT_QB
ATT_KEYS = (640, 1152, 1664, LP)
ATT_BLOCKS_PER_CLASS = ATT_STEPS // len(ATT_KEYS)


def _lane_head(width, per, mod=None):
    lane = lax.broadcasted_iota(jnp.int32, (1, width), 1)
    if mod is not None:
        lane = lane % mod
    return lane // per


def _attn_mask(i, kw):
    r = i * ATT_QB + lax.broadcasted_iota(jnp.int32, (ATT_QB, kw), 0)
    c = lax.broadcasted_iota(jnp.int32, (ATT_QB, kw), 1)
    return (c <= r) & ((c >= PAD_ROWS) | (r < PAD_ROWS))


def _attn_by_key_class(i, fn):
    for p, kw in enumerate(ATT_KEYS):
        @pl.when(i // ATT_BLOCKS_PER_CLASS == p)
        def _(kw=kw):
            fn(kw)


def _attn_specs(q, k, v, bias, rope):
    qspec = lambda blk, w=ATT_W: _bs((ATT_QB, w), lambda i: (i, blk))
    fspec = lambda blk, w=ATT_W: _bs((LP, w), lambda i: (0, blk))
    ins = [q[0], k[0], v[0]]
    specs = [qspec(q[1]), fspec(k[1]), fspec(v[1])]
    if bias is not None:
        ins += [bias[0], bias[1]]
        specs += [qspec(0, BLOCK), _bs((BLOCK, LP), lambda i: (0, 0))]
    if rope is not None:
        ins += [rope[0][0], rope[1][0]]
        specs += [qspec(rope[0][1], BLOCK), fspec(rope[1][1], BLOCK)]
    return ins, specs, qspec, fspec


def _attn_fwd(q, k, v, *, scale, name, bias=None, rope=None):
    ins, specs, qspec, fspec = _attn_specs(q, k, v, bias, rope)
    has_bias, has_rope = bias is not None, rope is not None

    def body(*refs):
        it = iter(refs)
        q_ref, k_ref, v_ref = next(it), next(it), next(it)
        if has_bias:
            c_ref, ct_ref = next(it), next(it)
        if has_rope:
            qr_ref, kr_ref = next(it), next(it)
        o_ref, lse_ref = next(it), next(it)
        i = pl.program_id(0)

        def block(kw):
            ok = _attn_mask(i, kw)
            qv, kv, vv = q_ref[...].astype(BF16), k_ref[0:kw, :].astype(BF16), v_ref[0:kw, :].astype(BF16)
            hid, l128 = _lane_head(ATT_W, FOX_HD), _lane_head(BLOCK, 1)
            if has_rope:
                rid = _lane_head(BLOCK, ROPE_HALF, 64)
                qrv, krv = qr_ref[...].astype(BF16), kr_ref[0:kw, :].astype(BF16)
            def head(h, carry):
                o_acc, lse_acc = carry
                s = _raw_bdot(jnp.where(hid == h, qv, 0.0), kv, 1, 1)
                if has_rope:
                    s = s + _raw_bdot(jnp.where(rid == h, qrv, 0.0), krv, 1, 1)
                s = s * scale
                if has_bias:
                    cq = jnp.sum(jnp.where(l128 == h, c_ref[...], 0.0), axis=1, keepdims=True)
                    s = s + (cq - ct_ref[pl.ds(h, 1), 0:kw])
                s = jnp.where(ok, s, NEG)
                m = jnp.max(s, axis=1, keepdims=True)
                p = jnp.exp(s - m)
                l = jnp.sum(p, axis=1, keepdims=True)
                o_acc = jnp.where(hid == h, _raw_bdot(p, vv, 1, 0) / l, o_acc)
                lse_acc = jnp.where(l128 == h, m + jnp.log(l), lse_acc)
                return o_acc, lse_acc

            o_acc, lse_acc = lax.fori_loop(
                0, FOX_HEADS, head, (jnp.zeros((ATT_QB, ATT_W), F32), jnp.zeros((ATT_QB, BLOCK), F32)), unroll=True)
            o_ref[...] = o_acc
            lse_ref[...] = lse_acc

        _attn_by_key_class(i, block)

    return pl.pallas_call(
        body, name=name, grid=(ATT_STEPS,), in_specs=specs, out_specs=[qspec(0), qspec(0, BLOCK)],
        out_shape=[jax.ShapeDtypeStruct((LP, ATT_W), F32), jax.ShapeDtypeStruct((LP, BLOCK), F32)],
        compiler_params=_cparams(("parallel",)),
    )(*ins)


def _attn_bwd(q, k, v, o, lse, do, *, scale, name, bias=None, rope=None):
    ins, specs, qspec, fspec = _attn_specs(q, k, v, bias, rope)
    has_bias, has_rope = bias is not None, rope is not None
    ins += [o, lse, do[0]]
    specs += [qspec(0), qspec(0, BLOCK), qspec(do[1])]

    def body(*refs):
        it = iter(refs)
        q_ref, k_ref, v_ref = next(it), next(it), next(it)
        if has_bias:
            c_ref, ct_ref = next(it), next(it)
        if has_rope:
            qr_ref, kr_ref = next(it), next(it)
        o_ref, lse_ref, do_ref = next(it), next(it), next(it)
        dq_ref, dk_ref, dv_ref = next(it), next(it), next(it)
        if has_bias:
            dc_ref, dct_ref = next(it), next(it)
        if has_rope:
            dqr_ref, dkr_ref = next(it), next(it)
        i = pl.program_id(0)

        @pl.when(i == 0)
        def _():
            dk_ref[...] = jnp.zeros_like(dk_ref)
            dv_ref[...] = jnp.zeros_like(dv_ref)
            if has_rope:
                dkr_ref[...] = jnp.zeros_like(dkr_ref)
            if has_bias:
                dct_ref[...] = jnp.zeros_like(dct_ref)

        def block(kw):
            ok = _attn_mask(i, kw)
            qv, kv, vv = q_ref[...].astype(BF16), k_ref[0:kw, :].astype(BF16), v_ref[0:kw, :].astype(BF16)
            dov, lsev = do_ref[...], lse_ref[...]
            dov_ov = dov * o_ref[...]
            dov = dov.astype(BF16)
            hid, l128 = _lane_head(ATT_W, FOX_HD), _lane_head(BLOCK, 1)
            if has_rope:
                rid = _lane_head(BLOCK, ROPE_HALF, 64)
                qrv, krv = qr_ref[...].astype(BF16), kr_ref[0:kw, :].astype(BF16)

            def head(h, carry):
                dq_acc, aux_acc = carry
                qm = jnp.where(hid == h, qv, 0.0)
                s = _raw_bdot(qm, kv, 1, 1)
                if has_rope:
                    qrm = jnp.where(rid == h, qrv, 0.0)
                    s = s + _raw_bdot(qrm, krv, 1, 1)
                s = s * scale
                if has_bias:
                    cq = jnp.sum(jnp.where(l128 == h, c_ref[...], 0.0), axis=1, keepdims=True)
                    s = s + (cq - ct_ref[pl.ds(h, 1), 0:kw])
                s = jnp.where(ok, s, NEG)
                p = jnp.exp(s - jnp.sum(jnp.where(l128 == h, lsev, 0.0), axis=1, keepdims=True))
                dom = jnp.where(hid == h, dov, 0.0)
                dp = _raw_bdot(dom, vv, 1, 1)
                delta = jnp.sum(jnp.where(hid == h, dov_ov, 0.0), axis=1, keepdims=True)
                ds = p * (dp - delta)
                dsb, pb = ds.astype(BF16), p.astype(BF16)
                dq_acc = jnp.where(hid == h, _raw_bdot(dsb, kv, 1, 0) * scale, dq_acc)
                dk_ref[0:kw, :] += _raw_bdot(dsb, qm, 0, 0) * scale
                dv_ref[0:kw, :] += _raw_bdot(pb, dom, 0, 0)
                if has_rope:
                    aux_acc = jnp.where(rid == h, _raw_bdot(dsb, krv, 1, 0) * scale, aux_acc)
                    dkr_ref[0:kw, :] += _raw_bdot(dsb, qrm, 0, 0) * scale
                if has_bias:
                    aux_acc = jnp.where(l128 == h, jnp.sum(ds, axis=1, keepdims=True), aux_acc)
                    dct_ref[pl.ds(h, 1), 0:kw] -= jnp.sum(ds, axis=0, keepdims=True)
                return dq_acc, aux_acc

            dq_acc, aux_acc = lax.fori_loop(
                0, FOX_HEADS, head, (jnp.zeros((ATT_QB, ATT_W), F32), jnp.zeros((ATT_QB, BLOCK), F32)))
            dq_ref[...] = dq_acc
            if has_bias:
                dc_ref[...] = aux_acc
            if has_rope:
                dqr_ref[...] = aux_acc

        _attn_by_key_class(i, block)

    wide = jax.ShapeDtypeStruct((LP, ATT_W), F32)
    narrow = jax.ShapeDtypeStruct((LP, BLOCK), F32)
    out_specs = [qspec(0), fspec(0), fspec(0)]
    out_shape = [wide, wide, wide]
    if has_bias:
        out_specs += [qspec(0, BLOCK), _bs((BLOCK, LP), lambda i: (0, 0))]
        out_shape += [narrow, jax.ShapeDtypeStruct((BLOCK, LP), F32)]
    if has_rope:
        out_specs += [qspec(0, BLOCK), fspec(0, BLOCK)]
        out_shape += [narrow, narrow]
    return pl.pallas_call(
        body, name=name, grid=(ATT_STEPS,), in_specs=specs, out_specs=out_specs, out_shape=out_shape,
        compiler_params=_cparams(("arbitrary",)),
    )(*ins)


def _loss_head(y, target, *, name):
    tile = TM

    def body(y_ref, t_ref, dy_ref, loss_ref):
        i = pl.program_id(0)
        rows = i * tile + lax.broadcasted_iota(jnp.int32, (tile, D_MODEL), 0)
        err = jnp.where(rows >= BLOCK, y_ref[...] - t_ref[...], 0.0)
        dy_ref[...] = err * (1.0 / D_MODEL)
        part = 0.5 * jnp.sum(jnp.sum(err * err, axis=1, keepdims=True) * (1.0 / D_MODEL), axis=0, keepdims=True)
        part = jnp.broadcast_to(part, (1, BLOCK))

        @pl.when(i == 0)
        def _():
            loss_ref[...] = part

        @pl.when(i > 0)
        def _():
            loss_ref[...] += part

    return pl.pallas_call(
        body, name=name, grid=(LP // tile,),
        in_specs=[_bs((tile, D_MODEL), lambda i: (i, 0)), _bs((tile, D_MODEL), lambda i: (i, 0))],
        out_specs=[_bs((tile, D_MODEL), lambda i: (i, 0)), _bs((1, BLOCK), lambda i: (0, 0))],
        out_shape=[jax.ShapeDtypeStruct((LP, D_MODEL), F32), jax.ShapeDtypeStruct((1, BLOCK), F32)],
        compiler_params=_cparams(("arbitrary",)),
    )(y, target)


def _adamw(w, gs, m, v, *, name, after=()):
    if w.ndim == 2:
        w, m, v = w[None], m[None], v[None]
        squeeze = True
    else:
        squeeze = False
    NL, R, C = w.shape
    assert len(gs) == NL
    CG = gs[0].shape[1]
    tile = _tile(R, 256, 8)

    def body(*refs):
        w_ref, g_refs = refs[0], refs[1:1 + NL]
        m_ref, v_ref = refs[1 + NL:3 + NL]
        go_ref, d_ref, nm_ref, nv_ref = refs[3 + NL + len(after):]
        gv = g_refs[0][:, :C]
        for j in range(1, NL):
            gv = jnp.where(pl.program_id(0) == j, g_refs[j][:, :C], gv)
        nm = ADAM_B1 * m_ref[...] + (1.0 - ADAM_B1) * gv
        nv = ADAM_B2 * v_ref[...] + (1.0 - ADAM_B2) * (gv * gv)
        m_hat = nm / (1.0 - ADAM_B1 ** ADAM_STEP)
        v_hat = nv / (1.0 - ADAM_B2 ** ADAM_STEP)
        go_ref[...] = gv
        d_ref[...] = -ADAM_LR * (m_hat / (jnp.sqrt(v_hat) + ADAM_EPS) + ADAM_WD * w_ref[...])
        nm_ref[...] = nm
        nv_ref[...] = nv

    spec = _bs((None, tile, C), lambda l, i: (l, i, 0))
    gspecs = [_bs((tile, CG), lambda l, i, j=j: (jnp.where(l == j, i, 0), 0)) for j in range(NL)]
    res = pl.pallas_call(
        body, name=name, grid=(NL, R // tile), in_specs=[spec, *gspecs, spec, spec, *[ANY] * len(after)],
        out_specs=[spec] * 4, out_shape=[jax.ShapeDtypeStruct((NL, R, C), F32)] * 4,
        compiler_params=_cparams(("parallel", "parallel")),
    )(w, *gs, m, v, *after)
    return [r[0] for r in res] if squeeze else res


def _my_pos():
    return lax.axis_index("x"), lax.axis_index("y"), lax.axis_index("c")


def _other_chips(x, y):
    return [(1 - x, y), (x, 1 - y), (1 - x, 1 - y)]


def _allgather_chips(shards):
    n = len(shards)
    per = 7

    def body(*refs):
        ins, outs = refs[:n], refs[n:2 * n]
        send_sems, recv_sems = refs[2 * n], refs[2 * n + 1]
        x, y, c = _my_pos()
        chips = _other_chips(x, y)
        sibling, me = (x, y, 1 - c), 2 * x + y

        def cp(a, kk, src, dst, to):
            return pltpu.make_async_remote_copy(src_ref=src, dst_ref=dst, send_sem=send_sems.at[per * a + kk],
                                                recv_sem=recv_sems.at[per * a + kk], device_id=to, device_id_type=MESH)

        sends = []
        for a in range(n):
            for j, chip in enumerate(chips):
                sends.append(cp(a, j, ins[a].at[c], outs[a].at[me, c], (*chip, c)))
            sends.append(cp(a, 3, ins[a], outs[a].at[me], sibling))
        for s in sends:
            s.start()
        for a in range(n):
            for j, chip in enumerate(chips):
                slab = outs[a].at[2 * chip[0] + chip[1], c]
                cp(a, j, slab, slab, (x, y, c)).wait_recv()
                fwd = cp(a, 4 + j, slab, slab, sibling)
                fwd.start()
                sends.append(fwd)
        for a in range(n):
            cp(a, 3, ins[a], outs[a].at[me], (x, y, c)).wait_recv()
            for j, chip in enumerate(chips):
                slab = outs[a].at[2 * chip[0] + chip[1], 1 - c]
                cp(a, 4 + j, slab, slab, (x, y, c)).wait_recv()
        for s in sends:
            s.wait_send()

    return pl.pallas_call(
        body, name="allgather_chips", in_specs=[ANY] * n, out_specs=[ANY] * n,
        out_shape=[jax.ShapeDtypeStruct((N_CHIPS,) + s.shape, s.dtype) for s in shards],
        scratch_shapes=[pltpu.SemaphoreType.DMA((per * n,)), pltpu.SemaphoreType.DMA((per * n,))],
    )(*shards)


def _rs_swap_rows(gs, tag):
    n = len(gs)

    def body(*refs):
        ins, outs = refs[:n], refs[n:2 * n]
        send_sems, recv_sems = refs[2 * n], refs[2 * n + 1]
        x, y, c = _my_pos()
        cps = []
        for a in range(n):
            half = ins[a].shape[1] // 2
            cps.append(pltpu.make_async_remote_copy(
                src_ref=ins[a].at[:, pl.ds((1 - c) * half, half)], dst_ref=outs[a], send_sem=send_sems.at[a],
                recv_sem=recv_sems.at[a], device_id=(x, y, 1 - c), device_id_type=MESH))
        for cp in cps:
            cp.start()
        for cp in cps:
            cp.wait()

    return pl.pallas_call(
        body, name=f"rs_swap_rows_{tag}", in_specs=[ANY] * n, out_specs=[ANY] * n,
        out_shape=[jax.ShapeDtypeStruct((N_CHIPS, g.shape[1] // 2, g.shape[2]), g.dtype) for g in gs],
        scratch_shapes=[pltpu.SemaphoreType.DMA((n,)), pltpu.SemaphoreType.DMA((n,))],
    )(*gs)


RS_ADD_VMEM_BYTES = 24 * 1024 * 1024


def _rs_tile(H, C, n):
    return _tile(H, max(16, RS_ADD_VMEM_BYTES // (28 * n * C)), 16)


def _rs_add_pair(gs, rs, pos, *, name):
    n = len(gs)
    _, H, C = rs[0].shape
    tile = _rs_tile(H, C, n)
    nt = H // tile

    def body(pos_ref, *refs):
        for a in range(n):
            s = refs[a][...] + refs[n + a][...]
            refs[2 * n + 2 * a][...] = s
            refs[2 * n + 2 * a + 1][...] = s.astype(BF16)

    spec = _bs((None, tile, C), lambda k, i, pos_ref: (k, i, 0))
    g_spec = _bs((None, tile, C), lambda k, i, pos_ref: (k, pos_ref[1] * nt + i, 0))
    grid_spec = pltpu.PrefetchScalarGridSpec(
        num_scalar_prefetch=1, grid=(N_CHIPS, nt), in_specs=[g_spec] * n + [spec] * n, out_specs=[spec] * (2 * n))
    res = pl.pallas_call(
        body, name=name, grid_spec=grid_spec,
        out_shape=[jax.ShapeDtypeStruct((N_CHIPS, H, C), F32), jax.ShapeDtypeStruct((N_CHIPS, H, C), BF16)] * n,
        compiler_params=_cparams(("parallel", "parallel")),
    )(pos, *gs, *rs)
    return [(res[2 * a], res[2 * a + 1]) for a in range(n)]


def _exchange_copies(srcs, lands, send_sems, recv_sems):
    x, y, c = _my_pos()
    starts, landing = [], []
    for a in range(len(srcs)):
        for j, chip in enumerate(_other_chips(x, y)):
            sems = dict(send_sem=send_sems.at[3 * a + j], recv_sem=recv_sems.at[3 * a + j], device_id_type=MESH)
            starts.append(pltpu.make_async_remote_copy(
                src_ref=srcs[a].at[2 * chip[0] + chip[1]], dst_ref=lands[a].at[j], device_id=(*chip, c), **sems))
            landing.append(pltpu.make_async_remote_copy(
                src_ref=lands[a].at[j], dst_ref=lands[a].at[j], device_id=(x, y, c), **sems))
    return starts, landing


def _gather_copies(srcs, lands, send_sems, recv_sems):
    x, y, c = _my_pos()
    me = 2 * x + y
    starts, landing = [], []
    for a in range(len(srcs)):
        half = srcs[a].shape[0] // 2
        mine = pl.ds(c * half, half)
        for j, chip in enumerate(_other_chips(x, y)):
            sems = dict(send_sem=send_sems.at[3 * a + j], recv_sem=recv_sems.at[3 * a + j], device_id_type=MESH)
            starts.append(pltpu.make_async_remote_copy(
                src_ref=srcs[a].at[mine], dst_ref=lands[a].at[me, mine], device_id=(*chip, c), **sems))
            slab = lands[a].at[2 * chip[0] + chip[1], mine]
            landing.append(pltpu.make_async_remote_copy(src_ref=slab, dst_ref=slab, device_id=(x, y, c), **sems))
    return starts, landing


HBM = pl.BlockSpec(memory_space=pltpu.HBM)
SEM = pl.BlockSpec(memory_space=pltpu.SEMAPHORE)


def _ici_start(copies_fn, srcs, land_shapes, *, name, after=(), sems_per_array=3):
    n, na = len(srcs), len(after)

    def body(*refs):
        starts, _ = copies_fn(refs[:n], refs[n:2 * n], refs[2 * n + na], refs[2 * n + na + 1])
        for cp in starts:
            cp.start()
        refs[-1][...] = jnp.zeros_like(refs[-1])

    sems = pltpu.SemaphoreType.DMA((sems_per_array * n,))
    hbm = lambda s: pltpu.HBM(s.shape, s.dtype)
    lands = [pltpu.with_memory_space_constraint(
        lax.empty(s.shape, s.dtype) if isinstance(s, jax.ShapeDtypeStruct) else s, pltpu.HBM) for s in land_shapes]
    res = pl.pallas_call(
        body, name=name, in_specs=[HBM] * (2 * n) + [ANY] * na,
        out_specs=(SEM, SEM, *[HBM] * (2 * n), pl.BlockSpec(memory_space=pltpu.VMEM)),
        out_shape=(sems, sems, *[hbm(s) for s in srcs], *[hbm(s) for s in land_shapes],
                   jax.ShapeDtypeStruct((8, BLOCK), F32)),
        input_output_aliases={i: 2 + i for i in range(2 * n)},
        compiler_params=pltpu.CompilerParams(has_side_effects=pltpu.SideEffectType.DATAFLOW_SIDE_EFFECTING),
    )(*[pltpu.with_memory_space_constraint(s, pltpu.HBM) for s in srcs], *lands, *after)
    return res[0], res[1], list(res[2:2 + n]), list(res[2 + n:2 + 2 * n]), res[-1]


def _ici_wait(copies_fn, send_sems, recv_sems, srcs, lands, after, *, name):
    n = len(srcs)
    after = list(after) if isinstance(after, (list, tuple)) else [after]

    def body(*refs):
        starts, landing = copies_fn(refs[:n], refs[n:2 * n], refs[2 * n], refs[2 * n + 1])
        for cp in starts:
            cp.wait_send()
        for cp in landing:
            cp.wait_recv()

    hbm = lambda s: pltpu.HBM(s.shape, s.dtype)
    res = pl.pallas_call(
        body, name=name, in_specs=[*[HBM] * (2 * n), SEM, SEM, *[ANY] * len(after)], out_specs=[HBM] * (2 * n),
        out_shape=[*[hbm(s) for s in srcs], *[hbm(s) for s in lands]],
        input_output_aliases={i: i for i in range(2 * n)},
        compiler_params=pltpu.CompilerParams(has_side_effects=pltpu.SideEffectType.DATAFLOW_SIDE_EFFECTING),
    )(*srcs, *lands, send_sems, recv_sems, *after)
    return list(res[:n]), list(res[n:])


D2D_COPIES = 4


def _d2d_copies(ins, outs, send_sems, recv_sems):
    x, y, c = _my_pos()
    me, sibling = 2 * x + y, (x, y, 1 - c)
    starts, landing = [], []
    for a in range(len(ins)):
        half = ins[a].shape[0] // 2
        mine, theirs = pl.ds(c * half, half), pl.ds((1 - c) * half, half)
        pairs = [(ins[a], outs[a].at[me], outs[a].at[me])]
        for chip in _other_chips(x, y):
            k = 2 * chip[0] + chip[1]
            pairs.append((outs[a].at[k, mine], outs[a].at[k, mine], outs[a].at[k, theirs]))
        for j, (src, dst, lands_here) in enumerate(pairs):
            sems = dict(send_sem=send_sems.at[D2D_COPIES * a + j], recv_sem=recv_sems.at[D2D_COPIES * a + j],
                        device_id_type=MESH)
            starts.append(pltpu.make_async_remote_copy(src_ref=src, dst_ref=dst, device_id=sibling, **sems))
            landing.append(pltpu.make_async_remote_copy(src_ref=lands_here, dst_ref=lands_here, device_id=(x, y, c),
                                                        **sems))
    return starts, landing


def _gather_d2d(shards, lands, tag):
    n = len(shards)

    def body(*refs):
        starts, landing = _d2d_copies(refs[:n], refs[2 * n:3 * n], refs[3 * n], refs[3 * n + 1])
        for cp in starts:
            cp.start()
        for cp in landing:
            cp.wait_recv()
        for cp in starts:
            cp.wait_send()

    return pl.pallas_call(
        body, name=f"gather_d2d_{tag}", in_specs=[ANY] * (2 * n), out_specs=[ANY] * n,
        out_shape=[jax.ShapeDtypeStruct(s.shape, s.dtype) for s in lands],
        input_output_aliases={n + a: a for a in range(n)},
        scratch_shapes=[pltpu.SemaphoreType.DMA((D2D_COPIES * n,)), pltpu.SemaphoreType.DMA((D2D_COPIES * n,))],
    )(*shards, *lands)


def _rs_add_chips(p32s, r16s, pos, *, name):
    n = len(p32s)
    _, H, C = p32s[0].shape
    tile = _rs_tile(H, C, n)
    nt = H // tile

    def body(pos_ref, *refs):
        for a in range(n):
            p_ref, r_ref = refs[a], refs[n + a]
            refs[2 * n + a][...] = ((p_ref[...] + r_ref[0].astype(F32)) + r_ref[1].astype(F32)) + r_ref[2].astype(F32)

    grid_spec = pltpu.PrefetchScalarGridSpec(
        num_scalar_prefetch=1, grid=(nt,),
        in_specs=[_bs((None, tile, C), lambda i, pos_ref: (pos_ref[0], i, 0))] * n
        + [_bs((3, tile, C), lambda i, pos_ref: (0, i, 0))] * n,
        out_specs=[_bs((tile, C), lambda i, pos_ref: (pos_ref[1] * nt + i, 0))] * n)
    return pl.pallas_call(
        body, name=name, grid_spec=grid_spec, out_shape=[jax.ShapeDtypeStruct((2 * H, C), F32)] * n,
        compiler_params=_cparams(("parallel",)),
    )(pos, *p32s, *r16s)


def _rs_join_rows(fs, tag):
    n = len(fs)

    def body(*refs):
        outs = refs[n:2 * n]
        send_sems, recv_sems = refs[2 * n], refs[2 * n + 1]
        x, y, c = _my_pos()
        for a in range(n):
            half = outs[a].shape[0] // 2
            mine = outs[a].at[pl.ds(c * half, half)]
            pltpu.make_async_remote_copy(src_ref=mine, dst_ref=mine, send_sem=send_sems.at[a],
                                         recv_sem=recv_sems.at[a], device_id=(x, y, 1 - c), device_id_type=MESH).start()
        for a in range(n):
            half = outs[a].shape[0] // 2
            pltpu.make_async_remote_copy(
                src_ref=outs[a].at[pl.ds(c * half, half)], dst_ref=outs[a].at[pl.ds((1 - c) * half, half)],
                send_sem=send_sems.at[a], recv_sem=recv_sems.at[a], device_id=(x, y, 1 - c), device_id_type=MESH).wait()

    return pl.pallas_call(
        body, name=f"rs_join_rows_{tag}", in_specs=[ANY] * n, out_specs=[ANY] * n,
        out_shape=[jax.ShapeDtypeStruct(f.shape, f.dtype) for f in fs],
        input_output_aliases={a: a for a in range(n)},
        scratch_shapes=[pltpu.SemaphoreType.DMA((n,)), pltpu.SemaphoreType.DMA((n,))],
    )(*fs)


def _pos_vector():
    x, y, c = _my_pos()
    return jnp.stack([2 * x + y, c]).astype(jnp.int32)


def _swap_copies(srcs, lands, send_sems, recv_sems):
    x, y, c = _my_pos()
    starts, landing = [], []
    for a in range(len(srcs)):
        half = srcs[a].shape[1] // 2
        sems = dict(send_sem=send_sems.at[3 * a], recv_sem=recv_sems.at[3 * a], device_id_type=MESH)
        starts.append(pltpu.make_async_remote_copy(
            src_ref=srcs[a].at[:, pl.ds((1 - c) * half, half)], dst_ref=lands[a], device_id=(x, y, 1 - c), **sems))
        landing.append(pltpu.make_async_remote_copy(src_ref=lands[a], dst_ref=lands[a], device_id=(x, y, c), **sems))
    return starts, landing


def _swap_land_shapes(gs):
    return [jax.ShapeDtypeStruct((N_CHIPS, g.shape[1] // 2, g.shape[2]), g.dtype) for g in gs]


def _same_shape_runs(arrays):
    runs, start = [], 0
    for i in range(1, len(arrays) + 1):
        if i == len(arrays) or arrays[i].shape != arrays[start].shape:
            runs.append((start, i))
            start = i
    return runs


def _rs_add_pairs(gs, r1, names, tag):
    pos = _pos_vector()
    out = []
    for a, b in _same_shape_runs(gs):
        out += _rs_add_pair(gs[a:b], r1[a:b], pos, name=f"rs_add_pair_{tag}_{names[a]}")
    return out


def _rs_pair_sums(gs, names, tag):
    return _rs_add_pairs(gs, _rs_swap_rows(gs, tag), names, tag)


def _rs_finish(pairs, r2, names, tag):
    pos = _pos_vector()
    p32s = [p[0] for p in pairs]
    fs = []
    for a, b in _same_shape_runs(p32s):
        fs += _rs_add_chips(p32s[a:b], r2[a:b], pos, name=f"rs_add_chips_{tag}_{names[a]}")
    return _rs_join_rows(fs, tag)


def _exchange_land_shapes(pairs):
    return [jax.ShapeDtypeStruct((3,) + p[1].shape[1:], p[1].dtype) for p in pairs]


def _allreduce_small(buf):
    R, W = buf.shape

    def body(b_ref, o_ref, gather, send_sems, recv_sems):
        x, y, c = _my_pos()
        me = 4 * x + 2 * y + c
        gather[me] = b_ref[...]
        cps = []
        for d in range(1, 8):
            peer = (x ^ (d >> 2), y ^ ((d >> 1) & 1), c ^ (d & 1))
            cps.append(pltpu.make_async_remote_copy(
                src_ref=b_ref, dst_ref=gather.at[me], send_sem=send_sems.at[d - 1], recv_sem=recv_sems.at[d - 1],
                device_id=peer, device_id_type=MESH))
        for cp in cps:
            cp.start()
        for d in range(1, 8):
            pltpu.make_async_remote_copy(
                src_ref=b_ref, dst_ref=gather.at[me ^ d], send_sem=send_sems.at[d - 1], recv_sem=recv_sems.at[d - 1],
                device_id=(x, y, c), device_id_type=MESH).wait_recv()
        for cp in cps:
            cp.wait_send()
        acc = gather[0]
        for d in range(1, 8):
            acc = acc + gather[d]
        o_ref[...] = acc

    vm = pl.BlockSpec(memory_space=pltpu.VMEM)
    return pl.pallas_call(
        body, name="allreduce_small", in_specs=[vm], out_specs=vm, out_shape=jax.ShapeDtypeStruct((R, W), F32),
        scratch_shapes=[pltpu.VMEM((8, R, W), F32), pltpu.SemaphoreType.DMA((7,)), pltpu.SemaphoreType.DMA((7,))],
    )(buf)


def _heads(a, h, d):
    return a.reshape(a.shape[0], h, d).transpose(1, 0, 2)


def _unheads(a):
    h, L, d = a.shape
    return a.transpose(1, 0, 2).reshape(L, h * d)


def _rope_tables():
    pos = jnp.maximum(jnp.arange(LP, dtype=F32) - PAD_ROWS, 0.0)
    inv_freq = 1.0 / (ROPE_THETA ** (jnp.arange(0, MLA_ROPE, 2, dtype=F32) / MLA_ROPE))
    ang = pos[:, None] * inv_freq[None, :]
    cos, sin = jnp.tile(jnp.cos(ang), (1, MLA_HEADS)), jnp.tile(jnp.sin(ang), (1, MLA_HEADS))
    return jnp.concatenate([cos, cos], axis=1), jnp.concatenate([-sin, sin], axis=1)


def _lane_pad(a, width=BLOCK):
    return jnp.pad(a, ((0, 0), (0, width - a.shape[1])))


def _pad_in_proj(w):
    sl = lambda start, size: w[:, start:start + size]
    return jnp.concatenate([
        sl(OC_Z, 512), sl(OC_XBC, 768), sl(OC_FQ, 256), sl(OC_FK, 256), sl(OC_FV, 256), sl(OC_CQ, 256), sl(OC_CKV, 128),
        _lane_pad(sl(OC_DT, SSD_HEADS)), _lane_pad(sl(OC_FR, FOX_HEADS)),
        jnp.tile(sl(OC_KR, ROPE_HALF), (1, MLA_HEADS)), jnp.tile(sl(OC_KR + ROPE_HALF, ROPE_HALF), (1, MLA_HEADS))], axis=1)


def _in_proj_grad_chunks(wp):
    rope = lambda start: wp[:, start:start + 64].reshape(wp.shape[0], MLA_HEADS, ROPE_HALF).sum(axis=1)
    segs = [(wp, PC_Z, 512), (wp, PC_XBC, 768), (wp, PC_DT, SSD_HEADS), (wp, PC_FQ, 256), (wp, PC_FK, 256),
            (wp, PC_FV, 256), (wp, PC_FR, FOX_HEADS), (wp, PC_CQ, 256), (wp, PC_CKV, 128),
            (rope(PC_KR), 0, ROPE_HALF), (rope(PC_KR + 64), 0, ROPE_HALF)]
    chunks = []
    for k in range(N_CHIPS):
        lo, hi, pos, pieces = k * IN_SHARD, (k + 1) * IN_SHARD, 0, []
        for arr, start, size in segs:
            a, b = max(lo, pos), min(hi, pos + size)
            if a < b:
                pieces.append(arr[:, start + a - pos:start + b - pos])
            pos += size
        pieces.append(jnp.zeros((wp.shape[0], IN_SHARD_P - IN_SHARD), wp.dtype))
        chunks.append(jnp.concatenate(pieces, axis=1))
    return jnp.stack(chunks)


def _regroup_uq(w):
    w3 = w.reshape(w.shape[0], MLA_HEADS, MLA_NOPE + MLA_ROPE)
    return jnp.concatenate([w3[:, :, :MLA_NOPE].reshape(w.shape[0], -1),
                            w3[:, :, MLA_NOPE:MLA_NOPE + ROPE_HALF].reshape(w.shape[0], -1),
                            w3[:, :, MLA_NOPE + ROPE_HALF:].reshape(w.shape[0], -1)], axis=1)


def _ungroup_uq(wp):
    n = wp.shape[0]
    return jnp.concatenate([wp[:, :256].reshape(n, MLA_HEADS, MLA_NOPE), wp[:, 256:320].reshape(n, MLA_HEADS, ROPE_HALF),
                            wp[:, 320:].reshape(n, MLA_HEADS, ROPE_HALF)], axis=2).reshape(n, -1)


def _regroup_ukv(w):
    w3 = w.reshape(w.shape[0], MLA_HEADS, MLA_NOPE + MLA_V)
    return jnp.concatenate([w3[:, :, :MLA_NOPE].reshape(w.shape[0], -1), w3[:, :, MLA_NOPE:].reshape(w.shape[0], -1)],
                           axis=1)


def _ungroup_ukv(wp):
    n = wp.shape[0]
    return jnp.concatenate([wp[:, :256].reshape(n, MLA_HEADS, MLA_NOPE), wp[:, 256:].reshape(n, MLA_HEADS, MLA_V)],
                           axis=2).reshape(n, -1)


TMF = 1088
N_IF = LP // TMF


def _chunk_rows_dx(g, w, l, chunk_h, *, name):
    N = w.shape[2]
    return _mm_core(g, w, a_spec=_bs((TMF, N), lambda i, j, k: (i, 0)),
                    b_spec=_bs((None, chunk_h, N), lambda i, j, k: (j, 0, 0)),
                    o_spec=_bs((TMF, chunk_h), lambda i, j, k: (i, j)), grid=(N_IF, N_CHIPS, 1),
                    out_shape=(LP, N_CHIPS * chunk_h), ca=1, cb=1, name=name)


def _chunk_rows_dw(a, g, chunk_h, *, name):
    N = g.shape[1]
    return _mm_core(a, g, a_spec=_bs((LP, chunk_h), lambda i, j, k: (0, i)), b_spec=_bs((LP, N), lambda i, j, k: (0, 0)),
                    o_spec=_bs((None, chunk_h, N), lambda i, j, k: (i, 0, 0)), grid=(N_CHIPS, 1, 1),
                    out_shape=(N_CHIPS, chunk_h, N), ca=0, cb=0, name=name)


def _ffn_up_swiglu(h, wg, wu, *, name, after=()):
    def body(h_ref, wg_ref, wu_ref, *refs):
        g_ref, u_ref, a_ref = refs[len(after):]
        hb = h_ref[...].astype(BF16)
        g = _raw_bdot(hb, wg_ref[...], 1, 1)
        u = _raw_bdot(hb, wu_ref[...], 1, 1)
        g_ref[...] = g
        u_ref[...] = u
        a_ref[...] = (_silu(g) * u).astype(a_ref.dtype)

    w_spec = _bs((None, HP, D_MODEL), lambda i, j: (j, 0, 0))
    o_spec = _bs((TMF, HP), lambda i, j: (i, j))
    return pl.pallas_call(
        body, name=name, grid=(N_IF, N_CHIPS),
        in_specs=[_bs((TMF, D_MODEL), lambda i, j: (i, 0)), w_spec, w_spec, *[ANY] * len(after)],
        out_specs=[o_spec] * 3,
        out_shape=[jax.ShapeDtypeStruct((LP, FP), F32), jax.ShapeDtypeStruct((LP, FP), F32),
                   jax.ShapeDtypeStruct((LP, FP), BF16)],
        compiler_params=_cparams(("parallel", "parallel")),
    )(h, wg, wu, *after)


def _ffn_down_dx_swiglu(do, wd, g, u, *, name):
    def body(do_ref, wd_ref, g_ref, u_ref, dg_ref, du_ref):
        dact = _raw_bdot(do_ref[...], wd_ref[...], 1, 1)
        gv = g_ref[...]
        sig = _sigmoid(gv)
        dg_ref[...] = (dact * u_ref[...] * (sig * (1.0 + gv * (1.0 - sig)))).astype(dg_ref.dtype)
        du_ref[...] = (dact * (gv * sig)).astype(du_ref.dtype)

    blk = _bs((TMF, HP), lambda i, j: (i, j))
    return pl.pallas_call(
        body, name=name, grid=(N_IF, N_CHIPS),
        in_specs=[_bs((TMF, D_MODEL), lambda i, j: (i, 0)), _bs((None, HP, D_MODEL), lambda i, j: (j, 0, 0)), blk, blk],
        out_specs=[blk, blk], out_shape=[jax.ShapeDtypeStruct((LP, FP), BF16)] * 2,
        compiler_params=_cparams(("parallel", "parallel")),
    )(do, wd, g, u)


def _ffn_gate_up_dw(dg, du, h, *, name):
    def body(dg_ref, du_ref, h_ref, wg_ref, wu_ref):
        hb = h_ref[...].astype(BF16)
        wg_ref[...] = _raw_bdot(dg_ref[...], hb, 0, 0)
        wu_ref[...] = _raw_bdot(du_ref[...], hb, 0, 0)

    a_spec = _bs((LP, HP), lambda k: (0, k))
    o_spec = _bs((None, HP, D_MODEL), lambda k: (k, 0, 0))
    return pl.pallas_call(
        body, name=name, grid=(N_CHIPS,), in_specs=[a_spec, a_spec, _bs((LP, D_MODEL), lambda k: (0, 0))],
        out_specs=[o_spec, o_spec], out_shape=[jax.ShapeDtypeStruct((N_CHIPS, HP, D_MODEL), F32)] * 2,
        compiler_params=_cparams(("parallel",)),
    )(dg, du, h)


def _ffn_gate_up_dx(dg, du, wg, wu, add, *, name):
    def body(dg_ref, du_ref, wg_ref, wu_ref, add_ref, o_ref, acc_ref):
        k = pl.program_id(1)

        @pl.when(k == 0)
        def _():
            acc_ref[...] = jnp.zeros_like(acc_ref)

        acc_ref[...] += _raw_bdot(dg_ref[...], wg_ref[...], 1, 0) + _raw_bdot(du_ref[...], wu_ref[...], 1, 0)

        @pl.when(k == N_CHIPS - 1)
        def _():
            o_ref[...] = acc_ref[...] + add_ref[...]

    a_spec = _bs((TMF, HP), lambda i, k: (i, k))
    w_spec = _bs((None, HP, D_MODEL), lambda i, k: (k, 0, 0))
    o_spec = _bs((TMF, D_MODEL), lambda i, k: (i, 0))
    return pl.pallas_call(
        body, name=name, grid=(N_IF, N_CHIPS), in_specs=[a_spec, a_spec, w_spec, w_spec, o_spec], out_specs=o_spec,
        out_shape=jax.ShapeDtypeStruct((LP, D_MODEL), F32), scratch_shapes=[pltpu.VMEM((TMF, D_MODEL), F32)],
        compiler_params=_cparams(("parallel", "arbitrary")),
    )(dg, du, wg, wu, add)


def _chunk_rows_mm_res_ln(a, w, chunk_h, h, gam, bet, scale, *, name):
    res_ln = _make_res_ln_fn(scale)

    def body(a_ref, w_ref, h_ref, g_ref, b_ref, o_ref, y_ref, yb_ref, acc_ref):
        k = pl.program_id(1)

        @pl.when(k == 0)
        def _():
            acc_ref[...] = jnp.zeros_like(acc_ref)

        acc_ref[...] += _raw_bdot(a_ref[...], w_ref[...], 1, 0)

        @pl.when(k == N_CHIPS - 1)
        def _():
            o = acc_ref[...]
            o_ref[...] = o
            (y,) = res_ln(0, h_ref[...], o, g_ref[...], b_ref[...])
            y_ref[...] = y
            yb_ref[...] = y.astype(yb_ref.dtype)

    row = _bs((TMF, D_MODEL), lambda i, k: (i, 0))
    par = _bs((1, D_MODEL), lambda i, k: (0, 0))
    return pl.pallas_call(
        body, name=name, grid=(N_IF, N_CHIPS),
        in_specs=[_bs((TMF, chunk_h), lambda i, k: (i, k)), _bs((None, chunk_h, D_MODEL), lambda i, k: (k, 0, 0)), row,
                  par, par],
        out_specs=[row, row, row],
        out_shape=[jax.ShapeDtypeStruct((LP, D_MODEL), F32)] * 2 + [jax.ShapeDtypeStruct((LP, D_MODEL), BF16)],
        scratch_shapes=[pltpu.VMEM((TMF, D_MODEL), F32)], compiler_params=_cparams(("parallel", "arbitrary")),
    )(a, w, h, gam, bet)


def _ffn_fwd(hp, W, pre, l, gam, bet, tag, after=()):
    h, hb = hp
    g, u, act = _ffn_up_swiglu(hb, W[pre + "_w_gate"][l], W[pre + "_w_up"][l], name=f"{tag}_up_swiglu", after=after)
    o, out, outb = _chunk_rows_mm_res_ln(act, W[pre + "_w_down"][l], HP, h, gam, bet, 0.5, name=f"{tag}_down_ln")
    return (out, outb), (h, hb, g, u, act, o)


def _ffn_bwd(dout, saved, W, pre, l, gam, bet, GB, tag):
    h, hb, g, u, act, o = saved
    (dh_a, do), (dgam, dbet) = _rowwise_bwd(_make_res_ln_fn(0.5), [h, o], [gam, bet], [dout], name=f"{tag}_ln_bwd",
                                            tile=TM, grad_dtypes=[F32, BF16])
    dg, du = _ffn_down_dx_swiglu(do, W[pre + "_w_down"][l], g, u, name=f"{tag}_down_dx_swiglu")
    GB[pre + "_w_down"] = _chunk_rows_dw(act, do, HP, name=f"{tag}_down_dw")
    GB[pre + "_w_gate"], GB[pre + "_w_up"] = _ffn_gate_up_dw(dg, du, hb, name=f"{tag}_gate_up_dw")
    dh = _ffn_gate_up_dx(dg, du, W[pre + "_w_gate"][l], W[pre + "_w_up"][l], dh_a, name=f"{tag}_gate_up_dx")
    return dh, dgam, dbet


def _mixer_fwd(hp1, W, l, cosf, sins, after=()):
    h1, h1b = hp1
    tag = f"l{l}"
    proj = _mm(h1b, W["w_in_p"][l], name=f"{tag}_in_proj", after=after)
    sv = {"h1": h1, "h1b": h1b, "proj": proj}
    conv_w, conv_b = W["conv_w"][l], W["conv_b"][l][None]
    xc = _conv_fwd(proj, PC_XBC // BLOCK, conv_w, conv_b, name=f"{tag}_conv")
    dt_bias = _lane_pad(W["dt_bias"][l][None])
    dtc, dtr = _ssd_dt_fwd(proj, PC_DT // BLOCK, dt_bias, name=f"{tag}_ssd_dt")
    xh = _heads(xc[:, :SSD_D], SSD_HEADS, SSD_HD)
    bm = _heads(xc[:, SSD_D:SSD_D + 128], SSD_GROUPS, SSD_STATE)
    cm = _heads(xc[:, SSD_D + 128:], SSD_GROUPS, SSD_STATE)
    alog = jnp.broadcast_to(W["a_log"][l][:, None, None], (SSD_HEADS, 1, BLOCK))
    yh, prevs = _ssd_fwd(xh, bm, cm, dtc, dtr, alog, name=f"{tag}_ssd")
    y_raw = _unheads(yh)
    dskip = jnp.repeat(W["d_skip"][l], SSD_HD)[None]
    normg = W["ssd_norm_g"][l][None]
    post_rows = [y_raw, (xc, 256, 0), (proj, 256, PC_Z // 256)]
    (y_ssd,) = _rowwise(_ssd_post_fn, post_rows, [dskip, normg], [SSD_D], name=f"{tag}_ssd_post", tile=TM,
                        ncol=SSD_GROUPS)
    sv.update(conv_w=conv_w, conv_b=conv_b, dt_bias=dt_bias, xh=xh, bm=bm, cm=cm, dtc=dtc, dtr=dtr, alog=alog,
              prevs=prevs, post_rows=post_rows, dskip=dskip, normg=normg)
    f_b = _lane_pad(W["fox_f_b"][l][None])
    cg, cgt = _fox_gate_fwd(proj, PC_FR // BLOCK, f_b, name=f"{tag}_fox_gate")
    fox_qkv = ((proj, PC_FQ // ATT_W), (proj, PC_FK // ATT_W), (proj, PC_FV // ATT_W))
    y_fox, lse_f = _attn_fwd(*fox_qkv, scale=FOX_HD ** -0.5, name=f"{tag}_fox_attn", bias=(cg, cgt))
    sv.update(f_b=f_b, cg=cg, cgt=cgt, fox_qkv=fox_qkv, y_fox=y_fox, lse_f=lse_f)
    gq, gkv = W["mla_q_norm_g"][l][None], W["mla_kv_norm_g"][l][None]
    norm_rows = [(proj, 256, PC_CQ // 256), (proj, BLOCK, PC_CKV // BLOCK)]
    qn, cn = _rowwise(_mla_norm_fn, norm_rows, [gq, gkv], [MLA_Q_LORA, MLA_KV_LORA], name=f"{tag}_mla_norm", tile=TM,
                      out_dtypes=[BF16, BF16])
    qh = _mm(qn, W["mla_w_uq_p"][l], name=f"{tag}_mla_uq")
    kvh = _mm(cn, W["mla_w_ukv_p"][l], name=f"{tag}_mla_ukv")
    qr, kr = _rowwise(_rope_fn, [(qh, BLOCK, 2), (proj, BLOCK, PC_KR // BLOCK), cosf, sins], [], [BLOCK, BLOCK],
                      name=f"{tag}_rope", tile=TM)
    mla_qkv = ((qh, 0), (kvh, 0), (kvh, 1))
    y_mla, lse_m = _attn_fwd(*mla_qkv, scale=(MLA_NOPE + MLA_ROPE) ** -0.5, name=f"{tag}_mla_attn",
                             rope=((qr, 0), (kr, 0)))
    sv.update(gq=gq, gkv=gkv, norm_rows=norm_rows, qn=qn, cn=cn, qr=qr, kr=kr, mla_qkv=mla_qkv, y_mla=y_mla, lse_m=lse_m)
    ycat = jnp.concatenate([y_ssd, y_fox, y_mla], axis=1).astype(BF16)
    mix, h2, h2b = _chunk_rows_mm_res_ln(ycat, W["w_out"][l], 256, h1, W["ln2_g"][l][None], W["ln2_b"][l][None], 1.0,
                                    name=f"{tag}_out_proj_ln2")
    sv.update(mix=mix, ycat=ycat)
    return (h2, h2b), sv


def _mixer_bwd(dh2, sv, W, l, cosf, sins, GB, zero=0.0):
    tag = f"l{l}"
    G = {}
    proj = sv["proj"]
    ln2g, ln2b = W["ln2_g"][l][None] + zero, W["ln2_b"][l][None]
    (dh1_a, dmix), (dln2g, dln2b) = _rowwise_bwd(
        _make_res_ln_fn(1.0), [sv["h1"], sv["mix"]], [ln2g, ln2b], [dh2], name=f"{tag}_ln2_bwd", tile=TM,
        grad_dtypes=[F32, BF16])
    G["ln2_g"], G["ln2_b"] = dln2g[0], dln2b[0]
    dycat = _chunk_rows_dx(dmix, W["w_out"][l], l, 256, name=f"{tag}_out_proj_dx")
    GB["w_out"] = _chunk_rows_dw(sv["ycat"], dmix, 256, name=f"{tag}_out_proj_dw")
    (dy_raw, dxs_a, dz), (ddskip, dnormg) = _rowwise_bwd(
        _ssd_post_fn, sv["post_rows"], [sv["dskip"], sv["normg"]], [dycat[:, :SSD_D]],
        name=f"{tag}_ssd_post_bwd", tile=TM, ncol=SSD_GROUPS)
    G["ssd_norm_g"] = dnormg[0]
    G["d_skip"] = ddskip.reshape(SSD_HEADS, SSD_HD).sum(axis=1)
    dxh, dbm, dcm, ddtc, ddtr, dal = _ssd_bwd(sv["xh"], sv["bm"], sv["cm"], sv["dtc"], sv["dtr"], sv["alog"],
                                              sv["prevs"], _heads(dy_raw, SSD_HEADS, SSD_HD), name=f"{tag}_ssd_bwd")
    G["a_log"] = dal[:, 0, 0]
    dxc = jnp.concatenate([dxs_a + _unheads(dxh), _unheads(dbm), _unheads(dcm)], axis=1)
    dxbc, G["conv_w"], dconv_b = _conv_bwd(proj, PC_XBC // BLOCK, sv["conv_w"], sv["conv_b"], dxc,
                                           name=f"{tag}_conv_bwd")
    G["conv_b"] = dconv_b[0]
    ddt_raw, ddt_bias = _ssd_dt_bwd(proj, PC_DT // BLOCK, sv["dt_bias"], ddtc, ddtr, name=f"{tag}_ssd_dt_bwd")
    G["dt_bias"] = ddt_bias[0, :SSD_HEADS]
    dfq, dfk, dfv, dcg, dcgt = _attn_bwd(*sv["fox_qkv"], sv["y_fox"], sv["lse_f"], (dycat, SSD_D // ATT_W),
                                         scale=FOX_HD ** -0.5, name=f"{tag}_fox_attn_bwd", bias=(sv["cg"], sv["cgt"]))
    df_raw, dfb = _fox_gate_bwd(proj, PC_FR // BLOCK, sv["f_b"], dcg, dcgt, name=f"{tag}_fox_gate_bwd")
    G["fox_f_b"] = dfb[0, :FOX_HEADS]
    dqn_h, dkn_h, dv_h, dqr, dkr = _attn_bwd(
        *sv["mla_qkv"], sv["y_mla"], sv["lse_m"], (dycat, (SSD_D + FOX_D) // ATT_W),
        scale=(MLA_NOPE + MLA_ROPE) ** -0.5, name=f"{tag}_mla_attn_bwd", rope=((sv["qr"], 0), (sv["kr"], 0)))
    dq_rope, dk_rope = _rowwise(_rope_t_fn, [dqr, dkr, cosf, sins], [], [BLOCK, BLOCK], name=f"{tag}_rope_bwd",
                                tile=TM)
    dqh = jnp.concatenate([dqn_h, dq_rope], axis=1).astype(BF16)
    dkvh = jnp.concatenate([dkn_h, dv_h], axis=1).astype(BF16)
    dqn = _mm(dqh, W["mla_w_uq_p"][l], tb=True, name=f"{tag}_mla_uq_dx")
    G["mla_w_uq_p"] = _mm(sv["qn"], dqh, ta=True, name=f"{tag}_mla_uq_dw")
    dcn = _mm(dkvh, W["mla_w_ukv_p"][l], tb=True, name=f"{tag}_mla_ukv_dx")
    G["mla_w_ukv_p"] = _mm(sv["cn"], dkvh, ta=True, name=f"{tag}_mla_ukv_dw")
    (dcq, dckv), (dgq, dgkv) = _rowwise_bwd(_mla_norm_fn, sv["norm_rows"], [sv["gq"], sv["gkv"]], [dqn, dcn],
                                            name=f"{tag}_mla_norm_bwd", tile=TM)
    G["mla_q_norm_g"], G["mla_kv_norm_g"] = dgq[0], dgkv[0]
    dproj = jnp.concatenate([dz, dxbc, dfq, dfk, dfv, dcq, dckv, ddt_raw, df_raw, dk_rope], axis=1).astype(BF16)
    dh1 = _mm(dproj, W["w_in_p"][l], tb=True, add=dh1_a, name=f"{tag}_in_proj_dx")
    G["w_in_p"] = _mm(sv["h1b"], dproj, ta=True, name=f"{tag}_in_proj_dw")
    return dh1, G


def _embed(x, meta):
    return jnp.concatenate([jnp.zeros((PAD_ROWS, D_MODEL), F32), meta, x], axis=0)


def _layer_fwd(h, W, l, cosf, sins):
    ln = lambda n: W[n][l][None]
    h1, s1 = _ffn_fwd(h, W, "ffn1", l, ln("ln1_g"), ln("ln1_b"), f"l{l}_ffn1")
    h2, sm = _mixer_fwd(h1, W, l, cosf, sins)
    h3, s2 = _ffn_fwd(h2, W, "ffn2", l, ln("ln3_g"), ln("ln3_b"), f"l{l}_ffn2")
    return h3, (s1, sm, s2)


def _layer_bwd(dh, saved, W, l, cosf, sins):
    ln = lambda n: W[n][l][None]
    s1, sm, s2 = saved
    G = {}
    dh, dg, db = _ffn_bwd(dh, s2, W, "ffn2", l, ln("ln3_g"), ln("ln3_b"), G, f"l{l}_ffn2")
    G["ln3_g"], G["ln3_b"] = dg[0], db[0]
    dh, Gm = _mixer_bwd(dh, sm, W, l, cosf, sins, G)
    G.update(Gm)
    dh, dg, db = _ffn_bwd(dh, s1, W, "ffn1", l, ln("ln1_g"), ln("ln1_b"), G, f"l{l}_ffn1")
    G["ln1_g"], G["ln1_b"] = dg[0], db[0]
    return dh, G


def _local_step(x, target, W):
    h = _embed(x, W["meta"])
    h = (h, h.astype(BF16))
    tgt = jnp.concatenate([jnp.zeros((BLOCK, D_MODEL), F32), target], axis=0)
    cosf, sins = _rope_tables()
    saved = []
    for l in range(DEPTH):
        h, sv = _layer_fwd(h, W, l, cosf, sins)
        saved.append(sv)
    dh, loss = _loss_head(h[0], tgt, name="loss_head")
    grads = [None] * DEPTH
    for l in reversed(range(DEPTH)):
        dh, grads[l] = _layer_bwd(dh, saved[l], W, l, cosf, sins)
    return loss, dh, grads


WEIGHTS = ['meta', 'ffn1_w_gate', 'ffn1_w_up', 'ffn1_w_down', 'ln1_g', 'ln1_b', 'w_in', 'conv_w', 'conv_b', 'dt_bias',
           'a_log', 'd_skip', 'ssd_norm_g', 'fox_f_b', 'mla_q_norm_g', 'mla_w_uq', 'mla_kv_norm_g', 'mla_w_ukv',
           'w_out', 'ln2_g', 'ln2_b', 'ffn2_w_gate', 'ffn2_w_up', 'ffn2_w_down', 'ln3_g', 'ln3_b']
SMALL = ["ln1_g", "ln1_b", "conv_b", "dt_bias", "a_log", "d_skip", "ssd_norm_g", "fox_f_b", "mla_q_norm_g",
         "mla_kv_norm_g", "ln2_g", "ln2_b", "ln3_g", "ln3_b"]
MATMUL_W = ["ffn1_w_gate", "ffn1_w_up", "ffn1_w_down", "w_in", "mla_w_uq", "mla_w_ukv", "w_out", "ffn2_w_gate",
            "ffn2_w_up", "ffn2_w_down"]
SMALL_ROWS = 312


def _pad_to(a, axis, size):
    pads = [(0, 0)] * a.ndim
    pads[axis] = (0, size - a.shape[axis])
    return jnp.pad(a, pads)


def _chip_cols(full, chip, width):
    return lax.dynamic_slice_in_dim(full, chip * width, width, axis=full.ndim - 1)


def kernel(x, meta, ffn1_w_gate, ffn1_w_up, ffn1_w_down, ln1_g, ln1_b, w_in, conv_w, conv_b, dt_bias, a_log, d_skip, ssd_norm_g, fox_f_b, mla_q_norm_g, mla_w_uq, mla_kv_norm_g, mla_w_ukv, w_out, ln2_g, ln2_b, ffn2_w_gate, ffn2_w_up, ffn2_w_down, ln3_g, ln3_b, loss_target, m_meta, m_ffn1_w_gate, m_ffn1_w_up, m_ffn1_w_down, m_ln1_g, m_ln1_b, m_w_in, m_conv_w, m_conv_b, m_dt_bias, m_a_log, m_d_skip, m_ssd_norm_g, m_fox_f_b, m_mla_q_norm_g, m_mla_w_uq, m_mla_kv_norm_g, m_mla_w_ukv, m_w_out, m_ln2_g, m_ln2_b, m_ffn2_w_gate, m_ffn2_w_up, m_ffn2_w_down, m_ln3_g, m_ln3_b, v_meta, v_ffn1_w_gate, v_ffn1_w_up, v_ffn1_w_down, v_ln1_g, v_ln1_b, v_w_in, v_conv_w, v_conv_b, v_dt_bias, v_a_log, v_d_skip, v_ssd_norm_g, v_fox_f_b, v_mla_q_norm_g, v_mla_w_uq, v_mla_kv_norm_g, v_mla_w_ukv, v_w_out, v_ln2_g, v_ln2_b, v_ffn2_w_gate, v_ffn2_w_up, v_ffn2_w_down, v_ln3_g, v_ln3_b):
    args = dict(locals())
    w = {n: args[n] for n in WEIGHTS}
    m = {n: args["m_" + n] for n in WEIGHTS}
    v = {n: args["v_" + n] for n in WEIGHTS}
    xcoord, ycoord, _ = _my_pos()
    chip = 2 * xcoord + ycoord

    tr = lambda a: jnp.swapaxes(a, 1, 2)

    def bf16_shard(n, l, zero=None):
        a = w[n] if zero is None else w[n] + zero
        if n.endswith("w_gate") or n.endswith("w_up"):
            a = _pad_to(tr(a), 1, HP)
        elif n.endswith("w_down"):
            a = _pad_to(a, 1, HP)
        elif n == "w_in":
            a = _pad_to(a, 2, IN_SHARD_P)
        return a[l].astype(BF16)

    land_shape = lambda s: jax.ShapeDtypeStruct((N_CHIPS,) + s.shape, s.dtype)

    def gather_start(names, l, tag, after):
        srcs = [bf16_shard(n, l, None if after is None else after[0, 0]) for n in names]
        return _ici_start(_gather_copies, srcs, [land_shape(s) for s in srcs], name=f"gather_ici_{tag}_start",
                          after=[tiny[0]] if after is None else [after])

    def gather_finish(handle, names, l, tag, after):
        srcs, lands = _ici_wait(_gather_copies, *handle[:4], after, name=f"gather_ici_{tag}_wait")
        use_gathered(l, names, _gather_d2d(srcs, lands, tag))

    def gather_d2d_start(handle, tag, after):
        srcs, lands = _ici_wait(_gather_copies, *handle[:4], after, name=f"gather_ici_{tag}_wait")
        return _ici_start(_d2d_copies, srcs, lands, name=f"gather_d2d_{tag}_start", sems_per_array=D2D_COPIES)

    def gather_d2d_finish(handle, names, l, tag, after):
        _, lands = _ici_wait(_d2d_copies, *handle[:4], after, name=f"gather_d2d_{tag}_wait")
        use_gathered(l, names, lands)

    tiny = _allgather_chips([w["meta"].reshape(2, N_META // 2, D_MODEL // N_CHIPS), w["conv_w"]])
    meta_full = jnp.concatenate([tiny[0][k].reshape(N_META, D_MODEL // N_CHIPS) for k in range(N_CHIPS)], axis=1)

    W = {n: [None] * DEPTH for n in MATMUL_W + ["w_in_p", "mla_w_uq_p", "mla_w_ukv_p"]}
    W["conv_w"] = jnp.concatenate([tiny[1][k] for k in range(N_CHIPS)], axis=-1)
    W["meta"] = meta_full
    for n in SMALL:
        W[n] = w[n]

    def use_gathered(l, names, lands):
        got = dict(zip(names, lands))
        cat = lambda n, cut=None: jnp.concatenate([got[n][k][..., :cut] for k in range(N_CHIPS)], axis=-1)
        for n in names:
            W[n][l] = got[n]
        if "w_in" in got:
            W["w_in_p"][l] = _pad_in_proj(cat("w_in", IN_SHARD))
            W["mla_w_uq_p"][l] = _regroup_uq(cat("mla_w_uq"))
            W["mla_w_ukv_p"][l] = _regroup_ukv(cat("mla_w_ukv"))

    def chunk_grads(G, names):
        def chunked(name, ungroup, width, pad):
            full = ungroup(G[name])
            return _pad_to(jnp.moveaxis(full.reshape(full.shape[0], N_CHIPS, width), 1, 0), 2, pad)
        special = {"mla_w_uq": ("mla_w_uq_p", _ungroup_uq, MLA_NOPE + MLA_ROPE, MLA_NOPE + MLA_ROPE),
                   "mla_w_ukv": ("mla_w_ukv_p", _ungroup_ukv, MLA_NOPE + MLA_V, MLA_NOPE + MLA_V)}
        return [_in_proj_grad_chunks(G["w_in_p"]) if n == "w_in" else chunked(*special[n]) if n in special else G[n]
                for n in names]

    def rs_start(G, names, tag):
        pairs = _rs_pair_sums(chunk_grads(G, names), names, tag)
        handle = _ici_start(_exchange_copies, [p[1] for p in pairs], _exchange_land_shapes(pairs),
                            name=f"rs_exchange_{tag}_start")
        return pairs, handle

    def swap_start(G, names, tag):
        gs = chunk_grads(G, names)
        return _ici_start(_swap_copies, gs, _swap_land_shapes(gs), name=f"rs_swap_{tag}_start")

    def exchange_start(swap_handle, names, tag, after):
        gs, r1 = _ici_wait(_swap_copies, *swap_handle[:4], after, name=f"rs_swap_{tag}_wait")
        pairs = _rs_add_pairs(gs, r1, names, tag)
        handle = _ici_start(_exchange_copies, [p[1] for p in pairs], _exchange_land_shapes(pairs),
                            name=f"rs_exchange_{tag}_start")
        return pairs, handle

    def rs_end(pairs, handle, names, tag, after):
        _, r2 = _ici_wait(_exchange_copies, *handle[:4], after, name=f"rs_exchange_{tag}_wait")
        return dict(zip(names, _rs_finish(pairs, r2, names, tag)))

    ffn1_w, mix_w, ffn2_w = MATMUL_W[:3], MATMUL_W[3:7], MATMUL_W[7:]
    g_a = gather_start(ffn1_w, 0, "l0_ffn1", None)
    g_b = gather_start(mix_w, 0, "l0_mix", g_a[4])
    g_c = gather_start(ffn2_w, 0, "l0_ffn2", g_b[4])
    g_l1 = gather_start(MATMUL_W, 1, "l1", g_c[4])
    token = g_l1[4]
    cosf, sins = _rope_tables()
    ln = lambda n, l: W[n][l][None]
    h = _embed(x[0] + token[0, 0], meta_full)
    h = (h, h.astype(BF16))
    gather_finish(g_a, ffn1_w, 0, "l0_ffn1", h[1])
    h1, s1 = _ffn_fwd(h, W, "ffn1", 0, ln("ln1_g", 0), ln("ln1_b", 0), "l0_ffn1")
    gather_finish(g_b, mix_w, 0, "l0_mix", h1[1])
    d_c = gather_d2d_start(g_c, "l0_ffn2", W["w_in_p"][0])
    h2, sm = _mixer_fwd(h1, W, 0, cosf, sins, after=[d_c[4]])
    gather_d2d_finish(d_c, ffn2_w, 0, "l0_ffn2", h2[1])
    d_l1 = gather_d2d_start(g_l1, "l1", W["ffn2_w_gate"][0])
    h, s2 = _ffn_fwd(h2, W, "ffn2", 0, ln("ln3_g", 0), ln("ln3_b", 0), "l0_ffn2", after=[d_l1[4]])
    saved0 = (s1, sm, s2)
    gather_d2d_finish(d_l1, MATMUL_W, 1, "l1", h[1])
    h, saved1 = _layer_fwd(h, W, 1, cosf, sins)
    tgt = jnp.concatenate([jnp.zeros((BLOCK, D_MODEL), F32), loss_target[0]], axis=0)
    dh, loss = _loss_head(h[0], tgt, name="loss_head")
    G = [None] * DEPTH
    dh, G[1] = _layer_bwd(dh, saved1, W, 1, cosf, sins)
    ffn2_w, mix_w, ffn1_w = MATMUL_W[7:], MATMUL_W[3:7], MATMUL_W[:3]
    sw_l1 = swap_start(G[1], MATMUL_W, "l1")
    G0 = {}
    dh, dg, db = _ffn_bwd(dh, s2, W, "ffn2", 0, ln("ln3_g", 0) + sw_l1[4][0, 0], ln("ln3_b", 0), G0, "l0_ffn2")
    G0["ln3_g"], G0["ln3_b"] = dg[0], db[0]
    pairs_l1, x_l1 = exchange_start(sw_l1, MATMUL_W, "l1", dh)
    sw_a = swap_start(G0, ffn2_w, "l0_ffn2")
    dh, Gm = _mixer_bwd(dh, sm, W, 0, cosf, sins, G0, zero=x_l1[4][0, 0] + sw_a[4][0, 0])
    G0.update(Gm)
    pairs_a, x_a = exchange_start(sw_a, ffn2_w, "l0_ffn2", dh)
    reduced1 = rs_end(pairs_l1, x_l1, MATMUL_W, "l1", dh)
    pairs_b, x_b = rs_start(G0, mix_w, "l0_mix")
    dh0, dg, db = _ffn_bwd(dh, s1, W, "ffn1", 0, ln("ln1_g", 0) + (x_a[4][0, 0] + x_b[4][0, 0]), ln("ln1_b", 0), G0,
                           "l0_ffn1")
    G0["ln1_g"], G0["ln1_b"] = dg[0], db[0]
    G[0] = G0
    reduced0 = rs_end(pairs_a, x_a, ffn2_w, "l0_ffn2", dh0)
    reduced0.update(rs_end(pairs_b, x_b, mix_w, "l0_mix", dh0))

    small_parts = [jnp.stack([G[l][n] for l in range(DEPTH)]).reshape(-1) for n in SMALL]
    small_parts += [jnp.stack([G[l]["conv_w"] for l in range(DEPTH)]).reshape(-1), dh0[PAD_ROWS:BLOCK].reshape(-1),
                    loss[0, :1]]
    sw_c = swap_start(G0, ffn1_w, "l0_ffn1")
    flat = jnp.concatenate(small_parts) + sw_c[4][0, 0]
    flat = jnp.pad(flat, (0, SMALL_ROWS * BLOCK - flat.shape[0]))
    red2d = _allreduce_small(flat.reshape(SMALL_ROWS, BLOCK))
    red = red2d.reshape(-1)
    pairs_c, x_c = exchange_start(sw_c, ffn1_w, "l0_ffn1", red2d)
    grads, off = {}, 0
    for n in SMALL:
        size = int(np.prod(w[n].shape))
        grads[n] = red[off:off + size].reshape(w[n].shape)
        off += size
    conv_full = red[off:off + DEPTH * SSD_CONV * 768].reshape(DEPTH, SSD_CONV, 768)
    off += DEPTH * SSD_CONV * 768
    dmeta_full = red[off:off + N_META * D_MODEL].reshape(N_META, D_MODEL)
    off += N_META * D_MODEL
    loss_out = red[off]
    grads["conv_w"] = _chip_cols(conv_full, chip, 768 // N_CHIPS)
    grads["meta"] = _chip_cols(dmeta_full, chip, D_MODEL // N_CHIPS)

    delta, new_m, new_v = {}, {}, {}

    def adamw_matmul_weights(names, after):
        done = []
        for n in names:
            gs = [reduced0[n], reduced1[n]]
            if n.endswith("w_gate") or n.endswith("w_up"):
                res = _adamw(tr(w[n]), gs, tr(m[n]), tr(v[n]), name=f"adamw_{n}", after=after)
                grads[n], delta[n], new_m[n], new_v[n] = [tr(r) for r in res]
            else:
                res = _adamw(w[n], gs, m[n], v[n], name=f"adamw_{n}", after=after)
                grads[n], delta[n], new_m[n], new_v[n] = res
            done.append(res[1])
        return done

    early_done = adamw_matmul_weights(ffn2_w + mix_w, [x_c[4]])
    rest = [n for n in WEIGHTS if n not in MATMUL_W]

    def pack_small(d):
        f = jnp.concatenate([d[n].reshape(-1) for n in rest])
        tot = -(-f.shape[0] // (8 * BLOCK)) * 8 * BLOCK
        return jnp.pad(f, (0, tot - f.shape[0])).reshape(-1, BLOCK)

    _, d2, m2, v2 = _adamw(pack_small(w), [pack_small(grads)], pack_small(m), pack_small(v), name="adamw_small",
                           after=[x_c[4]])
    reduced0.update(rs_end(pairs_c, x_c, ffn1_w, "l0_ffn1", [d2] + early_done))
    adamw_matmul_weights(ffn1_w, [])
    off = 0
    for n in rest:
        size = int(np.prod(w[n].shape))
        for dst, src in ((delta, d2), (new_m, m2), (new_v, v2)):
            dst[n] = src.reshape(-1)[off:off + size].reshape(w[n].shape)
        off += size

    grad_x = dh0[BLOCK:][None]
    return (loss_out, grad_x, *[grads[n] for n in WEIGHTS], *[delta[n] for n in WEIGHTS],
            *[new_m[n] for n in WEIGHTS], *[new_v[n] for n in WEIGHTS])
```

```python
import functools

import numpy as np
import jax
import jax.numpy as jnp
from jax import lax
from jax.experimental import pallas as pl
from jax.experimental.pallas import tpu as pltpu

F32 = jnp.float32
BF16 = jnp.bfloat16
MESH = pl.DeviceIdType.MESH

D_MODEL = 1024
SEQ = 2048
N_META = 16
BLOCK = 128
PAD_ROWS = 112
LP = PAD_ROWS + N_META + SEQ
N_CHUNK = LP // BLOCK
DEPTH = 2
D_FF = 2816
N_CHIPS = 4
FF_SHARD = D_FF // N_CHIPS
HP = 768
FP = N_CHIPS * HP
SSD_HEADS, SSD_HD, SSD_D, SSD_GROUPS, SSD_STATE, SSD_CONV = 8, 64, 512, 2, 64, 4
FOX_HEADS, FOX_HD, FOX_D = 4, 64, 256
MLA_HEADS, MLA_Q_LORA, MLA_KV_LORA, MLA_NOPE, MLA_ROPE, MLA_V, MLA_D = 4, 256, 128, 64, 32, 64, 256
ROPE_HALF = MLA_ROPE // 2
ROPE_THETA = 10000.0
N_IN = 2476
IN_SHARD = N_IN // N_CHIPS
IN_SHARD_P = 640
ALPHA = (2 * DEPTH) ** 0.25
EPS = 1e-5
ADAM_LR, ADAM_B1, ADAM_B2, ADAM_EPS, ADAM_WD, ADAM_STEP = 0.001, 0.9, 0.999, 1e-08, 0.01, 10
NEG = -1e30
TM = 544

VMEM_LIMIT_BYTES = 56 * 1024 * 1024

PC_Z, PC_XBC, PC_FQ, PC_FK, PC_FV, PC_CQ, PC_CKV, PC_DT, PC_FR, PC_KR, PC_END = (
    0, 512, 1280, 1536, 1792, 2048, 2304, 2432, 2560, 2688, 2816)
OC_Z, OC_XBC, OC_DT, OC_FQ, OC_FK, OC_FV, OC_FR, OC_CQ, OC_CKV, OC_KR = (
    0, 512, 1280, 1288, 1544, 1800, 2056, 2060, 2316, 2444)


def _cparams(sem=None):
    return pltpu.CompilerParams(dimension_semantics=sem, vmem_limit_bytes=VMEM_LIMIT_BYTES)


def _tile(n, cap, mult):
    best = None
    for t in range(mult, min(n, cap) + 1, mult):
        if n % t == 0:
            best = t
    return best if best is not None else n


def _bs(shape, fn):
    return pl.BlockSpec(shape, fn)


ANY = pl.BlockSpec(memory_space=pl.ANY)


def _dims(ca, cb):
    return (((ca,), (cb,)), ((), ()))


def _raw_bdot(a, b, ca, cb):
    return lax.dot_general(a.astype(BF16), b.astype(BF16), _dims(ca, cb), preferred_element_type=F32)


def _mm_core(a, b, *, a_spec, b_spec, o_spec, grid, out_shape, ca, cb, name, add=None, after=()):
    nk = grid[2]
    has_add = add is not None
    acc_shape = tuple(d for d in o_spec.block_shape if d is not None)

    def body(*refs):
        a_ref, b_ref = refs[0], refs[1]
        add_ref = refs[2] if has_add else None
        o_ref, acc_ref = refs[-2], refs[-1]
        k = pl.program_id(2)

        @pl.when(k == 0)
        def _():
            acc_ref[...] = jnp.zeros_like(acc_ref)

        acc_ref[...] += _raw_bdot(a_ref[...], b_ref[...], ca, cb)

        @pl.when(k == nk - 1)
        def _():
            r = acc_ref[...]
            if has_add:
                r = r + add_ref[...]
            o_ref[...] = r

    ins = [a, b] + ([add] if has_add else []) + list(after)
    in_specs = [a_spec, b_spec] + ([o_spec] if has_add else []) + [ANY] * len(after)
    return pl.pallas_call(
        body, name=name, grid=grid, in_specs=in_specs, out_specs=o_spec,
        out_shape=jax.ShapeDtypeStruct(out_shape, F32), scratch_shapes=[pltpu.VMEM(acc_shape, F32)],
        compiler_params=_cparams(("parallel", "parallel", "arbitrary")),
    )(*ins)


MM_VMEM_BUDGET = 40 * 1024 * 1024


def _divisors(n, mult):
    return [t for t in range(mult, n + 1, mult) if n % t == 0] or [n]


def _pick_tiles(M, N, K, a_bytes, b_bytes, ta, has_add):
    best = None
    for tm in _divisors(M, 128 if ta else 16):
        for tn in _divisors(N, 128):
            vmem = 2 * tm * K * a_bytes + 2 * K * tn * b_bytes + (3 + 2 * int(has_add)) * tm * tn * 4
            if vmem <= MM_VMEM_BUDGET:
                key = ((M // tm) * (N // tn), -tn)
                if best is None or key < best[0]:
                    best = (key, tm, tn)
    assert best is not None, (M, N, K)
    return best[1], best[2], K


def _mm(a, b, *, ta=False, tb=False, add=None, name, after=()):
    if ta:
        K, M = a.shape
    else:
        M, K = a.shape
    if tb:
        N, Kb = b.shape
    else:
        Kb, N = b.shape
    assert K == Kb, (a.shape, b.shape, ta, tb)
    tm, tn, tk = _pick_tiles(M, N, K, a.dtype.itemsize, b.dtype.itemsize, ta, add is not None)
    a_spec = _bs((tk, tm), lambda i, j, k: (k, i)) if ta else _bs((tm, tk), lambda i, j, k: (i, k))
    b_spec = _bs((tn, tk), lambda i, j, k: (j, k)) if tb else _bs((tk, tn), lambda i, j, k: (k, j))
    return _mm_core(a, b, a_spec=a_spec, b_spec=b_spec, o_spec=_bs((tm, tn), lambda i, j, k: (i, j)),
                    grid=(M // tm, N // tn, K // tk), out_shape=(M, N), ca=0 if ta else 1, cb=1 if tb else 0,
                    name=name, add=add, after=after)


def _row_entry(r, ncol):
    if isinstance(r, tuple):
        return r
    return r, r.shape[1] // ncol, 0


def _rowwise(fn, rows, pars, out_cols, *, name, tile, ncol=1, out_dtypes=None):
    rows = [_row_entry(r, ncol) for r in rows]
    L = rows[0][0].shape[0]
    nr, npar = len(rows), len(pars)
    in_specs = [_bs((tile, w), lambda g, i, o=o: (i, o + g)) for _, w, o in rows]
    in_specs += [_bs((p.shape[0], p.shape[1] // ncol), lambda g, i: (0, g)) for p in pars]
    out_specs = [_bs((tile, c // ncol), lambda g, i: (i, g)) for c in out_cols]

    def body(*refs):
        ins, outs = refs[:nr + npar], refs[nr + npar:]
        row0 = pl.program_id(1) * tile
        res = fn(row0, *[r[...] for r in ins])
        for o, v in zip(outs, res):
            o[...] = v.astype(o.dtype)

    return pl.pallas_call(
        body, name=name, grid=(ncol, L // tile), in_specs=in_specs, out_specs=out_specs,
        out_shape=[jax.ShapeDtypeStruct((L, c), d) for c, d in zip(out_cols, out_dtypes or [F32] * len(out_cols))],
        compiler_params=_cparams(("parallel", "parallel")),
    )(*[r[0] for r in rows], *pars)


def _rowwise_bwd(fn, rows, pars, douts, *, name, tile, ncol=1, row_grad=None, grad_dtypes=None):
    rows = [_row_entry(r, ncol) for r in rows]
    L = rows[0][0].shape[0]
    nr, npar, nd = len(rows), len(pars), len(douts)
    row_grad = [True] * nr if row_grad is None else row_grad
    in_specs = [_bs((tile, w), lambda g, i, o=o: (i, o + g)) for _, w, o in rows]
    in_specs += [_bs((p.shape[0], p.shape[1] // ncol), lambda g, i: (0, g)) for p in pars]
    in_specs += [_bs((tile, d.shape[1] // ncol), lambda g, i: (i, g)) for d in douts]
    g_widths = [w * ncol for (_, w, _), f in zip(rows, row_grad) if f]
    out_specs = [_bs((tile, w // ncol), lambda g, i: (i, g)) for w in g_widths]
    out_specs += [_bs((p.shape[0], p.shape[1] // ncol), lambda g, i: (0, g)) for p in pars]
    out_shape = [jax.ShapeDtypeStruct((L, w), d) for w, d in zip(g_widths, grad_dtypes or [F32] * len(g_widths))]
    out_shape += [jax.ShapeDtypeStruct(p.shape, F32) for p in pars]

    def body(*refs):
        ins = refs[:nr + npar]
        dos = refs[nr + npar:nr + npar + nd]
        outs = refs[nr + npar + nd:]
        i = pl.program_id(1)
        row0 = i * tile
        _, vjp = jax.vjp(lambda *a: tuple(fn(row0, *a)), *[r[...] for r in ins])
        grads = vjp(tuple(d[...].astype(F32) for d in dos))
        o = 0
        for j in range(nr):
            if row_grad[j]:
                outs[o][...] = grads[j].astype(outs[o].dtype)
                o += 1
        for j in range(npar):
            g, ref = grads[nr + j], outs[o + j]

            @pl.when(i == 0)
            def _(g=g, ref=ref):
                ref[...] = g

            @pl.when(i > 0)
            def _(g=g, ref=ref):
                ref[...] += g

    res = pl.pallas_call(
        body, name=name, grid=(ncol, L // tile), in_specs=in_specs, out_specs=out_specs, out_shape=out_shape,
        compiler_params=_cparams(("parallel", "arbitrary")),
    )(*[r[0] for r in rows], *pars, *douts)
    return res[:len(g_widths)], res[len(g_widths):]


def _sigmoid(x):
    return 1.0 / (1.0 + jnp.exp(-x))


def _softplus(x):
    return jnp.maximum(x, 0.0) + jnp.log(1.0 + jnp.exp(-jnp.abs(x)))


def _silu(x):
    return x * _sigmoid(x)


def _make_res_ln_fn(scale):
    def fn(row0, h, o, gam, bet):
        pre = ALPHA * h + scale * o
        mu = jnp.mean(pre, axis=-1, keepdims=True)
        xc = pre - mu
        var = jnp.mean(xc * xc, axis=-1, keepdims=True)
        return (xc * lax.rsqrt(var + EPS) * gam + bet,)
    return fn


def _ssd_post_fn(row0, y, xs, z, dskip, normg):
    v = (y + dskip * xs) * _silu(z)
    v = v * lax.rsqrt(jnp.mean(v * v, axis=-1, keepdims=True) + EPS)
    return (v * normg,)


def _mla_norm_fn(row0, cq, ckv, gq, gkv):
    qn = cq * lax.rsqrt(jnp.mean(cq * cq, axis=-1, keepdims=True) + EPS) * gq
    cn = ckv * lax.rsqrt(jnp.mean(ckv * ckv, axis=-1, keepdims=True) + EPS) * gkv
    return qn, cn


def _rope_fn(row0, q, k, cosf, sins):
    return (q * cosf + pltpu.roll(q, 64, 1) * sins, k * cosf + pltpu.roll(k, 64, 1) * sins)


def _rope_t_fn(row0, gq, gk, cosf, sins):
    return (gq * cosf + pltpu.roll(gq * sins, 64, 1), gk * cosf + pltpu.roll(gk * sins, 64, 1))


def _conv_fwd(x, x_off, w, b, *, name):
    C = w.shape[1]

    def body(x_ref, w_ref, b_ref, o_ref):
        rows = lax.broadcasted_iota(jnp.int32, (LP, BLOCK), 0)
        xv = jnp.where(rows >= PAD_ROWS, x_ref[...], 0.0)
        acc = b_ref[...] + w_ref[3:4, :] * xv
        for k in range(SSD_CONV - 1):
            acc = acc + w_ref[k:k + 1, :] * pltpu.roll(xv, SSD_CONV - 1 - k, 0)
        o_ref[...] = _silu(acc)

    return pl.pallas_call(
        body, name=name, grid=(C // BLOCK,),
        in_specs=[_bs((LP, BLOCK), lambda j: (0, j + x_off)), _bs((SSD_CONV, BLOCK), lambda j: (0, j)),
                  _bs((1, BLOCK), lambda j: (0, j))],
        out_specs=_bs((LP, BLOCK), lambda j: (0, j)),
        out_shape=jax.ShapeDtypeStruct((LP, C), F32), compiler_params=_cparams(("parallel",)),
    )(x, w, b)


def _conv_bwd(x, x_off, w, b, dout, *, name):
    C = w.shape[1]

    def body(x_ref, w_ref, b_ref, do_ref, dx_ref, dw_ref, db_ref):
        rows = lax.broadcasted_iota(jnp.int32, (LP, BLOCK), 0)
        real = rows >= PAD_ROWS
        xv = jnp.where(real, x_ref[...], 0.0)
        shifted = [pltpu.roll(xv, SSD_CONV - 1 - k, 0) for k in range(SSD_CONV - 1)] + [xv]
        acc = b_ref[...]
        for k in range(SSD_CONV):
            acc = acc + w_ref[k:k + 1, :] * shifted[k]
        sig = _sigmoid(acc)
        dacc = jnp.where(real, do_ref[...] * (sig * (1.0 + acc * (1.0 - sig))), 0.0)
        db_ref[...] = jnp.sum(dacc, axis=0, keepdims=True)
        dx = w_ref[3:4, :] * dacc
        for k in range(SSD_CONV):
            dw_ref[k:k + 1, :] = jnp.sum(dacc * shifted[k], axis=0, keepdims=True)
            if k < SSD_CONV - 1:
                dx = dx + w_ref[k:k + 1, :] * pltpu.roll(dacc, LP - (SSD_CONV - 1 - k), 0)
        dx_ref[...] = jnp.where(real, dx, 0.0)

    return pl.pallas_call(
        body, name=name, grid=(C // BLOCK,),
        in_specs=[_bs((LP, BLOCK), lambda j: (0, j + x_off)), _bs((SSD_CONV, BLOCK), lambda j: (0, j)),
                  _bs((1, BLOCK), lambda j: (0, j)), _bs((LP, BLOCK), lambda j: (0, j))],
        out_specs=[_bs((LP, BLOCK), lambda j: (0, j)), _bs((SSD_CONV, BLOCK), lambda j: (0, j)),
                   _bs((1, BLOCK), lambda j: (0, j))],
        out_shape=[jax.ShapeDtypeStruct((LP, C), F32), jax.ShapeDtypeStruct((SSD_CONV, C), F32),
                   jax.ShapeDtypeStruct((1, C), F32)],
        compiler_params=_cparams(("parallel",)),
    )(x, w, b, dout)


_BDIMS = {"nn": (((2,), (1,)), ((0,), (0,))), "nt": (((2,), (2,)), ((0,), (0,))), "tn": (((1,), (1,)), ((0,), (0,)))}


def _raw_bdot3(a, b, mode):
    return lax.dot_general(a.astype(BF16), b.astype(BF16), _BDIMS[mode], preferred_element_type=F32)


@functools.partial(jax.custom_vjp, nondiff_argnums=(2,))
def _bdot3(a, b, mode):
    return _raw_bdot3(a, b, mode)


def _bdot3_fwd(a, b, mode):
    return _raw_bdot3(a, b, mode), (a, b)


def _bdot3_bwd(mode, res, g):
    a, b = res
    if mode == "nn":
        return _raw_bdot3(g, b, "nt"), _raw_bdot3(a, g, "tn")
    if mode == "nt":
        return _raw_bdot3(g, b, "nn"), _raw_bdot3(g, a, "tn")
    return _raw_bdot3(b, g, "nt"), _raw_bdot3(a, g, "nn")


_bdot3.defvjp(_bdot3_fwd, _bdot3_bwd)


def _ssd_chunk(x, bm, cm, dt, dtt, alog, prev):
    rep = SSD_HEADS // SSD_GROUPS
    per_head = lambda t: jnp.broadcast_to(t[:, None], (SSD_GROUPS, rep) + t.shape[1:]).reshape((SSD_HEADS,) + t.shape[1:])
    bm, cm = per_head(bm), per_head(cm)
    lane_h = lax.broadcasted_iota(jnp.int32, (1, BLOCK), 1)
    row_h = lax.broadcasted_iota(jnp.int32, (BLOCK, 1), 0)
    dtc = jnp.stack([jnp.sum(jnp.where(lane_h == h, dt, 0.0), axis=1, keepdims=True) for h in range(SSD_HEADS)])
    dtr = jnp.stack([jnp.sum(jnp.where(row_h == h, dtt, 0.0), axis=0, keepdims=True) for h in range(SSD_HEADS)])
    lane = lax.broadcasted_iota(jnp.int32, alog.shape, 2)
    a_neg = -jnp.exp(jnp.sum(jnp.where(lane == 0, alog, 0.0), axis=2, keepdims=True))
    ac_in = dtc * a_neg
    ar_in = dtr * a_neg
    li = lax.broadcasted_iota(jnp.int32, (1, BLOCK, BLOCK), 1)
    si = lax.broadcasted_iota(jnp.int32, (1, BLOCK, BLOCK), 2)
    causal = li >= si
    acum_c = jnp.sum(jnp.where(causal, ar_in, 0.0), axis=2, keepdims=True)
    acum_r = jnp.sum(jnp.where(li <= si, ac_in, 0.0), axis=1, keepdims=True)
    total = jnp.sum(ar_in, axis=2, keepdims=True)
    seg = jnp.exp(jnp.where(causal, acum_c - acum_r, NEG))
    xdt = x * dtc
    cb = _bdot3(cm, bm, "nt")
    y = _bdot3(cb * seg, xdt, "nn") + _bdot3(cm, prev, "nt") * jnp.exp(acum_c)
    st = _bdot3(xdt, bm * jnp.exp(total - acum_c), "tn")
    return y, prev * jnp.exp(total) + st


def _ssd_dt_fwd(raw, raw_blk, bias, *, name):
    def body(raw_ref, b_ref, dt_ref, dtt_ref):
        rows = lax.broadcasted_iota(jnp.int32, (LP, BLOCK), 0)
        dt = jnp.where(rows >= PAD_ROWS, _softplus(raw_ref[...] + b_ref[...]), 0.0)
        dt_ref[...] = dt
        dtt_ref[...] = dt.T

    return pl.pallas_call(
        body, name=name, grid=(1,),
        in_specs=[_bs((LP, BLOCK), lambda j: (0, raw_blk)), _bs((1, BLOCK), lambda j: (0, 0))],
        out_specs=[_bs((LP, BLOCK), lambda j: (0, 0)), _bs((BLOCK, LP), lambda j: (0, 0))],
        out_shape=[jax.ShapeDtypeStruct((LP, BLOCK), F32), jax.ShapeDtypeStruct((BLOCK, LP), F32)],
        compiler_params=_cparams(("arbitrary",)),
    )(raw, bias)


def _ssd_dt_bwd(raw, raw_blk, bias, ddt, ddtt, *, name):
    def body(raw_ref, b_ref, ddt_ref, ddtt_ref, draw_ref, db_ref):
        rows = lax.broadcasted_iota(jnp.int32, (LP, BLOCK), 0)
        g = ddt_ref[...] + ddtt_ref[...].T
        draw = jnp.where(rows >= PAD_ROWS, g * _sigmoid(raw_ref[...] + b_ref[...]), 0.0)
        draw_ref[...] = draw
        db_ref[...] = jnp.sum(draw, axis=0, keepdims=True)

    return pl.pallas_call(
        body, name=name, grid=(1,),
        in_specs=[_bs((LP, BLOCK), lambda j: (0, raw_blk)), _bs((1, BLOCK), lambda j: (0, 0)),
                  _bs((LP, BLOCK), lambda j: (0, 0)), _bs((BLOCK, LP), lambda j: (0, 0))],
        out_specs=[_bs((LP, BLOCK), lambda j: (0, 0)), _bs((1, BLOCK), lambda j: (0, 0))],
        out_shape=[jax.ShapeDtypeStruct((LP, BLOCK), F32), jax.ShapeDtypeStruct((1, BLOCK), F32)],
        compiler_params=_cparams(("arbitrary",)),
    )(raw, bias, ddt, ddtt)


def _ssd_specs(rev):
    ci = (lambda c: N_CHUNK - 1 - c) if rev else (lambda c: c)
    x_spec = _bs((SSD_HEADS, BLOCK, SSD_HD), lambda c: (0, ci(c), 0))
    g_spec = _bs((SSD_GROUPS, BLOCK, SSD_STATE), lambda c: (0, ci(c), 0))
    dtc_spec = _bs((BLOCK, BLOCK), lambda c: (ci(c), 0))
    dtr_spec = _bs((BLOCK, BLOCK), lambda c: (0, ci(c)))
    al_spec = _bs((SSD_HEADS, 1, BLOCK), lambda c: (0, 0, 0))
    st_spec = _bs((None, SSD_HEADS, SSD_HD, SSD_STATE), lambda c: (ci(c), 0, 0, 0))
    return x_spec, g_spec, dtc_spec, dtr_spec, al_spec, st_spec


def _ssd_fwd(x, bm, cm, dtc, dtr, alog, *, name):
    x_spec, g_spec, dtc_spec, dtr_spec, al_spec, st_spec = _ssd_specs(False)

    def body(x_ref, b_ref, c_ref, dtc_ref, dtr_ref, al_ref, y_ref, prev_ref, state):
        @pl.when(pl.program_id(0) == 0)
        def _():
            state[...] = jnp.zeros_like(state)

        prev = state[...]
        prev_ref[...] = prev
        y, new = _ssd_chunk(x_ref[...], b_ref[...], c_ref[...], dtc_ref[...], dtr_ref[...], al_ref[...], prev)
        y_ref[...] = y
        state[...] = new

    return pl.pallas_call(
        body, name=name, grid=(N_CHUNK,),
        in_specs=[x_spec, g_spec, g_spec, dtc_spec, dtr_spec, al_spec], out_specs=[x_spec, st_spec],
        out_shape=[jax.ShapeDtypeStruct((SSD_HEADS, LP, SSD_HD), F32),
                   jax.ShapeDtypeStruct((N_CHUNK, SSD_HEADS, SSD_HD, SSD_STATE), F32)],
        scratch_shapes=[pltpu.VMEM((SSD_HEADS, SSD_HD, SSD_STATE), F32)],
        compiler_params=_cparams(("arbitrary",)),
    )(x, bm, cm, dtc, dtr, alog)


def _ssd_bwd(x, bm, cm, dtc, dtr, alog, prevs, dy, *, name):
    x_spec, g_spec, dtc_spec, dtr_spec, al_spec, st_spec = _ssd_specs(True)

    def body(x_ref, b_ref, c_ref, dtc_ref, dtr_ref, al_ref, prev_ref, dy_ref,
             dx_ref, db_ref, dc_ref, ddtc_ref, ddtr_ref, dal_ref, dstate):
        c = pl.program_id(0)

        @pl.when(c == 0)
        def _():
            dstate[...] = jnp.zeros_like(dstate)

        _, vjp = jax.vjp(_ssd_chunk, x_ref[...], b_ref[...], c_ref[...], dtc_ref[...], dtr_ref[...], al_ref[...],
                         prev_ref[...])
        dx, db, dc, ddtc, ddtr, dal, dprev = vjp((dy_ref[...], dstate[...]))
        dx_ref[...] = dx
        db_ref[...] = db
        dc_ref[...] = dc
        ddtc_ref[...] = ddtc
        ddtr_ref[...] = ddtr
        dstate[...] = dprev

        @pl.when(c == 0)
        def _():
            dal_ref[...] = dal

        @pl.when(c > 0)
        def _():
            dal_ref[...] += dal

    hs = jax.ShapeDtypeStruct((SSD_HEADS, LP, SSD_HD), F32)
    gs = jax.ShapeDtypeStruct((SSD_GROUPS, LP, SSD_STATE), F32)
    return pl.pallas_call(
        body, name=name, grid=(N_CHUNK,),
        in_specs=[x_spec, g_spec, g_spec, dtc_spec, dtr_spec, al_spec, st_spec, x_spec],
        out_specs=[x_spec, g_spec, g_spec, dtc_spec, dtr_spec, al_spec],
        out_shape=[hs, gs, gs, jax.ShapeDtypeStruct((LP, BLOCK), F32),
                   jax.ShapeDtypeStruct((BLOCK, LP), F32), jax.ShapeDtypeStruct((SSD_HEADS, 1, BLOCK), F32)],
        scratch_shapes=[pltpu.VMEM((SSD_HEADS, SSD_HD, SSD_STATE), F32)],
        compiler_params=_cparams(("arbitrary",)),
    )(x, bm, cm, dtc, dtr, alog, prevs, dy)


def _tri_dot(tri, v):
    hi = v.astype(BF16)
    r1 = v - hi.astype(F32)
    mid = r1.astype(BF16)
    lo = (r1 - mid.astype(F32)).astype(BF16)
    t = tri.astype(BF16)
    d = lambda p: lax.dot_general(t, p, _dims(1, 0), preferred_element_type=F32)
    return d(hi) + d(mid) + d(lo)


def _fox_gate_fwd(raw, raw_blk, bias, *, name):
    def body(raw_ref, b_ref, c_ref, ct_ref):
        li = lax.broadcasted_iota(jnp.int32, (BLOCK, BLOCK), 0)
        si = lax.broadcasted_iota(jnp.int32, (BLOCK, BLOCK), 1)
        tri = jnp.where(li >= si, 1.0, 0.0)
        carry = jnp.zeros((1, BLOCK), F32)
        for j in range(N_CHUNK):
            r = slice(j * BLOCK, (j + 1) * BLOCK)
            lf = jnp.where(j * BLOCK + li >= PAD_ROWS, -_softplus(-(raw_ref[r, :] + b_ref[...])), 0.0)
            cv = _tri_dot(tri, lf) + carry
            c_ref[r, :] = cv
            ct_ref[:, r] = cv.T
            carry = carry + jnp.sum(lf, axis=0, keepdims=True)

    return pl.pallas_call(
        body, name=name, grid=(1,),
        in_specs=[_bs((LP, BLOCK), lambda j: (0, raw_blk)), _bs((1, BLOCK), lambda j: (0, 0))],
        out_specs=[_bs((LP, BLOCK), lambda j: (0, 0)), _bs((BLOCK, LP), lambda j: (0, 0))],
        out_shape=[jax.ShapeDtypeStruct((LP, BLOCK), F32), jax.ShapeDtypeStruct((BLOCK, LP), F32)],
        compiler_params=_cparams(("arbitrary",)),
    )(raw, bias)


def _fox_gate_bwd(raw, raw_blk, bias, dc, dct, *, name):
    def body(raw_ref, b_ref, dc_ref, dct_ref, draw_ref, db_ref):
        li = lax.broadcasted_iota(jnp.int32, (BLOCK, BLOCK), 0)
        si = lax.broadcasted_iota(jnp.int32, (BLOCK, BLOCK), 1)
        tri_t = jnp.where(li <= si, 1.0, 0.0)
        carry = jnp.zeros((1, BLOCK), F32)
        dsum = jnp.zeros((1, BLOCK), F32)
        for j in reversed(range(N_CHUNK)):
            r = slice(j * BLOCK, (j + 1) * BLOCK)
            dcv = dc_ref[r, :] + dct_ref[:, r].T
            dlf = _tri_dot(tri_t, dcv) + carry
            carry = carry + jnp.sum(dcv, axis=0, keepdims=True)
            draw = jnp.where(j * BLOCK + li >= PAD_ROWS, dlf * (1.0 - _sigmoid(raw_ref[r, :] + b_ref[...])), 0.0)
            draw_ref[r, :] = draw
            dsum = dsum + jnp.sum(draw, axis=0, keepdims=True)
        db_ref[...] = dsum

    return pl.pallas_call(
        body, name=name, grid=(1,),
        in_specs=[_bs((LP, BLOCK), lambda j: (0, raw_blk)), _bs((1, BLOCK), lambda j: (0, 0)),
                  _bs((LP, BLOCK), lambda j: (0, 0)), _bs((BLOCK, LP), lambda j: (0, 0))],
        out_specs=[_bs((LP, BLOCK), lambda j: (0, 0)), _bs((1, BLOCK), lambda j: (0, 0))],
        out_shape=[jax.ShapeDtypeStruct((LP, BLOCK), F32), jax.ShapeDtypeStruct((1, BLOCK), F32)],
        compiler_params=_cparams(("arbitrary",)),
    )(raw, bias, dc, dct)


ATT_W = 256
ATT_QB = 272
ATT_STEPS = LP // ATT_QB
ATT_KEYS = (640, 1152, 1664, LP)
ATT_BLOCKS_PER_CLASS = ATT_STEPS // len(ATT_KEYS)


def _lane_head(width, per, mod=None):
    lane = lax.broadcasted_iota(jnp.int32, (1, width), 1)
    if mod is not None:
        lane = lane % mod
    return lane // per


def _attn_mask(i, kw):
    r = i * ATT_QB + lax.broadcasted_iota(jnp.int32, (ATT_QB, kw), 0)
    c = lax.broadcasted_iota(jnp.int32, (ATT_QB, kw), 1)
    return (c <= r) & ((c >= PAD_ROWS) | (r < PAD_ROWS))


def _attn_by_key_class(i, fn):
    for p, kw in enumerate(ATT_KEYS):
        @pl.when(i // ATT_BLOCKS_PER_CLASS == p)
        def _(kw=kw):
            fn(kw)


def _attn_specs(q, k, v, bias, rope):
    qspec = lambda blk, w=ATT_W: _bs((ATT_QB, w), lambda i: (i, blk))
    fspec = lambda blk, w=ATT_W: _bs((LP, w), lambda i: (0, blk))
    ins = [q[0], k[0], v[0]]
    specs = [qspec(q[1]), fspec(k[1]), fspec(v[1])]
    if bias is not None:
        ins += [bias[0], bias[1]]
        specs += [qspec(0, BLOCK), _bs((BLOCK, LP), lambda i: (0, 0))]
    if rope is not None:
        ins += [rope[0][0], rope[1][0]]
        specs += [qspec(rope[0][1], BLOCK), fspec(rope[1][1], BLOCK)]
    return ins, specs, qspec, fspec


def _attn_fwd(q, k, v, *, scale, name, bias=None, rope=None):
    ins, specs, qspec, fspec = _attn_specs(q, k, v, bias, rope)
    has_bias, has_rope = bias is not None, rope is not None

    def body(*refs):
        it = iter(refs)
        q_ref, k_ref, v_ref = next(it), next(it), next(it)
        if has_bias:
            c_ref, ct_ref = next(it), next(it)
        if has_rope:
            qr_ref, kr_ref = next(it), next(it)
        o_ref, lse_ref = next(it), next(it)
        i = pl.program_id(0)

        def block(kw):
            ok = _attn_mask(i, kw)
            qv, kv, vv = q_ref[...].astype(BF16), k_ref[0:kw, :].astype(BF16), v_ref[0:kw, :].astype(BF16)
            hid, l128 = _lane_head(ATT_W, FOX_HD), _lane_head(BLOCK, 1)
            if has_rope:
                rid = _lane_head(BLOCK, ROPE_HALF, 64)
                qrv, krv = qr_ref[...].astype(BF16), kr_ref[0:kw, :].astype(BF16)
            def head(h, carry):
                o_acc, lse_acc = carry
                s = _raw_bdot(jnp.where(hid == h, qv, 0.0), kv, 1, 1)
                if has_rope:
                    s = s + _raw_bdot(jnp.where(rid == h, qrv, 0.0), krv, 1, 1)
                s = s * scale
                if has_bias:
                    cq = jnp.sum(jnp.where(l128 == h, c_ref[...], 0.0), axis=1, keepdims=True)
                    s = s + (cq - ct_ref[pl.ds(h, 1), 0:kw])
                s = jnp.where(ok, s, NEG)
                m = jnp.max(s, axis=1, keepdims=True)
                p = jnp.exp(s - m)
                l = jnp.sum(p, axis=1, keepdims=True)
                o_acc = jnp.where(hid == h, _raw_bdot(p, vv, 1, 0) / l, o_acc)
                lse_acc = jnp.where(l128 == h, m + jnp.log(l), lse_acc)
                return o_acc, lse_acc

            o_acc, lse_acc = lax.fori_loop(
                0, FOX_HEADS, head, (jnp.zeros((ATT_QB, ATT_W), F32), jnp.zeros((ATT_QB, BLOCK), F32)), unroll=True)
            o_ref[...] = o_acc
            lse_ref[...] = lse_acc

        _attn_by_key_class(i, block)

    return pl.pallas_call(
        body, name=name, grid=(ATT_STEPS,), in_specs=specs, out_specs=[qspec(0), qspec(0, BLOCK)],
        out_shape=[jax.ShapeDtypeStruct((LP, ATT_W), F32), jax.ShapeDtypeStruct((LP, BLOCK), F32)],
        compiler_params=_cparams(("parallel",)),
    )(*ins)


def _attn_bwd(q, k, v, o, lse, do, *, scale, name, bias=None, rope=None):
    ins, specs, qspec, fspec = _attn_specs(q, k, v, bias, rope)
    has_bias, has_rope = bias is not None, rope is not None
    ins += [o, lse, do[0]]
    specs += [qspec(0), qspec(0, BLOCK), qspec(do[1])]

    def body(*refs):
        it = iter(refs)
        q_ref, k_ref, v_ref = next(it), next(it), next(it)
        if has_bias:
            c_ref, ct_ref = next(it), next(it)
        if has_rope:
            qr_ref, kr_ref = next(it), next(it)
        o_ref, lse_ref, do_ref = next(it), next(it), next(it)
        dq_ref, dk_ref, dv_ref = next(it), next(it), next(it)
        if has_bias:
            dc_ref, dct_ref = next(it), next(it)
        if has_rope:
            dqr_ref, dkr_ref = next(it), next(it)
        i = pl.program_id(0)

        @pl.when(i == 0)
        def _():
            dk_ref[...] = jnp.zeros_like(dk_ref)
            dv_ref[...] = jnp.zeros_like(dv_ref)
            if has_rope:
                dkr_ref[...] = jnp.zeros_like(dkr_ref)
            if has_bias:
                dct_ref[...] = jnp.zeros_like(dct_ref)

        def block(kw):
            ok = _attn_mask(i, kw)
            qv, kv, vv = q_ref[...].astype(BF16), k_ref[0:kw, :].astype(BF16), v_ref[0:kw, :].astype(BF16)
            dov, lsev = do_ref[...], lse_ref[...]
            dov_ov = dov * o_ref[...]
            dov = dov.astype(BF16)
            hid, l128 = _lane_head(ATT_W, FOX_HD), _lane_head(BLOCK, 1)
            if has_rope:
                rid = _lane_head(BLOCK, ROPE_HALF, 64)
                qrv, krv = qr_ref[...].astype(BF16), kr_ref[0:kw, :].astype(BF16)

            def head(h, carry):
                dq_acc, aux_acc = carry
                qm = jnp.where(hid == h, qv, 0.0)
                s = _raw_bdot(qm, kv, 1, 1)
                if has_rope:
                    qrm = jnp.where(rid == h, qrv, 0.0)
                    s = s + _raw_bdot(qrm, krv, 1, 1)
                s = s * scale
                if has_bias:
                    cq = jnp.sum(jnp.where(l128 == h, c_ref[...], 0.0), axis=1, keepdims=True)
                    s = s + (cq - ct_ref[pl.ds(h, 1), 0:kw])
                s = jnp.where(ok, s, NEG)
                p = jnp.exp(s - jnp.sum(jnp.where(l128 == h, lsev, 0.0), axis=1, keepdims=True))
                dom = jnp.where(hid == h, dov, 0.0)
                dp = _raw_bdot(dom, vv, 1, 1)
                delta = jnp.sum(jnp.where(hid == h, dov_ov, 0.0), axis=1, keepdims=True)
                ds = p * (dp - delta)
                dsb, pb = ds.astype(BF16), p.astype(BF16)
                dq_acc = jnp.where(hid == h, _raw_bdot(dsb, kv, 1, 0) * scale, dq_acc)
                dk_ref[0:kw, :] += _raw_bdot(dsb, qm, 0, 0) * scale
                dv_ref[0:kw, :] += _raw_bdot(pb, dom, 0, 0)
                if has_rope:
                    aux_acc = jnp.where(rid == h, _raw_bdot(dsb, krv, 1, 0) * scale, aux_acc)
                    dkr_ref[0:kw, :] += _raw_bdot(dsb, qrm, 0, 0) * scale
                if has_bias:
                    aux_acc = jnp.where(l128 == h, jnp.sum(ds, axis=1, keepdims=True), aux_acc)
                    dct_ref[pl.ds(h, 1), 0:kw] -= jnp.sum(ds, axis=0, keepdims=True)
                return dq_acc, aux_acc

            dq_acc, aux_acc = lax.fori_loop(
                0, FOX_HEADS, head, (jnp.zeros((ATT_QB, ATT_W), F32), jnp.zeros((ATT_QB, BLOCK), F32)))
            dq_ref[...] = dq_acc
            if has_bias:
                dc_ref[...] = aux_acc
            if has_rope:
                dqr_ref[...] = aux_acc

        _attn_by_key_class(i, block)

    wide = jax.ShapeDtypeStruct((LP, ATT_W), F32)
    narrow = jax.ShapeDtypeStruct((LP, BLOCK), F32)
    out_specs = [qspec(0), fspec(0), fspec(0)]
    out_shape = [wide, wide, wide]
    if has_bias:
        out_specs += [qspec(0, BLOCK), _bs((BLOCK, LP), lambda i: (0, 0))]
        out_shape += [narrow, jax.ShapeDtypeStruct((BLOCK, LP), F32)]
    if has_rope:
        out_specs += [qspec(0, BLOCK), fspec(0, BLOCK)]
        out_shape += [narrow, narrow]
    return pl.pallas_call(
        body, name=name, grid=(ATT_STEPS,), in_specs=specs, out_specs=out_specs, out_shape=out_shape,
        compiler_params=_cparams(("arbitrary",)),
    )(*ins)


def _loss_head(y, target, *, name):
    tile = TM

    def body(y_ref, t_ref, dy_ref, loss_ref):
        i = pl.program_id(0)
        rows = i * tile + lax.broadcasted_iota(jnp.int32, (tile, D_MODEL), 0)
        err = jnp.where(rows >= BLOCK, y_ref[...] - t_ref[...], 0.0)
        dy_ref[...] = err * (1.0 / D_MODEL)
        part = 0.5 * jnp.sum(jnp.sum(err * err, axis=1, keepdims=True) * (1.0 / D_MODEL), axis=0, keepdims=True)
        part = jnp.broadcast_to(part, (1, BLOCK))

        @pl.when(i == 0)
        def _():
            loss_ref[...] = part

        @pl.when(i > 0)
        def _():
            loss_ref[...] += part

    return pl.pallas_call(
        body, name=name, grid=(LP // tile,),
        in_specs=[_bs((tile, D_MODEL), lambda i: (i, 0)), _bs((tile, D_MODEL), lambda i: (i, 0))],
        out_specs=[_bs((tile, D_MODEL), lambda i: (i, 0)), _bs((1, BLOCK), lambda i: (0, 0))],
        out_shape=[jax.ShapeDtypeStruct((LP, D_MODEL), F32), jax.ShapeDtypeStruct((1, BLOCK), F32)],
        compiler_params=_cparams(("arbitrary",)),
    )(y, target)


def _adamw(w, gs, m, v, *, name, after=()):
    if w.ndim == 2:
        w, m, v = w[None], m[None], v[None]
        squeeze = True
    else:
        squeeze = False
    NL, R, C = w.shape
    assert len(gs) == NL
    CG = gs[0].shape[1]
    tile = _tile(R, 352, 8)

    def body(*refs):
        w_ref, g_refs = refs[0], refs[1:1 + NL]
        m_ref, v_ref = refs[1 + NL:3 + NL]
        go_ref, d_ref, nm_ref, nv_ref = refs[3 + NL + len(after):]
        gv = g_refs[0][:, :C]
        for j in range(1, NL):
            gv = jnp.where(pl.program_id(0) == j, g_refs[j][:, :C], gv)
        nm = ADAM_B1 * m_ref[...] + (1.0 - ADAM_B1) * gv
        nv = ADAM_B2 * v_ref[...] + (1.0 - ADAM_B2) * (gv * gv)
        m_hat = nm / (1.0 - ADAM_B1 ** ADAM_STEP)
        v_hat = nv / (1.0 - ADAM_B2 ** ADAM_STEP)
        go_ref[...] = gv
        d_ref[...] = -ADAM_LR * (m_hat / (jnp.sqrt(v_hat) + ADAM_EPS) + ADAM_WD * w_ref[...])
        nm_ref[...] = nm
        nv_ref[...] = nv

    spec = _bs((None, tile, C), lambda l, i: (l, i, 0))
    gspecs = [_bs((tile, CG), lambda l, i, j=j: (jnp.where(l == j, i, 0), 0)) for j in range(NL)]
    res = pl.pallas_call(
        body, name=name, grid=(NL, R // tile), in_specs=[spec, *gspecs, spec, spec, *[ANY] * len(after)],
        out_specs=[spec] * 4, out_shape=[jax.ShapeDtypeStruct((NL, R, C), F32)] * 4,
        compiler_params=_cparams(("parallel", "parallel")),
    )(w, *gs, m, v, *after)
    return [r[0] for r in res] if squeeze else res


def _my_pos():
    return lax.axis_index("x"), lax.axis_index("y"), lax.axis_index("c")


def _other_chips(x, y):
    return [(1 - x, y), (x, 1 - y), (1 - x, 1 - y)]


def _allgather_chips(shards):
    n = len(shards)
    per = 7

    def body(*refs):
        ins, outs = refs[:n], refs[n:2 * n]
        send_sems, recv_sems = refs[2 * n], refs[2 * n + 1]
        x, y, c = _my_pos()
        chips = _other_chips(x, y)
        sibling, me = (x, y, 1 - c), 2 * x + y

        def cp(a, kk, src, dst, to):
            return pltpu.make_async_remote_copy(src_ref=src, dst_ref=dst, send_sem=send_sems.at[per * a + kk],
                                                recv_sem=recv_sems.at[per * a + kk], device_id=to, device_id_type=MESH)

        sends = []
        for a in range(n):
            for j, chip in enumerate(chips):
                sends.append(cp(a, j, ins[a].at[c], outs[a].at[me, c], (*chip, c)))
            sends.append(cp(a, 3, ins[a], outs[a].at[me], sibling))
        for s in sends:
            s.start()
        for a in range(n):
            for j, chip in enumerate(chips):
                slab = outs[a].at[2 * chip[0] + chip[1], c]
                cp(a, j, slab, slab, (x, y, c)).wait_recv()
                fwd = cp(a, 4 + j, slab, slab, sibling)
                fwd.start()
                sends.append(fwd)
        for a in range(n):
            cp(a, 3, ins[a], outs[a].at[me], (x, y, c)).wait_recv()
            for j, chip in enumerate(chips):
                slab = outs[a].at[2 * chip[0] + chip[1], 1 - c]
                cp(a, 4 + j, slab, slab, (x, y, c)).wait_recv()
        for s in sends:
            s.wait_send()

    return pl.pallas_call(
        body, name="allgather_chips", in_specs=[ANY] * n, out_specs=[ANY] * n,
        out_shape=[jax.ShapeDtypeStruct((N_CHIPS,) + s.shape, s.dtype) for s in shards],
        scratch_shapes=[pltpu.SemaphoreType.DMA((per * n,)), pltpu.SemaphoreType.DMA((per * n,))],
    )(*shards)


def _rs_swap_rows(gs, tag):
    n = len(gs)

    def body(*refs):
        ins, outs = refs[:n], refs[n:2 * n]
        send_sems, recv_sems = refs[2 * n], refs[2 * n + 1]
        x, y, c = _my_pos()
        cps = []
        for a in range(n):
            half = ins[a].shape[1] // 2
            cps.append(pltpu.make_async_remote_copy(
                src_ref=ins[a].at[:, pl.ds((1 - c) * half, half)], dst_ref=outs[a], send_sem=send_sems.at[a],
                recv_sem=recv_sems.at[a], device_id=(x, y, 1 - c), device_id_type=MESH))
        for cp in cps:
            cp.start()
        for cp in cps:
            cp.wait()

    return pl.pallas_call(
        body, name=f"rs_swap_rows_{tag}", in_specs=[ANY] * n, out_specs=[ANY] * n,
        out_shape=[jax.ShapeDtypeStruct((N_CHIPS, g.shape[1] // 2, g.shape[2]), g.dtype) for g in gs],
        scratch_shapes=[pltpu.SemaphoreType.DMA((n,)), pltpu.SemaphoreType.DMA((n,))],
    )(*gs)


RS_ADD_VMEM_BYTES = 24 * 1024 * 1024


def _rs_tile(H, C, n):
    return _tile(H, max(16, RS_ADD_VMEM_BYTES // (28 * n * C)), 16)


def _rs_add_pair(gs, rs, pos, *, name):
    n = len(gs)
    _, H, C = rs[0].shape
    tile = _rs_tile(H, C, n)
    nt = H // tile

    def body(pos_ref, *refs):
        for a in range(n):
            s = refs[a][...] + refs[n + a][...]
            refs[2 * n + 2 * a][...] = s
            refs[2 * n + 2 * a + 1][...] = s.astype(BF16)

    spec = _bs((None, tile, C), lambda k, i, pos_ref: (k, i, 0))
    g_spec = _bs((None, tile, C), lambda k, i, pos_ref: (k, pos_ref[1] * nt + i, 0))
    grid_spec = pltpu.PrefetchScalarGridSpec(
        num_scalar_prefetch=1, grid=(N_CHIPS, nt), in_specs=[g_spec] * n + [spec] * n, out_specs=[spec] * (2 * n))
    res = pl.pallas_call(
        body, name=name, grid_spec=grid_spec,
        out_shape=[jax.ShapeDtypeStruct((N_CHIPS, H, C), F32), jax.ShapeDtypeStruct((N_CHIPS, H, C), BF16)] * n,
        compiler_params=_cparams(("parallel", "parallel")),
    )(pos, *gs, *rs)
    return [(res[2 * a], res[2 * a + 1]) for a in range(n)]


def _exchange_copies(srcs, lands, send_sems, recv_sems):
    x, y, c = _my_pos()
    starts, landing = [], []
    for a in range(len(srcs)):
        for j, chip in enumerate(_other_chips(x, y)):
            sems = dict(send_sem=send_sems.at[3 * a + j], recv_sem=recv_sems.at[3 * a + j], device_id_type=MESH)
            starts.append(pltpu.make_async_remote_copy(
                src_ref=srcs[a].at[2 * chip[0] + chip[1]], dst_ref=lands[a].at[j], device_id=(*chip, c), **sems))
            landing.append(pltpu.make_async_remote_copy(
                src_ref=lands[a].at[j], dst_ref=lands[a].at[j], device_id=(x, y, c), **sems))
    return starts, landing


def _gather_copies(srcs, lands, send_sems, recv_sems):
    x, y, c = _my_pos()
    me = 2 * x + y
    starts, landing = [], []
    for a in range(len(srcs)):
        half = srcs[a].shape[0] // 2
        mine = pl.ds(c * half, half)
        for j, chip in enumerate(_other_chips(x, y)):
            sems = dict(send_sem=send_sems.at[3 * a + j], recv_sem=recv_sems.at[3 * a + j], device_id_type=MESH)
            starts.append(pltpu.make_async_remote_copy(
                src_ref=srcs[a].at[mine], dst_ref=lands[a].at[me, mine], device_id=(*chip, c), **sems))
            slab = lands[a].at[2 * chip[0] + chip[1], mine]
            landing.append(pltpu.make_async_remote_copy(src_ref=slab, dst_ref=slab, device_id=(x, y, c), **sems))
    return starts, landing


HBM = pl.BlockSpec(memory_space=pltpu.HBM)
SEM = pl.BlockSpec(memory_space=pltpu.SEMAPHORE)


def _ici_start(copies_fn, srcs, land_shapes, *, name, after=(), sems_per_array=3):
    n, na = len(srcs), len(after)

    def body(*refs):
        starts, _ = copies_fn(refs[:n], refs[n:2 * n], refs[2 * n + na], refs[2 * n + na + 1])
        for cp in starts:
            cp.start()
        refs[-1][...] = jnp.zeros_like(refs[-1])

    sems = pltpu.SemaphoreType.DMA((sems_per_array * n,))
    hbm = lambda s: pltpu.HBM(s.shape, s.dtype)
    lands = [pltpu.with_memory_space_constraint(
        lax.empty(s.shape, s.dtype) if isinstance(s, jax.ShapeDtypeStruct) else s, pltpu.HBM) for s in land_shapes]
    res = pl.pallas_call(
        body, name=name, in_specs=[HBM] * (2 * n) + [ANY] * na,
        out_specs=(SEM, SEM, *[HBM] * (2 * n), pl.BlockSpec(memory_space=pltpu.VMEM)),
        out_shape=(sems, sems, *[hbm(s) for s in srcs], *[hbm(s) for s in land_shapes],
                   jax.ShapeDtypeStruct((8, BLOCK), F32)),
        input_output_aliases={i: 2 + i for i in range(2 * n)},
        compiler_params=pltpu.CompilerParams(has_side_effects=pltpu.SideEffectType.DATAFLOW_SIDE_EFFECTING),
    )(*[pltpu.with_memory_space_constraint(s, pltpu.HBM) for s in srcs], *lands, *after)
    return res[0], res[1], list(res[2:2 + n]), list(res[2 + n:2 + 2 * n]), res[-1]


def _ici_wait(copies_fn, send_sems, recv_sems, srcs, lands, after, *, name):
    n = len(srcs)
    after = list(after) if isinstance(after, (list, tuple)) else [after]

    def body(*refs):
        starts, landing = copies_fn(refs[:n], refs[n:2 * n], refs[2 * n], refs[2 * n + 1])
        for cp in starts:
            cp.wait_send()
        for cp in landing:
            cp.wait_recv()

    hbm = lambda s: pltpu.HBM(s.shape, s.dtype)
    res = pl.pallas_call(
        body, name=name, in_specs=[*[HBM] * (2 * n), SEM, SEM, *[ANY] * len(after)], out_specs=[HBM] * (2 * n),
        out_shape=[*[hbm(s) for s in srcs], *[hbm(s) for s in lands]],
        input_output_aliases={i: i for i in range(2 * n)},
        compiler_params=pltpu.CompilerParams(has_side_effects=pltpu.SideEffectType.DATAFLOW_SIDE_EFFECTING),
    )(*srcs, *lands, send_sems, recv_sems, *after)
    return list(res[:n]), list(res[n:])


D2D_COPIES = 4


def _d2d_copies(ins, outs, send_sems, recv_sems):
    x, y, c = _my_pos()
    me, sibling = 2 * x + y, (x, y, 1 - c)
    starts, landing = [], []
    for a in range(len(ins)):
        half = ins[a].shape[0] // 2
        mine, theirs = pl.ds(c * half, half), pl.ds((1 - c) * half, half)
        pairs = [(ins[a], outs[a].at[me], outs[a].at[me])]
        for chip in _other_chips(x, y):
            k = 2 * chip[0] + chip[1]
            pairs.append((outs[a].at[k, mine], outs[a].at[k, mine], outs[a].at[k, theirs]))
        for j, (src, dst, lands_here) in enumerate(pairs):
            sems = dict(send_sem=send_sems.at[D2D_COPIES * a + j], recv_sem=recv_sems.at[D2D_COPIES * a + j],
                        device_id_type=MESH)
            starts.append(pltpu.make_async_remote_copy(src_ref=src, dst_ref=dst, device_id=sibling, **sems))
            landing.append(pltpu.make_async_remote_copy(src_ref=lands_here, dst_ref=lands_here, device_id=(x, y, c),
                                                        **sems))
    return starts, landing


def _gather_d2d(shards, lands, tag):
    n = len(shards)

    def body(*refs):
        starts, landing = _d2d_copies(refs[:n], refs[2 * n:3 * n], refs[3 * n], refs[3 * n + 1])
        for cp in starts:
            cp.start()
        for cp in landing:
            cp.wait_recv()
        for cp in starts:
            cp.wait_send()

    return pl.pallas_call(
        body, name=f"gather_d2d_{tag}", in_specs=[ANY] * (2 * n), out_specs=[ANY] * n,
        out_shape=[jax.ShapeDtypeStruct(s.shape, s.dtype) for s in lands],
        input_output_aliases={n + a: a for a in range(n)},
        scratch_shapes=[pltpu.SemaphoreType.DMA((D2D_COPIES * n,)), pltpu.SemaphoreType.DMA((D2D_COPIES * n,))],
    )(*shards, *lands)


def _rs_add_chips(p32s, r16s, pos, *, name):
    n = len(p32s)
    _, H, C = p32s[0].shape
    tile = _rs_tile(H, C, n)
    nt = H // tile

    def body(pos_ref, *refs):
        for a in range(n):
            p_ref, r_ref = refs[a], refs[n + a]
            refs[2 * n + a][...] = ((p_ref[...] + r_ref[0].astype(F32)) + r_ref[1].astype(F32)) + r_ref[2].astype(F32)

    grid_spec = pltpu.PrefetchScalarGridSpec(
        num_scalar_prefetch=1, grid=(nt,),
        in_specs=[_bs((None, tile, C), lambda i, pos_ref: (pos_ref[0], i, 0))] * n
        + [_bs((3, tile, C), lambda i, pos_ref: (0, i, 0))] * n,
        out_specs=[_bs((tile, C), lambda i, pos_ref: (pos_ref[1] * nt + i, 0))] * n)
    return pl.pallas_call(
        body, name=name, grid_spec=grid_spec, out_shape=[jax.ShapeDtypeStruct((2 * H, C), F32)] * n,
        compiler_params=_cparams(("parallel",)),
    )(pos, *p32s, *r16s)


def _rs_join_rows(fs, tag):
    n = len(fs)

    def body(*refs):
        outs = refs[n:2 * n]
        send_sems, recv_sems = refs[2 * n], refs[2 * n + 1]
        x, y, c = _my_pos()
        for a in range(n):
            half = outs[a].shape[0] // 2
            mine = outs[a].at[pl.ds(c * half, half)]
            pltpu.make_async_remote_copy(src_ref=mine, dst_ref=mine, send_sem=send_sems.at[a],
                                         recv_sem=recv_sems.at[a], device_id=(x, y, 1 - c), device_id_type=MESH).start()
        for a in range(n):
            half = outs[a].shape[0] // 2
            pltpu.make_async_remote_copy(
                src_ref=outs[a].at[pl.ds(c * half, half)], dst_ref=outs[a].at[pl.ds((1 - c) * half, half)],
                send_sem=send_sems.at[a], recv_sem=recv_sems.at[a], device_id=(x, y, 1 - c), device_id_type=MESH).wait()

    return pl.pallas_call(
        body, name=f"rs_join_rows_{tag}", in_specs=[ANY] * n, out_specs=[ANY] * n,
        out_shape=[jax.ShapeDtypeStruct(f.shape, f.dtype) for f in fs],
        input_output_aliases={a: a for a in range(n)},
        scratch_shapes=[pltpu.SemaphoreType.DMA((n,)), pltpu.SemaphoreType.DMA((n,))],
    )(*fs)


def _pos_vector():
    x, y, c = _my_pos()
    return jnp.stack([2 * x + y, c]).astype(jnp.int32)


def _swap_copies(srcs, lands, send_sems, recv_sems):
    x, y, c = _my_pos()
    starts, landing = [], []
    for a in range(len(srcs)):
        half = srcs[a].shape[1] // 2
        sems = dict(send_sem=send_sems.at[3 * a], recv_sem=recv_sems.at[3 * a], device_id_type=MESH)
        starts.append(pltpu.make_async_remote_copy(
            src_ref=srcs[a].at[:, pl.ds((1 - c) * half, half)], dst_ref=lands[a], device_id=(x, y, 1 - c), **sems))
        landing.append(pltpu.make_async_remote_copy(src_ref=lands[a], dst_ref=lands[a], device_id=(x, y, c), **sems))
    return starts, landing


def _swap_land_shapes(gs):
    return [jax.ShapeDtypeStruct((N_CHIPS, g.shape[1] // 2, g.shape[2]), g.dtype) for g in gs]


def _same_shape_runs(arrays):
    runs, start = [], 0
    for i in range(1, len(arrays) + 1):
        if i == len(arrays) or arrays[i].shape != arrays[start].shape:
            runs.append((start, i))
            start = i
    return runs


def _rs_add_pairs(gs, r1, names, tag):
    pos = _pos_vector()
    out = []
    for a, b in _same_shape_runs(gs):
        out += _rs_add_pair(gs[a:b], r1[a:b], pos, name=f"rs_add_pair_{tag}_{names[a]}")
    return out


def _rs_pair_sums(gs, names, tag):
    return _rs_add_pairs(gs, _rs_swap_rows(gs, tag), names, tag)


def _rs_finish(pairs, r2, names, tag):
    pos = _pos_vector()
    p32s = [p[0] for p in pairs]
    fs = []
    for a, b in _same_shape_runs(p32s):
        fs += _rs_add_chips(p32s[a:b], r2[a:b], pos, name=f"rs_add_chips_{tag}_{names[a]}")
    return _rs_join_rows(fs, tag)


def _exchange_land_shapes(pairs):
    return [jax.ShapeDtypeStruct((3,) + p[1].shape[1:], p[1].dtype) for p in pairs]


def _allreduce_small(buf):
    R, W = buf.shape

    def body(b_ref, o_ref, gather, send_sems, recv_sems):
        x, y, c = _my_pos()
        me = 4 * x + 2 * y + c
        gather[me] = b_ref[...]
        cps = []
        for d in range(1, 8):
            peer = (x ^ (d >> 2), y ^ ((d >> 1) & 1), c ^ (d & 1))
            cps.append(pltpu.make_async_remote_copy(
                src_ref=b_ref, dst_ref=gather.at[me], send_sem=send_sems.at[d - 1], recv_sem=recv_sems.at[d - 1],
                device_id=peer, device_id_type=MESH))
        for cp in cps:
            cp.start()
        for d in range(1, 8):
            pltpu.make_async_remote_copy(
                src_ref=b_ref, dst_ref=gather.at[me ^ d], send_sem=send_sems.at[d - 1], recv_sem=recv_sems.at[d - 1],
                device_id=(x, y, c), device_id_type=MESH).wait_recv()
        for cp in cps:
            cp.wait_send()
        acc = gather[0]
        for d in range(1, 8):
            acc = acc + gather[d]
        o_ref[...] = acc

    vm = pl.BlockSpec(memory_space=pltpu.VMEM)
    return pl.pallas_call(
        body, name="allreduce_small", in_specs=[vm], out_specs=vm, out_shape=jax.ShapeDtypeStruct((R, W), F32),
        scratch_shapes=[pltpu.VMEM((8, R, W), F32), pltpu.SemaphoreType.DMA((7,)), pltpu.SemaphoreType.DMA((7,))],
    )(buf)


def _heads(a, h, d):
    return a.reshape(a.shape[0], h, d).transpose(1, 0, 2)


def _unheads(a):
    h, L, d = a.shape
    return a.transpose(1, 0, 2).reshape(L, h * d)


def _rope_tables():
    pos = jnp.maximum(jnp.arange(LP, dtype=F32) - PAD_ROWS, 0.0)
    inv_freq = 1.0 / (ROPE_THETA ** (jnp.arange(0, MLA_ROPE, 2, dtype=F32) / MLA_ROPE))
    ang = pos[:, None] * inv_freq[None, :]
    cos, sin = jnp.tile(jnp.cos(ang), (1, MLA_HEADS)), jnp.tile(jnp.sin(ang), (1, MLA_HEADS))
    return jnp.concatenate([cos, cos], axis=1), jnp.concatenate([-sin, sin], axis=1)


def _lane_pad(a, width=BLOCK):
    return jnp.pad(a, ((0, 0), (0, width - a.shape[1])))


def _pad_in_proj(w):
    sl = lambda start, size: w[:, start:start + size]
    return jnp.concatenate([
        sl(OC_Z, 512), sl(OC_XBC, 768), sl(OC_FQ, 256), sl(OC_FK, 256), sl(OC_FV, 256), sl(OC_CQ, 256), sl(OC_CKV, 128),
        _lane_pad(sl(OC_DT, SSD_HEADS)), _lane_pad(sl(OC_FR, FOX_HEADS)),
        jnp.tile(sl(OC_KR, ROPE_HALF), (1, MLA_HEADS)), jnp.tile(sl(OC_KR + ROPE_HALF, ROPE_HALF), (1, MLA_HEADS))], axis=1)


def _in_proj_grad_chunks(wp):
    rope = lambda start: wp[:, start:start + 64].reshape(wp.shape[0], MLA_HEADS, ROPE_HALF).sum(axis=1)
    segs = [(wp, PC_Z, 512), (wp, PC_XBC, 768), (wp, PC_DT, SSD_HEADS), (wp, PC_FQ, 256), (wp, PC_FK, 256),
            (wp, PC_FV, 256), (wp, PC_FR, FOX_HEADS), (wp, PC_CQ, 256), (wp, PC_CKV, 128),
            (rope(PC_KR), 0, ROPE_HALF), (rope(PC_KR + 64), 0, ROPE_HALF)]
    chunks = []
    for k in range(N_CHIPS):
        lo, hi, pos, pieces = k * IN_SHARD, (k + 1) * IN_SHARD, 0, []
        for arr, start, size in segs:
            a, b = max(lo, pos), min(hi, pos + size)
            if a < b:
                pieces.append(arr[:, start + a - pos:start + b - pos])
            pos += size
        pieces.append(jnp.zeros((wp.shape[0], IN_SHARD_P - IN_SHARD), wp.dtype))
        chunks.append(jnp.concatenate(pieces, axis=1))
    return jnp.stack(chunks)


def _regroup_uq(w):
    w3 = w.reshape(w.shape[0], MLA_HEADS, MLA_NOPE + MLA_ROPE)
    return jnp.concatenate([w3[:, :, :MLA_NOPE].reshape(w.shape[0], -1),
                            w3[:, :, MLA_NOPE:MLA_NOPE + ROPE_HALF].reshape(w.shape[0], -1),
                            w3[:, :, MLA_NOPE + ROPE_HALF:].reshape(w.shape[0], -1)], axis=1)


def _ungroup_uq(wp):
    n = wp.shape[0]
    return jnp.concatenate([wp[:, :256].reshape(n, MLA_HEADS, MLA_NOPE), wp[:, 256:320].reshape(n, MLA_HEADS, ROPE_HALF),
                            wp[:, 320:].reshape(n, MLA_HEADS, ROPE_HALF)], axis=2).reshape(n, -1)


def _regroup_ukv(w):
    w3 = w.reshape(w.shape[0], MLA_HEADS, MLA_NOPE + MLA_V)
    return jnp.concatenate([w3[:, :, :MLA_NOPE].reshape(w.shape[0], -1), w3[:, :, MLA_NOPE:].reshape(w.shape[0], -1)],
                           axis=1)


def _ungroup_ukv(wp):
    n = wp.shape[0]
    return jnp.concatenate([wp[:, :256].reshape(n, MLA_HEADS, MLA_NOPE), wp[:, 256:].reshape(n, MLA_HEADS, MLA_V)],
                           axis=2).reshape(n, -1)


TMF = 1088
N_IF = LP // TMF


def _chunk_rows_dx(g, w, l, chunk_h, *, name):
    N = w.shape[2]
    return _mm_core(g, w, a_spec=_bs((TMF, N), lambda i, j, k: (i, 0)),
                    b_spec=_bs((None, chunk_h, N), lambda i, j, k: (j, 0, 0)),
                    o_spec=_bs((TMF, chunk_h), lambda i, j, k: (i, j)), grid=(N_IF, N_CHIPS, 1),
                    out_shape=(LP, N_CHIPS * chunk_h), ca=1, cb=1, name=name)


def _chunk_rows_dw(a, g, chunk_h, *, name):
    N = g.shape[1]
    return _mm_core(a, g, a_spec=_bs((LP, chunk_h), lambda i, j, k: (0, i)), b_spec=_bs((LP, N), lambda i, j, k: (0, 0)),
                    o_spec=_bs((None, chunk_h, N), lambda i, j, k: (i, 0, 0)), grid=(N_CHIPS, 1, 1),
                    out_shape=(N_CHIPS, chunk_h, N), ca=0, cb=0, name=name)


def _ffn_up_swiglu(h, wg, wu, *, name, after=()):
    def body(h_ref, wg_ref, wu_ref, *refs):
        g_ref, u_ref, a_ref = refs[len(after):]
        hb = h_ref[...].astype(BF16)
        g = _raw_bdot(hb, wg_ref[...], 1, 1)
        u = _raw_bdot(hb, wu_ref[...], 1, 1)
        g_ref[...] = g
        u_ref[...] = u
        a_ref[...] = (_silu(g) * u).astype(a_ref.dtype)

    w_spec = _bs((None, HP, D_MODEL), lambda i, j: (j, 0, 0))
    o_spec = _bs((TMF, HP), lambda i, j: (i, j))
    return pl.pallas_call(
        body, name=name, grid=(N_IF, N_CHIPS),
        in_specs=[_bs((TMF, D_MODEL), lambda i, j: (i, 0)), w_spec, w_spec, *[ANY] * len(after)],
        out_specs=[o_spec] * 3,
        out_shape=[jax.ShapeDtypeStruct((LP, FP), F32), jax.ShapeDtypeStruct((LP, FP), F32),
                   jax.ShapeDtypeStruct((LP, FP), BF16)],
        compiler_params=_cparams(("parallel", "parallel")),
    )(h, wg, wu, *after)


def _ffn_down_dx_swiglu(do, wd, g, u, *, name):
    def body(do_ref, wd_ref, g_ref, u_ref, dg_ref, du_ref):
        dact = _raw_bdot(do_ref[...], wd_ref[...], 1, 1)
        gv = g_ref[...]
        sig = _sigmoid(gv)
        dg_ref[...] = (dact * u_ref[...] * (sig * (1.0 + gv * (1.0 - sig)))).astype(dg_ref.dtype)
        du_ref[...] = (dact * (gv * sig)).astype(du_ref.dtype)

    blk = _bs((TMF, HP), lambda i, j: (i, j))
    return pl.pallas_call(
        body, name=name, grid=(N_IF, N_CHIPS),
        in_specs=[_bs((TMF, D_MODEL), lambda i, j: (i, 0)), _bs((None, HP, D_MODEL), lambda i, j: (j, 0, 0)), blk, blk],
        out_specs=[blk, blk], out_shape=[jax.ShapeDtypeStruct((LP, FP), BF16)] * 2,
        compiler_params=_cparams(("parallel", "parallel")),
    )(do, wd, g, u)


def _ffn_gate_up_dw(dg, du, h, *, name):
    def body(dg_ref, du_ref, h_ref, wg_ref, wu_ref):
        hb = h_ref[...].astype(BF16)
        wg_ref[...] = _raw_bdot(dg_ref[...], hb, 0, 0)
        wu_ref[...] = _raw_bdot(du_ref[...], hb, 0, 0)

    a_spec = _bs((LP, HP), lambda k: (0, k))
    o_spec = _bs((None, HP, D_MODEL), lambda k: (k, 0, 0))
    return pl.pallas_call(
        body, name=name, grid=(N_CHIPS,), in_specs=[a_spec, a_spec, _bs((LP, D_MODEL), lambda k: (0, 0))],
        out_specs=[o_spec, o_spec], out_shape=[jax.ShapeDtypeStruct((N_CHIPS, HP, D_MODEL), F32)] * 2,
        compiler_params=_cparams(("parallel",)),
    )(dg, du, h)


def _ffn_gate_up_dx(dg, du, wg, wu, add, *, name):
    def body(dg_ref, du_ref, wg_ref, wu_ref, add_ref, o_ref, acc_ref):
        k = pl.program_id(1)

        @pl.when(k == 0)
        def _():
            acc_ref[...] = jnp.zeros_like(acc_ref)

        acc_ref[...] += _raw_bdot(dg_ref[...], wg_ref[...], 1, 0) + _raw_bdot(du_ref[...], wu_ref[...], 1, 0)

        @pl.when(k == N_CHIPS - 1)
        def _():
            o_ref[...] = acc_ref[...] + add_ref[...]

    a_spec = _bs((TMF, HP), lambda i, k: (i, k))
    w_spec = _bs((None, HP, D_MODEL), lambda i, k: (k, 0, 0))
    o_spec = _bs((TMF, D_MODEL), lambda i, k: (i, 0))
    return pl.pallas_call(
        body, name=name, grid=(N_IF, N_CHIPS), in_specs=[a_spec, a_spec, w_spec, w_spec, o_spec], out_specs=o_spec,
        out_shape=jax.ShapeDtypeStruct((LP, D_MODEL), F32), scratch_shapes=[pltpu.VMEM((TMF, D_MODEL), F32)],
        compiler_params=_cparams(("parallel", "arbitrary")),
    )(dg, du, wg, wu, add)


def _chunk_rows_mm_res_ln(a, w, chunk_h, h, gam, bet, scale, *, name):
    res_ln = _make_res_ln_fn(scale)

    def body(a_ref, w_ref, h_ref, g_ref, b_ref, o_ref, y_ref, yb_ref, acc_ref):
        k = pl.program_id(1)

        @pl.when(k == 0)
        def _():
            acc_ref[...] = jnp.zeros_like(acc_ref)

        acc_ref[...] += _raw_bdot(a_ref[...], w_ref[...], 1, 0)

        @pl.when(k == N_CHIPS - 1)
        def _():
            o = acc_ref[...]
            o_ref[...] = o
            (y,) = res_ln(0, h_ref[...], o, g_ref[...], b_ref[...])
            y_ref[...] = y
            yb_ref[...] = y.astype(yb_ref.dtype)

    row = _bs((TMF, D_MODEL), lambda i, k: (i, 0))
    par = _bs((1, D_MODEL), lambda i, k: (0, 0))
    return pl.pallas_call(
        body, name=name, grid=(N_IF, N_CHIPS),
        in_specs=[_bs((TMF, chunk_h), lambda i, k: (i, k)), _bs((None, chunk_h, D_MODEL), lambda i, k: (k, 0, 0)), row,
                  par, par],
        out_specs=[row, row, row],
        out_shape=[jax.ShapeDtypeStruct((LP, D_MODEL), F32)] * 2 + [jax.ShapeDtypeStruct((LP, D_MODEL), BF16)],
        scratch_shapes=[pltpu.VMEM((TMF, D_MODEL), F32)], compiler_params=_cparams(("parallel", "arbitrary")),
    )(a, w, h, gam, bet)


def _ffn_fwd(hp, W, pre, l, gam, bet, tag, after=()):
    h, hb = hp
    g, u, act = _ffn_up_swiglu(hb, W[pre + "_w_gate"][l], W[pre + "_w_up"][l], name=f"{tag}_up_swiglu", after=after)
    o, out, outb = _chunk_rows_mm_res_ln(act, W[pre + "_w_down"][l], HP, h, gam, bet, 0.5, name=f"{tag}_down_ln")
    return (out, outb), (h, hb, g, u, act, o)


def _ffn_bwd(dout, saved, W, pre, l, gam, bet, GB, tag):
    h, hb, g, u, act, o = saved
    (dh_a, do), (dgam, dbet) = _rowwise_bwd(_make_res_ln_fn(0.5), [h, o], [gam, bet], [dout], name=f"{tag}_ln_bwd",
                                            tile=TM, grad_dtypes=[F32, BF16])
    dg, du = _ffn_down_dx_swiglu(do, W[pre + "_w_down"][l], g, u, name=f"{tag}_down_dx_swiglu")
    GB[pre + "_w_down"] = _chunk_rows_dw(act, do, HP, name=f"{tag}_down_dw")
    GB[pre + "_w_gate"], GB[pre + "_w_up"] = _ffn_gate_up_dw(dg, du, hb, name=f"{tag}_gate_up_dw")
    dh = _ffn_gate_up_dx(dg, du, W[pre + "_w_gate"][l], W[pre + "_w_up"][l], dh_a, name=f"{tag}_gate_up_dx")
    return dh, dgam, dbet


def _mixer_fwd(hp1, W, l, cosf, sins, after=()):
    h1, h1b = hp1
    tag = f"l{l}"
    proj = _mm(h1b, W["w_in_p"][l], name=f"{tag}_in_proj", after=after)
    sv = {"h1": h1, "h1b": h1b, "proj": proj}
    conv_w, conv_b = W["conv_w"][l], W["conv_b"][l][None]
    xc = _conv_fwd(proj, PC_XBC // BLOCK, conv_w, conv_b, name=f"{tag}_conv")
    dt_bias = _lane_pad(W["dt_bias"][l][None])
    dtc, dtr = _ssd_dt_fwd(proj, PC_DT // BLOCK, dt_bias, name=f"{tag}_ssd_dt")
    xh = _heads(xc[:, :SSD_D], SSD_HEADS, SSD_HD)
    bm = _heads(xc[:, SSD_D:SSD_D + 128], SSD_GROUPS, SSD_STATE)
    cm = _heads(xc[:, SSD_D + 128:], SSD_GROUPS, SSD_STATE)
    alog = jnp.broadcast_to(W["a_log"][l][:, None, None], (SSD_HEADS, 1, BLOCK))
    yh, prevs = _ssd_fwd(xh, bm, cm, dtc, dtr, alog, name=f"{tag}_ssd")
    y_raw = _unheads(yh)
    dskip = jnp.repeat(W["d_skip"][l], SSD_HD)[None]
    normg = W["ssd_norm_g"][l][None]
    post_rows = [y_raw, (xc, 256, 0), (proj, 256, PC_Z // 256)]
    (y_ssd,) = _rowwise(_ssd_post_fn, post_rows, [dskip, normg], [SSD_D], name=f"{tag}_ssd_post", tile=TM,
                        ncol=SSD_GROUPS)
    sv.update(conv_w=conv_w, conv_b=conv_b, dt_bias=dt_bias, xh=xh, bm=bm, cm=cm, dtc=dtc, dtr=dtr, alog=alog,
              prevs=prevs, post_rows=post_rows, dskip=dskip, normg=normg)
    f_b = _lane_pad(W["fox_f_b"][l][None])
    cg, cgt = _fox_gate_fwd(proj, PC_FR // BLOCK, f_b, name=f"{tag}_fox_gate")
    fox_qkv = ((proj, PC_FQ // ATT_W), (proj, PC_FK // ATT_W), (proj, PC_FV // ATT_W))
    y_fox, lse_f = _attn_fwd(*fox_qkv, scale=FOX_HD ** -0.5, name=f"{tag}_fox_attn", bias=(cg, cgt))
    sv.update(f_b=f_b, cg=cg, cgt=cgt, fox_qkv=fox_qkv, y_fox=y_fox, lse_f=lse_f)
    gq, gkv = W["mla_q_norm_g"][l][None], W["mla_kv_norm_g"][l][None]
    norm_rows = [(proj, 256, PC_CQ // 256), (proj, BLOCK, PC_CKV // BLOCK)]
    qn, cn = _rowwise(_mla_norm_fn, norm_rows, [gq, gkv], [MLA_Q_LORA, MLA_KV_LORA], name=f"{tag}_mla_norm", tile=TM,
                      out_dtypes=[BF16, BF16])
    qh = _mm(qn, W["mla_w_uq_p"][l], name=f"{tag}_mla_uq")
    kvh = _mm(cn, W["mla_w_ukv_p"][l], name=f"{tag}_mla_ukv")
    qr, kr = _rowwise(_rope_fn, [(qh, BLOCK, 2), (proj, BLOCK, PC_KR // BLOCK), cosf, sins], [], [BLOCK, BLOCK],
                      name=f"{tag}_rope", tile=TM)
    mla_qkv = ((qh, 0), (kvh, 0), (kvh, 1))
    y_mla, lse_m = _attn_fwd(*mla_qkv, scale=(MLA_NOPE + MLA_ROPE) ** -0.5, name=f"{tag}_mla_attn",
                             rope=((qr, 0), (kr, 0)))
    sv.update(gq=gq, gkv=gkv, norm_rows=norm_rows, qn=qn, cn=cn, qr=qr, kr=kr, mla_qkv=mla_qkv, y_mla=y_mla, lse_m=lse_m)
    ycat = jnp.concatenate([y_ssd, y_fox, y_mla], axis=1).astype(BF16)
    mix, h2, h2b = _chunk_rows_mm_res_ln(ycat, W["w_out"][l], 256, h1, W["ln2_g"][l][None], W["ln2_b"][l][None], 1.0,
                                    name=f"{tag}_out_proj_ln2")
    sv.update(mix=mix, ycat=ycat)
    return (h2, h2b), sv


def _mixer_bwd(dh2, sv, W, l, cosf, sins, GB, zero=0.0):
    tag = f"l{l}"
    G = {}
    proj = sv["proj"]
    ln2g, ln2b = W["ln2_g"][l][None] + zero, W["ln2_b"][l][None]
    (dh1_a, dmix), (dln2g, dln2b) = _rowwise_bwd(
        _make_res_ln_fn(1.0), [sv["h1"], sv["mix"]], [ln2g, ln2b], [dh2], name=f"{tag}_ln2_bwd", tile=TM,
        grad_dtypes=[F32, BF16])
    G["ln2_g"], G["ln2_b"] = dln2g[0], dln2b[0]
    dycat = _chunk_rows_dx(dmix, W["w_out"][l], l, 256, name=f"{tag}_out_proj_dx")
    GB["w_out"] = _chunk_rows_dw(sv["ycat"], dmix, 256, name=f"{tag}_out_proj_dw")
    (dy_raw, dxs_a, dz), (ddskip, dnormg) = _rowwise_bwd(
        _ssd_post_fn, sv["post_rows"], [sv["dskip"], sv["normg"]], [dycat[:, :SSD_D]],
        name=f"{tag}_ssd_post_bwd", tile=TM, ncol=SSD_GROUPS)
    G["ssd_norm_g"] = dnormg[0]
    G["d_skip"] = ddskip.reshape(SSD_HEADS, SSD_HD).sum(axis=1)
    dxh, dbm, dcm, ddtc, ddtr, dal = _ssd_bwd(sv["xh"], sv["bm"], sv["cm"], sv["dtc"], sv["dtr"], sv["alog"],
                                              sv["prevs"], _heads(dy_raw, SSD_HEADS, SSD_HD), name=f"{tag}_ssd_bwd")
    G["a_log"] = dal[:, 0, 0]
    dxc = jnp.concatenate([dxs_a + _unheads(dxh), _unheads(dbm), _unheads(dcm)], axis=1)
    dxbc, G["conv_w"], dconv_b = _conv_bwd(proj, PC_XBC // BLOCK, sv["conv_w"], sv["conv_b"], dxc,
                                           name=f"{tag}_conv_bwd")
    G["conv_b"] = dconv_b[0]
    ddt_raw, ddt_bias = _ssd_dt_bwd(proj, PC_DT // BLOCK, sv["dt_bias"], ddtc, ddtr, name=f"{tag}_ssd_dt_bwd")
    G["dt_bias"] = ddt_bias[0, :SSD_HEADS]
    dfq, dfk, dfv, dcg, dcgt = _attn_bwd(*sv["fox_qkv"], sv["y_fox"], sv["lse_f"], (dycat, SSD_D // ATT_W),
                                         scale=FOX_HD ** -0.5, name=f"{tag}_fox_attn_bwd", bias=(sv["cg"], sv["cgt"]))
    df_raw, dfb = _fox_gate_bwd(proj, PC_FR // BLOCK, sv["f_b"], dcg, dcgt, name=f"{tag}_fox_gate_bwd")
    G["fox_f_b"] = dfb[0, :FOX_HEADS]
    dqn_h, dkn_h, dv_h, dqr, dkr = _attn_bwd(
        *sv["mla_qkv"], sv["y_mla"], sv["lse_m"], (dycat, (SSD_D + FOX_D) // ATT_W),
        scale=(MLA_NOPE + MLA_ROPE) ** -0.5, name=f"{tag}_mla_attn_bwd", rope=((sv["qr"], 0), (sv["kr"], 0)))
    dq_rope, dk_rope = _rowwise(_rope_t_fn, [dqr, dkr, cosf, sins], [], [BLOCK, BLOCK], name=f"{tag}_rope_bwd",
                                tile=TM)
    dqh = jnp.concatenate([dqn_h, dq_rope], axis=1).astype(BF16)
    dkvh = jnp.concatenate([dkn_h, dv_h], axis=1).astype(BF16)
    dqn = _mm(dqh, W["mla_w_uq_p"][l], tb=True, name=f"{tag}_mla_uq_dx")
    G["mla_w_uq_p"] = _mm(sv["qn"], dqh, ta=True, name=f"{tag}_mla_uq_dw")
    dcn = _mm(dkvh, W["mla_w_ukv_p"][l], tb=True, name=f"{tag}_mla_ukv_dx")
    G["mla_w_ukv_p"] = _mm(sv["cn"], dkvh, ta=True, name=f"{tag}_mla_ukv_dw")
    (dcq, dckv), (dgq, dgkv) = _rowwise_bwd(_mla_norm_fn, sv["norm_rows"], [sv["gq"], sv["gkv"]], [dqn, dcn],
                                            name=f"{tag}_mla_norm_bwd", tile=TM)
    G["mla_q_norm_g"], G["mla_kv_norm_g"] = dgq[0], dgkv[0]
    dproj = jnp.concatenate([dz, dxbc, dfq, dfk, dfv, dcq, dckv, ddt_raw, df_raw, dk_rope], axis=1).astype(BF16)
    dh1 = _mm(dproj, W["w_in_p"][l], tb=True, add=dh1_a, name=f"{tag}_in_proj_dx")
    G["w_in_p"] = _mm(sv["h1b"], dproj, ta=True, name=f"{tag}_in_proj_dw")
    return dh1, G


def _embed(x, meta):
    return jnp.concatenate([jnp.zeros((PAD_ROWS, D_MODEL), F32), meta, x], axis=0)


def _layer_fwd(h, W, l, cosf, sins):
    ln = lambda n: W[n][l][None]
    h1, s1 = _ffn_fwd(h, W, "ffn1", l, ln("ln1_g"), ln("ln1_b"), f"l{l}_ffn1")
    h2, sm = _mixer_fwd(h1, W, l, cosf, sins)
    h3, s2 = _ffn_fwd(h2, W, "ffn2", l, ln("ln3_g"), ln("ln3_b"), f"l{l}_ffn2")
    return h3, (s1, sm, s2)


def _layer_bwd(dh, saved, W, l, cosf, sins):
    ln = lambda n: W[n][l][None]
    s1, sm, s2 = saved
    G = {}
    dh, dg, db = _ffn_bwd(dh, s2, W, "ffn2", l, ln("ln3_g"), ln("ln3_b"), G, f"l{l}_ffn2")
    G["ln3_g"], G["ln3_b"] = dg[0], db[0]
    dh, Gm = _mixer_bwd(dh, sm, W, l, cosf, sins, G)
    G.update(Gm)
    dh, dg, db = _ffn_bwd(dh, s1, W, "ffn1", l, ln("ln1_g"), ln("ln1_b"), G, f"l{l}_ffn1")
    G["ln1_g"], G["ln1_b"] = dg[0], db[0]
    return dh, G


def _local_step(x, target, W):
    h = _embed(x, W["meta"])
    h = (h, h.astype(BF16))
    tgt = jnp.concatenate([jnp.zeros((BLOCK, D_MODEL), F32), target], axis=0)
    cosf, sins = _rope_tables()
    saved = []
    for l in range(DEPTH):
        h, sv = _layer_fwd(h, W, l, cosf, sins)
        saved.append(sv)
    dh, loss = _loss_head(h[0], tgt, name="loss_head")
    grads = [None] * DEPTH
    for l in reversed(range(DEPTH)):
        dh, grads[l] = _layer_bwd(dh, saved[l], W, l, cosf, sins)
    return loss, dh, grads


WEIGHTS = ['meta', 'ffn1_w_gate', 'ffn1_w_up', 'ffn1_w_down', 'ln1_g', 'ln1_b', 'w_in', 'conv_w', 'conv_b', 'dt_bias',
           'a_log', 'd_skip', 'ssd_norm_g', 'fox_f_b', 'mla_q_norm_g', 'mla_w_uq', 'mla_kv_norm_g', 'mla_w_ukv',
           'w_out', 'ln2_g', 'ln2_b', 'ffn2_w_gate', 'ffn2_w_up', 'ffn2_w_down', 'ln3_g', 'ln3_b']
SMALL = ["ln1_g", "ln1_b", "conv_b", "dt_bias", "a_log", "d_skip", "ssd_norm_g", "fox_f_b", "mla_q_norm_g",
         "mla_kv_norm_g", "ln2_g", "ln2_b", "ln3_g", "ln3_b"]
MATMUL_W = ["ffn1_w_gate", "ffn1_w_up", "ffn1_w_down", "w_in", "mla_w_uq", "mla_w_ukv", "w_out", "ffn2_w_gate",
            "ffn2_w_up", "ffn2_w_down"]
SMALL_ROWS = 312


def _pad_to(a, axis, size):
    pads = [(0, 0)] * a.ndim
    pads[axis] = (0, size - a.shape[axis])
    return jnp.pad(a, pads)


def _chip_cols(full, chip, width):
    return lax.dynamic_slice_in_dim(full, chip * width, width, axis=full.ndim - 1)


def kernel(x, meta, ffn1_w_gate, ffn1_w_up, ffn1_w_down, ln1_g, ln1_b, w_in, conv_w, conv_b, dt_bias, a_log, d_skip, ssd_norm_g, fox_f_b, mla_q_norm_g, mla_w_uq, mla_kv_norm_g, mla_w_ukv, w_out, ln2_g, ln2_b, ffn2_w_gate, ffn2_w_up, ffn2_w_down, ln3_g, ln3_b, loss_target, m_meta, m_ffn1_w_gate, m_ffn1_w_up, m_ffn1_w_down, m_ln1_g, m_ln1_b, m_w_in, m_conv_w, m_conv_b, m_dt_bias, m_a_log, m_d_skip, m_ssd_norm_g, m_fox_f_b, m_mla_q_norm_g, m_mla_w_uq, m_mla_kv_norm_g, m_mla_w_ukv, m_w_out, m_ln2_g, m_ln2_b, m_ffn2_w_gate, m_ffn2_w_up, m_ffn2_w_down, m_ln3_g, m_ln3_b, v_meta, v_ffn1_w_gate, v_ffn1_w_up, v_ffn1_w_down, v_ln1_g, v_ln1_b, v_w_in, v_conv_w, v_conv_b, v_dt_bias, v_a_log, v_d_skip, v_ssd_norm_g, v_fox_f_b, v_mla_q_norm_g, v_mla_w_uq, v_mla_kv_norm_g, v_mla_w_ukv, v_w_out, v_ln2_g, v_ln2_b, v_ffn2_w_gate, v_ffn2_w_up, v_ffn2_w_down, v_ln3_g, v_ln3_b):
    args = dict(locals())
    w = {n: args[n] for n in WEIGHTS}
    m = {n: args["m_" + n] for n in WEIGHTS}
    v = {n: args["v_" + n] for n in WEIGHTS}
    xcoord, ycoord, _ = _my_pos()
    chip = 2 * xcoord + ycoord

    tr = lambda a: jnp.swapaxes(a, 1, 2)

    def bf16_shard(n, l, zero=None):
        a = w[n] if zero is None else w[n] + zero
        if n.endswith("w_gate") or n.endswith("w_up"):
            a = _pad_to(tr(a), 1, HP)
        elif n.endswith("w_down"):
            a = _pad_to(a, 1, HP)
        elif n == "w_in":
            a = _pad_to(a, 2, IN_SHARD_P)
        return a[l].astype(BF16)

    land_shape = lambda s: jax.ShapeDtypeStruct((N_CHIPS,) + s.shape, s.dtype)

    def gather_start(names, l, tag, after):
        srcs = [bf16_shard(n, l, None if after is None else after[0, 0]) for n in names]
        return _ici_start(_gather_copies, srcs, [land_shape(s) for s in srcs], name=f"gather_ici_{tag}_start",
                          after=[tiny[0]] if after is None else [after])

    def gather_finish(handle, names, l, tag, after):
        srcs, lands = _ici_wait(_gather_copies, *handle[:4], after, name=f"gather_ici_{tag}_wait")
        use_gathered(l, names, _gather_d2d(srcs, lands, tag))

    def gather_d2d_start(handle, tag, after):
        srcs, lands = _ici_wait(_gather_copies, *handle[:4], after, name=f"gather_ici_{tag}_wait")
        return _ici_start(_d2d_copies, srcs, lands, name=f"gather_d2d_{tag}_start", sems_per_array=D2D_COPIES)

    def gather_d2d_finish(handle, names, l, tag, after):
        _, lands = _ici_wait(_d2d_copies, *handle[:4], after, name=f"gather_d2d_{tag}_wait")
        use_gathered(l, names, lands)

    tiny = _allgather_chips([w["meta"].reshape(2, N_META // 2, D_MODEL // N_CHIPS), w["conv_w"]])
    meta_full = jnp.concatenate([tiny[0][k].reshape(N_META, D_MODEL // N_CHIPS) for k in range(N_CHIPS)], axis=1)

    W = {n: [None] * DEPTH for n in MATMUL_W + ["w_in_p", "mla_w_uq_p", "mla_w_ukv_p"]}
    W["conv_w"] = jnp.concatenate([tiny[1][k] for k in range(N_CHIPS)], axis=-1)
    W["meta"] = meta_full
    for n in SMALL:
        W[n] = w[n]

    def use_gathered(l, names, lands):
        got = dict(zip(names, lands))
        cat = lambda n, cut=None: jnp.concatenate([got[n][k][..., :cut] for k in range(N_CHIPS)], axis=-1)
        for n in names:
            W[n][l] = got[n]
        if "w_in" in got:
            W["w_in_p"][l] = _pad_in_proj(cat("w_in", IN_SHARD))
            W["mla_w_uq_p"][l] = _regroup_uq(cat("mla_w_uq"))
            W["mla_w_ukv_p"][l] = _regroup_ukv(cat("mla_w_ukv"))

    def chunk_grads(G, names):
        def chunked(name, ungroup, width, pad):
            full = ungroup(G[name])
            return _pad_to(jnp.moveaxis(full.reshape(full.shape[0], N_CHIPS, width), 1, 0), 2, pad)
        special = {"mla_w_uq": ("mla_w_uq_p", _ungroup_uq, MLA_NOPE + MLA_ROPE, MLA_NOPE + MLA_ROPE),
                   "mla_w_ukv": ("mla_w_ukv_p", _ungroup_ukv, MLA_NOPE + MLA_V, MLA_NOPE + MLA_V)}
        return [_in_proj_grad_chunks(G["w_in_p"]) if n == "w_in" else chunked(*special[n]) if n in special else G[n]
                for n in names]

    def rs_start(G, names, tag):
        pairs = _rs_pair_sums(chunk_grads(G, names), names, tag)
        handle = _ici_start(_exchange_copies, [p[1] for p in pairs], _exchange_land_shapes(pairs),
                            name=f"rs_exchange_{tag}_start")
        return pairs, handle

    def swap_start(G, names, tag):
        gs = chunk_grads(G, names)
        return _ici_start(_swap_copies, gs, _swap_land_shapes(gs), name=f"rs_swap_{tag}_start")

    def exchange_start(swap_handle, names, tag, after):
        gs, r1 = _ici_wait(_swap_copies, *swap_handle[:4], after, name=f"rs_swap_{tag}_wait")
        pairs = _rs_add_pairs(gs, r1, names, tag)
        handle = _ici_start(_exchange_copies, [p[1] for p in pairs], _exchange_land_shapes(pairs),
                            name=f"rs_exchange_{tag}_start")
        return pairs, handle

    def rs_end(pairs, handle, names, tag, after):
        _, r2 = _ici_wait(_exchange_copies, *handle[:4], after, name=f"rs_exchange_{tag}_wait")
        return dict(zip(names, _rs_finish(pairs, r2, names, tag)))

    ffn1_w, mix_w, ffn2_w = MATMUL_W[:3], MATMUL_W[3:7], MATMUL_W[7:]
    g_a = gather_start(ffn1_w, 0, "l0_ffn1", None)
    g_b = gather_start(mix_w, 0, "l0_mix", g_a[4])
    g_c = gather_start(ffn2_w, 0, "l0_ffn2", g_b[4])
    g_l1 = gather_start(MATMUL_W, 1, "l1", g_c[4])
    token = g_l1[4]
    cosf, sins = _rope_tables()
    ln = lambda n, l: W[n][l][None]
    h = _embed(x[0] + token[0, 0], meta_full)
    h = (h, h.astype(BF16))
    gather_finish(g_a, ffn1_w, 0, "l0_ffn1", h[1])
    h1, s1 = _ffn_fwd(h, W, "ffn1", 0, ln("ln1_g", 0), ln("ln1_b", 0), "l0_ffn1")
    gather_finish(g_b, mix_w, 0, "l0_mix", h1[1])
    d_c = gather_d2d_start(g_c, "l0_ffn2", W["w_in_p"][0])
    h2, sm = _mixer_fwd(h1, W, 0, cosf, sins, after=[d_c[4]])
    gather_d2d_finish(d_c, ffn2_w, 0, "l0_ffn2", h2[1])
    d_l1 = gather_d2d_start(g_l1, "l1", W["ffn2_w_gate"][0])
    h, s2 = _ffn_fwd(h2, W, "ffn2", 0, ln("ln3_g", 0), ln("ln3_b", 0), "l0_ffn2", after=[d_l1[4]])
    saved0 = (s1, sm, s2)
    gather_d2d_finish(d_l1, MATMUL_W, 1, "l1", h[1])
    h, saved1 = _layer_fwd(h, W, 1, cosf, sins)
    tgt = jnp.concatenate([jnp.zeros((BLOCK, D_MODEL), F32), loss_target[0]], axis=0)
    dh, loss = _loss_head(h[0], tgt, name="loss_head")
    G = [None] * DEPTH
    dh, G[1] = _layer_bwd(dh, saved1, W, 1, cosf, sins)
    ffn2_w, mix_w, ffn1_w = MATMUL_W[7:], MATMUL_W[3:7], MATMUL_W[:3]
    sw_l1 = swap_start(G[1], MATMUL_W, "l1")
    G0 = {}
    dh, dg, db = _ffn_bwd(dh, s2, W, "ffn2", 0, ln("ln3_g", 0) + sw_l1[4][0, 0], ln("ln3_b", 0), G0, "l0_ffn2")
    G0["ln3_g"], G0["ln3_b"] = dg[0], db[0]
    pairs_l1, x_l1 = exchange_start(sw_l1, MATMUL_W, "l1", dh)
    sw_a = swap_start(G0, ffn2_w, "l0_ffn2")
    dh, Gm = _mixer_bwd(dh, sm, W, 0, cosf, sins, G0, zero=x_l1[4][0, 0] + sw_a[4][0, 0])
    G0.update(Gm)
    pairs_a, x_a = exchange_start(sw_a, ffn2_w, "l0_ffn2", dh)
    reduced1 = rs_end(pairs_l1, x_l1, MATMUL_W, "l1", dh)
    pairs_b, x_b = rs_start(G0, mix_w, "l0_mix")
    dh0, dg, db = _ffn_bwd(dh, s1, W, "ffn1", 0, ln("ln1_g", 0) + (x_a[4][0, 0] + x_b[4][0, 0]), ln("ln1_b", 0), G0,
                           "l0_ffn1")
    G0["ln1_g"], G0["ln1_b"] = dg[0], db[0]
    G[0] = G0
    reduced0 = rs_end(pairs_a, x_a, ffn2_w, "l0_ffn2", dh0)
    reduced0.update(rs_end(pairs_b, x_b, mix_w, "l0_mix", dh0))

    small_parts = [jnp.stack([G[l][n] for l in range(DEPTH)]).reshape(-1) for n in SMALL]
    small_parts += [jnp.stack([G[l]["conv_w"] for l in range(DEPTH)]).reshape(-1), dh0[PAD_ROWS:BLOCK].reshape(-1),
                    loss[0, :1]]
    sw_c = swap_start(G0, ffn1_w, "l0_ffn1")
    flat = jnp.concatenate(small_parts) + sw_c[4][0, 0]
    flat = jnp.pad(flat, (0, SMALL_ROWS * BLOCK - flat.shape[0]))
    red2d = _allreduce_small(flat.reshape(SMALL_ROWS, BLOCK))
    red = red2d.reshape(-1)
    pairs_c, x_c = exchange_start(sw_c, ffn1_w, "l0_ffn1", red2d)
    grads, off = {}, 0
    for n in SMALL:
        size = int(np.prod(w[n].shape))
        grads[n] = red[off:off + size].reshape(w[n].shape)
        off += size
    conv_full = red[off:off + DEPTH * SSD_CONV * 768].reshape(DEPTH, SSD_CONV, 768)
    off += DEPTH * SSD_CONV * 768
    dmeta_full = red[off:off + N_META * D_MODEL].reshape(N_META, D_MODEL)
    off += N_META * D_MODEL
    loss_out = red[off]
    grads["conv_w"] = _chip_cols(conv_full, chip, 768 // N_CHIPS)
    grads["meta"] = _chip_cols(dmeta_full, chip, D_MODEL // N_CHIPS)

    delta, new_m, new_v = {}, {}, {}

    def adamw_matmul_weights(names, after):
        done = []
        for n in names:
            gs = [reduced0[n], reduced1[n]]
            if n.endswith("w_gate") or n.endswith("w_up"):
                res = _adamw(tr(w[n]), gs, tr(m[n]), tr(v[n]), name=f"adamw_{n}", after=after)
                grads[n], delta[n], new_m[n], new_v[n] = [tr(r) for r in res]
            else:
                res = _adamw(w[n], gs, m[n], v[n], name=f"adamw_{n}", after=after)
                grads[n], delta[n], new_m[n], new_v[n] = res
            done.append(res[1])
        return done

    early_done = adamw_matmul_weights(ffn2_w + mix_w, [x_c[4]])
    rest = [n for n in WEIGHTS if n not in MATMUL_W]

    def pack_small(d):
        f = jnp.concatenate([d[n].reshape(-1) for n in rest])
        tot = -(-f.shape[0] // (8 * BLOCK)) * 8 * BLOCK
        return jnp.pad(f, (0, tot - f.shape[0])).reshape(-1, BLOCK)

    _, d2, m2, v2 = _adamw(pack_small(w), [pack_small(grads)], pack_small(m), pack_small(v), name="adamw_small",
                           after=[x_c[4]])
    reduced0.update(rs_end(pairs_c, x_c, ffn1_w, "l0_ffn1", [d2] + early_done))
    adamw_matmul_weights(ffn1_w, [])
    off = 0
    for n in rest:
        size = int(np.prod(w[n].shape))
        for dst, src in ((delta, d2), (new_m, m2), (new_v, v2)):
            dst[n] = src.reshape(-1)[off:off + size].reshape(w[n].shape)
        off += size

    grad_x = dh0[BLOCK:][None]
    return (loss_out, grad_x, *[grads[n] for n in WEIGHTS], *[delta[n] for n in WEIGHTS],
            *[new_m[n] for n in WEIGHTS], *[new_v[n] for n in WEIGHTS])
```

```python
import functools

import numpy as np
import jax
import jax.numpy as jnp
from jax import lax
from jax.experimental import pallas as pl
from jax.experimental.pallas import tpu as pltpu

F32 = jnp.float32
BF16 = jnp.bfloat16
MESH = pl.DeviceIdType.MESH

D_MODEL = 1024
SEQ = 2048
N_META = 16
BLOCK = 128
PAD_ROWS = 112
LP = PAD_ROWS + N_META + SEQ
N_CHUNK = LP // BLOCK
DEPTH = 2
D_FF = 2816
N_CHIPS = 4
FF_SHARD = D_FF // N_CHIPS
HP = 768
FP = N_CHIPS * HP
SSD_HEADS, SSD_HD, SSD_D, SSD_GROUPS, SSD_STATE, SSD_CONV = 8, 64, 512, 2, 64, 4
FOX_HEADS, FOX_HD, FOX_D = 4, 64, 256
MLA_HEADS, MLA_Q_LORA, MLA_KV_LORA, MLA_NOPE, MLA_ROPE, MLA_V, MLA_D = 4, 256, 128, 64, 32, 64, 256
ROPE_HALF = MLA_ROPE // 2
ROPE_THETA = 10000.0
N_IN = 2476
IN_SHARD = N_IN // N_CHIPS
IN_SHARD_P = 640
ALPHA = (2 * DEPTH) ** 0.25
EPS = 1e-5
ADAM_LR, ADAM_B1, ADAM_B2, ADAM_EPS, ADAM_WD, ADAM_STEP = 0.001, 0.9, 0.999, 1e-08, 0.01, 10
NEG = -1e30
TM = 544

VMEM_LIMIT_BYTES = 56 * 1024 * 1024

PC_Z, PC_XBC, PC_FQ, PC_FK, PC_FV, PC_CQ, PC_CKV, PC_DT, PC_FR, PC_KR, PC_END = (
    0, 512, 1280, 1536, 1792, 2048, 2304, 2432, 2560, 2688, 2816)
OC_Z, OC_XBC, OC_DT, OC_FQ, OC_FK, OC_FV, OC_FR, OC_CQ, OC_CKV, OC_KR = (
    0, 512, 1280, 1288, 1544, 1800, 2056, 2060, 2316, 2444)


def _cparams(sem=None):
    return pltpu.CompilerParams(dimension_semantics=sem, vmem_limit_bytes=VMEM_LIMIT_BYTES)


def _tile(n, cap, mult):
    best = None
    for t in range(mult, min(n, cap) + 1, mult):
        if n % t == 0:
            best = t
    return best if best is not None else n


def _bs(shape, fn):
    return pl.BlockSpec(shape, fn)


ANY = pl.BlockSpec(memory_space=pl.ANY)


def _dims(ca, cb):
    return (((ca,), (cb,)), ((), ()))


def _raw_bdot(a, b, ca, cb):
    return lax.dot_general(a.astype(BF16), b.astype(BF16), _dims(ca, cb), preferred_element_type=F32)


def _mm_core(a, b, *, a_spec, b_spec, o_spec, grid, out_shape, ca, cb, name, add=None, after=()):
    nk = grid[2]
    has_add = add is not None
    acc_shape = tuple(d for d in o_spec.block_shape if d is not None)

    def body(*refs):
        a_ref, b_ref = refs[0], refs[1]
        add_ref = refs[2] if has_add else None
        o_ref, acc_ref = refs[-2], refs[-1]
        k = pl.program_id(2)

        @pl.when(k == 0)
        def _():
            acc_ref[...] = jnp.zeros_like(acc_ref)

        acc_ref[...] += _raw_bdot(a_ref[...], b_ref[...], ca, cb)

        @pl.when(k == nk - 1)
        def _():
            r = acc_ref[...]
            if has_add:
                r = r + add_ref[...]
            o_ref[...] = r

    ins = [a, b] + ([add] if has_add else []) + list(after)
    in_specs = [a_spec, b_spec] + ([o_spec] if has_add else []) + [ANY] * len(after)
    return pl.pallas_call(
        body, name=name, grid=grid, in_specs=in_specs, out_specs=o_spec,
        out_shape=jax.ShapeDtypeStruct(out_shape, F32), scratch_shapes=[pltpu.VMEM(acc_shape, F32)],
        compiler_params=_cparams(("parallel", "parallel", "arbitrary")),
    )(*ins)


MM_VMEM_BUDGET = 40 * 1024 * 1024


def _divisors(n, mult):
    return [t for t in range(mult, n + 1, mult) if n % t == 0] or [n]


def _pick_tiles(M, N, K, a_bytes, b_bytes, ta, has_add):
    best = None
    for tm in _divisors(M, 128 if ta else 16):
        for tn in _divisors(N, 128):
            vmem = 2 * tm * K * a_bytes + 2 * K * tn * b_bytes + (3 + 2 * int(has_add)) * tm * tn * 4
            if vmem <= MM_VMEM_BUDGET:
                key = ((M // tm) * (N // tn), -tn)
                if best is None or key < best[0]:
                    best = (key, tm, tn)
    assert best is not None, (M, N, K)
    return best[1], best[2], K


def _mm(a, b, *, ta=False, tb=False, add=None, name, after=()):
    if ta:
        K, M = a.shape
    else:
        M, K = a.shape
    if tb:
        N, Kb = b.shape
    else:
        Kb, N = b.shape
    assert K == Kb, (a.shape, b.shape, ta, tb)
    tm, tn, tk = _pick_tiles(M, N, K, a.dtype.itemsize, b.dtype.itemsize, ta, add is not None)
    a_spec = _bs((tk, tm), lambda i, j, k: (k, i)) if ta else _bs((tm, tk), lambda i, j, k: (i, k))
    b_spec = _bs((tn, tk), lambda i, j, k: (j, k)) if tb else _bs((tk, tn), lambda i, j, k: (k, j))
    return _mm_core(a, b, a_spec=a_spec, b_spec=b_spec, o_spec=_bs((tm, tn), lambda i, j, k: (i, j)),
                    grid=(M // tm, N // tn, K // tk), out_shape=(M, N), ca=0 if ta else 1, cb=1 if tb else 0,
                    name=name, add=add, after=after)


def _row_entry(r, ncol):
    if isinstance(r, tuple):
        return r
    return r, r.shape[1] // ncol, 0


def _rowwise(fn, rows, pars, out_cols, *, name, tile, ncol=1, out_dtypes=None):
    rows = [_row_entry(r, ncol) for r in rows]
    L = rows[0][0].shape[0]
    nr, npar = len(rows), len(pars)
    in_specs = [_bs((tile, w), lambda g, i, o=o: (i, o + g)) for _, w, o in rows]
    in_specs += [_bs((p.shape[0], p.shape[1] // ncol), lambda g, i: (0, g)) for p in pars]
    out_specs = [_bs((tile, c // ncol), lambda g, i: (i, g)) for c in out_cols]

    def body(*refs):
        ins, outs = refs[:nr + npar], refs[nr + npar:]
        row0 = pl.program_id(1) * tile
        res = fn(row0, *[r[...] for r in ins])
        for o, v in zip(outs, res):
            o[...] = v.astype(o.dtype)

    return pl.pallas_call(
        body, name=name, grid=(ncol, L // tile), in_specs=in_specs, out_specs=out_specs,
        out_shape=[jax.ShapeDtypeStruct((L, c), d) for c, d in zip(out_cols, out_dtypes or [F32] * len(out_cols))],
        compiler_params=_cparams(("parallel", "parallel")),
    )(*[r[0] for r in rows], *pars)


def _rowwise_bwd(fn, rows, pars, douts, *, name, tile, ncol=1, row_grad=None, grad_dtypes=None):
    rows = [_row_entry(r, ncol) for r in rows]
    L = rows[0][0].shape[0]
    nr, npar, nd = len(rows), len(pars), len(douts)
    row_grad = [True] * nr if row_grad is None else row_grad
    in_specs = [_bs((tile, w), lambda g, i, o=o: (i, o + g)) for _, w, o in rows]
    in_specs += [_bs((p.shape[0], p.shape[1] // ncol), lambda g, i: (0, g)) for p in pars]
    in_specs += [_bs((tile, d.shape[1] // ncol), lambda g, i: (i, g)) for d in douts]
    g_widths = [w * ncol for (_, w, _), f in zip(rows, row_grad) if f]
    out_specs = [_bs((tile, w // ncol), lambda g, i: (i, g)) for w in g_widths]
    out_specs += [_bs((p.shape[0], p.shape[1] // ncol), lambda g, i: (0, g)) for p in pars]
    out_shape = [jax.ShapeDtypeStruct((L, w), d) for w, d in zip(g_widths, grad_dtypes or [F32] * len(g_widths))]
    out_shape += [jax.ShapeDtypeStruct(p.shape, F32) for p in pars]

    def body(*refs):
        ins = refs[:nr + npar]
        dos = refs[nr + npar:nr + npar + nd]
        outs = refs[nr + npar + nd:]
        i = pl.program_id(1)
        row0 = i * tile
        _, vjp = jax.vjp(lambda *a: tuple(fn(row0, *a)), *[r[...] for r in ins])
        grads = vjp(tuple(d[...].astype(F32) for d in dos))
        o = 0
        for j in range(nr):
            if row_grad[j]:
                outs[o][...] = grads[j].astype(outs[o].dtype)
                o += 1
        for j in range(npar):
            g, ref = grads[nr + j], outs[o + j]

            @pl.when(i == 0)
            def _(g=g, ref=ref):
                ref[...] = g

            @pl.when(i > 0)
            def _(g=g, ref=ref):
                ref[...] += g

    res = pl.pallas_call(
        body, name=name, grid=(ncol, L // tile), in_specs=in_specs, out_specs=out_specs, out_shape=out_shape,
        compiler_params=_cparams(("parallel", "arbitrary")),
    )(*[r[0] for r in rows], *pars, *douts)
    return res[:len(g_widths)], res[len(g_widths):]


def _sigmoid(x):
    return 1.0 / (1.0 + jnp.exp(-x))


def _softplus(x):
    return jnp.maximum(x, 0.0) + jnp.log(1.0 + jnp.exp(-jnp.abs(x)))


def _silu(x):
    return x * _sigmoid(x)


def _make_res_ln_fn(scale):
    def fn(row0, h, o, gam, bet):
        pre = ALPHA * h + scale * o
        mu = jnp.mean(pre, axis=-1, keepdims=True)
        xc = pre - mu
        var = jnp.mean(xc * xc, axis=-1, keepdims=True)
        return (xc * lax.rsqrt(var + EPS) * gam + bet,)
    return fn


def _ssd_post_fn(row0, y, xs, z, dskip, normg):
    v = (y + dskip * xs) * _silu(z)
    v = v * lax.rsqrt(jnp.mean(v * v, axis=-1, keepdims=True) + EPS)
    return (v * normg,)


def _mla_norm_fn(row0, cq, ckv, gq, gkv):
    qn = cq * lax.rsqrt(jnp.mean(cq * cq, axis=-1, keepdims=True) + EPS) * gq
    cn = ckv * lax.rsqrt(jnp.mean(ckv * ckv, axis=-1, keepdims=True) + EPS) * gkv
    return qn, cn


def _rope_fn(row0, q, k, cosf, sins):
    return (q * cosf + pltpu.roll(q, 64, 1) * sins, k * cosf + pltpu.roll(k, 64, 1) * sins)


def _rope_t_fn(row0, gq, gk, cosf, sins):
    return (gq * cosf + pltpu.roll(gq * sins, 64, 1), gk * cosf + pltpu.roll(gk * sins, 64, 1))


def _conv_fwd(x, x_off, w, b, *, name):
    C = w.shape[1]

    def body(x_ref, w_ref, b_ref, o_ref):
        rows = lax.broadcasted_iota(jnp.int32, (LP, BLOCK), 0)
        xv = jnp.where(rows >= PAD_ROWS, x_ref[...], 0.0)
        acc = b_ref[...] + w_ref[3:4, :] * xv
        for k in range(SSD_CONV - 1):
            acc = acc + w_ref[k:k + 1, :] * pltpu.roll(xv, SSD_CONV - 1 - k, 0)
        o_ref[...] = _silu(acc)

    return pl.pallas_call(
        body, name=name, grid=(C // BLOCK,),
        in_specs=[_bs((LP, BLOCK), lambda j: (0, j + x_off)), _bs((SSD_CONV, BLOCK), lambda j: (0, j)),
                  _bs((1, BLOCK), lambda j: (0, j))],
        out_specs=_bs((LP, BLOCK), lambda j: (0, j)),
        out_shape=jax.ShapeDtypeStruct((LP, C), F32), compiler_params=_cparams(("parallel",)),
    )(x, w, b)


def _conv_bwd(x, x_off, w, b, dout, *, name):
    C = w.shape[1]

    def body(x_ref, w_ref, b_ref, do_ref, dx_ref, dw_ref, db_ref):
        rows = lax.broadcasted_iota(jnp.int32, (LP, BLOCK), 0)
        real = rows >= PAD_ROWS
        xv = jnp.where(real, x_ref[...], 0.0)
        shifted = [pltpu.roll(xv, SSD_CONV - 1 - k, 0) for k in range(SSD_CONV - 1)] + [xv]
        acc = b_ref[...]
        for k in range(SSD_CONV):
            acc = acc + w_ref[k:k + 1, :] * shifted[k]
        sig = _sigmoid(acc)
        dacc = jnp.where(real, do_ref[...] * (sig * (1.0 + acc * (1.0 - sig))), 0.0)
        db_ref[...] = jnp.sum(dacc, axis=0, keepdims=True)
        dx = w_ref[3:4, :] * dacc
        for k in range(SSD_CONV):
            dw_ref[k:k + 1, :] = jnp.sum(dacc * shifted[k], axis=0, keepdims=True)
            if k < SSD_CONV - 1:
                dx = dx + w_ref[k:k + 1, :] * pltpu.roll(dacc, LP - (SSD_CONV - 1 - k), 0)
        dx_ref[...] = jnp.where(real, dx, 0.0)

    return pl.pallas_call(
        body, name=name, grid=(C // BLOCK,),
        in_specs=[_bs((LP, BLOCK), lambda j: (0, j + x_off)), _bs((SSD_CONV, BLOCK), lambda j: (0, j)),
                  _bs((1, BLOCK), lambda j: (0, j)), _bs((LP, BLOCK), lambda j: (0, j))],
        out_specs=[_bs((LP, BLOCK), lambda j: (0, j)), _bs((SSD_CONV, BLOCK), lambda j: (0, j)),
                   _bs((1, BLOCK), lambda j: (0, j))],
        out_shape=[jax.ShapeDtypeStruct((LP, C), F32), jax.ShapeDtypeStruct((SSD_CONV, C), F32),
                   jax.ShapeDtypeStruct((1, C), F32)],
        compiler_params=_cparams(("parallel",)),
    )(x, w, b, dout)


_BDIMS = {"nn": (((2,), (1,)), ((0,), (0,))), "nt": (((2,), (2,)), ((0,), (0,))), "tn": (((1,), (1,)), ((0,), (0,)))}


def _raw_bdot3(a, b, mode):
    return lax.dot_general(a.astype(BF16), b.astype(BF16), _BDIMS[mode], preferred_element_type=F32)


@functools.partial(jax.custom_vjp, nondiff_argnums=(2,))
def _bdot3(a, b, mode):
    return _raw_bdot3(a, b, mode)


def _bdot3_fwd(a, b, mode):
    return _raw_bdot3(a, b, mode), (a, b)


def _bdot3_bwd(mode, res, g):
    a, b = res
    if mode == "nn":
        return _raw_bdot3(g, b, "nt"), _raw_bdot3(a, g, "tn")
    if mode == "nt":
        return _raw_bdot3(g, b, "nn"), _raw_bdot3(g, a, "tn")
    return _raw_bdot3(b, g, "nt"), _raw_bdot3(a, g, "nn")


_bdot3.defvjp(_bdot3_fwd, _bdot3_bwd)


def _ssd_chunk(x, bm, cm, dt, dtt, alog, prev):
    rep = SSD_HEADS // SSD_GROUPS
    per_head = lambda t: jnp.broadcast_to(t[:, None], (SSD_GROUPS, rep) + t.shape[1:]).reshape((SSD_HEADS,) + t.shape[1:])
    bm, cm = per_head(bm), per_head(cm)
    lane_h = lax.broadcasted_iota(jnp.int32, (1, BLOCK), 1)
    row_h = lax.broadcasted_iota(jnp.int32, (BLOCK, 1), 0)
    dtc = jnp.stack([jnp.sum(jnp.where(lane_h == h, dt, 0.0), axis=1, keepdims=True) for h in range(SSD_HEADS)])
    dtr = jnp.stack([jnp.sum(jnp.where(row_h == h, dtt, 0.0), axis=0, keepdims=True) for h in range(SSD_HEADS)])
    lane = lax.broadcasted_iota(jnp.int32, alog.shape, 2)
    a_neg = -jnp.exp(jnp.sum(jnp.where(lane == 0, alog, 0.0), axis=2, keepdims=True))
    ac_in = dtc * a_neg
    ar_in = dtr * a_neg
    li = lax.broadcasted_iota(jnp.int32, (1, BLOCK, BLOCK), 1)
    si = lax.broadcasted_iota(jnp.int32, (1, BLOCK, BLOCK), 2)
    causal = li >= si
    acum_c = jnp.sum(jnp.where(causal, ar_in, 0.0), axis=2, keepdims=True)
    acum_r = jnp.sum(jnp.where(li <= si, ac_in, 0.0), axis=1, keepdims=True)
    total = jnp.sum(ar_in, axis=2, keepdims=True)
    seg = jnp.exp(jnp.where(causal, acum_c - acum_r, NEG))
    xdt = x * dtc
    cb = _bdot3(cm, bm, "nt")
    y = _bdot3(cb * seg, xdt, "nn") + _bdot3(cm, prev, "nt") * jnp.exp(acum_c)
    st = _bdot3(xdt, bm * jnp.exp(total - acum_c), "tn")
    return y, prev * jnp.exp(total) + st


def _ssd_dt_fwd(raw, raw_blk, bias, *, name):
    def body(raw_ref, b_ref, dt_ref, dtt_ref):
        rows = lax.broadcasted_iota(jnp.int32, (LP, BLOCK), 0)
        dt = jnp.where(rows >= PAD_ROWS, _softplus(raw_ref[...] + b_ref[...]), 0.0)
        dt_ref[...] = dt
        dtt_ref[...] = dt.T

    return pl.pallas_call(
        body, name=name, grid=(1,),
        in_specs=[_bs((LP, BLOCK), lambda j: (0, raw_blk)), _bs((1, BLOCK), lambda j: (0, 0))],
        out_specs=[_bs((LP, BLOCK), lambda j: (0, 0)), _bs((BLOCK, LP), lambda j: (0, 0))],
        out_shape=[jax.ShapeDtypeStruct((LP, BLOCK), F32), jax.ShapeDtypeStruct((BLOCK, LP), F32)],
        compiler_params=_cparams(("arbitrary",)),
    )(raw, bias)


def _ssd_dt_bwd(raw, raw_blk, bias, ddt, ddtt, *, name):
    def body(raw_ref, b_ref, ddt_ref, ddtt_ref, draw_ref, db_ref):
        rows = lax.broadcasted_iota(jnp.int32, (LP, BLOCK), 0)
        g = ddt_ref[...] + ddtt_ref[...].T
        draw = jnp.where(rows >= PAD_ROWS, g * _sigmoid(raw_ref[...] + b_ref[...]), 0.0)
        draw_ref[...] = draw
        db_ref[...] = jnp.sum(draw, axis=0, keepdims=True)

    return pl.pallas_call(
        body, name=name, grid=(1,),
        in_specs=[_bs((LP, BLOCK), lambda j: (0, raw_blk)), _bs((1, BLOCK), lambda j: (0, 0)),
                  _bs((LP, BLOCK), lambda j: (0, 0)), _bs((BLOCK, LP), lambda j: (0, 0))],
        out_specs=[_bs((LP, BLOCK), lambda j: (0, 0)), _bs((1, BLOCK), lambda j: (0, 0))],
        out_shape=[jax.ShapeDtypeStruct((LP, BLOCK), F32), jax.ShapeDtypeStruct((1, BLOCK), F32)],
        compiler_params=_cparams(("arbitrary",)),
    )(raw, bias, ddt, ddtt)


def _ssd_specs(rev):
    ci = (lambda c: N_CHUNK - 1 - c) if rev else (lambda c: c)
    x_spec = _bs((SSD_HEADS, BLOCK, SSD_HD), lambda c: (0, ci(c), 0))
    g_spec = _bs((SSD_GROUPS, BLOCK, SSD_STATE), lambda c: (0, ci(c), 0))
    dtc_spec = _bs((BLOCK, BLOCK), lambda c: (ci(c), 0))
    dtr_spec = _bs((BLOCK, BLOCK), lambda c: (0, ci(c)))
    al_spec = _bs((SSD_HEADS, 1, BLOCK), lambda c: (0, 0, 0))
    st_spec = _bs((None, SSD_HEADS, SSD_HD, SSD_STATE), lambda c: (ci(c), 0, 0, 0))
    return x_spec, g_spec, dtc_spec, dtr_spec, al_spec, st_spec


def _ssd_fwd(x, bm, cm, dtc, dtr, alog, *, name):
    x_spec, g_spec, dtc_spec, dtr_spec, al_spec, st_spec = _ssd_specs(False)

    def body(x_ref, b_ref, c_ref, dtc_ref, dtr_ref, al_ref, y_ref, prev_ref, state):
        @pl.when(pl.program_id(0) == 0)
        def _():
            state[...] = jnp.zeros_like(state)

        prev = state[...]
        prev_ref[...] = prev
        y, new = _ssd_chunk(x_ref[...], b_ref[...], c_ref[...], dtc_ref[...], dtr_ref[...], al_ref[...], prev)
        y_ref[...] = y
        state[...] = new

    return pl.pallas_call(
        body, name=name, grid=(N_CHUNK,),
        in_specs=[x_spec, g_spec, g_spec, dtc_spec, dtr_spec, al_spec], out_specs=[x_spec, st_spec],
        out_shape=[jax.ShapeDtypeStruct((SSD_HEADS, LP, SSD_HD), F32),
                   jax.ShapeDtypeStruct((N_CHUNK, SSD_HEADS, SSD_HD, SSD_STATE), F32)],
        scratch_shapes=[pltpu.VMEM((SSD_HEADS, SSD_HD, SSD_STATE), F32)],
        compiler_params=_cparams(("arbitrary",)),
    )(x, bm, cm, dtc, dtr, alog)


def _ssd_bwd(x, bm, cm, dtc, dtr, alog, prevs, dy, *, name):
    x_spec, g_spec, dtc_spec, dtr_spec, al_spec, st_spec = _ssd_specs(True)

    def body(x_ref, b_ref, c_ref, dtc_ref, dtr_ref, al_ref, prev_ref, dy_ref,
             dx_ref, db_ref, dc_ref, ddtc_ref, ddtr_ref, dal_ref, dstate):
        c = pl.program_id(0)

        @pl.when(c == 0)
        def _():
            dstate[...] = jnp.zeros_like(dstate)

        _, vjp = jax.vjp(_ssd_chunk, x_ref[...], b_ref[...], c_ref[...], dtc_ref[...], dtr_ref[...], al_ref[...],
                         prev_ref[...])
        dx, db, dc, ddtc, ddtr, dal, dprev = vjp((dy_ref[...], dstate[...]))
        dx_ref[...] = dx
        db_ref[...] = db
        dc_ref[...] = dc
        ddtc_ref[...] = ddtc
        ddtr_ref[...] = ddtr
        dstate[...] = dprev

        @pl.when(c == 0)
        def _():
            dal_ref[...] = dal

        @pl.when(c > 0)
        def _():
            dal_ref[...] += dal

    hs = jax.ShapeDtypeStruct((SSD_HEADS, LP, SSD_HD), F32)
    gs = jax.ShapeDtypeStruct((SSD_GROUPS, LP, SSD_STATE), F32)
    return pl.pallas_call(
        body, name=name, grid=(N_CHUNK,),
        in_specs=[x_spec, g_spec, g_spec, dtc_spec, dtr_spec, al_spec, st_spec, x_spec],
        out_specs=[x_spec, g_spec, g_spec, dtc_spec, dtr_spec, al_spec],
        out_shape=[hs, gs, gs, jax.ShapeDtypeStruct((LP, BLOCK), F32),
                   jax.ShapeDtypeStruct((BLOCK, LP), F32), jax.ShapeDtypeStruct((SSD_HEADS, 1, BLOCK), F32)],
        scratch_shapes=[pltpu.VMEM((SSD_HEADS, SSD_HD, SSD_STATE), F32)],
        compiler_params=_cparams(("arbitrary",)),
    )(x, bm, cm, dtc, dtr, alog, prevs, dy)


def _tri_dot(tri, v):
    hi = v.astype(BF16)
    r1 = v - hi.astype(F32)
    mid = r1.astype(BF16)
    lo = (r1 - mid.astype(F32)).astype(BF16)
    t = tri.astype(BF16)
    d = lambda p: lax.dot_general(t, p, _dims(1, 0), preferred_element_type=F32)
    return d(hi) + d(mid) + d(lo)


def _fox_gate_fwd(raw, raw_blk, bias, *, name):
    def body(raw_ref, b_ref, c_ref, ct_ref):
        li = lax.broadcasted_iota(jnp.int32, (BLOCK, BLOCK), 0)
        si = lax.broadcasted_iota(jnp.int32, (BLOCK, BLOCK), 1)
        tri = jnp.where(li >= si, 1.0, 0.0)
        carry = jnp.zeros((1, BLOCK), F32)
        for j in range(N_CHUNK):
            r = slice(j * BLOCK, (j + 1) * BLOCK)
            lf = jnp.where(j * BLOCK + li >= PAD_ROWS, -_softplus(-(raw_ref[r, :] + b_ref[...])), 0.0)
            cv = _tri_dot(tri, lf) + carry
            c_ref[r, :] = cv
            ct_ref[:, r] = cv.T
            carry = carry + jnp.sum(lf, axis=0, keepdims=True)

    return pl.pallas_call(
        body, name=name, grid=(1,),
        in_specs=[_bs((LP, BLOCK), lambda j: (0, raw_blk)), _bs((1, BLOCK), lambda j: (0, 0))],
        out_specs=[_bs((LP, BLOCK), lambda j: (0, 0)), _bs((BLOCK, LP), lambda j: (0, 0))],
        out_shape=[jax.ShapeDtypeStruct((LP, BLOCK), F32), jax.ShapeDtypeStruct((BLOCK, LP), F32)],
        compiler_params=_cparams(("arbitrary",)),
    )(raw, bias)


def _fox_gate_bwd(raw, raw_blk, bias, dc, dct, *, name):
    def body(raw_ref, b_ref, dc_ref, dct_ref, draw_ref, db_ref):
        li = lax.broadcasted_iota(jnp.int32, (BLOCK, BLOCK), 0)
        si = lax.broadcasted_iota(jnp.int32, (BLOCK, BLOCK), 1)
        tri_t = jnp.where(li <= si, 1.0, 0.0)
        carry = jnp.zeros((1, BLOCK), F32)
        dsum = jnp.zeros((1, BLOCK), F32)
        for j in reversed(range(N_CHUNK)):
            r = slice(j * BLOCK, (j + 1) * BLOCK)
            dcv = dc_ref[r, :] + dct_ref[:, r].T
            dlf = _tri_dot(tri_t, dcv) + carry
            carry = carry + jnp.sum(dcv, axis=0, keepdims=True)
            draw = jnp.where(j * BLOCK + li >= PAD_ROWS, dlf * (1.0 - _sigmoid(raw_ref[r, :] + b_ref[...])), 0.0)
            draw_ref[r, :] = draw
            dsum = dsum + jnp.sum(draw, axis=0, keepdims=True)
        db_ref[...] = dsum

    return pl.pallas_call(
        body, name=name, grid=(1,),
        in_specs=[_bs((LP, BLOCK), lambda j: (0, raw_blk)), _bs((1, BLOCK), lambda j: (0, 0)),
                  _bs((LP, BLOCK), lambda j: (0, 0)), _bs((BLOCK, LP), lambda j: (0, 0))],
        out_specs=[_bs((LP, BLOCK), lambda j: (0, 0)), _bs((1, BLOCK), lambda j: (0, 0))],
        out_shape=[jax.ShapeDtypeStruct((LP, BLOCK), F32), jax.ShapeDtypeStruct((1, BLOCK), F32)],
        compiler_params=_cparams(("arbitrary",)),
    )(raw, bias, dc, dct)


ATT_W = 256
ATT_QB = 272
ATT_STEPS = LP // ATT_QB
ATT_KEYS = (640, 1152, 1664, LP)
ATT_BLOCKS_PER_CLASS = ATT_STEPS // len(ATT_KEYS)


def _lane_head(width, per, mod=None):
    lane = lax.broadcasted_iota(jnp.int32, (1, width), 1)
    if mod is not None:
        lane = lane % mod
    return lane // per


def _attn_mask(i, kw):
    r = i * ATT_QB + lax.broadcasted_iota(jnp.int32, (ATT_QB, kw), 0)
    c = lax.broadcasted_iota(jnp.int32, (ATT_QB, kw), 1)
    return (c <= r) & ((c >= PAD_ROWS) | (r < PAD_ROWS))


def _attn_by_key_class(i, fn):
    for p, kw in enumerate(ATT_KEYS):
        @pl.when(i // ATT_BLOCKS_PER_CLASS == p)
        def _(kw=kw):
            fn(kw)


def _attn_specs(q, k, v, bias, rope):
    qspec = lambda blk, w=ATT_W: _bs((ATT_QB, w), lambda i: (i, blk))
    fspec = lambda blk, w=ATT_W: _bs((LP, w), lambda i: (0, blk))
    ins = [q[0], k[0], v[0]]
    specs = [qspec(q[1]), fspec(k[1]), fspec(v[1])]
    if bias is not None:
        ins += [bias[0], bias[1]]
        specs += [qspec(0, BLOCK), _bs((BLOCK, LP), lambda i: (0, 0))]
    if rope is not None:
        ins += [rope[0][0], rope[1][0]]
        specs += [qspec(rope[0][1], BLOCK), fspec(rope[1][1], BLOCK)]
    return ins, specs, qspec, fspec


def _attn_fwd(q, k, v, *, scale, name, bias=None, rope=None):
    ins, specs, qspec, fspec = _attn_specs(q, k, v, bias, rope)
    has_bias, has_rope = bias is not None, rope is not None

    def body(*refs):
        it = iter(refs)
        q_ref, k_ref, v_ref = next(it), next(it), next(it)
        if has_bias:
            c_ref, ct_ref = next(it), next(it)
        if has_rope:
            qr_ref, kr_ref = next(it), next(it)
        o_ref, lse_ref = next(it), next(it)
        i = pl.program_id(0)

        def block(kw):
            ok = _attn_mask(i, kw)
            qv, kv, vv = q_ref[...].astype(BF16), k_ref[0:kw, :].astype(BF16), v_ref[0:kw, :].astype(BF16)
            hid, l128 = _lane_head(ATT_W, FOX_HD), _lane_head(BLOCK, 1)
            if has_rope:
                rid = _lane_head(BLOCK, ROPE_HALF, 64)
                qrv, krv = qr_ref[...].astype(BF16), kr_ref[0:kw, :].astype(BF16)
            def head(h, carry):
                o_acc, lse_acc = carry
                s = _raw_bdot(jnp.where(hid == h, qv, 0.0), kv, 1, 1)
                if has_rope:
                    s = s + _raw_bdot(jnp.where(rid == h, qrv, 0.0), krv, 1, 1)
                s = s * scale
                if has_bias:
                    cq = jnp.sum(jnp.where(l128 == h, c_ref[...], 0.0), axis=1, keepdims=True)
                    s = s + (cq - ct_ref[pl.ds(h, 1), 0:kw])
                s = jnp.where(ok, s, NEG)
                m = jnp.max(s, axis=1, keepdims=True)
                p = jnp.exp(s - m)
                l = jnp.sum(p, axis=1, keepdims=True)
                o_acc = jnp.where(hid == h, _raw_bdot(p, vv, 1, 0) / l, o_acc)
                lse_acc = jnp.where(l128 == h, m + jnp.log(l), lse_acc)
                return o_acc, lse_acc

            o_acc, lse_acc = lax.fori_loop(
                0, FOX_HEADS, head, (jnp.zeros((ATT_QB, ATT_W), F32), jnp.zeros((ATT_QB, BLOCK), F32)), unroll=True)
            o_ref[...] = o_acc
            lse_ref[...] = lse_acc

        _attn_by_key_class(i, block)

    return pl.pallas_call(
        body, name=name, grid=(ATT_STEPS,), in_specs=specs, out_specs=[qspec(0), qspec(0, BLOCK)],
        out_shape=[jax.ShapeDtypeStruct((LP, ATT_W), F32), jax.ShapeDtypeStruct((LP, BLOCK), F32)],
        compiler_params=_cparams(("parallel",)),
    )(*ins)


def _attn_bwd(q, k, v, o, lse, do, *, scale, name, bias=None, rope=None):
    ins, specs, qspec, fspec = _attn_specs(q, k, v, bias, rope)
    has_bias, has_rope = bias is not None, rope is not None
    ins += [o, lse, do[0]]
    specs += [qspec(0), qspec(0, BLOCK), qspec(do[1])]

    def body(*refs):
        it = iter(refs)
        q_ref, k_ref, v_ref = next(it), next(it), next(it)
        if has_bias:
            c_ref, ct_ref = next(it), next(it)
        if has_rope:
            qr_ref, kr_ref = next(it), next(it)
        o_ref, lse_ref, do_ref = next(it), next(it), next(it)
        dq_ref, dk_ref, dv_ref = next(it), next(it), next(it)
        if has_bias:
            dc_ref, dct_ref = next(it), next(it)
        if has_rope:
            dqr_ref, dkr_ref = next(it), next(it)
        i = pl.program_id(0)

        @pl.when(i == 0)
        def _():
            dk_ref[...] = jnp.zeros_like(dk_ref)
            dv_ref[...] = jnp.zeros_like(dv_ref)
            if has_rope:
                dkr_ref[...] = jnp.zeros_like(dkr_ref)
            if has_bias:
                dct_ref[...] = jnp.zeros_like(dct_ref)

        def block(kw):
            ok = _attn_mask(i, kw)
            qv, kv, vv = q_ref[...].astype(BF16), k_ref[0:kw, :].astype(BF16), v_ref[0:kw, :].astype(BF16)
            dov, lsev = do_ref[...], lse_ref[...]
            dov_ov = dov * o_ref[...]
            dov = dov.astype(BF16)
            hid, l128 = _lane_head(ATT_W, FOX_HD), _lane_head(BLOCK, 1)
            if has_rope:
                rid = _lane_head(BLOCK, ROPE_HALF, 64)
                qrv, krv = qr_ref[...].astype(BF16), kr_ref[0:kw, :].astype(BF16)

            def head(h, carry):
                dq_acc, aux_acc = carry
                qm = jnp.where(hid == h, qv, 0.0)
                s = _raw_bdot(qm, kv, 1, 1)
                if has_rope:
                    qrm = jnp.where(rid == h, qrv, 0.0)
                    s = s + _raw_bdot(qrm, krv, 1, 1)
                s = s * scale
                if has_bias:
                    cq = jnp.sum(jnp.where(l128 == h, c_ref[...], 0.0), axis=1, keepdims=True)
                    s = s + (cq - ct_ref[pl.ds(h, 1), 0:kw])
                s = jnp.where(ok, s, NEG)
                p = jnp.exp(s - jnp.sum(jnp.where(l128 == h, lsev, 0.0), axis=1, keepdims=True))
                dom = jnp.where(hid == h, dov, 0.0)
                dp = _raw_bdot(dom, vv, 1, 1)
                delta = jnp.sum(jnp.where(hid == h, dov_ov, 0.0), axis=1, keepdims=True)
                ds = p * (dp - delta)
                dsb, pb = ds.astype(BF16), p.astype(BF16)
                dq_acc = jnp.where(hid == h, _raw_bdot(dsb, kv, 1, 0) * scale, dq_acc)
                dk_ref[0:kw, :] += _raw_bdot(dsb, qm, 0, 0) * scale
                dv_ref[0:kw, :] += _raw_bdot(pb, dom, 0, 0)
                if has_rope:
                    aux_acc = jnp.where(rid == h, _raw_bdot(dsb, krv, 1, 0) * scale, aux_acc)
                    dkr_ref[0:kw, :] += _raw_bdot(dsb, qrm, 0, 0) * scale
                if has_bias:
                    aux_acc = jnp.where(l128 == h, jnp.sum(ds, axis=1, keepdims=True), aux_acc)
                    dct_ref[pl.ds(h, 1), 0:kw] -= jnp.sum(ds, axis=0, keepdims=True)
                return dq_acc, aux_acc

            dq_acc, aux_acc = lax.fori_loop(
                0, FOX_HEADS, head, (jnp.zeros((ATT_QB, ATT_W), F32), jnp.zeros((ATT_QB, BLOCK), F32)))
            dq_ref[...] = dq_acc
            if has_bias:
                dc_ref[...] = aux_acc
            if has_rope:
                dqr_ref[...] = aux_acc

        _attn_by_key_class(i, block)

    wide = jax.ShapeDtypeStruct((LP, ATT_W), F32)
    narrow = jax.ShapeDtypeStruct((LP, BLOCK), F32)
    out_specs = [qspec(0), fspec(0), fspec(0)]
    out_shape = [wide, wide, wide]
    if has_bias:
        out_specs += [qspec(0, BLOCK), _bs((BLOCK, LP), lambda i: (0, 0))]
        out_shape += [narrow, jax.ShapeDtypeStruct((BLOCK, LP), F32)]
    if has_rope:
        out_specs += [qspec(0, BLOCK), fspec(0, BLOCK)]
        out_shape += [narrow, narrow]
    return pl.pallas_call(
        body, name=name, grid=(ATT_STEPS,), in_specs=specs, out_specs=out_specs, out_shape=out_shape,
        compiler_params=_cparams(("arbitrary",)),
    )(*ins)


def _loss_head(y, target, *, name):
    tile = TM

    def body(y_ref, t_ref, dy_ref, loss_ref):
        i = pl.program_id(0)
        rows = i * tile + lax.broadcasted_iota(jnp.int32, (tile, D_MODEL), 0)
        err = jnp.where(rows >= BLOCK, y_ref[...] - t_ref[...], 0.0)
        dy_ref[...] = err * (1.0 / D_MODEL)
        part = 0.5 * jnp.sum(jnp.sum(err * err, axis=1, keepdims=True) * (1.0 / D_MODEL), axis=0, keepdims=True)
        part = jnp.broadcast_to(part, (1, BLOCK))

        @pl.when(i == 0)
        def _():
            loss_ref[...] = part

        @pl.when(i > 0)
        def _():
            loss_ref[...] += part

    return pl.pallas_call(
        body, name=name, grid=(LP // tile,),
        in_specs=[_bs((tile, D_MODEL), lambda i: (i, 0)), _bs((tile, D_MODEL), lambda i: (i, 0))],
        out_specs=[_bs((tile, D_MODEL), lambda i: (i, 0)), _bs((1, BLOCK), lambda i: (0, 0))],
        out_shape=[jax.ShapeDtypeStruct((LP, D_MODEL), F32), jax.ShapeDtypeStruct((1, BLOCK), F32)],
        compiler_params=_cparams(("arbitrary",)),
    )(y, target)


def _ffn_shard_prep(a, *, name, after=()):
    rows = a.shape[1]

    def body(a_ref, *refs):
        outs = refs[len(after):]
        for l in range(DEPTH):
            outs[l][0:rows, :] = a_ref[l].astype(BF16)
            outs[l][rows:HP, :] = jnp.zeros((HP - rows, D_MODEL), BF16)

    vm = pl.BlockSpec(memory_space=pltpu.VMEM)
    return pl.pallas_call(
        body, name=name, in_specs=[vm] + [ANY] * len(after), out_specs=[vm] * DEPTH,
        out_shape=[jax.ShapeDtypeStruct((HP, D_MODEL), BF16)] * DEPTH, compiler_params=_cparams(),
    )(a, *after)


def _adamw(w, gs, m, v, *, name, after=()):
    if w.ndim == 2:
        w, m, v = w[None], m[None], v[None]
        squeeze = True
    else:
        squeeze = False
    NL, R, C = w.shape
    assert len(gs) == NL
    CG = gs[0].shape[1]
    tile = _tile(R, 352, 8)

    def body(*refs):
        w_ref, g_refs = refs[0], refs[1:1 + NL]
        m_ref, v_ref = refs[1 + NL:3 + NL]
        go_ref, d_ref, nm_ref, nv_ref = refs[3 + NL + len(after):]
        gv = g_refs[0][:, :C]
        for j in range(1, NL):
            gv = jnp.where(pl.program_id(0) == j, g_refs[j][:, :C], gv)
        nm = ADAM_B1 * m_ref[...] + (1.0 - ADAM_B1) * gv
        nv = ADAM_B2 * v_ref[...] + (1.0 - ADAM_B2) * (gv * gv)
        m_hat = nm / (1.0 - ADAM_B1 ** ADAM_STEP)
        v_hat = nv / (1.0 - ADAM_B2 ** ADAM_STEP)
        go_ref[...] = gv
        d_ref[...] = -ADAM_LR * (m_hat / (jnp.sqrt(v_hat) + ADAM_EPS) + ADAM_WD * w_ref[...])
        nm_ref[...] = nm
        nv_ref[...] = nv

    spec = _bs((None, tile, C), lambda l, i: (l, i, 0))
    gspecs = [_bs((tile, CG), lambda l, i, j=j: (jnp.where(l == j, i, 0), 0)) for j in range(NL)]
    res = pl.pallas_call(
        body, name=name, grid=(NL, R // tile), in_specs=[spec, *gspecs, spec, spec, *[ANY] * len(after)],
        out_specs=[spec] * 4, out_shape=[jax.ShapeDtypeStruct((NL, R, C), F32)] * 4,
        compiler_params=_cparams(("parallel", "parallel")),
    )(w, *gs, m, v, *after)
    return [r[0] for r in res] if squeeze else res


def _my_pos():
    return lax.axis_index("x"), lax.axis_index("y"), lax.axis_index("c")


def _other_chips(x, y):
    return [(1 - x, y), (x, 1 - y), (1 - x, 1 - y)]


def _allgather_chips(shards):
    n = len(shards)
    per = 7

    def body(*refs):
        ins, outs = refs[:n], refs[n:2 * n]
        send_sems, recv_sems = refs[2 * n], refs[2 * n + 1]
        x, y, c = _my_pos()
        chips = _other_chips(x, y)
        sibling, me = (x, y, 1 - c), 2 * x + y

        def cp(a, kk, src, dst, to):
            return pltpu.make_async_remote_copy(src_ref=src, dst_ref=dst, send_sem=send_sems.at[per * a + kk],
                                                recv_sem=recv_sems.at[per * a + kk], device_id=to, device_id_type=MESH)

        sends = []
        for a in range(n):
            for j, chip in enumerate(chips):
                sends.append(cp(a, j, ins[a].at[c], outs[a].at[me, c], (*chip, c)))
            sends.append(cp(a, 3, ins[a], outs[a].at[me], sibling))
        for s in sends:
            s.start()
        for a in range(n):
            for j, chip in enumerate(chips):
                slab = outs[a].at[2 * chip[0] + chip[1], c]
                cp(a, j, slab, slab, (x, y, c)).wait_recv()
                fwd = cp(a, 4 + j, slab, slab, sibling)
                fwd.start()
                sends.append(fwd)
        for a in range(n):
            cp(a, 3, ins[a], outs[a].at[me], (x, y, c)).wait_recv()
            for j, chip in enumerate(chips):
                slab = outs[a].at[2 * chip[0] + chip[1], 1 - c]
                cp(a, 4 + j, slab, slab, (x, y, c)).wait_recv()
        for s in sends:
            s.wait_send()

    return pl.pallas_call(
        body, name="allgather_chips", in_specs=[ANY] * n, out_specs=[ANY] * n,
        out_shape=[jax.ShapeDtypeStruct((N_CHIPS,) + s.shape, s.dtype) for s in shards],
        scratch_shapes=[pltpu.SemaphoreType.DMA((per * n,)), pltpu.SemaphoreType.DMA((per * n,))],
    )(*shards)


def _rs_swap_rows(gs, tag):
    n = len(gs)

    def body(*refs):
        ins, outs = refs[:n], refs[n:2 * n]
        send_sems, recv_sems = refs[2 * n], refs[2 * n + 1]
        x, y, c = _my_pos()
        cps = []
        for a in range(n):
            half = ins[a].shape[1] // 2
            cps.append(pltpu.make_async_remote_copy(
                src_ref=ins[a].at[:, pl.ds((1 - c) * half, half)], dst_ref=outs[a], send_sem=send_sems.at[a],
                recv_sem=recv_sems.at[a], device_id=(x, y, 1 - c), device_id_type=MESH))
        for cp in cps:
            cp.start()
        for cp in cps:
            cp.wait()

    return pl.pallas_call(
        body, name=f"rs_swap_rows_{tag}", in_specs=[ANY] * n, out_specs=[ANY] * n,
        out_shape=[jax.ShapeDtypeStruct((N_CHIPS, g.shape[1] // 2, g.shape[2]), g.dtype) for g in gs],
        scratch_shapes=[pltpu.SemaphoreType.DMA((n,)), pltpu.SemaphoreType.DMA((n,))],
    )(*gs)


RS_ADD_VMEM_BYTES = 24 * 1024 * 1024


def _rs_tile(H, C, n):
    return _tile(H, max(16, RS_ADD_VMEM_BYTES // (28 * n * C)), 16)


def _rs_add_pair(gs, rs, pos, *, name):
    n = len(gs)
    _, H, C = rs[0].shape
    tile = _rs_tile(H, C, n)
    nt = H // tile

    def body(pos_ref, *refs):
        for a in range(n):
            s = refs[a][...] + refs[n + a][...]
            refs[2 * n + 2 * a][...] = s
            refs[2 * n + 2 * a + 1][...] = s.astype(BF16)

    spec = _bs((None, tile, C), lambda k, i, pos_ref: (k, i, 0))
    g_spec = _bs((None, tile, C), lambda k, i, pos_ref: (k, pos_ref[1] * nt + i, 0))
    grid_spec = pltpu.PrefetchScalarGridSpec(
        num_scalar_prefetch=1, grid=(N_CHIPS, nt), in_specs=[g_spec] * n + [spec] * n, out_specs=[spec] * (2 * n))
    res = pl.pallas_call(
        body, name=name, grid_spec=grid_spec,
        out_shape=[jax.ShapeDtypeStruct((N_CHIPS, H, C), F32), jax.ShapeDtypeStruct((N_CHIPS, H, C), BF16)] * n,
        compiler_params=_cparams(("parallel", "parallel")),
    )(pos, *gs, *rs)
    return [(res[2 * a], res[2 * a + 1]) for a in range(n)]


def _exchange_copies(srcs, lands, send_sems, recv_sems):
    x, y, c = _my_pos()
    starts, landing = [], []
    for a in range(len(srcs)):
        for j, chip in enumerate(_other_chips(x, y)):
            sems = dict(send_sem=send_sems.at[3 * a + j], recv_sem=recv_sems.at[3 * a + j], device_id_type=MESH)
            starts.append(pltpu.make_async_remote_copy(
                src_ref=srcs[a].at[2 * chip[0] + chip[1]], dst_ref=lands[a].at[j], device_id=(*chip, c), **sems))
            landing.append(pltpu.make_async_remote_copy(
                src_ref=lands[a].at[j], dst_ref=lands[a].at[j], device_id=(x, y, c), **sems))
    return starts, landing


def _gather_copies(srcs, lands, send_sems, recv_sems):
    x, y, c = _my_pos()
    me = 2 * x + y
    starts, landing = [], []
    for a in range(len(srcs)):
        half = srcs[a].shape[0] // 2
        mine = pl.ds(c * half, half)
        for j, chip in enumerate(_other_chips(x, y)):
            sems = dict(send_sem=send_sems.at[3 * a + j], recv_sem=recv_sems.at[3 * a + j], device_id_type=MESH)
            starts.append(pltpu.make_async_remote_copy(
                src_ref=srcs[a].at[mine], dst_ref=lands[a].at[me, mine], device_id=(*chip, c), **sems))
            slab = lands[a].at[2 * chip[0] + chip[1], mine]
            landing.append(pltpu.make_async_remote_copy(src_ref=slab, dst_ref=slab, device_id=(x, y, c), **sems))
    return starts, landing


HBM = pl.BlockSpec(memory_space=pltpu.HBM)
SEM = pl.BlockSpec(memory_space=pltpu.SEMAPHORE)


def _ici_start(copies_fn, srcs, land_shapes, *, name, after=(), sems_per_array=3):
    n, na = len(srcs), len(after)

    def body(*refs):
        starts, _ = copies_fn(refs[:n], refs[n:2 * n], refs[2 * n + na], refs[2 * n + na + 1])
        for cp in starts:
            cp.start()
        refs[-1][...] = jnp.zeros_like(refs[-1])

    sems = pltpu.SemaphoreType.DMA((sems_per_array * n,))
    hbm = lambda s: pltpu.HBM(s.shape, s.dtype)
    lands = [pltpu.with_memory_space_constraint(
        lax.empty(s.shape, s.dtype) if isinstance(s, jax.ShapeDtypeStruct) else s, pltpu.HBM) for s in land_shapes]
    res = pl.pallas_call(
        body, name=name, in_specs=[HBM] * (2 * n) + [ANY] * na,
        out_specs=(SEM, SEM, *[HBM] * (2 * n), pl.BlockSpec(memory_space=pltpu.VMEM)),
        out_shape=(sems, sems, *[hbm(s) for s in srcs], *[hbm(s) for s in land_shapes],
                   jax.ShapeDtypeStruct((8, BLOCK), F32)),
        input_output_aliases={i: 2 + i for i in range(2 * n)},
        compiler_params=pltpu.CompilerParams(has_side_effects=pltpu.SideEffectType.DATAFLOW_SIDE_EFFECTING),
    )(*[pltpu.with_memory_space_constraint(s, pltpu.HBM) for s in srcs], *lands, *after)
    return res[0], res[1], list(res[2:2 + n]), list(res[2 + n:2 + 2 * n]), res[-1]


def _ici_wait(copies_fn, send_sems, recv_sems, srcs, lands, after, *, name):
    n = len(srcs)
    after = list(after) if isinstance(after, (list, tuple)) else [after]

    def body(*refs):
        starts, landing = copies_fn(refs[:n], refs[n:2 * n], refs[2 * n], refs[2 * n + 1])
        for cp in starts:
            cp.wait_send()
        for cp in landing:
            cp.wait_recv()

    hbm = lambda s: pltpu.HBM(s.shape, s.dtype)
    res = pl.pallas_call(
        body, name=name, in_specs=[*[HBM] * (2 * n), SEM, SEM, *[ANY] * len(after)], out_specs=[HBM] * (2 * n),
        out_shape=[*[hbm(s) for s in srcs], *[hbm(s) for s in lands]],
        input_output_aliases={i: i for i in range(2 * n)},
        compiler_params=pltpu.CompilerParams(has_side_effects=pltpu.SideEffectType.DATAFLOW_SIDE_EFFECTING),
    )(*srcs, *lands, send_sems, recv_sems, *after)
    return list(res[:n]), list(res[n:])


D2D_COPIES = 4


def _d2d_copies(ins, outs, send_sems, recv_sems):
    x, y, c = _my_pos()
    me, sibling = 2 * x + y, (x, y, 1 - c)
    starts, landing = [], []
    for a in range(len(ins)):
        half = ins[a].shape[0] // 2
        mine, theirs = pl.ds(c * half, half), pl.ds((1 - c) * half, half)
        pairs = [(ins[a], outs[a].at[me], outs[a].at[me])]
        for chip in _other_chips(x, y):
            k = 2 * chip[0] + chip[1]
            pairs.append((outs[a].at[k, mine], outs[a].at[k, mine], outs[a].at[k, theirs]))
        for j, (src, dst, lands_here) in enumerate(pairs):
            sems = dict(send_sem=send_sems.at[D2D_COPIES * a + j], recv_sem=recv_sems.at[D2D_COPIES * a + j],
                        device_id_type=MESH)
            starts.append(pltpu.make_async_remote_copy(src_ref=src, dst_ref=dst, device_id=sibling, **sems))
            landing.append(pltpu.make_async_remote_copy(src_ref=lands_here, dst_ref=lands_here, device_id=(x, y, c),
                                                        **sems))
    return starts, landing


def _gather_d2d(shards, lands, tag):
    n = len(shards)

    def body(*refs):
        starts, landing = _d2d_copies(refs[:n], refs[2 * n:3 * n], refs[3 * n], refs[3 * n + 1])
        for cp in starts:
            cp.start()
        for cp in landing:
            cp.wait_recv()
        for cp in starts:
            cp.wait_send()

    return pl.pallas_call(
        body, name=f"gather_d2d_{tag}", in_specs=[ANY] * (2 * n), out_specs=[ANY] * n,
        out_shape=[jax.ShapeDtypeStruct(s.shape, s.dtype) for s in lands],
        input_output_aliases={n + a: a for a in range(n)},
        scratch_shapes=[pltpu.SemaphoreType.DMA((D2D_COPIES * n,)), pltpu.SemaphoreType.DMA((D2D_COPIES * n,))],
    )(*shards, *lands)


def _rs_add_chips(p32s, r16s, pos, *, name):
    n = len(p32s)
    _, H, C = p32s[0].shape
    tile = _rs_tile(H, C, n)
    nt = H // tile

    def body(pos_ref, *refs):
        for a in range(n):
            p_ref, r_ref = refs[a], refs[n + a]
            refs[2 * n + a][...] = ((p_ref[...] + r_ref[0].astype(F32)) + r_ref[1].astype(F32)) + r_ref[2].astype(F32)

    grid_spec = pltpu.PrefetchScalarGridSpec(
        num_scalar_prefetch=1, grid=(nt,),
        in_specs=[_bs((None, tile, C), lambda i, pos_ref: (pos_ref[0], i, 0))] * n
        + [_bs((3, tile, C), lambda i, pos_ref: (0, i, 0))] * n,
        out_specs=[_bs((tile, C), lambda i, pos_ref: (pos_ref[1] * nt + i, 0))] * n)
    return pl.pallas_call(
        body, name=name, grid_spec=grid_spec, out_shape=[jax.ShapeDtypeStruct((2 * H, C), F32)] * n,
        compiler_params=_cparams(("parallel",)),
    )(pos, *p32s, *r16s)


def _rs_join_rows(fs, tag):
    n = len(fs)

    def body(*refs):
        outs = refs[n:2 * n]
        send_sems, recv_sems = refs[2 * n], refs[2 * n + 1]
        x, y, c = _my_pos()
        for a in range(n):
            half = outs[a].shape[0] // 2
            mine = outs[a].at[pl.ds(c * half, half)]
            pltpu.make_async_remote_copy(src_ref=mine, dst_ref=mine, send_sem=send_sems.at[a],
                                         recv_sem=recv_sems.at[a], device_id=(x, y, 1 - c), device_id_type=MESH).start()
        for a in range(n):
            half = outs[a].shape[0] // 2
            pltpu.make_async_remote_copy(
                src_ref=outs[a].at[pl.ds(c * half, half)], dst_ref=outs[a].at[pl.ds((1 - c) * half, half)],
                send_sem=send_sems.at[a], recv_sem=recv_sems.at[a], device_id=(x, y, 1 - c), device_id_type=MESH).wait()

    return pl.pallas_call(
        body, name=f"rs_join_rows_{tag}", in_specs=[ANY] * n, out_specs=[ANY] * n,
        out_shape=[jax.ShapeDtypeStruct(f.shape, f.dtype) for f in fs],
        input_output_aliases={a: a for a in range(n)},
        scratch_shapes=[pltpu.SemaphoreType.DMA((n,)), pltpu.SemaphoreType.DMA((n,))],
    )(*fs)


def _pos_vector():
    x, y, c = _my_pos()
    return jnp.stack([2 * x + y, c]).astype(jnp.int32)


def _swap_copies(srcs, lands, send_sems, recv_sems):
    x, y, c = _my_pos()
    starts, landing = [], []
    for a in range(len(srcs)):
        half = srcs[a].shape[1] // 2
        sems = dict(send_sem=send_sems.at[3 * a], recv_sem=recv_sems.at[3 * a], device_id_type=MESH)
        starts.append(pltpu.make_async_remote_copy(
            src_ref=srcs[a].at[:, pl.ds((1 - c) * half, half)], dst_ref=lands[a], device_id=(x, y, 1 - c), **sems))
        landing.append(pltpu.make_async_remote_copy(src_ref=lands[a], dst_ref=lands[a], device_id=(x, y, c), **sems))
    return starts, landing


def _swap_land_shapes(gs):
    return [jax.ShapeDtypeStruct((N_CHIPS, g.shape[1] // 2, g.shape[2]), g.dtype) for g in gs]


def _same_shape_runs(arrays):
    runs, start = [], 0
    for i in range(1, len(arrays) + 1):
        if i == len(arrays) or arrays[i].shape != arrays[start].shape:
            runs.append((start, i))
            start = i
    return runs


def _rs_add_pairs(gs, r1, names, tag):
    pos = _pos_vector()
    out = []
    for a, b in _same_shape_runs(gs):
        out += _rs_add_pair(gs[a:b], r1[a:b], pos, name=f"rs_add_pair_{tag}_{names[a]}")
    return out


def _rs_pair_sums(gs, names, tag):
    return _rs_add_pairs(gs, _rs_swap_rows(gs, tag), names, tag)


def _rs_finish(pairs, r2, names, tag):
    pos = _pos_vector()
    p32s = [p[0] for p in pairs]
    fs = []
    for a, b in _same_shape_runs(p32s):
        fs += _rs_add_chips(p32s[a:b], r2[a:b], pos, name=f"rs_add_chips_{tag}_{names[a]}")
    return _rs_join_rows(fs, tag)


def _exchange_land_shapes(pairs):
    return [jax.ShapeDtypeStruct((3,) + p[1].shape[1:], p[1].dtype) for p in pairs]


def _allreduce_small(buf):
    R, W = buf.shape

    def body(b_ref, o_ref, gather, send_sems, recv_sems):
        x, y, c = _my_pos()
        me = 4 * x + 2 * y + c
        gather[me] = b_ref[...]
        cps = []
        for d in range(1, 8):
            peer = (x ^ (d >> 2), y ^ ((d >> 1) & 1), c ^ (d & 1))
            cps.append(pltpu.make_async_remote_copy(
                src_ref=b_ref, dst_ref=gather.at[me], send_sem=send_sems.at[d - 1], recv_sem=recv_sems.at[d - 1],
                device_id=peer, device_id_type=MESH))
        for cp in cps:
            cp.start()
        for d in range(1, 8):
            pltpu.make_async_remote_copy(
                src_ref=b_ref, dst_ref=gather.at[me ^ d], send_sem=send_sems.at[d - 1], recv_sem=recv_sems.at[d - 1],
                device_id=(x, y, c), device_id_type=MESH).wait_recv()
        for cp in cps:
            cp.wait_send()
        acc = gather[0]
        for d in range(1, 8):
            acc = acc + gather[d]
        o_ref[...] = acc

    vm = pl.BlockSpec(memory_space=pltpu.VMEM)
    return pl.pallas_call(
        body, name="allreduce_small", in_specs=[vm], out_specs=vm, out_shape=jax.ShapeDtypeStruct((R, W), F32),
        scratch_shapes=[pltpu.VMEM((8, R, W), F32), pltpu.SemaphoreType.DMA((7,)), pltpu.SemaphoreType.DMA((7,))],
    )(buf)


def _heads(a, h, d):
    return a.reshape(a.shape[0], h, d).transpose(1, 0, 2)


def _unheads(a):
    h, L, d = a.shape
    return a.transpose(1, 0, 2).reshape(L, h * d)


def _rope_tables():
    pos = jnp.maximum(jnp.arange(LP, dtype=F32) - PAD_ROWS, 0.0)
    inv_freq = 1.0 / (ROPE_THETA ** (jnp.arange(0, MLA_ROPE, 2, dtype=F32) / MLA_ROPE))
    ang = pos[:, None] * inv_freq[None, :]
    cos, sin = jnp.tile(jnp.cos(ang), (1, MLA_HEADS)), jnp.tile(jnp.sin(ang), (1, MLA_HEADS))
    return jnp.concatenate([cos, cos], axis=1), jnp.concatenate([-sin, sin], axis=1)


def _lane_pad(a, width=BLOCK):
    return jnp.pad(a, ((0, 0), (0, width - a.shape[1])))


def _pad_in_proj(w):
    sl = lambda start, size: w[:, start:start + size]
    return jnp.concatenate([
        sl(OC_Z, 512), sl(OC_XBC, 768), sl(OC_FQ, 256), sl(OC_FK, 256), sl(OC_FV, 256), sl(OC_CQ, 256), sl(OC_CKV, 128),
        _lane_pad(sl(OC_DT, SSD_HEADS)), _lane_pad(sl(OC_FR, FOX_HEADS)),
        jnp.tile(sl(OC_KR, ROPE_HALF), (1, MLA_HEADS)), jnp.tile(sl(OC_KR + ROPE_HALF, ROPE_HALF), (1, MLA_HEADS))], axis=1)


def _in_proj_grad_chunks(wp):
    rope = lambda start: wp[:, start:start + 64].reshape(wp.shape[0], MLA_HEADS, ROPE_HALF).sum(axis=1)
    segs = [(wp, PC_Z, 512), (wp, PC_XBC, 768), (wp, PC_DT, SSD_HEADS), (wp, PC_FQ, 256), (wp, PC_FK, 256),
            (wp, PC_FV, 256), (wp, PC_FR, FOX_HEADS), (wp, PC_CQ, 256), (wp, PC_CKV, 128),
            (rope(PC_KR), 0, ROPE_HALF), (rope(PC_KR + 64), 0, ROPE_HALF)]
    chunks = []
    for k in range(N_CHIPS):
        lo, hi, pos, pieces = k * IN_SHARD, (k + 1) * IN_SHARD, 0, []
        for arr, start, size in segs:
            a, b = max(lo, pos), min(hi, pos + size)
            if a < b:
                pieces.append(arr[:, start + a - pos:start + b - pos])
            pos += size
        pieces.append(jnp.zeros((wp.shape[0], IN_SHARD_P - IN_SHARD), wp.dtype))
        chunks.append(jnp.concatenate(pieces, axis=1))
    return jnp.stack(chunks)


def _regroup_uq(w):
    w3 = w.reshape(w.shape[0], MLA_HEADS, MLA_NOPE + MLA_ROPE)
    return jnp.concatenate([w3[:, :, :MLA_NOPE].reshape(w.shape[0], -1),
                            w3[:, :, MLA_NOPE:MLA_NOPE + ROPE_HALF].reshape(w.shape[0], -1),
                            w3[:, :, MLA_NOPE + ROPE_HALF:].reshape(w.shape[0], -1)], axis=1)


def _ungroup_uq(wp):
    n = wp.shape[0]
    return jnp.concatenate([wp[:, :256].reshape(n, MLA_HEADS, MLA_NOPE), wp[:, 256:320].reshape(n, MLA_HEADS, ROPE_HALF),
                            wp[:, 320:].reshape(n, MLA_HEADS, ROPE_HALF)], axis=2).reshape(n, -1)


def _regroup_ukv(w):
    w3 = w.reshape(w.shape[0], MLA_HEADS, MLA_NOPE + MLA_V)
    return jnp.concatenate([w3[:, :, :MLA_NOPE].reshape(w.shape[0], -1), w3[:, :, MLA_NOPE:].reshape(w.shape[0], -1)],
                           axis=1)


def _ungroup_ukv(wp):
    n = wp.shape[0]
    return jnp.concatenate([wp[:, :256].reshape(n, MLA_HEADS, MLA_NOPE), wp[:, 256:].reshape(n, MLA_HEADS, MLA_V)],
                           axis=2).reshape(n, -1)


TMF = 1088
N_IF = LP // TMF


def _chunk_rows_dx(g, w, l, chunk_h, *, name):
    N = w.shape[2]
    return _mm_core(g, w, a_spec=_bs((TMF, N), lambda i, j, k: (i, 0)),
                    b_spec=_bs((None, chunk_h, N), lambda i, j, k: (j, 0, 0)),
                    o_spec=_bs((TMF, chunk_h), lambda i, j, k: (i, j)), grid=(N_IF, N_CHIPS, 1),
                    out_shape=(LP, N_CHIPS * chunk_h), ca=1, cb=1, name=name)


def _chunk_rows_dw(a, g, chunk_h, *, name):
    N = g.shape[1]
    return _mm_core(a, g, a_spec=_bs((LP, chunk_h), lambda i, j, k: (0, i)), b_spec=_bs((LP, N), lambda i, j, k: (0, 0)),
                    o_spec=_bs((None, chunk_h, N), lambda i, j, k: (i, 0, 0)), grid=(N_CHIPS, 1, 1),
                    out_shape=(N_CHIPS, chunk_h, N), ca=0, cb=0, name=name)


def _ffn_up_swiglu(h, wg, wu, *, name, after=()):
    def body(h_ref, wg_ref, wu_ref, *refs):
        g_ref, u_ref, a_ref = refs[len(after):]
        hb = h_ref[...].astype(BF16)
        g = _raw_bdot(hb, wg_ref[...], 1, 1)
        u = _raw_bdot(hb, wu_ref[...], 1, 1)
        g_ref[...] = g
        u_ref[...] = u
        a_ref[...] = (_silu(g) * u).astype(a_ref.dtype)

    w_spec = _bs((None, HP, D_MODEL), lambda i, j: (j, 0, 0))
    o_spec = _bs((TMF, HP), lambda i, j: (i, j))
    return pl.pallas_call(
        body, name=name, grid=(N_IF, N_CHIPS),
        in_specs=[_bs((TMF, D_MODEL), lambda i, j: (i, 0)), w_spec, w_spec, *[ANY] * len(after)],
        out_specs=[o_spec] * 3,
        out_shape=[jax.ShapeDtypeStruct((LP, FP), F32), jax.ShapeDtypeStruct((LP, FP), F32),
                   jax.ShapeDtypeStruct((LP, FP), BF16)],
        compiler_params=_cparams(("parallel", "parallel")),
    )(h, wg, wu, *after)


def _ffn_down_dx_swiglu(do, wd, g, u, *, name):
    def body(do_ref, wd_ref, g_ref, u_ref, dg_ref, du_ref):
        dact = _raw_bdot(do_ref[...], wd_ref[...], 1, 1)
        gv = g_ref[...]
        sig = _sigmoid(gv)
        dg_ref[...] = (dact * u_ref[...] * (sig * (1.0 + gv * (1.0 - sig)))).astype(dg_ref.dtype)
        du_ref[...] = (dact * (gv * sig)).astype(du_ref.dtype)

    blk = _bs((TMF, HP), lambda i, j: (i, j))
    return pl.pallas_call(
        body, name=name, grid=(N_IF, N_CHIPS),
        in_specs=[_bs((TMF, D_MODEL), lambda i, j: (i, 0)), _bs((None, HP, D_MODEL), lambda i, j: (j, 0, 0)), blk, blk],
        out_specs=[blk, blk], out_shape=[jax.ShapeDtypeStruct((LP, FP), BF16)] * 2,
        compiler_params=_cparams(("parallel", "parallel")),
    )(do, wd, g, u)


def _ffn_gate_up_dw(dg, du, h, *, name):
    def body(dg_ref, du_ref, h_ref, wg_ref, wu_ref):
        hb = h_ref[...].astype(BF16)
        wg_ref[...] = _raw_bdot(dg_ref[...], hb, 0, 0)
        wu_ref[...] = _raw_bdot(du_ref[...], hb, 0, 0)

    a_spec = _bs((LP, HP), lambda k: (0, k))
    o_spec = _bs((None, HP, D_MODEL), lambda k: (k, 0, 0))
    return pl.pallas_call(
        body, name=name, grid=(N_CHIPS,), in_specs=[a_spec, a_spec, _bs((LP, D_MODEL), lambda k: (0, 0))],
        out_specs=[o_spec, o_spec], out_shape=[jax.ShapeDtypeStruct((N_CHIPS, HP, D_MODEL), F32)] * 2,
        compiler_params=_cparams(("parallel",)),
    )(dg, du, h)


def _ffn_gate_up_dx(dg, du, wg, wu, add, *, name):
    def body(dg_ref, du_ref, wg_ref, wu_ref, add_ref, o_ref, acc_ref):
        k = pl.program_id(1)

        @pl.when(k == 0)
        def _():
            acc_ref[...] = jnp.zeros_like(acc_ref)

        acc_ref[...] += _raw_bdot(dg_ref[...], wg_ref[...], 1, 0) + _raw_bdot(du_ref[...], wu_ref[...], 1, 0)

        @pl.when(k == N_CHIPS - 1)
        def _():
            o_ref[...] = acc_ref[...] + add_ref[...]

    a_spec = _bs((TMF, HP), lambda i, k: (i, k))
    w_spec = _bs((None, HP, D_MODEL), lambda i, k: (k, 0, 0))
    o_spec = _bs((TMF, D_MODEL), lambda i, k: (i, 0))
    return pl.pallas_call(
        body, name=name, grid=(N_IF, N_CHIPS), in_specs=[a_spec, a_spec, w_spec, w_spec, o_spec], out_specs=o_spec,
        out_shape=jax.ShapeDtypeStruct((LP, D_MODEL), F32), scratch_shapes=[pltpu.VMEM((TMF, D_MODEL), F32)],
        compiler_params=_cparams(("parallel", "arbitrary")),
    )(dg, du, wg, wu, add)


def _chunk_rows_mm_res_ln(a, w, chunk_h, h, gam, bet, scale, *, name):
    res_ln = _make_res_ln_fn(scale)

    def body(a_ref, w_ref, h_ref, g_ref, b_ref, o_ref, y_ref, yb_ref, acc_ref):
        k = pl.program_id(1)

        @pl.when(k == 0)
        def _():
            acc_ref[...] = jnp.zeros_like(acc_ref)

        acc_ref[...] += _raw_bdot(a_ref[...], w_ref[...], 1, 0)

        @pl.when(k == N_CHIPS - 1)
        def _():
            o = acc_ref[...]
            o_ref[...] = o
            (y,) = res_ln(0, h_ref[...], o, g_ref[...], b_ref[...])
            y_ref[...] = y
            yb_ref[...] = y.astype(yb_ref.dtype)

    row = _bs((TMF, D_MODEL), lambda i, k: (i, 0))
    par = _bs((1, D_MODEL), lambda i, k: (0, 0))
    return pl.pallas_call(
        body, name=name, grid=(N_IF, N_CHIPS),
        in_specs=[_bs((TMF, chunk_h), lambda i, k: (i, k)), _bs((None, chunk_h, D_MODEL), lambda i, k: (k, 0, 0)), row,
                  par, par],
        out_specs=[row, row, row],
        out_shape=[jax.ShapeDtypeStruct((LP, D_MODEL), F32)] * 2 + [jax.ShapeDtypeStruct((LP, D_MODEL), BF16)],
        scratch_shapes=[pltpu.VMEM((TMF, D_MODEL), F32)], compiler_params=_cparams(("parallel", "arbitrary")),
    )(a, w, h, gam, bet)


def _ffn_fwd(hp, W, pre, l, gam, bet, tag, after=()):
    h, hb = hp
    g, u, act = _ffn_up_swiglu(hb, W[pre + "_w_gate"][l], W[pre + "_w_up"][l], name=f"{tag}_up_swiglu", after=after)
    o, out, outb = _chunk_rows_mm_res_ln(act, W[pre + "_w_down"][l], HP, h, gam, bet, 0.5, name=f"{tag}_down_ln")
    return (out, outb), (h, hb, g, u, act, o)


def _ffn_bwd(dout, saved, W, pre, l, gam, bet, GB, tag):
    h, hb, g, u, act, o = saved
    (dh_a, do), (dgam, dbet) = _rowwise_bwd(_make_res_ln_fn(0.5), [h, o], [gam, bet], [dout], name=f"{tag}_ln_bwd",
                                            tile=TM, grad_dtypes=[F32, BF16])
    dg, du = _ffn_down_dx_swiglu(do, W[pre + "_w_down"][l], g, u, name=f"{tag}_down_dx_swiglu")
    GB[pre + "_w_down"] = _chunk_rows_dw(act, do, HP, name=f"{tag}_down_dw")
    GB[pre + "_w_gate"], GB[pre + "_w_up"] = _ffn_gate_up_dw(dg, du, hb, name=f"{tag}_gate_up_dw")
    dh = _ffn_gate_up_dx(dg, du, W[pre + "_w_gate"][l], W[pre + "_w_up"][l], dh_a, name=f"{tag}_gate_up_dx")
    return dh, dgam, dbet


def _mixer_fwd(hp1, W, l, cosf, sins, after=()):
    h1, h1b = hp1
    tag = f"l{l}"
    proj = _mm(h1b, W["w_in_p"][l], name=f"{tag}_in_proj", after=after)
    sv = {"h1": h1, "h1b": h1b, "proj": proj}
    conv_w, conv_b = W["conv_w"][l], W["conv_b"][l][None]
    xc = _conv_fwd(proj, PC_XBC // BLOCK, conv_w, conv_b, name=f"{tag}_conv")
    dt_bias = _lane_pad(W["dt_bias"][l][None])
    dtc, dtr = _ssd_dt_fwd(proj, PC_DT // BLOCK, dt_bias, name=f"{tag}_ssd_dt")
    xh = _heads(xc[:, :SSD_D], SSD_HEADS, SSD_HD)
    bm = _heads(xc[:, SSD_D:SSD_D + 128], SSD_GROUPS, SSD_STATE)
    cm = _heads(xc[:, SSD_D + 128:], SSD_GROUPS, SSD_STATE)
    alog = jnp.broadcast_to(W["a_log"][l][:, None, None], (SSD_HEADS, 1, BLOCK))
    yh, prevs = _ssd_fwd(xh, bm, cm, dtc, dtr, alog, name=f"{tag}_ssd")
    y_raw = _unheads(yh)
    dskip = jnp.repeat(W["d_skip"][l], SSD_HD)[None]
    normg = W["ssd_norm_g"][l][None]
    post_rows = [y_raw, (xc, 256, 0), (proj, 256, PC_Z // 256)]
    (y_ssd,) = _rowwise(_ssd_post_fn, post_rows, [dskip, normg], [SSD_D], name=f"{tag}_ssd_post", tile=TM,
                        ncol=SSD_GROUPS)
    sv.update(conv_w=conv_w, conv_b=conv_b, dt_bias=dt_bias, xh=xh, bm=bm, cm=cm, dtc=dtc, dtr=dtr, alog=alog,
              prevs=prevs, post_rows=post_rows, dskip=dskip, normg=normg)
    f_b = _lane_pad(W["fox_f_b"][l][None])
    cg, cgt = _fox_gate_fwd(proj, PC_FR // BLOCK, f_b, name=f"{tag}_fox_gate")
    fox_qkv = ((proj, PC_FQ // ATT_W), (proj, PC_FK // ATT_W), (proj, PC_FV // ATT_W))
    y_fox, lse_f = _attn_fwd(*fox_qkv, scale=FOX_HD ** -0.5, name=f"{tag}_fox_attn", bias=(cg, cgt))
    sv.update(f_b=f_b, cg=cg, cgt=cgt, fox_qkv=fox_qkv, y_fox=y_fox, lse_f=lse_f)
    gq, gkv = W["mla_q_norm_g"][l][None], W["mla_kv_norm_g"][l][None]
    norm_rows = [(proj, 256, PC_CQ // 256), (proj, BLOCK, PC_CKV // BLOCK)]
    qn, cn = _rowwise(_mla_norm_fn, norm_rows, [gq, gkv], [MLA_Q_LORA, MLA_KV_LORA], name=f"{tag}_mla_norm", tile=TM,
                      out_dtypes=[BF16, BF16])
    qh = _mm(qn, W["mla_w_uq_p"][l], name=f"{tag}_mla_uq")
    kvh = _mm(cn, W["mla_w_ukv_p"][l], name=f"{tag}_mla_ukv")
    qr, kr = _rowwise(_rope_fn, [(qh, BLOCK, 2), (proj, BLOCK, PC_KR // BLOCK), cosf, sins], [], [BLOCK, BLOCK],
                      name=f"{tag}_rope", tile=TM)
    mla_qkv = ((qh, 0), (kvh, 0), (kvh, 1))
    y_mla, lse_m = _attn_fwd(*mla_qkv, scale=(MLA_NOPE + MLA_ROPE) ** -0.5, name=f"{tag}_mla_attn",
                             rope=((qr, 0), (kr, 0)))
    sv.update(gq=gq, gkv=gkv, norm_rows=norm_rows, qn=qn, cn=cn, qr=qr, kr=kr, mla_qkv=mla_qkv, y_mla=y_mla, lse_m=lse_m)
    ycat = jnp.concatenate([y_ssd, y_fox, y_mla], axis=1).astype(BF16)
    mix, h2, h2b = _chunk_rows_mm_res_ln(ycat, W["w_out"][l], 256, h1, W["ln2_g"][l][None], W["ln2_b"][l][None], 1.0,
                                    name=f"{tag}_out_proj_ln2")
    sv.update(mix=mix, ycat=ycat)
    return (h2, h2b), sv


def _mixer_bwd(dh2, sv, W, l, cosf, sins, GB, zero=0.0):
    tag = f"l{l}"
    G = {}
    proj = sv["proj"]
    ln2g, ln2b = W["ln2_g"][l][None] + zero, W["ln2_b"][l][None]
    (dh1_a, dmix), (dln2g, dln2b) = _rowwise_bwd(
        _make_res_ln_fn(1.0), [sv["h1"], sv["mix"]], [ln2g, ln2b], [dh2], name=f"{tag}_ln2_bwd", tile=TM,
        grad_dtypes=[F32, BF16])
    G["ln2_g"], G["ln2_b"] = dln2g[0], dln2b[0]
    dycat = _chunk_rows_dx(dmix, W["w_out"][l], l, 256, name=f"{tag}_out_proj_dx")
    GB["w_out"] = _chunk_rows_dw(sv["ycat"], dmix, 256, name=f"{tag}_out_proj_dw")
    (dy_raw, dxs_a, dz), (ddskip, dnormg) = _rowwise_bwd(
        _ssd_post_fn, sv["post_rows"], [sv["dskip"], sv["normg"]], [dycat[:, :SSD_D]],
        name=f"{tag}_ssd_post_bwd", tile=TM, ncol=SSD_GROUPS)
    G["ssd_norm_g"] = dnormg[0]
    G["d_skip"] = ddskip.reshape(SSD_HEADS, SSD_HD).sum(axis=1)
    dxh, dbm, dcm, ddtc, ddtr, dal = _ssd_bwd(sv["xh"], sv["bm"], sv["cm"], sv["dtc"], sv["dtr"], sv["alog"],
                                              sv["prevs"], _heads(dy_raw, SSD_HEADS, SSD_HD), name=f"{tag}_ssd_bwd")
    G["a_log"] = dal[:, 0, 0]
    dxc = jnp.concatenate([dxs_a + _unheads(dxh), _unheads(dbm), _unheads(dcm)], axis=1)
    dxbc, G["conv_w"], dconv_b = _conv_bwd(proj, PC_XBC // BLOCK, sv["conv_w"], sv["conv_b"], dxc,
                                           name=f"{tag}_conv_bwd")
    G["conv_b"] = dconv_b[0]
    ddt_raw, ddt_bias = _ssd_dt_bwd(proj, PC_DT // BLOCK, sv["dt_bias"], ddtc, ddtr, name=f"{tag}_ssd_dt_bwd")
    G["dt_bias"] = ddt_bias[0, :SSD_HEADS]
    dfq, dfk, dfv, dcg, dcgt = _attn_bwd(*sv["fox_qkv"], sv["y_fox"], sv["lse_f"], (dycat, SSD_D // ATT_W),
                                         scale=FOX_HD ** -0.5, name=f"{tag}_fox_attn_bwd", bias=(sv["cg"], sv["cgt"]))
    df_raw, dfb = _fox_gate_bwd(proj, PC_FR // BLOCK, sv["f_b"], dcg, dcgt, name=f"{tag}_fox_gate_bwd")
    G["fox_f_b"] = dfb[0, :FOX_HEADS]
    dqn_h, dkn_h, dv_h, dqr, dkr = _attn_bwd(
        *sv["mla_qkv"], sv["y_mla"], sv["lse_m"], (dycat, (SSD_D + FOX_D) // ATT_W),
        scale=(MLA_NOPE + MLA_ROPE) ** -0.5, name=f"{tag}_mla_attn_bwd", rope=((sv["qr"], 0), (sv["kr"], 0)))
    dq_rope, dk_rope = _rowwise(_rope_t_fn, [dqr, dkr, cosf, sins], [], [BLOCK, BLOCK], name=f"{tag}_rope_bwd",
                                tile=TM)
    dqh = jnp.concatenate([dqn_h, dq_rope], axis=1).astype(BF16)
    dkvh = jnp.concatenate([dkn_h, dv_h], axis=1).astype(BF16)
    dqn = _mm(dqh, W["mla_w_uq_p"][l], tb=True, name=f"{tag}_mla_uq_dx")
    G["mla_w_uq_p"] = _mm(sv["qn"], dqh, ta=True, name=f"{tag}_mla_uq_dw")
    dcn = _mm(dkvh, W["mla_w_ukv_p"][l], tb=True, name=f"{tag}_mla_ukv_dx")
    G["mla_w_ukv_p"] = _mm(sv["cn"], dkvh, ta=True, name=f"{tag}_mla_ukv_dw")
    (dcq, dckv), (dgq, dgkv) = _rowwise_bwd(_mla_norm_fn, sv["norm_rows"], [sv["gq"], sv["gkv"]], [dqn, dcn],
                                            name=f"{tag}_mla_norm_bwd", tile=TM)
    G["mla_q_norm_g"], G["mla_kv_norm_g"] = dgq[0], dgkv[0]
    dproj = jnp.concatenate([dz, dxbc, dfq, dfk, dfv, dcq, dckv, ddt_raw, df_raw, dk_rope], axis=1).astype(BF16)
    dh1 = _mm(dproj, W["w_in_p"][l], tb=True, add=dh1_a, name=f"{tag}_in_proj_dx")
    G["w_in_p"] = _mm(sv["h1b"], dproj, ta=True, name=f"{tag}_in_proj_dw")
    return dh1, G


def _embed(x, meta):
    return jnp.concatenate([jnp.zeros((PAD_ROWS, D_MODEL), F32), meta, x], axis=0)


def _layer_fwd(h, W, l, cosf, sins):
    ln = lambda n: W[n][l][None]
    h1, s1 = _ffn_fwd(h, W, "ffn1", l, ln("ln1_g"), ln("ln1_b"), f"l{l}_ffn1")
    h2, sm = _mixer_fwd(h1, W, l, cosf, sins)
    h3, s2 = _ffn_fwd(h2, W, "ffn2", l, ln("ln3_g"), ln("ln3_b"), f"l{l}_ffn2")
    return h3, (s1, sm, s2)


def _layer_bwd(dh, saved, W, l, cosf, sins):
    ln = lambda n: W[n][l][None]
    s1, sm, s2 = saved
    G = {}
    dh, dg, db = _ffn_bwd(dh, s2, W, "ffn2", l, ln("ln3_g"), ln("ln3_b"), G, f"l{l}_ffn2")
    G["ln3_g"], G["ln3_b"] = dg[0], db[0]
    dh, Gm = _mixer_bwd(dh, sm, W, l, cosf, sins, G)
    G.update(Gm)
    dh, dg, db = _ffn_bwd(dh, s1, W, "ffn1", l, ln("ln1_g"), ln("ln1_b"), G, f"l{l}_ffn1")
    G["ln1_g"], G["ln1_b"] = dg[0], db[0]
    return dh, G


def _local_step(x, target, W):
    h = _embed(x, W["meta"])
    h = (h, h.astype(BF16))
    tgt = jnp.concatenate([jnp.zeros((BLOCK, D_MODEL), F32), target], axis=0)
    cosf, sins = _rope_tables()
    saved = []
    for l in range(DEPTH):
        h, sv = _layer_fwd(h, W, l, cosf, sins)
        saved.append(sv)
    dh, loss = _loss_head(h[0], tgt, name="loss_head")
    grads = [None] * DEPTH
    for l in reversed(range(DEPTH)):
        dh, grads[l] = _layer_bwd(dh, saved[l], W, l, cosf, sins)
    return loss, dh, grads


WEIGHTS = ['meta', 'ffn1_w_gate', 'ffn1_w_up', 'ffn1_w_down', 'ln1_g', 'ln1_b', 'w_in', 'conv_w', 'conv_b', 'dt_bias',
           'a_log', 'd_skip', 'ssd_norm_g', 'fox_f_b', 'mla_q_norm_g', 'mla_w_uq', 'mla_kv_norm_g', 'mla_w_ukv',
           'w_out', 'ln2_g', 'ln2_b', 'ffn2_w_gate', 'ffn2_w_up', 'ffn2_w_down', 'ln3_g', 'ln3_b']
SMALL = ["ln1_g", "ln1_b", "conv_b", "dt_bias", "a_log", "d_skip", "ssd_norm_g", "fox_f_b", "mla_q_norm_g",
         "mla_kv_norm_g", "ln2_g", "ln2_b", "ln3_g", "ln3_b"]
MATMUL_W = ["ffn1_w_gate", "ffn1_w_up", "ffn1_w_down", "w_in", "mla_w_uq", "mla_w_ukv", "w_out", "ffn2_w_gate",
            "ffn2_w_up", "ffn2_w_down"]
SMALL_ROWS = 312


def _pad_to(a, axis, size):
    pads = [(0, 0)] * a.ndim
    pads[axis] = (0, size - a.shape[axis])
    return jnp.pad(a, pads)


def _chip_cols(full, chip, width):
    return lax.dynamic_slice_in_dim(full, chip * width, width, axis=full.ndim - 1)


def kernel(x, meta, ffn1_w_gate, ffn1_w_up, ffn1_w_down, ln1_g, ln1_b, w_in, conv_w, conv_b, dt_bias, a_log, d_skip, ssd_norm_g, fox_f_b, mla_q_norm_g, mla_w_uq, mla_kv_norm_g, mla_w_ukv, w_out, ln2_g, ln2_b, ffn2_w_gate, ffn2_w_up, ffn2_w_down, ln3_g, ln3_b, loss_target, m_meta, m_ffn1_w_gate, m_ffn1_w_up, m_ffn1_w_down, m_ln1_g, m_ln1_b, m_w_in, m_conv_w, m_conv_b, m_dt_bias, m_a_log, m_d_skip, m_ssd_norm_g, m_fox_f_b, m_mla_q_norm_g, m_mla_w_uq, m_mla_kv_norm_g, m_mla_w_ukv, m_w_out, m_ln2_g, m_ln2_b, m_ffn2_w_gate, m_ffn2_w_up, m_ffn2_w_down, m_ln3_g, m_ln3_b, v_meta, v_ffn1_w_gate, v_ffn1_w_up, v_ffn1_w_down, v_ln1_g, v_ln1_b, v_w_in, v_conv_w, v_conv_b, v_dt_bias, v_a_log, v_d_skip, v_ssd_norm_g, v_fox_f_b, v_mla_q_norm_g, v_mla_w_uq, v_mla_kv_norm_g, v_mla_w_ukv, v_w_out, v_ln2_g, v_ln2_b, v_ffn2_w_gate, v_ffn2_w_up, v_ffn2_w_down, v_ln3_g, v_ln3_b):
    args = dict(locals())
    w = {n: args[n] for n in WEIGHTS}
    m = {n: args["m_" + n] for n in WEIGHTS}
    v = {n: args["v_" + n] for n in WEIGHTS}
    xcoord, ycoord, _ = _my_pos()
    chip = 2 * xcoord + ycoord

    tr = lambda a: jnp.swapaxes(a, 1, 2)

    prepared = {}

    def bf16_shard(n, l, after=None):
        if n.startswith("ffn"):
            if n not in prepared:
                a = w[n] if n.endswith("w_down") else tr(w[n])
                prepared[n] = _ffn_shard_prep(a, name=f"prep_{n}", after=[] if after is None else [after])
            return prepared[n][l]
        a = w[n] if after is None else w[n] + after[0, 0]
        if n == "w_in":
            a = _pad_to(a, 2, IN_SHARD_P)
        return a[l].astype(BF16)

    land_shape = lambda s: jax.ShapeDtypeStruct((N_CHIPS,) + s.shape, s.dtype)

    def gather_start(names, l, tag, after):
        srcs = [bf16_shard(n, l, after) for n in names]
        return _ici_start(_gather_copies, srcs, [land_shape(s) for s in srcs], name=f"gather_ici_{tag}_start",
                          after=[tiny[0]] if after is None else [after])

    def gather_finish(handle, names, l, tag, after):
        srcs, lands = _ici_wait(_gather_copies, *handle[:4], after, name=f"gather_ici_{tag}_wait")
        use_gathered(l, names, _gather_d2d(srcs, lands, tag))

    def gather_d2d_start(handle, tag, after):
        srcs, lands = _ici_wait(_gather_copies, *handle[:4], after, name=f"gather_ici_{tag}_wait")
        return _ici_start(_d2d_copies, srcs, lands, name=f"gather_d2d_{tag}_start", sems_per_array=D2D_COPIES)

    def gather_d2d_finish(handle, names, l, tag, after):
        _, lands = _ici_wait(_d2d_copies, *handle[:4], after, name=f"gather_d2d_{tag}_wait")
        use_gathered(l, names, lands)

    tiny = _allgather_chips([w["meta"].reshape(2, N_META // 2, D_MODEL // N_CHIPS), w["conv_w"]])
    meta_full = jnp.concatenate([tiny[0][k].reshape(N_META, D_MODEL // N_CHIPS) for k in range(N_CHIPS)], axis=1)

    W = {n: [None] * DEPTH for n in MATMUL_W + ["w_in_p", "mla_w_uq_p", "mla_w_ukv_p"]}
    W["conv_w"] = jnp.concatenate([tiny[1][k] for k in range(N_CHIPS)], axis=-1)
    W["meta"] = meta_full
    for n in SMALL:
        W[n] = w[n]

    def use_gathered(l, names, lands):
        got = dict(zip(names, lands))
        cat = lambda n, cut=None: jnp.concatenate([got[n][k][..., :cut] for k in range(N_CHIPS)], axis=-1)
        for n in names:
            W[n][l] = got[n]
        if "w_in" in got:
            W["w_in_p"][l] = _pad_in_proj(cat("w_in", IN_SHARD))
            W["mla_w_uq_p"][l] = _regroup_uq(cat("mla_w_uq"))
            W["mla_w_ukv_p"][l] = _regroup_ukv(cat("mla_w_ukv"))

    def chunk_grads(G, names):
        def chunked(name, ungroup, width, pad):
            full = ungroup(G[name])
            return _pad_to(jnp.moveaxis(full.reshape(full.shape[0], N_CHIPS, width), 1, 0), 2, pad)
        special = {"mla_w_uq": ("mla_w_uq_p", _ungroup_uq, MLA_NOPE + MLA_ROPE, MLA_NOPE + MLA_ROPE),
                   "mla_w_ukv": ("mla_w_ukv_p", _ungroup_ukv, MLA_NOPE + MLA_V, MLA_NOPE + MLA_V)}
        return [_in_proj_grad_chunks(G["w_in_p"]) if n == "w_in" else chunked(*special[n]) if n in special else G[n]
                for n in names]

    def rs_start(G, names, tag):
        pairs = _rs_pair_sums(chunk_grads(G, names), names, tag)
        handle = _ici_start(_exchange_copies, [p[1] for p in pairs], _exchange_land_shapes(pairs),
                            name=f"rs_exchange_{tag}_start")
        return pairs, handle

    def swap_start(G, names, tag):
        gs = chunk_grads(G, names)
        return _ici_start(_swap_copies, gs, _swap_land_shapes(gs), name=f"rs_swap_{tag}_start")

    def exchange_start(swap_handle, names, tag, after):
        gs, r1 = _ici_wait(_swap_copies, *swap_handle[:4], after, name=f"rs_swap_{tag}_wait")
        pairs = _rs_add_pairs(gs, r1, names, tag)
        handle = _ici_start(_exchange_copies, [p[1] for p in pairs], _exchange_land_shapes(pairs),
                            name=f"rs_exchange_{tag}_start")
        return pairs, handle

    def rs_end(pairs, handle, names, tag, after):
        _, r2 = _ici_wait(_exchange_copies, *handle[:4], after, name=f"rs_exchange_{tag}_wait")
        return dict(zip(names, _rs_finish(pairs, r2, names, tag)))

    ffn1_w, mix_w, ffn2_w = MATMUL_W[:3], MATMUL_W[3:7], MATMUL_W[7:]
    g_a = gather_start(ffn1_w, 0, "l0_ffn1", None)
    g_b = gather_start(mix_w, 0, "l0_mix", g_a[4])
    g_c = gather_start(ffn2_w, 0, "l0_ffn2", g_b[4])
    g_l1 = gather_start(MATMUL_W, 1, "l1", g_c[4])
    token = g_l1[4]
    cosf, sins = _rope_tables()
    ln = lambda n, l: W[n][l][None]
    h = _embed(x[0] + token[0, 0], meta_full)
    h = (h, h.astype(BF16))
    gather_finish(g_a, ffn1_w, 0, "l0_ffn1", h[1])
    h1, s1 = _ffn_fwd(h, W, "ffn1", 0, ln("ln1_g", 0), ln("ln1_b", 0), "l0_ffn1")
    gather_finish(g_b, mix_w, 0, "l0_mix", h1[1])
    d_c = gather_d2d_start(g_c, "l0_ffn2", W["w_in_p"][0])
    h2, sm = _mixer_fwd(h1, W, 0, cosf, sins, after=[d_c[4]])
    gather_d2d_finish(d_c, ffn2_w, 0, "l0_ffn2", h2[1])
    d_l1 = gather_d2d_start(g_l1, "l1", W["ffn2_w_gate"][0])
    h, s2 = _ffn_fwd(h2, W, "ffn2", 0, ln("ln3_g", 0), ln("ln3_b", 0), "l0_ffn2", after=[d_l1[4]])
    saved0 = (s1, sm, s2)
    gather_d2d_finish(d_l1, MATMUL_W, 1, "l1", h[1])
    h, saved1 = _layer_fwd(h, W, 1, cosf, sins)
    tgt = jnp.concatenate([jnp.zeros((BLOCK, D_MODEL), F32), loss_target[0]], axis=0)
    dh, loss = _loss_head(h[0], tgt, name="loss_head")
    G = [None] * DEPTH
    dh, G[1] = _layer_bwd(dh, saved1, W, 1, cosf, sins)
    ffn2_w, mix_w, ffn1_w = MATMUL_W[7:], MATMUL_W[3:7], MATMUL_W[:3]
    sw_l1 = swap_start(G[1], MATMUL_W, "l1")
    G0 = {}
    dh, dg, db = _ffn_bwd(dh, s2, W, "ffn2", 0, ln("ln3_g", 0) + sw_l1[4][0, 0], ln("ln3_b", 0), G0, "l0_ffn2")
    G0["ln3_g"], G0["ln3_b"] = dg[0], db[0]
    pairs_l1, x_l1 = exchange_start(sw_l1, MATMUL_W, "l1", dh)
    sw_a = swap_start(G0, ffn2_w, "l0_ffn2")
    dh, Gm = _mixer_bwd(dh, sm, W, 0, cosf, sins, G0, zero=x_l1[4][0, 0] + sw_a[4][0, 0])
    G0.update(Gm)
    pairs_a, x_a = exchange_start(sw_a, ffn2_w, "l0_ffn2", dh)
    reduced1 = rs_end(pairs_l1, x_l1, MATMUL_W, "l1", dh)
    pairs_b, x_b = rs_start(G0, mix_w, "l0_mix")
    dh0, dg, db = _ffn_bwd(dh, s1, W, "ffn1", 0, ln("ln1_g", 0) + (x_a[4][0, 0] + x_b[4][0, 0]), ln("ln1_b", 0), G0,
                           "l0_ffn1")
    G0["ln1_g"], G0["ln1_b"] = dg[0], db[0]
    G[0] = G0
    reduced0 = rs_end(pairs_a, x_a, ffn2_w, "l0_ffn2", dh0)
    reduced0.update(rs_end(pairs_b, x_b, mix_w, "l0_mix", dh0))

    small_parts = [jnp.stack([G[l][n] for l in range(DEPTH)]).reshape(-1) for n in SMALL]
    small_parts += [jnp.stack([G[l]["conv_w"] for l in range(DEPTH)]).reshape(-1), dh0[PAD_ROWS:BLOCK].reshape(-1),
                    loss[0, :1]]
    sw_c = swap_start(G0, ffn1_w, "l0_ffn1")
    flat = jnp.concatenate(small_parts) + sw_c[4][0, 0]
    flat = jnp.pad(flat, (0, SMALL_ROWS * BLOCK - flat.shape[0]))
    red2d = _allreduce_small(flat.reshape(SMALL_ROWS, BLOCK))
    red = red2d.reshape(-1)
    pairs_c, x_c = exchange_start(sw_c, ffn1_w, "l0_ffn1", red2d)
    grads, off = {}, 0
    for n in SMALL:
        size = int(np.prod(w[n].shape))
        grads[n] = red[off:off + size].reshape(w[n].shape)
        off += size
    conv_full = red[off:off + DEPTH * SSD_CONV * 768].reshape(DEPTH, SSD_CONV, 768)
    off += DEPTH * SSD_CONV * 768
    dmeta_full = red[off:off + N_META * D_MODEL].reshape(N_META, D_MODEL)
    off += N_META * D_MODEL
    loss_out = red[off]
    grads["conv_w"] = _chip_cols(conv_full, chip, 768 // N_CHIPS)
    grads["meta"] = _chip_cols(dmeta_full, chip, D_MODEL // N_CHIPS)

    delta, new_m, new_v = {}, {}, {}

    def adamw_matmul_weights(names, after):
        done = []
        for n in names:
            gs = [reduced0[n], reduced1[n]]
            if n.endswith("w_gate") or n.endswith("w_up"):
                res = _adamw(tr(w[n]), gs, tr(m[n]), tr(v[n]), name=f"adamw_{n}", after=after)
                grads[n], delta[n], new_m[n], new_v[n] = [tr(r) for r in res]
            else:
                res = _adamw(w[n], gs, m[n], v[n], name=f"adamw_{n}", after=after)
                grads[n], delta[n], new_m[n], new_v[n] = res
            done.append(res[1])
        return done

    early_done = adamw_matmul_weights(ffn2_w + mix_w, [x_c[4]])
    rest = [n for n in WEIGHTS if n not in MATMUL_W]

    def pack_small(d):
        f = jnp.concatenate([d[n].reshape(-1) for n in rest])
        tot = -(-f.shape[0] // (8 * BLOCK)) * 8 * BLOCK
        return jnp.pad(f, (0, tot - f.shape[0])).reshape(-1, BLOCK)

    _, d2, m2, v2 = _adamw(pack_small(w), [pack_small(grads)], pack_small(m), pack_small(v), name="adamw_small",
                           after=[x_c[4]])
    reduced0.update(rs_end(pairs_c, x_c, ffn1_w, "l0_ffn1", [d2] + early_done))
    adamw_matmul_weights(ffn1_w, [])
    off = 0
    for n in rest:
        size = int(np.prod(w[n].shape))
        for dst, src in ((delta, d2), (new_m, m2), (new_v, v2)):
            dst[n] = src.reshape(-1)[off:off + size].reshape(w[n].shape)
        off += size

    grad_x = dh0[BLOCK:][None]
    return (loss_out, grad_x, *[grads[n] for n in WEIGHTS], *[delta[n] for n in WEIGHTS],
            *[new_m[n] for n in WEIGHTS], *[new_v[n] for n in WEIGHTS])
```

```python
import functools

import numpy as np
import jax
import jax.numpy as jnp
from jax import lax
from jax.experimental import pallas as pl
from jax.experimental.pallas import tpu as pltpu

F32 = jnp.float32
BF16 = jnp.bfloat16
MESH = pl.DeviceIdType.MESH

D_MODEL = 1024
SEQ = 2048
N_META = 16
BLOCK = 128
PAD_ROWS = 112
LP = PAD_ROWS + N_META + SEQ
N_CHUNK = LP // BLOCK
DEPTH = 2
D_FF = 2816
N_CHIPS = 4
FF_SHARD = D_FF // N_CHIPS
HP = 768
FP = N_CHIPS * HP
SSD_HEADS, SSD_HD, SSD_D, SSD_GROUPS, SSD_STATE, SSD_CONV = 8, 64, 512, 2, 64, 4
FOX_HEADS, FOX_HD, FOX_D = 4, 64, 256
MLA_HEADS, MLA_Q_LORA, MLA_KV_LORA, MLA_NOPE, MLA_ROPE, MLA_V, MLA_D = 4, 256, 128, 64, 32, 64, 256
ROPE_HALF = MLA_ROPE // 2
ROPE_THETA = 10000.0
N_IN = 2476
IN_SHARD = N_IN // N_CHIPS
IN_SHARD_P = 640
ALPHA = (2 * DEPTH) ** 0.25
EPS = 1e-5
ADAM_LR, ADAM_B1, ADAM_B2, ADAM_EPS, ADAM_WD, ADAM_STEP = 0.001, 0.9, 0.999, 1e-08, 0.01, 10
NEG = -1e30
TM = 544

VMEM_LIMIT_BYTES = 56 * 1024 * 1024

PC_Z, PC_XBC, PC_FQ, PC_FK, PC_FV, PC_CQ, PC_CKV, PC_DT, PC_FR, PC_KR, PC_END = (
    0, 512, 1280, 1536, 1792, 2048, 2304, 2432, 2560, 2688, 2816)
OC_Z, OC_XBC, OC_DT, OC_FQ, OC_FK, OC_FV, OC_FR, OC_CQ, OC_CKV, OC_KR = (
    0, 512, 1280, 1288, 1544, 1800, 2056, 2060, 2316, 2444)


def _cparams(sem=None):
    return pltpu.CompilerParams(dimension_semantics=sem, vmem_limit_bytes=VMEM_LIMIT_BYTES)


def _tile(n, cap, mult):
    best = None
    for t in range(mult, min(n, cap) + 1, mult):
        if n % t == 0:
            best = t
    return best if best is not None else n


def _bs(shape, fn):
    return pl.BlockSpec(shape, fn)


ANY = pl.BlockSpec(memory_space=pl.ANY)


def _dims(ca, cb):
    return (((ca,), (cb,)), ((), ()))


def _raw_bdot(a, b, ca, cb):
    return lax.dot_general(a.astype(BF16), b.astype(BF16), _dims(ca, cb), preferred_element_type=F32)


def _mm_core(a, b, *, a_spec, b_spec, o_spec, grid, out_shape, ca, cb, name, add=None, after=()):
    nk = grid[2]
    has_add = add is not None
    acc_shape = tuple(d for d in o_spec.block_shape if d is not None)

    def body(*refs):
        a_ref, b_ref = refs[0], refs[1]
        add_ref = refs[2] if has_add else None
        o_ref, acc_ref = refs[-2], refs[-1]
        k = pl.program_id(2)

        @pl.when(k == 0)
        def _():
            acc_ref[...] = jnp.zeros_like(acc_ref)

        acc_ref[...] += _raw_bdot(a_ref[...], b_ref[...], ca, cb)

        @pl.when(k == nk - 1)
        def _():
            r = acc_ref[...]
            if has_add:
                r = r + add_ref[...]
            o_ref[...] = r

    ins = [a, b] + ([add] if has_add else []) + list(after)
    in_specs = [a_spec, b_spec] + ([o_spec] if has_add else []) + [ANY] * len(after)
    return pl.pallas_call(
        body, name=name, grid=grid, in_specs=in_specs, out_specs=o_spec,
        out_shape=jax.ShapeDtypeStruct(out_shape, F32), scratch_shapes=[pltpu.VMEM(acc_shape, F32)],
        compiler_params=_cparams(("parallel", "parallel", "arbitrary")),
    )(*ins)


MM_VMEM_BUDGET = 40 * 1024 * 1024


def _divisors(n, mult):
    return [t for t in range(mult, n + 1, mult) if n % t == 0] or [n]


def _pick_tiles(M, N, K, a_bytes, b_bytes, ta, has_add):
    best = None
    for tm in _divisors(M, 128 if ta else 16):
        for tn in _divisors(N, 128):
            vmem = 2 * tm * K * a_bytes + 2 * K * tn * b_bytes + (3 + 2 * int(has_add)) * tm * tn * 4
            if vmem <= MM_VMEM_BUDGET:
                key = ((M // tm) * (N // tn), -tn)
                if best is None or key < best[0]:
                    best = (key, tm, tn)
    assert best is not None, (M, N, K)
    return best[1], best[2], K


def _mm(a, b, *, ta=False, tb=False, add=None, name, after=()):
    if ta:
        K, M = a.shape
    else:
        M, K = a.shape
    if tb:
        N, Kb = b.shape
    else:
        Kb, N = b.shape
    assert K == Kb, (a.shape, b.shape, ta, tb)
    tm, tn, tk = _pick_tiles(M, N, K, a.dtype.itemsize, b.dtype.itemsize, ta, add is not None)
    a_spec = _bs((tk, tm), lambda i, j, k: (k, i)) if ta else _bs((tm, tk), lambda i, j, k: (i, k))
    b_spec = _bs((tn, tk), lambda i, j, k: (j, k)) if tb else _bs((tk, tn), lambda i, j, k: (k, j))
    return _mm_core(a, b, a_spec=a_spec, b_spec=b_spec, o_spec=_bs((tm, tn), lambda i, j, k: (i, j)),
                    grid=(M // tm, N // tn, K // tk), out_shape=(M, N), ca=0 if ta else 1, cb=1 if tb else 0,
                    name=name, add=add, after=after)


def _row_entry(r, ncol):
    if isinstance(r, tuple):
        return r
    return r, r.shape[1] // ncol, 0


def _rowwise(fn, rows, pars, out_cols, *, name, tile, ncol=1, out_dtypes=None):
    rows = [_row_entry(r, ncol) for r in rows]
    L = rows[0][0].shape[0]
    nr, npar = len(rows), len(pars)
    in_specs = [_bs((tile, w), lambda g, i, o=o: (i, o + g)) for _, w, o in rows]
    in_specs += [_bs((p.shape[0], p.shape[1] // ncol), lambda g, i: (0, g)) for p in pars]
    out_specs = [_bs((tile, c // ncol), lambda g, i: (i, g)) for c in out_cols]

    def body(*refs):
        ins, outs = refs[:nr + npar], refs[nr + npar:]
        row0 = pl.program_id(1) * tile
        res = fn(row0, *[r[...] for r in ins])
        for o, v in zip(outs, res):
            o[...] = v.astype(o.dtype)

    return pl.pallas_call(
        body, name=name, grid=(ncol, L // tile), in_specs=in_specs, out_specs=out_specs,
        out_shape=[jax.ShapeDtypeStruct((L, c), d) for c, d in zip(out_cols, out_dtypes or [F32] * len(out_cols))],
        compiler_params=_cparams(("parallel", "parallel")),
    )(*[r[0] for r in rows], *pars)


def _rowwise_bwd(fn, rows, pars, douts, *, name, tile, ncol=1, row_grad=None, grad_dtypes=None):
    rows = [_row_entry(r, ncol) for r in rows]
    L = rows[0][0].shape[0]
    nr, npar, nd = len(rows), len(pars), len(douts)
    row_grad = [True] * nr if row_grad is None else row_grad
    in_specs = [_bs((tile, w), lambda g, i, o=o: (i, o + g)) for _, w, o in rows]
    in_specs += [_bs((p.shape[0], p.shape[1] // ncol), lambda g, i: (0, g)) for p in pars]
    in_specs += [_bs((tile, d.shape[1] // ncol), lambda g, i: (i, g)) for d in douts]
    g_widths = [w * ncol for (_, w, _), f in zip(rows, row_grad) if f]
    out_specs = [_bs((tile, w // ncol), lambda g, i: (i, g)) for w in g_widths]
    out_specs += [_bs((p.shape[0], p.shape[1] // ncol), lambda g, i: (0, g)) for p in pars]
    out_shape = [jax.ShapeDtypeStruct((L, w), d) for w, d in zip(g_widths, grad_dtypes or [F32] * len(g_widths))]
    out_shape += [jax.ShapeDtypeStruct(p.shape, F32) for p in pars]

    def body(*refs):
        ins = refs[:nr + npar]
        dos = refs[nr + npar:nr + npar + nd]
        outs = refs[nr + npar + nd:]
        i = pl.program_id(1)
        row0 = i * tile
        _, vjp = jax.vjp(lambda *a: tuple(fn(row0, *a)), *[r[...] for r in ins])
        grads = vjp(tuple(d[...].astype(F32) for d in dos))
        o = 0
        for j in range(nr):
            if row_grad[j]:
                outs[o][...] = grads[j].astype(outs[o].dtype)
                o += 1
        for j in range(npar):
            g, ref = grads[nr + j], outs[o + j]

            @pl.when(i == 0)
            def _(g=g, ref=ref):
                ref[...] = g

            @pl.when(i > 0)
            def _(g=g, ref=ref):
                ref[...] += g

    res = pl.pallas_call(
        body, name=name, grid=(ncol, L // tile), in_specs=in_specs, out_specs=out_specs, out_shape=out_shape,
        compiler_params=_cparams(("parallel", "arbitrary")),
    )(*[r[0] for r in rows], *pars, *douts)
    return res[:len(g_widths)], res[len(g_widths):]


def _sigmoid(x):
    return 1.0 / (1.0 + jnp.exp(-x))


def _softplus(x):
    return jnp.maximum(x, 0.0) + jnp.log(1.0 + jnp.exp(-jnp.abs(x)))


def _silu(x):
    return x * _sigmoid(x)


def _make_res_ln_fn(scale):
    def fn(row0, h, o, gam, bet):
        pre = ALPHA * h + scale * o
        mu = jnp.mean(pre, axis=-1, keepdims=True)
        xc = pre - mu
        var = jnp.mean(xc * xc, axis=-1, keepdims=True)
        return (xc * lax.rsqrt(var + EPS) * gam + bet,)
    return fn


def _ssd_post_fn(row0, y, xs, z, dskip, normg):
    v = (y + dskip * xs) * _silu(z)
    v = v * lax.rsqrt(jnp.mean(v * v, axis=-1, keepdims=True) + EPS)
    return (v * normg,)


def _mla_norm_fn(row0, cq, ckv, gq, gkv):
    qn = cq * lax.rsqrt(jnp.mean(cq * cq, axis=-1, keepdims=True) + EPS) * gq
    cn = ckv * lax.rsqrt(jnp.mean(ckv * ckv, axis=-1, keepdims=True) + EPS) * gkv
    return qn, cn


def _rope_fn(row0, q, k, cosf, sins):
    return (q * cosf + pltpu.roll(q, 64, 1) * sins, k * cosf + pltpu.roll(k, 64, 1) * sins)


def _rope_t_fn(row0, gq, gk, cosf, sins):
    return (gq * cosf + pltpu.roll(gq * sins, 64, 1), gk * cosf + pltpu.roll(gk * sins, 64, 1))


def _conv_fwd(x, x_off, w, b, *, name):
    C = w.shape[1]

    def body(x_ref, w_ref, b_ref, o_ref):
        rows = lax.broadcasted_iota(jnp.int32, (LP, BLOCK), 0)
        xv = jnp.where(rows >= PAD_ROWS, x_ref[...], 0.0)
        acc = b_ref[...] + w_ref[3:4, :] * xv
        for k in range(SSD_CONV - 1):
            acc = acc + w_ref[k:k + 1, :] * pltpu.roll(xv, SSD_CONV - 1 - k, 0)
        o_ref[...] = _silu(acc)

    return pl.pallas_call(
        body, name=name, grid=(C // BLOCK,),
        in_specs=[_bs((LP, BLOCK), lambda j: (0, j + x_off)), _bs((SSD_CONV, BLOCK), lambda j: (0, j)),
                  _bs((1, BLOCK), lambda j: (0, j))],
        out_specs=_bs((LP, BLOCK), lambda j: (0, j)),
        out_shape=jax.ShapeDtypeStruct((LP, C), F32), compiler_params=_cparams(("parallel",)),
    )(x, w, b)


def _conv_bwd(x, x_off, w, b, dout, *, name):
    C = w.shape[1]

    def body(x_ref, w_ref, b_ref, do_ref, dx_ref, dw_ref, db_ref):
        rows = lax.broadcasted_iota(jnp.int32, (LP, BLOCK), 0)
        real = rows >= PAD_ROWS
        xv = jnp.where(real, x_ref[...], 0.0)
        shifted = [pltpu.roll(xv, SSD_CONV - 1 - k, 0) for k in range(SSD_CONV - 1)] + [xv]
        acc = b_ref[...]
        for k in range(SSD_CONV):
            acc = acc + w_ref[k:k + 1, :] * shifted[k]
        sig = _sigmoid(acc)
        dacc = jnp.where(real, do_ref[...] * (sig * (1.0 + acc * (1.0 - sig))), 0.0)
        db_ref[...] = jnp.sum(dacc, axis=0, keepdims=True)
        dx = w_ref[3:4, :] * dacc
        for k in range(SSD_CONV):
            dw_ref[k:k + 1, :] = jnp.sum(dacc * shifted[k], axis=0, keepdims=True)
            if k < SSD_CONV - 1:
                dx = dx + w_ref[k:k + 1, :] * pltpu.roll(dacc, LP - (SSD_CONV - 1 - k), 0)
        dx_ref[...] = jnp.where(real, dx, 0.0)

    return pl.pallas_call(
        body, name=name, grid=(C // BLOCK,),
        in_specs=[_bs((LP, BLOCK), lambda j: (0, j + x_off)), _bs((SSD_CONV, BLOCK), lambda j: (0, j)),
                  _bs((1, BLOCK), lambda j: (0, j)), _bs((LP, BLOCK), lambda j: (0, j))],
        out_specs=[_bs((LP, BLOCK), lambda j: (0, j)), _bs((SSD_CONV, BLOCK), lambda j: (0, j)),
                   _bs((1, BLOCK), lambda j: (0, j))],
        out_shape=[jax.ShapeDtypeStruct((LP, C), F32), jax.ShapeDtypeStruct((SSD_CONV, C), F32),
                   jax.ShapeDtypeStruct((1, C), F32)],
        compiler_params=_cparams(("parallel",)),
    )(x, w, b, dout)


_BDIMS = {"nn": (((2,), (1,)), ((0,), (0,))), "nt": (((2,), (2,)), ((0,), (0,))), "tn": (((1,), (1,)), ((0,), (0,)))}


def _raw_bdot3(a, b, mode):
    return lax.dot_general(a.astype(BF16), b.astype(BF16), _BDIMS[mode], preferred_element_type=F32)


@functools.partial(jax.custom_vjp, nondiff_argnums=(2,))
def _bdot3(a, b, mode):
    return _raw_bdot3(a, b, mode)


def _bdot3_fwd(a, b, mode):
    return _raw_bdot3(a, b, mode), (a, b)


def _bdot3_bwd(mode, res, g):
    a, b = res
    if mode == "nn":
        return _raw_bdot3(g, b, "nt"), _raw_bdot3(a, g, "tn")
    if mode == "nt":
        return _raw_bdot3(g, b, "nn"), _raw_bdot3(g, a, "tn")
    return _raw_bdot3(b, g, "nt"), _raw_bdot3(a, g, "nn")


_bdot3.defvjp(_bdot3_fwd, _bdot3_bwd)


def _ssd_chunk(x, bm, cm, dt, dtt, alog, prev):
    rep = SSD_HEADS // SSD_GROUPS
    per_head = lambda t: jnp.broadcast_to(t[:, None], (SSD_GROUPS, rep) + t.shape[1:]).reshape((SSD_HEADS,) + t.shape[1:])
    bm, cm = per_head(bm), per_head(cm)
    lane_h = lax.broadcasted_iota(jnp.int32, (1, BLOCK), 1)
    row_h = lax.broadcasted_iota(jnp.int32, (BLOCK, 1), 0)
    dtc = jnp.stack([jnp.sum(jnp.where(lane_h == h, dt, 0.0), axis=1, keepdims=True) for h in range(SSD_HEADS)])
    dtr = jnp.stack([jnp.sum(jnp.where(row_h == h, dtt, 0.0), axis=0, keepdims=True) for h in range(SSD_HEADS)])
    lane = lax.broadcasted_iota(jnp.int32, alog.shape, 2)
    a_neg = -jnp.exp(jnp.sum(jnp.where(lane == 0, alog, 0.0), axis=2, keepdims=True))
    ac_in = dtc * a_neg
    ar_in = dtr * a_neg
    li = lax.broadcasted_iota(jnp.int32, (1, BLOCK, BLOCK), 1)
    si = lax.broadcasted_iota(jnp.int32, (1, BLOCK, BLOCK), 2)
    causal = li >= si
    acum_c = jnp.sum(jnp.where(causal, ar_in, 0.0), axis=2, keepdims=True)
    acum_r = jnp.sum(jnp.where(li <= si, ac_in, 0.0), axis=1, keepdims=True)
    total = jnp.sum(ar_in, axis=2, keepdims=True)
    seg = jnp.exp(jnp.where(causal, acum_c - acum_r, NEG))
    xdt = x * dtc
    cb = _bdot3(cm, bm, "nt")
    y = _bdot3(cb * seg, xdt, "nn") + _bdot3(cm, prev, "nt") * jnp.exp(acum_c)
    st = _bdot3(xdt, bm * jnp.exp(total - acum_c), "tn")
    return y, prev * jnp.exp(total) + st


def _ssd_dt_fwd(raw, raw_blk, bias, *, name):
    def body(raw_ref, b_ref, dt_ref, dtt_ref):
        rows = lax.broadcasted_iota(jnp.int32, (LP, BLOCK), 0)
        dt = jnp.where(rows >= PAD_ROWS, _softplus(raw_ref[...] + b_ref[...]), 0.0)
        dt_ref[...] = dt
        dtt_ref[...] = dt.T

    return pl.pallas_call(
        body, name=name, grid=(1,),
        in_specs=[_bs((LP, BLOCK), lambda j: (0, raw_blk)), _bs((1, BLOCK), lambda j: (0, 0))],
        out_specs=[_bs((LP, BLOCK), lambda j: (0, 0)), _bs((BLOCK, LP), lambda j: (0, 0))],
        out_shape=[jax.ShapeDtypeStruct((LP, BLOCK), F32), jax.ShapeDtypeStruct((BLOCK, LP), F32)],
        compiler_params=_cparams(("arbitrary",)),
    )(raw, bias)


def _ssd_dt_bwd(raw, raw_blk, bias, ddt, ddtt, *, name):
    def body(raw_ref, b_ref, ddt_ref, ddtt_ref, draw_ref, db_ref):
        rows = lax.broadcasted_iota(jnp.int32, (LP, BLOCK), 0)
        g = ddt_ref[...] + ddtt_ref[...].T
        draw = jnp.where(rows >= PAD_ROWS, g * _sigmoid(raw_ref[...] + b_ref[...]), 0.0)
        draw_ref[...] = draw
        db_ref[...] = jnp.sum(draw, axis=0, keepdims=True)

    return pl.pallas_call(
        body, name=name, grid=(1,),
        in_specs=[_bs((LP, BLOCK), lambda j: (0, raw_blk)), _bs((1, BLOCK), lambda j: (0, 0)),
                  _bs((LP, BLOCK), lambda j: (0, 0)), _bs((BLOCK, LP), lambda j: (0, 0))],
        out_specs=[_bs((LP, BLOCK), lambda j: (0, 0)), _bs((1, BLOCK), lambda j: (0, 0))],
        out_shape=[jax.ShapeDtypeStruct((LP, BLOCK), F32), jax.ShapeDtypeStruct((1, BLOCK), F32)],
        compiler_params=_cparams(("arbitrary",)),
    )(raw, bias, ddt, ddtt)


def _ssd_specs(rev):
    ci = (lambda c: N_CHUNK - 1 - c) if rev else (lambda c: c)
    x_spec = _bs((SSD_HEADS, BLOCK, SSD_HD), lambda c: (0, ci(c), 0))
    g_spec = _bs((SSD_GROUPS, BLOCK, SSD_STATE), lambda c: (0, ci(c), 0))
    dtc_spec = _bs((BLOCK, BLOCK), lambda c: (ci(c), 0))
    dtr_spec = _bs((BLOCK, BLOCK), lambda c: (0, ci(c)))
    al_spec = _bs((SSD_HEADS, 1, BLOCK), lambda c: (0, 0, 0))
    st_spec = _bs((None, SSD_HEADS, SSD_HD, SSD_STATE), lambda c: (ci(c), 0, 0, 0))
    return x_spec, g_spec, dtc_spec, dtr_spec, al_spec, st_spec


def _ssd_fwd(x, bm, cm, dtc, dtr, alog, *, name):
    x_spec, g_spec, dtc_spec, dtr_spec, al_spec, st_spec = _ssd_specs(False)

    def body(x_ref, b_ref, c_ref, dtc_ref, dtr_ref, al_ref, y_ref, prev_ref, state):
        @pl.when(pl.program_id(0) == 0)
        def _():
            state[...] = jnp.zeros_like(state)

        prev = state[...]
        prev_ref[...] = prev
        y, new = _ssd_chunk(x_ref[...], b_ref[...], c_ref[...], dtc_ref[...], dtr_ref[...], al_ref[...], prev)
        y_ref[...] = y
        state[...] = new

    return pl.pallas_call(
        body, name=name, grid=(N_CHUNK,),
        in_specs=[x_spec, g_spec, g_spec, dtc_spec, dtr_spec, al_spec], out_specs=[x_spec, st_spec],
        out_shape=[jax.ShapeDtypeStruct((SSD_HEADS, LP, SSD_HD), F32),
                   jax.ShapeDtypeStruct((N_CHUNK, SSD_HEADS, SSD_HD, SSD_STATE), F32)],
        scratch_shapes=[pltpu.VMEM((SSD_HEADS, SSD_HD, SSD_STATE), F32)],
        compiler_params=_cparams(("arbitrary",)),
    )(x, bm, cm, dtc, dtr, alog)


def _ssd_bwd(x, bm, cm, dtc, dtr, alog, prevs, dy, *, name):
    x_spec, g_spec, dtc_spec, dtr_spec, al_spec, st_spec = _ssd_specs(True)

    def body(x_ref, b_ref, c_ref, dtc_ref, dtr_ref, al_ref, prev_ref, dy_ref,
             dx_ref, db_ref, dc_ref, ddtc_ref, ddtr_ref, dal_ref, dstate):
        c = pl.program_id(0)

        @pl.when(c == 0)
        def _():
            dstate[...] = jnp.zeros_like(dstate)

        _, vjp = jax.vjp(_ssd_chunk, x_ref[...], b_ref[...], c_ref[...], dtc_ref[...], dtr_ref[...], al_ref[...],
                         prev_ref[...])
        dx, db, dc, ddtc, ddtr, dal, dprev = vjp((dy_ref[...], dstate[...]))
        dx_ref[...] = dx
        db_ref[...] = db
        dc_ref[...] = dc
        ddtc_ref[...] = ddtc
        ddtr_ref[...] = ddtr
        dstate[...] = dprev

        @pl.when(c == 0)
        def _():
            dal_ref[...] = dal

        @pl.when(c > 0)
        def _():
            dal_ref[...] += dal

    hs = jax.ShapeDtypeStruct((SSD_HEADS, LP, SSD_HD), F32)
    gs = jax.ShapeDtypeStruct((SSD_GROUPS, LP, SSD_STATE), F32)
    return pl.pallas_call(
        body, name=name, grid=(N_CHUNK,),
        in_specs=[x_spec, g_spec, g_spec, dtc_spec, dtr_spec, al_spec, st_spec, x_spec],
        out_specs=[x_spec, g_spec, g_spec, dtc_spec, dtr_spec, al_spec],
        out_shape=[hs, gs, gs, jax.ShapeDtypeStruct((LP, BLOCK), F32),
                   jax.ShapeDtypeStruct((BLOCK, LP), F32), jax.ShapeDtypeStruct((SSD_HEADS, 1, BLOCK), F32)],
        scratch_shapes=[pltpu.VMEM((SSD_HEADS, SSD_HD, SSD_STATE), F32)],
        compiler_params=_cparams(("arbitrary",)),
    )(x, bm, cm, dtc, dtr, alog, prevs, dy)


def _tri_dot(tri, v):
    hi = v.astype(BF16)
    r1 = v - hi.astype(F32)
    mid = r1.astype(BF16)
    lo = (r1 - mid.astype(F32)).astype(BF16)
    t = tri.astype(BF16)
    d = lambda p: lax.dot_general(t, p, _dims(1, 0), preferred_element_type=F32)
    return d(hi) + d(mid) + d(lo)


def _fox_gate_fwd(raw, raw_blk, bias, *, name):
    def body(raw_ref, b_ref, c_ref, ct_ref):
        li = lax.broadcasted_iota(jnp.int32, (BLOCK, BLOCK), 0)
        si = lax.broadcasted_iota(jnp.int32, (BLOCK, BLOCK), 1)
        tri = jnp.where(li >= si, 1.0, 0.0)
        carry = jnp.zeros((1, BLOCK), F32)
        for j in range(N_CHUNK):
            r = slice(j * BLOCK, (j + 1) * BLOCK)
            lf = jnp.where(j * BLOCK + li >= PAD_ROWS, -_softplus(-(raw_ref[r, :] + b_ref[...])), 0.0)
            cv = _tri_dot(tri, lf) + carry
            c_ref[r, :] = cv
            ct_ref[:, r] = cv.T
            carry = carry + jnp.sum(lf, axis=0, keepdims=True)

    return pl.pallas_call(
        body, name=name, grid=(1,),
        in_specs=[_bs((LP, BLOCK), lambda j: (0, raw_blk)), _bs((1, BLOCK), lambda j: (0, 0))],
        out_specs=[_bs((LP, BLOCK), lambda j: (0, 0)), _bs((BLOCK, LP), lambda j: (0, 0))],
        out_shape=[jax.ShapeDtypeStruct((LP, BLOCK), F32), jax.ShapeDtypeStruct((BLOCK, LP), F32)],
        compiler_params=_cparams(("arbitrary",)),
    )(raw, bias)


def _fox_gate_bwd(raw, raw_blk, bias, dc, dct, *, name):
    def body(raw_ref, b_ref, dc_ref, dct_ref, draw_ref, db_ref):
        li = lax.broadcasted_iota(jnp.int32, (BLOCK, BLOCK), 0)
        si = lax.broadcasted_iota(jnp.int32, (BLOCK, BLOCK), 1)
        tri_t = jnp.where(li <= si, 1.0, 0.0)
        carry = jnp.zeros((1, BLOCK), F32)
        dsum = jnp.zeros((1, BLOCK), F32)
        for j in reversed(range(N_CHUNK)):
            r = slice(j * BLOCK, (j + 1) * BLOCK)
            dcv = dc_ref[r, :] + dct_ref[:, r].T
            dlf = _tri_dot(tri_t, dcv) + carry
            carry = carry + jnp.sum(dcv, axis=0, keepdims=True)
            draw = jnp.where(j * BLOCK + li >= PAD_ROWS, dlf * (1.0 - _sigmoid(raw_ref[r, :] + b_ref[...])), 0.0)
            draw_ref[r, :] = draw
            dsum = dsum + jnp.sum(draw, axis=0, keepdims=True)
        db_ref[...] = dsum

    return pl.pallas_call(
        body, name=name, grid=(1,),
        in_specs=[_bs((LP, BLOCK), lambda j: (0, raw_blk)), _bs((1, BLOCK), lambda j: (0, 0)),
                  _bs((LP, BLOCK), lambda j: (0, 0)), _bs((BLOCK, LP), lambda j: (0, 0))],
        out_specs=[_bs((LP, BLOCK), lambda j: (0, 0)), _bs((1, BLOCK), lambda j: (0, 0))],
        out_shape=[jax.ShapeDtypeStruct((LP, BLOCK), F32), jax.ShapeDtypeStruct((1, BLOCK), F32)],
        compiler_params=_cparams(("arbitrary",)),
    )(raw, bias, dc, dct)


ATT_W = 256
ATT_QB = 272
ATT_STEPS = LP // ATT_QB
ATT_KEYS = (640, 1152, 1664, LP)
ATT_BLOCKS_PER_CLASS = ATT_STEPS // len(ATT_KEYS)


def _lane_head(width, per, mod=None):
    lane = lax.broadcasted_iota(jnp.int32, (1, width), 1)
    if mod is not None:
        lane = lane % mod
    return lane // per


def _attn_mask(i, kw):
    r = i * ATT_QB + lax.broadcasted_iota(jnp.int32, (ATT_QB, kw), 0)
    c = lax.broadcasted_iota(jnp.int32, (ATT_QB, kw), 1)
    return (c <= r) & ((c >= PAD_ROWS) | (r < PAD_ROWS))


def _attn_by_key_class(i, fn):
    for p, kw in enumerate(ATT_KEYS):
        @pl.when(i // ATT_BLOCKS_PER_CLASS == p)
        def _(kw=kw):
            fn(kw)


def _attn_specs(q, k, v, bias, rope):
    qspec = lambda blk, w=ATT_W: _bs((ATT_QB, w), lambda i: (i, blk))
    fspec = lambda blk, w=ATT_W: _bs((LP, w), lambda i: (0, blk))
    ins = [q[0], k[0], v[0]]
    specs = [qspec(q[1]), fspec(k[1]), fspec(v[1])]
    if bias is not None:
        ins += [bias[0], bias[1]]
        specs += [qspec(0, BLOCK), _bs((BLOCK, LP), lambda i: (0, 0))]
    if rope is not None:
        ins += [rope[0][0], rope[1][0]]
        specs += [qspec(rope[0][1], BLOCK), fspec(rope[1][1], BLOCK)]
    return ins, specs, qspec, fspec


def _attn_fwd(q, k, v, *, scale, name, bias=None, rope=None):
    ins, specs, qspec, fspec = _attn_specs(q, k, v, bias, rope)
    has_bias, has_rope = bias is not None, rope is not None

    def body(*refs):
        it = iter(refs)
        q_ref, k_ref, v_ref = next(it), next(it), next(it)
        if has_bias:
            c_ref, ct_ref = next(it), next(it)
        if has_rope:
            qr_ref, kr_ref = next(it), next(it)
        o_ref, lse_ref = next(it), next(it)
        i = pl.program_id(0)

        def block(kw):
            ok = _attn_mask(i, kw)
            qv, kv, vv = q_ref[...].astype(BF16), k_ref[0:kw, :].astype(BF16), v_ref[0:kw, :].astype(BF16)
            hid, l128 = _lane_head(ATT_W, FOX_HD), _lane_head(BLOCK, 1)
            if has_rope:
                rid = _lane_head(BLOCK, ROPE_HALF, 64)
                qrv, krv = qr_ref[...].astype(BF16), kr_ref[0:kw, :].astype(BF16)
            def head(h, carry):
                o_acc, lse_acc = carry
                s = _raw_bdot(jnp.where(hid == h, qv, 0.0), kv, 1, 1)
                if has_rope:
                    s = s + _raw_bdot(jnp.where(rid == h, qrv, 0.0), krv, 1, 1)
                s = s * scale
                if has_bias:
                    cq = jnp.sum(jnp.where(l128 == h, c_ref[...], 0.0), axis=1, keepdims=True)
                    s = s + (cq - ct_ref[pl.ds(h, 1), 0:kw])
                s = jnp.where(ok, s, NEG)
                m = jnp.max(s, axis=1, keepdims=True)
                p = jnp.exp(s - m)
                l = jnp.sum(p, axis=1, keepdims=True)
                o_acc = jnp.where(hid == h, _raw_bdot(p, vv, 1, 0) / l, o_acc)
                lse_acc = jnp.where(l128 == h, m + jnp.log(l), lse_acc)
                return o_acc, lse_acc

            o_acc, lse_acc = lax.fori_loop(
                0, FOX_HEADS, head, (jnp.zeros((ATT_QB, ATT_W), F32), jnp.zeros((ATT_QB, BLOCK), F32)), unroll=True)
            o_ref[...] = o_acc
            lse_ref[...] = lse_acc

        _attn_by_key_class(i, block)

    return pl.pallas_call(
        body, name=name, grid=(ATT_STEPS,), in_specs=specs, out_specs=[qspec(0), qspec(0, BLOCK)],
        out_shape=[jax.ShapeDtypeStruct((LP, ATT_W), F32), jax.ShapeDtypeStruct((LP, BLOCK), F32)],
        compiler_params=_cparams(("parallel",)),
    )(*ins)


def _attn_bwd(q, k, v, o, lse, do, *, scale, name, bias=None, rope=None):
    ins, specs, qspec, fspec = _attn_specs(q, k, v, bias, rope)
    has_bias, has_rope = bias is not None, rope is not None
    ins += [o, lse, do[0]]
    specs += [qspec(0), qspec(0, BLOCK), qspec(do[1])]

    def body(*refs):
        it = iter(refs)
        q_ref, k_ref, v_ref = next(it), next(it), next(it)
        if has_bias:
            c_ref, ct_ref = next(it), next(it)
        if has_rope:
            qr_ref, kr_ref = next(it), next(it)
        o_ref, lse_ref, do_ref = next(it), next(it), next(it)
        dq_ref, dk_ref, dv_ref = next(it), next(it), next(it)
        if has_bias:
            dc_ref, dct_ref = next(it), next(it)
        if has_rope:
            dqr_ref, dkr_ref = next(it), next(it)
        i = pl.program_id(0)

        @pl.when(i == 0)
        def _():
            dk_ref[...] = jnp.zeros_like(dk_ref)
            dv_ref[...] = jnp.zeros_like(dv_ref)
            if has_rope:
                dkr_ref[...] = jnp.zeros_like(dkr_ref)
            if has_bias:
                dct_ref[...] = jnp.zeros_like(dct_ref)

        def block(kw):
            ok = _attn_mask(i, kw)
            qv, kv, vv = q_ref[...].astype(BF16), k_ref[0:kw, :].astype(BF16), v_ref[0:kw, :].astype(BF16)
            dov, lsev = do_ref[...], lse_ref[...]
            dov_ov = dov * o_ref[...]
            dov = dov.astype(BF16)
            hid, l128 = _lane_head(ATT_W, FOX_HD), _lane_head(BLOCK, 1)
            if has_rope:
                rid = _lane_head(BLOCK, ROPE_HALF, 64)
                qrv, krv = qr_ref[...].astype(BF16), kr_ref[0:kw, :].astype(BF16)

            def head(h, carry):
                dq_acc, aux_acc = carry
                qm = jnp.where(hid == h, qv, 0.0)
                s = _raw_bdot(qm, kv, 1, 1)
                if has_rope:
                    qrm = jnp.where(rid == h, qrv, 0.0)
                    s = s + _raw_bdot(qrm, krv, 1, 1)
                s = s * scale
                if has_bias:
                    cq = jnp.sum(jnp.where(l128 == h, c_ref[...], 0.0), axis=1, keepdims=True)
                    s = s + (cq - ct_ref[pl.ds(h, 1), 0:kw])
                s = jnp.where(ok, s, NEG)
                p = jnp.exp(s - jnp.sum(jnp.where(l128 == h, lsev, 0.0), axis=1, keepdims=True))
                dom = jnp.where(hid == h, dov, 0.0)
                dp = _raw_bdot(dom, vv, 1, 1)
                delta = jnp.sum(jnp.where(hid == h, dov_ov, 0.0), axis=1, keepdims=True)
                ds = p * (dp - delta)
                dsb, pb = ds.astype(BF16), p.astype(BF16)
                dq_acc = jnp.where(hid == h, _raw_bdot(dsb, kv, 1, 0) * scale, dq_acc)
                dk_ref[0:kw, :] += _raw_bdot(dsb, qm, 0, 0) * scale
                dv_ref[0:kw, :] += _raw_bdot(pb, dom, 0, 0)
                if has_rope:
                    aux_acc = jnp.where(rid == h, _raw_bdot(dsb, krv, 1, 0) * scale, aux_acc)
                    dkr_ref[0:kw, :] += _raw_bdot(dsb, qrm, 0, 0) * scale
                if has_bias:
                    aux_acc = jnp.where(l128 == h, jnp.sum(ds, axis=1, keepdims=True), aux_acc)
                    dct_ref[pl.ds(h, 1), 0:kw] -= jnp.sum(ds, axis=0, keepdims=True)
                return dq_acc, aux_acc

            dq_acc, aux_acc = lax.fori_loop(
                0, FOX_HEADS, head, (jnp.zeros((ATT_QB, ATT_W), F32), jnp.zeros((ATT_QB, BLOCK), F32)))
            dq_ref[...] = dq_acc
            if has_bias:
                dc_ref[...] = aux_acc
            if has_rope:
                dqr_ref[...] = aux_acc

        _attn_by_key_class(i, block)

    wide = jax.ShapeDtypeStruct((LP, ATT_W), F32)
    narrow = jax.ShapeDtypeStruct((LP, BLOCK), F32)
    out_specs = [qspec(0), fspec(0), fspec(0)]
    out_shape = [wide, wide, wide]
    if has_bias:
        out_specs += [qspec(0, BLOCK), _bs((BLOCK, LP), lambda i: (0, 0))]
        out_shape += [narrow, jax.ShapeDtypeStruct((BLOCK, LP), F32)]
    if has_rope:
        out_specs += [qspec(0, BLOCK), fspec(0, BLOCK)]
        out_shape += [narrow, narrow]
    return pl.pallas_call(
        body, name=name, grid=(ATT_STEPS,), in_specs=specs, out_specs=out_specs, out_shape=out_shape,
        compiler_params=_cparams(("arbitrary",)),
    )(*ins)


def _loss_head(y, target, *, name):
    tile = TM

    def body(y_ref, t_ref, dy_ref, loss_ref):
        i = pl.program_id(0)
        rows = i * tile + lax.broadcasted_iota(jnp.int32, (tile, D_MODEL), 0)
        err = jnp.where(rows >= BLOCK, y_ref[...] - t_ref[...], 0.0)
        dy_ref[...] = err * (1.0 / D_MODEL)
        part = 0.5 * jnp.sum(jnp.sum(err * err, axis=1, keepdims=True) * (1.0 / D_MODEL), axis=0, keepdims=True)
        part = jnp.broadcast_to(part, (1, BLOCK))

        @pl.when(i == 0)
        def _():
            loss_ref[...] = part

        @pl.when(i > 0)
        def _():
            loss_ref[...] += part

    return pl.pallas_call(
        body, name=name, grid=(LP // tile,),
        in_specs=[_bs((tile, D_MODEL), lambda i: (i, 0)), _bs((tile, D_MODEL), lambda i: (i, 0))],
        out_specs=[_bs((tile, D_MODEL), lambda i: (i, 0)), _bs((1, BLOCK), lambda i: (0, 0))],
        out_shape=[jax.ShapeDtypeStruct((LP, D_MODEL), F32), jax.ShapeDtypeStruct((1, BLOCK), F32)],
        compiler_params=_cparams(("arbitrary",)),
    )(y, target)


def _adamw(w, gs, m, v, *, name, after=()):
    if w.ndim == 2:
        w, m, v = w[None], m[None], v[None]
        squeeze = True
    else:
        squeeze = False
    NL, R, C = w.shape
    assert len(gs) == NL
    CG = gs[0].shape[1]
    tile = _tile(R, 352, 8)

    def body(*refs):
        w_ref, g_refs = refs[0], refs[1:1 + NL]
        m_ref, v_ref = refs[1 + NL:3 + NL]
        go_ref, d_ref, nm_ref, nv_ref = refs[3 + NL + len(after):]
        gv = g_refs[0][:, :C]
        for j in range(1, NL):
            gv = jnp.where(pl.program_id(0) == j, g_refs[j][:, :C], gv)
        nm = ADAM_B1 * m_ref[...] + (1.0 - ADAM_B1) * gv
        nv = ADAM_B2 * v_ref[...] + (1.0 - ADAM_B2) * (gv * gv)
        m_hat = nm / (1.0 - ADAM_B1 ** ADAM_STEP)
        v_hat = nv / (1.0 - ADAM_B2 ** ADAM_STEP)
        go_ref[...] = gv
        d_ref[...] = -ADAM_LR * (m_hat / (jnp.sqrt(v_hat) + ADAM_EPS) + ADAM_WD * w_ref[...])
        nm_ref[...] = nm
        nv_ref[...] = nv

    spec = _bs((None, tile, C), lambda l, i: (l, i, 0))
    gspecs = [_bs((tile, CG), lambda l, i, j=j: (jnp.where(l == j, i, 0), 0)) for j in range(NL)]
    res = pl.pallas_call(
        body, name=name, grid=(NL, R // tile), in_specs=[spec, *gspecs, spec, spec, *[ANY] * len(after)],
        out_specs=[spec] * 4, out_shape=[jax.ShapeDtypeStruct((NL, R, C), F32)] * 4,
        compiler_params=_cparams(("parallel", "parallel")),
    )(w, *gs, m, v, *after)
    return [r[0] for r in res] if squeeze else res


def _my_pos():
    return lax.axis_index("x"), lax.axis_index("y"), lax.axis_index("c")


def _other_chips(x, y):
    return [(1 - x, y), (x, 1 - y), (1 - x, 1 - y)]


def _allgather_chips(shards):
    n = len(shards)
    per = 7

    def body(*refs):
        ins, outs = refs[:n], refs[n:2 * n]
        send_sems, recv_sems = refs[2 * n], refs[2 * n + 1]
        x, y, c = _my_pos()
        chips = _other_chips(x, y)
        sibling, me = (x, y, 1 - c), 2 * x + y

        def cp(a, kk, src, dst, to):
            return pltpu.make_async_remote_copy(src_ref=src, dst_ref=dst, send_sem=send_sems.at[per * a + kk],
                                                recv_sem=recv_sems.at[per * a + kk], device_id=to, device_id_type=MESH)

        sends = []
        for a in range(n):
            for j, chip in enumerate(chips):
                sends.append(cp(a, j, ins[a].at[c], outs[a].at[me, c], (*chip, c)))
            sends.append(cp(a, 3, ins[a], outs[a].at[me], sibling))
        for s in sends:
            s.start()
        for a in range(n):
            for j, chip in enumerate(chips):
                slab = outs[a].at[2 * chip[0] + chip[1], c]
                cp(a, j, slab, slab, (x, y, c)).wait_recv()
                fwd = cp(a, 4 + j, slab, slab, sibling)
                fwd.start()
                sends.append(fwd)
        for a in range(n):
            cp(a, 3, ins[a], outs[a].at[me], (x, y, c)).wait_recv()
            for j, chip in enumerate(chips):
                slab = outs[a].at[2 * chip[0] + chip[1], 1 - c]
                cp(a, 4 + j, slab, slab, (x, y, c)).wait_recv()
        for s in sends:
            s.wait_send()

    return pl.pallas_call(
        body, name="allgather_chips", in_specs=[ANY] * n, out_specs=[ANY] * n,
        out_shape=[jax.ShapeDtypeStruct((N_CHIPS,) + s.shape, s.dtype) for s in shards],
        scratch_shapes=[pltpu.SemaphoreType.DMA((per * n,)), pltpu.SemaphoreType.DMA((per * n,))],
    )(*shards)


def _rs_swap_rows(gs, tag):
    n = len(gs)

    def body(*refs):
        ins, outs = refs[:n], refs[n:2 * n]
        send_sems, recv_sems = refs[2 * n], refs[2 * n + 1]
        x, y, c = _my_pos()
        cps = []
        for a in range(n):
            half = ins[a].shape[1] // 2
            cps.append(pltpu.make_async_remote_copy(
                src_ref=ins[a].at[:, pl.ds((1 - c) * half, half)], dst_ref=outs[a], send_sem=send_sems.at[a],
                recv_sem=recv_sems.at[a], device_id=(x, y, 1 - c), device_id_type=MESH))
        for cp in cps:
            cp.start()
        for cp in cps:
            cp.wait()

    return pl.pallas_call(
        body, name=f"rs_swap_rows_{tag}", in_specs=[ANY] * n, out_specs=[ANY] * n,
        out_shape=[jax.ShapeDtypeStruct((N_CHIPS, g.shape[1] // 2, g.shape[2]), g.dtype) for g in gs],
        scratch_shapes=[pltpu.SemaphoreType.DMA((n,)), pltpu.SemaphoreType.DMA((n,))],
    )(*gs)


RS_ADD_VMEM_BYTES = 40 * 1024 * 1024


def _rs_tile(H, C, n):
    return _tile(H, max(16, RS_ADD_VMEM_BYTES // (28 * n * C)), 16)


def _rs_add_pair(gs, rs, pos, *, name):
    n = len(gs)
    _, H, C = rs[0].shape
    tile = _rs_tile(H, C, n)
    nt = H // tile

    def body(pos_ref, *refs):
        for a in range(n):
            s = refs[a][...] + refs[n + a][...]
            refs[2 * n + 2 * a][...] = s
            refs[2 * n + 2 * a + 1][...] = s.astype(BF16)

    spec = _bs((None, tile, C), lambda k, i, pos_ref: (k, i, 0))
    g_spec = _bs((None, tile, C), lambda k, i, pos_ref: (k, pos_ref[1] * nt + i, 0))
    grid_spec = pltpu.PrefetchScalarGridSpec(
        num_scalar_prefetch=1, grid=(N_CHIPS, nt), in_specs=[g_spec] * n + [spec] * n, out_specs=[spec] * (2 * n))
    res = pl.pallas_call(
        body, name=name, grid_spec=grid_spec,
        out_shape=[jax.ShapeDtypeStruct((N_CHIPS, H, C), F32), jax.ShapeDtypeStruct((N_CHIPS, H, C), BF16)] * n,
        compiler_params=_cparams(("parallel", "parallel")),
    )(pos, *gs, *rs)
    return [(res[2 * a], res[2 * a + 1]) for a in range(n)]


def _exchange_copies(srcs, lands, send_sems, recv_sems):
    x, y, c = _my_pos()
    starts, landing = [], []
    for a in range(len(srcs)):
        for j, chip in enumerate(_other_chips(x, y)):
            sems = dict(send_sem=send_sems.at[3 * a + j], recv_sem=recv_sems.at[3 * a + j], device_id_type=MESH)
            starts.append(pltpu.make_async_remote_copy(
                src_ref=srcs[a].at[2 * chip[0] + chip[1]], dst_ref=lands[a].at[j], device_id=(*chip, c), **sems))
            landing.append(pltpu.make_async_remote_copy(
                src_ref=lands[a].at[j], dst_ref=lands[a].at[j], device_id=(x, y, c), **sems))
    return starts, landing


def _gather_copies(srcs, lands, send_sems, recv_sems):
    x, y, c = _my_pos()
    me = 2 * x + y
    starts, landing = [], []
    for a in range(len(srcs)):
        half = srcs[a].shape[0] // 2
        mine = pl.ds(c * half, half)
        for j, chip in enumerate(_other_chips(x, y)):
            sems = dict(send_sem=send_sems.at[3 * a + j], recv_sem=recv_sems.at[3 * a + j], device_id_type=MESH)
            starts.append(pltpu.make_async_remote_copy(
                src_ref=srcs[a].at[mine], dst_ref=lands[a].at[me, mine], device_id=(*chip, c), **sems))
            slab = lands[a].at[2 * chip[0] + chip[1], mine]
            landing.append(pltpu.make_async_remote_copy(src_ref=slab, dst_ref=slab, device_id=(x, y, c), **sems))
    return starts, landing


HBM = pl.BlockSpec(memory_space=pltpu.HBM)
SEM = pl.BlockSpec(memory_space=pltpu.SEMAPHORE)


def _ici_start(copies_fn, srcs, land_shapes, *, name, after=(), sems_per_array=3):
    n, na = len(srcs), len(after)

    def body(*refs):
        starts, _ = copies_fn(refs[:n], refs[n:2 * n], refs[2 * n + na], refs[2 * n + na + 1])
        for cp in starts:
            cp.start()
        refs[-1][...] = jnp.zeros_like(refs[-1])

    sems = pltpu.SemaphoreType.DMA((sems_per_array * n,))
    hbm = lambda s: pltpu.HBM(s.shape, s.dtype)
    lands = [pltpu.with_memory_space_constraint(
        lax.empty(s.shape, s.dtype) if isinstance(s, jax.ShapeDtypeStruct) else s, pltpu.HBM) for s in land_shapes]
    res = pl.pallas_call(
        body, name=name, in_specs=[HBM] * (2 * n) + [ANY] * na,
        out_specs=(SEM, SEM, *[HBM] * (2 * n), pl.BlockSpec(memory_space=pltpu.VMEM)),
        out_shape=(sems, sems, *[hbm(s) for s in srcs], *[hbm(s) for s in land_shapes],
                   jax.ShapeDtypeStruct((8, BLOCK), F32)),
        input_output_aliases={i: 2 + i for i in range(2 * n)},
        compiler_params=pltpu.CompilerParams(has_side_effects=pltpu.SideEffectType.DATAFLOW_SIDE_EFFECTING),
    )(*[pltpu.with_memory_space_constraint(s, pltpu.HBM) for s in srcs], *lands, *after)
    return res[0], res[1], list(res[2:2 + n]), list(res[2 + n:2 + 2 * n]), res[-1]


def _ici_wait(copies_fn, send_sems, recv_sems, srcs, lands, after, *, name):
    n = len(srcs)
    after = list(after) if isinstance(after, (list, tuple)) else [after]

    def body(*refs):
        starts, landing = copies_fn(refs[:n], refs[n:2 * n], refs[2 * n], refs[2 * n + 1])
        for cp in starts:
            cp.wait_send()
        for cp in landing:
            cp.wait_recv()

    hbm = lambda s: pltpu.HBM(s.shape, s.dtype)
    res = pl.pallas_call(
        body, name=name, in_specs=[*[HBM] * (2 * n), SEM, SEM, *[ANY] * len(after)], out_specs=[HBM] * (2 * n),
        out_shape=[*[hbm(s) for s in srcs], *[hbm(s) for s in lands]],
        input_output_aliases={i: i for i in range(2 * n)},
        compiler_params=pltpu.CompilerParams(has_side_effects=pltpu.SideEffectType.DATAFLOW_SIDE_EFFECTING),
    )(*srcs, *lands, send_sems, recv_sems, *after)
    return list(res[:n]), list(res[n:])


D2D_COPIES = 4


def _d2d_copies(ins, outs, send_sems, recv_sems):
    x, y, c = _my_pos()
    me, sibling = 2 * x + y, (x, y, 1 - c)
    starts, landing = [], []
    for a in range(len(ins)):
        half = ins[a].shape[0] // 2
        mine, theirs = pl.ds(c * half, half), pl.ds((1 - c) * half, half)
        pairs = [(ins[a], outs[a].at[me], outs[a].at[me])]
        for chip in _other_chips(x, y):
            k = 2 * chip[0] + chip[1]
            pairs.append((outs[a].at[k, mine], outs[a].at[k, mine], outs[a].at[k, theirs]))
        for j, (src, dst, lands_here) in enumerate(pairs):
            sems = dict(send_sem=send_sems.at[D2D_COPIES * a + j], recv_sem=recv_sems.at[D2D_COPIES * a + j],
                        device_id_type=MESH)
            starts.append(pltpu.make_async_remote_copy(src_ref=src, dst_ref=dst, device_id=sibling, **sems))
            landing.append(pltpu.make_async_remote_copy(src_ref=lands_here, dst_ref=lands_here, device_id=(x, y, c),
                                                        **sems))
    return starts, landing


def _gather_d2d(shards, lands, tag):
    n = len(shards)

    def body(*refs):
        starts, landing = _d2d_copies(refs[:n], refs[2 * n:3 * n], refs[3 * n], refs[3 * n + 1])
        for cp in starts:
            cp.start()
        for cp in landing:
            cp.wait_recv()
        for cp in starts:
            cp.wait_send()

    return pl.pallas_call(
        body, name=f"gather_d2d_{tag}", in_specs=[ANY] * (2 * n), out_specs=[ANY] * n,
        out_shape=[jax.ShapeDtypeStruct(s.shape, s.dtype) for s in lands],
        input_output_aliases={n + a: a for a in range(n)},
        scratch_shapes=[pltpu.SemaphoreType.DMA((D2D_COPIES * n,)), pltpu.SemaphoreType.DMA((D2D_COPIES * n,))],
    )(*shards, *lands)


def _rs_add_chips(p32s, r16s, pos, *, name):
    n = len(p32s)
    _, H, C = p32s[0].shape
    tile = _rs_tile(H, C, n)
    nt = H // tile

    def body(pos_ref, *refs):
        for a in range(n):
            p_ref, r_ref = refs[a], refs[n + a]
            refs[2 * n + a][...] = ((p_ref[...] + r_ref[0].astype(F32)) + r_ref[1].astype(F32)) + r_ref[2].astype(F32)

    grid_spec = pltpu.PrefetchScalarGridSpec(
        num_scalar_prefetch=1, grid=(nt,),
        in_specs=[_bs((None, tile, C), lambda i, pos_ref: (pos_ref[0], i, 0))] * n
        + [_bs((3, tile, C), lambda i, pos_ref: (0, i, 0))] * n,
        out_specs=[_bs((tile, C), lambda i, pos_ref: (pos_ref[1] * nt + i, 0))] * n)
    return pl.pallas_call(
        body, name=name, grid_spec=grid_spec, out_shape=[jax.ShapeDtypeStruct((2 * H, C), F32)] * n,
        compiler_params=_cparams(("parallel",)),
    )(pos, *p32s, *r16s)


def _rs_join_rows(fs, tag):
    n = len(fs)

    def body(*refs):
        outs = refs[n:2 * n]
        send_sems, recv_sems = refs[2 * n], refs[2 * n + 1]
        x, y, c = _my_pos()
        for a in range(n):
            half = outs[a].shape[0] // 2
            mine = outs[a].at[pl.ds(c * half, half)]
            pltpu.make_async_remote_copy(src_ref=mine, dst_ref=mine, send_sem=send_sems.at[a],
                                         recv_sem=recv_sems.at[a], device_id=(x, y, 1 - c), device_id_type=MESH).start()
        for a in range(n):
            half = outs[a].shape[0] // 2
            pltpu.make_async_remote_copy(
                src_ref=outs[a].at[pl.ds(c * half, half)], dst_ref=outs[a].at[pl.ds((1 - c) * half, half)],
                send_sem=send_sems.at[a], recv_sem=recv_sems.at[a], device_id=(x, y, 1 - c), device_id_type=MESH).wait()

    return pl.pallas_call(
        body, name=f"rs_join_rows_{tag}", in_specs=[ANY] * n, out_specs=[ANY] * n,
        out_shape=[jax.ShapeDtypeStruct(f.shape, f.dtype) for f in fs],
        input_output_aliases={a: a for a in range(n)},
        scratch_shapes=[pltpu.SemaphoreType.DMA((n,)), pltpu.SemaphoreType.DMA((n,))],
    )(*fs)


def _pos_vector():
    x, y, c = _my_pos()
    return jnp.stack([2 * x + y, c]).astype(jnp.int32)


def _swap_copies(srcs, lands, send_sems, recv_sems):
    x, y, c = _my_pos()
    starts, landing = [], []
    for a in range(len(srcs)):
        half = srcs[a].shape[1] // 2
        sems = dict(send_sem=send_sems.at[3 * a], recv_sem=recv_sems.at[3 * a], device_id_type=MESH)
        starts.append(pltpu.make_async_remote_copy(
            src_ref=srcs[a].at[:, pl.ds((1 - c) * half, half)], dst_ref=lands[a], device_id=(x, y, 1 - c), **sems))
        landing.append(pltpu.make_async_remote_copy(src_ref=lands[a], dst_ref=lands[a], device_id=(x, y, c), **sems))
    return starts, landing


def _swap_land_shapes(gs):
    return [jax.ShapeDtypeStruct((N_CHIPS, g.shape[1] // 2, g.shape[2]), g.dtype) for g in gs]


def _same_shape_runs(arrays):
    runs, start = [], 0
    for i in range(1, len(arrays) + 1):
        if i == len(arrays) or arrays[i].shape != arrays[start].shape:
            runs.append((start, i))
            start = i
    return runs


def _rs_add_pairs(gs, r1, names, tag):
    pos = _pos_vector()
    out = []
    for a, b in _same_shape_runs(gs):
        out += _rs_add_pair(gs[a:b], r1[a:b], pos, name=f"rs_add_pair_{tag}_{names[a]}")
    return out


def _rs_pair_sums(gs, names, tag):
    return _rs_add_pairs(gs, _rs_swap_rows(gs, tag), names, tag)


def _rs_finish(pairs, r2, names, tag):
    pos = _pos_vector()
    p32s = [p[0] for p in pairs]
    fs = []
    for a, b in _same_shape_runs(p32s):
        fs += _rs_add_chips(p32s[a:b], r2[a:b], pos, name=f"rs_add_chips_{tag}_{names[a]}")
    return _rs_join_rows(fs, tag)


def _exchange_land_shapes(pairs):
    return [jax.ShapeDtypeStruct((3,) + p[1].shape[1:], p[1].dtype) for p in pairs]


def _allreduce_small(buf):
    R, W = buf.shape

    def body(b_ref, o_ref, gather, send_sems, recv_sems):
        x, y, c = _my_pos()
        me = 4 * x + 2 * y + c
        gather[me] = b_ref[...]
        cps = []
        for d in range(1, 8):
            peer = (x ^ (d >> 2), y ^ ((d >> 1) & 1), c ^ (d & 1))
            cps.append(pltpu.make_async_remote_copy(
                src_ref=b_ref, dst_ref=gather.at[me], send_sem=send_sems.at[d - 1], recv_sem=recv_sems.at[d - 1],
                device_id=peer, device_id_type=MESH))
        for cp in cps:
            cp.start()
        for d in range(1, 8):
            pltpu.make_async_remote_copy(
                src_ref=b_ref, dst_ref=gather.at[me ^ d], send_sem=send_sems.at[d - 1], recv_sem=recv_sems.at[d - 1],
                device_id=(x, y, c), device_id_type=MESH).wait_recv()
        for cp in cps:
            cp.wait_send()
        acc = gather[0]
        for d in range(1, 8):
            acc = acc + gather[d]
        o_ref[...] = acc

    vm = pl.BlockSpec(memory_space=pltpu.VMEM)
    return pl.pallas_call(
        body, name="allreduce_small", in_specs=[vm], out_specs=vm, out_shape=jax.ShapeDtypeStruct((R, W), F32),
        scratch_shapes=[pltpu.VMEM((8, R, W), F32), pltpu.SemaphoreType.DMA((7,)), pltpu.SemaphoreType.DMA((7,))],
    )(buf)


def _heads(a, h, d):
    return a.reshape(a.shape[0], h, d).transpose(1, 0, 2)


def _unheads(a):
    h, L, d = a.shape
    return a.transpose(1, 0, 2).reshape(L, h * d)


def _rope_tables():
    pos = jnp.maximum(jnp.arange(LP, dtype=F32) - PAD_ROWS, 0.0)
    inv_freq = 1.0 / (ROPE_THETA ** (jnp.arange(0, MLA_ROPE, 2, dtype=F32) / MLA_ROPE))
    ang = pos[:, None] * inv_freq[None, :]
    cos, sin = jnp.tile(jnp.cos(ang), (1, MLA_HEADS)), jnp.tile(jnp.sin(ang), (1, MLA_HEADS))
    return jnp.concatenate([cos, cos], axis=1), jnp.concatenate([-sin, sin], axis=1)


def _lane_pad(a, width=BLOCK):
    return jnp.pad(a, ((0, 0), (0, width - a.shape[1])))


def _pad_in_proj(w):
    sl = lambda start, size: w[:, start:start + size]
    return jnp.concatenate([
        sl(OC_Z, 512), sl(OC_XBC, 768), sl(OC_FQ, 256), sl(OC_FK, 256), sl(OC_FV, 256), sl(OC_CQ, 256), sl(OC_CKV, 128),
        _lane_pad(sl(OC_DT, SSD_HEADS)), _lane_pad(sl(OC_FR, FOX_HEADS)),
        jnp.tile(sl(OC_KR, ROPE_HALF), (1, MLA_HEADS)), jnp.tile(sl(OC_KR + ROPE_HALF, ROPE_HALF), (1, MLA_HEADS))], axis=1)


def _in_proj_grad_chunks(wp):
    rope = lambda start: wp[:, start:start + 64].reshape(wp.shape[0], MLA_HEADS, ROPE_HALF).sum(axis=1)
    segs = [(wp, PC_Z, 512), (wp, PC_XBC, 768), (wp, PC_DT, SSD_HEADS), (wp, PC_FQ, 256), (wp, PC_FK, 256),
            (wp, PC_FV, 256), (wp, PC_FR, FOX_HEADS), (wp, PC_CQ, 256), (wp, PC_CKV, 128),
            (rope(PC_KR), 0, ROPE_HALF), (rope(PC_KR + 64), 0, ROPE_HALF)]
    chunks = []
    for k in range(N_CHIPS):
        lo, hi, pos, pieces = k * IN_SHARD, (k + 1) * IN_SHARD, 0, []
        for arr, start, size in segs:
            a, b = max(lo, pos), min(hi, pos + size)
            if a < b:
                pieces.append(arr[:, start + a - pos:start + b - pos])
            pos += size
        pieces.append(jnp.zeros((wp.shape[0], IN_SHARD_P - IN_SHARD), wp.dtype))
        chunks.append(jnp.concatenate(pieces, axis=1))
    return jnp.stack(chunks)


def _regroup_uq(w):
    w3 = w.reshape(w.shape[0], MLA_HEADS, MLA_NOPE + MLA_ROPE)
    return jnp.concatenate([w3[:, :, :MLA_NOPE].reshape(w.shape[0], -1),
                            w3[:, :, MLA_NOPE:MLA_NOPE + ROPE_HALF].reshape(w.shape[0], -1),
                            w3[:, :, MLA_NOPE + ROPE_HALF:].reshape(w.shape[0], -1)], axis=1)


def _ungroup_uq(wp):
    n = wp.shape[0]
    return jnp.concatenate([wp[:, :256].reshape(n, MLA_HEADS, MLA_NOPE), wp[:, 256:320].reshape(n, MLA_HEADS, ROPE_HALF),
                            wp[:, 320:].reshape(n, MLA_HEADS, ROPE_HALF)], axis=2).reshape(n, -1)


def _regroup_ukv(w):
    w3 = w.reshape(w.shape[0], MLA_HEADS, MLA_NOPE + MLA_V)
    return jnp.concatenate([w3[:, :, :MLA_NOPE].reshape(w.shape[0], -1), w3[:, :, MLA_NOPE:].reshape(w.shape[0], -1)],
                           axis=1)


def _ungroup_ukv(wp):
    n = wp.shape[0]
    return jnp.concatenate([wp[:, :256].reshape(n, MLA_HEADS, MLA_NOPE), wp[:, 256:].reshape(n, MLA_HEADS, MLA_V)],
                           axis=2).reshape(n, -1)


TMF = 1088
N_IF = LP // TMF


def _chunk_rows_dx(g, w, l, chunk_h, *, name):
    N = w.shape[2]
    return _mm_core(g, w, a_spec=_bs((TMF, N), lambda i, j, k: (i, 0)),
                    b_spec=_bs((None, chunk_h, N), lambda i, j, k: (j, 0, 0)),
                    o_spec=_bs((TMF, chunk_h), lambda i, j, k: (i, j)), grid=(N_IF, N_CHIPS, 1),
                    out_shape=(LP, N_CHIPS * chunk_h), ca=1, cb=1, name=name)


def _chunk_rows_dw(a, g, chunk_h, *, name):
    N = g.shape[1]
    return _mm_core(a, g, a_spec=_bs((LP, chunk_h), lambda i, j, k: (0, i)), b_spec=_bs((LP, N), lambda i, j, k: (0, 0)),
                    o_spec=_bs((None, chunk_h, N), lambda i, j, k: (i, 0, 0)), grid=(N_CHIPS, 1, 1),
                    out_shape=(N_CHIPS, chunk_h, N), ca=0, cb=0, name=name)


def _ffn_up_swiglu(h, wg, wu, *, name, after=()):
    def body(h_ref, wg_ref, wu_ref, *refs):
        g_ref, u_ref, a_ref = refs[len(after):]
        hb = h_ref[...].astype(BF16)
        g = _raw_bdot(hb, wg_ref[...], 1, 1)
        u = _raw_bdot(hb, wu_ref[...], 1, 1)
        g_ref[...] = g
        u_ref[...] = u
        a_ref[...] = (_silu(g) * u).astype(a_ref.dtype)

    w_spec = _bs((None, HP, D_MODEL), lambda i, j: (j, 0, 0))
    o_spec = _bs((TMF, HP), lambda i, j: (i, j))
    return pl.pallas_call(
        body, name=name, grid=(N_IF, N_CHIPS),
        in_specs=[_bs((TMF, D_MODEL), lambda i, j: (i, 0)), w_spec, w_spec, *[ANY] * len(after)],
        out_specs=[o_spec] * 3,
        out_shape=[jax.ShapeDtypeStruct((LP, FP), F32), jax.ShapeDtypeStruct((LP, FP), F32),
                   jax.ShapeDtypeStruct((LP, FP), BF16)],
        compiler_params=_cparams(("parallel", "parallel")),
    )(h, wg, wu, *after)


def _ffn_down_dx_swiglu(do, wd, g, u, *, name):
    def body(do_ref, wd_ref, g_ref, u_ref, dg_ref, du_ref):
        dact = _raw_bdot(do_ref[...], wd_ref[...], 1, 1)
        gv = g_ref[...]
        sig = _sigmoid(gv)
        dg_ref[...] = (dact * u_ref[...] * (sig * (1.0 + gv * (1.0 - sig)))).astype(dg_ref.dtype)
        du_ref[...] = (dact * (gv * sig)).astype(du_ref.dtype)

    blk = _bs((TMF, HP), lambda i, j: (i, j))
    return pl.pallas_call(
        body, name=name, grid=(N_IF, N_CHIPS),
        in_specs=[_bs((TMF, D_MODEL), lambda i, j: (i, 0)), _bs((None, HP, D_MODEL), lambda i, j: (j, 0, 0)), blk, blk],
        out_specs=[blk, blk], out_shape=[jax.ShapeDtypeStruct((LP, FP), BF16)] * 2,
        compiler_params=_cparams(("parallel", "parallel")),
    )(do, wd, g, u)


def _ffn_gate_up_dw(dg, du, h, *, name):
    def body(dg_ref, du_ref, h_ref, wg_ref, wu_ref):
        hb = h_ref[...].astype(BF16)
        wg_ref[...] = _raw_bdot(dg_ref[...], hb, 0, 0)
        wu_ref[...] = _raw_bdot(du_ref[...], hb, 0, 0)

    a_spec = _bs((LP, HP), lambda k: (0, k))
    o_spec = _bs((None, HP, D_MODEL), lambda k: (k, 0, 0))
    return pl.pallas_call(
        body, name=name, grid=(N_CHIPS,), in_specs=[a_spec, a_spec, _bs((LP, D_MODEL), lambda k: (0, 0))],
        out_specs=[o_spec, o_spec], out_shape=[jax.ShapeDtypeStruct((N_CHIPS, HP, D_MODEL), F32)] * 2,
        compiler_params=_cparams(("parallel",)),
    )(dg, du, h)


def _ffn_gate_up_dx(dg, du, wg, wu, add, *, name):
    def body(dg_ref, du_ref, wg_ref, wu_ref, add_ref, o_ref, acc_ref):
        k = pl.program_id(1)

        @pl.when(k == 0)
        def _():
            acc_ref[...] = jnp.zeros_like(acc_ref)

        acc_ref[...] += _raw_bdot(dg_ref[...], wg_ref[...], 1, 0) + _raw_bdot(du_ref[...], wu_ref[...], 1, 0)

        @pl.when(k == N_CHIPS - 1)
        def _():
            o_ref[...] = acc_ref[...] + add_ref[...]

    a_spec = _bs((TMF, HP), lambda i, k: (i, k))
    w_spec = _bs((None, HP, D_MODEL), lambda i, k: (k, 0, 0))
    o_spec = _bs((TMF, D_MODEL), lambda i, k: (i, 0))
    return pl.pallas_call(
        body, name=name, grid=(N_IF, N_CHIPS), in_specs=[a_spec, a_spec, w_spec, w_spec, o_spec], out_specs=o_spec,
        out_shape=jax.ShapeDtypeStruct((LP, D_MODEL), F32), scratch_shapes=[pltpu.VMEM((TMF, D_MODEL), F32)],
        compiler_params=_cparams(("parallel", "arbitrary")),
    )(dg, du, wg, wu, add)


def _chunk_rows_mm_res_ln(a, w, chunk_h, h, gam, bet, scale, *, name):
    res_ln = _make_res_ln_fn(scale)

    def body(a_ref, w_ref, h_ref, g_ref, b_ref, o_ref, y_ref, yb_ref, acc_ref):
        k = pl.program_id(1)

        @pl.when(k == 0)
        def _():
            acc_ref[...] = jnp.zeros_like(acc_ref)

        acc_ref[...] += _raw_bdot(a_ref[...], w_ref[...], 1, 0)

        @pl.when(k == N_CHIPS - 1)
        def _():
            o = acc_ref[...]
            o_ref[...] = o
            (y,) = res_ln(0, h_ref[...], o, g_ref[...], b_ref[...])
            y_ref[...] = y
            yb_ref[...] = y.astype(yb_ref.dtype)

    row = _bs((TMF, D_MODEL), lambda i, k: (i, 0))
    par = _bs((1, D_MODEL), lambda i, k: (0, 0))
    return pl.pallas_call(
        body, name=name, grid=(N_IF, N_CHIPS),
        in_specs=[_bs((TMF, chunk_h), lambda i, k: (i, k)), _bs((None, chunk_h, D_MODEL), lambda i, k: (k, 0, 0)), row,
                  par, par],
        out_specs=[row, row, row],
        out_shape=[jax.ShapeDtypeStruct((LP, D_MODEL), F32)] * 2 + [jax.ShapeDtypeStruct((LP, D_MODEL), BF16)],
        scratch_shapes=[pltpu.VMEM((TMF, D_MODEL), F32)], compiler_params=_cparams(("parallel", "arbitrary")),
    )(a, w, h, gam, bet)


def _ffn_fwd(hp, W, pre, l, gam, bet, tag, after=()):
    h, hb = hp
    g, u, act = _ffn_up_swiglu(hb, W[pre + "_w_gate"][l], W[pre + "_w_up"][l], name=f"{tag}_up_swiglu", after=after)
    o, out, outb = _chunk_rows_mm_res_ln(act, W[pre + "_w_down"][l], HP, h, gam, bet, 0.5, name=f"{tag}_down_ln")
    return (out, outb), (h, hb, g, u, act, o)


def _ffn_bwd(dout, saved, W, pre, l, gam, bet, GB, tag):
    h, hb, g, u, act, o = saved
    (dh_a, do), (dgam, dbet) = _rowwise_bwd(_make_res_ln_fn(0.5), [h, o], [gam, bet], [dout], name=f"{tag}_ln_bwd",
                                            tile=TM, grad_dtypes=[F32, BF16])
    dg, du = _ffn_down_dx_swiglu(do, W[pre + "_w_down"][l], g, u, name=f"{tag}_down_dx_swiglu")
    GB[pre + "_w_down"] = _chunk_rows_dw(act, do, HP, name=f"{tag}_down_dw")
    GB[pre + "_w_gate"], GB[pre + "_w_up"] = _ffn_gate_up_dw(dg, du, hb, name=f"{tag}_gate_up_dw")
    dh = _ffn_gate_up_dx(dg, du, W[pre + "_w_gate"][l], W[pre + "_w_up"][l], dh_a, name=f"{tag}_gate_up_dx")
    return dh, dgam, dbet


def _mixer_fwd(hp1, W, l, cosf, sins, after=()):
    h1, h1b = hp1
    tag = f"l{l}"
    proj = _mm(h1b, W["w_in_p"][l], name=f"{tag}_in_proj", after=after)
    sv = {"h1": h1, "h1b": h1b, "proj": proj}
    conv_w, conv_b = W["conv_w"][l], W["conv_b"][l][None]
    xc = _conv_fwd(proj, PC_XBC // BLOCK, conv_w, conv_b, name=f"{tag}_conv")
    dt_bias = _lane_pad(W["dt_bias"][l][None])
    dtc, dtr = _ssd_dt_fwd(proj, PC_DT // BLOCK, dt_bias, name=f"{tag}_ssd_dt")
    xh = _heads(xc[:, :SSD_D], SSD_HEADS, SSD_HD)
    bm = _heads(xc[:, SSD_D:SSD_D + 128], SSD_GROUPS, SSD_STATE)
    cm = _heads(xc[:, SSD_D + 128:], SSD_GROUPS, SSD_STATE)
    alog = jnp.broadcast_to(W["a_log"][l][:, None, None], (SSD_HEADS, 1, BLOCK))
    yh, prevs = _ssd_fwd(xh, bm, cm, dtc, dtr, alog, name=f"{tag}_ssd")
    y_raw = _unheads(yh)
    dskip = jnp.repeat(W["d_skip"][l], SSD_HD)[None]
    normg = W["ssd_norm_g"][l][None]
    post_rows = [y_raw, (xc, 256, 0), (proj, 256, PC_Z // 256)]
    (y_ssd,) = _rowwise(_ssd_post_fn, post_rows, [dskip, normg], [SSD_D], name=f"{tag}_ssd_post", tile=TM,
                        ncol=SSD_GROUPS)
    sv.update(conv_w=conv_w, conv_b=conv_b, dt_bias=dt_bias, xh=xh, bm=bm, cm=cm, dtc=dtc, dtr=dtr, alog=alog,
              prevs=prevs, post_rows=post_rows, dskip=dskip, normg=normg)
    f_b = _lane_pad(W["fox_f_b"][l][None])
    cg, cgt = _fox_gate_fwd(proj, PC_FR // BLOCK, f_b, name=f"{tag}_fox_gate")
    fox_qkv = ((proj, PC_FQ // ATT_W), (proj, PC_FK // ATT_W), (proj, PC_FV // ATT_W))
    y_fox, lse_f = _attn_fwd(*fox_qkv, scale=FOX_HD ** -0.5, name=f"{tag}_fox_attn", bias=(cg, cgt))
    sv.update(f_b=f_b, cg=cg, cgt=cgt, fox_qkv=fox_qkv, y_fox=y_fox, lse_f=lse_f)
    gq, gkv = W["mla_q_norm_g"][l][None], W["mla_kv_norm_g"][l][None]
    norm_rows = [(proj, 256, PC_CQ // 256), (proj, BLOCK, PC_CKV // BLOCK)]
    qn, cn = _rowwise(_mla_norm_fn, norm_rows, [gq, gkv], [MLA_Q_LORA, MLA_KV_LORA], name=f"{tag}_mla_norm", tile=TM,
                      out_dtypes=[BF16, BF16])
    qh = _mm(qn, W["mla_w_uq_p"][l], name=f"{tag}_mla_uq")
    kvh = _mm(cn, W["mla_w_ukv_p"][l], name=f"{tag}_mla_ukv")
    qr, kr = _rowwise(_rope_fn, [(qh, BLOCK, 2), (proj, BLOCK, PC_KR // BLOCK), cosf, sins], [], [BLOCK, BLOCK],
                      name=f"{tag}_rope", tile=TM)
    mla_qkv = ((qh, 0), (kvh, 0), (kvh, 1))
    y_mla, lse_m = _attn_fwd(*mla_qkv, scale=(MLA_NOPE + MLA_ROPE) ** -0.5, name=f"{tag}_mla_attn",
                             rope=((qr, 0), (kr, 0)))
    sv.update(gq=gq, gkv=gkv, norm_rows=norm_rows, qn=qn, cn=cn, qr=qr, kr=kr, mla_qkv=mla_qkv, y_mla=y_mla, lse_m=lse_m)
    ycat = jnp.concatenate([y_ssd, y_fox, y_mla], axis=1).astype(BF16)
    mix, h2, h2b = _chunk_rows_mm_res_ln(ycat, W["w_out"][l], 256, h1, W["ln2_g"][l][None], W["ln2_b"][l][None], 1.0,
                                    name=f"{tag}_out_proj_ln2")
    sv.update(mix=mix, ycat=ycat)
    return (h2, h2b), sv


def _mixer_bwd(dh2, sv, W, l, cosf, sins, GB, zero=0.0):
    tag = f"l{l}"
    G = {}
    proj = sv["proj"]
    ln2g, ln2b = W["ln2_g"][l][None] + zero, W["ln2_b"][l][None]
    (dh1_a, dmix), (dln2g, dln2b) = _rowwise_bwd(
        _make_res_ln_fn(1.0), [sv["h1"], sv["mix"]], [ln2g, ln2b], [dh2], name=f"{tag}_ln2_bwd", tile=TM,
        grad_dtypes=[F32, BF16])
    G["ln2_g"], G["ln2_b"] = dln2g[0], dln2b[0]
    dycat = _chunk_rows_dx(dmix, W["w_out"][l], l, 256, name=f"{tag}_out_proj_dx")
    GB["w_out"] = _chunk_rows_dw(sv["ycat"], dmix, 256, name=f"{tag}_out_proj_dw")
    (dy_raw, dxs_a, dz), (ddskip, dnormg) = _rowwise_bwd(
        _ssd_post_fn, sv["post_rows"], [sv["dskip"], sv["normg"]], [dycat[:, :SSD_D]],
        name=f"{tag}_ssd_post_bwd", tile=TM, ncol=SSD_GROUPS)
    G["ssd_norm_g"] = dnormg[0]
    G["d_skip"] = ddskip.reshape(SSD_HEADS, SSD_HD).sum(axis=1)
    dxh, dbm, dcm, ddtc, ddtr, dal = _ssd_bwd(sv["xh"], sv["bm"], sv["cm"], sv["dtc"], sv["dtr"], sv["alog"],
                                              sv["prevs"], _heads(dy_raw, SSD_HEADS, SSD_HD), name=f"{tag}_ssd_bwd")
    G["a_log"] = dal[:, 0, 0]
    dxc = jnp.concatenate([dxs_a + _unheads(dxh), _unheads(dbm), _unheads(dcm)], axis=1)
    dxbc, G["conv_w"], dconv_b = _conv_bwd(proj, PC_XBC // BLOCK, sv["conv_w"], sv["conv_b"], dxc,
                                           name=f"{tag}_conv_bwd")
    G["conv_b"] = dconv_b[0]
    ddt_raw, ddt_bias = _ssd_dt_bwd(proj, PC_DT // BLOCK, sv["dt_bias"], ddtc, ddtr, name=f"{tag}_ssd_dt_bwd")
    G["dt_bias"] = ddt_bias[0, :SSD_HEADS]
    dfq, dfk, dfv, dcg, dcgt = _attn_bwd(*sv["fox_qkv"], sv["y_fox"], sv["lse_f"], (dycat, SSD_D // ATT_W),
                                         scale=FOX_HD ** -0.5, name=f"{tag}_fox_attn_bwd", bias=(sv["cg"], sv["cgt"]))
    df_raw, dfb = _fox_gate_bwd(proj, PC_FR // BLOCK, sv["f_b"], dcg, dcgt, name=f"{tag}_fox_gate_bwd")
    G["fox_f_b"] = dfb[0, :FOX_HEADS]
    dqn_h, dkn_h, dv_h, dqr, dkr = _attn_bwd(
        *sv["mla_qkv"], sv["y_mla"], sv["lse_m"], (dycat, (SSD_D + FOX_D) // ATT_W),
        scale=(MLA_NOPE + MLA_ROPE) ** -0.5, name=f"{tag}_mla_attn_bwd", rope=((sv["qr"], 0), (sv["kr"], 0)))
    dq_rope, dk_rope = _rowwise(_rope_t_fn, [dqr, dkr, cosf, sins], [], [BLOCK, BLOCK], name=f"{tag}_rope_bwd",
                                tile=TM)
    dqh = jnp.concatenate([dqn_h, dq_rope], axis=1).astype(BF16)
    dkvh = jnp.concatenate([dkn_h, dv_h], axis=1).astype(BF16)
    dqn = _mm(dqh, W["mla_w_uq_p"][l], tb=True, name=f"{tag}_mla_uq_dx")
    G["mla_w_uq_p"] = _mm(sv["qn"], dqh, ta=True, name=f"{tag}_mla_uq_dw")
    dcn = _mm(dkvh, W["mla_w_ukv_p"][l], tb=True, name=f"{tag}_mla_ukv_dx")
    G["mla_w_ukv_p"] = _mm(sv["cn"], dkvh, ta=True, name=f"{tag}_mla_ukv_dw")
    (dcq, dckv), (dgq, dgkv) = _rowwise_bwd(_mla_norm_fn, sv["norm_rows"], [sv["gq"], sv["gkv"]], [dqn, dcn],
                                            name=f"{tag}_mla_norm_bwd", tile=TM)
    G["mla_q_norm_g"], G["mla_kv_norm_g"] = dgq[0], dgkv[0]
    dproj = jnp.concatenate([dz, dxbc, dfq, dfk, dfv, dcq, dckv, ddt_raw, df_raw, dk_rope], axis=1).astype(BF16)
    dh1 = _mm(dproj, W["w_in_p"][l], tb=True, add=dh1_a, name=f"{tag}_in_proj_dx")
    G["w_in_p"] = _mm(sv["h1b"], dproj, ta=True, name=f"{tag}_in_proj_dw")
    return dh1, G


def _embed(x, meta):
    return jnp.concatenate([jnp.zeros((PAD_ROWS, D_MODEL), F32), meta, x], axis=0)


def _layer_fwd(h, W, l, cosf, sins):
    ln = lambda n: W[n][l][None]
    h1, s1 = _ffn_fwd(h, W, "ffn1", l, ln("ln1_g"), ln("ln1_b"), f"l{l}_ffn1")
    h2, sm = _mixer_fwd(h1, W, l, cosf, sins)
    h3, s2 = _ffn_fwd(h2, W, "ffn2", l, ln("ln3_g"), ln("ln3_b"), f"l{l}_ffn2")
    return h3, (s1, sm, s2)


def _layer_bwd(dh, saved, W, l, cosf, sins):
    ln = lambda n: W[n][l][None]
    s1, sm, s2 = saved
    G = {}
    dh, dg, db = _ffn_bwd(dh, s2, W, "ffn2", l, ln("ln3_g"), ln("ln3_b"), G, f"l{l}_ffn2")
    G["ln3_g"], G["ln3_b"] = dg[0], db[0]
    dh, Gm = _mixer_bwd(dh, sm, W, l, cosf, sins, G)
    G.update(Gm)
    dh, dg, db = _ffn_bwd(dh, s1, W, "ffn1", l, ln("ln1_g"), ln("ln1_b"), G, f"l{l}_ffn1")
    G["ln1_g"], G["ln1_b"] = dg[0], db[0]
    return dh, G


def _local_step(x, target, W):
    h = _embed(x, W["meta"])
    h = (h, h.astype(BF16))
    tgt = jnp.concatenate([jnp.zeros((BLOCK, D_MODEL), F32), target], axis=0)
    cosf, sins = _rope_tables()
    saved = []
    for l in range(DEPTH):
        h, sv = _layer_fwd(h, W, l, cosf, sins)
        saved.append(sv)
    dh, loss = _loss_head(h[0], tgt, name="loss_head")
    grads = [None] * DEPTH
    for l in reversed(range(DEPTH)):
        dh, grads[l] = _layer_bwd(dh, saved[l], W, l, cosf, sins)
    return loss, dh, grads


WEIGHTS = ['meta', 'ffn1_w_gate', 'ffn1_w_up', 'ffn1_w_down', 'ln1_g', 'ln1_b', 'w_in', 'conv_w', 'conv_b', 'dt_bias',
           'a_log', 'd_skip', 'ssd_norm_g', 'fox_f_b', 'mla_q_norm_g', 'mla_w_uq', 'mla_kv_norm_g', 'mla_w_ukv',
           'w_out', 'ln2_g', 'ln2_b', 'ffn2_w_gate', 'ffn2_w_up', 'ffn2_w_down', 'ln3_g', 'ln3_b']
SMALL = ["ln1_g", "ln1_b", "conv_b", "dt_bias", "a_log", "d_skip", "ssd_norm_g", "fox_f_b", "mla_q_norm_g",
         "mla_kv_norm_g", "ln2_g", "ln2_b", "ln3_g", "ln3_b"]
MATMUL_W = ["ffn1_w_gate", "ffn1_w_up", "ffn1_w_down", "w_in", "mla_w_uq", "mla_w_ukv", "w_out", "ffn2_w_gate",
            "ffn2_w_up", "ffn2_w_down"]
SMALL_ROWS = 312


def _pad_to(a, axis, size):
    pads = [(0, 0)] * a.ndim
    pads[axis] = (0, size - a.shape[axis])
    return jnp.pad(a, pads)


def _chip_cols(full, chip, width):
    return lax.dynamic_slice_in_dim(full, chip * width, width, axis=full.ndim - 1)


def kernel(x, meta, ffn1_w_gate, ffn1_w_up, ffn1_w_down, ln1_g, ln1_b, w_in, conv_w, conv_b, dt_bias, a_log, d_skip, ssd_norm_g, fox_f_b, mla_q_norm_g, mla_w_uq, mla_kv_norm_g, mla_w_ukv, w_out, ln2_g, ln2_b, ffn2_w_gate, ffn2_w_up, ffn2_w_down, ln3_g, ln3_b, loss_target, m_meta, m_ffn1_w_gate, m_ffn1_w_up, m_ffn1_w_down, m_ln1_g, m_ln1_b, m_w_in, m_conv_w, m_conv_b, m_dt_bias, m_a_log, m_d_skip, m_ssd_norm_g, m_fox_f_b, m_mla_q_norm_g, m_mla_w_uq, m_mla_kv_norm_g, m_mla_w_ukv, m_w_out, m_ln2_g, m_ln2_b, m_ffn2_w_gate, m_ffn2_w_up, m_ffn2_w_down, m_ln3_g, m_ln3_b, v_meta, v_ffn1_w_gate, v_ffn1_w_up, v_ffn1_w_down, v_ln1_g, v_ln1_b, v_w_in, v_conv_w, v_conv_b, v_dt_bias, v_a_log, v_d_skip, v_ssd_norm_g, v_fox_f_b, v_mla_q_norm_g, v_mla_w_uq, v_mla_kv_norm_g, v_mla_w_ukv, v_w_out, v_ln2_g, v_ln2_b, v_ffn2_w_gate, v_ffn2_w_up, v_ffn2_w_down, v_ln3_g, v_ln3_b):
    args = dict(locals())
    w = {n: args[n] for n in WEIGHTS}
    m = {n: args["m_" + n] for n in WEIGHTS}
    v = {n: args["v_" + n] for n in WEIGHTS}
    xcoord, ycoord, _ = _my_pos()
    chip = 2 * xcoord + ycoord

    tr = lambda a: jnp.swapaxes(a, 1, 2)

    def bf16_shard(n, l, zero=None):
        a = w[n] if zero is None else w[n] + zero
        if n.endswith("w_gate") or n.endswith("w_up"):
            a = _pad_to(tr(a), 1, HP)
        elif n.endswith("w_down"):
            a = _pad_to(a, 1, HP)
        elif n == "w_in":
            a = _pad_to(a, 2, IN_SHARD_P)
        return a[l].astype(BF16)

    land_shape = lambda s: jax.ShapeDtypeStruct((N_CHIPS,) + s.shape, s.dtype)

    def gather_start(names, l, tag, after):
        srcs = [bf16_shard(n, l, None if after is None else after[0, 0]) for n in names]
        return _ici_start(_gather_copies, srcs, [land_shape(s) for s in srcs], name=f"gather_ici_{tag}_start",
                          after=[tiny[0]] if after is None else [after])

    def gather_finish(handle, names, l, tag, after):
        srcs, lands = _ici_wait(_gather_copies, *handle[:4], after, name=f"gather_ici_{tag}_wait")
        use_gathered(l, names, _gather_d2d(srcs, lands, tag))

    def gather_d2d_start(handle, tag, after):
        srcs, lands = _ici_wait(_gather_copies, *handle[:4], after, name=f"gather_ici_{tag}_wait")
        return _ici_start(_d2d_copies, srcs, lands, name=f"gather_d2d_{tag}_start", sems_per_array=D2D_COPIES)

    def gather_d2d_finish(handle, names, l, tag, after):
        _, lands = _ici_wait(_d2d_copies, *handle[:4], after, name=f"gather_d2d_{tag}_wait")
        use_gathered(l, names, lands)

    tiny = _allgather_chips([w["meta"].reshape(2, N_META // 2, D_MODEL // N_CHIPS), w["conv_w"]])
    meta_full = jnp.concatenate([tiny[0][k].reshape(N_META, D_MODEL // N_CHIPS) for k in range(N_CHIPS)], axis=1)

    W = {n: [None] * DEPTH for n in MATMUL_W + ["w_in_p", "mla_w_uq_p", "mla_w_ukv_p"]}
    W["conv_w"] = jnp.concatenate([tiny[1][k] for k in range(N_CHIPS)], axis=-1)
    W["meta"] = meta_full
    for n in SMALL:
        W[n] = w[n]

    def use_gathered(l, names, lands):
        got = dict(zip(names, lands))
        cat = lambda n, cut=None: jnp.concatenate([got[n][k][..., :cut] for k in range(N_CHIPS)], axis=-1)
        for n in names:
            W[n][l] = got[n]
        if "w_in" in got:
            W["w_in_p"][l] = _pad_in_proj(cat("w_in", IN_SHARD))
            W["mla_w_uq_p"][l] = _regroup_uq(cat("mla_w_uq"))
            W["mla_w_ukv_p"][l] = _regroup_ukv(cat("mla_w_ukv"))

    def chunk_grads(G, names):
        def chunked(name, ungroup, width, pad):
            full = ungroup(G[name])
            return _pad_to(jnp.moveaxis(full.reshape(full.shape[0], N_CHIPS, width), 1, 0), 2, pad)
        special = {"mla_w_uq": ("mla_w_uq_p", _ungroup_uq, MLA_NOPE + MLA_ROPE, MLA_NOPE + MLA_ROPE),
                   "mla_w_ukv": ("mla_w_ukv_p", _ungroup_ukv, MLA_NOPE + MLA_V, MLA_NOPE + MLA_V)}
        return [_in_proj_grad_chunks(G["w_in_p"]) if n == "w_in" else chunked(*special[n]) if n in special else G[n]
                for n in names]

    def rs_start(G, names, tag):
        pairs = _rs_pair_sums(chunk_grads(G, names), names, tag)
        handle = _ici_start(_exchange_copies, [p[1] for p in pairs], _exchange_land_shapes(pairs),
                            name=f"rs_exchange_{tag}_start")
        return pairs, handle

    def swap_start(G, names, tag):
        gs = chunk_grads(G, names)
        return _ici_start(_swap_copies, gs, _swap_land_shapes(gs), name=f"rs_swap_{tag}_start")

    def exchange_start(swap_handle, names, tag, after):
        gs, r1 = _ici_wait(_swap_copies, *swap_handle[:4], after, name=f"rs_swap_{tag}_wait")
        pairs = _rs_add_pairs(gs, r1, names, tag)
        handle = _ici_start(_exchange_copies, [p[1] for p in pairs], _exchange_land_shapes(pairs),
                            name=f"rs_exchange_{tag}_start")
        return pairs, handle

    def rs_end(pairs, handle, names, tag, after):
        _, r2 = _ici_wait(_exchange_copies, *handle[:4], after, name=f"rs_exchange_{tag}_wait")
        return dict(zip(names, _rs_finish(pairs, r2, names, tag)))

    ffn1_w, mix_w, ffn2_w = MATMUL_W[:3], MATMUL_W[3:7], MATMUL_W[7:]
    g_a = gather_start(ffn1_w, 0, "l0_ffn1", None)
    g_b = gather_start(mix_w, 0, "l0_mix", g_a[4])
    g_c = gather_start(ffn2_w, 0, "l0_ffn2", g_b[4])
    g_l1 = gather_start(MATMUL_W, 1, "l1", g_c[4])
    token = g_l1[4]
    cosf, sins = _rope_tables()
    ln = lambda n, l: W[n][l][None]
    h = _embed(x[0] + token[0, 0], meta_full)
    h = (h, h.astype(BF16))
    gather_finish(g_a, ffn1_w, 0, "l0_ffn1", h[1])
    h1, s1 = _ffn_fwd(h, W, "ffn1", 0, ln("ln1_g", 0), ln("ln1_b", 0), "l0_ffn1")
    gather_finish(g_b, mix_w, 0, "l0_mix", h1[1])
    d_c = gather_d2d_start(g_c, "l0_ffn2", W["w_in_p"][0])
    h2, sm = _mixer_fwd(h1, W, 0, cosf, sins, after=[d_c[4]])
    gather_d2d_finish(d_c, ffn2_w, 0, "l0_ffn2", h2[1])
    d_l1 = gather_d2d_start(g_l1, "l1", W["ffn2_w_gate"][0])
    h, s2 = _ffn_fwd(h2, W, "ffn2", 0, ln("ln3_g", 0), ln("ln3_b", 0), "l0_ffn2", after=[d_l1[4]])
    saved0 = (s1, sm, s2)
    gather_d2d_finish(d_l1, MATMUL_W, 1, "l1", h[1])
    h, saved1 = _layer_fwd(h, W, 1, cosf, sins)
    tgt = jnp.concatenate([jnp.zeros((BLOCK, D_MODEL), F32), loss_target[0]], axis=0)
    dh, loss = _loss_head(h[0], tgt, name="loss_head")
    G = [None] * DEPTH
    dh, G[1] = _layer_bwd(dh, saved1, W, 1, cosf, sins)
    ffn2_w, mix_w, ffn1_w = MATMUL_W[7:], MATMUL_W[3:7], MATMUL_W[:3]
    sw_l1 = swap_start(G[1], MATMUL_W, "l1")
    G0 = {}
    dh, dg, db = _ffn_bwd(dh, s2, W, "ffn2", 0, ln("ln3_g", 0) + sw_l1[4][0, 0], ln("ln3_b", 0), G0, "l0_ffn2")
    G0["ln3_g"], G0["ln3_b"] = dg[0], db[0]
    pairs_l1, x_l1 = exchange_start(sw_l1, MATMUL_W, "l1", dh)
    sw_a = swap_start(G0, ffn2_w, "l0_ffn2")
    dh, Gm = _mixer_bwd(dh, sm, W, 0, cosf, sins, G0, zero=x_l1[4][0, 0] + sw_a[4][0, 0])
    G0.update(Gm)
    pairs_a, x_a = exchange_start(sw_a, ffn2_w, "l0_ffn2", dh)
    reduced1 = rs_end(pairs_l1, x_l1, MATMUL_W, "l1", dh)
    pairs_b, x_b = rs_start(G0, mix_w, "l0_mix")
    dh0, dg, db = _ffn_bwd(dh, s1, W, "ffn1", 0, ln("ln1_g", 0) + (x_a[4][0, 0] + x_b[4][0, 0]), ln("ln1_b", 0), G0,
                           "l0_ffn1")
    G0["ln1_g"], G0["ln1_b"] = dg[0], db[0]
    G[0] = G0
    reduced0 = rs_end(pairs_a, x_a, ffn2_w, "l0_ffn2", dh0)
    reduced0.update(rs_end(pairs_b, x_b, mix_w, "l0_mix", dh0))

    small_parts = [jnp.stack([G[l][n] for l in range(DEPTH)]).reshape(-1) for n in SMALL]
    small_parts += [jnp.stack([G[l]["conv_w"] for l in range(DEPTH)]).reshape(-1), dh0[PAD_ROWS:BLOCK].reshape(-1),
                    loss[0, :1]]
    sw_c = swap_start(G0, ffn1_w, "l0_ffn1")
    flat = jnp.concatenate(small_parts) + sw_c[4][0, 0]
    flat = jnp.pad(flat, (0, SMALL_ROWS * BLOCK - flat.shape[0]))
    red2d = _allreduce_small(flat.reshape(SMALL_ROWS, BLOCK))
    red = red2d.reshape(-1)
    pairs_c, x_c = exchange_start(sw_c, ffn1_w, "l0_ffn1", red2d)
    grads, off = {}, 0
    for n in SMALL:
        size = int(np.prod(w[n].shape))
        grads[n] = red[off:off + size].reshape(w[n].shape)
        off += size
    conv_full = red[off:off + DEPTH * SSD_CONV * 768].reshape(DEPTH, SSD_CONV, 768)
    off += DEPTH * SSD_CONV * 768
    dmeta_full = red[off:off + N_META * D_MODEL].reshape(N_META, D_MODEL)
    off += N_META * D_MODEL
    loss_out = red[off]
    grads["conv_w"] = _chip_cols(conv_full, chip, 768 // N_CHIPS)
    grads["meta"] = _chip_cols(dmeta_full, chip, D_MODEL // N_CHIPS)

    delta, new_m, new_v = {}, {}, {}

    def adamw_matmul_weights(names, after):
        done = []
        for n in names:
            gs = [reduced0[n], reduced1[n]]
            if n.endswith("w_gate") or n.endswith("w_up"):
                res = _adamw(tr(w[n]), gs, tr(m[n]), tr(v[n]), name=f"adamw_{n}", after=after)
                grads[n], delta[n], new_m[n], new_v[n] = [tr(r) for r in res]
            else:
                res = _adamw(w[n], gs, m[n], v[n], name=f"adamw_{n}", after=after)
                grads[n], delta[n], new_m[n], new_v[n] = res
            done.append(res[1])
        return done

    early_done = adamw_matmul_weights(ffn2_w + mix_w, [x_c[4]])
    rest = [n for n in WEIGHTS if n not in MATMUL_W]

    def pack_small(d):
        f = jnp.concatenate([d[n].reshape(-1) for n in rest])
        tot = -(-f.shape[0] // (8 * BLOCK)) * 8 * BLOCK
        return jnp.pad(f, (0, tot - f.shape[0])).reshape(-1, BLOCK)

    _, d2, m2, v2 = _adamw(pack_small(w), [pack_small(grads)], pack_small(m), pack_small(v), name="adamw_small",
                           after=[x_c[4]])
    reduced0.update(rs_end(pairs_c, x_c, ffn1_w, "l0_ffn1", [d2] + early_done))
    adamw_matmul_weights(ffn1_w, [])
    off = 0
    for n in rest:
        size = int(np.prod(w[n].shape))
        for dst, src in ((delta, d2), (new_m, m2), (new_v, v2)):
            dst[n] = src.reshape(-1)[off:off + size].reshape(w[n].shape)
        off += size

    grad_x = dh0[BLOCK:][None]
    return (loss_out, grad_x, *[grads[n] for n in WEIGHTS], *[delta[n] for n in WEIGHTS],
            *[new_m[n] for n in WEIGHTS], *[new_v[n] for n in WEIGHTS])
```
